```python
import math
import jax, jax.numpy as jnp
from jax import lax
import numpy as np

D_MODEL = 2048
BATCH = 8
SEQ = 2048
DEPTH = 1

ATTN_HEADS = 16
HEAD_DIM = 128
ATTN_W = ATTN_HEADS * HEAD_DIM
DILATED_GROUPS = ((128, 1), (512, 4), (2048, 16))
BLK = 128
LRU_W = D_MODEL
LRU_HEADS = 16
LRU_BLK = LRU_W // LRU_HEADS
CONV_W = 4
LRU_C = 8.0
D_FF = 4 * D_MODEL
EPS = 1e-6
IN_SPLITS = (ATTN_W, ATTN_W, ATTN_W, LRU_W, LRU_W, D_MODEL, D_MODEL)
IN_COLS = sum(IN_SPLITS)

kernel_name = "hybrid_dilated_attn_rglru_gated_block"


def rms_norm(x, g):
    xf = x.astype(jnp.float32)
    y = xf * lax.rsqrt(jnp.mean(xf * xf, axis=-1, keepdims=True) + EPS)
    return (y * g.astype(jnp.float32)).astype(x.dtype)


def alibi_slopes(n_heads):
    return 2.0 ** (-8.0 * jnp.arange(1, n_heads + 1, dtype=jnp.float32) / n_heads)


def dilated_group_attention(q, k, v, slopes, window, dilation):
    b, s, h, dh = q.shape
    d = dilation
    span = window // d
    u_len = s // d
    n_blk = -(-u_len // BLK)
    u_pad = n_blk * BLK
    qs = q.reshape(b, u_len, d, h, dh)
    ks = k.reshape(b, u_len, d, h, dh)
    vs = v.reshape(b, u_len, d, h, dh)
    qb = jnp.pad(qs, ((0, 0), (0, u_pad - u_len), (0, 0), (0, 0), (0, 0))).reshape(b, n_blk, BLK, d, h, dh)
    pad_kv = ((0, 0), (BLK, u_pad - u_len), (0, 0), (0, 0), (0, 0))
    kp = jnp.pad(ks, pad_kv).reshape(b, n_blk + 1, BLK, d, h, dh)
    vp = jnp.pad(vs, pad_kv).reshape(b, n_blk + 1, BLK, d, h, dh)
    kwin = jnp.concatenate([kp[:, :-1], kp[:, 1:]], axis=2)
    vwin = jnp.concatenate([vp[:, :-1], vp[:, 1:]], axis=2)
    scores = jnp.einsum('bnqrhe,bnkrhe->bnrhqk', qb, kwin).astype(jnp.float32) * (HEAD_DIM ** -0.5)
    i = jnp.arange(BLK)[:, None]
    j = jnp.arange(2 * BLK)[None, :]
    diff = BLK + i - j
    band = (diff >= 0) & (diff <= span)
    n_idx = jnp.arange(n_blk)[:, None, None]
    valid = band[None] & ((n_idx - 1) * BLK + j[None] >= 0)
    bias = -(slopes * d)[:, None, None] * diff.astype(jnp.float32)[None]
    scores = scores + bias[None, None, None]
    scores = jnp.where(valid[None, :, None, None], scores, -jnp.inf)
    m = jnp.max(scores, axis=-1, keepdims=True)
    p = jnp.exp(scores - m)
    l = jnp.sum(p, axis=-1)
    o = jnp.einsum('bnrhqk,bnkrhe->bnqrhe', p.astype(v.dtype), vwin).astype(jnp.float32)
    l_t = jnp.transpose(l, (0, 1, 4, 2, 3))
    o = o / l_t[..., None]
    lse = jnp.transpose(m[..., 0] + jnp.log(l), (0, 1, 4, 2, 3))
    o = o.reshape(b, u_pad, d, h, dh)[:, :u_len].reshape(b, s, h, dh)
    lse = lse.reshape(b, u_pad, d, h)[:, :u_len].reshape(b, s, h)
    return o, lse


def dilated_attention(q, k, v):
    b, s, _ = q.shape
    q = q.reshape(b, s, ATTN_HEADS, HEAD_DIM)
    k = k.reshape(b, s, ATTN_HEADS, HEAD_DIM)
    v = v.reshape(b, s, ATTN_HEADS, HEAD_DIM)
    slopes = alibi_slopes(ATTN_HEADS)
    outs, lses = [], []
    for window, dilation in DILATED_GROUPS:
        o, lse = dilated_group_attention(q, k, v, slopes, window, dilation)
        outs.append(o)
        lses.append(lse)
    w = jax.nn.softmax(jnp.stack(lses, axis=0), axis=0)
    out = jnp.sum(w[..., None] * jnp.stack(outs, axis=0), axis=0)
    return out.reshape(b, s, ATTN_W).astype(q.dtype)


def rg_lru_branch(xr, gate, conv_w, conv_b, wa, ba, wx, bx, lam):
    b, s, c = xr.shape
    xc = lax.conv_general_dilated(
        xr, conv_w.reshape(CONV_W, 1, c).astype(xr.dtype), window_strides=(1,),
        padding=[(CONV_W - 1, 0)], dimension_numbers=('NWC', 'WIO', 'NWC'),
        feature_group_count=c) + conv_b
    xh = xc.reshape(b, s, LRU_HEADS, LRU_BLK)
    r = jax.nn.sigmoid((jnp.einsum('bshi,hij->bshj', xh, wa).reshape(b, s, c) + ba).astype(jnp.float32))
    ig = jax.nn.sigmoid((jnp.einsum('bshi,hij->bshj', xh, wx).reshape(b, s, c) + bx).astype(jnp.float32))
    log_a = -LRU_C * r * jax.nn.softplus(-lam.astype(jnp.float32))
    a = jnp.exp(log_a)
    mult = jnp.sqrt(-jnp.expm1(2.0 * log_a))
    u = mult * (ig * xc.astype(jnp.float32))

    def combine(c1, c2):
        a1, b1 = c1
        a2, b2 = c2
        return a1 * a2, a2 * b1 + b2

    _, hseq = lax.associative_scan(combine, (a, u), axis=1)
    return (hseq * jax.nn.gelu(gate.astype(jnp.float32))).astype(xr.dtype)


def _fwd_setup_inputs(seed: int = 0) -> dict:
    key = jax.random.key(seed)
    ks = jax.random.split(key, 20)
    f32 = jnp.float32
    nrm = lambda k, shape, scale: jax.random.normal(k, shape, f32) * scale
    a8 = jax.random.uniform(ks[9], (DEPTH, LRU_W), f32, 0.9, 0.999)
    a_base = a8 ** (1.0 / LRU_C)
    lru_lambda = jnp.log(a_base) - jnp.log1p(-a_base)
    return {
        "x": jax.random.normal(ks[0], (BATCH, SEQ, D_MODEL), f32),
        "norm_mix_g": 1.0 + nrm(ks[1], (DEPTH, D_MODEL), 0.02),
        "w_in": nrm(ks[2], (DEPTH, D_MODEL, IN_COLS), D_MODEL ** -0.5),
        "conv_w": nrm(ks[3], (DEPTH, CONV_W, LRU_W), CONV_W ** -0.5),
        "conv_b": nrm(ks[4], (DEPTH, LRU_W), 0.02),
        "lru_wa": nrm(ks[5], (DEPTH, LRU_HEADS, LRU_BLK, LRU_BLK), LRU_BLK ** -0.5),
        "lru_ba": nrm(ks[6], (DEPTH, LRU_W), 0.02),
        "lru_wx": nrm(ks[7], (DEPTH, LRU_HEADS, LRU_BLK, LRU_BLK), LRU_BLK ** -0.5),
        "lru_bx": nrm(ks[8], (DEPTH, LRU_W), 0.02),
        "lru_lambda": lru_lambda,
        "w_proj_attn": nrm(ks[10], (DEPTH, ATTN_W, D_MODEL), ATTN_W ** -0.5),
        "w_proj_lru": nrm(ks[11], (DEPTH, LRU_W, D_MODEL), LRU_W ** -0.5),
        "w_out": nrm(ks[12], (DEPTH, D_MODEL, D_MODEL), D_MODEL ** -0.5),
        "norm_mlp_g": 1.0 + nrm(ks[13], (DEPTH, D_MODEL), 0.02),
        "w_up": nrm(ks[14], (DEPTH, D_MODEL, D_FF), D_MODEL ** -0.5),
        "w_down": nrm(ks[15], (DEPTH, D_FF, D_MODEL), D_FF ** -0.5),
        "norm_final_g": 1.0 + nrm(ks[16], (D_MODEL,), 0.02),
    }


def _fwd_reference(x, norm_mix_g, w_in, conv_w, conv_b, lru_wa, lru_ba, lru_wx, lru_bx, lru_lambda,
              w_proj_attn, w_proj_lru, w_out, norm_mlp_g, w_up, w_down, norm_final_g):
    h = x
    split_idx = list(np.cumsum(IN_SPLITS)[:-1])
    for l in range(DEPTH):
        xn = rms_norm(h, norm_mix_g[l])
        proj = jnp.einsum('bsd,dc->bsc', xn, w_in[l])
        q, k, v, xr, xg, g_attn, g_lru = jnp.split(proj, split_idx, axis=-1)
        y_attn = dilated_attention(q, k, v)
        y_lru = rg_lru_branch(xr, xg, conv_w[l], conv_b[l], lru_wa[l], lru_ba[l],
                              lru_wx[l], lru_bx[l], lru_lambda[l])
        merged = (jax.nn.sigmoid(g_attn) * jnp.einsum('bsc,cd->bsd', y_attn, w_proj_attn[l])
                  + jax.nn.sigmoid(g_lru) * jnp.einsum('bsc,cd->bsd', y_lru, w_proj_lru[l]))
        h = h + jnp.einsum('bsd,de->bse', merged, w_out[l])
        hn = rms_norm(h, norm_mlp_g[l])
        hid = jnp.square(jax.nn.relu(jnp.einsum('bsd,df->bsf', hn, w_up[l])))
        h = h + jnp.einsum('bsf,fd->bsd', hid, w_down[l])
    return rms_norm(h, norm_final_g)


import jax as _jax
import jax.numpy as _jnp

TWIN_FORMAT = 'train_step'
FWD_PARAMS = ['x', 'norm_mix_g', 'w_in', 'conv_w', 'conv_b', 'lru_wa', 'lru_ba', 'lru_wx', 'lru_bx', 'lru_lambda', 'w_proj_attn', 'w_proj_lru', 'w_out', 'norm_mlp_g', 'w_up', 'w_down', 'norm_final_g']
TWIN_WEIGHTS = ['norm_mix_g', 'w_in', 'conv_w', 'conv_b', 'lru_wa', 'lru_ba', 'lru_wx', 'lru_bx', 'lru_lambda', 'w_proj_attn', 'w_proj_lru', 'w_out', 'norm_mlp_g', 'w_up', 'w_down', 'norm_final_g']
TWIN_DIFF_INPUT = 'x'
TWIN_INPUTS = ['x', 'norm_mix_g', 'w_in', 'conv_w', 'conv_b', 'lru_wa', 'lru_ba', 'lru_wx', 'lru_bx', 'lru_lambda', 'w_proj_attn', 'w_proj_lru', 'w_out', 'norm_mlp_g', 'w_up', 'w_down', 'norm_final_g', 'loss_target', 'm_norm_mix_g', 'm_w_in', 'm_conv_w', 'm_conv_b', 'm_lru_wa', 'm_lru_ba', 'm_lru_wx', 'm_lru_bx', 'm_lru_lambda', 'm_w_proj_attn', 'm_w_proj_lru', 'm_w_out', 'm_norm_mlp_g', 'm_w_up', 'm_w_down', 'm_norm_final_g', 'v_norm_mix_g', 'v_w_in', 'v_conv_w', 'v_conv_b', 'v_lru_wa', 'v_lru_ba', 'v_lru_wx', 'v_lru_bx', 'v_lru_lambda', 'v_w_proj_attn', 'v_w_proj_lru', 'v_w_out', 'v_norm_mlp_g', 'v_w_up', 'v_w_down', 'v_norm_final_g']
TWIN_OUTPUTS = ['loss', 'grad_x', 'grad_norm_mix_g', 'grad_w_in', 'grad_conv_w', 'grad_conv_b', 'grad_lru_wa', 'grad_lru_ba', 'grad_lru_wx', 'grad_lru_bx', 'grad_lru_lambda', 'grad_w_proj_attn', 'grad_w_proj_lru', 'grad_w_out', 'grad_norm_mlp_g', 'grad_w_up', 'grad_w_down', 'grad_norm_final_g', 'delta_norm_mix_g', 'delta_w_in', 'delta_conv_w', 'delta_conv_b', 'delta_lru_wa', 'delta_lru_ba', 'delta_lru_wx', 'delta_lru_bx', 'delta_lru_lambda', 'delta_w_proj_attn', 'delta_w_proj_lru', 'delta_w_out', 'delta_norm_mlp_g', 'delta_w_up', 'delta_w_down', 'delta_norm_final_g', 'new_m_norm_mix_g', 'new_m_w_in', 'new_m_conv_w', 'new_m_conv_b', 'new_m_lru_wa', 'new_m_lru_ba', 'new_m_lru_wx', 'new_m_lru_bx', 'new_m_lru_lambda', 'new_m_w_proj_attn', 'new_m_w_proj_lru', 'new_m_w_out', 'new_m_norm_mlp_g', 'new_m_w_up', 'new_m_w_down', 'new_m_norm_final_g', 'new_v_norm_mix_g', 'new_v_w_in', 'new_v_conv_w', 'new_v_conv_b', 'new_v_lru_wa', 'new_v_lru_ba', 'new_v_lru_wx', 'new_v_lru_bx', 'new_v_lru_lambda', 'new_v_w_proj_attn', 'new_v_w_proj_lru', 'new_v_w_out', 'new_v_norm_mlp_g', 'new_v_w_up', 'new_v_w_down', 'new_v_norm_final_g']
TWIN_LEAF_KINDS = {'loss': 'loss', 'grad_x': 'grad_x', 'grad_norm_mix_g': 'grad_w', 'grad_w_in': 'grad_w', 'grad_conv_w': 'grad_w', 'grad_conv_b': 'grad_w', 'grad_lru_wa': 'grad_w', 'grad_lru_ba': 'grad_w', 'grad_lru_wx': 'grad_w', 'grad_lru_bx': 'grad_w', 'grad_lru_lambda': 'grad_w', 'grad_w_proj_attn': 'grad_w', 'grad_w_proj_lru': 'grad_w', 'grad_w_out': 'grad_w', 'grad_norm_mlp_g': 'grad_w', 'grad_w_up': 'grad_w', 'grad_w_down': 'grad_w', 'grad_norm_final_g': 'grad_w', 'delta_norm_mix_g': 'delta_w', 'delta_w_in': 'delta_w', 'delta_conv_w': 'delta_w', 'delta_conv_b': 'delta_w', 'delta_lru_wa': 'delta_w', 'delta_lru_ba': 'delta_w', 'delta_lru_wx': 'delta_w', 'delta_lru_bx': 'delta_w', 'delta_lru_lambda': 'delta_w', 'delta_w_proj_attn': 'delta_w', 'delta_w_proj_lru': 'delta_w', 'delta_w_out': 'delta_w', 'delta_norm_mlp_g': 'delta_w', 'delta_w_up': 'delta_w', 'delta_w_down': 'delta_w', 'delta_norm_final_g': 'delta_w', 'new_m_norm_mix_g': 'new_m', 'new_m_w_in': 'new_m', 'new_m_conv_w': 'new_m', 'new_m_conv_b': 'new_m', 'new_m_lru_wa': 'new_m', 'new_m_lru_ba': 'new_m', 'new_m_lru_wx': 'new_m', 'new_m_lru_bx': 'new_m', 'new_m_lru_lambda': 'new_m', 'new_m_w_proj_attn': 'new_m', 'new_m_w_proj_lru': 'new_m', 'new_m_w_out': 'new_m', 'new_m_norm_mlp_g': 'new_m', 'new_m_w_up': 'new_m', 'new_m_w_down': 'new_m', 'new_m_norm_final_g': 'new_m', 'new_v_norm_mix_g': 'new_v', 'new_v_w_in': 'new_v', 'new_v_conv_w': 'new_v', 'new_v_conv_b': 'new_v', 'new_v_lru_wa': 'new_v', 'new_v_lru_ba': 'new_v', 'new_v_lru_wx': 'new_v', 'new_v_lru_bx': 'new_v', 'new_v_lru_lambda': 'new_v', 'new_v_w_proj_attn': 'new_v', 'new_v_w_proj_lru': 'new_v', 'new_v_w_out': 'new_v', 'new_v_norm_mlp_g': 'new_v', 'new_v_w_up': 'new_v', 'new_v_w_down': 'new_v', 'new_v_norm_final_g': 'new_v'}


def _forward(args):
    return _fwd_reference(*[args[k] for k in FWD_PARAMS])


def _output_shape():
    out = _jax.eval_shape(lambda: _forward(_fwd_setup_inputs(0)))
    return out.shape, out.dtype

N_MICROBATCH = 1
ADAM_LR = 0.001
ADAM_B1 = 0.9
ADAM_B2 = 0.999
ADAM_EPS = 1e-08
ADAM_WD = 0.01
ADAM_STEP = 10
PER_EXAMPLE_BATCH_AXIS = {'x': 0, 'loss_target': 0}
SHARED_INPUTS = []
_WEIGHT_DTYPES = {'norm_mix_g': _jnp.float32, 'w_in': _jnp.float32, 'conv_w': _jnp.float32, 'conv_b': _jnp.float32, 'lru_wa': _jnp.float32, 'lru_ba': _jnp.float32, 'lru_wx': _jnp.float32, 'lru_bx': _jnp.float32, 'lru_lambda': _jnp.float32, 'w_proj_attn': _jnp.float32, 'w_proj_lru': _jnp.float32, 'w_out': _jnp.float32, 'norm_mlp_g': _jnp.float32, 'w_up': _jnp.float32, 'w_down': _jnp.float32, 'norm_final_g': _jnp.float32}
MOMENT_SCALE = {'norm_mix_g': 2.936898e-02, 'w_in': 1.076281e-02, 'conv_w': 1.513226e-02, 'conv_b': 1.661684e-01, 'lru_wa': 4.790886e-03, 'lru_ba': 4.181291e-03, 'lru_wx': 8.576943e-03, 'lru_bx': 5.432366e-03, 'lru_lambda': 8.584181e-03, 'w_proj_attn': 1.466021e-02, 'w_proj_lru': 1.506777e-02, 'w_out': 1.988178e-02, 'norm_mlp_g': 5.421030e-02, 'w_up': 2.760148e-02, 'w_down': 5.175438e-02, 'norm_final_g': 8.076260e+00}


def _to_microbatches(a, axis):
    t = _jnp.moveaxis(a, axis, 0)
    t = t.reshape((N_MICROBATCH, t.shape[0] // N_MICROBATCH) + t.shape[1:])
    return _jnp.moveaxis(t, 1, axis + 1)


def setup_inputs(seed: int = 0) -> dict:
    inp = _fwd_setup_inputs(seed)
    key = _jax.random.fold_in(_jax.random.key(seed), 7919)
    shape, _ = _output_shape()
    out = dict(inp)
    out["loss_target"] = _jax.random.normal(_jax.random.fold_in(key, 0), shape, _jnp.float32)
    for i, name in enumerate(TWIN_WEIGHTS):
        w = inp[name].astype(_jnp.float32)
        if MOMENT_SCALE is None:
            s = _jnp.sqrt(_jnp.mean(_jnp.square(w)) + 1e-30)
        else:
            s = MOMENT_SCALE[name]
        km, kv = _jax.random.split(_jax.random.fold_in(key, i + 1))
        out[name] = w
        out["m_" + name] = s * _jax.random.normal(km, w.shape, _jnp.float32)
        out["v_" + name] = (s * s) * _jax.random.uniform(kv, w.shape, _jnp.float32, 0.5, 1.5)
    if N_MICROBATCH > 1:
        for name, axis in PER_EXAMPLE_BATCH_AXIS.items():
            out[name] = _to_microbatches(out[name], axis)
    return {'x': out['x'], 'norm_mix_g': out['norm_mix_g'], 'w_in': out['w_in'], 'conv_w': out['conv_w'], 'conv_b': out['conv_b'], 'lru_wa': out['lru_wa'], 'lru_ba': out['lru_ba'], 'lru_wx': out['lru_wx'], 'lru_bx': out['lru_bx'], 'lru_lambda': out['lru_lambda'], 'w_proj_attn': out['w_proj_attn'], 'w_proj_lru': out['w_proj_lru'], 'w_out': out['w_out'], 'norm_mlp_g': out['norm_mlp_g'], 'w_up': out['w_up'], 'w_down': out['w_down'], 'norm_final_g': out['norm_final_g'], 'loss_target': out['loss_target'], 'm_norm_mix_g': out['m_norm_mix_g'], 'm_w_in': out['m_w_in'], 'm_conv_w': out['m_conv_w'], 'm_conv_b': out['m_conv_b'], 'm_lru_wa': out['m_lru_wa'], 'm_lru_ba': out['m_lru_ba'], 'm_lru_wx': out['m_lru_wx'], 'm_lru_bx': out['m_lru_bx'], 'm_lru_lambda': out['m_lru_lambda'], 'm_w_proj_attn': out['m_w_proj_attn'], 'm_w_proj_lru': out['m_w_proj_lru'], 'm_w_out': out['m_w_out'], 'm_norm_mlp_g': out['m_norm_mlp_g'], 'm_w_up': out['m_w_up'], 'm_w_down': out['m_w_down'], 'm_norm_final_g': out['m_norm_final_g'], 'v_norm_mix_g': out['v_norm_mix_g'], 'v_w_in': out['v_w_in'], 'v_conv_w': out['v_conv_w'], 'v_conv_b': out['v_conv_b'], 'v_lru_wa': out['v_lru_wa'], 'v_lru_ba': out['v_lru_ba'], 'v_lru_wx': out['v_lru_wx'], 'v_lru_bx': out['v_lru_bx'], 'v_lru_lambda': out['v_lru_lambda'], 'v_w_proj_attn': out['v_w_proj_attn'], 'v_w_proj_lru': out['v_w_proj_lru'], 'v_w_out': out['v_w_out'], 'v_norm_mlp_g': out['v_norm_mlp_g'], 'v_w_up': out['v_w_up'], 'v_w_down': out['v_w_down'], 'v_norm_final_g': out['v_norm_final_g']}


def _loss(weights, diff, rest, loss_target):
    with _jax.named_scope("forward"):
        args = {**rest, TWIN_DIFF_INPUT: diff, **{k: w.astype(_WEIGHT_DTYPES[k]) for k, w in weights.items()}}
        y = _forward(args)
    with _jax.named_scope("loss_head"):
        err = _jnp.square(y.astype(_jnp.float32) - loss_target)
        return 0.5 * _jnp.sum(_jnp.mean(err, axis=-1)) if err.ndim else 0.5 * err


def _adamw(w, g, m, v):
    m = ADAM_B1 * m + (1.0 - ADAM_B1) * g
    v = ADAM_B2 * v + (1.0 - ADAM_B2) * _jnp.square(g)
    m_hat = m / (1.0 - ADAM_B1 ** ADAM_STEP)
    v_hat = v / (1.0 - ADAM_B2 ** ADAM_STEP)
    delta = -ADAM_LR * (m_hat / (_jnp.sqrt(v_hat) + ADAM_EPS) + ADAM_WD * w)
    return delta, m, v


def reference(x, norm_mix_g, w_in, conv_w, conv_b, lru_wa, lru_ba, lru_wx, lru_bx, lru_lambda, w_proj_attn, w_proj_lru, w_out, norm_mlp_g, w_up, w_down, norm_final_g, loss_target, m_norm_mix_g, m_w_in, m_conv_w, m_conv_b, m_lru_wa, m_lru_ba, m_lru_wx, m_lru_bx, m_lru_lambda, m_w_proj_attn, m_w_proj_lru, m_w_out, m_norm_mlp_g, m_w_up, m_w_down, m_norm_final_g, v_norm_mix_g, v_w_in, v_conv_w, v_conv_b, v_lru_wa, v_lru_ba, v_lru_wx, v_lru_bx, v_lru_lambda, v_w_proj_attn, v_w_proj_lru, v_w_out, v_norm_mlp_g, v_w_up, v_w_down, v_norm_final_g):
    given = dict(x=x, norm_mix_g=norm_mix_g, w_in=w_in, conv_w=conv_w, conv_b=conv_b, lru_wa=lru_wa, lru_ba=lru_ba, lru_wx=lru_wx, lru_bx=lru_bx, lru_lambda=lru_lambda, w_proj_attn=w_proj_attn, w_proj_lru=w_proj_lru, w_out=w_out, norm_mlp_g=norm_mlp_g, w_up=w_up, w_down=w_down, norm_final_g=norm_final_g, loss_target=loss_target, m_norm_mix_g=m_norm_mix_g, m_w_in=m_w_in, m_conv_w=m_conv_w, m_conv_b=m_conv_b, m_lru_wa=m_lru_wa, m_lru_ba=m_lru_ba, m_lru_wx=m_lru_wx, m_lru_bx=m_lru_bx, m_lru_lambda=m_lru_lambda, m_w_proj_attn=m_w_proj_attn, m_w_proj_lru=m_w_proj_lru, m_w_out=m_w_out, m_norm_mlp_g=m_norm_mlp_g, m_w_up=m_w_up, m_w_down=m_w_down, m_norm_final_g=m_norm_final_g, v_norm_mix_g=v_norm_mix_g, v_w_in=v_w_in, v_conv_w=v_conv_w, v_conv_b=v_conv_b, v_lru_wa=v_lru_wa, v_lru_ba=v_lru_ba, v_lru_wx=v_lru_wx, v_lru_bx=v_lru_bx, v_lru_lambda=v_lru_lambda, v_w_proj_attn=v_w_proj_attn, v_w_proj_lru=v_w_proj_lru, v_w_out=v_w_out, v_norm_mlp_g=v_norm_mlp_g, v_w_up=v_w_up, v_w_down=v_w_down, v_norm_final_g=v_norm_final_g)
    weights = {n: given[n] for n in TWIN_WEIGHTS}
    shared = {n: given[n] for n in SHARED_INPUTS}
    per_example = {n: given[n] for n in ['x']}
    grad_fn = _jax.value_and_grad(_loss, argnums=(0, 1))

    def one_microbatch(ex, loss_target):
        ex = dict(ex)
        diff = ex.pop(TWIN_DIFF_INPUT)
        return grad_fn(weights, diff, {**shared, **ex}, loss_target)

    if N_MICROBATCH == 1:
        loss, (grad_w, grad_x) = one_microbatch(per_example, given["loss_target"])
    else:
        def body(carry, xs):
            loss_sum, grad_sum = carry
            l_k, (gw_k, gx_k) = one_microbatch(xs[0], xs[1])
            with _jax.named_scope("update"):
                return (loss_sum + l_k, _jax.tree.map(_jnp.add, grad_sum, gw_k)), gx_k

        init = (_jnp.zeros((), _jnp.float32), _jax.tree.map(_jnp.zeros_like, weights))
        (loss, grad_w), grad_x = _jax.lax.scan(body, init, (per_example, given["loss_target"]))
    with _jax.named_scope("update"):
        delta_w, new_m, new_v = {}, {}, {}
        for n in TWIN_WEIGHTS:
            delta_w[n], new_m[n], new_v[n] = _adamw(weights[n], grad_w[n], given["m_" + n], given["v_" + n])
    return (loss, grad_x, *[grad_w[n] for n in TWIN_WEIGHTS], *[delta_w[n] for n in TWIN_WEIGHTS],
            *[new_m[n] for n in TWIN_WEIGHTS], *[new_v[n] for n in TWIN_WEIGHTS])
```

```python
import jax
import jax.numpy as jnp
from jax import lax
from jax.experimental import pallas as pl
from jax.experimental.pallas import tpu as pltpu

F32, BF16 = jnp.float32, jnp.bfloat16
MESH = pl.DeviceIdType.MESH
HBM = pl.BlockSpec(memory_space=pltpu.HBM)
N_DEV = 8
N_CHIP = 4
HEAD = 128
SPAN = 128
DILATIONS = (1, 4, 16)
CONV_TAPS = 4
LRU_C = 8.0
NORM_EPS = 1e-6
LANES = 128
SUBLANES = 8
VMEM_LIMIT = 56 * 1024 * 1024
ADAM_LR, ADAM_B1, ADAM_B2, ADAM_EPS, ADAM_WD, ADAM_STEP = 0.001, 0.9, 0.999, 1e-08, 0.01, 10
ADAM_C1 = 1.0 - ADAM_B1 ** ADAM_STEP
ADAM_C2 = 1.0 - ADAM_B2 ** ADAM_STEP
NEG = -1e30


def _params(sem=None):
    return pltpu.CompilerParams(dimension_semantics=sem, vmem_limit_bytes=VMEM_LIMIT)


def _sigmoid(v):
    return 1.0 / (1.0 + jnp.exp(-v))


def _gelu(v):
    k = 0.7978845608028654
    return 0.5 * v * (1.0 + jnp.tanh(k * (v + 0.044715 * v * v * v)))


def _gelu_grad(v):
    k = 0.7978845608028654
    t = jnp.tanh(k * (v + 0.044715 * v * v * v))
    return 0.5 * (1.0 + t) + 0.5 * v * (1.0 - t * t) * k * (1.0 + 3.0 * 0.044715 * v * v)


NN = (((1,), (0,)), ((), ()))
NT = (((1,), (1,)), ((), ()))
TN = (((0,), (0,)), ((), ()))


def _mm(name, a, a_spec, b, b_spec, dn, grid, out_shapes, out_specs, acc_block, epilogue=None, extras=(), extra_specs=()):
    nk, ne, no = grid[2], len(extras), len(out_shapes)

    def body(*refs):
        a_ref, b_ref = refs[0], refs[1]
        ex, outs = refs[2:2 + ne], refs[2 + ne:2 + ne + no]
        part = lax.dot_general(a_ref[...], b_ref[...], dn, preferred_element_type=F32)

        def finish(acc):
            vals = epilogue(acc, *[e[...] for e in ex]) if epilogue is not None else (acc,)
            for o, v in zip(outs, vals):
                o[...] = v.astype(o.dtype)

        if nk == 1:
            finish(part)
        else:
            acc_ref, k = refs[-1], pl.program_id(2)

            @pl.when(k == 0)
            def _():
                acc_ref[...] = part

            @pl.when(k > 0)
            def _():
                acc_ref[...] += part

            @pl.when(k == nk - 1)
            def _():
                finish(acc_ref[...])

    return pl.pallas_call(
        body, out_shape=out_shapes, grid=grid, in_specs=[a_spec, b_spec, *extra_specs], out_specs=out_specs,
        scratch_shapes=[pltpu.VMEM(acc_block, F32)] if nk > 1 else [],
        compiler_params=_params(("parallel", "parallel", "arbitrary")), name=name)(a, b, *extras)


def _blk(n, pref):
    return pref if n % pref == 0 else n


def _kblk(k):
    return k if k <= 2048 else next(b for b in (2048, 1024, 512) if k % b == 0)


def _mm_fwd(name, a, w, col0, ncols, out_dtypes, epilogue=None, extras=(), extra_specs_fn=None, seg_out=None, bm=1024, bn=1024):
    m, k = a.shape
    bm, bn = _blk(m, bm), _blk(ncols, bn)
    bk = _kblk(k)
    nk = k // bk
    cb0 = col0 // bn
    grid = (m // bm, ncols // bn, nk)
    a_spec = pl.BlockSpec((bm, bk), lambda i, j, kk: (i, kk))
    b_spec = pl.BlockSpec((bk, bn), lambda i, j, kk: (kk, cb0 + j))
    if seg_out is None:
        shapes = [jax.ShapeDtypeStruct((m, ncols), dt) for dt in out_dtypes]
        specs = [pl.BlockSpec((bm, bn), lambda i, j, kk: (i, j)) for _ in out_dtypes]
    else:
        per = seg_out // bn
        shapes = [jax.ShapeDtypeStruct((ncols // seg_out, m, seg_out), dt) for dt in out_dtypes]
        specs = [pl.BlockSpec((None, bm, bn), lambda i, j, kk: (j // per, i, j % per)) for _ in out_dtypes]
    ex_specs = extra_specs_fn(bm, bn) if extra_specs_fn else ()
    return _mm(name, a, a_spec, w, b_spec, NN, grid, shapes, specs, (bm, bn), epilogue, extras, ex_specs)


def _mm_nt(name, a, w, out_dtypes, epilogue=None, extras=(), extra_specs_fn=None, bm=1024, bn=1024):
    m, k = a.shape
    n = w.shape[0]
    bm, bn = _blk(m, bm), _blk(n, bn)
    bk = _kblk(k)
    grid = (m // bm, n // bn, k // bk)
    a_spec = pl.BlockSpec((bm, bk), lambda i, j, kk: (i, kk))
    b_spec = pl.BlockSpec((bn, bk), lambda i, j, kk: (j, kk))
    shapes = [jax.ShapeDtypeStruct((m, n), dt) for dt in out_dtypes]
    specs = [pl.BlockSpec((bm, bn), lambda i, j, kk: (i, j)) for _ in out_dtypes]
    ex_specs = extra_specs_fn(bm, bn) if extra_specs_fn else ()
    return _mm(name, a, a_spec, w, b_spec, NT, grid, shapes, specs, (bm, bn), epilogue, extras, ex_specs)


def _mm_tn(name, a, b, bm=1024, bn=1024):
    t, m = a.shape
    n = b.shape[1]
    bm, bn = _blk(m, bm), _blk(n, bn)
    grid = (m // bm, n // bn, 1)
    a_spec = pl.BlockSpec((t, bm), lambda i, j, kk: (0, i))
    b_spec = pl.BlockSpec((t, bn), lambda i, j, kk: (0, j))
    return _mm(name, a, a_spec, b, b_spec, TN, grid, [jax.ShapeDtypeStruct((m, n), BF16)],
               [pl.BlockSpec((bm, bn), lambda i, j, kk: (i, j))], (bm, bn))[0]


ROWS = 256


def _row_spec(d):
    return pl.BlockSpec((ROWS, d), lambda i: (i, 0))


def _vec_spec(d, rows=1):
    return pl.BlockSpec((rows, d), lambda i: (0, 0))


def _rms_fwd(name, x, g):
    s, d = x.shape

    def body(x_ref, g_ref, o_ref):
        xv = x_ref[...]
        r = lax.rsqrt(jnp.mean(xv * xv, axis=-1, keepdims=True) + NORM_EPS)
        o_ref[...] = (xv * r * g_ref[...]).astype(BF16)

    return pl.pallas_call(body, out_shape=jax.ShapeDtypeStruct((s, d), BF16), grid=(s // ROWS,),
                          in_specs=[_row_spec(d), _vec_spec(d)], out_specs=_row_spec(d),
                          compiler_params=_params(("parallel",)), name=name)(x, g)


def _rms_bwd_math(xv, g, dy):
    r = lax.rsqrt(jnp.mean(xv * xv, axis=-1, keepdims=True) + NORM_EPS)
    n = xv * r
    z = dy * g
    dx = r * (z - n * jnp.mean(z * n, axis=-1, keepdims=True))
    return dx, jnp.sum(dy * n, axis=0, keepdims=True)


def _rms_bwd(name, x, g, dy, resid):
    s, d = x.shape

    def body(x_ref, g_ref, dy_ref, r_ref, dx_ref, dxb_ref, dg_ref):
        dx, dg = _rms_bwd_math(x_ref[...], g_ref[...], dy_ref[...])
        dx = dx + r_ref[...]
        dx_ref[...] = dx
        dxb_ref[...] = dx.astype(BF16)

        @pl.when(pl.program_id(0) == 0)
        def _():
            dg_ref[...] = jnp.zeros_like(dg_ref)

        dg_ref[...] += dg

    return pl.pallas_call(
        body, out_shape=[jax.ShapeDtypeStruct((s, d), F32), jax.ShapeDtypeStruct((s, d), BF16), jax.ShapeDtypeStruct((1, d), F32)],
        grid=(s // ROWS,), in_specs=[_row_spec(d), _vec_spec(d), _row_spec(d), _row_spec(d)],
        out_specs=[_row_spec(d), _row_spec(d), _vec_spec(d)], compiler_params=_params(("arbitrary",)), name=name)(x, g, dy, resid)


def _final_loss(h2, tgt, g):
    s, d = h2.shape

    def body(x_ref, t_ref, g_ref, dx_ref, dxb_ref, dg_ref, ls_ref):
        xv, gv = x_ref[...], g_ref[...]
        r = lax.rsqrt(jnp.mean(xv * xv, axis=-1, keepdims=True) + NORM_EPS)
        diff = xv * r * gv - t_ref[...]
        dx, dg = _rms_bwd_math(xv, gv, diff * (1.0 / d))
        dx_ref[...] = dx
        dxb_ref[...] = dx.astype(BF16)

        @pl.when(pl.program_id(0) == 0)
        def _():
            dg_ref[...] = jnp.zeros_like(dg_ref)
            ls_ref[...] = jnp.zeros_like(ls_ref)

        dg_ref[...] += dg
        ls_ref[...] += jnp.sum(diff * diff, axis=0, keepdims=True)

    return pl.pallas_call(
        body, out_shape=[jax.ShapeDtypeStruct((s, d), F32), jax.ShapeDtypeStruct((s, d), BF16),
                         jax.ShapeDtypeStruct((1, d), F32), jax.ShapeDtypeStruct((1, d), F32)],
        grid=(s // ROWS,), in_specs=[_row_spec(d), _row_spec(d), _vec_spec(d)],
        out_specs=[_row_spec(d), _row_spec(d), _vec_spec(d), _vec_spec(d)],
        compiler_params=_params(("arbitrary",)), name="final_norm_loss")(h2, tgt, g)


def _attn_units(s):
    units = []
    for gi, d in enumerate(DILATIONS):
        for r in range(d):
            for n in range(s // d // HEAD):
                units.append((gi, d, r, n))
    return units


def _stream_rows(d, r, first_blk, n_blk):
    start, size = r + first_blk * HEAD * d, n_blk * HEAD
    return pl.ds(start, size) if d == 1 else pl.ds(start, size, stride=d)


def _attn_scores(q_ref, k_ref, slope, d, r, n):
    nkb = 1 if n == 0 else 2
    qrows, krows = _stream_rows(d, r, n, 1), _stream_rows(d, r, n - nkb + 1, nkb)
    qb = q_ref[qrows, :].astype(BF16)
    kb = k_ref[krows, :].astype(BF16)
    sc = lax.dot_general(qb, kb, NT, preferred_element_type=F32) * (HEAD ** -0.5)
    qi = lax.broadcasted_iota(jnp.int32, (HEAD, nkb * HEAD), 0)
    kj = lax.broadcasted_iota(jnp.int32, (HEAD, nkb * HEAD), 1)
    dist = (nkb - 1) * HEAD + qi - kj
    valid = (dist >= 0) & (dist <= SPAN)
    sc = sc - (slope * d) * dist.astype(F32)
    return jnp.where(valid, sc, NEG), valid, qb, kb, qrows, krows


def _attn_fwd(qkv, slopes):
    _, s, dm = qkv.shape
    units = _attn_units(s)

    def body(sl_ref, q_ref, k_ref, v_ref, att_ref, lse_ref, *scr):
        o_scr, l_scr = scr[:3], scr[3:]
        slope = sl_ref[pl.program_id(0)]
        for gi, d, r, n in units:
            sc, _, _, _, qrows, krows = _attn_scores(q_ref, k_ref, slope, d, r, n)
            m = jnp.max(sc, axis=-1, keepdims=True)
            p = jnp.exp(sc - m)
            l = jnp.sum(p, axis=-1, keepdims=True)
            vb = v_ref[krows, :].astype(BF16)
            o = lax.dot_general(p.astype(BF16), vb, NN, preferred_element_type=F32) / l
            o_scr[gi][qrows, :] = o
            l_scr[gi][qrows, :] = jnp.broadcast_to(m + jnp.log(l), (HEAD, HEAD))
        l0, l1, l2 = l_scr[0][...], l_scr[1][...], l_scr[2][...]
        m = jnp.maximum(jnp.maximum(l0, l1), l2)
        w0, w1, w2 = jnp.exp(l0 - m), jnp.exp(l1 - m), jnp.exp(l2 - m)
        tot = w0 + w1 + w2
        att_ref[...] = ((w0 * o_scr[0][...] + w1 * o_scr[1][...] + w2 * o_scr[2][...]) / tot).astype(BF16)
        lse_ref[...] = m + jnp.log(tot)

    def seg(i):
        return pl.BlockSpec((None, s, HEAD), lambda h: (i, 0, h))

    col = pl.BlockSpec((s, HEAD), lambda h: (0, h))
    return pl.pallas_call(
        body, out_shape=[jax.ShapeDtypeStruct((s, dm), BF16), jax.ShapeDtypeStruct((s, dm), F32)], grid=(dm // HEAD,),
        in_specs=[pl.BlockSpec(memory_space=pltpu.SMEM), seg(0), seg(1), seg(2)], out_specs=[col, col],
        scratch_shapes=[pltpu.VMEM((s, HEAD), F32)] * (2 * len(DILATIONS)),
        compiler_params=_params(("parallel",)), name="attn_fwd")(slopes, qkv, qkv, qkv)


def _attn_bwd(qkv, datt, att, lse, slopes):
    _, s, dm = qkv.shape
    units = _attn_units(s)

    def body(sl_ref, q_ref, k_ref, v_ref, do_ref, att_ref, lse_ref, dq_ref, dk_ref, dv_ref, dq_scr, dk_scr, dv_scr, dl_scr):
        slope = sl_ref[pl.program_id(0)]
        delta = jnp.sum(do_ref[...] * att_ref[...].astype(F32), axis=-1, keepdims=True)
        dl_scr[...] = jnp.broadcast_to(delta, (s, HEAD))
        dq_scr[...] = jnp.zeros_like(dq_scr)
        dk_scr[...] = jnp.zeros_like(dk_scr)
        dv_scr[...] = jnp.zeros_like(dv_scr)
        for gi, d, r, n in units:
            sc, valid, qb, kb, qrows, krows = _attn_scores(q_ref, k_ref, slope, d, r, n)
            p = jnp.where(valid, jnp.exp(sc - lse_ref[qrows, :][:, 0:1]), 0.0)
            vb = v_ref[krows, :].astype(BF16)
            dob = do_ref[qrows, :].astype(BF16)
            dp = lax.dot_general(dob, vb, NT, preferred_element_type=F32)
            ds = (p * (dp - dl_scr[qrows, :][:, 0:1]) * (HEAD ** -0.5)).astype(BF16)
            dq_scr[qrows, :] += lax.dot_general(ds, kb, NN, preferred_element_type=F32)
            dk_scr[krows, :] += lax.dot_general(ds, qb, TN, preferred_element_type=F32)
            dv_scr[krows, :] += lax.dot_general(p.astype(BF16), dob, TN, preferred_element_type=F32)
        dq_ref[...] = dq_scr[...].astype(BF16)
        dk_ref[...] = dk_scr[...].astype(BF16)
        dv_ref[...] = dv_scr[...].astype(BF16)

    def seg(i):
        return pl.BlockSpec((None, s, HEAD), lambda h: (i, 0, h))

    col = pl.BlockSpec((s, HEAD), lambda h: (0, h))
    return pl.pallas_call(
        body, out_shape=[jax.ShapeDtypeStruct((s, dm), BF16)] * 3, grid=(dm // HEAD,),
        in_specs=[pl.BlockSpec(memory_space=pltpu.SMEM), seg(0), seg(1), seg(2), col, col, col], out_specs=[col, col, col],
        scratch_shapes=[pltpu.VMEM((s, HEAD), F32)] * 4,
        compiler_params=_params(("parallel",)), name="attn_bwd")(slopes, qkv, qkv, qkv, datt, att, lse)


VEC_CB, VEC_BA, VEC_BX, VEC_LAM = 0, 1, 2, 3


def _to_3d(ref3, val):
    lw = val.shape[1] // SUBLANES
    for j in range(SUBLANES):
        ref3[:, j, :] = val[:, j * lw:(j + 1) * lw]


def _from_3d(ref3):
    return jnp.concatenate([ref3[:, j, :] for j in range(SUBLANES)], axis=1)


def _softplus(z):
    return jnp.maximum(z, 0.0) + jnp.log1p(jnp.exp(-jnp.abs(z)))


def _gate_math(xc, wa_ref, wx_ref, vec):
    xcb = xc.astype(BF16)
    nh = xc.shape[1] // HEAD
    pre_a = jnp.concatenate([jnp.dot(xcb[:, h * HEAD:(h + 1) * HEAD], wa_ref[h], preferred_element_type=F32) for h in range(nh)], axis=1)
    pre_x = jnp.concatenate([jnp.dot(xcb[:, h * HEAD:(h + 1) * HEAD], wx_ref[h], preferred_element_type=F32) for h in range(nh)], axis=1)
    ra = _sigmoid(pre_a + vec[VEC_BA:VEC_BA + 1])
    ig = _sigmoid(pre_x + vec[VEC_BX:VEC_BX + 1])
    sp = _softplus(-vec[VEC_LAM:VEC_LAM + 1])
    log_a = -LRU_C * ra * sp
    a = jnp.exp(log_a)
    z = 2.0 * log_a
    one_minus_a2 = jnp.where(z > -0.01, -z * (1.0 + z * (0.5 + z * (1.0 / 6.0))), 1.0 - jnp.exp(z))
    mult = jnp.sqrt(one_minus_a2)
    return dict(xcb=xcb, ra=ra, ig=ig, sp=sp, a=a, mult=mult)


def _conv_pad_prev(pad_ref, cur, halo, first):
    pad_ref[0:SUBLANES, :] = jnp.where(first, 0.0, halo)
    pad_ref[SUBLANES:SUBLANES + cur.shape[0], :] = cur


def _gates_fwd(rest, cw8, vec8, wa, wx):
    _, s, d = rest.shape
    lw = d // SUBLANES
    hb = ROWS // SUBLANES

    def body(x_ref, halo_ref, cw_ref, vec_ref, wa_ref, wx_ref, a_ref, u_ref, xc_ref, pad):
        _conv_pad_prev(pad, x_ref[...], halo_ref[...], pl.program_id(0) == 0)
        vec = vec_ref[...]
        xc = vec[VEC_CB:VEC_CB + 1]
        for k in range(CONV_TAPS):
            xc = xc + cw_ref[k:k + 1, :] * pad[pl.ds(SUBLANES - (CONV_TAPS - 1) + k, ROWS), :]
        gm = _gate_math(xc, wa_ref, wx_ref, vec)
        xc_ref[...] = xc
        _to_3d(a_ref, gm["a"])
        _to_3d(u_ref, gm["mult"] * (gm["ig"] * xc))

    spec3 = pl.BlockSpec((ROWS, SUBLANES, lw), lambda i: (i, 0, 0))
    wspec = pl.BlockSpec(wa.shape, lambda i: (0, 0, 0))
    return pl.pallas_call(
        body, out_shape=[jax.ShapeDtypeStruct((s, SUBLANES, lw), F32)] * 2 + [jax.ShapeDtypeStruct((s, d), F32)], grid=(s // ROWS,),
        in_specs=[pl.BlockSpec((None, ROWS, d), lambda i: (0, i, 0)),
                  pl.BlockSpec((None, SUBLANES, d), lambda i: (0, jnp.maximum(i * hb - 1, 0), 0)),
                  _vec_spec(d, SUBLANES), _vec_spec(d, SUBLANES), wspec, wspec],
        out_specs=[spec3, spec3, _row_spec(d)], scratch_shapes=[pltpu.VMEM((ROWS + SUBLANES, d), F32)],
        compiler_params=_params(("parallel",)), name="lru_gates_fwd")(rest, rest, cw8, vec8, wa, wx)


def _scan_fwd(a3, u3):
    s, _, lw = a3.shape

    def body(a_ref, u_ref, h_ref, hp_ref, carry):
        @pl.when(pl.program_id(0) == 0)
        def _():
            carry[...] = jnp.zeros_like(carry)

        def step(t, h):
            hp_ref[t] = h
            hn = a_ref[t] * h + u_ref[t]
            h_ref[t] = hn
            return hn

        carry[...] = lax.fori_loop(0, ROWS, step, carry[...], unroll=8)

    spec3 = pl.BlockSpec((ROWS, SUBLANES, lw), lambda i: (i, 0, 0))
    return pl.pallas_call(body, out_shape=[jax.ShapeDtypeStruct(a3.shape, F32)] * 2, grid=(s // ROWS,), in_specs=[spec3, spec3],
                          out_specs=[spec3, spec3], scratch_shapes=[pltpu.VMEM((SUBLANES, lw), F32)],
                          compiler_params=_params(("arbitrary",)), name="lru_scan_fwd")(a3, u3)


def _lru_out(h3, rest):
    s, _, lw = h3.shape
    d = lw * SUBLANES

    def body(h_ref, g_ref, y_ref, h2_ref):
        h = _from_3d(h_ref)
        h2_ref[...] = h
        y_ref[...] = (h * _gelu(g_ref[...])).astype(BF16)

    return pl.pallas_call(
        body, out_shape=[jax.ShapeDtypeStruct((s, d), BF16), jax.ShapeDtypeStruct((s, d), F32)], grid=(s // ROWS,),
        in_specs=[pl.BlockSpec((ROWS, SUBLANES, lw), lambda i: (i, 0, 0)), pl.BlockSpec((None, ROWS, d), lambda i: (1, i, 0))],
        out_specs=[_row_spec(d), _row_spec(d)], compiler_params=_params(("parallel",)), name="lru_out")(h3, rest)


def _scan_bwd(a3, hp3, dh):
    s, _, lw = a3.shape
    d = lw * SUBLANES
    nb = s // ROWS

    def body(a_ref, hp_ref, dh_ref, g_ref, da_ref, dh3, carry):
        @pl.when(pl.program_id(0) == 0)
        def _():
            carry[...] = jnp.zeros_like(carry)

        _to_3d(dh3, dh_ref[...])

        def step(j, c):
            t = ROWS - 1 - j
            g = dh3[t] + c
            g_ref[t] = g
            da_ref[t] = g * hp_ref[t]
            return a_ref[t] * g

        carry[...] = lax.fori_loop(0, ROWS, step, carry[...], unroll=8)

    spec3 = pl.BlockSpec((ROWS, SUBLANES, lw), lambda i: (nb - 1 - i, 0, 0))
    return pl.pallas_call(
        body, out_shape=[jax.ShapeDtypeStruct(a3.shape, F32)] * 2, grid=(nb,),
        in_specs=[spec3, spec3, pl.BlockSpec((ROWS, d), lambda i: (nb - 1 - i, 0))], out_specs=[spec3, spec3],
        scratch_shapes=[pltpu.VMEM((ROWS, SUBLANES, lw), F32), pltpu.VMEM((SUBLANES, lw), F32)],
        compiler_params=_params(("arbitrary",)), name="lru_scan_bwd")(a3, hp3, dh)


def _gates_bwd(g3, da3, xc, wa, wx, vec8):
    s, d = xc.shape
    lw = d // SUBLANES
    nh = d // HEAD

    def body(g_ref, da_ref, xc_ref, wa_ref, wx_ref, vec_ref, dxc_ref, dwa_ref, dwx_ref, dvec_ref):
        @pl.when(pl.program_id(0) == 0)
        def _():
            dwa_ref[...] = jnp.zeros_like(dwa_ref)
            dwx_ref[...] = jnp.zeros_like(dwx_ref)
            dvec_ref[...] = jnp.zeros_like(dvec_ref)

        xc_v, vec = xc_ref[...], vec_ref[...]
        du, da = _from_3d(g_ref), _from_3d(da_ref)
        gm = _gate_math(xc_v, wa_ref, wx_ref, vec)
        ra, ig, sp, a, mult = gm["ra"], gm["ig"], gm["sp"], gm["a"], gm["mult"]
        dmult = du * ig * xc_v
        dlog_a = da * a - dmult * (a * a) / mult
        dpre_a = dlog_a * (-LRU_C * sp) * ra * (1.0 - ra)
        dpre_x = du * mult * xc_v * ig * (1.0 - ig)
        dlam = jnp.sum(dlog_a * (-LRU_C * ra), axis=0, keepdims=True) * (-_sigmoid(-vec[VEC_LAM:VEC_LAM + 1]))
        dvec_ref[VEC_BA:VEC_BA + 1, :] += jnp.sum(dpre_a, axis=0, keepdims=True)
        dvec_ref[VEC_BX:VEC_BX + 1, :] += jnp.sum(dpre_x, axis=0, keepdims=True)
        dvec_ref[VEC_LAM:VEC_LAM + 1, :] += dlam
        dab, dxb, xcb = dpre_a.astype(BF16), dpre_x.astype(BF16), gm["xcb"]
        back = []
        for h in range(nh):
            cols = slice(h * HEAD, (h + 1) * HEAD)
            dwa_ref[h] += lax.dot_general(xcb[:, cols], dab[:, cols], TN, preferred_element_type=F32)
            dwx_ref[h] += lax.dot_general(xcb[:, cols], dxb[:, cols], TN, preferred_element_type=F32)
            back.append(lax.dot_general(dab[:, cols], wa_ref[h], NT, preferred_element_type=F32)
                        + lax.dot_general(dxb[:, cols], wx_ref[h], NT, preferred_element_type=F32))
        dxc_ref[...] = du * mult * ig + jnp.concatenate(back, axis=1)

    spec3 = pl.BlockSpec((ROWS, SUBLANES, lw), lambda i: (i, 0, 0))
    wspec = pl.BlockSpec(wa.shape, lambda i: (0, 0, 0))
    return pl.pallas_call(
        body, out_shape=[jax.ShapeDtypeStruct((s, d), F32), jax.ShapeDtypeStruct(wa.shape, F32), jax.ShapeDtypeStruct(wa.shape, F32),
                         jax.ShapeDtypeStruct((SUBLANES, d), F32)],
        grid=(s // ROWS,), in_specs=[spec3, spec3, _row_spec(d), wspec, wspec, _vec_spec(d, SUBLANES)],
        out_specs=[_row_spec(d), wspec, wspec, _vec_spec(d, SUBLANES)],
        compiler_params=_params(("arbitrary",)), name="lru_gates_bwd")(g3, da3, xc, wa, wx, vec8)


def _conv_bwd(dxc, rest, cw8):
    s, d = dxc.shape
    hb = ROWS // SUBLANES
    last = s // SUBLANES - 1

    def body(dc_ref, dnext_ref, x_ref, xprev_ref, cw_ref, dx_ref, dcw_ref, padd, padx):
        i = pl.program_id(0)

        @pl.when(i == 0)
        def _():
            dcw_ref[...] = jnp.zeros_like(dcw_ref)

        dc = dc_ref[...]
        padd[0:ROWS, :] = dc
        padd[ROWS:ROWS + SUBLANES, :] = jnp.where(i == pl.num_programs(0) - 1, 0.0, dnext_ref[...])
        _conv_pad_prev(padx, x_ref[...], xprev_ref[...], i == 0)
        dx = jnp.zeros_like(dc)
        for k in range(CONV_TAPS):
            dx = dx + cw_ref[k:k + 1, :] * padd[pl.ds(CONV_TAPS - 1 - k, ROWS), :]
            dcw_ref[k:k + 1, :] += jnp.sum(dc * padx[pl.ds(SUBLANES - (CONV_TAPS - 1) + k, ROWS), :], axis=0, keepdims=True)
        dcw_ref[CONV_TAPS:CONV_TAPS + 1, :] += jnp.sum(dc, axis=0, keepdims=True)
        dx_ref[...] = dx.astype(BF16)

    return pl.pallas_call(
        body, out_shape=[jax.ShapeDtypeStruct((s, d), BF16), jax.ShapeDtypeStruct((SUBLANES, d), F32)], grid=(s // ROWS,),
        in_specs=[_row_spec(d), pl.BlockSpec((SUBLANES, d), lambda i: (jnp.minimum((i + 1) * hb, last), 0)),
                  pl.BlockSpec((None, ROWS, d), lambda i: (0, i, 0)),
                  pl.BlockSpec((None, SUBLANES, d), lambda i: (0, jnp.maximum(i * hb - 1, 0), 0)), _vec_spec(d, SUBLANES)],
        out_specs=[_row_spec(d), _vec_spec(d, SUBLANES)],
        scratch_shapes=[pltpu.VMEM((ROWS + SUBLANES, d), F32), pltpu.VMEM((ROWS + SUBLANES, d), F32)],
        compiler_params=_params(("arbitrary",)), name="lru_conv_bwd")(dxc, dxc, rest, rest, cw8)


def _coords():
    return lax.axis_index("x"), lax.axis_index("y"), lax.axis_index("c")


def _other_chips(x, y):
    return [(1 - x, y), (x, 1 - y), (1 - x, 1 - y)]


def _slab(ref, kind, shard_shape, idx):
    r, c = shard_shape
    if kind == "col":
        return ref.at[:, pl.ds(pl.multiple_of(idx * c, LANES), c)]
    if kind == "row":
        return ref.at[pl.ds(pl.multiple_of(idx * r, SUBLANES), r), :]
    return ref.at[idx]


def _full_shape(shard_shape, kind):
    r, c = shard_shape
    return {"col": (r, c * N_DEV), "row": (r * N_DEV, c), "slot": (N_DEV, r, c)}[kind]


def _all_gather(shards, kinds):
    n = len(shards)
    shapes = [s.shape for s in shards]

    def body(*refs):
        ins, outs = refs[:n], refs[n:2 * n]
        send_sems, recv_sems, local_sems = refs[2 * n:]
        x, y, c = _coords()
        me, sib = (x, y, c), (x, y, 1 - c)
        chips = _other_chips(x, y)

        def part(i, dev):
            return _slab(outs[i], kinds[i], shapes[i], 4 * dev[0] + 2 * dev[1] + dev[2])

        def copy(i, k, block, to, src=None):
            return pltpu.make_async_remote_copy(
                src_ref=part(i, block) if src is None else src, dst_ref=part(i, block),
                send_sem=send_sems.at[7 * i + k], recv_sem=recv_sems.at[7 * i + k], device_id=to, device_id_type=MESH)

        started = []
        for i in range(n):
            for k, to in enumerate([sib] + [(*chip, c) for chip in chips]):
                cp = copy(i, k, me, to, src=ins[i])
                cp.start()
                started.append(cp)
        mine = [pltpu.make_async_copy(ins[i], part(i, me), local_sems.at[i]) for i in range(n)]
        for cp in mine:
            cp.start()
        for i in range(n):
            for j, chip in enumerate(chips):
                copy(i, 1 + j, (*chip, c), me).wait_recv()
                cp = copy(i, 4 + j, (*chip, c), sib)
                cp.start()
                started.append(cp)
        for i in range(n):
            copy(i, 0, sib, me).wait_recv()
            for j, chip in enumerate(chips):
                copy(i, 4 + j, (*chip, 1 - c), me).wait_recv()
        for cp in started:
            cp.wait_send()
        for cp in mine:
            cp.wait()

    return pl.pallas_call(
        body, out_shape=[jax.ShapeDtypeStruct(_full_shape(s.shape, k), s.dtype) for s, k in zip(shards, kinds)],
        in_specs=[HBM] * n, out_specs=[HBM] * n,
        scratch_shapes=[pltpu.SemaphoreType.DMA((7 * n,)), pltpu.SemaphoreType.DMA((7 * n,)), pltpu.SemaphoreType.DMA((n,))],
        name="all_gather_weights")(*shards)


def _exchange_siblings(partials, kinds, shard_shapes):
    n = len(partials)

    def body(*refs):
        ins, outs = refs[:n], refs[n:2 * n]
        send_sems, recv_sems = refs[2 * n:]
        x, y, c = _coords()
        cps = []
        for i in range(n):
            for q in range(N_CHIP):
                cps.append(pltpu.make_async_remote_copy(
                    src_ref=_slab(ins[i], kinds[i], shard_shapes[i], 2 * q + (1 - c)), dst_ref=outs[i].at[q],
                    send_sem=send_sems.at[N_CHIP * i + q], recv_sem=recv_sems.at[N_CHIP * i + q],
                    device_id=(x, y, 1 - c), device_id_type=MESH))
        for cp in cps:
            cp.start()
        for cp in cps:
            cp.wait()

    return pl.pallas_call(
        body, out_shape=[jax.ShapeDtypeStruct((N_CHIP, *s), BF16) for s in shard_shapes], in_specs=[HBM] * n, out_specs=[HBM] * n,
        scratch_shapes=[pltpu.SemaphoreType.DMA((N_CHIP * n,)), pltpu.SemaphoreType.DMA((N_CHIP * n,))],
        name="reduce_scatter_siblings")(*partials)


def _exchange_chips(chip_sums):
    n = len(chip_sums)

    def body(*refs):
        ins, outs = refs[:n], refs[n:2 * n]
        send_sems, recv_sems = refs[2 * n:]
        x, y, c = _coords()
        cps = []
        for i in range(n):
            for k, (cx, cy) in enumerate(_other_chips(x, y)):
                cps.append(pltpu.make_async_remote_copy(
                    src_ref=ins[i].at[2 * cx + cy], dst_ref=outs[i].at[k], send_sem=send_sems.at[3 * i + k],
                    recv_sem=recv_sems.at[3 * i + k], device_id=(cx, cy, c), device_id_type=MESH))
        for cp in cps:
            cp.start()
        for cp in cps:
            cp.wait()

    return pl.pallas_call(
        body, out_shape=[jax.ShapeDtypeStruct((3, *t.shape[1:]), BF16) for t in chip_sums], in_specs=[HBM] * n, out_specs=[HBM] * n,
        scratch_shapes=[pltpu.SemaphoreType.DMA((3 * n,)), pltpu.SemaphoreType.DMA((3 * n,))],
        name="reduce_scatter_chips")(*chip_sums)


def _all_reduce_small(packed):
    rows = packed.shape[0] // N_DEV

    def body(p_ref, out_ref, rb, tot, send_sems, recv_sems):
        x, y, c = _coords()
        me = 4 * x + 2 * y + c

        def peer(k):
            return (x ^ (k >> 2), y ^ ((k >> 1) & 1), c ^ (k & 1))

        def rows_of(idx):
            return pl.ds(pl.multiple_of(idx * rows, SUBLANES), rows)

        def piece(ref, idx):
            return ref.at[rows_of(idx), :]

        scatter = [pltpu.make_async_remote_copy(src_ref=piece(p_ref, me ^ k), dst_ref=rb.at[k], send_sem=send_sems.at[k],
                                                recv_sem=recv_sems.at[k], device_id=peer(k), device_id_type=MESH) for k in range(1, N_DEV)]
        for cp in scatter:
            cp.start()
        acc = p_ref[rows_of(me), :]
        for cp in scatter:
            cp.wait_recv()
        for k in range(1, N_DEV):
            acc = acc + rb[k]
        tot[...] = acc
        out_ref[rows_of(me), :] = acc
        gather = [pltpu.make_async_remote_copy(src_ref=tot, dst_ref=piece(out_ref, me), send_sem=send_sems.at[N_DEV + k],
                                               recv_sem=recv_sems.at[N_DEV + k], device_id=peer(k), device_id_type=MESH)
                  for k in range(1, N_DEV)]
        for cp in gather:
            cp.start()
        for k in range(1, N_DEV):
            pltpu.make_async_remote_copy(src_ref=tot, dst_ref=piece(out_ref, me ^ k), send_sem=send_sems.at[N_DEV + k],
                                         recv_sem=recv_sems.at[N_DEV + k], device_id=peer(k), device_id_type=MESH).wait_recv()
        for cp in scatter + gather:
            cp.wait_send()

    vm = pl.BlockSpec(memory_space=pltpu.VMEM)
    return pl.pallas_call(
        body, out_shape=jax.ShapeDtypeStruct(packed.shape, F32), in_specs=[vm], out_specs=vm,
        scratch_shapes=[pltpu.VMEM((N_DEV, rows, LANES), F32), pltpu.VMEM((rows, LANES), F32),
                        pltpu.SemaphoreType.DMA((2 * N_DEV,)), pltpu.SemaphoreType.DMA((2 * N_DEV,))],
        compiler_params=pltpu.CompilerParams(vmem_limit_bytes=VMEM_LIMIT), name="all_reduce_small")(packed)


def _adamw_math(g, w, m, v):
    m = ADAM_B1 * m + (1.0 - ADAM_B1) * g
    v = ADAM_B2 * v + (1.0 - ADAM_B2) * (g * g)
    delta = -ADAM_LR * ((m / ADAM_C1) / (jnp.sqrt(v / ADAM_C2) + ADAM_EPS) + ADAM_WD * w)
    return delta, m, v


def _slab_spec(kind, shard_shape, tr, slab_of):
    r, c = shard_shape
    if kind == "col":
        return pl.BlockSpec((tr, c), lambda q, i, sc: (i, slab_of(q, sc)))
    return pl.BlockSpec((tr, c), lambda q, i, sc: (slab_of(q, sc) * (r // tr) + i, 0))


def _chip_sum(name, partial, recv, kind, shard_shape, core):
    r, c = shard_shape
    tr = _blk(r, 256)

    def body(core_ref, p_ref, r_ref, o_ref):
        o_ref[...] = (p_ref[...].astype(F32) + r_ref[...].astype(F32)).astype(BF16)

    spec4 = pl.BlockSpec((None, tr, c), lambda q, i, sc: (q, i, 0))
    grid_spec = pltpu.PrefetchScalarGridSpec(
        num_scalar_prefetch=1, grid=(N_CHIP, r // tr),
        in_specs=[_slab_spec(kind, shard_shape, tr, lambda q, sc: 2 * q + sc[0]), spec4], out_specs=spec4)
    return pl.pallas_call(body, out_shape=jax.ShapeDtypeStruct((N_CHIP, r, c), BF16), grid_spec=grid_spec,
                          compiler_params=_params(("parallel", "parallel")), name=name)(core, partial, recv)


def _adamw_shard(name, chip_sums, recv, w, m, v, chip):
    r, c = w.shape
    tr = _blk(r, 128)

    def body(chip_ref, t_ref, r_ref, w_ref, m_ref, v_ref, g_out, d_out, m_out, v_out):
        g = t_ref[...].astype(F32)
        for k in range(3):
            g = g + r_ref[k].astype(F32)
        g_out[...] = g
        d_out[...], m_out[...], v_out[...] = _adamw_math(g, w_ref[...], m_ref[...], v_ref[...])

    blk = pl.BlockSpec((tr, c), lambda i, sc: (i, 0))
    grid_spec = pltpu.PrefetchScalarGridSpec(
        num_scalar_prefetch=1, grid=(r // tr,),
        in_specs=[pl.BlockSpec((None, tr, c), lambda i, sc: (sc[0], i, 0)), pl.BlockSpec((3, tr, c), lambda i, sc: (0, i, 0)), blk, blk, blk],
        out_specs=[blk] * 4)
    return pl.pallas_call(body, out_shape=[jax.ShapeDtypeStruct((r, c), F32)] * 4, grid_spec=grid_spec,
                          compiler_params=_params(("parallel",)), name=name)(chip, chip_sums, recv, w, m, v)


def _adamw_small(name, g, w, m, v):
    def body(g_ref, w_ref, m_ref, v_ref, d_out, m_out, v_out):
        d_out[...], m_out[...], v_out[...] = _adamw_math(g_ref[...], w_ref[...], m_ref[...], v_ref[...])

    vm = pl.BlockSpec(memory_space=pltpu.VMEM)
    return pl.pallas_call(body, out_shape=[jax.ShapeDtypeStruct(g.shape, F32)] * 3, in_specs=[vm] * 4, out_specs=[vm] * 3,
                          compiler_params=pltpu.CompilerParams(vmem_limit_bytes=VMEM_LIMIT), name=name)(g, w, m, v)


def _pack_rows(arrays, total_rows):
    flat = [a.reshape(-1, LANES) for a in arrays]
    used = sum(f.shape[0] for f in flat)
    return jnp.concatenate(flat + [jnp.zeros((total_rows - used, LANES), F32)], axis=0)


def _unpack_rows(packed, like):
    out, at = [], 0
    for a in like:
        n = a.size // LANES
        out.append(packed[at:at + n].reshape(a.shape))
        at += n
    return out


def kernel(x, norm_mix_g, w_in, conv_w, conv_b, lru_wa, lru_ba, lru_wx, lru_bx, lru_lambda, w_proj_attn, w_proj_lru, w_out, norm_mlp_g, w_up, w_down, norm_final_g, loss_target, m_norm_mix_g, m_w_in, m_conv_w, m_conv_b, m_lru_wa, m_lru_ba, m_lru_wx, m_lru_bx, m_lru_lambda, m_w_proj_attn, m_w_proj_lru, m_w_out, m_norm_mlp_g, m_w_up, m_w_down, m_norm_final_g, v_norm_mix_g, v_w_in, v_conv_w, v_conv_b, v_lru_wa, v_lru_ba, v_lru_wx, v_lru_bx, v_lru_lambda, v_w_proj_attn, v_w_proj_lru, v_w_out, v_norm_mlp_g, v_w_up, v_w_down, v_norm_final_g):
    xs, tgt = x[0], loss_target[0]
    s, d = xs.shape
    nh = d // HEAD
    ix, iy, ic = _coords()
    core = jnp.reshape(ic, (1,)).astype(jnp.int32)
    chip = jnp.reshape(2 * ix + iy, (1,)).astype(jnp.int32)
    dev = 4 * ix + 2 * iy + ic

    big = [w_in[0], w_proj_attn[0], w_proj_lru[0], w_out[0], w_up[0], w_down[0]]
    big_m = [m_w_in[0], m_w_proj_attn[0], m_w_proj_lru[0], m_w_out[0], m_w_up[0], m_w_down[0]]
    big_v = [v_w_in[0], v_w_proj_attn[0], v_w_proj_lru[0], v_w_out[0], v_w_up[0], v_w_down[0]]
    kinds = ["col", "row", "row", "row", "col", "row"]
    pad_taps = lambda t: jnp.pad(t, ((0, SUBLANES - CONV_TAPS), (0, 0)))
    gathered = _all_gather([w.astype(BF16) for w in big] + [pad_taps(conv_w[0])], kinds + ["slot"])
    win, wpa, wpl, wout, wup, wdown, cw_slots = gathered
    cw8 = jnp.transpose(cw_slots, (1, 0, 2)).reshape(SUBLANES, d)
    row_id = lax.broadcasted_iota(jnp.int32, (SUBLANES, d), 0)
    vec8 = sum(jnp.where(row_id == k, t, 0.0) for k, t in ((VEC_CB, conv_b), (VEC_BA, lru_ba), (VEC_BX, lru_bx), (VEC_LAM, lru_lambda)))
    wa16, wx16 = lru_wa[0].astype(BF16), lru_wx[0].astype(BF16)
    slopes = 2.0 ** (-8.0 * jnp.arange(1, nh + 1, dtype=F32) / nh)

    def seg_specs(*segs):
        return lambda bm, bn: [pl.BlockSpec((None, bm, bn), (lambda i, j, kk, sg=sg: (sg, i, j))) for sg in segs]

    def plain_specs(k):
        return lambda bm, bn: [pl.BlockSpec((bm, bn), lambda i, j, kk: (i, j)) for _ in range(k)]

    xn = _rms_fwd("norm_mix", xs, norm_mix_g)
    qkv = _mm_fwd("proj_qkv", xn, win, 0, 3 * d, [F32], seg_out=d)[0]
    rest = _mm_fwd("proj_rest", xn, win, 3 * d, 4 * d, [F32], seg_out=d)[0]
    att, lse = _attn_fwd(qkv, slopes)
    a3, u3, xc = _gates_fwd(rest, cw8, vec8, wa16, wx16)
    h3, hp3 = _scan_fwd(a3, u3)
    ylru, h2d = _lru_out(h3, rest)
    pa = _mm_fwd("proj_attn", att, wpa, 0, d, [F32])[0]

    def merge(acc, pa_b, ga, gl):
        return acc, _sigmoid(ga) * pa_b + _sigmoid(gl) * acc

    plr, merged = _mm_fwd("proj_lru_merge", ylru, wpl, 0, d, [F32, BF16], merge, (pa, rest, rest),
                          lambda bm, bn: plain_specs(1)(bm, bn) + seg_specs(2, 3)(bm, bn), bm=512)
    h1 = _mm_fwd("mix_out", merged, wout, 0, d, [F32], lambda acc, r: (acc + r,), (xs,), plain_specs(1))[0]
    hn = _rms_fwd("norm_mlp", h1, norm_mlp_g)

    def relu2(acc):
        return acc, jnp.square(jnp.maximum(acc, 0.0))

    up, hid = _mm_fwd("mlp_up", hn, wup, 0, wup.shape[1], [BF16, BF16], relu2)
    h2 = _mm_fwd("mlp_down", hid, wdown, 0, d, [F32], lambda acc, r: (acc + r,), (h1,), plain_specs(1))[0]
    dh2, dh2b, dg3, loss_lanes = _final_loss(h2, tgt, norm_final_g.reshape(1, d))
    loss = lax.psum(0.5 / d * jnp.sum(loss_lanes), ("x", "y", "c"))

    dup = _mm_nt("mlp_down_dx", dh2b, wdown, [BF16], lambda acc, u: (acc * (2.0 * jnp.maximum(u.astype(F32), 0.0)),), (up,), plain_specs(1))[0]
    g_wdown = _mm_tn("mlp_down_dw", hid, dh2b)
    dhn = _mm_nt("mlp_up_dx", dup, wup, [F32])[0]
    g_wup = _mm_tn("mlp_up_dw", hn, dup)
    dh1, dh1b, dg2 = _rms_bwd("norm_mlp_bwd", h1, norm_mlp_g, dhn, dh2)

    def merge_bwd(acc, pa_b, pl_b, ga, gl):
        sa, sl = _sigmoid(ga), _sigmoid(gl)
        return acc * sa, acc * sl, acc * pa_b * sa * (1.0 - sa), acc * pl_b * sl * (1.0 - sl)

    dpa, dpl, dga, dgl = _mm_nt("mix_out_dx", dh1b, wout, [BF16] * 4, merge_bwd, (pa, plr, rest, rest),
                                lambda bm, bn: plain_specs(2)(bm, bn) + seg_specs(2, 3)(bm, bn), bm=512)
    g_wout = _mm_tn("mix_out_dw", merged, dh1b)
    datt = _mm_nt("proj_attn_dx", dpa, wpa, [F32])[0]
    g_wpa = _mm_tn("proj_attn_dw", att, dpa)

    def lru_out_bwd(acc, h_b, gate):
        return acc * _gelu(gate), acc * h_b * _gelu_grad(gate)

    dh, dxg = _mm_nt("proj_lru_dx", dpl, wpl, [F32, BF16], lru_out_bwd, (h2d, rest),
                     lambda bm, bn: plain_specs(1)(bm, bn) + seg_specs(1)(bm, bn), bm=512)
    g_wpl = _mm_tn("proj_lru_dw", ylru, dpl)
    dq, dk, dv = _attn_bwd(qkv, datt, att, lse, slopes)
    g3, da3 = _scan_bwd(a3, hp3, dh)
    dxc, dwa, dwx, dvec = _gates_bwd(g3, da3, xc, wa16, wx16, vec8)
    dxr, dconv = _conv_bwd(dxc, rest, cw8)
    dproj = jnp.concatenate([dq, dk, dv, dxr, dxg, dga, dgl], axis=1)
    dxn = _mm_nt("proj_in_dx", dproj, win, [F32])[0]
    g_win = _mm_tn("proj_in_dw", xn, dproj)
    grad_x, _, dg1 = _rms_bwd("norm_mix_bwd", xs, norm_mix_g, dxn, dh1)

    partials = [g_win, g_wpa, g_wpl, g_wout, g_wup, g_wdown]
    shard_shapes = [w.shape for w in big]
    from_sibling = _exchange_siblings(partials, kinds, shard_shapes)
    chip_sums = [_chip_sum(f"chip_sum_{i}", partials[i], from_sibling[i], kinds[i], shard_shapes[i], core) for i in range(len(big))]
    from_chips = _exchange_chips(chip_sums)
    big_out = [_adamw_shard(f"adamw_{i}", chip_sums[i], from_chips[i], big[i], big_m[i], big_v[i], chip) for i in range(len(big))]

    small_w = [norm_mix_g, conv_b, lru_wa, lru_ba, lru_wx, lru_bx, lru_lambda, norm_mlp_g, norm_final_g]
    small_m = [m_norm_mix_g, m_conv_b, m_lru_wa, m_lru_ba, m_lru_wx, m_lru_bx, m_lru_lambda, m_norm_mlp_g, m_norm_final_g]
    small_v = [v_norm_mix_g, v_conv_b, v_lru_wa, v_lru_ba, v_lru_wx, v_lru_bx, v_lru_lambda, v_norm_mlp_g, v_norm_final_g]
    small_g = [dg1, dconv[CONV_TAPS:CONV_TAPS + 1], dwa, dvec[VEC_BA:VEC_BA + 1], dwx, dvec[VEC_BX:VEC_BX + 1],
               dvec[VEC_LAM:VEC_LAM + 1], dg2, dg3, dconv[0:CONV_TAPS]]
    n_rows = sum(g.size for g in small_g) // LANES
    per_dev = -(-n_rows // (N_DEV * SUBLANES)) * SUBLANES
    total = _all_reduce_small(_pack_rows(small_g, N_DEV * per_dev))
    sw_rows = -(-(n_rows - CONV_TAPS * d // LANES) // SUBLANES) * SUBLANES
    s_delta, s_m, s_v = _adamw_small("adamw_small", total[:sw_rows], _pack_rows(small_w, sw_rows), _pack_rows(small_m, sw_rows),
                                     _pack_rows(small_v, sw_rows))
    sg = _unpack_rows(total, [w for w in small_w] + [jnp.zeros((1, CONV_TAPS, d), F32)])
    s_grad, g_cw_full = sg[:-1], sg[-1]
    s_delta, s_m, s_v = (_unpack_rows(t, small_w) for t in (s_delta, s_m, s_v))
    cshard = conv_w.shape[2]
    g_cw = lax.dynamic_slice(g_cw_full, (0, 0, dev * cshard), (1, CONV_TAPS, cshard))
    cw_delta, cw_m, cw_v = (t[:CONV_TAPS][None] for t in _adamw_small(
        "adamw_conv_w", pad_taps(g_cw[0]), pad_taps(conv_w[0]), pad_taps(m_conv_w[0]), pad_taps(v_conv_w[0])))

    names = ["norm_mix_g", "w_in", "conv_w", "conv_b", "lru_wa", "lru_ba", "lru_wx", "lru_bx", "lru_lambda", "w_proj_attn", "w_proj_lru",
             "w_out", "norm_mlp_g", "w_up", "w_down", "norm_final_g"]
    small_names = ["norm_mix_g", "conv_b", "lru_wa", "lru_ba", "lru_wx", "lru_bx", "lru_lambda", "norm_mlp_g", "norm_final_g"]
    big_names = ["w_in", "w_proj_attn", "w_proj_lru", "w_out", "w_up", "w_down"]
    res = {"conv_w": (g_cw, cw_delta, cw_m, cw_v)}
    for i, nm in enumerate(small_names):
        res[nm] = (s_grad[i], s_delta[i], s_m[i], s_v[i])
    for i, nm in enumerate(big_names):
        res[nm] = tuple(t[None] for t in big_out[i])
    return (loss, grad_x[None], *[res[nm][0] for nm in names], *[res[nm][1] for nm in names],
            *[res[nm][2] for nm in names], *[res[nm][3] for nm in names])
```

```python
import jax
import jax.numpy as jnp
from jax import lax
from jax.experimental import pallas as pl
from jax.experimental.pallas import tpu as pltpu
from jax.experimental.pallas import tpu_sc as plsc

F32, BF16 = jnp.float32, jnp.bfloat16
MESH = pl.DeviceIdType.MESH
HBM = pl.BlockSpec(memory_space=pltpu.HBM)
N_DEV = 8
N_CHIP = 4
HEAD = 128
SPAN = 128
DILATIONS = (1, 4, 16)
CONV_TAPS = 4
LRU_C = 8.0
NORM_EPS = 1e-6
LANES = 128
SUBLANES = 8
VMEM_LIMIT = 56 * 1024 * 1024
ADAM_LR, ADAM_B1, ADAM_B2, ADAM_EPS, ADAM_WD, ADAM_STEP = 0.001, 0.9, 0.999, 1e-08, 0.01, 10
ADAM_C1 = 1.0 - ADAM_B1 ** ADAM_STEP
ADAM_C2 = 1.0 - ADAM_B2 ** ADAM_STEP
NEG = -1e30


def _params(sem=None):
    return pltpu.CompilerParams(dimension_semantics=sem, vmem_limit_bytes=VMEM_LIMIT)


def _sigmoid(v):
    return 1.0 / (1.0 + jnp.exp(-v))


def _gelu(v):
    k = 0.7978845608028654
    return 0.5 * v * (1.0 + jnp.tanh(k * (v + 0.044715 * v * v * v)))


def _gelu_grad(v):
    k = 0.7978845608028654
    t = jnp.tanh(k * (v + 0.044715 * v * v * v))
    return 0.5 * (1.0 + t) + 0.5 * v * (1.0 - t * t) * k * (1.0 + 3.0 * 0.044715 * v * v)


NN = (((1,), (0,)), ((), ()))
NT = (((1,), (1,)), ((), ()))
TN = (((0,), (0,)), ((), ()))


def _mm(name, a, a_spec, b, b_spec, dn, grid, out_shapes, out_specs, acc_block, epilogue=None, extras=(), extra_specs=()):
    nk, ne, no = grid[2], len(extras), len(out_shapes)

    def body(*refs):
        a_ref, b_ref = refs[0], refs[1]
        ex, outs = refs[2:2 + ne], refs[2 + ne:2 + ne + no]
        part = lax.dot_general(a_ref[...], b_ref[...], dn, preferred_element_type=F32)

        def finish(acc):
            vals = epilogue(acc, *[e[...] for e in ex]) if epilogue is not None else (acc,)
            for o, v in zip(outs, vals):
                o[...] = v.astype(o.dtype)

        if nk == 1:
            finish(part)
        else:
            acc_ref, k = refs[-1], pl.program_id(2)

            @pl.when(k == 0)
            def _():
                acc_ref[...] = part

            @pl.when(k > 0)
            def _():
                acc_ref[...] += part

            @pl.when(k == nk - 1)
            def _():
                finish(acc_ref[...])

    return pl.pallas_call(
        body, out_shape=out_shapes, grid=grid, in_specs=[a_spec, b_spec, *extra_specs], out_specs=out_specs,
        scratch_shapes=[pltpu.VMEM(acc_block, F32)] if nk > 1 else [],
        compiler_params=_params(("parallel", "parallel", "arbitrary")), name=name)(a, b, *extras)


def _blk(n, pref):
    return pref if n % pref == 0 else n


def _kblk(k):
    return k if k <= 2048 else next(b for b in (2048, 1024, 512) if k % b == 0)


def _mm_fwd(name, a, w, col0, ncols, out_dtypes, epilogue=None, extras=(), extra_specs_fn=None, seg_out=None, bm=1024, bn=1024):
    m, k = a.shape
    bm, bn = _blk(m, bm), _blk(ncols, bn)
    bk = _kblk(k)
    nk = k // bk
    cb0 = col0 // bn
    grid = (m // bm, ncols // bn, nk)
    a_spec = pl.BlockSpec((bm, bk), lambda i, j, kk: (i, kk))
    b_spec = pl.BlockSpec((bk, bn), lambda i, j, kk: (kk, cb0 + j))
    if seg_out is None:
        shapes = [jax.ShapeDtypeStruct((m, ncols), dt) for dt in out_dtypes]
        specs = [pl.BlockSpec((bm, bn), lambda i, j, kk: (i, j)) for _ in out_dtypes]
    else:
        per = seg_out // bn
        shapes = [jax.ShapeDtypeStruct((ncols // seg_out, m, seg_out), dt) for dt in out_dtypes]
        specs = [pl.BlockSpec((None, bm, bn), lambda i, j, kk: (j // per, i, j % per)) for _ in out_dtypes]
    ex_specs = extra_specs_fn(bm, bn) if extra_specs_fn else ()
    return _mm(name, a, a_spec, w, b_spec, NN, grid, shapes, specs, (bm, bn), epilogue, extras, ex_specs)


def _mm_nt(name, a, w, out_dtypes, epilogue=None, extras=(), extra_specs_fn=None, bm=1024, bn=1024):
    m, k = a.shape
    n = w.shape[0]
    bm, bn = _blk(m, bm), _blk(n, bn)
    bk = _kblk(k)
    grid = (m // bm, n // bn, k // bk)
    a_spec = pl.BlockSpec((bm, bk), lambda i, j, kk: (i, kk))
    b_spec = pl.BlockSpec((bn, bk), lambda i, j, kk: (j, kk))
    shapes = [jax.ShapeDtypeStruct((m, n), dt) for dt in out_dtypes]
    specs = [pl.BlockSpec((bm, bn), lambda i, j, kk: (i, j)) for _ in out_dtypes]
    ex_specs = extra_specs_fn(bm, bn) if extra_specs_fn else ()
    return _mm(name, a, a_spec, w, b_spec, NT, grid, shapes, specs, (bm, bn), epilogue, extras, ex_specs)


def _mm_tn(name, a, b, bm=1024, bn=1024):
    t, m = a.shape
    n = b.shape[1]
    bm, bn = _blk(m, bm), _blk(n, bn)
    grid = (m // bm, n // bn, 1)
    a_spec = pl.BlockSpec((t, bm), lambda i, j, kk: (0, i))
    b_spec = pl.BlockSpec((t, bn), lambda i, j, kk: (0, j))
    return _mm(name, a, a_spec, b, b_spec, TN, grid, [jax.ShapeDtypeStruct((m, n), BF16)],
               [pl.BlockSpec((bm, bn), lambda i, j, kk: (i, j))], (bm, bn))[0]


ROWS = 256


def _row_spec(d):
    return pl.BlockSpec((ROWS, d), lambda i: (i, 0))


def _vec_spec(d, rows=1):
    return pl.BlockSpec((rows, d), lambda i: (0, 0))


def _rms_fwd(name, x, g):
    s, d = x.shape

    def body(x_ref, g_ref, o_ref):
        xv = x_ref[...]
        r = lax.rsqrt(jnp.mean(xv * xv, axis=-1, keepdims=True) + NORM_EPS)
        o_ref[...] = (xv * r * g_ref[...]).astype(BF16)

    return pl.pallas_call(body, out_shape=jax.ShapeDtypeStruct((s, d), BF16), grid=(s // ROWS,),
                          in_specs=[_row_spec(d), _vec_spec(d)], out_specs=_row_spec(d),
                          compiler_params=_params(("parallel",)), name=name)(x, g)


def _rms_bwd_math(xv, g, dy):
    r = lax.rsqrt(jnp.mean(xv * xv, axis=-1, keepdims=True) + NORM_EPS)
    n = xv * r
    z = dy * g
    dx = r * (z - n * jnp.mean(z * n, axis=-1, keepdims=True))
    return dx, jnp.sum(dy * n, axis=0, keepdims=True)


def _rms_bwd(name, x, g, dy, resid):
    s, d = x.shape

    def body(x_ref, g_ref, dy_ref, r_ref, dx_ref, dxb_ref, dg_ref):
        dx, dg = _rms_bwd_math(x_ref[...], g_ref[...], dy_ref[...])
        dx = dx + r_ref[...]
        dx_ref[...] = dx
        dxb_ref[...] = dx.astype(BF16)

        @pl.when(pl.program_id(0) == 0)
        def _():
            dg_ref[...] = jnp.zeros_like(dg_ref)

        dg_ref[...] += dg

    return pl.pallas_call(
        body, out_shape=[jax.ShapeDtypeStruct((s, d), F32), jax.ShapeDtypeStruct((s, d), BF16), jax.ShapeDtypeStruct((1, d), F32)],
        grid=(s // ROWS,), in_specs=[_row_spec(d), _vec_spec(d), _row_spec(d), _row_spec(d)],
        out_specs=[_row_spec(d), _row_spec(d), _vec_spec(d)], compiler_params=_params(("arbitrary",)), name=name)(x, g, dy, resid)


def _final_loss(h2, tgt, g):
    s, d = h2.shape

    def body(x_ref, t_ref, g_ref, dx_ref, dxb_ref, dg_ref, ls_ref):
        xv, gv = x_ref[...], g_ref[...]
        r = lax.rsqrt(jnp.mean(xv * xv, axis=-1, keepdims=True) + NORM_EPS)
        diff = xv * r * gv - t_ref[...]
        dx, dg = _rms_bwd_math(xv, gv, diff * (1.0 / d))
        dx_ref[...] = dx
        dxb_ref[...] = dx.astype(BF16)

        @pl.when(pl.program_id(0) == 0)
        def _():
            dg_ref[...] = jnp.zeros_like(dg_ref)
            ls_ref[...] = jnp.zeros_like(ls_ref)

        dg_ref[...] += dg
        ls_ref[...] += jnp.sum(diff * diff, axis=0, keepdims=True)

    return pl.pallas_call(
        body, out_shape=[jax.ShapeDtypeStruct((s, d), F32), jax.ShapeDtypeStruct((s, d), BF16),
                         jax.ShapeDtypeStruct((1, d), F32), jax.ShapeDtypeStruct((1, d), F32)],
        grid=(s // ROWS,), in_specs=[_row_spec(d), _row_spec(d), _vec_spec(d)],
        out_specs=[_row_spec(d), _row_spec(d), _vec_spec(d), _vec_spec(d)],
        compiler_params=_params(("arbitrary",)), name="final_norm_loss")(h2, tgt, g)


def _attn_units(s):
    units = []
    for gi, d in enumerate(DILATIONS):
        for r in range(d):
            for n in range(s // d // HEAD):
                units.append((gi, d, r, n))
    return units


def _stream_rows(d, r, first_blk, n_blk):
    start, size = r + first_blk * HEAD * d, n_blk * HEAD
    return pl.ds(start, size) if d == 1 else pl.ds(start, size, stride=d)


def _attn_scores(q_ref, k_ref, slope, d, r, n):
    nkb = 1 if n == 0 else 2
    qrows, krows = _stream_rows(d, r, n, 1), _stream_rows(d, r, n - nkb + 1, nkb)
    qb = q_ref[qrows, :].astype(BF16)
    kb = k_ref[krows, :].astype(BF16)
    sc = lax.dot_general(qb, kb, NT, preferred_element_type=F32) * (HEAD ** -0.5)
    qi = lax.broadcasted_iota(jnp.int32, (HEAD, nkb * HEAD), 0)
    kj = lax.broadcasted_iota(jnp.int32, (HEAD, nkb * HEAD), 1)
    dist = (nkb - 1) * HEAD + qi - kj
    valid = (dist >= 0) & (dist <= SPAN)
    sc = sc - (slope * d) * dist.astype(F32)
    return jnp.where(valid, sc, NEG), valid, qb, kb, qrows, krows


def _attn_fwd(qkv, slopes):
    _, s, dm = qkv.shape
    units = _attn_units(s)

    def body(sl_ref, q_ref, k_ref, v_ref, att_ref, lse_ref, *scr):
        o_scr, l_scr = scr[:3], scr[3:]
        slope = sl_ref[pl.program_id(0)]
        for gi, d, r, n in units:
            sc, _, _, _, qrows, krows = _attn_scores(q_ref, k_ref, slope, d, r, n)
            m = jnp.max(sc, axis=-1, keepdims=True)
            p = jnp.exp(sc - m)
            l = jnp.sum(p, axis=-1, keepdims=True)
            vb = v_ref[krows, :].astype(BF16)
            o = lax.dot_general(p.astype(BF16), vb, NN, preferred_element_type=F32) / l
            o_scr[gi][qrows, :] = o
            l_scr[gi][qrows, :] = jnp.broadcast_to(m + jnp.log(l), (HEAD, HEAD))
        l0, l1, l2 = l_scr[0][...], l_scr[1][...], l_scr[2][...]
        m = jnp.maximum(jnp.maximum(l0, l1), l2)
        w0, w1, w2 = jnp.exp(l0 - m), jnp.exp(l1 - m), jnp.exp(l2 - m)
        tot = w0 + w1 + w2
        att_ref[...] = ((w0 * o_scr[0][...] + w1 * o_scr[1][...] + w2 * o_scr[2][...]) / tot).astype(BF16)
        lse_ref[...] = m + jnp.log(tot)

    def seg(i):
        return pl.BlockSpec((None, s, HEAD), lambda h: (i, 0, h))

    col = pl.BlockSpec((s, HEAD), lambda h: (0, h))
    return pl.pallas_call(
        body, out_shape=[jax.ShapeDtypeStruct((s, dm), BF16), jax.ShapeDtypeStruct((s, dm), F32)], grid=(dm // HEAD,),
        in_specs=[pl.BlockSpec(memory_space=pltpu.SMEM), seg(0), seg(1), seg(2)], out_specs=[col, col],
        scratch_shapes=[pltpu.VMEM((s, HEAD), F32)] * (2 * len(DILATIONS)),
        compiler_params=_params(("parallel",)), name="attn_fwd")(slopes, qkv, qkv, qkv)


def _attn_bwd(qkv, datt, att, lse, slopes):
    _, s, dm = qkv.shape
    units = _attn_units(s)

    def body(sl_ref, q_ref, k_ref, v_ref, do_ref, att_ref, lse_ref, dq_ref, dk_ref, dv_ref, dq_scr, dk_scr, dv_scr, dl_scr):
        slope = sl_ref[pl.program_id(0)]
        delta = jnp.sum(do_ref[...] * att_ref[...].astype(F32), axis=-1, keepdims=True)
        dl_scr[...] = jnp.broadcast_to(delta, (s, HEAD))
        dq_scr[...] = jnp.zeros_like(dq_scr)
        dk_scr[...] = jnp.zeros_like(dk_scr)
        dv_scr[...] = jnp.zeros_like(dv_scr)
        for gi, d, r, n in units:
            sc, valid, qb, kb, qrows, krows = _attn_scores(q_ref, k_ref, slope, d, r, n)
            p = jnp.where(valid, jnp.exp(sc - lse_ref[qrows, :][:, 0:1]), 0.0)
            vb = v_ref[krows, :].astype(BF16)
            dob = do_ref[qrows, :].astype(BF16)
            dp = lax.dot_general(dob, vb, NT, preferred_element_type=F32)
            ds = (p * (dp - dl_scr[qrows, :][:, 0:1]) * (HEAD ** -0.5)).astype(BF16)
            dq_scr[qrows, :] += lax.dot_general(ds, kb, NN, preferred_element_type=F32)
            dk_scr[krows, :] += lax.dot_general(ds, qb, TN, preferred_element_type=F32)
            dv_scr[krows, :] += lax.dot_general(p.astype(BF16), dob, TN, preferred_element_type=F32)
        dq_ref[...] = dq_scr[...].astype(BF16)
        dk_ref[...] = dk_scr[...].astype(BF16)
        dv_ref[...] = dv_scr[...].astype(BF16)

    def seg(i):
        return pl.BlockSpec((None, s, HEAD), lambda h: (i, 0, h))

    col = pl.BlockSpec((s, HEAD), lambda h: (0, h))
    return pl.pallas_call(
        body, out_shape=[jax.ShapeDtypeStruct((s, dm), BF16)] * 3, grid=(dm // HEAD,),
        in_specs=[pl.BlockSpec(memory_space=pltpu.SMEM), seg(0), seg(1), seg(2), col, col, col], out_specs=[col, col, col],
        scratch_shapes=[pltpu.VMEM((s, HEAD), F32)] * 4,
        compiler_params=_params(("parallel",)), name="attn_bwd")(slopes, qkv, qkv, qkv, datt, att, lse)


VEC_CB, VEC_BA, VEC_BX, VEC_LAM = 0, 1, 2, 3


def _to_3d(ref3, val):
    lw = val.shape[1] // SUBLANES
    for j in range(SUBLANES):
        ref3[:, j, :] = val[:, j * lw:(j + 1) * lw]


def _from_3d(ref3):
    return jnp.concatenate([ref3[:, j, :] for j in range(SUBLANES)], axis=1)


def _softplus(z):
    return jnp.maximum(z, 0.0) + jnp.log1p(jnp.exp(-jnp.abs(z)))


def _gate_math(xc, wa_ref, wx_ref, vec):
    xcb = xc.astype(BF16)
    nh = xc.shape[1] // HEAD
    pre_a = jnp.concatenate([jnp.dot(xcb[:, h * HEAD:(h + 1) * HEAD], wa_ref[h], preferred_element_type=F32) for h in range(nh)], axis=1)
    pre_x = jnp.concatenate([jnp.dot(xcb[:, h * HEAD:(h + 1) * HEAD], wx_ref[h], preferred_element_type=F32) for h in range(nh)], axis=1)
    ra = _sigmoid(pre_a + vec[VEC_BA:VEC_BA + 1])
    ig = _sigmoid(pre_x + vec[VEC_BX:VEC_BX + 1])
    sp = _softplus(-vec[VEC_LAM:VEC_LAM + 1])
    log_a = -LRU_C * ra * sp
    a = jnp.exp(log_a)
    z = 2.0 * log_a
    one_minus_a2 = jnp.where(z > -0.01, -z * (1.0 + z * (0.5 + z * (1.0 / 6.0))), 1.0 - jnp.exp(z))
    mult = jnp.sqrt(one_minus_a2)
    return dict(xcb=xcb, ra=ra, ig=ig, sp=sp, a=a, mult=mult)


def _conv_pad_prev(pad_ref, cur, halo, first):
    pad_ref[0:SUBLANES, :] = jnp.where(first, 0.0, halo)
    pad_ref[SUBLANES:SUBLANES + cur.shape[0], :] = cur


def _gates_fwd(rest, cw8, vec8, wa, wx):
    _, s, d = rest.shape
    lw = d // SUBLANES
    hb = ROWS // SUBLANES

    def body(x_ref, halo_ref, cw_ref, vec_ref, wa_ref, wx_ref, a_ref, u_ref, xc_ref, pad):
        _conv_pad_prev(pad, x_ref[...], halo_ref[...], pl.program_id(0) == 0)
        vec = vec_ref[...]
        xc = vec[VEC_CB:VEC_CB + 1]
        for k in range(CONV_TAPS):
            xc = xc + cw_ref[k:k + 1, :] * pad[pl.ds(SUBLANES - (CONV_TAPS - 1) + k, ROWS), :]
        gm = _gate_math(xc, wa_ref, wx_ref, vec)
        xc_ref[...] = xc
        _to_3d(a_ref, gm["a"])
        _to_3d(u_ref, gm["mult"] * (gm["ig"] * xc))

    spec3 = pl.BlockSpec((ROWS, SUBLANES, lw), lambda i: (i, 0, 0))
    wspec = pl.BlockSpec(wa.shape, lambda i: (0, 0, 0))
    return pl.pallas_call(
        body, out_shape=[jax.ShapeDtypeStruct((s, SUBLANES, lw), F32)] * 2 + [jax.ShapeDtypeStruct((s, d), F32)], grid=(s // ROWS,),
        in_specs=[pl.BlockSpec((None, ROWS, d), lambda i: (0, i, 0)),
                  pl.BlockSpec((None, SUBLANES, d), lambda i: (0, jnp.maximum(i * hb - 1, 0), 0)),
                  _vec_spec(d, SUBLANES), _vec_spec(d, SUBLANES), wspec, wspec],
        out_specs=[spec3, spec3, _row_spec(d)], scratch_shapes=[pltpu.VMEM((ROWS + SUBLANES, d), F32)],
        compiler_params=_params(("parallel",)), name="lru_gates_fwd")(rest, rest, cw8, vec8, wa, wx)


def _scan_fwd(a3, u3):
    s, _, lw = a3.shape

    def body(a_ref, u_ref, h_ref, hp_ref, carry):
        @pl.when(pl.program_id(0) == 0)
        def _():
            carry[...] = jnp.zeros_like(carry)

        def step(t, h):
            hp_ref[t] = h
            hn = a_ref[t] * h + u_ref[t]
            h_ref[t] = hn
            return hn

        carry[...] = lax.fori_loop(0, ROWS, step, carry[...], unroll=8)

    spec3 = pl.BlockSpec((ROWS, SUBLANES, lw), lambda i: (i, 0, 0))
    return pl.pallas_call(body, out_shape=[jax.ShapeDtypeStruct(a3.shape, F32)] * 2, grid=(s // ROWS,), in_specs=[spec3, spec3],
                          out_specs=[spec3, spec3], scratch_shapes=[pltpu.VMEM((SUBLANES, lw), F32)],
                          compiler_params=_params(("arbitrary",)), name="lru_scan_fwd")(a3, u3)


def _lru_out(h3, rest):
    s, _, lw = h3.shape
    d = lw * SUBLANES

    def body(h_ref, g_ref, y_ref, h2_ref):
        h = _from_3d(h_ref)
        h2_ref[...] = h
        y_ref[...] = (h * _gelu(g_ref[...])).astype(BF16)

    return pl.pallas_call(
        body, out_shape=[jax.ShapeDtypeStruct((s, d), BF16), jax.ShapeDtypeStruct((s, d), F32)], grid=(s // ROWS,),
        in_specs=[pl.BlockSpec((ROWS, SUBLANES, lw), lambda i: (i, 0, 0)), pl.BlockSpec((None, ROWS, d), lambda i: (1, i, 0))],
        out_specs=[_row_spec(d), _row_spec(d)], compiler_params=_params(("parallel",)), name="lru_out")(h3, rest)


def _scan_bwd(a3, hp3, dh):
    s, _, lw = a3.shape
    d = lw * SUBLANES
    nb = s // ROWS

    def body(a_ref, hp_ref, dh_ref, g_ref, da_ref, dh3, carry):
        @pl.when(pl.program_id(0) == 0)
        def _():
            carry[...] = jnp.zeros_like(carry)

        _to_3d(dh3, dh_ref[...])

        def step(j, c):
            t = ROWS - 1 - j
            g = dh3[t] + c
            g_ref[t] = g
            da_ref[t] = g * hp_ref[t]
            return a_ref[t] * g

        carry[...] = lax.fori_loop(0, ROWS, step, carry[...], unroll=8)

    spec3 = pl.BlockSpec((ROWS, SUBLANES, lw), lambda i: (nb - 1 - i, 0, 0))
    return pl.pallas_call(
        body, out_shape=[jax.ShapeDtypeStruct(a3.shape, F32)] * 2, grid=(nb,),
        in_specs=[spec3, spec3, pl.BlockSpec((ROWS, d), lambda i: (nb - 1 - i, 0))], out_specs=[spec3, spec3],
        scratch_shapes=[pltpu.VMEM((ROWS, SUBLANES, lw), F32), pltpu.VMEM((SUBLANES, lw), F32)],
        compiler_params=_params(("arbitrary",)), name="lru_scan_bwd")(a3, hp3, dh)


def _gates_bwd(g3, da3, xc, wa, wx, vec8):
    s, d = xc.shape
    lw = d // SUBLANES
    nh = d // HEAD

    def body(g_ref, da_ref, xc_ref, wa_ref, wx_ref, vec_ref, dxc_ref, dwa_ref, dwx_ref, dvec_ref):
        @pl.when(pl.program_id(0) == 0)
        def _():
            dwa_ref[...] = jnp.zeros_like(dwa_ref)
            dwx_ref[...] = jnp.zeros_like(dwx_ref)
            dvec_ref[...] = jnp.zeros_like(dvec_ref)

        xc_v, vec = xc_ref[...], vec_ref[...]
        du, da = _from_3d(g_ref), _from_3d(da_ref)
        gm = _gate_math(xc_v, wa_ref, wx_ref, vec)
        ra, ig, sp, a, mult = gm["ra"], gm["ig"], gm["sp"], gm["a"], gm["mult"]
        dmult = du * ig * xc_v
        dlog_a = da * a - dmult * (a * a) / mult
        dpre_a = dlog_a * (-LRU_C * sp) * ra * (1.0 - ra)
        dpre_x = du * mult * xc_v * ig * (1.0 - ig)
        dlam = jnp.sum(dlog_a * (-LRU_C * ra), axis=0, keepdims=True) * (-_sigmoid(-vec[VEC_LAM:VEC_LAM + 1]))
        dvec_ref[VEC_BA:VEC_BA + 1, :] += jnp.sum(dpre_a, axis=0, keepdims=True)
        dvec_ref[VEC_BX:VEC_BX + 1, :] += jnp.sum(dpre_x, axis=0, keepdims=True)
        dvec_ref[VEC_LAM:VEC_LAM + 1, :] += dlam
        dab, dxb, xcb = dpre_a.astype(BF16), dpre_x.astype(BF16), gm["xcb"]
        back = []
        for h in range(nh):
            cols = slice(h * HEAD, (h + 1) * HEAD)
            dwa_ref[h] += lax.dot_general(xcb[:, cols], dab[:, cols], TN, preferred_element_type=F32)
            dwx_ref[h] += lax.dot_general(xcb[:, cols], dxb[:, cols], TN, preferred_element_type=F32)
            back.append(lax.dot_general(dab[:, cols], wa_ref[h], NT, preferred_element_type=F32)
                        + lax.dot_general(dxb[:, cols], wx_ref[h], NT, preferred_element_type=F32))
        dxc_ref[...] = du * mult * ig + jnp.concatenate(back, axis=1)

    spec3 = pl.BlockSpec((ROWS, SUBLANES, lw), lambda i: (i, 0, 0))
    wspec = pl.BlockSpec(wa.shape, lambda i: (0, 0, 0))
    return pl.pallas_call(
        body, out_shape=[jax.ShapeDtypeStruct((s, d), F32), jax.ShapeDtypeStruct(wa.shape, F32), jax.ShapeDtypeStruct(wa.shape, F32),
                         jax.ShapeDtypeStruct((SUBLANES, d), F32)],
        grid=(s // ROWS,), in_specs=[spec3, spec3, _row_spec(d), wspec, wspec, _vec_spec(d, SUBLANES)],
        out_specs=[_row_spec(d), wspec, wspec, _vec_spec(d, SUBLANES)],
        compiler_params=_params(("arbitrary",)), name="lru_gates_bwd")(g3, da3, xc, wa, wx, vec8)


def _conv_bwd(dxc, rest, cw8):
    s, d = dxc.shape
    hb = ROWS // SUBLANES
    last = s // SUBLANES - 1

    def body(dc_ref, dnext_ref, x_ref, xprev_ref, cw_ref, dx_ref, dcw_ref, padd, padx):
        i = pl.program_id(0)

        @pl.when(i == 0)
        def _():
            dcw_ref[...] = jnp.zeros_like(dcw_ref)

        dc = dc_ref[...]
        padd[0:ROWS, :] = dc
        padd[ROWS:ROWS + SUBLANES, :] = jnp.where(i == pl.num_programs(0) - 1, 0.0, dnext_ref[...])
        _conv_pad_prev(padx, x_ref[...], xprev_ref[...], i == 0)
        dx = jnp.zeros_like(dc)
        for k in range(CONV_TAPS):
            dx = dx + cw_ref[k:k + 1, :] * padd[pl.ds(CONV_TAPS - 1 - k, ROWS), :]
            dcw_ref[k:k + 1, :] += jnp.sum(dc * padx[pl.ds(SUBLANES - (CONV_TAPS - 1) + k, ROWS), :], axis=0, keepdims=True)
        dcw_ref[CONV_TAPS:CONV_TAPS + 1, :] += jnp.sum(dc, axis=0, keepdims=True)
        dx_ref[...] = dx.astype(BF16)

    return pl.pallas_call(
        body, out_shape=[jax.ShapeDtypeStruct((s, d), BF16), jax.ShapeDtypeStruct((SUBLANES, d), F32)], grid=(s // ROWS,),
        in_specs=[_row_spec(d), pl.BlockSpec((SUBLANES, d), lambda i: (jnp.minimum((i + 1) * hb, last), 0)),
                  pl.BlockSpec((None, ROWS, d), lambda i: (0, i, 0)),
                  pl.BlockSpec((None, SUBLANES, d), lambda i: (0, jnp.maximum(i * hb - 1, 0), 0)), _vec_spec(d, SUBLANES)],
        out_specs=[_row_spec(d), _vec_spec(d, SUBLANES)],
        scratch_shapes=[pltpu.VMEM((ROWS + SUBLANES, d), F32), pltpu.VMEM((ROWS + SUBLANES, d), F32)],
        compiler_params=_params(("arbitrary",)), name="lru_conv_bwd")(dxc, dxc, rest, rest, cw8)


def _coords():
    return lax.axis_index("x"), lax.axis_index("y"), lax.axis_index("c")


def _other_chips(x, y):
    return [(1 - x, y), (x, 1 - y), (1 - x, 1 - y)]


def _slab(ref, kind, shard_shape, idx):
    r, c = shard_shape
    if kind == "col":
        return ref.at[:, pl.ds(pl.multiple_of(idx * c, LANES), c)]
    if kind == "row":
        return ref.at[pl.ds(pl.multiple_of(idx * r, SUBLANES), r), :]
    return ref.at[idx]


def _full_shape(shard_shape, kind):
    r, c = shard_shape
    return {"col": (r, c * N_DEV), "row": (r * N_DEV, c), "slot": (N_DEV, r, c)}[kind]


def _all_gather(name, shards, kinds, sequencer_id=None):
    n = len(shards)
    shapes = [s.shape for s in shards]

    def body(*refs):
        ins, outs = refs[:n], refs[n:2 * n]
        send_sems, recv_sems, local_sems = refs[2 * n:]
        x, y, c = _coords()
        me, sib = (x, y, c), (x, y, 1 - c)
        chips = _other_chips(x, y)
        if sequencer_id is not None:
            barrier = pltpu.get_barrier_semaphore()
            for peer in [sib] + [(*chip, c) for chip in chips]:
                pl.semaphore_signal(barrier, inc=1, device_id=peer, device_id_type=MESH)
            pl.semaphore_wait(barrier, 1 + len(chips))

        def part(i, dev):
            return _slab(outs[i], kinds[i], shapes[i], 4 * dev[0] + 2 * dev[1] + dev[2])

        def copy(i, k, block, to, src=None):
            return pltpu.make_async_remote_copy(
                src_ref=part(i, block) if src is None else src, dst_ref=part(i, block),
                send_sem=send_sems.at[7 * i + k], recv_sem=recv_sems.at[7 * i + k], device_id=to, device_id_type=MESH)

        started = []
        for i in range(n):
            for k, to in enumerate([sib] + [(*chip, c) for chip in chips]):
                cp = copy(i, k, me, to, src=ins[i])
                cp.start()
                started.append(cp)
        mine = [pltpu.make_async_copy(ins[i], part(i, me), local_sems.at[i]) for i in range(n)]
        for cp in mine:
            cp.start()
        for i in range(n):
            for j, chip in enumerate(chips):
                copy(i, 1 + j, (*chip, c), me).wait_recv()
                cp = copy(i, 4 + j, (*chip, c), sib)
                cp.start()
                started.append(cp)
        for i in range(n):
            copy(i, 0, sib, me).wait_recv()
            for j, chip in enumerate(chips):
                copy(i, 4 + j, (*chip, 1 - c), me).wait_recv()
        for cp in started:
            cp.wait_send()
        for cp in mine:
            cp.wait()

    out_shape = [jax.ShapeDtypeStruct(_full_shape(s.shape, k), s.dtype) for s, k in zip(shards, kinds)]
    sems = [pltpu.SemaphoreType.DMA((7 * n,)), pltpu.SemaphoreType.DMA((7 * n,)), pltpu.SemaphoreType.DMA((n,))]
    if sequencer_id is None:
        return pl.pallas_call(body, out_shape=out_shape, in_specs=[HBM] * n, out_specs=[HBM] * n, scratch_shapes=sems, name=name)(*shards)
    return pl.kernel(body, out_type=out_shape, mesh=plsc.ScalarSubcoreMesh(axis_name="seq", num_cores=1), name=name,
                     scratch_types=sems, compiler_params=pltpu.CompilerParams(collective_id=sequencer_id))(*shards)


def _exchange_siblings(partials, kinds, shard_shapes):
    n = len(partials)

    def body(*refs):
        ins, outs = refs[:n], refs[n:2 * n]
        send_sems, recv_sems = refs[2 * n:]
        x, y, c = _coords()
        cps = []
        for i in range(n):
            for q in range(N_CHIP):
                cps.append(pltpu.make_async_remote_copy(
                    src_ref=_slab(ins[i], kinds[i], shard_shapes[i], 2 * q + (1 - c)), dst_ref=outs[i].at[q],
                    send_sem=send_sems.at[N_CHIP * i + q], recv_sem=recv_sems.at[N_CHIP * i + q],
                    device_id=(x, y, 1 - c), device_id_type=MESH))
        for cp in cps:
            cp.start()
        for cp in cps:
            cp.wait()

    return pl.pallas_call(
        body, out_shape=[jax.ShapeDtypeStruct((N_CHIP, *s), BF16) for s in shard_shapes], in_specs=[HBM] * n, out_specs=[HBM] * n,
        scratch_shapes=[pltpu.SemaphoreType.DMA((N_CHIP * n,)), pltpu.SemaphoreType.DMA((N_CHIP * n,))],
        name="reduce_scatter_siblings")(*partials)


def _exchange_chips(chip_sums):
    n = len(chip_sums)

    def body(*refs):
        ins, outs = refs[:n], refs[n:2 * n]
        send_sems, recv_sems = refs[2 * n:]
        x, y, c = _coords()
        cps = []
        for i in range(n):
            for k, (cx, cy) in enumerate(_other_chips(x, y)):
                cps.append(pltpu.make_async_remote_copy(
                    src_ref=ins[i].at[2 * cx + cy], dst_ref=outs[i].at[k], send_sem=send_sems.at[3 * i + k],
                    recv_sem=recv_sems.at[3 * i + k], device_id=(cx, cy, c), device_id_type=MESH))
        for cp in cps:
            cp.start()
        for cp in cps:
            cp.wait()

    return pl.pallas_call(
        body, out_shape=[jax.ShapeDtypeStruct((3, *t.shape[1:]), BF16) for t in chip_sums], in_specs=[HBM] * n, out_specs=[HBM] * n,
        scratch_shapes=[pltpu.SemaphoreType.DMA((3 * n,)), pltpu.SemaphoreType.DMA((3 * n,))],
        name="reduce_scatter_chips")(*chip_sums)


def _all_reduce_small(packed):
    rows = packed.shape[0] // N_DEV

    def body(p_ref, out_ref, rb, tot, send_sems, recv_sems):
        x, y, c = _coords()
        me = 4 * x + 2 * y + c

        def peer(k):
            return (x ^ (k >> 2), y ^ ((k >> 1) & 1), c ^ (k & 1))

        def rows_of(idx):
            return pl.ds(pl.multiple_of(idx * rows, SUBLANES), rows)

        def piece(ref, idx):
            return ref.at[rows_of(idx), :]

        scatter = [pltpu.make_async_remote_copy(src_ref=piece(p_ref, me ^ k), dst_ref=rb.at[k], send_sem=send_sems.at[k],
                                                recv_sem=recv_sems.at[k], device_id=peer(k), device_id_type=MESH) for k in range(1, N_DEV)]
        for cp in scatter:
            cp.start()
        acc = p_ref[rows_of(me), :]
        for cp in scatter:
            cp.wait_recv()
        for k in range(1, N_DEV):
            acc = acc + rb[k]
        tot[...] = acc
        out_ref[rows_of(me), :] = acc
        gather = [pltpu.make_async_remote_copy(src_ref=tot, dst_ref=piece(out_ref, me), send_sem=send_sems.at[N_DEV + k],
                                               recv_sem=recv_sems.at[N_DEV + k], device_id=peer(k), device_id_type=MESH)
                  for k in range(1, N_DEV)]
        for cp in gather:
            cp.start()
        for k in range(1, N_DEV):
            pltpu.make_async_remote_copy(src_ref=tot, dst_ref=piece(out_ref, me ^ k), send_sem=send_sems.at[N_DEV + k],
                                         recv_sem=recv_sems.at[N_DEV + k], device_id=peer(k), device_id_type=MESH).wait_recv()
        for cp in scatter + gather:
            cp.wait_send()

    vm = pl.BlockSpec(memory_space=pltpu.VMEM)
    return pl.pallas_call(
        body, out_shape=jax.ShapeDtypeStruct(packed.shape, F32), in_specs=[vm], out_specs=vm,
        scratch_shapes=[pltpu.VMEM((N_DEV, rows, LANES), F32), pltpu.VMEM((rows, LANES), F32),
                        pltpu.SemaphoreType.DMA((2 * N_DEV,)), pltpu.SemaphoreType.DMA((2 * N_DEV,))],
        compiler_params=pltpu.CompilerParams(vmem_limit_bytes=VMEM_LIMIT), name="all_reduce_small")(packed)


def _adamw_math(g, w, m, v):
    m = ADAM_B1 * m + (1.0 - ADAM_B1) * g
    v = ADAM_B2 * v + (1.0 - ADAM_B2) * (g * g)
    delta = -ADAM_LR * ((m / ADAM_C1) / (jnp.sqrt(v / ADAM_C2) + ADAM_EPS) + ADAM_WD * w)
    return delta, m, v


def _slab_spec(kind, shard_shape, tr, slab_of):
    r, c = shard_shape
    if kind == "col":
        return pl.BlockSpec((tr, c), lambda q, i, sc: (i, slab_of(q, sc)))
    return pl.BlockSpec((tr, c), lambda q, i, sc: (slab_of(q, sc) * (r // tr) + i, 0))


def _chip_sum(name, partial, recv, kind, shard_shape, core):
    r, c = shard_shape
    tr = _blk(r, 256)

    def body(core_ref, p_ref, r_ref, o_ref):
        o_ref[...] = (p_ref[...].astype(F32) + r_ref[...].astype(F32)).astype(BF16)

    spec4 = pl.BlockSpec((None, tr, c), lambda q, i, sc: (q, i, 0))
    grid_spec = pltpu.PrefetchScalarGridSpec(
        num_scalar_prefetch=1, grid=(N_CHIP, r // tr),
        in_specs=[_slab_spec(kind, shard_shape, tr, lambda q, sc: 2 * q + sc[0]), spec4], out_specs=spec4)
    return pl.pallas_call(body, out_shape=jax.ShapeDtypeStruct((N_CHIP, r, c), BF16), grid_spec=grid_spec,
                          compiler_params=_params(("parallel", "parallel")), name=name)(core, partial, recv)


def _adamw_shard(name, chip_sums, recv, w, m, v, chip):
    r, c = w.shape
    tr = _blk(r, 128)

    def body(chip_ref, t_ref, r_ref, w_ref, m_ref, v_ref, g_out, d_out, m_out, v_out):
        g = t_ref[...].astype(F32)
        for k in range(3):
            g = g + r_ref[k].astype(F32)
        g_out[...] = g
        d_out[...], m_out[...], v_out[...] = _adamw_math(g, w_ref[...], m_ref[...], v_ref[...])

    blk = pl.BlockSpec((tr, c), lambda i, sc: (i, 0))
    grid_spec = pltpu.PrefetchScalarGridSpec(
        num_scalar_prefetch=1, grid=(r // tr,),
        in_specs=[pl.BlockSpec((None, tr, c), lambda i, sc: (sc[0], i, 0)), pl.BlockSpec((3, tr, c), lambda i, sc: (0, i, 0)), blk, blk, blk],
        out_specs=[blk] * 4)
    return pl.pallas_call(body, out_shape=[jax.ShapeDtypeStruct((r, c), F32)] * 4, grid_spec=grid_spec,
                          compiler_params=_params(("parallel",)), name=name)(chip, chip_sums, recv, w, m, v)


def _adamw_small(name, g, w, m, v):
    def body(g_ref, w_ref, m_ref, v_ref, d_out, m_out, v_out):
        d_out[...], m_out[...], v_out[...] = _adamw_math(g_ref[...], w_ref[...], m_ref[...], v_ref[...])

    vm = pl.BlockSpec(memory_space=pltpu.VMEM)
    return pl.pallas_call(body, out_shape=[jax.ShapeDtypeStruct(g.shape, F32)] * 3, in_specs=[vm] * 4, out_specs=[vm] * 3,
                          compiler_params=pltpu.CompilerParams(vmem_limit_bytes=VMEM_LIMIT), name=name)(g, w, m, v)


def _pack_rows(arrays, total_rows):
    flat = [a.reshape(-1, LANES) for a in arrays]
    used = sum(f.shape[0] for f in flat)
    return jnp.concatenate(flat + [jnp.zeros((total_rows - used, LANES), F32)], axis=0)


def _unpack_rows(packed, like):
    out, at = [], 0
    for a in like:
        n = a.size // LANES
        out.append(packed[at:at + n].reshape(a.shape))
        at += n
    return out


def kernel(x, norm_mix_g, w_in, conv_w, conv_b, lru_wa, lru_ba, lru_wx, lru_bx, lru_lambda, w_proj_attn, w_proj_lru, w_out, norm_mlp_g, w_up, w_down, norm_final_g, loss_target, m_norm_mix_g, m_w_in, m_conv_w, m_conv_b, m_lru_wa, m_lru_ba, m_lru_wx, m_lru_bx, m_lru_lambda, m_w_proj_attn, m_w_proj_lru, m_w_out, m_norm_mlp_g, m_w_up, m_w_down, m_norm_final_g, v_norm_mix_g, v_w_in, v_conv_w, v_conv_b, v_lru_wa, v_lru_ba, v_lru_wx, v_lru_bx, v_lru_lambda, v_w_proj_attn, v_w_proj_lru, v_w_out, v_norm_mlp_g, v_w_up, v_w_down, v_norm_final_g):
    xs, tgt = x[0], loss_target[0]
    s, d = xs.shape
    nh = d // HEAD
    ix, iy, ic = _coords()
    core = jnp.reshape(ic, (1,)).astype(jnp.int32)
    chip = jnp.reshape(2 * ix + iy, (1,)).astype(jnp.int32)
    dev = 4 * ix + 2 * iy + ic

    big = [w_in[0], w_proj_attn[0], w_proj_lru[0], w_out[0], w_up[0], w_down[0]]
    big_m = [m_w_in[0], m_w_proj_attn[0], m_w_proj_lru[0], m_w_out[0], m_w_up[0], m_w_down[0]]
    big_v = [v_w_in[0], v_w_proj_attn[0], v_w_proj_lru[0], v_w_out[0], v_w_up[0], v_w_down[0]]
    kinds = ["col", "row", "row", "row", "col", "row"]
    pad_taps = lambda t: jnp.pad(t, ((0, SUBLANES - CONV_TAPS), (0, 0)))
    shards = [w.astype(BF16) for w in big]
    win, cw_slots = _all_gather("all_gather_w_in", [shards[0], pad_taps(conv_w[0])], ["col", "slot"])
    later = lax.optimization_barrier((shards[1:], win))[0]
    wpa, wpl, wout, wup, wdown = _all_gather("all_gather_rest", later, kinds[1:], sequencer_id=1)
    cw8 = jnp.transpose(cw_slots, (1, 0, 2)).reshape(SUBLANES, d)
    row_id = lax.broadcasted_iota(jnp.int32, (SUBLANES, d), 0)
    vec8 = sum(jnp.where(row_id == k, t, 0.0) for k, t in ((VEC_CB, conv_b), (VEC_BA, lru_ba), (VEC_BX, lru_bx), (VEC_LAM, lru_lambda)))
    wa16, wx16 = lru_wa[0].astype(BF16), lru_wx[0].astype(BF16)
    slopes = 2.0 ** (-8.0 * jnp.arange(1, nh + 1, dtype=F32) / nh)

    def seg_specs(*segs):
        return lambda bm, bn: [pl.BlockSpec((None, bm, bn), (lambda i, j, kk, sg=sg: (sg, i, j))) for sg in segs]

    def plain_specs(k):
        return lambda bm, bn: [pl.BlockSpec((bm, bn), lambda i, j, kk: (i, j)) for _ in range(k)]

    xn = _rms_fwd("norm_mix", xs, norm_mix_g)
    qkv = _mm_fwd("proj_qkv", xn, win, 0, 3 * d, [F32], seg_out=d)[0]
    rest = _mm_fwd("proj_rest", xn, win, 3 * d, 4 * d, [F32], seg_out=d)[0]
    att, lse = _attn_fwd(qkv, slopes)
    a3, u3, xc = _gates_fwd(rest, cw8, vec8, wa16, wx16)
    h3, hp3 = _scan_fwd(a3, u3)
    ylru, h2d = _lru_out(h3, rest)
    pa = _mm_fwd("proj_attn", att, wpa, 0, d, [F32])[0]

    def merge(acc, pa_b, ga, gl):
        return acc, _sigmoid(ga) * pa_b + _sigmoid(gl) * acc

    plr, merged = _mm_fwd("proj_lru_merge", ylru, wpl, 0, d, [F32, BF16], merge, (pa, rest, rest),
                          lambda bm, bn: plain_specs(1)(bm, bn) + seg_specs(2, 3)(bm, bn), bm=512)
    h1 = _mm_fwd("mix_out", merged, wout, 0, d, [F32], lambda acc, r: (acc + r,), (xs,), plain_specs(1))[0]
    hn = _rms_fwd("norm_mlp", h1, norm_mlp_g)

    def relu2(acc):
        return acc, jnp.square(jnp.maximum(acc, 0.0))

    up, hid = _mm_fwd("mlp_up", hn, wup, 0, wup.shape[1], [BF16, BF16], relu2)
    h2 = _mm_fwd("mlp_down", hid, wdown, 0, d, [F32], lambda acc, r: (acc + r,), (h1,), plain_specs(1))[0]
    dh2, dh2b, dg3, loss_lanes = _final_loss(h2, tgt, norm_final_g.reshape(1, d))
    loss = lax.psum(0.5 / d * jnp.sum(loss_lanes), ("x", "y", "c"))

    dup = _mm_nt("mlp_down_dx", dh2b, wdown, [BF16], lambda acc, u: (acc * (2.0 * jnp.maximum(u.astype(F32), 0.0)),), (up,), plain_specs(1))[0]
    g_wdown = _mm_tn("mlp_down_dw", hid, dh2b)
    dhn = _mm_nt("mlp_up_dx", dup, wup, [F32])[0]
    g_wup = _mm_tn("mlp_up_dw", hn, dup)
    dh1, dh1b, dg2 = _rms_bwd("norm_mlp_bwd", h1, norm_mlp_g, dhn, dh2)

    def merge_bwd(acc, pa_b, pl_b, ga, gl):
        sa, sl = _sigmoid(ga), _sigmoid(gl)
        return acc * sa, acc * sl, acc * pa_b * sa * (1.0 - sa), acc * pl_b * sl * (1.0 - sl)

    dpa, dpl, dga, dgl = _mm_nt("mix_out_dx", dh1b, wout, [BF16] * 4, merge_bwd, (pa, plr, rest, rest),
                                lambda bm, bn: plain_specs(2)(bm, bn) + seg_specs(2, 3)(bm, bn), bm=512)
    g_wout = _mm_tn("mix_out_dw", merged, dh1b)
    datt = _mm_nt("proj_attn_dx", dpa, wpa, [F32])[0]
    g_wpa = _mm_tn("proj_attn_dw", att, dpa)

    def lru_out_bwd(acc, h_b, gate):
        return acc * _gelu(gate), acc * h_b * _gelu_grad(gate)

    dh, dxg = _mm_nt("proj_lru_dx", dpl, wpl, [F32, BF16], lru_out_bwd, (h2d, rest),
                     lambda bm, bn: plain_specs(1)(bm, bn) + seg_specs(1)(bm, bn), bm=512)
    g_wpl = _mm_tn("proj_lru_dw", ylru, dpl)
    dq, dk, dv = _attn_bwd(qkv, datt, att, lse, slopes)
    g3, da3 = _scan_bwd(a3, hp3, dh)
    dxc, dwa, dwx, dvec = _gates_bwd(g3, da3, xc, wa16, wx16, vec8)
    dxr, dconv = _conv_bwd(dxc, rest, cw8)
    dproj = jnp.concatenate([dq, dk, dv, dxr, dxg, dga, dgl], axis=1)
    dxn = _mm_nt("proj_in_dx", dproj, win, [F32])[0]
    g_win = _mm_tn("proj_in_dw", xn, dproj)
    grad_x, _, dg1 = _rms_bwd("norm_mix_bwd", xs, norm_mix_g, dxn, dh1)

    partials = [g_win, g_wpa, g_wpl, g_wout, g_wup, g_wdown]
    shard_shapes = [w.shape for w in big]
    from_sibling = _exchange_siblings(partials, kinds, shard_shapes)
    chip_sums = [_chip_sum(f"chip_sum_{i}", partials[i], from_sibling[i], kinds[i], shard_shapes[i], core) for i in range(len(big))]
    from_chips = _exchange_chips(chip_sums)
    big_out = [_adamw_shard(f"adamw_{i}", chip_sums[i], from_chips[i], big[i], big_m[i], big_v[i], chip) for i in range(len(big))]

    small_w = [norm_mix_g, conv_b, lru_wa, lru_ba, lru_wx, lru_bx, lru_lambda, norm_mlp_g, norm_final_g]
    small_m = [m_norm_mix_g, m_conv_b, m_lru_wa, m_lru_ba, m_lru_wx, m_lru_bx, m_lru_lambda, m_norm_mlp_g, m_norm_final_g]
    small_v = [v_norm_mix_g, v_conv_b, v_lru_wa, v_lru_ba, v_lru_wx, v_lru_bx, v_lru_lambda, v_norm_mlp_g, v_norm_final_g]
    small_g = [dg1, dconv[CONV_TAPS:CONV_TAPS + 1], dwa, dvec[VEC_BA:VEC_BA + 1], dwx, dvec[VEC_BX:VEC_BX + 1],
               dvec[VEC_LAM:VEC_LAM + 1], dg2, dg3, dconv[0:CONV_TAPS]]
    n_rows = sum(g.size for g in small_g) // LANES
    per_dev = -(-n_rows // (N_DEV * SUBLANES)) * SUBLANES
    total = _all_reduce_small(_pack_rows(small_g, N_DEV * per_dev))
    sw_rows = -(-(n_rows - CONV_TAPS * d // LANES) // SUBLANES) * SUBLANES
    s_delta, s_m, s_v = _adamw_small("adamw_small", total[:sw_rows], _pack_rows(small_w, sw_rows), _pack_rows(small_m, sw_rows),
                                     _pack_rows(small_v, sw_rows))
    sg = _unpack_rows(total, [w for w in small_w] + [jnp.zeros((1, CONV_TAPS, d), F32)])
    s_grad, g_cw_full = sg[:-1], sg[-1]
    s_delta, s_m, s_v = (_unpack_rows(t, small_w) for t in (s_delta, s_m, s_v))
    cshard = conv_w.shape[2]
    g_cw = lax.dynamic_slice(g_cw_full, (0, 0, dev * cshard), (1, CONV_TAPS, cshard))
    cw_delta, cw_m, cw_v = (t[:CONV_TAPS][None] for t in _adamw_small(
        "adamw_conv_w", pad_taps(g_cw[0]), pad_taps(conv_w[0]), pad_taps(m_conv_w[0]), pad_taps(v_conv_w[0])))

    names = ["norm_mix_g", "w_in", "conv_w", "conv_b", "lru_wa", "lru_ba", "lru_wx", "lru_bx", "lru_lambda", "w_proj_attn", "w_proj_lru",
             "w_out", "norm_mlp_g", "w_up", "w_down", "norm_final_g"]
    small_names = ["norm_mix_g", "conv_b", "lru_wa", "lru_ba", "lru_wx", "lru_bx", "lru_lambda", "norm_mlp_g", "norm_final_g"]
    big_names = ["w_in", "w_proj_attn", "w_proj_lru", "w_out", "w_up", "w_down"]
    res = {"conv_w": (g_cw, cw_delta, cw_m, cw_v)}
    for i, nm in enumerate(small_names):
        res[nm] = (s_grad[i], s_delta[i], s_m[i], s_v[i])
    for i, nm in enumerate(big_names):
        res[nm] = tuple(t[None] for t in big_out[i])
    return (loss, grad_x[None], *[res[nm][0] for nm in names], *[res[nm][1] for nm in names],
            *[res[nm][2] for nm in names], *[res[nm][3] for nm in names])
```

```python
import jax
import jax.numpy as jnp
from jax import lax
from jax.experimental import pallas as pl
from jax.experimental.pallas import tpu as pltpu
from jax.experimental.pallas import tpu_sc as plsc

F32, BF16 = jnp.float32, jnp.bfloat16
MESH = pl.DeviceIdType.MESH
HBM = pl.BlockSpec(memory_space=pltpu.HBM)
N_DEV = 8
N_CHIP = 4
HEAD = 128
SPAN = 128
DILATIONS = (1, 4, 16)
CONV_TAPS = 4
LRU_C = 8.0
NORM_EPS = 1e-6
LANES = 128
SUBLANES = 8
VMEM_LIMIT = 56 * 1024 * 1024
ADAM_LR, ADAM_B1, ADAM_B2, ADAM_EPS, ADAM_WD, ADAM_STEP = 0.001, 0.9, 0.999, 1e-08, 0.01, 10
ADAM_C1 = 1.0 - ADAM_B1 ** ADAM_STEP
ADAM_C2 = 1.0 - ADAM_B2 ** ADAM_STEP
NEG = -1e30


def _params(sem=None):
    return pltpu.CompilerParams(dimension_semantics=sem, vmem_limit_bytes=VMEM_LIMIT)


def _sigmoid(v):
    return 1.0 / (1.0 + jnp.exp(-v))


def _gelu(v):
    k = 0.7978845608028654
    return 0.5 * v * (1.0 + jnp.tanh(k * (v + 0.044715 * v * v * v)))


def _gelu_grad(v):
    k = 0.7978845608028654
    t = jnp.tanh(k * (v + 0.044715 * v * v * v))
    return 0.5 * (1.0 + t) + 0.5 * v * (1.0 - t * t) * k * (1.0 + 3.0 * 0.044715 * v * v)


NN = (((1,), (0,)), ((), ()))
NT = (((1,), (1,)), ((), ()))
TN = (((0,), (0,)), ((), ()))


def _mm(name, a, a_spec, b, b_spec, dn, grid, out_shapes, out_specs, acc_block, epilogue=None, extras=(), extra_specs=()):
    nk, ne, no = grid[2], len(extras), len(out_shapes)

    def body(*refs):
        a_ref, b_ref = refs[0], refs[1]
        ex, outs = refs[2:2 + ne], refs[2 + ne:2 + ne + no]
        part = lax.dot_general(a_ref[...], b_ref[...], dn, preferred_element_type=F32)

        def finish(acc):
            vals = epilogue(acc, *[e[...] for e in ex]) if epilogue is not None else (acc,)
            for o, v in zip(outs, vals):
                o[...] = v.astype(o.dtype)

        if nk == 1:
            finish(part)
        else:
            acc_ref, k = refs[-1], pl.program_id(2)

            @pl.when(k == 0)
            def _():
                acc_ref[...] = part

            @pl.when(k > 0)
            def _():
                acc_ref[...] += part

            @pl.when(k == nk - 1)
            def _():
                finish(acc_ref[...])

    return pl.pallas_call(
        body, out_shape=out_shapes, grid=grid, in_specs=[a_spec, b_spec, *extra_specs], out_specs=out_specs,
        scratch_shapes=[pltpu.VMEM(acc_block, F32)] if nk > 1 else [],
        compiler_params=_params(("parallel", "parallel", "arbitrary")), name=name)(a, b, *extras)


def _blk(n, pref):
    return pref if n % pref == 0 else n


def _kblk(k):
    return k if k <= 2048 else next(b for b in (2048, 1024, 512) if k % b == 0)


def _mm_fwd(name, a, w, col0, ncols, out_dtypes, epilogue=None, extras=(), extra_specs_fn=None, seg_out=None, bm=1024, bn=1024):
    m, k = a.shape
    bm, bn = _blk(m, bm), _blk(ncols, bn)
    bk = _kblk(k)
    nk = k // bk
    cb0 = col0 // bn
    grid = (m // bm, ncols // bn, nk)
    a_spec = pl.BlockSpec((bm, bk), lambda i, j, kk: (i, kk))
    b_spec = pl.BlockSpec((bk, bn), lambda i, j, kk: (kk, cb0 + j))
    if seg_out is None:
        shapes = [jax.ShapeDtypeStruct((m, ncols), dt) for dt in out_dtypes]
        specs = [pl.BlockSpec((bm, bn), lambda i, j, kk: (i, j)) for _ in out_dtypes]
    else:
        per = seg_out // bn
        shapes = [jax.ShapeDtypeStruct((ncols // seg_out, m, seg_out), dt) for dt in out_dtypes]
        specs = [pl.BlockSpec((None, bm, bn), lambda i, j, kk: (j // per, i, j % per)) for _ in out_dtypes]
    ex_specs = extra_specs_fn(bm, bn) if extra_specs_fn else ()
    return _mm(name, a, a_spec, w, b_spec, NN, grid, shapes, specs, (bm, bn), epilogue, extras, ex_specs)


def _mm_nt(name, a, w, out_dtypes, epilogue=None, extras=(), extra_specs_fn=None, bm=1024, bn=1024):
    m, k = a.shape
    n = w.shape[0]
    bm, bn = _blk(m, bm), _blk(n, bn)
    bk = _kblk(k)
    grid = (m // bm, n // bn, k // bk)
    a_spec = pl.BlockSpec((bm, bk), lambda i, j, kk: (i, kk))
    b_spec = pl.BlockSpec((bn, bk), lambda i, j, kk: (j, kk))
    shapes = [jax.ShapeDtypeStruct((m, n), dt) for dt in out_dtypes]
    specs = [pl.BlockSpec((bm, bn), lambda i, j, kk: (i, j)) for _ in out_dtypes]
    ex_specs = extra_specs_fn(bm, bn) if extra_specs_fn else ()
    return _mm(name, a, a_spec, w, b_spec, NT, grid, shapes, specs, (bm, bn), epilogue, extras, ex_specs)


def _mm_tn(name, a, b, bm=1024, bn=1024):
    t, m = a.shape
    n = b.shape[1]
    bm, bn = _blk(m, bm), _blk(n, bn)
    grid = (m // bm, n // bn, 1)
    a_spec = pl.BlockSpec((t, bm), lambda i, j, kk: (0, i))
    b_spec = pl.BlockSpec((t, bn), lambda i, j, kk: (0, j))
    return _mm(name, a, a_spec, b, b_spec, TN, grid, [jax.ShapeDtypeStruct((m, n), BF16)],
               [pl.BlockSpec((bm, bn), lambda i, j, kk: (i, j))], (bm, bn))[0]


ROWS = 256


def _row_spec(d):
    return pl.BlockSpec((ROWS, d), lambda i: (i, 0))


def _vec_spec(d, rows=1):
    return pl.BlockSpec((rows, d), lambda i: (0, 0))


def _rms_fwd(name, x, g):
    s, d = x.shape

    def body(x_ref, g_ref, o_ref):
        xv = x_ref[...]
        r = lax.rsqrt(jnp.mean(xv * xv, axis=-1, keepdims=True) + NORM_EPS)
        o_ref[...] = (xv * r * g_ref[...]).astype(BF16)

    return pl.pallas_call(body, out_shape=jax.ShapeDtypeStruct((s, d), BF16), grid=(s // ROWS,),
                          in_specs=[_row_spec(d), _vec_spec(d)], out_specs=_row_spec(d),
                          compiler_params=_params(("parallel",)), name=name)(x, g)


def _rms_bwd_math(xv, g, dy):
    r = lax.rsqrt(jnp.mean(xv * xv, axis=-1, keepdims=True) + NORM_EPS)
    n = xv * r
    z = dy * g
    dx = r * (z - n * jnp.mean(z * n, axis=-1, keepdims=True))
    return dx, jnp.sum(dy * n, axis=0, keepdims=True)


def _rms_bwd(name, x, g, dy, resid):
    s, d = x.shape

    def body(x_ref, g_ref, dy_ref, r_ref, dx_ref, dxb_ref, dg_ref):
        dx, dg = _rms_bwd_math(x_ref[...], g_ref[...], dy_ref[...])
        dx = dx + r_ref[...]
        dx_ref[...] = dx
        dxb_ref[...] = dx.astype(BF16)

        @pl.when(pl.program_id(0) == 0)
        def _():
            dg_ref[...] = jnp.zeros_like(dg_ref)

        dg_ref[...] += dg

    return pl.pallas_call(
        body, out_shape=[jax.ShapeDtypeStruct((s, d), F32), jax.ShapeDtypeStruct((s, d), BF16), jax.ShapeDtypeStruct((1, d), F32)],
        grid=(s // ROWS,), in_specs=[_row_spec(d), _vec_spec(d), _row_spec(d), _row_spec(d)],
        out_specs=[_row_spec(d), _row_spec(d), _vec_spec(d)], compiler_params=_params(("arbitrary",)), name=name)(x, g, dy, resid)


def _final_loss(h2, tgt, g):
    s, d = h2.shape

    def body(x_ref, t_ref, g_ref, dx_ref, dxb_ref, dg_ref, ls_ref):
        xv, gv = x_ref[...], g_ref[...]
        r = lax.rsqrt(jnp.mean(xv * xv, axis=-1, keepdims=True) + NORM_EPS)
        diff = xv * r * gv - t_ref[...]
        dx, dg = _rms_bwd_math(xv, gv, diff * (1.0 / d))
        dx_ref[...] = dx
        dxb_ref[...] = dx.astype(BF16)

        @pl.when(pl.program_id(0) == 0)
        def _():
            dg_ref[...] = jnp.zeros_like(dg_ref)
            ls_ref[...] = jnp.zeros_like(ls_ref)

        dg_ref[...] += dg
        ls_ref[...] += jnp.sum(diff * diff, axis=0, keepdims=True)

    return pl.pallas_call(
        body, out_shape=[jax.ShapeDtypeStruct((s, d), F32), jax.ShapeDtypeStruct((s, d), BF16),
                         jax.ShapeDtypeStruct((1, d), F32), jax.ShapeDtypeStruct((1, d), F32)],
        grid=(s // ROWS,), in_specs=[_row_spec(d), _row_spec(d), _vec_spec(d)],
        out_specs=[_row_spec(d), _row_spec(d), _vec_spec(d), _vec_spec(d)],
        compiler_params=_params(("arbitrary",)), name="final_norm_loss")(h2, tgt, g)


def _attn_units(s):
    units = []
    for gi, d in enumerate(DILATIONS):
        for r in range(d):
            for n in range(s // d // HEAD):
                units.append((gi, d, r, n))
    return units


def _stream_rows(d, r, first_blk, n_blk):
    start, size = r + first_blk * HEAD * d, n_blk * HEAD
    return pl.ds(start, size) if d == 1 else pl.ds(start, size, stride=d)


def _attn_scores(q_ref, k_ref, slope, d, r, n):
    nkb = 1 if n == 0 else 2
    qrows, krows = _stream_rows(d, r, n, 1), _stream_rows(d, r, n - nkb + 1, nkb)
    qb = q_ref[qrows, :].astype(BF16)
    kb = k_ref[krows, :].astype(BF16)
    sc = lax.dot_general(qb, kb, NT, preferred_element_type=F32) * (HEAD ** -0.5)
    qi = lax.broadcasted_iota(jnp.int32, (HEAD, nkb * HEAD), 0)
    kj = lax.broadcasted_iota(jnp.int32, (HEAD, nkb * HEAD), 1)
    dist = (nkb - 1) * HEAD + qi - kj
    valid = (dist >= 0) & (dist <= SPAN)
    sc = sc - (slope * d) * dist.astype(F32)
    return jnp.where(valid, sc, NEG), valid, qb, kb, qrows, krows


def _attn_fwd(qkv, slopes):
    _, s, dm = qkv.shape
    units = _attn_units(s)

    def body(sl_ref, q_ref, k_ref, v_ref, att_ref, lse_ref, *scr):
        o_scr, l_scr = scr[:3], scr[3:]
        slope = sl_ref[pl.program_id(0)]
        for gi, d, r, n in units:
            sc, _, _, _, qrows, krows = _attn_scores(q_ref, k_ref, slope, d, r, n)
            m = jnp.max(sc, axis=-1, keepdims=True)
            p = jnp.exp(sc - m)
            l = jnp.sum(p, axis=-1, keepdims=True)
            vb = v_ref[krows, :].astype(BF16)
            o = lax.dot_general(p.astype(BF16), vb, NN, preferred_element_type=F32) / l
            o_scr[gi][qrows, :] = o
            l_scr[gi][qrows, :] = jnp.broadcast_to(m + jnp.log(l), (HEAD, HEAD))
        l0, l1, l2 = l_scr[0][...], l_scr[1][...], l_scr[2][...]
        m = jnp.maximum(jnp.maximum(l0, l1), l2)
        w0, w1, w2 = jnp.exp(l0 - m), jnp.exp(l1 - m), jnp.exp(l2 - m)
        tot = w0 + w1 + w2
        att_ref[...] = ((w0 * o_scr[0][...] + w1 * o_scr[1][...] + w2 * o_scr[2][...]) / tot).astype(BF16)
        lse_ref[...] = m + jnp.log(tot)

    def seg(i):
        return pl.BlockSpec((None, s, HEAD), lambda h: (i, 0, h))

    col = pl.BlockSpec((s, HEAD), lambda h: (0, h))
    return pl.pallas_call(
        body, out_shape=[jax.ShapeDtypeStruct((s, dm), BF16), jax.ShapeDtypeStruct((s, dm), F32)], grid=(dm // HEAD,),
        in_specs=[pl.BlockSpec(memory_space=pltpu.SMEM), seg(0), seg(1), seg(2)], out_specs=[col, col],
        scratch_shapes=[pltpu.VMEM((s, HEAD), F32)] * (2 * len(DILATIONS)),
        compiler_params=_params(("parallel",)), name="attn_fwd")(slopes, qkv, qkv, qkv)


def _attn_bwd(qkv, datt, att, lse, slopes):
    _, s, dm = qkv.shape
    units = _attn_units(s)

    def body(sl_ref, q_ref, k_ref, v_ref, do_ref, att_ref, lse_ref, dq_ref, dk_ref, dv_ref, dq_scr, dk_scr, dv_scr, dl_scr):
        slope = sl_ref[pl.program_id(0)]
        delta = jnp.sum(do_ref[...] * att_ref[...].astype(F32), axis=-1, keepdims=True)
        dl_scr[...] = jnp.broadcast_to(delta, (s, HEAD))
        dq_scr[...] = jnp.zeros_like(dq_scr)
        dk_scr[...] = jnp.zeros_like(dk_scr)
        dv_scr[...] = jnp.zeros_like(dv_scr)
        for gi, d, r, n in units:
            sc, valid, qb, kb, qrows, krows = _attn_scores(q_ref, k_ref, slope, d, r, n)
            p = jnp.where(valid, jnp.exp(sc - lse_ref[qrows, :][:, 0:1]), 0.0)
            vb = v_ref[krows, :].astype(BF16)
            dob = do_ref[qrows, :].astype(BF16)
            dp = lax.dot_general(dob, vb, NT, preferred_element_type=F32)
            ds = (p * (dp - dl_scr[qrows, :][:, 0:1]) * (HEAD ** -0.5)).astype(BF16)
            dq_scr[qrows, :] += lax.dot_general(ds, kb, NN, preferred_element_type=F32)
            dk_scr[krows, :] += lax.dot_general(ds, qb, TN, preferred_element_type=F32)
            dv_scr[krows, :] += lax.dot_general(p.astype(BF16), dob, TN, preferred_element_type=F32)
        dq_ref[...] = dq_scr[...].astype(BF16)
        dk_ref[...] = dk_scr[...].astype(BF16)
        dv_ref[...] = dv_scr[...].astype(BF16)

    def seg(i):
        return pl.BlockSpec((None, s, HEAD), lambda h: (i, 0, h))

    col = pl.BlockSpec((s, HEAD), lambda h: (0, h))
    return pl.pallas_call(
        body, out_shape=[jax.ShapeDtypeStruct((s, dm), BF16)] * 3, grid=(dm // HEAD,),
        in_specs=[pl.BlockSpec(memory_space=pltpu.SMEM), seg(0), seg(1), seg(2), col, col, col], out_specs=[col, col, col],
        scratch_shapes=[pltpu.VMEM((s, HEAD), F32)] * 4,
        compiler_params=_params(("parallel",)), name="attn_bwd")(slopes, qkv, qkv, qkv, datt, att, lse)


VEC_CB, VEC_BA, VEC_BX, VEC_LAM = 0, 1, 2, 3


def _to_3d(ref3, val):
    lw = val.shape[1] // SUBLANES
    for j in range(SUBLANES):
        ref3[:, j, :] = val[:, j * lw:(j + 1) * lw]


def _from_3d(ref3):
    return jnp.concatenate([ref3[:, j, :] for j in range(SUBLANES)], axis=1)


def _softplus(z):
    return jnp.maximum(z, 0.0) + jnp.log1p(jnp.exp(-jnp.abs(z)))


def _gate_math(xc, wa_ref, wx_ref, vec):
    xcb = xc.astype(BF16)
    nh = xc.shape[1] // HEAD
    pre_a = jnp.concatenate([jnp.dot(xcb[:, h * HEAD:(h + 1) * HEAD], wa_ref[h], preferred_element_type=F32) for h in range(nh)], axis=1)
    pre_x = jnp.concatenate([jnp.dot(xcb[:, h * HEAD:(h + 1) * HEAD], wx_ref[h], preferred_element_type=F32) for h in range(nh)], axis=1)
    ra = _sigmoid(pre_a + vec[VEC_BA:VEC_BA + 1])
    ig = _sigmoid(pre_x + vec[VEC_BX:VEC_BX + 1])
    sp = _softplus(-vec[VEC_LAM:VEC_LAM + 1])
    log_a = -LRU_C * ra * sp
    a = jnp.exp(log_a)
    z = 2.0 * log_a
    one_minus_a2 = jnp.where(z > -0.01, -z * (1.0 + z * (0.5 + z * (1.0 / 6.0))), 1.0 - jnp.exp(z))
    mult = jnp.sqrt(one_minus_a2)
    return dict(xcb=xcb, ra=ra, ig=ig, sp=sp, a=a, mult=mult)


def _conv_pad_prev(pad_ref, cur, halo, first):
    pad_ref[0:SUBLANES, :] = jnp.where(first, 0.0, halo)
    pad_ref[SUBLANES:SUBLANES + cur.shape[0], :] = cur


def _gates_fwd(rest, cw8, vec8, wa, wx):
    _, s, d = rest.shape
    lw = d // SUBLANES
    hb = ROWS // SUBLANES

    def body(x_ref, halo_ref, cw_ref, vec_ref, wa_ref, wx_ref, a_ref, u_ref, xc_ref, pad):
        _conv_pad_prev(pad, x_ref[...], halo_ref[...], pl.program_id(0) == 0)
        vec = vec_ref[...]
        xc = vec[VEC_CB:VEC_CB + 1]
        for k in range(CONV_TAPS):
            xc = xc + cw_ref[k:k + 1, :] * pad[pl.ds(SUBLANES - (CONV_TAPS - 1) + k, ROWS), :]
        gm = _gate_math(xc, wa_ref, wx_ref, vec)
        xc_ref[...] = xc
        _to_3d(a_ref, gm["a"])
        _to_3d(u_ref, gm["mult"] * (gm["ig"] * xc))

    spec3 = pl.BlockSpec((ROWS, SUBLANES, lw), lambda i: (i, 0, 0))
    wspec = pl.BlockSpec(wa.shape, lambda i: (0, 0, 0))
    return pl.pallas_call(
        body, out_shape=[jax.ShapeDtypeStruct((s, SUBLANES, lw), F32)] * 2 + [jax.ShapeDtypeStruct((s, d), F32)], grid=(s // ROWS,),
        in_specs=[pl.BlockSpec((None, ROWS, d), lambda i: (0, i, 0)),
                  pl.BlockSpec((None, SUBLANES, d), lambda i: (0, jnp.maximum(i * hb - 1, 0), 0)),
                  _vec_spec(d, SUBLANES), _vec_spec(d, SUBLANES), wspec, wspec],
        out_specs=[spec3, spec3, _row_spec(d)], scratch_shapes=[pltpu.VMEM((ROWS + SUBLANES, d), F32)],
        compiler_params=_params(("parallel",)), name="lru_gates_fwd")(rest, rest, cw8, vec8, wa, wx)


def _scan_fwd(a3, u3):
    s, _, lw = a3.shape

    def body(a_ref, u_ref, h_ref, hp_ref, carry):
        @pl.when(pl.program_id(0) == 0)
        def _():
            carry[...] = jnp.zeros_like(carry)

        def step(t, h):
            hp_ref[t] = h
            hn = a_ref[t] * h + u_ref[t]
            h_ref[t] = hn
            return hn

        carry[...] = lax.fori_loop(0, ROWS, step, carry[...], unroll=8)

    spec3 = pl.BlockSpec((ROWS, SUBLANES, lw), lambda i: (i, 0, 0))
    return pl.pallas_call(body, out_shape=[jax.ShapeDtypeStruct(a3.shape, F32)] * 2, grid=(s // ROWS,), in_specs=[spec3, spec3],
                          out_specs=[spec3, spec3], scratch_shapes=[pltpu.VMEM((SUBLANES, lw), F32)],
                          compiler_params=_params(("arbitrary",)), name="lru_scan_fwd")(a3, u3)


def _lru_out(h3, rest):
    s, _, lw = h3.shape
    d = lw * SUBLANES

    def body(h_ref, g_ref, y_ref, h2_ref):
        h = _from_3d(h_ref)
        h2_ref[...] = h
        y_ref[...] = (h * _gelu(g_ref[...])).astype(BF16)

    return pl.pallas_call(
        body, out_shape=[jax.ShapeDtypeStruct((s, d), BF16), jax.ShapeDtypeStruct((s, d), F32)], grid=(s // ROWS,),
        in_specs=[pl.BlockSpec((ROWS, SUBLANES, lw), lambda i: (i, 0, 0)), pl.BlockSpec((None, ROWS, d), lambda i: (1, i, 0))],
        out_specs=[_row_spec(d), _row_spec(d)], compiler_params=_params(("parallel",)), name="lru_out")(h3, rest)


def _scan_bwd(a3, hp3, dh):
    s, _, lw = a3.shape
    d = lw * SUBLANES
    nb = s // ROWS

    def body(a_ref, hp_ref, dh_ref, g_ref, da_ref, dh3, carry):
        @pl.when(pl.program_id(0) == 0)
        def _():
            carry[...] = jnp.zeros_like(carry)

        _to_3d(dh3, dh_ref[...])

        def step(j, c):
            t = ROWS - 1 - j
            g = dh3[t] + c
            g_ref[t] = g
            da_ref[t] = g * hp_ref[t]
            return a_ref[t] * g

        carry[...] = lax.fori_loop(0, ROWS, step, carry[...], unroll=8)

    spec3 = pl.BlockSpec((ROWS, SUBLANES, lw), lambda i: (nb - 1 - i, 0, 0))
    return pl.pallas_call(
        body, out_shape=[jax.ShapeDtypeStruct(a3.shape, F32)] * 2, grid=(nb,),
        in_specs=[spec3, spec3, pl.BlockSpec((ROWS, d), lambda i: (nb - 1 - i, 0))], out_specs=[spec3, spec3],
        scratch_shapes=[pltpu.VMEM((ROWS, SUBLANES, lw), F32), pltpu.VMEM((SUBLANES, lw), F32)],
        compiler_params=_params(("arbitrary",)), name="lru_scan_bwd")(a3, hp3, dh)


def _gates_bwd(g3, da3, xc, wa, wx, vec8):
    s, d = xc.shape
    lw = d // SUBLANES
    nh = d // HEAD

    def body(g_ref, da_ref, xc_ref, wa_ref, wx_ref, vec_ref, dxc_ref, dwa_ref, dwx_ref, dvec_ref):
        @pl.when(pl.program_id(0) == 0)
        def _():
            dwa_ref[...] = jnp.zeros_like(dwa_ref)
            dwx_ref[...] = jnp.zeros_like(dwx_ref)
            dvec_ref[...] = jnp.zeros_like(dvec_ref)

        xc_v, vec = xc_ref[...], vec_ref[...]
        du, da = _from_3d(g_ref), _from_3d(da_ref)
        gm = _gate_math(xc_v, wa_ref, wx_ref, vec)
        ra, ig, sp, a, mult = gm["ra"], gm["ig"], gm["sp"], gm["a"], gm["mult"]
        dmult = du * ig * xc_v
        dlog_a = da * a - dmult * (a * a) / mult
        dpre_a = dlog_a * (-LRU_C * sp) * ra * (1.0 - ra)
        dpre_x = du * mult * xc_v * ig * (1.0 - ig)
        dlam = jnp.sum(dlog_a * (-LRU_C * ra), axis=0, keepdims=True) * (-_sigmoid(-vec[VEC_LAM:VEC_LAM + 1]))
        dvec_ref[VEC_BA:VEC_BA + 1, :] += jnp.sum(dpre_a, axis=0, keepdims=True)
        dvec_ref[VEC_BX:VEC_BX + 1, :] += jnp.sum(dpre_x, axis=0, keepdims=True)
        dvec_ref[VEC_LAM:VEC_LAM + 1, :] += dlam
        dab, dxb, xcb = dpre_a.astype(BF16), dpre_x.astype(BF16), gm["xcb"]
        back = []
        for h in range(nh):
            cols = slice(h * HEAD, (h + 1) * HEAD)
            dwa_ref[h] += lax.dot_general(xcb[:, cols], dab[:, cols], TN, preferred_element_type=F32)
            dwx_ref[h] += lax.dot_general(xcb[:, cols], dxb[:, cols], TN, preferred_element_type=F32)
            back.append(lax.dot_general(dab[:, cols], wa_ref[h], NT, preferred_element_type=F32)
                        + lax.dot_general(dxb[:, cols], wx_ref[h], NT, preferred_element_type=F32))
        dxc_ref[...] = du * mult * ig + jnp.concatenate(back, axis=1)

    spec3 = pl.BlockSpec((ROWS, SUBLANES, lw), lambda i: (i, 0, 0))
    wspec = pl.BlockSpec(wa.shape, lambda i: (0, 0, 0))
    return pl.pallas_call(
        body, out_shape=[jax.ShapeDtypeStruct((s, d), F32), jax.ShapeDtypeStruct(wa.shape, F32), jax.ShapeDtypeStruct(wa.shape, F32),
                         jax.ShapeDtypeStruct((SUBLANES, d), F32)],
        grid=(s // ROWS,), in_specs=[spec3, spec3, _row_spec(d), wspec, wspec, _vec_spec(d, SUBLANES)],
        out_specs=[_row_spec(d), wspec, wspec, _vec_spec(d, SUBLANES)],
        compiler_params=_params(("arbitrary",)), name="lru_gates_bwd")(g3, da3, xc, wa, wx, vec8)


def _conv_bwd(dxc, rest, cw8):
    s, d = dxc.shape
    hb = ROWS // SUBLANES
    last = s // SUBLANES - 1

    def body(dc_ref, dnext_ref, x_ref, xprev_ref, cw_ref, dx_ref, dcw_ref, padd, padx):
        i = pl.program_id(0)

        @pl.when(i == 0)
        def _():
            dcw_ref[...] = jnp.zeros_like(dcw_ref)

        dc = dc_ref[...]
        padd[0:ROWS, :] = dc
        padd[ROWS:ROWS + SUBLANES, :] = jnp.where(i == pl.num_programs(0) - 1, 0.0, dnext_ref[...])
        _conv_pad_prev(padx, x_ref[...], xprev_ref[...], i == 0)
        dx = jnp.zeros_like(dc)
        for k in range(CONV_TAPS):
            dx = dx + cw_ref[k:k + 1, :] * padd[pl.ds(CONV_TAPS - 1 - k, ROWS), :]
            dcw_ref[k:k + 1, :] += jnp.sum(dc * padx[pl.ds(SUBLANES - (CONV_TAPS - 1) + k, ROWS), :], axis=0, keepdims=True)
        dcw_ref[CONV_TAPS:CONV_TAPS + 1, :] += jnp.sum(dc, axis=0, keepdims=True)
        dx_ref[...] = dx.astype(BF16)

    return pl.pallas_call(
        body, out_shape=[jax.ShapeDtypeStruct((s, d), BF16), jax.ShapeDtypeStruct((SUBLANES, d), F32)], grid=(s // ROWS,),
        in_specs=[_row_spec(d), pl.BlockSpec((SUBLANES, d), lambda i: (jnp.minimum((i + 1) * hb, last), 0)),
                  pl.BlockSpec((None, ROWS, d), lambda i: (0, i, 0)),
                  pl.BlockSpec((None, SUBLANES, d), lambda i: (0, jnp.maximum(i * hb - 1, 0), 0)), _vec_spec(d, SUBLANES)],
        out_specs=[_row_spec(d), _vec_spec(d, SUBLANES)],
        scratch_shapes=[pltpu.VMEM((ROWS + SUBLANES, d), F32), pltpu.VMEM((ROWS + SUBLANES, d), F32)],
        compiler_params=_params(("arbitrary",)), name="lru_conv_bwd")(dxc, dxc, rest, rest, cw8)


def _coords():
    return lax.axis_index("x"), lax.axis_index("y"), lax.axis_index("c")


def _other_chips(x, y):
    return [(1 - x, y), (x, 1 - y), (1 - x, 1 - y)]


def _slab(ref, kind, shard_shape, idx):
    r, c = shard_shape
    if kind == "col":
        return ref.at[:, pl.ds(pl.multiple_of(idx * c, LANES), c)]
    if kind == "row":
        return ref.at[pl.ds(pl.multiple_of(idx * r, SUBLANES), r), :]
    return ref.at[idx]


def _full_shape(shard_shape, kind):
    r, c = shard_shape
    return {"col": (r, c * N_DEV), "row": (r * N_DEV, c), "slot": (N_DEV, r, c)}[kind]


def _handshake(peers):
    barrier = pltpu.get_barrier_semaphore()
    for peer in peers:
        pl.semaphore_signal(barrier, inc=1, device_id=peer, device_id_type=MESH)
    pl.semaphore_wait(barrier, len(peers))


def _launch(name, body, out_shape, operands, sems, sequencer_id):
    if sequencer_id is None:
        return pl.pallas_call(body, out_shape=out_shape, in_specs=[HBM] * len(operands), out_specs=[HBM] * len(out_shape),
                              scratch_shapes=sems, name=name)(*operands)
    return pl.kernel(body, out_type=out_shape, mesh=plsc.ScalarSubcoreMesh(axis_name="seq", num_cores=1), name=name,
                     scratch_types=sems, compiler_params=pltpu.CompilerParams(collective_id=sequencer_id))(*operands)


def _all_gather(name, shards, kinds, sequencer_id=None):
    n = len(shards)
    shapes = [s.shape for s in shards]

    def body(*refs):
        ins, outs = refs[:n], refs[n:2 * n]
        send_sems, recv_sems, local_sems = refs[2 * n:]
        x, y, c = _coords()
        me, sib = (x, y, c), (x, y, 1 - c)
        chips = _other_chips(x, y)
        if sequencer_id is not None:
            _handshake([sib] + [(*chip, c) for chip in chips])

        def part(i, dev):
            return _slab(outs[i], kinds[i], shapes[i], 4 * dev[0] + 2 * dev[1] + dev[2])

        def copy(i, k, block, to, src=None):
            return pltpu.make_async_remote_copy(
                src_ref=part(i, block) if src is None else src, dst_ref=part(i, block),
                send_sem=send_sems.at[7 * i + k], recv_sem=recv_sems.at[7 * i + k], device_id=to, device_id_type=MESH)

        started = []
        for i in range(n):
            for k, to in enumerate([sib] + [(*chip, c) for chip in chips]):
                cp = copy(i, k, me, to, src=ins[i])
                cp.start()
                started.append(cp)
        mine = [pltpu.make_async_copy(ins[i], part(i, me), local_sems.at[i]) for i in range(n)]
        for cp in mine:
            cp.start()
        for i in range(n):
            for j, chip in enumerate(chips):
                copy(i, 1 + j, (*chip, c), me).wait_recv()
                cp = copy(i, 4 + j, (*chip, c), sib)
                cp.start()
                started.append(cp)
        for i in range(n):
            copy(i, 0, sib, me).wait_recv()
            for j, chip in enumerate(chips):
                copy(i, 4 + j, (*chip, 1 - c), me).wait_recv()
        for cp in started:
            cp.wait_send()
        for cp in mine:
            cp.wait()

    out_shape = [jax.ShapeDtypeStruct(_full_shape(s.shape, k), s.dtype) for s, k in zip(shards, kinds)]
    sems = [pltpu.SemaphoreType.DMA((7 * n,)), pltpu.SemaphoreType.DMA((7 * n,)), pltpu.SemaphoreType.DMA((n,))]
    return _launch(name, body, out_shape, shards, sems, sequencer_id)


def _exchange_siblings(name, partials, kinds, shard_shapes, sequencer_id=None):
    n = len(partials)

    def body(*refs):
        ins, outs = refs[:n], refs[n:2 * n]
        send_sems, recv_sems = refs[2 * n:]
        x, y, c = _coords()
        if sequencer_id is not None:
            _handshake([(x, y, 1 - c)])
        cps = []
        for i in range(n):
            for q in range(N_CHIP):
                cps.append(pltpu.make_async_remote_copy(
                    src_ref=_slab(ins[i], kinds[i], shard_shapes[i], 2 * q + (1 - c)), dst_ref=outs[i].at[q],
                    send_sem=send_sems.at[N_CHIP * i + q], recv_sem=recv_sems.at[N_CHIP * i + q],
                    device_id=(x, y, 1 - c), device_id_type=MESH))
        for cp in cps:
            cp.start()
        for cp in cps:
            cp.wait()

    return _launch(name, body, [jax.ShapeDtypeStruct((N_CHIP, *s), BF16) for s in shard_shapes], partials,
                   [pltpu.SemaphoreType.DMA((N_CHIP * n,)), pltpu.SemaphoreType.DMA((N_CHIP * n,))], sequencer_id)


def _exchange_chips(name, chip_sums, sequencer_id=None):
    n = len(chip_sums)

    def body(*refs):
        ins, outs = refs[:n], refs[n:2 * n]
        send_sems, recv_sems = refs[2 * n:]
        x, y, c = _coords()
        if sequencer_id is not None:
            _handshake([(cx, cy, c) for cx, cy in _other_chips(x, y)])
        cps = []
        for i in range(n):
            for k, (cx, cy) in enumerate(_other_chips(x, y)):
                cps.append(pltpu.make_async_remote_copy(
                    src_ref=ins[i].at[2 * cx + cy], dst_ref=outs[i].at[k], send_sem=send_sems.at[3 * i + k],
                    recv_sem=recv_sems.at[3 * i + k], device_id=(cx, cy, c), device_id_type=MESH))
        for cp in cps:
            cp.start()
        for cp in cps:
            cp.wait()

    return _launch(name, body, [jax.ShapeDtypeStruct((3, *t.shape[1:]), BF16) for t in chip_sums], chip_sums,
                   [pltpu.SemaphoreType.DMA((3 * n,)), pltpu.SemaphoreType.DMA((3 * n,))], sequencer_id)


def _all_reduce_small(packed):
    rows = packed.shape[0] // N_DEV

    def body(p_ref, out_ref, rb, tot, send_sems, recv_sems):
        x, y, c = _coords()
        me = 4 * x + 2 * y + c

        def peer(k):
            return (x ^ (k >> 2), y ^ ((k >> 1) & 1), c ^ (k & 1))

        def rows_of(idx):
            return pl.ds(pl.multiple_of(idx * rows, SUBLANES), rows)

        def piece(ref, idx):
            return ref.at[rows_of(idx), :]

        scatter = [pltpu.make_async_remote_copy(src_ref=piece(p_ref, me ^ k), dst_ref=rb.at[k], send_sem=send_sems.at[k],
                                                recv_sem=recv_sems.at[k], device_id=peer(k), device_id_type=MESH) for k in range(1, N_DEV)]
        for cp in scatter:
            cp.start()
        acc = p_ref[rows_of(me), :]
        for cp in scatter:
            cp.wait_recv()
        for k in range(1, N_DEV):
            acc = acc + rb[k]
        tot[...] = acc
        out_ref[rows_of(me), :] = acc
        gather = [pltpu.make_async_remote_copy(src_ref=tot, dst_ref=piece(out_ref, me), send_sem=send_sems.at[N_DEV + k],
                                               recv_sem=recv_sems.at[N_DEV + k], device_id=peer(k), device_id_type=MESH)
                  for k in range(1, N_DEV)]
        for cp in gather:
            cp.start()
        for k in range(1, N_DEV):
            pltpu.make_async_remote_copy(src_ref=tot, dst_ref=piece(out_ref, me ^ k), send_sem=send_sems.at[N_DEV + k],
                                         recv_sem=recv_sems.at[N_DEV + k], device_id=peer(k), device_id_type=MESH).wait_recv()
        for cp in scatter + gather:
            cp.wait_send()

    vm = pl.BlockSpec(memory_space=pltpu.VMEM)
    return pl.pallas_call(
        body, out_shape=jax.ShapeDtypeStruct(packed.shape, F32), in_specs=[vm], out_specs=vm,
        scratch_shapes=[pltpu.VMEM((N_DEV, rows, LANES), F32), pltpu.VMEM((rows, LANES), F32),
                        pltpu.SemaphoreType.DMA((2 * N_DEV,)), pltpu.SemaphoreType.DMA((2 * N_DEV,))],
        compiler_params=pltpu.CompilerParams(vmem_limit_bytes=VMEM_LIMIT), name="all_reduce_small")(packed)


def _adamw_math(g, w, m, v):
    m = ADAM_B1 * m + (1.0 - ADAM_B1) * g
    v = ADAM_B2 * v + (1.0 - ADAM_B2) * (g * g)
    delta = -ADAM_LR * ((m / ADAM_C1) / (jnp.sqrt(v / ADAM_C2) + ADAM_EPS) + ADAM_WD * w)
    return delta, m, v


def _slab_spec(kind, shard_shape, tr, slab_of):
    r, c = shard_shape
    if kind == "col":
        return pl.BlockSpec((tr, c), lambda q, i, sc: (i, slab_of(q, sc)))
    return pl.BlockSpec((tr, c), lambda q, i, sc: (slab_of(q, sc) * (r // tr) + i, 0))


def _chip_sum(name, partial, recv, kind, shard_shape, core):
    r, c = shard_shape
    tr = _blk(r, 256)

    def body(core_ref, p_ref, r_ref, o_ref):
        o_ref[...] = (p_ref[...].astype(F32) + r_ref[...].astype(F32)).astype(BF16)

    spec4 = pl.BlockSpec((None, tr, c), lambda q, i, sc: (q, i, 0))
    grid_spec = pltpu.PrefetchScalarGridSpec(
        num_scalar_prefetch=1, grid=(N_CHIP, r // tr),
        in_specs=[_slab_spec(kind, shard_shape, tr, lambda q, sc: 2 * q + sc[0]), spec4], out_specs=spec4)
    return pl.pallas_call(body, out_shape=jax.ShapeDtypeStruct((N_CHIP, r, c), BF16), grid_spec=grid_spec,
                          compiler_params=_params(("parallel", "parallel")), name=name)(core, partial, recv)


def _adamw_shard(name, chip_sums, recv, w, m, v, chip):
    r, c = w.shape
    tr = _blk(r, 128)

    def body(chip_ref, t_ref, r_ref, w_ref, m_ref, v_ref, g_out, d_out, m_out, v_out):
        g = t_ref[...].astype(F32)
        for k in range(3):
            g = g + r_ref[k].astype(F32)
        g_out[...] = g
        d_out[...], m_out[...], v_out[...] = _adamw_math(g, w_ref[...], m_ref[...], v_ref[...])

    blk = pl.BlockSpec((tr, c), lambda i, sc: (i, 0))
    grid_spec = pltpu.PrefetchScalarGridSpec(
        num_scalar_prefetch=1, grid=(r // tr,),
        in_specs=[pl.BlockSpec((None, tr, c), lambda i, sc: (sc[0], i, 0)), pl.BlockSpec((3, tr, c), lambda i, sc: (0, i, 0)), blk, blk, blk],
        out_specs=[blk] * 4)
    return pl.pallas_call(body, out_shape=[jax.ShapeDtypeStruct((r, c), F32)] * 4, grid_spec=grid_spec,
                          compiler_params=_params(("parallel",)), name=name)(chip, chip_sums, recv, w, m, v)


def _adamw_small(name, g, w, m, v):
    def body(g_ref, w_ref, m_ref, v_ref, d_out, m_out, v_out):
        d_out[...], m_out[...], v_out[...] = _adamw_math(g_ref[...], w_ref[...], m_ref[...], v_ref[...])

    vm = pl.BlockSpec(memory_space=pltpu.VMEM)
    return pl.pallas_call(body, out_shape=[jax.ShapeDtypeStruct(g.shape, F32)] * 3, in_specs=[vm] * 4, out_specs=[vm] * 3,
                          compiler_params=pltpu.CompilerParams(vmem_limit_bytes=VMEM_LIMIT), name=name)(g, w, m, v)


def _pack_rows(arrays, total_rows):
    flat = [a.reshape(-1, LANES) for a in arrays]
    used = sum(f.shape[0] for f in flat)
    return jnp.concatenate(flat + [jnp.zeros((total_rows - used, LANES), F32)], axis=0)


def _unpack_rows(packed, like):
    out, at = [], 0
    for a in like:
        n = a.size // LANES
        out.append(packed[at:at + n].reshape(a.shape))
        at += n
    return out


def kernel(x, norm_mix_g, w_in, conv_w, conv_b, lru_wa, lru_ba, lru_wx, lru_bx, lru_lambda, w_proj_attn, w_proj_lru, w_out, norm_mlp_g, w_up, w_down, norm_final_g, loss_target, m_norm_mix_g, m_w_in, m_conv_w, m_conv_b, m_lru_wa, m_lru_ba, m_lru_wx, m_lru_bx, m_lru_lambda, m_w_proj_attn, m_w_proj_lru, m_w_out, m_norm_mlp_g, m_w_up, m_w_down, m_norm_final_g, v_norm_mix_g, v_w_in, v_conv_w, v_conv_b, v_lru_wa, v_lru_ba, v_lru_wx, v_lru_bx, v_lru_lambda, v_w_proj_attn, v_w_proj_lru, v_w_out, v_norm_mlp_g, v_w_up, v_w_down, v_norm_final_g):
    xs, tgt = x[0], loss_target[0]
    s, d = xs.shape
    nh = d // HEAD
    ix, iy, ic = _coords()
    core = jnp.reshape(ic, (1,)).astype(jnp.int32)
    chip = jnp.reshape(2 * ix + iy, (1,)).astype(jnp.int32)
    dev = 4 * ix + 2 * iy + ic

    big = [w_in[0], w_proj_attn[0], w_proj_lru[0], w_out[0], w_up[0], w_down[0]]
    big_m = [m_w_in[0], m_w_proj_attn[0], m_w_proj_lru[0], m_w_out[0], m_w_up[0], m_w_down[0]]
    big_v = [v_w_in[0], v_w_proj_attn[0], v_w_proj_lru[0], v_w_out[0], v_w_up[0], v_w_down[0]]
    kinds = ["col", "row", "row", "row", "col", "row"]
    pad_taps = lambda t: jnp.pad(t, ((0, SUBLANES - CONV_TAPS), (0, 0)))
    shards = [w.astype(BF16) for w in big]
    win, cw_slots = _all_gather("all_gather_w_in", [shards[0], pad_taps(conv_w[0])], ["col", "slot"])
    later = lax.optimization_barrier((shards[1:], win))[0]
    wpa, wpl, wout, wup, wdown = _all_gather("all_gather_rest", later, kinds[1:], sequencer_id=1)
    cw8 = jnp.transpose(cw_slots, (1, 0, 2)).reshape(SUBLANES, d)
    row_id = lax.broadcasted_iota(jnp.int32, (SUBLANES, d), 0)
    vec8 = sum(jnp.where(row_id == k, t, 0.0) for k, t in ((VEC_CB, conv_b), (VEC_BA, lru_ba), (VEC_BX, lru_bx), (VEC_LAM, lru_lambda)))
    wa16, wx16 = lru_wa[0].astype(BF16), lru_wx[0].astype(BF16)
    slopes = 2.0 ** (-8.0 * jnp.arange(1, nh + 1, dtype=F32) / nh)

    def seg_specs(*segs):
        return lambda bm, bn: [pl.BlockSpec((None, bm, bn), (lambda i, j, kk, sg=sg: (sg, i, j))) for sg in segs]

    def plain_specs(k):
        return lambda bm, bn: [pl.BlockSpec((bm, bn), lambda i, j, kk: (i, j)) for _ in range(k)]

    xn = _rms_fwd("norm_mix", xs, norm_mix_g)
    qkv = _mm_fwd("proj_qkv", xn, win, 0, 3 * d, [F32], seg_out=d)[0]
    rest = _mm_fwd("proj_rest", xn, win, 3 * d, 4 * d, [F32], seg_out=d)[0]
    att, lse = _attn_fwd(qkv, slopes)
    a3, u3, xc = _gates_fwd(rest, cw8, vec8, wa16, wx16)
    h3, hp3 = _scan_fwd(a3, u3)
    ylru, h2d = _lru_out(h3, rest)
    pa = _mm_fwd("proj_attn", att, wpa, 0, d, [F32])[0]

    def merge(acc, pa_b, ga, gl):
        return acc, _sigmoid(ga) * pa_b + _sigmoid(gl) * acc

    plr, merged = _mm_fwd("proj_lru_merge", ylru, wpl, 0, d, [F32, BF16], merge, (pa, rest, rest),
                          lambda bm, bn: plain_specs(1)(bm, bn) + seg_specs(2, 3)(bm, bn), bm=512)
    h1 = _mm_fwd("mix_out", merged, wout, 0, d, [F32], lambda acc, r: (acc + r,), (xs,), plain_specs(1))[0]
    hn = _rms_fwd("norm_mlp", h1, norm_mlp_g)

    def relu2(acc):
        return acc, jnp.square(jnp.maximum(acc, 0.0))

    up, hid = _mm_fwd("mlp_up", hn, wup, 0, wup.shape[1], [BF16, BF16], relu2)
    h2 = _mm_fwd("mlp_down", hid, wdown, 0, d, [F32], lambda acc, r: (acc + r,), (h1,), plain_specs(1))[0]
    dh2, dh2b, dg3, loss_lanes = _final_loss(h2, tgt, norm_final_g.reshape(1, d))
    loss = lax.psum(0.5 / d * jnp.sum(loss_lanes), ("x", "y", "c"))

    def reduce_group(tag, idxs, partials, sequencer_id):
        kk, shp = [kinds[i] for i in idxs], [big[i].shape for i in idxs]
        from_sibling = _exchange_siblings(f"rs_siblings_{tag}", partials, kk, shp)
        sums = [_chip_sum(f"chip_sum_{i}", p, f, kinds[i], big[i].shape, core) for i, p, f in zip(idxs, partials, from_sibling)]
        from_chips = _exchange_chips(f"rs_chips_{tag}", sums, sequencer_id)
        return {i: (t, r) for i, t, r in zip(idxs, sums, from_chips)}

    dup = _mm_nt("mlp_down_dx", dh2b, wdown, [BF16], lambda acc, u: (acc * (2.0 * jnp.maximum(u.astype(F32), 0.0)),), (up,), plain_specs(1))[0]
    g_wdown = _mm_tn("mlp_down_dw", hid, dh2b)
    dhn = _mm_nt("mlp_up_dx", dup, wup, [F32])[0]
    g_wup = _mm_tn("mlp_up_dw", hn, dup)
    reduced = reduce_group("mlp", [4, 5], [g_wup, g_wdown], 2)
    dh1, dh1b, dg2 = _rms_bwd("norm_mlp_bwd", h1, norm_mlp_g, dhn, dh2)

    def merge_bwd(acc, pa_b, pl_b, ga, gl):
        sa, sl = _sigmoid(ga), _sigmoid(gl)
        return acc * sa, acc * sl, acc * pa_b * sa * (1.0 - sa), acc * pl_b * sl * (1.0 - sl)

    dpa, dpl, dga, dgl = _mm_nt("mix_out_dx", dh1b, wout, [BF16] * 4, merge_bwd, (pa, plr, rest, rest),
                                lambda bm, bn: plain_specs(2)(bm, bn) + seg_specs(2, 3)(bm, bn), bm=512)
    g_wout = _mm_tn("mix_out_dw", merged, dh1b)
    datt = _mm_nt("proj_attn_dx", dpa, wpa, [F32])[0]
    g_wpa = _mm_tn("proj_attn_dw", att, dpa)

    def lru_out_bwd(acc, h_b, gate):
        return acc * _gelu(gate), acc * h_b * _gelu_grad(gate)

    dh, dxg = _mm_nt("proj_lru_dx", dpl, wpl, [F32, BF16], lru_out_bwd, (h2d, rest),
                     lambda bm, bn: plain_specs(1)(bm, bn) + seg_specs(1)(bm, bn), bm=512)
    g_wpl = _mm_tn("proj_lru_dw", ylru, dpl)
    reduced.update(reduce_group("mix", [1, 2, 3], [g_wpa, g_wpl, g_wout], 3))
    dq, dk, dv = _attn_bwd(qkv, datt, att, lse, slopes)
    g3, da3 = _scan_bwd(a3, hp3, dh)
    dxc, dwa, dwx, dvec = _gates_bwd(g3, da3, xc, wa16, wx16, vec8)
    dxr, dconv = _conv_bwd(dxc, rest, cw8)
    dproj = jnp.concatenate([dq, dk, dv, dxr, dxg, dga, dgl], axis=1)
    g_win = _mm_tn("proj_in_dw", xn, dproj)
    reduced.update(reduce_group("in", [0], [g_win], 4))
    dproj = lax.optimization_barrier((dproj, reduced[0][0]))[0]
    dxn = _mm_nt("proj_in_dx", dproj, win, [F32])[0]
    grad_x, _, dg1 = _rms_bwd("norm_mix_bwd", xs, norm_mix_g, dxn, dh1)

    big_out = [_adamw_shard(f"adamw_{i}", reduced[i][0], reduced[i][1], big[i], big_m[i], big_v[i], chip) for i in range(len(big))]

    small_w = [norm_mix_g, conv_b, lru_wa, lru_ba, lru_wx, lru_bx, lru_lambda, norm_mlp_g, norm_final_g]
    small_m = [m_norm_mix_g, m_conv_b, m_lru_wa, m_lru_ba, m_lru_wx, m_lru_bx, m_lru_lambda, m_norm_mlp_g, m_norm_final_g]
    small_v = [v_norm_mix_g, v_conv_b, v_lru_wa, v_lru_ba, v_lru_wx, v_lru_bx, v_lru_lambda, v_norm_mlp_g, v_norm_final_g]
    small_g = [dg1, dconv[CONV_TAPS:CONV_TAPS + 1], dwa, dvec[VEC_BA:VEC_BA + 1], dwx, dvec[VEC_BX:VEC_BX + 1],
               dvec[VEC_LAM:VEC_LAM + 1], dg2, dg3, dconv[0:CONV_TAPS]]
    n_rows = sum(g.size for g in small_g) // LANES
    per_dev = -(-n_rows // (N_DEV * SUBLANES)) * SUBLANES
    total = _all_reduce_small(_pack_rows(small_g, N_DEV * per_dev))
    sw_rows = -(-(n_rows - CONV_TAPS * d // LANES) // SUBLANES) * SUBLANES
    s_delta, s_m, s_v = _adamw_small("adamw_small", total[:sw_rows], _pack_rows(small_w, sw_rows), _pack_rows(small_m, sw_rows),
                                     _pack_rows(small_v, sw_rows))
    sg = _unpack_rows(total, [w for w in small_w] + [jnp.zeros((1, CONV_TAPS, d), F32)])
    s_grad, g_cw_full = sg[:-1], sg[-1]
    s_delta, s_m, s_v = (_unpack_rows(t, small_w) for t in (s_delta, s_m, s_v))
    cshard = conv_w.shape[2]
    g_cw = lax.dynamic_slice(g_cw_full, (0, 0, dev * cshard), (1, CONV_TAPS, cshard))
    cw_delta, cw_m, cw_v = (t[:CONV_TAPS][None] for t in _adamw_small(
        "adamw_conv_w", pad_taps(g_cw[0]), pad_taps(conv_w[0]), pad_taps(m_conv_w[0]), pad_taps(v_conv_w[0])))

    names = ["norm_mix_g", "w_in", "conv_w", "conv_b", "lru_wa", "lru_ba", "lru_wx", "lru_bx", "lru_lambda", "w_proj_attn", "w_proj_lru",
             "w_out", "norm_mlp_g", "w_up", "w_down", "norm_final_g"]
    small_names = ["norm_mix_g", "conv_b", "lru_wa", "lru_ba", "lru_wx", "lru_bx", "lru_lambda", "norm_mlp_g", "norm_final_g"]
    big_names = ["w_in", "w_proj_attn", "w_proj_lru", "w_out", "w_up", "w_down"]
    res = {"conv_w": (g_cw, cw_delta, cw_m, cw_v)}
    for i, nm in enumerate(small_names):
        res[nm] = (s_grad[i], s_delta[i], s_m[i], s_v[i])
    for i, nm in enumerate(big_names):
        res[nm] = tuple(t[None] for t in big_out[i])
    return (loss, grad_x[None], *[res[nm][0] for nm in names], *[res[nm][1] for nm in names],
            *[res[nm][2] for nm in names], *[res[nm][3] for nm in names])
```

```python
import jax
import jax.numpy as jnp
from jax import lax
from jax.experimental import pallas as pl
from jax.experimental.pallas import tpu as pltpu
from jax.experimental.pallas import tpu_sc as plsc

F32, BF16 = jnp.float32, jnp.bfloat16
MESH = pl.DeviceIdType.MESH
HBM = pl.BlockSpec(memory_space=pltpu.HBM)
N_DEV = 8
N_CHIP = 4
HEAD = 128
SPAN = 128
DILATIONS = (1, 4, 16)
CONV_TAPS = 4
LRU_C = 8.0
NORM_EPS = 1e-6
LANES = 128
SUBLANES = 8
VMEM_LIMIT = 56 * 1024 * 1024
ADAM_LR, ADAM_B1, ADAM_B2, ADAM_EPS, ADAM_WD, ADAM_STEP = 0.001, 0.9, 0.999, 1e-08, 0.01, 10
ADAM_C1 = 1.0 - ADAM_B1 ** ADAM_STEP
ADAM_C2 = 1.0 - ADAM_B2 ** ADAM_STEP
NEG = -1e30


def _params(sem=None):
    return pltpu.CompilerParams(dimension_semantics=sem, vmem_limit_bytes=VMEM_LIMIT)


def _sigmoid(v):
    return 1.0 / (1.0 + jnp.exp(-v))


def _gelu(v):
    k = 0.7978845608028654
    return 0.5 * v * (1.0 + jnp.tanh(k * (v + 0.044715 * v * v * v)))


def _gelu_grad(v):
    k = 0.7978845608028654
    t = jnp.tanh(k * (v + 0.044715 * v * v * v))
    return 0.5 * (1.0 + t) + 0.5 * v * (1.0 - t * t) * k * (1.0 + 3.0 * 0.044715 * v * v)


NN = (((1,), (0,)), ((), ()))
NT = (((1,), (1,)), ((), ()))
TN = (((0,), (0,)), ((), ()))


def _mm(name, a, a_spec, b, b_spec, dn, grid, out_shapes, out_specs, acc_block, epilogue=None, extras=(), extra_specs=()):
    nk, ne, no = grid[2], len(extras), len(out_shapes)

    def body(*refs):
        a_ref, b_ref = refs[0], refs[1]
        ex, outs = refs[2:2 + ne], refs[2 + ne:2 + ne + no]
        part = lax.dot_general(a_ref[...], b_ref[...], dn, preferred_element_type=F32)

        def finish(acc):
            vals = epilogue(acc, *[e[...] for e in ex]) if epilogue is not None else (acc,)
            for o, v in zip(outs, vals):
                o[...] = v.astype(o.dtype)

        if nk == 1:
            finish(part)
        else:
            acc_ref, k = refs[-1], pl.program_id(2)

            @pl.when(k == 0)
            def _():
                acc_ref[...] = part

            @pl.when(k > 0)
            def _():
                acc_ref[...] += part

            @pl.when(k == nk - 1)
            def _():
                finish(acc_ref[...])

    return pl.pallas_call(
        body, out_shape=out_shapes, grid=grid, in_specs=[a_spec, b_spec, *extra_specs], out_specs=out_specs,
        scratch_shapes=[pltpu.VMEM(acc_block, F32)] if nk > 1 else [],
        compiler_params=_params(("parallel", "parallel", "arbitrary")), name=name)(a, b, *extras)


def _blk(n, pref):
    return pref if n % pref == 0 else n


def _kblk(k):
    return k if k <= 2048 else next(b for b in (2048, 1024, 512) if k % b == 0)


def _mm_fwd(name, a, w, col0, ncols, out_dtypes, epilogue=None, extras=(), extra_specs_fn=None, seg_out=None, bm=1024, bn=1024):
    m, k = a.shape
    bm, bn = _blk(m, bm), _blk(ncols, bn)
    bk = _kblk(k)
    nk = k // bk
    cb0 = col0 // bn
    grid = (m // bm, ncols // bn, nk)
    a_spec = pl.BlockSpec((bm, bk), lambda i, j, kk: (i, kk))
    b_spec = pl.BlockSpec((bk, bn), lambda i, j, kk: (kk, cb0 + j))
    if seg_out is None:
        shapes = [jax.ShapeDtypeStruct((m, ncols), dt) for dt in out_dtypes]
        specs = [pl.BlockSpec((bm, bn), lambda i, j, kk: (i, j)) for _ in out_dtypes]
    else:
        per = seg_out // bn
        shapes = [jax.ShapeDtypeStruct((ncols // seg_out, m, seg_out), dt) for dt in out_dtypes]
        specs = [pl.BlockSpec((None, bm, bn), lambda i, j, kk: (j // per, i, j % per)) for _ in out_dtypes]
    ex_specs = extra_specs_fn(bm, bn) if extra_specs_fn else ()
    return _mm(name, a, a_spec, w, b_spec, NN, grid, shapes, specs, (bm, bn), epilogue, extras, ex_specs)


def _mm_nt(name, a, w, out_dtypes, epilogue=None, extras=(), extra_specs_fn=None, bm=1024, bn=1024):
    m, k = a.shape
    n = w.shape[0]
    bm, bn = _blk(m, bm), _blk(n, bn)
    bk = _kblk(k)
    grid = (m // bm, n // bn, k // bk)
    a_spec = pl.BlockSpec((bm, bk), lambda i, j, kk: (i, kk))
    b_spec = pl.BlockSpec((bn, bk), lambda i, j, kk: (j, kk))
    shapes = [jax.ShapeDtypeStruct((m, n), dt) for dt in out_dtypes]
    specs = [pl.BlockSpec((bm, bn), lambda i, j, kk: (i, j)) for _ in out_dtypes]
    ex_specs = extra_specs_fn(bm, bn) if extra_specs_fn else ()
    return _mm(name, a, a_spec, w, b_spec, NT, grid, shapes, specs, (bm, bn), epilogue, extras, ex_specs)


def _mm_tn(name, a, b, part=(0, 1), bm=1024, bn=1024):
    t, m = a.shape
    n = b.shape[1]
    m = m // part[1]
    bm, bn = _blk(m, bm), _blk(n, bn)
    grid = (m // bm, n // bn, 1)
    i0 = part[0] * (m // bm)
    a_spec = pl.BlockSpec((t, bm), lambda i, j, kk: (0, i0 + i))
    b_spec = pl.BlockSpec((t, bn), lambda i, j, kk: (0, j))
    return _mm(name, a, a_spec, b, b_spec, TN, grid, [jax.ShapeDtypeStruct((m, n), BF16)],
               [pl.BlockSpec((bm, bn), lambda i, j, kk: (i, j))], (bm, bn))[0]


ROWS = 256


def _row_spec(d):
    return pl.BlockSpec((ROWS, d), lambda i: (i, 0))


def _vec_spec(d, rows=1):
    return pl.BlockSpec((rows, d), lambda i: (0, 0))


def _rms_fwd(name, x, g):
    s, d = x.shape

    def body(x_ref, g_ref, o_ref):
        xv = x_ref[...]
        r = lax.rsqrt(jnp.mean(xv * xv, axis=-1, keepdims=True) + NORM_EPS)
        o_ref[...] = (xv * r * g_ref[...]).astype(BF16)

    return pl.pallas_call(body, out_shape=jax.ShapeDtypeStruct((s, d), BF16), grid=(s // ROWS,),
                          in_specs=[_row_spec(d), _vec_spec(d)], out_specs=_row_spec(d),
                          compiler_params=_params(("parallel",)), name=name)(x, g)


def _rms_bwd_math(xv, g, dy):
    r = lax.rsqrt(jnp.mean(xv * xv, axis=-1, keepdims=True) + NORM_EPS)
    n = xv * r
    z = dy * g
    dx = r * (z - n * jnp.mean(z * n, axis=-1, keepdims=True))
    return dx, jnp.sum(dy * n, axis=0, keepdims=True)


def _rms_bwd(name, x, g, dy, resid):
    s, d = x.shape

    def body(x_ref, g_ref, dy_ref, r_ref, dx_ref, dxb_ref, dg_ref):
        dx, dg = _rms_bwd_math(x_ref[...], g_ref[...], dy_ref[...])
        dx = dx + r_ref[...]
        dx_ref[...] = dx
        dxb_ref[...] = dx.astype(BF16)

        @pl.when(pl.program_id(0) == 0)
        def _():
            dg_ref[...] = jnp.zeros_like(dg_ref)

        dg_ref[...] += dg

    return pl.pallas_call(
        body, out_shape=[jax.ShapeDtypeStruct((s, d), F32), jax.ShapeDtypeStruct((s, d), BF16), jax.ShapeDtypeStruct((1, d), F32)],
        grid=(s // ROWS,), in_specs=[_row_spec(d), _vec_spec(d), _row_spec(d), _row_spec(d)],
        out_specs=[_row_spec(d), _row_spec(d), _vec_spec(d)], compiler_params=_params(("arbitrary",)), name=name)(x, g, dy, resid)


def _final_loss(h2, tgt, g):
    s, d = h2.shape

    def body(x_ref, t_ref, g_ref, dx_ref, dxb_ref, dg_ref, ls_ref):
        xv, gv = x_ref[...], g_ref[...]
        r = lax.rsqrt(jnp.mean(xv * xv, axis=-1, keepdims=True) + NORM_EPS)
        diff = xv * r * gv - t_ref[...]
        dx, dg = _rms_bwd_math(xv, gv, diff * (1.0 / d))
        dx_ref[...] = dx
        dxb_ref[...] = dx.astype(BF16)

        @pl.when(pl.program_id(0) == 0)
        def _():
            dg_ref[...] = jnp.zeros_like(dg_ref)
            ls_ref[...] = jnp.zeros_like(ls_ref)

        dg_ref[...] += dg
        ls_ref[...] += jnp.sum(diff * diff, axis=0, keepdims=True)

    return pl.pallas_call(
        body, out_shape=[jax.ShapeDtypeStruct((s, d), F32), jax.ShapeDtypeStruct((s, d), BF16),
                         jax.ShapeDtypeStruct((1, d), F32), jax.ShapeDtypeStruct((1, d), F32)],
        grid=(s // ROWS,), in_specs=[_row_spec(d), _row_spec(d), _vec_spec(d)],
        out_specs=[_row_spec(d), _row_spec(d), _vec_spec(d), _vec_spec(d)],
        compiler_params=_params(("arbitrary",)), name="final_norm_loss")(h2, tgt, g)


def _attn_units(s):
    units = []
    for gi, d in enumerate(DILATIONS):
        for r in range(d):
            for n in range(s // d // HEAD):
                units.append((gi, d, r, n))
    return units


def _stream_rows(d, r, first_blk, n_blk):
    start, size = r + first_blk * HEAD * d, n_blk * HEAD
    return pl.ds(start, size) if d == 1 else pl.ds(start, size, stride=d)


def _attn_scores(q_ref, k_ref, slope, d, r, n):
    nkb = 1 if n == 0 else 2
    qrows, krows = _stream_rows(d, r, n, 1), _stream_rows(d, r, n - nkb + 1, nkb)
    qb = q_ref[qrows, :].astype(BF16)
    kb = k_ref[krows, :].astype(BF16)
    sc = lax.dot_general(qb, kb, NT, preferred_element_type=F32) * (HEAD ** -0.5)
    qi = lax.broadcasted_iota(jnp.int32, (HEAD, nkb * HEAD), 0)
    kj = lax.broadcasted_iota(jnp.int32, (HEAD, nkb * HEAD), 1)
    dist = (nkb - 1) * HEAD + qi - kj
    valid = (dist >= 0) & (dist <= SPAN)
    sc = sc - (slope * d) * dist.astype(F32)
    return jnp.where(valid, sc, NEG), valid, qb, kb, qrows, krows


def _attn_fwd(qkv, slopes):
    _, s, dm = qkv.shape
    units = _attn_units(s)

    def body(sl_ref, q_ref, k_ref, v_ref, att_ref, lse_ref, *scr):
        o_scr, l_scr = scr[:3], scr[3:]
        slope = sl_ref[pl.program_id(0)]
        for gi, d, r, n in units:
            sc, _, _, _, qrows, krows = _attn_scores(q_ref, k_ref, slope, d, r, n)
            m = jnp.max(sc, axis=-1, keepdims=True)
            p = jnp.exp(sc - m)
            l = jnp.sum(p, axis=-1, keepdims=True)
            vb = v_ref[krows, :].astype(BF16)
            o = lax.dot_general(p.astype(BF16), vb, NN, preferred_element_type=F32) / l
            o_scr[gi][qrows, :] = o
            l_scr[gi][qrows, :] = jnp.broadcast_to(m + jnp.log(l), (HEAD, HEAD))
        l0, l1, l2 = l_scr[0][...], l_scr[1][...], l_scr[2][...]
        m = jnp.maximum(jnp.maximum(l0, l1), l2)
        w0, w1, w2 = jnp.exp(l0 - m), jnp.exp(l1 - m), jnp.exp(l2 - m)
        tot = w0 + w1 + w2
        att_ref[...] = ((w0 * o_scr[0][...] + w1 * o_scr[1][...] + w2 * o_scr[2][...]) / tot).astype(BF16)
        lse_ref[...] = m + jnp.log(tot)

    def seg(i):
        return pl.BlockSpec((None, s, HEAD), lambda h: (i, 0, h))

    col = pl.BlockSpec((s, HEAD), lambda h: (0, h))
    return pl.pallas_call(
        body, out_shape=[jax.ShapeDtypeStruct((s, dm), BF16), jax.ShapeDtypeStruct((s, dm), F32)], grid=(dm // HEAD,),
        in_specs=[pl.BlockSpec(memory_space=pltpu.SMEM), seg(0), seg(1), seg(2)], out_specs=[col, col],
        scratch_shapes=[pltpu.VMEM((s, HEAD), F32)] * (2 * len(DILATIONS)),
        compiler_params=_params(("parallel",)), name="attn_fwd")(slopes, qkv, qkv, qkv)


def _attn_bwd(qkv, datt, att, lse, slopes):
    _, s, dm = qkv.shape
    units = _attn_units(s)

    def body(sl_ref, q_ref, k_ref, v_ref, do_ref, att_ref, lse_ref, dq_ref, dk_ref, dv_ref, dq_scr, dk_scr, dv_scr, dl_scr):
        slope = sl_ref[pl.program_id(0)]
        delta = jnp.sum(do_ref[...] * att_ref[...].astype(F32), axis=-1, keepdims=True)
        dl_scr[...] = jnp.broadcast_to(delta, (s, HEAD))
        dq_scr[...] = jnp.zeros_like(dq_scr)
        dk_scr[...] = jnp.zeros_like(dk_scr)
        dv_scr[...] = jnp.zeros_like(dv_scr)
        for gi, d, r, n in units:
            sc, valid, qb, kb, qrows, krows = _attn_scores(q_ref, k_ref, slope, d, r, n)
            p = jnp.where(valid, jnp.exp(sc - lse_ref[qrows, :][:, 0:1]), 0.0)
            vb = v_ref[krows, :].astype(BF16)
            dob = do_ref[qrows, :].astype(BF16)
            dp = lax.dot_general(dob, vb, NT, preferred_element_type=F32)
            ds = (p * (dp - dl_scr[qrows, :][:, 0:1]) * (HEAD ** -0.5)).astype(BF16)
            dq_scr[qrows, :] += lax.dot_general(ds, kb, NN, preferred_element_type=F32)
            dk_scr[krows, :] += lax.dot_general(ds, qb, TN, preferred_element_type=F32)
            dv_scr[krows, :] += lax.dot_general(p.astype(BF16), dob, TN, preferred_element_type=F32)
        dq_ref[...] = dq_scr[...].astype(BF16)
        dk_ref[...] = dk_scr[...].astype(BF16)
        dv_ref[...] = dv_scr[...].astype(BF16)

    def seg(i):
        return pl.BlockSpec((None, s, HEAD), lambda h: (i, 0, h))

    col = pl.BlockSpec((s, HEAD), lambda h: (0, h))
    return pl.pallas_call(
        body, out_shape=[jax.ShapeDtypeStruct((s, dm), BF16)] * 3, grid=(dm // HEAD,),
        in_specs=[pl.BlockSpec(memory_space=pltpu.SMEM), seg(0), seg(1), seg(2), col, col, col], out_specs=[col, col, col],
        scratch_shapes=[pltpu.VMEM((s, HEAD), F32)] * 4,
        compiler_params=_params(("parallel",)), name="attn_bwd")(slopes, qkv, qkv, qkv, datt, att, lse)


VEC_CB, VEC_BA, VEC_BX, VEC_LAM = 0, 1, 2, 3


def _to_3d(ref3, val):
    lw = val.shape[1] // SUBLANES
    for j in range(SUBLANES):
        ref3[:, j, :] = val[:, j * lw:(j + 1) * lw]


def _from_3d(ref3):
    return jnp.concatenate([ref3[:, j, :] for j in range(SUBLANES)], axis=1)


def _softplus(z):
    return jnp.maximum(z, 0.0) + jnp.log1p(jnp.exp(-jnp.abs(z)))


def _gate_math(xc, wa_ref, wx_ref, vec):
    xcb = xc.astype(BF16)
    nh = xc.shape[1] // HEAD
    pre_a = jnp.concatenate([jnp.dot(xcb[:, h * HEAD:(h + 1) * HEAD], wa_ref[h], preferred_element_type=F32) for h in range(nh)], axis=1)
    pre_x = jnp.concatenate([jnp.dot(xcb[:, h * HEAD:(h + 1) * HEAD], wx_ref[h], preferred_element_type=F32) for h in range(nh)], axis=1)
    ra = _sigmoid(pre_a + vec[VEC_BA:VEC_BA + 1])
    ig = _sigmoid(pre_x + vec[VEC_BX:VEC_BX + 1])
    sp = _softplus(-vec[VEC_LAM:VEC_LAM + 1])
    log_a = -LRU_C * ra * sp
    a = jnp.exp(log_a)
    z = 2.0 * log_a
    one_minus_a2 = jnp.where(z > -0.01, -z * (1.0 + z * (0.5 + z * (1.0 / 6.0))), 1.0 - jnp.exp(z))
    mult = jnp.sqrt(one_minus_a2)
    return dict(xcb=xcb, ra=ra, ig=ig, sp=sp, a=a, mult=mult)


def _conv_pad_prev(pad_ref, cur, halo, first):
    pad_ref[0:SUBLANES, :] = jnp.where(first, 0.0, halo)
    pad_ref[SUBLANES:SUBLANES + cur.shape[0], :] = cur


def _gates_fwd(rest, cw8, vec8, wa, wx):
    _, s, d = rest.shape
    lw = d // SUBLANES
    hb = ROWS // SUBLANES

    def body(x_ref, halo_ref, cw_ref, vec_ref, wa_ref, wx_ref, a_ref, u_ref, xc_ref, pad):
        _conv_pad_prev(pad, x_ref[...], halo_ref[...], pl.program_id(0) == 0)
        vec = vec_ref[...]
        xc = vec[VEC_CB:VEC_CB + 1]
        for k in range(CONV_TAPS):
            xc = xc + cw_ref[k:k + 1, :] * pad[pl.ds(SUBLANES - (CONV_TAPS - 1) + k, ROWS), :]
        gm = _gate_math(xc, wa_ref, wx_ref, vec)
        xc_ref[...] = xc
        _to_3d(a_ref, gm["a"])
        _to_3d(u_ref, gm["mult"] * (gm["ig"] * xc))

    spec3 = pl.BlockSpec((ROWS, SUBLANES, lw), lambda i: (i, 0, 0))
    wspec = pl.BlockSpec(wa.shape, lambda i: (0, 0, 0))
    return pl.pallas_call(
        body, out_shape=[jax.ShapeDtypeStruct((s, SUBLANES, lw), F32)] * 2 + [jax.ShapeDtypeStruct((s, d), F32)], grid=(s // ROWS,),
        in_specs=[pl.BlockSpec((None, ROWS, d), lambda i: (0, i, 0)),
                  pl.BlockSpec((None, SUBLANES, d), lambda i: (0, jnp.maximum(i * hb - 1, 0), 0)),
                  _vec_spec(d, SUBLANES), _vec_spec(d, SUBLANES), wspec, wspec],
        out_specs=[spec3, spec3, _row_spec(d)], scratch_shapes=[pltpu.VMEM((ROWS + SUBLANES, d), F32)],
        compiler_params=_params(("parallel",)), name="lru_gates_fwd")(rest, rest, cw8, vec8, wa, wx)


def _scan_fwd(a3, u3):
    s, _, lw = a3.shape

    def body(a_ref, u_ref, h_ref, hp_ref, carry):
        @pl.when(pl.program_id(0) == 0)
        def _():
            carry[...] = jnp.zeros_like(carry)

        def step(t, h):
            hp_ref[t] = h
            hn = a_ref[t] * h + u_ref[t]
            h_ref[t] = hn
            return hn

        carry[...] = lax.fori_loop(0, ROWS, step, carry[...], unroll=8)

    spec3 = pl.BlockSpec((ROWS, SUBLANES, lw), lambda i: (i, 0, 0))
    return pl.pallas_call(body, out_shape=[jax.ShapeDtypeStruct(a3.shape, F32)] * 2, grid=(s // ROWS,), in_specs=[spec3, spec3],
                          out_specs=[spec3, spec3], scratch_shapes=[pltpu.VMEM((SUBLANES, lw), F32)],
                          compiler_params=_params(("arbitrary",)), name="lru_scan_fwd")(a3, u3)


def _lru_out(h3, rest):
    s, _, lw = h3.shape
    d = lw * SUBLANES

    def body(h_ref, g_ref, y_ref, h2_ref):
        h = _from_3d(h_ref)
        h2_ref[...] = h
        y_ref[...] = (h * _gelu(g_ref[...])).astype(BF16)

    return pl.pallas_call(
        body, out_shape=[jax.ShapeDtypeStruct((s, d), BF16), jax.ShapeDtypeStruct((s, d), F32)], grid=(s // ROWS,),
        in_specs=[pl.BlockSpec((ROWS, SUBLANES, lw), lambda i: (i, 0, 0)), pl.BlockSpec((None, ROWS, d), lambda i: (1, i, 0))],
        out_specs=[_row_spec(d), _row_spec(d)], compiler_params=_params(("parallel",)), name="lru_out")(h3, rest)


def _scan_bwd(a3, hp3, dh):
    s, _, lw = a3.shape
    d = lw * SUBLANES
    nb = s // ROWS

    def body(a_ref, hp_ref, dh_ref, g_ref, da_ref, dh3, carry):
        @pl.when(pl.program_id(0) == 0)
        def _():
            carry[...] = jnp.zeros_like(carry)

        _to_3d(dh3, dh_ref[...])

        def step(j, c):
            t = ROWS - 1 - j
            g = dh3[t] + c
            g_ref[t] = g
            da_ref[t] = g * hp_ref[t]
            return a_ref[t] * g

        carry[...] = lax.fori_loop(0, ROWS, step, carry[...], unroll=8)

    spec3 = pl.BlockSpec((ROWS, SUBLANES, lw), lambda i: (nb - 1 - i, 0, 0))
    return pl.pallas_call(
        body, out_shape=[jax.ShapeDtypeStruct(a3.shape, F32)] * 2, grid=(nb,),
        in_specs=[spec3, spec3, pl.BlockSpec((ROWS, d), lambda i: (nb - 1 - i, 0))], out_specs=[spec3, spec3],
        scratch_shapes=[pltpu.VMEM((ROWS, SUBLANES, lw), F32), pltpu.VMEM((SUBLANES, lw), F32)],
        compiler_params=_params(("arbitrary",)), name="lru_scan_bwd")(a3, hp3, dh)


def _gates_bwd(g3, da3, xc, wa, wx, vec8):
    s, d = xc.shape
    lw = d // SUBLANES
    nh = d // HEAD

    def body(g_ref, da_ref, xc_ref, wa_ref, wx_ref, vec_ref, dxc_ref, dwa_ref, dwx_ref, dvec_ref):
        @pl.when(pl.program_id(0) == 0)
        def _():
            dwa_ref[...] = jnp.zeros_like(dwa_ref)
            dwx_ref[...] = jnp.zeros_like(dwx_ref)
            dvec_ref[...] = jnp.zeros_like(dvec_ref)

        xc_v, vec = xc_ref[...], vec_ref[...]
        du, da = _from_3d(g_ref), _from_3d(da_ref)
        gm = _gate_math(xc_v, wa_ref, wx_ref, vec)
        ra, ig, sp, a, mult = gm["ra"], gm["ig"], gm["sp"], gm["a"], gm["mult"]
        dmult = du * ig * xc_v
        dlog_a = da * a - dmult * (a * a) / mult
        dpre_a = dlog_a * (-LRU_C * sp) * ra * (1.0 - ra)
        dpre_x = du * mult * xc_v * ig * (1.0 - ig)
        dlam = jnp.sum(dlog_a * (-LRU_C * ra), axis=0, keepdims=True) * (-_sigmoid(-vec[VEC_LAM:VEC_LAM + 1]))
        dvec_ref[VEC_BA:VEC_BA + 1, :] += jnp.sum(dpre_a, axis=0, keepdims=True)
        dvec_ref[VEC_BX:VEC_BX + 1, :] += jnp.sum(dpre_x, axis=0, keepdims=True)
        dvec_ref[VEC_LAM:VEC_LAM + 1, :] += dlam
        dab, dxb, xcb = dpre_a.astype(BF16), dpre_x.astype(BF16), gm["xcb"]
        back = []
        for h in range(nh):
            cols = slice(h * HEAD, (h + 1) * HEAD)
            dwa_ref[h] += lax.dot_general(xcb[:, cols], dab[:, cols], TN, preferred_element_type=F32)
            dwx_ref[h] += lax.dot_general(xcb[:, cols], dxb[:, cols], TN, preferred_element_type=F32)
            back.append(lax.dot_general(dab[:, cols], wa_ref[h], NT, preferred_element_type=F32)
                        + lax.dot_general(dxb[:, cols], wx_ref[h], NT, preferred_element_type=F32))
        dxc_ref[...] = du * mult * ig + jnp.concatenate(back, axis=1)

    spec3 = pl.BlockSpec((ROWS, SUBLANES, lw), lambda i: (i, 0, 0))
    wspec = pl.BlockSpec(wa.shape, lambda i: (0, 0, 0))
    return pl.pallas_call(
        body, out_shape=[jax.ShapeDtypeStruct((s, d), F32), jax.ShapeDtypeStruct(wa.shape, F32), jax.ShapeDtypeStruct(wa.shape, F32),
                         jax.ShapeDtypeStruct((SUBLANES, d), F32)],
        grid=(s // ROWS,), in_specs=[spec3, spec3, _row_spec(d), wspec, wspec, _vec_spec(d, SUBLANES)],
        out_specs=[_row_spec(d), wspec, wspec, _vec_spec(d, SUBLANES)],
        compiler_params=_params(("arbitrary",)), name="lru_gates_bwd")(g3, da3, xc, wa, wx, vec8)


def _conv_bwd(dxc, rest, cw8):
    s, d = dxc.shape
    hb = ROWS // SUBLANES
    last = s // SUBLANES - 1

    def body(dc_ref, dnext_ref, x_ref, xprev_ref, cw_ref, dx_ref, dcw_ref, padd, padx):
        i = pl.program_id(0)

        @pl.when(i == 0)
        def _():
            dcw_ref[...] = jnp.zeros_like(dcw_ref)

        dc = dc_ref[...]
        padd[0:ROWS, :] = dc
        padd[ROWS:ROWS + SUBLANES, :] = jnp.where(i == pl.num_programs(0) - 1, 0.0, dnext_ref[...])
        _conv_pad_prev(padx, x_ref[...], xprev_ref[...], i == 0)
        dx = jnp.zeros_like(dc)
        for k in range(CONV_TAPS):
            dx = dx + cw_ref[k:k + 1, :] * padd[pl.ds(CONV_TAPS - 1 - k, ROWS), :]
            dcw_ref[k:k + 1, :] += jnp.sum(dc * padx[pl.ds(SUBLANES - (CONV_TAPS - 1) + k, ROWS), :], axis=0, keepdims=True)
        dcw_ref[CONV_TAPS:CONV_TAPS + 1, :] += jnp.sum(dc, axis=0, keepdims=True)
        dx_ref[...] = dx.astype(BF16)

    return pl.pallas_call(
        body, out_shape=[jax.ShapeDtypeStruct((s, d), BF16), jax.ShapeDtypeStruct((SUBLANES, d), F32)], grid=(s // ROWS,),
        in_specs=[_row_spec(d), pl.BlockSpec((SUBLANES, d), lambda i: (jnp.minimum((i + 1) * hb, last), 0)),
                  pl.BlockSpec((None, ROWS, d), lambda i: (0, i, 0)),
                  pl.BlockSpec((None, SUBLANES, d), lambda i: (0, jnp.maximum(i * hb - 1, 0), 0)), _vec_spec(d, SUBLANES)],
        out_specs=[_row_spec(d), _vec_spec(d, SUBLANES)],
        scratch_shapes=[pltpu.VMEM((ROWS + SUBLANES, d), F32), pltpu.VMEM((ROWS + SUBLANES, d), F32)],
        compiler_params=_params(("arbitrary",)), name="lru_conv_bwd")(dxc, dxc, rest, rest, cw8)


def _coords():
    return lax.axis_index("x"), lax.axis_index("y"), lax.axis_index("c")


def _other_chips(x, y):
    return [(1 - x, y), (x, 1 - y), (1 - x, 1 - y)]


def _slab(ref, kind, shard_shape, idx):
    r, c = shard_shape
    if kind == "col":
        return ref.at[:, pl.ds(pl.multiple_of(idx * c, LANES), c)]
    if kind == "row":
        return ref.at[pl.ds(pl.multiple_of(idx * r, SUBLANES), r), :]
    return ref.at[idx]


def _full_shape(shard_shape, kind):
    r, c = shard_shape
    return {"col": (r, c * N_DEV), "row": (r * N_DEV, c), "slot": (N_DEV, r, c)}[kind]


def _handshake(peers):
    barrier = pltpu.get_barrier_semaphore()
    for peer in peers:
        pl.semaphore_signal(barrier, inc=1, device_id=peer, device_id_type=MESH)
    pl.semaphore_wait(barrier, len(peers))


def _launch(name, body, out_shape, operands, sems, sequencer_id):
    if sequencer_id is None:
        return pl.pallas_call(body, out_shape=out_shape, in_specs=[HBM] * len(operands), out_specs=[HBM] * len(out_shape),
                              scratch_shapes=sems, name=name)(*operands)
    return pl.kernel(body, out_type=out_shape, mesh=plsc.ScalarSubcoreMesh(axis_name="seq", num_cores=1), name=name,
                     scratch_types=sems, compiler_params=pltpu.CompilerParams(collective_id=sequencer_id))(*operands)


def _all_gather(name, shards, kinds, sequencer_id=None):
    n = len(shards)
    shapes = [s.shape for s in shards]

    def body(*refs):
        ins, outs = refs[:n], refs[n:2 * n]
        send_sems, recv_sems, local_sems = refs[2 * n:]
        x, y, c = _coords()
        me, sib = (x, y, c), (x, y, 1 - c)
        chips = _other_chips(x, y)
        if sequencer_id is not None:
            _handshake([sib] + [(*chip, c) for chip in chips])

        def part(i, dev):
            return _slab(outs[i], kinds[i], shapes[i], 4 * dev[0] + 2 * dev[1] + dev[2])

        def copy(i, k, block, to, src=None):
            return pltpu.make_async_remote_copy(
                src_ref=part(i, block) if src is None else src, dst_ref=part(i, block),
                send_sem=send_sems.at[7 * i + k], recv_sem=recv_sems.at[7 * i + k], device_id=to, device_id_type=MESH)

        started = []
        for i in range(n):
            for k, to in enumerate([sib] + [(*chip, c) for chip in chips]):
                cp = copy(i, k, me, to, src=ins[i])
                cp.start()
                started.append(cp)
        mine = [pltpu.make_async_copy(ins[i], part(i, me), local_sems.at[i]) for i in range(n)]
        for cp in mine:
            cp.start()
        for i in range(n):
            for j, chip in enumerate(chips):
                copy(i, 1 + j, (*chip, c), me).wait_recv()
                cp = copy(i, 4 + j, (*chip, c), sib)
                cp.start()
                started.append(cp)
        for i in range(n):
            copy(i, 0, sib, me).wait_recv()
            for j, chip in enumerate(chips):
                copy(i, 4 + j, (*chip, 1 - c), me).wait_recv()
        for cp in started:
            cp.wait_send()
        for cp in mine:
            cp.wait()

    out_shape = [jax.ShapeDtypeStruct(_full_shape(s.shape, k), s.dtype) for s, k in zip(shards, kinds)]
    sems = [pltpu.SemaphoreType.DMA((7 * n,)), pltpu.SemaphoreType.DMA((7 * n,)), pltpu.SemaphoreType.DMA((n,))]
    return _launch(name, body, out_shape, shards, sems, sequencer_id)


def _exchange_siblings(name, partials, kinds, shard_shapes, sequencer_id=None):
    n = len(partials)

    def body(*refs):
        ins, outs = refs[:n], refs[n:2 * n]
        send_sems, recv_sems = refs[2 * n:]
        x, y, c = _coords()
        if sequencer_id is not None:
            _handshake([(x, y, 1 - c)])
        cps = []
        for i in range(n):
            for q in range(N_CHIP):
                cps.append(pltpu.make_async_remote_copy(
                    src_ref=_slab(ins[i], kinds[i], shard_shapes[i], 2 * q + (1 - c)), dst_ref=outs[i].at[q],
                    send_sem=send_sems.at[N_CHIP * i + q], recv_sem=recv_sems.at[N_CHIP * i + q],
                    device_id=(x, y, 1 - c), device_id_type=MESH))
        for cp in cps:
            cp.start()
        for cp in cps:
            cp.wait()

    return _launch(name, body, [jax.ShapeDtypeStruct((N_CHIP, *s), BF16) for s in shard_shapes], partials,
                   [pltpu.SemaphoreType.DMA((N_CHIP * n,)), pltpu.SemaphoreType.DMA((N_CHIP * n,))], sequencer_id)


def _exchange_chips(name, chip_sums, sequencer_id=None):
    n = len(chip_sums)

    def body(*refs):
        ins, outs = refs[:n], refs[n:2 * n]
        send_sems, recv_sems = refs[2 * n:]
        x, y, c = _coords()
        if sequencer_id is not None:
            _handshake([(cx, cy, c) for cx, cy in _other_chips(x, y)])
        cps = []
        for i in range(n):
            for k, (cx, cy) in enumerate(_other_chips(x, y)):
                cps.append(pltpu.make_async_remote_copy(
                    src_ref=ins[i].at[2 * cx + cy], dst_ref=outs[i].at[k], send_sem=send_sems.at[3 * i + k],
                    recv_sem=recv_sems.at[3 * i + k], device_id=(cx, cy, c), device_id_type=MESH))
        for cp in cps:
            cp.start()
        for cp in cps:
            cp.wait()

    return _launch(name, body, [jax.ShapeDtypeStruct((3, *t.shape[1:]), BF16) for t in chip_sums], chip_sums,
                   [pltpu.SemaphoreType.DMA((3 * n,)), pltpu.SemaphoreType.DMA((3 * n,))], sequencer_id)


def _all_reduce_small(packed):
    rows = packed.shape[0] // N_DEV

    def body(p_ref, out_ref, rb, tot, send_sems, recv_sems):
        x, y, c = _coords()
        me = 4 * x + 2 * y + c

        def peer(k):
            return (x ^ (k >> 2), y ^ ((k >> 1) & 1), c ^ (k & 1))

        def rows_of(idx):
            return pl.ds(pl.multiple_of(idx * rows, SUBLANES), rows)

        def piece(ref, idx):
            return ref.at[rows_of(idx), :]

        scatter = [pltpu.make_async_remote_copy(src_ref=piece(p_ref, me ^ k), dst_ref=rb.at[k], send_sem=send_sems.at[k],
                                                recv_sem=recv_sems.at[k], device_id=peer(k), device_id_type=MESH) for k in range(1, N_DEV)]
        for cp in scatter:
            cp.start()
        acc = p_ref[rows_of(me), :]
        for cp in scatter:
            cp.wait_recv()
        for k in range(1, N_DEV):
            acc = acc + rb[k]
        tot[...] = acc
        out_ref[rows_of(me), :] = acc
        gather = [pltpu.make_async_remote_copy(src_ref=tot, dst_ref=piece(out_ref, me), send_sem=send_sems.at[N_DEV + k],
                                               recv_sem=recv_sems.at[N_DEV + k], device_id=peer(k), device_id_type=MESH)
                  for k in range(1, N_DEV)]
        for cp in gather:
            cp.start()
        for k in range(1, N_DEV):
            pltpu.make_async_remote_copy(src_ref=tot, dst_ref=piece(out_ref, me ^ k), send_sem=send_sems.at[N_DEV + k],
                                         recv_sem=recv_sems.at[N_DEV + k], device_id=peer(k), device_id_type=MESH).wait_recv()
        for cp in scatter + gather:
            cp.wait_send()

    vm = pl.BlockSpec(memory_space=pltpu.VMEM)
    return pl.pallas_call(
        body, out_shape=jax.ShapeDtypeStruct(packed.shape, F32), in_specs=[vm], out_specs=vm,
        scratch_shapes=[pltpu.VMEM((N_DEV, rows, LANES), F32), pltpu.VMEM((rows, LANES), F32),
                        pltpu.SemaphoreType.DMA((2 * N_DEV,)), pltpu.SemaphoreType.DMA((2 * N_DEV,))],
        compiler_params=pltpu.CompilerParams(vmem_limit_bytes=VMEM_LIMIT), name="all_reduce_small")(packed)


def _adamw_math(g, w, m, v):
    m = ADAM_B1 * m + (1.0 - ADAM_B1) * g
    v = ADAM_B2 * v + (1.0 - ADAM_B2) * (g * g)
    delta = -ADAM_LR * ((m / ADAM_C1) / (jnp.sqrt(v / ADAM_C2) + ADAM_EPS) + ADAM_WD * w)
    return delta, m, v


def _slab_spec(kind, shard_shape, tr, slab_of):
    r, c = shard_shape
    if kind == "col":
        return pl.BlockSpec((tr, c), lambda q, i, sc: (i, slab_of(q, sc)))
    return pl.BlockSpec((tr, c), lambda q, i, sc: (slab_of(q, sc) * (r // tr) + i, 0))


def _chip_sum(name, partial, recv, kind, shard_shape, core):
    r, c = shard_shape
    tr = _blk(r, 256)

    def body(core_ref, p_ref, r_ref, o_ref):
        o_ref[...] = (p_ref[...].astype(F32) + r_ref[...].astype(F32)).astype(BF16)

    spec4 = pl.BlockSpec((None, tr, c), lambda q, i, sc: (q, i, 0))
    grid_spec = pltpu.PrefetchScalarGridSpec(
        num_scalar_prefetch=1, grid=(N_CHIP, r // tr),
        in_specs=[_slab_spec(kind, shard_shape, tr, lambda q, sc: 2 * q + sc[0]), spec4], out_specs=spec4)
    return pl.pallas_call(body, out_shape=jax.ShapeDtypeStruct((N_CHIP, r, c), BF16), grid_spec=grid_spec,
                          compiler_params=_params(("parallel", "parallel")), name=name)(core, partial, recv)


def _adamw_shard(name, parts, w, m, v, chip):
    r, c = w.shape
    n_parts = len(parts)
    tr = _blk(r // n_parts, 128)
    per = r // n_parts // tr

    def body(chip_ref, *refs):
        src, (w_ref, m_ref, v_ref), (g_out, d_out, m_out, v_out) = refs[:2 * n_parts], refs[2 * n_parts:2 * n_parts + 3], refs[2 * n_parts + 3:]
        for p in range(n_parts):
            @pl.when(pl.program_id(0) // per == p)
            def _():
                g = src[2 * p][...].astype(F32)
                for k in range(3):
                    g = g + src[2 * p + 1][k].astype(F32)
                g_out[...] = g
                d_out[...], m_out[...], v_out[...] = _adamw_math(g, w_ref[...], m_ref[...], v_ref[...])

    def part_specs(p):
        at = lambda i: jnp.clip(i - p * per, 0, per - 1)
        return [pl.BlockSpec((None, tr, c), lambda i, sc: (sc[0], at(i), 0)), pl.BlockSpec((3, tr, c), lambda i, sc: (0, at(i), 0))]

    blk = pl.BlockSpec((tr, c), lambda i, sc: (i, 0))
    grid_spec = pltpu.PrefetchScalarGridSpec(
        num_scalar_prefetch=1, grid=(r // tr,), in_specs=[s for p in range(n_parts) for s in part_specs(p)] + [blk, blk, blk], out_specs=[blk] * 4)
    return pl.pallas_call(body, out_shape=[jax.ShapeDtypeStruct((r, c), F32)] * 4, grid_spec=grid_spec,
                          compiler_params=_params(("parallel",)), name=name)(chip, *[a for p in parts for a in p], w, m, v)


def _adamw_small(name, g, w, m, v):
    def body(g_ref, w_ref, m_ref, v_ref, d_out, m_out, v_out):
        d_out[...], m_out[...], v_out[...] = _adamw_math(g_ref[...], w_ref[...], m_ref[...], v_ref[...])

    vm = pl.BlockSpec(memory_space=pltpu.VMEM)
    return pl.pallas_call(body, out_shape=[jax.ShapeDtypeStruct(g.shape, F32)] * 3, in_specs=[vm] * 4, out_specs=[vm] * 3,
                          compiler_params=pltpu.CompilerParams(vmem_limit_bytes=VMEM_LIMIT), name=name)(g, w, m, v)


def _pack_rows(arrays, total_rows):
    flat = [a.reshape(-1, LANES) for a in arrays]
    used = sum(f.shape[0] for f in flat)
    return jnp.concatenate(flat + [jnp.zeros((total_rows - used, LANES), F32)], axis=0)


def _unpack_rows(packed, like):
    out, at = [], 0
    for a in like:
        n = a.size // LANES
        out.append(packed[at:at + n].reshape(a.shape))
        at += n
    return out


def kernel(x, norm_mix_g, w_in, conv_w, conv_b, lru_wa, lru_ba, lru_wx, lru_bx, lru_lambda, w_proj_attn, w_proj_lru, w_out, norm_mlp_g, w_up, w_down, norm_final_g, loss_target, m_norm_mix_g, m_w_in, m_conv_w, m_conv_b, m_lru_wa, m_lru_ba, m_lru_wx, m_lru_bx, m_lru_lambda, m_w_proj_attn, m_w_proj_lru, m_w_out, m_norm_mlp_g, m_w_up, m_w_down, m_norm_final_g, v_norm_mix_g, v_w_in, v_conv_w, v_conv_b, v_lru_wa, v_lru_ba, v_lru_wx, v_lru_bx, v_lru_lambda, v_w_proj_attn, v_w_proj_lru, v_w_out, v_norm_mlp_g, v_w_up, v_w_down, v_norm_final_g):
    xs, tgt = x[0], loss_target[0]
    s, d = xs.shape
    nh = d // HEAD
    ix, iy, ic = _coords()
    core = jnp.reshape(ic, (1,)).astype(jnp.int32)
    chip = jnp.reshape(2 * ix + iy, (1,)).astype(jnp.int32)
    dev = 4 * ix + 2 * iy + ic

    big = [w_in[0], w_proj_attn[0], w_proj_lru[0], w_out[0], w_up[0], w_down[0]]
    big_m = [m_w_in[0], m_w_proj_attn[0], m_w_proj_lru[0], m_w_out[0], m_w_up[0], m_w_down[0]]
    big_v = [v_w_in[0], v_w_proj_attn[0], v_w_proj_lru[0], v_w_out[0], v_w_up[0], v_w_down[0]]
    kinds = ["col", "row", "row", "row", "col", "row"]
    pad_taps = lambda t: jnp.pad(t, ((0, SUBLANES - CONV_TAPS), (0, 0)))
    shards = [w.astype(BF16) for w in big]
    win, cw_slots = _all_gather("all_gather_w_in", [shards[0], pad_taps(conv_w[0])], ["col", "slot"])
    later = lax.optimization_barrier((shards[1:], win))[0]
    wpa, wpl, wout = _all_gather("all_gather_mix", later[:3], kinds[1:4], sequencer_id=1)
    wup, wdown = _all_gather("all_gather_mlp", later[3:], kinds[4:], sequencer_id=5)
    cw8 = jnp.transpose(cw_slots, (1, 0, 2)).reshape(SUBLANES, d)
    row_id = lax.broadcasted_iota(jnp.int32, (SUBLANES, d), 0)
    vec8 = sum(jnp.where(row_id == k, t, 0.0) for k, t in ((VEC_CB, conv_b), (VEC_BA, lru_ba), (VEC_BX, lru_bx), (VEC_LAM, lru_lambda)))
    wa16, wx16 = lru_wa[0].astype(BF16), lru_wx[0].astype(BF16)
    slopes = 2.0 ** (-8.0 * jnp.arange(1, nh + 1, dtype=F32) / nh)

    def seg_specs(*segs):
        return lambda bm, bn: [pl.BlockSpec((None, bm, bn), (lambda i, j, kk, sg=sg: (sg, i, j))) for sg in segs]

    def plain_specs(k):
        return lambda bm, bn: [pl.BlockSpec((bm, bn), lambda i, j, kk: (i, j)) for _ in range(k)]

    xn = _rms_fwd("norm_mix", xs, norm_mix_g)
    qkv = _mm_fwd("proj_qkv", xn, win, 0, 3 * d, [F32], seg_out=d)[0]
    rest = _mm_fwd("proj_rest", xn, win, 3 * d, 4 * d, [F32], seg_out=d)[0]
    att, lse = _attn_fwd(qkv, slopes)
    a3, u3, xc = _gates_fwd(rest, cw8, vec8, wa16, wx16)
    h3, hp3 = _scan_fwd(a3, u3)
    ylru, h2d = _lru_out(h3, rest)
    pa = _mm_fwd("proj_attn", att, wpa, 0, d, [F32])[0]

    def merge(acc, pa_b, ga, gl):
        return acc, _sigmoid(ga) * pa_b + _sigmoid(gl) * acc

    plr, merged = _mm_fwd("proj_lru_merge", ylru, wpl, 0, d, [F32, BF16], merge, (pa, rest, rest),
                          lambda bm, bn: plain_specs(1)(bm, bn) + seg_specs(2, 3)(bm, bn), bm=512)
    h1 = _mm_fwd("mix_out", merged, wout, 0, d, [F32], lambda acc, r: (acc + r,), (xs,), plain_specs(1))[0]
    hn = _rms_fwd("norm_mlp", h1, norm_mlp_g)

    def relu2(acc):
        return acc, jnp.square(jnp.maximum(acc, 0.0))

    up, hid = _mm_fwd("mlp_up", hn, wup, 0, wup.shape[1], [BF16, BF16], relu2)
    h2 = _mm_fwd("mlp_down", hid, wdown, 0, d, [F32], lambda acc, r: (acc + r,), (h1,), plain_specs(1))[0]
    dh2, dh2b, dg3, loss_lanes = _final_loss(h2, tgt, norm_final_g.reshape(1, d))
    loss = lax.psum(0.5 / d * jnp.sum(loss_lanes), ("x", "y", "c"))

    def reduce_group(tag, kk, shp, partials, sequencer_id):
        from_sibling = _exchange_siblings(f"rs_siblings_{tag}", partials, kk, shp)
        sums = [_chip_sum(f"chip_sum_{tag}_{i}", p, f, k, sh, core) for i, (p, f, k, sh) in enumerate(zip(partials, from_sibling, kk, shp))]
        return list(zip(sums, _exchange_chips(f"rs_chips_{tag}", sums, sequencer_id)))

    dup = _mm_nt("mlp_down_dx", dh2b, wdown, [BF16], lambda acc, u: (acc * (2.0 * jnp.maximum(u.astype(F32), 0.0)),), (up,), plain_specs(1))[0]
    g_wdown = _mm_tn("mlp_down_dw", hid, dh2b)
    dhn = _mm_nt("mlp_up_dx", dup, wup, [F32])[0]
    g_wup = _mm_tn("mlp_up_dw", hn, dup)
    red_up, red_down = reduce_group("mlp", kinds[4:], [w.shape for w in big[4:]], [g_wup, g_wdown], 2)
    dhn = lax.optimization_barrier((dhn, red_up[0], red_down[0]))[0]
    dh1, dh1b, dg2 = _rms_bwd("norm_mlp_bwd", h1, norm_mlp_g, dhn, dh2)

    def merge_bwd(acc, pa_b, pl_b, ga, gl):
        sa, sl = _sigmoid(ga), _sigmoid(gl)
        return acc * sa, acc * sl, acc * pa_b * sa * (1.0 - sa), acc * pl_b * sl * (1.0 - sl)

    dpa, dpl, dga, dgl = _mm_nt("mix_out_dx", dh1b, wout, [BF16] * 4, merge_bwd, (pa, plr, rest, rest),
                                lambda bm, bn: plain_specs(2)(bm, bn) + seg_specs(2, 3)(bm, bn), bm=512)
    g_wout = _mm_tn("mix_out_dw", merged, dh1b)
    datt = _mm_nt("proj_attn_dx", dpa, wpa, [F32])[0]
    g_wpa = _mm_tn("proj_attn_dw", att, dpa)

    def lru_out_bwd(acc, h_b, gate):
        return acc * _gelu(gate), acc * h_b * _gelu_grad(gate)

    dh, dxg = _mm_nt("proj_lru_dx", dpl, wpl, [F32, BF16], lru_out_bwd, (h2d, rest),
                     lambda bm, bn: plain_specs(1)(bm, bn) + seg_specs(1)(bm, bn), bm=512)
    g_wpl = _mm_tn("proj_lru_dw", ylru, dpl)
    red_pa, red_pl, red_out = reduce_group("mix", kinds[1:4], [w.shape for w in big[1:4]], [g_wpa, g_wpl, g_wout], 3)
    dq, dk, dv = _attn_bwd(qkv, datt, att, lse, slopes)
    g3, da3 = _scan_bwd(a3, hp3, dh)
    dxc, dwa, dwx, dvec = _gates_bwd(g3, da3, xc, wa16, wx16, vec8)
    dxr, dconv = _conv_bwd(dxc, rest, cw8)
    dproj = jnp.concatenate([dq, dk, dv, dxr, dxg, dga, dgl], axis=1)
    big_out = {i: _adamw_shard(f"adamw_{i}", [red], big[i], big_m[i], big_v[i], chip)
               for i, red in ((1, red_pa), (2, red_pl), (3, red_out), (4, red_up), (5, red_down))}
    dproj = lax.optimization_barrier((dproj, [big_out[i] for i in range(1, 6)]))[0]
    half = (big[0].shape[0] // 2, big[0].shape[1])
    red_in = []
    for p in range(2):
        g_half = _mm_tn(f"proj_in_dw_{p}", xn, dproj, part=(p, 2))
        red_in += reduce_group(f"in_{p}", ["col"], [half], [g_half], 4 + 2 * p)
        dproj = lax.optimization_barrier((dproj, red_in[-1][0]))[0]
    dxn = _mm_nt("proj_in_dx", dproj, win, [F32])[0]
    grad_x, _, dg1 = _rms_bwd("norm_mix_bwd", xs, norm_mix_g, dxn, dh1)
    big_out[0] = _adamw_shard("adamw_0", red_in, big[0], big_m[0], big_v[0], chip)

    small_w = [norm_mix_g, conv_b, lru_wa, lru_ba, lru_wx, lru_bx, lru_lambda, norm_mlp_g, norm_final_g]
    small_m = [m_norm_mix_g, m_conv_b, m_lru_wa, m_lru_ba, m_lru_wx, m_lru_bx, m_lru_lambda, m_norm_mlp_g, m_norm_final_g]
    small_v = [v_norm_mix_g, v_conv_b, v_lru_wa, v_lru_ba, v_lru_wx, v_lru_bx, v_lru_lambda, v_norm_mlp_g, v_norm_final_g]
    small_g = [dg1, dconv[CONV_TAPS:CONV_TAPS + 1], dwa, dvec[VEC_BA:VEC_BA + 1], dwx, dvec[VEC_BX:VEC_BX + 1],
               dvec[VEC_LAM:VEC_LAM + 1], dg2, dg3, dconv[0:CONV_TAPS]]
    n_rows = sum(g.size for g in small_g) // LANES
    per_dev = -(-n_rows // (N_DEV * SUBLANES)) * SUBLANES
    total = _all_reduce_small(_pack_rows(small_g, N_DEV * per_dev))
    sw_rows = -(-(n_rows - CONV_TAPS * d // LANES) // SUBLANES) * SUBLANES
    s_delta, s_m, s_v = _adamw_small("adamw_small", total[:sw_rows], _pack_rows(small_w, sw_rows), _pack_rows(small_m, sw_rows),
                                     _pack_rows(small_v, sw_rows))
    sg = _unpack_rows(total, [w for w in small_w] + [jnp.zeros((1, CONV_TAPS, d), F32)])
    s_grad, g_cw_full = sg[:-1], sg[-1]
    s_delta, s_m, s_v = (_unpack_rows(t, small_w) for t in (s_delta, s_m, s_v))
    cshard = conv_w.shape[2]
    g_cw = lax.dynamic_slice(g_cw_full, (0, 0, dev * cshard), (1, CONV_TAPS, cshard))
    cw_delta, cw_m, cw_v = (t[:CONV_TAPS][None] for t in _adamw_small(
        "adamw_conv_w", pad_taps(g_cw[0]), pad_taps(conv_w[0]), pad_taps(m_conv_w[0]), pad_taps(v_conv_w[0])))

    names = ["norm_mix_g", "w_in", "conv_w", "conv_b", "lru_wa", "lru_ba", "lru_wx", "lru_bx", "lru_lambda", "w_proj_attn", "w_proj_lru",
             "w_out", "norm_mlp_g", "w_up", "w_down", "norm_final_g"]
    small_names = ["norm_mix_g", "conv_b", "lru_wa", "lru_ba", "lru_wx", "lru_bx", "lru_lambda", "norm_mlp_g", "norm_final_g"]
    big_names = ["w_in", "w_proj_attn", "w_proj_lru", "w_out", "w_up", "w_down"]
    res = {"conv_w": (g_cw, cw_delta, cw_m, cw_v)}
    for i, nm in enumerate(small_names):
        res[nm] = (s_grad[i], s_delta[i], s_m[i], s_v[i])
    for i, nm in enumerate(big_names):
        res[nm] = tuple(t[None] for t in big_out[i])
    return (loss, grad_x[None], *[res[nm][0] for nm in names], *[res[nm][1] for nm in names],
            *[res[nm][2] for nm in names], *[res[nm][3] for nm in names])
```

```python
import jax
import jax.numpy as jnp
from jax import lax
from jax.experimental import pallas as pl
from jax.experimental.pallas import tpu as pltpu
from jax.experimental.pallas import tpu_sc as plsc

F32, BF16 = jnp.float32, jnp.bfloat16
MESH = pl.DeviceIdType.MESH
HBM = pl.BlockSpec(memory_space=pltpu.HBM)
N_DEV = 8
N_CHIP = 4
HEAD = 128
SPAN = 128
DILATIONS = (1, 4, 16)
CONV_TAPS = 4
LRU_C = 8.0
NORM_EPS = 1e-6
LANES = 128
SUBLANES = 8
VMEM_LIMIT = 56 * 1024 * 1024
ADAM_LR, ADAM_B1, ADAM_B2, ADAM_EPS, ADAM_WD, ADAM_STEP = 0.001, 0.9, 0.999, 1e-08, 0.01, 10
ADAM_C1 = 1.0 - ADAM_B1 ** ADAM_STEP
ADAM_C2 = 1.0 - ADAM_B2 ** ADAM_STEP
NEG = -1e30


def _params(sem=None):
    return pltpu.CompilerParams(dimension_semantics=sem, vmem_limit_bytes=VMEM_LIMIT)


def _sigmoid(v):
    return 1.0 / (1.0 + jnp.exp(-v))


def _gelu(v):
    k = 0.7978845608028654
    return 0.5 * v * (1.0 + jnp.tanh(k * (v + 0.044715 * v * v * v)))


def _gelu_grad(v):
    k = 0.7978845608028654
    t = jnp.tanh(k * (v + 0.044715 * v * v * v))
    return 0.5 * (1.0 + t) + 0.5 * v * (1.0 - t * t) * k * (1.0 + 3.0 * 0.044715 * v * v)


NN = (((1,), (0,)), ((), ()))
NT = (((1,), (1,)), ((), ()))
TN = (((0,), (0,)), ((), ()))


def _mm(name, a, a_spec, b, b_spec, dn, grid, out_shapes, out_specs, acc_block, epilogue=None, extras=(), extra_specs=()):
    nk, ne, no = grid[2], len(extras), len(out_shapes)

    def body(*refs):
        a_ref, b_ref = refs[0], refs[1]
        ex, outs = refs[2:2 + ne], refs[2 + ne:2 + ne + no]
        part = lax.dot_general(a_ref[...], b_ref[...], dn, preferred_element_type=F32)

        def finish(acc):
            vals = epilogue(acc, *[e[...] for e in ex]) if epilogue is not None else (acc,)
            for o, v in zip(outs, vals):
                o[...] = v.astype(o.dtype)

        if nk == 1:
            finish(part)
        else:
            acc_ref, k = refs[-1], pl.program_id(2)

            @pl.when(k == 0)
            def _():
                acc_ref[...] = part

            @pl.when(k > 0)
            def _():
                acc_ref[...] += part

            @pl.when(k == nk - 1)
            def _():
                finish(acc_ref[...])

    return pl.pallas_call(
        body, out_shape=out_shapes, grid=grid, in_specs=[a_spec, b_spec, *extra_specs], out_specs=out_specs,
        scratch_shapes=[pltpu.VMEM(acc_block, F32)] if nk > 1 else [],
        compiler_params=_params(("parallel", "parallel", "arbitrary")), name=name)(a, b, *extras)


def _blk(n, pref):
    return pref if n % pref == 0 else n


def _kblk(k):
    return k if k <= 2048 else next(b for b in (2048, 1024, 512) if k % b == 0)


def _mm_fwd(name, a, w, col0, ncols, out_dtypes, epilogue=None, extras=(), extra_specs_fn=None, seg_out=None, bm=1024, bn=1024):
    m, k = a.shape
    bm, bn = _blk(m, bm), _blk(ncols, bn)
    bk = _kblk(k)
    nk = k // bk
    cb0 = col0 // bn
    grid = (m // bm, ncols // bn, nk)
    a_spec = pl.BlockSpec((bm, bk), lambda i, j, kk: (i, kk))
    b_spec = pl.BlockSpec((bk, bn), lambda i, j, kk: (kk, cb0 + j))
    if seg_out is None:
        shapes = [jax.ShapeDtypeStruct((m, ncols), dt) for dt in out_dtypes]
        specs = [pl.BlockSpec((bm, bn), lambda i, j, kk: (i, j)) for _ in out_dtypes]
    else:
        per = seg_out // bn
        shapes = [jax.ShapeDtypeStruct((ncols // seg_out, m, seg_out), dt) for dt in out_dtypes]
        specs = [pl.BlockSpec((None, bm, bn), lambda i, j, kk: (j // per, i, j % per)) for _ in out_dtypes]
    ex_specs = extra_specs_fn(bm, bn) if extra_specs_fn else ()
    return _mm(name, a, a_spec, w, b_spec, NN, grid, shapes, specs, (bm, bn), epilogue, extras, ex_specs)


def _mm_nt(name, a, w, out_dtypes, epilogue=None, extras=(), extra_specs_fn=None, bm=1024, bn=1024):
    m, k = a.shape
    n = w.shape[0]
    bm, bn = _blk(m, bm), _blk(n, bn)
    bk = _kblk(k)
    grid = (m // bm, n // bn, k // bk)
    a_spec = pl.BlockSpec((bm, bk), lambda i, j, kk: (i, kk))
    b_spec = pl.BlockSpec((bn, bk), lambda i, j, kk: (j, kk))
    shapes = [jax.ShapeDtypeStruct((m, n), dt) for dt in out_dtypes]
    specs = [pl.BlockSpec((bm, bn), lambda i, j, kk: (i, j)) for _ in out_dtypes]
    ex_specs = extra_specs_fn(bm, bn) if extra_specs_fn else ()
    return _mm(name, a, a_spec, w, b_spec, NT, grid, shapes, specs, (bm, bn), epilogue, extras, ex_specs)


def _mm_tn(name, a, b, part=(0, 1), bm=1024, bn=1024):
    t, m = a.shape
    n = b.shape[1]
    m = m // part[1]
    bm, bn = _blk(m, bm), _blk(n, bn)
    grid = (m // bm, n // bn, 1)
    i0 = part[0] * (m // bm)
    a_spec = pl.BlockSpec((t, bm), lambda i, j, kk: (0, i0 + i))
    b_spec = pl.BlockSpec((t, bn), lambda i, j, kk: (0, j))
    return _mm(name, a, a_spec, b, b_spec, TN, grid, [jax.ShapeDtypeStruct((m, n), BF16)],
               [pl.BlockSpec((bm, bn), lambda i, j, kk: (i, j))], (bm, bn))[0]


ROWS = 256


def _row_spec(d):
    return pl.BlockSpec((ROWS, d), lambda i: (i, 0))


def _vec_spec(d, rows=1):
    return pl.BlockSpec((rows, d), lambda i: (0, 0))


def _rms_fwd(name, x, g):
    s, d = x.shape

    def body(x_ref, g_ref, o_ref):
        xv = x_ref[...]
        r = lax.rsqrt(jnp.mean(xv * xv, axis=-1, keepdims=True) + NORM_EPS)
        o_ref[...] = (xv * r * g_ref[...]).astype(BF16)

    return pl.pallas_call(body, out_shape=jax.ShapeDtypeStruct((s, d), BF16), grid=(s // ROWS,),
                          in_specs=[_row_spec(d), _vec_spec(d)], out_specs=_row_spec(d),
                          compiler_params=_params(("parallel",)), name=name)(x, g)


def _rms_bwd_math(xv, g, dy):
    r = lax.rsqrt(jnp.mean(xv * xv, axis=-1, keepdims=True) + NORM_EPS)
    n = xv * r
    z = dy * g
    dx = r * (z - n * jnp.mean(z * n, axis=-1, keepdims=True))
    return dx, jnp.sum(dy * n, axis=0, keepdims=True)


def _rms_bwd(name, x, g, dy, resid):
    s, d = x.shape

    def body(x_ref, g_ref, dy_ref, r_ref, dx_ref, dxb_ref, dg_ref):
        dx, dg = _rms_bwd_math(x_ref[...], g_ref[...], dy_ref[...])
        dx = dx + r_ref[...]
        dx_ref[...] = dx
        dxb_ref[...] = dx.astype(BF16)

        @pl.when(pl.program_id(0) == 0)
        def _():
            dg_ref[...] = jnp.zeros_like(dg_ref)

        dg_ref[...] += dg

    return pl.pallas_call(
        body, out_shape=[jax.ShapeDtypeStruct((s, d), F32), jax.ShapeDtypeStruct((s, d), BF16), jax.ShapeDtypeStruct((1, d), F32)],
        grid=(s // ROWS,), in_specs=[_row_spec(d), _vec_spec(d), _row_spec(d), _row_spec(d)],
        out_specs=[_row_spec(d), _row_spec(d), _vec_spec(d)], compiler_params=_params(("arbitrary",)), name=name)(x, g, dy, resid)


def _final_loss(h2, tgt, g):
    s, d = h2.shape

    def body(x_ref, t_ref, g_ref, dx_ref, dxb_ref, dg_ref, ls_ref):
        xv, gv = x_ref[...], g_ref[...]
        r = lax.rsqrt(jnp.mean(xv * xv, axis=-1, keepdims=True) + NORM_EPS)
        diff = xv * r * gv - t_ref[...]
        dx, dg = _rms_bwd_math(xv, gv, diff * (1.0 / d))
        dx_ref[...] = dx
        dxb_ref[...] = dx.astype(BF16)

        @pl.when(pl.program_id(0) == 0)
        def _():
            dg_ref[...] = jnp.zeros_like(dg_ref)
            ls_ref[...] = jnp.zeros_like(ls_ref)

        dg_ref[...] += dg
        ls_ref[...] += jnp.sum(diff * diff, axis=0, keepdims=True)

    return pl.pallas_call(
        body, out_shape=[jax.ShapeDtypeStruct((s, d), F32), jax.ShapeDtypeStruct((s, d), BF16),
                         jax.ShapeDtypeStruct((1, d), F32), jax.ShapeDtypeStruct((1, d), F32)],
        grid=(s // ROWS,), in_specs=[_row_spec(d), _row_spec(d), _vec_spec(d)],
        out_specs=[_row_spec(d), _row_spec(d), _vec_spec(d), _vec_spec(d)],
        compiler_params=_params(("arbitrary",)), name="final_norm_loss")(h2, tgt, g)


ATTN_Q = 128


ATTN_BATCH = 8


def _attn_units(s):
    units = []
    for gi, d in enumerate(DILATIONS):
        for r in range(d):
            for q0 in range(0, s // d, ATTN_Q):
                k0 = max(q0 - SPAN, 0)
                units.append((gi, d, r, q0, k0, q0 + ATTN_Q - k0))
    return units


def _stream_rows(d, r, start, size):
    return pl.ds(r + start * d, size) if d == 1 else pl.ds(r + start * d, size, stride=d)


def _attn_scores(q_ref, k_ref, slope, d, r, q0, k0, nk):
    qrows, krows = _stream_rows(d, r, q0, ATTN_Q), _stream_rows(d, r, k0, nk)
    qb = q_ref[qrows, :].astype(BF16)
    kb = k_ref[krows, :].astype(BF16)
    sc = lax.dot_general(qb, kb, NT, preferred_element_type=F32) * (HEAD ** -0.5)
    qi = lax.broadcasted_iota(jnp.int32, (ATTN_Q, nk), 0)
    kj = lax.broadcasted_iota(jnp.int32, (ATTN_Q, nk), 1)
    dist = (q0 - k0) + qi - kj
    valid = (dist >= 0) & (dist <= SPAN)
    sc = sc - (slope * d) * dist.astype(F32)
    return jnp.where(valid, sc, NEG), valid, qb, kb, qrows, krows


def _attn_fwd(qkv, slopes):
    _, s, dm = qkv.shape
    units = _attn_units(s)

    def body(sl_ref, q_ref, k_ref, v_ref, att_ref, lse_ref, *scr):
        o_scr, l_scr = scr[:3], scr[3:]
        slope = sl_ref[pl.program_id(0)]
        for first in range(0, len(units), ATTN_BATCH):
            batch = units[first:first + ATTN_BATCH]
            scored = [_attn_scores(q_ref, k_ref, slope, d, r, q0, k0, nk) for _, d, r, q0, k0, nk in batch]
            soft = []
            for sc, _, _, _, _, _ in scored:
                m = jnp.max(sc, axis=-1, keepdims=True)
                p = jnp.exp(sc - m)
                soft.append((m, p, jnp.sum(p, axis=-1, keepdims=True)))
            outs = [lax.dot_general(p.astype(BF16), v_ref[sco[5], :].astype(BF16), NN, preferred_element_type=F32)
                    for (m, p, l), sco in zip(soft, scored)]
            for (gi, *_), (m, p, l), sco, o in zip(batch, soft, scored, outs):
                o_scr[gi][sco[4], :] = o / l
                l_scr[gi][sco[4], :] = jnp.broadcast_to(m + jnp.log(l), (ATTN_Q, HEAD))
        l0, l1, l2 = l_scr[0][...], l_scr[1][...], l_scr[2][...]
        m = jnp.maximum(jnp.maximum(l0, l1), l2)
        w0, w1, w2 = jnp.exp(l0 - m), jnp.exp(l1 - m), jnp.exp(l2 - m)
        tot = w0 + w1 + w2
        att_ref[...] = ((w0 * o_scr[0][...] + w1 * o_scr[1][...] + w2 * o_scr[2][...]) / tot).astype(BF16)
        lse_ref[...] = m + jnp.log(tot)

    def seg(i):
        return pl.BlockSpec((None, s, HEAD), lambda h: (i, 0, h))

    col = pl.BlockSpec((s, HEAD), lambda h: (0, h))
    return pl.pallas_call(
        body, out_shape=[jax.ShapeDtypeStruct((s, dm), BF16), jax.ShapeDtypeStruct((s, dm), F32)], grid=(dm // HEAD,),
        in_specs=[pl.BlockSpec(memory_space=pltpu.SMEM), seg(0), seg(1), seg(2)], out_specs=[col, col],
        scratch_shapes=[pltpu.VMEM((s, HEAD), F32)] * (2 * len(DILATIONS)),
        compiler_params=_params(("parallel",)), name="attn_fwd")(slopes, qkv, qkv, qkv)


def _attn_bwd(qkv, datt, att, lse, slopes):
    _, s, dm = qkv.shape
    units = _attn_units(s)

    def body(sl_ref, q_ref, k_ref, v_ref, do_ref, att_ref, lse_ref, dq_ref, dk_ref, dv_ref, dq_scr, dk_scr, dv_scr, dl_scr):
        slope = sl_ref[pl.program_id(0)]
        delta = jnp.sum(do_ref[...] * att_ref[...].astype(F32), axis=-1, keepdims=True)
        dl_scr[...] = jnp.broadcast_to(delta, (s, HEAD))
        dq_scr[...] = jnp.zeros_like(dq_scr)
        dk_scr[...] = jnp.zeros_like(dk_scr)
        dv_scr[...] = jnp.zeros_like(dv_scr)
        for first in range(0, len(units), ATTN_BATCH):
            scored = [_attn_scores(q_ref, k_ref, slope, d, r, q0, k0, nk) for _, d, r, q0, k0, nk in units[first:first + ATTN_BATCH]]
            dobs = [do_ref[sco[4], :].astype(BF16) for sco in scored]
            dps = [lax.dot_general(dob, v_ref[sco[5], :].astype(BF16), NT, preferred_element_type=F32) for dob, sco in zip(dobs, scored)]
            ps = [jnp.where(sco[1], jnp.exp(sco[0] - lse_ref[sco[4], :][:, 0:1]), 0.0) for sco in scored]
            dss = [(p * (dp - dl_scr[sco[4], :][:, 0:1]) * (HEAD ** -0.5)).astype(BF16) for p, dp, sco in zip(ps, dps, scored)]
            dqs = [lax.dot_general(ds, sco[3], NN, preferred_element_type=F32) for ds, sco in zip(dss, scored)]
            dks = [lax.dot_general(ds, sco[2], TN, preferred_element_type=F32) for ds, sco in zip(dss, scored)]
            dvs = [lax.dot_general(p.astype(BF16), dob, TN, preferred_element_type=F32) for p, dob in zip(ps, dobs)]
            for sco, dq, dk, dv in zip(scored, dqs, dks, dvs):
                dq_scr[sco[4], :] += dq
                dk_scr[sco[5], :] += dk
                dv_scr[sco[5], :] += dv
        dq_ref[...] = dq_scr[...].astype(BF16)
        dk_ref[...] = dk_scr[...].astype(BF16)
        dv_ref[...] = dv_scr[...].astype(BF16)

    def seg(i):
        return pl.BlockSpec((None, s, HEAD), lambda h: (i, 0, h))

    col = pl.BlockSpec((s, HEAD), lambda h: (0, h))
    return pl.pallas_call(
        body, out_shape=[jax.ShapeDtypeStruct((s, dm), BF16)] * 3, grid=(dm // HEAD,),
        in_specs=[pl.BlockSpec(memory_space=pltpu.SMEM), seg(0), seg(1), seg(2), col, col, col], out_specs=[col, col, col],
        scratch_shapes=[pltpu.VMEM((s, HEAD), F32)] * 4,
        compiler_params=_params(("parallel",)), name="attn_bwd")(slopes, qkv, qkv, qkv, datt, att, lse)


VEC_CB, VEC_BA, VEC_BX, VEC_LAM = 0, 1, 2, 3


def _to_3d(ref3, val):
    lw = val.shape[1] // SUBLANES
    for j in range(SUBLANES):
        ref3[:, j, :] = val[:, j * lw:(j + 1) * lw]


def _from_3d(ref3):
    return jnp.concatenate([ref3[:, j, :] for j in range(SUBLANES)], axis=1)


def _softplus(z):
    return jnp.maximum(z, 0.0) + jnp.log1p(jnp.exp(-jnp.abs(z)))


def _gate_math(xc, wa_ref, wx_ref, vec):
    xcb = xc.astype(BF16)
    nh = xc.shape[1] // HEAD
    pre_a = jnp.concatenate([jnp.dot(xcb[:, h * HEAD:(h + 1) * HEAD], wa_ref[h], preferred_element_type=F32) for h in range(nh)], axis=1)
    pre_x = jnp.concatenate([jnp.dot(xcb[:, h * HEAD:(h + 1) * HEAD], wx_ref[h], preferred_element_type=F32) for h in range(nh)], axis=1)
    ra = _sigmoid(pre_a + vec[VEC_BA:VEC_BA + 1])
    ig = _sigmoid(pre_x + vec[VEC_BX:VEC_BX + 1])
    sp = _softplus(-vec[VEC_LAM:VEC_LAM + 1])
    log_a = -LRU_C * ra * sp
    a = jnp.exp(log_a)
    z = 2.0 * log_a
    one_minus_a2 = jnp.where(z > -0.01, -z * (1.0 + z * (0.5 + z * (1.0 / 6.0))), 1.0 - jnp.exp(z))
    mult = jnp.sqrt(one_minus_a2)
    return dict(xcb=xcb, ra=ra, ig=ig, sp=sp, a=a, mult=mult)


def _conv_pad_prev(pad_ref, cur, halo, first):
    pad_ref[0:SUBLANES, :] = jnp.where(first, 0.0, halo)
    pad_ref[SUBLANES:SUBLANES + cur.shape[0], :] = cur


def _gates_fwd(rest, cw8, vec8, wa, wx):
    _, s, d = rest.shape
    lw = d // SUBLANES
    hb = ROWS // SUBLANES

    def body(x_ref, halo_ref, cw_ref, vec_ref, wa_ref, wx_ref, a_ref, u_ref, xc_ref, pad):
        _conv_pad_prev(pad, x_ref[...], halo_ref[...], pl.program_id(0) == 0)
        vec = vec_ref[...]
        xc = vec[VEC_CB:VEC_CB + 1]
        for k in range(CONV_TAPS):
            xc = xc + cw_ref[k:k + 1, :] * pad[pl.ds(SUBLANES - (CONV_TAPS - 1) + k, ROWS), :]
        gm = _gate_math(xc, wa_ref, wx_ref, vec)
        xc_ref[...] = xc
        _to_3d(a_ref, gm["a"])
        _to_3d(u_ref, gm["mult"] * (gm["ig"] * xc))

    spec3 = pl.BlockSpec((ROWS, SUBLANES, lw), lambda i: (i, 0, 0))
    wspec = pl.BlockSpec(wa.shape, lambda i: (0, 0, 0))
    return pl.pallas_call(
        body, out_shape=[jax.ShapeDtypeStruct((s, SUBLANES, lw), F32)] * 2 + [jax.ShapeDtypeStruct((s, d), F32)], grid=(s // ROWS,),
        in_specs=[pl.BlockSpec((None, ROWS, d), lambda i: (0, i, 0)),
                  pl.BlockSpec((None, SUBLANES, d), lambda i: (0, jnp.maximum(i * hb - 1, 0), 0)),
                  _vec_spec(d, SUBLANES), _vec_spec(d, SUBLANES), wspec, wspec],
        out_specs=[spec3, spec3, _row_spec(d)], scratch_shapes=[pltpu.VMEM((ROWS + SUBLANES, d), F32)],
        compiler_params=_params(("parallel",)), name="lru_gates_fwd")(rest, rest, cw8, vec8, wa, wx)


def _scan_fwd(a3, u3):
    s, _, lw = a3.shape

    def body(a_ref, u_ref, h_ref, hp_ref, carry):
        @pl.when(pl.program_id(0) == 0)
        def _():
            carry[...] = jnp.zeros_like(carry)

        def step(t, h):
            hp_ref[t] = h
            hn = a_ref[t] * h + u_ref[t]
            h_ref[t] = hn
            return hn

        carry[...] = lax.fori_loop(0, ROWS, step, carry[...], unroll=8)

    spec3 = pl.BlockSpec((ROWS, SUBLANES, lw), lambda i: (i, 0, 0))
    return pl.pallas_call(body, out_shape=[jax.ShapeDtypeStruct(a3.shape, F32)] * 2, grid=(s // ROWS,), in_specs=[spec3, spec3],
                          out_specs=[spec3, spec3], scratch_shapes=[pltpu.VMEM((SUBLANES, lw), F32)],
                          compiler_params=_params(("arbitrary",)), name="lru_scan_fwd")(a3, u3)


def _lru_out(h3, rest):
    s, _, lw = h3.shape
    d = lw * SUBLANES

    def body(h_ref, g_ref, y_ref, h2_ref):
        h = _from_3d(h_ref)
        h2_ref[...] = h
        y_ref[...] = (h * _gelu(g_ref[...])).astype(BF16)

    return pl.pallas_call(
        body, out_shape=[jax.ShapeDtypeStruct((s, d), BF16), jax.ShapeDtypeStruct((s, d), F32)], grid=(s // ROWS,),
        in_specs=[pl.BlockSpec((ROWS, SUBLANES, lw), lambda i: (i, 0, 0)), pl.BlockSpec((None, ROWS, d), lambda i: (1, i, 0))],
        out_specs=[_row_spec(d), _row_spec(d)], compiler_params=_params(("parallel",)), name="lru_out")(h3, rest)


def _scan_bwd(a3, hp3, dh):
    s, _, lw = a3.shape
    d = lw * SUBLANES
    nb = s // ROWS

    def body(a_ref, hp_ref, dh_ref, g_ref, da_ref, dh3, carry):
        @pl.when(pl.program_id(0) == 0)
        def _():
            carry[...] = jnp.zeros_like(carry)

        _to_3d(dh3, dh_ref[...])

        def step(j, c):
            t = ROWS - 1 - j
            g = dh3[t] + c
            g_ref[t] = g
            da_ref[t] = g * hp_ref[t]
            return a_ref[t] * g

        carry[...] = lax.fori_loop(0, ROWS, step, carry[...], unroll=8)

    spec3 = pl.BlockSpec((ROWS, SUBLANES, lw), lambda i: (nb - 1 - i, 0, 0))
    return pl.pallas_call(
        body, out_shape=[jax.ShapeDtypeStruct(a3.shape, F32)] * 2, grid=(nb,),
        in_specs=[spec3, spec3, pl.BlockSpec((ROWS, d), lambda i: (nb - 1 - i, 0))], out_specs=[spec3, spec3],
        scratch_shapes=[pltpu.VMEM((ROWS, SUBLANES, lw), F32), pltpu.VMEM((SUBLANES, lw), F32)],
        compiler_params=_params(("arbitrary",)), name="lru_scan_bwd")(a3, hp3, dh)


def _gates_bwd(g3, da3, xc, wa, wx, vec8):
    s, d = xc.shape
    lw = d // SUBLANES
    nh = d // HEAD

    def body(g_ref, da_ref, xc_ref, wa_ref, wx_ref, vec_ref, dxc_ref, dwa_ref, dwx_ref, dvec_ref):
        @pl.when(pl.program_id(0) == 0)
        def _():
            dwa_ref[...] = jnp.zeros_like(dwa_ref)
            dwx_ref[...] = jnp.zeros_like(dwx_ref)
            dvec_ref[...] = jnp.zeros_like(dvec_ref)

        xc_v, vec = xc_ref[...], vec_ref[...]
        du, da = _from_3d(g_ref), _from_3d(da_ref)
        gm = _gate_math(xc_v, wa_ref, wx_ref, vec)
        ra, ig, sp, a, mult = gm["ra"], gm["ig"], gm["sp"], gm["a"], gm["mult"]
        dmult = du * ig * xc_v
        dlog_a = da * a - dmult * (a * a) / mult
        dpre_a = dlog_a * (-LRU_C * sp) * ra * (1.0 - ra)
        dpre_x = du * mult * xc_v * ig * (1.0 - ig)
        dlam = jnp.sum(dlog_a * (-LRU_C * ra), axis=0, keepdims=True) * (-_sigmoid(-vec[VEC_LAM:VEC_LAM + 1]))
        dvec_ref[VEC_BA:VEC_BA + 1, :] += jnp.sum(dpre_a, axis=0, keepdims=True)
        dvec_ref[VEC_BX:VEC_BX + 1, :] += jnp.sum(dpre_x, axis=0, keepdims=True)
        dvec_ref[VEC_LAM:VEC_LAM + 1, :] += dlam
        dab, dxb, xcb = dpre_a.astype(BF16), dpre_x.astype(BF16), gm["xcb"]
        back = []
        for h in range(nh):
            cols = slice(h * HEAD, (h + 1) * HEAD)
            dwa_ref[h] += lax.dot_general(xcb[:, cols], dab[:, cols], TN, preferred_element_type=F32)
            dwx_ref[h] += lax.dot_general(xcb[:, cols], dxb[:, cols], TN, preferred_element_type=F32)
            back.append(lax.dot_general(dab[:, cols], wa_ref[h], NT, preferred_element_type=F32)
                        + lax.dot_general(dxb[:, cols], wx_ref[h], NT, preferred_element_type=F32))
        dxc_ref[...] = du * mult * ig + jnp.concatenate(back, axis=1)

    spec3 = pl.BlockSpec((ROWS, SUBLANES, lw), lambda i: (i, 0, 0))
    wspec = pl.BlockSpec(wa.shape, lambda i: (0, 0, 0))
    return pl.pallas_call(
        body, out_shape=[jax.ShapeDtypeStruct((s, d), F32), jax.ShapeDtypeStruct(wa.shape, F32), jax.ShapeDtypeStruct(wa.shape, F32),
                         jax.ShapeDtypeStruct((SUBLANES, d), F32)],
        grid=(s // ROWS,), in_specs=[spec3, spec3, _row_spec(d), wspec, wspec, _vec_spec(d, SUBLANES)],
        out_specs=[_row_spec(d), wspec, wspec, _vec_spec(d, SUBLANES)],
        compiler_params=_params(("arbitrary",)), name="lru_gates_bwd")(g3, da3, xc, wa, wx, vec8)


def _conv_bwd(dxc, rest, cw8):
    s, d = dxc.shape
    hb = ROWS // SUBLANES
    last = s // SUBLANES - 1

    def body(dc_ref, dnext_ref, x_ref, xprev_ref, cw_ref, dx_ref, dcw_ref, padd, padx):
        i = pl.program_id(0)

        @pl.when(i == 0)
        def _():
            dcw_ref[...] = jnp.zeros_like(dcw_ref)

        dc = dc_ref[...]
        padd[0:ROWS, :] = dc
        padd[ROWS:ROWS + SUBLANES, :] = jnp.where(i == pl.num_programs(0) - 1, 0.0, dnext_ref[...])
        _conv_pad_prev(padx, x_ref[...], xprev_ref[...], i == 0)
        dx = jnp.zeros_like(dc)
        for k in range(CONV_TAPS):
            dx = dx + cw_ref[k:k + 1, :] * padd[pl.ds(CONV_TAPS - 1 - k, ROWS), :]
            dcw_ref[k:k + 1, :] += jnp.sum(dc * padx[pl.ds(SUBLANES - (CONV_TAPS - 1) + k, ROWS), :], axis=0, keepdims=True)
        dcw_ref[CONV_TAPS:CONV_TAPS + 1, :] += jnp.sum(dc, axis=0, keepdims=True)
        dx_ref[...] = dx.astype(BF16)

    return pl.pallas_call(
        body, out_shape=[jax.ShapeDtypeStruct((s, d), BF16), jax.ShapeDtypeStruct((SUBLANES, d), F32)], grid=(s // ROWS,),
        in_specs=[_row_spec(d), pl.BlockSpec((SUBLANES, d), lambda i: (jnp.minimum((i + 1) * hb, last), 0)),
                  pl.BlockSpec((None, ROWS, d), lambda i: (0, i, 0)),
                  pl.BlockSpec((None, SUBLANES, d), lambda i: (0, jnp.maximum(i * hb - 1, 0), 0)), _vec_spec(d, SUBLANES)],
        out_specs=[_row_spec(d), _vec_spec(d, SUBLANES)],
        scratch_shapes=[pltpu.VMEM((ROWS + SUBLANES, d), F32), pltpu.VMEM((ROWS + SUBLANES, d), F32)],
        compiler_params=_params(("arbitrary",)), name="lru_conv_bwd")(dxc, dxc, rest, rest, cw8)


def _coords():
    return lax.axis_index("x"), lax.axis_index("y"), lax.axis_index("c")


def _other_chips(x, y):
    return [(1 - x, y), (x, 1 - y), (1 - x, 1 - y)]


def _slab(ref, kind, shard_shape, idx):
    r, c = shard_shape
    if kind == "col":
        return ref.at[:, pl.ds(pl.multiple_of(idx * c, LANES), c)]
    if kind == "row":
        return ref.at[pl.ds(pl.multiple_of(idx * r, SUBLANES), r), :]
    return ref.at[idx]


def _full_shape(shard_shape, kind):
    r, c = shard_shape
    return {"col": (r, c * N_DEV), "row": (r * N_DEV, c), "slot": (N_DEV, r, c)}[kind]


def _handshake(peers):
    barrier = pltpu.get_barrier_semaphore()
    for peer in peers:
        pl.semaphore_signal(barrier, inc=1, device_id=peer, device_id_type=MESH)
    pl.semaphore_wait(barrier, len(peers))


def _launch(name, body, out_shape, operands, sems, sequencer_id):
    if sequencer_id is None:
        return pl.pallas_call(body, out_shape=out_shape, in_specs=[HBM] * len(operands), out_specs=[HBM] * len(out_shape),
                              scratch_shapes=sems, name=name)(*operands)
    return pl.kernel(body, out_type=out_shape, mesh=plsc.ScalarSubcoreMesh(axis_name="seq", num_cores=1), name=name,
                     scratch_types=sems, compiler_params=pltpu.CompilerParams(collective_id=sequencer_id))(*operands)


def _all_gather(name, shards, kinds, sequencer_id=None):
    n = len(shards)
    shapes = [s.shape for s in shards]

    def body(*refs):
        ins, outs = refs[:n], refs[n:2 * n]
        send_sems, recv_sems, local_sems = refs[2 * n:]
        x, y, c = _coords()
        me, sib = (x, y, c), (x, y, 1 - c)
        chips = _other_chips(x, y)
        if sequencer_id is not None:
            _handshake([sib] + [(*chip, c) for chip in chips])

        def part(i, dev):
            return _slab(outs[i], kinds[i], shapes[i], 4 * dev[0] + 2 * dev[1] + dev[2])

        def copy(i, k, block, to, src=None):
            return pltpu.make_async_remote_copy(
                src_ref=part(i, block) if src is None else src, dst_ref=part(i, block),
                send_sem=send_sems.at[7 * i + k], recv_sem=recv_sems.at[7 * i + k], device_id=to, device_id_type=MESH)

        started = []
        for i in range(n):
            for k, to in enumerate([sib] + [(*chip, c) for chip in chips]):
                cp = copy(i, k, me, to, src=ins[i])
                cp.start()
                started.append(cp)
        mine = [pltpu.make_async_copy(ins[i], part(i, me), local_sems.at[i]) for i in range(n)]
        for cp in mine:
            cp.start()
        for i in range(n):
            for j, chip in enumerate(chips):
                copy(i, 1 + j, (*chip, c), me).wait_recv()
                cp = copy(i, 4 + j, (*chip, c), sib)
                cp.start()
                started.append(cp)
        for i in range(n):
            copy(i, 0, sib, me).wait_recv()
            for j, chip in enumerate(chips):
                copy(i, 4 + j, (*chip, 1 - c), me).wait_recv()
        for cp in started:
            cp.wait_send()
        for cp in mine:
            cp.wait()

    out_shape = [jax.ShapeDtypeStruct(_full_shape(s.shape, k), s.dtype) for s, k in zip(shards, kinds)]
    sems = [pltpu.SemaphoreType.DMA((7 * n,)), pltpu.SemaphoreType.DMA((7 * n,)), pltpu.SemaphoreType.DMA((n,))]
    return _launch(name, body, out_shape, shards, sems, sequencer_id)


def _exchange_siblings(name, partials, kinds, shard_shapes, sequencer_id=None):
    n = len(partials)

    def body(*refs):
        ins, outs = refs[:n], refs[n:2 * n]
        send_sems, recv_sems = refs[2 * n:]
        x, y, c = _coords()
        if sequencer_id is not None:
            _handshake([(x, y, 1 - c)])
        cps = []
        for i in range(n):
            for q in range(N_CHIP):
                cps.append(pltpu.make_async_remote_copy(
                    src_ref=_slab(ins[i], kinds[i], shard_shapes[i], 2 * q + (1 - c)), dst_ref=outs[i].at[q],
                    send_sem=send_sems.at[N_CHIP * i + q], recv_sem=recv_sems.at[N_CHIP * i + q],
                    device_id=(x, y, 1 - c), device_id_type=MESH))
        for cp in cps:
            cp.start()
        for cp in cps:
            cp.wait()

    return _launch(name, body, [jax.ShapeDtypeStruct((N_CHIP, *s), BF16) for s in shard_shapes], partials,
                   [pltpu.SemaphoreType.DMA((N_CHIP * n,)), pltpu.SemaphoreType.DMA((N_CHIP * n,))], sequencer_id)


def _exchange_chips(name, chip_sums, sequencer_id=None):
    n = len(chip_sums)

    def body(*refs):
        ins, outs = refs[:n], refs[n:2 * n]
        send_sems, recv_sems = refs[2 * n:]
        x, y, c = _coords()
        if sequencer_id is not None:
            _handshake([(cx, cy, c) for cx, cy in _other_chips(x, y)])
        cps = []
        for i in range(n):
            for k, (cx, cy) in enumerate(_other_chips(x, y)):
                cps.append(pltpu.make_async_remote_copy(
                    src_ref=ins[i].at[2 * cx + cy], dst_ref=outs[i].at[k], send_sem=send_sems.at[3 * i + k],
                    recv_sem=recv_sems.at[3 * i + k], device_id=(cx, cy, c), device_id_type=MESH))
        for cp in cps:
            cp.start()
        for cp in cps:
            cp.wait()

    return _launch(name, body, [jax.ShapeDtypeStruct((3, *t.shape[1:]), BF16) for t in chip_sums], chip_sums,
                   [pltpu.SemaphoreType.DMA((3 * n,)), pltpu.SemaphoreType.DMA((3 * n,))], sequencer_id)


def _all_reduce_small(packed):
    rows = packed.shape[0] // N_DEV

    def body(p_ref, out_ref, rb, tot, send_sems, recv_sems):
        x, y, c = _coords()
        me = 4 * x + 2 * y + c

        def peer(k):
            return (x ^ (k >> 2), y ^ ((k >> 1) & 1), c ^ (k & 1))

        def rows_of(idx):
            return pl.ds(pl.multiple_of(idx * rows, SUBLANES), rows)

        def piece(ref, idx):
            return ref.at[rows_of(idx), :]

        scatter = [pltpu.make_async_remote_copy(src_ref=piece(p_ref, me ^ k), dst_ref=rb.at[k], send_sem=send_sems.at[k],
                                                recv_sem=recv_sems.at[k], device_id=peer(k), device_id_type=MESH) for k in range(1, N_DEV)]
        for cp in scatter:
            cp.start()
        acc = p_ref[rows_of(me), :]
        for cp in scatter:
            cp.wait_recv()
        for k in range(1, N_DEV):
            acc = acc + rb[k]
        tot[...] = acc
        out_ref[rows_of(me), :] = acc
        gather = [pltpu.make_async_remote_copy(src_ref=tot, dst_ref=piece(out_ref, me), send_sem=send_sems.at[N_DEV + k],
                                               recv_sem=recv_sems.at[N_DEV + k], device_id=peer(k), device_id_type=MESH)
                  for k in range(1, N_DEV)]
        for cp in gather:
            cp.start()
        for k in range(1, N_DEV):
            pltpu.make_async_remote_copy(src_ref=tot, dst_ref=piece(out_ref, me ^ k), send_sem=send_sems.at[N_DEV + k],
                                         recv_sem=recv_sems.at[N_DEV + k], device_id=peer(k), device_id_type=MESH).wait_recv()
        for cp in scatter + gather:
            cp.wait_send()

    vm = pl.BlockSpec(memory_space=pltpu.VMEM)
    return pl.pallas_call(
        body, out_shape=jax.ShapeDtypeStruct(packed.shape, F32), in_specs=[vm], out_specs=vm,
        scratch_shapes=[pltpu.VMEM((N_DEV, rows, LANES), F32), pltpu.VMEM((rows, LANES), F32),
                        pltpu.SemaphoreType.DMA((2 * N_DEV,)), pltpu.SemaphoreType.DMA((2 * N_DEV,))],
        compiler_params=pltpu.CompilerParams(vmem_limit_bytes=VMEM_LIMIT), name="all_reduce_small")(packed)


def _adamw_math(g, w, m, v):
    m = ADAM_B1 * m + (1.0 - ADAM_B1) * g
    v = ADAM_B2 * v + (1.0 - ADAM_B2) * (g * g)
    delta = -ADAM_LR * ((m / ADAM_C1) / (jnp.sqrt(v / ADAM_C2) + ADAM_EPS) + ADAM_WD * w)
    return delta, m, v


def _slab_spec(kind, shard_shape, tr, slab_of):
    r, c = shard_shape
    if kind == "col":
        return pl.BlockSpec((tr, c), lambda q, i, sc: (i, slab_of(q, sc)))
    return pl.BlockSpec((tr, c), lambda q, i, sc: (slab_of(q, sc) * (r // tr) + i, 0))


def _chip_sum(name, partial, recv, kind, shard_shape, core):
    r, c = shard_shape
    tr = _blk(r, 256)

    def body(core_ref, p_ref, r_ref, o_ref):
        o_ref[...] = (p_ref[...].astype(F32) + r_ref[...].astype(F32)).astype(BF16)

    spec4 = pl.BlockSpec((None, tr, c), lambda q, i, sc: (q, i, 0))
    grid_spec = pltpu.PrefetchScalarGridSpec(
        num_scalar_prefetch=1, grid=(N_CHIP, r // tr),
        in_specs=[_slab_spec(kind, shard_shape, tr, lambda q, sc: 2 * q + sc[0]), spec4], out_specs=spec4)
    return pl.pallas_call(body, out_shape=jax.ShapeDtypeStruct((N_CHIP, r, c), BF16), grid_spec=grid_spec,
                          compiler_params=_params(("parallel", "parallel")), name=name)(core, partial, recv)


def _adamw_shard(name, parts, w, m, v, chip):
    r, c = w.shape
    n_parts = len(parts)
    tr = _blk(r // n_parts, 128)
    per = r // n_parts // tr

    def body(chip_ref, *refs):
        src, (w_ref, m_ref, v_ref), (g_out, d_out, m_out, v_out) = refs[:2 * n_parts], refs[2 * n_parts:2 * n_parts + 3], refs[2 * n_parts + 3:]
        for p in range(n_parts):
            @pl.when(pl.program_id(0) // per == p)
            def _():
                g = src[2 * p][...].astype(F32)
                for k in range(3):
                    g = g + src[2 * p + 1][k].astype(F32)
                g_out[...] = g
                d_out[...], m_out[...], v_out[...] = _adamw_math(g, w_ref[...], m_ref[...], v_ref[...])

    def part_specs(p):
        at = lambda i: jnp.clip(i - p * per, 0, per - 1)
        return [pl.BlockSpec((None, tr, c), lambda i, sc: (sc[0], at(i), 0)), pl.BlockSpec((3, tr, c), lambda i, sc: (0, at(i), 0))]

    blk = pl.BlockSpec((tr, c), lambda i, sc: (i, 0))
    grid_spec = pltpu.PrefetchScalarGridSpec(
        num_scalar_prefetch=1, grid=(r // tr,), in_specs=[s for p in range(n_parts) for s in part_specs(p)] + [blk, blk, blk], out_specs=[blk] * 4)
    return pl.pallas_call(body, out_shape=[jax.ShapeDtypeStruct((r, c), F32)] * 4, grid_spec=grid_spec,
                          compiler_params=_params(("parallel",)), name=name)(chip, *[a for p in parts for a in p], w, m, v)


def _adamw_small(name, g, w, m, v):
    def body(g_ref, w_ref, m_ref, v_ref, d_out, m_out, v_out):
        d_out[...], m_out[...], v_out[...] = _adamw_math(g_ref[...], w_ref[...], m_ref[...], v_ref[...])

    vm = pl.BlockSpec(memory_space=pltpu.VMEM)
    return pl.pallas_call(body, out_shape=[jax.ShapeDtypeStruct(g.shape, F32)] * 3, in_specs=[vm] * 4, out_specs=[vm] * 3,
                          compiler_params=pltpu.CompilerParams(vmem_limit_bytes=VMEM_LIMIT), name=name)(g, w, m, v)


def _pack_rows(arrays, total_rows):
    flat = [a.reshape(-1, LANES) for a in arrays]
    used = sum(f.shape[0] for f in flat)
    return jnp.concatenate(flat + [jnp.zeros((total_rows - used, LANES), F32)], axis=0)


def _unpack_rows(packed, like):
    out, at = [], 0
    for a in like:
        n = a.size // LANES
        out.append(packed[at:at + n].reshape(a.shape))
        at += n
    return out


def kernel(x, norm_mix_g, w_in, conv_w, conv_b, lru_wa, lru_ba, lru_wx, lru_bx, lru_lambda, w_proj_attn, w_proj_lru, w_out, norm_mlp_g, w_up, w_down, norm_final_g, loss_target, m_norm_mix_g, m_w_in, m_conv_w, m_conv_b, m_lru_wa, m_lru_ba, m_lru_wx, m_lru_bx, m_lru_lambda, m_w_proj_attn, m_w_proj_lru, m_w_out, m_norm_mlp_g, m_w_up, m_w_down, m_norm_final_g, v_norm_mix_g, v_w_in, v_conv_w, v_conv_b, v_lru_wa, v_lru_ba, v_lru_wx, v_lru_bx, v_lru_lambda, v_w_proj_attn, v_w_proj_lru, v_w_out, v_norm_mlp_g, v_w_up, v_w_down, v_norm_final_g):
    xs, tgt = x[0], loss_target[0]
    s, d = xs.shape
    nh = d // HEAD
    ix, iy, ic = _coords()
    core = jnp.reshape(ic, (1,)).astype(jnp.int32)
    chip = jnp.reshape(2 * ix + iy, (1,)).astype(jnp.int32)
    dev = 4 * ix + 2 * iy + ic

    big = [w_in[0], w_proj_attn[0], w_proj_lru[0], w_out[0], w_up[0], w_down[0]]
    big_m = [m_w_in[0], m_w_proj_attn[0], m_w_proj_lru[0], m_w_out[0], m_w_up[0], m_w_down[0]]
    big_v = [v_w_in[0], v_w_proj_attn[0], v_w_proj_lru[0], v_w_out[0], v_w_up[0], v_w_down[0]]
    kinds = ["col", "row", "row", "row", "col", "row"]
    pad_taps = lambda t: jnp.pad(t, ((0, SUBLANES - CONV_TAPS), (0, 0)))
    shards = [w.astype(BF16) for w in big]
    win, cw_slots = _all_gather("all_gather_w_in", [shards[0], pad_taps(conv_w[0])], ["col", "slot"])
    later = lax.optimization_barrier((shards[1:], win))[0]
    wpa, wpl, wout = _all_gather("all_gather_mix", later[:3], kinds[1:4], sequencer_id=1)
    wup, wdown = _all_gather("all_gather_mlp", later[3:], kinds[4:], sequencer_id=5)
    cw8 = jnp.transpose(cw_slots, (1, 0, 2)).reshape(SUBLANES, d)
    row_id = lax.broadcasted_iota(jnp.int32, (SUBLANES, d), 0)
    vec8 = sum(jnp.where(row_id == k, t, 0.0) for k, t in ((VEC_CB, conv_b), (VEC_BA, lru_ba), (VEC_BX, lru_bx), (VEC_LAM, lru_lambda)))
    wa16, wx16 = lru_wa[0].astype(BF16), lru_wx[0].astype(BF16)
    slopes = 2.0 ** (-8.0 * jnp.arange(1, nh + 1, dtype=F32) / nh)

    def seg_specs(*segs):
        return lambda bm, bn: [pl.BlockSpec((None, bm, bn), (lambda i, j, kk, sg=sg: (sg, i, j))) for sg in segs]

    def plain_specs(k):
        return lambda bm, bn: [pl.BlockSpec((bm, bn), lambda i, j, kk: (i, j)) for _ in range(k)]

    xn = _rms_fwd("norm_mix", xs, norm_mix_g)
    qkv = _mm_fwd("proj_qkv", xn, win, 0, 3 * d, [F32], seg_out=d)[0]
    rest = _mm_fwd("proj_rest", xn, win, 3 * d, 4 * d, [F32], seg_out=d)[0]
    att, lse = _attn_fwd(qkv, slopes)
    a3, u3, xc = _gates_fwd(rest, cw8, vec8, wa16, wx16)
    h3, hp3 = _scan_fwd(a3, u3)
    ylru, h2d = _lru_out(h3, rest)
    pa = _mm_fwd("proj_attn", att, wpa, 0, d, [F32])[0]

    def merge(acc, pa_b, ga, gl):
        return acc, _sigmoid(ga) * pa_b + _sigmoid(gl) * acc

    plr, merged = _mm_fwd("proj_lru_merge", ylru, wpl, 0, d, [F32, BF16], merge, (pa, rest, rest),
                          lambda bm, bn: plain_specs(1)(bm, bn) + seg_specs(2, 3)(bm, bn), bm=512)
    h1 = _mm_fwd("mix_out", merged, wout, 0, d, [F32], lambda acc, r: (acc + r,), (xs,), plain_specs(1))[0]
    hn = _rms_fwd("norm_mlp", h1, norm_mlp_g)

    def relu2(acc):
        return acc, jnp.square(jnp.maximum(acc, 0.0))

    up, hid = _mm_fwd("mlp_up", hn, wup, 0, wup.shape[1], [BF16, BF16], relu2)
    h2 = _mm_fwd("mlp_down", hid, wdown, 0, d, [F32], lambda acc, r: (acc + r,), (h1,), plain_specs(1))[0]
    dh2, dh2b, dg3, loss_lanes = _final_loss(h2, tgt, norm_final_g.reshape(1, d))
    loss = lax.psum(0.5 / d * jnp.sum(loss_lanes), ("x", "y", "c"))

    def reduce_group(tag, kk, shp, partials, sequencer_id):
        from_sibling = _exchange_siblings(f"rs_siblings_{tag}", partials, kk, shp)
        sums = [_chip_sum(f"chip_sum_{tag}_{i}", p, f, k, sh, core) for i, (p, f, k, sh) in enumerate(zip(partials, from_sibling, kk, shp))]
        return list(zip(sums, _exchange_chips(f"rs_chips_{tag}", sums, sequencer_id)))

    dup = _mm_nt("mlp_down_dx", dh2b, wdown, [BF16], lambda acc, u: (acc * (2.0 * jnp.maximum(u.astype(F32), 0.0)),), (up,), plain_specs(1))[0]
    g_wdown = _mm_tn("mlp_down_dw", hid, dh2b)
    dhn = _mm_nt("mlp_up_dx", dup, wup, [F32])[0]
    g_wup = _mm_tn("mlp_up_dw", hn, dup)
    red_up, red_down = reduce_group("mlp", kinds[4:], [w.shape for w in big[4:]], [g_wup, g_wdown], 2)
    dhn = lax.optimization_barrier((dhn, red_up[0], red_down[0]))[0]
    dh1, dh1b, dg2 = _rms_bwd("norm_mlp_bwd", h1, norm_mlp_g, dhn, dh2)

    def merge_bwd(acc, pa_b, pl_b, ga, gl):
        sa, sl = _sigmoid(ga), _sigmoid(gl)
        return acc * sa, acc * sl, acc * pa_b * sa * (1.0 - sa), acc * pl_b * sl * (1.0 - sl)

    dpa, dpl, dga, dgl = _mm_nt("mix_out_dx", dh1b, wout, [BF16] * 4, merge_bwd, (pa, plr, rest, rest),
                                lambda bm, bn: plain_specs(2)(bm, bn) + seg_specs(2, 3)(bm, bn), bm=512)
    g_wout = _mm_tn("mix_out_dw", merged, dh1b)
    datt = _mm_nt("proj_attn_dx", dpa, wpa, [F32])[0]
    g_wpa = _mm_tn("proj_attn_dw", att, dpa)

    def lru_out_bwd(acc, h_b, gate):
        return acc * _gelu(gate), acc * h_b * _gelu_grad(gate)

    dh, dxg = _mm_nt("proj_lru_dx", dpl, wpl, [F32, BF16], lru_out_bwd, (h2d, rest),
                     lambda bm, bn: plain_specs(1)(bm, bn) + seg_specs(1)(bm, bn), bm=512)
    g_wpl = _mm_tn("proj_lru_dw", ylru, dpl)
    red_pa, red_pl, red_out = reduce_group("mix", kinds[1:4], [w.shape for w in big[1:4]], [g_wpa, g_wpl, g_wout], 3)
    dq, dk, dv = _attn_bwd(qkv, datt, att, lse, slopes)
    g3, da3 = _scan_bwd(a3, hp3, dh)
    dxc, dwa, dwx, dvec = _gates_bwd(g3, da3, xc, wa16, wx16, vec8)
    dxr, dconv = _conv_bwd(dxc, rest, cw8)
    dproj = jnp.concatenate([dq, dk, dv, dxr, dxg, dga, dgl], axis=1)
    big_out = {i: _adamw_shard(f"adamw_{i}", [red], big[i], big_m[i], big_v[i], chip) for i, red in ((4, red_up), (5, red_down))}
    red_mix = lax.optimization_barrier(((red_pa, red_pl, red_out), (big_out[4], big_out[5])))[0]
    big_out.update({i: _adamw_shard(f"adamw_{i}", [red], big[i], big_m[i], big_v[i], chip) for i, red in zip((1, 2, 3), red_mix)})
    dproj = lax.optimization_barrier((dproj, [big_out[i] for i in range(1, 6)]))[0]
    half = (big[0].shape[0] // 2, big[0].shape[1])
    red_in = []
    for p in range(2):
        g_half = _mm_tn(f"proj_in_dw_{p}", xn, dproj, part=(p, 2))
        red_in += reduce_group(f"in_{p}", ["col"], [half], [g_half], 4 + 2 * p)
        dproj = lax.optimization_barrier((dproj, red_in[-1][0]))[0]
    dxn = _mm_nt("proj_in_dx", dproj, win, [F32])[0]
    grad_x, _, dg1 = _rms_bwd("norm_mix_bwd", xs, norm_mix_g, dxn, dh1)
    big_out[0] = _adamw_shard("adamw_0", red_in, big[0], big_m[0], big_v[0], chip)

    small_w = [norm_mix_g, conv_b, lru_wa, lru_ba, lru_wx, lru_bx, lru_lambda, norm_mlp_g, norm_final_g]
    small_m = [m_norm_mix_g, m_conv_b, m_lru_wa, m_lru_ba, m_lru_wx, m_lru_bx, m_lru_lambda, m_norm_mlp_g, m_norm_final_g]
    small_v = [v_norm_mix_g, v_conv_b, v_lru_wa, v_lru_ba, v_lru_wx, v_lru_bx, v_lru_lambda, v_norm_mlp_g, v_norm_final_g]
    small_g = [dg1, dconv[CONV_TAPS:CONV_TAPS + 1], dwa, dvec[VEC_BA:VEC_BA + 1], dwx, dvec[VEC_BX:VEC_BX + 1],
               dvec[VEC_LAM:VEC_LAM + 1], dg2, dg3, dconv[0:CONV_TAPS]]
    n_rows = sum(g.size for g in small_g) // LANES
    per_dev = -(-n_rows // (N_DEV * SUBLANES)) * SUBLANES
    total = _all_reduce_small(_pack_rows(small_g, N_DEV * per_dev))
    sw_rows = -(-(n_rows - CONV_TAPS * d // LANES) // SUBLANES) * SUBLANES
    s_delta, s_m, s_v = _adamw_small("adamw_small", total[:sw_rows], _pack_rows(small_w, sw_rows), _pack_rows(small_m, sw_rows),
                                     _pack_rows(small_v, sw_rows))
    sg = _unpack_rows(total, [w for w in small_w] + [jnp.zeros((1, CONV_TAPS, d), F32)])
    s_grad, g_cw_full = sg[:-1], sg[-1]
    s_delta, s_m, s_v = (_unpack_rows(t, small_w) for t in (s_delta, s_m, s_v))
    cshard = conv_w.shape[2]
    g_cw = lax.dynamic_slice(g_cw_full, (0, 0, dev * cshard), (1, CONV_TAPS, cshard))
    cw_delta, cw_m, cw_v = (t[:CONV_TAPS][None] for t in _adamw_small(
        "adamw_conv_w", pad_taps(g_cw[0]), pad_taps(conv_w[0]), pad_taps(m_conv_w[0]), pad_taps(v_conv_w[0])))

    names = ["norm_mix_g", "w_in", "conv_w", "conv_b", "lru_wa", "lru_ba", "lru_wx", "lru_bx", "lru_lambda", "w_proj_attn", "w_proj_lru",
             "w_out", "norm_mlp_g", "w_up", "w_down", "norm_final_g"]
    small_names = ["norm_mix_g", "conv_b", "lru_wa", "lru_ba", "lru_wx", "lru_bx", "lru_lambda", "norm_mlp_g", "norm_final_g"]
    big_names = ["w_in", "w_proj_attn", "w_proj_lru", "w_out", "w_up", "w_down"]
    res = {"conv_w": (g_cw, cw_delta, cw_m, cw_v)}
    for i, nm in enumerate(small_names):
        res[nm] = (s_grad[i], s_delta[i], s_m[i], s_v[i])
    for i, nm in enumerate(big_names):
        res[nm] = tuple(t[None] for t in big_out[i])
    return (loss, grad_x[None], *[res[nm][0] for nm in names], *[res[nm][1] for nm in names],
            *[res[nm][2] for nm in names], *[res[nm][3] for nm in names])
```

```python
import jax
import jax.numpy as jnp
from jax import lax
from jax.experimental import pallas as pl
from jax.experimental.pallas import tpu as pltpu
from jax.experimental.pallas import tpu_sc as plsc

F32, BF16 = jnp.float32, jnp.bfloat16
MESH = pl.DeviceIdType.MESH
HBM = pl.BlockSpec(memory_space=pltpu.HBM)
N_DEV = 8
N_CHIP = 4
HEAD = 128
SPAN = 128
DILATIONS = (1, 4, 16)
CONV_TAPS = 4
LRU_C = 8.0
NORM_EPS = 1e-6
LANES = 128
SUBLANES = 8
VMEM_LIMIT = 56 * 1024 * 1024
ADAM_LR, ADAM_B1, ADAM_B2, ADAM_EPS, ADAM_WD, ADAM_STEP = 0.001, 0.9, 0.999, 1e-08, 0.01, 10
ADAM_C1 = 1.0 - ADAM_B1 ** ADAM_STEP
ADAM_C2 = 1.0 - ADAM_B2 ** ADAM_STEP
NEG = -1e30


def _params(sem=None):
    return pltpu.CompilerParams(dimension_semantics=sem, vmem_limit_bytes=VMEM_LIMIT)


def _sigmoid(v):
    return 1.0 / (1.0 + jnp.exp(-v))


def _gelu(v):
    k = 0.7978845608028654
    return 0.5 * v * (1.0 + jnp.tanh(k * (v + 0.044715 * v * v * v)))


def _gelu_grad(v):
    k = 0.7978845608028654
    t = jnp.tanh(k * (v + 0.044715 * v * v * v))
    return 0.5 * (1.0 + t) + 0.5 * v * (1.0 - t * t) * k * (1.0 + 3.0 * 0.044715 * v * v)


NN = (((1,), (0,)), ((), ()))
NT = (((1,), (1,)), ((), ()))
TN = (((0,), (0,)), ((), ()))


def _mm(name, a, a_spec, b, b_spec, dn, grid, out_shapes, out_specs, acc_block, epilogue=None, extras=(), extra_specs=()):
    nk, ne, no = grid[2], len(extras), len(out_shapes)

    def body(*refs):
        a_ref, b_ref = refs[0], refs[1]
        ex, outs = refs[2:2 + ne], refs[2 + ne:2 + ne + no]
        part = lax.dot_general(a_ref[...], b_ref[...], dn, preferred_element_type=F32)

        def finish(acc):
            vals = epilogue(acc, *[e[...] for e in ex]) if epilogue is not None else (acc,)
            for o, v in zip(outs, vals):
                o[...] = v.astype(o.dtype)

        if nk == 1:
            finish(part)
        else:
            acc_ref, k = refs[-1], pl.program_id(2)

            @pl.when(k == 0)
            def _():
                acc_ref[...] = part

            @pl.when(k > 0)
            def _():
                acc_ref[...] += part

            @pl.when(k == nk - 1)
            def _():
                finish(acc_ref[...])

    return pl.pallas_call(
        body, out_shape=out_shapes, grid=grid, in_specs=[a_spec, b_spec, *extra_specs], out_specs=out_specs,
        scratch_shapes=[pltpu.VMEM(acc_block, F32)] if nk > 1 else [],
        compiler_params=_params(("parallel", "parallel", "arbitrary")), name=name)(a, b, *extras)


def _blk(n, pref):
    return pref if n % pref == 0 else n


def _kblk(k):
    return k if k <= 2048 else next(b for b in (2048, 1024, 512) if k % b == 0)


def _mm_fwd(name, a, w, col0, ncols, out_dtypes, epilogue=None, extras=(), extra_specs_fn=None, seg_out=None, bm=1024, bn=1024):
    m, k = a.shape
    bm, bn = _blk(m, bm), _blk(ncols, bn)
    bk = _kblk(k)
    nk = k // bk
    cb0 = col0 // bn
    grid = (m // bm, ncols // bn, nk)
    a_spec = pl.BlockSpec((bm, bk), lambda i, j, kk: (i, kk))
    b_spec = pl.BlockSpec((bk, bn), lambda i, j, kk: (kk, cb0 + j))
    if seg_out is None:
        shapes = [jax.ShapeDtypeStruct((m, ncols), dt) for dt in out_dtypes]
        specs = [pl.BlockSpec((bm, bn), lambda i, j, kk: (i, j)) for _ in out_dtypes]
    else:
        per = seg_out // bn
        shapes = [jax.ShapeDtypeStruct((ncols // seg_out, m, seg_out), dt) for dt in out_dtypes]
        specs = [pl.BlockSpec((None, bm, bn), lambda i, j, kk: (j // per, i, j % per)) for _ in out_dtypes]
    ex_specs = extra_specs_fn(bm, bn) if extra_specs_fn else ()
    return _mm(name, a, a_spec, w, b_spec, NN, grid, shapes, specs, (bm, bn), epilogue, extras, ex_specs)


def _mm_nt(name, a, w, out_dtypes, epilogue=None, extras=(), extra_specs_fn=None, bm=1024, bn=1024):
    m, k = a.shape
    n = w.shape[0]
    bm, bn = _blk(m, bm), _blk(n, bn)
    bk = _kblk(k)
    grid = (m // bm, n // bn, k // bk)
    a_spec = pl.BlockSpec((bm, bk), lambda i, j, kk: (i, kk))
    b_spec = pl.BlockSpec((bn, bk), lambda i, j, kk: (j, kk))
    shapes = [jax.ShapeDtypeStruct((m, n), dt) for dt in out_dtypes]
    specs = [pl.BlockSpec((bm, bn), lambda i, j, kk: (i, j)) for _ in out_dtypes]
    ex_specs = extra_specs_fn(bm, bn) if extra_specs_fn else ()
    return _mm(name, a, a_spec, w, b_spec, NT, grid, shapes, specs, (bm, bn), epilogue, extras, ex_specs)


def _mm_tn(name, a, b, part=(0, 1), bm=1024, bn=1024):
    t, m = a.shape
    n = b.shape[1]
    m = m // part[1]
    bm, bn = _blk(m, bm), _blk(n, bn)
    grid = (m // bm, n // bn, 1)
    i0 = part[0] * (m // bm)
    a_spec = pl.BlockSpec((t, bm), lambda i, j, kk: (0, i0 + i))
    b_spec = pl.BlockSpec((t, bn), lambda i, j, kk: (0, j))
    return _mm(name, a, a_spec, b, b_spec, TN, grid, [jax.ShapeDtypeStruct((m, n), BF16)],
               [pl.BlockSpec((bm, bn), lambda i, j, kk: (i, j))], (bm, bn))[0]


ROWS = 256


def _row_spec(d):
    return pl.BlockSpec((ROWS, d), lambda i: (i, 0))


def _vec_spec(d, rows=1):
    return pl.BlockSpec((rows, d), lambda i: (0, 0))


def _rms_fwd(name, x, g):
    s, d = x.shape

    def body(x_ref, g_ref, o_ref):
        xv = x_ref[...]
        r = lax.rsqrt(jnp.mean(xv * xv, axis=-1, keepdims=True) + NORM_EPS)
        o_ref[...] = (xv * r * g_ref[...]).astype(BF16)

    return pl.pallas_call(body, out_shape=jax.ShapeDtypeStruct((s, d), BF16), grid=(s // ROWS,),
                          in_specs=[_row_spec(d), _vec_spec(d)], out_specs=_row_spec(d),
                          compiler_params=_params(("parallel",)), name=name)(x, g)


def _rms_bwd_math(xv, g, dy):
    r = lax.rsqrt(jnp.mean(xv * xv, axis=-1, keepdims=True) + NORM_EPS)
    n = xv * r
    z = dy * g
    dx = r * (z - n * jnp.mean(z * n, axis=-1, keepdims=True))
    return dx, jnp.sum(dy * n, axis=0, keepdims=True)


def _rms_bwd(name, x, g, dy, resid):
    s, d = x.shape

    def body(x_ref, g_ref, dy_ref, r_ref, dx_ref, dxb_ref, dg_ref):
        dx, dg = _rms_bwd_math(x_ref[...], g_ref[...], dy_ref[...])
        dx = dx + r_ref[...]
        dx_ref[...] = dx
        dxb_ref[...] = dx.astype(BF16)

        @pl.when(pl.program_id(0) == 0)
        def _():
            dg_ref[...] = jnp.zeros_like(dg_ref)

        dg_ref[...] += dg

    return pl.pallas_call(
        body, out_shape=[jax.ShapeDtypeStruct((s, d), F32), jax.ShapeDtypeStruct((s, d), BF16), jax.ShapeDtypeStruct((1, d), F32)],
        grid=(s // ROWS,), in_specs=[_row_spec(d), _vec_spec(d), _row_spec(d), _row_spec(d)],
        out_specs=[_row_spec(d), _row_spec(d), _vec_spec(d)], compiler_params=_params(("arbitrary",)), name=name)(x, g, dy, resid)


def _final_loss(h2, tgt, g):
    s, d = h2.shape

    def body(x_ref, t_ref, g_ref, dx_ref, dxb_ref, dg_ref, ls_ref):
        xv, gv = x_ref[...], g_ref[...]
        r = lax.rsqrt(jnp.mean(xv * xv, axis=-1, keepdims=True) + NORM_EPS)
        diff = xv * r * gv - t_ref[...]
        dx, dg = _rms_bwd_math(xv, gv, diff * (1.0 / d))
        dx_ref[...] = dx
        dxb_ref[...] = dx.astype(BF16)

        @pl.when(pl.program_id(0) == 0)
        def _():
            dg_ref[...] = jnp.zeros_like(dg_ref)
            ls_ref[...] = jnp.zeros_like(ls_ref)

        dg_ref[...] += dg
        ls_ref[...] += jnp.sum(diff * diff, axis=0, keepdims=True)

    return pl.pallas_call(
        body, out_shape=[jax.ShapeDtypeStruct((s, d), F32), jax.ShapeDtypeStruct((s, d), BF16),
                         jax.ShapeDtypeStruct((1, d), F32), jax.ShapeDtypeStruct((1, d), F32)],
        grid=(s // ROWS,), in_specs=[_row_spec(d), _row_spec(d), _vec_spec(d)],
        out_specs=[_row_spec(d), _row_spec(d), _vec_spec(d), _vec_spec(d)],
        compiler_params=_params(("arbitrary",)), name="final_norm_loss")(h2, tgt, g)


ATTN_Q = 128


ATTN_BATCH = 8


def _attn_units(s):
    units = []
    for gi, d in enumerate(DILATIONS):
        for r in range(d):
            for q0 in range(0, s // d, ATTN_Q):
                k0 = max(q0 - SPAN, 0)
                units.append((gi, d, r, q0, k0, q0 + ATTN_Q - k0))
    return units


def _stream_rows(d, r, start, size):
    return pl.ds(r + start * d, size) if d == 1 else pl.ds(r + start * d, size, stride=d)


def _attn_scores(q_ref, k_ref, slope, d, r, q0, k0, nk):
    qrows, krows = _stream_rows(d, r, q0, ATTN_Q), _stream_rows(d, r, k0, nk)
    qb = q_ref[qrows, :].astype(BF16)
    kb = k_ref[krows, :].astype(BF16)
    sc = lax.dot_general(qb, kb, NT, preferred_element_type=F32) * (HEAD ** -0.5)
    qi = lax.broadcasted_iota(jnp.int32, (ATTN_Q, nk), 0)
    kj = lax.broadcasted_iota(jnp.int32, (ATTN_Q, nk), 1)
    dist = (q0 - k0) + qi - kj
    valid = (dist >= 0) & (dist <= SPAN)
    sc = sc - (slope * d) * dist.astype(F32)
    return jnp.where(valid, sc, NEG), valid, qb, kb, qrows, krows


def _attn_fwd(qkv, slopes):
    _, s, dm = qkv.shape
    units = _attn_units(s)

    def body(sl_ref, q_ref, k_ref, v_ref, att_ref, lse_ref, *scr):
        o_scr, l_scr = scr[:3], scr[3:]
        slope = sl_ref[pl.program_id(0)]
        for first in range(0, len(units), ATTN_BATCH):
            batch = units[first:first + ATTN_BATCH]
            scored = [_attn_scores(q_ref, k_ref, slope, d, r, q0, k0, nk) for _, d, r, q0, k0, nk in batch]
            soft = []
            for sc, _, _, _, _, _ in scored:
                m = jnp.max(sc, axis=-1, keepdims=True)
                p = jnp.exp(sc - m)
                soft.append((m, p, jnp.sum(p, axis=-1, keepdims=True)))
            outs = [lax.dot_general(p.astype(BF16), v_ref[sco[5], :].astype(BF16), NN, preferred_element_type=F32)
                    for (m, p, l), sco in zip(soft, scored)]
            for (gi, *_), (m, p, l), sco, o in zip(batch, soft, scored, outs):
                o_scr[gi][sco[4], :] = o / l
                l_scr[gi][sco[4], :] = jnp.broadcast_to(m + jnp.log(l), (ATTN_Q, HEAD))
        l0, l1, l2 = l_scr[0][...], l_scr[1][...], l_scr[2][...]
        m = jnp.maximum(jnp.maximum(l0, l1), l2)
        w0, w1, w2 = jnp.exp(l0 - m), jnp.exp(l1 - m), jnp.exp(l2 - m)
        tot = w0 + w1 + w2
        att_ref[...] = ((w0 * o_scr[0][...] + w1 * o_scr[1][...] + w2 * o_scr[2][...]) / tot).astype(BF16)
        lse_ref[...] = m + jnp.log(tot)

    def seg(i):
        return pl.BlockSpec((None, s, HEAD), lambda h: (i, 0, h))

    col = pl.BlockSpec((s, HEAD), lambda h: (0, h))
    return pl.pallas_call(
        body, out_shape=[jax.ShapeDtypeStruct((s, dm), BF16), jax.ShapeDtypeStruct((s, dm), F32)], grid=(dm // HEAD,),
        in_specs=[pl.BlockSpec(memory_space=pltpu.SMEM), seg(0), seg(1), seg(2)], out_specs=[col, col],
        scratch_shapes=[pltpu.VMEM((s, HEAD), F32)] * (2 * len(DILATIONS)),
        compiler_params=_params(("parallel",)), name="attn_fwd")(slopes, qkv, qkv, qkv)


def _attn_bwd(qkv, datt, att, lse, slopes):
    _, s, dm = qkv.shape
    units = _attn_units(s)

    def body(sl_ref, q_ref, k_ref, v_ref, do_ref, att_ref, lse_ref, dq_ref, dk_ref, dv_ref, dq_scr, dk_scr, dv_scr, dl_scr):
        slope = sl_ref[pl.program_id(0)]
        delta = jnp.sum(do_ref[...] * att_ref[...].astype(F32), axis=-1, keepdims=True)
        dl_scr[...] = jnp.broadcast_to(delta, (s, HEAD))
        dq_scr[...] = jnp.zeros_like(dq_scr)
        dk_scr[...] = jnp.zeros_like(dk_scr)
        dv_scr[...] = jnp.zeros_like(dv_scr)
        for first in range(0, len(units), ATTN_BATCH):
            scored = [_attn_scores(q_ref, k_ref, slope, d, r, q0, k0, nk) for _, d, r, q0, k0, nk in units[first:first + ATTN_BATCH]]
            dobs = [do_ref[sco[4], :].astype(BF16) for sco in scored]
            dps = [lax.dot_general(dob, v_ref[sco[5], :].astype(BF16), NT, preferred_element_type=F32) for dob, sco in zip(dobs, scored)]
            ps = [jnp.where(sco[1], jnp.exp(sco[0] - lse_ref[sco[4], :][:, 0:1]), 0.0) for sco in scored]
            dss = [(p * (dp - dl_scr[sco[4], :][:, 0:1]) * (HEAD ** -0.5)).astype(BF16) for p, dp, sco in zip(ps, dps, scored)]
            dqs = [lax.dot_general(ds, sco[3], NN, preferred_element_type=F32) for ds, sco in zip(dss, scored)]
            dks = [lax.dot_general(ds, sco[2], TN, preferred_element_type=F32) for ds, sco in zip(dss, scored)]
            dvs = [lax.dot_general(p.astype(BF16), dob, TN, preferred_element_type=F32) for p, dob in zip(ps, dobs)]
            for sco, dq, dk, dv in zip(scored, dqs, dks, dvs):
                dq_scr[sco[4], :] += dq
                dk_scr[sco[5], :] += dk
                dv_scr[sco[5], :] += dv
        dq_ref[...] = dq_scr[...].astype(BF16)
        dk_ref[...] = dk_scr[...].astype(BF16)
        dv_ref[...] = dv_scr[...].astype(BF16)

    def seg(i):
        return pl.BlockSpec((None, s, HEAD), lambda h: (i, 0, h))

    col = pl.BlockSpec((s, HEAD), lambda h: (0, h))
    return pl.pallas_call(
        body, out_shape=[jax.ShapeDtypeStruct((s, dm), BF16)] * 3, grid=(dm // HEAD,),
        in_specs=[pl.BlockSpec(memory_space=pltpu.SMEM), seg(0), seg(1), seg(2), col, col, col], out_specs=[col, col, col],
        scratch_shapes=[pltpu.VMEM((s, HEAD), F32)] * 4,
        compiler_params=_params(("parallel",)), name="attn_bwd")(slopes, qkv, qkv, qkv, datt, att, lse)


VEC_CB, VEC_BA, VEC_BX, VEC_LAM = 0, 1, 2, 3


def _to_3d(ref3, val):
    lw = val.shape[1] // SUBLANES
    for j in range(SUBLANES):
        ref3[:, j, :] = val[:, j * lw:(j + 1) * lw]


def _from_3d(ref3):
    return jnp.concatenate([ref3[:, j, :] for j in range(SUBLANES)], axis=1)


def _softplus(z):
    return jnp.maximum(z, 0.0) + jnp.log1p(jnp.exp(-jnp.abs(z)))


def _gate_math(xc, wa_ref, wx_ref, vec):
    xcb = xc.astype(BF16)
    nh = xc.shape[1] // HEAD
    pre_a = jnp.concatenate([jnp.dot(xcb[:, h * HEAD:(h + 1) * HEAD], wa_ref[h], preferred_element_type=F32) for h in range(nh)], axis=1)
    pre_x = jnp.concatenate([jnp.dot(xcb[:, h * HEAD:(h + 1) * HEAD], wx_ref[h], preferred_element_type=F32) for h in range(nh)], axis=1)
    ra = _sigmoid(pre_a + vec[VEC_BA:VEC_BA + 1])
    ig = _sigmoid(pre_x + vec[VEC_BX:VEC_BX + 1])
    sp = _softplus(-vec[VEC_LAM:VEC_LAM + 1])
    log_a = -LRU_C * ra * sp
    a = jnp.exp(log_a)
    z = 2.0 * log_a
    one_minus_a2 = jnp.where(z > -0.01, -z * (1.0 + z * (0.5 + z * (1.0 / 6.0))), 1.0 - jnp.exp(z))
    mult = jnp.sqrt(one_minus_a2)
    return dict(xcb=xcb, ra=ra, ig=ig, sp=sp, a=a, mult=mult)


def _conv_pad_prev(pad_ref, cur, halo, first):
    pad_ref[0:SUBLANES, :] = jnp.where(first, 0.0, halo)
    pad_ref[SUBLANES:SUBLANES + cur.shape[0], :] = cur


def _gates_fwd(rest, cw8, vec8, wa, wx):
    _, s, d = rest.shape
    lw = d // SUBLANES
    hb = ROWS // SUBLANES

    def body(x_ref, halo_ref, cw_ref, vec_ref, wa_ref, wx_ref, a_ref, u_ref, xc_ref, pad):
        _conv_pad_prev(pad, x_ref[...], halo_ref[...], pl.program_id(0) == 0)
        vec = vec_ref[...]
        xc = vec[VEC_CB:VEC_CB + 1]
        for k in range(CONV_TAPS):
            xc = xc + cw_ref[k:k + 1, :] * pad[pl.ds(SUBLANES - (CONV_TAPS - 1) + k, ROWS), :]
        gm = _gate_math(xc, wa_ref, wx_ref, vec)
        xc_ref[...] = xc
        _to_3d(a_ref, gm["a"])
        _to_3d(u_ref, gm["mult"] * (gm["ig"] * xc))

    spec3 = pl.BlockSpec((ROWS, SUBLANES, lw), lambda i: (i, 0, 0))
    wspec = pl.BlockSpec(wa.shape, lambda i: (0, 0, 0))
    return pl.pallas_call(
        body, out_shape=[jax.ShapeDtypeStruct((s, SUBLANES, lw), F32)] * 2 + [jax.ShapeDtypeStruct((s, d), F32)], grid=(s // ROWS,),
        in_specs=[pl.BlockSpec((None, ROWS, d), lambda i: (0, i, 0)),
                  pl.BlockSpec((None, SUBLANES, d), lambda i: (0, jnp.maximum(i * hb - 1, 0), 0)),
                  _vec_spec(d, SUBLANES), _vec_spec(d, SUBLANES), wspec, wspec],
        out_specs=[spec3, spec3, _row_spec(d)], scratch_shapes=[pltpu.VMEM((ROWS + SUBLANES, d), F32)],
        compiler_params=_params(("parallel",)), name="lru_gates_fwd")(rest, rest, cw8, vec8, wa, wx)


def _scan_fwd(a3, u3):
    s, _, lw = a3.shape

    def body(a_ref, u_ref, h_ref, hp_ref, carry):
        @pl.when(pl.program_id(0) == 0)
        def _():
            carry[...] = jnp.zeros_like(carry)

        def step(t, h):
            hp_ref[t] = h
            hn = a_ref[t] * h + u_ref[t]
            h_ref[t] = hn
            return hn

        carry[...] = lax.fori_loop(0, ROWS, step, carry[...], unroll=8)

    spec3 = pl.BlockSpec((ROWS, SUBLANES, lw), lambda i: (i, 0, 0))
    return pl.pallas_call(body, out_shape=[jax.ShapeDtypeStruct(a3.shape, F32)] * 2, grid=(s // ROWS,), in_specs=[spec3, spec3],
                          out_specs=[spec3, spec3], scratch_shapes=[pltpu.VMEM((SUBLANES, lw), F32)],
                          compiler_params=_params(("arbitrary",)), name="lru_scan_fwd")(a3, u3)


def _lru_out(h3, rest):
    s, _, lw = h3.shape
    d = lw * SUBLANES

    def body(h_ref, g_ref, y_ref, h2_ref):
        h = _from_3d(h_ref)
        h2_ref[...] = h
        y_ref[...] = (h * _gelu(g_ref[...])).astype(BF16)

    return pl.pallas_call(
        body, out_shape=[jax.ShapeDtypeStruct((s, d), BF16), jax.ShapeDtypeStruct((s, d), F32)], grid=(s // ROWS,),
        in_specs=[pl.BlockSpec((ROWS, SUBLANES, lw), lambda i: (i, 0, 0)), pl.BlockSpec((None, ROWS, d), lambda i: (1, i, 0))],
        out_specs=[_row_spec(d), _row_spec(d)], compiler_params=_params(("parallel",)), name="lru_out")(h3, rest)


def _scan_bwd(a3, hp3, dh):
    s, _, lw = a3.shape
    d = lw * SUBLANES
    nb = s // ROWS

    def body(a_ref, hp_ref, dh_ref, g_ref, da_ref, dh3, carry):
        @pl.when(pl.program_id(0) == 0)
        def _():
            carry[...] = jnp.zeros_like(carry)

        _to_3d(dh3, dh_ref[...])

        def step(j, c):
            t = ROWS - 1 - j
            g = dh3[t] + c
            g_ref[t] = g
            da_ref[t] = g * hp_ref[t]
            return a_ref[t] * g

        carry[...] = lax.fori_loop(0, ROWS, step, carry[...], unroll=8)

    spec3 = pl.BlockSpec((ROWS, SUBLANES, lw), lambda i: (nb - 1 - i, 0, 0))
    return pl.pallas_call(
        body, out_shape=[jax.ShapeDtypeStruct(a3.shape, F32)] * 2, grid=(nb,),
        in_specs=[spec3, spec3, pl.BlockSpec((ROWS, d), lambda i: (nb - 1 - i, 0))], out_specs=[spec3, spec3],
        scratch_shapes=[pltpu.VMEM((ROWS, SUBLANES, lw), F32), pltpu.VMEM((SUBLANES, lw), F32)],
        compiler_params=_params(("arbitrary",)), name="lru_scan_bwd")(a3, hp3, dh)


def _gates_bwd(g3, da3, xc, wa, wx, vec8):
    s, d = xc.shape
    lw = d // SUBLANES
    nh = d // HEAD

    def body(g_ref, da_ref, xc_ref, wa_ref, wx_ref, vec_ref, dxc_ref, dwa_ref, dwx_ref, dvec_ref):
        @pl.when(pl.program_id(0) == 0)
        def _():
            dwa_ref[...] = jnp.zeros_like(dwa_ref)
            dwx_ref[...] = jnp.zeros_like(dwx_ref)
            dvec_ref[...] = jnp.zeros_like(dvec_ref)

        xc_v, vec = xc_ref[...], vec_ref[...]
        du, da = _from_3d(g_ref), _from_3d(da_ref)
        gm = _gate_math(xc_v, wa_ref, wx_ref, vec)
        ra, ig, sp, a, mult = gm["ra"], gm["ig"], gm["sp"], gm["a"], gm["mult"]
        dmult = du * ig * xc_v
        dlog_a = da * a - dmult * (a * a) / mult
        dpre_a = dlog_a * (-LRU_C * sp) * ra * (1.0 - ra)
        dpre_x = du * mult * xc_v * ig * (1.0 - ig)
        dlam = jnp.sum(dlog_a * (-LRU_C * ra), axis=0, keepdims=True) * (-_sigmoid(-vec[VEC_LAM:VEC_LAM + 1]))
        dvec_ref[VEC_BA:VEC_BA + 1, :] += jnp.sum(dpre_a, axis=0, keepdims=True)
        dvec_ref[VEC_BX:VEC_BX + 1, :] += jnp.sum(dpre_x, axis=0, keepdims=True)
        dvec_ref[VEC_LAM:VEC_LAM + 1, :] += dlam
        dab, dxb, xcb = dpre_a.astype(BF16), dpre_x.astype(BF16), gm["xcb"]
        back = []
        for h in range(nh):
            cols = slice(h * HEAD, (h + 1) * HEAD)
            dwa_ref[h] += lax.dot_general(xcb[:, cols], dab[:, cols], TN, preferred_element_type=F32)
            dwx_ref[h] += lax.dot_general(xcb[:, cols], dxb[:, cols], TN, preferred_element_type=F32)
            back.append(lax.dot_general(dab[:, cols], wa_ref[h], NT, preferred_element_type=F32)
                        + lax.dot_general(dxb[:, cols], wx_ref[h], NT, preferred_element_type=F32))
        dxc_ref[...] = du * mult * ig + jnp.concatenate(back, axis=1)

    spec3 = pl.BlockSpec((ROWS, SUBLANES, lw), lambda i: (i, 0, 0))
    wspec = pl.BlockSpec(wa.shape, lambda i: (0, 0, 0))
    return pl.pallas_call(
        body, out_shape=[jax.ShapeDtypeStruct((s, d), F32), jax.ShapeDtypeStruct(wa.shape, F32), jax.ShapeDtypeStruct(wa.shape, F32),
                         jax.ShapeDtypeStruct((SUBLANES, d), F32)],
        grid=(s // ROWS,), in_specs=[spec3, spec3, _row_spec(d), wspec, wspec, _vec_spec(d, SUBLANES)],
        out_specs=[_row_spec(d), wspec, wspec, _vec_spec(d, SUBLANES)],
        compiler_params=_params(("arbitrary",)), name="lru_gates_bwd")(g3, da3, xc, wa, wx, vec8)


def _conv_bwd(dxc, rest, cw8):
    s, d = dxc.shape
    hb = ROWS // SUBLANES
    last = s // SUBLANES - 1

    def body(dc_ref, dnext_ref, x_ref, xprev_ref, cw_ref, dx_ref, dcw_ref, padd, padx):
        i = pl.program_id(0)

        @pl.when(i == 0)
        def _():
            dcw_ref[...] = jnp.zeros_like(dcw_ref)

        dc = dc_ref[...]
        padd[0:ROWS, :] = dc
        padd[ROWS:ROWS + SUBLANES, :] = jnp.where(i == pl.num_programs(0) - 1, 0.0, dnext_ref[...])
        _conv_pad_prev(padx, x_ref[...], xprev_ref[...], i == 0)
        dx = jnp.zeros_like(dc)
        for k in range(CONV_TAPS):
            dx = dx + cw_ref[k:k + 1, :] * padd[pl.ds(CONV_TAPS - 1 - k, ROWS), :]
            dcw_ref[k:k + 1, :] += jnp.sum(dc * padx[pl.ds(SUBLANES - (CONV_TAPS - 1) + k, ROWS), :], axis=0, keepdims=True)
        dcw_ref[CONV_TAPS:CONV_TAPS + 1, :] += jnp.sum(dc, axis=0, keepdims=True)
        dx_ref[...] = dx.astype(BF16)

    return pl.pallas_call(
        body, out_shape=[jax.ShapeDtypeStruct((s, d), BF16), jax.ShapeDtypeStruct((SUBLANES, d), F32)], grid=(s // ROWS,),
        in_specs=[_row_spec(d), pl.BlockSpec((SUBLANES, d), lambda i: (jnp.minimum((i + 1) * hb, last), 0)),
                  pl.BlockSpec((None, ROWS, d), lambda i: (0, i, 0)),
                  pl.BlockSpec((None, SUBLANES, d), lambda i: (0, jnp.maximum(i * hb - 1, 0), 0)), _vec_spec(d, SUBLANES)],
        out_specs=[_row_spec(d), _vec_spec(d, SUBLANES)],
        scratch_shapes=[pltpu.VMEM((ROWS + SUBLANES, d), F32), pltpu.VMEM((ROWS + SUBLANES, d), F32)],
        compiler_params=_params(("arbitrary",)), name="lru_conv_bwd")(dxc, dxc, rest, rest, cw8)


def _coords():
    return lax.axis_index("x"), lax.axis_index("y"), lax.axis_index("c")


def _other_chips(x, y):
    return [(1 - x, y), (x, 1 - y), (1 - x, 1 - y)]


def _slab(ref, kind, shard_shape, idx, half=None):
    r, c = shard_shape
    r0, nr = (0, r) if half is None else (half * (r // 2), r // 2)
    if kind == "col":
        return ref.at[pl.ds(r0, nr), pl.ds(pl.multiple_of(idx * c, LANES), c)]
    if kind == "row":
        return ref.at[pl.ds(pl.multiple_of(idx * r, SUBLANES) + r0, nr), :]
    return ref.at[idx, pl.ds(r0, nr), :]


def _full_shape(shard_shape, kind):
    r, c = shard_shape
    return {"col": (r, c * N_DEV), "row": (r * N_DEV, c), "slot": (N_DEV, r, c)}[kind]


def _handshake(peers):
    barrier = pltpu.get_barrier_semaphore()
    for peer in peers:
        pl.semaphore_signal(barrier, inc=1, device_id=peer, device_id_type=MESH)
    pl.semaphore_wait(barrier, len(peers))


def _launch(name, body, out_shape, operands, sems, sequencer_id):
    if sequencer_id is None:
        return pl.pallas_call(body, out_shape=out_shape, in_specs=[HBM] * len(operands), out_specs=[HBM] * len(out_shape),
                              scratch_shapes=sems, name=name)(*operands)
    return pl.kernel(body, out_type=out_shape, mesh=plsc.ScalarSubcoreMesh(axis_name="seq", num_cores=1), name=name,
                     scratch_types=sems, compiler_params=pltpu.CompilerParams(collective_id=sequencer_id))(*operands)


AG_COPIES = 10


def _all_gather(name, shards, kinds, sequencer_id=None):
    n = len(shards)
    shapes = [s.shape for s in shards]

    def body(*refs):
        ins, outs = refs[:n], refs[n:2 * n]
        send_sems, recv_sems, local_sems = refs[2 * n:]
        x, y, c = _coords()
        me, sib, xn, yn, dg = (x, y, c), (x, y, 1 - c), (1 - x, y, c), (x, 1 - y, c), (1 - x, 1 - y, c)
        if sequencer_id is not None:
            _handshake([sib, xn, yn])

        def part(i, dev, half=None):
            return _slab(outs[i], kinds[i], shapes[i], 4 * dev[0] + 2 * dev[1] + dev[2], half)

        def copy(i, k, block, half, to, own=False):
            r = shapes[i][0]
            src = part(i, block, half) if not own else (ins[i] if half is None else ins[i].at[pl.ds(half * (r // 2), r // 2), :])
            return pltpu.make_async_remote_copy(
                src_ref=src, dst_ref=part(i, block, half), send_sem=send_sems.at[AG_COPIES * i + k],
                recv_sem=recv_sems.at[AG_COPIES * i + k], device_id=to, device_id_type=MESH)

        def other_core(dev):
            return (dev[0], dev[1], 1 - c)

        started = []

        def start(cp):
            cp.start()
            started.append(cp)

        for i in range(n):
            start(copy(i, 1, me, 0, xn, own=True))
            start(copy(i, 4, me, 1, yn, own=True))
            start(copy(i, 2, me, 1, xn, own=True))
            start(copy(i, 3, me, 0, yn, own=True))
            start(copy(i, 0, me, None, sib, own=True))
        mine = [pltpu.make_async_copy(ins[i], part(i, me), local_sems.at[i]) for i in range(n)]
        for cp in mine:
            cp.start()
        for i in range(n):
            copy(i, 1, xn, 0, me).wait_recv()
            start(copy(i, 5, xn, 0, yn))
            copy(i, 4, yn, 1, me).wait_recv()
            start(copy(i, 6, yn, 1, xn))
        for i in range(n):
            copy(i, 2, xn, 1, me).wait_recv()
            start(copy(i, 7, xn, None, sib))
            copy(i, 3, yn, 0, me).wait_recv()
            start(copy(i, 8, yn, None, sib))
        for i in range(n):
            copy(i, 5, dg, 0, me).wait_recv()
            copy(i, 6, dg, 1, me).wait_recv()
            start(copy(i, 9, dg, None, sib))
        for i in range(n):
            copy(i, 0, sib, None, me).wait_recv()
            for k, dev in ((7, xn), (8, yn), (9, dg)):
                copy(i, k, other_core(dev), None, me).wait_recv()
        for cp in started:
            cp.wait_send()
        for cp in mine:
            cp.wait()

    out_shape = [jax.ShapeDtypeStruct(_full_shape(s.shape, k), s.dtype) for s, k in zip(shards, kinds)]
    sems = [pltpu.SemaphoreType.DMA((AG_COPIES * n,)), pltpu.SemaphoreType.DMA((AG_COPIES * n,)), pltpu.SemaphoreType.DMA((n,))]
    return _launch(name, body, out_shape, shards, sems, sequencer_id)


def _exchange_siblings(name, partials, kinds, shard_shapes, sequencer_id=None):
    n = len(partials)

    def body(*refs):
        ins, outs = refs[:n], refs[n:2 * n]
        send_sems, recv_sems = refs[2 * n:]
        x, y, c = _coords()
        if sequencer_id is not None:
            _handshake([(x, y, 1 - c)])
        cps = []
        for i in range(n):
            for q in range(N_CHIP):
                cps.append(pltpu.make_async_remote_copy(
                    src_ref=_slab(ins[i], kinds[i], shard_shapes[i], 2 * q + (1 - c)), dst_ref=outs[i].at[q],
                    send_sem=send_sems.at[N_CHIP * i + q], recv_sem=recv_sems.at[N_CHIP * i + q],
                    device_id=(x, y, 1 - c), device_id_type=MESH))
        for cp in cps:
            cp.start()
        for cp in cps:
            cp.wait()

    return _launch(name, body, [jax.ShapeDtypeStruct((N_CHIP, *s), BF16) for s in shard_shapes], partials,
                   [pltpu.SemaphoreType.DMA((N_CHIP * n,)), pltpu.SemaphoreType.DMA((N_CHIP * n,))], sequencer_id)


def _exchange_chips(name, chip_sums, sequencer_id=None):
    n = len(chip_sums)

    def body(*refs):
        ins, outs = refs[:n], refs[n:2 * n]
        send_sems, recv_sems = refs[2 * n:]
        x, y, c = _coords()
        if sequencer_id is not None:
            _handshake([(cx, cy, c) for cx, cy in _other_chips(x, y)])
        cps = []
        for i in range(n):
            for k, (cx, cy) in enumerate(_other_chips(x, y)):
                cps.append(pltpu.make_async_remote_copy(
                    src_ref=ins[i].at[2 * cx + cy], dst_ref=outs[i].at[k], send_sem=send_sems.at[3 * i + k],
                    recv_sem=recv_sems.at[3 * i + k], device_id=(cx, cy, c), device_id_type=MESH))
        for cp in cps:
            cp.start()
        for cp in cps:
            cp.wait()

    return _launch(name, body, [jax.ShapeDtypeStruct((3, *t.shape[1:]), BF16) for t in chip_sums], chip_sums,
                   [pltpu.SemaphoreType.DMA((3 * n,)), pltpu.SemaphoreType.DMA((3 * n,))], sequencer_id)


def _all_reduce_small(packed):
    rows = packed.shape[0] // N_DEV

    def body(p_ref, out_ref, rb, tot, send_sems, recv_sems):
        x, y, c = _coords()
        me = 4 * x + 2 * y + c

        def peer(k):
            return (x ^ (k >> 2), y ^ ((k >> 1) & 1), c ^ (k & 1))

        def rows_of(idx):
            return pl.ds(pl.multiple_of(idx * rows, SUBLANES), rows)

        def piece(ref, idx):
            return ref.at[rows_of(idx), :]

        scatter = [pltpu.make_async_remote_copy(src_ref=piece(p_ref, me ^ k), dst_ref=rb.at[k], send_sem=send_sems.at[k],
                                                recv_sem=recv_sems.at[k], device_id=peer(k), device_id_type=MESH) for k in range(1, N_DEV)]
        for cp in scatter:
            cp.start()
        acc = p_ref[rows_of(me), :]
        for cp in scatter:
            cp.wait_recv()
        for k in range(1, N_DEV):
            acc = acc + rb[k]
        tot[...] = acc
        out_ref[rows_of(me), :] = acc
        gather = [pltpu.make_async_remote_copy(src_ref=tot, dst_ref=piece(out_ref, me), send_sem=send_sems.at[N_DEV + k],
                                               recv_sem=recv_sems.at[N_DEV + k], device_id=peer(k), device_id_type=MESH)
                  for k in range(1, N_DEV)]
        for cp in gather:
            cp.start()
        for k in range(1, N_DEV):
            pltpu.make_async_remote_copy(src_ref=tot, dst_ref=piece(out_ref, me ^ k), send_sem=send_sems.at[N_DEV + k],
                                         recv_sem=recv_sems.at[N_DEV + k], device_id=peer(k), device_id_type=MESH).wait_recv()
        for cp in scatter + gather:
            cp.wait_send()

    vm = pl.BlockSpec(memory_space=pltpu.VMEM)
    return pl.pallas_call(
        body, out_shape=jax.ShapeDtypeStruct(packed.shape, F32), in_specs=[vm], out_specs=vm,
        scratch_shapes=[pltpu.VMEM((N_DEV, rows, LANES), F32), pltpu.VMEM((rows, LANES), F32),
                        pltpu.SemaphoreType.DMA((2 * N_DEV,)), pltpu.SemaphoreType.DMA((2 * N_DEV,))],
        compiler_params=pltpu.CompilerParams(vmem_limit_bytes=VMEM_LIMIT), name="all_reduce_small")(packed)


def _adamw_math(g, w, m, v):
    m = ADAM_B1 * m + (1.0 - ADAM_B1) * g
    v = ADAM_B2 * v + (1.0 - ADAM_B2) * (g * g)
    delta = -ADAM_LR * ((m / ADAM_C1) / (jnp.sqrt(v / ADAM_C2) + ADAM_EPS) + ADAM_WD * w)
    return delta, m, v


def _slab_spec(kind, shard_shape, tr, slab_of):
    r, c = shard_shape
    if kind == "col":
        return pl.BlockSpec((tr, c), lambda q, i, sc: (i, slab_of(q, sc)))
    return pl.BlockSpec((tr, c), lambda q, i, sc: (slab_of(q, sc) * (r // tr) + i, 0))


def _chip_sum(name, partial, recv, kind, shard_shape, core):
    r, c = shard_shape
    tr = _blk(r, 256)

    def body(core_ref, p_ref, r_ref, o_ref):
        o_ref[...] = (p_ref[...].astype(F32) + r_ref[...].astype(F32)).astype(BF16)

    spec4 = pl.BlockSpec((None, tr, c), lambda q, i, sc: (q, i, 0))
    grid_spec = pltpu.PrefetchScalarGridSpec(
        num_scalar_prefetch=1, grid=(N_CHIP, r // tr),
        in_specs=[_slab_spec(kind, shard_shape, tr, lambda q, sc: 2 * q + sc[0]), spec4], out_specs=spec4)
    return pl.pallas_call(body, out_shape=jax.ShapeDtypeStruct((N_CHIP, r, c), BF16), grid_spec=grid_spec,
                          compiler_params=_params(("parallel", "parallel")), name=name)(core, partial, recv)


def _adamw_shard(name, parts, w, m, v, chip):
    r, c = w.shape
    n_parts = len(parts)
    tr = _blk(r // n_parts, 128)
    per = r // n_parts // tr

    def body(chip_ref, *refs):
        src, (w_ref, m_ref, v_ref), (g_out, d_out, m_out, v_out) = refs[:2 * n_parts], refs[2 * n_parts:2 * n_parts + 3], refs[2 * n_parts + 3:]
        for p in range(n_parts):
            @pl.when(pl.program_id(0) // per == p)
            def _():
                g = src[2 * p][...].astype(F32)
                for k in range(3):
                    g = g + src[2 * p + 1][k].astype(F32)
                g_out[...] = g
                d_out[...], m_out[...], v_out[...] = _adamw_math(g, w_ref[...], m_ref[...], v_ref[...])

    def part_specs(p):
        at = lambda i: jnp.clip(i - p * per, 0, per - 1)
        return [pl.BlockSpec((None, tr, c), lambda i, sc: (sc[0], at(i), 0)), pl.BlockSpec((3, tr, c), lambda i, sc: (0, at(i), 0))]

    blk = pl.BlockSpec((tr, c), lambda i, sc: (i, 0))
    grid_spec = pltpu.PrefetchScalarGridSpec(
        num_scalar_prefetch=1, grid=(r // tr,), in_specs=[s for p in range(n_parts) for s in part_specs(p)] + [blk, blk, blk], out_specs=[blk] * 4)
    return pl.pallas_call(body, out_shape=[jax.ShapeDtypeStruct((r, c), F32)] * 4, grid_spec=grid_spec,
                          compiler_params=_params(("parallel",)), name=name)(chip, *[a for p in parts for a in p], w, m, v)


def _adamw_small(name, g, w, m, v):
    def body(g_ref, w_ref, m_ref, v_ref, d_out, m_out, v_out):
        d_out[...], m_out[...], v_out[...] = _adamw_math(g_ref[...], w_ref[...], m_ref[...], v_ref[...])

    vm = pl.BlockSpec(memory_space=pltpu.VMEM)
    return pl.pallas_call(body, out_shape=[jax.ShapeDtypeStruct(g.shape, F32)] * 3, in_specs=[vm] * 4, out_specs=[vm] * 3,
                          compiler_params=pltpu.CompilerParams(vmem_limit_bytes=VMEM_LIMIT), name=name)(g, w, m, v)


def _pack_rows(arrays, total_rows):
    flat = [a.reshape(-1, LANES) for a in arrays]
    used = sum(f.shape[0] for f in flat)
    return jnp.concatenate(flat + [jnp.zeros((total_rows - used, LANES), F32)], axis=0)


def _unpack_rows(packed, like):
    out, at = [], 0
    for a in like:
        n = a.size // LANES
        out.append(packed[at:at + n].reshape(a.shape))
        at += n
    return out


def kernel(x, norm_mix_g, w_in, conv_w, conv_b, lru_wa, lru_ba, lru_wx, lru_bx, lru_lambda, w_proj_attn, w_proj_lru, w_out, norm_mlp_g, w_up, w_down, norm_final_g, loss_target, m_norm_mix_g, m_w_in, m_conv_w, m_conv_b, m_lru_wa, m_lru_ba, m_lru_wx, m_lru_bx, m_lru_lambda, m_w_proj_attn, m_w_proj_lru, m_w_out, m_norm_mlp_g, m_w_up, m_w_down, m_norm_final_g, v_norm_mix_g, v_w_in, v_conv_w, v_conv_b, v_lru_wa, v_lru_ba, v_lru_wx, v_lru_bx, v_lru_lambda, v_w_proj_attn, v_w_proj_lru, v_w_out, v_norm_mlp_g, v_w_up, v_w_down, v_norm_final_g):
    xs, tgt = x[0], loss_target[0]
    s, d = xs.shape
    nh = d // HEAD
    ix, iy, ic = _coords()
    core = jnp.reshape(ic, (1,)).astype(jnp.int32)
    chip = jnp.reshape(2 * ix + iy, (1,)).astype(jnp.int32)
    dev = 4 * ix + 2 * iy + ic

    big = [w_in[0], w_proj_attn[0], w_proj_lru[0], w_out[0], w_up[0], w_down[0]]
    big_m = [m_w_in[0], m_w_proj_attn[0], m_w_proj_lru[0], m_w_out[0], m_w_up[0], m_w_down[0]]
    big_v = [v_w_in[0], v_w_proj_attn[0], v_w_proj_lru[0], v_w_out[0], v_w_up[0], v_w_down[0]]
    kinds = ["col", "row", "row", "row", "col", "row"]
    pad_taps = lambda t: jnp.pad(t, ((0, SUBLANES - CONV_TAPS), (0, 0)))
    pad_taps2 = lambda t: jnp.pad(t, ((0, 2 * SUBLANES - CONV_TAPS), (0, 0)))
    shards = [w.astype(BF16) for w in big]
    win, cw_slots = _all_gather("all_gather_w_in", [shards[0], pad_taps2(conv_w[0])], ["col", "slot"])
    later = lax.optimization_barrier((shards[1:], win))[0]
    wpa, wpl, wout = _all_gather("all_gather_mix", later[:3], kinds[1:4], sequencer_id=1)
    wup, wdown = _all_gather("all_gather_mlp", later[3:], kinds[4:], sequencer_id=5)
    cw8 = jnp.transpose(cw_slots[:, :SUBLANES], (1, 0, 2)).reshape(SUBLANES, d)
    row_id = lax.broadcasted_iota(jnp.int32, (SUBLANES, d), 0)
    vec8 = sum(jnp.where(row_id == k, t, 0.0) for k, t in ((VEC_CB, conv_b), (VEC_BA, lru_ba), (VEC_BX, lru_bx), (VEC_LAM, lru_lambda)))
    wa16, wx16 = lru_wa[0].astype(BF16), lru_wx[0].astype(BF16)
    slopes = 2.0 ** (-8.0 * jnp.arange(1, nh + 1, dtype=F32) / nh)

    def seg_specs(*segs):
        return lambda bm, bn: [pl.BlockSpec((None, bm, bn), (lambda i, j, kk, sg=sg: (sg, i, j))) for sg in segs]

    def plain_specs(k):
        return lambda bm, bn: [pl.BlockSpec((bm, bn), lambda i, j, kk: (i, j)) for _ in range(k)]

    xn = _rms_fwd("norm_mix", xs, norm_mix_g)
    qkv = _mm_fwd("proj_qkv", xn, win, 0, 3 * d, [F32], seg_out=d)[0]
    rest = _mm_fwd("proj_rest", xn, win, 3 * d, 4 * d, [F32], seg_out=d)[0]
    att, lse = _attn_fwd(qkv, slopes)
    a3, u3, xc = _gates_fwd(rest, cw8, vec8, wa16, wx16)
    h3, hp3 = _scan_fwd(a3, u3)
    ylru, h2d = _lru_out(h3, rest)
    pa = _mm_fwd("proj_attn", att, wpa, 0, d, [F32])[0]

    def merge(acc, pa_b, ga, gl):
        return acc, _sigmoid(ga) * pa_b + _sigmoid(gl) * acc

    plr, merged = _mm_fwd("proj_lru_merge", ylru, wpl, 0, d, [F32, BF16], merge, (pa, rest, rest),
                          lambda bm, bn: plain_specs(1)(bm, bn) + seg_specs(2, 3)(bm, bn), bm=512)
    h1 = _mm_fwd("mix_out", merged, wout, 0, d, [F32], lambda acc, r: (acc + r,), (xs,), plain_specs(1))[0]
    hn = _rms_fwd("norm_mlp", h1, norm_mlp_g)

    def relu2(acc):
        return acc, jnp.square(jnp.maximum(acc, 0.0))

    up, hid = _mm_fwd("mlp_up", hn, wup, 0, wup.shape[1], [BF16, BF16], relu2)
    h2 = _mm_fwd("mlp_down", hid, wdown, 0, d, [F32], lambda acc, r: (acc + r,), (h1,), plain_specs(1))[0]
    dh2, dh2b, dg3, loss_lanes = _final_loss(h2, tgt, norm_final_g.reshape(1, d))
    loss = lax.psum(0.5 / d * jnp.sum(loss_lanes), ("x", "y", "c"))

    def reduce_group(tag, kk, shp, partials, sequencer_id):
        from_sibling = _exchange_siblings(f"rs_siblings_{tag}", partials, kk, shp)
        sums = [_chip_sum(f"chip_sum_{tag}_{i}", p, f, k, sh, core) for i, (p, f, k, sh) in enumerate(zip(partials, from_sibling, kk, shp))]
        return list(zip(sums, _exchange_chips(f"rs_chips_{tag}", sums, sequencer_id)))

    dup = _mm_nt("mlp_down_dx", dh2b, wdown, [BF16], lambda acc, u: (acc * (2.0 * jnp.maximum(u.astype(F32), 0.0)),), (up,), plain_specs(1))[0]
    g_wdown = _mm_tn("mlp_down_dw", hid, dh2b)
    dhn = _mm_nt("mlp_up_dx", dup, wup, [F32])[0]
    g_wup = _mm_tn("mlp_up_dw", hn, dup)
    red_up, red_down = reduce_group("mlp", kinds[4:], [w.shape for w in big[4:]], [g_wup, g_wdown], 2)
    dhn = lax.optimization_barrier((dhn, red_up[0], red_down[0]))[0]
    dh1, dh1b, dg2 = _rms_bwd("norm_mlp_bwd", h1, norm_mlp_g, dhn, dh2)

    def merge_bwd(acc, pa_b, pl_b, ga, gl):
        sa, sl = _sigmoid(ga), _sigmoid(gl)
        return acc * sa, acc * sl, acc * pa_b * sa * (1.0 - sa), acc * pl_b * sl * (1.0 - sl)

    dpa, dpl, dga, dgl = _mm_nt("mix_out_dx", dh1b, wout, [BF16] * 4, merge_bwd, (pa, plr, rest, rest),
                                lambda bm, bn: plain_specs(2)(bm, bn) + seg_specs(2, 3)(bm, bn), bm=512)
    g_wout = _mm_tn("mix_out_dw", merged, dh1b)
    datt = _mm_nt("proj_attn_dx", dpa, wpa, [F32])[0]
    g_wpa = _mm_tn("proj_attn_dw", att, dpa)

    def lru_out_bwd(acc, h_b, gate):
        return acc * _gelu(gate), acc * h_b * _gelu_grad(gate)

    dh, dxg = _mm_nt("proj_lru_dx", dpl, wpl, [F32, BF16], lru_out_bwd, (h2d, rest),
                     lambda bm, bn: plain_specs(1)(bm, bn) + seg_specs(1)(bm, bn), bm=512)
    g_wpl = _mm_tn("proj_lru_dw", ylru, dpl)
    red_pa, red_pl, red_out = reduce_group("mix", kinds[1:4], [w.shape for w in big[1:4]], [g_wpa, g_wpl, g_wout], 3)
    dq, dk, dv = _attn_bwd(qkv, datt, att, lse, slopes)
    g3, da3 = _scan_bwd(a3, hp3, dh)
    dxc, dwa, dwx, dvec = _gates_bwd(g3, da3, xc, wa16, wx16, vec8)
    dxr, dconv = _conv_bwd(dxc, rest, cw8)
    dproj = jnp.concatenate([dq, dk, dv, dxr, dxg, dga, dgl], axis=1)
    big_out = {i: _adamw_shard(f"adamw_{i}", [red], big[i], big_m[i], big_v[i], chip) for i, red in ((4, red_up), (5, red_down))}
    red_mix = lax.optimization_barrier(((red_pa, red_pl, red_out), (big_out[4], big_out[5])))[0]
    big_out.update({i: _adamw_shard(f"adamw_{i}", [red], big[i], big_m[i], big_v[i], chip) for i, red in zip((1, 2, 3), red_mix)})
    dproj = lax.optimization_barrier((dproj, [big_out[i] for i in range(1, 6)]))[0]
    half = (big[0].shape[0] // 2, big[0].shape[1])
    red_in = []
    for p in range(2):
        g_half = _mm_tn(f"proj_in_dw_{p}", xn, dproj, part=(p, 2))
        red_in += reduce_group(f"in_{p}", ["col"], [half], [g_half], 4 + 2 * p)
        dproj = lax.optimization_barrier((dproj, red_in[-1][0]))[0]
    dxn = _mm_nt("proj_in_dx", dproj, win, [F32])[0]
    grad_x, _, dg1 = _rms_bwd("norm_mix_bwd", xs, norm_mix_g, dxn, dh1)
    big_out[0] = _adamw_shard("adamw_0", red_in, big[0], big_m[0], big_v[0], chip)

    small_w = [norm_mix_g, conv_b, lru_wa, lru_ba, lru_wx, lru_bx, lru_lambda, norm_mlp_g, norm_final_g]
    small_m = [m_norm_mix_g, m_conv_b, m_lru_wa, m_lru_ba, m_lru_wx, m_lru_bx, m_lru_lambda, m_norm_mlp_g, m_norm_final_g]
    small_v = [v_norm_mix_g, v_conv_b, v_lru_wa, v_lru_ba, v_lru_wx, v_lru_bx, v_lru_lambda, v_norm_mlp_g, v_norm_final_g]
    small_g = [dg1, dconv[CONV_TAPS:CONV_TAPS + 1], dwa, dvec[VEC_BA:VEC_BA + 1], dwx, dvec[VEC_BX:VEC_BX + 1],
               dvec[VEC_LAM:VEC_LAM + 1], dg2, dg3, dconv[0:CONV_TAPS]]
    n_rows = sum(g.size for g in small_g) // LANES
    per_dev = -(-n_rows // (N_DEV * SUBLANES)) * SUBLANES
    total = _all_reduce_small(_pack_rows(small_g, N_DEV * per_dev))
    sw_rows = -(-(n_rows - CONV_TAPS * d // LANES) // SUBLANES) * SUBLANES
    s_delta, s_m, s_v = _adamw_small("adamw_small", total[:sw_rows], _pack_rows(small_w, sw_rows), _pack_rows(small_m, sw_rows),
                                     _pack_rows(small_v, sw_rows))
    sg = _unpack_rows(total, [w for w in small_w] + [jnp.zeros((1, CONV_TAPS, d), F32)])
    s_grad, g_cw_full = sg[:-1], sg[-1]
    s_delta, s_m, s_v = (_unpack_rows(t, small_w) for t in (s_delta, s_m, s_v))
    cshard = conv_w.shape[2]
    g_cw = lax.dynamic_slice(g_cw_full, (0, 0, dev * cshard), (1, CONV_TAPS, cshard))
    cw_delta, cw_m, cw_v = (t[:CONV_TAPS][None] for t in _adamw_small(
        "adamw_conv_w", pad_taps(g_cw[0]), pad_taps(conv_w[0]), pad_taps(m_conv_w[0]), pad_taps(v_conv_w[0])))

    names = ["norm_mix_g", "w_in", "conv_w", "conv_b", "lru_wa", "lru_ba", "lru_wx", "lru_bx", "lru_lambda", "w_proj_attn", "w_proj_lru",
             "w_out", "norm_mlp_g", "w_up", "w_down", "norm_final_g"]
    small_names = ["norm_mix_g", "conv_b", "lru_wa", "lru_ba", "lru_wx", "lru_bx", "lru_lambda", "norm_mlp_g", "norm_final_g"]
    big_names = ["w_in", "w_proj_attn", "w_proj_lru", "w_out", "w_up", "w_down"]
    res = {"conv_w": (g_cw, cw_delta, cw_m, cw_v)}
    for i, nm in enumerate(small_names):
        res[nm] = (s_grad[i], s_delta[i], s_m[i], s_v[i])
    for i, nm in enumerate(big_names):
        res[nm] = tuple(t[None] for t in big_out[i])
    return (loss, grad_x[None], *[res[nm][0] for nm in names], *[res[nm][1] for nm in names],
            *[res[nm][2] for nm in names], *[res[nm][3] for nm in names])
```

```python
import jax
import jax.numpy as jnp
from jax import lax
from jax.experimental import pallas as pl
from jax.experimental.pallas import tpu as pltpu
from jax.experimental.pallas import tpu_sc as plsc

F32, BF16 = jnp.float32, jnp.bfloat16
MESH = pl.DeviceIdType.MESH
HBM = pl.BlockSpec(memory_space=pltpu.HBM)
N_DEV = 8
N_CHIP = 4
HEAD = 128
SPAN = 128
DILATIONS = (1, 4, 16)
CONV_TAPS = 4
LRU_C = 8.0
NORM_EPS = 1e-6
LANES = 128
SUBLANES = 8
VMEM_LIMIT = 56 * 1024 * 1024
ADAM_LR, ADAM_B1, ADAM_B2, ADAM_EPS, ADAM_WD, ADAM_STEP = 0.001, 0.9, 0.999, 1e-08, 0.01, 10
ADAM_C1 = 1.0 - ADAM_B1 ** ADAM_STEP
ADAM_C2 = 1.0 - ADAM_B2 ** ADAM_STEP
NEG = -1e30


def _params(sem=None):
    return pltpu.CompilerParams(dimension_semantics=sem, vmem_limit_bytes=VMEM_LIMIT)


def _sigmoid(v):
    return 1.0 / (1.0 + jnp.exp(-v))


def _gelu(v):
    k = 0.7978845608028654
    return 0.5 * v * (1.0 + jnp.tanh(k * (v + 0.044715 * v * v * v)))


def _gelu_grad(v):
    k = 0.7978845608028654
    t = jnp.tanh(k * (v + 0.044715 * v * v * v))
    return 0.5 * (1.0 + t) + 0.5 * v * (1.0 - t * t) * k * (1.0 + 3.0 * 0.044715 * v * v)


NN = (((1,), (0,)), ((), ()))
NT = (((1,), (1,)), ((), ()))
TN = (((0,), (0,)), ((), ()))


def _mm(name, a, a_spec, b, b_spec, dn, grid, out_shapes, out_specs, acc_block, epilogue=None, extras=(), extra_specs=(), side=None):
    nk, ne, no = grid[2], len(extras), len(out_shapes)
    side_ops, side_shapes, side_copies, make_copies = side if side is not None else ((), (), 0, None)
    ns_in, ns_out = len(side_ops), len(side_shapes)

    def body(*refs):
        a_ref, b_ref = refs[0], refs[1]
        ex, side_in = refs[2:2 + ne], refs[2 + ne:2 + ne + ns_in]
        outs = refs[2 + ne + ns_in:2 + ne + ns_in + no]
        side_out = refs[2 + ne + ns_in + no:2 + ne + ns_in + no + ns_out]
        scratch = refs[2 + ne + ns_in + no + ns_out:]
        at = [pl.program_id(ax) for ax in range(3)]
        if side is not None:
            @pl.when((at[0] == 0) & (at[1] == 0) & (at[2] == 0))
            def _():
                for cp in make_copies(side_in, side_out, scratch[-2], scratch[-1]):
                    cp.start()

        part = lax.dot_general(a_ref[...], b_ref[...], dn, preferred_element_type=F32)

        def finish(acc):
            vals = epilogue(acc, *[e[...] for e in ex]) if epilogue is not None else (acc,)
            for o, v in zip(outs, vals):
                o[...] = v.astype(o.dtype)

        if nk == 1:
            finish(part)
        else:
            acc_ref, k = scratch[0], at[2]

            @pl.when(k == 0)
            def _():
                acc_ref[...] = part

            @pl.when(k > 0)
            def _():
                acc_ref[...] += part

            @pl.when(k == nk - 1)
            def _():
                finish(acc_ref[...])

        if side is not None:
            @pl.when((at[0] == grid[0] - 1) & (at[1] == grid[1] - 1) & (at[2] == grid[2] - 1))
            def _():
                for cp in make_copies(side_in, side_out, scratch[-2], scratch[-1]):
                    cp.wait()

    scratch_shapes = [pltpu.VMEM(acc_block, F32)] if nk > 1 else []
    if side is not None:
        scratch_shapes += [pltpu.SemaphoreType.DMA((side_copies,)), pltpu.SemaphoreType.DMA((side_copies,))]
    res = pl.pallas_call(
        body, out_shape=[*out_shapes, *side_shapes], grid=grid, in_specs=[a_spec, b_spec, *extra_specs, *[HBM] * ns_in],
        out_specs=[*out_specs, *[HBM] * ns_out], scratch_shapes=scratch_shapes,
        compiler_params=_params(("arbitrary",) * 3 if side is not None else ("parallel", "parallel", "arbitrary")),
        name=name)(a, b, *extras, *side_ops)
    return res if side is None else (res[:no], res[no:])


def _blk(n, pref):
    return pref if n % pref == 0 else n


def _kblk(k):
    return k if k <= 2048 else next(b for b in (2048, 1024, 512) if k % b == 0)


def _mm_fwd(name, a, w, col0, ncols, out_dtypes, epilogue=None, extras=(), extra_specs_fn=None, seg_out=None, bm=1024, bn=1024):
    m, k = a.shape
    bm, bn = _blk(m, bm), _blk(ncols, bn)
    bk = _kblk(k)
    nk = k // bk
    cb0 = col0 // bn
    grid = (m // bm, ncols // bn, nk)
    a_spec = pl.BlockSpec((bm, bk), lambda i, j, kk: (i, kk))
    b_spec = pl.BlockSpec((bk, bn), lambda i, j, kk: (kk, cb0 + j))
    if seg_out is None:
        shapes = [jax.ShapeDtypeStruct((m, ncols), dt) for dt in out_dtypes]
        specs = [pl.BlockSpec((bm, bn), lambda i, j, kk: (i, j)) for _ in out_dtypes]
    else:
        per = seg_out // bn
        shapes = [jax.ShapeDtypeStruct((ncols // seg_out, m, seg_out), dt) for dt in out_dtypes]
        specs = [pl.BlockSpec((None, bm, bn), lambda i, j, kk: (j // per, i, j % per)) for _ in out_dtypes]
    ex_specs = extra_specs_fn(bm, bn) if extra_specs_fn else ()
    return _mm(name, a, a_spec, w, b_spec, NN, grid, shapes, specs, (bm, bn), epilogue, extras, ex_specs)


def _mm_nt(name, a, w, out_dtypes, epilogue=None, extras=(), extra_specs_fn=None, part=(0, 1), side=None, bm=1024, bn=1024):
    m, k = a.shape
    n = w.shape[0]
    m = m // part[1]
    bm, bn = _blk(m, bm), _blk(n, bn)
    bk = _kblk(k)
    grid = (m // bm, n // bn, k // bk)
    i0 = part[0] * (m // bm)
    a_spec = pl.BlockSpec((bm, bk), lambda i, j, kk: (i0 + i, kk))
    b_spec = pl.BlockSpec((bn, bk), lambda i, j, kk: (j, kk))
    shapes = [jax.ShapeDtypeStruct((m, n), dt) for dt in out_dtypes]
    specs = [pl.BlockSpec((bm, bn), lambda i, j, kk: (i, j)) for _ in out_dtypes]
    ex_specs = extra_specs_fn(bm, bn) if extra_specs_fn else ()
    return _mm(name, a, a_spec, w, b_spec, NT, grid, shapes, specs, (bm, bn), epilogue, extras, ex_specs, side)


def _mm_tn(name, a, b, part=(0, 1), side=None, bm=1024, bn=1024):
    t, m = a.shape
    n = b.shape[1]
    m = m // part[1]
    bm, bn = _blk(m, bm), _blk(n, bn)
    grid = (m // bm, n // bn, 1)
    i0 = part[0] * (m // bm)
    a_spec = pl.BlockSpec((t, bm), lambda i, j, kk: (0, i0 + i))
    b_spec = pl.BlockSpec((t, bn), lambda i, j, kk: (0, j))
    res = _mm(name, a, a_spec, b, b_spec, TN, grid, [jax.ShapeDtypeStruct((m, n), BF16)],
              [pl.BlockSpec((bm, bn), lambda i, j, kk: (i, j))], (bm, bn), side=side)
    return res[0] if side is None else (res[0][0], res[1])


ROWS = 256


def _row_spec(d):
    return pl.BlockSpec((ROWS, d), lambda i: (i, 0))


def _vec_spec(d, rows=1):
    return pl.BlockSpec((rows, d), lambda i: (0, 0))


def _rms_fwd(name, x, g):
    s, d = x.shape

    def body(x_ref, g_ref, o_ref):
        xv = x_ref[...]
        r = lax.rsqrt(jnp.mean(xv * xv, axis=-1, keepdims=True) + NORM_EPS)
        o_ref[...] = (xv * r * g_ref[...]).astype(BF16)

    return pl.pallas_call(body, out_shape=jax.ShapeDtypeStruct((s, d), BF16), grid=(s // ROWS,),
                          in_specs=[_row_spec(d), _vec_spec(d)], out_specs=_row_spec(d),
                          compiler_params=_params(("parallel",)), name=name)(x, g)


def _rms_bwd_math(xv, g, dy):
    r = lax.rsqrt(jnp.mean(xv * xv, axis=-1, keepdims=True) + NORM_EPS)
    n = xv * r
    z = dy * g
    dx = r * (z - n * jnp.mean(z * n, axis=-1, keepdims=True))
    return dx, jnp.sum(dy * n, axis=0, keepdims=True)


def _rms_bwd(name, x, g, dy, resid):
    s, d = x.shape

    def body(x_ref, g_ref, dy_ref, r_ref, dx_ref, dxb_ref, dg_ref):
        dx, dg = _rms_bwd_math(x_ref[...], g_ref[...], dy_ref[...])
        dx = dx + r_ref[...]
        dx_ref[...] = dx
        dxb_ref[...] = dx.astype(BF16)

        @pl.when(pl.program_id(0) == 0)
        def _():
            dg_ref[...] = jnp.zeros_like(dg_ref)

        dg_ref[...] += dg

    return pl.pallas_call(
        body, out_shape=[jax.ShapeDtypeStruct((s, d), F32), jax.ShapeDtypeStruct((s, d), BF16), jax.ShapeDtypeStruct((1, d), F32)],
        grid=(s // ROWS,), in_specs=[_row_spec(d), _vec_spec(d), _row_spec(d), _row_spec(d)],
        out_specs=[_row_spec(d), _row_spec(d), _vec_spec(d)], compiler_params=_params(("arbitrary",)), name=name)(x, g, dy, resid)


def _final_loss(h2, tgt, g):
    s, d = h2.shape

    def body(x_ref, t_ref, g_ref, dx_ref, dxb_ref, dg_ref, ls_ref):
        xv, gv = x_ref[...], g_ref[...]
        r = lax.rsqrt(jnp.mean(xv * xv, axis=-1, keepdims=True) + NORM_EPS)
        diff = xv * r * gv - t_ref[...]
        dx, dg = _rms_bwd_math(xv, gv, diff * (1.0 / d))
        dx_ref[...] = dx
        dxb_ref[...] = dx.astype(BF16)

        @pl.when(pl.program_id(0) == 0)
        def _():
            dg_ref[...] = jnp.zeros_like(dg_ref)
            ls_ref[...] = jnp.zeros_like(ls_ref)

        dg_ref[...] += dg
        ls_ref[...] += jnp.sum(diff * diff, axis=0, keepdims=True)

    return pl.pallas_call(
        body, out_shape=[jax.ShapeDtypeStruct((s, d), F32), jax.ShapeDtypeStruct((s, d), BF16),
                         jax.ShapeDtypeStruct((1, d), F32), jax.ShapeDtypeStruct((1, d), F32)],
        grid=(s // ROWS,), in_specs=[_row_spec(d), _row_spec(d), _vec_spec(d)],
        out_specs=[_row_spec(d), _row_spec(d), _vec_spec(d), _vec_spec(d)],
        compiler_params=_params(("arbitrary",)), name="final_norm_loss")(h2, tgt, g)


ATTN_Q = 128


ATTN_BATCH = 8


def _attn_units(s):
    units = []
    for gi, d in enumerate(DILATIONS):
        for r in range(d):
            for q0 in range(0, s // d, ATTN_Q):
                k0 = max(q0 - SPAN, 0)
                units.append((gi, d, r, q0, k0, q0 + ATTN_Q - k0))
    return units


def _stream_rows(d, r, start, size):
    return pl.ds(r + start * d, size) if d == 1 else pl.ds(r + start * d, size, stride=d)


def _attn_scores(q_ref, k_ref, slope, d, r, q0, k0, nk):
    qrows, krows = _stream_rows(d, r, q0, ATTN_Q), _stream_rows(d, r, k0, nk)
    qb = q_ref[qrows, :].astype(BF16)
    kb = k_ref[krows, :].astype(BF16)
    sc = lax.dot_general(qb, kb, NT, preferred_element_type=F32) * (HEAD ** -0.5)
    qi = lax.broadcasted_iota(jnp.int32, (ATTN_Q, nk), 0)
    kj = lax.broadcasted_iota(jnp.int32, (ATTN_Q, nk), 1)
    dist = (q0 - k0) + qi - kj
    valid = (dist >= 0) & (dist <= SPAN)
    sc = sc - (slope * d) * dist.astype(F32)
    return jnp.where(valid, sc, NEG), valid, qb, kb, qrows, krows


def _attn_fwd(qkv, slopes):
    _, s, dm = qkv.shape
    units = _attn_units(s)

    def body(sl_ref, q_ref, k_ref, v_ref, att_ref, lse_ref, *scr):
        o_scr, l_scr = scr[:3], scr[3:]
        slope = sl_ref[pl.program_id(0)]
        for first in range(0, len(units), ATTN_BATCH):
            batch = units[first:first + ATTN_BATCH]
            scored = [_attn_scores(q_ref, k_ref, slope, d, r, q0, k0, nk) for _, d, r, q0, k0, nk in batch]
            soft = []
            for sc, _, _, _, _, _ in scored:
                m = jnp.max(sc, axis=-1, keepdims=True)
                p = jnp.exp(sc - m)
                soft.append((m, p, jnp.sum(p, axis=-1, keepdims=True)))
            outs = [lax.dot_general(p.astype(BF16), v_ref[sco[5], :].astype(BF16), NN, preferred_element_type=F32)
                    for (m, p, l), sco in zip(soft, scored)]
            for (gi, *_), (m, p, l), sco, o in zip(batch, soft, scored, outs):
                o_scr[gi][sco[4], :] = o / l
                l_scr[gi][sco[4], :] = jnp.broadcast_to(m + jnp.log(l), (ATTN_Q, HEAD))
        l0, l1, l2 = l_scr[0][...], l_scr[1][...], l_scr[2][...]
        m = jnp.maximum(jnp.maximum(l0, l1), l2)
        w0, w1, w2 = jnp.exp(l0 - m), jnp.exp(l1 - m), jnp.exp(l2 - m)
        tot = w0 + w1 + w2
        att_ref[...] = ((w0 * o_scr[0][...] + w1 * o_scr[1][...] + w2 * o_scr[2][...]) / tot).astype(BF16)
        lse_ref[...] = m + jnp.log(tot)

    def seg(i):
        return pl.BlockSpec((None, s, HEAD), lambda h: (i, 0, h))

    col = pl.BlockSpec((s, HEAD), lambda h: (0, h))
    return pl.pallas_call(
        body, out_shape=[jax.ShapeDtypeStruct((s, dm), BF16), jax.ShapeDtypeStruct((s, dm), F32)], grid=(dm // HEAD,),
        in_specs=[pl.BlockSpec(memory_space=pltpu.SMEM), seg(0), seg(1), seg(2)], out_specs=[col, col],
        scratch_shapes=[pltpu.VMEM((s, HEAD), F32)] * (2 * len(DILATIONS)),
        compiler_params=_params(("parallel",)), name="attn_fwd")(slopes, qkv, qkv, qkv)


def _attn_bwd(qkv, datt, att, lse, slopes):
    _, s, dm = qkv.shape
    units = _attn_units(s)

    def body(sl_ref, q_ref, k_ref, v_ref, do_ref, att_ref, lse_ref, dq_ref, dk_ref, dv_ref, dq_scr, dk_scr, dv_scr, dl_scr):
        slope = sl_ref[pl.program_id(0)]
        delta = jnp.sum(do_ref[...] * att_ref[...].astype(F32), axis=-1, keepdims=True)
        dl_scr[...] = jnp.broadcast_to(delta, (s, HEAD))
        dq_scr[...] = jnp.zeros_like(dq_scr)
        dk_scr[...] = jnp.zeros_like(dk_scr)
        dv_scr[...] = jnp.zeros_like(dv_scr)
        for first in range(0, len(units), ATTN_BATCH):
            scored = [_attn_scores(q_ref, k_ref, slope, d, r, q0, k0, nk) for _, d, r, q0, k0, nk in units[first:first + ATTN_BATCH]]
            dobs = [do_ref[sco[4], :].astype(BF16) for sco in scored]
            dps = [lax.dot_general(dob, v_ref[sco[5], :].astype(BF16), NT, preferred_element_type=F32) for dob, sco in zip(dobs, scored)]
            ps = [jnp.where(sco[1], jnp.exp(sco[0] - lse_ref[sco[4], :][:, 0:1]), 0.0) for sco in scored]
            dss = [(p * (dp - dl_scr[sco[4], :][:, 0:1]) * (HEAD ** -0.5)).astype(BF16) for p, dp, sco in zip(ps, dps, scored)]
            dqs = [lax.dot_general(ds, sco[3], NN, preferred_element_type=F32) for ds, sco in zip(dss, scored)]
            dks = [lax.dot_general(ds, sco[2], TN, preferred_element_type=F32) for ds, sco in zip(dss, scored)]
            dvs = [lax.dot_general(p.astype(BF16), dob, TN, preferred_element_type=F32) for p, dob in zip(ps, dobs)]
            for sco, dq, dk, dv in zip(scored, dqs, dks, dvs):
                dq_scr[sco[4], :] += dq
                dk_scr[sco[5], :] += dk
                dv_scr[sco[5], :] += dv
        dq_ref[...] = dq_scr[...].astype(BF16)
        dk_ref[...] = dk_scr[...].astype(BF16)
        dv_ref[...] = dv_scr[...].astype(BF16)

    def seg(i):
        return pl.BlockSpec((None, s, HEAD), lambda h: (i, 0, h))

    col = pl.BlockSpec((s, HEAD), lambda h: (0, h))
    return pl.pallas_call(
        body, out_shape=[jax.ShapeDtypeStruct((s, dm), BF16)] * 3, grid=(dm // HEAD,),
        in_specs=[pl.BlockSpec(memory_space=pltpu.SMEM), seg(0), seg(1), seg(2), col, col, col], out_specs=[col, col, col],
        scratch_shapes=[pltpu.VMEM((s, HEAD), F32)] * 4,
        compiler_params=_params(("parallel",)), name="attn_bwd")(slopes, qkv, qkv, qkv, datt, att, lse)


VEC_CB, VEC_BA, VEC_BX, VEC_LAM = 0, 1, 2, 3


def _to_3d(ref3, val):
    lw = val.shape[1] // SUBLANES
    for j in range(SUBLANES):
        ref3[:, j, :] = val[:, j * lw:(j + 1) * lw]


def _from_3d(ref3):
    return jnp.concatenate([ref3[:, j, :] for j in range(SUBLANES)], axis=1)


def _softplus(z):
    return jnp.maximum(z, 0.0) + jnp.log1p(jnp.exp(-jnp.abs(z)))


def _gate_math(xc, wa_ref, wx_ref, vec):
    xcb = xc.astype(BF16)
    nh = xc.shape[1] // HEAD
    pre_a = jnp.concatenate([jnp.dot(xcb[:, h * HEAD:(h + 1) * HEAD], wa_ref[h], preferred_element_type=F32) for h in range(nh)], axis=1)
    pre_x = jnp.concatenate([jnp.dot(xcb[:, h * HEAD:(h + 1) * HEAD], wx_ref[h], preferred_element_type=F32) for h in range(nh)], axis=1)
    ra = _sigmoid(pre_a + vec[VEC_BA:VEC_BA + 1])
    ig = _sigmoid(pre_x + vec[VEC_BX:VEC_BX + 1])
    sp = _softplus(-vec[VEC_LAM:VEC_LAM + 1])
    log_a = -LRU_C * ra * sp
    a = jnp.exp(log_a)
    z = 2.0 * log_a
    one_minus_a2 = jnp.where(z > -0.01, -z * (1.0 + z * (0.5 + z * (1.0 / 6.0))), 1.0 - jnp.exp(z))
    mult = jnp.sqrt(one_minus_a2)
    return dict(xcb=xcb, ra=ra, ig=ig, sp=sp, a=a, mult=mult)


def _conv_pad_prev(pad_ref, cur, halo, first):
    pad_ref[0:SUBLANES, :] = jnp.where(first, 0.0, halo)
    pad_ref[SUBLANES:SUBLANES + cur.shape[0], :] = cur


def _gates_fwd(rest, cw8, vec8, wa, wx):
    _, s, d = rest.shape
    lw = d // SUBLANES
    hb = ROWS // SUBLANES

    def body(x_ref, halo_ref, cw_ref, vec_ref, wa_ref, wx_ref, a_ref, u_ref, xc_ref, pad):
        _conv_pad_prev(pad, x_ref[...], halo_ref[...], pl.program_id(0) == 0)
        vec = vec_ref[...]
        xc = vec[VEC_CB:VEC_CB + 1]
        for k in range(CONV_TAPS):
            xc = xc + cw_ref[k:k + 1, :] * pad[pl.ds(SUBLANES - (CONV_TAPS - 1) + k, ROWS), :]
        gm = _gate_math(xc, wa_ref, wx_ref, vec)
        xc_ref[...] = xc
        _to_3d(a_ref, gm["a"])
        _to_3d(u_ref, gm["mult"] * (gm["ig"] * xc))

    spec3 = pl.BlockSpec((ROWS, SUBLANES, lw), lambda i: (i, 0, 0))
    wspec = pl.BlockSpec(wa.shape, lambda i: (0, 0, 0))
    return pl.pallas_call(
        body, out_shape=[jax.ShapeDtypeStruct((s, SUBLANES, lw), F32)] * 2 + [jax.ShapeDtypeStruct((s, d), F32)], grid=(s // ROWS,),
        in_specs=[pl.BlockSpec((None, ROWS, d), lambda i: (0, i, 0)),
                  pl.BlockSpec((None, SUBLANES, d), lambda i: (0, jnp.maximum(i * hb - 1, 0), 0)),
                  _vec_spec(d, SUBLANES), _vec_spec(d, SUBLANES), wspec, wspec],
        out_specs=[spec3, spec3, _row_spec(d)], scratch_shapes=[pltpu.VMEM((ROWS + SUBLANES, d), F32)],
        compiler_params=_params(("parallel",)), name="lru_gates_fwd")(rest, rest, cw8, vec8, wa, wx)


def _scan_fwd(a3, u3):
    s, _, lw = a3.shape

    def body(a_ref, u_ref, h_ref, hp_ref, carry):
        @pl.when(pl.program_id(0) == 0)
        def _():
            carry[...] = jnp.zeros_like(carry)

        def step(t, h):
            hp_ref[t] = h
            hn = a_ref[t] * h + u_ref[t]
            h_ref[t] = hn
            return hn

        carry[...] = lax.fori_loop(0, ROWS, step, carry[...], unroll=8)

    spec3 = pl.BlockSpec((ROWS, SUBLANES, lw), lambda i: (i, 0, 0))
    return pl.pallas_call(body, out_shape=[jax.ShapeDtypeStruct(a3.shape, F32)] * 2, grid=(s // ROWS,), in_specs=[spec3, spec3],
                          out_specs=[spec3, spec3], scratch_shapes=[pltpu.VMEM((SUBLANES, lw), F32)],
                          compiler_params=_params(("arbitrary",)), name="lru_scan_fwd")(a3, u3)


def _lru_out(h3, rest):
    s, _, lw = h3.shape
    d = lw * SUBLANES

    def body(h_ref, g_ref, y_ref, h2_ref):
        h = _from_3d(h_ref)
        h2_ref[...] = h
        y_ref[...] = (h * _gelu(g_ref[...])).astype(BF16)

    return pl.pallas_call(
        body, out_shape=[jax.ShapeDtypeStruct((s, d), BF16), jax.ShapeDtypeStruct((s, d), F32)], grid=(s // ROWS,),
        in_specs=[pl.BlockSpec((ROWS, SUBLANES, lw), lambda i: (i, 0, 0)), pl.BlockSpec((None, ROWS, d), lambda i: (1, i, 0))],
        out_specs=[_row_spec(d), _row_spec(d)], compiler_params=_params(("parallel",)), name="lru_out")(h3, rest)


def _scan_bwd(a3, hp3, dh):
    s, _, lw = a3.shape
    d = lw * SUBLANES
    nb = s // ROWS

    def body(a_ref, hp_ref, dh_ref, g_ref, da_ref, dh3, carry):
        @pl.when(pl.program_id(0) == 0)
        def _():
            carry[...] = jnp.zeros_like(carry)

        _to_3d(dh3, dh_ref[...])

        def step(j, c):
            t = ROWS - 1 - j
            g = dh3[t] + c
            g_ref[t] = g
            da_ref[t] = g * hp_ref[t]
            return a_ref[t] * g

        carry[...] = lax.fori_loop(0, ROWS, step, carry[...], unroll=8)

    spec3 = pl.BlockSpec((ROWS, SUBLANES, lw), lambda i: (nb - 1 - i, 0, 0))
    return pl.pallas_call(
        body, out_shape=[jax.ShapeDtypeStruct(a3.shape, F32)] * 2, grid=(nb,),
        in_specs=[spec3, spec3, pl.BlockSpec((ROWS, d), lambda i: (nb - 1 - i, 0))], out_specs=[spec3, spec3],
        scratch_shapes=[pltpu.VMEM((ROWS, SUBLANES, lw), F32), pltpu.VMEM((SUBLANES, lw), F32)],
        compiler_params=_params(("arbitrary",)), name="lru_scan_bwd")(a3, hp3, dh)


def _gates_bwd(g3, da3, xc, wa, wx, vec8):
    s, d = xc.shape
    lw = d // SUBLANES
    nh = d // HEAD

    def body(g_ref, da_ref, xc_ref, wa_ref, wx_ref, vec_ref, dxc_ref, dwa_ref, dwx_ref, dvec_ref):
        @pl.when(pl.program_id(0) == 0)
        def _():
            dwa_ref[...] = jnp.zeros_like(dwa_ref)
            dwx_ref[...] = jnp.zeros_like(dwx_ref)
            dvec_ref[...] = jnp.zeros_like(dvec_ref)

        xc_v, vec = xc_ref[...], vec_ref[...]
        du, da = _from_3d(g_ref), _from_3d(da_ref)
        gm = _gate_math(xc_v, wa_ref, wx_ref, vec)
        ra, ig, sp, a, mult = gm["ra"], gm["ig"], gm["sp"], gm["a"], gm["mult"]
        dmult = du * ig * xc_v
        dlog_a = da * a - dmult * (a * a) / mult
        dpre_a = dlog_a * (-LRU_C * sp) * ra * (1.0 - ra)
        dpre_x = du * mult * xc_v * ig * (1.0 - ig)
        dlam = jnp.sum(dlog_a * (-LRU_C * ra), axis=0, keepdims=True) * (-_sigmoid(-vec[VEC_LAM:VEC_LAM + 1]))
        dvec_ref[VEC_BA:VEC_BA + 1, :] += jnp.sum(dpre_a, axis=0, keepdims=True)
        dvec_ref[VEC_BX:VEC_BX + 1, :] += jnp.sum(dpre_x, axis=0, keepdims=True)
        dvec_ref[VEC_LAM:VEC_LAM + 1, :] += dlam
        dab, dxb, xcb = dpre_a.astype(BF16), dpre_x.astype(BF16), gm["xcb"]
        back = []
        for h in range(nh):
            cols = slice(h * HEAD, (h + 1) * HEAD)
            dwa_ref[h] += lax.dot_general(xcb[:, cols], dab[:, cols], TN, preferred_element_type=F32)
            dwx_ref[h] += lax.dot_general(xcb[:, cols], dxb[:, cols], TN, preferred_element_type=F32)
            back.append(lax.dot_general(dab[:, cols], wa_ref[h], NT, preferred_element_type=F32)
                        + lax.dot_general(dxb[:, cols], wx_ref[h], NT, preferred_element_type=F32))
        dxc_ref[...] = du * mult * ig + jnp.concatenate(back, axis=1)

    spec3 = pl.BlockSpec((ROWS, SUBLANES, lw), lambda i: (i, 0, 0))
    wspec = pl.BlockSpec(wa.shape, lambda i: (0, 0, 0))
    return pl.pallas_call(
        body, out_shape=[jax.ShapeDtypeStruct((s, d), F32), jax.ShapeDtypeStruct(wa.shape, F32), jax.ShapeDtypeStruct(wa.shape, F32),
                         jax.ShapeDtypeStruct((SUBLANES, d), F32)],
        grid=(s // ROWS,), in_specs=[spec3, spec3, _row_spec(d), wspec, wspec, _vec_spec(d, SUBLANES)],
        out_specs=[_row_spec(d), wspec, wspec, _vec_spec(d, SUBLANES)],
        compiler_params=_params(("arbitrary",)), name="lru_gates_bwd")(g3, da3, xc, wa, wx, vec8)


def _conv_bwd(dxc, rest, cw8):
    s, d = dxc.shape
    hb = ROWS // SUBLANES
    last = s // SUBLANES - 1

    def body(dc_ref, dnext_ref, x_ref, xprev_ref, cw_ref, dx_ref, dcw_ref, padd, padx):
        i = pl.program_id(0)

        @pl.when(i == 0)
        def _():
            dcw_ref[...] = jnp.zeros_like(dcw_ref)

        dc = dc_ref[...]
        padd[0:ROWS, :] = dc
        padd[ROWS:ROWS + SUBLANES, :] = jnp.where(i == pl.num_programs(0) - 1, 0.0, dnext_ref[...])
        _conv_pad_prev(padx, x_ref[...], xprev_ref[...], i == 0)
        dx = jnp.zeros_like(dc)
        for k in range(CONV_TAPS):
            dx = dx + cw_ref[k:k + 1, :] * padd[pl.ds(CONV_TAPS - 1 - k, ROWS), :]
            dcw_ref[k:k + 1, :] += jnp.sum(dc * padx[pl.ds(SUBLANES - (CONV_TAPS - 1) + k, ROWS), :], axis=0, keepdims=True)
        dcw_ref[CONV_TAPS:CONV_TAPS + 1, :] += jnp.sum(dc, axis=0, keepdims=True)
        dx_ref[...] = dx.astype(BF16)

    return pl.pallas_call(
        body, out_shape=[jax.ShapeDtypeStruct((s, d), BF16), jax.ShapeDtypeStruct((SUBLANES, d), F32)], grid=(s // ROWS,),
        in_specs=[_row_spec(d), pl.BlockSpec((SUBLANES, d), lambda i: (jnp.minimum((i + 1) * hb, last), 0)),
                  pl.BlockSpec((None, ROWS, d), lambda i: (0, i, 0)),
                  pl.BlockSpec((None, SUBLANES, d), lambda i: (0, jnp.maximum(i * hb - 1, 0), 0)), _vec_spec(d, SUBLANES)],
        out_specs=[_row_spec(d), _vec_spec(d, SUBLANES)],
        scratch_shapes=[pltpu.VMEM((ROWS + SUBLANES, d), F32), pltpu.VMEM((ROWS + SUBLANES, d), F32)],
        compiler_params=_params(("arbitrary",)), name="lru_conv_bwd")(dxc, dxc, rest, rest, cw8)


def _coords():
    return lax.axis_index("x"), lax.axis_index("y"), lax.axis_index("c")


def _other_chips(x, y):
    return [(1 - x, y), (x, 1 - y), (1 - x, 1 - y)]


def _slab(ref, kind, shard_shape, idx, half=None):
    r, c = shard_shape
    r0, nr = (0, r) if half is None else (half * (r // 2), r // 2)
    if kind == "col":
        return ref.at[pl.ds(r0, nr), pl.ds(pl.multiple_of(idx * c, LANES), c)]
    if kind == "row":
        return ref.at[pl.ds(pl.multiple_of(idx * r, SUBLANES) + r0, nr), :]
    return ref.at[idx, pl.ds(r0, nr), :]


def _full_shape(shard_shape, kind):
    r, c = shard_shape
    return {"col": (r, c * N_DEV), "row": (r * N_DEV, c), "slot": (N_DEV, r, c)}[kind]


def _handshake(peers):
    barrier = pltpu.get_barrier_semaphore()
    for peer in peers:
        pl.semaphore_signal(barrier, inc=1, device_id=peer, device_id_type=MESH)
    pl.semaphore_wait(barrier, len(peers))


def _launch(name, body, out_shape, operands, sems, sequencer_id):
    if sequencer_id is None:
        return pl.pallas_call(body, out_shape=out_shape, in_specs=[HBM] * len(operands), out_specs=[HBM] * len(out_shape),
                              scratch_shapes=sems, name=name)(*operands)
    return pl.kernel(body, out_type=out_shape, mesh=plsc.ScalarSubcoreMesh(axis_name="seq", num_cores=1), name=name,
                     scratch_types=sems, compiler_params=pltpu.CompilerParams(collective_id=sequencer_id))(*operands)


AG_COPIES = 10


def _all_gather(name, shards, kinds, sequencer_id=None):
    n = len(shards)
    shapes = [s.shape for s in shards]

    def body(*refs):
        ins, outs = refs[:n], refs[n:2 * n]
        send_sems, recv_sems, local_sems = refs[2 * n:]
        x, y, c = _coords()
        me, sib, xn, yn, dg = (x, y, c), (x, y, 1 - c), (1 - x, y, c), (x, 1 - y, c), (1 - x, 1 - y, c)
        if sequencer_id is not None:
            _handshake([sib, xn, yn])

        def part(i, dev, half=None):
            return _slab(outs[i], kinds[i], shapes[i], 4 * dev[0] + 2 * dev[1] + dev[2], half)

        def copy(i, k, block, half, to, own=False):
            r = shapes[i][0]
            src = part(i, block, half) if not own else (ins[i] if half is None else ins[i].at[pl.ds(half * (r // 2), r // 2), :])
            return pltpu.make_async_remote_copy(
                src_ref=src, dst_ref=part(i, block, half), send_sem=send_sems.at[AG_COPIES * i + k],
                recv_sem=recv_sems.at[AG_COPIES * i + k], device_id=to, device_id_type=MESH)

        def other_core(dev):
            return (dev[0], dev[1], 1 - c)

        started = []

        def start(cp):
            cp.start()
            started.append(cp)

        for i in range(n):
            start(copy(i, 1, me, 0, xn, own=True))
            start(copy(i, 4, me, 1, yn, own=True))
            start(copy(i, 2, me, 1, xn, own=True))
            start(copy(i, 3, me, 0, yn, own=True))
            start(copy(i, 0, me, None, sib, own=True))
        mine = [pltpu.make_async_copy(ins[i], part(i, me), local_sems.at[i]) for i in range(n)]
        for cp in mine:
            cp.start()
        for i in range(n):
            copy(i, 1, xn, 0, me).wait_recv()
            start(copy(i, 5, xn, 0, yn))
            copy(i, 4, yn, 1, me).wait_recv()
            start(copy(i, 6, yn, 1, xn))
        for i in range(n):
            copy(i, 2, xn, 1, me).wait_recv()
            start(copy(i, 7, xn, None, sib))
            copy(i, 3, yn, 0, me).wait_recv()
            start(copy(i, 8, yn, None, sib))
        for i in range(n):
            copy(i, 5, dg, 0, me).wait_recv()
            copy(i, 6, dg, 1, me).wait_recv()
            start(copy(i, 9, dg, None, sib))
        for i in range(n):
            copy(i, 0, sib, None, me).wait_recv()
            for k, dev in ((7, xn), (8, yn), (9, dg)):
                copy(i, k, other_core(dev), None, me).wait_recv()
        for cp in started:
            cp.wait_send()
        for cp in mine:
            cp.wait()

    out_shape = [jax.ShapeDtypeStruct(_full_shape(s.shape, k), s.dtype) for s, k in zip(shards, kinds)]
    sems = [pltpu.SemaphoreType.DMA((AG_COPIES * n,)), pltpu.SemaphoreType.DMA((AG_COPIES * n,)), pltpu.SemaphoreType.DMA((n,))]
    return _launch(name, body, out_shape, shards, sems, sequencer_id)


def _sibling_copies(kinds, shard_shapes):
    def make(ins, outs, send_sems, recv_sems):
        x, y, c = _coords()
        return [pltpu.make_async_remote_copy(
            src_ref=_slab(ins[i], kinds[i], shard_shapes[i], 2 * q + (1 - c)), dst_ref=outs[i].at[q],
            send_sem=send_sems.at[N_CHIP * i + q], recv_sem=recv_sems.at[N_CHIP * i + q],
            device_id=(x, y, 1 - c), device_id_type=MESH) for i in range(len(ins)) for q in range(N_CHIP)]
    return make


def _sibling_side(partials, kinds, shard_shapes):
    return (partials, [jax.ShapeDtypeStruct((N_CHIP, *s), BF16) for s in shard_shapes], N_CHIP * len(partials),
            _sibling_copies(kinds, shard_shapes))


def _exchange_siblings(name, partials, kinds, shard_shapes, sequencer_id=None):
    n = len(partials)
    make = _sibling_copies(kinds, shard_shapes)

    def body(*refs):
        if sequencer_id is not None:
            x, y, c = _coords()
            _handshake([(x, y, 1 - c)])
        cps = make(refs[:n], refs[n:2 * n], refs[2 * n], refs[2 * n + 1])
        for cp in cps:
            cp.start()
        for cp in cps:
            cp.wait()

    return _launch(name, body, [jax.ShapeDtypeStruct((N_CHIP, *s), BF16) for s in shard_shapes], partials,
                   [pltpu.SemaphoreType.DMA((N_CHIP * n,)), pltpu.SemaphoreType.DMA((N_CHIP * n,))], sequencer_id)


def _exchange_chips(name, chip_sums, sequencer_id=None):
    n = len(chip_sums)

    def body(*refs):
        ins, outs = refs[:n], refs[n:2 * n]
        send_sems, recv_sems = refs[2 * n:]
        x, y, c = _coords()
        if sequencer_id is not None:
            _handshake([(cx, cy, c) for cx, cy in _other_chips(x, y)])
        cps = []
        for i in range(n):
            for k, (cx, cy) in enumerate(_other_chips(x, y)):
                cps.append(pltpu.make_async_remote_copy(
                    src_ref=ins[i].at[2 * cx + cy], dst_ref=outs[i].at[k], send_sem=send_sems.at[3 * i + k],
                    recv_sem=recv_sems.at[3 * i + k], device_id=(cx, cy, c), device_id_type=MESH))
        for cp in cps:
            cp.start()
        for cp in cps:
            cp.wait()

    return _launch(name, body, [jax.ShapeDtypeStruct((3, *t.shape[1:]), BF16) for t in chip_sums], chip_sums,
                   [pltpu.SemaphoreType.DMA((3 * n,)), pltpu.SemaphoreType.DMA((3 * n,))], sequencer_id)


def _all_reduce_small(packed):
    rows = packed.shape[0] // N_DEV

    def body(p_ref, out_ref, rb, tot, send_sems, recv_sems):
        x, y, c = _coords()
        me = 4 * x + 2 * y + c

        def peer(k):
            return (x ^ (k >> 2), y ^ ((k >> 1) & 1), c ^ (k & 1))

        def rows_of(idx):
            return pl.ds(pl.multiple_of(idx * rows, SUBLANES), rows)

        def piece(ref, idx):
            return ref.at[rows_of(idx), :]

        scatter = [pltpu.make_async_remote_copy(src_ref=piece(p_ref, me ^ k), dst_ref=rb.at[k], send_sem=send_sems.at[k],
                                                recv_sem=recv_sems.at[k], device_id=peer(k), device_id_type=MESH) for k in range(1, N_DEV)]
        for cp in scatter:
            cp.start()
        acc = p_ref[rows_of(me), :]
        for cp in scatter:
            cp.wait_recv()
        for k in range(1, N_DEV):
            acc = acc + rb[k]
        tot[...] = acc
        out_ref[rows_of(me), :] = acc
        gather = [pltpu.make_async_remote_copy(src_ref=tot, dst_ref=piece(out_ref, me), send_sem=send_sems.at[N_DEV + k],
                                               recv_sem=recv_sems.at[N_DEV + k], device_id=peer(k), device_id_type=MESH)
                  for k in range(1, N_DEV)]
        for cp in gather:
            cp.start()
        for k in range(1, N_DEV):
            pltpu.make_async_remote_copy(src_ref=tot, dst_ref=piece(out_ref, me ^ k), send_sem=send_sems.at[N_DEV + k],
                                         recv_sem=recv_sems.at[N_DEV + k], device_id=peer(k), device_id_type=MESH).wait_recv()
        for cp in scatter + gather:
            cp.wait_send()

    vm = pl.BlockSpec(memory_space=pltpu.VMEM)
    return pl.pallas_call(
        body, out_shape=jax.ShapeDtypeStruct(packed.shape, F32), in_specs=[vm], out_specs=vm,
        scratch_shapes=[pltpu.VMEM((N_DEV, rows, LANES), F32), pltpu.VMEM((rows, LANES), F32),
                        pltpu.SemaphoreType.DMA((2 * N_DEV,)), pltpu.SemaphoreType.DMA((2 * N_DEV,))],
        compiler_params=pltpu.CompilerParams(vmem_limit_bytes=VMEM_LIMIT), name="all_reduce_small")(packed)


def _adamw_math(g, w, m, v):
    m = ADAM_B1 * m + (1.0 - ADAM_B1) * g
    v = ADAM_B2 * v + (1.0 - ADAM_B2) * (g * g)
    delta = -ADAM_LR * ((m / ADAM_C1) / (jnp.sqrt(v / ADAM_C2) + ADAM_EPS) + ADAM_WD * w)
    return delta, m, v


def _slab_spec(kind, shard_shape, tr, slab_of):
    r, c = shard_shape
    if kind == "col":
        return pl.BlockSpec((tr, c), lambda q, i, sc: (i, slab_of(q, sc)))
    return pl.BlockSpec((tr, c), lambda q, i, sc: (slab_of(q, sc) * (r // tr) + i, 0))


def _chip_sum(name, partial, recv, kind, shard_shape, core):
    r, c = shard_shape
    tr = _blk(r, 256)

    def body(core_ref, p_ref, r_ref, o_ref):
        o_ref[...] = (p_ref[...].astype(F32) + r_ref[...].astype(F32)).astype(BF16)

    spec4 = pl.BlockSpec((None, tr, c), lambda q, i, sc: (q, i, 0))
    grid_spec = pltpu.PrefetchScalarGridSpec(
        num_scalar_prefetch=1, grid=(N_CHIP, r // tr),
        in_specs=[_slab_spec(kind, shard_shape, tr, lambda q, sc: 2 * q + sc[0]), spec4], out_specs=spec4)
    return pl.pallas_call(body, out_shape=jax.ShapeDtypeStruct((N_CHIP, r, c), BF16), grid_spec=grid_spec,
                          compiler_params=_params(("parallel", "parallel")), name=name)(core, partial, recv)


def _adamw_shard(name, parts, w, m, v, chip):
    r, c = w.shape
    n_parts = len(parts)
    tr = _blk(r // n_parts, 128)
    per = r // n_parts // tr

    def body(chip_ref, *refs):
        src, (w_ref, m_ref, v_ref), (g_out, d_out, m_out, v_out) = refs[:2 * n_parts], refs[2 * n_parts:2 * n_parts + 3], refs[2 * n_parts + 3:]
        for p in range(n_parts):
            @pl.when(pl.program_id(0) // per == p)
            def _():
                g = src[2 * p][...].astype(F32)
                for k in range(3):
                    g = g + src[2 * p + 1][k].astype(F32)
                g_out[...] = g
                d_out[...], m_out[...], v_out[...] = _adamw_math(g, w_ref[...], m_ref[...], v_ref[...])

    def part_specs(p):
        at = lambda i: jnp.clip(i - p * per, 0, per - 1)
        return [pl.BlockSpec((None, tr, c), lambda i, sc: (sc[0], at(i), 0)), pl.BlockSpec((3, tr, c), lambda i, sc: (0, at(i), 0))]

    blk = pl.BlockSpec((tr, c), lambda i, sc: (i, 0))
    grid_spec = pltpu.PrefetchScalarGridSpec(
        num_scalar_prefetch=1, grid=(r // tr,), in_specs=[s for p in range(n_parts) for s in part_specs(p)] + [blk, blk, blk], out_specs=[blk] * 4)
    return pl.pallas_call(body, out_shape=[jax.ShapeDtypeStruct((r, c), F32)] * 4, grid_spec=grid_spec,
                          compiler_params=_params(("parallel",)), name=name)(chip, *[a for p in parts for a in p], w, m, v)


def _adamw_small(name, g, w, m, v):
    def body(g_ref, w_ref, m_ref, v_ref, d_out, m_out, v_out):
        d_out[...], m_out[...], v_out[...] = _adamw_math(g_ref[...], w_ref[...], m_ref[...], v_ref[...])

    vm = pl.BlockSpec(memory_space=pltpu.VMEM)
    return pl.pallas_call(body, out_shape=[jax.ShapeDtypeStruct(g.shape, F32)] * 3, in_specs=[vm] * 4, out_specs=[vm] * 3,
                          compiler_params=pltpu.CompilerParams(vmem_limit_bytes=VMEM_LIMIT), name=name)(g, w, m, v)


def _pack_rows(arrays, total_rows):
    flat = [a.reshape(-1, LANES) for a in arrays]
    used = sum(f.shape[0] for f in flat)
    return jnp.concatenate(flat + [jnp.zeros((total_rows - used, LANES), F32)], axis=0)


def _unpack_rows(packed, like):
    out, at = [], 0
    for a in like:
        n = a.size // LANES
        out.append(packed[at:at + n].reshape(a.shape))
        at += n
    return out


def kernel(x, norm_mix_g, w_in, conv_w, conv_b, lru_wa, lru_ba, lru_wx, lru_bx, lru_lambda, w_proj_attn, w_proj_lru, w_out, norm_mlp_g, w_up, w_down, norm_final_g, loss_target, m_norm_mix_g, m_w_in, m_conv_w, m_conv_b, m_lru_wa, m_lru_ba, m_lru_wx, m_lru_bx, m_lru_lambda, m_w_proj_attn, m_w_proj_lru, m_w_out, m_norm_mlp_g, m_w_up, m_w_down, m_norm_final_g, v_norm_mix_g, v_w_in, v_conv_w, v_conv_b, v_lru_wa, v_lru_ba, v_lru_wx, v_lru_bx, v_lru_lambda, v_w_proj_attn, v_w_proj_lru, v_w_out, v_norm_mlp_g, v_w_up, v_w_down, v_norm_final_g):
    xs, tgt = x[0], loss_target[0]
    s, d = xs.shape
    nh = d // HEAD
    ix, iy, ic = _coords()
    core = jnp.reshape(ic, (1,)).astype(jnp.int32)
    chip = jnp.reshape(2 * ix + iy, (1,)).astype(jnp.int32)
    dev = 4 * ix + 2 * iy + ic

    big = [w_in[0], w_proj_attn[0], w_proj_lru[0], w_out[0], w_up[0], w_down[0]]
    big_m = [m_w_in[0], m_w_proj_attn[0], m_w_proj_lru[0], m_w_out[0], m_w_up[0], m_w_down[0]]
    big_v = [v_w_in[0], v_w_proj_attn[0], v_w_proj_lru[0], v_w_out[0], v_w_up[0], v_w_down[0]]
    kinds = ["col", "row", "row", "row", "col", "row"]
    pad_taps = lambda t: jnp.pad(t, ((0, SUBLANES - CONV_TAPS), (0, 0)))
    pad_taps2 = lambda t: jnp.pad(t, ((0, 2 * SUBLANES - CONV_TAPS), (0, 0)))
    shards = [w.astype(BF16) for w in big]
    win, cw_slots = _all_gather("all_gather_w_in", [shards[0], pad_taps2(conv_w[0])], ["col", "slot"])
    later = lax.optimization_barrier((shards[1:], win))[0]
    wpa, wpl, wout = _all_gather("all_gather_mix", later[:3], kinds[1:4], sequencer_id=1)
    wup, wdown = _all_gather("all_gather_mlp", later[3:], kinds[4:], sequencer_id=5)
    cw8 = jnp.transpose(cw_slots[:, :SUBLANES], (1, 0, 2)).reshape(SUBLANES, d)
    row_id = lax.broadcasted_iota(jnp.int32, (SUBLANES, d), 0)
    vec8 = sum(jnp.where(row_id == k, t, 0.0) for k, t in ((VEC_CB, conv_b), (VEC_BA, lru_ba), (VEC_BX, lru_bx), (VEC_LAM, lru_lambda)))
    wa16, wx16 = lru_wa[0].astype(BF16), lru_wx[0].astype(BF16)
    slopes = 2.0 ** (-8.0 * jnp.arange(1, nh + 1, dtype=F32) / nh)

    def seg_specs(*segs):
        return lambda bm, bn: [pl.BlockSpec((None, bm, bn), (lambda i, j, kk, sg=sg: (sg, i, j))) for sg in segs]

    def plain_specs(k):
        return lambda bm, bn: [pl.BlockSpec((bm, bn), lambda i, j, kk: (i, j)) for _ in range(k)]

    xn = _rms_fwd("norm_mix", xs, norm_mix_g)
    qkv = _mm_fwd("proj_qkv", xn, win, 0, 3 * d, [F32], seg_out=d)[0]
    rest = _mm_fwd("proj_rest", xn, win, 3 * d, 4 * d, [F32], seg_out=d)[0]
    att, lse = _attn_fwd(qkv, slopes)
    a3, u3, xc = _gates_fwd(rest, cw8, vec8, wa16, wx16)
    h3, hp3 = _scan_fwd(a3, u3)
    ylru, h2d = _lru_out(h3, rest)
    pa = _mm_fwd("proj_attn", att, wpa, 0, d, [F32])[0]

    def merge(acc, pa_b, ga, gl):
        return acc, _sigmoid(ga) * pa_b + _sigmoid(gl) * acc

    plr, merged = _mm_fwd("proj_lru_merge", ylru, wpl, 0, d, [F32, BF16], merge, (pa, rest, rest),
                          lambda bm, bn: plain_specs(1)(bm, bn) + seg_specs(2, 3)(bm, bn), bm=512)
    h1 = _mm_fwd("mix_out", merged, wout, 0, d, [F32], lambda acc, r: (acc + r,), (xs,), plain_specs(1))[0]
    hn = _rms_fwd("norm_mlp", h1, norm_mlp_g)

    def relu2(acc):
        return acc, jnp.square(jnp.maximum(acc, 0.0))

    up, hid = _mm_fwd("mlp_up", hn, wup, 0, wup.shape[1], [BF16, BF16], relu2)
    h2 = _mm_fwd("mlp_down", hid, wdown, 0, d, [F32], lambda acc, r: (acc + r,), (h1,), plain_specs(1))[0]
    dh2, dh2b, dg3, loss_lanes = _final_loss(h2, tgt, norm_final_g.reshape(1, d))
    loss = lax.psum(0.5 / d * jnp.sum(loss_lanes), ("x", "y", "c"))

    def reduce_group(tag, kk, shp, partials, from_sibling, sequencer_id):
        sums = [_chip_sum(f"chip_sum_{tag}_{i}", p, f, k, sh, core) for i, (p, f, k, sh) in enumerate(zip(partials, from_sibling, kk, shp))]
        return list(zip(sums, _exchange_chips(f"rs_chips_{tag}", sums, sequencer_id)))

    dup = _mm_nt("mlp_down_dx", dh2b, wdown, [BF16], lambda acc, u: (acc * (2.0 * jnp.maximum(u.astype(F32), 0.0)),), (up,), plain_specs(1))[0]
    g_wdown = _mm_tn("mlp_down_dw", hid, dh2b)
    g_wup = _mm_tn("mlp_up_dw", hn, dup)
    shp_mlp = [w.shape for w in big[4:]]
    (dhn,), sib_mlp = _mm_nt("mlp_up_dx", dup, wup, [F32], side=_sibling_side([g_wup, g_wdown], kinds[4:], shp_mlp))
    red_up, red_down = reduce_group("mlp", kinds[4:], shp_mlp, [g_wup, g_wdown], sib_mlp, 2)
    dhn = lax.optimization_barrier((dhn, red_up[0], red_down[0]))[0]
    dh1, dh1b, dg2 = _rms_bwd("norm_mlp_bwd", h1, norm_mlp_g, dhn, dh2)

    def merge_bwd(acc, pa_b, pl_b, ga, gl):
        sa, sl = _sigmoid(ga), _sigmoid(gl)
        return acc * sa, acc * sl, acc * pa_b * sa * (1.0 - sa), acc * pl_b * sl * (1.0 - sl)

    dpa, dpl, dga, dgl = _mm_nt("mix_out_dx", dh1b, wout, [BF16] * 4, merge_bwd, (pa, plr, rest, rest),
                                lambda bm, bn: plain_specs(2)(bm, bn) + seg_specs(2, 3)(bm, bn), bm=512)
    g_wout = _mm_tn("mix_out_dw", merged, dh1b)
    datt = _mm_nt("proj_attn_dx", dpa, wpa, [F32])[0]
    g_wpa = _mm_tn("proj_attn_dw", att, dpa)

    def lru_out_bwd(acc, h_b, gate):
        return acc * _gelu(gate), acc * h_b * _gelu_grad(gate)

    g_wpl = _mm_tn("proj_lru_dw", ylru, dpl)
    shp_mix = [w.shape for w in big[1:4]]
    (dh, dxg), sib_mix = _mm_nt("proj_lru_dx", dpl, wpl, [F32, BF16], lru_out_bwd, (h2d, rest),
                                lambda bm, bn: plain_specs(1)(bm, bn) + seg_specs(1)(bm, bn), bm=512,
                                side=_sibling_side([g_wpa, g_wpl, g_wout], kinds[1:4], shp_mix))
    red_pa, red_pl, red_out = reduce_group("mix", kinds[1:4], shp_mix, [g_wpa, g_wpl, g_wout], sib_mix, 3)
    dq, dk, dv = _attn_bwd(qkv, datt, att, lse, slopes)
    g3, da3 = _scan_bwd(a3, hp3, dh)
    dxc, dwa, dwx, dvec = _gates_bwd(g3, da3, xc, wa16, wx16, vec8)
    dxr, dconv = _conv_bwd(dxc, rest, cw8)
    dproj = jnp.concatenate([dq, dk, dv, dxr, dxg, dga, dgl], axis=1)
    big_out = {i: _adamw_shard(f"adamw_{i}", [red], big[i], big_m[i], big_v[i], chip) for i, red in ((4, red_up), (5, red_down))}
    red_mix = lax.optimization_barrier(((red_pa, red_pl, red_out), (big_out[4], big_out[5])))[0]
    big_out.update({i: _adamw_shard(f"adamw_{i}", [red], big[i], big_m[i], big_v[i], chip) for i, red in zip((1, 2, 3), red_mix)})
    dproj = lax.optimization_barrier((dproj, [big_out[i] for i in range(1, 6)]))[0]
    half = (big[0].shape[0] // 2, big[0].shape[1])
    g_in0 = _mm_tn("proj_in_dw_0", xn, dproj, part=(0, 2))
    g_in1, sib_in0 = _mm_tn("proj_in_dw_1", xn, dproj, part=(1, 2), side=_sibling_side([g_in0], ["col"], [half]))
    red_in = reduce_group("in_0", ["col"], [half], [g_in0], sib_in0, 4)
    dproj = lax.optimization_barrier((dproj, red_in[0][0]))[0]
    (dxn0,), sib_in1 = _mm_nt("proj_in_dx_0", dproj, win, [F32], part=(0, 2), side=_sibling_side([g_in1], ["col"], [half]))
    red_in += reduce_group("in_1", ["col"], [half], [g_in1], sib_in1, 6)
    dproj = lax.optimization_barrier((dproj, red_in[1][0]))[0]
    dxn1 = _mm_nt("proj_in_dx_1", dproj, win, [F32], part=(1, 2))[0]
    dxn = jnp.concatenate([dxn0, dxn1], axis=0)
    grad_x, _, dg1 = _rms_bwd("norm_mix_bwd", xs, norm_mix_g, dxn, dh1)
    big_out[0] = _adamw_shard("adamw_0", red_in, big[0], big_m[0], big_v[0], chip)

    small_w = [norm_mix_g, conv_b, lru_wa, lru_ba, lru_wx, lru_bx, lru_lambda, norm_mlp_g, norm_final_g]
    small_m = [m_norm_mix_g, m_conv_b, m_lru_wa, m_lru_ba, m_lru_wx, m_lru_bx, m_lru_lambda, m_norm_mlp_g, m_norm_final_g]
    small_v = [v_norm_mix_g, v_conv_b, v_lru_wa, v_lru_ba, v_lru_wx, v_lru_bx, v_lru_lambda, v_norm_mlp_g, v_norm_final_g]
    small_g = [dg1, dconv[CONV_TAPS:CONV_TAPS + 1], dwa, dvec[VEC_BA:VEC_BA + 1], dwx, dvec[VEC_BX:VEC_BX + 1],
               dvec[VEC_LAM:VEC_LAM + 1], dg2, dg3, dconv[0:CONV_TAPS]]
    n_rows = sum(g.size for g in small_g) // LANES
    per_dev = -(-n_rows // (N_DEV * SUBLANES)) * SUBLANES
    total = _all_reduce_small(_pack_rows(small_g, N_DEV * per_dev))
    sw_rows = -(-(n_rows - CONV_TAPS * d // LANES) // SUBLANES) * SUBLANES
    s_delta, s_m, s_v = _adamw_small("adamw_small", total[:sw_rows], _pack_rows(small_w, sw_rows), _pack_rows(small_m, sw_rows),
                                     _pack_rows(small_v, sw_rows))
    sg = _unpack_rows(total, [w for w in small_w] + [jnp.zeros((1, CONV_TAPS, d), F32)])
    s_grad, g_cw_full = sg[:-1], sg[-1]
    s_delta, s_m, s_v = (_unpack_rows(t, small_w) for t in (s_delta, s_m, s_v))
    cshard = conv_w.shape[2]
    g_cw = lax.dynamic_slice(g_cw_full, (0, 0, dev * cshard), (1, CONV_TAPS, cshard))
    cw_delta, cw_m, cw_v = (t[:CONV_TAPS][None] for t in _adamw_small(
        "adamw_conv_w", pad_taps(g_cw[0]), pad_taps(conv_w[0]), pad_taps(m_conv_w[0]), pad_taps(v_conv_w[0])))

    names = ["norm_mix_g", "w_in", "conv_w", "conv_b", "lru_wa", "lru_ba", "lru_wx", "lru_bx", "lru_lambda", "w_proj_attn", "w_proj_lru",
             "w_out", "norm_mlp_g", "w_up", "w_down", "norm_final_g"]
    small_names = ["norm_mix_g", "conv_b", "lru_wa", "lru_ba", "lru_wx", "lru_bx", "lru_lambda", "norm_mlp_g", "norm_final_g"]
    big_names = ["w_in", "w_proj_attn", "w_proj_lru", "w_out", "w_up", "w_down"]
    res = {"conv_w": (g_cw, cw_delta, cw_m, cw_v)}
    for i, nm in enumerate(small_names):
        res[nm] = (s_grad[i], s_delta[i], s_m[i], s_v[i])
    for i, nm in enumerate(big_names):
        res[nm] = tuple(t[None] for t in big_out[i])
    return (loss, grad_x[None], *[res[nm][0] for nm in names], *[res[nm][1] for nm in names],
            *[res[nm][2] for nm in names], *[res[nm][3] for nm in names])
```

```python
import jax
import jax.numpy as jnp
from jax import lax
from jax.experimental import pallas as pl
from jax.experimental.pallas import tpu as pltpu
from jax.experimental.pallas import tpu_sc as plsc

F32, BF16 = jnp.float32, jnp.bfloat16
MESH = pl.DeviceIdType.MESH
HBM = pl.BlockSpec(memory_space=pltpu.HBM)
N_DEV = 8
N_CHIP = 4
HEAD = 128
SPAN = 128
DILATIONS = (1, 4, 16)
CONV_TAPS = 4
LRU_C = 8.0
NORM_EPS = 1e-6
LANES = 128
SUBLANES = 8
VMEM_LIMIT = 56 * 1024 * 1024
ADAM_LR, ADAM_B1, ADAM_B2, ADAM_EPS, ADAM_WD, ADAM_STEP = 0.001, 0.9, 0.999, 1e-08, 0.01, 10
ADAM_C1 = 1.0 - ADAM_B1 ** ADAM_STEP
ADAM_C2 = 1.0 - ADAM_B2 ** ADAM_STEP
NEG = -1e30


def _params(sem=None):
    return pltpu.CompilerParams(dimension_semantics=sem, vmem_limit_bytes=VMEM_LIMIT)


def _sigmoid(v):
    return 1.0 / (1.0 + jnp.exp(-v))


def _gelu(v):
    k = 0.7978845608028654
    return 0.5 * v * (1.0 + jnp.tanh(k * (v + 0.044715 * v * v * v)))


def _gelu_grad(v):
    k = 0.7978845608028654
    t = jnp.tanh(k * (v + 0.044715 * v * v * v))
    return 0.5 * (1.0 + t) + 0.5 * v * (1.0 - t * t) * k * (1.0 + 3.0 * 0.044715 * v * v)


NN = (((1,), (0,)), ((), ()))
NT = (((1,), (1,)), ((), ()))
TN = (((0,), (0,)), ((), ()))


def _mm(name, a, a_spec, b, b_spec, dn, grid, out_shapes, out_specs, acc_block, epilogue=None, extras=(), extra_specs=(), side=None):
    nk, ne, no = grid[2], len(extras), len(out_shapes)
    side_ops, side_shapes, side_copies, make_copies = side if side is not None else ((), (), 0, None)
    ns_in, ns_out = len(side_ops), len(side_shapes)

    def body(*refs):
        a_ref, b_ref = refs[0], refs[1]
        ex, side_in = refs[2:2 + ne], refs[2 + ne:2 + ne + ns_in]
        outs = refs[2 + ne + ns_in:2 + ne + ns_in + no]
        side_out = refs[2 + ne + ns_in + no:2 + ne + ns_in + no + ns_out]
        scratch = refs[2 + ne + ns_in + no + ns_out:]
        at = [pl.program_id(ax) for ax in range(3)]
        if side is not None:
            @pl.when((at[0] == 0) & (at[1] == 0) & (at[2] == 0))
            def _():
                for cp in make_copies(side_in, side_out, scratch[-2], scratch[-1]):
                    cp.start()

        part = lax.dot_general(a_ref[...], b_ref[...], dn, preferred_element_type=F32)

        def finish(acc):
            vals = epilogue(acc, *[e[...] for e in ex]) if epilogue is not None else (acc,)
            for o, v in zip(outs, vals):
                o[...] = v.astype(o.dtype)

        if nk == 1:
            finish(part)
        else:
            acc_ref, k = scratch[0], at[2]

            @pl.when(k == 0)
            def _():
                acc_ref[...] = part

            @pl.when(k > 0)
            def _():
                acc_ref[...] += part

            @pl.when(k == nk - 1)
            def _():
                finish(acc_ref[...])

        if side is not None:
            @pl.when((at[0] == grid[0] - 1) & (at[1] == grid[1] - 1) & (at[2] == grid[2] - 1))
            def _():
                for cp in make_copies(side_in, side_out, scratch[-2], scratch[-1]):
                    cp.wait()

    scratch_shapes = [pltpu.VMEM(acc_block, F32)] if nk > 1 else []
    if side is not None:
        scratch_shapes += [pltpu.SemaphoreType.DMA((side_copies,)), pltpu.SemaphoreType.DMA((side_copies,))]
    res = pl.pallas_call(
        body, out_shape=[*out_shapes, *side_shapes], grid=grid, in_specs=[a_spec, b_spec, *extra_specs, *[HBM] * ns_in],
        out_specs=[*out_specs, *[HBM] * ns_out], scratch_shapes=scratch_shapes,
        compiler_params=_params(("arbitrary",) * 3 if side is not None else ("parallel", "parallel", "arbitrary")),
        name=name)(a, b, *extras, *side_ops)
    return res if side is None else (res[:no], res[no:])


def _blk(n, pref):
    return pref if n % pref == 0 else n


def _kblk(k):
    return k if k <= 2048 else next(b for b in (2048, 1024, 512) if k % b == 0)


def _mm_fwd(name, a, w, col0, ncols, out_dtypes, epilogue=None, extras=(), extra_specs_fn=None, seg_out=None, bm=1024, bn=1024):
    m, k = a.shape
    bm, bn = _blk(m, bm), _blk(ncols, bn)
    bk = _kblk(k)
    nk = k // bk
    cb0 = col0 // bn
    grid = (m // bm, ncols // bn, nk)
    a_spec = pl.BlockSpec((bm, bk), lambda i, j, kk: (i, kk))
    b_spec = pl.BlockSpec((bk, bn), lambda i, j, kk: (kk, cb0 + j))
    if seg_out is None:
        shapes = [jax.ShapeDtypeStruct((m, ncols), dt) for dt in out_dtypes]
        specs = [pl.BlockSpec((bm, bn), lambda i, j, kk: (i, j)) for _ in out_dtypes]
    else:
        per = seg_out // bn
        shapes = [jax.ShapeDtypeStruct((ncols // seg_out, m, seg_out), dt) for dt in out_dtypes]
        specs = [pl.BlockSpec((None, bm, bn), lambda i, j, kk: (j // per, i, j % per)) for _ in out_dtypes]
    ex_specs = extra_specs_fn(bm, bn) if extra_specs_fn else ()
    return _mm(name, a, a_spec, w, b_spec, NN, grid, shapes, specs, (bm, bn), epilogue, extras, ex_specs)


def _mm_nt(name, a, w, out_dtypes, epilogue=None, extras=(), extra_specs_fn=None, part=(0, 1), side=None, bm=1024, bn=1024):
    m, k = a.shape
    n = w.shape[0]
    m = m // part[1]
    bm, bn = _blk(m, bm), _blk(n, bn)
    bk = _kblk(k)
    grid = (m // bm, n // bn, k // bk)
    i0 = part[0] * (m // bm)
    a_spec = pl.BlockSpec((bm, bk), lambda i, j, kk: (i0 + i, kk))
    b_spec = pl.BlockSpec((bn, bk), lambda i, j, kk: (j, kk))
    shapes = [jax.ShapeDtypeStruct((m, n), dt) for dt in out_dtypes]
    specs = [pl.BlockSpec((bm, bn), lambda i, j, kk: (i, j)) for _ in out_dtypes]
    ex_specs = extra_specs_fn(bm, bn) if extra_specs_fn else ()
    return _mm(name, a, a_spec, w, b_spec, NT, grid, shapes, specs, (bm, bn), epilogue, extras, ex_specs, side)


def _mm_tn(name, a, b, part=(0, 1), side=None, bm=1024, bn=1024):
    t, m = a.shape
    n = b.shape[1]
    m = m // part[1]
    bm, bn = _blk(m, bm), _blk(n, bn)
    grid = (m // bm, n // bn, 1)
    i0 = part[0] * (m // bm)
    a_spec = pl.BlockSpec((t, bm), lambda i, j, kk: (0, i0 + i))
    b_spec = pl.BlockSpec((t, bn), lambda i, j, kk: (0, j))
    res = _mm(name, a, a_spec, b, b_spec, TN, grid, [jax.ShapeDtypeStruct((m, n), BF16)],
              [pl.BlockSpec((bm, bn), lambda i, j, kk: (i, j))], (bm, bn), side=side)
    return res[0] if side is None else (res[0][0], res[1])


ROWS = 256


def _row_spec(d):
    return pl.BlockSpec((ROWS, d), lambda i: (i, 0))


def _vec_spec(d, rows=1):
    return pl.BlockSpec((rows, d), lambda i: (0, 0))


def _rms_fwd(name, x, g):
    s, d = x.shape

    def body(x_ref, g_ref, o_ref):
        xv = x_ref[...]
        r = lax.rsqrt(jnp.mean(xv * xv, axis=-1, keepdims=True) + NORM_EPS)
        o_ref[...] = (xv * r * g_ref[...]).astype(BF16)

    return pl.pallas_call(body, out_shape=jax.ShapeDtypeStruct((s, d), BF16), grid=(s // ROWS,),
                          in_specs=[_row_spec(d), _vec_spec(d)], out_specs=_row_spec(d),
                          compiler_params=_params(("parallel",)), name=name)(x, g)


def _rms_bwd_math(xv, g, dy):
    r = lax.rsqrt(jnp.mean(xv * xv, axis=-1, keepdims=True) + NORM_EPS)
    n = xv * r
    z = dy * g
    dx = r * (z - n * jnp.mean(z * n, axis=-1, keepdims=True))
    return dx, jnp.sum(dy * n, axis=0, keepdims=True)


def _rms_bwd(name, x, g, dy, resid):
    s, d = x.shape

    def body(x_ref, g_ref, dy_ref, r_ref, dx_ref, dxb_ref, dg_ref):
        dx, dg = _rms_bwd_math(x_ref[...], g_ref[...], dy_ref[...])
        dx = dx + r_ref[...]
        dx_ref[...] = dx
        dxb_ref[...] = dx.astype(BF16)

        @pl.when(pl.program_id(0) == 0)
        def _():
            dg_ref[...] = jnp.zeros_like(dg_ref)

        dg_ref[...] += dg

    return pl.pallas_call(
        body, out_shape=[jax.ShapeDtypeStruct((s, d), F32), jax.ShapeDtypeStruct((s, d), BF16), jax.ShapeDtypeStruct((1, d), F32)],
        grid=(s // ROWS,), in_specs=[_row_spec(d), _vec_spec(d), _row_spec(d), _row_spec(d)],
        out_specs=[_row_spec(d), _row_spec(d), _vec_spec(d)], compiler_params=_params(("arbitrary",)), name=name)(x, g, dy, resid)


def _final_loss(h2, tgt, g):
    s, d = h2.shape

    def body(x_ref, t_ref, g_ref, dx_ref, dxb_ref, dg_ref, ls_ref):
        xv, gv = x_ref[...], g_ref[...]
        r = lax.rsqrt(jnp.mean(xv * xv, axis=-1, keepdims=True) + NORM_EPS)
        diff = xv * r * gv - t_ref[...]
        dx, dg = _rms_bwd_math(xv, gv, diff * (1.0 / d))
        dx_ref[...] = dx
        dxb_ref[...] = dx.astype(BF16)

        @pl.when(pl.program_id(0) == 0)
        def _():
            dg_ref[...] = jnp.zeros_like(dg_ref)
            ls_ref[...] = jnp.zeros_like(ls_ref)

        dg_ref[...] += dg
        ls_ref[...] += jnp.sum(diff * diff, axis=0, keepdims=True)

    return pl.pallas_call(
        body, out_shape=[jax.ShapeDtypeStruct((s, d), F32), jax.ShapeDtypeStruct((s, d), BF16),
                         jax.ShapeDtypeStruct((1, d), F32), jax.ShapeDtypeStruct((1, d), F32)],
        grid=(s // ROWS,), in_specs=[_row_spec(d), _row_spec(d), _vec_spec(d)],
        out_specs=[_row_spec(d), _row_spec(d), _vec_spec(d), _vec_spec(d)],
        compiler_params=_params(("arbitrary",)), name="final_norm_loss")(h2, tgt, g)


ATTN_Q = 128


ATTN_BATCH = 8


def _attn_units(s):
    units = []
    for gi, d in enumerate(DILATIONS):
        for r in range(d):
            for q0 in range(0, s // d, ATTN_Q):
                k0 = max(q0 - SPAN, 0)
                units.append((gi, d, r, q0, k0, q0 + ATTN_Q - k0))
    return units


def _stream_rows(d, r, start, size):
    return pl.ds(r + start * d, size) if d == 1 else pl.ds(r + start * d, size, stride=d)


def _attn_scores(q_ref, k_ref, slope, d, r, q0, k0, nk):
    qrows, krows = _stream_rows(d, r, q0, ATTN_Q), _stream_rows(d, r, k0, nk)
    qb = q_ref[qrows, :].astype(BF16)
    kb = k_ref[krows, :].astype(BF16)
    sc = lax.dot_general(qb, kb, NT, preferred_element_type=F32) * (HEAD ** -0.5)
    qi = lax.broadcasted_iota(jnp.int32, (ATTN_Q, nk), 0)
    kj = lax.broadcasted_iota(jnp.int32, (ATTN_Q, nk), 1)
    dist = (q0 - k0) + qi - kj
    valid = (dist >= 0) & (dist <= SPAN)
    sc = sc - (slope * d) * dist.astype(F32)
    return jnp.where(valid, sc, NEG), valid, qb, kb, qrows, krows


def _attn_fwd(qkv, slopes):
    _, s, dm = qkv.shape
    units = _attn_units(s)

    def body(sl_ref, q_ref, k_ref, v_ref, att_ref, lse_ref, *scr):
        o_scr, l_scr = scr[:3], scr[3:]
        slope = sl_ref[pl.program_id(0)]
        for first in range(0, len(units), ATTN_BATCH):
            batch = units[first:first + ATTN_BATCH]
            scored = [_attn_scores(q_ref, k_ref, slope, d, r, q0, k0, nk) for _, d, r, q0, k0, nk in batch]
            soft = []
            for sc, _, _, _, _, _ in scored:
                m = jnp.max(sc, axis=-1, keepdims=True)
                p = jnp.exp(sc - m)
                soft.append((m, p, jnp.sum(p, axis=-1, keepdims=True)))
            outs = [lax.dot_general(p.astype(BF16), v_ref[sco[5], :].astype(BF16), NN, preferred_element_type=F32)
                    for (m, p, l), sco in zip(soft, scored)]
            for (gi, *_), (m, p, l), sco, o in zip(batch, soft, scored, outs):
                o_scr[gi][sco[4], :] = o / l
                l_scr[gi][sco[4], :] = jnp.broadcast_to(m + jnp.log(l), (ATTN_Q, HEAD))
        l0, l1, l2 = l_scr[0][...], l_scr[1][...], l_scr[2][...]
        m = jnp.maximum(jnp.maximum(l0, l1), l2)
        w0, w1, w2 = jnp.exp(l0 - m), jnp.exp(l1 - m), jnp.exp(l2 - m)
        tot = w0 + w1 + w2
        att_ref[...] = ((w0 * o_scr[0][...] + w1 * o_scr[1][...] + w2 * o_scr[2][...]) / tot).astype(BF16)
        lse_ref[...] = m + jnp.log(tot)

    def seg(i):
        return pl.BlockSpec((None, s, HEAD), lambda h: (i, 0, h))

    col = pl.BlockSpec((s, HEAD), lambda h: (0, h))
    return pl.pallas_call(
        body, out_shape=[jax.ShapeDtypeStruct((s, dm), BF16), jax.ShapeDtypeStruct((s, dm), F32)], grid=(dm // HEAD,),
        in_specs=[pl.BlockSpec(memory_space=pltpu.SMEM), seg(0), seg(1), seg(2)], out_specs=[col, col],
        scratch_shapes=[pltpu.VMEM((s, HEAD), F32)] * (2 * len(DILATIONS)),
        compiler_params=_params(("parallel",)), name="attn_fwd")(slopes, qkv, qkv, qkv)


def _attn_bwd(qkv, datt, att, lse, slopes):
    _, s, dm = qkv.shape
    units = _attn_units(s)

    def body(sl_ref, q_ref, k_ref, v_ref, do_ref, att_ref, lse_ref, dq_ref, dk_ref, dv_ref, dq_scr, dk_scr, dv_scr, dl_scr):
        slope = sl_ref[pl.program_id(0)]
        delta = jnp.sum(do_ref[...] * att_ref[...].astype(F32), axis=-1, keepdims=True)
        dl_scr[...] = jnp.broadcast_to(delta, (s, HEAD))
        dq_scr[...] = jnp.zeros_like(dq_scr)
        dk_scr[...] = jnp.zeros_like(dk_scr)
        dv_scr[...] = jnp.zeros_like(dv_scr)
        for first in range(0, len(units), ATTN_BATCH):
            scored = [_attn_scores(q_ref, k_ref, slope, d, r, q0, k0, nk) for _, d, r, q0, k0, nk in units[first:first + ATTN_BATCH]]
            dobs = [do_ref[sco[4], :].astype(BF16) for sco in scored]
            dps = [lax.dot_general(dob, v_ref[sco[5], :].astype(BF16), NT, preferred_element_type=F32) for dob, sco in zip(dobs, scored)]
            ps = [jnp.where(sco[1], jnp.exp(sco[0] - lse_ref[sco[4], :][:, 0:1]), 0.0) for sco in scored]
            dss = [(p * (dp - dl_scr[sco[4], :][:, 0:1]) * (HEAD ** -0.5)).astype(BF16) for p, dp, sco in zip(ps, dps, scored)]
            dqs = [lax.dot_general(ds, sco[3], NN, preferred_element_type=F32) for ds, sco in zip(dss, scored)]
            dks = [lax.dot_general(ds, sco[2], TN, preferred_element_type=F32) for ds, sco in zip(dss, scored)]
            dvs = [lax.dot_general(p.astype(BF16), dob, TN, preferred_element_type=F32) for p, dob in zip(ps, dobs)]
            for sco, dq, dk, dv in zip(scored, dqs, dks, dvs):
                dq_scr[sco[4], :] += dq
                dk_scr[sco[5], :] += dk
                dv_scr[sco[5], :] += dv
        dq_ref[...] = dq_scr[...].astype(BF16)
        dk_ref[...] = dk_scr[...].astype(BF16)
        dv_ref[...] = dv_scr[...].astype(BF16)

    def seg(i):
        return pl.BlockSpec((None, s, HEAD), lambda h: (i, 0, h))

    col = pl.BlockSpec((s, HEAD), lambda h: (0, h))
    return pl.pallas_call(
        body, out_shape=[jax.ShapeDtypeStruct((s, dm), BF16)] * 3, grid=(dm // HEAD,),
        in_specs=[pl.BlockSpec(memory_space=pltpu.SMEM), seg(0), seg(1), seg(2), col, col, col], out_specs=[col, col, col],
        scratch_shapes=[pltpu.VMEM((s, HEAD), F32)] * 4,
        compiler_params=_params(("parallel",)), name="attn_bwd")(slopes, qkv, qkv, qkv, datt, att, lse)


VEC_CB, VEC_BA, VEC_BX, VEC_LAM = 0, 1, 2, 3


def _to_3d(ref3, val):
    lw = val.shape[1] // SUBLANES
    for j in range(SUBLANES):
        ref3[:, j, :] = val[:, j * lw:(j + 1) * lw]


def _from_3d(ref3):
    return jnp.concatenate([ref3[:, j, :] for j in range(SUBLANES)], axis=1)


def _softplus(z):
    return jnp.maximum(z, 0.0) + jnp.log1p(jnp.exp(-jnp.abs(z)))


def _gate_math(xc, wa_ref, wx_ref, vec):
    xcb = xc.astype(BF16)
    nh = xc.shape[1] // HEAD
    pre_a = jnp.concatenate([jnp.dot(xcb[:, h * HEAD:(h + 1) * HEAD], wa_ref[h], preferred_element_type=F32) for h in range(nh)], axis=1)
    pre_x = jnp.concatenate([jnp.dot(xcb[:, h * HEAD:(h + 1) * HEAD], wx_ref[h], preferred_element_type=F32) for h in range(nh)], axis=1)
    ra = _sigmoid(pre_a + vec[VEC_BA:VEC_BA + 1])
    ig = _sigmoid(pre_x + vec[VEC_BX:VEC_BX + 1])
    sp = _softplus(-vec[VEC_LAM:VEC_LAM + 1])
    log_a = -LRU_C * ra * sp
    a = jnp.exp(log_a)
    z = 2.0 * log_a
    one_minus_a2 = jnp.where(z > -0.01, -z * (1.0 + z * (0.5 + z * (1.0 / 6.0))), 1.0 - jnp.exp(z))
    mult = jnp.sqrt(one_minus_a2)
    return dict(xcb=xcb, ra=ra, ig=ig, sp=sp, a=a, mult=mult)


def _conv_pad_prev(pad_ref, cur, halo, first):
    pad_ref[0:SUBLANES, :] = jnp.where(first, 0.0, halo)
    pad_ref[SUBLANES:SUBLANES + cur.shape[0], :] = cur


def _gates_fwd(rest, cw8, vec8, wa, wx):
    _, s, d = rest.shape
    lw = d // SUBLANES
    hb = ROWS // SUBLANES

    def body(x_ref, halo_ref, cw_ref, vec_ref, wa_ref, wx_ref, a_ref, u_ref, xc_ref, pad):
        _conv_pad_prev(pad, x_ref[...], halo_ref[...], pl.program_id(0) == 0)
        vec = vec_ref[...]
        xc = vec[VEC_CB:VEC_CB + 1]
        for k in range(CONV_TAPS):
            xc = xc + cw_ref[k:k + 1, :] * pad[pl.ds(SUBLANES - (CONV_TAPS - 1) + k, ROWS), :]
        gm = _gate_math(xc, wa_ref, wx_ref, vec)
        xc_ref[...] = xc
        _to_3d(a_ref, gm["a"])
        _to_3d(u_ref, gm["mult"] * (gm["ig"] * xc))

    spec3 = pl.BlockSpec((ROWS, SUBLANES, lw), lambda i: (i, 0, 0))
    wspec = pl.BlockSpec(wa.shape, lambda i: (0, 0, 0))
    return pl.pallas_call(
        body, out_shape=[jax.ShapeDtypeStruct((s, SUBLANES, lw), F32)] * 2 + [jax.ShapeDtypeStruct((s, d), F32)], grid=(s // ROWS,),
        in_specs=[pl.BlockSpec((None, ROWS, d), lambda i: (0, i, 0)),
                  pl.BlockSpec((None, SUBLANES, d), lambda i: (0, jnp.maximum(i * hb - 1, 0), 0)),
                  _vec_spec(d, SUBLANES), _vec_spec(d, SUBLANES), wspec, wspec],
        out_specs=[spec3, spec3, _row_spec(d)], scratch_shapes=[pltpu.VMEM((ROWS + SUBLANES, d), F32)],
        compiler_params=_params(("parallel",)), name="lru_gates_fwd")(rest, rest, cw8, vec8, wa, wx)


def _scan_fwd(a3, u3):
    s, _, lw = a3.shape

    def body(a_ref, u_ref, h_ref, hp_ref, carry):
        @pl.when(pl.program_id(0) == 0)
        def _():
            carry[...] = jnp.zeros_like(carry)

        def step(t, h):
            hp_ref[t] = h
            hn = a_ref[t] * h + u_ref[t]
            h_ref[t] = hn
            return hn

        carry[...] = lax.fori_loop(0, ROWS, step, carry[...], unroll=8)

    spec3 = pl.BlockSpec((ROWS, SUBLANES, lw), lambda i: (i, 0, 0))
    return pl.pallas_call(body, out_shape=[jax.ShapeDtypeStruct(a3.shape, F32)] * 2, grid=(s // ROWS,), in_specs=[spec3, spec3],
                          out_specs=[spec3, spec3], scratch_shapes=[pltpu.VMEM((SUBLANES, lw), F32)],
                          compiler_params=_params(("arbitrary",)), name="lru_scan_fwd")(a3, u3)


def _lru_out(h3, rest):
    s, _, lw = h3.shape
    d = lw * SUBLANES

    def body(h_ref, g_ref, y_ref, h2_ref):
        h = _from_3d(h_ref)
        h2_ref[...] = h
        y_ref[...] = (h * _gelu(g_ref[...])).astype(BF16)

    return pl.pallas_call(
        body, out_shape=[jax.ShapeDtypeStruct((s, d), BF16), jax.ShapeDtypeStruct((s, d), F32)], grid=(s // ROWS,),
        in_specs=[pl.BlockSpec((ROWS, SUBLANES, lw), lambda i: (i, 0, 0)), pl.BlockSpec((None, ROWS, d), lambda i: (1, i, 0))],
        out_specs=[_row_spec(d), _row_spec(d)], compiler_params=_params(("parallel",)), name="lru_out")(h3, rest)


def _scan_bwd(a3, hp3, dh):
    s, _, lw = a3.shape
    d = lw * SUBLANES
    nb = s // ROWS

    def body(a_ref, hp_ref, dh_ref, g_ref, da_ref, dh3, carry):
        @pl.when(pl.program_id(0) == 0)
        def _():
            carry[...] = jnp.zeros_like(carry)

        _to_3d(dh3, dh_ref[...])

        def step(j, c):
            t = ROWS - 1 - j
            g = dh3[t] + c
            g_ref[t] = g
            da_ref[t] = g * hp_ref[t]
            return a_ref[t] * g

        carry[...] = lax.fori_loop(0, ROWS, step, carry[...], unroll=8)

    spec3 = pl.BlockSpec((ROWS, SUBLANES, lw), lambda i: (nb - 1 - i, 0, 0))
    return pl.pallas_call(
        body, out_shape=[jax.ShapeDtypeStruct(a3.shape, F32)] * 2, grid=(nb,),
        in_specs=[spec3, spec3, pl.BlockSpec((ROWS, d), lambda i: (nb - 1 - i, 0))], out_specs=[spec3, spec3],
        scratch_shapes=[pltpu.VMEM((ROWS, SUBLANES, lw), F32), pltpu.VMEM((SUBLANES, lw), F32)],
        compiler_params=_params(("arbitrary",)), name="lru_scan_bwd")(a3, hp3, dh)


def _gates_bwd(g3, da3, xc, wa, wx, vec8):
    s, d = xc.shape
    lw = d // SUBLANES
    nh = d // HEAD

    def body(g_ref, da_ref, xc_ref, wa_ref, wx_ref, vec_ref, dxc_ref, dwa_ref, dwx_ref, dvec_ref):
        @pl.when(pl.program_id(0) == 0)
        def _():
            dwa_ref[...] = jnp.zeros_like(dwa_ref)
            dwx_ref[...] = jnp.zeros_like(dwx_ref)
            dvec_ref[...] = jnp.zeros_like(dvec_ref)

        xc_v, vec = xc_ref[...], vec_ref[...]
        du, da = _from_3d(g_ref), _from_3d(da_ref)
        gm = _gate_math(xc_v, wa_ref, wx_ref, vec)
        ra, ig, sp, a, mult = gm["ra"], gm["ig"], gm["sp"], gm["a"], gm["mult"]
        dmult = du * ig * xc_v
        dlog_a = da * a - dmult * (a * a) / mult
        dpre_a = dlog_a * (-LRU_C * sp) * ra * (1.0 - ra)
        dpre_x = du * mult * xc_v * ig * (1.0 - ig)
        dlam = jnp.sum(dlog_a * (-LRU_C * ra), axis=0, keepdims=True) * (-_sigmoid(-vec[VEC_LAM:VEC_LAM + 1]))
        dvec_ref[VEC_BA:VEC_BA + 1, :] += jnp.sum(dpre_a, axis=0, keepdims=True)
        dvec_ref[VEC_BX:VEC_BX + 1, :] += jnp.sum(dpre_x, axis=0, keepdims=True)
        dvec_ref[VEC_LAM:VEC_LAM + 1, :] += dlam
        dab, dxb, xcb = dpre_a.astype(BF16), dpre_x.astype(BF16), gm["xcb"]
        back = []
        for h in range(nh):
            cols = slice(h * HEAD, (h + 1) * HEAD)
            dwa_ref[h] += lax.dot_general(xcb[:, cols], dab[:, cols], TN, preferred_element_type=F32)
            dwx_ref[h] += lax.dot_general(xcb[:, cols], dxb[:, cols], TN, preferred_element_type=F32)
            back.append(lax.dot_general(dab[:, cols], wa_ref[h], NT, preferred_element_type=F32)
                        + lax.dot_general(dxb[:, cols], wx_ref[h], NT, preferred_element_type=F32))
        dxc_ref[...] = du * mult * ig + jnp.concatenate(back, axis=1)

    spec3 = pl.BlockSpec((ROWS, SUBLANES, lw), lambda i: (i, 0, 0))
    wspec = pl.BlockSpec(wa.shape, lambda i: (0, 0, 0))
    return pl.pallas_call(
        body, out_shape=[jax.ShapeDtypeStruct((s, d), F32), jax.ShapeDtypeStruct(wa.shape, F32), jax.ShapeDtypeStruct(wa.shape, F32),
                         jax.ShapeDtypeStruct((SUBLANES, d), F32)],
        grid=(s // ROWS,), in_specs=[spec3, spec3, _row_spec(d), wspec, wspec, _vec_spec(d, SUBLANES)],
        out_specs=[_row_spec(d), wspec, wspec, _vec_spec(d, SUBLANES)],
        compiler_params=_params(("arbitrary",)), name="lru_gates_bwd")(g3, da3, xc, wa, wx, vec8)


def _conv_bwd(dxc, rest, cw8):
    s, d = dxc.shape
    hb = ROWS // SUBLANES
    last = s // SUBLANES - 1

    def body(dc_ref, dnext_ref, x_ref, xprev_ref, cw_ref, dx_ref, dcw_ref, padd, padx):
        i = pl.program_id(0)

        @pl.when(i == 0)
        def _():
            dcw_ref[...] = jnp.zeros_like(dcw_ref)

        dc = dc_ref[...]
        padd[0:ROWS, :] = dc
        padd[ROWS:ROWS + SUBLANES, :] = jnp.where(i == pl.num_programs(0) - 1, 0.0, dnext_ref[...])
        _conv_pad_prev(padx, x_ref[...], xprev_ref[...], i == 0)
        dx = jnp.zeros_like(dc)
        for k in range(CONV_TAPS):
            dx = dx + cw_ref[k:k + 1, :] * padd[pl.ds(CONV_TAPS - 1 - k, ROWS), :]
            dcw_ref[k:k + 1, :] += jnp.sum(dc * padx[pl.ds(SUBLANES - (CONV_TAPS - 1) + k, ROWS), :], axis=0, keepdims=True)
        dcw_ref[CONV_TAPS:CONV_TAPS + 1, :] += jnp.sum(dc, axis=0, keepdims=True)
        dx_ref[...] = dx.astype(BF16)

    return pl.pallas_call(
        body, out_shape=[jax.ShapeDtypeStruct((s, d), BF16), jax.ShapeDtypeStruct((SUBLANES, d), F32)], grid=(s // ROWS,),
        in_specs=[_row_spec(d), pl.BlockSpec((SUBLANES, d), lambda i: (jnp.minimum((i + 1) * hb, last), 0)),
                  pl.BlockSpec((None, ROWS, d), lambda i: (0, i, 0)),
                  pl.BlockSpec((None, SUBLANES, d), lambda i: (0, jnp.maximum(i * hb - 1, 0), 0)), _vec_spec(d, SUBLANES)],
        out_specs=[_row_spec(d), _vec_spec(d, SUBLANES)],
        scratch_shapes=[pltpu.VMEM((ROWS + SUBLANES, d), F32), pltpu.VMEM((ROWS + SUBLANES, d), F32)],
        compiler_params=_params(("arbitrary",)), name="lru_conv_bwd")(dxc, dxc, rest, rest, cw8)


def _coords():
    return lax.axis_index("x"), lax.axis_index("y"), lax.axis_index("c")


def _other_chips(x, y):
    return [(1 - x, y), (x, 1 - y), (1 - x, 1 - y)]


def _slab(ref, kind, shard_shape, idx, half=None):
    r, c = shard_shape
    r0, nr = (0, r) if half is None else (half * (r // 2), r // 2)
    if kind == "col":
        return ref.at[pl.ds(r0, nr), pl.ds(pl.multiple_of(idx * c, LANES), c)]
    if kind == "row":
        return ref.at[pl.ds(pl.multiple_of(idx * r, SUBLANES) + r0, nr), :]
    return ref.at[idx, pl.ds(r0, nr), :]


def _full_shape(shard_shape, kind):
    r, c = shard_shape
    return {"col": (r, c * N_DEV), "row": (r * N_DEV, c), "slot": (N_DEV, r, c)}[kind]


def _handshake(peers):
    barrier = pltpu.get_barrier_semaphore()
    for peer in peers:
        pl.semaphore_signal(barrier, inc=1, device_id=peer, device_id_type=MESH)
    pl.semaphore_wait(barrier, len(peers))


def _launch(name, body, out_shape, operands, sems, sequencer_id):
    if sequencer_id is None:
        return pl.pallas_call(body, out_shape=out_shape, in_specs=[HBM] * len(operands), out_specs=[HBM] * len(out_shape),
                              scratch_shapes=sems, name=name)(*operands)
    return pl.kernel(body, out_type=out_shape, mesh=plsc.ScalarSubcoreMesh(axis_name="seq", num_cores=1), name=name,
                     scratch_types=sems, compiler_params=pltpu.CompilerParams(collective_id=sequencer_id))(*operands)


AG_COPIES = 10


def _all_gather(name, shards, kinds, sequencer_id=None):
    n = len(shards)
    shapes = [s.shape for s in shards]

    def body(*refs):
        ins, outs = refs[:n], refs[n:2 * n]
        send_sems, recv_sems, local_sems = refs[2 * n:]
        x, y, c = _coords()
        me, sib, xn, yn, dg = (x, y, c), (x, y, 1 - c), (1 - x, y, c), (x, 1 - y, c), (1 - x, 1 - y, c)
        if sequencer_id is not None:
            _handshake([sib, xn, yn])

        def part(i, dev, half=None):
            return _slab(outs[i], kinds[i], shapes[i], 4 * dev[0] + 2 * dev[1] + dev[2], half)

        def copy(i, k, block, half, to, own=False):
            r = shapes[i][0]
            src = part(i, block, half) if not own else (ins[i] if half is None else ins[i].at[pl.ds(half * (r // 2), r // 2), :])
            return pltpu.make_async_remote_copy(
                src_ref=src, dst_ref=part(i, block, half), send_sem=send_sems.at[AG_COPIES * i + k],
                recv_sem=recv_sems.at[AG_COPIES * i + k], device_id=to, device_id_type=MESH)

        def other_core(dev):
            return (dev[0], dev[1], 1 - c)

        started = []

        def start(cp):
            cp.start()
            started.append(cp)

        for i in range(n):
            start(copy(i, 1, me, 0, xn, own=True))
            start(copy(i, 4, me, 1, yn, own=True))
            start(copy(i, 2, me, 1, xn, own=True))
            start(copy(i, 3, me, 0, yn, own=True))
            start(copy(i, 0, me, None, sib, own=True))
        mine = [pltpu.make_async_copy(ins[i], part(i, me), local_sems.at[i]) for i in range(n)]
        for cp in mine:
            cp.start()
        for i in range(n):
            copy(i, 1, xn, 0, me).wait_recv()
            start(copy(i, 5, xn, 0, yn))
            copy(i, 4, yn, 1, me).wait_recv()
            start(copy(i, 6, yn, 1, xn))
        for i in range(n):
            copy(i, 2, xn, 1, me).wait_recv()
            start(copy(i, 7, xn, None, sib))
            copy(i, 3, yn, 0, me).wait_recv()
            start(copy(i, 8, yn, None, sib))
        for i in range(n):
            copy(i, 5, dg, 0, me).wait_recv()
            copy(i, 6, dg, 1, me).wait_recv()
            start(copy(i, 9, dg, None, sib))
        for i in range(n):
            copy(i, 0, sib, None, me).wait_recv()
            for k, dev in ((7, xn), (8, yn), (9, dg)):
                copy(i, k, other_core(dev), None, me).wait_recv()
        for cp in started:
            cp.wait_send()
        for cp in mine:
            cp.wait()

    out_shape = [jax.ShapeDtypeStruct(_full_shape(s.shape, k), s.dtype) for s, k in zip(shards, kinds)]
    sems = [pltpu.SemaphoreType.DMA((AG_COPIES * n,)), pltpu.SemaphoreType.DMA((AG_COPIES * n,)), pltpu.SemaphoreType.DMA((n,))]
    return _launch(name, body, out_shape, shards, sems, sequencer_id)


def _sibling_copies(kinds, shard_shapes):
    def make(ins, outs, send_sems, recv_sems):
        x, y, c = _coords()
        return [pltpu.make_async_remote_copy(
            src_ref=_slab(ins[i], kinds[i], shard_shapes[i], 2 * q + (1 - c)), dst_ref=outs[i].at[q],
            send_sem=send_sems.at[N_CHIP * i + q], recv_sem=recv_sems.at[N_CHIP * i + q],
            device_id=(x, y, 1 - c), device_id_type=MESH) for i in range(len(ins)) for q in range(N_CHIP)]
    return make


def _sibling_side(partials, kinds, shard_shapes):
    return (partials, [jax.ShapeDtypeStruct((N_CHIP, *s), BF16) for s in shard_shapes], N_CHIP * len(partials),
            _sibling_copies(kinds, shard_shapes))


def _exchange_siblings(name, partials, kinds, shard_shapes, sequencer_id=None):
    n = len(partials)
    make = _sibling_copies(kinds, shard_shapes)

    def body(*refs):
        if sequencer_id is not None:
            x, y, c = _coords()
            _handshake([(x, y, 1 - c)])
        cps = make(refs[:n], refs[n:2 * n], refs[2 * n], refs[2 * n + 1])
        for cp in cps:
            cp.start()
        for cp in cps:
            cp.wait()

    return _launch(name, body, [jax.ShapeDtypeStruct((N_CHIP, *s), BF16) for s in shard_shapes], partials,
                   [pltpu.SemaphoreType.DMA((N_CHIP * n,)), pltpu.SemaphoreType.DMA((N_CHIP * n,))], sequencer_id)


def _exchange_chips(name, chip_sums, sequencer_id=None):
    n = len(chip_sums)

    def body(*refs):
        ins, outs = refs[:n], refs[n:2 * n]
        send_sems, recv_sems = refs[2 * n:]
        x, y, c = _coords()
        if sequencer_id is not None:
            _handshake([(cx, cy, c) for cx, cy in _other_chips(x, y)])
        cps = []
        for i in range(n):
            for k, (cx, cy) in enumerate(_other_chips(x, y)):
                cps.append(pltpu.make_async_remote_copy(
                    src_ref=ins[i].at[2 * cx + cy], dst_ref=outs[i].at[k], send_sem=send_sems.at[3 * i + k],
                    recv_sem=recv_sems.at[3 * i + k], device_id=(cx, cy, c), device_id_type=MESH))
        for cp in cps:
            cp.start()
        for cp in cps:
            cp.wait()

    return _launch(name, body, [jax.ShapeDtypeStruct((3, *t.shape[1:]), BF16) for t in chip_sums], chip_sums,
                   [pltpu.SemaphoreType.DMA((3 * n,)), pltpu.SemaphoreType.DMA((3 * n,))], sequencer_id)


def _all_reduce_small(name, packed):
    rows = packed.shape[0] // N_DEV

    def body(p_ref, out_ref, rb, tot, send_sems, recv_sems):
        x, y, c = _coords()
        me = 4 * x + 2 * y + c

        def peer(k):
            return (x ^ (k >> 2), y ^ ((k >> 1) & 1), c ^ (k & 1))

        def rows_of(idx):
            return pl.ds(pl.multiple_of(idx * rows, SUBLANES), rows)

        def piece(ref, idx):
            return ref.at[rows_of(idx), :]

        scatter = [pltpu.make_async_remote_copy(src_ref=piece(p_ref, me ^ k), dst_ref=rb.at[k], send_sem=send_sems.at[k],
                                                recv_sem=recv_sems.at[k], device_id=peer(k), device_id_type=MESH) for k in range(1, N_DEV)]
        for cp in scatter:
            cp.start()
        acc = p_ref[rows_of(me), :]
        for cp in scatter:
            cp.wait_recv()
        for k in range(1, N_DEV):
            acc = acc + rb[k]
        tot[...] = acc
        out_ref[rows_of(me), :] = acc
        gather = [pltpu.make_async_remote_copy(src_ref=tot, dst_ref=piece(out_ref, me), send_sem=send_sems.at[N_DEV + k],
                                               recv_sem=recv_sems.at[N_DEV + k], device_id=peer(k), device_id_type=MESH)
                  for k in range(1, N_DEV)]
        for cp in gather:
            cp.start()
        for k in range(1, N_DEV):
            pltpu.make_async_remote_copy(src_ref=tot, dst_ref=piece(out_ref, me ^ k), send_sem=send_sems.at[N_DEV + k],
                                         recv_sem=recv_sems.at[N_DEV + k], device_id=peer(k), device_id_type=MESH).wait_recv()
        for cp in scatter + gather:
            cp.wait_send()

    vm = pl.BlockSpec(memory_space=pltpu.VMEM)
    return pl.pallas_call(
        body, out_shape=jax.ShapeDtypeStruct(packed.shape, F32), in_specs=[vm], out_specs=vm,
        scratch_shapes=[pltpu.VMEM((N_DEV, rows, LANES), F32), pltpu.VMEM((rows, LANES), F32),
                        pltpu.SemaphoreType.DMA((2 * N_DEV,)), pltpu.SemaphoreType.DMA((2 * N_DEV,))],
        compiler_params=pltpu.CompilerParams(vmem_limit_bytes=VMEM_LIMIT), name=name)(packed)


def _adamw_math(g, w, m, v):
    m = ADAM_B1 * m + (1.0 - ADAM_B1) * g
    v = ADAM_B2 * v + (1.0 - ADAM_B2) * (g * g)
    delta = -ADAM_LR * ((m / ADAM_C1) / (jnp.sqrt(v / ADAM_C2) + ADAM_EPS) + ADAM_WD * w)
    return delta, m, v


def _slab_spec(kind, shard_shape, tr, slab_of):
    r, c = shard_shape
    if kind == "col":
        return pl.BlockSpec((tr, c), lambda q, i, sc: (i, slab_of(q, sc)))
    return pl.BlockSpec((tr, c), lambda q, i, sc: (slab_of(q, sc) * (r // tr) + i, 0))


def _chip_sum(name, partial, recv, kind, shard_shape, core):
    r, c = shard_shape
    tr = _blk(r, 256)

    def body(core_ref, p_ref, r_ref, o_ref):
        o_ref[...] = (p_ref[...].astype(F32) + r_ref[...].astype(F32)).astype(BF16)

    spec4 = pl.BlockSpec((None, tr, c), lambda q, i, sc: (q, i, 0))
    grid_spec = pltpu.PrefetchScalarGridSpec(
        num_scalar_prefetch=1, grid=(N_CHIP, r // tr),
        in_specs=[_slab_spec(kind, shard_shape, tr, lambda q, sc: 2 * q + sc[0]), spec4], out_specs=spec4)
    return pl.pallas_call(body, out_shape=jax.ShapeDtypeStruct((N_CHIP, r, c), BF16), grid_spec=grid_spec,
                          compiler_params=_params(("parallel", "parallel")), name=name)(core, partial, recv)


def _adamw_shard(name, parts, w, m, v, chip):
    r, c = w.shape
    n_parts = len(parts)
    tr = _blk(r // n_parts, 128)
    per = r // n_parts // tr

    def body(chip_ref, *refs):
        src, (w_ref, m_ref, v_ref), (g_out, d_out, m_out, v_out) = refs[:2 * n_parts], refs[2 * n_parts:2 * n_parts + 3], refs[2 * n_parts + 3:]
        for p in range(n_parts):
            @pl.when(pl.program_id(0) // per == p)
            def _():
                g = src[2 * p][...].astype(F32)
                for k in range(3):
                    g = g + src[2 * p + 1][k].astype(F32)
                g_out[...] = g
                d_out[...], m_out[...], v_out[...] = _adamw_math(g, w_ref[...], m_ref[...], v_ref[...])

    def part_specs(p):
        at = lambda i: jnp.clip(i - p * per, 0, per - 1)
        return [pl.BlockSpec((None, tr, c), lambda i, sc: (sc[0], at(i), 0)), pl.BlockSpec((3, tr, c), lambda i, sc: (0, at(i), 0))]

    blk = pl.BlockSpec((tr, c), lambda i, sc: (i, 0))
    grid_spec = pltpu.PrefetchScalarGridSpec(
        num_scalar_prefetch=1, grid=(r // tr,), in_specs=[s for p in range(n_parts) for s in part_specs(p)] + [blk, blk, blk], out_specs=[blk] * 4)
    return pl.pallas_call(body, out_shape=[jax.ShapeDtypeStruct((r, c), F32)] * 4, grid_spec=grid_spec,
                          compiler_params=_params(("parallel",)), name=name)(chip, *[a for p in parts for a in p], w, m, v)


def _adamw_small(name, g, w, m, v):
    def body(g_ref, w_ref, m_ref, v_ref, d_out, m_out, v_out):
        d_out[...], m_out[...], v_out[...] = _adamw_math(g_ref[...], w_ref[...], m_ref[...], v_ref[...])

    vm = pl.BlockSpec(memory_space=pltpu.VMEM)
    return pl.pallas_call(body, out_shape=[jax.ShapeDtypeStruct(g.shape, F32)] * 3, in_specs=[vm] * 4, out_specs=[vm] * 3,
                          compiler_params=pltpu.CompilerParams(vmem_limit_bytes=VMEM_LIMIT), name=name)(g, w, m, v)


def _pack_rows(arrays, total_rows):
    flat = [a.reshape(-1, LANES) for a in arrays]
    used = sum(f.shape[0] for f in flat)
    return jnp.concatenate(flat + [jnp.zeros((total_rows - used, LANES), F32)], axis=0)


def _unpack_rows(packed, like):
    out, at = [], 0
    for a in like:
        n = a.size // LANES
        out.append(packed[at:at + n].reshape(a.shape))
        at += n
    return out


def kernel(x, norm_mix_g, w_in, conv_w, conv_b, lru_wa, lru_ba, lru_wx, lru_bx, lru_lambda, w_proj_attn, w_proj_lru, w_out, norm_mlp_g, w_up, w_down, norm_final_g, loss_target, m_norm_mix_g, m_w_in, m_conv_w, m_conv_b, m_lru_wa, m_lru_ba, m_lru_wx, m_lru_bx, m_lru_lambda, m_w_proj_attn, m_w_proj_lru, m_w_out, m_norm_mlp_g, m_w_up, m_w_down, m_norm_final_g, v_norm_mix_g, v_w_in, v_conv_w, v_conv_b, v_lru_wa, v_lru_ba, v_lru_wx, v_lru_bx, v_lru_lambda, v_w_proj_attn, v_w_proj_lru, v_w_out, v_norm_mlp_g, v_w_up, v_w_down, v_norm_final_g):
    xs, tgt = x[0], loss_target[0]
    s, d = xs.shape
    nh = d // HEAD
    ix, iy, ic = _coords()
    core = jnp.reshape(ic, (1,)).astype(jnp.int32)
    chip = jnp.reshape(2 * ix + iy, (1,)).astype(jnp.int32)
    dev = 4 * ix + 2 * iy + ic

    big = [w_in[0], w_proj_attn[0], w_proj_lru[0], w_out[0], w_up[0], w_down[0]]
    big_m = [m_w_in[0], m_w_proj_attn[0], m_w_proj_lru[0], m_w_out[0], m_w_up[0], m_w_down[0]]
    big_v = [v_w_in[0], v_w_proj_attn[0], v_w_proj_lru[0], v_w_out[0], v_w_up[0], v_w_down[0]]
    kinds = ["col", "row", "row", "row", "col", "row"]
    pad_taps = lambda t: jnp.pad(t, ((0, SUBLANES - CONV_TAPS), (0, 0)))
    pad_taps2 = lambda t: jnp.pad(t, ((0, 2 * SUBLANES - CONV_TAPS), (0, 0)))
    shards = [w.astype(BF16) for w in big]
    win, cw_slots = _all_gather("all_gather_w_in", [shards[0], pad_taps2(conv_w[0])], ["col", "slot"])
    later = lax.optimization_barrier((shards[1:], win))[0]
    wpa, wpl, wout = _all_gather("all_gather_mix", later[:3], kinds[1:4], sequencer_id=1)
    wup, wdown = _all_gather("all_gather_mlp", later[3:], kinds[4:], sequencer_id=5)
    cw8 = jnp.transpose(cw_slots[:, :SUBLANES], (1, 0, 2)).reshape(SUBLANES, d)
    row_id = lax.broadcasted_iota(jnp.int32, (SUBLANES, d), 0)
    vec8 = sum(jnp.where(row_id == k, t, 0.0) for k, t in ((VEC_CB, conv_b), (VEC_BA, lru_ba), (VEC_BX, lru_bx), (VEC_LAM, lru_lambda)))
    wa16, wx16 = lru_wa[0].astype(BF16), lru_wx[0].astype(BF16)
    slopes = 2.0 ** (-8.0 * jnp.arange(1, nh + 1, dtype=F32) / nh)

    def seg_specs(*segs):
        return lambda bm, bn: [pl.BlockSpec((None, bm, bn), (lambda i, j, kk, sg=sg: (sg, i, j))) for sg in segs]

    def plain_specs(k):
        return lambda bm, bn: [pl.BlockSpec((bm, bn), lambda i, j, kk: (i, j)) for _ in range(k)]

    xn = _rms_fwd("norm_mix", xs, norm_mix_g)
    qkv = _mm_fwd("proj_qkv", xn, win, 0, 3 * d, [F32], seg_out=d)[0]
    rest = _mm_fwd("proj_rest", xn, win, 3 * d, 4 * d, [F32], seg_out=d)[0]
    att, lse = _attn_fwd(qkv, slopes)
    a3, u3, xc = _gates_fwd(rest, cw8, vec8, wa16, wx16)
    h3, hp3 = _scan_fwd(a3, u3)
    ylru, h2d = _lru_out(h3, rest)
    pa = _mm_fwd("proj_attn", att, wpa, 0, d, [F32])[0]

    def merge(acc, pa_b, ga, gl):
        return acc, _sigmoid(ga) * pa_b + _sigmoid(gl) * acc

    plr, merged = _mm_fwd("proj_lru_merge", ylru, wpl, 0, d, [F32, BF16], merge, (pa, rest, rest),
                          lambda bm, bn: plain_specs(1)(bm, bn) + seg_specs(2, 3)(bm, bn), bm=512)
    h1 = _mm_fwd("mix_out", merged, wout, 0, d, [F32], lambda acc, r: (acc + r,), (xs,), plain_specs(1))[0]
    hn = _rms_fwd("norm_mlp", h1, norm_mlp_g)

    def relu2(acc):
        return acc, jnp.square(jnp.maximum(acc, 0.0))

    up, hid = _mm_fwd("mlp_up", hn, wup, 0, wup.shape[1], [BF16, BF16], relu2)
    h2 = _mm_fwd("mlp_down", hid, wdown, 0, d, [F32], lambda acc, r: (acc + r,), (h1,), plain_specs(1))[0]
    dh2, dh2b, dg3, loss_lanes = _final_loss(h2, tgt, norm_final_g.reshape(1, d))
    loss = lax.psum(0.5 / d * jnp.sum(loss_lanes), ("x", "y", "c"))
    dh2b = lax.optimization_barrier((dh2b, loss))[0]

    def reduce_group(tag, kk, shp, partials, from_sibling, sequencer_id):
        sums = [_chip_sum(f"chip_sum_{tag}_{i}", p, f, k, sh, core) for i, (p, f, k, sh) in enumerate(zip(partials, from_sibling, kk, shp))]
        return list(zip(sums, _exchange_chips(f"rs_chips_{tag}", sums, sequencer_id)))

    dup = _mm_nt("mlp_down_dx", dh2b, wdown, [BF16], lambda acc, u: (acc * (2.0 * jnp.maximum(u.astype(F32), 0.0)),), (up,), plain_specs(1))[0]
    g_wdown = _mm_tn("mlp_down_dw", hid, dh2b)
    g_wup = _mm_tn("mlp_up_dw", hn, dup)
    shp_mlp = [w.shape for w in big[4:]]
    (dhn,), sib_mlp = _mm_nt("mlp_up_dx", dup, wup, [F32], side=_sibling_side([g_wup, g_wdown], kinds[4:], shp_mlp))
    red_up, red_down = reduce_group("mlp", kinds[4:], shp_mlp, [g_wup, g_wdown], sib_mlp, 2)
    dhn = lax.optimization_barrier((dhn, red_up[0], red_down[0]))[0]
    dh1, dh1b, dg2 = _rms_bwd("norm_mlp_bwd", h1, norm_mlp_g, dhn, dh2)

    def merge_bwd(acc, pa_b, pl_b, ga, gl):
        sa, sl = _sigmoid(ga), _sigmoid(gl)
        return acc * sa, acc * sl, acc * pa_b * sa * (1.0 - sa), acc * pl_b * sl * (1.0 - sl)

    dpa, dpl, dga, dgl = _mm_nt("mix_out_dx", dh1b, wout, [BF16] * 4, merge_bwd, (pa, plr, rest, rest),
                                lambda bm, bn: plain_specs(2)(bm, bn) + seg_specs(2, 3)(bm, bn), bm=512)
    g_wout = _mm_tn("mix_out_dw", merged, dh1b)
    datt = _mm_nt("proj_attn_dx", dpa, wpa, [F32])[0]
    g_wpa = _mm_tn("proj_attn_dw", att, dpa)

    def lru_out_bwd(acc, h_b, gate):
        return acc * _gelu(gate), acc * h_b * _gelu_grad(gate)

    g_wpl = _mm_tn("proj_lru_dw", ylru, dpl)
    shp_mix = [w.shape for w in big[1:4]]
    (dh, dxg), sib_mix = _mm_nt("proj_lru_dx", dpl, wpl, [F32, BF16], lru_out_bwd, (h2d, rest),
                                lambda bm, bn: plain_specs(1)(bm, bn) + seg_specs(1)(bm, bn), bm=512,
                                side=_sibling_side([g_wpa, g_wpl, g_wout], kinds[1:4], shp_mix))
    red_pa, red_pl, red_out = reduce_group("mix", kinds[1:4], shp_mix, [g_wpa, g_wpl, g_wout], sib_mix, 3)
    dq, dk, dv = _attn_bwd(qkv, datt, att, lse, slopes)
    g3, da3 = _scan_bwd(a3, hp3, dh)
    dxc, dwa, dwx, dvec = _gates_bwd(g3, da3, xc, wa16, wx16, vec8)
    dxr, dconv = _conv_bwd(dxc, rest, cw8)

    def small_step(tag, grads, ws, ms, vs, like, after):
        n_rows = sum(g.size for g in grads) // LANES
        per_dev = -(-n_rows // (N_DEV * SUBLANES)) * SUBLANES
        packed = lax.optimization_barrier((_pack_rows(grads, N_DEV * per_dev), after))[0]
        total = _all_reduce_small(f"all_reduce_{tag}", packed)
        w_rows = -(-(sum(w.size for w in ws) // LANES) // SUBLANES) * SUBLANES
        upd = _adamw_small(f"adamw_{tag}", total[:w_rows], _pack_rows(ws, w_rows), _pack_rows(ms, w_rows), _pack_rows(vs, w_rows))
        return _unpack_rows(total, like), [_unpack_rows(t, ws) for t in upd]

    early_w = [conv_b, lru_wa, lru_ba, lru_wx, lru_bx, lru_lambda, norm_mlp_g, norm_final_g]
    early_m = [m_conv_b, m_lru_wa, m_lru_ba, m_lru_wx, m_lru_bx, m_lru_lambda, m_norm_mlp_g, m_norm_final_g]
    early_v = [v_conv_b, v_lru_wa, v_lru_ba, v_lru_wx, v_lru_bx, v_lru_lambda, v_norm_mlp_g, v_norm_final_g]
    early_g = [dconv[CONV_TAPS:CONV_TAPS + 1], dwa, dvec[VEC_BA:VEC_BA + 1], dwx, dvec[VEC_BX:VEC_BX + 1],
               dvec[VEC_LAM:VEC_LAM + 1], dg2, dg3, dconv[0:CONV_TAPS]]
    early_sum, early_upd = small_step("small", early_g, early_w, early_m, early_v,
                                      early_w + [jax.ShapeDtypeStruct((1, CONV_TAPS, d), F32)], dq)
    g_cw_full = early_sum[-1]
    cshard = conv_w.shape[2]
    g_cw = lax.dynamic_slice(g_cw_full, (0, 0, dev * cshard), (1, CONV_TAPS, cshard))
    cw_delta, cw_m, cw_v = (t[:CONV_TAPS][None] for t in _adamw_small(
        "adamw_conv_w", pad_taps(g_cw[0]), pad_taps(conv_w[0]), pad_taps(m_conv_w[0]), pad_taps(v_conv_w[0])))
    dq = lax.optimization_barrier((dq, early_sum))[0]

    dproj = jnp.concatenate([dq, dk, dv, dxr, dxg, dga, dgl], axis=1)
    half = (big[0].shape[0] // 2, big[0].shape[1])
    g_in0 = _mm_tn("proj_in_dw_0", xn, dproj, part=(0, 2))
    g_in1, sib_in0 = _mm_tn("proj_in_dw_1", xn, dproj, part=(1, 2), side=_sibling_side([g_in0], ["col"], [half]))
    red_in = reduce_group("in_0", ["col"], [half], [g_in0], sib_in0, 4)
    dproj = lax.optimization_barrier((dproj, red_in[0][0]))[0]
    (dxn0,), sib_in1 = _mm_nt("proj_in_dx_0", dproj, win, [F32], part=(0, 2), side=_sibling_side([g_in1], ["col"], [half]))
    red_in += reduce_group("in_1", ["col"], [half], [g_in1], sib_in1, 6)
    dproj = lax.optimization_barrier((dproj, red_in[1][0]))[0]
    dxn1 = _mm_nt("proj_in_dx_1", dproj, win, [F32], part=(1, 2))[0]
    dxn = jnp.concatenate([dxn0, dxn1], axis=0)
    grad_x, _, dg1 = _rms_bwd("norm_mix_bwd", xs, norm_mix_g, dxn, dh1)
    red_up, red_down = lax.optimization_barrier(((red_up, red_down), dg1))[0]
    big_out = {i: _adamw_shard(f"adamw_{i}", [red], big[i], big_m[i], big_v[i], chip) for i, red in ((4, red_up), (5, red_down))}
    red_mix = lax.optimization_barrier(((red_pa, red_pl, red_out), (big_out[4], big_out[5])))[0]
    big_out.update({i: _adamw_shard(f"adamw_{i}", [red], big[i], big_m[i], big_v[i], chip) for i, red in zip((1, 2, 3), red_mix)})
    late_sum, late_upd = small_step("norm_mix", [dg1], [norm_mix_g], [m_norm_mix_g], [v_norm_mix_g], [norm_mix_g], big_out[3])
    big_out[0] = _adamw_shard("adamw_0", red_in, big[0], big_m[0], big_v[0], chip)
    s_grad = late_sum + early_sum[:-1]
    s_delta, s_m, s_v = (late_upd[j] + early_upd[j] for j in range(3))


    names = ["norm_mix_g", "w_in", "conv_w", "conv_b", "lru_wa", "lru_ba", "lru_wx", "lru_bx", "lru_lambda", "w_proj_attn", "w_proj_lru",
             "w_out", "norm_mlp_g", "w_up", "w_down", "norm_final_g"]
    small_names = ["norm_mix_g", "conv_b", "lru_wa", "lru_ba", "lru_wx", "lru_bx", "lru_lambda", "norm_mlp_g", "norm_final_g"]
    big_names = ["w_in", "w_proj_attn", "w_proj_lru", "w_out", "w_up", "w_down"]
    res = {"conv_w": (g_cw, cw_delta, cw_m, cw_v)}
    for i, nm in enumerate(small_names):
        res[nm] = (s_grad[i], s_delta[i], s_m[i], s_v[i])
    for i, nm in enumerate(big_names):
        res[nm] = tuple(t[None] for t in big_out[i])
    return (loss, grad_x[None], *[res[nm][0] for nm in names], *[res[nm][1] for nm in names],
            *[res[nm][2] for nm in names], *[res[nm][3] for nm in names])
```

```python
import jax
import jax.numpy as jnp
from jax import lax
from jax.experimental import pallas as pl
from jax.experimental.pallas import tpu as pltpu
from jax.experimental.pallas import tpu_sc as plsc

F32, BF16 = jnp.float32, jnp.bfloat16
MESH = pl.DeviceIdType.MESH
HBM = pl.BlockSpec(memory_space=pltpu.HBM)
N_DEV = 8
N_CHIP = 4
HEAD = 128
SPAN = 128
DILATIONS = (1, 4, 16)
CONV_TAPS = 4
LRU_C = 8.0
NORM_EPS = 1e-6
LANES = 128
SUBLANES = 8
VMEM_LIMIT = 56 * 1024 * 1024
ADAM_LR, ADAM_B1, ADAM_B2, ADAM_EPS, ADAM_WD, ADAM_STEP = 0.001, 0.9, 0.999, 1e-08, 0.01, 10
ADAM_C1 = 1.0 - ADAM_B1 ** ADAM_STEP
ADAM_C2 = 1.0 - ADAM_B2 ** ADAM_STEP
NEG = -1e30


def _params(sem=None):
    return pltpu.CompilerParams(dimension_semantics=sem, vmem_limit_bytes=VMEM_LIMIT)


def _sigmoid(v):
    return 1.0 / (1.0 + jnp.exp(-v))


def _gelu(v):
    k = 0.7978845608028654
    return 0.5 * v * (1.0 + jnp.tanh(k * (v + 0.044715 * v * v * v)))


def _gelu_grad(v):
    k = 0.7978845608028654
    t = jnp.tanh(k * (v + 0.044715 * v * v * v))
    return 0.5 * (1.0 + t) + 0.5 * v * (1.0 - t * t) * k * (1.0 + 3.0 * 0.044715 * v * v)


NN = (((1,), (0,)), ((), ()))
NT = (((1,), (1,)), ((), ()))
TN = (((0,), (0,)), ((), ()))


def _mm(name, a, a_spec, b, b_spec, dn, grid, out_shapes, out_specs, acc_block, epilogue=None, extras=(), extra_specs=(), side=None):
    nk, ne, no = grid[2], len(extras), len(out_shapes)
    side_ops, side_shapes, side_copies, make_copies = side if side is not None else ((), (), 0, None)
    ns_in, ns_out = len(side_ops), len(side_shapes)

    def body(*refs):
        a_ref, b_ref = refs[0], refs[1]
        ex, side_in = refs[2:2 + ne], refs[2 + ne:2 + ne + ns_in]
        outs = refs[2 + ne + ns_in:2 + ne + ns_in + no]
        side_out = refs[2 + ne + ns_in + no:2 + ne + ns_in + no + ns_out]
        scratch = refs[2 + ne + ns_in + no + ns_out:]
        at = [pl.program_id(ax) for ax in range(3)]
        if side is not None:
            @pl.when((at[0] == 0) & (at[1] == 0) & (at[2] == 0))
            def _():
                for cp in make_copies(side_in, side_out, scratch[-2], scratch[-1]):
                    cp.start()

        part = lax.dot_general(a_ref[...], b_ref[...], dn, preferred_element_type=F32)

        def finish(acc):
            vals = epilogue(acc, *[e[...] for e in ex]) if epilogue is not None else (acc,)
            for o, v in zip(outs, vals):
                o[...] = v.astype(o.dtype)

        if nk == 1:
            finish(part)
        else:
            acc_ref, k = scratch[0], at[2]

            @pl.when(k == 0)
            def _():
                acc_ref[...] = part

            @pl.when(k > 0)
            def _():
                acc_ref[...] += part

            @pl.when(k == nk - 1)
            def _():
                finish(acc_ref[...])

        if side is not None:
            @pl.when((at[0] == grid[0] - 1) & (at[1] == grid[1] - 1) & (at[2] == grid[2] - 1))
            def _():
                for cp in make_copies(side_in, side_out, scratch[-2], scratch[-1]):
                    cp.wait()

    scratch_shapes = [pltpu.VMEM(acc_block, F32)] if nk > 1 else []
    if side is not None:
        scratch_shapes += [pltpu.SemaphoreType.DMA((side_copies,)), pltpu.SemaphoreType.DMA((side_copies,))]
    res = pl.pallas_call(
        body, out_shape=[*out_shapes, *side_shapes], grid=grid, in_specs=[a_spec, b_spec, *extra_specs, *[HBM] * ns_in],
        out_specs=[*out_specs, *[HBM] * ns_out], scratch_shapes=scratch_shapes,
        compiler_params=_params(("arbitrary",) * 3 if side is not None else ("parallel", "parallel", "arbitrary")),
        name=name)(a, b, *extras, *side_ops)
    return res if side is None else (res[:no], res[no:])


def _blk(n, pref):
    return pref if n % pref == 0 else n


def _kblk(k):
    return k if k <= 2048 else next(b for b in (2048, 1024, 512) if k % b == 0)


def _mm_fwd(name, a, w, col0, ncols, out_dtypes, epilogue=None, extras=(), extra_specs_fn=None, seg_out=None, bm=1024, bn=1024):
    m, k = a.shape
    bm, bn = _blk(m, bm), _blk(ncols, bn)
    bk = _kblk(k)
    nk = k // bk
    cb0 = col0 // bn
    grid = (m // bm, ncols // bn, nk)
    a_spec = pl.BlockSpec((bm, bk), lambda i, j, kk: (i, kk))
    b_spec = pl.BlockSpec((bk, bn), lambda i, j, kk: (kk, cb0 + j))
    if seg_out is None:
        shapes = [jax.ShapeDtypeStruct((m, ncols), dt) for dt in out_dtypes]
        specs = [pl.BlockSpec((bm, bn), lambda i, j, kk: (i, j)) for _ in out_dtypes]
    else:
        per = seg_out // bn
        shapes = [jax.ShapeDtypeStruct((ncols // seg_out, m, seg_out), dt) for dt in out_dtypes]
        specs = [pl.BlockSpec((None, bm, bn), lambda i, j, kk: (j // per, i, j % per)) for _ in out_dtypes]
    ex_specs = extra_specs_fn(bm, bn) if extra_specs_fn else ()
    return _mm(name, a, a_spec, w, b_spec, NN, grid, shapes, specs, (bm, bn), epilogue, extras, ex_specs)


def _mm_nt(name, a, w, out_dtypes, epilogue=None, extras=(), extra_specs_fn=None, part=(0, 1), side=None, bm=1024, bn=1024):
    m, k = a.shape
    n = w.shape[0]
    m = m // part[1]
    bm, bn = _blk(m, bm), _blk(n, bn)
    bk = _kblk(k)
    grid = (m // bm, n // bn, k // bk)
    i0 = part[0] * (m // bm)
    a_spec = pl.BlockSpec((bm, bk), lambda i, j, kk: (i0 + i, kk))
    b_spec = pl.BlockSpec((bn, bk), lambda i, j, kk: (j, kk))
    shapes = [jax.ShapeDtypeStruct((m, n), dt) for dt in out_dtypes]
    specs = [pl.BlockSpec((bm, bn), lambda i, j, kk: (i, j)) for _ in out_dtypes]
    ex_specs = extra_specs_fn(bm, bn) if extra_specs_fn else ()
    return _mm(name, a, a_spec, w, b_spec, NT, grid, shapes, specs, (bm, bn), epilogue, extras, ex_specs, side)


def _mm_tn(name, a, b, part=(0, 1), side=None, bm=1024, bn=1024):
    t, m = a.shape
    n = b.shape[1]
    m = m // part[1]
    bm, bn = _blk(m, bm), _blk(n, bn)
    grid = (m // bm, n // bn, 1)
    i0 = part[0] * (m // bm)
    a_spec = pl.BlockSpec((t, bm), lambda i, j, kk: (0, i0 + i))
    b_spec = pl.BlockSpec((t, bn), lambda i, j, kk: (0, j))
    res = _mm(name, a, a_spec, b, b_spec, TN, grid, [jax.ShapeDtypeStruct((m, n), BF16)],
              [pl.BlockSpec((bm, bn), lambda i, j, kk: (i, j))], (bm, bn), side=side)
    return res[0] if side is None else (res[0][0], res[1])


ROWS = 256


def _row_spec(d):
    return pl.BlockSpec((ROWS, d), lambda i: (i, 0))


def _vec_spec(d, rows=1):
    return pl.BlockSpec((rows, d), lambda i: (0, 0))


def _rms_fwd(name, x, g):
    s, d = x.shape

    def body(x_ref, g_ref, o_ref):
        xv = x_ref[...]
        r = lax.rsqrt(jnp.mean(xv * xv, axis=-1, keepdims=True) + NORM_EPS)
        o_ref[...] = (xv * r * g_ref[...]).astype(BF16)

    return pl.pallas_call(body, out_shape=jax.ShapeDtypeStruct((s, d), BF16), grid=(s // ROWS,),
                          in_specs=[_row_spec(d), _vec_spec(d)], out_specs=_row_spec(d),
                          compiler_params=_params(("parallel",)), name=name)(x, g)


def _rms_bwd_math(xv, g, dy):
    r = lax.rsqrt(jnp.mean(xv * xv, axis=-1, keepdims=True) + NORM_EPS)
    n = xv * r
    z = dy * g
    dx = r * (z - n * jnp.mean(z * n, axis=-1, keepdims=True))
    return dx, jnp.sum(dy * n, axis=0, keepdims=True)


def _rms_bwd(name, x, g, dy, resid):
    s, d = x.shape

    def body(x_ref, g_ref, dy_ref, r_ref, dx_ref, dxb_ref, dg_ref):
        dx, dg = _rms_bwd_math(x_ref[...], g_ref[...], dy_ref[...])
        dx = dx + r_ref[...]
        dx_ref[...] = dx
        dxb_ref[...] = dx.astype(BF16)

        @pl.when(pl.program_id(0) == 0)
        def _():
            dg_ref[...] = jnp.zeros_like(dg_ref)

        dg_ref[...] += dg

    return pl.pallas_call(
        body, out_shape=[jax.ShapeDtypeStruct((s, d), F32), jax.ShapeDtypeStruct((s, d), BF16), jax.ShapeDtypeStruct((1, d), F32)],
        grid=(s // ROWS,), in_specs=[_row_spec(d), _vec_spec(d), _row_spec(d), _row_spec(d)],
        out_specs=[_row_spec(d), _row_spec(d), _vec_spec(d)], compiler_params=_params(("arbitrary",)), name=name)(x, g, dy, resid)


def _final_loss(h2, tgt, g):
    s, d = h2.shape

    def body(x_ref, t_ref, g_ref, dx_ref, dxb_ref, dg_ref, ls_ref):
        xv, gv = x_ref[...], g_ref[...]
        r = lax.rsqrt(jnp.mean(xv * xv, axis=-1, keepdims=True) + NORM_EPS)
        diff = xv * r * gv - t_ref[...]
        dx, dg = _rms_bwd_math(xv, gv, diff * (1.0 / d))
        dx_ref[...] = dx
        dxb_ref[...] = dx.astype(BF16)

        @pl.when(pl.program_id(0) == 0)
        def _():
            dg_ref[...] = jnp.zeros_like(dg_ref)
            ls_ref[...] = jnp.zeros_like(ls_ref)

        dg_ref[...] += dg
        ls_ref[...] += jnp.sum(diff * diff, axis=0, keepdims=True)

    return pl.pallas_call(
        body, out_shape=[jax.ShapeDtypeStruct((s, d), F32), jax.ShapeDtypeStruct((s, d), BF16),
                         jax.ShapeDtypeStruct((1, d), F32), jax.ShapeDtypeStruct((1, d), F32)],
        grid=(s // ROWS,), in_specs=[_row_spec(d), _row_spec(d), _vec_spec(d)],
        out_specs=[_row_spec(d), _row_spec(d), _vec_spec(d), _vec_spec(d)],
        compiler_params=_params(("arbitrary",)), name="final_norm_loss")(h2, tgt, g)


ATTN_Q = 128


ATTN_BATCH = 8


def _attn_units(s):
    units = []
    for gi, d in enumerate(DILATIONS):
        for r in range(d):
            for q0 in range(0, s // d, ATTN_Q):
                k0 = max(q0 - SPAN, 0)
                units.append((gi, d, r, q0, k0, q0 + ATTN_Q - k0))
    return units


def _stream_rows(d, r, start, size):
    return pl.ds(r + start * d, size) if d == 1 else pl.ds(r + start * d, size, stride=d)


def _attn_scores(q_ref, k_ref, slope, d, r, q0, k0, nk):
    qrows, krows = _stream_rows(d, r, q0, ATTN_Q), _stream_rows(d, r, k0, nk)
    qb = q_ref[qrows, :].astype(BF16)
    kb = k_ref[krows, :].astype(BF16)
    sc = lax.dot_general(qb, kb, NT, preferred_element_type=F32) * (HEAD ** -0.5)
    qi = lax.broadcasted_iota(jnp.int32, (ATTN_Q, nk), 0)
    kj = lax.broadcasted_iota(jnp.int32, (ATTN_Q, nk), 1)
    dist = (q0 - k0) + qi - kj
    valid = (dist >= 0) & (dist <= SPAN)
    sc = sc - (slope * d) * dist.astype(F32)
    return jnp.where(valid, sc, NEG), valid, qb, kb, qrows, krows


def _attn_fwd(qkv, slopes):
    _, s, dm = qkv.shape
    units = _attn_units(s)

    def body(sl_ref, q_ref, k_ref, v_ref, att_ref, lse_ref, *scr):
        o_scr, l_scr = scr[:3], scr[3:]
        slope = sl_ref[pl.program_id(0)]
        for first in range(0, len(units), ATTN_BATCH):
            batch = units[first:first + ATTN_BATCH]
            scored = [_attn_scores(q_ref, k_ref, slope, d, r, q0, k0, nk) for _, d, r, q0, k0, nk in batch]
            soft = []
            for sc, _, _, _, _, _ in scored:
                m = jnp.max(sc, axis=-1, keepdims=True)
                p = jnp.exp(sc - m)
                soft.append((m, p, jnp.sum(p, axis=-1, keepdims=True)))
            outs = [lax.dot_general(p.astype(BF16), v_ref[sco[5], :].astype(BF16), NN, preferred_element_type=F32)
                    for (m, p, l), sco in zip(soft, scored)]
            for (gi, *_), (m, p, l), sco, o in zip(batch, soft, scored, outs):
                o_scr[gi][sco[4], :] = o / l
                l_scr[gi][sco[4], :] = jnp.broadcast_to(m + jnp.log(l), (ATTN_Q, HEAD))
        l0, l1, l2 = l_scr[0][...], l_scr[1][...], l_scr[2][...]
        m = jnp.maximum(jnp.maximum(l0, l1), l2)
        w0, w1, w2 = jnp.exp(l0 - m), jnp.exp(l1 - m), jnp.exp(l2 - m)
        tot = w0 + w1 + w2
        att_ref[...] = ((w0 * o_scr[0][...] + w1 * o_scr[1][...] + w2 * o_scr[2][...]) / tot).astype(BF16)
        lse_ref[...] = m + jnp.log(tot)

    def seg(i):
        return pl.BlockSpec((None, s, HEAD), lambda h: (i, 0, h))

    col = pl.BlockSpec((s, HEAD), lambda h: (0, h))
    return pl.pallas_call(
        body, out_shape=[jax.ShapeDtypeStruct((s, dm), BF16), jax.ShapeDtypeStruct((s, dm), F32)], grid=(dm // HEAD,),
        in_specs=[pl.BlockSpec(memory_space=pltpu.SMEM), seg(0), seg(1), seg(2)], out_specs=[col, col],
        scratch_shapes=[pltpu.VMEM((s, HEAD), F32)] * (2 * len(DILATIONS)),
        compiler_params=_params(("parallel",)), name="attn_fwd")(slopes, qkv, qkv, qkv)


def _attn_bwd(qkv, datt, att, lse, slopes):
    _, s, dm = qkv.shape
    units = _attn_units(s)

    def body(sl_ref, q_ref, k_ref, v_ref, do_ref, att_ref, lse_ref, dq_ref, dk_ref, dv_ref, dq_scr, dk_scr, dv_scr, dl_scr):
        slope = sl_ref[pl.program_id(0)]
        delta = jnp.sum(do_ref[...] * att_ref[...].astype(F32), axis=-1, keepdims=True)
        dl_scr[...] = jnp.broadcast_to(delta, (s, HEAD))
        dq_scr[...] = jnp.zeros_like(dq_scr)
        dk_scr[...] = jnp.zeros_like(dk_scr)
        dv_scr[...] = jnp.zeros_like(dv_scr)
        for first in range(0, len(units), ATTN_BATCH):
            scored = [_attn_scores(q_ref, k_ref, slope, d, r, q0, k0, nk) for _, d, r, q0, k0, nk in units[first:first + ATTN_BATCH]]
            dobs = [do_ref[sco[4], :].astype(BF16) for sco in scored]
            dps = [lax.dot_general(dob, v_ref[sco[5], :].astype(BF16), NT, preferred_element_type=F32) for dob, sco in zip(dobs, scored)]
            ps = [jnp.where(sco[1], jnp.exp(sco[0] - lse_ref[sco[4], :][:, 0:1]), 0.0) for sco in scored]
            dss = [(p * (dp - dl_scr[sco[4], :][:, 0:1]) * (HEAD ** -0.5)).astype(BF16) for p, dp, sco in zip(ps, dps, scored)]
            dqs = [lax.dot_general(ds, sco[3], NN, preferred_element_type=F32) for ds, sco in zip(dss, scored)]
            dks = [lax.dot_general(ds, sco[2], TN, preferred_element_type=F32) for ds, sco in zip(dss, scored)]
            dvs = [lax.dot_general(p.astype(BF16), dob, TN, preferred_element_type=F32) for p, dob in zip(ps, dobs)]
            for sco, dq, dk, dv in zip(scored, dqs, dks, dvs):
                dq_scr[sco[4], :] += dq
                dk_scr[sco[5], :] += dk
                dv_scr[sco[5], :] += dv
        dq_ref[...] = dq_scr[...].astype(BF16)
        dk_ref[...] = dk_scr[...].astype(BF16)
        dv_ref[...] = dv_scr[...].astype(BF16)

    def seg(i):
        return pl.BlockSpec((None, s, HEAD), lambda h: (i, 0, h))

    col = pl.BlockSpec((s, HEAD), lambda h: (0, h))
    return pl.pallas_call(
        body, out_shape=[jax.ShapeDtypeStruct((s, dm), BF16)] * 3, grid=(dm // HEAD,),
        in_specs=[pl.BlockSpec(memory_space=pltpu.SMEM), seg(0), seg(1), seg(2), col, col, col], out_specs=[col, col, col],
        scratch_shapes=[pltpu.VMEM((s, HEAD), F32)] * 4,
        compiler_params=_params(("parallel",)), name="attn_bwd")(slopes, qkv, qkv, qkv, datt, att, lse)


VEC_CB, VEC_BA, VEC_BX, VEC_LAM = 0, 1, 2, 3


def _to_3d(ref3, val):
    lw = val.shape[1] // SUBLANES
    for j in range(SUBLANES):
        ref3[:, j, :] = val[:, j * lw:(j + 1) * lw]


def _from_3d(ref3):
    return jnp.concatenate([ref3[:, j, :] for j in range(SUBLANES)], axis=1)


def _softplus(z):
    return jnp.maximum(z, 0.0) + jnp.log1p(jnp.exp(-jnp.abs(z)))


def _gate_math(xc, wa_ref, wx_ref, vec):
    xcb = xc.astype(BF16)
    nh = xc.shape[1] // HEAD
    pre_a = jnp.concatenate([jnp.dot(xcb[:, h * HEAD:(h + 1) * HEAD], wa_ref[h], preferred_element_type=F32) for h in range(nh)], axis=1)
    pre_x = jnp.concatenate([jnp.dot(xcb[:, h * HEAD:(h + 1) * HEAD], wx_ref[h], preferred_element_type=F32) for h in range(nh)], axis=1)
    ra = _sigmoid(pre_a + vec[VEC_BA:VEC_BA + 1])
    ig = _sigmoid(pre_x + vec[VEC_BX:VEC_BX + 1])
    sp = _softplus(-vec[VEC_LAM:VEC_LAM + 1])
    log_a = -LRU_C * ra * sp
    a = jnp.exp(log_a)
    z = 2.0 * log_a
    one_minus_a2 = jnp.where(z > -0.01, -z * (1.0 + z * (0.5 + z * (1.0 / 6.0))), 1.0 - jnp.exp(z))
    mult = jnp.sqrt(one_minus_a2)
    return dict(xcb=xcb, ra=ra, ig=ig, sp=sp, a=a, mult=mult)


def _conv_pad_prev(pad_ref, cur, halo, first):
    pad_ref[0:SUBLANES, :] = jnp.where(first, 0.0, halo)
    pad_ref[SUBLANES:SUBLANES + cur.shape[0], :] = cur


def _gates_fwd(rest, cw8, vec8, wa, wx):
    _, s, d = rest.shape
    lw = d // SUBLANES
    hb = ROWS // SUBLANES

    def body(x_ref, halo_ref, cw_ref, vec_ref, wa_ref, wx_ref, a_ref, u_ref, xc_ref, pad):
        _conv_pad_prev(pad, x_ref[...], halo_ref[...], pl.program_id(0) == 0)
        vec = vec_ref[...]
        xc = vec[VEC_CB:VEC_CB + 1]
        for k in range(CONV_TAPS):
            xc = xc + cw_ref[k:k + 1, :] * pad[pl.ds(SUBLANES - (CONV_TAPS - 1) + k, ROWS), :]
        gm = _gate_math(xc, wa_ref, wx_ref, vec)
        xc_ref[...] = xc
        _to_3d(a_ref, gm["a"])
        _to_3d(u_ref, gm["mult"] * (gm["ig"] * xc))

    spec3 = pl.BlockSpec((ROWS, SUBLANES, lw), lambda i: (i, 0, 0))
    wspec = pl.BlockSpec(wa.shape, lambda i: (0, 0, 0))
    return pl.pallas_call(
        body, out_shape=[jax.ShapeDtypeStruct((s, SUBLANES, lw), F32)] * 2 + [jax.ShapeDtypeStruct((s, d), F32)], grid=(s // ROWS,),
        in_specs=[pl.BlockSpec((None, ROWS, d), lambda i: (0, i, 0)),
                  pl.BlockSpec((None, SUBLANES, d), lambda i: (0, jnp.maximum(i * hb - 1, 0), 0)),
                  _vec_spec(d, SUBLANES), _vec_spec(d, SUBLANES), wspec, wspec],
        out_specs=[spec3, spec3, _row_spec(d)], scratch_shapes=[pltpu.VMEM((ROWS + SUBLANES, d), F32)],
        compiler_params=_params(("parallel",)), name="lru_gates_fwd")(rest, rest, cw8, vec8, wa, wx)


def _scan_fwd(a3, u3):
    s, _, lw = a3.shape

    def body(a_ref, u_ref, h_ref, hp_ref, carry):
        @pl.when(pl.program_id(0) == 0)
        def _():
            carry[...] = jnp.zeros_like(carry)

        def step(t, h):
            hp_ref[t] = h
            hn = a_ref[t] * h + u_ref[t]
            h_ref[t] = hn
            return hn

        carry[...] = lax.fori_loop(0, ROWS, step, carry[...], unroll=8)

    spec3 = pl.BlockSpec((ROWS, SUBLANES, lw), lambda i: (i, 0, 0))
    return pl.pallas_call(body, out_shape=[jax.ShapeDtypeStruct(a3.shape, F32)] * 2, grid=(s // ROWS,), in_specs=[spec3, spec3],
                          out_specs=[spec3, spec3], scratch_shapes=[pltpu.VMEM((SUBLANES, lw), F32)],
                          compiler_params=_params(("arbitrary",)), name="lru_scan_fwd")(a3, u3)


def _lru_out(h3, rest):
    s, _, lw = h3.shape
    d = lw * SUBLANES

    def body(h_ref, g_ref, y_ref, h2_ref):
        h = _from_3d(h_ref)
        h2_ref[...] = h
        y_ref[...] = (h * _gelu(g_ref[...])).astype(BF16)

    return pl.pallas_call(
        body, out_shape=[jax.ShapeDtypeStruct((s, d), BF16), jax.ShapeDtypeStruct((s, d), F32)], grid=(s // ROWS,),
        in_specs=[pl.BlockSpec((ROWS, SUBLANES, lw), lambda i: (i, 0, 0)), pl.BlockSpec((None, ROWS, d), lambda i: (1, i, 0))],
        out_specs=[_row_spec(d), _row_spec(d)], compiler_params=_params(("parallel",)), name="lru_out")(h3, rest)


def _scan_bwd(a3, hp3, dh):
    s, _, lw = a3.shape
    d = lw * SUBLANES
    nb = s // ROWS

    def body(a_ref, hp_ref, dh_ref, g_ref, da_ref, dh3, carry):
        @pl.when(pl.program_id(0) == 0)
        def _():
            carry[...] = jnp.zeros_like(carry)

        _to_3d(dh3, dh_ref[...])

        def step(j, c):
            t = ROWS - 1 - j
            g = dh3[t] + c
            g_ref[t] = g
            da_ref[t] = g * hp_ref[t]
            return a_ref[t] * g

        carry[...] = lax.fori_loop(0, ROWS, step, carry[...], unroll=8)

    spec3 = pl.BlockSpec((ROWS, SUBLANES, lw), lambda i: (nb - 1 - i, 0, 0))
    return pl.pallas_call(
        body, out_shape=[jax.ShapeDtypeStruct(a3.shape, F32)] * 2, grid=(nb,),
        in_specs=[spec3, spec3, pl.BlockSpec((ROWS, d), lambda i: (nb - 1 - i, 0))], out_specs=[spec3, spec3],
        scratch_shapes=[pltpu.VMEM((ROWS, SUBLANES, lw), F32), pltpu.VMEM((SUBLANES, lw), F32)],
        compiler_params=_params(("arbitrary",)), name="lru_scan_bwd")(a3, hp3, dh)


def _gates_bwd(g3, da3, xc, wa, wx, vec8):
    s, d = xc.shape
    lw = d // SUBLANES
    nh = d // HEAD

    def body(g_ref, da_ref, xc_ref, wa_ref, wx_ref, vec_ref, dxc_ref, dwa_ref, dwx_ref, dvec_ref):
        @pl.when(pl.program_id(0) == 0)
        def _():
            dwa_ref[...] = jnp.zeros_like(dwa_ref)
            dwx_ref[...] = jnp.zeros_like(dwx_ref)
            dvec_ref[...] = jnp.zeros_like(dvec_ref)

        xc_v, vec = xc_ref[...], vec_ref[...]
        du, da = _from_3d(g_ref), _from_3d(da_ref)
        gm = _gate_math(xc_v, wa_ref, wx_ref, vec)
        ra, ig, sp, a, mult = gm["ra"], gm["ig"], gm["sp"], gm["a"], gm["mult"]
        dmult = du * ig * xc_v
        dlog_a = da * a - dmult * (a * a) / mult
        dpre_a = dlog_a * (-LRU_C * sp) * ra * (1.0 - ra)
        dpre_x = du * mult * xc_v * ig * (1.0 - ig)
        dlam = jnp.sum(dlog_a * (-LRU_C * ra), axis=0, keepdims=True) * (-_sigmoid(-vec[VEC_LAM:VEC_LAM + 1]))
        dvec_ref[VEC_BA:VEC_BA + 1, :] += jnp.sum(dpre_a, axis=0, keepdims=True)
        dvec_ref[VEC_BX:VEC_BX + 1, :] += jnp.sum(dpre_x, axis=0, keepdims=True)
        dvec_ref[VEC_LAM:VEC_LAM + 1, :] += dlam
        dab, dxb, xcb = dpre_a.astype(BF16), dpre_x.astype(BF16), gm["xcb"]
        back = []
        for h in range(nh):
            cols = slice(h * HEAD, (h + 1) * HEAD)
            dwa_ref[h] += lax.dot_general(xcb[:, cols], dab[:, cols], TN, preferred_element_type=F32)
            dwx_ref[h] += lax.dot_general(xcb[:, cols], dxb[:, cols], TN, preferred_element_type=F32)
            back.append(lax.dot_general(dab[:, cols], wa_ref[h], NT, preferred_element_type=F32)
                        + lax.dot_general(dxb[:, cols], wx_ref[h], NT, preferred_element_type=F32))
        dxc_ref[...] = du * mult * ig + jnp.concatenate(back, axis=1)

    spec3 = pl.BlockSpec((ROWS, SUBLANES, lw), lambda i: (i, 0, 0))
    wspec = pl.BlockSpec(wa.shape, lambda i: (0, 0, 0))
    return pl.pallas_call(
        body, out_shape=[jax.ShapeDtypeStruct((s, d), F32), jax.ShapeDtypeStruct(wa.shape, F32), jax.ShapeDtypeStruct(wa.shape, F32),
                         jax.ShapeDtypeStruct((SUBLANES, d), F32)],
        grid=(s // ROWS,), in_specs=[spec3, spec3, _row_spec(d), wspec, wspec, _vec_spec(d, SUBLANES)],
        out_specs=[_row_spec(d), wspec, wspec, _vec_spec(d, SUBLANES)],
        compiler_params=_params(("arbitrary",)), name="lru_gates_bwd")(g3, da3, xc, wa, wx, vec8)


def _conv_bwd(dxc, rest, cw8):
    s, d = dxc.shape
    hb = ROWS // SUBLANES
    last = s // SUBLANES - 1

    def body(dc_ref, dnext_ref, x_ref, xprev_ref, cw_ref, dx_ref, dcw_ref, padd, padx):
        i = pl.program_id(0)

        @pl.when(i == 0)
        def _():
            dcw_ref[...] = jnp.zeros_like(dcw_ref)

        dc = dc_ref[...]
        padd[0:ROWS, :] = dc
        padd[ROWS:ROWS + SUBLANES, :] = jnp.where(i == pl.num_programs(0) - 1, 0.0, dnext_ref[...])
        _conv_pad_prev(padx, x_ref[...], xprev_ref[...], i == 0)
        dx = jnp.zeros_like(dc)
        for k in range(CONV_TAPS):
            dx = dx + cw_ref[k:k + 1, :] * padd[pl.ds(CONV_TAPS - 1 - k, ROWS), :]
            dcw_ref[k:k + 1, :] += jnp.sum(dc * padx[pl.ds(SUBLANES - (CONV_TAPS - 1) + k, ROWS), :], axis=0, keepdims=True)
        dcw_ref[CONV_TAPS:CONV_TAPS + 1, :] += jnp.sum(dc, axis=0, keepdims=True)
        dx_ref[...] = dx.astype(BF16)

    return pl.pallas_call(
        body, out_shape=[jax.ShapeDtypeStruct((s, d), BF16), jax.ShapeDtypeStruct((SUBLANES, d), F32)], grid=(s // ROWS,),
        in_specs=[_row_spec(d), pl.BlockSpec((SUBLANES, d), lambda i: (jnp.minimum((i + 1) * hb, last), 0)),
                  pl.BlockSpec((None, ROWS, d), lambda i: (0, i, 0)),
                  pl.BlockSpec((None, SUBLANES, d), lambda i: (0, jnp.maximum(i * hb - 1, 0), 0)), _vec_spec(d, SUBLANES)],
        out_specs=[_row_spec(d), _vec_spec(d, SUBLANES)],
        scratch_shapes=[pltpu.VMEM((ROWS + SUBLANES, d), F32), pltpu.VMEM((ROWS + SUBLANES, d), F32)],
        compiler_params=_params(("arbitrary",)), name="lru_conv_bwd")(dxc, dxc, rest, rest, cw8)


def _coords():
    return lax.axis_index("x"), lax.axis_index("y"), lax.axis_index("c")


def _other_chips(x, y):
    return [(1 - x, y), (x, 1 - y), (1 - x, 1 - y)]


def _slab(ref, kind, shard_shape, idx, half=None):
    r, c = shard_shape
    r0, nr = (0, r) if half is None else (half * (r // 2), r // 2)
    if kind == "col":
        return ref.at[pl.ds(r0, nr), pl.ds(pl.multiple_of(idx * c, LANES), c)]
    if kind == "row":
        return ref.at[pl.ds(pl.multiple_of(idx * r, SUBLANES) + r0, nr), :]
    return ref.at[idx, pl.ds(r0, nr), :]


def _full_shape(shard_shape, kind):
    r, c = shard_shape
    return {"col": (r, c * N_DEV), "row": (r * N_DEV, c), "slot": (N_DEV, r, c)}[kind]


def _handshake(peers):
    barrier = pltpu.get_barrier_semaphore()
    for peer in peers:
        pl.semaphore_signal(barrier, inc=1, device_id=peer, device_id_type=MESH)
    pl.semaphore_wait(barrier, len(peers))


def _launch(name, body, out_shape, operands, sems, sequencer_id):
    if sequencer_id is None:
        return pl.pallas_call(body, out_shape=out_shape, in_specs=[HBM] * len(operands), out_specs=[HBM] * len(out_shape),
                              scratch_shapes=sems, name=name)(*operands)
    return pl.kernel(body, out_type=out_shape, mesh=plsc.ScalarSubcoreMesh(axis_name="seq", num_cores=1), name=name,
                     scratch_types=sems, compiler_params=pltpu.CompilerParams(collective_id=sequencer_id))(*operands)


AG_COPIES = 10


def _all_gather(name, shards, kinds, sequencer_id=None):
    n = len(shards)
    shapes = [s.shape for s in shards]

    def body(*refs):
        ins, outs = refs[:n], refs[n:2 * n]
        send_sems, recv_sems, local_sems = refs[2 * n:]
        x, y, c = _coords()
        me, sib, xn, yn, dg = (x, y, c), (x, y, 1 - c), (1 - x, y, c), (x, 1 - y, c), (1 - x, 1 - y, c)
        if sequencer_id is not None:
            _handshake([sib, xn, yn])

        def part(i, dev, half=None):
            return _slab(outs[i], kinds[i], shapes[i], 4 * dev[0] + 2 * dev[1] + dev[2], half)

        def copy(i, k, block, half, to, own=False):
            r = shapes[i][0]
            src = part(i, block, half) if not own else (ins[i] if half is None else ins[i].at[pl.ds(half * (r // 2), r // 2), :])
            return pltpu.make_async_remote_copy(
                src_ref=src, dst_ref=part(i, block, half), send_sem=send_sems.at[AG_COPIES * i + k],
                recv_sem=recv_sems.at[AG_COPIES * i + k], device_id=to, device_id_type=MESH)

        def other_core(dev):
            return (dev[0], dev[1], 1 - c)

        started = []

        def start(cp):
            cp.start()
            started.append(cp)

        for i in range(n):
            start(copy(i, 1, me, 0, xn, own=True))
            start(copy(i, 4, me, 1, yn, own=True))
            start(copy(i, 2, me, 1, xn, own=True))
            start(copy(i, 3, me, 0, yn, own=True))
            start(copy(i, 0, me, None, sib, own=True))
        mine = [pltpu.make_async_copy(ins[i], part(i, me), local_sems.at[i]) for i in range(n)]
        for cp in mine:
            cp.start()
        for i in range(n):
            copy(i, 1, xn, 0, me).wait_recv()
            start(copy(i, 5, xn, 0, yn))
            copy(i, 4, yn, 1, me).wait_recv()
            start(copy(i, 6, yn, 1, xn))
        for i in range(n):
            copy(i, 2, xn, 1, me).wait_recv()
            start(copy(i, 7, xn, None, sib))
            copy(i, 3, yn, 0, me).wait_recv()
            start(copy(i, 8, yn, None, sib))
        for i in range(n):
            copy(i, 5, dg, 0, me).wait_recv()
            copy(i, 6, dg, 1, me).wait_recv()
            start(copy(i, 9, dg, None, sib))
        for i in range(n):
            copy(i, 0, sib, None, me).wait_recv()
            for k, dev in ((7, xn), (8, yn), (9, dg)):
                copy(i, k, other_core(dev), None, me).wait_recv()
        for cp in started:
            cp.wait_send()
        for cp in mine:
            cp.wait()

    out_shape = [jax.ShapeDtypeStruct(_full_shape(s.shape, k), s.dtype) for s, k in zip(shards, kinds)]
    sems = [pltpu.SemaphoreType.DMA((AG_COPIES * n,)), pltpu.SemaphoreType.DMA((AG_COPIES * n,)), pltpu.SemaphoreType.DMA((n,))]
    return _launch(name, body, out_shape, shards, sems, sequencer_id)


def _sibling_copies(kinds, shard_shapes):
    def make(ins, outs, send_sems, recv_sems):
        x, y, c = _coords()
        return [pltpu.make_async_remote_copy(
            src_ref=_slab(ins[i], kinds[i], shard_shapes[i], 2 * q + (1 - c)), dst_ref=outs[i].at[q],
            send_sem=send_sems.at[N_CHIP * i + q], recv_sem=recv_sems.at[N_CHIP * i + q],
            device_id=(x, y, 1 - c), device_id_type=MESH) for i in range(len(ins)) for q in range(N_CHIP)]
    return make


def _sibling_side(partials, kinds, shard_shapes):
    return (partials, [jax.ShapeDtypeStruct((N_CHIP, *s), BF16) for s in shard_shapes], N_CHIP * len(partials),
            _sibling_copies(kinds, shard_shapes))


def _exchange_siblings(name, partials, kinds, shard_shapes, sequencer_id=None):
    n = len(partials)
    make = _sibling_copies(kinds, shard_shapes)

    def body(*refs):
        if sequencer_id is not None:
            x, y, c = _coords()
            _handshake([(x, y, 1 - c)])
        cps = make(refs[:n], refs[n:2 * n], refs[2 * n], refs[2 * n + 1])
        for cp in cps:
            cp.start()
        for cp in cps:
            cp.wait()

    return _launch(name, body, [jax.ShapeDtypeStruct((N_CHIP, *s), BF16) for s in shard_shapes], partials,
                   [pltpu.SemaphoreType.DMA((N_CHIP * n,)), pltpu.SemaphoreType.DMA((N_CHIP * n,))], sequencer_id)


def _exchange_chips(name, chip_sums, sequencer_id=None):
    n = len(chip_sums)

    def body(*refs):
        ins, outs = refs[:n], refs[n:2 * n]
        send_sems, recv_sems = refs[2 * n:]
        x, y, c = _coords()
        if sequencer_id is not None:
            _handshake([(cx, cy, c) for cx, cy in _other_chips(x, y)])
        cps = []
        for i in range(n):
            for k, (cx, cy) in enumerate(_other_chips(x, y)):
                cps.append(pltpu.make_async_remote_copy(
                    src_ref=ins[i].at[2 * cx + cy], dst_ref=outs[i].at[k], send_sem=send_sems.at[3 * i + k],
                    recv_sem=recv_sems.at[3 * i + k], device_id=(cx, cy, c), device_id_type=MESH))
        for cp in cps:
            cp.start()
        for cp in cps:
            cp.wait()

    return _launch(name, body, [jax.ShapeDtypeStruct((3, *t.shape[1:]), BF16) for t in chip_sums], chip_sums,
                   [pltpu.SemaphoreType.DMA((3 * n,)), pltpu.SemaphoreType.DMA((3 * n,))], sequencer_id)


def _all_reduce_small(name, packed):
    rows = packed.shape[0] // N_DEV

    def body(p_ref, out_ref, rb, tot, send_sems, recv_sems):
        x, y, c = _coords()
        me = 4 * x + 2 * y + c

        def peer(k):
            return (x ^ (k >> 2), y ^ ((k >> 1) & 1), c ^ (k & 1))

        def rows_of(idx):
            return pl.ds(pl.multiple_of(idx * rows, SUBLANES), rows)

        def piece(ref, idx):
            return ref.at[rows_of(idx), :]

        scatter = [pltpu.make_async_remote_copy(src_ref=piece(p_ref, me ^ k), dst_ref=rb.at[k], send_sem=send_sems.at[k],
                                                recv_sem=recv_sems.at[k], device_id=peer(k), device_id_type=MESH) for k in range(1, N_DEV)]
        for cp in scatter:
            cp.start()
        acc = p_ref[rows_of(me), :]
        for cp in scatter:
            cp.wait_recv()
        for k in range(1, N_DEV):
            acc = acc + rb[k]
        tot[...] = acc
        out_ref[rows_of(me), :] = acc
        gather = [pltpu.make_async_remote_copy(src_ref=tot, dst_ref=piece(out_ref, me), send_sem=send_sems.at[N_DEV + k],
                                               recv_sem=recv_sems.at[N_DEV + k], device_id=peer(k), device_id_type=MESH)
                  for k in range(1, N_DEV)]
        for cp in gather:
            cp.start()
        for k in range(1, N_DEV):
            pltpu.make_async_remote_copy(src_ref=tot, dst_ref=piece(out_ref, me ^ k), send_sem=send_sems.at[N_DEV + k],
                                         recv_sem=recv_sems.at[N_DEV + k], device_id=peer(k), device_id_type=MESH).wait_recv()
        for cp in scatter + gather:
            cp.wait_send()

    vm = pl.BlockSpec(memory_space=pltpu.VMEM)
    return pl.pallas_call(
        body, out_shape=jax.ShapeDtypeStruct(packed.shape, F32), in_specs=[vm], out_specs=vm,
        scratch_shapes=[pltpu.VMEM((N_DEV, rows, LANES), F32), pltpu.VMEM((rows, LANES), F32),
                        pltpu.SemaphoreType.DMA((2 * N_DEV,)), pltpu.SemaphoreType.DMA((2 * N_DEV,))],
        compiler_params=pltpu.CompilerParams(vmem_limit_bytes=VMEM_LIMIT), name=name)(packed)


def _adamw_math(g, w, m, v):
    m = ADAM_B1 * m + (1.0 - ADAM_B1) * g
    v = ADAM_B2 * v + (1.0 - ADAM_B2) * (g * g)
    delta = -ADAM_LR * ((m / ADAM_C1) / (jnp.sqrt(v / ADAM_C2) + ADAM_EPS) + ADAM_WD * w)
    return delta, m, v


def _slab_spec(kind, shard_shape, tr, slab_of):
    r, c = shard_shape
    if kind == "col":
        return pl.BlockSpec((tr, c), lambda q, i, sc: (i, slab_of(q, sc)))
    return pl.BlockSpec((tr, c), lambda q, i, sc: (slab_of(q, sc) * (r // tr) + i, 0))


def _chip_sum(name, partial, recv, kind, shard_shape, core):
    r, c = shard_shape
    tr = _blk(r, 256)

    def body(core_ref, p_ref, r_ref, o_ref):
        o_ref[...] = (p_ref[...].astype(F32) + r_ref[...].astype(F32)).astype(BF16)

    spec4 = pl.BlockSpec((None, tr, c), lambda q, i, sc: (q, i, 0))
    grid_spec = pltpu.PrefetchScalarGridSpec(
        num_scalar_prefetch=1, grid=(N_CHIP, r // tr),
        in_specs=[_slab_spec(kind, shard_shape, tr, lambda q, sc: 2 * q + sc[0]), spec4], out_specs=spec4)
    return pl.pallas_call(body, out_shape=jax.ShapeDtypeStruct((N_CHIP, r, c), BF16), grid_spec=grid_spec,
                          compiler_params=_params(("parallel", "parallel")), name=name)(core, partial, recv)


def _adamw_shard(name, parts, w, m, v, chip):
    r, c = w.shape
    n_parts = len(parts)
    tr = _blk(r // n_parts, 128)
    per = r // n_parts // tr

    def body(chip_ref, *refs):
        src, (w_ref, m_ref, v_ref), (g_out, d_out, m_out, v_out) = refs[:2 * n_parts], refs[2 * n_parts:2 * n_parts + 3], refs[2 * n_parts + 3:]
        for p in range(n_parts):
            @pl.when(pl.program_id(0) // per == p)
            def _():
                g = src[2 * p][...].astype(F32)
                for k in range(3):
                    g = g + src[2 * p + 1][k].astype(F32)
                g_out[...] = g
                d_out[...], m_out[...], v_out[...] = _adamw_math(g, w_ref[...], m_ref[...], v_ref[...])

    def part_specs(p):
        at = lambda i: jnp.clip(i - p * per, 0, per - 1)
        return [pl.BlockSpec((None, tr, c), lambda i, sc: (sc[0], at(i), 0)), pl.BlockSpec((3, tr, c), lambda i, sc: (0, at(i), 0))]

    blk = pl.BlockSpec((tr, c), lambda i, sc: (i, 0))
    grid_spec = pltpu.PrefetchScalarGridSpec(
        num_scalar_prefetch=1, grid=(r // tr,), in_specs=[s for p in range(n_parts) for s in part_specs(p)] + [blk, blk, blk], out_specs=[blk] * 4)
    return pl.pallas_call(body, out_shape=[jax.ShapeDtypeStruct((r, c), F32)] * 4, grid_spec=grid_spec,
                          compiler_params=_params(("parallel",)), name=name)(chip, *[a for p in parts for a in p], w, m, v)


def _adamw_small(name, g, w, m, v):
    def body(g_ref, w_ref, m_ref, v_ref, d_out, m_out, v_out):
        d_out[...], m_out[...], v_out[...] = _adamw_math(g_ref[...], w_ref[...], m_ref[...], v_ref[...])

    vm = pl.BlockSpec(memory_space=pltpu.VMEM)
    return pl.pallas_call(body, out_shape=[jax.ShapeDtypeStruct(g.shape, F32)] * 3, in_specs=[vm] * 4, out_specs=[vm] * 3,
                          compiler_params=pltpu.CompilerParams(vmem_limit_bytes=VMEM_LIMIT), name=name)(g, w, m, v)


def _pack_rows(arrays, total_rows):
    flat = [a.reshape(-1, LANES) for a in arrays]
    used = sum(f.shape[0] for f in flat)
    return jnp.concatenate(flat + [jnp.zeros((total_rows - used, LANES), F32)], axis=0)


def _unpack_rows(packed, like):
    out, at = [], 0
    for a in like:
        n = a.size // LANES
        out.append(packed[at:at + n].reshape(a.shape))
        at += n
    return out


def kernel(x, norm_mix_g, w_in, conv_w, conv_b, lru_wa, lru_ba, lru_wx, lru_bx, lru_lambda, w_proj_attn, w_proj_lru, w_out, norm_mlp_g, w_up, w_down, norm_final_g, loss_target, m_norm_mix_g, m_w_in, m_conv_w, m_conv_b, m_lru_wa, m_lru_ba, m_lru_wx, m_lru_bx, m_lru_lambda, m_w_proj_attn, m_w_proj_lru, m_w_out, m_norm_mlp_g, m_w_up, m_w_down, m_norm_final_g, v_norm_mix_g, v_w_in, v_conv_w, v_conv_b, v_lru_wa, v_lru_ba, v_lru_wx, v_lru_bx, v_lru_lambda, v_w_proj_attn, v_w_proj_lru, v_w_out, v_norm_mlp_g, v_w_up, v_w_down, v_norm_final_g):
    xs, tgt = x[0], loss_target[0]
    s, d = xs.shape
    nh = d // HEAD
    ix, iy, ic = _coords()
    core = jnp.reshape(ic, (1,)).astype(jnp.int32)
    chip = jnp.reshape(2 * ix + iy, (1,)).astype(jnp.int32)
    dev = 4 * ix + 2 * iy + ic

    big = [w_in[0], w_proj_attn[0], w_proj_lru[0], w_out[0], w_up[0], w_down[0]]
    big_m = [m_w_in[0], m_w_proj_attn[0], m_w_proj_lru[0], m_w_out[0], m_w_up[0], m_w_down[0]]
    big_v = [v_w_in[0], v_w_proj_attn[0], v_w_proj_lru[0], v_w_out[0], v_w_up[0], v_w_down[0]]
    kinds = ["col", "row", "row", "row", "col", "row"]
    pad_taps = lambda t: jnp.pad(t, ((0, SUBLANES - CONV_TAPS), (0, 0)))
    pad_taps2 = lambda t: jnp.pad(t, ((0, 2 * SUBLANES - CONV_TAPS), (0, 0)))
    shards = [w.astype(BF16) for w in big]
    win, cw_slots = _all_gather("all_gather_w_in", [shards[0], pad_taps2(conv_w[0])], ["col", "slot"])
    later = lax.optimization_barrier((shards[1:], win))[0]
    wpa, wpl, wout = _all_gather("all_gather_mix", later[:3], kinds[1:4], sequencer_id=1)
    wup, wdown = _all_gather("all_gather_mlp", later[3:], kinds[4:], sequencer_id=5)
    cw8 = jnp.transpose(cw_slots[:, :SUBLANES], (1, 0, 2)).reshape(SUBLANES, d)
    row_id = lax.broadcasted_iota(jnp.int32, (SUBLANES, d), 0)
    vec8 = sum(jnp.where(row_id == k, t, 0.0) for k, t in ((VEC_CB, conv_b), (VEC_BA, lru_ba), (VEC_BX, lru_bx), (VEC_LAM, lru_lambda)))
    wa16, wx16 = lru_wa[0].astype(BF16), lru_wx[0].astype(BF16)
    slopes = 2.0 ** (-8.0 * jnp.arange(1, nh + 1, dtype=F32) / nh)

    def seg_specs(*segs):
        return lambda bm, bn: [pl.BlockSpec((None, bm, bn), (lambda i, j, kk, sg=sg: (sg, i, j))) for sg in segs]

    def plain_specs(k):
        return lambda bm, bn: [pl.BlockSpec((bm, bn), lambda i, j, kk: (i, j)) for _ in range(k)]

    xn = _rms_fwd("norm_mix", xs, norm_mix_g)
    qkv = _mm_fwd("proj_qkv", xn, win, 0, 3 * d, [F32], seg_out=d)[0]
    rest = _mm_fwd("proj_rest", xn, win, 3 * d, 4 * d, [F32], seg_out=d)[0]
    att, lse = _attn_fwd(qkv, slopes)
    a3, u3, xc = _gates_fwd(rest, cw8, vec8, wa16, wx16)
    h3, hp3 = _scan_fwd(a3, u3)
    ylru, h2d = _lru_out(h3, rest)
    pa = _mm_fwd("proj_attn", att, wpa, 0, d, [F32])[0]

    def merge(acc, pa_b, ga, gl):
        return acc, _sigmoid(ga) * pa_b + _sigmoid(gl) * acc

    plr, merged = _mm_fwd("proj_lru_merge", ylru, wpl, 0, d, [F32, BF16], merge, (pa, rest, rest),
                          lambda bm, bn: plain_specs(1)(bm, bn) + seg_specs(2, 3)(bm, bn), bm=512)
    h1 = _mm_fwd("mix_out", merged, wout, 0, d, [F32], lambda acc, r: (acc + r,), (xs,), plain_specs(1))[0]
    hn = _rms_fwd("norm_mlp", h1, norm_mlp_g)

    def relu2(acc):
        return acc, jnp.square(jnp.maximum(acc, 0.0))

    up, hid = _mm_fwd("mlp_up", hn, wup, 0, wup.shape[1], [BF16, BF16], relu2)
    h2 = _mm_fwd("mlp_down", hid, wdown, 0, d, [F32], lambda acc, r: (acc + r,), (h1,), plain_specs(1))[0]
    dh2, dh2b, dg3, loss_lanes = _final_loss(h2, tgt, norm_final_g.reshape(1, d))
    loss = lax.psum(0.5 / d * jnp.sum(loss_lanes), ("x", "y", "c"))
    dh2b = lax.optimization_barrier((dh2b, loss))[0]

    def reduce_group(tag, kk, shp, partials, from_sibling, sequencer_id):
        sums = [_chip_sum(f"chip_sum_{tag}_{i}", p, f, k, sh, core) for i, (p, f, k, sh) in enumerate(zip(partials, from_sibling, kk, shp))]
        return list(zip(sums, _exchange_chips(f"rs_chips_{tag}", sums, sequencer_id)))

    dup = _mm_nt("mlp_down_dx", dh2b, wdown, [BF16], lambda acc, u: (acc * (2.0 * jnp.maximum(u.astype(F32), 0.0)),), (up,), plain_specs(1))[0]
    g_wdown = _mm_tn("mlp_down_dw", hid, dh2b)
    g_wup = _mm_tn("mlp_up_dw", hn, dup)
    shp_mlp = [w.shape for w in big[4:]]
    (dhn,), sib_mlp = _mm_nt("mlp_up_dx", dup, wup, [F32], side=_sibling_side([g_wup, g_wdown], kinds[4:], shp_mlp))
    red_up, red_down = reduce_group("mlp", kinds[4:], shp_mlp, [g_wup, g_wdown], sib_mlp, 2)
    dhn = lax.optimization_barrier((dhn, red_up[0], red_down[0]))[0]
    dh1, dh1b, dg2 = _rms_bwd("norm_mlp_bwd", h1, norm_mlp_g, dhn, dh2)

    def merge_bwd(acc, pa_b, pl_b, ga, gl):
        sa, sl = _sigmoid(ga), _sigmoid(gl)
        return acc * sa, acc * sl, acc * pa_b * sa * (1.0 - sa), acc * pl_b * sl * (1.0 - sl)

    dpa, dpl, dga, dgl = _mm_nt("mix_out_dx", dh1b, wout, [BF16] * 4, merge_bwd, (pa, plr, rest, rest),
                                lambda bm, bn: plain_specs(2)(bm, bn) + seg_specs(2, 3)(bm, bn), bm=512)
    g_wout = _mm_tn("mix_out_dw", merged, dh1b)
    datt = _mm_nt("proj_attn_dx", dpa, wpa, [F32])[0]
    g_wpa = _mm_tn("proj_attn_dw", att, dpa)

    def lru_out_bwd(acc, h_b, gate):
        return acc * _gelu(gate), acc * h_b * _gelu_grad(gate)

    g_wpl = _mm_tn("proj_lru_dw", ylru, dpl)
    shp_mix = [w.shape for w in big[1:4]]
    (dh, dxg), sib_mix = _mm_nt("proj_lru_dx", dpl, wpl, [F32, BF16], lru_out_bwd, (h2d, rest),
                                lambda bm, bn: plain_specs(1)(bm, bn) + seg_specs(1)(bm, bn), bm=512,
                                side=_sibling_side([g_wpa, g_wpl, g_wout], kinds[1:4], shp_mix))
    red_pa, red_pl, red_out = reduce_group("mix", kinds[1:4], shp_mix, [g_wpa, g_wpl, g_wout], sib_mix, 3)
    dq, dk, dv = _attn_bwd(qkv, datt, att, lse, slopes)
    g3, da3 = _scan_bwd(a3, hp3, dh)
    dxc, dwa, dwx, dvec = _gates_bwd(g3, da3, xc, wa16, wx16, vec8)
    dxr, dconv = _conv_bwd(dxc, rest, cw8)

    def small_step(tag, grads, ws, ms, vs, like, after):
        n_rows = sum(g.size for g in grads) // LANES
        per_dev = -(-n_rows // (N_DEV * SUBLANES)) * SUBLANES
        packed = lax.optimization_barrier((_pack_rows(grads, N_DEV * per_dev), after))[0]
        total = _all_reduce_small(f"all_reduce_{tag}", packed)
        w_rows = -(-(sum(w.size for w in ws) // LANES) // SUBLANES) * SUBLANES
        upd = _adamw_small(f"adamw_{tag}", total[:w_rows], _pack_rows(ws, w_rows), _pack_rows(ms, w_rows), _pack_rows(vs, w_rows))
        return _unpack_rows(total, like), [_unpack_rows(t, ws) for t in upd]

    early_w = [conv_b, lru_wa, lru_ba, lru_wx, lru_bx, lru_lambda, norm_mlp_g, norm_final_g]
    early_m = [m_conv_b, m_lru_wa, m_lru_ba, m_lru_wx, m_lru_bx, m_lru_lambda, m_norm_mlp_g, m_norm_final_g]
    early_v = [v_conv_b, v_lru_wa, v_lru_ba, v_lru_wx, v_lru_bx, v_lru_lambda, v_norm_mlp_g, v_norm_final_g]
    early_g = [dconv[CONV_TAPS:CONV_TAPS + 1], dwa, dvec[VEC_BA:VEC_BA + 1], dwx, dvec[VEC_BX:VEC_BX + 1],
               dvec[VEC_LAM:VEC_LAM + 1], dg2, dg3, dconv[0:CONV_TAPS]]
    dq = lax.optimization_barrier((dq, red_up[1], red_down[1]))[0]
    early_sum, early_upd = small_step("small", early_g, early_w, early_m, early_v,
                                      early_w + [jax.ShapeDtypeStruct((1, CONV_TAPS, d), F32)], dq)
    g_cw_full = early_sum[-1]
    cshard = conv_w.shape[2]
    g_cw = lax.dynamic_slice(g_cw_full, (0, 0, dev * cshard), (1, CONV_TAPS, cshard))
    cw_delta, cw_m, cw_v = (t[:CONV_TAPS][None] for t in _adamw_small(
        "adamw_conv_w", pad_taps(g_cw[0]), pad_taps(conv_w[0]), pad_taps(m_conv_w[0]), pad_taps(v_conv_w[0])))
    dq = lax.optimization_barrier((dq, early_sum))[0]

    dproj = jnp.concatenate([dq, dk, dv, dxr, dxg, dga, dgl], axis=1)
    dproj = lax.optimization_barrier((dproj, red_pa[1], red_pl[1], red_out[1]))[0]
    half = (big[0].shape[0] // 2, big[0].shape[1])
    g_in0 = _mm_tn("proj_in_dw_0", xn, dproj, part=(0, 2))
    g_in1, sib_in0 = _mm_tn("proj_in_dw_1", xn, dproj, part=(1, 2), side=_sibling_side([g_in0], ["col"], [half]))
    red_in = reduce_group("in_0", ["col"], [half], [g_in0], sib_in0, 4)
    dproj = lax.optimization_barrier((dproj, red_in[0][0]))[0]
    (dxn0,), sib_in1 = _mm_nt("proj_in_dx_0", dproj, win, [F32], part=(0, 2), side=_sibling_side([g_in1], ["col"], [half]))
    red_in += reduce_group("in_1", ["col"], [half], [g_in1], sib_in1, 6)
    dproj = lax.optimization_barrier((dproj, red_in[1][0]))[0]
    dxn1 = _mm_nt("proj_in_dx_1", dproj, win, [F32], part=(1, 2))[0]
    dxn = jnp.concatenate([dxn0, dxn1], axis=0)
    dxn = lax.optimization_barrier((dxn, red_in[0][1]))[0]
    grad_x, _, dg1 = _rms_bwd("norm_mix_bwd", xs, norm_mix_g, dxn, dh1)
    red_up, red_down = lax.optimization_barrier(((red_up, red_down), dg1))[0]
    big_out = {i: _adamw_shard(f"adamw_{i}", [red], big[i], big_m[i], big_v[i], chip) for i, red in ((4, red_up), (5, red_down))}
    big_out.update({i: _adamw_shard(f"adamw_{i}", [red], big[i], big_m[i], big_v[i], chip) for i, red in ((1, red_pa), (2, red_pl), (3, red_out))})
    late_sum, late_upd = small_step("norm_mix", [dg1], [norm_mix_g], [m_norm_mix_g], [v_norm_mix_g], [norm_mix_g], big_out[3])
    big_out[0] = _adamw_shard("adamw_0", red_in, big[0], big_m[0], big_v[0], chip)
    s_grad = late_sum + early_sum[:-1]
    s_delta, s_m, s_v = (late_upd[j] + early_upd[j] for j in range(3))


    names = ["norm_mix_g", "w_in", "conv_w", "conv_b", "lru_wa", "lru_ba", "lru_wx", "lru_bx", "lru_lambda", "w_proj_attn", "w_proj_lru",
             "w_out", "norm_mlp_g", "w_up", "w_down", "norm_final_g"]
    small_names = ["norm_mix_g", "conv_b", "lru_wa", "lru_ba", "lru_wx", "lru_bx", "lru_lambda", "norm_mlp_g", "norm_final_g"]
    big_names = ["w_in", "w_proj_attn", "w_proj_lru", "w_out", "w_up", "w_down"]
    res = {"conv_w": (g_cw, cw_delta, cw_m, cw_v)}
    for i, nm in enumerate(small_names):
        res[nm] = (s_grad[i], s_delta[i], s_m[i], s_v[i])
    for i, nm in enumerate(big_names):
        res[nm] = tuple(t[None] for t in big_out[i])
    return (loss, grad_x[None], *[res[nm][0] for nm in names], *[res[nm][1] for nm in names],
            *[res[nm][2] for nm in names], *[res[nm][3] for nm in names])
```

```python
import jax
import jax.numpy as jnp
from jax import lax
from jax.experimental import pallas as pl
from jax.experimental.pallas import tpu as pltpu
from jax.experimental.pallas import tpu_sc as plsc

F32, BF16 = jnp.float32, jnp.bfloat16
MESH = pl.DeviceIdType.MESH
HBM = pl.BlockSpec(memory_space=pltpu.HBM)
N_DEV = 8
N_CHIP = 4
HEAD = 128
SPAN = 128
DILATIONS = (1, 4, 16)
CONV_TAPS = 4
LRU_C = 8.0
NORM_EPS = 1e-6
LANES = 128
SUBLANES = 8
VMEM_LIMIT = 56 * 1024 * 1024
ADAM_LR, ADAM_B1, ADAM_B2, ADAM_EPS, ADAM_WD, ADAM_STEP = 0.001, 0.9, 0.999, 1e-08, 0.01, 10
ADAM_C1 = 1.0 - ADAM_B1 ** ADAM_STEP
ADAM_C2 = 1.0 - ADAM_B2 ** ADAM_STEP
NEG = -1e30


def _params(sem=None):
    return pltpu.CompilerParams(dimension_semantics=sem, vmem_limit_bytes=VMEM_LIMIT)


def _sigmoid(v):
    return 1.0 / (1.0 + jnp.exp(-v))


def _gelu(v):
    k = 0.7978845608028654
    return 0.5 * v * (1.0 + jnp.tanh(k * (v + 0.044715 * v * v * v)))


def _gelu_grad(v):
    k = 0.7978845608028654
    t = jnp.tanh(k * (v + 0.044715 * v * v * v))
    return 0.5 * (1.0 + t) + 0.5 * v * (1.0 - t * t) * k * (1.0 + 3.0 * 0.044715 * v * v)


NN = (((1,), (0,)), ((), ()))
NT = (((1,), (1,)), ((), ()))
TN = (((0,), (0,)), ((), ()))


def _mm(name, a, a_spec, b, b_spec, dn, grid, out_shapes, out_specs, acc_block, epilogue=None, extras=(), extra_specs=(), side=None):
    nk, ne, no = grid[2], len(extras), len(out_shapes)
    side_ops, side_shapes, side_copies, make_copies = side if side is not None else ((), (), 0, None)
    ns_in, ns_out = len(side_ops), len(side_shapes)

    def body(*refs):
        a_ref, b_ref = refs[0], refs[1]
        ex, side_in = refs[2:2 + ne], refs[2 + ne:2 + ne + ns_in]
        outs = refs[2 + ne + ns_in:2 + ne + ns_in + no]
        side_out = refs[2 + ne + ns_in + no:2 + ne + ns_in + no + ns_out]
        scratch = refs[2 + ne + ns_in + no + ns_out:]
        at = [pl.program_id(ax) for ax in range(3)]
        if side is not None:
            @pl.when((at[0] == 0) & (at[1] == 0) & (at[2] == 0))
            def _():
                for cp in make_copies(side_in, side_out, scratch[-2], scratch[-1]):
                    cp.start()

        part = lax.dot_general(a_ref[...], b_ref[...], dn, preferred_element_type=F32)

        def finish(acc):
            vals = epilogue(acc, *[e[...] for e in ex]) if epilogue is not None else (acc,)
            for o, v in zip(outs, vals):
                o[...] = v.astype(o.dtype)

        if nk == 1:
            finish(part)
        else:
            acc_ref, k = scratch[0], at[2]

            @pl.when(k == 0)
            def _():
                acc_ref[...] = part

            @pl.when(k > 0)
            def _():
                acc_ref[...] += part

            @pl.when(k == nk - 1)
            def _():
                finish(acc_ref[...])

        if side is not None:
            @pl.when((at[0] == grid[0] - 1) & (at[1] == grid[1] - 1) & (at[2] == grid[2] - 1))
            def _():
                for cp in make_copies(side_in, side_out, scratch[-2], scratch[-1]):
                    cp.wait()

    scratch_shapes = [pltpu.VMEM(acc_block, F32)] if nk > 1 else []
    if side is not None:
        scratch_shapes += [pltpu.SemaphoreType.DMA((side_copies,)), pltpu.SemaphoreType.DMA((side_copies,))]
    res = pl.pallas_call(
        body, out_shape=[*out_shapes, *side_shapes], grid=grid, in_specs=[a_spec, b_spec, *extra_specs, *[HBM] * ns_in],
        out_specs=[*out_specs, *[HBM] * ns_out], scratch_shapes=scratch_shapes,
        compiler_params=_params(("arbitrary",) * 3 if side is not None else ("parallel", "parallel", "arbitrary")),
        name=name)(a, b, *extras, *side_ops)
    return res if side is None else (res[:no], res[no:])


def _blk(n, pref):
    return pref if n % pref == 0 else n


def _kblk(k):
    return k if k <= 2048 else next(b for b in (2048, 1024, 512) if k % b == 0)


def _mm_fwd(name, a, w, col0, ncols, out_dtypes, epilogue=None, extras=(), extra_specs_fn=None, seg_out=None, bm=1024, bn=1024):
    m, k = a.shape
    bm, bn = _blk(m, bm), _blk(ncols, bn)
    bk = _kblk(k)
    nk = k // bk
    cb0 = col0 // bn
    grid = (m // bm, ncols // bn, nk)
    a_spec = pl.BlockSpec((bm, bk), lambda i, j, kk: (i, kk))
    b_spec = pl.BlockSpec((bk, bn), lambda i, j, kk: (kk, cb0 + j))
    if seg_out is None:
        shapes = [jax.ShapeDtypeStruct((m, ncols), dt) for dt in out_dtypes]
        specs = [pl.BlockSpec((bm, bn), lambda i, j, kk: (i, j)) for _ in out_dtypes]
    else:
        per = seg_out // bn
        shapes = [jax.ShapeDtypeStruct((ncols // seg_out, m, seg_out), dt) for dt in out_dtypes]
        specs = [pl.BlockSpec((None, bm, bn), lambda i, j, kk: (j // per, i, j % per)) for _ in out_dtypes]
    ex_specs = extra_specs_fn(bm, bn) if extra_specs_fn else ()
    return _mm(name, a, a_spec, w, b_spec, NN, grid, shapes, specs, (bm, bn), epilogue, extras, ex_specs)


def _mm_nt(name, a, w, out_dtypes, epilogue=None, extras=(), extra_specs_fn=None, part=(0, 1), side=None, bm=1024, bn=1024):
    m, k = a.shape
    n = w.shape[0]
    m = m // part[1]
    bm, bn = _blk(m, bm), _blk(n, bn)
    bk = _kblk(k)
    grid = (m // bm, n // bn, k // bk)
    i0 = part[0] * (m // bm)
    a_spec = pl.BlockSpec((bm, bk), lambda i, j, kk: (i0 + i, kk))
    b_spec = pl.BlockSpec((bn, bk), lambda i, j, kk: (j, kk))
    shapes = [jax.ShapeDtypeStruct((m, n), dt) for dt in out_dtypes]
    specs = [pl.BlockSpec((bm, bn), lambda i, j, kk: (i, j)) for _ in out_dtypes]
    ex_specs = extra_specs_fn(bm, bn) if extra_specs_fn else ()
    return _mm(name, a, a_spec, w, b_spec, NT, grid, shapes, specs, (bm, bn), epilogue, extras, ex_specs, side)


def _mm_tn(name, a, b, part=(0, 1), side=None, bm=1024, bn=1024):
    t, m = a.shape
    n = b.shape[1]
    m = m // part[1]
    bm, bn = _blk(m, bm), _blk(n, bn)
    grid = (m // bm, n // bn, 1)
    i0 = part[0] * (m // bm)
    a_spec = pl.BlockSpec((t, bm), lambda i, j, kk: (0, i0 + i))
    b_spec = pl.BlockSpec((t, bn), lambda i, j, kk: (0, j))
    res = _mm(name, a, a_spec, b, b_spec, TN, grid, [jax.ShapeDtypeStruct((m, n), BF16)],
              [pl.BlockSpec((bm, bn), lambda i, j, kk: (i, j))], (bm, bn), side=side)
    return res[0] if side is None else (res[0][0], res[1])


ROWS = 256


def _row_spec(d):
    return pl.BlockSpec((ROWS, d), lambda i: (i, 0))


def _vec_spec(d, rows=1):
    return pl.BlockSpec((rows, d), lambda i: (0, 0))


def _rms_fwd(name, x, g):
    s, d = x.shape

    def body(x_ref, g_ref, o_ref):
        xv = x_ref[...]
        r = lax.rsqrt(jnp.mean(xv * xv, axis=-1, keepdims=True) + NORM_EPS)
        o_ref[...] = (xv * r * g_ref[...]).astype(BF16)

    return pl.pallas_call(body, out_shape=jax.ShapeDtypeStruct((s, d), BF16), grid=(s // ROWS,),
                          in_specs=[_row_spec(d), _vec_spec(d)], out_specs=_row_spec(d),
                          compiler_params=_params(("parallel",)), name=name)(x, g)


def _rms_bwd_math(xv, g, dy):
    r = lax.rsqrt(jnp.mean(xv * xv, axis=-1, keepdims=True) + NORM_EPS)
    n = xv * r
    z = dy * g
    dx = r * (z - n * jnp.mean(z * n, axis=-1, keepdims=True))
    return dx, jnp.sum(dy * n, axis=0, keepdims=True)


def _rms_bwd(name, x, g, dy, resid):
    s, d = x.shape

    def body(x_ref, g_ref, dy_ref, r_ref, dx_ref, dxb_ref, dg_ref):
        dx, dg = _rms_bwd_math(x_ref[...], g_ref[...], dy_ref[...])
        dx = dx + r_ref[...]
        dx_ref[...] = dx
        dxb_ref[...] = dx.astype(BF16)

        @pl.when(pl.program_id(0) == 0)
        def _():
            dg_ref[...] = jnp.zeros_like(dg_ref)

        dg_ref[...] += dg

    return pl.pallas_call(
        body, out_shape=[jax.ShapeDtypeStruct((s, d), F32), jax.ShapeDtypeStruct((s, d), BF16), jax.ShapeDtypeStruct((1, d), F32)],
        grid=(s // ROWS,), in_specs=[_row_spec(d), _vec_spec(d), _row_spec(d), _row_spec(d)],
        out_specs=[_row_spec(d), _row_spec(d), _vec_spec(d)], compiler_params=_params(("arbitrary",)), name=name)(x, g, dy, resid)


def _final_loss(h2, tgt, g):
    s, d = h2.shape

    def body(x_ref, t_ref, g_ref, dx_ref, dxb_ref, dg_ref, ls_ref):
        xv, gv = x_ref[...], g_ref[...]
        r = lax.rsqrt(jnp.mean(xv * xv, axis=-1, keepdims=True) + NORM_EPS)
        diff = xv * r * gv - t_ref[...]
        dx, dg = _rms_bwd_math(xv, gv, diff * (1.0 / d))
        dx_ref[...] = dx
        dxb_ref[...] = dx.astype(BF16)

        @pl.when(pl.program_id(0) == 0)
        def _():
            dg_ref[...] = jnp.zeros_like(dg_ref)
            ls_ref[...] = jnp.zeros_like(ls_ref)

        dg_ref[...] += dg
        ls_ref[...] += jnp.sum(diff * diff, axis=0, keepdims=True)

    return pl.pallas_call(
        body, out_shape=[jax.ShapeDtypeStruct((s, d), F32), jax.ShapeDtypeStruct((s, d), BF16),
                         jax.ShapeDtypeStruct((1, d), F32), jax.ShapeDtypeStruct((1, d), F32)],
        grid=(s // ROWS,), in_specs=[_row_spec(d), _row_spec(d), _vec_spec(d)],
        out_specs=[_row_spec(d), _row_spec(d), _vec_spec(d), _vec_spec(d)],
        compiler_params=_params(("arbitrary",)), name="final_norm_loss")(h2, tgt, g)


ATTN_Q = 128


ATTN_BATCH = 8


def _attn_units(s):
    units = []
    for gi, d in enumerate(DILATIONS):
        for r in range(d):
            for q0 in range(0, s // d, ATTN_Q):
                k0 = max(q0 - SPAN, 0)
                units.append((gi, d, r, q0, k0, q0 + ATTN_Q - k0))
    return units


def _stream_rows(d, r, start, size):
    return pl.ds(r + start * d, size) if d == 1 else pl.ds(r + start * d, size, stride=d)


def _attn_scores(q_ref, k_ref, slope, d, r, q0, k0, nk):
    qrows, krows = _stream_rows(d, r, q0, ATTN_Q), _stream_rows(d, r, k0, nk)
    qb = q_ref[qrows, :].astype(BF16)
    kb = k_ref[krows, :].astype(BF16)
    sc = lax.dot_general(qb, kb, NT, preferred_element_type=F32) * (HEAD ** -0.5)
    qi = lax.broadcasted_iota(jnp.int32, (ATTN_Q, nk), 0)
    kj = lax.broadcasted_iota(jnp.int32, (ATTN_Q, nk), 1)
    dist = (q0 - k0) + qi - kj
    valid = (dist >= 0) & (dist <= SPAN)
    sc = sc - (slope * d) * dist.astype(F32)
    return jnp.where(valid, sc, NEG), valid, qb, kb, qrows, krows


def _attn_fwd(proj, dm, slopes):
    s = proj.shape[0]
    units = _attn_units(s)

    def body(sl_ref, q_ref, k_ref, v_ref, att_ref, lse_ref, *scr):
        o_scr, l_scr = scr[:3], scr[3:]
        slope = sl_ref[pl.program_id(0)]
        for first in range(0, len(units), ATTN_BATCH):
            batch = units[first:first + ATTN_BATCH]
            scored = [_attn_scores(q_ref, k_ref, slope, d, r, q0, k0, nk) for _, d, r, q0, k0, nk in batch]
            soft = []
            for sc, _, _, _, _, _ in scored:
                m = jnp.max(sc, axis=-1, keepdims=True)
                p = jnp.exp(sc - m)
                soft.append((m, p, jnp.sum(p, axis=-1, keepdims=True)))
            outs = [lax.dot_general(p.astype(BF16), v_ref[sco[5], :].astype(BF16), NN, preferred_element_type=F32)
                    for (m, p, l), sco in zip(soft, scored)]
            for (gi, *_), (m, p, l), sco, o in zip(batch, soft, scored, outs):
                o_scr[gi][sco[4], :] = o / l
                l_scr[gi][sco[4], :] = jnp.broadcast_to(m + jnp.log(l), (ATTN_Q, HEAD))
        l0, l1, l2 = l_scr[0][...], l_scr[1][...], l_scr[2][...]
        m = jnp.maximum(jnp.maximum(l0, l1), l2)
        w0, w1, w2 = jnp.exp(l0 - m), jnp.exp(l1 - m), jnp.exp(l2 - m)
        tot = w0 + w1 + w2
        att_ref[...] = ((w0 * o_scr[0][...] + w1 * o_scr[1][...] + w2 * o_scr[2][...]) / tot).astype(BF16)
        lse_ref[...] = m + jnp.log(tot)

    def seg(i):
        return pl.BlockSpec((s, HEAD), lambda h: (0, i * (dm // HEAD) + h))

    col = pl.BlockSpec((s, HEAD), lambda h: (0, h))
    return pl.pallas_call(
        body, out_shape=[jax.ShapeDtypeStruct((s, dm), BF16), jax.ShapeDtypeStruct((s, dm), F32)], grid=(dm // HEAD,),
        in_specs=[pl.BlockSpec(memory_space=pltpu.SMEM), seg(0), seg(1), seg(2)], out_specs=[col, col],
        scratch_shapes=[pltpu.VMEM((s, HEAD), F32)] * (2 * len(DILATIONS)),
        compiler_params=_params(("parallel",)), name="attn_fwd")(slopes, proj, proj, proj)


def _attn_bwd(proj, dm, datt, att, lse, slopes):
    s = proj.shape[0]
    units = _attn_units(s)

    def body(sl_ref, q_ref, k_ref, v_ref, do_ref, att_ref, lse_ref, dq_ref, dk_ref, dv_ref, dq_scr, dk_scr, dv_scr, dl_scr):
        slope = sl_ref[pl.program_id(0)]
        delta = jnp.sum(do_ref[...] * att_ref[...].astype(F32), axis=-1, keepdims=True)
        dl_scr[...] = jnp.broadcast_to(delta, (s, HEAD))
        dq_scr[...] = jnp.zeros_like(dq_scr)
        dk_scr[...] = jnp.zeros_like(dk_scr)
        dv_scr[...] = jnp.zeros_like(dv_scr)
        for first in range(0, len(units), ATTN_BATCH):
            scored = [_attn_scores(q_ref, k_ref, slope, d, r, q0, k0, nk) for _, d, r, q0, k0, nk in units[first:first + ATTN_BATCH]]
            dobs = [do_ref[sco[4], :].astype(BF16) for sco in scored]
            dps = [lax.dot_general(dob, v_ref[sco[5], :].astype(BF16), NT, preferred_element_type=F32) for dob, sco in zip(dobs, scored)]
            ps = [jnp.where(sco[1], jnp.exp(sco[0] - lse_ref[sco[4], :][:, 0:1]), 0.0) for sco in scored]
            dss = [(p * (dp - dl_scr[sco[4], :][:, 0:1]) * (HEAD ** -0.5)).astype(BF16) for p, dp, sco in zip(ps, dps, scored)]
            dqs = [lax.dot_general(ds, sco[3], NN, preferred_element_type=F32) for ds, sco in zip(dss, scored)]
            dks = [lax.dot_general(ds, sco[2], TN, preferred_element_type=F32) for ds, sco in zip(dss, scored)]
            dvs = [lax.dot_general(p.astype(BF16), dob, TN, preferred_element_type=F32) for p, dob in zip(ps, dobs)]
            for sco, dq, dk, dv in zip(scored, dqs, dks, dvs):
                dq_scr[sco[4], :] += dq
                dk_scr[sco[5], :] += dk
                dv_scr[sco[5], :] += dv
        dq_ref[...] = dq_scr[...].astype(BF16)
        dk_ref[...] = dk_scr[...].astype(BF16)
        dv_ref[...] = dv_scr[...].astype(BF16)

    def seg(i):
        return pl.BlockSpec((s, HEAD), lambda h: (0, i * (dm // HEAD) + h))

    col = pl.BlockSpec((s, HEAD), lambda h: (0, h))
    return pl.pallas_call(
        body, out_shape=[jax.ShapeDtypeStruct((s, dm), BF16)] * 3, grid=(dm // HEAD,),
        in_specs=[pl.BlockSpec(memory_space=pltpu.SMEM), seg(0), seg(1), seg(2), col, col, col], out_specs=[col, col, col],
        scratch_shapes=[pltpu.VMEM((s, HEAD), F32)] * 4,
        compiler_params=_params(("parallel",)), name="attn_bwd")(slopes, proj, proj, proj, datt, att, lse)


VEC_CB, VEC_BA, VEC_BX, VEC_LAM = 0, 1, 2, 3
SEG_Q, SEG_K, SEG_V, SEG_X, SEG_GATE, SEG_GA, SEG_GL = range(7)


def _to_3d(ref3, val):
    lw = val.shape[1] // SUBLANES
    for j in range(SUBLANES):
        ref3[:, j, :] = val[:, j * lw:(j + 1) * lw]


def _from_3d(ref3):
    return jnp.concatenate([ref3[:, j, :] for j in range(SUBLANES)], axis=1)


def _softplus(z):
    return jnp.maximum(z, 0.0) + jnp.log1p(jnp.exp(-jnp.abs(z)))


def _gate_math(xc, wa_ref, wx_ref, vec):
    xcb = xc.astype(BF16)
    nh = xc.shape[1] // HEAD
    pre_a = jnp.concatenate([jnp.dot(xcb[:, h * HEAD:(h + 1) * HEAD], wa_ref[h], preferred_element_type=F32) for h in range(nh)], axis=1)
    pre_x = jnp.concatenate([jnp.dot(xcb[:, h * HEAD:(h + 1) * HEAD], wx_ref[h], preferred_element_type=F32) for h in range(nh)], axis=1)
    ra = _sigmoid(pre_a + vec[VEC_BA:VEC_BA + 1])
    ig = _sigmoid(pre_x + vec[VEC_BX:VEC_BX + 1])
    sp = _softplus(-vec[VEC_LAM:VEC_LAM + 1])
    log_a = -LRU_C * ra * sp
    a = jnp.exp(log_a)
    z = 2.0 * log_a
    one_minus_a2 = jnp.where(z > -0.01, -z * (1.0 + z * (0.5 + z * (1.0 / 6.0))), 1.0 - jnp.exp(z))
    mult = jnp.sqrt(one_minus_a2)
    return dict(xcb=xcb, ra=ra, ig=ig, sp=sp, a=a, mult=mult)


def _conv_pad_prev(pad_ref, cur, halo, first):
    pad_ref[0:SUBLANES, :] = jnp.where(first, 0.0, halo)
    pad_ref[SUBLANES:SUBLANES + cur.shape[0], :] = cur


def _gates_fwd(proj, d, cw8, vec8, wa, wx):
    s = proj.shape[0]
    lw = d // SUBLANES
    hb = ROWS // SUBLANES

    def body(x_ref, halo_ref, cw_ref, vec_ref, wa_ref, wx_ref, a_ref, u_ref, xc_ref, pad):
        _conv_pad_prev(pad, x_ref[...], halo_ref[...], pl.program_id(0) == 0)
        vec = vec_ref[...]
        xc = vec[VEC_CB:VEC_CB + 1]
        for k in range(CONV_TAPS):
            xc = xc + cw_ref[k:k + 1, :] * pad[pl.ds(SUBLANES - (CONV_TAPS - 1) + k, ROWS), :]
        gm = _gate_math(xc, wa_ref, wx_ref, vec)
        xc_ref[...] = xc
        _to_3d(a_ref, gm["a"])
        _to_3d(u_ref, gm["mult"] * (gm["ig"] * xc))

    spec3 = pl.BlockSpec((ROWS, SUBLANES, lw), lambda i: (i, 0, 0))
    wspec = pl.BlockSpec(wa.shape, lambda i: (0, 0, 0))
    return pl.pallas_call(
        body, out_shape=[jax.ShapeDtypeStruct((s, SUBLANES, lw), F32)] * 2 + [jax.ShapeDtypeStruct((s, d), F32)], grid=(s // ROWS,),
        in_specs=[pl.BlockSpec((ROWS, d), lambda i: (i, SEG_X)),
                  pl.BlockSpec((SUBLANES, d), lambda i: (jnp.maximum(i * hb - 1, 0), SEG_X)),
                  _vec_spec(d, SUBLANES), _vec_spec(d, SUBLANES), wspec, wspec],
        out_specs=[spec3, spec3, _row_spec(d)], scratch_shapes=[pltpu.VMEM((ROWS + SUBLANES, d), F32)],
        compiler_params=_params(("parallel",)), name="lru_gates_fwd")(proj, proj, cw8, vec8, wa, wx)


def _scan_fwd(a3, u3):
    s, _, lw = a3.shape

    def body(a_ref, u_ref, h_ref, hp_ref, carry):
        @pl.when(pl.program_id(0) == 0)
        def _():
            carry[...] = jnp.zeros_like(carry)

        def step(t, h):
            hp_ref[t] = h
            hn = a_ref[t] * h + u_ref[t]
            h_ref[t] = hn
            return hn

        carry[...] = lax.fori_loop(0, ROWS, step, carry[...], unroll=8)

    spec3 = pl.BlockSpec((ROWS, SUBLANES, lw), lambda i: (i, 0, 0))
    return pl.pallas_call(body, out_shape=[jax.ShapeDtypeStruct(a3.shape, F32)] * 2, grid=(s // ROWS,), in_specs=[spec3, spec3],
                          out_specs=[spec3, spec3], scratch_shapes=[pltpu.VMEM((SUBLANES, lw), F32)],
                          compiler_params=_params(("arbitrary",)), name="lru_scan_fwd")(a3, u3)


def _lru_out(h3, proj):
    s, _, lw = h3.shape
    d = lw * SUBLANES

    def body(h_ref, g_ref, y_ref, h2_ref):
        h = _from_3d(h_ref)
        h2_ref[...] = h
        y_ref[...] = (h * _gelu(g_ref[...])).astype(BF16)

    return pl.pallas_call(
        body, out_shape=[jax.ShapeDtypeStruct((s, d), BF16), jax.ShapeDtypeStruct((s, d), F32)], grid=(s // ROWS,),
        in_specs=[pl.BlockSpec((ROWS, SUBLANES, lw), lambda i: (i, 0, 0)), pl.BlockSpec((ROWS, d), lambda i: (i, SEG_GATE))],
        out_specs=[_row_spec(d), _row_spec(d)], compiler_params=_params(("parallel",)), name="lru_out")(h3, proj)


def _scan_bwd(a3, hp3, dh):
    s, _, lw = a3.shape
    d = lw * SUBLANES
    nb = s // ROWS

    def body(a_ref, hp_ref, dh_ref, g_ref, da_ref, dh3, carry):
        @pl.when(pl.program_id(0) == 0)
        def _():
            carry[...] = jnp.zeros_like(carry)

        _to_3d(dh3, dh_ref[...])

        def step(j, c):
            t = ROWS - 1 - j
            g = dh3[t] + c
            g_ref[t] = g
            da_ref[t] = g * hp_ref[t]
            return a_ref[t] * g

        carry[...] = lax.fori_loop(0, ROWS, step, carry[...], unroll=8)

    spec3 = pl.BlockSpec((ROWS, SUBLANES, lw), lambda i: (nb - 1 - i, 0, 0))
    return pl.pallas_call(
        body, out_shape=[jax.ShapeDtypeStruct(a3.shape, F32)] * 2, grid=(nb,),
        in_specs=[spec3, spec3, pl.BlockSpec((ROWS, d), lambda i: (nb - 1 - i, 0))], out_specs=[spec3, spec3],
        scratch_shapes=[pltpu.VMEM((ROWS, SUBLANES, lw), F32), pltpu.VMEM((SUBLANES, lw), F32)],
        compiler_params=_params(("arbitrary",)), name="lru_scan_bwd")(a3, hp3, dh)


def _gates_bwd(g3, da3, xc, wa, wx, vec8):
    s, d = xc.shape
    lw = d // SUBLANES
    nh = d // HEAD

    def body(g_ref, da_ref, xc_ref, wa_ref, wx_ref, vec_ref, dxc_ref, dwa_ref, dwx_ref, dvec_ref):
        @pl.when(pl.program_id(0) == 0)
        def _():
            dwa_ref[...] = jnp.zeros_like(dwa_ref)
            dwx_ref[...] = jnp.zeros_like(dwx_ref)
            dvec_ref[...] = jnp.zeros_like(dvec_ref)

        xc_v, vec = xc_ref[...], vec_ref[...]
        du, da = _from_3d(g_ref), _from_3d(da_ref)
        gm = _gate_math(xc_v, wa_ref, wx_ref, vec)
        ra, ig, sp, a, mult = gm["ra"], gm["ig"], gm["sp"], gm["a"], gm["mult"]
        dmult = du * ig * xc_v
        dlog_a = da * a - dmult * (a * a) / mult
        dpre_a = dlog_a * (-LRU_C * sp) * ra * (1.0 - ra)
        dpre_x = du * mult * xc_v * ig * (1.0 - ig)
        dlam = jnp.sum(dlog_a * (-LRU_C * ra), axis=0, keepdims=True) * (-_sigmoid(-vec[VEC_LAM:VEC_LAM + 1]))
        dvec_ref[VEC_BA:VEC_BA + 1, :] += jnp.sum(dpre_a, axis=0, keepdims=True)
        dvec_ref[VEC_BX:VEC_BX + 1, :] += jnp.sum(dpre_x, axis=0, keepdims=True)
        dvec_ref[VEC_LAM:VEC_LAM + 1, :] += dlam
        dab, dxb, xcb = dpre_a.astype(BF16), dpre_x.astype(BF16), gm["xcb"]
        back = []
        for h in range(nh):
            cols = slice(h * HEAD, (h + 1) * HEAD)
            dwa_ref[h] += lax.dot_general(xcb[:, cols], dab[:, cols], TN, preferred_element_type=F32)
            dwx_ref[h] += lax.dot_general(xcb[:, cols], dxb[:, cols], TN, preferred_element_type=F32)
            back.append(lax.dot_general(dab[:, cols], wa_ref[h], NT, preferred_element_type=F32)
                        + lax.dot_general(dxb[:, cols], wx_ref[h], NT, preferred_element_type=F32))
        dxc_ref[...] = du * mult * ig + jnp.concatenate(back, axis=1)

    spec3 = pl.BlockSpec((ROWS, SUBLANES, lw), lambda i: (i, 0, 0))
    wspec = pl.BlockSpec(wa.shape, lambda i: (0, 0, 0))
    return pl.pallas_call(
        body, out_shape=[jax.ShapeDtypeStruct((s, d), F32), jax.ShapeDtypeStruct(wa.shape, F32), jax.ShapeDtypeStruct(wa.shape, F32),
                         jax.ShapeDtypeStruct((SUBLANES, d), F32)],
        grid=(s // ROWS,), in_specs=[spec3, spec3, _row_spec(d), wspec, wspec, _vec_spec(d, SUBLANES)],
        out_specs=[_row_spec(d), wspec, wspec, _vec_spec(d, SUBLANES)],
        compiler_params=_params(("arbitrary",)), name="lru_gates_bwd")(g3, da3, xc, wa, wx, vec8)


def _conv_bwd(dxc, proj, cw8):
    s, d = dxc.shape
    hb = ROWS // SUBLANES
    last = s // SUBLANES - 1

    def body(dc_ref, dnext_ref, x_ref, xprev_ref, cw_ref, dx_ref, dcw_ref, padd, padx):
        i = pl.program_id(0)

        @pl.when(i == 0)
        def _():
            dcw_ref[...] = jnp.zeros_like(dcw_ref)

        dc = dc_ref[...]
        padd[0:ROWS, :] = dc
        padd[ROWS:ROWS + SUBLANES, :] = jnp.where(i == pl.num_programs(0) - 1, 0.0, dnext_ref[...])
        _conv_pad_prev(padx, x_ref[...], xprev_ref[...], i == 0)
        dx = jnp.zeros_like(dc)
        for k in range(CONV_TAPS):
            dx = dx + cw_ref[k:k + 1, :] * padd[pl.ds(CONV_TAPS - 1 - k, ROWS), :]
            dcw_ref[k:k + 1, :] += jnp.sum(dc * padx[pl.ds(SUBLANES - (CONV_TAPS - 1) + k, ROWS), :], axis=0, keepdims=True)
        dcw_ref[CONV_TAPS:CONV_TAPS + 1, :] += jnp.sum(dc, axis=0, keepdims=True)
        dx_ref[...] = dx.astype(BF16)

    return pl.pallas_call(
        body, out_shape=[jax.ShapeDtypeStruct((s, d), BF16), jax.ShapeDtypeStruct((SUBLANES, d), F32)], grid=(s // ROWS,),
        in_specs=[_row_spec(d), pl.BlockSpec((SUBLANES, d), lambda i: (jnp.minimum((i + 1) * hb, last), 0)),
                  pl.BlockSpec((ROWS, d), lambda i: (i, SEG_X)),
                  pl.BlockSpec((SUBLANES, d), lambda i: (jnp.maximum(i * hb - 1, 0), SEG_X)), _vec_spec(d, SUBLANES)],
        out_specs=[_row_spec(d), _vec_spec(d, SUBLANES)],
        scratch_shapes=[pltpu.VMEM((ROWS + SUBLANES, d), F32), pltpu.VMEM((ROWS + SUBLANES, d), F32)],
        compiler_params=_params(("arbitrary",)), name="lru_conv_bwd")(dxc, dxc, proj, proj, cw8)


def _coords():
    return lax.axis_index("x"), lax.axis_index("y"), lax.axis_index("c")


def _other_chips(x, y):
    return [(1 - x, y), (x, 1 - y), (1 - x, 1 - y)]


def _slab(ref, kind, shard_shape, idx, half=None):
    r, c = shard_shape
    r0, nr = (0, r) if half is None else (half * (r // 2), r // 2)
    if kind == "col":
        return ref.at[pl.ds(r0, nr), pl.ds(pl.multiple_of(idx * c, LANES), c)]
    if kind == "row":
        return ref.at[pl.ds(pl.multiple_of(idx * r, SUBLANES) + r0, nr), :]
    return ref.at[idx, pl.ds(r0, nr), :]


def _full_shape(shard_shape, kind):
    r, c = shard_shape
    return {"col": (r, c * N_DEV), "row": (r * N_DEV, c), "slot": (N_DEV, r, c)}[kind]


def _handshake(peers):
    barrier = pltpu.get_barrier_semaphore()
    for peer in peers:
        pl.semaphore_signal(barrier, inc=1, device_id=peer, device_id_type=MESH)
    pl.semaphore_wait(barrier, len(peers))


def _launch(name, body, out_shape, operands, sems, sequencer_id):
    if sequencer_id is None:
        return pl.pallas_call(body, out_shape=out_shape, in_specs=[HBM] * len(operands), out_specs=[HBM] * len(out_shape),
                              scratch_shapes=sems, name=name)(*operands)
    return pl.kernel(body, out_type=out_shape, mesh=plsc.ScalarSubcoreMesh(axis_name="seq", num_cores=1), name=name,
                     scratch_types=sems, compiler_params=pltpu.CompilerParams(collective_id=sequencer_id))(*operands)


AG_COPIES = 10


def _all_gather(name, shards, kinds, sequencer_id=None):
    n = len(shards)
    shapes = [s.shape for s in shards]

    def body(*refs):
        ins, outs = refs[:n], refs[n:2 * n]
        send_sems, recv_sems, local_sems = refs[2 * n:]
        x, y, c = _coords()
        me, sib, xn, yn, dg = (x, y, c), (x, y, 1 - c), (1 - x, y, c), (x, 1 - y, c), (1 - x, 1 - y, c)
        if sequencer_id is not None:
            _handshake([sib, xn, yn])

        def part(i, dev, half=None):
            return _slab(outs[i], kinds[i], shapes[i], 4 * dev[0] + 2 * dev[1] + dev[2], half)

        def copy(i, k, block, half, to, own=False):
            r = shapes[i][0]
            src = part(i, block, half) if not own else (ins[i] if half is None else ins[i].at[pl.ds(half * (r // 2), r // 2), :])
            return pltpu.make_async_remote_copy(
                src_ref=src, dst_ref=part(i, block, half), send_sem=send_sems.at[AG_COPIES * i + k],
                recv_sem=recv_sems.at[AG_COPIES * i + k], device_id=to, device_id_type=MESH)

        def other_core(dev):
            return (dev[0], dev[1], 1 - c)

        started = []

        def start(cp):
            cp.start()
            started.append(cp)

        for i in range(n):
            start(copy(i, 1, me, 0, xn, own=True))
            start(copy(i, 4, me, 1, yn, own=True))
            start(copy(i, 2, me, 1, xn, own=True))
            start(copy(i, 3, me, 0, yn, own=True))
            start(copy(i, 0, me, None, sib, own=True))
        mine = [pltpu.make_async_copy(ins[i], part(i, me), local_sems.at[i]) for i in range(n)]
        for cp in mine:
            cp.start()
        for i in range(n):
            copy(i, 1, xn, 0, me).wait_recv()
            start(copy(i, 5, xn, 0, yn))
            copy(i, 4, yn, 1, me).wait_recv()
            start(copy(i, 6, yn, 1, xn))
        for i in range(n):
            copy(i, 2, xn, 1, me).wait_recv()
            start(copy(i, 7, xn, None, sib))
            copy(i, 3, yn, 0, me).wait_recv()
            start(copy(i, 8, yn, None, sib))
        for i in range(n):
            copy(i, 5, dg, 0, me).wait_recv()
            copy(i, 6, dg, 1, me).wait_recv()
            start(copy(i, 9, dg, None, sib))
        for i in range(n):
            copy(i, 0, sib, None, me).wait_recv()
            for k, dev in ((7, xn), (8, yn), (9, dg)):
                copy(i, k, other_core(dev), None, me).wait_recv()
        for cp in started:
            cp.wait_send()
        for cp in mine:
            cp.wait()

    out_shape = [jax.ShapeDtypeStruct(_full_shape(s.shape, k), s.dtype) for s, k in zip(shards, kinds)]
    sems = [pltpu.SemaphoreType.DMA((AG_COPIES * n,)), pltpu.SemaphoreType.DMA((AG_COPIES * n,)), pltpu.SemaphoreType.DMA((n,))]
    return _launch(name, body, out_shape, shards, sems, sequencer_id)


def _gather_w_in_near(shard, small):
    r, cs = shard.shape

    def body(in_ref, small_ref, out_ref, slots_ref, send_sems, recv_sems, local_sem):
        x, y, c = _coords()
        me, sib, xn, yn = (x, y, c), (x, y, 1 - c), (1 - x, y, c), (x, 1 - y, c)
        me_idx = 4 * x + 2 * y + c
        tiny = [pltpu.make_async_remote_copy(src_ref=small_ref, dst_ref=slots_ref.at[me_idx], send_sem=send_sems.at[AG_COPIES + k],
                                             recv_sem=recv_sems.at[AG_COPIES + k], device_id=(x ^ (k >> 2), y ^ ((k >> 1) & 1), c ^ (k & 1)),
                                             device_id_type=MESH) for k in range(1, N_DEV)]
        for cp in tiny:
            cp.start()
        slots_ref_mine = pltpu.make_async_copy(small_ref, slots_ref.at[me_idx], local_sem.at[1])
        slots_ref_mine.start()

        def part(dev, half=None):
            return _slab(out_ref, "col", (r, cs), 4 * dev[0] + 2 * dev[1] + dev[2], half)

        def copy(k, block, half, to, own=False):
            src = part(block, half) if not own else (in_ref if half is None else in_ref.at[pl.ds(half * (r // 2), r // 2), :])
            return pltpu.make_async_remote_copy(src_ref=src, dst_ref=part(block, half), send_sem=send_sems.at[k],
                                                recv_sem=recv_sems.at[k], device_id=to, device_id_type=MESH)

        started = [copy(1, me, 0, xn, own=True), copy(4, me, 1, yn, own=True), copy(2, me, 1, xn, own=True),
                   copy(3, me, 0, yn, own=True), copy(0, me, None, sib, own=True)]
        for cp in started:
            cp.start()
        mine = pltpu.make_async_copy(in_ref, part(me), local_sem.at[0])
        mine.start()
        for k, k_half2, dev, k_fwd in ((1, 2, xn, 7), (4, 3, yn, 8)):
            copy(k, dev, 0 if k == 1 else 1, me).wait_recv()
            copy(k_half2, dev, 1 if k == 1 else 0, me).wait_recv()
            started.append(copy(k_fwd, dev, None, sib))
            started[-1].start()
        copy(0, sib, None, me).wait_recv()
        for k, dev in ((7, xn), (8, yn)):
            copy(k, (dev[0], dev[1], 1 - c), None, me).wait_recv()
        for k in range(1, N_DEV):
            pltpu.make_async_remote_copy(src_ref=small_ref, dst_ref=slots_ref.at[me_idx ^ k], send_sem=send_sems.at[AG_COPIES + k],
                                         recv_sem=recv_sems.at[AG_COPIES + k], device_id=me, device_id_type=MESH).wait_recv()
        for cp in started + tiny:
            cp.wait_send()
        mine.wait()
        slots_ref_mine.wait()

    n_sems = AG_COPIES + N_DEV
    return pl.pallas_call(
        body, out_shape=[jax.ShapeDtypeStruct((r, cs * N_DEV), shard.dtype), jax.ShapeDtypeStruct((N_DEV, *small.shape), small.dtype)],
        in_specs=[HBM, HBM], out_specs=[HBM, HBM],
        scratch_shapes=[pltpu.SemaphoreType.DMA((n_sems,)), pltpu.SemaphoreType.DMA((n_sems,)), pltpu.SemaphoreType.DMA((2,))],
        name="all_gather_w_in_near")(shard, small)


def _gather_w_in_far(near, cs, sequencer_id):
    r = near.shape[0]

    def body(near_ref, out_ref, send_sems, recv_sems):
        x, y, c = _coords()
        sib, xn, yn = (x, y, 1 - c), (1 - x, y, c), (x, 1 - y, c)
        _handshake([sib, xn, yn])

        def rows(half):
            return pl.ds(half * (r // 2), r // 2)

        def forward(k, dev, half, to):
            return pltpu.make_async_remote_copy(
                src_ref=_slab(near_ref, "col", (r, cs), 4 * dev[0] + 2 * dev[1] + dev[2], half), dst_ref=out_ref.at[rows(half), pl.ds(0, cs)],
                send_sem=send_sems.at[k], recv_sem=recv_sems.at[k], device_id=to, device_id_type=MESH)

        fwd = [forward(5, xn, 0, yn), forward(6, yn, 1, xn)]
        for cp in fwd:
            cp.start()
        for cp in fwd:
            cp.wait_recv()
        to_sib = pltpu.make_async_remote_copy(src_ref=out_ref.at[:, pl.ds(0, cs)], dst_ref=out_ref.at[:, pl.ds(cs, cs)],
                                              send_sem=send_sems.at[9], recv_sem=recv_sems.at[9], device_id=sib, device_id_type=MESH)
        to_sib.start()
        to_sib.wait()
        for cp in fwd:
            cp.wait_send()

    return _launch("all_gather_w_in_far", body, [jax.ShapeDtypeStruct((r, 2 * cs), near.dtype)], [near],
                   [pltpu.SemaphoreType.DMA((AG_COPIES,)), pltpu.SemaphoreType.DMA((AG_COPIES,))], sequencer_id)[0]


def _proj_in(name, xn, w, order, first, count, cs, prev=None, bm=1024):
    s, k = xn.shape
    bm = _blk(s, bm)
    near = prev is None

    def body(order_ref, a_ref, b_ref, *rest):
        rest[-1][...] = jnp.dot(a_ref[...], b_ref[...], preferred_element_type=F32)

    grid_spec = pltpu.PrefetchScalarGridSpec(
        num_scalar_prefetch=1, grid=(count, s // bm),
        in_specs=[pl.BlockSpec((bm, k), lambda j, i, o: (i, 0)),
                  pl.BlockSpec((k, cs), (lambda j, i, o: (0, o[first + j])) if near else (lambda j, i, o: (0, j)))] + ([] if near else [HBM]),
        out_specs=pl.BlockSpec((bm, cs), lambda j, i, o: (i, o[first + j])))
    return pl.pallas_call(body, out_shape=jax.ShapeDtypeStruct((s, N_DEV * cs), F32), grid_spec=grid_spec,
                          input_output_aliases={} if near else {3: 0}, compiler_params=_params(("arbitrary", "arbitrary")),
                          name=name)(order, xn, w, *([] if near else [prev]))


def _sibling_copies(kinds, shard_shapes):
    def make(ins, outs, send_sems, recv_sems):
        x, y, c = _coords()
        return [pltpu.make_async_remote_copy(
            src_ref=_slab(ins[i], kinds[i], shard_shapes[i], 2 * q + (1 - c)), dst_ref=outs[i].at[q],
            send_sem=send_sems.at[N_CHIP * i + q], recv_sem=recv_sems.at[N_CHIP * i + q],
            device_id=(x, y, 1 - c), device_id_type=MESH) for i in range(len(ins)) for q in range(N_CHIP)]
    return make


def _sibling_side(partials, kinds, shard_shapes):
    return (partials, [jax.ShapeDtypeStruct((N_CHIP, *s), BF16) for s in shard_shapes], N_CHIP * len(partials),
            _sibling_copies(kinds, shard_shapes))


def _exchange_siblings(name, partials, kinds, shard_shapes, sequencer_id=None):
    n = len(partials)
    make = _sibling_copies(kinds, shard_shapes)

    def body(*refs):
        if sequencer_id is not None:
            x, y, c = _coords()
            _handshake([(x, y, 1 - c)])
        cps = make(refs[:n], refs[n:2 * n], refs[2 * n], refs[2 * n + 1])
        for cp in cps:
            cp.start()
        for cp in cps:
            cp.wait()

    return _launch(name, body, [jax.ShapeDtypeStruct((N_CHIP, *s), BF16) for s in shard_shapes], partials,
                   [pltpu.SemaphoreType.DMA((N_CHIP * n,)), pltpu.SemaphoreType.DMA((N_CHIP * n,))], sequencer_id)


def _exchange_chips(name, chip_sums, sequencer_id=None):
    n = len(chip_sums)

    def body(*refs):
        ins, outs = refs[:n], refs[n:2 * n]
        send_sems, recv_sems = refs[2 * n:]
        x, y, c = _coords()
        if sequencer_id is not None:
            _handshake([(cx, cy, c) for cx, cy in _other_chips(x, y)])
        cps = []
        for i in range(n):
            for k, (cx, cy) in enumerate(_other_chips(x, y)):
                cps.append(pltpu.make_async_remote_copy(
                    src_ref=ins[i].at[2 * cx + cy], dst_ref=outs[i].at[k], send_sem=send_sems.at[3 * i + k],
                    recv_sem=recv_sems.at[3 * i + k], device_id=(cx, cy, c), device_id_type=MESH))
        for cp in cps:
            cp.start()
        for cp in cps:
            cp.wait()

    return _launch(name, body, [jax.ShapeDtypeStruct((3, *t.shape[1:]), BF16) for t in chip_sums], chip_sums,
                   [pltpu.SemaphoreType.DMA((3 * n,)), pltpu.SemaphoreType.DMA((3 * n,))], sequencer_id)


def _all_reduce_small(name, packed):
    rows = packed.shape[0] // N_DEV

    def body(p_ref, out_ref, rb, tot, send_sems, recv_sems):
        x, y, c = _coords()
        me = 4 * x + 2 * y + c

        def peer(k):
            return (x ^ (k >> 2), y ^ ((k >> 1) & 1), c ^ (k & 1))

        def rows_of(idx):
            return pl.ds(pl.multiple_of(idx * rows, SUBLANES), rows)

        def piece(ref, idx):
            return ref.at[rows_of(idx), :]

        scatter = [pltpu.make_async_remote_copy(src_ref=piece(p_ref, me ^ k), dst_ref=rb.at[k], send_sem=send_sems.at[k],
                                                recv_sem=recv_sems.at[k], device_id=peer(k), device_id_type=MESH) for k in range(1, N_DEV)]
        for cp in scatter:
            cp.start()
        acc = p_ref[rows_of(me), :]
        for cp in scatter:
            cp.wait_recv()
        for k in range(1, N_DEV):
            acc = acc + rb[k]
        tot[...] = acc
        out_ref[rows_of(me), :] = acc
        gather = [pltpu.make_async_remote_copy(src_ref=tot, dst_ref=piece(out_ref, me), send_sem=send_sems.at[N_DEV + k],
                                               recv_sem=recv_sems.at[N_DEV + k], device_id=peer(k), device_id_type=MESH)
                  for k in range(1, N_DEV)]
        for cp in gather:
            cp.start()
        for k in range(1, N_DEV):
            pltpu.make_async_remote_copy(src_ref=tot, dst_ref=piece(out_ref, me ^ k), send_sem=send_sems.at[N_DEV + k],
                                         recv_sem=recv_sems.at[N_DEV + k], device_id=peer(k), device_id_type=MESH).wait_recv()
        for cp in scatter + gather:
            cp.wait_send()

    vm = pl.BlockSpec(memory_space=pltpu.VMEM)
    return pl.pallas_call(
        body, out_shape=jax.ShapeDtypeStruct(packed.shape, F32), in_specs=[vm], out_specs=vm,
        scratch_shapes=[pltpu.VMEM((N_DEV, rows, LANES), F32), pltpu.VMEM((rows, LANES), F32),
                        pltpu.SemaphoreType.DMA((2 * N_DEV,)), pltpu.SemaphoreType.DMA((2 * N_DEV,))],
        compiler_params=pltpu.CompilerParams(vmem_limit_bytes=VMEM_LIMIT), name=name)(packed)


def _adamw_math(g, w, m, v):
    m = ADAM_B1 * m + (1.0 - ADAM_B1) * g
    v = ADAM_B2 * v + (1.0 - ADAM_B2) * (g * g)
    delta = -ADAM_LR * ((m / ADAM_C1) / (jnp.sqrt(v / ADAM_C2) + ADAM_EPS) + ADAM_WD * w)
    return delta, m, v


def _slab_spec(kind, shard_shape, tr, slab_of):
    r, c = shard_shape
    if kind == "col":
        return pl.BlockSpec((tr, c), lambda q, i, sc: (i, slab_of(q, sc)))
    return pl.BlockSpec((tr, c), lambda q, i, sc: (slab_of(q, sc) * (r // tr) + i, 0))


def _chip_sum(name, partial, recv, kind, shard_shape, core):
    r, c = shard_shape
    tr = _blk(r, 256)

    def body(core_ref, p_ref, r_ref, o_ref):
        o_ref[...] = (p_ref[...].astype(F32) + r_ref[...].astype(F32)).astype(BF16)

    spec4 = pl.BlockSpec((None, tr, c), lambda q, i, sc: (q, i, 0))
    grid_spec = pltpu.PrefetchScalarGridSpec(
        num_scalar_prefetch=1, grid=(N_CHIP, r // tr),
        in_specs=[_slab_spec(kind, shard_shape, tr, lambda q, sc: 2 * q + sc[0]), spec4], out_specs=spec4)
    return pl.pallas_call(body, out_shape=jax.ShapeDtypeStruct((N_CHIP, r, c), BF16), grid_spec=grid_spec,
                          compiler_params=_params(("parallel", "parallel")), name=name)(core, partial, recv)


def _adamw_shard(name, parts, w, m, v, chip):
    r, c = w.shape
    n_parts = len(parts)
    tr = _blk(r // n_parts, 128)
    per = r // n_parts // tr

    def body(chip_ref, *refs):
        src, (w_ref, m_ref, v_ref), (g_out, d_out, m_out, v_out) = refs[:2 * n_parts], refs[2 * n_parts:2 * n_parts + 3], refs[2 * n_parts + 3:]
        for p in range(n_parts):
            @pl.when(pl.program_id(0) // per == p)
            def _():
                g = src[2 * p][...].astype(F32)
                for k in range(3):
                    g = g + src[2 * p + 1][k].astype(F32)
                g_out[...] = g
                d_out[...], m_out[...], v_out[...] = _adamw_math(g, w_ref[...], m_ref[...], v_ref[...])

    def part_specs(p):
        at = lambda i: jnp.clip(i - p * per, 0, per - 1)
        return [pl.BlockSpec((None, tr, c), lambda i, sc: (sc[0], at(i), 0)), pl.BlockSpec((3, tr, c), lambda i, sc: (0, at(i), 0))]

    blk = pl.BlockSpec((tr, c), lambda i, sc: (i, 0))
    grid_spec = pltpu.PrefetchScalarGridSpec(
        num_scalar_prefetch=1, grid=(r // tr,), in_specs=[s for p in range(n_parts) for s in part_specs(p)] + [blk, blk, blk], out_specs=[blk] * 4)
    return pl.pallas_call(body, out_shape=[jax.ShapeDtypeStruct((r, c), F32)] * 4, grid_spec=grid_spec,
                          compiler_params=_params(("parallel",)), name=name)(chip, *[a for p in parts for a in p], w, m, v)


def _adamw_small(name, g, w, m, v):
    def body(g_ref, w_ref, m_ref, v_ref, d_out, m_out, v_out):
        d_out[...], m_out[...], v_out[...] = _adamw_math(g_ref[...], w_ref[...], m_ref[...], v_ref[...])

    vm = pl.BlockSpec(memory_space=pltpu.VMEM)
    return pl.pallas_call(body, out_shape=[jax.ShapeDtypeStruct(g.shape, F32)] * 3, in_specs=[vm] * 4, out_specs=[vm] * 3,
                          compiler_params=pltpu.CompilerParams(vmem_limit_bytes=VMEM_LIMIT), name=name)(g, w, m, v)


def _pack_rows(arrays, total_rows):
    flat = [a.reshape(-1, LANES) for a in arrays]
    used = sum(f.shape[0] for f in flat)
    return jnp.concatenate(flat + [jnp.zeros((total_rows - used, LANES), F32)], axis=0)


def _unpack_rows(packed, like):
    out, at = [], 0
    for a in like:
        n = a.size // LANES
        out.append(packed[at:at + n].reshape(a.shape))
        at += n
    return out


def kernel(x, norm_mix_g, w_in, conv_w, conv_b, lru_wa, lru_ba, lru_wx, lru_bx, lru_lambda, w_proj_attn, w_proj_lru, w_out, norm_mlp_g, w_up, w_down, norm_final_g, loss_target, m_norm_mix_g, m_w_in, m_conv_w, m_conv_b, m_lru_wa, m_lru_ba, m_lru_wx, m_lru_bx, m_lru_lambda, m_w_proj_attn, m_w_proj_lru, m_w_out, m_norm_mlp_g, m_w_up, m_w_down, m_norm_final_g, v_norm_mix_g, v_w_in, v_conv_w, v_conv_b, v_lru_wa, v_lru_ba, v_lru_wx, v_lru_bx, v_lru_lambda, v_w_proj_attn, v_w_proj_lru, v_w_out, v_norm_mlp_g, v_w_up, v_w_down, v_norm_final_g):
    xs, tgt = x[0], loss_target[0]
    s, d = xs.shape
    nh = d // HEAD
    ix, iy, ic = _coords()
    core = jnp.reshape(ic, (1,)).astype(jnp.int32)
    chip = jnp.reshape(2 * ix + iy, (1,)).astype(jnp.int32)
    dev = 4 * ix + 2 * iy + ic

    big = [w_in[0], w_proj_attn[0], w_proj_lru[0], w_out[0], w_up[0], w_down[0]]
    big_m = [m_w_in[0], m_w_proj_attn[0], m_w_proj_lru[0], m_w_out[0], m_w_up[0], m_w_down[0]]
    big_v = [v_w_in[0], v_w_proj_attn[0], v_w_proj_lru[0], v_w_out[0], v_w_up[0], v_w_down[0]]
    kinds = ["col", "row", "row", "row", "col", "row"]
    pad_taps = lambda t: jnp.pad(t, ((0, SUBLANES - CONV_TAPS), (0, 0)))
    shards = [w.astype(BF16) for w in big]
    cs_in = shards[0].shape[1]
    win_near, cw_slots = _gather_w_in_near(shards[0], pad_taps(conv_w[0]))
    win_far = _gather_w_in_far(win_near, cs_in, 7)
    later = lax.optimization_barrier((shards[1:], win_near))[0]
    wpa, wpl, wout = _all_gather("all_gather_mix", later[:3], kinds[1:4], sequencer_id=1)
    wup, wdown = _all_gather("all_gather_mlp", later[3:], kinds[4:], sequencer_id=5)
    order = jnp.stack([4 * cx + 2 * cy + cc for cx, cy in ((ix, iy), (1 - ix, iy), (ix, 1 - iy), (1 - ix, 1 - iy)) for cc in (ic, 1 - ic)]).astype(jnp.int32)
    cw8 = jnp.transpose(cw_slots, (1, 0, 2)).reshape(SUBLANES, d)
    row_id = lax.broadcasted_iota(jnp.int32, (SUBLANES, d), 0)
    vec8 = sum(jnp.where(row_id == k, t, 0.0) for k, t in ((VEC_CB, conv_b), (VEC_BA, lru_ba), (VEC_BX, lru_bx), (VEC_LAM, lru_lambda)))
    wa16, wx16 = lru_wa[0].astype(BF16), lru_wx[0].astype(BF16)
    slopes = 2.0 ** (-8.0 * jnp.arange(1, nh + 1, dtype=F32) / nh)

    def seg_specs(*segs):
        return lambda bm, bn: [pl.BlockSpec((bm, bn), (lambda i, j, kk, sg=sg: (i, sg * (d // bn) + j))) for sg in segs]

    def plain_specs(k):
        return lambda bm, bn: [pl.BlockSpec((bm, bn), lambda i, j, kk: (i, j)) for _ in range(k)]

    xn = _rms_fwd("norm_mix", xs, norm_mix_g)
    proj = _proj_in("proj_in_near", xn, win_near, order, 0, 6, cs_in)
    proj = _proj_in("proj_in_far", xn, win_far, order, 6, 2, cs_in, prev=proj)
    win = lax.dynamic_update_slice(win_near, win_far[:, :cs_in], (0, order[6] * cs_in))
    win = lax.dynamic_update_slice(win, win_far[:, cs_in:], (0, order[7] * cs_in))
    att, lse = _attn_fwd(proj, d, slopes)
    a3, u3, xc = _gates_fwd(proj, d, cw8, vec8, wa16, wx16)
    h3, hp3 = _scan_fwd(a3, u3)
    ylru, h2d = _lru_out(h3, proj)
    pa = _mm_fwd("proj_attn", att, wpa, 0, d, [F32])[0]

    def merge(acc, pa_b, ga, gl):
        return acc, _sigmoid(ga) * pa_b + _sigmoid(gl) * acc

    plr, merged = _mm_fwd("proj_lru_merge", ylru, wpl, 0, d, [F32, BF16], merge, (pa, proj, proj),
                          lambda bm, bn: plain_specs(1)(bm, bn) + seg_specs(SEG_GA, SEG_GL)(bm, bn), bm=512)
    h1 = _mm_fwd("mix_out", merged, wout, 0, d, [F32], lambda acc, r: (acc + r,), (xs,), plain_specs(1))[0]
    hn = _rms_fwd("norm_mlp", h1, norm_mlp_g)

    def relu2(acc):
        return acc, jnp.square(jnp.maximum(acc, 0.0))

    up, hid = _mm_fwd("mlp_up", hn, wup, 0, wup.shape[1], [BF16, BF16], relu2)
    h2 = _mm_fwd("mlp_down", hid, wdown, 0, d, [F32], lambda acc, r: (acc + r,), (h1,), plain_specs(1))[0]
    dh2, dh2b, dg3, loss_lanes = _final_loss(h2, tgt, norm_final_g.reshape(1, d))
    loss = lax.psum(0.5 / d * jnp.sum(loss_lanes), ("x", "y", "c"))
    dh2b = lax.optimization_barrier((dh2b, loss))[0]

    def reduce_group(tag, kk, shp, partials, from_sibling, sequencer_id):
        sums = [_chip_sum(f"chip_sum_{tag}_{i}", p, f, k, sh, core) for i, (p, f, k, sh) in enumerate(zip(partials, from_sibling, kk, shp))]
        return list(zip(sums, _exchange_chips(f"rs_chips_{tag}", sums, sequencer_id)))

    dup = _mm_nt("mlp_down_dx", dh2b, wdown, [BF16], lambda acc, u: (acc * (2.0 * jnp.maximum(u.astype(F32), 0.0)),), (up,), plain_specs(1))[0]
    g_wdown = _mm_tn("mlp_down_dw", hid, dh2b)
    g_wup = _mm_tn("mlp_up_dw", hn, dup)
    shp_mlp = [w.shape for w in big[4:]]
    (dhn,), sib_mlp = _mm_nt("mlp_up_dx", dup, wup, [F32], side=_sibling_side([g_wup, g_wdown], kinds[4:], shp_mlp))
    red_up, red_down = reduce_group("mlp", kinds[4:], shp_mlp, [g_wup, g_wdown], sib_mlp, 2)
    dhn = lax.optimization_barrier((dhn, red_up[0], red_down[0]))[0]
    dh1, dh1b, dg2 = _rms_bwd("norm_mlp_bwd", h1, norm_mlp_g, dhn, dh2)

    def merge_bwd(acc, pa_b, pl_b, ga, gl):
        sa, sl = _sigmoid(ga), _sigmoid(gl)
        return acc * sa, acc * sl, acc * pa_b * sa * (1.0 - sa), acc * pl_b * sl * (1.0 - sl)

    dpa, dpl, dga, dgl = _mm_nt("mix_out_dx", dh1b, wout, [BF16] * 4, merge_bwd, (pa, plr, proj, proj),
                                lambda bm, bn: plain_specs(2)(bm, bn) + seg_specs(SEG_GA, SEG_GL)(bm, bn), bm=512)
    g_wout = _mm_tn("mix_out_dw", merged, dh1b)
    datt = _mm_nt("proj_attn_dx", dpa, wpa, [F32])[0]
    g_wpa = _mm_tn("proj_attn_dw", att, dpa)

    def lru_out_bwd(acc, h_b, gate):
        return acc * _gelu(gate), acc * h_b * _gelu_grad(gate)

    g_wpl = _mm_tn("proj_lru_dw", ylru, dpl)
    shp_mix = [w.shape for w in big[1:4]]
    (dh, dxg), sib_mix = _mm_nt("proj_lru_dx", dpl, wpl, [F32, BF16], lru_out_bwd, (h2d, proj),
                                lambda bm, bn: plain_specs(1)(bm, bn) + seg_specs(SEG_GATE)(bm, bn), bm=512,
                                side=_sibling_side([g_wpa, g_wpl, g_wout], kinds[1:4], shp_mix))
    red_pa, red_pl, red_out = reduce_group("mix", kinds[1:4], shp_mix, [g_wpa, g_wpl, g_wout], sib_mix, 3)
    dq, dk, dv = _attn_bwd(proj, d, datt, att, lse, slopes)
    g3, da3 = _scan_bwd(a3, hp3, dh)
    dxc, dwa, dwx, dvec = _gates_bwd(g3, da3, xc, wa16, wx16, vec8)
    dxr, dconv = _conv_bwd(dxc, proj, cw8)

    def small_step(tag, grads, ws, ms, vs, like, after):
        n_rows = sum(g.size for g in grads) // LANES
        per_dev = -(-n_rows // (N_DEV * SUBLANES)) * SUBLANES
        packed = lax.optimization_barrier((_pack_rows(grads, N_DEV * per_dev), after))[0]
        total = _all_reduce_small(f"all_reduce_{tag}", packed)
        w_rows = -(-(sum(w.size for w in ws) // LANES) // SUBLANES) * SUBLANES
        upd = _adamw_small(f"adamw_{tag}", total[:w_rows], _pack_rows(ws, w_rows), _pack_rows(ms, w_rows), _pack_rows(vs, w_rows))
        return _unpack_rows(total, like), [_unpack_rows(t, ws) for t in upd]

    early_w = [conv_b, lru_wa, lru_ba, lru_wx, lru_bx, lru_lambda, norm_mlp_g, norm_final_g]
    early_m = [m_conv_b, m_lru_wa, m_lru_ba, m_lru_wx, m_lru_bx, m_lru_lambda, m_norm_mlp_g, m_norm_final_g]
    early_v = [v_conv_b, v_lru_wa, v_lru_ba, v_lru_wx, v_lru_bx, v_lru_lambda, v_norm_mlp_g, v_norm_final_g]
    early_g = [dconv[CONV_TAPS:CONV_TAPS + 1], dwa, dvec[VEC_BA:VEC_BA + 1], dwx, dvec[VEC_BX:VEC_BX + 1],
               dvec[VEC_LAM:VEC_LAM + 1], dg2, dg3, dconv[0:CONV_TAPS]]
    dq = lax.optimization_barrier((dq, red_up[1], red_down[1]))[0]
    early_sum, early_upd = small_step("small", early_g, early_w, early_m, early_v,
                                      early_w + [jax.ShapeDtypeStruct((1, CONV_TAPS, d), F32)], dq)
    g_cw_full = early_sum[-1]
    cshard = conv_w.shape[2]
    g_cw = lax.dynamic_slice(g_cw_full, (0, 0, dev * cshard), (1, CONV_TAPS, cshard))
    cw_delta, cw_m, cw_v = (t[:CONV_TAPS][None] for t in _adamw_small(
        "adamw_conv_w", pad_taps(g_cw[0]), pad_taps(conv_w[0]), pad_taps(m_conv_w[0]), pad_taps(v_conv_w[0])))
    dq = lax.optimization_barrier((dq, early_sum))[0]

    dproj = jnp.concatenate([dq, dk, dv, dxr, dxg, dga, dgl], axis=1)
    dproj = lax.optimization_barrier((dproj, red_pa[1], red_pl[1], red_out[1]))[0]
    half = (big[0].shape[0] // 2, big[0].shape[1])
    g_in0 = _mm_tn("proj_in_dw_0", xn, dproj, part=(0, 2))
    g_in1, sib_in0 = _mm_tn("proj_in_dw_1", xn, dproj, part=(1, 2), side=_sibling_side([g_in0], ["col"], [half]))
    red_in = reduce_group("in_0", ["col"], [half], [g_in0], sib_in0, 4)
    dproj = lax.optimization_barrier((dproj, red_in[0][0]))[0]
    (dxn0,), sib_in1 = _mm_nt("proj_in_dx_0", dproj, win, [F32], part=(0, 2), side=_sibling_side([g_in1], ["col"], [half]))
    red_in += reduce_group("in_1", ["col"], [half], [g_in1], sib_in1, 6)
    dproj = lax.optimization_barrier((dproj, red_in[1][0]))[0]
    dxn1 = _mm_nt("proj_in_dx_1", dproj, win, [F32], part=(1, 2))[0]
    dxn = jnp.concatenate([dxn0, dxn1], axis=0)
    dxn = lax.optimization_barrier((dxn, red_in[0][1]))[0]
    grad_x, _, dg1 = _rms_bwd("norm_mix_bwd", xs, norm_mix_g, dxn, dh1)
    red_up, red_down = lax.optimization_barrier(((red_up, red_down), dg1))[0]
    big_out = {i: _adamw_shard(f"adamw_{i}", [red], big[i], big_m[i], big_v[i], chip) for i, red in ((4, red_up), (5, red_down))}
    big_out.update({i: _adamw_shard(f"adamw_{i}", [red], big[i], big_m[i], big_v[i], chip) for i, red in ((1, red_pa), (2, red_pl), (3, red_out))})
    late_sum, late_upd = small_step("norm_mix", [dg1], [norm_mix_g], [m_norm_mix_g], [v_norm_mix_g], [norm_mix_g], big_out[3])
    big_out[0] = _adamw_shard("adamw_0", red_in, big[0], big_m[0], big_v[0], chip)
    s_grad = late_sum + early_sum[:-1]
    s_delta, s_m, s_v = (late_upd[j] + early_upd[j] for j in range(3))


    names = ["norm_mix_g", "w_in", "conv_w", "conv_b", "lru_wa", "lru_ba", "lru_wx", "lru_bx", "lru_lambda", "w_proj_attn", "w_proj_lru",
             "w_out", "norm_mlp_g", "w_up", "w_down", "norm_final_g"]
    small_names = ["norm_mix_g", "conv_b", "lru_wa", "lru_ba", "lru_wx", "lru_bx", "lru_lambda", "norm_mlp_g", "norm_final_g"]
    big_names = ["w_in", "w_proj_attn", "w_proj_lru", "w_out", "w_up", "w_down"]
    res = {"conv_w": (g_cw, cw_delta, cw_m, cw_v)}
    for i, nm in enumerate(small_names):
        res[nm] = (s_grad[i], s_delta[i], s_m[i], s_v[i])
    for i, nm in enumerate(big_names):
        res[nm] = tuple(t[None] for t in big_out[i])
    return (loss, grad_x[None], *[res[nm][0] for nm in names], *[res[nm][1] for nm in names],
            *[res[nm][2] for nm in names], *[res[nm][3] for nm in names])
```

```python
import jax
import jax.numpy as jnp
from jax import lax
from jax.experimental import pallas as pl
from jax.experimental.pallas import tpu as pltpu
from jax.experimental.pallas import tpu_sc as plsc

F32, BF16 = jnp.float32, jnp.bfloat16
MESH = pl.DeviceIdType.MESH
HBM = pl.BlockSpec(memory_space=pltpu.HBM)
N_DEV = 8
N_CHIP = 4
HEAD = 128
SPAN = 128
DILATIONS = (1, 4, 16)
CONV_TAPS = 4
LRU_C = 8.0
NORM_EPS = 1e-6
LANES = 128
SUBLANES = 8
VMEM_LIMIT = 56 * 1024 * 1024
ADAM_LR, ADAM_B1, ADAM_B2, ADAM_EPS, ADAM_WD, ADAM_STEP = 0.001, 0.9, 0.999, 1e-08, 0.01, 10
ADAM_C1 = 1.0 - ADAM_B1 ** ADAM_STEP
ADAM_C2 = 1.0 - ADAM_B2 ** ADAM_STEP
NEG = -1e30


def _params(sem=None):
    return pltpu.CompilerParams(dimension_semantics=sem, vmem_limit_bytes=VMEM_LIMIT)


def _sigmoid(v):
    return 1.0 / (1.0 + jnp.exp(-v))


def _gelu(v):
    k = 0.7978845608028654
    return 0.5 * v * (1.0 + jnp.tanh(k * (v + 0.044715 * v * v * v)))


def _gelu_grad(v):
    k = 0.7978845608028654
    t = jnp.tanh(k * (v + 0.044715 * v * v * v))
    return 0.5 * (1.0 + t) + 0.5 * v * (1.0 - t * t) * k * (1.0 + 3.0 * 0.044715 * v * v)


NN = (((1,), (0,)), ((), ()))
NT = (((1,), (1,)), ((), ()))
TN = (((0,), (0,)), ((), ()))


def _mm(name, a, a_spec, b, b_spec, dn, grid, out_shapes, out_specs, acc_block, epilogue=None, extras=(), extra_specs=(), side=None):
    nk, ne, no = grid[2], len(extras), len(out_shapes)
    side_ops, side_shapes, side_copies, make_copies = side if side is not None else ((), (), 0, None)
    ns_in, ns_out = len(side_ops), len(side_shapes)

    def body(*refs):
        a_ref, b_ref = refs[0], refs[1]
        ex, side_in = refs[2:2 + ne], refs[2 + ne:2 + ne + ns_in]
        outs = refs[2 + ne + ns_in:2 + ne + ns_in + no]
        side_out = refs[2 + ne + ns_in + no:2 + ne + ns_in + no + ns_out]
        scratch = refs[2 + ne + ns_in + no + ns_out:]
        at = [pl.program_id(ax) for ax in range(3)]
        if side is not None:
            @pl.when((at[0] == 0) & (at[1] == 0) & (at[2] == 0))
            def _():
                for cp in make_copies(side_in, side_out, scratch[-2], scratch[-1]):
                    cp.start()

        part = lax.dot_general(a_ref[...], b_ref[...], dn, preferred_element_type=F32)

        def finish(acc):
            vals = epilogue(acc, *[e[...] for e in ex]) if epilogue is not None else (acc,)
            for o, v in zip(outs, vals):
                o[...] = v.astype(o.dtype)

        if nk == 1:
            finish(part)
        else:
            acc_ref, k = scratch[0], at[2]

            @pl.when(k == 0)
            def _():
                acc_ref[...] = part

            @pl.when(k > 0)
            def _():
                acc_ref[...] += part

            @pl.when(k == nk - 1)
            def _():
                finish(acc_ref[...])

        if side is not None:
            @pl.when((at[0] == grid[0] - 1) & (at[1] == grid[1] - 1) & (at[2] == grid[2] - 1))
            def _():
                for cp in make_copies(side_in, side_out, scratch[-2], scratch[-1]):
                    cp.wait()

    scratch_shapes = [pltpu.VMEM(acc_block, F32)] if nk > 1 else []
    if side is not None:
        scratch_shapes += [pltpu.SemaphoreType.DMA((side_copies,)), pltpu.SemaphoreType.DMA((side_copies,))]
    res = pl.pallas_call(
        body, out_shape=[*out_shapes, *side_shapes], grid=grid, in_specs=[a_spec, b_spec, *extra_specs, *[HBM] * ns_in],
        out_specs=[*out_specs, *[HBM] * ns_out], scratch_shapes=scratch_shapes,
        compiler_params=_params(("arbitrary",) * 3 if side is not None else ("parallel", "parallel", "arbitrary")),
        name=name)(a, b, *extras, *side_ops)
    return res if side is None else (res[:no], res[no:])


def _blk(n, pref):
    return pref if n % pref == 0 else n


def _kblk(k):
    return k if k <= 2048 else next(b for b in (2048, 1024, 512) if k % b == 0)


def _mm_fwd(name, a, w, col0, ncols, out_dtypes, epilogue=None, extras=(), extra_specs_fn=None, seg_out=None, bm=1024, bn=1024):
    m, k = a.shape
    bm, bn = _blk(m, bm), _blk(ncols, bn)
    bk = _kblk(k)
    nk = k // bk
    cb0 = col0 // bn
    grid = (m // bm, ncols // bn, nk)
    a_spec = pl.BlockSpec((bm, bk), lambda i, j, kk: (i, kk))
    b_spec = pl.BlockSpec((bk, bn), lambda i, j, kk: (kk, cb0 + j))
    if seg_out is None:
        shapes = [jax.ShapeDtypeStruct((m, ncols), dt) for dt in out_dtypes]
        specs = [pl.BlockSpec((bm, bn), lambda i, j, kk: (i, j)) for _ in out_dtypes]
    else:
        per = seg_out // bn
        shapes = [jax.ShapeDtypeStruct((ncols // seg_out, m, seg_out), dt) for dt in out_dtypes]
        specs = [pl.BlockSpec((None, bm, bn), lambda i, j, kk: (j // per, i, j % per)) for _ in out_dtypes]
    ex_specs = extra_specs_fn(bm, bn) if extra_specs_fn else ()
    return _mm(name, a, a_spec, w, b_spec, NN, grid, shapes, specs, (bm, bn), epilogue, extras, ex_specs)


def _mm_nt(name, a, w, out_dtypes, epilogue=None, extras=(), extra_specs_fn=None, part=(0, 1), side=None, bm=1024, bn=1024):
    m, k = a.shape
    n = w.shape[0]
    m = m // part[1]
    bm, bn = _blk(m, bm), _blk(n, bn)
    bk = _kblk(k)
    grid = (m // bm, n // bn, k // bk)
    i0 = part[0] * (m // bm)
    a_spec = pl.BlockSpec((bm, bk), lambda i, j, kk: (i0 + i, kk))
    b_spec = pl.BlockSpec((bn, bk), lambda i, j, kk: (j, kk))
    shapes = [jax.ShapeDtypeStruct((m, n), dt) for dt in out_dtypes]
    specs = [pl.BlockSpec((bm, bn), lambda i, j, kk: (i, j)) for _ in out_dtypes]
    ex_specs = extra_specs_fn(bm, bn) if extra_specs_fn else ()
    return _mm(name, a, a_spec, w, b_spec, NT, grid, shapes, specs, (bm, bn), epilogue, extras, ex_specs, side)


def _mm_tn(name, a, b, part=(0, 1), side=None, bm=1024, bn=1024):
    t, m = a.shape
    n = b.shape[1]
    m = m // part[1]
    bm, bn = _blk(m, bm), _blk(n, bn)
    grid = (m // bm, n // bn, 1)
    i0 = part[0] * (m // bm)
    a_spec = pl.BlockSpec((t, bm), lambda i, j, kk: (0, i0 + i))
    b_spec = pl.BlockSpec((t, bn), lambda i, j, kk: (0, j))
    res = _mm(name, a, a_spec, b, b_spec, TN, grid, [jax.ShapeDtypeStruct((m, n), BF16)],
              [pl.BlockSpec((bm, bn), lambda i, j, kk: (i, j))], (bm, bn), side=side)
    return res[0] if side is None else (res[0][0], res[1])


ROWS = 256


def _row_spec(d):
    return pl.BlockSpec((ROWS, d), lambda i: (i, 0))


def _vec_spec(d, rows=1):
    return pl.BlockSpec((rows, d), lambda i: (0, 0))


def _rms_fwd(name, x, g):
    s, d = x.shape

    def body(x_ref, g_ref, o_ref):
        xv = x_ref[...]
        r = lax.rsqrt(jnp.mean(xv * xv, axis=-1, keepdims=True) + NORM_EPS)
        o_ref[...] = (xv * r * g_ref[...]).astype(BF16)

    return pl.pallas_call(body, out_shape=jax.ShapeDtypeStruct((s, d), BF16), grid=(s // ROWS,),
                          in_specs=[_row_spec(d), _vec_spec(d)], out_specs=_row_spec(d),
                          compiler_params=_params(("parallel",)), name=name)(x, g)


def _rms_bwd_math(xv, g, dy):
    r = lax.rsqrt(jnp.mean(xv * xv, axis=-1, keepdims=True) + NORM_EPS)
    n = xv * r
    z = dy * g
    dx = r * (z - n * jnp.mean(z * n, axis=-1, keepdims=True))
    return dx, jnp.sum(dy * n, axis=0, keepdims=True)


def _rms_bwd(name, x, g, dy, resid):
    s, d = x.shape

    def body(x_ref, g_ref, dy_ref, r_ref, dx_ref, dxb_ref, dg_ref):
        dx, dg = _rms_bwd_math(x_ref[...], g_ref[...], dy_ref[...])
        dx = dx + r_ref[...]
        dx_ref[...] = dx
        dxb_ref[...] = dx.astype(BF16)

        @pl.when(pl.program_id(0) == 0)
        def _():
            dg_ref[...] = jnp.zeros_like(dg_ref)

        dg_ref[...] += dg

    return pl.pallas_call(
        body, out_shape=[jax.ShapeDtypeStruct((s, d), F32), jax.ShapeDtypeStruct((s, d), BF16), jax.ShapeDtypeStruct((1, d), F32)],
        grid=(s // ROWS,), in_specs=[_row_spec(d), _vec_spec(d), _row_spec(d), _row_spec(d)],
        out_specs=[_row_spec(d), _row_spec(d), _vec_spec(d)], compiler_params=_params(("arbitrary",)), name=name)(x, g, dy, resid)


def _final_loss(h2, tgt, g):
    s, d = h2.shape

    def body(x_ref, t_ref, g_ref, dx_ref, dxb_ref, dg_ref, ls_ref):
        xv, gv = x_ref[...], g_ref[...]
        r = lax.rsqrt(jnp.mean(xv * xv, axis=-1, keepdims=True) + NORM_EPS)
        diff = xv * r * gv - t_ref[...]
        dx, dg = _rms_bwd_math(xv, gv, diff * (1.0 / d))
        dx_ref[...] = dx
        dxb_ref[...] = dx.astype(BF16)

        @pl.when(pl.program_id(0) == 0)
        def _():
            dg_ref[...] = jnp.zeros_like(dg_ref)
            ls_ref[...] = jnp.zeros_like(ls_ref)

        dg_ref[...] += dg
        ls_ref[...] += jnp.sum(diff * diff, axis=0, keepdims=True)

    return pl.pallas_call(
        body, out_shape=[jax.ShapeDtypeStruct((s, d), F32), jax.ShapeDtypeStruct((s, d), BF16),
                         jax.ShapeDtypeStruct((1, d), F32), jax.ShapeDtypeStruct((1, d), F32)],
        grid=(s // ROWS,), in_specs=[_row_spec(d), _row_spec(d), _vec_spec(d)],
        out_specs=[_row_spec(d), _row_spec(d), _vec_spec(d), _vec_spec(d)],
        compiler_params=_params(("arbitrary",)), name="final_norm_loss")(h2, tgt, g)


ATTN_Q = 128


ATTN_BATCH = 8


def _attn_units(s):
    units = []
    for gi, d in enumerate(DILATIONS):
        for r in range(d):
            for q0 in range(0, s // d, ATTN_Q):
                k0 = max(q0 - SPAN, 0)
                units.append((gi, d, r, q0, k0, q0 + ATTN_Q - k0))
    return units


def _stream_rows(d, r, start, size):
    return pl.ds(r + start * d, size) if d == 1 else pl.ds(r + start * d, size, stride=d)


def _attn_scores(q_ref, k_ref, slope, d, r, q0, k0, nk):
    qrows, krows = _stream_rows(d, r, q0, ATTN_Q), _stream_rows(d, r, k0, nk)
    qb = q_ref[qrows, :].astype(BF16)
    kb = k_ref[krows, :].astype(BF16)
    sc = lax.dot_general(qb, kb, NT, preferred_element_type=F32) * (HEAD ** -0.5)
    qi = lax.broadcasted_iota(jnp.int32, (ATTN_Q, nk), 0)
    kj = lax.broadcasted_iota(jnp.int32, (ATTN_Q, nk), 1)
    dist = (q0 - k0) + qi - kj
    valid = (dist >= 0) & (dist <= SPAN)
    sc = sc - (slope * d) * dist.astype(F32)
    return jnp.where(valid, sc, NEG), valid, qb, kb, qrows, krows


def _attn_fwd(proj, dm, slopes):
    s = proj.shape[0]
    units = _attn_units(s)

    def body(sl_ref, q_ref, k_ref, v_ref, att_ref, lse_ref, *scr):
        o_scr, l_scr = scr[:3], scr[3:]
        slope = sl_ref[pl.program_id(0)]
        for first in range(0, len(units), ATTN_BATCH):
            batch = units[first:first + ATTN_BATCH]
            scored = [_attn_scores(q_ref, k_ref, slope, d, r, q0, k0, nk) for _, d, r, q0, k0, nk in batch]
            soft = []
            for sc, _, _, _, _, _ in scored:
                m = jnp.max(sc, axis=-1, keepdims=True)
                p = jnp.exp(sc - m)
                soft.append((m, p, jnp.sum(p, axis=-1, keepdims=True)))
            outs = [lax.dot_general(p.astype(BF16), v_ref[sco[5], :].astype(BF16), NN, preferred_element_type=F32)
                    for (m, p, l), sco in zip(soft, scored)]
            for (gi, *_), (m, p, l), sco, o in zip(batch, soft, scored, outs):
                o_scr[gi][sco[4], :] = o / l
                l_scr[gi][sco[4], :] = jnp.broadcast_to(m + jnp.log(l), (ATTN_Q, HEAD))
        l0, l1, l2 = l_scr[0][...], l_scr[1][...], l_scr[2][...]
        m = jnp.maximum(jnp.maximum(l0, l1), l2)
        w0, w1, w2 = jnp.exp(l0 - m), jnp.exp(l1 - m), jnp.exp(l2 - m)
        tot = w0 + w1 + w2
        att_ref[...] = ((w0 * o_scr[0][...] + w1 * o_scr[1][...] + w2 * o_scr[2][...]) / tot).astype(BF16)
        lse_ref[...] = m + jnp.log(tot)

    def seg(i):
        return pl.BlockSpec((s, HEAD), lambda h: (0, i * (dm // HEAD) + h))

    col = pl.BlockSpec((s, HEAD), lambda h: (0, h))
    return pl.pallas_call(
        body, out_shape=[jax.ShapeDtypeStruct((s, dm), BF16), jax.ShapeDtypeStruct((s, dm), F32)], grid=(dm // HEAD,),
        in_specs=[pl.BlockSpec(memory_space=pltpu.SMEM), seg(0), seg(1), seg(2)], out_specs=[col, col],
        scratch_shapes=[pltpu.VMEM((s, HEAD), F32)] * (2 * len(DILATIONS)),
        compiler_params=_params(("parallel",)), name="attn_fwd")(slopes, proj, proj, proj)


def _attn_bwd(proj, dm, datt, att, lse, slopes):
    s = proj.shape[0]
    units = _attn_units(s)

    def body(sl_ref, q_ref, k_ref, v_ref, do_ref, att_ref, lse_ref, dq_ref, dk_ref, dv_ref, dq_scr, dk_scr, dv_scr, dl_scr):
        slope = sl_ref[pl.program_id(0)]
        delta = jnp.sum(do_ref[...] * att_ref[...].astype(F32), axis=-1, keepdims=True)
        dl_scr[...] = jnp.broadcast_to(delta, (s, HEAD))
        dq_scr[...] = jnp.zeros_like(dq_scr)
        dk_scr[...] = jnp.zeros_like(dk_scr)
        dv_scr[...] = jnp.zeros_like(dv_scr)
        for first in range(0, len(units), ATTN_BATCH):
            scored = [_attn_scores(q_ref, k_ref, slope, d, r, q0, k0, nk) for _, d, r, q0, k0, nk in units[first:first + ATTN_BATCH]]
            dobs = [do_ref[sco[4], :].astype(BF16) for sco in scored]
            dps = [lax.dot_general(dob, v_ref[sco[5], :].astype(BF16), NT, preferred_element_type=F32) for dob, sco in zip(dobs, scored)]
            ps = [jnp.where(sco[1], jnp.exp(sco[0] - lse_ref[sco[4], :][:, 0:1]), 0.0) for sco in scored]
            dss = [(p * (dp - dl_scr[sco[4], :][:, 0:1]) * (HEAD ** -0.5)).astype(BF16) for p, dp, sco in zip(ps, dps, scored)]
            dqs = [lax.dot_general(ds, sco[3], NN, preferred_element_type=F32) for ds, sco in zip(dss, scored)]
            dks = [lax.dot_general(ds, sco[2], TN, preferred_element_type=F32) for ds, sco in zip(dss, scored)]
            dvs = [lax.dot_general(p.astype(BF16), dob, TN, preferred_element_type=F32) for p, dob in zip(ps, dobs)]
            for sco, dq, dk, dv in zip(scored, dqs, dks, dvs):
                dq_scr[sco[4], :] += dq
                dk_scr[sco[5], :] += dk
                dv_scr[sco[5], :] += dv
        dq_ref[...] = dq_scr[...].astype(BF16)
        dk_ref[...] = dk_scr[...].astype(BF16)
        dv_ref[...] = dv_scr[...].astype(BF16)

    def seg(i):
        return pl.BlockSpec((s, HEAD), lambda h: (0, i * (dm // HEAD) + h))

    col = pl.BlockSpec((s, HEAD), lambda h: (0, h))
    return pl.pallas_call(
        body, out_shape=[jax.ShapeDtypeStruct((s, dm), BF16)] * 3, grid=(dm // HEAD,),
        in_specs=[pl.BlockSpec(memory_space=pltpu.SMEM), seg(0), seg(1), seg(2), col, col, col], out_specs=[col, col, col],
        scratch_shapes=[pltpu.VMEM((s, HEAD), F32)] * 4,
        compiler_params=_params(("parallel",)), name="attn_bwd")(slopes, proj, proj, proj, datt, att, lse)


VEC_CB, VEC_BA, VEC_BX, VEC_LAM = 0, 1, 2, 3
SEG_Q, SEG_K, SEG_V, SEG_X, SEG_GATE, SEG_GA, SEG_GL = range(7)


def _to_3d(ref3, val):
    lw = val.shape[1] // SUBLANES
    for j in range(SUBLANES):
        ref3[:, j, :] = val[:, j * lw:(j + 1) * lw]


def _from_3d(ref3):
    return jnp.concatenate([ref3[:, j, :] for j in range(SUBLANES)], axis=1)


def _softplus(z):
    return jnp.maximum(z, 0.0) + jnp.log1p(jnp.exp(-jnp.abs(z)))


def _gate_math(xc, wa_ref, wx_ref, vec):
    xcb = xc.astype(BF16)
    nh = xc.shape[1] // HEAD
    pre_a = jnp.concatenate([jnp.dot(xcb[:, h * HEAD:(h + 1) * HEAD], wa_ref[h], preferred_element_type=F32) for h in range(nh)], axis=1)
    pre_x = jnp.concatenate([jnp.dot(xcb[:, h * HEAD:(h + 1) * HEAD], wx_ref[h], preferred_element_type=F32) for h in range(nh)], axis=1)
    ra = _sigmoid(pre_a + vec[VEC_BA:VEC_BA + 1])
    ig = _sigmoid(pre_x + vec[VEC_BX:VEC_BX + 1])
    sp = _softplus(-vec[VEC_LAM:VEC_LAM + 1])
    log_a = -LRU_C * ra * sp
    a = jnp.exp(log_a)
    z = 2.0 * log_a
    one_minus_a2 = jnp.where(z > -0.01, -z * (1.0 + z * (0.5 + z * (1.0 / 6.0))), 1.0 - jnp.exp(z))
    mult = jnp.sqrt(one_minus_a2)
    return dict(xcb=xcb, ra=ra, ig=ig, sp=sp, a=a, mult=mult)


def _conv_pad_prev(pad_ref, cur, halo, first):
    pad_ref[0:SUBLANES, :] = jnp.where(first, 0.0, halo)
    pad_ref[SUBLANES:SUBLANES + cur.shape[0], :] = cur


def _gates_fwd(proj, d, cw8, vec8, wa, wx):
    s = proj.shape[0]
    lw = d // SUBLANES
    hb = ROWS // SUBLANES

    def body(x_ref, halo_ref, cw_ref, vec_ref, wa_ref, wx_ref, a_ref, u_ref, xc_ref, pad):
        _conv_pad_prev(pad, x_ref[...], halo_ref[...], pl.program_id(0) == 0)
        vec = vec_ref[...]
        xc = vec[VEC_CB:VEC_CB + 1]
        for k in range(CONV_TAPS):
            xc = xc + cw_ref[k:k + 1, :] * pad[pl.ds(SUBLANES - (CONV_TAPS - 1) + k, ROWS), :]
        gm = _gate_math(xc, wa_ref, wx_ref, vec)
        xc_ref[...] = xc
        _to_3d(a_ref, gm["a"])
        _to_3d(u_ref, gm["mult"] * (gm["ig"] * xc))

    spec3 = pl.BlockSpec((ROWS, SUBLANES, lw), lambda i: (i, 0, 0))
    wspec = pl.BlockSpec(wa.shape, lambda i: (0, 0, 0))
    return pl.pallas_call(
        body, out_shape=[jax.ShapeDtypeStruct((s, SUBLANES, lw), F32)] * 2 + [jax.ShapeDtypeStruct((s, d), F32)], grid=(s // ROWS,),
        in_specs=[pl.BlockSpec((ROWS, d), lambda i: (i, SEG_X)),
                  pl.BlockSpec((SUBLANES, d), lambda i: (jnp.maximum(i * hb - 1, 0), SEG_X)),
                  _vec_spec(d, SUBLANES), _vec_spec(d, SUBLANES), wspec, wspec],
        out_specs=[spec3, spec3, _row_spec(d)], scratch_shapes=[pltpu.VMEM((ROWS + SUBLANES, d), F32)],
        compiler_params=_params(("parallel",)), name="lru_gates_fwd")(proj, proj, cw8, vec8, wa, wx)


def _scan_fwd(a3, u3):
    s, _, lw = a3.shape

    def body(a_ref, u_ref, h_ref, hp_ref, carry):
        @pl.when(pl.program_id(0) == 0)
        def _():
            carry[...] = jnp.zeros_like(carry)

        def step(t, h):
            hp_ref[t] = h
            hn = a_ref[t] * h + u_ref[t]
            h_ref[t] = hn
            return hn

        carry[...] = lax.fori_loop(0, ROWS, step, carry[...], unroll=8)

    spec3 = pl.BlockSpec((ROWS, SUBLANES, lw), lambda i: (i, 0, 0))
    return pl.pallas_call(body, out_shape=[jax.ShapeDtypeStruct(a3.shape, F32)] * 2, grid=(s // ROWS,), in_specs=[spec3, spec3],
                          out_specs=[spec3, spec3], scratch_shapes=[pltpu.VMEM((SUBLANES, lw), F32)],
                          compiler_params=_params(("arbitrary",)), name="lru_scan_fwd")(a3, u3)


def _lru_out(h3, proj):
    s, _, lw = h3.shape
    d = lw * SUBLANES

    def body(h_ref, g_ref, y_ref, h2_ref):
        h = _from_3d(h_ref)
        h2_ref[...] = h
        y_ref[...] = (h * _gelu(g_ref[...])).astype(BF16)

    return pl.pallas_call(
        body, out_shape=[jax.ShapeDtypeStruct((s, d), BF16), jax.ShapeDtypeStruct((s, d), F32)], grid=(s // ROWS,),
        in_specs=[pl.BlockSpec((ROWS, SUBLANES, lw), lambda i: (i, 0, 0)), pl.BlockSpec((ROWS, d), lambda i: (i, SEG_GATE))],
        out_specs=[_row_spec(d), _row_spec(d)], compiler_params=_params(("parallel",)), name="lru_out")(h3, proj)


def _scan_bwd(a3, hp3, dh):
    s, _, lw = a3.shape
    d = lw * SUBLANES
    nb = s // ROWS

    def body(a_ref, hp_ref, dh_ref, g_ref, da_ref, dh3, carry):
        @pl.when(pl.program_id(0) == 0)
        def _():
            carry[...] = jnp.zeros_like(carry)

        _to_3d(dh3, dh_ref[...])

        def step(j, c):
            t = ROWS - 1 - j
            g = dh3[t] + c
            g_ref[t] = g
            da_ref[t] = g * hp_ref[t]
            return a_ref[t] * g

        carry[...] = lax.fori_loop(0, ROWS, step, carry[...], unroll=8)

    spec3 = pl.BlockSpec((ROWS, SUBLANES, lw), lambda i: (nb - 1 - i, 0, 0))
    return pl.pallas_call(
        body, out_shape=[jax.ShapeDtypeStruct(a3.shape, F32)] * 2, grid=(nb,),
        in_specs=[spec3, spec3, pl.BlockSpec((ROWS, d), lambda i: (nb - 1 - i, 0))], out_specs=[spec3, spec3],
        scratch_shapes=[pltpu.VMEM((ROWS, SUBLANES, lw), F32), pltpu.VMEM((SUBLANES, lw), F32)],
        compiler_params=_params(("arbitrary",)), name="lru_scan_bwd")(a3, hp3, dh)


def _gates_bwd(g3, da3, xc, wa, wx, vec8):
    s, d = xc.shape
    lw = d // SUBLANES
    nh = d // HEAD

    def body(g_ref, da_ref, xc_ref, wa_ref, wx_ref, vec_ref, dxc_ref, dwa_ref, dwx_ref, dvec_ref):
        @pl.when(pl.program_id(0) == 0)
        def _():
            dwa_ref[...] = jnp.zeros_like(dwa_ref)
            dwx_ref[...] = jnp.zeros_like(dwx_ref)
            dvec_ref[...] = jnp.zeros_like(dvec_ref)

        xc_v, vec = xc_ref[...], vec_ref[...]
        du, da = _from_3d(g_ref), _from_3d(da_ref)
        gm = _gate_math(xc_v, wa_ref, wx_ref, vec)
        ra, ig, sp, a, mult = gm["ra"], gm["ig"], gm["sp"], gm["a"], gm["mult"]
        dmult = du * ig * xc_v
        dlog_a = da * a - dmult * (a * a) / mult
        dpre_a = dlog_a * (-LRU_C * sp) * ra * (1.0 - ra)
        dpre_x = du * mult * xc_v * ig * (1.0 - ig)
        dlam = jnp.sum(dlog_a * (-LRU_C * ra), axis=0, keepdims=True) * (-_sigmoid(-vec[VEC_LAM:VEC_LAM + 1]))
        dvec_ref[VEC_BA:VEC_BA + 1, :] += jnp.sum(dpre_a, axis=0, keepdims=True)
        dvec_ref[VEC_BX:VEC_BX + 1, :] += jnp.sum(dpre_x, axis=0, keepdims=True)
        dvec_ref[VEC_LAM:VEC_LAM + 1, :] += dlam
        dab, dxb, xcb = dpre_a.astype(BF16), dpre_x.astype(BF16), gm["xcb"]
        back = []
        for h in range(nh):
            cols = slice(h * HEAD, (h + 1) * HEAD)
            dwa_ref[h] += lax.dot_general(xcb[:, cols], dab[:, cols], TN, preferred_element_type=F32)
            dwx_ref[h] += lax.dot_general(xcb[:, cols], dxb[:, cols], TN, preferred_element_type=F32)
            back.append(lax.dot_general(dab[:, cols], wa_ref[h], NT, preferred_element_type=F32)
                        + lax.dot_general(dxb[:, cols], wx_ref[h], NT, preferred_element_type=F32))
        dxc_ref[...] = du * mult * ig + jnp.concatenate(back, axis=1)

    spec3 = pl.BlockSpec((ROWS, SUBLANES, lw), lambda i: (i, 0, 0))
    wspec = pl.BlockSpec(wa.shape, lambda i: (0, 0, 0))
    return pl.pallas_call(
        body, out_shape=[jax.ShapeDtypeStruct((s, d), F32), jax.ShapeDtypeStruct(wa.shape, F32), jax.ShapeDtypeStruct(wa.shape, F32),
                         jax.ShapeDtypeStruct((SUBLANES, d), F32)],
        grid=(s // ROWS,), in_specs=[spec3, spec3, _row_spec(d), wspec, wspec, _vec_spec(d, SUBLANES)],
        out_specs=[_row_spec(d), wspec, wspec, _vec_spec(d, SUBLANES)],
        compiler_params=_params(("arbitrary",)), name="lru_gates_bwd")(g3, da3, xc, wa, wx, vec8)


def _conv_bwd(dxc, proj, cw8):
    s, d = dxc.shape
    hb = ROWS // SUBLANES
    last = s // SUBLANES - 1

    def body(dc_ref, dnext_ref, x_ref, xprev_ref, cw_ref, dx_ref, dcw_ref, padd, padx):
        i = pl.program_id(0)

        @pl.when(i == 0)
        def _():
            dcw_ref[...] = jnp.zeros_like(dcw_ref)

        dc = dc_ref[...]
        padd[0:ROWS, :] = dc
        padd[ROWS:ROWS + SUBLANES, :] = jnp.where(i == pl.num_programs(0) - 1, 0.0, dnext_ref[...])
        _conv_pad_prev(padx, x_ref[...], xprev_ref[...], i == 0)
        dx = jnp.zeros_like(dc)
        for k in range(CONV_TAPS):
            dx = dx + cw_ref[k:k + 1, :] * padd[pl.ds(CONV_TAPS - 1 - k, ROWS), :]
            dcw_ref[k:k + 1, :] += jnp.sum(dc * padx[pl.ds(SUBLANES - (CONV_TAPS - 1) + k, ROWS), :], axis=0, keepdims=True)
        dcw_ref[CONV_TAPS:CONV_TAPS + 1, :] += jnp.sum(dc, axis=0, keepdims=True)
        dx_ref[...] = dx.astype(BF16)

    return pl.pallas_call(
        body, out_shape=[jax.ShapeDtypeStruct((s, d), BF16), jax.ShapeDtypeStruct((SUBLANES, d), F32)], grid=(s // ROWS,),
        in_specs=[_row_spec(d), pl.BlockSpec((SUBLANES, d), lambda i: (jnp.minimum((i + 1) * hb, last), 0)),
                  pl.BlockSpec((ROWS, d), lambda i: (i, SEG_X)),
                  pl.BlockSpec((SUBLANES, d), lambda i: (jnp.maximum(i * hb - 1, 0), SEG_X)), _vec_spec(d, SUBLANES)],
        out_specs=[_row_spec(d), _vec_spec(d, SUBLANES)],
        scratch_shapes=[pltpu.VMEM((ROWS + SUBLANES, d), F32), pltpu.VMEM((ROWS + SUBLANES, d), F32)],
        compiler_params=_params(("arbitrary",)), name="lru_conv_bwd")(dxc, dxc, proj, proj, cw8)


def _coords():
    return lax.axis_index("x"), lax.axis_index("y"), lax.axis_index("c")


def _other_chips(x, y):
    return [(1 - x, y), (x, 1 - y), (1 - x, 1 - y)]


def _slab(ref, kind, shard_shape, idx, half=None):
    r, c = shard_shape
    r0, nr = (0, r) if half is None else (half * (r // 2), r // 2)
    if kind == "col":
        return ref.at[pl.ds(r0, nr), pl.ds(pl.multiple_of(idx * c, LANES), c)]
    if kind == "row":
        return ref.at[pl.ds(pl.multiple_of(idx * r, SUBLANES) + r0, nr), :]
    return ref.at[idx, pl.ds(r0, nr), :]


def _full_shape(shard_shape, kind):
    r, c = shard_shape
    return {"col": (r, c * N_DEV), "row": (r * N_DEV, c), "slot": (N_DEV, r, c)}[kind]


def _handshake(peers):
    barrier = pltpu.get_barrier_semaphore()
    for peer in peers:
        pl.semaphore_signal(barrier, inc=1, device_id=peer, device_id_type=MESH)
    pl.semaphore_wait(barrier, len(peers))


def _launch(name, body, out_shape, operands, sems, sequencer_id):
    if sequencer_id is None:
        return pl.pallas_call(body, out_shape=out_shape, in_specs=[HBM] * len(operands), out_specs=[HBM] * len(out_shape),
                              scratch_shapes=sems, name=name)(*operands)
    return pl.kernel(body, out_type=out_shape, mesh=plsc.ScalarSubcoreMesh(axis_name="seq", num_cores=1), name=name,
                     scratch_types=sems, compiler_params=pltpu.CompilerParams(collective_id=sequencer_id))(*operands)


AG_COPIES = 10


def _all_gather(name, shards, kinds, sequencer_id=None):
    n = len(shards)
    shapes = [s.shape for s in shards]

    def body(*refs):
        ins, outs = refs[:n], refs[n:2 * n]
        send_sems, recv_sems, local_sems = refs[2 * n:]
        x, y, c = _coords()
        me, sib, xn, yn, dg = (x, y, c), (x, y, 1 - c), (1 - x, y, c), (x, 1 - y, c), (1 - x, 1 - y, c)
        if sequencer_id is not None:
            _handshake([sib, xn, yn])

        def part(i, dev, half=None):
            return _slab(outs[i], kinds[i], shapes[i], 4 * dev[0] + 2 * dev[1] + dev[2], half)

        def copy(i, k, block, half, to, own=False):
            r = shapes[i][0]
            src = part(i, block, half) if not own else (ins[i] if half is None else ins[i].at[pl.ds(half * (r // 2), r // 2), :])
            return pltpu.make_async_remote_copy(
                src_ref=src, dst_ref=part(i, block, half), send_sem=send_sems.at[AG_COPIES * i + k],
                recv_sem=recv_sems.at[AG_COPIES * i + k], device_id=to, device_id_type=MESH)

        def other_core(dev):
            return (dev[0], dev[1], 1 - c)

        started = []

        def start(cp):
            cp.start()
            started.append(cp)

        for i in range(n):
            start(copy(i, 1, me, 0, xn, own=True))
            start(copy(i, 4, me, 1, yn, own=True))
            start(copy(i, 2, me, 1, xn, own=True))
            start(copy(i, 3, me, 0, yn, own=True))
            start(copy(i, 0, me, None, sib, own=True))
        mine = [pltpu.make_async_copy(ins[i], part(i, me), local_sems.at[i]) for i in range(n)]
        for cp in mine:
            cp.start()
        for i in range(n):
            copy(i, 1, xn, 0, me).wait_recv()
            start(copy(i, 5, xn, 0, yn))
            copy(i, 4, yn, 1, me).wait_recv()
            start(copy(i, 6, yn, 1, xn))
        for i in range(n):
            copy(i, 2, xn, 1, me).wait_recv()
            start(copy(i, 7, xn, None, sib))
            copy(i, 3, yn, 0, me).wait_recv()
            start(copy(i, 8, yn, None, sib))
        for i in range(n):
            copy(i, 5, dg, 0, me).wait_recv()
            copy(i, 6, dg, 1, me).wait_recv()
            start(copy(i, 9, dg, None, sib))
        for i in range(n):
            copy(i, 0, sib, None, me).wait_recv()
            for k, dev in ((7, xn), (8, yn), (9, dg)):
                copy(i, k, other_core(dev), None, me).wait_recv()
        for cp in started:
            cp.wait_send()
        for cp in mine:
            cp.wait()

    out_shape = [jax.ShapeDtypeStruct(_full_shape(s.shape, k), s.dtype) for s, k in zip(shards, kinds)]
    sems = [pltpu.SemaphoreType.DMA((AG_COPIES * n,)), pltpu.SemaphoreType.DMA((AG_COPIES * n,)), pltpu.SemaphoreType.DMA((n,))]
    return _launch(name, body, out_shape, shards, sems, sequencer_id)


def _gather_w_in_near(shard, small):
    r, cs = shard.shape

    def body(in_ref, small_ref, out_ref, slots_ref, send_sems, recv_sems, local_sem):
        x, y, c = _coords()
        me, sib, xn, yn = (x, y, c), (x, y, 1 - c), (1 - x, y, c), (x, 1 - y, c)
        me_idx = 4 * x + 2 * y + c
        tiny = [pltpu.make_async_remote_copy(src_ref=small_ref, dst_ref=slots_ref.at[me_idx], send_sem=send_sems.at[AG_COPIES + k],
                                             recv_sem=recv_sems.at[AG_COPIES + k], device_id=(x ^ (k >> 2), y ^ ((k >> 1) & 1), c ^ (k & 1)),
                                             device_id_type=MESH) for k in range(1, N_DEV)]
        for cp in tiny:
            cp.start()
        slots_ref_mine = pltpu.make_async_copy(small_ref, slots_ref.at[me_idx], local_sem.at[1])
        slots_ref_mine.start()

        def part(dev, half=None):
            return _slab(out_ref, "col", (r, cs), 4 * dev[0] + 2 * dev[1] + dev[2], half)

        def copy(k, block, half, to, own=False):
            src = part(block, half) if not own else (in_ref if half is None else in_ref.at[pl.ds(half * (r // 2), r // 2), :])
            return pltpu.make_async_remote_copy(src_ref=src, dst_ref=part(block, half), send_sem=send_sems.at[k],
                                                recv_sem=recv_sems.at[k], device_id=to, device_id_type=MESH)

        started = [copy(1, me, 0, xn, own=True), copy(4, me, 1, yn, own=True), copy(2, me, 1, xn, own=True),
                   copy(3, me, 0, yn, own=True), copy(0, me, None, sib, own=True)]
        for cp in started:
            cp.start()
        mine = pltpu.make_async_copy(in_ref, part(me), local_sem.at[0])
        mine.start()
        for k, k_half2, dev, k_fwd in ((1, 2, xn, 7), (4, 3, yn, 8)):
            copy(k, dev, 0 if k == 1 else 1, me).wait_recv()
            copy(k_half2, dev, 1 if k == 1 else 0, me).wait_recv()
            started.append(copy(k_fwd, dev, None, sib))
            started[-1].start()
        copy(0, sib, None, me).wait_recv()
        for k, dev in ((7, xn), (8, yn)):
            copy(k, (dev[0], dev[1], 1 - c), None, me).wait_recv()
        for k in range(1, N_DEV):
            pltpu.make_async_remote_copy(src_ref=small_ref, dst_ref=slots_ref.at[me_idx ^ k], send_sem=send_sems.at[AG_COPIES + k],
                                         recv_sem=recv_sems.at[AG_COPIES + k], device_id=me, device_id_type=MESH).wait_recv()
        for cp in started + tiny:
            cp.wait_send()
        mine.wait()
        slots_ref_mine.wait()

    n_sems = AG_COPIES + N_DEV
    return pl.pallas_call(
        body, out_shape=[jax.ShapeDtypeStruct((r, cs * N_DEV), shard.dtype), jax.ShapeDtypeStruct((N_DEV, *small.shape), small.dtype)],
        in_specs=[HBM, HBM], out_specs=[HBM, HBM],
        scratch_shapes=[pltpu.SemaphoreType.DMA((n_sems,)), pltpu.SemaphoreType.DMA((n_sems,)), pltpu.SemaphoreType.DMA((2,))],
        name="all_gather_w_in_near")(shard, small)


def _gather_w_in_far(near, cs, sequencer_id):
    r = near.shape[0]

    def body(near_ref, out_ref, send_sems, recv_sems):
        x, y, c = _coords()
        sib, xn, yn = (x, y, 1 - c), (1 - x, y, c), (x, 1 - y, c)
        _handshake([sib, xn, yn])

        def rows(half):
            return pl.ds(half * (r // 2), r // 2)

        def forward(k, dev, half, to):
            return pltpu.make_async_remote_copy(
                src_ref=_slab(near_ref, "col", (r, cs), 4 * dev[0] + 2 * dev[1] + dev[2], half), dst_ref=out_ref.at[rows(half), pl.ds(0, cs)],
                send_sem=send_sems.at[k], recv_sem=recv_sems.at[k], device_id=to, device_id_type=MESH)

        fwd = [forward(5, xn, 0, yn), forward(6, yn, 1, xn)]
        for cp in fwd:
            cp.start()
        for cp in fwd:
            cp.wait_recv()
        to_sib = pltpu.make_async_remote_copy(src_ref=out_ref.at[:, pl.ds(0, cs)], dst_ref=out_ref.at[:, pl.ds(cs, cs)],
                                              send_sem=send_sems.at[9], recv_sem=recv_sems.at[9], device_id=sib, device_id_type=MESH)
        to_sib.start()
        to_sib.wait()
        for cp in fwd:
            cp.wait_send()

    return _launch("all_gather_w_in_far", body, [jax.ShapeDtypeStruct((r, 2 * cs), near.dtype)], [near],
                   [pltpu.SemaphoreType.DMA((AG_COPIES,)), pltpu.SemaphoreType.DMA((AG_COPIES,))], sequencer_id)[0]


def _proj_in(name, xn, w, order, first, count, cs, prev=None, bm=1024):
    s, k = xn.shape
    bm = _blk(s, bm)
    near = prev is None

    def body(order_ref, a_ref, b_ref, *rest):
        rest[-1][...] = jnp.dot(a_ref[...], b_ref[...], preferred_element_type=F32)

    grid_spec = pltpu.PrefetchScalarGridSpec(
        num_scalar_prefetch=1, grid=(count, s // bm),
        in_specs=[pl.BlockSpec((bm, k), lambda j, i, o: (i, 0)),
                  pl.BlockSpec((k, cs), (lambda j, i, o: (0, o[first + j])) if near else (lambda j, i, o: (0, j)))] + ([] if near else [HBM]),
        out_specs=pl.BlockSpec((bm, cs), lambda j, i, o: (i, o[first + j])))
    return pl.pallas_call(body, out_shape=jax.ShapeDtypeStruct((s, N_DEV * cs), F32), grid_spec=grid_spec,
                          input_output_aliases={} if near else {3: 0}, compiler_params=_params(("arbitrary", "arbitrary")),
                          name=name)(order, xn, w, *([] if near else [prev]))


def _place_far(near, far, order, cs):
    def body(order_ref, near_ref, far_ref, out_ref, sems):
        cps = [pltpu.make_async_copy(far_ref.at[:, pl.ds(j * cs, cs)],
                                     out_ref.at[:, pl.ds(pl.multiple_of(order_ref[6 + j] * cs, LANES), cs)], sems.at[j]) for j in range(2)]
        for cp in cps:
            cp.start()
        for cp in cps:
            cp.wait()

    return pl.pallas_call(body, out_shape=jax.ShapeDtypeStruct(near.shape, near.dtype),
                          in_specs=[pl.BlockSpec(memory_space=pltpu.SMEM), HBM, HBM], out_specs=HBM,
                          scratch_shapes=[pltpu.SemaphoreType.DMA((2,))], input_output_aliases={1: 0}, name="place_w_in_far")(order, near, far)


def _sibling_copies(kinds, shard_shapes):
    def make(ins, outs, send_sems, recv_sems):
        x, y, c = _coords()
        return [pltpu.make_async_remote_copy(
            src_ref=_slab(ins[i], kinds[i], shard_shapes[i], 2 * q + (1 - c)), dst_ref=outs[i].at[q],
            send_sem=send_sems.at[N_CHIP * i + q], recv_sem=recv_sems.at[N_CHIP * i + q],
            device_id=(x, y, 1 - c), device_id_type=MESH) for i in range(len(ins)) for q in range(N_CHIP)]
    return make


def _sibling_side(partials, kinds, shard_shapes):
    return (partials, [jax.ShapeDtypeStruct((N_CHIP, *s), BF16) for s in shard_shapes], N_CHIP * len(partials),
            _sibling_copies(kinds, shard_shapes))


def _exchange_siblings(name, partials, kinds, shard_shapes, sequencer_id=None):
    n = len(partials)
    make = _sibling_copies(kinds, shard_shapes)

    def body(*refs):
        if sequencer_id is not None:
            x, y, c = _coords()
            _handshake([(x, y, 1 - c)])
        cps = make(refs[:n], refs[n:2 * n], refs[2 * n], refs[2 * n + 1])
        for cp in cps:
            cp.start()
        for cp in cps:
            cp.wait()

    return _launch(name, body, [jax.ShapeDtypeStruct((N_CHIP, *s), BF16) for s in shard_shapes], partials,
                   [pltpu.SemaphoreType.DMA((N_CHIP * n,)), pltpu.SemaphoreType.DMA((N_CHIP * n,))], sequencer_id)


def _exchange_chips(name, chip_sums, sequencer_id=None):
    n = len(chip_sums)

    def body(*refs):
        ins, outs = refs[:n], refs[n:2 * n]
        send_sems, recv_sems = refs[2 * n:]
        x, y, c = _coords()
        if sequencer_id is not None:
            _handshake([(cx, cy, c) for cx, cy in _other_chips(x, y)])
        cps = []
        for i in range(n):
            for k, (cx, cy) in enumerate(_other_chips(x, y)):
                cps.append(pltpu.make_async_remote_copy(
                    src_ref=ins[i].at[2 * cx + cy], dst_ref=outs[i].at[k], send_sem=send_sems.at[3 * i + k],
                    recv_sem=recv_sems.at[3 * i + k], device_id=(cx, cy, c), device_id_type=MESH))
        for cp in cps:
            cp.start()
        for cp in cps:
            cp.wait()

    return _launch(name, body, [jax.ShapeDtypeStruct((3, *t.shape[1:]), BF16) for t in chip_sums], chip_sums,
                   [pltpu.SemaphoreType.DMA((3 * n,)), pltpu.SemaphoreType.DMA((3 * n,))], sequencer_id)


def _all_reduce_small(name, packed):
    rows = packed.shape[0] // N_DEV

    def body(p_ref, out_ref, rb, tot, send_sems, recv_sems):
        x, y, c = _coords()
        me = 4 * x + 2 * y + c

        def peer(k):
            return (x ^ (k >> 2), y ^ ((k >> 1) & 1), c ^ (k & 1))

        def rows_of(idx):
            return pl.ds(pl.multiple_of(idx * rows, SUBLANES), rows)

        def piece(ref, idx):
            return ref.at[rows_of(idx), :]

        scatter = [pltpu.make_async_remote_copy(src_ref=piece(p_ref, me ^ k), dst_ref=rb.at[k], send_sem=send_sems.at[k],
                                                recv_sem=recv_sems.at[k], device_id=peer(k), device_id_type=MESH) for k in range(1, N_DEV)]
        for cp in scatter:
            cp.start()
        acc = p_ref[rows_of(me), :]
        for cp in scatter:
            cp.wait_recv()
        for k in range(1, N_DEV):
            acc = acc + rb[k]
        tot[...] = acc
        out_ref[rows_of(me), :] = acc
        gather = [pltpu.make_async_remote_copy(src_ref=tot, dst_ref=piece(out_ref, me), send_sem=send_sems.at[N_DEV + k],
                                               recv_sem=recv_sems.at[N_DEV + k], device_id=peer(k), device_id_type=MESH)
                  for k in range(1, N_DEV)]
        for cp in gather:
            cp.start()
        for k in range(1, N_DEV):
            pltpu.make_async_remote_copy(src_ref=tot, dst_ref=piece(out_ref, me ^ k), send_sem=send_sems.at[N_DEV + k],
                                         recv_sem=recv_sems.at[N_DEV + k], device_id=peer(k), device_id_type=MESH).wait_recv()
        for cp in scatter + gather:
            cp.wait_send()

    vm = pl.BlockSpec(memory_space=pltpu.VMEM)
    return pl.pallas_call(
        body, out_shape=jax.ShapeDtypeStruct(packed.shape, F32), in_specs=[vm], out_specs=vm,
        scratch_shapes=[pltpu.VMEM((N_DEV, rows, LANES), F32), pltpu.VMEM((rows, LANES), F32),
                        pltpu.SemaphoreType.DMA((2 * N_DEV,)), pltpu.SemaphoreType.DMA((2 * N_DEV,))],
        compiler_params=pltpu.CompilerParams(vmem_limit_bytes=VMEM_LIMIT), name=name)(packed)


def _adamw_math(g, w, m, v):
    m = ADAM_B1 * m + (1.0 - ADAM_B1) * g
    v = ADAM_B2 * v + (1.0 - ADAM_B2) * (g * g)
    delta = -ADAM_LR * ((m / ADAM_C1) / (jnp.sqrt(v / ADAM_C2) + ADAM_EPS) + ADAM_WD * w)
    return delta, m, v


def _slab_spec(kind, shard_shape, tr, slab_of):
    r, c = shard_shape
    if kind == "col":
        return pl.BlockSpec((tr, c), lambda q, i, sc: (i, slab_of(q, sc)))
    return pl.BlockSpec((tr, c), lambda q, i, sc: (slab_of(q, sc) * (r // tr) + i, 0))


def _chip_sum(name, partial, recv, kind, shard_shape, core):
    r, c = shard_shape
    tr = _blk(r, 1024)

    def body(core_ref, p_ref, r_ref, o_ref):
        o_ref[...] = (p_ref[...].astype(F32) + r_ref[...].astype(F32)).astype(BF16)

    spec4 = pl.BlockSpec((None, tr, c), lambda q, i, sc: (q, i, 0))
    grid_spec = pltpu.PrefetchScalarGridSpec(
        num_scalar_prefetch=1, grid=(N_CHIP, r // tr),
        in_specs=[_slab_spec(kind, shard_shape, tr, lambda q, sc: 2 * q + sc[0]), spec4], out_specs=spec4)
    return pl.pallas_call(body, out_shape=jax.ShapeDtypeStruct((N_CHIP, r, c), BF16), grid_spec=grid_spec,
                          compiler_params=_params(("parallel", "parallel")), name=name)(core, partial, recv)


def _adamw_shard(name, parts, w, m, v, chip):
    r, c = w.shape
    n_parts = len(parts)
    tr = _blk(r // n_parts, 256)
    per = r // n_parts // tr

    def body(chip_ref, *refs):
        src, (w_ref, m_ref, v_ref), (g_out, d_out, m_out, v_out) = refs[:2 * n_parts], refs[2 * n_parts:2 * n_parts + 3], refs[2 * n_parts + 3:]
        for p in range(n_parts):
            @pl.when(pl.program_id(0) // per == p)
            def _():
                g = src[2 * p][...].astype(F32)
                for k in range(3):
                    g = g + src[2 * p + 1][k].astype(F32)
                g_out[...] = g
                d_out[...], m_out[...], v_out[...] = _adamw_math(g, w_ref[...], m_ref[...], v_ref[...])

    def part_specs(p):
        at = lambda i: jnp.clip(i - p * per, 0, per - 1)
        return [pl.BlockSpec((None, tr, c), lambda i, sc: (sc[0], at(i), 0)), pl.BlockSpec((3, tr, c), lambda i, sc: (0, at(i), 0))]

    blk = pl.BlockSpec((tr, c), lambda i, sc: (i, 0))
    grid_spec = pltpu.PrefetchScalarGridSpec(
        num_scalar_prefetch=1, grid=(r // tr,), in_specs=[s for p in range(n_parts) for s in part_specs(p)] + [blk, blk, blk], out_specs=[blk] * 4)
    return pl.pallas_call(body, out_shape=[jax.ShapeDtypeStruct((r, c), F32)] * 4, grid_spec=grid_spec,
                          compiler_params=_params(("parallel",)), name=name)(chip, *[a for p in parts for a in p], w, m, v)


def _adamw_small(name, g, w, m, v):
    def body(g_ref, w_ref, m_ref, v_ref, d_out, m_out, v_out):
        d_out[...], m_out[...], v_out[...] = _adamw_math(g_ref[...], w_ref[...], m_ref[...], v_ref[...])

    vm = pl.BlockSpec(memory_space=pltpu.VMEM)
    return pl.pallas_call(body, out_shape=[jax.ShapeDtypeStruct(g.shape, F32)] * 3, in_specs=[vm] * 4, out_specs=[vm] * 3,
                          compiler_params=pltpu.CompilerParams(vmem_limit_bytes=VMEM_LIMIT), name=name)(g, w, m, v)


def _pack_rows(arrays, total_rows):
    flat = [a.reshape(-1, LANES) for a in arrays]
    used = sum(f.shape[0] for f in flat)
    return jnp.concatenate(flat + [jnp.zeros((total_rows - used, LANES), F32)], axis=0)


def _unpack_rows(packed, like):
    out, at = [], 0
    for a in like:
        n = a.size // LANES
        out.append(packed[at:at + n].reshape(a.shape))
        at += n
    return out


def kernel(x, norm_mix_g, w_in, conv_w, conv_b, lru_wa, lru_ba, lru_wx, lru_bx, lru_lambda, w_proj_attn, w_proj_lru, w_out, norm_mlp_g, w_up, w_down, norm_final_g, loss_target, m_norm_mix_g, m_w_in, m_conv_w, m_conv_b, m_lru_wa, m_lru_ba, m_lru_wx, m_lru_bx, m_lru_lambda, m_w_proj_attn, m_w_proj_lru, m_w_out, m_norm_mlp_g, m_w_up, m_w_down, m_norm_final_g, v_norm_mix_g, v_w_in, v_conv_w, v_conv_b, v_lru_wa, v_lru_ba, v_lru_wx, v_lru_bx, v_lru_lambda, v_w_proj_attn, v_w_proj_lru, v_w_out, v_norm_mlp_g, v_w_up, v_w_down, v_norm_final_g):
    xs, tgt = x[0], loss_target[0]
    s, d = xs.shape
    nh = d // HEAD
    ix, iy, ic = _coords()
    core = jnp.reshape(ic, (1,)).astype(jnp.int32)
    chip = jnp.reshape(2 * ix + iy, (1,)).astype(jnp.int32)
    dev = 4 * ix + 2 * iy + ic

    big = [w_in[0], w_proj_attn[0], w_proj_lru[0], w_out[0], w_up[0], w_down[0]]
    big_m = [m_w_in[0], m_w_proj_attn[0], m_w_proj_lru[0], m_w_out[0], m_w_up[0], m_w_down[0]]
    big_v = [v_w_in[0], v_w_proj_attn[0], v_w_proj_lru[0], v_w_out[0], v_w_up[0], v_w_down[0]]
    kinds = ["col", "row", "row", "row", "col", "row"]
    pad_taps = lambda t: jnp.pad(t, ((0, SUBLANES - CONV_TAPS), (0, 0)))
    shards = [w.astype(BF16) for w in big]
    cs_in = shards[0].shape[1]
    win_near, cw_slots = _gather_w_in_near(shards[0], pad_taps(conv_w[0]))
    win_far = _gather_w_in_far(win_near, cs_in, 7)
    later = lax.optimization_barrier((shards[1:], win_near))[0]
    wpa, wpl, wout = _all_gather("all_gather_mix", later[:3], kinds[1:4], sequencer_id=1)
    wup, wdown = _all_gather("all_gather_mlp", later[3:], kinds[4:], sequencer_id=5)
    order = jnp.stack([4 * cx + 2 * cy + cc for cx, cy in ((ix, iy), (1 - ix, iy), (ix, 1 - iy), (1 - ix, 1 - iy)) for cc in (ic, 1 - ic)]).astype(jnp.int32)
    cw8 = jnp.transpose(cw_slots, (1, 0, 2)).reshape(SUBLANES, d)
    row_id = lax.broadcasted_iota(jnp.int32, (SUBLANES, d), 0)
    vec8 = sum(jnp.where(row_id == k, t, 0.0) for k, t in ((VEC_CB, conv_b), (VEC_BA, lru_ba), (VEC_BX, lru_bx), (VEC_LAM, lru_lambda)))
    wa16, wx16 = lru_wa[0].astype(BF16), lru_wx[0].astype(BF16)
    slopes = 2.0 ** (-8.0 * jnp.arange(1, nh + 1, dtype=F32) / nh)

    def seg_specs(*segs):
        return lambda bm, bn: [pl.BlockSpec((bm, bn), (lambda i, j, kk, sg=sg: (i, sg * (d // bn) + j))) for sg in segs]

    def plain_specs(k):
        return lambda bm, bn: [pl.BlockSpec((bm, bn), lambda i, j, kk: (i, j)) for _ in range(k)]

    xn = _rms_fwd("norm_mix", xs, norm_mix_g)
    proj = _proj_in("proj_in_near", xn, win_near, order, 0, 6, cs_in)
    proj = _proj_in("proj_in_far", xn, win_far, order, 6, 2, cs_in, prev=proj)
    win = _place_far(lax.optimization_barrier((win_near, proj))[0], win_far, order, cs_in)
    att, lse = _attn_fwd(proj, d, slopes)
    a3, u3, xc = _gates_fwd(proj, d, cw8, vec8, wa16, wx16)
    h3, hp3 = _scan_fwd(a3, u3)
    ylru, h2d = _lru_out(h3, proj)
    pa = _mm_fwd("proj_attn", att, wpa, 0, d, [F32])[0]

    def merge(acc, pa_b, ga, gl):
        return acc, _sigmoid(ga) * pa_b + _sigmoid(gl) * acc

    plr, merged = _mm_fwd("proj_lru_merge", ylru, wpl, 0, d, [F32, BF16], merge, (pa, proj, proj),
                          lambda bm, bn: plain_specs(1)(bm, bn) + seg_specs(SEG_GA, SEG_GL)(bm, bn), bm=512)
    h1 = _mm_fwd("mix_out", merged, wout, 0, d, [F32], lambda acc, r: (acc + r,), (xs,), plain_specs(1))[0]
    hn = _rms_fwd("norm_mlp", h1, norm_mlp_g)

    def relu2(acc):
        return acc, jnp.square(jnp.maximum(acc, 0.0))

    up, hid = _mm_fwd("mlp_up", hn, wup, 0, wup.shape[1], [BF16, BF16], relu2)
    h2 = _mm_fwd("mlp_down", hid, wdown, 0, d, [F32], lambda acc, r: (acc + r,), (h1,), plain_specs(1))[0]
    dh2, dh2b, dg3, loss_lanes = _final_loss(h2, tgt, norm_final_g.reshape(1, d))
    loss = lax.psum(0.5 / d * jnp.sum(loss_lanes), ("x", "y", "c"))
    dh2b = lax.optimization_barrier((dh2b, loss))[0]

    def reduce_group(tag, kk, shp, partials, from_sibling, sequencer_id):
        sums = [_chip_sum(f"chip_sum_{tag}_{i}", p, f, k, sh, core) for i, (p, f, k, sh) in enumerate(zip(partials, from_sibling, kk, shp))]
        return list(zip(sums, _exchange_chips(f"rs_chips_{tag}", sums, sequencer_id)))

    dup = _mm_nt("mlp_down_dx", dh2b, wdown, [BF16], lambda acc, u: (acc * (2.0 * jnp.maximum(u.astype(F32), 0.0)),), (up,), plain_specs(1))[0]
    g_wdown = _mm_tn("mlp_down_dw", hid, dh2b)
    g_wup = _mm_tn("mlp_up_dw", hn, dup)
    shp_mlp = [w.shape for w in big[4:]]
    (dhn,), sib_mlp = _mm_nt("mlp_up_dx", dup, wup, [F32], side=_sibling_side([g_wup, g_wdown], kinds[4:], shp_mlp))
    red_up, red_down = reduce_group("mlp", kinds[4:], shp_mlp, [g_wup, g_wdown], sib_mlp, 2)
    dhn = lax.optimization_barrier((dhn, red_up[0], red_down[0]))[0]
    dh1, dh1b, dg2 = _rms_bwd("norm_mlp_bwd", h1, norm_mlp_g, dhn, dh2)

    def merge_bwd(acc, pa_b, pl_b, ga, gl):
        sa, sl = _sigmoid(ga), _sigmoid(gl)
        return acc * sa, acc * sl, acc * pa_b * sa * (1.0 - sa), acc * pl_b * sl * (1.0 - sl)

    dpa, dpl, dga, dgl = _mm_nt("mix_out_dx", dh1b, wout, [BF16] * 4, merge_bwd, (pa, plr, proj, proj),
                                lambda bm, bn: plain_specs(2)(bm, bn) + seg_specs(SEG_GA, SEG_GL)(bm, bn), bm=512)
    g_wout = _mm_tn("mix_out_dw", merged, dh1b)
    datt = _mm_nt("proj_attn_dx", dpa, wpa, [F32])[0]
    g_wpa = _mm_tn("proj_attn_dw", att, dpa)

    def lru_out_bwd(acc, h_b, gate):
        return acc * _gelu(gate), acc * h_b * _gelu_grad(gate)

    g_wpl = _mm_tn("proj_lru_dw", ylru, dpl)
    shp_mix = [w.shape for w in big[1:4]]
    (dh, dxg), sib_mix = _mm_nt("proj_lru_dx", dpl, wpl, [F32, BF16], lru_out_bwd, (h2d, proj),
                                lambda bm, bn: plain_specs(1)(bm, bn) + seg_specs(SEG_GATE)(bm, bn), bm=512,
                                side=_sibling_side([g_wpa, g_wpl, g_wout], kinds[1:4], shp_mix))
    red_pa, red_pl, red_out = reduce_group("mix", kinds[1:4], shp_mix, [g_wpa, g_wpl, g_wout], sib_mix, 3)
    dq, dk, dv = _attn_bwd(proj, d, datt, att, lse, slopes)
    g3, da3 = _scan_bwd(a3, hp3, dh)
    dxc, dwa, dwx, dvec = _gates_bwd(g3, da3, xc, wa16, wx16, vec8)
    dxr, dconv = _conv_bwd(dxc, proj, cw8)

    def small_step(tag, grads, ws, ms, vs, like, after):
        n_rows = sum(g.size for g in grads) // LANES
        per_dev = -(-n_rows // (N_DEV * SUBLANES)) * SUBLANES
        packed = lax.optimization_barrier((_pack_rows(grads, N_DEV * per_dev), after))[0]
        total = _all_reduce_small(f"all_reduce_{tag}", packed)
        w_rows = -(-(sum(w.size for w in ws) // LANES) // SUBLANES) * SUBLANES
        upd = _adamw_small(f"adamw_{tag}", total[:w_rows], _pack_rows(ws, w_rows), _pack_rows(ms, w_rows), _pack_rows(vs, w_rows))
        return _unpack_rows(total, like), [_unpack_rows(t, ws) for t in upd]

    early_w = [conv_b, lru_wa, lru_ba, lru_wx, lru_bx, lru_lambda, norm_mlp_g, norm_final_g]
    early_m = [m_conv_b, m_lru_wa, m_lru_ba, m_lru_wx, m_lru_bx, m_lru_lambda, m_norm_mlp_g, m_norm_final_g]
    early_v = [v_conv_b, v_lru_wa, v_lru_ba, v_lru_wx, v_lru_bx, v_lru_lambda, v_norm_mlp_g, v_norm_final_g]
    early_g = [dconv[CONV_TAPS:CONV_TAPS + 1], dwa, dvec[VEC_BA:VEC_BA + 1], dwx, dvec[VEC_BX:VEC_BX + 1],
               dvec[VEC_LAM:VEC_LAM + 1], dg2, dg3, dconv[0:CONV_TAPS]]
    dq = lax.optimization_barrier((dq, red_up[1], red_down[1]))[0]
    early_sum, early_upd = small_step("small", early_g, early_w, early_m, early_v,
                                      early_w + [jax.ShapeDtypeStruct((1, CONV_TAPS, d), F32)], dq)
    g_cw_full = early_sum[-1]
    cshard = conv_w.shape[2]
    g_cw = lax.dynamic_slice(g_cw_full, (0, 0, dev * cshard), (1, CONV_TAPS, cshard))
    cw_delta, cw_m, cw_v = (t[:CONV_TAPS][None] for t in _adamw_small(
        "adamw_conv_w", pad_taps(g_cw[0]), pad_taps(conv_w[0]), pad_taps(m_conv_w[0]), pad_taps(v_conv_w[0])))
    dq = lax.optimization_barrier((dq, early_sum))[0]

    dproj = jnp.concatenate([dq, dk, dv, dxr, dxg, dga, dgl], axis=1)
    dproj = lax.optimization_barrier((dproj, red_pa[1], red_pl[1], red_out[1]))[0]
    half = (big[0].shape[0] // 2, big[0].shape[1])
    g_in0 = _mm_tn("proj_in_dw_0", xn, dproj, part=(0, 2))
    g_in1, sib_in0 = _mm_tn("proj_in_dw_1", xn, dproj, part=(1, 2), side=_sibling_side([g_in0], ["col"], [half]))
    red_in = reduce_group("in_0", ["col"], [half], [g_in0], sib_in0, 4)
    dproj = lax.optimization_barrier((dproj, red_in[0][0]))[0]
    (dxn0,), sib_in1 = _mm_nt("proj_in_dx_0", dproj, win, [F32], part=(0, 2), side=_sibling_side([g_in1], ["col"], [half]))
    red_in += reduce_group("in_1", ["col"], [half], [g_in1], sib_in1, 6)
    dproj = lax.optimization_barrier((dproj, red_in[1][0]))[0]
    dxn1 = _mm_nt("proj_in_dx_1", dproj, win, [F32], part=(1, 2))[0]
    dxn = jnp.concatenate([dxn0, dxn1], axis=0)
    dxn = lax.optimization_barrier((dxn, red_in[0][1]))[0]
    grad_x, _, dg1 = _rms_bwd("norm_mix_bwd", xs, norm_mix_g, dxn, dh1)
    red_up, red_down = lax.optimization_barrier(((red_up, red_down), dg1))[0]
    big_out = {i: _adamw_shard(f"adamw_{i}", [red], big[i], big_m[i], big_v[i], chip) for i, red in ((4, red_up), (5, red_down))}
    big_out.update({i: _adamw_shard(f"adamw_{i}", [red], big[i], big_m[i], big_v[i], chip) for i, red in ((1, red_pa), (2, red_pl), (3, red_out))})
    late_sum, late_upd = small_step("norm_mix", [dg1], [norm_mix_g], [m_norm_mix_g], [v_norm_mix_g], [norm_mix_g], big_out[3])
    big_out[0] = _adamw_shard("adamw_0", red_in, big[0], big_m[0], big_v[0], chip)
    s_grad = late_sum + early_sum[:-1]
    s_delta, s_m, s_v = (late_upd[j] + early_upd[j] for j in range(3))


    names = ["norm_mix_g", "w_in", "conv_w", "conv_b", "lru_wa", "lru_ba", "lru_wx", "lru_bx", "lru_lambda", "w_proj_attn", "w_proj_lru",
             "w_out", "norm_mlp_g", "w_up", "w_down", "norm_final_g"]
    small_names = ["norm_mix_g", "conv_b", "lru_wa", "lru_ba", "lru_wx", "lru_bx", "lru_lambda", "norm_mlp_g", "norm_final_g"]
    big_names = ["w_in", "w_proj_attn", "w_proj_lru", "w_out", "w_up", "w_down"]
    res = {"conv_w": (g_cw, cw_delta, cw_m, cw_v)}
    for i, nm in enumerate(small_names):
        res[nm] = (s_grad[i], s_delta[i], s_m[i], s_v[i])
    for i, nm in enumerate(big_names):
        res[nm] = tuple(t[None] for t in big_out[i])
    return (loss, grad_x[None], *[res[nm][0] for nm in names], *[res[nm][1] for nm in names],
            *[res[nm][2] for nm in names], *[res[nm][3] for nm in names])
```

```python
import jax
import jax.numpy as jnp
from jax import lax
from jax.experimental import pallas as pl
from jax.experimental.pallas import tpu as pltpu
from jax.experimental.pallas import tpu_sc as plsc

F32, BF16 = jnp.float32, jnp.bfloat16
MESH = pl.DeviceIdType.MESH
HBM = pl.BlockSpec(memory_space=pltpu.HBM)
N_DEV = 8
N_CHIP = 4
HEAD = 128
SPAN = 128
DILATIONS = (1, 4, 16)
CONV_TAPS = 4
LRU_C = 8.0
NORM_EPS = 1e-6
LANES = 128
SUBLANES = 8
VMEM_LIMIT = 56 * 1024 * 1024
ADAM_LR, ADAM_B1, ADAM_B2, ADAM_EPS, ADAM_WD, ADAM_STEP = 0.001, 0.9, 0.999, 1e-08, 0.01, 10
ADAM_C1 = 1.0 - ADAM_B1 ** ADAM_STEP
ADAM_C2 = 1.0 - ADAM_B2 ** ADAM_STEP
NEG = -1e30


def _params(sem=None):
    return pltpu.CompilerParams(dimension_semantics=sem, vmem_limit_bytes=VMEM_LIMIT)


def _sigmoid(v):
    return 1.0 / (1.0 + jnp.exp(-v))


def _gelu(v):
    k = 0.7978845608028654
    return 0.5 * v * (1.0 + jnp.tanh(k * (v + 0.044715 * v * v * v)))


def _gelu_grad(v):
    k = 0.7978845608028654
    t = jnp.tanh(k * (v + 0.044715 * v * v * v))
    return 0.5 * (1.0 + t) + 0.5 * v * (1.0 - t * t) * k * (1.0 + 3.0 * 0.044715 * v * v)


NN = (((1,), (0,)), ((), ()))
NT = (((1,), (1,)), ((), ()))
TN = (((0,), (0,)), ((), ()))


def _mm(name, a, a_spec, b, b_spec, dn, grid, out_shapes, out_specs, acc_block, epilogue=None, extras=(), extra_specs=(), side=None):
    nk, ne, no = grid[2], len(extras), len(out_shapes)
    side_ops, side_shapes, side_copies, make_copies = side if side is not None else ((), (), 0, None)
    ns_in, ns_out = len(side_ops), len(side_shapes)

    def body(*refs):
        a_ref, b_ref = refs[0], refs[1]
        ex, side_in = refs[2:2 + ne], refs[2 + ne:2 + ne + ns_in]
        outs = refs[2 + ne + ns_in:2 + ne + ns_in + no]
        side_out = refs[2 + ne + ns_in + no:2 + ne + ns_in + no + ns_out]
        scratch = refs[2 + ne + ns_in + no + ns_out:]
        at = [pl.program_id(ax) for ax in range(3)]
        if side is not None:
            @pl.when((at[0] == 0) & (at[1] == 0) & (at[2] == 0))
            def _():
                for cp in make_copies(side_in, side_out, scratch[-2], scratch[-1]):
                    cp.start()

        part = lax.dot_general(a_ref[...], b_ref[...], dn, preferred_element_type=F32)

        def finish(acc):
            vals = epilogue(acc, *[e[...] for e in ex]) if epilogue is not None else (acc,)
            for o, v in zip(outs, vals):
                o[...] = v.astype(o.dtype)

        if nk == 1:
            finish(part)
        else:
            acc_ref, k = scratch[0], at[2]

            @pl.when(k == 0)
            def _():
                acc_ref[...] = part

            @pl.when(k > 0)
            def _():
                acc_ref[...] += part

            @pl.when(k == nk - 1)
            def _():
                finish(acc_ref[...])

        if side is not None:
            @pl.when((at[0] == grid[0] - 1) & (at[1] == grid[1] - 1) & (at[2] == grid[2] - 1))
            def _():
                for cp in make_copies(side_in, side_out, scratch[-2], scratch[-1]):
                    cp.wait()

    scratch_shapes = [pltpu.VMEM(acc_block, F32)] if nk > 1 else []
    if side is not None:
        scratch_shapes += [pltpu.SemaphoreType.DMA((side_copies,)), pltpu.SemaphoreType.DMA((side_copies,))]
    res = pl.pallas_call(
        body, out_shape=[*out_shapes, *side_shapes], grid=grid, in_specs=[a_spec, b_spec, *extra_specs, *[HBM] * ns_in],
        out_specs=[*out_specs, *[HBM] * ns_out], scratch_shapes=scratch_shapes,
        compiler_params=_params(("arbitrary",) * 3 if side is not None else ("parallel", "parallel", "arbitrary")),
        name=name)(a, b, *extras, *side_ops)
    return res if side is None else (res[:no], res[no:])


def _blk(n, pref):
    return pref if n % pref == 0 else n


def _kblk(k):
    return k if k <= 2048 else next(b for b in (2048, 1024, 512) if k % b == 0)


def _mm_fwd(name, a, w, col0, ncols, out_dtypes, epilogue=None, extras=(), extra_specs_fn=None, seg_out=None, bm=1024, bn=1024):
    m, k = a.shape
    bm, bn = _blk(m, bm), _blk(ncols, bn)
    bk = _kblk(k)
    nk = k // bk
    cb0 = col0 // bn
    grid = (m // bm, ncols // bn, nk)
    a_spec = pl.BlockSpec((bm, bk), lambda i, j, kk: (i, kk))
    b_spec = pl.BlockSpec((bk, bn), lambda i, j, kk: (kk, cb0 + j))
    if seg_out is None:
        shapes = [jax.ShapeDtypeStruct((m, ncols), dt) for dt in out_dtypes]
        specs = [pl.BlockSpec((bm, bn), lambda i, j, kk: (i, j)) for _ in out_dtypes]
    else:
        per = seg_out // bn
        shapes = [jax.ShapeDtypeStruct((ncols // seg_out, m, seg_out), dt) for dt in out_dtypes]
        specs = [pl.BlockSpec((None, bm, bn), lambda i, j, kk: (j // per, i, j % per)) for _ in out_dtypes]
    ex_specs = extra_specs_fn(bm, bn) if extra_specs_fn else ()
    return _mm(name, a, a_spec, w, b_spec, NN, grid, shapes, specs, (bm, bn), epilogue, extras, ex_specs)


def _mm_nt(name, a, w, out_dtypes, epilogue=None, extras=(), extra_specs_fn=None, part=(0, 1), side=None, bm=1024, bn=1024):
    m, k = a.shape
    n = w.shape[0]
    m = m // part[1]
    bm, bn = _blk(m, bm), _blk(n, bn)
    bk = _kblk(k)
    grid = (m // bm, n // bn, k // bk)
    i0 = part[0] * (m // bm)
    a_spec = pl.BlockSpec((bm, bk), lambda i, j, kk: (i0 + i, kk))
    b_spec = pl.BlockSpec((bn, bk), lambda i, j, kk: (j, kk))
    shapes = [jax.ShapeDtypeStruct((m, n), dt) for dt in out_dtypes]
    specs = [pl.BlockSpec((bm, bn), lambda i, j, kk: (i, j)) for _ in out_dtypes]
    ex_specs = extra_specs_fn(bm, bn) if extra_specs_fn else ()
    return _mm(name, a, a_spec, w, b_spec, NT, grid, shapes, specs, (bm, bn), epilogue, extras, ex_specs, side)


def _mm_tn(name, a, b, part=(0, 1), side=None, bm=1024, bn=1024):
    t, m = a.shape
    n = b.shape[1]
    m = m // part[1]
    bm, bn = _blk(m, bm), _blk(n, bn)
    grid = (m // bm, n // bn, 1)
    i0 = part[0] * (m // bm)
    a_spec = pl.BlockSpec((t, bm), lambda i, j, kk: (0, i0 + i))
    b_spec = pl.BlockSpec((t, bn), lambda i, j, kk: (0, j))
    res = _mm(name, a, a_spec, b, b_spec, TN, grid, [jax.ShapeDtypeStruct((m, n), BF16)],
              [pl.BlockSpec((bm, bn), lambda i, j, kk: (i, j))], (bm, bn), side=side)
    return res[0] if side is None else (res[0][0], res[1])


ROWS = 256


def _row_spec(d):
    return pl.BlockSpec((ROWS, d), lambda i: (i, 0))


def _vec_spec(d, rows=1):
    return pl.BlockSpec((rows, d), lambda i: (0, 0))


def _rms_fwd(name, x, g):
    s, d = x.shape

    def body(x_ref, g_ref, o_ref):
        xv = x_ref[...]
        r = lax.rsqrt(jnp.mean(xv * xv, axis=-1, keepdims=True) + NORM_EPS)
        o_ref[...] = (xv * r * g_ref[...]).astype(BF16)

    return pl.pallas_call(body, out_shape=jax.ShapeDtypeStruct((s, d), BF16), grid=(s // ROWS,),
                          in_specs=[_row_spec(d), _vec_spec(d)], out_specs=_row_spec(d),
                          compiler_params=_params(("parallel",)), name=name)(x, g)


def _rms_bwd_math(xv, g, dy):
    r = lax.rsqrt(jnp.mean(xv * xv, axis=-1, keepdims=True) + NORM_EPS)
    n = xv * r
    z = dy * g
    dx = r * (z - n * jnp.mean(z * n, axis=-1, keepdims=True))
    return dx, jnp.sum(dy * n, axis=0, keepdims=True)


def _rms_bwd(name, x, g, dy, resid):
    s, d = x.shape

    def body(x_ref, g_ref, dy_ref, r_ref, dx_ref, dxb_ref, dg_ref):
        dx, dg = _rms_bwd_math(x_ref[...], g_ref[...], dy_ref[...])
        dx = dx + r_ref[...]
        dx_ref[...] = dx
        dxb_ref[...] = dx.astype(BF16)

        @pl.when(pl.program_id(0) == 0)
        def _():
            dg_ref[...] = jnp.zeros_like(dg_ref)

        dg_ref[...] += dg

    return pl.pallas_call(
        body, out_shape=[jax.ShapeDtypeStruct((s, d), F32), jax.ShapeDtypeStruct((s, d), BF16), jax.ShapeDtypeStruct((1, d), F32)],
        grid=(s // ROWS,), in_specs=[_row_spec(d), _vec_spec(d), _row_spec(d), _row_spec(d)],
        out_specs=[_row_spec(d), _row_spec(d), _vec_spec(d)], compiler_params=_params(("arbitrary",)), name=name)(x, g, dy, resid)


def _final_loss(h2, tgt, g):
    s, d = h2.shape

    def body(x_ref, t_ref, g_ref, dx_ref, dxb_ref, dg_ref, ls_ref):
        xv, gv = x_ref[...], g_ref[...]
        r = lax.rsqrt(jnp.mean(xv * xv, axis=-1, keepdims=True) + NORM_EPS)
        diff = xv * r * gv - t_ref[...]
        dx, dg = _rms_bwd_math(xv, gv, diff * (1.0 / d))
        dx_ref[...] = dx
        dxb_ref[...] = dx.astype(BF16)

        @pl.when(pl.program_id(0) == 0)
        def _():
            dg_ref[...] = jnp.zeros_like(dg_ref)
            ls_ref[...] = jnp.zeros_like(ls_ref)

        dg_ref[...] += dg
        ls_ref[...] += jnp.sum(diff * diff, axis=0, keepdims=True)

    return pl.pallas_call(
        body, out_shape=[jax.ShapeDtypeStruct((s, d), F32), jax.ShapeDtypeStruct((s, d), BF16),
                         jax.ShapeDtypeStruct((1, d), F32), jax.ShapeDtypeStruct((1, d), F32)],
        grid=(s // ROWS,), in_specs=[_row_spec(d), _row_spec(d), _vec_spec(d)],
        out_specs=[_row_spec(d), _row_spec(d), _vec_spec(d), _vec_spec(d)],
        compiler_params=_params(("arbitrary",)), name="final_norm_loss")(h2, tgt, g)


ATTN_Q = 128


ATTN_BATCH = 8


def _attn_units(s):
    units = []
    for gi, d in enumerate(DILATIONS):
        for r in range(d):
            for q0 in range(0, s // d, ATTN_Q):
                k0 = max(q0 - SPAN, 0)
                units.append((gi, d, r, q0, k0, q0 + ATTN_Q - k0))
    return units


def _stream_rows(d, r, start, size):
    return pl.ds(r + start * d, size) if d == 1 else pl.ds(r + start * d, size, stride=d)


def _attn_scores(q_ref, k_ref, slope, d, r, q0, k0, nk):
    qrows, krows = _stream_rows(d, r, q0, ATTN_Q), _stream_rows(d, r, k0, nk)
    qb = q_ref[qrows, :].astype(BF16)
    kb = k_ref[krows, :].astype(BF16)
    sc = lax.dot_general(qb, kb, NT, preferred_element_type=F32) * (HEAD ** -0.5)
    qi = lax.broadcasted_iota(jnp.int32, (ATTN_Q, nk), 0)
    kj = lax.broadcasted_iota(jnp.int32, (ATTN_Q, nk), 1)
    dist = (q0 - k0) + qi - kj
    valid = (dist >= 0) & (dist <= SPAN)
    sc = sc - (slope * d) * dist.astype(F32)
    return jnp.where(valid, sc, NEG), valid, qb, kb, qrows, krows


def _attn_fwd(proj, dm, slopes):
    s = proj.shape[0]
    units = _attn_units(s)

    def body(sl_ref, q_ref, k_ref, v_ref, att_ref, lse_ref, *scr):
        o_scr, l_scr = scr[:3], scr[3:]
        slope = sl_ref[pl.program_id(0)]
        for first in range(0, len(units), ATTN_BATCH):
            batch = units[first:first + ATTN_BATCH]
            scored = [_attn_scores(q_ref, k_ref, slope, d, r, q0, k0, nk) for _, d, r, q0, k0, nk in batch]
            soft = []
            for sc, _, _, _, _, _ in scored:
                m = jnp.max(sc, axis=-1, keepdims=True)
                p = jnp.exp(sc - m)
                soft.append((m, p, jnp.sum(p, axis=-1, keepdims=True)))
            outs = [lax.dot_general(p.astype(BF16), v_ref[sco[5], :].astype(BF16), NN, preferred_element_type=F32)
                    for (m, p, l), sco in zip(soft, scored)]
            for (gi, *_), (m, p, l), sco, o in zip(batch, soft, scored, outs):
                o_scr[gi][sco[4], :] = o / l
                l_scr[gi][sco[4], :] = jnp.broadcast_to(m + jnp.log(l), (ATTN_Q, HEAD))
        l0, l1, l2 = l_scr[0][...], l_scr[1][...], l_scr[2][...]
        m = jnp.maximum(jnp.maximum(l0, l1), l2)
        w0, w1, w2 = jnp.exp(l0 - m), jnp.exp(l1 - m), jnp.exp(l2 - m)
        tot = w0 + w1 + w2
        att_ref[...] = ((w0 * o_scr[0][...] + w1 * o_scr[1][...] + w2 * o_scr[2][...]) / tot).astype(BF16)
        lse_ref[...] = m + jnp.log(tot)

    def seg(i):
        return pl.BlockSpec((s, HEAD), lambda h: (0, i * (dm // HEAD) + h))

    col = pl.BlockSpec((s, HEAD), lambda h: (0, h))
    return pl.pallas_call(
        body, out_shape=[jax.ShapeDtypeStruct((s, dm), BF16), jax.ShapeDtypeStruct((s, dm), F32)], grid=(dm // HEAD,),
        in_specs=[pl.BlockSpec(memory_space=pltpu.SMEM), seg(0), seg(1), seg(2)], out_specs=[col, col],
        scratch_shapes=[pltpu.VMEM((s, HEAD), F32)] * (2 * len(DILATIONS)),
        compiler_params=_params(("parallel",)), name="attn_fwd")(slopes, proj, proj, proj)


def _attn_bwd(proj, dm, datt, att, lse, slopes):
    s = proj.shape[0]
    units = _attn_units(s)

    def body(sl_ref, q_ref, k_ref, v_ref, do_ref, att_ref, lse_ref, dq_ref, dk_ref, dv_ref, dq_scr, dk_scr, dv_scr, dl_scr):
        slope = sl_ref[pl.program_id(0)]
        delta = jnp.sum(do_ref[...] * att_ref[...].astype(F32), axis=-1, keepdims=True)
        dl_scr[...] = jnp.broadcast_to(delta, (s, HEAD))
        dq_scr[...] = jnp.zeros_like(dq_scr)
        dk_scr[...] = jnp.zeros_like(dk_scr)
        dv_scr[...] = jnp.zeros_like(dv_scr)
        for first in range(0, len(units), ATTN_BATCH):
            scored = [_attn_scores(q_ref, k_ref, slope, d, r, q0, k0, nk) for _, d, r, q0, k0, nk in units[first:first + ATTN_BATCH]]
            dobs = [do_ref[sco[4], :].astype(BF16) for sco in scored]
            dps = [lax.dot_general(dob, v_ref[sco[5], :].astype(BF16), NT, preferred_element_type=F32) for dob, sco in zip(dobs, scored)]
            ps = [jnp.where(sco[1], jnp.exp(sco[0] - lse_ref[sco[4], :][:, 0:1]), 0.0) for sco in scored]
            dss = [(p * (dp - dl_scr[sco[4], :][:, 0:1]) * (HEAD ** -0.5)).astype(BF16) for p, dp, sco in zip(ps, dps, scored)]
            dqs = [lax.dot_general(ds, sco[3], NN, preferred_element_type=F32) for ds, sco in zip(dss, scored)]
            dks = [lax.dot_general(ds, sco[2], TN, preferred_element_type=F32) for ds, sco in zip(dss, scored)]
            dvs = [lax.dot_general(p.astype(BF16), dob, TN, preferred_element_type=F32) for p, dob in zip(ps, dobs)]
            for sco, dq, dk, dv in zip(scored, dqs, dks, dvs):
                dq_scr[sco[4], :] += dq
                dk_scr[sco[5], :] += dk
                dv_scr[sco[5], :] += dv
        dq_ref[...] = dq_scr[...].astype(BF16)
        dk_ref[...] = dk_scr[...].astype(BF16)
        dv_ref[...] = dv_scr[...].astype(BF16)

    def seg(i):
        return pl.BlockSpec((s, HEAD), lambda h: (0, i * (dm // HEAD) + h))

    col = pl.BlockSpec((s, HEAD), lambda h: (0, h))
    return pl.pallas_call(
        body, out_shape=[jax.ShapeDtypeStruct((s, dm), BF16)] * 3, grid=(dm // HEAD,),
        in_specs=[pl.BlockSpec(memory_space=pltpu.SMEM), seg(0), seg(1), seg(2), col, col, col], out_specs=[col, col, col],
        scratch_shapes=[pltpu.VMEM((s, HEAD), F32)] * 4,
        compiler_params=_params(("parallel",)), name="attn_bwd")(slopes, proj, proj, proj, datt, att, lse)


VEC_CB, VEC_BA, VEC_BX, VEC_LAM = 0, 1, 2, 3
SEG_Q, SEG_K, SEG_V, SEG_X, SEG_GATE, SEG_GA, SEG_GL = range(7)


def _to_3d(ref3, val):
    lw = val.shape[1] // SUBLANES
    for j in range(SUBLANES):
        ref3[:, j, :] = val[:, j * lw:(j + 1) * lw]


def _from_3d(ref3):
    return jnp.concatenate([ref3[:, j, :] for j in range(SUBLANES)], axis=1)


def _softplus(z):
    return jnp.maximum(z, 0.0) + jnp.log1p(jnp.exp(-jnp.abs(z)))


def _gate_math(xc, wa_ref, wx_ref, vec):
    xcb = xc.astype(BF16)
    nh = xc.shape[1] // HEAD
    pre_a = jnp.concatenate([jnp.dot(xcb[:, h * HEAD:(h + 1) * HEAD], wa_ref[h], preferred_element_type=F32) for h in range(nh)], axis=1)
    pre_x = jnp.concatenate([jnp.dot(xcb[:, h * HEAD:(h + 1) * HEAD], wx_ref[h], preferred_element_type=F32) for h in range(nh)], axis=1)
    ra = _sigmoid(pre_a + vec[VEC_BA:VEC_BA + 1])
    ig = _sigmoid(pre_x + vec[VEC_BX:VEC_BX + 1])
    sp = _softplus(-vec[VEC_LAM:VEC_LAM + 1])
    log_a = -LRU_C * ra * sp
    a = jnp.exp(log_a)
    z = 2.0 * log_a
    one_minus_a2 = jnp.where(z > -0.01, -z * (1.0 + z * (0.5 + z * (1.0 / 6.0))), 1.0 - jnp.exp(z))
    mult = jnp.sqrt(one_minus_a2)
    return dict(xcb=xcb, ra=ra, ig=ig, sp=sp, a=a, mult=mult)


def _conv_pad_prev(pad_ref, cur, halo, first):
    pad_ref[0:SUBLANES, :] = jnp.where(first, 0.0, halo)
    pad_ref[SUBLANES:SUBLANES + cur.shape[0], :] = cur


def _gates_fwd(proj, d, cw8, vec8, wa, wx):
    s = proj.shape[0]
    lw = d // SUBLANES
    hb = ROWS // SUBLANES

    def body(x_ref, halo_ref, cw_ref, vec_ref, wa_ref, wx_ref, a_ref, u_ref, xc_ref, pad):
        _conv_pad_prev(pad, x_ref[...], halo_ref[...], pl.program_id(0) == 0)
        vec = vec_ref[...]
        xc = vec[VEC_CB:VEC_CB + 1]
        for k in range(CONV_TAPS):
            xc = xc + cw_ref[k:k + 1, :] * pad[pl.ds(SUBLANES - (CONV_TAPS - 1) + k, ROWS), :]
        gm = _gate_math(xc, wa_ref, wx_ref, vec)
        xc_ref[...] = xc
        _to_3d(a_ref, gm["a"])
        _to_3d(u_ref, gm["mult"] * (gm["ig"] * xc))

    spec3 = pl.BlockSpec((ROWS, SUBLANES, lw), lambda i: (i, 0, 0))
    wspec = pl.BlockSpec(wa.shape, lambda i: (0, 0, 0))
    return pl.pallas_call(
        body, out_shape=[jax.ShapeDtypeStruct((s, SUBLANES, lw), F32)] * 2 + [jax.ShapeDtypeStruct((s, d), F32)], grid=(s // ROWS,),
        in_specs=[pl.BlockSpec((ROWS, d), lambda i: (i, SEG_X)),
                  pl.BlockSpec((SUBLANES, d), lambda i: (jnp.maximum(i * hb - 1, 0), SEG_X)),
                  _vec_spec(d, SUBLANES), _vec_spec(d, SUBLANES), wspec, wspec],
        out_specs=[spec3, spec3, _row_spec(d)], scratch_shapes=[pltpu.VMEM((ROWS + SUBLANES, d), F32)],
        compiler_params=_params(("parallel",)), name="lru_gates_fwd")(proj, proj, cw8, vec8, wa, wx)


def _scan_fwd(a3, u3):
    s, _, lw = a3.shape

    def body(a_ref, u_ref, h_ref, hp_ref, carry):
        @pl.when(pl.program_id(0) == 0)
        def _():
            carry[...] = jnp.zeros_like(carry)

        def step(t, h):
            hp_ref[t] = h
            hn = a_ref[t] * h + u_ref[t]
            h_ref[t] = hn
            return hn

        carry[...] = lax.fori_loop(0, ROWS, step, carry[...], unroll=8)

    spec3 = pl.BlockSpec((ROWS, SUBLANES, lw), lambda i: (i, 0, 0))
    return pl.pallas_call(body, out_shape=[jax.ShapeDtypeStruct(a3.shape, F32)] * 2, grid=(s // ROWS,), in_specs=[spec3, spec3],
                          out_specs=[spec3, spec3], scratch_shapes=[pltpu.VMEM((SUBLANES, lw), F32)],
                          compiler_params=_params(("arbitrary",)), name="lru_scan_fwd")(a3, u3)


def _lru_out(h3, proj):
    s, _, lw = h3.shape
    d = lw * SUBLANES

    def body(h_ref, g_ref, y_ref, h2_ref):
        h = _from_3d(h_ref)
        h2_ref[...] = h
        y_ref[...] = (h * _gelu(g_ref[...])).astype(BF16)

    return pl.pallas_call(
        body, out_shape=[jax.ShapeDtypeStruct((s, d), BF16), jax.ShapeDtypeStruct((s, d), F32)], grid=(s // ROWS,),
        in_specs=[pl.BlockSpec((ROWS, SUBLANES, lw), lambda i: (i, 0, 0)), pl.BlockSpec((ROWS, d), lambda i: (i, SEG_GATE))],
        out_specs=[_row_spec(d), _row_spec(d)], compiler_params=_params(("parallel",)), name="lru_out")(h3, proj)


def _scan_bwd(a3, hp3, dh):
    s, _, lw = a3.shape
    d = lw * SUBLANES
    nb = s // ROWS

    def body(a_ref, hp_ref, dh_ref, g_ref, da_ref, dh3, carry):
        @pl.when(pl.program_id(0) == 0)
        def _():
            carry[...] = jnp.zeros_like(carry)

        _to_3d(dh3, dh_ref[...])

        def step(j, c):
            t = ROWS - 1 - j
            g = dh3[t] + c
            g_ref[t] = g
            da_ref[t] = g * hp_ref[t]
            return a_ref[t] * g

        carry[...] = lax.fori_loop(0, ROWS, step, carry[...], unroll=8)

    spec3 = pl.BlockSpec((ROWS, SUBLANES, lw), lambda i: (nb - 1 - i, 0, 0))
    return pl.pallas_call(
        body, out_shape=[jax.ShapeDtypeStruct(a3.shape, F32)] * 2, grid=(nb,),
        in_specs=[spec3, spec3, pl.BlockSpec((ROWS, d), lambda i: (nb - 1 - i, 0))], out_specs=[spec3, spec3],
        scratch_shapes=[pltpu.VMEM((ROWS, SUBLANES, lw), F32), pltpu.VMEM((SUBLANES, lw), F32)],
        compiler_params=_params(("arbitrary",)), name="lru_scan_bwd")(a3, hp3, dh)


def _gates_bwd(g3, da3, xc, wa, wx, vec8):
    s, d = xc.shape
    lw = d // SUBLANES
    nh = d // HEAD

    def body(g_ref, da_ref, xc_ref, wa_ref, wx_ref, vec_ref, dxc_ref, dwa_ref, dwx_ref, dvec_ref):
        @pl.when(pl.program_id(0) == 0)
        def _():
            dwa_ref[...] = jnp.zeros_like(dwa_ref)
            dwx_ref[...] = jnp.zeros_like(dwx_ref)
            dvec_ref[...] = jnp.zeros_like(dvec_ref)

        xc_v, vec = xc_ref[...], vec_ref[...]
        du, da = _from_3d(g_ref), _from_3d(da_ref)
        gm = _gate_math(xc_v, wa_ref, wx_ref, vec)
        ra, ig, sp, a, mult = gm["ra"], gm["ig"], gm["sp"], gm["a"], gm["mult"]
        dmult = du * ig * xc_v
        dlog_a = da * a - dmult * (a * a) / mult
        dpre_a = dlog_a * (-LRU_C * sp) * ra * (1.0 - ra)
        dpre_x = du * mult * xc_v * ig * (1.0 - ig)
        dlam = jnp.sum(dlog_a * (-LRU_C * ra), axis=0, keepdims=True) * (-_sigmoid(-vec[VEC_LAM:VEC_LAM + 1]))
        dvec_ref[VEC_BA:VEC_BA + 1, :] += jnp.sum(dpre_a, axis=0, keepdims=True)
        dvec_ref[VEC_BX:VEC_BX + 1, :] += jnp.sum(dpre_x, axis=0, keepdims=True)
        dvec_ref[VEC_LAM:VEC_LAM + 1, :] += dlam
        dab, dxb, xcb = dpre_a.astype(BF16), dpre_x.astype(BF16), gm["xcb"]
        back = []
        for h in range(nh):
            cols = slice(h * HEAD, (h + 1) * HEAD)
            dwa_ref[h] += lax.dot_general(xcb[:, cols], dab[:, cols], TN, preferred_element_type=F32)
            dwx_ref[h] += lax.dot_general(xcb[:, cols], dxb[:, cols], TN, preferred_element_type=F32)
            back.append(lax.dot_general(dab[:, cols], wa_ref[h], NT, preferred_element_type=F32)
                        + lax.dot_general(dxb[:, cols], wx_ref[h], NT, preferred_element_type=F32))
        dxc_ref[...] = du * mult * ig + jnp.concatenate(back, axis=1)

    spec3 = pl.BlockSpec((ROWS, SUBLANES, lw), lambda i: (i, 0, 0))
    wspec = pl.BlockSpec(wa.shape, lambda i: (0, 0, 0))
    return pl.pallas_call(
        body, out_shape=[jax.ShapeDtypeStruct((s, d), F32), jax.ShapeDtypeStruct(wa.shape, F32), jax.ShapeDtypeStruct(wa.shape, F32),
                         jax.ShapeDtypeStruct((SUBLANES, d), F32)],
        grid=(s // ROWS,), in_specs=[spec3, spec3, _row_spec(d), wspec, wspec, _vec_spec(d, SUBLANES)],
        out_specs=[_row_spec(d), wspec, wspec, _vec_spec(d, SUBLANES)],
        compiler_params=_params(("arbitrary",)), name="lru_gates_bwd")(g3, da3, xc, wa, wx, vec8)


def _conv_bwd(dxc, proj, cw8):
    s, d = dxc.shape
    hb = ROWS // SUBLANES
    last = s // SUBLANES - 1

    def body(dc_ref, dnext_ref, x_ref, xprev_ref, cw_ref, dx_ref, dcw_ref, padd, padx):
        i = pl.program_id(0)

        @pl.when(i == 0)
        def _():
            dcw_ref[...] = jnp.zeros_like(dcw_ref)

        dc = dc_ref[...]
        padd[0:ROWS, :] = dc
        padd[ROWS:ROWS + SUBLANES, :] = jnp.where(i == pl.num_programs(0) - 1, 0.0, dnext_ref[...])
        _conv_pad_prev(padx, x_ref[...], xprev_ref[...], i == 0)
        dx = jnp.zeros_like(dc)
        for k in range(CONV_TAPS):
            dx = dx + cw_ref[k:k + 1, :] * padd[pl.ds(CONV_TAPS - 1 - k, ROWS), :]
            dcw_ref[k:k + 1, :] += jnp.sum(dc * padx[pl.ds(SUBLANES - (CONV_TAPS - 1) + k, ROWS), :], axis=0, keepdims=True)
        dcw_ref[CONV_TAPS:CONV_TAPS + 1, :] += jnp.sum(dc, axis=0, keepdims=True)
        dx_ref[...] = dx.astype(BF16)

    return pl.pallas_call(
        body, out_shape=[jax.ShapeDtypeStruct((s, d), BF16), jax.ShapeDtypeStruct((SUBLANES, d), F32)], grid=(s // ROWS,),
        in_specs=[_row_spec(d), pl.BlockSpec((SUBLANES, d), lambda i: (jnp.minimum((i + 1) * hb, last), 0)),
                  pl.BlockSpec((ROWS, d), lambda i: (i, SEG_X)),
                  pl.BlockSpec((SUBLANES, d), lambda i: (jnp.maximum(i * hb - 1, 0), SEG_X)), _vec_spec(d, SUBLANES)],
        out_specs=[_row_spec(d), _vec_spec(d, SUBLANES)],
        scratch_shapes=[pltpu.VMEM((ROWS + SUBLANES, d), F32), pltpu.VMEM((ROWS + SUBLANES, d), F32)],
        compiler_params=_params(("arbitrary",)), name="lru_conv_bwd")(dxc, dxc, proj, proj, cw8)


def _coords():
    return lax.axis_index("x"), lax.axis_index("y"), lax.axis_index("c")


def _other_chips(x, y):
    return [(1 - x, y), (x, 1 - y), (1 - x, 1 - y)]


def _slab(ref, kind, shard_shape, idx, half=None):
    r, c = shard_shape
    r0, nr = (0, r) if half is None else (half * (r // 2), r // 2)
    if kind == "col":
        return ref.at[pl.ds(r0, nr), pl.ds(pl.multiple_of(idx * c, LANES), c)]
    if kind == "row":
        return ref.at[pl.ds(pl.multiple_of(idx * r, SUBLANES) + r0, nr), :]
    return ref.at[idx, pl.ds(r0, nr), :]


def _full_shape(shard_shape, kind):
    r, c = shard_shape
    return {"col": (r, c * N_DEV), "row": (r * N_DEV, c), "slot": (N_DEV, r, c)}[kind]


def _handshake(peers):
    barrier = pltpu.get_barrier_semaphore()
    for peer in peers:
        pl.semaphore_signal(barrier, inc=1, device_id=peer, device_id_type=MESH)
    pl.semaphore_wait(barrier, len(peers))


def _launch(name, body, out_shape, operands, sems, sequencer_id):
    if sequencer_id is None:
        return pl.pallas_call(body, out_shape=out_shape, in_specs=[HBM] * len(operands), out_specs=[HBM] * len(out_shape),
                              scratch_shapes=sems, name=name)(*operands)
    return pl.kernel(body, out_type=out_shape, mesh=plsc.ScalarSubcoreMesh(axis_name="seq", num_cores=1), name=name,
                     scratch_types=sems, compiler_params=pltpu.CompilerParams(collective_id=sequencer_id))(*operands)


AG_COPIES = 10


def _all_gather(name, shards, kinds, sequencer_id=None):
    n = len(shards)
    shapes = [s.shape for s in shards]

    def body(*refs):
        ins, outs = refs[:n], refs[n:2 * n]
        send_sems, recv_sems, local_sems = refs[2 * n:]
        x, y, c = _coords()
        me, sib, xn, yn, dg = (x, y, c), (x, y, 1 - c), (1 - x, y, c), (x, 1 - y, c), (1 - x, 1 - y, c)
        if sequencer_id is not None:
            _handshake([sib, xn, yn])

        def part(i, dev, half=None):
            return _slab(outs[i], kinds[i], shapes[i], 4 * dev[0] + 2 * dev[1] + dev[2], half)

        def copy(i, k, block, half, to, own=False):
            r = shapes[i][0]
            src = part(i, block, half) if not own else (ins[i] if half is None else ins[i].at[pl.ds(half * (r // 2), r // 2), :])
            return pltpu.make_async_remote_copy(
                src_ref=src, dst_ref=part(i, block, half), send_sem=send_sems.at[AG_COPIES * i + k],
                recv_sem=recv_sems.at[AG_COPIES * i + k], device_id=to, device_id_type=MESH)

        def other_core(dev):
            return (dev[0], dev[1], 1 - c)

        started = []

        def start(cp):
            cp.start()
            started.append(cp)

        for i in range(n):
            start(copy(i, 1, me, 0, xn, own=True))
            start(copy(i, 4, me, 1, yn, own=True))
            start(copy(i, 2, me, 1, xn, own=True))
            start(copy(i, 3, me, 0, yn, own=True))
            start(copy(i, 0, me, None, sib, own=True))
        mine = [pltpu.make_async_copy(ins[i], part(i, me), local_sems.at[i]) for i in range(n)]
        for cp in mine:
            cp.start()
        for i in range(n):
            copy(i, 1, xn, 0, me).wait_recv()
            start(copy(i, 5, xn, 0, yn))
            copy(i, 4, yn, 1, me).wait_recv()
            start(copy(i, 6, yn, 1, xn))
        for i in range(n):
            copy(i, 2, xn, 1, me).wait_recv()
            start(copy(i, 7, xn, None, sib))
            copy(i, 3, yn, 0, me).wait_recv()
            start(copy(i, 8, yn, None, sib))
        for i in range(n):
            copy(i, 5, dg, 0, me).wait_recv()
            copy(i, 6, dg, 1, me).wait_recv()
            start(copy(i, 9, dg, None, sib))
        for i in range(n):
            copy(i, 0, sib, None, me).wait_recv()
            for k, dev in ((7, xn), (8, yn), (9, dg)):
                copy(i, k, other_core(dev), None, me).wait_recv()
        for cp in started:
            cp.wait_send()
        for cp in mine:
            cp.wait()

    out_shape = [jax.ShapeDtypeStruct(_full_shape(s.shape, k), s.dtype) for s, k in zip(shards, kinds)]
    sems = [pltpu.SemaphoreType.DMA((AG_COPIES * n,)), pltpu.SemaphoreType.DMA((AG_COPIES * n,)), pltpu.SemaphoreType.DMA((n,))]
    return _launch(name, body, out_shape, shards, sems, sequencer_id)


def _sibling_copies(kinds, shard_shapes):
    def make(ins, outs, send_sems, recv_sems):
        x, y, c = _coords()
        return [pltpu.make_async_remote_copy(
            src_ref=_slab(ins[i], kinds[i], shard_shapes[i], 2 * q + (1 - c)), dst_ref=outs[i].at[q],
            send_sem=send_sems.at[N_CHIP * i + q], recv_sem=recv_sems.at[N_CHIP * i + q],
            device_id=(x, y, 1 - c), device_id_type=MESH) for i in range(len(ins)) for q in range(N_CHIP)]
    return make


def _sibling_side(partials, kinds, shard_shapes):
    return (partials, [jax.ShapeDtypeStruct((N_CHIP, *s), BF16) for s in shard_shapes], N_CHIP * len(partials),
            _sibling_copies(kinds, shard_shapes))


def _exchange_siblings(name, partials, kinds, shard_shapes, sequencer_id=None):
    n = len(partials)
    make = _sibling_copies(kinds, shard_shapes)

    def body(*refs):
        if sequencer_id is not None:
            x, y, c = _coords()
            _handshake([(x, y, 1 - c)])
        cps = make(refs[:n], refs[n:2 * n], refs[2 * n], refs[2 * n + 1])
        for cp in cps:
            cp.start()
        for cp in cps:
            cp.wait()

    return _launch(name, body, [jax.ShapeDtypeStruct((N_CHIP, *s), BF16) for s in shard_shapes], partials,
                   [pltpu.SemaphoreType.DMA((N_CHIP * n,)), pltpu.SemaphoreType.DMA((N_CHIP * n,))], sequencer_id)


def _exchange_chips(name, chip_sums, sequencer_id=None):
    n = len(chip_sums)

    def body(*refs):
        ins, outs = refs[:n], refs[n:2 * n]
        send_sems, recv_sems = refs[2 * n:]
        x, y, c = _coords()
        if sequencer_id is not None:
            _handshake([(cx, cy, c) for cx, cy in _other_chips(x, y)])
        cps = []
        for i in range(n):
            for k, (cx, cy) in enumerate(_other_chips(x, y)):
                cps.append(pltpu.make_async_remote_copy(
                    src_ref=ins[i].at[2 * cx + cy], dst_ref=outs[i].at[k], send_sem=send_sems.at[3 * i + k],
                    recv_sem=recv_sems.at[3 * i + k], device_id=(cx, cy, c), device_id_type=MESH))
        for cp in cps:
            cp.start()
        for cp in cps:
            cp.wait()

    return _launch(name, body, [jax.ShapeDtypeStruct((3, *t.shape[1:]), BF16) for t in chip_sums], chip_sums,
                   [pltpu.SemaphoreType.DMA((3 * n,)), pltpu.SemaphoreType.DMA((3 * n,))], sequencer_id)


def _all_reduce_small(name, packed):
    rows = packed.shape[0] // N_DEV

    def body(p_ref, out_ref, rb, tot, send_sems, recv_sems):
        x, y, c = _coords()
        me = 4 * x + 2 * y + c

        def peer(k):
            return (x ^ (k >> 2), y ^ ((k >> 1) & 1), c ^ (k & 1))

        def rows_of(idx):
            return pl.ds(pl.multiple_of(idx * rows, SUBLANES), rows)

        def piece(ref, idx):
            return ref.at[rows_of(idx), :]

        scatter = [pltpu.make_async_remote_copy(src_ref=piece(p_ref, me ^ k), dst_ref=rb.at[k], send_sem=send_sems.at[k],
                                                recv_sem=recv_sems.at[k], device_id=peer(k), device_id_type=MESH) for k in range(1, N_DEV)]
        for cp in scatter:
            cp.start()
        acc = p_ref[rows_of(me), :]
        for cp in scatter:
            cp.wait_recv()
        for k in range(1, N_DEV):
            acc = acc + rb[k]
        tot[...] = acc
        out_ref[rows_of(me), :] = acc
        gather = [pltpu.make_async_remote_copy(src_ref=tot, dst_ref=piece(out_ref, me), send_sem=send_sems.at[N_DEV + k],
                                               recv_sem=recv_sems.at[N_DEV + k], device_id=peer(k), device_id_type=MESH)
                  for k in range(1, N_DEV)]
        for cp in gather:
            cp.start()
        for k in range(1, N_DEV):
            pltpu.make_async_remote_copy(src_ref=tot, dst_ref=piece(out_ref, me ^ k), send_sem=send_sems.at[N_DEV + k],
                                         recv_sem=recv_sems.at[N_DEV + k], device_id=peer(k), device_id_type=MESH).wait_recv()
        for cp in scatter + gather:
            cp.wait_send()

    vm = pl.BlockSpec(memory_space=pltpu.VMEM)
    return pl.pallas_call(
        body, out_shape=jax.ShapeDtypeStruct(packed.shape, F32), in_specs=[vm], out_specs=vm,
        scratch_shapes=[pltpu.VMEM((N_DEV, rows, LANES), F32), pltpu.VMEM((rows, LANES), F32),
                        pltpu.SemaphoreType.DMA((2 * N_DEV,)), pltpu.SemaphoreType.DMA((2 * N_DEV,))],
        compiler_params=pltpu.CompilerParams(vmem_limit_bytes=VMEM_LIMIT), name=name)(packed)


def _adamw_math(g, w, m, v):
    m = ADAM_B1 * m + (1.0 - ADAM_B1) * g
    v = ADAM_B2 * v + (1.0 - ADAM_B2) * (g * g)
    delta = -ADAM_LR * ((m / ADAM_C1) / (jnp.sqrt(v / ADAM_C2) + ADAM_EPS) + ADAM_WD * w)
    return delta, m, v


def _slab_spec(kind, shard_shape, tr, slab_of):
    r, c = shard_shape
    if kind == "col":
        return pl.BlockSpec((tr, c), lambda q, i, sc: (i, slab_of(q, sc)))
    return pl.BlockSpec((tr, c), lambda q, i, sc: (slab_of(q, sc) * (r // tr) + i, 0))


def _chip_sum(name, partial, recv, kind, shard_shape, core):
    r, c = shard_shape
    tr = _blk(r, 1024)

    def body(core_ref, p_ref, r_ref, o_ref):
        o_ref[...] = (p_ref[...].astype(F32) + r_ref[...].astype(F32)).astype(BF16)

    spec4 = pl.BlockSpec((None, tr, c), lambda q, i, sc: (q, i, 0))
    grid_spec = pltpu.PrefetchScalarGridSpec(
        num_scalar_prefetch=1, grid=(N_CHIP, r // tr),
        in_specs=[_slab_spec(kind, shard_shape, tr, lambda q, sc: 2 * q + sc[0]), spec4], out_specs=spec4)
    return pl.pallas_call(body, out_shape=jax.ShapeDtypeStruct((N_CHIP, r, c), BF16), grid_spec=grid_spec,
                          compiler_params=_params(("parallel", "parallel")), name=name)(core, partial, recv)


def _adamw_shard(name, parts, w, m, v, chip):
    r, c = w.shape
    n_parts = len(parts)
    tr = _blk(r // n_parts, 256)
    per = r // n_parts // tr

    def body(chip_ref, *refs):
        src, (w_ref, m_ref, v_ref), (g_out, d_out, m_out, v_out) = refs[:2 * n_parts], refs[2 * n_parts:2 * n_parts + 3], refs[2 * n_parts + 3:]
        for p in range(n_parts):
            @pl.when(pl.program_id(0) // per == p)
            def _():
                g = src[2 * p][...].astype(F32)
                for k in range(3):
                    g = g + src[2 * p + 1][k].astype(F32)
                g_out[...] = g
                d_out[...], m_out[...], v_out[...] = _adamw_math(g, w_ref[...], m_ref[...], v_ref[...])

    def part_specs(p):
        at = lambda i: jnp.clip(i - p * per, 0, per - 1)
        return [pl.BlockSpec((None, tr, c), lambda i, sc: (sc[0], at(i), 0)), pl.BlockSpec((3, tr, c), lambda i, sc: (0, at(i), 0))]

    blk = pl.BlockSpec((tr, c), lambda i, sc: (i, 0))
    grid_spec = pltpu.PrefetchScalarGridSpec(
        num_scalar_prefetch=1, grid=(r // tr,), in_specs=[s for p in range(n_parts) for s in part_specs(p)] + [blk, blk, blk], out_specs=[blk] * 4)
    return pl.pallas_call(body, out_shape=[jax.ShapeDtypeStruct((r, c), F32)] * 4, grid_spec=grid_spec,
                          compiler_params=_params(("parallel",)), name=name)(chip, *[a for p in parts for a in p], w, m, v)


def _adamw_small(name, g, w, m, v):
    def body(g_ref, w_ref, m_ref, v_ref, d_out, m_out, v_out):
        d_out[...], m_out[...], v_out[...] = _adamw_math(g_ref[...], w_ref[...], m_ref[...], v_ref[...])

    vm = pl.BlockSpec(memory_space=pltpu.VMEM)
    return pl.pallas_call(body, out_shape=[jax.ShapeDtypeStruct(g.shape, F32)] * 3, in_specs=[vm] * 4, out_specs=[vm] * 3,
                          compiler_params=pltpu.CompilerParams(vmem_limit_bytes=VMEM_LIMIT), name=name)(g, w, m, v)


def _pack_rows(arrays, total_rows):
    flat = [a.reshape(-1, LANES) for a in arrays]
    used = sum(f.shape[0] for f in flat)
    return jnp.concatenate(flat + [jnp.zeros((total_rows - used, LANES), F32)], axis=0)


def _unpack_rows(packed, like):
    out, at = [], 0
    for a in like:
        n = a.size // LANES
        out.append(packed[at:at + n].reshape(a.shape))
        at += n
    return out


def kernel(x, norm_mix_g, w_in, conv_w, conv_b, lru_wa, lru_ba, lru_wx, lru_bx, lru_lambda, w_proj_attn, w_proj_lru, w_out, norm_mlp_g, w_up, w_down, norm_final_g, loss_target, m_norm_mix_g, m_w_in, m_conv_w, m_conv_b, m_lru_wa, m_lru_ba, m_lru_wx, m_lru_bx, m_lru_lambda, m_w_proj_attn, m_w_proj_lru, m_w_out, m_norm_mlp_g, m_w_up, m_w_down, m_norm_final_g, v_norm_mix_g, v_w_in, v_conv_w, v_conv_b, v_lru_wa, v_lru_ba, v_lru_wx, v_lru_bx, v_lru_lambda, v_w_proj_attn, v_w_proj_lru, v_w_out, v_norm_mlp_g, v_w_up, v_w_down, v_norm_final_g):
    xs, tgt = x[0], loss_target[0]
    s, d = xs.shape
    nh = d // HEAD
    ix, iy, ic = _coords()
    core = jnp.reshape(ic, (1,)).astype(jnp.int32)
    chip = jnp.reshape(2 * ix + iy, (1,)).astype(jnp.int32)
    dev = 4 * ix + 2 * iy + ic

    big = [w_in[0], w_proj_attn[0], w_proj_lru[0], w_out[0], w_up[0], w_down[0]]
    big_m = [m_w_in[0], m_w_proj_attn[0], m_w_proj_lru[0], m_w_out[0], m_w_up[0], m_w_down[0]]
    big_v = [v_w_in[0], v_w_proj_attn[0], v_w_proj_lru[0], v_w_out[0], v_w_up[0], v_w_down[0]]
    kinds = ["col", "row", "row", "row", "col", "row"]
    pad_taps = lambda t: jnp.pad(t, ((0, SUBLANES - CONV_TAPS), (0, 0)))
    shards = [w.astype(BF16) for w in big]
    pad_taps2 = lambda t: jnp.pad(t, ((0, 2 * SUBLANES - CONV_TAPS), (0, 0)))
    win, cw_slots = _all_gather("all_gather_w_in", [shards[0], pad_taps2(conv_w[0])], ["col", "slot"])
    later = lax.optimization_barrier((shards[1:], win))[0]
    wpa, wpl, wout = _all_gather("all_gather_mix", later[:3], kinds[1:4], sequencer_id=1)
    wup, wdown = _all_gather("all_gather_mlp", later[3:], kinds[4:], sequencer_id=5)
    cw8 = jnp.transpose(cw_slots[:, :SUBLANES], (1, 0, 2)).reshape(SUBLANES, d)
    row_id = lax.broadcasted_iota(jnp.int32, (SUBLANES, d), 0)
    vec8 = sum(jnp.where(row_id == k, t, 0.0) for k, t in ((VEC_CB, conv_b), (VEC_BA, lru_ba), (VEC_BX, lru_bx), (VEC_LAM, lru_lambda)))
    wa16, wx16 = lru_wa[0].astype(BF16), lru_wx[0].astype(BF16)
    slopes = 2.0 ** (-8.0 * jnp.arange(1, nh + 1, dtype=F32) / nh)

    def seg_specs(*segs):
        return lambda bm, bn: [pl.BlockSpec((bm, bn), (lambda i, j, kk, sg=sg: (i, sg * (d // bn) + j))) for sg in segs]

    def plain_specs(k):
        return lambda bm, bn: [pl.BlockSpec((bm, bn), lambda i, j, kk: (i, j)) for _ in range(k)]

    xn = _rms_fwd("norm_mix", xs, norm_mix_g)
    proj = _mm_fwd("proj_in", xn, win, 0, 7 * d, [F32])[0]
    att, lse = _attn_fwd(proj, d, slopes)
    a3, u3, xc = _gates_fwd(proj, d, cw8, vec8, wa16, wx16)
    h3, hp3 = _scan_fwd(a3, u3)
    ylru, h2d = _lru_out(h3, proj)
    pa = _mm_fwd("proj_attn", att, wpa, 0, d, [F32])[0]

    def merge(acc, pa_b, ga, gl):
        return acc, _sigmoid(ga) * pa_b + _sigmoid(gl) * acc

    plr, merged = _mm_fwd("proj_lru_merge", ylru, wpl, 0, d, [F32, BF16], merge, (pa, proj, proj),
                          lambda bm, bn: plain_specs(1)(bm, bn) + seg_specs(SEG_GA, SEG_GL)(bm, bn), bm=512)
    h1 = _mm_fwd("mix_out", merged, wout, 0, d, [F32], lambda acc, r: (acc + r,), (xs,), plain_specs(1))[0]
    hn = _rms_fwd("norm_mlp", h1, norm_mlp_g)

    def relu2(acc):
        return acc, jnp.square(jnp.maximum(acc, 0.0))

    up, hid = _mm_fwd("mlp_up", hn, wup, 0, wup.shape[1], [BF16, BF16], relu2)
    h2 = _mm_fwd("mlp_down", hid, wdown, 0, d, [F32], lambda acc, r: (acc + r,), (h1,), plain_specs(1))[0]
    dh2, dh2b, dg3, loss_lanes = _final_loss(h2, tgt, norm_final_g.reshape(1, d))
    loss = lax.psum(0.5 / d * jnp.sum(loss_lanes), ("x", "y", "c"))
    dh2b = lax.optimization_barrier((dh2b, loss))[0]

    def reduce_group(tag, kk, shp, partials, from_sibling, sequencer_id):
        sums = [_chip_sum(f"chip_sum_{tag}_{i}", p, f, k, sh, core) for i, (p, f, k, sh) in enumerate(zip(partials, from_sibling, kk, shp))]
        return list(zip(sums, _exchange_chips(f"rs_chips_{tag}", sums, sequencer_id)))

    dup = _mm_nt("mlp_down_dx", dh2b, wdown, [BF16], lambda acc, u: (acc * (2.0 * jnp.maximum(u.astype(F32), 0.0)),), (up,), plain_specs(1))[0]
    g_wdown = _mm_tn("mlp_down_dw", hid, dh2b)
    g_wup = _mm_tn("mlp_up_dw", hn, dup)
    shp_mlp = [w.shape for w in big[4:]]
    (dhn,), sib_mlp = _mm_nt("mlp_up_dx", dup, wup, [F32], side=_sibling_side([g_wup, g_wdown], kinds[4:], shp_mlp))
    red_up, red_down = reduce_group("mlp", kinds[4:], shp_mlp, [g_wup, g_wdown], sib_mlp, 2)
    dhn = lax.optimization_barrier((dhn, red_up[0], red_down[0]))[0]
    dh1, dh1b, dg2 = _rms_bwd("norm_mlp_bwd", h1, norm_mlp_g, dhn, dh2)

    def merge_bwd(acc, pa_b, pl_b, ga, gl):
        sa, sl = _sigmoid(ga), _sigmoid(gl)
        return acc * sa, acc * sl, acc * pa_b * sa * (1.0 - sa), acc * pl_b * sl * (1.0 - sl)

    dpa, dpl, dga, dgl = _mm_nt("mix_out_dx", dh1b, wout, [BF16] * 4, merge_bwd, (pa, plr, proj, proj),
                                lambda bm, bn: plain_specs(2)(bm, bn) + seg_specs(SEG_GA, SEG_GL)(bm, bn), bm=512)
    g_wout = _mm_tn("mix_out_dw", merged, dh1b)
    datt = _mm_nt("proj_attn_dx", dpa, wpa, [F32])[0]
    g_wpa = _mm_tn("proj_attn_dw", att, dpa)

    def lru_out_bwd(acc, h_b, gate):
        return acc * _gelu(gate), acc * h_b * _gelu_grad(gate)

    g_wpl = _mm_tn("proj_lru_dw", ylru, dpl)
    shp_mix = [w.shape for w in big[1:4]]
    (dh, dxg), sib_mix = _mm_nt("proj_lru_dx", dpl, wpl, [F32, BF16], lru_out_bwd, (h2d, proj),
                                lambda bm, bn: plain_specs(1)(bm, bn) + seg_specs(SEG_GATE)(bm, bn), bm=512,
                                side=_sibling_side([g_wpa, g_wpl, g_wout], kinds[1:4], shp_mix))
    red_pa, red_pl, red_out = reduce_group("mix", kinds[1:4], shp_mix, [g_wpa, g_wpl, g_wout], sib_mix, 3)
    dq, dk, dv = _attn_bwd(proj, d, datt, att, lse, slopes)
    g3, da3 = _scan_bwd(a3, hp3, dh)
    dxc, dwa, dwx, dvec = _gates_bwd(g3, da3, xc, wa16, wx16, vec8)
    dxr, dconv = _conv_bwd(dxc, proj, cw8)

    def small_step(tag, grads, ws, ms, vs, like, after):
        n_rows = sum(g.size for g in grads) // LANES
        per_dev = -(-n_rows // (N_DEV * SUBLANES)) * SUBLANES
        packed = lax.optimization_barrier((_pack_rows(grads, N_DEV * per_dev), after))[0]
        total = _all_reduce_small(f"all_reduce_{tag}", packed)
        w_rows = -(-(sum(w.size for w in ws) // LANES) // SUBLANES) * SUBLANES
        upd = _adamw_small(f"adamw_{tag}", total[:w_rows], _pack_rows(ws, w_rows), _pack_rows(ms, w_rows), _pack_rows(vs, w_rows))
        return _unpack_rows(total, like), [_unpack_rows(t, ws) for t in upd]

    early_w = [conv_b, lru_wa, lru_ba, lru_wx, lru_bx, lru_lambda, norm_mlp_g, norm_final_g]
    early_m = [m_conv_b, m_lru_wa, m_lru_ba, m_lru_wx, m_lru_bx, m_lru_lambda, m_norm_mlp_g, m_norm_final_g]
    early_v = [v_conv_b, v_lru_wa, v_lru_ba, v_lru_wx, v_lru_bx, v_lru_lambda, v_norm_mlp_g, v_norm_final_g]
    early_g = [dconv[CONV_TAPS:CONV_TAPS + 1], dwa, dvec[VEC_BA:VEC_BA + 1], dwx, dvec[VEC_BX:VEC_BX + 1],
               dvec[VEC_LAM:VEC_LAM + 1], dg2, dg3, dconv[0:CONV_TAPS]]
    dq = lax.optimization_barrier((dq, red_up[1], red_down[1]))[0]
    early_sum, early_upd = small_step("small", early_g, early_w, early_m, early_v,
                                      early_w + [jax.ShapeDtypeStruct((1, CONV_TAPS, d), F32)], dq)
    g_cw_full = early_sum[-1]
    cshard = conv_w.shape[2]
    g_cw = lax.dynamic_slice(g_cw_full, (0, 0, dev * cshard), (1, CONV_TAPS, cshard))
    cw_delta, cw_m, cw_v = (t[:CONV_TAPS][None] for t in _adamw_small(
        "adamw_conv_w", pad_taps(g_cw[0]), pad_taps(conv_w[0]), pad_taps(m_conv_w[0]), pad_taps(v_conv_w[0])))
    dq = lax.optimization_barrier((dq, early_sum))[0]

    dproj = jnp.concatenate([dq, dk, dv, dxr, dxg, dga, dgl], axis=1)
    dproj = lax.optimization_barrier((dproj, red_pa[1], red_pl[1], red_out[1]))[0]
    half = (big[0].shape[0] // 2, big[0].shape[1])
    g_in0 = _mm_tn("proj_in_dw_0", xn, dproj, part=(0, 2))
    g_in1, sib_in0 = _mm_tn("proj_in_dw_1", xn, dproj, part=(1, 2), side=_sibling_side([g_in0], ["col"], [half]))
    red_in = reduce_group("in_0", ["col"], [half], [g_in0], sib_in0, 4)
    dproj = lax.optimization_barrier((dproj, red_in[0][0]))[0]
    (dxn0,), sib_in1 = _mm_nt("proj_in_dx_0", dproj, win, [F32], part=(0, 2), side=_sibling_side([g_in1], ["col"], [half]))
    red_in += reduce_group("in_1", ["col"], [half], [g_in1], sib_in1, 6)
    dproj = lax.optimization_barrier((dproj, red_in[1][0]))[0]
    dxn1 = _mm_nt("proj_in_dx_1", dproj, win, [F32], part=(1, 2))[0]
    dxn = jnp.concatenate([dxn0, dxn1], axis=0)
    dxn = lax.optimization_barrier((dxn, red_in[0][1]))[0]
    grad_x, _, dg1 = _rms_bwd("norm_mix_bwd", xs, norm_mix_g, dxn, dh1)
    red_up, red_down = lax.optimization_barrier(((red_up, red_down), dg1))[0]
    big_out = {i: _adamw_shard(f"adamw_{i}", [red], big[i], big_m[i], big_v[i], chip) for i, red in ((4, red_up), (5, red_down))}
    big_out.update({i: _adamw_shard(f"adamw_{i}", [red], big[i], big_m[i], big_v[i], chip) for i, red in ((1, red_pa), (2, red_pl), (3, red_out))})
    late_sum, late_upd = small_step("norm_mix", [dg1], [norm_mix_g], [m_norm_mix_g], [v_norm_mix_g], [norm_mix_g], big_out[3])
    big_out[0] = _adamw_shard("adamw_0", red_in, big[0], big_m[0], big_v[0], chip)
    s_grad = late_sum + early_sum[:-1]
    s_delta, s_m, s_v = (late_upd[j] + early_upd[j] for j in range(3))


    names = ["norm_mix_g", "w_in", "conv_w", "conv_b", "lru_wa", "lru_ba", "lru_wx", "lru_bx", "lru_lambda", "w_proj_attn", "w_proj_lru",
             "w_out", "norm_mlp_g", "w_up", "w_down", "norm_final_g"]
    small_names = ["norm_mix_g", "conv_b", "lru_wa", "lru_ba", "lru_wx", "lru_bx", "lru_lambda", "norm_mlp_g", "norm_final_g"]
    big_names = ["w_in", "w_proj_attn", "w_proj_lru", "w_out", "w_up", "w_down"]
    res = {"conv_w": (g_cw, cw_delta, cw_m, cw_v)}
    for i, nm in enumerate(small_names):
        res[nm] = (s_grad[i], s_delta[i], s_m[i], s_v[i])
    for i, nm in enumerate(big_names):
        res[nm] = tuple(t[None] for t in big_out[i])
    return (loss, grad_x[None], *[res[nm][0] for nm in names], *[res[nm][1] for nm in names],
            *[res[nm][2] for nm in names], *[res[nm][3] for nm in names])
```

```python
import jax
import jax.numpy as jnp
from jax import lax
from jax.experimental import pallas as pl
from jax.experimental.pallas import tpu as pltpu
from jax.experimental.pallas import tpu_sc as plsc

F32, BF16 = jnp.float32, jnp.bfloat16
MESH = pl.DeviceIdType.MESH
HBM = pl.BlockSpec(memory_space=pltpu.HBM)
N_DEV = 8
N_CHIP = 4
HEAD = 128
SPAN = 128
DILATIONS = (1, 4, 16)
CONV_TAPS = 4
LRU_C = 8.0
NORM_EPS = 1e-6
LANES = 128
SUBLANES = 8
VMEM_LIMIT = 56 * 1024 * 1024
ADAM_LR, ADAM_B1, ADAM_B2, ADAM_EPS, ADAM_WD, ADAM_STEP = 0.001, 0.9, 0.999, 1e-08, 0.01, 10
ADAM_C1 = 1.0 - ADAM_B1 ** ADAM_STEP
ADAM_C2 = 1.0 - ADAM_B2 ** ADAM_STEP
NEG = -1e30


def _params(sem=None):
    return pltpu.CompilerParams(dimension_semantics=sem, vmem_limit_bytes=VMEM_LIMIT)


def _sigmoid(v):
    return 1.0 / (1.0 + jnp.exp(-v))


def _gelu(v):
    k = 0.7978845608028654
    return 0.5 * v * (1.0 + jnp.tanh(k * (v + 0.044715 * v * v * v)))


def _gelu_grad(v):
    k = 0.7978845608028654
    t = jnp.tanh(k * (v + 0.044715 * v * v * v))
    return 0.5 * (1.0 + t) + 0.5 * v * (1.0 - t * t) * k * (1.0 + 3.0 * 0.044715 * v * v)


NN = (((1,), (0,)), ((), ()))
NT = (((1,), (1,)), ((), ()))
TN = (((0,), (0,)), ((), ()))


def _mm(name, a, a_spec, b, b_spec, dn, grid, out_shapes, out_specs, acc_block, epilogue=None, extras=(), extra_specs=(), side=None):
    nk, ne, no = grid[2], len(extras), len(out_shapes)
    side_ops, side_shapes, side_copies, make_copies = side if side is not None else ((), (), 0, None)
    ns_in, ns_out = len(side_ops), len(side_shapes)

    def body(*refs):
        a_ref, b_ref = refs[0], refs[1]
        ex, side_in = refs[2:2 + ne], refs[2 + ne:2 + ne + ns_in]
        outs = refs[2 + ne + ns_in:2 + ne + ns_in + no]
        side_out = refs[2 + ne + ns_in + no:2 + ne + ns_in + no + ns_out]
        scratch = refs[2 + ne + ns_in + no + ns_out:]
        at = [pl.program_id(ax) for ax in range(3)]
        if side is not None:
            @pl.when((at[0] == 0) & (at[1] == 0) & (at[2] == 0))
            def _():
                for cp in make_copies(side_in, side_out, scratch[-2], scratch[-1]):
                    cp.start()

        part = lax.dot_general(a_ref[...], b_ref[...], dn, preferred_element_type=F32)

        def finish(acc):
            vals = epilogue(acc, *[e[...] for e in ex]) if epilogue is not None else (acc,)
            for o, v in zip(outs, vals):
                o[...] = v.astype(o.dtype)

        if nk == 1:
            finish(part)
        else:
            acc_ref, k = scratch[0], at[2]

            @pl.when(k == 0)
            def _():
                acc_ref[...] = part

            @pl.when(k > 0)
            def _():
                acc_ref[...] += part

            @pl.when(k == nk - 1)
            def _():
                finish(acc_ref[...])

        if side is not None:
            @pl.when((at[0] == grid[0] - 1) & (at[1] == grid[1] - 1) & (at[2] == grid[2] - 1))
            def _():
                for cp in make_copies(side_in, side_out, scratch[-2], scratch[-1]):
                    cp.wait()

    scratch_shapes = [pltpu.VMEM(acc_block, F32)] if nk > 1 else []
    if side is not None:
        scratch_shapes += [pltpu.SemaphoreType.DMA((side_copies,)), pltpu.SemaphoreType.DMA((side_copies,))]
    res = pl.pallas_call(
        body, out_shape=[*out_shapes, *side_shapes], grid=grid, in_specs=[a_spec, b_spec, *extra_specs, *[HBM] * ns_in],
        out_specs=[*out_specs, *[HBM] * ns_out], scratch_shapes=scratch_shapes,
        compiler_params=_params(("arbitrary",) * 3 if side is not None else ("parallel", "parallel", "arbitrary")),
        name=name)(a, b, *extras, *side_ops)
    return res if side is None else (res[:no], res[no:])


def _blk(n, pref):
    return pref if n % pref == 0 else n


def _kblk(k):
    return k if k <= 2048 else next(b for b in (2048, 1024, 512) if k % b == 0)


def _mm_fwd(name, a, w, col0, ncols, out_dtypes, epilogue=None, extras=(), extra_specs_fn=None, seg_out=None, bm=1024, bn=1024):
    m, k = a.shape
    bm, bn = _blk(m, bm), _blk(ncols, bn)
    bk = _kblk(k)
    nk = k // bk
    cb0 = col0 // bn
    grid = (m // bm, ncols // bn, nk)
    a_spec = pl.BlockSpec((bm, bk), lambda i, j, kk: (i, kk))
    b_spec = pl.BlockSpec((bk, bn), lambda i, j, kk: (kk, cb0 + j))
    if seg_out is None:
        shapes = [jax.ShapeDtypeStruct((m, ncols), dt) for dt in out_dtypes]
        specs = [pl.BlockSpec((bm, bn), lambda i, j, kk: (i, j)) for _ in out_dtypes]
    else:
        per = seg_out // bn
        shapes = [jax.ShapeDtypeStruct((ncols // seg_out, m, seg_out), dt) for dt in out_dtypes]
        specs = [pl.BlockSpec((None, bm, bn), lambda i, j, kk: (j // per, i, j % per)) for _ in out_dtypes]
    ex_specs = extra_specs_fn(bm, bn) if extra_specs_fn else ()
    return _mm(name, a, a_spec, w, b_spec, NN, grid, shapes, specs, (bm, bn), epilogue, extras, ex_specs)


def _mm_nt(name, a, w, out_dtypes, epilogue=None, extras=(), extra_specs_fn=None, part=(0, 1), side=None, bm=1024, bn=1024):
    n = w.shape[0]
    if a.ndim == 3:
        seg_cols, m, k = a.shape[2], a.shape[1], a.shape[0] * a.shape[2]
    else:
        m, k = a.shape
    m = m // part[1]
    bm, bn = _blk(m, bm), _blk(n, bn)
    bk = _kblk(k)
    grid = (m // bm, n // bn, k // bk)
    i0 = part[0] * (m // bm)
    if a.ndim == 3:
        per = seg_cols // bk
        a_spec = pl.BlockSpec((None, bm, bk), lambda i, j, kk: (kk // per, i0 + i, kk % per))
    else:
        a_spec = pl.BlockSpec((bm, bk), lambda i, j, kk: (i0 + i, kk))
    b_spec = pl.BlockSpec((bn, bk), lambda i, j, kk: (j, kk))
    shapes = [jax.ShapeDtypeStruct((m, n), dt) for dt in out_dtypes]
    specs = [pl.BlockSpec((bm, bn), lambda i, j, kk: (i, j)) for _ in out_dtypes]
    ex_specs = extra_specs_fn(bm, bn) if extra_specs_fn else ()
    return _mm(name, a, a_spec, w, b_spec, NT, grid, shapes, specs, (bm, bn), epilogue, extras, ex_specs, side)


def _mm_tn(name, a, b, part=(0, 1), side=None, bm=1024, bn=1024):
    t, m = a.shape
    n = b.shape[1] if b.ndim == 2 else b.shape[0] * b.shape[2]
    m = m // part[1]
    bm, bn = _blk(m, bm), _blk(n, bn)
    grid = (m // bm, n // bn, 1)
    i0 = part[0] * (m // bm)
    a_spec = pl.BlockSpec((t, bm), lambda i, j, kk: (0, i0 + i))
    if b.ndim == 3:
        per = b.shape[2] // bn
        b_spec = pl.BlockSpec((None, t, bn), lambda i, j, kk: (j // per, 0, j % per))
    else:
        b_spec = pl.BlockSpec((t, bn), lambda i, j, kk: (0, j))
    res = _mm(name, a, a_spec, b, b_spec, TN, grid, [jax.ShapeDtypeStruct((m, n), BF16)],
              [pl.BlockSpec((bm, bn), lambda i, j, kk: (i, j))], (bm, bn), side=side)
    return res[0] if side is None else (res[0][0], res[1])


ROWS = 256


def _row_spec(d):
    return pl.BlockSpec((ROWS, d), lambda i: (i, 0))


def _vec_spec(d, rows=1):
    return pl.BlockSpec((rows, d), lambda i: (0, 0))


def _rms_fwd(name, x, g):
    s, d = x.shape

    def body(x_ref, g_ref, o_ref):
        xv = x_ref[...]
        r = lax.rsqrt(jnp.mean(xv * xv, axis=-1, keepdims=True) + NORM_EPS)
        o_ref[...] = (xv * r * g_ref[...]).astype(BF16)

    return pl.pallas_call(body, out_shape=jax.ShapeDtypeStruct((s, d), BF16), grid=(s // ROWS,),
                          in_specs=[_row_spec(d), _vec_spec(d)], out_specs=_row_spec(d),
                          compiler_params=_params(("parallel",)), name=name)(x, g)


def _rms_bwd_math(xv, g, dy):
    r = lax.rsqrt(jnp.mean(xv * xv, axis=-1, keepdims=True) + NORM_EPS)
    n = xv * r
    z = dy * g
    dx = r * (z - n * jnp.mean(z * n, axis=-1, keepdims=True))
    return dx, jnp.sum(dy * n, axis=0, keepdims=True)


def _rms_bwd(name, x, g, dy, resid):
    s, d = x.shape

    def body(x_ref, g_ref, dy_ref, r_ref, dx_ref, dxb_ref, dg_ref):
        dx, dg = _rms_bwd_math(x_ref[...], g_ref[...], dy_ref[...])
        dx = dx + r_ref[...]
        dx_ref[...] = dx
        dxb_ref[...] = dx.astype(BF16)

        @pl.when(pl.program_id(0) == 0)
        def _():
            dg_ref[...] = jnp.zeros_like(dg_ref)

        dg_ref[...] += dg

    return pl.pallas_call(
        body, out_shape=[jax.ShapeDtypeStruct((s, d), F32), jax.ShapeDtypeStruct((s, d), BF16), jax.ShapeDtypeStruct((1, d), F32)],
        grid=(s // ROWS,), in_specs=[_row_spec(d), _vec_spec(d), _row_spec(d), _row_spec(d)],
        out_specs=[_row_spec(d), _row_spec(d), _vec_spec(d)], compiler_params=_params(("arbitrary",)), name=name)(x, g, dy, resid)


def _final_loss(h2, tgt, g):
    s, d = h2.shape

    def body(x_ref, t_ref, g_ref, dx_ref, dxb_ref, dg_ref, ls_ref):
        xv, gv = x_ref[...], g_ref[...]
        r = lax.rsqrt(jnp.mean(xv * xv, axis=-1, keepdims=True) + NORM_EPS)
        diff = xv * r * gv - t_ref[...]
        dx, dg = _rms_bwd_math(xv, gv, diff * (1.0 / d))
        dx_ref[...] = dx
        dxb_ref[...] = dx.astype(BF16)

        @pl.when(pl.program_id(0) == 0)
        def _():
            dg_ref[...] = jnp.zeros_like(dg_ref)
            ls_ref[...] = jnp.zeros_like(ls_ref)

        dg_ref[...] += dg
        ls_ref[...] += jnp.sum(diff * diff, axis=0, keepdims=True)

    return pl.pallas_call(
        body, out_shape=[jax.ShapeDtypeStruct((s, d), F32), jax.ShapeDtypeStruct((s, d), BF16),
                         jax.ShapeDtypeStruct((1, d), F32), jax.ShapeDtypeStruct((1, d), F32)],
        grid=(s // ROWS,), in_specs=[_row_spec(d), _row_spec(d), _vec_spec(d)],
        out_specs=[_row_spec(d), _row_spec(d), _vec_spec(d), _vec_spec(d)],
        compiler_params=_params(("arbitrary",)), name="final_norm_loss")(h2, tgt, g)


ATTN_Q = 128


ATTN_BATCH = 8


def _attn_units(s):
    units = []
    for gi, d in enumerate(DILATIONS):
        for r in range(d):
            for q0 in range(0, s // d, ATTN_Q):
                k0 = max(q0 - SPAN, 0)
                units.append((gi, d, r, q0, k0, q0 + ATTN_Q - k0))
    return units


def _stream_rows(d, r, start, size):
    return pl.ds(r + start * d, size) if d == 1 else pl.ds(r + start * d, size, stride=d)


def _attn_scores(q_ref, k_ref, slope, d, r, q0, k0, nk):
    qrows, krows = _stream_rows(d, r, q0, ATTN_Q), _stream_rows(d, r, k0, nk)
    qb = q_ref[qrows, :].astype(BF16)
    kb = k_ref[krows, :].astype(BF16)
    sc = lax.dot_general(qb, kb, NT, preferred_element_type=F32) * (HEAD ** -0.5)
    qi = lax.broadcasted_iota(jnp.int32, (ATTN_Q, nk), 0)
    kj = lax.broadcasted_iota(jnp.int32, (ATTN_Q, nk), 1)
    dist = (q0 - k0) + qi - kj
    valid = (dist >= 0) & (dist <= SPAN)
    sc = sc - (slope * d) * dist.astype(F32)
    return jnp.where(valid, sc, NEG), valid, qb, kb, qrows, krows


def _attn_fwd(proj, dm, slopes):
    s = proj.shape[0]
    units = _attn_units(s)

    def body(sl_ref, q_ref, k_ref, v_ref, att_ref, lse_ref, *scr):
        o_scr, l_scr = scr[:3], scr[3:]
        slope = sl_ref[pl.program_id(0)]
        for first in range(0, len(units), ATTN_BATCH):
            batch = units[first:first + ATTN_BATCH]
            scored = [_attn_scores(q_ref, k_ref, slope, d, r, q0, k0, nk) for _, d, r, q0, k0, nk in batch]
            soft = []
            for sc, _, _, _, _, _ in scored:
                m = jnp.max(sc, axis=-1, keepdims=True)
                p = jnp.exp(sc - m)
                soft.append((m, p, jnp.sum(p, axis=-1, keepdims=True)))
            outs = [lax.dot_general(p.astype(BF16), v_ref[sco[5], :].astype(BF16), NN, preferred_element_type=F32)
                    for (m, p, l), sco in zip(soft, scored)]
            for (gi, *_), (m, p, l), sco, o in zip(batch, soft, scored, outs):
                o_scr[gi][sco[4], :] = o / l
                l_scr[gi][sco[4], :] = jnp.broadcast_to(m + jnp.log(l), (ATTN_Q, HEAD))
        l0, l1, l2 = l_scr[0][...], l_scr[1][...], l_scr[2][...]
        m = jnp.maximum(jnp.maximum(l0, l1), l2)
        w0, w1, w2 = jnp.exp(l0 - m), jnp.exp(l1 - m), jnp.exp(l2 - m)
        tot = w0 + w1 + w2
        att_ref[...] = ((w0 * o_scr[0][...] + w1 * o_scr[1][...] + w2 * o_scr[2][...]) / tot).astype(BF16)
        lse_ref[...] = m + jnp.log(tot)

    def seg(i):
        return pl.BlockSpec((s, HEAD), lambda h: (0, i * (dm // HEAD) + h))

    col = pl.BlockSpec((s, HEAD), lambda h: (0, h))
    return pl.pallas_call(
        body, out_shape=[jax.ShapeDtypeStruct((s, dm), BF16), jax.ShapeDtypeStruct((s, dm), F32)], grid=(dm // HEAD,),
        in_specs=[pl.BlockSpec(memory_space=pltpu.SMEM), seg(0), seg(1), seg(2)], out_specs=[col, col],
        scratch_shapes=[pltpu.VMEM((s, HEAD), F32)] * (2 * len(DILATIONS)),
        compiler_params=_params(("parallel",)), name="attn_fwd")(slopes, proj, proj, proj)


def _attn_bwd(proj, dm, datt, att, lse, slopes, others):
    s = proj.shape[0]
    units = _attn_units(s)

    def body(sl_ref, q_ref, k_ref, v_ref, do_ref, att_ref, lse_ref, o3, o4, o5, o6, out_ref, dq_scr, dk_scr, dv_scr, dl_scr):
        slope = sl_ref[pl.program_id(0)]
        delta = jnp.sum(do_ref[...] * att_ref[...].astype(F32), axis=-1, keepdims=True)
        dl_scr[...] = jnp.broadcast_to(delta, (s, HEAD))
        dq_scr[...] = jnp.zeros_like(dq_scr)
        dk_scr[...] = jnp.zeros_like(dk_scr)
        dv_scr[...] = jnp.zeros_like(dv_scr)
        for first in range(0, len(units), ATTN_BATCH):
            scored = [_attn_scores(q_ref, k_ref, slope, d, r, q0, k0, nk) for _, d, r, q0, k0, nk in units[first:first + ATTN_BATCH]]
            dobs = [do_ref[sco[4], :].astype(BF16) for sco in scored]
            dps = [lax.dot_general(dob, v_ref[sco[5], :].astype(BF16), NT, preferred_element_type=F32) for dob, sco in zip(dobs, scored)]
            ps = [jnp.where(sco[1], jnp.exp(sco[0] - lse_ref[sco[4], :][:, 0:1]), 0.0) for sco in scored]
            dss = [(p * (dp - dl_scr[sco[4], :][:, 0:1]) * (HEAD ** -0.5)).astype(BF16) for p, dp, sco in zip(ps, dps, scored)]
            dqs = [lax.dot_general(ds, sco[3], NN, preferred_element_type=F32) for ds, sco in zip(dss, scored)]
            dks = [lax.dot_general(ds, sco[2], TN, preferred_element_type=F32) for ds, sco in zip(dss, scored)]
            dvs = [lax.dot_general(p.astype(BF16), dob, TN, preferred_element_type=F32) for p, dob in zip(ps, dobs)]
            for sco, dq, dk, dv in zip(scored, dqs, dks, dvs):
                dq_scr[sco[4], :] += dq
                dk_scr[sco[5], :] += dk
                dv_scr[sco[5], :] += dv
        for j, scr in enumerate((dq_scr, dk_scr, dv_scr)):
            out_ref[j] = scr[...].astype(BF16)
        for j, other in enumerate((o3, o4, o5, o6)):
            out_ref[3 + j] = other[...]

    def seg(i):
        return pl.BlockSpec((s, HEAD), lambda h: (0, i * (dm // HEAD) + h))

    col = pl.BlockSpec((s, HEAD), lambda h: (0, h))
    return pl.pallas_call(
        body, out_shape=jax.ShapeDtypeStruct((7, s, dm), BF16), grid=(dm // HEAD,),
        in_specs=[pl.BlockSpec(memory_space=pltpu.SMEM), seg(0), seg(1), seg(2), col, col, col, col, col, col, col],
        out_specs=pl.BlockSpec((7, s, HEAD), lambda h: (0, 0, h)), scratch_shapes=[pltpu.VMEM((s, HEAD), F32)] * 4,
        compiler_params=_params(("parallel",)), name="attn_bwd")(slopes, proj, proj, proj, datt, att, lse, *others)


VEC_CB, VEC_BA, VEC_BX, VEC_LAM = 0, 1, 2, 3
SEG_Q, SEG_K, SEG_V, SEG_X, SEG_GATE, SEG_GA, SEG_GL = range(7)


def _to_3d(ref3, val):
    lw = val.shape[1] // SUBLANES
    for j in range(SUBLANES):
        ref3[:, j, :] = val[:, j * lw:(j + 1) * lw]


def _from_3d(ref3):
    return jnp.concatenate([ref3[:, j, :] for j in range(SUBLANES)], axis=1)


def _softplus(z):
    return jnp.maximum(z, 0.0) + jnp.log1p(jnp.exp(-jnp.abs(z)))


def _gate_math(xc, wa_ref, wx_ref, vec):
    xcb = xc.astype(BF16)
    nh = xc.shape[1] // HEAD
    pre_a = jnp.concatenate([jnp.dot(xcb[:, h * HEAD:(h + 1) * HEAD], wa_ref[h], preferred_element_type=F32) for h in range(nh)], axis=1)
    pre_x = jnp.concatenate([jnp.dot(xcb[:, h * HEAD:(h + 1) * HEAD], wx_ref[h], preferred_element_type=F32) for h in range(nh)], axis=1)
    ra = _sigmoid(pre_a + vec[VEC_BA:VEC_BA + 1])
    ig = _sigmoid(pre_x + vec[VEC_BX:VEC_BX + 1])
    sp = _softplus(-vec[VEC_LAM:VEC_LAM + 1])
    log_a = -LRU_C * ra * sp
    a = jnp.exp(log_a)
    z = 2.0 * log_a
    one_minus_a2 = jnp.where(z > -0.01, -z * (1.0 + z * (0.5 + z * (1.0 / 6.0))), 1.0 - jnp.exp(z))
    mult = jnp.sqrt(one_minus_a2)
    return dict(xcb=xcb, ra=ra, ig=ig, sp=sp, a=a, mult=mult)


def _conv_pad_prev(pad_ref, cur, halo, first):
    pad_ref[0:SUBLANES, :] = jnp.where(first, 0.0, halo)
    pad_ref[SUBLANES:SUBLANES + cur.shape[0], :] = cur


def _gates_fwd(proj, d, cw8, vec8, wa, wx):
    s = proj.shape[0]
    lw = d // SUBLANES
    hb = ROWS // SUBLANES

    def body(x_ref, halo_ref, cw_ref, vec_ref, wa_ref, wx_ref, a_ref, u_ref, xc_ref, pad):
        _conv_pad_prev(pad, x_ref[...], halo_ref[...], pl.program_id(0) == 0)
        vec = vec_ref[...]
        xc = vec[VEC_CB:VEC_CB + 1]
        for k in range(CONV_TAPS):
            xc = xc + cw_ref[k:k + 1, :] * pad[pl.ds(SUBLANES - (CONV_TAPS - 1) + k, ROWS), :]
        gm = _gate_math(xc, wa_ref, wx_ref, vec)
        xc_ref[...] = xc
        _to_3d(a_ref, gm["a"])
        _to_3d(u_ref, gm["mult"] * (gm["ig"] * xc))

    spec3 = pl.BlockSpec((ROWS, SUBLANES, lw), lambda i: (i, 0, 0))
    wspec = pl.BlockSpec(wa.shape, lambda i: (0, 0, 0))
    return pl.pallas_call(
        body, out_shape=[jax.ShapeDtypeStruct((s, SUBLANES, lw), F32)] * 2 + [jax.ShapeDtypeStruct((s, d), F32)], grid=(s // ROWS,),
        in_specs=[pl.BlockSpec((ROWS, d), lambda i: (i, SEG_X)),
                  pl.BlockSpec((SUBLANES, d), lambda i: (jnp.maximum(i * hb - 1, 0), SEG_X)),
                  _vec_spec(d, SUBLANES), _vec_spec(d, SUBLANES), wspec, wspec],
        out_specs=[spec3, spec3, _row_spec(d)], scratch_shapes=[pltpu.VMEM((ROWS + SUBLANES, d), F32)],
        compiler_params=_params(("parallel",)), name="lru_gates_fwd")(proj, proj, cw8, vec8, wa, wx)


def _scan_fwd(a3, u3, proj):
    s, _, lw = a3.shape
    d = lw * SUBLANES

    def body(a_ref, u_ref, g_ref, hp_ref, h2_ref, y_ref, h3, carry):
        @pl.when(pl.program_id(0) == 0)
        def _():
            carry[...] = jnp.zeros_like(carry)

        def step(t, h):
            hp_ref[t] = h
            hn = a_ref[t] * h + u_ref[t]
            h3[t] = hn
            return hn

        carry[...] = lax.fori_loop(0, ROWS, step, carry[...], unroll=8)
        h = _from_3d(h3)
        h2_ref[...] = h
        y_ref[...] = (h * _gelu(g_ref[...])).astype(BF16)

    spec3 = pl.BlockSpec((ROWS, SUBLANES, lw), lambda i: (i, 0, 0))
    return pl.pallas_call(
        body, out_shape=[jax.ShapeDtypeStruct(a3.shape, F32), jax.ShapeDtypeStruct((s, d), F32), jax.ShapeDtypeStruct((s, d), BF16)],
        grid=(s // ROWS,), in_specs=[spec3, spec3, pl.BlockSpec((ROWS, d), lambda i: (i, SEG_GATE))],
        out_specs=[spec3, _row_spec(d), _row_spec(d)],
        scratch_shapes=[pltpu.VMEM((ROWS, SUBLANES, lw), F32), pltpu.VMEM((SUBLANES, lw), F32)],
        compiler_params=_params(("arbitrary",)), name="lru_scan_fwd")(a3, u3, proj)


def _scan_bwd(a3, hp3, dh):
    s, _, lw = a3.shape
    d = lw * SUBLANES
    nb = s // ROWS

    def body(a_ref, hp_ref, dh_ref, g_ref, da_ref, dh3, carry):
        @pl.when(pl.program_id(0) == 0)
        def _():
            carry[...] = jnp.zeros_like(carry)

        _to_3d(dh3, dh_ref[...])

        def step(j, c):
            t = ROWS - 1 - j
            g = dh3[t] + c
            g_ref[t] = g
            da_ref[t] = g * hp_ref[t]
            return a_ref[t] * g

        carry[...] = lax.fori_loop(0, ROWS, step, carry[...], unroll=8)

    spec3 = pl.BlockSpec((ROWS, SUBLANES, lw), lambda i: (nb - 1 - i, 0, 0))
    return pl.pallas_call(
        body, out_shape=[jax.ShapeDtypeStruct(a3.shape, F32)] * 2, grid=(nb,),
        in_specs=[spec3, spec3, pl.BlockSpec((ROWS, d), lambda i: (nb - 1 - i, 0))], out_specs=[spec3, spec3],
        scratch_shapes=[pltpu.VMEM((ROWS, SUBLANES, lw), F32), pltpu.VMEM((SUBLANES, lw), F32)],
        compiler_params=_params(("arbitrary",)), name="lru_scan_bwd")(a3, hp3, dh)


def _gates_bwd(g3, da3, xc, wa, wx, vec8):
    s, d = xc.shape
    lw = d // SUBLANES
    nh = d // HEAD

    def body(g_ref, da_ref, xc_ref, wa_ref, wx_ref, vec_ref, dxc_ref, dwa_ref, dwx_ref, dvec_ref):
        @pl.when(pl.program_id(0) == 0)
        def _():
            dwa_ref[...] = jnp.zeros_like(dwa_ref)
            dwx_ref[...] = jnp.zeros_like(dwx_ref)
            dvec_ref[...] = jnp.zeros_like(dvec_ref)

        xc_v, vec = xc_ref[...], vec_ref[...]
        du, da = _from_3d(g_ref), _from_3d(da_ref)
        gm = _gate_math(xc_v, wa_ref, wx_ref, vec)
        ra, ig, sp, a, mult = gm["ra"], gm["ig"], gm["sp"], gm["a"], gm["mult"]
        dmult = du * ig * xc_v
        dlog_a = da * a - dmult * (a * a) / mult
        dpre_a = dlog_a * (-LRU_C * sp) * ra * (1.0 - ra)
        dpre_x = du * mult * xc_v * ig * (1.0 - ig)
        dlam = jnp.sum(dlog_a * (-LRU_C * ra), axis=0, keepdims=True) * (-_sigmoid(-vec[VEC_LAM:VEC_LAM + 1]))
        dvec_ref[VEC_BA:VEC_BA + 1, :] += jnp.sum(dpre_a, axis=0, keepdims=True)
        dvec_ref[VEC_BX:VEC_BX + 1, :] += jnp.sum(dpre_x, axis=0, keepdims=True)
        dvec_ref[VEC_LAM:VEC_LAM + 1, :] += dlam
        dab, dxb, xcb = dpre_a.astype(BF16), dpre_x.astype(BF16), gm["xcb"]
        back = []
        for h in range(nh):
            cols = slice(h * HEAD, (h + 1) * HEAD)
            dwa_ref[h] += lax.dot_general(xcb[:, cols], dab[:, cols], TN, preferred_element_type=F32)
            dwx_ref[h] += lax.dot_general(xcb[:, cols], dxb[:, cols], TN, preferred_element_type=F32)
            back.append(lax.dot_general(dab[:, cols], wa_ref[h], NT, preferred_element_type=F32)
                        + lax.dot_general(dxb[:, cols], wx_ref[h], NT, preferred_element_type=F32))
        dxc_ref[...] = du * mult * ig + jnp.concatenate(back, axis=1)

    spec3 = pl.BlockSpec((ROWS, SUBLANES, lw), lambda i: (i, 0, 0))
    wspec = pl.BlockSpec(wa.shape, lambda i: (0, 0, 0))
    return pl.pallas_call(
        body, out_shape=[jax.ShapeDtypeStruct((s, d), F32), jax.ShapeDtypeStruct(wa.shape, F32), jax.ShapeDtypeStruct(wa.shape, F32),
                         jax.ShapeDtypeStruct((SUBLANES, d), F32)],
        grid=(s // ROWS,), in_specs=[spec3, spec3, _row_spec(d), wspec, wspec, _vec_spec(d, SUBLANES)],
        out_specs=[_row_spec(d), wspec, wspec, _vec_spec(d, SUBLANES)],
        compiler_params=_params(("arbitrary",)), name="lru_gates_bwd")(g3, da3, xc, wa, wx, vec8)


def _conv_bwd(dxc, proj, cw8):
    s, d = dxc.shape
    hb = ROWS // SUBLANES
    last = s // SUBLANES - 1

    def body(dc_ref, dnext_ref, x_ref, xprev_ref, cw_ref, dx_ref, dcw_ref, padd, padx):
        i = pl.program_id(0)

        @pl.when(i == 0)
        def _():
            dcw_ref[...] = jnp.zeros_like(dcw_ref)

        dc = dc_ref[...]
        padd[0:ROWS, :] = dc
        padd[ROWS:ROWS + SUBLANES, :] = jnp.where(i == pl.num_programs(0) - 1, 0.0, dnext_ref[...])
        _conv_pad_prev(padx, x_ref[...], xprev_ref[...], i == 0)
        dx = jnp.zeros_like(dc)
        for k in range(CONV_TAPS):
            dx = dx + cw_ref[k:k + 1, :] * padd[pl.ds(CONV_TAPS - 1 - k, ROWS), :]
            dcw_ref[k:k + 1, :] += jnp.sum(dc * padx[pl.ds(SUBLANES - (CONV_TAPS - 1) + k, ROWS), :], axis=0, keepdims=True)
        dcw_ref[CONV_TAPS:CONV_TAPS + 1, :] += jnp.sum(dc, axis=0, keepdims=True)
        dx_ref[...] = dx.astype(BF16)

    return pl.pallas_call(
        body, out_shape=[jax.ShapeDtypeStruct((s, d), BF16), jax.ShapeDtypeStruct((SUBLANES, d), F32)], grid=(s // ROWS,),
        in_specs=[_row_spec(d), pl.BlockSpec((SUBLANES, d), lambda i: (jnp.minimum((i + 1) * hb, last), 0)),
                  pl.BlockSpec((ROWS, d), lambda i: (i, SEG_X)),
                  pl.BlockSpec((SUBLANES, d), lambda i: (jnp.maximum(i * hb - 1, 0), SEG_X)), _vec_spec(d, SUBLANES)],
        out_specs=[_row_spec(d), _vec_spec(d, SUBLANES)],
        scratch_shapes=[pltpu.VMEM((ROWS + SUBLANES, d), F32), pltpu.VMEM((ROWS + SUBLANES, d), F32)],
        compiler_params=_params(("arbitrary",)), name="lru_conv_bwd")(dxc, dxc, proj, proj, cw8)


def _coords():
    return lax.axis_index("x"), lax.axis_index("y"), lax.axis_index("c")


def _other_chips(x, y):
    return [(1 - x, y), (x, 1 - y), (1 - x, 1 - y)]


def _slab(ref, kind, shard_shape, idx, half=None):
    r, c = shard_shape
    r0, nr = (0, r) if half is None else (half * (r // 2), r // 2)
    if kind == "col":
        return ref.at[pl.ds(r0, nr), pl.ds(pl.multiple_of(idx * c, LANES), c)]
    if kind == "row":
        return ref.at[pl.ds(pl.multiple_of(idx * r, SUBLANES) + r0, nr), :]
    return ref.at[idx, pl.ds(r0, nr), :]


def _full_shape(shard_shape, kind):
    r, c = shard_shape
    return {"col": (r, c * N_DEV), "row": (r * N_DEV, c), "slot": (N_DEV, r, c)}[kind]


def _handshake(peers):
    barrier = pltpu.get_barrier_semaphore()
    for peer in peers:
        pl.semaphore_signal(barrier, inc=1, device_id=peer, device_id_type=MESH)
    pl.semaphore_wait(barrier, len(peers))


def _launch(name, body, out_shape, operands, sems, sequencer_id):
    if sequencer_id is None:
        return pl.pallas_call(body, out_shape=out_shape, in_specs=[HBM] * len(operands), out_specs=[HBM] * len(out_shape),
                              scratch_shapes=sems, name=name)(*operands)
    return pl.kernel(body, out_type=out_shape, mesh=plsc.ScalarSubcoreMesh(axis_name="seq", num_cores=1), name=name,
                     scratch_types=sems, compiler_params=pltpu.CompilerParams(collective_id=sequencer_id))(*operands)


AG_COPIES = 10


def _all_gather(name, shards, kinds, sequencer_id=None):
    n = len(shards)
    shapes = [s.shape for s in shards]

    def body(*refs):
        ins, outs = refs[:n], refs[n:2 * n]
        send_sems, recv_sems, local_sems = refs[2 * n:]
        x, y, c = _coords()
        me, sib, xn, yn, dg = (x, y, c), (x, y, 1 - c), (1 - x, y, c), (x, 1 - y, c), (1 - x, 1 - y, c)
        if sequencer_id is not None:
            _handshake([sib, xn, yn])

        def part(i, dev, half=None):
            return _slab(outs[i], kinds[i], shapes[i], 4 * dev[0] + 2 * dev[1] + dev[2], half)

        def copy(i, k, block, half, to, own=False):
            r = shapes[i][0]
            src = part(i, block, half) if not own else (ins[i] if half is None else ins[i].at[pl.ds(half * (r // 2), r // 2), :])
            return pltpu.make_async_remote_copy(
                src_ref=src, dst_ref=part(i, block, half), send_sem=send_sems.at[AG_COPIES * i + k],
                recv_sem=recv_sems.at[AG_COPIES * i + k], device_id=to, device_id_type=MESH)

        def other_core(dev):
            return (dev[0], dev[1], 1 - c)

        started = []

        def start(cp):
            cp.start()
            started.append(cp)

        for i in range(n):
            start(copy(i, 1, me, 0, xn, own=True))
            start(copy(i, 4, me, 1, yn, own=True))
            start(copy(i, 2, me, 1, xn, own=True))
            start(copy(i, 3, me, 0, yn, own=True))
            start(copy(i, 0, me, None, sib, own=True))
        mine = [pltpu.make_async_copy(ins[i], part(i, me), local_sems.at[i]) for i in range(n)]
        for cp in mine:
            cp.start()
        for i in range(n):
            copy(i, 1, xn, 0, me).wait_recv()
            start(copy(i, 5, xn, 0, yn))
            copy(i, 4, yn, 1, me).wait_recv()
            start(copy(i, 6, yn, 1, xn))
        for i in range(n):
            copy(i, 2, xn, 1, me).wait_recv()
            start(copy(i, 7, xn, None, sib))
            copy(i, 3, yn, 0, me).wait_recv()
            start(copy(i, 8, yn, None, sib))
        for i in range(n):
            copy(i, 5, dg, 0, me).wait_recv()
            copy(i, 6, dg, 1, me).wait_recv()
            start(copy(i, 9, dg, None, sib))
        for i in range(n):
            copy(i, 0, sib, None, me).wait_recv()
            for k, dev in ((7, xn), (8, yn), (9, dg)):
                copy(i, k, other_core(dev), None, me).wait_recv()
        for cp in started:
            cp.wait_send()
        for cp in mine:
            cp.wait()

    out_shape = [jax.ShapeDtypeStruct(_full_shape(s.shape, k), s.dtype) for s, k in zip(shards, kinds)]
    sems = [pltpu.SemaphoreType.DMA((AG_COPIES * n,)), pltpu.SemaphoreType.DMA((AG_COPIES * n,)), pltpu.SemaphoreType.DMA((n,))]
    return _launch(name, body, out_shape, shards, sems, sequencer_id)


def _sibling_copies(kinds, shard_shapes):
    def make(ins, outs, send_sems, recv_sems):
        x, y, c = _coords()
        return [pltpu.make_async_remote_copy(
            src_ref=_slab(ins[i], kinds[i], shard_shapes[i], 2 * q + (1 - c)), dst_ref=outs[i].at[q],
            send_sem=send_sems.at[N_CHIP * i + q], recv_sem=recv_sems.at[N_CHIP * i + q],
            device_id=(x, y, 1 - c), device_id_type=MESH) for i in range(len(ins)) for q in range(N_CHIP)]
    return make


def _sibling_side(partials, kinds, shard_shapes):
    return (partials, [jax.ShapeDtypeStruct((N_CHIP, *s), BF16) for s in shard_shapes], N_CHIP * len(partials),
            _sibling_copies(kinds, shard_shapes))


def _exchange_siblings(name, partials, kinds, shard_shapes, sequencer_id=None):
    n = len(partials)
    make = _sibling_copies(kinds, shard_shapes)

    def body(*refs):
        if sequencer_id is not None:
            x, y, c = _coords()
            _handshake([(x, y, 1 - c)])
        cps = make(refs[:n], refs[n:2 * n], refs[2 * n], refs[2 * n + 1])
        for cp in cps:
            cp.start()
        for cp in cps:
            cp.wait()

    return _launch(name, body, [jax.ShapeDtypeStruct((N_CHIP, *s), BF16) for s in shard_shapes], partials,
                   [pltpu.SemaphoreType.DMA((N_CHIP * n,)), pltpu.SemaphoreType.DMA((N_CHIP * n,))], sequencer_id)


def _exchange_chips(name, chip_sums, sequencer_id=None):
    n = len(chip_sums)

    def body(*refs):
        ins, outs = refs[:n], refs[n:2 * n]
        send_sems, recv_sems = refs[2 * n:]
        x, y, c = _coords()
        if sequencer_id is not None:
            _handshake([(cx, cy, c) for cx, cy in _other_chips(x, y)])
        cps = []
        for i in range(n):
            for k, (cx, cy) in enumerate(_other_chips(x, y)):
                cps.append(pltpu.make_async_remote_copy(
                    src_ref=ins[i].at[2 * cx + cy], dst_ref=outs[i].at[k], send_sem=send_sems.at[3 * i + k],
                    recv_sem=recv_sems.at[3 * i + k], device_id=(cx, cy, c), device_id_type=MESH))
        for cp in cps:
            cp.start()
        for cp in cps:
            cp.wait()

    return _launch(name, body, [jax.ShapeDtypeStruct((3, *t.shape[1:]), BF16) for t in chip_sums], chip_sums,
                   [pltpu.SemaphoreType.DMA((3 * n,)), pltpu.SemaphoreType.DMA((3 * n,))], sequencer_id)


def _all_reduce_small(name, packed):
    rows = packed.shape[0] // N_DEV

    def body(p_ref, out_ref, rb, tot, send_sems, recv_sems):
        x, y, c = _coords()
        me = 4 * x + 2 * y + c

        def peer(k):
            return (x ^ (k >> 2), y ^ ((k >> 1) & 1), c ^ (k & 1))

        def rows_of(idx):
            return pl.ds(pl.multiple_of(idx * rows, SUBLANES), rows)

        def piece(ref, idx):
            return ref.at[rows_of(idx), :]

        scatter = [pltpu.make_async_remote_copy(src_ref=piece(p_ref, me ^ k), dst_ref=rb.at[k], send_sem=send_sems.at[k],
                                                recv_sem=recv_sems.at[k], device_id=peer(k), device_id_type=MESH) for k in range(1, N_DEV)]
        for cp in scatter:
            cp.start()
        acc = p_ref[rows_of(me), :]
        for cp in scatter:
            cp.wait_recv()
        for k in range(1, N_DEV):
            acc = acc + rb[k]
        tot[...] = acc
        out_ref[rows_of(me), :] = acc
        gather = [pltpu.make_async_remote_copy(src_ref=tot, dst_ref=piece(out_ref, me), send_sem=send_sems.at[N_DEV + k],
                                               recv_sem=recv_sems.at[N_DEV + k], device_id=peer(k), device_id_type=MESH)
                  for k in range(1, N_DEV)]
        for cp in gather:
            cp.start()
        for k in range(1, N_DEV):
            pltpu.make_async_remote_copy(src_ref=tot, dst_ref=piece(out_ref, me ^ k), send_sem=send_sems.at[N_DEV + k],
                                         recv_sem=recv_sems.at[N_DEV + k], device_id=peer(k), device_id_type=MESH).wait_recv()
        for cp in scatter + gather:
            cp.wait_send()

    vm = pl.BlockSpec(memory_space=pltpu.VMEM)
    return pl.pallas_call(
        body, out_shape=jax.ShapeDtypeStruct(packed.shape, F32), in_specs=[vm], out_specs=vm,
        scratch_shapes=[pltpu.VMEM((N_DEV, rows, LANES), F32), pltpu.VMEM((rows, LANES), F32),
                        pltpu.SemaphoreType.DMA((2 * N_DEV,)), pltpu.SemaphoreType.DMA((2 * N_DEV,))],
        compiler_params=pltpu.CompilerParams(vmem_limit_bytes=VMEM_LIMIT), name=name)(packed)


def _adamw_math(g, w, m, v):
    m = ADAM_B1 * m + (1.0 - ADAM_B1) * g
    v = ADAM_B2 * v + (1.0 - ADAM_B2) * (g * g)
    delta = -ADAM_LR * ((m / ADAM_C1) / (jnp.sqrt(v / ADAM_C2) + ADAM_EPS) + ADAM_WD * w)
    return delta, m, v


def _slab_spec(kind, shard_shape, tr, slab_of):
    r, c = shard_shape
    if kind == "col":
        return pl.BlockSpec((tr, c), lambda q, i, sc: (i, slab_of(q, sc)))
    return pl.BlockSpec((tr, c), lambda q, i, sc: (slab_of(q, sc) * (r // tr) + i, 0))


def _chip_sum(name, partial, recv, kind, shard_shape, core):
    r, c = shard_shape
    tr = _blk(r, 1024)

    def body(core_ref, p_ref, r_ref, o_ref):
        o_ref[...] = (p_ref[...].astype(F32) + r_ref[...].astype(F32)).astype(BF16)

    spec4 = pl.BlockSpec((None, tr, c), lambda q, i, sc: (q, i, 0))
    grid_spec = pltpu.PrefetchScalarGridSpec(
        num_scalar_prefetch=1, grid=(N_CHIP, r // tr),
        in_specs=[_slab_spec(kind, shard_shape, tr, lambda q, sc: 2 * q + sc[0]), spec4], out_specs=spec4)
    return pl.pallas_call(body, out_shape=jax.ShapeDtypeStruct((N_CHIP, r, c), BF16), grid_spec=grid_spec,
                          compiler_params=_params(("parallel", "parallel")), name=name)(core, partial, recv)


def _adamw_shard(name, parts, w, m, v, chip):
    r, c = w.shape
    n_parts = len(parts)
    tr = _blk(r // n_parts, 256)
    per = r // n_parts // tr

    def body(chip_ref, *refs):
        src, (w_ref, m_ref, v_ref), (g_out, d_out, m_out, v_out) = refs[:2 * n_parts], refs[2 * n_parts:2 * n_parts + 3], refs[2 * n_parts + 3:]
        for p in range(n_parts):
            @pl.when(pl.program_id(0) // per == p)
            def _():
                g = src[2 * p][...].astype(F32)
                for k in range(3):
                    g = g + src[2 * p + 1][k].astype(F32)
                g_out[...] = g
                d_out[...], m_out[...], v_out[...] = _adamw_math(g, w_ref[...], m_ref[...], v_ref[...])

    def part_specs(p):
        at = lambda i: jnp.clip(i - p * per, 0, per - 1)
        return [pl.BlockSpec((None, tr, c), lambda i, sc: (sc[0], at(i), 0)), pl.BlockSpec((3, tr, c), lambda i, sc: (0, at(i), 0))]

    blk = pl.BlockSpec((tr, c), lambda i, sc: (i, 0))
    grid_spec = pltpu.PrefetchScalarGridSpec(
        num_scalar_prefetch=1, grid=(r // tr,), in_specs=[s for p in range(n_parts) for s in part_specs(p)] + [blk, blk, blk], out_specs=[blk] * 4)
    return pl.pallas_call(body, out_shape=[jax.ShapeDtypeStruct((r, c), F32)] * 4, grid_spec=grid_spec,
                          compiler_params=_params(("parallel",)), name=name)(chip, *[a for p in parts for a in p], w, m, v)


def _adamw_small(name, g, w, m, v):
    def body(g_ref, w_ref, m_ref, v_ref, d_out, m_out, v_out):
        d_out[...], m_out[...], v_out[...] = _adamw_math(g_ref[...], w_ref[...], m_ref[...], v_ref[...])

    vm = pl.BlockSpec(memory_space=pltpu.VMEM)
    return pl.pallas_call(body, out_shape=[jax.ShapeDtypeStruct(g.shape, F32)] * 3, in_specs=[vm] * 4, out_specs=[vm] * 3,
                          compiler_params=pltpu.CompilerParams(vmem_limit_bytes=VMEM_LIMIT), name=name)(g, w, m, v)


def _pack_rows(arrays, total_rows):
    flat = [a.reshape(-1, LANES) for a in arrays]
    used = sum(f.shape[0] for f in flat)
    return jnp.concatenate(flat + [jnp.zeros((total_rows - used, LANES), F32)], axis=0)


def _unpack_rows(packed, like):
    out, at = [], 0
    for a in like:
        n = a.size // LANES
        out.append(packed[at:at + n].reshape(a.shape))
        at += n
    return out


def kernel(x, norm_mix_g, w_in, conv_w, conv_b, lru_wa, lru_ba, lru_wx, lru_bx, lru_lambda, w_proj_attn, w_proj_lru, w_out, norm_mlp_g, w_up, w_down, norm_final_g, loss_target, m_norm_mix_g, m_w_in, m_conv_w, m_conv_b, m_lru_wa, m_lru_ba, m_lru_wx, m_lru_bx, m_lru_lambda, m_w_proj_attn, m_w_proj_lru, m_w_out, m_norm_mlp_g, m_w_up, m_w_down, m_norm_final_g, v_norm_mix_g, v_w_in, v_conv_w, v_conv_b, v_lru_wa, v_lru_ba, v_lru_wx, v_lru_bx, v_lru_lambda, v_w_proj_attn, v_w_proj_lru, v_w_out, v_norm_mlp_g, v_w_up, v_w_down, v_norm_final_g):
    xs, tgt = x[0], loss_target[0]
    s, d = xs.shape
    nh = d // HEAD
    ix, iy, ic = _coords()
    core = jnp.reshape(ic, (1,)).astype(jnp.int32)
    chip = jnp.reshape(2 * ix + iy, (1,)).astype(jnp.int32)
    dev = 4 * ix + 2 * iy + ic

    big = [w_in[0], w_proj_attn[0], w_proj_lru[0], w_out[0], w_up[0], w_down[0]]
    big_m = [m_w_in[0], m_w_proj_attn[0], m_w_proj_lru[0], m_w_out[0], m_w_up[0], m_w_down[0]]
    big_v = [v_w_in[0], v_w_proj_attn[0], v_w_proj_lru[0], v_w_out[0], v_w_up[0], v_w_down[0]]
    kinds = ["col", "row", "row", "row", "col", "row"]
    pad_taps = lambda t: jnp.pad(t, ((0, SUBLANES - CONV_TAPS), (0, 0)))
    shards = [w.astype(BF16) for w in big]
    pad_taps2 = lambda t: jnp.pad(t, ((0, 2 * SUBLANES - CONV_TAPS), (0, 0)))
    win, cw_slots = _all_gather("all_gather_w_in", [shards[0], pad_taps2(conv_w[0])], ["col", "slot"])
    later = lax.optimization_barrier((shards[1:], win))[0]
    wpa, wpl, wout = _all_gather("all_gather_mix", later[:3], kinds[1:4], sequencer_id=1)
    wup, wdown = _all_gather("all_gather_mlp", later[3:], kinds[4:], sequencer_id=5)
    cw8 = jnp.transpose(cw_slots[:, :SUBLANES], (1, 0, 2)).reshape(SUBLANES, d)
    row_id = lax.broadcasted_iota(jnp.int32, (SUBLANES, d), 0)
    vec8 = sum(jnp.where(row_id == k, t, 0.0) for k, t in ((VEC_CB, conv_b), (VEC_BA, lru_ba), (VEC_BX, lru_bx), (VEC_LAM, lru_lambda)))
    wa16, wx16 = lru_wa[0].astype(BF16), lru_wx[0].astype(BF16)
    slopes = 2.0 ** (-8.0 * jnp.arange(1, nh + 1, dtype=F32) / nh)

    def seg_specs(*segs):
        return lambda bm, bn: [pl.BlockSpec((bm, bn), (lambda i, j, kk, sg=sg: (i, sg * (d // bn) + j))) for sg in segs]

    def plain_specs(k):
        return lambda bm, bn: [pl.BlockSpec((bm, bn), lambda i, j, kk: (i, j)) for _ in range(k)]

    xn = _rms_fwd("norm_mix", xs, norm_mix_g)
    proj = _mm_fwd("proj_in", xn, win, 0, 7 * d, [F32])[0]
    att, lse = _attn_fwd(proj, d, slopes)
    a3, u3, xc = _gates_fwd(proj, d, cw8, vec8, wa16, wx16)
    hp3, h2d, ylru = _scan_fwd(a3, u3, proj)
    pa = _mm_fwd("proj_attn", att, wpa, 0, d, [F32])[0]

    def merge(acc, pa_b, ga, gl):
        return acc, _sigmoid(ga) * pa_b + _sigmoid(gl) * acc

    plr, merged = _mm_fwd("proj_lru_merge", ylru, wpl, 0, d, [F32, BF16], merge, (pa, proj, proj),
                          lambda bm, bn: plain_specs(1)(bm, bn) + seg_specs(SEG_GA, SEG_GL)(bm, bn), bm=512)
    h1 = _mm_fwd("mix_out", merged, wout, 0, d, [F32], lambda acc, r: (acc + r,), (xs,), plain_specs(1))[0]
    hn = _rms_fwd("norm_mlp", h1, norm_mlp_g)

    def relu2(acc):
        return acc, jnp.square(jnp.maximum(acc, 0.0))

    up, hid = _mm_fwd("mlp_up", hn, wup, 0, wup.shape[1], [BF16, BF16], relu2)
    h2 = _mm_fwd("mlp_down", hid, wdown, 0, d, [F32], lambda acc, r: (acc + r,), (h1,), plain_specs(1))[0]
    dh2, dh2b, dg3, loss_lanes = _final_loss(h2, tgt, norm_final_g.reshape(1, d))
    loss = lax.psum(0.5 / d * jnp.sum(loss_lanes), ("x", "y", "c"))
    dh2b = lax.optimization_barrier((dh2b, loss))[0]

    def reduce_group(tag, kk, shp, partials, from_sibling, sequencer_id):
        sums = [_chip_sum(f"chip_sum_{tag}_{i}", p, f, k, sh, core) for i, (p, f, k, sh) in enumerate(zip(partials, from_sibling, kk, shp))]
        return list(zip(sums, _exchange_chips(f"rs_chips_{tag}", sums, sequencer_id)))

    dup = _mm_nt("mlp_down_dx", dh2b, wdown, [BF16], lambda acc, u: (acc * (2.0 * jnp.maximum(u.astype(F32), 0.0)),), (up,), plain_specs(1))[0]
    g_wdown = _mm_tn("mlp_down_dw", hid, dh2b)
    g_wup = _mm_tn("mlp_up_dw", hn, dup)
    shp_mlp = [w.shape for w in big[4:]]
    (dhn,), sib_mlp = _mm_nt("mlp_up_dx", dup, wup, [F32], side=_sibling_side([g_wup, g_wdown], kinds[4:], shp_mlp))
    red_up, red_down = reduce_group("mlp", kinds[4:], shp_mlp, [g_wup, g_wdown], sib_mlp, 2)
    dhn = lax.optimization_barrier((dhn, red_up[0], red_down[0]))[0]
    dh1, dh1b, dg2 = _rms_bwd("norm_mlp_bwd", h1, norm_mlp_g, dhn, dh2)

    def merge_bwd(acc, pa_b, pl_b, ga, gl):
        sa, sl = _sigmoid(ga), _sigmoid(gl)
        return acc * sa, acc * sl, acc * pa_b * sa * (1.0 - sa), acc * pl_b * sl * (1.0 - sl)

    dpa, dpl, dga, dgl = _mm_nt("mix_out_dx", dh1b, wout, [BF16] * 4, merge_bwd, (pa, plr, proj, proj),
                                lambda bm, bn: plain_specs(2)(bm, bn) + seg_specs(SEG_GA, SEG_GL)(bm, bn), bm=512)
    g_wout = _mm_tn("mix_out_dw", merged, dh1b)
    datt = _mm_nt("proj_attn_dx", dpa, wpa, [F32])[0]
    g_wpa = _mm_tn("proj_attn_dw", att, dpa)

    def lru_out_bwd(acc, h_b, gate):
        return acc * _gelu(gate), acc * h_b * _gelu_grad(gate)

    g_wpl = _mm_tn("proj_lru_dw", ylru, dpl)
    shp_mix = [w.shape for w in big[1:4]]
    (dh, dxg), sib_mix = _mm_nt("proj_lru_dx", dpl, wpl, [F32, BF16], lru_out_bwd, (h2d, proj),
                                lambda bm, bn: plain_specs(1)(bm, bn) + seg_specs(SEG_GATE)(bm, bn), bm=512,
                                side=_sibling_side([g_wpa, g_wpl, g_wout], kinds[1:4], shp_mix))
    red_pa, red_pl, red_out = reduce_group("mix", kinds[1:4], shp_mix, [g_wpa, g_wpl, g_wout], sib_mix, 3)
    g3, da3 = _scan_bwd(a3, hp3, dh)
    dxc, dwa, dwx, dvec = _gates_bwd(g3, da3, xc, wa16, wx16, vec8)
    dxr, dconv = _conv_bwd(dxc, proj, cw8)
    dproj = _attn_bwd(proj, d, datt, att, lse, slopes, (dxr, dxg, dga, dgl))

    def small_step(tag, grads, ws, ms, vs, like, after):
        n_rows = sum(g.size for g in grads) // LANES
        per_dev = -(-n_rows // (N_DEV * SUBLANES)) * SUBLANES
        packed = lax.optimization_barrier((_pack_rows(grads, N_DEV * per_dev), after))[0]
        total = _all_reduce_small(f"all_reduce_{tag}", packed)
        w_rows = -(-(sum(w.size for w in ws) // LANES) // SUBLANES) * SUBLANES
        upd = _adamw_small(f"adamw_{tag}", total[:w_rows], _pack_rows(ws, w_rows), _pack_rows(ms, w_rows), _pack_rows(vs, w_rows))
        return _unpack_rows(total, like), [_unpack_rows(t, ws) for t in upd]

    early_w = [conv_b, lru_wa, lru_ba, lru_wx, lru_bx, lru_lambda, norm_mlp_g, norm_final_g]
    early_m = [m_conv_b, m_lru_wa, m_lru_ba, m_lru_wx, m_lru_bx, m_lru_lambda, m_norm_mlp_g, m_norm_final_g]
    early_v = [v_conv_b, v_lru_wa, v_lru_ba, v_lru_wx, v_lru_bx, v_lru_lambda, v_norm_mlp_g, v_norm_final_g]
    early_g = [dconv[CONV_TAPS:CONV_TAPS + 1], dwa, dvec[VEC_BA:VEC_BA + 1], dwx, dvec[VEC_BX:VEC_BX + 1],
               dvec[VEC_LAM:VEC_LAM + 1], dg2, dg3, dconv[0:CONV_TAPS]]
    dproj = lax.optimization_barrier((dproj, red_up[1], red_down[1]))[0]
    early_sum, early_upd = small_step("small", early_g, early_w, early_m, early_v,
                                      early_w + [jax.ShapeDtypeStruct((1, CONV_TAPS, d), F32)], dproj)
    g_cw_full = early_sum[-1]
    cshard = conv_w.shape[2]
    g_cw = lax.dynamic_slice(g_cw_full, (0, 0, dev * cshard), (1, CONV_TAPS, cshard))
    cw_delta, cw_m, cw_v = (t[:CONV_TAPS][None] for t in _adamw_small(
        "adamw_conv_w", pad_taps(g_cw[0]), pad_taps(conv_w[0]), pad_taps(m_conv_w[0]), pad_taps(v_conv_w[0])))
    dproj = lax.optimization_barrier((dproj, early_sum, red_pa[1], red_pl[1], red_out[1]))[0]
    half = (big[0].shape[0] // 2, big[0].shape[1])
    g_in0 = _mm_tn("proj_in_dw_0", xn, dproj, part=(0, 2))
    g_in1, sib_in0 = _mm_tn("proj_in_dw_1", xn, dproj, part=(1, 2), side=_sibling_side([g_in0], ["col"], [half]))
    red_in = reduce_group("in_0", ["col"], [half], [g_in0], sib_in0, 4)
    dproj = lax.optimization_barrier((dproj, red_in[0][0]))[0]
    (dxn0,), sib_in1 = _mm_nt("proj_in_dx_0", dproj, win, [F32], part=(0, 2), side=_sibling_side([g_in1], ["col"], [half]))
    red_in += reduce_group("in_1", ["col"], [half], [g_in1], sib_in1, 6)
    dproj = lax.optimization_barrier((dproj, red_in[1][0]))[0]
    dxn1 = _mm_nt("proj_in_dx_1", dproj, win, [F32], part=(1, 2))[0]
    dxn = jnp.concatenate([dxn0, dxn1], axis=0)
    dxn = lax.optimization_barrier((dxn, red_in[0][1]))[0]
    grad_x, _, dg1 = _rms_bwd("norm_mix_bwd", xs, norm_mix_g, dxn, dh1)
    red_up, red_down = lax.optimization_barrier(((red_up, red_down), dg1))[0]
    big_out = {i: _adamw_shard(f"adamw_{i}", [red], big[i], big_m[i], big_v[i], chip) for i, red in ((4, red_up), (5, red_down))}
    big_out.update({i: _adamw_shard(f"adamw_{i}", [red], big[i], big_m[i], big_v[i], chip) for i, red in ((1, red_pa), (2, red_pl), (3, red_out))})
    late_sum, late_upd = small_step("norm_mix", [dg1], [norm_mix_g], [m_norm_mix_g], [v_norm_mix_g], [norm_mix_g], (big_out[4], big_out[5]))
    big_out[0] = _adamw_shard("adamw_0", red_in, big[0], big_m[0], big_v[0], chip)
    s_grad = late_sum + early_sum[:-1]
    s_delta, s_m, s_v = (late_upd[j] + early_upd[j] for j in range(3))


    names = ["norm_mix_g", "w_in", "conv_w", "conv_b", "lru_wa", "lru_ba", "lru_wx", "lru_bx", "lru_lambda", "w_proj_attn", "w_proj_lru",
             "w_out", "norm_mlp_g", "w_up", "w_down", "norm_final_g"]
    small_names = ["norm_mix_g", "conv_b", "lru_wa", "lru_ba", "lru_wx", "lru_bx", "lru_lambda", "norm_mlp_g", "norm_final_g"]
    big_names = ["w_in", "w_proj_attn", "w_proj_lru", "w_out", "w_up", "w_down"]
    res = {"conv_w": (g_cw, cw_delta, cw_m, cw_v)}
    for i, nm in enumerate(small_names):
        res[nm] = (s_grad[i], s_delta[i], s_m[i], s_v[i])
    for i, nm in enumerate(big_names):
        res[nm] = tuple(t[None] for t in big_out[i])
    return (loss, grad_x[None], *[res[nm][0] for nm in names], *[res[nm][1] for nm in names],
            *[res[nm][2] for nm in names], *[res[nm][3] for nm in names])
```

```python
import jax
import jax.numpy as jnp
from jax import lax
from jax.experimental import pallas as pl
from jax.experimental.pallas import tpu as pltpu
from jax.experimental.pallas import tpu_sc as plsc

F32, BF16 = jnp.float32, jnp.bfloat16
MESH = pl.DeviceIdType.MESH
HBM = pl.BlockSpec(memory_space=pltpu.HBM)
N_DEV = 8
N_CHIP = 4
HEAD = 128
SPAN = 128
DILATIONS = (1, 4, 16)
CONV_TAPS = 4
LRU_C = 8.0
NORM_EPS = 1e-6
LANES = 128
SUBLANES = 8
VMEM_LIMIT = 56 * 1024 * 1024
ADAM_LR, ADAM_B1, ADAM_B2, ADAM_EPS, ADAM_WD, ADAM_STEP = 0.001, 0.9, 0.999, 1e-08, 0.01, 10
ADAM_C1 = 1.0 - ADAM_B1 ** ADAM_STEP
ADAM_C2 = 1.0 - ADAM_B2 ** ADAM_STEP
NEG = -1e30


def _params(sem=None):
    return pltpu.CompilerParams(dimension_semantics=sem, vmem_limit_bytes=VMEM_LIMIT)


def _sigmoid(v):
    return 1.0 / (1.0 + jnp.exp(-v))


def _gelu(v):
    k = 0.7978845608028654
    return 0.5 * v * (1.0 + jnp.tanh(k * (v + 0.044715 * v * v * v)))


def _gelu_grad(v):
    k = 0.7978845608028654
    t = jnp.tanh(k * (v + 0.044715 * v * v * v))
    return 0.5 * (1.0 + t) + 0.5 * v * (1.0 - t * t) * k * (1.0 + 3.0 * 0.044715 * v * v)


NN = (((1,), (0,)), ((), ()))
NT = (((1,), (1,)), ((), ()))
TN = (((0,), (0,)), ((), ()))


def _mm(name, a, a_spec, b, b_spec, dn, grid, out_shapes, out_specs, acc_block, epilogue=None, extras=(), extra_specs=(), side=None):
    nk, ne, no = grid[2], len(extras), len(out_shapes)
    side_ops, side_shapes, side_copies, make_copies = side if side is not None else ((), (), 0, None)
    ns_in, ns_out = len(side_ops), len(side_shapes)

    def body(*refs):
        a_ref, b_ref = refs[0], refs[1]
        ex, side_in = refs[2:2 + ne], refs[2 + ne:2 + ne + ns_in]
        outs = refs[2 + ne + ns_in:2 + ne + ns_in + no]
        side_out = refs[2 + ne + ns_in + no:2 + ne + ns_in + no + ns_out]
        scratch = refs[2 + ne + ns_in + no + ns_out:]
        at = [pl.program_id(ax) for ax in range(3)]
        if side is not None:
            @pl.when((at[0] == 0) & (at[1] == 0) & (at[2] == 0))
            def _():
                for cp in make_copies(side_in, side_out, scratch[-2], scratch[-1]):
                    cp.start()

        part = lax.dot_general(a_ref[...], b_ref[...], dn, preferred_element_type=F32)

        def finish(acc):
            vals = epilogue(acc, *[e[...] for e in ex]) if epilogue is not None else (acc,)
            for o, v in zip(outs, vals):
                o[...] = v.astype(o.dtype)

        if nk == 1:
            finish(part)
        else:
            acc_ref, k = scratch[0], at[2]

            @pl.when(k == 0)
            def _():
                acc_ref[...] = part

            @pl.when(k > 0)
            def _():
                acc_ref[...] += part

            @pl.when(k == nk - 1)
            def _():
                finish(acc_ref[...])

        if side is not None:
            @pl.when((at[0] == grid[0] - 1) & (at[1] == grid[1] - 1) & (at[2] == grid[2] - 1))
            def _():
                for cp in make_copies(side_in, side_out, scratch[-2], scratch[-1]):
                    cp.wait()

    scratch_shapes = [pltpu.VMEM(acc_block, F32)] if nk > 1 else []
    if side is not None:
        scratch_shapes += [pltpu.SemaphoreType.DMA((side_copies,)), pltpu.SemaphoreType.DMA((side_copies,))]
    res = pl.pallas_call(
        body, out_shape=[*out_shapes, *side_shapes], grid=grid, in_specs=[a_spec, b_spec, *extra_specs, *[HBM] * ns_in],
        out_specs=[*out_specs, *[HBM] * ns_out], scratch_shapes=scratch_shapes,
        compiler_params=_params(("arbitrary",) * 3 if side is not None else ("parallel", "parallel", "arbitrary")),
        name=name)(a, b, *extras, *side_ops)
    return res if side is None else (res[:no], res[no:])


def _blk(n, pref):
    return pref if n % pref == 0 else n


def _kblk(k):
    return k if k <= 2048 else next(b for b in (2048, 1024, 512) if k % b == 0)


def _mm_fwd(name, a, w, col0, ncols, out_dtypes, epilogue=None, extras=(), extra_specs_fn=None, seg_out=None, bm=1024, bn=1024):
    m, k = a.shape
    bm, bn = _blk(m, bm), _blk(ncols, bn)
    bk = _kblk(k)
    nk = k // bk
    cb0 = col0 // bn
    grid = (m // bm, ncols // bn, nk)
    a_spec = pl.BlockSpec((bm, bk), lambda i, j, kk: (i, kk))
    b_spec = pl.BlockSpec((bk, bn), lambda i, j, kk: (kk, cb0 + j))
    if seg_out is None:
        shapes = [jax.ShapeDtypeStruct((m, ncols), dt) for dt in out_dtypes]
        specs = [pl.BlockSpec((bm, bn), lambda i, j, kk: (i, j)) for _ in out_dtypes]
    else:
        per = seg_out // bn
        shapes = [jax.ShapeDtypeStruct((ncols // seg_out, m, seg_out), dt) for dt in out_dtypes]
        specs = [pl.BlockSpec((None, bm, bn), lambda i, j, kk: (j // per, i, j % per)) for _ in out_dtypes]
    ex_specs = extra_specs_fn(bm, bn) if extra_specs_fn else ()
    return _mm(name, a, a_spec, w, b_spec, NN, grid, shapes, specs, (bm, bn), epilogue, extras, ex_specs)


def _mm_nt(name, a, w, out_dtypes, epilogue=None, extras=(), extra_specs_fn=None, part=(0, 1), side=None, bm=1024, bn=1024):
    n = w.shape[0]
    if a.ndim == 3:
        seg_cols, m, k = a.shape[2], a.shape[1], a.shape[0] * a.shape[2]
    else:
        m, k = a.shape
    m = m // part[1]
    bm, bn = _blk(m, bm), _blk(n, bn)
    bk = _kblk(k)
    grid = (m // bm, n // bn, k // bk)
    i0 = part[0] * (m // bm)
    if a.ndim == 3:
        per = seg_cols // bk
        a_spec = pl.BlockSpec((None, bm, bk), lambda i, j, kk: (kk // per, i0 + i, kk % per))
    else:
        a_spec = pl.BlockSpec((bm, bk), lambda i, j, kk: (i0 + i, kk))
    b_spec = pl.BlockSpec((bn, bk), lambda i, j, kk: (j, kk))
    shapes = [jax.ShapeDtypeStruct((m, n), dt) for dt in out_dtypes]
    specs = [pl.BlockSpec((bm, bn), lambda i, j, kk: (i, j)) for _ in out_dtypes]
    ex_specs = extra_specs_fn(bm, bn) if extra_specs_fn else ()
    return _mm(name, a, a_spec, w, b_spec, NT, grid, shapes, specs, (bm, bn), epilogue, extras, ex_specs, side)


def _mm_tn(name, a, b, part=(0, 1), side=None, bm=1024, bn=1024):
    t, m = a.shape
    n = b.shape[1] if b.ndim == 2 else b.shape[0] * b.shape[2]
    m = m // part[1]
    bm, bn = _blk(m, bm), _blk(n, bn)
    grid = (m // bm, n // bn, 1)
    i0 = part[0] * (m // bm)
    a_spec = pl.BlockSpec((t, bm), lambda i, j, kk: (0, i0 + i))
    if b.ndim == 3:
        per = b.shape[2] // bn
        b_spec = pl.BlockSpec((None, t, bn), lambda i, j, kk: (j // per, 0, j % per))
    else:
        b_spec = pl.BlockSpec((t, bn), lambda i, j, kk: (0, j))
    res = _mm(name, a, a_spec, b, b_spec, TN, grid, [jax.ShapeDtypeStruct((m, n), BF16)],
              [pl.BlockSpec((bm, bn), lambda i, j, kk: (i, j))], (bm, bn), side=side)
    return res[0] if side is None else (res[0][0], res[1])


ROWS = 256


def _row_spec(d):
    return pl.BlockSpec((ROWS, d), lambda i: (i, 0))


def _vec_spec(d, rows=1):
    return pl.BlockSpec((rows, d), lambda i: (0, 0))


def _rms_fwd(name, x, g):
    s, d = x.shape

    def body(x_ref, g_ref, o_ref):
        xv = x_ref[...]
        r = lax.rsqrt(jnp.mean(xv * xv, axis=-1, keepdims=True) + NORM_EPS)
        o_ref[...] = (xv * r * g_ref[...]).astype(BF16)

    return pl.pallas_call(body, out_shape=jax.ShapeDtypeStruct((s, d), BF16), grid=(s // ROWS,),
                          in_specs=[_row_spec(d), _vec_spec(d)], out_specs=_row_spec(d),
                          compiler_params=_params(("parallel",)), name=name)(x, g)


def _rms_bwd_math(xv, g, dy):
    r = lax.rsqrt(jnp.mean(xv * xv, axis=-1, keepdims=True) + NORM_EPS)
    n = xv * r
    z = dy * g
    dx = r * (z - n * jnp.mean(z * n, axis=-1, keepdims=True))
    return dx, jnp.sum(dy * n, axis=0, keepdims=True)


def _rms_bwd(name, x, g, dy, resid):
    s, d = x.shape

    def body(x_ref, g_ref, dy_ref, r_ref, dx_ref, dxb_ref, dg_ref):
        dx, dg = _rms_bwd_math(x_ref[...], g_ref[...], dy_ref[...])
        dx = dx + r_ref[...]
        dx_ref[...] = dx
        dxb_ref[...] = dx.astype(BF16)

        @pl.when(pl.program_id(0) == 0)
        def _():
            dg_ref[...] = jnp.zeros_like(dg_ref)

        dg_ref[...] += dg

    return pl.pallas_call(
        body, out_shape=[jax.ShapeDtypeStruct((s, d), F32), jax.ShapeDtypeStruct((s, d), BF16), jax.ShapeDtypeStruct((1, d), F32)],
        grid=(s // ROWS,), in_specs=[_row_spec(d), _vec_spec(d), _row_spec(d), _row_spec(d)],
        out_specs=[_row_spec(d), _row_spec(d), _vec_spec(d)], compiler_params=_params(("arbitrary",)), name=name)(x, g, dy, resid)


def _final_loss(h2, tgt, g):
    s, d = h2.shape

    def body(x_ref, t_ref, g_ref, dx_ref, dxb_ref, dg_ref, ls_ref):
        xv, gv = x_ref[...], g_ref[...]
        r = lax.rsqrt(jnp.mean(xv * xv, axis=-1, keepdims=True) + NORM_EPS)
        diff = xv * r * gv - t_ref[...]
        dx, dg = _rms_bwd_math(xv, gv, diff * (1.0 / d))
        dx_ref[...] = dx
        dxb_ref[...] = dx.astype(BF16)

        @pl.when(pl.program_id(0) == 0)
        def _():
            dg_ref[...] = jnp.zeros_like(dg_ref)
            ls_ref[...] = jnp.zeros_like(ls_ref)

        dg_ref[...] += dg
        ls_ref[...] += jnp.sum(diff * diff, axis=0, keepdims=True)

    return pl.pallas_call(
        body, out_shape=[jax.ShapeDtypeStruct((s, d), F32), jax.ShapeDtypeStruct((s, d), BF16),
                         jax.ShapeDtypeStruct((1, d), F32), jax.ShapeDtypeStruct((1, d), F32)],
        grid=(s // ROWS,), in_specs=[_row_spec(d), _row_spec(d), _vec_spec(d)],
        out_specs=[_row_spec(d), _row_spec(d), _vec_spec(d), _vec_spec(d)],
        compiler_params=_params(("arbitrary",)), name="final_norm_loss")(h2, tgt, g)


ATTN_Q = 128


ATTN_BATCH = 8


def _attn_units(s):
    units = []
    for gi, d in enumerate(DILATIONS):
        for r in range(d):
            for q0 in range(0, s // d, ATTN_Q):
                k0 = max(q0 - SPAN, 0)
                units.append((gi, d, r, q0, k0, q0 + ATTN_Q - k0))
    return units


def _stream_rows(d, r, start, size):
    return pl.ds(r + start * d, size) if d == 1 else pl.ds(r + start * d, size, stride=d)


def _attn_scores(q_ref, k_ref, slope, d, r, q0, k0, nk):
    qrows, krows = _stream_rows(d, r, q0, ATTN_Q), _stream_rows(d, r, k0, nk)
    qb = q_ref[qrows, :].astype(BF16)
    kb = k_ref[krows, :].astype(BF16)
    sc = lax.dot_general(qb, kb, NT, preferred_element_type=F32) * (HEAD ** -0.5)
    qi = lax.broadcasted_iota(jnp.int32, (ATTN_Q, nk), 0)
    kj = lax.broadcasted_iota(jnp.int32, (ATTN_Q, nk), 1)
    dist = (q0 - k0) + qi - kj
    valid = (dist >= 0) & (dist <= SPAN)
    sc = sc - (slope * d) * dist.astype(F32)
    return jnp.where(valid, sc, NEG), valid, qb, kb, qrows, krows


def _attn_fwd(proj, dm, slopes):
    s = proj.shape[0]
    units = _attn_units(s)

    def body(sl_ref, q_ref, k_ref, v_ref, att_ref, lse_ref, *scr):
        o_scr, l_scr = scr[:3], scr[3:]
        slope = sl_ref[pl.program_id(0)]
        for first in range(0, len(units), ATTN_BATCH):
            batch = units[first:first + ATTN_BATCH]
            scored = [_attn_scores(q_ref, k_ref, slope, d, r, q0, k0, nk) for _, d, r, q0, k0, nk in batch]
            soft = []
            for sc, _, _, _, _, _ in scored:
                m = jnp.max(sc, axis=-1, keepdims=True)
                p = jnp.exp(sc - m)
                soft.append((m, p, jnp.sum(p, axis=-1, keepdims=True)))
            outs = [lax.dot_general(p.astype(BF16), v_ref[sco[5], :].astype(BF16), NN, preferred_element_type=F32)
                    for (m, p, l), sco in zip(soft, scored)]
            for (gi, *_), (m, p, l), sco, o in zip(batch, soft, scored, outs):
                o_scr[gi][sco[4], :] = o / l
                l_scr[gi][sco[4], :] = jnp.broadcast_to(m + jnp.log(l), (ATTN_Q, HEAD))
        l0, l1, l2 = l_scr[0][...], l_scr[1][...], l_scr[2][...]
        m = jnp.maximum(jnp.maximum(l0, l1), l2)
        w0, w1, w2 = jnp.exp(l0 - m), jnp.exp(l1 - m), jnp.exp(l2 - m)
        tot = w0 + w1 + w2
        att_ref[...] = ((w0 * o_scr[0][...] + w1 * o_scr[1][...] + w2 * o_scr[2][...]) / tot).astype(BF16)
        lse_ref[...] = m + jnp.log(tot)

    def seg(i):
        return pl.BlockSpec((s, HEAD), lambda h: (0, i * (dm // HEAD) + h))

    col = pl.BlockSpec((s, HEAD), lambda h: (0, h))
    return pl.pallas_call(
        body, out_shape=[jax.ShapeDtypeStruct((s, dm), BF16), jax.ShapeDtypeStruct((s, dm), F32)], grid=(dm // HEAD,),
        in_specs=[pl.BlockSpec(memory_space=pltpu.SMEM), seg(0), seg(1), seg(2)], out_specs=[col, col],
        scratch_shapes=[pltpu.VMEM((s, HEAD), F32)] * (2 * len(DILATIONS)),
        compiler_params=_params(("parallel",)), name="attn_fwd")(slopes, proj, proj, proj)


def _attn_bwd(proj, dm, datt, att, lse, slopes, others):
    s = proj.shape[0]
    units = _attn_units(s)

    def body(sl_ref, q_ref, k_ref, v_ref, do_ref, att_ref, lse_ref, o3, o4, o5, o6, out_ref, dq_scr, dk_scr, dv_scr, dl_scr):
        slope = sl_ref[pl.program_id(0)]
        delta = jnp.sum(do_ref[...] * att_ref[...].astype(F32), axis=-1, keepdims=True)
        dl_scr[...] = jnp.broadcast_to(delta, (s, HEAD))
        dq_scr[...] = jnp.zeros_like(dq_scr)
        dk_scr[...] = jnp.zeros_like(dk_scr)
        dv_scr[...] = jnp.zeros_like(dv_scr)
        for first in range(0, len(units), ATTN_BATCH):
            scored = [_attn_scores(q_ref, k_ref, slope, d, r, q0, k0, nk) for _, d, r, q0, k0, nk in units[first:first + ATTN_BATCH]]
            dobs = [do_ref[sco[4], :].astype(BF16) for sco in scored]
            dps = [lax.dot_general(dob, v_ref[sco[5], :].astype(BF16), NT, preferred_element_type=F32) for dob, sco in zip(dobs, scored)]
            ps = [jnp.where(sco[1], jnp.exp(sco[0] - lse_ref[sco[4], :][:, 0:1]), 0.0) for sco in scored]
            dss = [(p * (dp - dl_scr[sco[4], :][:, 0:1]) * (HEAD ** -0.5)).astype(BF16) for p, dp, sco in zip(ps, dps, scored)]
            dqs = [lax.dot_general(ds, sco[3], NN, preferred_element_type=F32) for ds, sco in zip(dss, scored)]
            dks = [lax.dot_general(ds, sco[2], TN, preferred_element_type=F32) for ds, sco in zip(dss, scored)]
            dvs = [lax.dot_general(p.astype(BF16), dob, TN, preferred_element_type=F32) for p, dob in zip(ps, dobs)]
            for sco, dq, dk, dv in zip(scored, dqs, dks, dvs):
                dq_scr[sco[4], :] += dq
                dk_scr[sco[5], :] += dk
                dv_scr[sco[5], :] += dv
        for j, scr in enumerate((dq_scr, dk_scr, dv_scr)):
            out_ref[j] = scr[...].astype(BF16)
        for j, other in enumerate((o3, o4, o5, o6)):
            out_ref[3 + j] = other[...]

    def seg(i):
        return pl.BlockSpec((s, HEAD), lambda h: (0, i * (dm // HEAD) + h))

    col = pl.BlockSpec((s, HEAD), lambda h: (0, h))
    return pl.pallas_call(
        body, out_shape=jax.ShapeDtypeStruct((7, s, dm), BF16), grid=(dm // HEAD,),
        in_specs=[pl.BlockSpec(memory_space=pltpu.SMEM), seg(0), seg(1), seg(2), col, col, col, col, col, col, col],
        out_specs=pl.BlockSpec((7, s, HEAD), lambda h: (0, 0, h)), scratch_shapes=[pltpu.VMEM((s, HEAD), F32)] * 4,
        compiler_params=_params(("parallel",)), name="attn_bwd")(slopes, proj, proj, proj, datt, att, lse, *others)


VEC_CB, VEC_BA, VEC_BX, VEC_LAM = 0, 1, 2, 3
SEG_Q, SEG_K, SEG_V, SEG_X, SEG_GATE, SEG_GA, SEG_GL = range(7)


def _to_3d(ref3, val):
    lw = val.shape[1] // SUBLANES
    for j in range(SUBLANES):
        ref3[:, j, :] = val[:, j * lw:(j + 1) * lw]


def _from_3d(ref3):
    return jnp.concatenate([ref3[:, j, :] for j in range(SUBLANES)], axis=1)


def _softplus(z):
    return jnp.maximum(z, 0.0) + jnp.log1p(jnp.exp(-jnp.abs(z)))


def _gate_math(xc, wa_ref, wx_ref, vec):
    xcb = xc.astype(BF16)
    nh = xc.shape[1] // HEAD
    pre_a = jnp.concatenate([jnp.dot(xcb[:, h * HEAD:(h + 1) * HEAD], wa_ref[h], preferred_element_type=F32) for h in range(nh)], axis=1)
    pre_x = jnp.concatenate([jnp.dot(xcb[:, h * HEAD:(h + 1) * HEAD], wx_ref[h], preferred_element_type=F32) for h in range(nh)], axis=1)
    ra = _sigmoid(pre_a + vec[VEC_BA:VEC_BA + 1])
    ig = _sigmoid(pre_x + vec[VEC_BX:VEC_BX + 1])
    sp = _softplus(-vec[VEC_LAM:VEC_LAM + 1])
    log_a = -LRU_C * ra * sp
    a = jnp.exp(log_a)
    z = 2.0 * log_a
    one_minus_a2 = jnp.where(z > -0.01, -z * (1.0 + z * (0.5 + z * (1.0 / 6.0))), 1.0 - jnp.exp(z))
    mult = jnp.sqrt(one_minus_a2)
    return dict(xcb=xcb, ra=ra, ig=ig, sp=sp, a=a, mult=mult)


def _conv_pad_prev(pad_ref, cur, halo, first):
    pad_ref[0:SUBLANES, :] = jnp.where(first, 0.0, halo)
    pad_ref[SUBLANES:SUBLANES + cur.shape[0], :] = cur


def _gates_fwd(proj, d, cw8, vec8, wa, wx):
    s = proj.shape[0]
    hb = ROWS // SUBLANES

    def body(x_ref, halo_ref, cw_ref, vec_ref, wa_ref, wx_ref, a_ref, u_ref, xc_ref, pad):
        _conv_pad_prev(pad, x_ref[...], halo_ref[...], pl.program_id(0) == 0)
        vec = vec_ref[...]
        xc = vec[VEC_CB:VEC_CB + 1]
        for k in range(CONV_TAPS):
            xc = xc + cw_ref[k:k + 1, :] * pad[pl.ds(SUBLANES - (CONV_TAPS - 1) + k, ROWS), :]
        gm = _gate_math(xc, wa_ref, wx_ref, vec)
        xc_ref[...] = xc
        a_ref[...] = gm["a"]
        u_ref[...] = gm["mult"] * (gm["ig"] * xc)

    wspec = pl.BlockSpec(wa.shape, lambda i: (0, 0, 0))
    return pl.pallas_call(
        body, out_shape=[jax.ShapeDtypeStruct((s, d), F32)] * 3, grid=(s // ROWS,),
        in_specs=[pl.BlockSpec((ROWS, d), lambda i: (i, SEG_X)),
                  pl.BlockSpec((SUBLANES, d), lambda i: (jnp.maximum(i * hb - 1, 0), SEG_X)),
                  _vec_spec(d, SUBLANES), _vec_spec(d, SUBLANES), wspec, wspec],
        out_specs=[_row_spec(d)] * 3, scratch_shapes=[pltpu.VMEM((ROWS + SUBLANES, d), F32)],
        compiler_params=_params(("parallel",)), name="lru_gates_fwd")(proj, proj, cw8, vec8, wa, wx)


def _scan_fwd(a, u, proj):
    s, d = a.shape
    lw = d // SUBLANES

    def body(a_ref, u_ref, g_ref, hp_ref, h2_ref, y_ref, a3, u3, h3, carry):
        @pl.when(pl.program_id(0) == 0)
        def _():
            carry[...] = jnp.zeros_like(carry)

        _to_3d(a3, a_ref[...])
        _to_3d(u3, u_ref[...])

        def step(t, h):
            hp_ref[t] = h
            hn = a3[t] * h + u3[t]
            h3[t] = hn
            return hn

        carry[...] = lax.fori_loop(0, ROWS, step, carry[...], unroll=8)
        h = _from_3d(h3)
        h2_ref[...] = h
        y_ref[...] = (h * _gelu(g_ref[...])).astype(BF16)

    spec3 = pl.BlockSpec((ROWS, SUBLANES, lw), lambda i: (i, 0, 0))
    return pl.pallas_call(
        body, out_shape=[jax.ShapeDtypeStruct((s, SUBLANES, lw), F32), jax.ShapeDtypeStruct((s, d), F32), jax.ShapeDtypeStruct((s, d), BF16)],
        grid=(s // ROWS,), in_specs=[_row_spec(d), _row_spec(d), pl.BlockSpec((ROWS, d), lambda i: (i, SEG_GATE))],
        out_specs=[spec3, _row_spec(d), _row_spec(d)],
        scratch_shapes=[pltpu.VMEM((ROWS, SUBLANES, lw), F32)] * 3 + [pltpu.VMEM((SUBLANES, lw), F32)],
        compiler_params=_params(("arbitrary",)), name="lru_scan_fwd")(a, u, proj)


def _scan_bwd(a, hp3, dh):
    s, d = a.shape
    lw = d // SUBLANES
    nb = s // ROWS

    def body(a_ref, hp_ref, dh_ref, g_ref, da_ref, a3, dh3, g3, da3, carry):
        @pl.when(pl.program_id(0) == 0)
        def _():
            carry[...] = jnp.zeros_like(carry)

        _to_3d(a3, a_ref[...])
        _to_3d(dh3, dh_ref[...])

        def step(j, c):
            t = ROWS - 1 - j
            g = dh3[t] + c
            g3[t] = g
            da3[t] = g * hp_ref[t]
            return a3[t] * g

        carry[...] = lax.fori_loop(0, ROWS, step, carry[...], unroll=8)
        g_ref[...] = _from_3d(g3)
        da_ref[...] = _from_3d(da3)

    rows = pl.BlockSpec((ROWS, d), lambda i: (nb - 1 - i, 0))
    return pl.pallas_call(
        body, out_shape=[jax.ShapeDtypeStruct((s, d), F32)] * 2, grid=(nb,),
        in_specs=[rows, pl.BlockSpec((ROWS, SUBLANES, lw), lambda i: (nb - 1 - i, 0, 0)), rows], out_specs=[rows, rows],
        scratch_shapes=[pltpu.VMEM((ROWS, SUBLANES, lw), F32)] * 4 + [pltpu.VMEM((SUBLANES, lw), F32)],
        compiler_params=_params(("arbitrary",)), name="lru_scan_bwd")(a, hp3, dh)


def _gates_bwd(g, da_in, xc, wa, wx, vec8):
    s, d = xc.shape
    nh = d // HEAD

    def body(g_ref, da_ref, xc_ref, wa_ref, wx_ref, vec_ref, dxc_ref, dwa_ref, dwx_ref, dvec_ref):
        @pl.when(pl.program_id(0) == 0)
        def _():
            dwa_ref[...] = jnp.zeros_like(dwa_ref)
            dwx_ref[...] = jnp.zeros_like(dwx_ref)
            dvec_ref[...] = jnp.zeros_like(dvec_ref)

        xc_v, vec = xc_ref[...], vec_ref[...]
        du, da = g_ref[...], da_ref[...]
        gm = _gate_math(xc_v, wa_ref, wx_ref, vec)
        ra, ig, sp, a, mult = gm["ra"], gm["ig"], gm["sp"], gm["a"], gm["mult"]
        dmult = du * ig * xc_v
        dlog_a = da * a - dmult * (a * a) / mult
        dpre_a = dlog_a * (-LRU_C * sp) * ra * (1.0 - ra)
        dpre_x = du * mult * xc_v * ig * (1.0 - ig)
        dlam = jnp.sum(dlog_a * (-LRU_C * ra), axis=0, keepdims=True) * (-_sigmoid(-vec[VEC_LAM:VEC_LAM + 1]))
        dvec_ref[VEC_BA:VEC_BA + 1, :] += jnp.sum(dpre_a, axis=0, keepdims=True)
        dvec_ref[VEC_BX:VEC_BX + 1, :] += jnp.sum(dpre_x, axis=0, keepdims=True)
        dvec_ref[VEC_LAM:VEC_LAM + 1, :] += dlam
        dab, dxb, xcb = dpre_a.astype(BF16), dpre_x.astype(BF16), gm["xcb"]
        back = []
        for h in range(nh):
            cols = slice(h * HEAD, (h + 1) * HEAD)
            dwa_ref[h] += lax.dot_general(xcb[:, cols], dab[:, cols], TN, preferred_element_type=F32)
            dwx_ref[h] += lax.dot_general(xcb[:, cols], dxb[:, cols], TN, preferred_element_type=F32)
            back.append(lax.dot_general(dab[:, cols], wa_ref[h], NT, preferred_element_type=F32)
                        + lax.dot_general(dxb[:, cols], wx_ref[h], NT, preferred_element_type=F32))
        dxc_ref[...] = du * mult * ig + jnp.concatenate(back, axis=1)

    wspec = pl.BlockSpec(wa.shape, lambda i: (0, 0, 0))
    return pl.pallas_call(
        body, out_shape=[jax.ShapeDtypeStruct((s, d), F32), jax.ShapeDtypeStruct(wa.shape, F32), jax.ShapeDtypeStruct(wa.shape, F32),
                         jax.ShapeDtypeStruct((SUBLANES, d), F32)],
        grid=(s // ROWS,), in_specs=[_row_spec(d), _row_spec(d), _row_spec(d), wspec, wspec, _vec_spec(d, SUBLANES)],
        out_specs=[_row_spec(d), wspec, wspec, _vec_spec(d, SUBLANES)],
        compiler_params=_params(("arbitrary",)), name="lru_gates_bwd")(g, da_in, xc, wa, wx, vec8)


def _conv_bwd(dxc, proj, cw8):
    s, d = dxc.shape
    hb = ROWS // SUBLANES
    last = s // SUBLANES - 1

    def body(dc_ref, dnext_ref, x_ref, xprev_ref, cw_ref, dx_ref, dcw_ref, padd, padx):
        i = pl.program_id(0)

        @pl.when(i == 0)
        def _():
            dcw_ref[...] = jnp.zeros_like(dcw_ref)

        dc = dc_ref[...]
        padd[0:ROWS, :] = dc
        padd[ROWS:ROWS + SUBLANES, :] = jnp.where(i == pl.num_programs(0) - 1, 0.0, dnext_ref[...])
        _conv_pad_prev(padx, x_ref[...], xprev_ref[...], i == 0)
        dx = jnp.zeros_like(dc)
        for k in range(CONV_TAPS):
            dx = dx + cw_ref[k:k + 1, :] * padd[pl.ds(CONV_TAPS - 1 - k, ROWS), :]
            dcw_ref[k:k + 1, :] += jnp.sum(dc * padx[pl.ds(SUBLANES - (CONV_TAPS - 1) + k, ROWS), :], axis=0, keepdims=True)
        dcw_ref[CONV_TAPS:CONV_TAPS + 1, :] += jnp.sum(dc, axis=0, keepdims=True)
        dx_ref[...] = dx.astype(BF16)

    return pl.pallas_call(
        body, out_shape=[jax.ShapeDtypeStruct((s, d), BF16), jax.ShapeDtypeStruct((SUBLANES, d), F32)], grid=(s // ROWS,),
        in_specs=[_row_spec(d), pl.BlockSpec((SUBLANES, d), lambda i: (jnp.minimum((i + 1) * hb, last), 0)),
                  pl.BlockSpec((ROWS, d), lambda i: (i, SEG_X)),
                  pl.BlockSpec((SUBLANES, d), lambda i: (jnp.maximum(i * hb - 1, 0), SEG_X)), _vec_spec(d, SUBLANES)],
        out_specs=[_row_spec(d), _vec_spec(d, SUBLANES)],
        scratch_shapes=[pltpu.VMEM((ROWS + SUBLANES, d), F32), pltpu.VMEM((ROWS + SUBLANES, d), F32)],
        compiler_params=_params(("arbitrary",)), name="lru_conv_bwd")(dxc, dxc, proj, proj, cw8)


def _coords():
    return lax.axis_index("x"), lax.axis_index("y"), lax.axis_index("c")


def _other_chips(x, y):
    return [(1 - x, y), (x, 1 - y), (1 - x, 1 - y)]


def _slab(ref, kind, shard_shape, idx, half=None):
    r, c = shard_shape
    r0, nr = (0, r) if half is None else (half * (r // 2), r // 2)
    if kind == "col":
        return ref.at[pl.ds(r0, nr), pl.ds(pl.multiple_of(idx * c, LANES), c)]
    if kind == "row":
        return ref.at[pl.ds(pl.multiple_of(idx * r, SUBLANES) + r0, nr), :]
    return ref.at[idx, pl.ds(r0, nr), :]


def _full_shape(shard_shape, kind):
    r, c = shard_shape
    return {"col": (r, c * N_DEV), "row": (r * N_DEV, c), "slot": (N_DEV, r, c)}[kind]


def _handshake(peers):
    barrier = pltpu.get_barrier_semaphore()
    for peer in peers:
        pl.semaphore_signal(barrier, inc=1, device_id=peer, device_id_type=MESH)
    pl.semaphore_wait(barrier, len(peers))


def _launch(name, body, out_shape, operands, sems, sequencer_id):
    if sequencer_id is None:
        return pl.pallas_call(body, out_shape=out_shape, in_specs=[HBM] * len(operands), out_specs=[HBM] * len(out_shape),
                              scratch_shapes=sems, name=name)(*operands)
    return pl.kernel(body, out_type=out_shape, mesh=plsc.ScalarSubcoreMesh(axis_name="seq", num_cores=1), name=name,
                     scratch_types=sems, compiler_params=pltpu.CompilerParams(collective_id=sequencer_id))(*operands)


AG_COPIES = 10


def _all_gather(name, shards, kinds, sequencer_id=None):
    n = len(shards)
    shapes = [s.shape for s in shards]

    def body(*refs):
        ins, outs = refs[:n], refs[n:2 * n]
        send_sems, recv_sems, local_sems = refs[2 * n:]
        x, y, c = _coords()
        me, sib, xn, yn, dg = (x, y, c), (x, y, 1 - c), (1 - x, y, c), (x, 1 - y, c), (1 - x, 1 - y, c)
        if sequencer_id is not None:
            _handshake([sib, xn, yn])

        def part(i, dev, half=None):
            return _slab(outs[i], kinds[i], shapes[i], 4 * dev[0] + 2 * dev[1] + dev[2], half)

        def copy(i, k, block, half, to, own=False):
            r = shapes[i][0]
            src = part(i, block, half) if not own else (ins[i] if half is None else ins[i].at[pl.ds(half * (r // 2), r // 2), :])
            return pltpu.make_async_remote_copy(
                src_ref=src, dst_ref=part(i, block, half), send_sem=send_sems.at[AG_COPIES * i + k],
                recv_sem=recv_sems.at[AG_COPIES * i + k], device_id=to, device_id_type=MESH)

        def other_core(dev):
            return (dev[0], dev[1], 1 - c)

        started = []

        def start(cp):
            cp.start()
            started.append(cp)

        for i in range(n):
            start(copy(i, 1, me, 0, xn, own=True))
            start(copy(i, 4, me, 1, yn, own=True))
            start(copy(i, 2, me, 1, xn, own=True))
            start(copy(i, 3, me, 0, yn, own=True))
            start(copy(i, 0, me, None, sib, own=True))
        mine = [pltpu.make_async_copy(ins[i], part(i, me), local_sems.at[i]) for i in range(n)]
        for cp in mine:
            cp.start()
        for i in range(n):
            copy(i, 1, xn, 0, me).wait_recv()
            start(copy(i, 5, xn, 0, yn))
            copy(i, 4, yn, 1, me).wait_recv()
            start(copy(i, 6, yn, 1, xn))
        for i in range(n):
            copy(i, 2, xn, 1, me).wait_recv()
            start(copy(i, 7, xn, None, sib))
            copy(i, 3, yn, 0, me).wait_recv()
            start(copy(i, 8, yn, None, sib))
        for i in range(n):
            copy(i, 5, dg, 0, me).wait_recv()
            copy(i, 6, dg, 1, me).wait_recv()
            start(copy(i, 9, dg, None, sib))
        for i in range(n):
            copy(i, 0, sib, None, me).wait_recv()
            for k, dev in ((7, xn), (8, yn), (9, dg)):
                copy(i, k, other_core(dev), None, me).wait_recv()
        for cp in started:
            cp.wait_send()
        for cp in mine:
            cp.wait()

    out_shape = [jax.ShapeDtypeStruct(_full_shape(s.shape, k), s.dtype) for s, k in zip(shards, kinds)]
    sems = [pltpu.SemaphoreType.DMA((AG_COPIES * n,)), pltpu.SemaphoreType.DMA((AG_COPIES * n,)), pltpu.SemaphoreType.DMA((n,))]
    return _launch(name, body, out_shape, shards, sems, sequencer_id)


def _sibling_copies(kinds, shard_shapes):
    def make(ins, outs, send_sems, recv_sems):
        x, y, c = _coords()
        return [pltpu.make_async_remote_copy(
            src_ref=_slab(ins[i], kinds[i], shard_shapes[i], 2 * q + (1 - c)), dst_ref=outs[i].at[q],
            send_sem=send_sems.at[N_CHIP * i + q], recv_sem=recv_sems.at[N_CHIP * i + q],
            device_id=(x, y, 1 - c), device_id_type=MESH) for i in range(len(ins)) for q in range(N_CHIP)]
    return make


def _sibling_side(partials, kinds, shard_shapes):
    return (partials, [jax.ShapeDtypeStruct((N_CHIP, *s), BF16) for s in shard_shapes], N_CHIP * len(partials),
            _sibling_copies(kinds, shard_shapes))


def _exchange_siblings(name, partials, kinds, shard_shapes, sequencer_id=None):
    n = len(partials)
    make = _sibling_copies(kinds, shard_shapes)

    def body(*refs):
        if sequencer_id is not None:
            x, y, c = _coords()
            _handshake([(x, y, 1 - c)])
        cps = make(refs[:n], refs[n:2 * n], refs[2 * n], refs[2 * n + 1])
        for cp in cps:
            cp.start()
        for cp in cps:
            cp.wait()

    return _launch(name, body, [jax.ShapeDtypeStruct((N_CHIP, *s), BF16) for s in shard_shapes], partials,
                   [pltpu.SemaphoreType.DMA((N_CHIP * n,)), pltpu.SemaphoreType.DMA((N_CHIP * n,))], sequencer_id)


def _exchange_chips(name, chip_sums, sequencer_id=None):
    n = len(chip_sums)

    def body(*refs):
        ins, outs = refs[:n], refs[n:2 * n]
        send_sems, recv_sems = refs[2 * n:]
        x, y, c = _coords()
        if sequencer_id is not None:
            _handshake([(cx, cy, c) for cx, cy in _other_chips(x, y)])
        cps = []
        for i in range(n):
            for k, (cx, cy) in enumerate(_other_chips(x, y)):
                cps.append(pltpu.make_async_remote_copy(
                    src_ref=ins[i].at[2 * cx + cy], dst_ref=outs[i].at[k], send_sem=send_sems.at[3 * i + k],
                    recv_sem=recv_sems.at[3 * i + k], device_id=(cx, cy, c), device_id_type=MESH))
        for cp in cps:
            cp.start()
        for cp in cps:
            cp.wait()

    return _launch(name, body, [jax.ShapeDtypeStruct((3, *t.shape[1:]), BF16) for t in chip_sums], chip_sums,
                   [pltpu.SemaphoreType.DMA((3 * n,)), pltpu.SemaphoreType.DMA((3 * n,))], sequencer_id)


def _all_reduce_small(name, packed):
    rows = packed.shape[0] // N_DEV

    def body(p_ref, out_ref, rb, tot, send_sems, recv_sems):
        x, y, c = _coords()
        me = 4 * x + 2 * y + c

        def peer(k):
            return (x ^ (k >> 2), y ^ ((k >> 1) & 1), c ^ (k & 1))

        def rows_of(idx):
            return pl.ds(pl.multiple_of(idx * rows, SUBLANES), rows)

        def piece(ref, idx):
            return ref.at[rows_of(idx), :]

        scatter = [pltpu.make_async_remote_copy(src_ref=piece(p_ref, me ^ k), dst_ref=rb.at[k], send_sem=send_sems.at[k],
                                                recv_sem=recv_sems.at[k], device_id=peer(k), device_id_type=MESH) for k in range(1, N_DEV)]
        for cp in scatter:
            cp.start()
        acc = p_ref[rows_of(me), :]
        for cp in scatter:
            cp.wait_recv()
        for k in range(1, N_DEV):
            acc = acc + rb[k]
        tot[...] = acc
        out_ref[rows_of(me), :] = acc
        gather = [pltpu.make_async_remote_copy(src_ref=tot, dst_ref=piece(out_ref, me), send_sem=send_sems.at[N_DEV + k],
                                               recv_sem=recv_sems.at[N_DEV + k], device_id=peer(k), device_id_type=MESH)
                  for k in range(1, N_DEV)]
        for cp in gather:
            cp.start()
        for k in range(1, N_DEV):
            pltpu.make_async_remote_copy(src_ref=tot, dst_ref=piece(out_ref, me ^ k), send_sem=send_sems.at[N_DEV + k],
                                         recv_sem=recv_sems.at[N_DEV + k], device_id=peer(k), device_id_type=MESH).wait_recv()
        for cp in scatter + gather:
            cp.wait_send()

    vm = pl.BlockSpec(memory_space=pltpu.VMEM)
    return pl.pallas_call(
        body, out_shape=jax.ShapeDtypeStruct(packed.shape, F32), in_specs=[vm], out_specs=vm,
        scratch_shapes=[pltpu.VMEM((N_DEV, rows, LANES), F32), pltpu.VMEM((rows, LANES), F32),
                        pltpu.SemaphoreType.DMA((2 * N_DEV,)), pltpu.SemaphoreType.DMA((2 * N_DEV,))],
        compiler_params=pltpu.CompilerParams(vmem_limit_bytes=VMEM_LIMIT), name=name)(packed)


def _adamw_math(g, w, m, v):
    m = ADAM_B1 * m + (1.0 - ADAM_B1) * g
    v = ADAM_B2 * v + (1.0 - ADAM_B2) * (g * g)
    delta = -ADAM_LR * ((m / ADAM_C1) / (jnp.sqrt(v / ADAM_C2) + ADAM_EPS) + ADAM_WD * w)
    return delta, m, v


def _slab_spec(kind, shard_shape, tr, slab_of):
    r, c = shard_shape
    if kind == "col":
        return pl.BlockSpec((tr, c), lambda q, i, sc: (i, slab_of(q, sc)))
    return pl.BlockSpec((tr, c), lambda q, i, sc: (slab_of(q, sc) * (r // tr) + i, 0))


def _chip_sum(name, partial, recv, kind, shard_shape, core):
    r, c = shard_shape
    tr = _blk(r, 1024)

    def body(core_ref, p_ref, r_ref, o_ref):
        o_ref[...] = (p_ref[...].astype(F32) + r_ref[...].astype(F32)).astype(BF16)

    spec4 = pl.BlockSpec((None, tr, c), lambda q, i, sc: (q, i, 0))
    grid_spec = pltpu.PrefetchScalarGridSpec(
        num_scalar_prefetch=1, grid=(N_CHIP, r // tr),
        in_specs=[_slab_spec(kind, shard_shape, tr, lambda q, sc: 2 * q + sc[0]), spec4], out_specs=spec4)
    return pl.pallas_call(body, out_shape=jax.ShapeDtypeStruct((N_CHIP, r, c), BF16), grid_spec=grid_spec,
                          compiler_params=_params(("parallel", "parallel")), name=name)(core, partial, recv)


def _adamw_shard(name, parts, w, m, v, chip):
    r, c = w.shape
    n_parts = len(parts)
    tr = _blk(r // n_parts, 256)
    per = r // n_parts // tr

    def body(chip_ref, *refs):
        src, (w_ref, m_ref, v_ref), (g_out, d_out, m_out, v_out) = refs[:2 * n_parts], refs[2 * n_parts:2 * n_parts + 3], refs[2 * n_parts + 3:]
        for p in range(n_parts):
            @pl.when(pl.program_id(0) // per == p)
            def _():
                g = src[2 * p][...].astype(F32)
                for k in range(3):
                    g = g + src[2 * p + 1][k].astype(F32)
                g_out[...] = g
                d_out[...], m_out[...], v_out[...] = _adamw_math(g, w_ref[...], m_ref[...], v_ref[...])

    def part_specs(p):
        at = lambda i: jnp.clip(i - p * per, 0, per - 1)
        return [pl.BlockSpec((None, tr, c), lambda i, sc: (sc[0], at(i), 0)), pl.BlockSpec((3, tr, c), lambda i, sc: (0, at(i), 0))]

    blk = pl.BlockSpec((tr, c), lambda i, sc: (i, 0))
    grid_spec = pltpu.PrefetchScalarGridSpec(
        num_scalar_prefetch=1, grid=(r // tr,), in_specs=[s for p in range(n_parts) for s in part_specs(p)] + [blk, blk, blk], out_specs=[blk] * 4)
    return pl.pallas_call(body, out_shape=[jax.ShapeDtypeStruct((r, c), F32)] * 4, grid_spec=grid_spec,
                          compiler_params=_params(("parallel",)), name=name)(chip, *[a for p in parts for a in p], w, m, v)


def _adamw_small(name, g, w, m, v):
    def body(g_ref, w_ref, m_ref, v_ref, d_out, m_out, v_out):
        d_out[...], m_out[...], v_out[...] = _adamw_math(g_ref[...], w_ref[...], m_ref[...], v_ref[...])

    vm = pl.BlockSpec(memory_space=pltpu.VMEM)
    return pl.pallas_call(body, out_shape=[jax.ShapeDtypeStruct(g.shape, F32)] * 3, in_specs=[vm] * 4, out_specs=[vm] * 3,
                          compiler_params=pltpu.CompilerParams(vmem_limit_bytes=VMEM_LIMIT), name=name)(g, w, m, v)


def _pack_rows(arrays, total_rows):
    flat = [a.reshape(-1, LANES) for a in arrays]
    used = sum(f.shape[0] for f in flat)
    return jnp.concatenate(flat + [jnp.zeros((total_rows - used, LANES), F32)], axis=0)


def _unpack_rows(packed, like):
    out, at = [], 0
    for a in like:
        n = a.size // LANES
        out.append(packed[at:at + n].reshape(a.shape))
        at += n
    return out


def kernel(x, norm_mix_g, w_in, conv_w, conv_b, lru_wa, lru_ba, lru_wx, lru_bx, lru_lambda, w_proj_attn, w_proj_lru, w_out, norm_mlp_g, w_up, w_down, norm_final_g, loss_target, m_norm_mix_g, m_w_in, m_conv_w, m_conv_b, m_lru_wa, m_lru_ba, m_lru_wx, m_lru_bx, m_lru_lambda, m_w_proj_attn, m_w_proj_lru, m_w_out, m_norm_mlp_g, m_w_up, m_w_down, m_norm_final_g, v_norm_mix_g, v_w_in, v_conv_w, v_conv_b, v_lru_wa, v_lru_ba, v_lru_wx, v_lru_bx, v_lru_lambda, v_w_proj_attn, v_w_proj_lru, v_w_out, v_norm_mlp_g, v_w_up, v_w_down, v_norm_final_g):
    xs, tgt = x[0], loss_target[0]
    s, d = xs.shape
    nh = d // HEAD
    ix, iy, ic = _coords()
    core = jnp.reshape(ic, (1,)).astype(jnp.int32)
    chip = jnp.reshape(2 * ix + iy, (1,)).astype(jnp.int32)
    dev = 4 * ix + 2 * iy + ic

    big = [w_in[0], w_proj_attn[0], w_proj_lru[0], w_out[0], w_up[0], w_down[0]]
    big_m = [m_w_in[0], m_w_proj_attn[0], m_w_proj_lru[0], m_w_out[0], m_w_up[0], m_w_down[0]]
    big_v = [v_w_in[0], v_w_proj_attn[0], v_w_proj_lru[0], v_w_out[0], v_w_up[0], v_w_down[0]]
    kinds = ["col", "row", "row", "row", "col", "row"]
    pad_taps = lambda t: jnp.pad(t, ((0, SUBLANES - CONV_TAPS), (0, 0)))
    shards = [w.astype(BF16) for w in big]
    pad_taps2 = lambda t: jnp.pad(t, ((0, 2 * SUBLANES - CONV_TAPS), (0, 0)))
    win, cw_slots = _all_gather("all_gather_w_in", [shards[0], pad_taps2(conv_w[0])], ["col", "slot"])
    later = lax.optimization_barrier((shards[1:], win))[0]
    wpa, wpl, wout = _all_gather("all_gather_mix", later[:3], kinds[1:4], sequencer_id=1)
    wup, wdown = _all_gather("all_gather_mlp", later[3:], kinds[4:], sequencer_id=5)
    cw8 = jnp.transpose(cw_slots[:, :SUBLANES], (1, 0, 2)).reshape(SUBLANES, d)
    row_id = lax.broadcasted_iota(jnp.int32, (SUBLANES, d), 0)
    vec8 = sum(jnp.where(row_id == k, t, 0.0) for k, t in ((VEC_CB, conv_b), (VEC_BA, lru_ba), (VEC_BX, lru_bx), (VEC_LAM, lru_lambda)))
    wa16, wx16 = lru_wa[0].astype(BF16), lru_wx[0].astype(BF16)
    slopes = 2.0 ** (-8.0 * jnp.arange(1, nh + 1, dtype=F32) / nh)

    def seg_specs(*segs):
        return lambda bm, bn: [pl.BlockSpec((bm, bn), (lambda i, j, kk, sg=sg: (i, sg * (d // bn) + j))) for sg in segs]

    def plain_specs(k):
        return lambda bm, bn: [pl.BlockSpec((bm, bn), lambda i, j, kk: (i, j)) for _ in range(k)]

    xn = _rms_fwd("norm_mix", xs, norm_mix_g)
    proj = _mm_fwd("proj_in", xn, win, 0, 7 * d, [F32])[0]
    att, lse = _attn_fwd(proj, d, slopes)
    a3, u3, xc = _gates_fwd(proj, d, cw8, vec8, wa16, wx16)
    hp3, h2d, ylru = _scan_fwd(a3, u3, proj)
    pa = _mm_fwd("proj_attn", att, wpa, 0, d, [F32])[0]

    def merge(acc, pa_b, ga, gl):
        return acc, _sigmoid(ga) * pa_b + _sigmoid(gl) * acc

    plr, merged = _mm_fwd("proj_lru_merge", ylru, wpl, 0, d, [F32, BF16], merge, (pa, proj, proj),
                          lambda bm, bn: plain_specs(1)(bm, bn) + seg_specs(SEG_GA, SEG_GL)(bm, bn), bm=512)
    h1 = _mm_fwd("mix_out", merged, wout, 0, d, [F32], lambda acc, r: (acc + r,), (xs,), plain_specs(1))[0]
    hn = _rms_fwd("norm_mlp", h1, norm_mlp_g)

    def relu2(acc):
        return acc, jnp.square(jnp.maximum(acc, 0.0))

    up, hid = _mm_fwd("mlp_up", hn, wup, 0, wup.shape[1], [BF16, BF16], relu2)
    h2 = _mm_fwd("mlp_down", hid, wdown, 0, d, [F32], lambda acc, r: (acc + r,), (h1,), plain_specs(1))[0]
    dh2, dh2b, dg3, loss_lanes = _final_loss(h2, tgt, norm_final_g.reshape(1, d))
    loss = lax.psum(0.5 / d * jnp.sum(loss_lanes), ("x", "y", "c"))
    dh2b = lax.optimization_barrier((dh2b, loss))[0]

    def reduce_group(tag, kk, shp, partials, from_sibling, sequencer_id):
        sums = [_chip_sum(f"chip_sum_{tag}_{i}", p, f, k, sh, core) for i, (p, f, k, sh) in enumerate(zip(partials, from_sibling, kk, shp))]
        return list(zip(sums, _exchange_chips(f"rs_chips_{tag}", sums, sequencer_id)))

    dup = _mm_nt("mlp_down_dx", dh2b, wdown, [BF16], lambda acc, u: (acc * (2.0 * jnp.maximum(u.astype(F32), 0.0)),), (up,), plain_specs(1))[0]
    g_wdown = _mm_tn("mlp_down_dw", hid, dh2b)
    g_wup = _mm_tn("mlp_up_dw", hn, dup)
    shp_mlp = [w.shape for w in big[4:]]
    (dhn,), sib_mlp = _mm_nt("mlp_up_dx", dup, wup, [F32], side=_sibling_side([g_wup, g_wdown], kinds[4:], shp_mlp))
    red_up, red_down = reduce_group("mlp", kinds[4:], shp_mlp, [g_wup, g_wdown], sib_mlp, 2)
    dhn = lax.optimization_barrier((dhn, red_up[0], red_down[0]))[0]
    dh1, dh1b, dg2 = _rms_bwd("norm_mlp_bwd", h1, norm_mlp_g, dhn, dh2)

    def merge_bwd(acc, pa_b, pl_b, ga, gl):
        sa, sl = _sigmoid(ga), _sigmoid(gl)
        return acc * sa, acc * sl, acc * pa_b * sa * (1.0 - sa), acc * pl_b * sl * (1.0 - sl)

    dpa, dpl, dga, dgl = _mm_nt("mix_out_dx", dh1b, wout, [BF16] * 4, merge_bwd, (pa, plr, proj, proj),
                                lambda bm, bn: plain_specs(2)(bm, bn) + seg_specs(SEG_GA, SEG_GL)(bm, bn), bm=512)
    g_wout = _mm_tn("mix_out_dw", merged, dh1b)
    datt = _mm_nt("proj_attn_dx", dpa, wpa, [F32])[0]
    g_wpa = _mm_tn("proj_attn_dw", att, dpa)

    def lru_out_bwd(acc, h_b, gate):
        return acc * _gelu(gate), acc * h_b * _gelu_grad(gate)

    g_wpl = _mm_tn("proj_lru_dw", ylru, dpl)
    shp_mix = [w.shape for w in big[1:4]]
    (dh, dxg), sib_mix = _mm_nt("proj_lru_dx", dpl, wpl, [F32, BF16], lru_out_bwd, (h2d, proj),
                                lambda bm, bn: plain_specs(1)(bm, bn) + seg_specs(SEG_GATE)(bm, bn), bm=512,
                                side=_sibling_side([g_wpa, g_wpl, g_wout], kinds[1:4], shp_mix))
    red_pa, red_pl, red_out = reduce_group("mix", kinds[1:4], shp_mix, [g_wpa, g_wpl, g_wout], sib_mix, 3)
    g3, da3 = _scan_bwd(a3, hp3, dh)
    dxc, dwa, dwx, dvec = _gates_bwd(g3, da3, xc, wa16, wx16, vec8)
    dxr, dconv = _conv_bwd(dxc, proj, cw8)
    dproj = _attn_bwd(proj, d, datt, att, lse, slopes, (dxr, dxg, dga, dgl))

    def small_step(tag, grads, ws, ms, vs, like, after):
        n_rows = sum(g.size for g in grads) // LANES
        per_dev = -(-n_rows // (N_DEV * SUBLANES)) * SUBLANES
        packed = lax.optimization_barrier((_pack_rows(grads, N_DEV * per_dev), after))[0]
        total = _all_reduce_small(f"all_reduce_{tag}", packed)
        w_rows = -(-(sum(w.size for w in ws) // LANES) // SUBLANES) * SUBLANES
        upd = _adamw_small(f"adamw_{tag}", total[:w_rows], _pack_rows(ws, w_rows), _pack_rows(ms, w_rows), _pack_rows(vs, w_rows))
        return _unpack_rows(total, like), [_unpack_rows(t, ws) for t in upd]

    early_w = [conv_b, lru_wa, lru_ba, lru_wx, lru_bx, lru_lambda, norm_mlp_g, norm_final_g]
    early_m = [m_conv_b, m_lru_wa, m_lru_ba, m_lru_wx, m_lru_bx, m_lru_lambda, m_norm_mlp_g, m_norm_final_g]
    early_v = [v_conv_b, v_lru_wa, v_lru_ba, v_lru_wx, v_lru_bx, v_lru_lambda, v_norm_mlp_g, v_norm_final_g]
    early_g = [dconv[CONV_TAPS:CONV_TAPS + 1], dwa, dvec[VEC_BA:VEC_BA + 1], dwx, dvec[VEC_BX:VEC_BX + 1],
               dvec[VEC_LAM:VEC_LAM + 1], dg2, dg3, dconv[0:CONV_TAPS]]
    dproj = lax.optimization_barrier((dproj, red_up[1], red_down[1]))[0]
    early_sum, early_upd = small_step("small", early_g, early_w, early_m, early_v,
                                      early_w + [jax.ShapeDtypeStruct((1, CONV_TAPS, d), F32)], dproj)
    g_cw_full = early_sum[-1]
    cshard = conv_w.shape[2]
    g_cw = lax.dynamic_slice(g_cw_full, (0, 0, dev * cshard), (1, CONV_TAPS, cshard))
    cw_delta, cw_m, cw_v = (t[:CONV_TAPS][None] for t in _adamw_small(
        "adamw_conv_w", pad_taps(g_cw[0]), pad_taps(conv_w[0]), pad_taps(m_conv_w[0]), pad_taps(v_conv_w[0])))
    dproj = lax.optimization_barrier((dproj, early_sum, red_pa[1], red_pl[1], red_out[1]))[0]
    half = (big[0].shape[0] // 2, big[0].shape[1])
    g_in0 = _mm_tn("proj_in_dw_0", xn, dproj, part=(0, 2))
    g_in1, sib_in0 = _mm_tn("proj_in_dw_1", xn, dproj, part=(1, 2), side=_sibling_side([g_in0], ["col"], [half]))
    red_in = reduce_group("in_0", ["col"], [half], [g_in0], sib_in0, 4)
    dproj = lax.optimization_barrier((dproj, red_in[0][0]))[0]
    (dxn0,), sib_in1 = _mm_nt("proj_in_dx_0", dproj, win, [F32], part=(0, 2), side=_sibling_side([g_in1], ["col"], [half]))
    red_in += reduce_group("in_1", ["col"], [half], [g_in1], sib_in1, 6)
    dproj = lax.optimization_barrier((dproj, red_in[1][0]))[0]
    dxn1 = _mm_nt("proj_in_dx_1", dproj, win, [F32], part=(1, 2))[0]
    dxn = jnp.concatenate([dxn0, dxn1], axis=0)
    dxn = lax.optimization_barrier((dxn, red_in[0][1]))[0]
    grad_x, _, dg1 = _rms_bwd("norm_mix_bwd", xs, norm_mix_g, dxn, dh1)
    red_up, red_down = lax.optimization_barrier(((red_up, red_down), dg1))[0]
    big_out = {i: _adamw_shard(f"adamw_{i}", [red], big[i], big_m[i], big_v[i], chip) for i, red in ((4, red_up), (5, red_down))}
    big_out.update({i: _adamw_shard(f"adamw_{i}", [red], big[i], big_m[i], big_v[i], chip) for i, red in ((1, red_pa), (2, red_pl), (3, red_out))})
    late_sum, late_upd = small_step("norm_mix", [dg1], [norm_mix_g], [m_norm_mix_g], [v_norm_mix_g], [norm_mix_g], (big_out[4], big_out[5]))
    big_out[0] = _adamw_shard("adamw_0", red_in, big[0], big_m[0], big_v[0], chip)
    s_grad = late_sum + early_sum[:-1]
    s_delta, s_m, s_v = (late_upd[j] + early_upd[j] for j in range(3))


    names = ["norm_mix_g", "w_in", "conv_w", "conv_b", "lru_wa", "lru_ba", "lru_wx", "lru_bx", "lru_lambda", "w_proj_attn", "w_proj_lru",
             "w_out", "norm_mlp_g", "w_up", "w_down", "norm_final_g"]
    small_names = ["norm_mix_g", "conv_b", "lru_wa", "lru_ba", "lru_wx", "lru_bx", "lru_lambda", "norm_mlp_g", "norm_final_g"]
    big_names = ["w_in", "w_proj_attn", "w_proj_lru", "w_out", "w_up", "w_down"]
    res = {"conv_w": (g_cw, cw_delta, cw_m, cw_v)}
    for i, nm in enumerate(small_names):
        res[nm] = (s_grad[i], s_delta[i], s_m[i], s_v[i])
    for i, nm in enumerate(big_names):
        res[nm] = tuple(t[None] for t in big_out[i])
    return (loss, grad_x[None], *[res[nm][0] for nm in names], *[res[nm][1] for nm in names],
            *[res[nm][2] for nm in names], *[res[nm][3] for nm in names])
```

```python
import jax
import jax.numpy as jnp
from jax import lax
from jax.experimental import pallas as pl
from jax.experimental.pallas import tpu as pltpu
from jax.experimental.pallas import tpu_sc as plsc

F32, BF16 = jnp.float32, jnp.bfloat16
MESH = pl.DeviceIdType.MESH
HBM = pl.BlockSpec(memory_space=pltpu.HBM)
N_DEV = 8
N_CHIP = 4
HEAD = 128
SPAN = 128
DILATIONS = (1, 4, 16)
CONV_TAPS = 4
LRU_C = 8.0
NORM_EPS = 1e-6
LANES = 128
SUBLANES = 8
VMEM_LIMIT = 56 * 1024 * 1024
ADAM_LR, ADAM_B1, ADAM_B2, ADAM_EPS, ADAM_WD, ADAM_STEP = 0.001, 0.9, 0.999, 1e-08, 0.01, 10
ADAM_C1 = 1.0 - ADAM_B1 ** ADAM_STEP
ADAM_C2 = 1.0 - ADAM_B2 ** ADAM_STEP
NEG = -1e30


def _params(sem=None):
    return pltpu.CompilerParams(dimension_semantics=sem, vmem_limit_bytes=VMEM_LIMIT)


def _sigmoid(v):
    return 1.0 / (1.0 + jnp.exp(-v))


def _gelu(v):
    k = 0.7978845608028654
    return 0.5 * v * (1.0 + jnp.tanh(k * (v + 0.044715 * v * v * v)))


def _gelu_grad(v):
    k = 0.7978845608028654
    t = jnp.tanh(k * (v + 0.044715 * v * v * v))
    return 0.5 * (1.0 + t) + 0.5 * v * (1.0 - t * t) * k * (1.0 + 3.0 * 0.044715 * v * v)


NN = (((1,), (0,)), ((), ()))
NT = (((1,), (1,)), ((), ()))
TN = (((0,), (0,)), ((), ()))


def _mm(name, a, a_spec, b, b_spec, dn, grid, out_shapes, out_specs, acc_block, epilogue=None, extras=(), extra_specs=(), side=None):
    nk, ne, no = grid[2], len(extras), len(out_shapes)
    side_ops, side_shapes, side_copies, make_copies = side if side is not None else ((), (), 0, None)
    ns_in, ns_out = len(side_ops), len(side_shapes)

    def body(*refs):
        a_ref, b_ref = refs[0], refs[1]
        ex, side_in = refs[2:2 + ne], refs[2 + ne:2 + ne + ns_in]
        outs = refs[2 + ne + ns_in:2 + ne + ns_in + no]
        side_out = refs[2 + ne + ns_in + no:2 + ne + ns_in + no + ns_out]
        scratch = refs[2 + ne + ns_in + no + ns_out:]
        at = [pl.program_id(ax) for ax in range(3)]
        if side is not None:
            @pl.when((at[0] == 0) & (at[1] == 0) & (at[2] == 0))
            def _():
                for cp in make_copies(side_in, side_out, scratch[-2], scratch[-1]):
                    cp.start()

        part = lax.dot_general(a_ref[...], b_ref[...], dn, preferred_element_type=F32)

        def finish(acc):
            vals = epilogue(acc, *[e[...] for e in ex]) if epilogue is not None else (acc,)
            for o, v in zip(outs, vals):
                o[...] = v.astype(o.dtype)

        if nk == 1:
            finish(part)
        else:
            acc_ref, k = scratch[0], at[2]

            @pl.when(k == 0)
            def _():
                acc_ref[...] = part

            @pl.when(k > 0)
            def _():
                acc_ref[...] += part

            @pl.when(k == nk - 1)
            def _():
                finish(acc_ref[...])

        if side is not None:
            @pl.when((at[0] == grid[0] - 1) & (at[1] == grid[1] - 1) & (at[2] == grid[2] - 1))
            def _():
                for cp in make_copies(side_in, side_out, scratch[-2], scratch[-1]):
                    cp.wait()

    scratch_shapes = [pltpu.VMEM(acc_block, F32)] if nk > 1 else []
    if side is not None:
        scratch_shapes += [pltpu.SemaphoreType.DMA((side_copies,)), pltpu.SemaphoreType.DMA((side_copies,))]
    res = pl.pallas_call(
        body, out_shape=[*out_shapes, *side_shapes], grid=grid, in_specs=[a_spec, b_spec, *extra_specs, *[HBM] * ns_in],
        out_specs=[*out_specs, *[HBM] * ns_out], scratch_shapes=scratch_shapes,
        compiler_params=_params(("arbitrary",) * 3 if side is not None else ("parallel", "parallel", "arbitrary")),
        name=name)(a, b, *extras, *side_ops)
    return res if side is None else (res[:no], res[no:])


def _blk(n, pref):
    return pref if n % pref == 0 else n


def _kblk(k):
    return k if k <= 2048 else next(b for b in (2048, 1024, 512) if k % b == 0)


def _mm_fwd(name, a, w, col0, ncols, out_dtypes, epilogue=None, extras=(), extra_specs_fn=None, seg_out=None, bm=1024, bn=1024):
    m, k = a.shape
    bm, bn = _blk(m, bm), _blk(ncols, bn)
    bk = _kblk(k)
    nk = k // bk
    cb0 = col0 // bn
    grid = (m // bm, ncols // bn, nk)
    a_spec = pl.BlockSpec((bm, bk), lambda i, j, kk: (i, kk))
    b_spec = pl.BlockSpec((bk, bn), lambda i, j, kk: (kk, cb0 + j))
    if seg_out is None:
        shapes = [jax.ShapeDtypeStruct((m, ncols), dt) for dt in out_dtypes]
        specs = [pl.BlockSpec((bm, bn), lambda i, j, kk: (i, j)) for _ in out_dtypes]
    else:
        per = seg_out // bn
        shapes = [jax.ShapeDtypeStruct((ncols // seg_out, m, seg_out), dt) for dt in out_dtypes]
        specs = [pl.BlockSpec((None, bm, bn), lambda i, j, kk: (j // per, i, j % per)) for _ in out_dtypes]
    ex_specs = extra_specs_fn(bm, bn) if extra_specs_fn else ()
    return _mm(name, a, a_spec, w, b_spec, NN, grid, shapes, specs, (bm, bn), epilogue, extras, ex_specs)


def _mm_nt(name, a, w, out_dtypes, epilogue=None, extras=(), extra_specs_fn=None, part=(0, 1), side=None, bm=1024, bn=1024):
    n = w.shape[0]
    if a.ndim == 3:
        seg_cols, m, k = a.shape[2], a.shape[1], a.shape[0] * a.shape[2]
    else:
        m, k = a.shape
    m = m // part[1]
    bm, bn = _blk(m, bm), _blk(n, bn)
    bk = _kblk(k)
    grid = (m // bm, n // bn, k // bk)
    i0 = part[0] * (m // bm)
    if a.ndim == 3:
        per = seg_cols // bk
        a_spec = pl.BlockSpec((None, bm, bk), lambda i, j, kk: (kk // per, i0 + i, kk % per))
    else:
        a_spec = pl.BlockSpec((bm, bk), lambda i, j, kk: (i0 + i, kk))
    b_spec = pl.BlockSpec((bn, bk), lambda i, j, kk: (j, kk))
    shapes = [jax.ShapeDtypeStruct((m, n), dt) for dt in out_dtypes]
    specs = [pl.BlockSpec((bm, bn), lambda i, j, kk: (i, j)) for _ in out_dtypes]
    ex_specs = extra_specs_fn(bm, bn) if extra_specs_fn else ()
    return _mm(name, a, a_spec, w, b_spec, NT, grid, shapes, specs, (bm, bn), epilogue, extras, ex_specs, side)


def _mm_tn(name, a, b, part=(0, 1), side=None, bm=1024, bn=2048):
    t, m = a.shape
    n = b.shape[1] if b.ndim == 2 else b.shape[0] * b.shape[2]
    m = m // part[1]
    bm, bn = _blk(m, bm), _blk(n, bn)
    grid = (m // bm, n // bn, 1)
    i0 = part[0] * (m // bm)
    a_spec = pl.BlockSpec((t, bm), lambda i, j, kk: (0, i0 + i))
    if b.ndim == 3:
        per = b.shape[2] // bn
        b_spec = pl.BlockSpec((None, t, bn), lambda i, j, kk: (j // per, 0, j % per))
    else:
        b_spec = pl.BlockSpec((t, bn), lambda i, j, kk: (0, j))
    res = _mm(name, a, a_spec, b, b_spec, TN, grid, [jax.ShapeDtypeStruct((m, n), BF16)],
              [pl.BlockSpec((bm, bn), lambda i, j, kk: (i, j))], (bm, bn), side=side)
    return res[0] if side is None else (res[0][0], res[1])


ROWS = 256


def _row_spec(d):
    return pl.BlockSpec((ROWS, d), lambda i: (i, 0))


def _vec_spec(d, rows=1):
    return pl.BlockSpec((rows, d), lambda i: (0, 0))


def _rms_fwd(name, x, g):
    s, d = x.shape

    def body(x_ref, g_ref, o_ref):
        xv = x_ref[...]
        r = lax.rsqrt(jnp.mean(xv * xv, axis=-1, keepdims=True) + NORM_EPS)
        o_ref[...] = (xv * r * g_ref[...]).astype(BF16)

    return pl.pallas_call(body, out_shape=jax.ShapeDtypeStruct((s, d), BF16), grid=(s // ROWS,),
                          in_specs=[_row_spec(d), _vec_spec(d)], out_specs=_row_spec(d),
                          compiler_params=_params(("parallel",)), name=name)(x, g)


def _rms_bwd_math(xv, g, dy):
    r = lax.rsqrt(jnp.mean(xv * xv, axis=-1, keepdims=True) + NORM_EPS)
    n = xv * r
    z = dy * g
    dx = r * (z - n * jnp.mean(z * n, axis=-1, keepdims=True))
    return dx, jnp.sum(dy * n, axis=0, keepdims=True)


def _rms_bwd(name, x, g, dy, resid):
    s, d = x.shape

    def body(x_ref, g_ref, dy_ref, r_ref, dx_ref, dxb_ref, dg_ref):
        dx, dg = _rms_bwd_math(x_ref[...], g_ref[...], dy_ref[...])
        dx = dx + r_ref[...]
        dx_ref[...] = dx
        dxb_ref[...] = dx.astype(BF16)

        @pl.when(pl.program_id(0) == 0)
        def _():
            dg_ref[...] = jnp.zeros_like(dg_ref)

        dg_ref[...] += dg

    return pl.pallas_call(
        body, out_shape=[jax.ShapeDtypeStruct((s, d), F32), jax.ShapeDtypeStruct((s, d), BF16), jax.ShapeDtypeStruct((1, d), F32)],
        grid=(s // ROWS,), in_specs=[_row_spec(d), _vec_spec(d), _row_spec(d), _row_spec(d)],
        out_specs=[_row_spec(d), _row_spec(d), _vec_spec(d)], compiler_params=_params(("arbitrary",)), name=name)(x, g, dy, resid)


def _final_loss(h2, tgt, g):
    s, d = h2.shape

    def body(x_ref, t_ref, g_ref, dx_ref, dxb_ref, dg_ref, ls_ref):
        xv, gv = x_ref[...], g_ref[...]
        r = lax.rsqrt(jnp.mean(xv * xv, axis=-1, keepdims=True) + NORM_EPS)
        diff = xv * r * gv - t_ref[...]
        dx, dg = _rms_bwd_math(xv, gv, diff * (1.0 / d))
        dx_ref[...] = dx
        dxb_ref[...] = dx.astype(BF16)

        @pl.when(pl.program_id(0) == 0)
        def _():
            dg_ref[...] = jnp.zeros_like(dg_ref)
            ls_ref[...] = jnp.zeros_like(ls_ref)

        dg_ref[...] += dg
        ls_ref[...] += jnp.sum(diff * diff, axis=0, keepdims=True)

    return pl.pallas_call(
        body, out_shape=[jax.ShapeDtypeStruct((s, d), F32), jax.ShapeDtypeStruct((s, d), BF16),
                         jax.ShapeDtypeStruct((1, d), F32), jax.ShapeDtypeStruct((1, d), F32)],
        grid=(s // ROWS,), in_specs=[_row_spec(d), _row_spec(d), _vec_spec(d)],
        out_specs=[_row_spec(d), _row_spec(d), _vec_spec(d), _vec_spec(d)],
        compiler_params=_params(("arbitrary",)), name="final_norm_loss")(h2, tgt, g)


ATTN_Q = 128


ATTN_BATCH = 8


def _attn_units(s):
    units = []
    for gi, d in enumerate(DILATIONS):
        for r in range(d):
            for q0 in range(0, s // d, ATTN_Q):
                k0 = max(q0 - SPAN, 0)
                units.append((gi, d, r, q0, k0, q0 + ATTN_Q - k0))
    return units


def _stream_rows(d, r, start, size):
    return pl.ds(r + start * d, size) if d == 1 else pl.ds(r + start * d, size, stride=d)


def _attn_scores(q_ref, k_ref, slope, d, r, q0, k0, nk):
    qrows, krows = _stream_rows(d, r, q0, ATTN_Q), _stream_rows(d, r, k0, nk)
    qb = q_ref[qrows, :].astype(BF16)
    kb = k_ref[krows, :].astype(BF16)
    sc = lax.dot_general(qb, kb, NT, preferred_element_type=F32) * (HEAD ** -0.5)
    qi = lax.broadcasted_iota(jnp.int32, (ATTN_Q, nk), 0)
    kj = lax.broadcasted_iota(jnp.int32, (ATTN_Q, nk), 1)
    dist = (q0 - k0) + qi - kj
    valid = (dist >= 0) & (dist <= SPAN)
    sc = sc - (slope * d) * dist.astype(F32)
    return jnp.where(valid, sc, NEG), valid, qb, kb, qrows, krows


def _attn_fwd(proj, dm, slopes):
    s = proj.shape[0]
    units = _attn_units(s)

    def body(sl_ref, q_ref, k_ref, v_ref, att_ref, lse_ref, *scr):
        o_scr, l_scr = scr[:3], scr[3:]
        slope = sl_ref[pl.program_id(0)]
        for first in range(0, len(units), ATTN_BATCH):
            batch = units[first:first + ATTN_BATCH]
            scored = [_attn_scores(q_ref, k_ref, slope, d, r, q0, k0, nk) for _, d, r, q0, k0, nk in batch]
            soft = []
            for sc, _, _, _, _, _ in scored:
                m = jnp.max(sc, axis=-1, keepdims=True)
                p = jnp.exp(sc - m)
                soft.append((m, p, jnp.sum(p, axis=-1, keepdims=True)))
            outs = [lax.dot_general(p.astype(BF16), v_ref[sco[5], :].astype(BF16), NN, preferred_element_type=F32)
                    for (m, p, l), sco in zip(soft, scored)]
            for (gi, *_), (m, p, l), sco, o in zip(batch, soft, scored, outs):
                o_scr[gi][sco[4], :] = o / l
                l_scr[gi][sco[4], :] = jnp.broadcast_to(m + jnp.log(l), (ATTN_Q, HEAD))
        l0, l1, l2 = l_scr[0][...], l_scr[1][...], l_scr[2][...]
        m = jnp.maximum(jnp.maximum(l0, l1), l2)
        w0, w1, w2 = jnp.exp(l0 - m), jnp.exp(l1 - m), jnp.exp(l2 - m)
        tot = w0 + w1 + w2
        att_ref[...] = ((w0 * o_scr[0][...] + w1 * o_scr[1][...] + w2 * o_scr[2][...]) / tot).astype(BF16)
        lse_ref[...] = m + jnp.log(tot)

    def seg(i):
        return pl.BlockSpec((s, HEAD), lambda h: (0, i * (dm // HEAD) + h))

    col = pl.BlockSpec((s, HEAD), lambda h: (0, h))
    return pl.pallas_call(
        body, out_shape=[jax.ShapeDtypeStruct((s, dm), BF16), jax.ShapeDtypeStruct((s, dm), F32)], grid=(dm // HEAD,),
        in_specs=[pl.BlockSpec(memory_space=pltpu.SMEM), seg(0), seg(1), seg(2)], out_specs=[col, col],
        scratch_shapes=[pltpu.VMEM((s, HEAD), F32)] * (2 * len(DILATIONS)),
        compiler_params=_params(("parallel",)), name="attn_fwd")(slopes, proj, proj, proj)


def _attn_bwd(proj, dm, datt, att, lse, slopes, others):
    s = proj.shape[0]
    units = _attn_units(s)

    def body(sl_ref, q_ref, k_ref, v_ref, do_ref, att_ref, lse_ref, o3, o4, o5, o6, out_ref, dq_scr, dk_scr, dv_scr, dl_scr):
        slope = sl_ref[pl.program_id(0)]
        delta = jnp.sum(do_ref[...] * att_ref[...].astype(F32), axis=-1, keepdims=True)
        dl_scr[...] = jnp.broadcast_to(delta, (s, HEAD))
        dq_scr[...] = jnp.zeros_like(dq_scr)
        dk_scr[...] = jnp.zeros_like(dk_scr)
        dv_scr[...] = jnp.zeros_like(dv_scr)
        for first in range(0, len(units), ATTN_BATCH):
            scored = [_attn_scores(q_ref, k_ref, slope, d, r, q0, k0, nk) for _, d, r, q0, k0, nk in units[first:first + ATTN_BATCH]]
            dobs = [do_ref[sco[4], :].astype(BF16) for sco in scored]
            dps = [lax.dot_general(dob, v_ref[sco[5], :].astype(BF16), NT, preferred_element_type=F32) for dob, sco in zip(dobs, scored)]
            ps = [jnp.where(sco[1], jnp.exp(sco[0] - lse_ref[sco[4], :][:, 0:1]), 0.0) for sco in scored]
            dss = [(p * (dp - dl_scr[sco[4], :][:, 0:1]) * (HEAD ** -0.5)).astype(BF16) for p, dp, sco in zip(ps, dps, scored)]
            dqs = [lax.dot_general(ds, sco[3], NN, preferred_element_type=F32) for ds, sco in zip(dss, scored)]
            dks = [lax.dot_general(ds, sco[2], TN, preferred_element_type=F32) for ds, sco in zip(dss, scored)]
            dvs = [lax.dot_general(p.astype(BF16), dob, TN, preferred_element_type=F32) for p, dob in zip(ps, dobs)]
            for sco, dq, dk, dv in zip(scored, dqs, dks, dvs):
                dq_scr[sco[4], :] += dq
                dk_scr[sco[5], :] += dk
                dv_scr[sco[5], :] += dv
        for j, scr in enumerate((dq_scr, dk_scr, dv_scr)):
            out_ref[j] = scr[...].astype(BF16)
        for j, other in enumerate((o3, o4, o5, o6)):
            out_ref[3 + j] = other[...]

    def seg(i):
        return pl.BlockSpec((s, HEAD), lambda h: (0, i * (dm // HEAD) + h))

    col = pl.BlockSpec((s, HEAD), lambda h: (0, h))
    return pl.pallas_call(
        body, out_shape=jax.ShapeDtypeStruct((7, s, dm), BF16), grid=(dm // HEAD,),
        in_specs=[pl.BlockSpec(memory_space=pltpu.SMEM), seg(0), seg(1), seg(2), col, col, col, col, col, col, col],
        out_specs=pl.BlockSpec((7, s, HEAD), lambda h: (0, 0, h)), scratch_shapes=[pltpu.VMEM((s, HEAD), F32)] * 4,
        compiler_params=_params(("parallel",)), name="attn_bwd")(slopes, proj, proj, proj, datt, att, lse, *others)


VEC_CB, VEC_BA, VEC_BX, VEC_LAM = 0, 1, 2, 3
SEG_Q, SEG_K, SEG_V, SEG_X, SEG_GATE, SEG_GA, SEG_GL = range(7)


def _to_3d(ref3, val):
    lw = val.shape[1] // SUBLANES
    for j in range(SUBLANES):
        ref3[:, j, :] = val[:, j * lw:(j + 1) * lw]


def _from_3d(ref3):
    return jnp.concatenate([ref3[:, j, :] for j in range(SUBLANES)], axis=1)


def _softplus(z):
    return jnp.maximum(z, 0.0) + jnp.log1p(jnp.exp(-jnp.abs(z)))


def _gate_math(xc, wa_ref, wx_ref, vec):
    xcb = xc.astype(BF16)
    nh = xc.shape[1] // HEAD
    pre_a = jnp.concatenate([jnp.dot(xcb[:, h * HEAD:(h + 1) * HEAD], wa_ref[h], preferred_element_type=F32) for h in range(nh)], axis=1)
    pre_x = jnp.concatenate([jnp.dot(xcb[:, h * HEAD:(h + 1) * HEAD], wx_ref[h], preferred_element_type=F32) for h in range(nh)], axis=1)
    ra = _sigmoid(pre_a + vec[VEC_BA:VEC_BA + 1])
    ig = _sigmoid(pre_x + vec[VEC_BX:VEC_BX + 1])
    sp = _softplus(-vec[VEC_LAM:VEC_LAM + 1])
    log_a = -LRU_C * ra * sp
    a = jnp.exp(log_a)
    z = 2.0 * log_a
    one_minus_a2 = jnp.where(z > -0.01, -z * (1.0 + z * (0.5 + z * (1.0 / 6.0))), 1.0 - jnp.exp(z))
    mult = jnp.sqrt(one_minus_a2)
    return dict(xcb=xcb, ra=ra, ig=ig, sp=sp, a=a, mult=mult)


def _conv_pad_prev(pad_ref, cur, halo, first):
    pad_ref[0:SUBLANES, :] = jnp.where(first, 0.0, halo)
    pad_ref[SUBLANES:SUBLANES + cur.shape[0], :] = cur


def _gates_fwd(proj, d, cw8, vec8, wa, wx):
    s = proj.shape[0]
    hb = ROWS // SUBLANES

    def body(x_ref, halo_ref, cw_ref, vec_ref, wa_ref, wx_ref, a_ref, u_ref, xc_ref, pad):
        _conv_pad_prev(pad, x_ref[...], halo_ref[...], pl.program_id(0) == 0)
        vec = vec_ref[...]
        xc = vec[VEC_CB:VEC_CB + 1]
        for k in range(CONV_TAPS):
            xc = xc + cw_ref[k:k + 1, :] * pad[pl.ds(SUBLANES - (CONV_TAPS - 1) + k, ROWS), :]
        gm = _gate_math(xc, wa_ref, wx_ref, vec)
        xc_ref[...] = xc
        a_ref[...] = gm["a"]
        u_ref[...] = gm["mult"] * (gm["ig"] * xc)

    wspec = pl.BlockSpec(wa.shape, lambda i: (0, 0, 0))
    return pl.pallas_call(
        body, out_shape=[jax.ShapeDtypeStruct((s, d), F32)] * 3, grid=(s // ROWS,),
        in_specs=[pl.BlockSpec((ROWS, d), lambda i: (i, SEG_X)),
                  pl.BlockSpec((SUBLANES, d), lambda i: (jnp.maximum(i * hb - 1, 0), SEG_X)),
                  _vec_spec(d, SUBLANES), _vec_spec(d, SUBLANES), wspec, wspec],
        out_specs=[_row_spec(d)] * 3, scratch_shapes=[pltpu.VMEM((ROWS + SUBLANES, d), F32)],
        compiler_params=_params(("parallel",)), name="lru_gates_fwd")(proj, proj, cw8, vec8, wa, wx)


def _scan_fwd(a, u, proj):
    s, d = a.shape
    lw = d // SUBLANES

    def body(a_ref, u_ref, g_ref, hp_ref, h2_ref, y_ref, a3, u3, h3, carry):
        @pl.when(pl.program_id(0) == 0)
        def _():
            carry[...] = jnp.zeros_like(carry)

        _to_3d(a3, a_ref[...])
        _to_3d(u3, u_ref[...])

        def step(t, h):
            hp_ref[t] = h
            hn = a3[t] * h + u3[t]
            h3[t] = hn
            return hn

        carry[...] = lax.fori_loop(0, ROWS, step, carry[...], unroll=8)
        h = _from_3d(h3)
        h2_ref[...] = h
        y_ref[...] = (h * _gelu(g_ref[...])).astype(BF16)

    spec3 = pl.BlockSpec((ROWS, SUBLANES, lw), lambda i: (i, 0, 0))
    return pl.pallas_call(
        body, out_shape=[jax.ShapeDtypeStruct((s, SUBLANES, lw), F32), jax.ShapeDtypeStruct((s, d), F32), jax.ShapeDtypeStruct((s, d), BF16)],
        grid=(s // ROWS,), in_specs=[_row_spec(d), _row_spec(d), pl.BlockSpec((ROWS, d), lambda i: (i, SEG_GATE))],
        out_specs=[spec3, _row_spec(d), _row_spec(d)],
        scratch_shapes=[pltpu.VMEM((ROWS, SUBLANES, lw), F32)] * 3 + [pltpu.VMEM((SUBLANES, lw), F32)],
        compiler_params=_params(("arbitrary",)), name="lru_scan_fwd")(a, u, proj)


def _scan_bwd(a, hp3, dh):
    s, d = a.shape
    lw = d // SUBLANES
    nb = s // ROWS

    def body(a_ref, hp_ref, dh_ref, g_ref, da_ref, a3, dh3, g3, da3, carry):
        @pl.when(pl.program_id(0) == 0)
        def _():
            carry[...] = jnp.zeros_like(carry)

        _to_3d(a3, a_ref[...])
        _to_3d(dh3, dh_ref[...])

        def step(j, c):
            t = ROWS - 1 - j
            g = dh3[t] + c
            g3[t] = g
            da3[t] = g * hp_ref[t]
            return a3[t] * g

        carry[...] = lax.fori_loop(0, ROWS, step, carry[...], unroll=8)
        g_ref[...] = _from_3d(g3)
        da_ref[...] = _from_3d(da3)

    rows = pl.BlockSpec((ROWS, d), lambda i: (nb - 1 - i, 0))
    return pl.pallas_call(
        body, out_shape=[jax.ShapeDtypeStruct((s, d), F32)] * 2, grid=(nb,),
        in_specs=[rows, pl.BlockSpec((ROWS, SUBLANES, lw), lambda i: (nb - 1 - i, 0, 0)), rows], out_specs=[rows, rows],
        scratch_shapes=[pltpu.VMEM((ROWS, SUBLANES, lw), F32)] * 4 + [pltpu.VMEM((SUBLANES, lw), F32)],
        compiler_params=_params(("arbitrary",)), name="lru_scan_bwd")(a, hp3, dh)


def _gates_bwd(g, da_in, xc, wa, wx, vec8):
    s, d = xc.shape
    nh = d // HEAD

    def body(g_ref, da_ref, xc_ref, wa_ref, wx_ref, vec_ref, dxc_ref, dwa_ref, dwx_ref, dvec_ref):
        @pl.when(pl.program_id(0) == 0)
        def _():
            dwa_ref[...] = jnp.zeros_like(dwa_ref)
            dwx_ref[...] = jnp.zeros_like(dwx_ref)
            dvec_ref[...] = jnp.zeros_like(dvec_ref)

        xc_v, vec = xc_ref[...], vec_ref[...]
        du, da = g_ref[...], da_ref[...]
        gm = _gate_math(xc_v, wa_ref, wx_ref, vec)
        ra, ig, sp, a, mult = gm["ra"], gm["ig"], gm["sp"], gm["a"], gm["mult"]
        dmult = du * ig * xc_v
        dlog_a = da * a - dmult * (a * a) / mult
        dpre_a = dlog_a * (-LRU_C * sp) * ra * (1.0 - ra)
        dpre_x = du * mult * xc_v * ig * (1.0 - ig)
        dlam = jnp.sum(dlog_a * (-LRU_C * ra), axis=0, keepdims=True) * (-_sigmoid(-vec[VEC_LAM:VEC_LAM + 1]))
        dvec_ref[VEC_BA:VEC_BA + 1, :] += jnp.sum(dpre_a, axis=0, keepdims=True)
        dvec_ref[VEC_BX:VEC_BX + 1, :] += jnp.sum(dpre_x, axis=0, keepdims=True)
        dvec_ref[VEC_LAM:VEC_LAM + 1, :] += dlam
        dab, dxb, xcb = dpre_a.astype(BF16), dpre_x.astype(BF16), gm["xcb"]
        back = []
        for h in range(nh):
            cols = slice(h * HEAD, (h + 1) * HEAD)
            dwa_ref[h] += lax.dot_general(xcb[:, cols], dab[:, cols], TN, preferred_element_type=F32)
            dwx_ref[h] += lax.dot_general(xcb[:, cols], dxb[:, cols], TN, preferred_element_type=F32)
            back.append(lax.dot_general(dab[:, cols], wa_ref[h], NT, preferred_element_type=F32)
                        + lax.dot_general(dxb[:, cols], wx_ref[h], NT, preferred_element_type=F32))
        dxc_ref[...] = du * mult * ig + jnp.concatenate(back, axis=1)

    wspec = pl.BlockSpec(wa.shape, lambda i: (0, 0, 0))
    return pl.pallas_call(
        body, out_shape=[jax.ShapeDtypeStruct((s, d), F32), jax.ShapeDtypeStruct(wa.shape, F32), jax.ShapeDtypeStruct(wa.shape, F32),
                         jax.ShapeDtypeStruct((SUBLANES, d), F32)],
        grid=(s // ROWS,), in_specs=[_row_spec(d), _row_spec(d), _row_spec(d), wspec, wspec, _vec_spec(d, SUBLANES)],
        out_specs=[_row_spec(d), wspec, wspec, _vec_spec(d, SUBLANES)],
        compiler_params=_params(("arbitrary",)), name="lru_gates_bwd")(g, da_in, xc, wa, wx, vec8)


def _conv_bwd(dxc, proj, cw8):
    s, d = dxc.shape
    hb = ROWS // SUBLANES
    last = s // SUBLANES - 1

    def body(dc_ref, dnext_ref, x_ref, xprev_ref, cw_ref, dx_ref, dcw_ref, padd, padx):
        i = pl.program_id(0)

        @pl.when(i == 0)
        def _():
            dcw_ref[...] = jnp.zeros_like(dcw_ref)

        dc = dc_ref[...]
        padd[0:ROWS, :] = dc
        padd[ROWS:ROWS + SUBLANES, :] = jnp.where(i == pl.num_programs(0) - 1, 0.0, dnext_ref[...])
        _conv_pad_prev(padx, x_ref[...], xprev_ref[...], i == 0)
        dx = jnp.zeros_like(dc)
        for k in range(CONV_TAPS):
            dx = dx + cw_ref[k:k + 1, :] * padd[pl.ds(CONV_TAPS - 1 - k, ROWS), :]
            dcw_ref[k:k + 1, :] += jnp.sum(dc * padx[pl.ds(SUBLANES - (CONV_TAPS - 1) + k, ROWS), :], axis=0, keepdims=True)
        dcw_ref[CONV_TAPS:CONV_TAPS + 1, :] += jnp.sum(dc, axis=0, keepdims=True)
        dx_ref[...] = dx.astype(BF16)

    return pl.pallas_call(
        body, out_shape=[jax.ShapeDtypeStruct((s, d), BF16), jax.ShapeDtypeStruct((SUBLANES, d), F32)], grid=(s // ROWS,),
        in_specs=[_row_spec(d), pl.BlockSpec((SUBLANES, d), lambda i: (jnp.minimum((i + 1) * hb, last), 0)),
                  pl.BlockSpec((ROWS, d), lambda i: (i, SEG_X)),
                  pl.BlockSpec((SUBLANES, d), lambda i: (jnp.maximum(i * hb - 1, 0), SEG_X)), _vec_spec(d, SUBLANES)],
        out_specs=[_row_spec(d), _vec_spec(d, SUBLANES)],
        scratch_shapes=[pltpu.VMEM((ROWS + SUBLANES, d), F32), pltpu.VMEM((ROWS + SUBLANES, d), F32)],
        compiler_params=_params(("arbitrary",)), name="lru_conv_bwd")(dxc, dxc, proj, proj, cw8)


def _coords():
    return lax.axis_index("x"), lax.axis_index("y"), lax.axis_index("c")


def _other_chips(x, y):
    return [(1 - x, y), (x, 1 - y), (1 - x, 1 - y)]


def _slab(ref, kind, shard_shape, idx, half=None):
    r, c = shard_shape
    r0, nr = (0, r) if half is None else (half * (r // 2), r // 2)
    if kind == "col":
        return ref.at[pl.ds(r0, nr), pl.ds(pl.multiple_of(idx * c, LANES), c)]
    if kind == "row":
        return ref.at[pl.ds(pl.multiple_of(idx * r, SUBLANES) + r0, nr), :]
    return ref.at[idx, pl.ds(r0, nr), :]


def _full_shape(shard_shape, kind):
    r, c = shard_shape
    return {"col": (r, c * N_DEV), "row": (r * N_DEV, c), "slot": (N_DEV, r, c)}[kind]


def _handshake(peers):
    barrier = pltpu.get_barrier_semaphore()
    for peer in peers:
        pl.semaphore_signal(barrier, inc=1, device_id=peer, device_id_type=MESH)
    pl.semaphore_wait(barrier, len(peers))


def _launch(name, body, out_shape, operands, sems, sequencer_id):
    if sequencer_id is None:
        return pl.pallas_call(body, out_shape=out_shape, in_specs=[HBM] * len(operands), out_specs=[HBM] * len(out_shape),
                              scratch_shapes=sems, name=name)(*operands)
    return pl.kernel(body, out_type=out_shape, mesh=plsc.ScalarSubcoreMesh(axis_name="seq", num_cores=1), name=name,
                     scratch_types=sems, compiler_params=pltpu.CompilerParams(collective_id=sequencer_id))(*operands)


AG_COPIES = 10


def _all_gather(name, shards, kinds, sequencer_id=None):
    n = len(shards)
    shapes = [s.shape for s in shards]

    def body(*refs):
        ins, outs = refs[:n], refs[n:2 * n]
        send_sems, recv_sems, local_sems = refs[2 * n:]
        x, y, c = _coords()
        me, sib, xn, yn, dg = (x, y, c), (x, y, 1 - c), (1 - x, y, c), (x, 1 - y, c), (1 - x, 1 - y, c)
        if sequencer_id is not None:
            _handshake([sib, xn, yn])

        def part(i, dev, half=None):
            return _slab(outs[i], kinds[i], shapes[i], 4 * dev[0] + 2 * dev[1] + dev[2], half)

        def copy(i, k, block, half, to, own=False):
            r = shapes[i][0]
            src = part(i, block, half) if not own else (ins[i] if half is None else ins[i].at[pl.ds(half * (r // 2), r // 2), :])
            return pltpu.make_async_remote_copy(
                src_ref=src, dst_ref=part(i, block, half), send_sem=send_sems.at[AG_COPIES * i + k],
                recv_sem=recv_sems.at[AG_COPIES * i + k], device_id=to, device_id_type=MESH)

        def other_core(dev):
            return (dev[0], dev[1], 1 - c)

        started = []

        def start(cp):
            cp.start()
            started.append(cp)

        for i in range(n):
            start(copy(i, 1, me, 0, xn, own=True))
            start(copy(i, 4, me, 1, yn, own=True))
            start(copy(i, 2, me, 1, xn, own=True))
            start(copy(i, 3, me, 0, yn, own=True))
            start(copy(i, 0, me, None, sib, own=True))
        mine = [pltpu.make_async_copy(ins[i], part(i, me), local_sems.at[i]) for i in range(n)]
        for cp in mine:
            cp.start()
        for i in range(n):
            copy(i, 1, xn, 0, me).wait_recv()
            start(copy(i, 5, xn, 0, yn))
            copy(i, 4, yn, 1, me).wait_recv()
            start(copy(i, 6, yn, 1, xn))
        for i in range(n):
            copy(i, 2, xn, 1, me).wait_recv()
            start(copy(i, 7, xn, None, sib))
            copy(i, 3, yn, 0, me).wait_recv()
            start(copy(i, 8, yn, None, sib))
        for i in range(n):
            copy(i, 5, dg, 0, me).wait_recv()
            copy(i, 6, dg, 1, me).wait_recv()
            start(copy(i, 9, dg, None, sib))
        for i in range(n):
            copy(i, 0, sib, None, me).wait_recv()
            for k, dev in ((7, xn), (8, yn), (9, dg)):
                copy(i, k, other_core(dev), None, me).wait_recv()
        for cp in started:
            cp.wait_send()
        for cp in mine:
            cp.wait()

    out_shape = [jax.ShapeDtypeStruct(_full_shape(s.shape, k), s.dtype) for s, k in zip(shards, kinds)]
    sems = [pltpu.SemaphoreType.DMA((AG_COPIES * n,)), pltpu.SemaphoreType.DMA((AG_COPIES * n,)), pltpu.SemaphoreType.DMA((n,))]
    return _launch(name, body, out_shape, shards, sems, sequencer_id)


def _sibling_copies(kinds, shard_shapes):
    def make(ins, outs, send_sems, recv_sems):
        x, y, c = _coords()
        return [pltpu.make_async_remote_copy(
            src_ref=_slab(ins[i], kinds[i], shard_shapes[i], 2 * q + (1 - c)), dst_ref=outs[i].at[q],
            send_sem=send_sems.at[N_CHIP * i + q], recv_sem=recv_sems.at[N_CHIP * i + q],
            device_id=(x, y, 1 - c), device_id_type=MESH) for i in range(len(ins)) for q in range(N_CHIP)]
    return make


def _sibling_side(partials, kinds, shard_shapes):
    return (partials, [jax.ShapeDtypeStruct((N_CHIP, *s), BF16) for s in shard_shapes], N_CHIP * len(partials),
            _sibling_copies(kinds, shard_shapes))


def _exchange_siblings(name, partials, kinds, shard_shapes, sequencer_id=None):
    n = len(partials)
    make = _sibling_copies(kinds, shard_shapes)

    def body(*refs):
        if sequencer_id is not None:
            x, y, c = _coords()
            _handshake([(x, y, 1 - c)])
        cps = make(refs[:n], refs[n:2 * n], refs[2 * n], refs[2 * n + 1])
        for cp in cps:
            cp.start()
        for cp in cps:
            cp.wait()

    return _launch(name, body, [jax.ShapeDtypeStruct((N_CHIP, *s), BF16) for s in shard_shapes], partials,
                   [pltpu.SemaphoreType.DMA((N_CHIP * n,)), pltpu.SemaphoreType.DMA((N_CHIP * n,))], sequencer_id)


def _exchange_chips(name, chip_sums, sequencer_id=None):
    n = len(chip_sums)

    def body(*refs):
        ins, outs = refs[:n], refs[n:2 * n]
        send_sems, recv_sems = refs[2 * n:]
        x, y, c = _coords()
        if sequencer_id is not None:
            _handshake([(cx, cy, c) for cx, cy in _other_chips(x, y)])
        cps = []
        for i in range(n):
            for k, (cx, cy) in enumerate(_other_chips(x, y)):
                cps.append(pltpu.make_async_remote_copy(
                    src_ref=ins[i].at[2 * cx + cy], dst_ref=outs[i].at[k], send_sem=send_sems.at[3 * i + k],
                    recv_sem=recv_sems.at[3 * i + k], device_id=(cx, cy, c), device_id_type=MESH))
        for cp in cps:
            cp.start()
        for cp in cps:
            cp.wait()

    return _launch(name, body, [jax.ShapeDtypeStruct((3, *t.shape[1:]), BF16) for t in chip_sums], chip_sums,
                   [pltpu.SemaphoreType.DMA((3 * n,)), pltpu.SemaphoreType.DMA((3 * n,))], sequencer_id)


def _all_reduce_small(name, packed):
    rows = packed.shape[0] // N_DEV

    def body(p_ref, out_ref, rb, tot, send_sems, recv_sems):
        x, y, c = _coords()
        me = 4 * x + 2 * y + c

        def peer(k):
            return (x ^ (k >> 2), y ^ ((k >> 1) & 1), c ^ (k & 1))

        def rows_of(idx):
            return pl.ds(pl.multiple_of(idx * rows, SUBLANES), rows)

        def piece(ref, idx):
            return ref.at[rows_of(idx), :]

        scatter = [pltpu.make_async_remote_copy(src_ref=piece(p_ref, me ^ k), dst_ref=rb.at[k], send_sem=send_sems.at[k],
                                                recv_sem=recv_sems.at[k], device_id=peer(k), device_id_type=MESH) for k in range(1, N_DEV)]
        for cp in scatter:
            cp.start()
        acc = p_ref[rows_of(me), :]
        for cp in scatter:
            cp.wait_recv()
        for k in range(1, N_DEV):
            acc = acc + rb[k]
        tot[...] = acc
        out_ref[rows_of(me), :] = acc
        gather = [pltpu.make_async_remote_copy(src_ref=tot, dst_ref=piece(out_ref, me), send_sem=send_sems.at[N_DEV + k],
                                               recv_sem=recv_sems.at[N_DEV + k], device_id=peer(k), device_id_type=MESH)
                  for k in range(1, N_DEV)]
        for cp in gather:
            cp.start()
        for k in range(1, N_DEV):
            pltpu.make_async_remote_copy(src_ref=tot, dst_ref=piece(out_ref, me ^ k), send_sem=send_sems.at[N_DEV + k],
                                         recv_sem=recv_sems.at[N_DEV + k], device_id=peer(k), device_id_type=MESH).wait_recv()
        for cp in scatter + gather:
            cp.wait_send()

    vm = pl.BlockSpec(memory_space=pltpu.VMEM)
    return pl.pallas_call(
        body, out_shape=jax.ShapeDtypeStruct(packed.shape, F32), in_specs=[vm], out_specs=vm,
        scratch_shapes=[pltpu.VMEM((N_DEV, rows, LANES), F32), pltpu.VMEM((rows, LANES), F32),
                        pltpu.SemaphoreType.DMA((2 * N_DEV,)), pltpu.SemaphoreType.DMA((2 * N_DEV,))],
        compiler_params=pltpu.CompilerParams(vmem_limit_bytes=VMEM_LIMIT), name=name)(packed)


def _adamw_math(g, w, m, v):
    m = ADAM_B1 * m + (1.0 - ADAM_B1) * g
    v = ADAM_B2 * v + (1.0 - ADAM_B2) * (g * g)
    delta = -ADAM_LR * ((m / ADAM_C1) / (jnp.sqrt(v / ADAM_C2) + ADAM_EPS) + ADAM_WD * w)
    return delta, m, v


def _slab_spec(kind, shard_shape, tr, slab_of):
    r, c = shard_shape
    if kind == "col":
        return pl.BlockSpec((tr, c), lambda q, i, sc: (i, slab_of(q, sc)))
    return pl.BlockSpec((tr, c), lambda q, i, sc: (slab_of(q, sc) * (r // tr) + i, 0))


def _chip_sum(name, partial, recv, kind, shard_shape, core):
    r, c = shard_shape
    tr = _blk(r, 1024)

    def body(core_ref, p_ref, r_ref, o_ref):
        o_ref[...] = (p_ref[...].astype(F32) + r_ref[...].astype(F32)).astype(BF16)

    spec4 = pl.BlockSpec((None, tr, c), lambda q, i, sc: (q, i, 0))
    grid_spec = pltpu.PrefetchScalarGridSpec(
        num_scalar_prefetch=1, grid=(N_CHIP, r // tr),
        in_specs=[_slab_spec(kind, shard_shape, tr, lambda q, sc: 2 * q + sc[0]), spec4], out_specs=spec4)
    return pl.pallas_call(body, out_shape=jax.ShapeDtypeStruct((N_CHIP, r, c), BF16), grid_spec=grid_spec,
                          compiler_params=_params(("parallel", "parallel")), name=name)(core, partial, recv)


def _adamw_shard(name, parts, w, m, v, chip):
    r, c = w.shape
    n_parts = len(parts)
    tr = _blk(r // n_parts, 256)
    per = r // n_parts // tr

    def body(chip_ref, *refs):
        src, (w_ref, m_ref, v_ref), (g_out, d_out, m_out, v_out) = refs[:2 * n_parts], refs[2 * n_parts:2 * n_parts + 3], refs[2 * n_parts + 3:]
        for p in range(n_parts):
            @pl.when(pl.program_id(0) // per == p)
            def _():
                g = src[2 * p][...].astype(F32)
                for k in range(3):
                    g = g + src[2 * p + 1][k].astype(F32)
                g_out[...] = g
                d_out[...], m_out[...], v_out[...] = _adamw_math(g, w_ref[...], m_ref[...], v_ref[...])

    def part_specs(p):
        at = lambda i: jnp.clip(i - p * per, 0, per - 1)
        return [pl.BlockSpec((None, tr, c), lambda i, sc: (sc[0], at(i), 0)), pl.BlockSpec((3, tr, c), lambda i, sc: (0, at(i), 0))]

    blk = pl.BlockSpec((tr, c), lambda i, sc: (i, 0))
    grid_spec = pltpu.PrefetchScalarGridSpec(
        num_scalar_prefetch=1, grid=(r // tr,), in_specs=[s for p in range(n_parts) for s in part_specs(p)] + [blk, blk, blk], out_specs=[blk] * 4)
    return pl.pallas_call(body, out_shape=[jax.ShapeDtypeStruct((r, c), F32)] * 4, grid_spec=grid_spec,
                          compiler_params=_params(("parallel",)), name=name)(chip, *[a for p in parts for a in p], w, m, v)


def _adamw_small(name, g, w, m, v):
    def body(g_ref, w_ref, m_ref, v_ref, d_out, m_out, v_out):
        d_out[...], m_out[...], v_out[...] = _adamw_math(g_ref[...], w_ref[...], m_ref[...], v_ref[...])

    vm = pl.BlockSpec(memory_space=pltpu.VMEM)
    return pl.pallas_call(body, out_shape=[jax.ShapeDtypeStruct(g.shape, F32)] * 3, in_specs=[vm] * 4, out_specs=[vm] * 3,
                          compiler_params=pltpu.CompilerParams(vmem_limit_bytes=VMEM_LIMIT), name=name)(g, w, m, v)


def _pack_rows(arrays, total_rows):
    flat = [a.reshape(-1, LANES) for a in arrays]
    used = sum(f.shape[0] for f in flat)
    return jnp.concatenate(flat + [jnp.zeros((total_rows - used, LANES), F32)], axis=0)


def _unpack_rows(packed, like):
    out, at = [], 0
    for a in like:
        n = a.size // LANES
        out.append(packed[at:at + n].reshape(a.shape))
        at += n
    return out


def kernel(x, norm_mix_g, w_in, conv_w, conv_b, lru_wa, lru_ba, lru_wx, lru_bx, lru_lambda, w_proj_attn, w_proj_lru, w_out, norm_mlp_g, w_up, w_down, norm_final_g, loss_target, m_norm_mix_g, m_w_in, m_conv_w, m_conv_b, m_lru_wa, m_lru_ba, m_lru_wx, m_lru_bx, m_lru_lambda, m_w_proj_attn, m_w_proj_lru, m_w_out, m_norm_mlp_g, m_w_up, m_w_down, m_norm_final_g, v_norm_mix_g, v_w_in, v_conv_w, v_conv_b, v_lru_wa, v_lru_ba, v_lru_wx, v_lru_bx, v_lru_lambda, v_w_proj_attn, v_w_proj_lru, v_w_out, v_norm_mlp_g, v_w_up, v_w_down, v_norm_final_g):
    xs, tgt = x[0], loss_target[0]
    s, d = xs.shape
    nh = d // HEAD
    ix, iy, ic = _coords()
    core = jnp.reshape(ic, (1,)).astype(jnp.int32)
    chip = jnp.reshape(2 * ix + iy, (1,)).astype(jnp.int32)
    dev = 4 * ix + 2 * iy + ic

    big = [w_in[0], w_proj_attn[0], w_proj_lru[0], w_out[0], w_up[0], w_down[0]]
    big_m = [m_w_in[0], m_w_proj_attn[0], m_w_proj_lru[0], m_w_out[0], m_w_up[0], m_w_down[0]]
    big_v = [v_w_in[0], v_w_proj_attn[0], v_w_proj_lru[0], v_w_out[0], v_w_up[0], v_w_down[0]]
    kinds = ["col", "row", "row", "row", "col", "row"]
    pad_taps = lambda t: jnp.pad(t, ((0, SUBLANES - CONV_TAPS), (0, 0)))
    shards = [w.astype(BF16) for w in big]
    pad_taps2 = lambda t: jnp.pad(t, ((0, 2 * SUBLANES - CONV_TAPS), (0, 0)))
    win, cw_slots = _all_gather("all_gather_w_in", [shards[0], pad_taps2(conv_w[0])], ["col", "slot"])
    later = lax.optimization_barrier((shards[1:], win))[0]
    wpa, wpl, wout = _all_gather("all_gather_mix", later[:3], kinds[1:4], sequencer_id=1)
    wup, wdown = _all_gather("all_gather_mlp", later[3:], kinds[4:], sequencer_id=5)
    cw8 = jnp.transpose(cw_slots[:, :SUBLANES], (1, 0, 2)).reshape(SUBLANES, d)
    row_id = lax.broadcasted_iota(jnp.int32, (SUBLANES, d), 0)
    vec8 = sum(jnp.where(row_id == k, t, 0.0) for k, t in ((VEC_CB, conv_b), (VEC_BA, lru_ba), (VEC_BX, lru_bx), (VEC_LAM, lru_lambda)))
    wa16, wx16 = lru_wa[0].astype(BF16), lru_wx[0].astype(BF16)
    slopes = 2.0 ** (-8.0 * jnp.arange(1, nh + 1, dtype=F32) / nh)

    def seg_specs(*segs):
        return lambda bm, bn: [pl.BlockSpec((bm, bn), (lambda i, j, kk, sg=sg: (i, sg * (d // bn) + j))) for sg in segs]

    def plain_specs(k):
        return lambda bm, bn: [pl.BlockSpec((bm, bn), lambda i, j, kk: (i, j)) for _ in range(k)]

    xn = _rms_fwd("norm_mix", xs, norm_mix_g)
    proj = _mm_fwd("proj_in", xn, win, 0, 7 * d, [F32], bm=2048)[0]
    att, lse = _attn_fwd(proj, d, slopes)
    a3, u3, xc = _gates_fwd(proj, d, cw8, vec8, wa16, wx16)
    hp3, h2d, ylru = _scan_fwd(a3, u3, proj)
    pa = _mm_fwd("proj_attn", att, wpa, 0, d, [F32], bm=2048)[0]

    def merge(acc, pa_b, ga, gl):
        return acc, _sigmoid(ga) * pa_b + _sigmoid(gl) * acc

    plr, merged = _mm_fwd("proj_lru_merge", ylru, wpl, 0, d, [F32, BF16], merge, (pa, proj, proj),
                          lambda bm, bn: plain_specs(1)(bm, bn) + seg_specs(SEG_GA, SEG_GL)(bm, bn), bm=512)
    h1 = _mm_fwd("mix_out", merged, wout, 0, d, [F32], lambda acc, r: (acc + r,), (xs,), plain_specs(1))[0]
    hn = _rms_fwd("norm_mlp", h1, norm_mlp_g)

    def relu2(acc):
        return acc, jnp.square(jnp.maximum(acc, 0.0))

    up, hid = _mm_fwd("mlp_up", hn, wup, 0, wup.shape[1], [BF16, BF16], relu2, bm=2048)
    h2 = _mm_fwd("mlp_down", hid, wdown, 0, d, [F32], lambda acc, r: (acc + r,), (h1,), plain_specs(1))[0]
    dh2, dh2b, dg3, loss_lanes = _final_loss(h2, tgt, norm_final_g.reshape(1, d))
    loss = lax.psum(0.5 / d * jnp.sum(loss_lanes), ("x", "y", "c"))
    dh2b = lax.optimization_barrier((dh2b, loss))[0]

    def reduce_group(tag, kk, shp, partials, from_sibling, sequencer_id):
        sums = [_chip_sum(f"chip_sum_{tag}_{i}", p, f, k, sh, core) for i, (p, f, k, sh) in enumerate(zip(partials, from_sibling, kk, shp))]
        return list(zip(sums, _exchange_chips(f"rs_chips_{tag}", sums, sequencer_id)))

    dup = _mm_nt("mlp_down_dx", dh2b, wdown, [BF16], lambda acc, u: (acc * (2.0 * jnp.maximum(u.astype(F32), 0.0)),), (up,), plain_specs(1))[0]
    g_wdown = _mm_tn("mlp_down_dw", hid, dh2b)
    g_wup = _mm_tn("mlp_up_dw", hn, dup)
    shp_mlp = [w.shape for w in big[4:]]
    (dhn,), sib_mlp = _mm_nt("mlp_up_dx", dup, wup, [F32], side=_sibling_side([g_wup, g_wdown], kinds[4:], shp_mlp))
    red_up, red_down = reduce_group("mlp", kinds[4:], shp_mlp, [g_wup, g_wdown], sib_mlp, 2)
    dhn = lax.optimization_barrier((dhn, red_up[0], red_down[0]))[0]
    dh1, dh1b, dg2 = _rms_bwd("norm_mlp_bwd", h1, norm_mlp_g, dhn, dh2)

    def merge_bwd(acc, pa_b, pl_b, ga, gl):
        sa, sl = _sigmoid(ga), _sigmoid(gl)
        return acc * sa, acc * sl, acc * pa_b * sa * (1.0 - sa), acc * pl_b * sl * (1.0 - sl)

    dpa, dpl, dga, dgl = _mm_nt("mix_out_dx", dh1b, wout, [BF16] * 4, merge_bwd, (pa, plr, proj, proj),
                                lambda bm, bn: plain_specs(2)(bm, bn) + seg_specs(SEG_GA, SEG_GL)(bm, bn), bm=512)
    g_wout = _mm_tn("mix_out_dw", merged, dh1b)
    datt = _mm_nt("proj_attn_dx", dpa, wpa, [F32], bm=2048)[0]
    g_wpa = _mm_tn("proj_attn_dw", att, dpa)

    def lru_out_bwd(acc, h_b, gate):
        return acc * _gelu(gate), acc * h_b * _gelu_grad(gate)

    g_wpl = _mm_tn("proj_lru_dw", ylru, dpl)
    shp_mix = [w.shape for w in big[1:4]]
    (dh, dxg), sib_mix = _mm_nt("proj_lru_dx", dpl, wpl, [F32, BF16], lru_out_bwd, (h2d, proj),
                                lambda bm, bn: plain_specs(1)(bm, bn) + seg_specs(SEG_GATE)(bm, bn), bm=512,
                                side=_sibling_side([g_wpa, g_wpl, g_wout], kinds[1:4], shp_mix))
    red_pa, red_pl, red_out = reduce_group("mix", kinds[1:4], shp_mix, [g_wpa, g_wpl, g_wout], sib_mix, 3)
    g3, da3 = _scan_bwd(a3, hp3, dh)
    dxc, dwa, dwx, dvec = _gates_bwd(g3, da3, xc, wa16, wx16, vec8)
    dxr, dconv = _conv_bwd(dxc, proj, cw8)
    dproj = _attn_bwd(proj, d, datt, att, lse, slopes, (dxr, dxg, dga, dgl))

    def small_step(tag, grads, ws, ms, vs, like, after):
        n_rows = sum(g.size for g in grads) // LANES
        per_dev = -(-n_rows // (N_DEV * SUBLANES)) * SUBLANES
        packed = lax.optimization_barrier((_pack_rows(grads, N_DEV * per_dev), after))[0]
        total = _all_reduce_small(f"all_reduce_{tag}", packed)
        w_rows = -(-(sum(w.size for w in ws) // LANES) // SUBLANES) * SUBLANES
        upd = _adamw_small(f"adamw_{tag}", total[:w_rows], _pack_rows(ws, w_rows), _pack_rows(ms, w_rows), _pack_rows(vs, w_rows))
        return _unpack_rows(total, like), [_unpack_rows(t, ws) for t in upd]

    early_w = [conv_b, lru_wa, lru_ba, lru_wx, lru_bx, lru_lambda, norm_mlp_g, norm_final_g]
    early_m = [m_conv_b, m_lru_wa, m_lru_ba, m_lru_wx, m_lru_bx, m_lru_lambda, m_norm_mlp_g, m_norm_final_g]
    early_v = [v_conv_b, v_lru_wa, v_lru_ba, v_lru_wx, v_lru_bx, v_lru_lambda, v_norm_mlp_g, v_norm_final_g]
    early_g = [dconv[CONV_TAPS:CONV_TAPS + 1], dwa, dvec[VEC_BA:VEC_BA + 1], dwx, dvec[VEC_BX:VEC_BX + 1],
               dvec[VEC_LAM:VEC_LAM + 1], dg2, dg3, dconv[0:CONV_TAPS]]
    dproj = lax.optimization_barrier((dproj, red_up[1], red_down[1]))[0]
    early_sum, early_upd = small_step("small", early_g, early_w, early_m, early_v,
                                      early_w + [jax.ShapeDtypeStruct((1, CONV_TAPS, d), F32)], dproj)
    g_cw_full = early_sum[-1]
    cshard = conv_w.shape[2]
    g_cw = lax.dynamic_slice(g_cw_full, (0, 0, dev * cshard), (1, CONV_TAPS, cshard))
    cw_delta, cw_m, cw_v = (t[:CONV_TAPS][None] for t in _adamw_small(
        "adamw_conv_w", pad_taps(g_cw[0]), pad_taps(conv_w[0]), pad_taps(m_conv_w[0]), pad_taps(v_conv_w[0])))
    dproj = lax.optimization_barrier((dproj, early_sum, red_pa[1], red_pl[1], red_out[1]))[0]
    half = (big[0].shape[0] // 2, big[0].shape[1])
    g_in0 = _mm_tn("proj_in_dw_0", xn, dproj, part=(0, 2))
    g_in1, sib_in0 = _mm_tn("proj_in_dw_1", xn, dproj, part=(1, 2), side=_sibling_side([g_in0], ["col"], [half]))
    red_in = reduce_group("in_0", ["col"], [half], [g_in0], sib_in0, 4)
    dproj = lax.optimization_barrier((dproj, red_in[0][0]))[0]
    (dxn0,), sib_in1 = _mm_nt("proj_in_dx_0", dproj, win, [F32], part=(0, 2), side=_sibling_side([g_in1], ["col"], [half]))
    red_in += reduce_group("in_1", ["col"], [half], [g_in1], sib_in1, 6)
    dproj = lax.optimization_barrier((dproj, red_in[1][0]))[0]
    dxn1 = _mm_nt("proj_in_dx_1", dproj, win, [F32], part=(1, 2))[0]
    dxn = jnp.concatenate([dxn0, dxn1], axis=0)
    dxn = lax.optimization_barrier((dxn, red_in[0][1]))[0]
    grad_x, _, dg1 = _rms_bwd("norm_mix_bwd", xs, norm_mix_g, dxn, dh1)
    red_up, red_down = lax.optimization_barrier(((red_up, red_down), dg1))[0]
    big_out = {i: _adamw_shard(f"adamw_{i}", [red], big[i], big_m[i], big_v[i], chip) for i, red in ((4, red_up), (5, red_down))}
    big_out.update({i: _adamw_shard(f"adamw_{i}", [red], big[i], big_m[i], big_v[i], chip) for i, red in ((1, red_pa), (2, red_pl), (3, red_out))})
    late_sum, late_upd = small_step("norm_mix", [dg1], [norm_mix_g], [m_norm_mix_g], [v_norm_mix_g], [norm_mix_g], (big_out[4], big_out[5]))
    big_out[0] = _adamw_shard("adamw_0", red_in, big[0], big_m[0], big_v[0], chip)
    s_grad = late_sum + early_sum[:-1]
    s_delta, s_m, s_v = (late_upd[j] + early_upd[j] for j in range(3))


    names = ["norm_mix_g", "w_in", "conv_w", "conv_b", "lru_wa", "lru_ba", "lru_wx", "lru_bx", "lru_lambda", "w_proj_attn", "w_proj_lru",
             "w_out", "norm_mlp_g", "w_up", "w_down", "norm_final_g"]
    small_names = ["norm_mix_g", "conv_b", "lru_wa", "lru_ba", "lru_wx", "lru_bx", "lru_lambda", "norm_mlp_g", "norm_final_g"]
    big_names = ["w_in", "w_proj_attn", "w_proj_lru", "w_out", "w_up", "w_down"]
    res = {"conv_w": (g_cw, cw_delta, cw_m, cw_v)}
    for i, nm in enumerate(small_names):
        res[nm] = (s_grad[i], s_delta[i], s_m[i], s_v[i])
    for i, nm in enumerate(big_names):
        res[nm] = tuple(t[None] for t in big_out[i])
    return (loss, grad_x[None], *[res[nm][0] for nm in names], *[res[nm][1] for nm in names],
            *[res[nm][2] for nm in names], *[res[nm][3] for nm in names])
```

```python
import jax
import jax.numpy as jnp
from jax import lax
from jax.experimental import pallas as pl
from jax.experimental.pallas import tpu as pltpu
from jax.experimental.pallas import tpu_sc as plsc

F32, BF16 = jnp.float32, jnp.bfloat16
MESH = pl.DeviceIdType.MESH
HBM = pl.BlockSpec(memory_space=pltpu.HBM)
N_DEV = 8
N_CHIP = 4
HEAD = 128
SPAN = 128
DILATIONS = (1, 4, 16)
CONV_TAPS = 4
LRU_C = 8.0
NORM_EPS = 1e-6
LANES = 128
SUBLANES = 8
VMEM_LIMIT = 56 * 1024 * 1024
ADAM_LR, ADAM_B1, ADAM_B2, ADAM_EPS, ADAM_WD, ADAM_STEP = 0.001, 0.9, 0.999, 1e-08, 0.01, 10
ADAM_C1 = 1.0 - ADAM_B1 ** ADAM_STEP
ADAM_C2 = 1.0 - ADAM_B2 ** ADAM_STEP
NEG = -1e30


def _params(sem=None):
    return pltpu.CompilerParams(dimension_semantics=sem, vmem_limit_bytes=VMEM_LIMIT)


def _sigmoid(v):
    return 1.0 / (1.0 + jnp.exp(-v))


def _gelu(v):
    k = 0.7978845608028654
    return 0.5 * v * (1.0 + jnp.tanh(k * (v + 0.044715 * v * v * v)))


def _gelu_grad(v):
    k = 0.7978845608028654
    t = jnp.tanh(k * (v + 0.044715 * v * v * v))
    return 0.5 * (1.0 + t) + 0.5 * v * (1.0 - t * t) * k * (1.0 + 3.0 * 0.044715 * v * v)


NN = (((1,), (0,)), ((), ()))
NT = (((1,), (1,)), ((), ()))
TN = (((0,), (0,)), ((), ()))


def _mm(name, a, a_spec, b, b_spec, dn, grid, out_shapes, out_specs, acc_block, epilogue=None, extras=(), extra_specs=(), side=None):
    nk, ne, no = grid[2], len(extras), len(out_shapes)
    side_ops, side_shapes, side_copies, make_copies = side if side is not None else ((), (), 0, None)
    ns_in, ns_out = len(side_ops), len(side_shapes)

    def body(*refs):
        a_ref, b_ref = refs[0], refs[1]
        ex, side_in = refs[2:2 + ne], refs[2 + ne:2 + ne + ns_in]
        outs = refs[2 + ne + ns_in:2 + ne + ns_in + no]
        side_out = refs[2 + ne + ns_in + no:2 + ne + ns_in + no + ns_out]
        scratch = refs[2 + ne + ns_in + no + ns_out:]
        at = [pl.program_id(ax) for ax in range(3)]
        if side is not None:
            @pl.when((at[0] == 0) & (at[1] == 0) & (at[2] == 0))
            def _():
                for cp in make_copies(side_in, side_out, scratch[-2], scratch[-1]):
                    cp.start()

        part = lax.dot_general(a_ref[...], b_ref[...], dn, preferred_element_type=F32)

        def finish(acc):
            vals = epilogue(acc, *[e[...] for e in ex]) if epilogue is not None else (acc,)
            for o, v in zip(outs, vals):
                o[...] = v.astype(o.dtype)

        if nk == 1:
            finish(part)
        else:
            acc_ref, k = scratch[0], at[2]

            @pl.when(k == 0)
            def _():
                acc_ref[...] = part

            @pl.when(k > 0)
            def _():
                acc_ref[...] += part

            @pl.when(k == nk - 1)
            def _():
                finish(acc_ref[...])

        if side is not None:
            @pl.when((at[0] == grid[0] - 1) & (at[1] == grid[1] - 1) & (at[2] == grid[2] - 1))
            def _():
                for cp in make_copies(side_in, side_out, scratch[-2], scratch[-1]):
                    cp.wait()

    scratch_shapes = [pltpu.VMEM(acc_block, F32)] if nk > 1 else []
    if side is not None:
        scratch_shapes += [pltpu.SemaphoreType.DMA((side_copies,)), pltpu.SemaphoreType.DMA((side_copies,))]
    res = pl.pallas_call(
        body, out_shape=[*out_shapes, *side_shapes], grid=grid, in_specs=[a_spec, b_spec, *extra_specs, *[HBM] * ns_in],
        out_specs=[*out_specs, *[HBM] * ns_out], scratch_shapes=scratch_shapes,
        compiler_params=_params(("arbitrary",) * 3 if side is not None else ("parallel", "parallel", "arbitrary")),
        name=name)(a, b, *extras, *side_ops)
    return res if side is None else (res[:no], res[no:])


def _blk(n, pref):
    return pref if n % pref == 0 else n


def _kblk(k):
    return k if k <= 2048 else next(b for b in (2048, 1024, 512) if k % b == 0)


def _mm_fwd(name, a, w, col0, ncols, out_dtypes, epilogue=None, extras=(), extra_specs_fn=None, seg_out=None, bm=1024, bn=1024):
    m, k = a.shape
    bm, bn = _blk(m, bm), _blk(ncols, bn)
    bk = _kblk(k)
    nk = k // bk
    cb0 = col0 // bn
    grid = (m // bm, ncols // bn, nk)
    a_spec = pl.BlockSpec((bm, bk), lambda i, j, kk: (i, kk))
    b_spec = pl.BlockSpec((bk, bn), lambda i, j, kk: (kk, cb0 + j))
    if seg_out is None:
        shapes = [jax.ShapeDtypeStruct((m, ncols), dt) for dt in out_dtypes]
        specs = [pl.BlockSpec((bm, bn), lambda i, j, kk: (i, j)) for _ in out_dtypes]
    else:
        per = seg_out // bn
        shapes = [jax.ShapeDtypeStruct((ncols // seg_out, m, seg_out), dt) for dt in out_dtypes]
        specs = [pl.BlockSpec((None, bm, bn), lambda i, j, kk: (j // per, i, j % per)) for _ in out_dtypes]
    ex_specs = extra_specs_fn(bm, bn) if extra_specs_fn else ()
    return _mm(name, a, a_spec, w, b_spec, NN, grid, shapes, specs, (bm, bn), epilogue, extras, ex_specs)


def _mm_nt(name, a, w, out_dtypes, epilogue=None, extras=(), extra_specs_fn=None, part=(0, 1), side=None, bm=1024, bn=1024):
    n = w.shape[0]
    if a.ndim == 3:
        seg_cols, m, k = a.shape[2], a.shape[1], a.shape[0] * a.shape[2]
    else:
        m, k = a.shape
    m = m // part[1]
    bm, bn = _blk(m, bm), _blk(n, bn)
    bk = _kblk(k)
    grid = (m // bm, n // bn, k // bk)
    i0 = part[0] * (m // bm)
    if a.ndim == 3:
        per = seg_cols // bk
        a_spec = pl.BlockSpec((None, bm, bk), lambda i, j, kk: (kk // per, i0 + i, kk % per))
    else:
        a_spec = pl.BlockSpec((bm, bk), lambda i, j, kk: (i0 + i, kk))
    b_spec = pl.BlockSpec((bn, bk), lambda i, j, kk: (j, kk))
    shapes = [jax.ShapeDtypeStruct((m, n), dt) for dt in out_dtypes]
    specs = [pl.BlockSpec((bm, bn), lambda i, j, kk: (i, j)) for _ in out_dtypes]
    ex_specs = extra_specs_fn(bm, bn) if extra_specs_fn else ()
    return _mm(name, a, a_spec, w, b_spec, NT, grid, shapes, specs, (bm, bn), epilogue, extras, ex_specs, side)


def _mm_tn(name, a, b, part=(0, 1), side=None, bm=1024, bn=2048):
    t, m = a.shape
    n = b.shape[1] if b.ndim == 2 else b.shape[0] * b.shape[2]
    m = m // part[1]
    bm, bn = _blk(m, bm), _blk(n, bn)
    grid = (m // bm, n // bn, 1)
    i0 = part[0] * (m // bm)
    a_spec = pl.BlockSpec((t, bm), lambda i, j, kk: (0, i0 + i))
    if b.ndim == 3:
        per = b.shape[2] // bn
        b_spec = pl.BlockSpec((None, t, bn), lambda i, j, kk: (j // per, 0, j % per))
    else:
        b_spec = pl.BlockSpec((t, bn), lambda i, j, kk: (0, j))
    res = _mm(name, a, a_spec, b, b_spec, TN, grid, [jax.ShapeDtypeStruct((m, n), BF16)],
              [pl.BlockSpec((bm, bn), lambda i, j, kk: (i, j))], (bm, bn), side=side)
    return res[0] if side is None else (res[0][0], res[1])


ROWS = 256


def _row_spec(d):
    return pl.BlockSpec((ROWS, d), lambda i: (i, 0))


def _vec_spec(d, rows=1):
    return pl.BlockSpec((rows, d), lambda i: (0, 0))


def _rms_fwd(name, x, g):
    s, d = x.shape

    def body(x_ref, g_ref, o_ref):
        xv = x_ref[...]
        r = lax.rsqrt(jnp.mean(xv * xv, axis=-1, keepdims=True) + NORM_EPS)
        o_ref[...] = (xv * r * g_ref[...]).astype(BF16)

    return pl.pallas_call(body, out_shape=jax.ShapeDtypeStruct((s, d), BF16), grid=(s // ROWS,),
                          in_specs=[_row_spec(d), _vec_spec(d)], out_specs=_row_spec(d),
                          compiler_params=_params(("parallel",)), name=name)(x, g)


def _rms_bwd_math(xv, g, dy):
    r = lax.rsqrt(jnp.mean(xv * xv, axis=-1, keepdims=True) + NORM_EPS)
    n = xv * r
    z = dy * g
    dx = r * (z - n * jnp.mean(z * n, axis=-1, keepdims=True))
    return dx, jnp.sum(dy * n, axis=0, keepdims=True)


def _rms_bwd(name, x, g, dy, resid):
    s, d = x.shape

    def body(x_ref, g_ref, dy_ref, r_ref, dx_ref, dxb_ref, dg_ref):
        dx, dg = _rms_bwd_math(x_ref[...], g_ref[...], dy_ref[...])
        dx = dx + r_ref[...]
        dx_ref[...] = dx
        dxb_ref[...] = dx.astype(BF16)

        @pl.when(pl.program_id(0) == 0)
        def _():
            dg_ref[...] = jnp.zeros_like(dg_ref)

        dg_ref[...] += dg

    return pl.pallas_call(
        body, out_shape=[jax.ShapeDtypeStruct((s, d), F32), jax.ShapeDtypeStruct((s, d), BF16), jax.ShapeDtypeStruct((1, d), F32)],
        grid=(s // ROWS,), in_specs=[_row_spec(d), _vec_spec(d), _row_spec(d), _row_spec(d)],
        out_specs=[_row_spec(d), _row_spec(d), _vec_spec(d)], compiler_params=_params(("arbitrary",)), name=name)(x, g, dy, resid)


def _final_loss(h2, tgt, g):
    s, d = h2.shape

    def body(x_ref, t_ref, g_ref, dx_ref, dxb_ref, dg_ref, ls_ref):
        xv, gv = x_ref[...], g_ref[...]
        r = lax.rsqrt(jnp.mean(xv * xv, axis=-1, keepdims=True) + NORM_EPS)
        diff = xv * r * gv - t_ref[...]
        dx, dg = _rms_bwd_math(xv, gv, diff * (1.0 / d))
        dx_ref[...] = dx
        dxb_ref[...] = dx.astype(BF16)

        @pl.when(pl.program_id(0) == 0)
        def _():
            dg_ref[...] = jnp.zeros_like(dg_ref)
            ls_ref[...] = jnp.zeros_like(ls_ref)

        dg_ref[...] += dg
        ls_ref[...] += jnp.sum(diff * diff, axis=0, keepdims=True)

    return pl.pallas_call(
        body, out_shape=[jax.ShapeDtypeStruct((s, d), F32), jax.ShapeDtypeStruct((s, d), BF16),
                         jax.ShapeDtypeStruct((1, d), F32), jax.ShapeDtypeStruct((1, d), F32)],
        grid=(s // ROWS,), in_specs=[_row_spec(d), _row_spec(d), _vec_spec(d)],
        out_specs=[_row_spec(d), _row_spec(d), _vec_spec(d), _vec_spec(d)],
        compiler_params=_params(("arbitrary",)), name="final_norm_loss")(h2, tgt, g)


ATTN_Q = 128


ATTN_BATCH = 8


def _attn_units(s):
    units = []
    for gi, d in enumerate(DILATIONS):
        for r in range(d):
            for q0 in range(0, s // d, ATTN_Q):
                k0 = max(q0 - SPAN, 0)
                units.append((gi, d, r, q0, k0, q0 + ATTN_Q - k0))
    return units


def _stream_rows(d, r, start, size):
    return pl.ds(r + start * d, size) if d == 1 else pl.ds(r + start * d, size, stride=d)


def _attn_scores(q_ref, k_ref, slope, d, r, q0, k0, nk):
    qrows, krows = _stream_rows(d, r, q0, ATTN_Q), _stream_rows(d, r, k0, nk)
    qb = q_ref[qrows, :].astype(BF16)
    kb = k_ref[krows, :].astype(BF16)
    sc = lax.dot_general(qb, kb, NT, preferred_element_type=F32) * (HEAD ** -0.5)
    qi = lax.broadcasted_iota(jnp.int32, (ATTN_Q, nk), 0)
    kj = lax.broadcasted_iota(jnp.int32, (ATTN_Q, nk), 1)
    dist = (q0 - k0) + qi - kj
    valid = (dist >= 0) & (dist <= SPAN)
    sc = sc - (slope * d) * dist.astype(F32)
    return jnp.where(valid, sc, NEG), valid, qb, kb, qrows, krows


def _attn_fwd(proj, dm, slopes):
    s = proj.shape[0]
    units = _attn_units(s)

    def body(sl_ref, q_ref, k_ref, v_ref, att_ref, lse_ref, *scr):
        o_scr, l_scr = scr[:3], scr[3:]
        slope = sl_ref[pl.program_id(0)]
        for first in range(0, len(units), ATTN_BATCH):
            batch = units[first:first + ATTN_BATCH]
            scored = [_attn_scores(q_ref, k_ref, slope, d, r, q0, k0, nk) for _, d, r, q0, k0, nk in batch]
            soft = []
            for sc, _, _, _, _, _ in scored:
                m = jnp.max(sc, axis=-1, keepdims=True)
                p = jnp.exp(sc - m)
                soft.append((m, p, jnp.sum(p, axis=-1, keepdims=True)))
            outs = [lax.dot_general(p.astype(BF16), v_ref[sco[5], :].astype(BF16), NN, preferred_element_type=F32)
                    for (m, p, l), sco in zip(soft, scored)]
            for (gi, *_), (m, p, l), sco, o in zip(batch, soft, scored, outs):
                o_scr[gi][sco[4], :] = o / l
                l_scr[gi][sco[4], :] = jnp.broadcast_to(m + jnp.log(l), (ATTN_Q, HEAD))
        l0, l1, l2 = l_scr[0][...], l_scr[1][...], l_scr[2][...]
        m = jnp.maximum(jnp.maximum(l0, l1), l2)
        w0, w1, w2 = jnp.exp(l0 - m), jnp.exp(l1 - m), jnp.exp(l2 - m)
        tot = w0 + w1 + w2
        att_ref[...] = ((w0 * o_scr[0][...] + w1 * o_scr[1][...] + w2 * o_scr[2][...]) / tot).astype(BF16)
        lse_ref[...] = m + jnp.log(tot)

    def seg(i):
        return pl.BlockSpec((s, HEAD), lambda h: (0, i * (dm // HEAD) + h))

    col = pl.BlockSpec((s, HEAD), lambda h: (0, h))
    return pl.pallas_call(
        body, out_shape=[jax.ShapeDtypeStruct((s, dm), BF16), jax.ShapeDtypeStruct((s, dm), F32)], grid=(dm // HEAD,),
        in_specs=[pl.BlockSpec(memory_space=pltpu.SMEM), seg(0), seg(1), seg(2)], out_specs=[col, col],
        scratch_shapes=[pltpu.VMEM((s, HEAD), F32)] * (2 * len(DILATIONS)),
        compiler_params=_params(("parallel",)), name="attn_fwd")(slopes, proj, proj, proj)


def _attn_bwd(proj, dm, datt, att, lse, slopes, others):
    s = proj.shape[0]
    units = _attn_units(s)

    def body(sl_ref, q_ref, k_ref, v_ref, do_ref, att_ref, lse_ref, o3, o4, o5, o6, out_ref, dq_scr, dk_scr, dv_scr, dl_scr):
        slope = sl_ref[pl.program_id(0)]
        delta = jnp.sum(do_ref[...] * att_ref[...].astype(F32), axis=-1, keepdims=True)
        dl_scr[...] = jnp.broadcast_to(delta, (s, HEAD))
        dq_scr[...] = jnp.zeros_like(dq_scr)
        dk_scr[...] = jnp.zeros_like(dk_scr)
        dv_scr[...] = jnp.zeros_like(dv_scr)
        for first in range(0, len(units), ATTN_BATCH):
            scored = [_attn_scores(q_ref, k_ref, slope, d, r, q0, k0, nk) for _, d, r, q0, k0, nk in units[first:first + ATTN_BATCH]]
            dobs = [do_ref[sco[4], :].astype(BF16) for sco in scored]
            dps = [lax.dot_general(dob, v_ref[sco[5], :].astype(BF16), NT, preferred_element_type=F32) for dob, sco in zip(dobs, scored)]
            ps = [jnp.where(sco[1], jnp.exp(sco[0] - lse_ref[sco[4], :][:, 0:1]), 0.0) for sco in scored]
            dss = [(p * (dp - dl_scr[sco[4], :][:, 0:1]) * (HEAD ** -0.5)).astype(BF16) for p, dp, sco in zip(ps, dps, scored)]
            dqs = [lax.dot_general(ds, sco[3], NN, preferred_element_type=F32) for ds, sco in zip(dss, scored)]
            dks = [lax.dot_general(ds, sco[2], TN, preferred_element_type=F32) for ds, sco in zip(dss, scored)]
            dvs = [lax.dot_general(p.astype(BF16), dob, TN, preferred_element_type=F32) for p, dob in zip(ps, dobs)]
            for sco, dq, dk, dv in zip(scored, dqs, dks, dvs):
                dq_scr[sco[4], :] += dq
                dk_scr[sco[5], :] += dk
                dv_scr[sco[5], :] += dv
        for j, scr in enumerate((dq_scr, dk_scr, dv_scr)):
            out_ref[j] = scr[...].astype(BF16)
        for j, other in enumerate((o3, o4, o5, o6)):
            out_ref[3 + j] = other[...]

    def seg(i):
        return pl.BlockSpec((s, HEAD), lambda h: (0, i * (dm // HEAD) + h))

    col = pl.BlockSpec((s, HEAD), lambda h: (0, h))
    return pl.pallas_call(
        body, out_shape=jax.ShapeDtypeStruct((7, s, dm), BF16), grid=(dm // HEAD,),
        in_specs=[pl.BlockSpec(memory_space=pltpu.SMEM), seg(0), seg(1), seg(2), col, col, col, col, col, col, col],
        out_specs=pl.BlockSpec((7, s, HEAD), lambda h: (0, 0, h)), scratch_shapes=[pltpu.VMEM((s, HEAD), F32)] * 4,
        compiler_params=_params(("parallel",)), name="attn_bwd")(slopes, proj, proj, proj, datt, att, lse, *others)


VEC_CB, VEC_BA, VEC_BX, VEC_LAM = 0, 1, 2, 3
SEG_Q, SEG_K, SEG_V, SEG_X, SEG_GATE, SEG_GA, SEG_GL = range(7)


def _to_3d(ref3, val):
    lw = val.shape[1] // SUBLANES
    for j in range(SUBLANES):
        ref3[:, j, :] = val[:, j * lw:(j + 1) * lw]


def _from_3d(ref3):
    return jnp.concatenate([ref3[:, j, :] for j in range(SUBLANES)], axis=1)


def _softplus(z):
    return jnp.maximum(z, 0.0) + jnp.log1p(jnp.exp(-jnp.abs(z)))


def _gate_math(xc, wa_ref, wx_ref, vec):
    xcb = xc.astype(BF16)
    nh = xc.shape[1] // HEAD
    pre_a = jnp.concatenate([jnp.dot(xcb[:, h * HEAD:(h + 1) * HEAD], wa_ref[h], preferred_element_type=F32) for h in range(nh)], axis=1)
    pre_x = jnp.concatenate([jnp.dot(xcb[:, h * HEAD:(h + 1) * HEAD], wx_ref[h], preferred_element_type=F32) for h in range(nh)], axis=1)
    ra = _sigmoid(pre_a + vec[VEC_BA:VEC_BA + 1])
    ig = _sigmoid(pre_x + vec[VEC_BX:VEC_BX + 1])
    sp = _softplus(-vec[VEC_LAM:VEC_LAM + 1])
    log_a = -LRU_C * ra * sp
    a = jnp.exp(log_a)
    z = 2.0 * log_a
    one_minus_a2 = jnp.where(z > -0.01, -z * (1.0 + z * (0.5 + z * (1.0 / 6.0))), 1.0 - jnp.exp(z))
    mult = jnp.sqrt(one_minus_a2)
    return dict(xcb=xcb, ra=ra, ig=ig, sp=sp, a=a, mult=mult)


def _conv_pad_prev(pad_ref, cur, halo, first):
    pad_ref[0:SUBLANES, :] = jnp.where(first, 0.0, halo)
    pad_ref[SUBLANES:SUBLANES + cur.shape[0], :] = cur


def _gates_fwd(proj, d, cw8, vec8, wa, wx):
    s = proj.shape[0]
    hb = ROWS // SUBLANES

    def body(x_ref, halo_ref, cw_ref, vec_ref, wa_ref, wx_ref, a_ref, u_ref, xc_ref, pad):
        _conv_pad_prev(pad, x_ref[...], halo_ref[...], pl.program_id(0) == 0)
        vec = vec_ref[...]
        xc = vec[VEC_CB:VEC_CB + 1]
        for k in range(CONV_TAPS):
            xc = xc + cw_ref[k:k + 1, :] * pad[pl.ds(SUBLANES - (CONV_TAPS - 1) + k, ROWS), :]
        gm = _gate_math(xc, wa_ref, wx_ref, vec)
        xc_ref[...] = xc
        a_ref[...] = gm["a"]
        u_ref[...] = gm["mult"] * (gm["ig"] * xc)

    wspec = pl.BlockSpec(wa.shape, lambda i: (0, 0, 0))
    return pl.pallas_call(
        body, out_shape=[jax.ShapeDtypeStruct((s, d), F32)] * 3, grid=(s // ROWS,),
        in_specs=[pl.BlockSpec((ROWS, d), lambda i: (i, SEG_X)),
                  pl.BlockSpec((SUBLANES, d), lambda i: (jnp.maximum(i * hb - 1, 0), SEG_X)),
                  _vec_spec(d, SUBLANES), _vec_spec(d, SUBLANES), wspec, wspec],
        out_specs=[_row_spec(d)] * 3, scratch_shapes=[pltpu.VMEM((ROWS + SUBLANES, d), F32)],
        compiler_params=_params(("parallel",)), name="lru_gates_fwd")(proj, proj, cw8, vec8, wa, wx)


def _scan_fwd(a, u, proj):
    s, d = a.shape
    lw = d // SUBLANES

    def body(a_ref, u_ref, g_ref, hp_ref, h2_ref, y_ref, a3, u3, h3, carry):
        @pl.when(pl.program_id(0) == 0)
        def _():
            carry[...] = jnp.zeros_like(carry)

        _to_3d(a3, a_ref[...])
        _to_3d(u3, u_ref[...])

        def step(t, h):
            hp_ref[t] = h
            hn = a3[t] * h + u3[t]
            h3[t] = hn
            return hn

        carry[...] = lax.fori_loop(0, ROWS, step, carry[...], unroll=8)
        h = _from_3d(h3)
        h2_ref[...] = h.astype(BF16)
        y_ref[...] = (h * _gelu(g_ref[...])).astype(BF16)

    spec3 = pl.BlockSpec((ROWS, SUBLANES, lw), lambda i: (i, 0, 0))
    return pl.pallas_call(
        body, out_shape=[jax.ShapeDtypeStruct((s, SUBLANES, lw), F32), jax.ShapeDtypeStruct((s, d), BF16), jax.ShapeDtypeStruct((s, d), BF16)],
        grid=(s // ROWS,), in_specs=[_row_spec(d), _row_spec(d), pl.BlockSpec((ROWS, d), lambda i: (i, SEG_GATE))],
        out_specs=[spec3, _row_spec(d), _row_spec(d)],
        scratch_shapes=[pltpu.VMEM((ROWS, SUBLANES, lw), F32)] * 3 + [pltpu.VMEM((SUBLANES, lw), F32)],
        compiler_params=_params(("arbitrary",)), name="lru_scan_fwd")(a, u, proj)


def _scan_bwd(a, hp3, dh):
    s, d = a.shape
    lw = d // SUBLANES
    nb = s // ROWS

    def body(a_ref, hp_ref, dh_ref, g_ref, da_ref, a3, dh3, g3, da3, carry):
        @pl.when(pl.program_id(0) == 0)
        def _():
            carry[...] = jnp.zeros_like(carry)

        _to_3d(a3, a_ref[...])
        _to_3d(dh3, dh_ref[...])

        def step(j, c):
            t = ROWS - 1 - j
            g = dh3[t] + c
            g3[t] = g
            da3[t] = g * hp_ref[t]
            return a3[t] * g

        carry[...] = lax.fori_loop(0, ROWS, step, carry[...], unroll=8)
        g_ref[...] = _from_3d(g3)
        da_ref[...] = _from_3d(da3)

    rows = pl.BlockSpec((ROWS, d), lambda i: (nb - 1 - i, 0))
    return pl.pallas_call(
        body, out_shape=[jax.ShapeDtypeStruct((s, d), F32)] * 2, grid=(nb,),
        in_specs=[rows, pl.BlockSpec((ROWS, SUBLANES, lw), lambda i: (nb - 1 - i, 0, 0)), rows], out_specs=[rows, rows],
        scratch_shapes=[pltpu.VMEM((ROWS, SUBLANES, lw), F32)] * 4 + [pltpu.VMEM((SUBLANES, lw), F32)],
        compiler_params=_params(("arbitrary",)), name="lru_scan_bwd")(a, hp3, dh)


def _gates_bwd(g, da_in, xc, wa, wx, vec8):
    s, d = xc.shape
    nh = d // HEAD

    def body(g_ref, da_ref, xc_ref, wa_ref, wx_ref, vec_ref, dxc_ref, dwa_ref, dwx_ref, dvec_ref):
        @pl.when(pl.program_id(0) == 0)
        def _():
            dwa_ref[...] = jnp.zeros_like(dwa_ref)
            dwx_ref[...] = jnp.zeros_like(dwx_ref)
            dvec_ref[...] = jnp.zeros_like(dvec_ref)

        xc_v, vec = xc_ref[...], vec_ref[...]
        du, da = g_ref[...], da_ref[...]
        gm = _gate_math(xc_v, wa_ref, wx_ref, vec)
        ra, ig, sp, a, mult = gm["ra"], gm["ig"], gm["sp"], gm["a"], gm["mult"]
        dmult = du * ig * xc_v
        dlog_a = da * a - dmult * (a * a) / mult
        dpre_a = dlog_a * (-LRU_C * sp) * ra * (1.0 - ra)
        dpre_x = du * mult * xc_v * ig * (1.0 - ig)
        dlam = jnp.sum(dlog_a * (-LRU_C * ra), axis=0, keepdims=True) * (-_sigmoid(-vec[VEC_LAM:VEC_LAM + 1]))
        dvec_ref[VEC_BA:VEC_BA + 1, :] += jnp.sum(dpre_a, axis=0, keepdims=True)
        dvec_ref[VEC_BX:VEC_BX + 1, :] += jnp.sum(dpre_x, axis=0, keepdims=True)
        dvec_ref[VEC_LAM:VEC_LAM + 1, :] += dlam
        dab, dxb, xcb = dpre_a.astype(BF16), dpre_x.astype(BF16), gm["xcb"]
        back = []
        for h in range(nh):
            cols = slice(h * HEAD, (h + 1) * HEAD)
            dwa_ref[h] += lax.dot_general(xcb[:, cols], dab[:, cols], TN, preferred_element_type=F32)
            dwx_ref[h] += lax.dot_general(xcb[:, cols], dxb[:, cols], TN, preferred_element_type=F32)
            back.append(lax.dot_general(dab[:, cols], wa_ref[h], NT, preferred_element_type=F32)
                        + lax.dot_general(dxb[:, cols], wx_ref[h], NT, preferred_element_type=F32))
        dxc_ref[...] = du * mult * ig + jnp.concatenate(back, axis=1)

    wspec = pl.BlockSpec(wa.shape, lambda i: (0, 0, 0))
    return pl.pallas_call(
        body, out_shape=[jax.ShapeDtypeStruct((s, d), F32), jax.ShapeDtypeStruct(wa.shape, F32), jax.ShapeDtypeStruct(wa.shape, F32),
                         jax.ShapeDtypeStruct((SUBLANES, d), F32)],
        grid=(s // ROWS,), in_specs=[_row_spec(d), _row_spec(d), _row_spec(d), wspec, wspec, _vec_spec(d, SUBLANES)],
        out_specs=[_row_spec(d), wspec, wspec, _vec_spec(d, SUBLANES)],
        compiler_params=_params(("arbitrary",)), name="lru_gates_bwd")(g, da_in, xc, wa, wx, vec8)


def _conv_bwd(dxc, proj, cw8):
    s, d = dxc.shape
    hb = ROWS // SUBLANES
    last = s // SUBLANES - 1

    def body(dc_ref, dnext_ref, x_ref, xprev_ref, cw_ref, dx_ref, dcw_ref, padd, padx):
        i = pl.program_id(0)

        @pl.when(i == 0)
        def _():
            dcw_ref[...] = jnp.zeros_like(dcw_ref)

        dc = dc_ref[...]
        padd[0:ROWS, :] = dc
        padd[ROWS:ROWS + SUBLANES, :] = jnp.where(i == pl.num_programs(0) - 1, 0.0, dnext_ref[...])
        _conv_pad_prev(padx, x_ref[...], xprev_ref[...], i == 0)
        dx = jnp.zeros_like(dc)
        for k in range(CONV_TAPS):
            dx = dx + cw_ref[k:k + 1, :] * padd[pl.ds(CONV_TAPS - 1 - k, ROWS), :]
            dcw_ref[k:k + 1, :] += jnp.sum(dc * padx[pl.ds(SUBLANES - (CONV_TAPS - 1) + k, ROWS), :], axis=0, keepdims=True)
        dcw_ref[CONV_TAPS:CONV_TAPS + 1, :] += jnp.sum(dc, axis=0, keepdims=True)
        dx_ref[...] = dx.astype(BF16)

    return pl.pallas_call(
        body, out_shape=[jax.ShapeDtypeStruct((s, d), BF16), jax.ShapeDtypeStruct((SUBLANES, d), F32)], grid=(s // ROWS,),
        in_specs=[_row_spec(d), pl.BlockSpec((SUBLANES, d), lambda i: (jnp.minimum((i + 1) * hb, last), 0)),
                  pl.BlockSpec((ROWS, d), lambda i: (i, SEG_X)),
                  pl.BlockSpec((SUBLANES, d), lambda i: (jnp.maximum(i * hb - 1, 0), SEG_X)), _vec_spec(d, SUBLANES)],
        out_specs=[_row_spec(d), _vec_spec(d, SUBLANES)],
        scratch_shapes=[pltpu.VMEM((ROWS + SUBLANES, d), F32), pltpu.VMEM((ROWS + SUBLANES, d), F32)],
        compiler_params=_params(("arbitrary",)), name="lru_conv_bwd")(dxc, dxc, proj, proj, cw8)


def _coords():
    return lax.axis_index("x"), lax.axis_index("y"), lax.axis_index("c")


def _other_chips(x, y):
    return [(1 - x, y), (x, 1 - y), (1 - x, 1 - y)]


def _slab(ref, kind, shard_shape, idx, half=None):
    r, c = shard_shape
    r0, nr = (0, r) if half is None else (half * (r // 2), r // 2)
    if kind == "col":
        return ref.at[pl.ds(r0, nr), pl.ds(pl.multiple_of(idx * c, LANES), c)]
    if kind == "row":
        return ref.at[pl.ds(pl.multiple_of(idx * r, SUBLANES) + r0, nr), :]
    return ref.at[idx, pl.ds(r0, nr), :]


def _full_shape(shard_shape, kind):
    r, c = shard_shape
    return {"col": (r, c * N_DEV), "row": (r * N_DEV, c), "slot": (N_DEV, r, c)}[kind]


def _handshake(peers):
    barrier = pltpu.get_barrier_semaphore()
    for peer in peers:
        pl.semaphore_signal(barrier, inc=1, device_id=peer, device_id_type=MESH)
    pl.semaphore_wait(barrier, len(peers))


def _launch(name, body, out_shape, operands, sems, sequencer_id):
    if sequencer_id is None:
        return pl.pallas_call(body, out_shape=out_shape, in_specs=[HBM] * len(operands), out_specs=[HBM] * len(out_shape),
                              scratch_shapes=sems, name=name)(*operands)
    return pl.kernel(body, out_type=out_shape, mesh=plsc.ScalarSubcoreMesh(axis_name="seq", num_cores=1), name=name,
                     scratch_types=sems, compiler_params=pltpu.CompilerParams(collective_id=sequencer_id))(*operands)


AG_COPIES = 10


def _all_gather(name, shards, kinds, sequencer_id=None):
    n = len(shards)
    shapes = [s.shape for s in shards]

    def body(*refs):
        ins, outs = refs[:n], refs[n:2 * n]
        send_sems, recv_sems, local_sems = refs[2 * n:]
        x, y, c = _coords()
        me, sib, xn, yn, dg = (x, y, c), (x, y, 1 - c), (1 - x, y, c), (x, 1 - y, c), (1 - x, 1 - y, c)
        if sequencer_id is not None:
            _handshake([sib, xn, yn])

        def part(i, dev, half=None):
            return _slab(outs[i], kinds[i], shapes[i], 4 * dev[0] + 2 * dev[1] + dev[2], half)

        def copy(i, k, block, half, to, own=False):
            r = shapes[i][0]
            src = part(i, block, half) if not own else (ins[i] if half is None else ins[i].at[pl.ds(half * (r // 2), r // 2), :])
            return pltpu.make_async_remote_copy(
                src_ref=src, dst_ref=part(i, block, half), send_sem=send_sems.at[AG_COPIES * i + k],
                recv_sem=recv_sems.at[AG_COPIES * i + k], device_id=to, device_id_type=MESH)

        def other_core(dev):
            return (dev[0], dev[1], 1 - c)

        started = []

        def start(cp):
            cp.start()
            started.append(cp)

        for i in range(n):
            start(copy(i, 1, me, 0, xn, own=True))
            start(copy(i, 4, me, 1, yn, own=True))
            start(copy(i, 2, me, 1, xn, own=True))
            start(copy(i, 3, me, 0, yn, own=True))
            start(copy(i, 0, me, None, sib, own=True))
        mine = [pltpu.make_async_copy(ins[i], part(i, me), local_sems.at[i]) for i in range(n)]
        for cp in mine:
            cp.start()
        for i in range(n):
            copy(i, 1, xn, 0, me).wait_recv()
            start(copy(i, 5, xn, 0, yn))
            copy(i, 4, yn, 1, me).wait_recv()
            start(copy(i, 6, yn, 1, xn))
        for i in range(n):
            copy(i, 2, xn, 1, me).wait_recv()
            start(copy(i, 7, xn, None, sib))
            copy(i, 3, yn, 0, me).wait_recv()
            start(copy(i, 8, yn, None, sib))
        for i in range(n):
            copy(i, 5, dg, 0, me).wait_recv()
            copy(i, 6, dg, 1, me).wait_recv()
            start(copy(i, 9, dg, None, sib))
        for i in range(n):
            copy(i, 0, sib, None, me).wait_recv()
            for k, dev in ((7, xn), (8, yn), (9, dg)):
                copy(i, k, other_core(dev), None, me).wait_recv()
        for cp in started:
            cp.wait_send()
        for cp in mine:
            cp.wait()

    out_shape = [jax.ShapeDtypeStruct(_full_shape(s.shape, k), s.dtype) for s, k in zip(shards, kinds)]
    sems = [pltpu.SemaphoreType.DMA((AG_COPIES * n,)), pltpu.SemaphoreType.DMA((AG_COPIES * n,)), pltpu.SemaphoreType.DMA((n,))]
    return _launch(name, body, out_shape, shards, sems, sequencer_id)


def _sibling_copies(kinds, shard_shapes):
    def make(ins, outs, send_sems, recv_sems):
        x, y, c = _coords()
        return [pltpu.make_async_remote_copy(
            src_ref=_slab(ins[i], kinds[i], shard_shapes[i], 2 * q + (1 - c)), dst_ref=outs[i].at[q],
            send_sem=send_sems.at[N_CHIP * i + q], recv_sem=recv_sems.at[N_CHIP * i + q],
            device_id=(x, y, 1 - c), device_id_type=MESH) for i in range(len(ins)) for q in range(N_CHIP)]
    return make


def _sibling_side(partials, kinds, shard_shapes):
    return (partials, [jax.ShapeDtypeStruct((N_CHIP, *s), BF16) for s in shard_shapes], N_CHIP * len(partials),
            _sibling_copies(kinds, shard_shapes))


def _exchange_siblings(name, partials, kinds, shard_shapes, sequencer_id=None):
    n = len(partials)
    make = _sibling_copies(kinds, shard_shapes)

    def body(*refs):
        if sequencer_id is not None:
            x, y, c = _coords()
            _handshake([(x, y, 1 - c)])
        cps = make(refs[:n], refs[n:2 * n], refs[2 * n], refs[2 * n + 1])
        for cp in cps:
            cp.start()
        for cp in cps:
            cp.wait()

    return _launch(name, body, [jax.ShapeDtypeStruct((N_CHIP, *s), BF16) for s in shard_shapes], partials,
                   [pltpu.SemaphoreType.DMA((N_CHIP * n,)), pltpu.SemaphoreType.DMA((N_CHIP * n,))], sequencer_id)


def _exchange_chips(name, chip_sums, sequencer_id=None):
    n = len(chip_sums)

    def body(*refs):
        ins, outs = refs[:n], refs[n:2 * n]
        send_sems, recv_sems = refs[2 * n:]
        x, y, c = _coords()
        if sequencer_id is not None:
            _handshake([(cx, cy, c) for cx, cy in _other_chips(x, y)])
        cps = []
        for i in range(n):
            for k, (cx, cy) in enumerate(_other_chips(x, y)):
                cps.append(pltpu.make_async_remote_copy(
                    src_ref=ins[i].at[2 * cx + cy], dst_ref=outs[i].at[k], send_sem=send_sems.at[3 * i + k],
                    recv_sem=recv_sems.at[3 * i + k], device_id=(cx, cy, c), device_id_type=MESH))
        for cp in cps:
            cp.start()
        for cp in cps:
            cp.wait()

    return _launch(name, body, [jax.ShapeDtypeStruct((3, *t.shape[1:]), BF16) for t in chip_sums], chip_sums,
                   [pltpu.SemaphoreType.DMA((3 * n,)), pltpu.SemaphoreType.DMA((3 * n,))], sequencer_id)


def _all_reduce_small(name, packed):
    rows = packed.shape[0] // N_DEV

    def body(p_ref, out_ref, rb, tot, send_sems, recv_sems):
        x, y, c = _coords()
        me = 4 * x + 2 * y + c

        def peer(k):
            return (x ^ (k >> 2), y ^ ((k >> 1) & 1), c ^ (k & 1))

        def rows_of(idx):
            return pl.ds(pl.multiple_of(idx * rows, SUBLANES), rows)

        def piece(ref, idx):
            return ref.at[rows_of(idx), :]

        scatter = [pltpu.make_async_remote_copy(src_ref=piece(p_ref, me ^ k), dst_ref=rb.at[k], send_sem=send_sems.at[k],
                                                recv_sem=recv_sems.at[k], device_id=peer(k), device_id_type=MESH) for k in range(1, N_DEV)]
        for cp in scatter:
            cp.start()
        acc = p_ref[rows_of(me), :]
        for cp in scatter:
            cp.wait_recv()
        for k in range(1, N_DEV):
            acc = acc + rb[k]
        tot[...] = acc
        out_ref[rows_of(me), :] = acc
        gather = [pltpu.make_async_remote_copy(src_ref=tot, dst_ref=piece(out_ref, me), send_sem=send_sems.at[N_DEV + k],
                                               recv_sem=recv_sems.at[N_DEV + k], device_id=peer(k), device_id_type=MESH)
                  for k in range(1, N_DEV)]
        for cp in gather:
            cp.start()
        for k in range(1, N_DEV):
            pltpu.make_async_remote_copy(src_ref=tot, dst_ref=piece(out_ref, me ^ k), send_sem=send_sems.at[N_DEV + k],
                                         recv_sem=recv_sems.at[N_DEV + k], device_id=peer(k), device_id_type=MESH).wait_recv()
        for cp in scatter + gather:
            cp.wait_send()

    vm = pl.BlockSpec(memory_space=pltpu.VMEM)
    return pl.pallas_call(
        body, out_shape=jax.ShapeDtypeStruct(packed.shape, F32), in_specs=[vm], out_specs=vm,
        scratch_shapes=[pltpu.VMEM((N_DEV, rows, LANES), F32), pltpu.VMEM((rows, LANES), F32),
                        pltpu.SemaphoreType.DMA((2 * N_DEV,)), pltpu.SemaphoreType.DMA((2 * N_DEV,))],
        compiler_params=pltpu.CompilerParams(vmem_limit_bytes=VMEM_LIMIT), name=name)(packed)


def _adamw_math(g, w, m, v):
    m = ADAM_B1 * m + (1.0 - ADAM_B1) * g
    v = ADAM_B2 * v + (1.0 - ADAM_B2) * (g * g)
    delta = -ADAM_LR * ((m / ADAM_C1) / (jnp.sqrt(v / ADAM_C2) + ADAM_EPS) + ADAM_WD * w)
    return delta, m, v


def _slab_spec(kind, shard_shape, tr, slab_of):
    r, c = shard_shape
    if kind == "col":
        return pl.BlockSpec((tr, c), lambda q, i, sc: (i, slab_of(q, sc)))
    return pl.BlockSpec((tr, c), lambda q, i, sc: (slab_of(q, sc) * (r // tr) + i, 0))


def _chip_sum(name, partial, recv, kind, shard_shape, core):
    r, c = shard_shape
    tr = _blk(r, 1024)

    def body(core_ref, p_ref, r_ref, o_ref):
        o_ref[...] = (p_ref[...].astype(F32) + r_ref[...].astype(F32)).astype(BF16)

    spec4 = pl.BlockSpec((None, tr, c), lambda q, i, sc: (q, i, 0))
    grid_spec = pltpu.PrefetchScalarGridSpec(
        num_scalar_prefetch=1, grid=(N_CHIP, r // tr),
        in_specs=[_slab_spec(kind, shard_shape, tr, lambda q, sc: 2 * q + sc[0]), spec4], out_specs=spec4)
    return pl.pallas_call(body, out_shape=jax.ShapeDtypeStruct((N_CHIP, r, c), BF16), grid_spec=grid_spec,
                          compiler_params=_params(("parallel", "parallel")), name=name)(core, partial, recv)


def _adamw_shard(name, parts, w, m, v, chip):
    r, c = w.shape
    n_parts = len(parts)
    tr = _blk(r // n_parts, 256)
    per = r // n_parts // tr

    def body(chip_ref, *refs):
        src, (w_ref, m_ref, v_ref), (g_out, d_out, m_out, v_out) = refs[:2 * n_parts], refs[2 * n_parts:2 * n_parts + 3], refs[2 * n_parts + 3:]
        for p in range(n_parts):
            @pl.when(pl.program_id(0) // per == p)
            def _():
                g = src[2 * p][...].astype(F32)
                for k in range(3):
                    g = g + src[2 * p + 1][k].astype(F32)
                g_out[...] = g
                d_out[...], m_out[...], v_out[...] = _adamw_math(g, w_ref[...], m_ref[...], v_ref[...])

    def part_specs(p):
        at = lambda i: jnp.clip(i - p * per, 0, per - 1)
        return [pl.BlockSpec((None, tr, c), lambda i, sc: (sc[0], at(i), 0)), pl.BlockSpec((3, tr, c), lambda i, sc: (0, at(i), 0))]

    blk = pl.BlockSpec((tr, c), lambda i, sc: (i, 0))
    grid_spec = pltpu.PrefetchScalarGridSpec(
        num_scalar_prefetch=1, grid=(r // tr,), in_specs=[s for p in range(n_parts) for s in part_specs(p)] + [blk, blk, blk], out_specs=[blk] * 4)
    return pl.pallas_call(body, out_shape=[jax.ShapeDtypeStruct((r, c), F32)] * 4, grid_spec=grid_spec,
                          compiler_params=_params(("parallel",)), name=name)(chip, *[a for p in parts for a in p], w, m, v)


def _adamw_small(name, g, w, m, v):
    def body(g_ref, w_ref, m_ref, v_ref, d_out, m_out, v_out):
        d_out[...], m_out[...], v_out[...] = _adamw_math(g_ref[...], w_ref[...], m_ref[...], v_ref[...])

    vm = pl.BlockSpec(memory_space=pltpu.VMEM)
    return pl.pallas_call(body, out_shape=[jax.ShapeDtypeStruct(g.shape, F32)] * 3, in_specs=[vm] * 4, out_specs=[vm] * 3,
                          compiler_params=pltpu.CompilerParams(vmem_limit_bytes=VMEM_LIMIT), name=name)(g, w, m, v)


def _pack_rows(arrays, total_rows):
    flat = [a.reshape(-1, LANES) for a in arrays]
    used = sum(f.shape[0] for f in flat)
    return jnp.concatenate(flat + [jnp.zeros((total_rows - used, LANES), F32)], axis=0)


def _unpack_rows(packed, like):
    out, at = [], 0
    for a in like:
        n = a.size // LANES
        out.append(packed[at:at + n].reshape(a.shape))
        at += n
    return out


def kernel(x, norm_mix_g, w_in, conv_w, conv_b, lru_wa, lru_ba, lru_wx, lru_bx, lru_lambda, w_proj_attn, w_proj_lru, w_out, norm_mlp_g, w_up, w_down, norm_final_g, loss_target, m_norm_mix_g, m_w_in, m_conv_w, m_conv_b, m_lru_wa, m_lru_ba, m_lru_wx, m_lru_bx, m_lru_lambda, m_w_proj_attn, m_w_proj_lru, m_w_out, m_norm_mlp_g, m_w_up, m_w_down, m_norm_final_g, v_norm_mix_g, v_w_in, v_conv_w, v_conv_b, v_lru_wa, v_lru_ba, v_lru_wx, v_lru_bx, v_lru_lambda, v_w_proj_attn, v_w_proj_lru, v_w_out, v_norm_mlp_g, v_w_up, v_w_down, v_norm_final_g):
    xs, tgt = x[0], loss_target[0]
    s, d = xs.shape
    nh = d // HEAD
    ix, iy, ic = _coords()
    core = jnp.reshape(ic, (1,)).astype(jnp.int32)
    chip = jnp.reshape(2 * ix + iy, (1,)).astype(jnp.int32)
    dev = 4 * ix + 2 * iy + ic

    big = [w_in[0], w_proj_attn[0], w_proj_lru[0], w_out[0], w_up[0], w_down[0]]
    big_m = [m_w_in[0], m_w_proj_attn[0], m_w_proj_lru[0], m_w_out[0], m_w_up[0], m_w_down[0]]
    big_v = [v_w_in[0], v_w_proj_attn[0], v_w_proj_lru[0], v_w_out[0], v_w_up[0], v_w_down[0]]
    kinds = ["col", "row", "row", "row", "col", "row"]
    pad_taps = lambda t: jnp.pad(t, ((0, SUBLANES - CONV_TAPS), (0, 0)))
    shards = [w.astype(BF16) for w in big]
    pad_taps2 = lambda t: jnp.pad(t, ((0, 2 * SUBLANES - CONV_TAPS), (0, 0)))
    win, cw_slots = _all_gather("all_gather_w_in", [shards[0], pad_taps2(conv_w[0])], ["col", "slot"])
    later = lax.optimization_barrier((shards[1:], win))[0]
    wpa, wpl, wout = _all_gather("all_gather_mix", later[:3], kinds[1:4], sequencer_id=1)
    wup, wdown = _all_gather("all_gather_mlp", later[3:], kinds[4:], sequencer_id=5)
    cw8 = jnp.transpose(cw_slots[:, :SUBLANES], (1, 0, 2)).reshape(SUBLANES, d)
    row_id = lax.broadcasted_iota(jnp.int32, (SUBLANES, d), 0)
    vec8 = sum(jnp.where(row_id == k, t, 0.0) for k, t in ((VEC_CB, conv_b), (VEC_BA, lru_ba), (VEC_BX, lru_bx), (VEC_LAM, lru_lambda)))
    wa16, wx16 = lru_wa[0].astype(BF16), lru_wx[0].astype(BF16)
    slopes = 2.0 ** (-8.0 * jnp.arange(1, nh + 1, dtype=F32) / nh)

    def seg_specs(*segs):
        return lambda bm, bn: [pl.BlockSpec((bm, bn), (lambda i, j, kk, sg=sg: (i, sg * (d // bn) + j))) for sg in segs]

    def plain_specs(k):
        return lambda bm, bn: [pl.BlockSpec((bm, bn), lambda i, j, kk: (i, j)) for _ in range(k)]

    xn = _rms_fwd("norm_mix", xs, norm_mix_g)
    proj = _mm_fwd("proj_in", xn, win, 0, 7 * d, [F32], bm=2048)[0]
    att, lse = _attn_fwd(proj, d, slopes)
    a3, u3, xc = _gates_fwd(proj, d, cw8, vec8, wa16, wx16)
    hp3, h2d, ylru = _scan_fwd(a3, u3, proj)
    pa = _mm_fwd("proj_attn", att, wpa, 0, d, [BF16], bm=2048)[0]

    def merge(acc, pa_b, ga, gl):
        return acc, _sigmoid(ga) * pa_b.astype(F32) + _sigmoid(gl) * acc

    plr, merged = _mm_fwd("proj_lru_merge", ylru, wpl, 0, d, [BF16, BF16], merge, (pa, proj, proj),
                          lambda bm, bn: plain_specs(1)(bm, bn) + seg_specs(SEG_GA, SEG_GL)(bm, bn), bm=512)
    h1 = _mm_fwd("mix_out", merged, wout, 0, d, [F32], lambda acc, r: (acc + r,), (xs,), plain_specs(1))[0]
    hn = _rms_fwd("norm_mlp", h1, norm_mlp_g)

    def relu2(acc):
        return acc, jnp.square(jnp.maximum(acc, 0.0))

    up, hid = _mm_fwd("mlp_up", hn, wup, 0, wup.shape[1], [BF16, BF16], relu2, bm=2048)
    h2 = _mm_fwd("mlp_down", hid, wdown, 0, d, [F32], lambda acc, r: (acc + r,), (h1,), plain_specs(1))[0]
    dh2, dh2b, dg3, loss_lanes = _final_loss(h2, tgt, norm_final_g.reshape(1, d))
    loss = lax.psum(0.5 / d * jnp.sum(loss_lanes), ("x", "y", "c"))
    dh2b = lax.optimization_barrier((dh2b, loss))[0]

    def reduce_group(tag, kk, shp, partials, from_sibling, sequencer_id):
        sums = [_chip_sum(f"chip_sum_{tag}_{i}", p, f, k, sh, core) for i, (p, f, k, sh) in enumerate(zip(partials, from_sibling, kk, shp))]
        return list(zip(sums, _exchange_chips(f"rs_chips_{tag}", sums, sequencer_id)))

    dup = _mm_nt("mlp_down_dx", dh2b, wdown, [BF16], lambda acc, u: (acc * (2.0 * jnp.maximum(u.astype(F32), 0.0)),), (up,), plain_specs(1))[0]
    g_wdown = _mm_tn("mlp_down_dw", hid, dh2b)
    g_wup = _mm_tn("mlp_up_dw", hn, dup)
    shp_mlp = [w.shape for w in big[4:]]
    (dhn,), sib_mlp = _mm_nt("mlp_up_dx", dup, wup, [F32], side=_sibling_side([g_wup, g_wdown], kinds[4:], shp_mlp))
    red_up, red_down = reduce_group("mlp", kinds[4:], shp_mlp, [g_wup, g_wdown], sib_mlp, 2)
    dhn = lax.optimization_barrier((dhn, red_up[0], red_down[0]))[0]
    dh1, dh1b, dg2 = _rms_bwd("norm_mlp_bwd", h1, norm_mlp_g, dhn, dh2)

    def merge_bwd(acc, pa_b, pl_b, ga, gl):
        sa, sl = _sigmoid(ga), _sigmoid(gl)
        return acc * sa, acc * sl, acc * pa_b.astype(F32) * sa * (1.0 - sa), acc * pl_b.astype(F32) * sl * (1.0 - sl)

    dpa, dpl, dga, dgl = _mm_nt("mix_out_dx", dh1b, wout, [BF16] * 4, merge_bwd, (pa, plr, proj, proj),
                                lambda bm, bn: plain_specs(2)(bm, bn) + seg_specs(SEG_GA, SEG_GL)(bm, bn), bm=512)
    g_wout = _mm_tn("mix_out_dw", merged, dh1b)
    datt = _mm_nt("proj_attn_dx", dpa, wpa, [F32], bm=2048)[0]
    g_wpa = _mm_tn("proj_attn_dw", att, dpa)

    def lru_out_bwd(acc, h_b, gate):
        return acc * _gelu(gate), acc * h_b.astype(F32) * _gelu_grad(gate)

    g_wpl = _mm_tn("proj_lru_dw", ylru, dpl)
    shp_mix = [w.shape for w in big[1:4]]
    (dh, dxg), sib_mix = _mm_nt("proj_lru_dx", dpl, wpl, [F32, BF16], lru_out_bwd, (h2d, proj),
                                lambda bm, bn: plain_specs(1)(bm, bn) + seg_specs(SEG_GATE)(bm, bn), bm=512,
                                side=_sibling_side([g_wpa, g_wpl, g_wout], kinds[1:4], shp_mix))
    red_pa, red_pl, red_out = reduce_group("mix", kinds[1:4], shp_mix, [g_wpa, g_wpl, g_wout], sib_mix, 3)
    g3, da3 = _scan_bwd(a3, hp3, dh)
    dxc, dwa, dwx, dvec = _gates_bwd(g3, da3, xc, wa16, wx16, vec8)
    dxr, dconv = _conv_bwd(dxc, proj, cw8)
    dproj = _attn_bwd(proj, d, datt, att, lse, slopes, (dxr, dxg, dga, dgl))

    def small_step(tag, grads, ws, ms, vs, like, after):
        n_rows = sum(g.size for g in grads) // LANES
        per_dev = -(-n_rows // (N_DEV * SUBLANES)) * SUBLANES
        packed = lax.optimization_barrier((_pack_rows(grads, N_DEV * per_dev), after))[0]
        total = _all_reduce_small(f"all_reduce_{tag}", packed)
        w_rows = -(-(sum(w.size for w in ws) // LANES) // SUBLANES) * SUBLANES
        upd = _adamw_small(f"adamw_{tag}", total[:w_rows], _pack_rows(ws, w_rows), _pack_rows(ms, w_rows), _pack_rows(vs, w_rows))
        return _unpack_rows(total, like), [_unpack_rows(t, ws) for t in upd]

    early_w = [conv_b, lru_wa, lru_ba, lru_wx, lru_bx, lru_lambda, norm_mlp_g, norm_final_g]
    early_m = [m_conv_b, m_lru_wa, m_lru_ba, m_lru_wx, m_lru_bx, m_lru_lambda, m_norm_mlp_g, m_norm_final_g]
    early_v = [v_conv_b, v_lru_wa, v_lru_ba, v_lru_wx, v_lru_bx, v_lru_lambda, v_norm_mlp_g, v_norm_final_g]
    early_g = [dconv[CONV_TAPS:CONV_TAPS + 1], dwa, dvec[VEC_BA:VEC_BA + 1], dwx, dvec[VEC_BX:VEC_BX + 1],
               dvec[VEC_LAM:VEC_LAM + 1], dg2, dg3, dconv[0:CONV_TAPS]]
    dproj = lax.optimization_barrier((dproj, red_up[1], red_down[1]))[0]
    early_sum, early_upd = small_step("small", early_g, early_w, early_m, early_v,
                                      early_w + [jax.ShapeDtypeStruct((1, CONV_TAPS, d), F32)], dproj)
    g_cw_full = early_sum[-1]
    cshard = conv_w.shape[2]
    g_cw = lax.dynamic_slice(g_cw_full, (0, 0, dev * cshard), (1, CONV_TAPS, cshard))
    cw_delta, cw_m, cw_v = (t[:CONV_TAPS][None] for t in _adamw_small(
        "adamw_conv_w", pad_taps(g_cw[0]), pad_taps(conv_w[0]), pad_taps(m_conv_w[0]), pad_taps(v_conv_w[0])))
    dproj = lax.optimization_barrier((dproj, early_sum, red_pa[1], red_pl[1], red_out[1]))[0]
    half = (big[0].shape[0] // 2, big[0].shape[1])
    g_in0 = _mm_tn("proj_in_dw_0", xn, dproj, part=(0, 2))
    g_in1, sib_in0 = _mm_tn("proj_in_dw_1", xn, dproj, part=(1, 2), side=_sibling_side([g_in0], ["col"], [half]))
    red_in = reduce_group("in_0", ["col"], [half], [g_in0], sib_in0, 4)
    dproj = lax.optimization_barrier((dproj, red_in[0][0]))[0]
    (dxn0,), sib_in1 = _mm_nt("proj_in_dx_0", dproj, win, [F32], part=(0, 2), side=_sibling_side([g_in1], ["col"], [half]))
    red_in += reduce_group("in_1", ["col"], [half], [g_in1], sib_in1, 6)
    dproj = lax.optimization_barrier((dproj, red_in[1][0]))[0]
    dxn1 = _mm_nt("proj_in_dx_1", dproj, win, [F32], part=(1, 2))[0]
    dxn = jnp.concatenate([dxn0, dxn1], axis=0)
    dxn = lax.optimization_barrier((dxn, red_in[0][1]))[0]
    grad_x, _, dg1 = _rms_bwd("norm_mix_bwd", xs, norm_mix_g, dxn, dh1)
    red_up, red_down = lax.optimization_barrier(((red_up, red_down), dg1))[0]
    big_out = {i: _adamw_shard(f"adamw_{i}", [red], big[i], big_m[i], big_v[i], chip) for i, red in ((4, red_up), (5, red_down))}
    big_out.update({i: _adamw_shard(f"adamw_{i}", [red], big[i], big_m[i], big_v[i], chip) for i, red in ((1, red_pa), (2, red_pl), (3, red_out))})
    late_sum, late_upd = small_step("norm_mix", [dg1], [norm_mix_g], [m_norm_mix_g], [v_norm_mix_g], [norm_mix_g], (big_out[4], big_out[5]))
    big_out[0] = _adamw_shard("adamw_0", red_in, big[0], big_m[0], big_v[0], chip)
    s_grad = late_sum + early_sum[:-1]
    s_delta, s_m, s_v = (late_upd[j] + early_upd[j] for j in range(3))


    names = ["norm_mix_g", "w_in", "conv_w", "conv_b", "lru_wa", "lru_ba", "lru_wx", "lru_bx", "lru_lambda", "w_proj_attn", "w_proj_lru",
             "w_out", "norm_mlp_g", "w_up", "w_down", "norm_final_g"]
    small_names = ["norm_mix_g", "conv_b", "lru_wa", "lru_ba", "lru_wx", "lru_bx", "lru_lambda", "norm_mlp_g", "norm_final_g"]
    big_names = ["w_in", "w_proj_attn", "w_proj_lru", "w_out", "w_up", "w_down"]
    res = {"conv_w": (g_cw, cw_delta, cw_m, cw_v)}
    for i, nm in enumerate(small_names):
        res[nm] = (s_grad[i], s_delta[i], s_m[i], s_v[i])
    for i, nm in enumerate(big_names):
        res[nm] = tuple(t[None] for t in big_out[i])
    return (loss, grad_x[None], *[res[nm][0] for nm in names], *[res[nm][1] for nm in names],
            *[res[nm][2] for nm in names], *[res[nm][3] for nm in names])
```

```python
import jax
import jax.numpy as jnp
from jax import lax
from jax.experimental import pallas as pl
from jax.experimental.pallas import tpu as pltpu
from jax.experimental.pallas import tpu_sc as plsc

F32, BF16 = jnp.float32, jnp.bfloat16
MESH = pl.DeviceIdType.MESH
HBM = pl.BlockSpec(memory_space=pltpu.HBM)
N_DEV = 8
N_CHIP = 4
HEAD = 128
SPAN = 128
DILATIONS = (1, 4, 16)
CONV_TAPS = 4
LRU_C = 8.0
NORM_EPS = 1e-6
LANES = 128
SUBLANES = 8
VMEM_LIMIT = 56 * 1024 * 1024
ADAM_LR, ADAM_B1, ADAM_B2, ADAM_EPS, ADAM_WD, ADAM_STEP = 0.001, 0.9, 0.999, 1e-08, 0.01, 10
ADAM_C1 = 1.0 - ADAM_B1 ** ADAM_STEP
ADAM_C2 = 1.0 - ADAM_B2 ** ADAM_STEP
NEG = -1e30


def _params(sem=None):
    return pltpu.CompilerParams(dimension_semantics=sem, vmem_limit_bytes=VMEM_LIMIT)


def _sigmoid(v):
    return 1.0 / (1.0 + jnp.exp(-v))


def _gelu(v):
    k = 0.7978845608028654
    return 0.5 * v * (1.0 + jnp.tanh(k * (v + 0.044715 * v * v * v)))


def _gelu_grad(v):
    k = 0.7978845608028654
    t = jnp.tanh(k * (v + 0.044715 * v * v * v))
    return 0.5 * (1.0 + t) + 0.5 * v * (1.0 - t * t) * k * (1.0 + 3.0 * 0.044715 * v * v)


NN = (((1,), (0,)), ((), ()))
NT = (((1,), (1,)), ((), ()))
TN = (((0,), (0,)), ((), ()))


def _mm(name, a, a_spec, b, b_spec, dn, grid, out_shapes, out_specs, acc_block, epilogue=None, extras=(), extra_specs=(), side=None):
    nk, ne, no = grid[2], len(extras), len(out_shapes)
    side_ops, side_shapes, side_copies, make_copies = side if side is not None else ((), (), 0, None)
    ns_in, ns_out = len(side_ops), len(side_shapes)

    def body(*refs):
        a_ref, b_ref = refs[0], refs[1]
        ex, side_in = refs[2:2 + ne], refs[2 + ne:2 + ne + ns_in]
        outs = refs[2 + ne + ns_in:2 + ne + ns_in + no]
        side_out = refs[2 + ne + ns_in + no:2 + ne + ns_in + no + ns_out]
        scratch = refs[2 + ne + ns_in + no + ns_out:]
        at = [pl.program_id(ax) for ax in range(3)]
        if side is not None:
            @pl.when((at[0] == 0) & (at[1] == 0) & (at[2] == 0))
            def _():
                for cp in make_copies(side_in, side_out, scratch[-2], scratch[-1]):
                    cp.start()

        part = lax.dot_general(a_ref[...], b_ref[...], dn, preferred_element_type=F32)

        def finish(acc):
            vals = epilogue(acc, *[e[...] for e in ex]) if epilogue is not None else (acc,)
            for o, v in zip(outs, vals):
                o[...] = v.astype(o.dtype)

        if nk == 1:
            finish(part)
        else:
            acc_ref, k = scratch[0], at[2]

            @pl.when(k == 0)
            def _():
                acc_ref[...] = part

            @pl.when(k > 0)
            def _():
                acc_ref[...] += part

            @pl.when(k == nk - 1)
            def _():
                finish(acc_ref[...])

        if side is not None:
            @pl.when((at[0] == grid[0] - 1) & (at[1] == grid[1] - 1) & (at[2] == grid[2] - 1))
            def _():
                for cp in make_copies(side_in, side_out, scratch[-2], scratch[-1]):
                    cp.wait()

    scratch_shapes = [pltpu.VMEM(acc_block, F32)] if nk > 1 else []
    if side is not None:
        scratch_shapes += [pltpu.SemaphoreType.DMA((side_copies,)), pltpu.SemaphoreType.DMA((side_copies,))]
    res = pl.pallas_call(
        body, out_shape=[*out_shapes, *side_shapes], grid=grid, in_specs=[a_spec, b_spec, *extra_specs, *[HBM] * ns_in],
        out_specs=[*out_specs, *[HBM] * ns_out], scratch_shapes=scratch_shapes,
        compiler_params=_params(("arbitrary",) * 3 if side is not None else ("parallel", "parallel", "arbitrary")),
        name=name)(a, b, *extras, *side_ops)
    return res if side is None else (res[:no], res[no:])


def _blk(n, pref):
    return pref if n % pref == 0 else n


def _kblk(k):
    return k if k <= 2048 else next(b for b in (2048, 1024, 512) if k % b == 0)


def _mm_fwd(name, a, w, col0, ncols, out_dtypes, epilogue=None, extras=(), extra_specs_fn=None, seg_out=None, bm=1024, bn=1024):
    m, k = a.shape
    bm, bn = _blk(m, bm), _blk(ncols, bn)
    bk = _kblk(k)
    nk = k // bk
    cb0 = col0 // bn
    grid = (m // bm, ncols // bn, nk)
    a_spec = pl.BlockSpec((bm, bk), lambda i, j, kk: (i, kk))
    b_spec = pl.BlockSpec((bk, bn), lambda i, j, kk: (kk, cb0 + j))
    if seg_out is None:
        shapes = [jax.ShapeDtypeStruct((m, ncols), dt) for dt in out_dtypes]
        specs = [pl.BlockSpec((bm, bn), lambda i, j, kk: (i, j)) for _ in out_dtypes]
    else:
        per = seg_out // bn
        shapes = [jax.ShapeDtypeStruct((ncols // seg_out, m, seg_out), dt) for dt in out_dtypes]
        specs = [pl.BlockSpec((None, bm, bn), lambda i, j, kk: (j // per, i, j % per)) for _ in out_dtypes]
    ex_specs = extra_specs_fn(bm, bn) if extra_specs_fn else ()
    return _mm(name, a, a_spec, w, b_spec, NN, grid, shapes, specs, (bm, bn), epilogue, extras, ex_specs)


def _mm_nt(name, a, w, out_dtypes, epilogue=None, extras=(), extra_specs_fn=None, part=(0, 1), side=None, bm=1024, bn=1024):
    n = w.shape[0]
    if a.ndim == 3:
        seg_cols, m, k = a.shape[2], a.shape[1], a.shape[0] * a.shape[2]
    else:
        m, k = a.shape
    m = m // part[1]
    bm, bn = _blk(m, bm), _blk(n, bn)
    bk = _kblk(k)
    grid = (m // bm, n // bn, k // bk)
    i0 = part[0] * (m // bm)
    if a.ndim == 3:
        per = seg_cols // bk
        a_spec = pl.BlockSpec((None, bm, bk), lambda i, j, kk: (kk // per, i0 + i, kk % per))
    else:
        a_spec = pl.BlockSpec((bm, bk), lambda i, j, kk: (i0 + i, kk))
    b_spec = pl.BlockSpec((bn, bk), lambda i, j, kk: (j, kk))
    shapes = [jax.ShapeDtypeStruct((m, n), dt) for dt in out_dtypes]
    specs = [pl.BlockSpec((bm, bn), lambda i, j, kk: (i, j)) for _ in out_dtypes]
    ex_specs = extra_specs_fn(bm, bn) if extra_specs_fn else ()
    return _mm(name, a, a_spec, w, b_spec, NT, grid, shapes, specs, (bm, bn), epilogue, extras, ex_specs, side)


def _mm_tn(name, a, b, part=(0, 1), side=None, bm=1024, bn=2048):
    t, m = a.shape
    n = b.shape[1] if b.ndim == 2 else b.shape[0] * b.shape[2]
    m = m // part[1]
    bm, bn = _blk(m, bm), _blk(n, bn)
    grid = (m // bm, n // bn, 1)
    i0 = part[0] * (m // bm)
    a_spec = pl.BlockSpec((t, bm), lambda i, j, kk: (0, i0 + i))
    if b.ndim == 3:
        per = b.shape[2] // bn
        b_spec = pl.BlockSpec((None, t, bn), lambda i, j, kk: (j // per, 0, j % per))
    else:
        b_spec = pl.BlockSpec((t, bn), lambda i, j, kk: (0, j))
    res = _mm(name, a, a_spec, b, b_spec, TN, grid, [jax.ShapeDtypeStruct((m, n), BF16)],
              [pl.BlockSpec((bm, bn), lambda i, j, kk: (i, j))], (bm, bn), side=side)
    return res[0] if side is None else (res[0][0], res[1])


ROWS = 256


def _row_spec(d):
    return pl.BlockSpec((ROWS, d), lambda i: (i, 0))


def _vec_spec(d, rows=1):
    return pl.BlockSpec((rows, d), lambda i: (0, 0))


def _rms_fwd(name, x, g):
    s, d = x.shape

    def body(x_ref, g_ref, o_ref):
        xv = x_ref[...]
        r = lax.rsqrt(jnp.mean(xv * xv, axis=-1, keepdims=True) + NORM_EPS)
        o_ref[...] = (xv * r * g_ref[...]).astype(BF16)

    return pl.pallas_call(body, out_shape=jax.ShapeDtypeStruct((s, d), BF16), grid=(s // ROWS,),
                          in_specs=[_row_spec(d), _vec_spec(d)], out_specs=_row_spec(d),
                          compiler_params=_params(("parallel",)), name=name)(x, g)


def _rms_bwd_math(xv, g, dy):
    r = lax.rsqrt(jnp.mean(xv * xv, axis=-1, keepdims=True) + NORM_EPS)
    n = xv * r
    z = dy * g
    dx = r * (z - n * jnp.mean(z * n, axis=-1, keepdims=True))
    return dx, jnp.sum(dy * n, axis=0, keepdims=True)


def _rms_bwd(name, x, g, dy, resid):
    s, d = x.shape

    def body(x_ref, g_ref, dy_ref, r_ref, dx_ref, dxb_ref, dg_ref):
        dx, dg = _rms_bwd_math(x_ref[...], g_ref[...], dy_ref[...])
        dx = dx + r_ref[...]
        dx_ref[...] = dx
        dxb_ref[...] = dx.astype(BF16)

        @pl.when(pl.program_id(0) == 0)
        def _():
            dg_ref[...] = jnp.zeros_like(dg_ref)

        dg_ref[...] += dg

    return pl.pallas_call(
        body, out_shape=[jax.ShapeDtypeStruct((s, d), F32), jax.ShapeDtypeStruct((s, d), BF16), jax.ShapeDtypeStruct((1, d), F32)],
        grid=(s // ROWS,), in_specs=[_row_spec(d), _vec_spec(d), _row_spec(d), _row_spec(d)],
        out_specs=[_row_spec(d), _row_spec(d), _vec_spec(d)], compiler_params=_params(("arbitrary",)), name=name)(x, g, dy, resid)


def _final_loss(h2, tgt, g):
    s, d = h2.shape

    def body(x_ref, t_ref, g_ref, dx_ref, dxb_ref, dg_ref, ls_ref):
        xv, gv = x_ref[...], g_ref[...]
        r = lax.rsqrt(jnp.mean(xv * xv, axis=-1, keepdims=True) + NORM_EPS)
        diff = xv * r * gv - t_ref[...]
        dx, dg = _rms_bwd_math(xv, gv, diff * (1.0 / d))
        dx_ref[...] = dx
        dxb_ref[...] = dx.astype(BF16)

        @pl.when(pl.program_id(0) == 0)
        def _():
            dg_ref[...] = jnp.zeros_like(dg_ref)
            ls_ref[...] = jnp.zeros_like(ls_ref)

        dg_ref[...] += dg
        ls_ref[...] += jnp.sum(diff * diff, axis=0, keepdims=True)

    return pl.pallas_call(
        body, out_shape=[jax.ShapeDtypeStruct((s, d), F32), jax.ShapeDtypeStruct((s, d), BF16),
                         jax.ShapeDtypeStruct((1, d), F32), jax.ShapeDtypeStruct((1, d), F32)],
        grid=(s // ROWS,), in_specs=[_row_spec(d), _row_spec(d), _vec_spec(d)],
        out_specs=[_row_spec(d), _row_spec(d), _vec_spec(d), _vec_spec(d)],
        compiler_params=_params(("arbitrary",)), name="final_norm_loss")(h2, tgt, g)


ATTN_Q = 128


ATTN_BATCH = 8


def _attn_units(s):
    units = []
    for gi, d in enumerate(DILATIONS):
        for r in range(d):
            for q0 in range(0, s // d, ATTN_Q):
                k0 = max(q0 - SPAN, 0)
                units.append((gi, d, r, q0, k0, q0 + ATTN_Q - k0))
    return units


def _stream_rows(d, r, start, size):
    return pl.ds(r + start * d, size) if d == 1 else pl.ds(r + start * d, size, stride=d)


def _attn_scores(q_ref, k_ref, slope, d, r, q0, k0, nk):
    qrows, krows = _stream_rows(d, r, q0, ATTN_Q), _stream_rows(d, r, k0, nk)
    qb = q_ref[qrows, :].astype(BF16)
    kb = k_ref[krows, :].astype(BF16)
    sc = lax.dot_general(qb, kb, NT, preferred_element_type=F32) * (HEAD ** -0.5)
    qi = lax.broadcasted_iota(jnp.int32, (ATTN_Q, nk), 0)
    kj = lax.broadcasted_iota(jnp.int32, (ATTN_Q, nk), 1)
    dist = (q0 - k0) + qi - kj
    valid = (dist >= 0) & (dist <= SPAN)
    sc = sc - (slope * d) * dist.astype(F32)
    return jnp.where(valid, sc, NEG), valid, qb, kb, qrows, krows


def _attn_fwd(proj, dm, slopes):
    s = proj.shape[0]
    units = _attn_units(s)

    def body(sl_ref, q_ref, k_ref, v_ref, att_ref, lse_ref, *scr):
        o_scr, l_scr = scr[:3], scr[3:]
        slope = sl_ref[pl.program_id(0)]
        for first in range(0, len(units), ATTN_BATCH):
            batch = units[first:first + ATTN_BATCH]
            scored = [_attn_scores(q_ref, k_ref, slope, d, r, q0, k0, nk) for _, d, r, q0, k0, nk in batch]
            soft = []
            for sc, _, _, _, _, _ in scored:
                m = jnp.max(sc, axis=-1, keepdims=True)
                p = jnp.exp(sc - m)
                soft.append((m, p, jnp.sum(p, axis=-1, keepdims=True)))
            outs = [lax.dot_general(p.astype(BF16), v_ref[sco[5], :].astype(BF16), NN, preferred_element_type=F32)
                    for (m, p, l), sco in zip(soft, scored)]
            for (gi, *_), (m, p, l), sco, o in zip(batch, soft, scored, outs):
                o_scr[gi][sco[4], :] = o / l
                l_scr[gi][sco[4], :] = jnp.broadcast_to(m + jnp.log(l), (ATTN_Q, HEAD))
        l0, l1, l2 = l_scr[0][...], l_scr[1][...], l_scr[2][...]
        m = jnp.maximum(jnp.maximum(l0, l1), l2)
        w0, w1, w2 = jnp.exp(l0 - m), jnp.exp(l1 - m), jnp.exp(l2 - m)
        tot = w0 + w1 + w2
        att_ref[...] = ((w0 * o_scr[0][...] + w1 * o_scr[1][...] + w2 * o_scr[2][...]) / tot).astype(BF16)
        lse_ref[...] = m + jnp.log(tot)

    def seg(i):
        return pl.BlockSpec((s, HEAD), lambda h: (0, i * (dm // HEAD) + h))

    col = pl.BlockSpec((s, HEAD), lambda h: (0, h))
    return pl.pallas_call(
        body, out_shape=[jax.ShapeDtypeStruct((s, dm), BF16), jax.ShapeDtypeStruct((s, dm), F32)], grid=(dm // HEAD,),
        in_specs=[pl.BlockSpec(memory_space=pltpu.SMEM), seg(0), seg(1), seg(2)], out_specs=[col, col],
        scratch_shapes=[pltpu.VMEM((s, HEAD), F32)] * (2 * len(DILATIONS)),
        compiler_params=_params(("parallel",)), name="attn_fwd")(slopes, proj, proj, proj)


def _attn_bwd(proj, dm, datt, att, lse, slopes, others):
    s = proj.shape[0]
    units = _attn_units(s)

    def body(sl_ref, q_ref, k_ref, v_ref, do_ref, att_ref, lse_ref, o3, o4, o5, o6, out_ref, dq_scr, dk_scr, dv_scr, dl_scr):
        slope = sl_ref[pl.program_id(0)]
        delta = jnp.sum(do_ref[...] * att_ref[...].astype(F32), axis=-1, keepdims=True)
        dl_scr[...] = jnp.broadcast_to(delta, (s, HEAD))
        dq_scr[...] = jnp.zeros_like(dq_scr)
        dk_scr[...] = jnp.zeros_like(dk_scr)
        dv_scr[...] = jnp.zeros_like(dv_scr)
        for first in range(0, len(units), ATTN_BATCH):
            scored = [_attn_scores(q_ref, k_ref, slope, d, r, q0, k0, nk) for _, d, r, q0, k0, nk in units[first:first + ATTN_BATCH]]
            dobs = [do_ref[sco[4], :].astype(BF16) for sco in scored]
            dps = [lax.dot_general(dob, v_ref[sco[5], :].astype(BF16), NT, preferred_element_type=F32) for dob, sco in zip(dobs, scored)]
            ps = [jnp.where(sco[1], jnp.exp(sco[0] - lse_ref[sco[4], :][:, 0:1]), 0.0) for sco in scored]
            dss = [(p * (dp - dl_scr[sco[4], :][:, 0:1]) * (HEAD ** -0.5)).astype(BF16) for p, dp, sco in zip(ps, dps, scored)]
            dqs = [lax.dot_general(ds, sco[3], NN, preferred_element_type=F32) for ds, sco in zip(dss, scored)]
            dks = [lax.dot_general(ds, sco[2], TN, preferred_element_type=F32) for ds, sco in zip(dss, scored)]
            dvs = [lax.dot_general(p.astype(BF16), dob, TN, preferred_element_type=F32) for p, dob in zip(ps, dobs)]
            for sco, dq, dk, dv in zip(scored, dqs, dks, dvs):
                dq_scr[sco[4], :] += dq
                dk_scr[sco[5], :] += dk
                dv_scr[sco[5], :] += dv
        for j, scr in enumerate((dq_scr, dk_scr, dv_scr)):
            out_ref[j] = scr[...].astype(BF16)
        for j, other in enumerate((o3, o4, o5, o6)):
            out_ref[3 + j] = other[...]

    def seg(i):
        return pl.BlockSpec((s, HEAD), lambda h: (0, i * (dm // HEAD) + h))

    col = pl.BlockSpec((s, HEAD), lambda h: (0, h))
    return pl.pallas_call(
        body, out_shape=jax.ShapeDtypeStruct((7, s, dm), BF16), grid=(dm // HEAD,),
        in_specs=[pl.BlockSpec(memory_space=pltpu.SMEM), seg(0), seg(1), seg(2), col, col, col, col, col, col, col],
        out_specs=pl.BlockSpec((7, s, HEAD), lambda h: (0, 0, h)), scratch_shapes=[pltpu.VMEM((s, HEAD), F32)] * 4,
        compiler_params=_params(("parallel",)), name="attn_bwd")(slopes, proj, proj, proj, datt, att, lse, *others)


VEC_CB, VEC_BA, VEC_BX, VEC_LAM = 0, 1, 2, 3
SEG_Q, SEG_K, SEG_V, SEG_X, SEG_GATE, SEG_GA, SEG_GL = range(7)


def _to_3d(ref3, val):
    lw = val.shape[1] // SUBLANES
    for j in range(SUBLANES):
        ref3[:, j, :] = val[:, j * lw:(j + 1) * lw]


def _from_3d(ref3):
    return jnp.concatenate([ref3[:, j, :] for j in range(SUBLANES)], axis=1)


def _softplus(z):
    return jnp.maximum(z, 0.0) + jnp.log1p(jnp.exp(-jnp.abs(z)))


def _gate_math(xc, wa_ref, wx_ref, vec):
    xcb = xc.astype(BF16)
    nh = xc.shape[1] // HEAD
    pre_a = jnp.concatenate([jnp.dot(xcb[:, h * HEAD:(h + 1) * HEAD], wa_ref[h], preferred_element_type=F32) for h in range(nh)], axis=1)
    pre_x = jnp.concatenate([jnp.dot(xcb[:, h * HEAD:(h + 1) * HEAD], wx_ref[h], preferred_element_type=F32) for h in range(nh)], axis=1)
    ra = _sigmoid(pre_a + vec[VEC_BA:VEC_BA + 1])
    ig = _sigmoid(pre_x + vec[VEC_BX:VEC_BX + 1])
    sp = _softplus(-vec[VEC_LAM:VEC_LAM + 1])
    log_a = -LRU_C * ra * sp
    a = jnp.exp(log_a)
    z = 2.0 * log_a
    one_minus_a2 = jnp.where(z > -0.01, -z * (1.0 + z * (0.5 + z * (1.0 / 6.0))), 1.0 - jnp.exp(z))
    mult = jnp.sqrt(one_minus_a2)
    return dict(xcb=xcb, ra=ra, ig=ig, sp=sp, a=a, mult=mult)


def _conv_pad_prev(pad_ref, cur, halo, first):
    pad_ref[0:SUBLANES, :] = jnp.where(first, 0.0, halo)
    pad_ref[SUBLANES:SUBLANES + cur.shape[0], :] = cur


def _gates_fwd(proj, d, cw8, vec8, wa, wx):
    s = proj.shape[0]
    hb = ROWS // SUBLANES

    def body(x_ref, halo_ref, cw_ref, vec_ref, wa_ref, wx_ref, a_ref, u_ref, xc_ref, pad):
        _conv_pad_prev(pad, x_ref[...], halo_ref[...], pl.program_id(0) == 0)
        vec = vec_ref[...]
        xc = vec[VEC_CB:VEC_CB + 1]
        for k in range(CONV_TAPS):
            xc = xc + cw_ref[k:k + 1, :] * pad[pl.ds(SUBLANES - (CONV_TAPS - 1) + k, ROWS), :]
        gm = _gate_math(xc, wa_ref, wx_ref, vec)
        xc_ref[...] = xc
        a_ref[...] = gm["a"]
        u_ref[...] = gm["mult"] * (gm["ig"] * xc)

    wspec = pl.BlockSpec(wa.shape, lambda i: (0, 0, 0))
    return pl.pallas_call(
        body, out_shape=[jax.ShapeDtypeStruct((s, d), F32)] * 3, grid=(s // ROWS,),
        in_specs=[pl.BlockSpec((ROWS, d), lambda i: (i, SEG_X)),
                  pl.BlockSpec((SUBLANES, d), lambda i: (jnp.maximum(i * hb - 1, 0), SEG_X)),
                  _vec_spec(d, SUBLANES), _vec_spec(d, SUBLANES), wspec, wspec],
        out_specs=[_row_spec(d)] * 3, scratch_shapes=[pltpu.VMEM((ROWS + SUBLANES, d), F32)],
        compiler_params=_params(("parallel",)), name="lru_gates_fwd")(proj, proj, cw8, vec8, wa, wx)


def _scan_fwd(a, u, proj):
    s, d = a.shape
    lw = d // SUBLANES

    def body(a_ref, u_ref, g_ref, hp_ref, h2_ref, y_ref, a3, u3, h3, carry):
        @pl.when(pl.program_id(0) == 0)
        def _():
            carry[...] = jnp.zeros_like(carry)

        _to_3d(a3, a_ref[...])
        _to_3d(u3, u_ref[...])

        def step(t, h):
            hp_ref[t] = h
            hn = a3[t] * h + u3[t]
            h3[t] = hn
            return hn

        carry[...] = lax.fori_loop(0, ROWS, step, carry[...], unroll=8)
        h = _from_3d(h3)
        h2_ref[...] = h.astype(BF16)
        y_ref[...] = (h * _gelu(g_ref[...])).astype(BF16)

    spec3 = pl.BlockSpec((ROWS, SUBLANES, lw), lambda i: (i, 0, 0))
    return pl.pallas_call(
        body, out_shape=[jax.ShapeDtypeStruct((s, SUBLANES, lw), F32), jax.ShapeDtypeStruct((s, d), BF16), jax.ShapeDtypeStruct((s, d), BF16)],
        grid=(s // ROWS,), in_specs=[_row_spec(d), _row_spec(d), pl.BlockSpec((ROWS, d), lambda i: (i, SEG_GATE))],
        out_specs=[spec3, _row_spec(d), _row_spec(d)],
        scratch_shapes=[pltpu.VMEM((ROWS, SUBLANES, lw), F32)] * 3 + [pltpu.VMEM((SUBLANES, lw), F32)],
        compiler_params=_params(("arbitrary",)), name="lru_scan_fwd")(a, u, proj)


def _scan_bwd(a, hp3, dh):
    s, d = a.shape
    lw = d // SUBLANES
    nb = s // ROWS

    def body(a_ref, hp_ref, dh_ref, g_ref, da_ref, a3, dh3, g3, da3, carry):
        @pl.when(pl.program_id(0) == 0)
        def _():
            carry[...] = jnp.zeros_like(carry)

        _to_3d(a3, a_ref[...])
        _to_3d(dh3, dh_ref[...])

        def step(j, c):
            t = ROWS - 1 - j
            g = dh3[t] + c
            g3[t] = g
            da3[t] = g * hp_ref[t]
            return a3[t] * g

        carry[...] = lax.fori_loop(0, ROWS, step, carry[...], unroll=8)
        g_ref[...] = _from_3d(g3)
        da_ref[...] = _from_3d(da3)

    rows = pl.BlockSpec((ROWS, d), lambda i: (nb - 1 - i, 0))
    return pl.pallas_call(
        body, out_shape=[jax.ShapeDtypeStruct((s, d), F32)] * 2, grid=(nb,),
        in_specs=[rows, pl.BlockSpec((ROWS, SUBLANES, lw), lambda i: (nb - 1 - i, 0, 0)), rows], out_specs=[rows, rows],
        scratch_shapes=[pltpu.VMEM((ROWS, SUBLANES, lw), F32)] * 4 + [pltpu.VMEM((SUBLANES, lw), F32)],
        compiler_params=_params(("arbitrary",)), name="lru_scan_bwd")(a, hp3, dh)


def _gates_bwd(g, da_in, xc, wa, wx, vec8):
    s, d = xc.shape
    nh = d // HEAD

    def body(g_ref, da_ref, xc_ref, wa_ref, wx_ref, vec_ref, dxc_ref, dwa_ref, dwx_ref, dvec_ref):
        @pl.when(pl.program_id(0) == 0)
        def _():
            dwa_ref[...] = jnp.zeros_like(dwa_ref)
            dwx_ref[...] = jnp.zeros_like(dwx_ref)
            dvec_ref[...] = jnp.zeros_like(dvec_ref)

        xc_v, vec = xc_ref[...], vec_ref[...]
        du, da = g_ref[...], da_ref[...]
        gm = _gate_math(xc_v, wa_ref, wx_ref, vec)
        ra, ig, sp, a, mult = gm["ra"], gm["ig"], gm["sp"], gm["a"], gm["mult"]
        dmult = du * ig * xc_v
        dlog_a = da * a - dmult * (a * a) / mult
        dpre_a = dlog_a * (-LRU_C * sp) * ra * (1.0 - ra)
        dpre_x = du * mult * xc_v * ig * (1.0 - ig)
        dlam = jnp.sum(dlog_a * (-LRU_C * ra), axis=0, keepdims=True) * (-_sigmoid(-vec[VEC_LAM:VEC_LAM + 1]))
        dvec_ref[VEC_BA:VEC_BA + 1, :] += jnp.sum(dpre_a, axis=0, keepdims=True)
        dvec_ref[VEC_BX:VEC_BX + 1, :] += jnp.sum(dpre_x, axis=0, keepdims=True)
        dvec_ref[VEC_LAM:VEC_LAM + 1, :] += dlam
        dab, dxb, xcb = dpre_a.astype(BF16), dpre_x.astype(BF16), gm["xcb"]
        back = []
        for h in range(nh):
            cols = slice(h * HEAD, (h + 1) * HEAD)
            dwa_ref[h] += lax.dot_general(xcb[:, cols], dab[:, cols], TN, preferred_element_type=F32)
            dwx_ref[h] += lax.dot_general(xcb[:, cols], dxb[:, cols], TN, preferred_element_type=F32)
            back.append(lax.dot_general(dab[:, cols], wa_ref[h], NT, preferred_element_type=F32)
                        + lax.dot_general(dxb[:, cols], wx_ref[h], NT, preferred_element_type=F32))
        dxc_ref[...] = du * mult * ig + jnp.concatenate(back, axis=1)

    wspec = pl.BlockSpec(wa.shape, lambda i: (0, 0, 0))
    return pl.pallas_call(
        body, out_shape=[jax.ShapeDtypeStruct((s, d), F32), jax.ShapeDtypeStruct(wa.shape, F32), jax.ShapeDtypeStruct(wa.shape, F32),
                         jax.ShapeDtypeStruct((SUBLANES, d), F32)],
        grid=(s // ROWS,), in_specs=[_row_spec(d), _row_spec(d), _row_spec(d), wspec, wspec, _vec_spec(d, SUBLANES)],
        out_specs=[_row_spec(d), wspec, wspec, _vec_spec(d, SUBLANES)],
        compiler_params=_params(("arbitrary",)), name="lru_gates_bwd")(g, da_in, xc, wa, wx, vec8)


def _conv_bwd(dxc, proj, cw8):
    s, d = dxc.shape
    hb = ROWS // SUBLANES
    last = s // SUBLANES - 1

    def body(dc_ref, dnext_ref, x_ref, xprev_ref, cw_ref, dx_ref, dcw_ref, padd, padx):
        i = pl.program_id(0)

        @pl.when(i == 0)
        def _():
            dcw_ref[...] = jnp.zeros_like(dcw_ref)

        dc = dc_ref[...]
        padd[0:ROWS, :] = dc
        padd[ROWS:ROWS + SUBLANES, :] = jnp.where(i == pl.num_programs(0) - 1, 0.0, dnext_ref[...])
        _conv_pad_prev(padx, x_ref[...], xprev_ref[...], i == 0)
        dx = jnp.zeros_like(dc)
        for k in range(CONV_TAPS):
            dx = dx + cw_ref[k:k + 1, :] * padd[pl.ds(CONV_TAPS - 1 - k, ROWS), :]
            dcw_ref[k:k + 1, :] += jnp.sum(dc * padx[pl.ds(SUBLANES - (CONV_TAPS - 1) + k, ROWS), :], axis=0, keepdims=True)
        dcw_ref[CONV_TAPS:CONV_TAPS + 1, :] += jnp.sum(dc, axis=0, keepdims=True)
        dx_ref[...] = dx.astype(BF16)

    return pl.pallas_call(
        body, out_shape=[jax.ShapeDtypeStruct((s, d), BF16), jax.ShapeDtypeStruct((SUBLANES, d), F32)], grid=(s // ROWS,),
        in_specs=[_row_spec(d), pl.BlockSpec((SUBLANES, d), lambda i: (jnp.minimum((i + 1) * hb, last), 0)),
                  pl.BlockSpec((ROWS, d), lambda i: (i, SEG_X)),
                  pl.BlockSpec((SUBLANES, d), lambda i: (jnp.maximum(i * hb - 1, 0), SEG_X)), _vec_spec(d, SUBLANES)],
        out_specs=[_row_spec(d), _vec_spec(d, SUBLANES)],
        scratch_shapes=[pltpu.VMEM((ROWS + SUBLANES, d), F32), pltpu.VMEM((ROWS + SUBLANES, d), F32)],
        compiler_params=_params(("arbitrary",)), name="lru_conv_bwd")(dxc, dxc, proj, proj, cw8)


def _coords():
    return lax.axis_index("x"), lax.axis_index("y"), lax.axis_index("c")


def _other_chips(x, y):
    return [(1 - x, y), (x, 1 - y), (1 - x, 1 - y)]


def _slab(ref, kind, shard_shape, idx, half=None):
    r, c = shard_shape
    r0, nr = (0, r) if half is None else (half * (r // 2), r // 2)
    if kind == "col":
        return ref.at[pl.ds(r0, nr), pl.ds(pl.multiple_of(idx * c, LANES), c)]
    if kind == "row":
        return ref.at[pl.ds(pl.multiple_of(idx * r, SUBLANES) + r0, nr), :]
    return ref.at[idx, pl.ds(r0, nr), :]


def _full_shape(shard_shape, kind):
    r, c = shard_shape
    return {"col": (r, c * N_DEV), "row": (r * N_DEV, c), "slot": (N_DEV, r, c)}[kind]


def _handshake(peers):
    barrier = pltpu.get_barrier_semaphore()
    for peer in peers:
        pl.semaphore_signal(barrier, inc=1, device_id=peer, device_id_type=MESH)
    pl.semaphore_wait(barrier, len(peers))


def _launch(name, body, out_shape, operands, sems, sequencer_id):
    if sequencer_id is None:
        return pl.pallas_call(body, out_shape=out_shape, in_specs=[HBM] * len(operands), out_specs=[HBM] * len(out_shape),
                              scratch_shapes=sems, name=name)(*operands)
    return pl.kernel(body, out_type=out_shape, mesh=plsc.ScalarSubcoreMesh(axis_name="seq", num_cores=1), name=name,
                     scratch_types=sems, compiler_params=pltpu.CompilerParams(collective_id=sequencer_id))(*operands)


AG_COPIES = 10


def _all_gather(name, shards, kinds, sequencer_id=None):
    n = len(shards)
    shapes = [s.shape for s in shards]

    def body(*refs):
        ins, outs = refs[:n], refs[n:2 * n]
        send_sems, recv_sems, local_sems = refs[2 * n:]
        x, y, c = _coords()
        me, sib, xn, yn, dg = (x, y, c), (x, y, 1 - c), (1 - x, y, c), (x, 1 - y, c), (1 - x, 1 - y, c)
        if sequencer_id is not None:
            _handshake([sib, xn, yn])

        def part(i, dev, half=None):
            return _slab(outs[i], kinds[i], shapes[i], 4 * dev[0] + 2 * dev[1] + dev[2], half)

        def copy(i, k, block, half, to, own=False):
            r = shapes[i][0]
            src = part(i, block, half) if not own else (ins[i] if half is None else ins[i].at[pl.ds(half * (r // 2), r // 2), :])
            return pltpu.make_async_remote_copy(
                src_ref=src, dst_ref=part(i, block, half), send_sem=send_sems.at[AG_COPIES * i + k],
                recv_sem=recv_sems.at[AG_COPIES * i + k], device_id=to, device_id_type=MESH)

        def other_core(dev):
            return (dev[0], dev[1], 1 - c)

        started = []

        def start(cp):
            cp.start()
            started.append(cp)

        for i in range(n):
            start(copy(i, 1, me, 0, xn, own=True))
            start(copy(i, 4, me, 1, yn, own=True))
            start(copy(i, 2, me, 1, xn, own=True))
            start(copy(i, 3, me, 0, yn, own=True))
            start(copy(i, 0, me, None, sib, own=True))
        mine = [pltpu.make_async_copy(ins[i], part(i, me), local_sems.at[i]) for i in range(n)]
        for cp in mine:
            cp.start()
        for i in range(n):
            copy(i, 1, xn, 0, me).wait_recv()
            start(copy(i, 5, xn, 0, yn))
            copy(i, 4, yn, 1, me).wait_recv()
            start(copy(i, 6, yn, 1, xn))
        for i in range(n):
            copy(i, 2, xn, 1, me).wait_recv()
            start(copy(i, 7, xn, None, sib))
            copy(i, 3, yn, 0, me).wait_recv()
            start(copy(i, 8, yn, None, sib))
        for i in range(n):
            copy(i, 5, dg, 0, me).wait_recv()
            copy(i, 6, dg, 1, me).wait_recv()
            start(copy(i, 9, dg, None, sib))
        for i in range(n):
            copy(i, 0, sib, None, me).wait_recv()
            for k, dev in ((7, xn), (8, yn), (9, dg)):
                copy(i, k, other_core(dev), None, me).wait_recv()
        for cp in started:
            cp.wait_send()
        for cp in mine:
            cp.wait()

    out_shape = [jax.ShapeDtypeStruct(_full_shape(s.shape, k), s.dtype) for s, k in zip(shards, kinds)]
    sems = [pltpu.SemaphoreType.DMA((AG_COPIES * n,)), pltpu.SemaphoreType.DMA((AG_COPIES * n,)), pltpu.SemaphoreType.DMA((n,))]
    return _launch(name, body, out_shape, shards, sems, sequencer_id)


def _sibling_copies(kinds, shard_shapes):
    def make(ins, outs, send_sems, recv_sems):
        x, y, c = _coords()
        return [pltpu.make_async_remote_copy(
            src_ref=_slab(ins[i], kinds[i], shard_shapes[i], 2 * q + (1 - c)), dst_ref=outs[i].at[q],
            send_sem=send_sems.at[N_CHIP * i + q], recv_sem=recv_sems.at[N_CHIP * i + q],
            device_id=(x, y, 1 - c), device_id_type=MESH) for i in range(len(ins)) for q in range(N_CHIP)]
    return make


def _sibling_side(partials, kinds, shard_shapes):
    return (partials, [jax.ShapeDtypeStruct((N_CHIP, *s), BF16) for s in shard_shapes], N_CHIP * len(partials),
            _sibling_copies(kinds, shard_shapes))


def _exchange_siblings(name, partials, kinds, shard_shapes, sequencer_id=None):
    n = len(partials)
    make = _sibling_copies(kinds, shard_shapes)

    def body(*refs):
        if sequencer_id is not None:
            x, y, c = _coords()
            _handshake([(x, y, 1 - c)])
        cps = make(refs[:n], refs[n:2 * n], refs[2 * n], refs[2 * n + 1])
        for cp in cps:
            cp.start()
        for cp in cps:
            cp.wait()

    return _launch(name, body, [jax.ShapeDtypeStruct((N_CHIP, *s), BF16) for s in shard_shapes], partials,
                   [pltpu.SemaphoreType.DMA((N_CHIP * n,)), pltpu.SemaphoreType.DMA((N_CHIP * n,))], sequencer_id)


def _exchange_chips(name, chip_sums, sequencer_id=None):
    n = len(chip_sums)

    def body(*refs):
        ins, outs = refs[:n], refs[n:2 * n]
        send_sems, recv_sems = refs[2 * n:]
        x, y, c = _coords()
        if sequencer_id is not None:
            _handshake([(cx, cy, c) for cx, cy in _other_chips(x, y)])
        cps = []
        for i in range(n):
            for k, (cx, cy) in enumerate(_other_chips(x, y)):
                cps.append(pltpu.make_async_remote_copy(
                    src_ref=ins[i].at[2 * cx + cy], dst_ref=outs[i].at[k], send_sem=send_sems.at[3 * i + k],
                    recv_sem=recv_sems.at[3 * i + k], device_id=(cx, cy, c), device_id_type=MESH))
        for cp in cps:
            cp.start()
        for cp in cps:
            cp.wait()

    return _launch(name, body, [jax.ShapeDtypeStruct((3, *t.shape[1:]), BF16) for t in chip_sums], chip_sums,
                   [pltpu.SemaphoreType.DMA((3 * n,)), pltpu.SemaphoreType.DMA((3 * n,))], sequencer_id)


def _all_reduce_small(name, packed):
    rows = packed.shape[0] // N_DEV

    def body(p_ref, out_ref, rb, tot, send_sems, recv_sems):
        x, y, c = _coords()
        me = 4 * x + 2 * y + c

        def peer(k):
            return (x ^ (k >> 2), y ^ ((k >> 1) & 1), c ^ (k & 1))

        def rows_of(idx):
            return pl.ds(pl.multiple_of(idx * rows, SUBLANES), rows)

        def piece(ref, idx):
            return ref.at[rows_of(idx), :]

        scatter = [pltpu.make_async_remote_copy(src_ref=piece(p_ref, me ^ k), dst_ref=rb.at[k], send_sem=send_sems.at[k],
                                                recv_sem=recv_sems.at[k], device_id=peer(k), device_id_type=MESH) for k in range(1, N_DEV)]
        for cp in scatter:
            cp.start()
        acc = p_ref[rows_of(me), :]
        for cp in scatter:
            cp.wait_recv()
        for k in range(1, N_DEV):
            acc = acc + rb[k]
        tot[...] = acc
        out_ref[rows_of(me), :] = acc
        gather = [pltpu.make_async_remote_copy(src_ref=tot, dst_ref=piece(out_ref, me), send_sem=send_sems.at[N_DEV + k],
                                               recv_sem=recv_sems.at[N_DEV + k], device_id=peer(k), device_id_type=MESH)
                  for k in range(1, N_DEV)]
        for cp in gather:
            cp.start()
        for k in range(1, N_DEV):
            pltpu.make_async_remote_copy(src_ref=tot, dst_ref=piece(out_ref, me ^ k), send_sem=send_sems.at[N_DEV + k],
                                         recv_sem=recv_sems.at[N_DEV + k], device_id=peer(k), device_id_type=MESH).wait_recv()
        for cp in scatter + gather:
            cp.wait_send()

    vm = pl.BlockSpec(memory_space=pltpu.VMEM)
    return pl.pallas_call(
        body, out_shape=jax.ShapeDtypeStruct(packed.shape, F32), in_specs=[vm], out_specs=vm,
        scratch_shapes=[pltpu.VMEM((N_DEV, rows, LANES), F32), pltpu.VMEM((rows, LANES), F32),
                        pltpu.SemaphoreType.DMA((2 * N_DEV,)), pltpu.SemaphoreType.DMA((2 * N_DEV,))],
        compiler_params=pltpu.CompilerParams(vmem_limit_bytes=VMEM_LIMIT), name=name)(packed)


def _adamw_math(g, w, m, v):
    m = ADAM_B1 * m + (1.0 - ADAM_B1) * g
    v = ADAM_B2 * v + (1.0 - ADAM_B2) * (g * g)
    delta = -ADAM_LR * ((m / ADAM_C1) / (jnp.sqrt(v / ADAM_C2) + ADAM_EPS) + ADAM_WD * w)
    return delta, m, v


def _slab_spec(kind, shard_shape, tr, slab_of):
    r, c = shard_shape
    if kind == "col":
        return pl.BlockSpec((tr, c), lambda q, i, sc: (i, slab_of(q, sc)))
    return pl.BlockSpec((tr, c), lambda q, i, sc: (slab_of(q, sc) * (r // tr) + i, 0))


def _chip_sum(name, partial, recv, kind, shard_shape, core):
    r, c = shard_shape
    tr = _blk(r, 1024)

    def body(core_ref, p_ref, r_ref, o_ref):
        o_ref[...] = (p_ref[...].astype(F32) + r_ref[...].astype(F32)).astype(BF16)

    spec4 = pl.BlockSpec((None, tr, c), lambda q, i, sc: (q, i, 0))
    grid_spec = pltpu.PrefetchScalarGridSpec(
        num_scalar_prefetch=1, grid=(N_CHIP, r // tr),
        in_specs=[_slab_spec(kind, shard_shape, tr, lambda q, sc: 2 * q + sc[0]), spec4], out_specs=spec4)
    return pl.pallas_call(body, out_shape=jax.ShapeDtypeStruct((N_CHIP, r, c), BF16), grid_spec=grid_spec,
                          compiler_params=_params(("parallel", "parallel")), name=name)(core, partial, recv)


def _adamw_shard(name, parts, w, m, v, chip):
    r, c = w.shape
    n_parts = len(parts)
    tr = _blk(r // n_parts, 256)
    per = r // n_parts // tr

    def body(chip_ref, *refs):
        src, (w_ref, m_ref, v_ref), (g_out, d_out, m_out, v_out) = refs[:2 * n_parts], refs[2 * n_parts:2 * n_parts + 3], refs[2 * n_parts + 3:]
        for p in range(n_parts):
            @pl.when(pl.program_id(0) // per == p)
            def _():
                g = src[2 * p][...].astype(F32)
                for k in range(3):
                    g = g + src[2 * p + 1][k].astype(F32)
                g_out[...] = g
                d_out[...], m_out[...], v_out[...] = _adamw_math(g, w_ref[...], m_ref[...], v_ref[...])

    def part_specs(p):
        at = lambda i: jnp.clip(i - p * per, 0, per - 1)
        return [pl.BlockSpec((None, tr, c), lambda i, sc: (sc[0], at(i), 0)), pl.BlockSpec((3, tr, c), lambda i, sc: (0, at(i), 0))]

    blk = pl.BlockSpec((tr, c), lambda i, sc: (i, 0))
    grid_spec = pltpu.PrefetchScalarGridSpec(
        num_scalar_prefetch=1, grid=(r // tr,), in_specs=[s for p in range(n_parts) for s in part_specs(p)] + [blk, blk, blk], out_specs=[blk] * 4)
    return pl.pallas_call(body, out_shape=[jax.ShapeDtypeStruct((r, c), F32)] * 4, grid_spec=grid_spec,
                          compiler_params=_params(("parallel",)), name=name)(chip, *[a for p in parts for a in p], w, m, v)


def _adamw_small(name, g, w, m, v):
    def body(g_ref, w_ref, m_ref, v_ref, d_out, m_out, v_out):
        d_out[...], m_out[...], v_out[...] = _adamw_math(g_ref[...], w_ref[...], m_ref[...], v_ref[...])

    vm = pl.BlockSpec(memory_space=pltpu.VMEM)
    return pl.pallas_call(body, out_shape=[jax.ShapeDtypeStruct(g.shape, F32)] * 3, in_specs=[vm] * 4, out_specs=[vm] * 3,
                          compiler_params=pltpu.CompilerParams(vmem_limit_bytes=VMEM_LIMIT), name=name)(g, w, m, v)


def _pack_rows(arrays, total_rows):
    flat = [a.reshape(-1, LANES) for a in arrays]
    used = sum(f.shape[0] for f in flat)
    return jnp.concatenate(flat + [jnp.zeros((total_rows - used, LANES), F32)], axis=0)


def _unpack_rows(packed, like):
    out, at = [], 0
    for a in like:
        n = a.size // LANES
        out.append(packed[at:at + n].reshape(a.shape))
        at += n
    return out


def kernel(x, norm_mix_g, w_in, conv_w, conv_b, lru_wa, lru_ba, lru_wx, lru_bx, lru_lambda, w_proj_attn, w_proj_lru, w_out, norm_mlp_g, w_up, w_down, norm_final_g, loss_target, m_norm_mix_g, m_w_in, m_conv_w, m_conv_b, m_lru_wa, m_lru_ba, m_lru_wx, m_lru_bx, m_lru_lambda, m_w_proj_attn, m_w_proj_lru, m_w_out, m_norm_mlp_g, m_w_up, m_w_down, m_norm_final_g, v_norm_mix_g, v_w_in, v_conv_w, v_conv_b, v_lru_wa, v_lru_ba, v_lru_wx, v_lru_bx, v_lru_lambda, v_w_proj_attn, v_w_proj_lru, v_w_out, v_norm_mlp_g, v_w_up, v_w_down, v_norm_final_g):
    xs, tgt = x[0], loss_target[0]
    s, d = xs.shape
    nh = d // HEAD
    ix, iy, ic = _coords()
    core = jnp.reshape(ic, (1,)).astype(jnp.int32)
    chip = jnp.reshape(2 * ix + iy, (1,)).astype(jnp.int32)
    dev = 4 * ix + 2 * iy + ic

    big = [w_in[0], w_proj_attn[0], w_proj_lru[0], w_out[0], w_up[0], w_down[0]]
    big_m = [m_w_in[0], m_w_proj_attn[0], m_w_proj_lru[0], m_w_out[0], m_w_up[0], m_w_down[0]]
    big_v = [v_w_in[0], v_w_proj_attn[0], v_w_proj_lru[0], v_w_out[0], v_w_up[0], v_w_down[0]]
    kinds = ["col", "row", "row", "row", "col", "row"]
    pad_taps = lambda t: jnp.pad(t, ((0, SUBLANES - CONV_TAPS), (0, 0)))
    shards = [w.astype(BF16) for w in big]
    pad_taps2 = lambda t: jnp.pad(t, ((0, 2 * SUBLANES - CONV_TAPS), (0, 0)))
    win, cw_slots = _all_gather("all_gather_w_in", [shards[0], pad_taps2(conv_w[0])], ["col", "slot"])
    later = lax.optimization_barrier((shards[1:], win))[0]
    wpa, wpl, wout = _all_gather("all_gather_mix", later[:3], kinds[1:4], sequencer_id=1)
    wup, wdown = _all_gather("all_gather_mlp", later[3:], kinds[4:], sequencer_id=5)
    cw8 = jnp.transpose(cw_slots[:, :SUBLANES], (1, 0, 2)).reshape(SUBLANES, d)
    row_id = lax.broadcasted_iota(jnp.int32, (SUBLANES, d), 0)
    vec8 = sum(jnp.where(row_id == k, t, 0.0) for k, t in ((VEC_CB, conv_b), (VEC_BA, lru_ba), (VEC_BX, lru_bx), (VEC_LAM, lru_lambda)))
    wa16, wx16 = lru_wa[0].astype(BF16), lru_wx[0].astype(BF16)
    slopes = 2.0 ** (-8.0 * jnp.arange(1, nh + 1, dtype=F32) / nh)

    def seg_specs(*segs):
        return lambda bm, bn: [pl.BlockSpec((bm, bn), (lambda i, j, kk, sg=sg: (i, sg * (d // bn) + j))) for sg in segs]

    def plain_specs(k):
        return lambda bm, bn: [pl.BlockSpec((bm, bn), lambda i, j, kk: (i, j)) for _ in range(k)]

    xn = _rms_fwd("norm_mix", xs, norm_mix_g)
    proj = _mm_fwd("proj_in", xn, win, 0, 7 * d, [F32], bm=2048)[0]
    att, lse = _attn_fwd(proj, d, slopes)
    a3, u3, xc = _gates_fwd(proj, d, cw8, vec8, wa16, wx16)
    hp3, h2d, ylru = _scan_fwd(a3, u3, proj)
    pa = _mm_fwd("proj_attn", att, wpa, 0, d, [BF16], bm=2048)[0]

    def merge(acc, pa_b, ga, gl):
        return acc, _sigmoid(ga) * pa_b.astype(F32) + _sigmoid(gl) * acc

    plr, merged = _mm_fwd("proj_lru_merge", ylru, wpl, 0, d, [BF16, BF16], merge, (pa, proj, proj),
                          lambda bm, bn: plain_specs(1)(bm, bn) + seg_specs(SEG_GA, SEG_GL)(bm, bn), bn=512)
    h1 = _mm_fwd("mix_out", merged, wout, 0, d, [F32], lambda acc, r: (acc + r,), (xs,), plain_specs(1))[0]
    hn = _rms_fwd("norm_mlp", h1, norm_mlp_g)

    def relu2(acc):
        return acc, jnp.square(jnp.maximum(acc, 0.0))

    up, hid = _mm_fwd("mlp_up", hn, wup, 0, wup.shape[1], [BF16, BF16], relu2, bm=2048)
    h2 = _mm_fwd("mlp_down", hid, wdown, 0, d, [F32], lambda acc, r: (acc + r,), (h1,), plain_specs(1))[0]
    dh2, dh2b, dg3, loss_lanes = _final_loss(h2, tgt, norm_final_g.reshape(1, d))
    loss = lax.psum(0.5 / d * jnp.sum(loss_lanes), ("x", "y", "c"))
    dh2b = lax.optimization_barrier((dh2b, loss))[0]

    def reduce_group(tag, kk, shp, partials, from_sibling, sequencer_id):
        sums = [_chip_sum(f"chip_sum_{tag}_{i}", p, f, k, sh, core) for i, (p, f, k, sh) in enumerate(zip(partials, from_sibling, kk, shp))]
        return list(zip(sums, _exchange_chips(f"rs_chips_{tag}", sums, sequencer_id)))

    dup = _mm_nt("mlp_down_dx", dh2b, wdown, [BF16], lambda acc, u: (acc * (2.0 * jnp.maximum(u.astype(F32), 0.0)),), (up,), plain_specs(1))[0]
    g_wdown = _mm_tn("mlp_down_dw", hid, dh2b)
    g_wup = _mm_tn("mlp_up_dw", hn, dup)
    shp_mlp = [w.shape for w in big[4:]]
    (dhn,), sib_mlp = _mm_nt("mlp_up_dx", dup, wup, [F32], side=_sibling_side([g_wup, g_wdown], kinds[4:], shp_mlp))
    red_up, red_down = reduce_group("mlp", kinds[4:], shp_mlp, [g_wup, g_wdown], sib_mlp, 2)
    dhn = lax.optimization_barrier((dhn, red_up[0], red_down[0]))[0]
    dh1, dh1b, dg2 = _rms_bwd("norm_mlp_bwd", h1, norm_mlp_g, dhn, dh2)

    def merge_bwd(acc, pa_b, pl_b, ga, gl):
        sa, sl = _sigmoid(ga), _sigmoid(gl)
        return acc * sa, acc * sl, acc * pa_b.astype(F32) * sa * (1.0 - sa), acc * pl_b.astype(F32) * sl * (1.0 - sl)

    dpa, dpl, dga, dgl = _mm_nt("mix_out_dx", dh1b, wout, [BF16] * 4, merge_bwd, (pa, plr, proj, proj),
                                lambda bm, bn: plain_specs(2)(bm, bn) + seg_specs(SEG_GA, SEG_GL)(bm, bn), bn=512)
    g_wout = _mm_tn("mix_out_dw", merged, dh1b)
    datt = _mm_nt("proj_attn_dx", dpa, wpa, [F32], bm=2048)[0]
    g_wpa = _mm_tn("proj_attn_dw", att, dpa)

    def lru_out_bwd(acc, h_b, gate):
        return acc * _gelu(gate), acc * h_b.astype(F32) * _gelu_grad(gate)

    g_wpl = _mm_tn("proj_lru_dw", ylru, dpl)
    shp_mix = [w.shape for w in big[1:4]]
    (dh, dxg), sib_mix = _mm_nt("proj_lru_dx", dpl, wpl, [F32, BF16], lru_out_bwd, (h2d, proj),
                                lambda bm, bn: plain_specs(1)(bm, bn) + seg_specs(SEG_GATE)(bm, bn), bn=512,
                                side=_sibling_side([g_wpa, g_wpl, g_wout], kinds[1:4], shp_mix))
    red_pa, red_pl, red_out = reduce_group("mix", kinds[1:4], shp_mix, [g_wpa, g_wpl, g_wout], sib_mix, 3)
    g3, da3 = _scan_bwd(a3, hp3, dh)
    dxc, dwa, dwx, dvec = _gates_bwd(g3, da3, xc, wa16, wx16, vec8)
    dxr, dconv = _conv_bwd(dxc, proj, cw8)
    dproj = _attn_bwd(proj, d, datt, att, lse, slopes, (dxr, dxg, dga, dgl))

    def small_step(tag, grads, ws, ms, vs, like, after):
        n_rows = sum(g.size for g in grads) // LANES
        per_dev = -(-n_rows // (N_DEV * SUBLANES)) * SUBLANES
        packed = lax.optimization_barrier((_pack_rows(grads, N_DEV * per_dev), after))[0]
        total = _all_reduce_small(f"all_reduce_{tag}", packed)
        w_rows = -(-(sum(w.size for w in ws) // LANES) // SUBLANES) * SUBLANES
        upd = _adamw_small(f"adamw_{tag}", total[:w_rows], _pack_rows(ws, w_rows), _pack_rows(ms, w_rows), _pack_rows(vs, w_rows))
        return _unpack_rows(total, like), [_unpack_rows(t, ws) for t in upd]

    early_w = [conv_b, lru_wa, lru_ba, lru_wx, lru_bx, lru_lambda, norm_mlp_g, norm_final_g]
    early_m = [m_conv_b, m_lru_wa, m_lru_ba, m_lru_wx, m_lru_bx, m_lru_lambda, m_norm_mlp_g, m_norm_final_g]
    early_v = [v_conv_b, v_lru_wa, v_lru_ba, v_lru_wx, v_lru_bx, v_lru_lambda, v_norm_mlp_g, v_norm_final_g]
    early_g = [dconv[CONV_TAPS:CONV_TAPS + 1], dwa, dvec[VEC_BA:VEC_BA + 1], dwx, dvec[VEC_BX:VEC_BX + 1],
               dvec[VEC_LAM:VEC_LAM + 1], dg2, dg3, dconv[0:CONV_TAPS]]
    dproj = lax.optimization_barrier((dproj, red_up[1], red_down[1]))[0]
    early_sum, early_upd = small_step("small", early_g, early_w, early_m, early_v,
                                      early_w + [jax.ShapeDtypeStruct((1, CONV_TAPS, d), F32)], dproj)
    g_cw_full = early_sum[-1]
    cshard = conv_w.shape[2]
    g_cw = lax.dynamic_slice(g_cw_full, (0, 0, dev * cshard), (1, CONV_TAPS, cshard))
    cw_delta, cw_m, cw_v = (t[:CONV_TAPS][None] for t in _adamw_small(
        "adamw_conv_w", pad_taps(g_cw[0]), pad_taps(conv_w[0]), pad_taps(m_conv_w[0]), pad_taps(v_conv_w[0])))
    dproj = lax.optimization_barrier((dproj, early_sum, red_pa[1], red_pl[1], red_out[1]))[0]
    half = (big[0].shape[0] // 2, big[0].shape[1])
    g_in0 = _mm_tn("proj_in_dw_0", xn, dproj, part=(0, 2))
    g_in1, sib_in0 = _mm_tn("proj_in_dw_1", xn, dproj, part=(1, 2), side=_sibling_side([g_in0], ["col"], [half]))
    red_in = reduce_group("in_0", ["col"], [half], [g_in0], sib_in0, 4)
    dproj = lax.optimization_barrier((dproj, red_in[0][0]))[0]
    (dxn0,), sib_in1 = _mm_nt("proj_in_dx_0", dproj, win, [F32], part=(0, 2), side=_sibling_side([g_in1], ["col"], [half]))
    red_in += reduce_group("in_1", ["col"], [half], [g_in1], sib_in1, 6)
    dproj = lax.optimization_barrier((dproj, red_in[1][0]))[0]
    dxn1 = _mm_nt("proj_in_dx_1", dproj, win, [F32], part=(1, 2))[0]
    dxn = jnp.concatenate([dxn0, dxn1], axis=0)
    dxn = lax.optimization_barrier((dxn, red_in[0][1]))[0]
    grad_x, _, dg1 = _rms_bwd("norm_mix_bwd", xs, norm_mix_g, dxn, dh1)
    red_up, red_down = lax.optimization_barrier(((red_up, red_down), dg1))[0]
    big_out = {i: _adamw_shard(f"adamw_{i}", [red], big[i], big_m[i], big_v[i], chip) for i, red in ((4, red_up), (5, red_down))}
    big_out.update({i: _adamw_shard(f"adamw_{i}", [red], big[i], big_m[i], big_v[i], chip) for i, red in ((1, red_pa), (2, red_pl), (3, red_out))})
    late_sum, late_upd = small_step("norm_mix", [dg1], [norm_mix_g], [m_norm_mix_g], [v_norm_mix_g], [norm_mix_g], (big_out[4], big_out[5]))
    big_out[0] = _adamw_shard("adamw_0", red_in, big[0], big_m[0], big_v[0], chip)
    s_grad = late_sum + early_sum[:-1]
    s_delta, s_m, s_v = (late_upd[j] + early_upd[j] for j in range(3))


    names = ["norm_mix_g", "w_in", "conv_w", "conv_b", "lru_wa", "lru_ba", "lru_wx", "lru_bx", "lru_lambda", "w_proj_attn", "w_proj_lru",
             "w_out", "norm_mlp_g", "w_up", "w_down", "norm_final_g"]
    small_names = ["norm_mix_g", "conv_b", "lru_wa", "lru_ba", "lru_wx", "lru_bx", "lru_lambda", "norm_mlp_g", "norm_final_g"]
    big_names = ["w_in", "w_proj_attn", "w_proj_lru", "w_out", "w_up", "w_down"]
    res = {"conv_w": (g_cw, cw_delta, cw_m, cw_v)}
    for i, nm in enumerate(small_names):
        res[nm] = (s_grad[i], s_delta[i], s_m[i], s_v[i])
    for i, nm in enumerate(big_names):
        res[nm] = tuple(t[None] for t in big_out[i])
    return (loss, grad_x[None], *[res[nm][0] for nm in names], *[res[nm][1] for nm in names],
            *[res[nm][2] for nm in names], *[res[nm][3] for nm in names])
```

```python
import jax
import jax.numpy as jnp
from jax import lax
from jax.experimental import pallas as pl
from jax.experimental.pallas import tpu as pltpu
from jax.experimental.pallas import tpu_sc as plsc

F32, BF16 = jnp.float32, jnp.bfloat16
MESH = pl.DeviceIdType.MESH
HBM = pl.BlockSpec(memory_space=pltpu.HBM)
N_DEV = 8
N_CHIP = 4
HEAD = 128
SPAN = 128
DILATIONS = (1, 4, 16)
CONV_TAPS = 4
LRU_C = 8.0
NORM_EPS = 1e-6
LANES = 128
SUBLANES = 8
VMEM_LIMIT = 56 * 1024 * 1024
ADAM_LR, ADAM_B1, ADAM_B2, ADAM_EPS, ADAM_WD, ADAM_STEP = 0.001, 0.9, 0.999, 1e-08, 0.01, 10
ADAM_C1 = 1.0 - ADAM_B1 ** ADAM_STEP
ADAM_C2 = 1.0 - ADAM_B2 ** ADAM_STEP
NEG = -1e30


def _params(sem=None):
    return pltpu.CompilerParams(dimension_semantics=sem, vmem_limit_bytes=VMEM_LIMIT)


def _sigmoid(v):
    return 1.0 / (1.0 + jnp.exp(-v))


def _gelu(v):
    k = 0.7978845608028654
    return 0.5 * v * (1.0 + jnp.tanh(k * (v + 0.044715 * v * v * v)))


def _gelu_grad(v):
    k = 0.7978845608028654
    t = jnp.tanh(k * (v + 0.044715 * v * v * v))
    return 0.5 * (1.0 + t) + 0.5 * v * (1.0 - t * t) * k * (1.0 + 3.0 * 0.044715 * v * v)


NN = (((1,), (0,)), ((), ()))
NT = (((1,), (1,)), ((), ()))
TN = (((0,), (0,)), ((), ()))


def _mm(name, a, a_spec, b, b_spec, dn, grid, out_shapes, out_specs, acc_block, epilogue=None, extras=(), extra_specs=(), side=None):
    nk, ne, no = grid[2], len(extras), len(out_shapes)
    side_ops, side_shapes, side_copies, make_copies = side if side is not None else ((), (), 0, None)
    ns_in, ns_out = len(side_ops), len(side_shapes)

    def body(*refs):
        a_ref, b_ref = refs[0], refs[1]
        ex, side_in = refs[2:2 + ne], refs[2 + ne:2 + ne + ns_in]
        outs = refs[2 + ne + ns_in:2 + ne + ns_in + no]
        side_out = refs[2 + ne + ns_in + no:2 + ne + ns_in + no + ns_out]
        scratch = refs[2 + ne + ns_in + no + ns_out:]
        at = [pl.program_id(ax) for ax in range(3)]
        if side is not None:
            @pl.when((at[0] == 0) & (at[1] == 0) & (at[2] == 0))
            def _():
                for cp in make_copies(side_in, side_out, scratch[-2], scratch[-1]):
                    cp.start()

        part = lax.dot_general(a_ref[...], b_ref[...], dn, preferred_element_type=F32)

        def finish(acc):
            vals = epilogue(acc, *[e[...] for e in ex]) if epilogue is not None else (acc,)
            for o, v in zip(outs, vals):
                o[...] = v.astype(o.dtype)

        if nk == 1:
            finish(part)
        else:
            acc_ref, k = scratch[0], at[2]

            @pl.when(k == 0)
            def _():
                acc_ref[...] = part

            @pl.when(k > 0)
            def _():
                acc_ref[...] += part

            @pl.when(k == nk - 1)
            def _():
                finish(acc_ref[...])

        if side is not None:
            @pl.when((at[0] == grid[0] - 1) & (at[1] == grid[1] - 1) & (at[2] == grid[2] - 1))
            def _():
                for cp in make_copies(side_in, side_out, scratch[-2], scratch[-1]):
                    cp.wait()

    scratch_shapes = [pltpu.VMEM(acc_block, F32)] if nk > 1 else []
    if side is not None:
        scratch_shapes += [pltpu.SemaphoreType.DMA((side_copies,)), pltpu.SemaphoreType.DMA((side_copies,))]
    res = pl.pallas_call(
        body, out_shape=[*out_shapes, *side_shapes], grid=grid, in_specs=[a_spec, b_spec, *extra_specs, *[HBM] * ns_in],
        out_specs=[*out_specs, *[HBM] * ns_out], scratch_shapes=scratch_shapes,
        compiler_params=_params(("arbitrary",) * 3 if side is not None else ("parallel", "parallel", "arbitrary")),
        name=name)(a, b, *extras, *side_ops)
    return res if side is None else (res[:no], res[no:])


def _blk(n, pref):
    return pref if n % pref == 0 else n


def _kblk(k):
    return k if k <= 2048 else next(b for b in (2048, 1024, 512) if k % b == 0)


def _mm_fwd(name, a, w, col0, ncols, out_dtypes, epilogue=None, extras=(), extra_specs_fn=None, seg_out=None, bm=1024, bn=1024):
    m, k = a.shape
    bm, bn = _blk(m, bm), _blk(ncols, bn)
    bk = _kblk(k)
    nk = k // bk
    cb0 = col0 // bn
    grid = (m // bm, ncols // bn, nk)
    a_spec = pl.BlockSpec((bm, bk), lambda i, j, kk: (i, kk))
    b_spec = pl.BlockSpec((bk, bn), lambda i, j, kk: (kk, cb0 + j))
    if seg_out is None:
        shapes = [jax.ShapeDtypeStruct((m, ncols), dt) for dt in out_dtypes]
        specs = [pl.BlockSpec((bm, bn), lambda i, j, kk: (i, j)) for _ in out_dtypes]
    else:
        per = seg_out // bn
        shapes = [jax.ShapeDtypeStruct((ncols // seg_out, m, seg_out), dt) for dt in out_dtypes]
        specs = [pl.BlockSpec((None, bm, bn), lambda i, j, kk: (j // per, i, j % per)) for _ in out_dtypes]
    ex_specs = extra_specs_fn(bm, bn) if extra_specs_fn else ()
    return _mm(name, a, a_spec, w, b_spec, NN, grid, shapes, specs, (bm, bn), epilogue, extras, ex_specs)


def _mm_nt(name, a, w, out_dtypes, epilogue=None, extras=(), extra_specs_fn=None, part=(0, 1), side=None, bm=1024, bn=1024):
    n = w.shape[0]
    if a.ndim == 3:
        seg_cols, m, k = a.shape[2], a.shape[1], a.shape[0] * a.shape[2]
    else:
        m, k = a.shape
    m = m // part[1]
    bm, bn = _blk(m, bm), _blk(n, bn)
    bk = _kblk(k)
    grid = (m // bm, n // bn, k // bk)
    i0 = part[0] * (m // bm)
    if a.ndim == 3:
        per = seg_cols // bk
        a_spec = pl.BlockSpec((None, bm, bk), lambda i, j, kk: (kk // per, i0 + i, kk % per))
    else:
        a_spec = pl.BlockSpec((bm, bk), lambda i, j, kk: (i0 + i, kk))
    b_spec = pl.BlockSpec((bn, bk), lambda i, j, kk: (j, kk))
    shapes = [jax.ShapeDtypeStruct((m, n), dt) for dt in out_dtypes]
    specs = [pl.BlockSpec((bm, bn), lambda i, j, kk: (i, j)) for _ in out_dtypes]
    ex_specs = extra_specs_fn(bm, bn) if extra_specs_fn else ()
    return _mm(name, a, a_spec, w, b_spec, NT, grid, shapes, specs, (bm, bn), epilogue, extras, ex_specs, side)


def _mm_tn(name, a, b, part=(0, 1), side=None, bm=1024, bn=2048):
    t, m = a.shape
    n = b.shape[1] if b.ndim == 2 else b.shape[0] * b.shape[2]
    m = m // part[1]
    bm, bn = _blk(m, bm), _blk(n, bn)
    grid = (m // bm, n // bn, 1)
    i0 = part[0] * (m // bm)
    a_spec = pl.BlockSpec((t, bm), lambda i, j, kk: (0, i0 + i))
    if b.ndim == 3:
        per = b.shape[2] // bn
        b_spec = pl.BlockSpec((None, t, bn), lambda i, j, kk: (j // per, 0, j % per))
    else:
        b_spec = pl.BlockSpec((t, bn), lambda i, j, kk: (0, j))
    res = _mm(name, a, a_spec, b, b_spec, TN, grid, [jax.ShapeDtypeStruct((m, n), BF16)],
              [pl.BlockSpec((bm, bn), lambda i, j, kk: (i, j))], (bm, bn), side=side)
    return res[0] if side is None else (res[0][0], res[1])


ROWS = 256


def _row_spec(d):
    return pl.BlockSpec((ROWS, d), lambda i: (i, 0))


def _vec_spec(d, rows=1):
    return pl.BlockSpec((rows, d), lambda i: (0, 0))


def _rms_fwd(name, x, g):
    s, d = x.shape

    def body(x_ref, g_ref, o_ref):
        xv = x_ref[...]
        r = lax.rsqrt(jnp.mean(xv * xv, axis=-1, keepdims=True) + NORM_EPS)
        o_ref[...] = (xv * r * g_ref[...]).astype(BF16)

    return pl.pallas_call(body, out_shape=jax.ShapeDtypeStruct((s, d), BF16), grid=(s // ROWS,),
                          in_specs=[_row_spec(d), _vec_spec(d)], out_specs=_row_spec(d),
                          compiler_params=_params(("parallel",)), name=name)(x, g)


def _rms_bwd_math(xv, g, dy):
    r = lax.rsqrt(jnp.mean(xv * xv, axis=-1, keepdims=True) + NORM_EPS)
    n = xv * r
    z = dy * g
    dx = r * (z - n * jnp.mean(z * n, axis=-1, keepdims=True))
    return dx, jnp.sum(dy * n, axis=0, keepdims=True)


def _rms_bwd(name, x, g, dy, resid):
    s, d = x.shape

    def body(x_ref, g_ref, dy_ref, r_ref, dx_ref, dxb_ref, dg_ref):
        dx, dg = _rms_bwd_math(x_ref[...], g_ref[...], dy_ref[...])
        dx = dx + r_ref[...]
        dx_ref[...] = dx
        dxb_ref[...] = dx.astype(BF16)

        @pl.when(pl.program_id(0) == 0)
        def _():
            dg_ref[...] = jnp.zeros_like(dg_ref)

        dg_ref[...] += dg

    return pl.pallas_call(
        body, out_shape=[jax.ShapeDtypeStruct((s, d), F32), jax.ShapeDtypeStruct((s, d), BF16), jax.ShapeDtypeStruct((1, d), F32)],
        grid=(s // ROWS,), in_specs=[_row_spec(d), _vec_spec(d), _row_spec(d), _row_spec(d)],
        out_specs=[_row_spec(d), _row_spec(d), _vec_spec(d)], compiler_params=_params(("arbitrary",)), name=name)(x, g, dy, resid)


def _final_loss(h2, tgt, g):
    s, d = h2.shape

    def body(x_ref, t_ref, g_ref, dx_ref, dxb_ref, dg_ref, ls_ref):
        xv, gv = x_ref[...], g_ref[...]
        r = lax.rsqrt(jnp.mean(xv * xv, axis=-1, keepdims=True) + NORM_EPS)
        diff = xv * r * gv - t_ref[...]
        dx, dg = _rms_bwd_math(xv, gv, diff * (1.0 / d))
        dx_ref[...] = dx
        dxb_ref[...] = dx.astype(BF16)

        @pl.when(pl.program_id(0) == 0)
        def _():
            dg_ref[...] = jnp.zeros_like(dg_ref)
            ls_ref[...] = jnp.zeros_like(ls_ref)

        dg_ref[...] += dg
        ls_ref[...] += jnp.sum(diff * diff, axis=0, keepdims=True)

    return pl.pallas_call(
        body, out_shape=[jax.ShapeDtypeStruct((s, d), F32), jax.ShapeDtypeStruct((s, d), BF16),
                         jax.ShapeDtypeStruct((1, d), F32), jax.ShapeDtypeStruct((1, d), F32)],
        grid=(s // ROWS,), in_specs=[_row_spec(d), _row_spec(d), _vec_spec(d)],
        out_specs=[_row_spec(d), _row_spec(d), _vec_spec(d), _vec_spec(d)],
        compiler_params=_params(("arbitrary",)), name="final_norm_loss")(h2, tgt, g)


ATTN_Q = 128


ATTN_BATCH = 8


def _attn_units(s):
    units = []
    for gi, d in enumerate(DILATIONS):
        for r in range(d):
            for q0 in range(0, s // d, ATTN_Q):
                k0 = max(q0 - SPAN, 0)
                units.append((gi, d, r, q0, k0, q0 + ATTN_Q - k0))
    return units


def _stream_rows(d, r, start, size):
    return pl.ds(r + start * d, size) if d == 1 else pl.ds(r + start * d, size, stride=d)


def _attn_scores(q_ref, k_ref, slope, d, r, q0, k0, nk):
    qrows, krows = _stream_rows(d, r, q0, ATTN_Q), _stream_rows(d, r, k0, nk)
    qb = q_ref[qrows, :].astype(BF16)
    kb = k_ref[krows, :].astype(BF16)
    sc = lax.dot_general(qb, kb, NT, preferred_element_type=F32) * (HEAD ** -0.5)
    qi = lax.broadcasted_iota(jnp.int32, (ATTN_Q, nk), 0)
    kj = lax.broadcasted_iota(jnp.int32, (ATTN_Q, nk), 1)
    dist = (q0 - k0) + qi - kj
    valid = (dist >= 0) & (dist <= SPAN)
    sc = sc - (slope * d) * dist.astype(F32)
    return jnp.where(valid, sc, NEG), valid, qb, kb, qrows, krows


def _attn_fwd(proj, dm, slopes):
    s = proj.shape[0]
    units = _attn_units(s)

    def body(sl_ref, q_ref, k_ref, v_ref, att_ref, lse_ref, *scr):
        o_scr, l_scr = scr[:3], scr[3:]
        slope = sl_ref[pl.program_id(0)]
        for first in range(0, len(units), ATTN_BATCH):
            batch = units[first:first + ATTN_BATCH]
            scored = [_attn_scores(q_ref, k_ref, slope, d, r, q0, k0, nk) for _, d, r, q0, k0, nk in batch]
            soft = []
            for sc, _, _, _, _, _ in scored:
                m = jnp.max(sc, axis=-1, keepdims=True)
                p = jnp.exp(sc - m)
                soft.append((m, p, jnp.sum(p, axis=-1, keepdims=True)))
            outs = [lax.dot_general(p.astype(BF16), v_ref[sco[5], :].astype(BF16), NN, preferred_element_type=F32)
                    for (m, p, l), sco in zip(soft, scored)]
            for (gi, *_), (m, p, l), sco, o in zip(batch, soft, scored, outs):
                o_scr[gi][sco[4], :] = o / l
                l_scr[gi][sco[4], :] = jnp.broadcast_to(m + jnp.log(l), (ATTN_Q, HEAD))
        l0, l1, l2 = l_scr[0][...], l_scr[1][...], l_scr[2][...]
        m = jnp.maximum(jnp.maximum(l0, l1), l2)
        w0, w1, w2 = jnp.exp(l0 - m), jnp.exp(l1 - m), jnp.exp(l2 - m)
        tot = w0 + w1 + w2
        att_ref[...] = ((w0 * o_scr[0][...] + w1 * o_scr[1][...] + w2 * o_scr[2][...]) / tot).astype(BF16)
        lse_ref[...] = m + jnp.log(tot)

    def seg(i):
        return pl.BlockSpec((s, HEAD), lambda h: (0, i * (dm // HEAD) + h))

    col = pl.BlockSpec((s, HEAD), lambda h: (0, h))
    return pl.pallas_call(
        body, out_shape=[jax.ShapeDtypeStruct((s, dm), BF16), jax.ShapeDtypeStruct((s, dm), F32)], grid=(dm // HEAD,),
        in_specs=[pl.BlockSpec(memory_space=pltpu.SMEM), seg(0), seg(1), seg(2)], out_specs=[col, col],
        scratch_shapes=[pltpu.VMEM((s, HEAD), F32)] * (2 * len(DILATIONS)),
        compiler_params=_params(("parallel",)), name="attn_fwd")(slopes, proj, proj, proj)


def _attn_bwd(proj, dm, datt, att, lse, slopes, others):
    s = proj.shape[0]
    units = _attn_units(s)

    def body(sl_ref, q_ref, k_ref, v_ref, do_ref, att_ref, lse_ref, o3, o4, o5, o6, out_ref, dq_scr, dk_scr, dv_scr, dl_scr):
        slope = sl_ref[pl.program_id(0)]
        delta = jnp.sum(do_ref[...] * att_ref[...].astype(F32), axis=-1, keepdims=True)
        dl_scr[...] = jnp.broadcast_to(delta, (s, HEAD))
        dq_scr[...] = jnp.zeros_like(dq_scr)
        dk_scr[...] = jnp.zeros_like(dk_scr)
        dv_scr[...] = jnp.zeros_like(dv_scr)
        for first in range(0, len(units), ATTN_BATCH):
            scored = [_attn_scores(q_ref, k_ref, slope, d, r, q0, k0, nk) for _, d, r, q0, k0, nk in units[first:first + ATTN_BATCH]]
            dobs = [do_ref[sco[4], :].astype(BF16) for sco in scored]
            dps = [lax.dot_general(dob, v_ref[sco[5], :].astype(BF16), NT, preferred_element_type=F32) for dob, sco in zip(dobs, scored)]
            ps = [jnp.where(sco[1], jnp.exp(sco[0] - lse_ref[sco[4], :][:, 0:1]), 0.0) for sco in scored]
            dss = [(p * (dp - dl_scr[sco[4], :][:, 0:1]) * (HEAD ** -0.5)).astype(BF16) for p, dp, sco in zip(ps, dps, scored)]
            dqs = [lax.dot_general(ds, sco[3], NN, preferred_element_type=F32) for ds, sco in zip(dss, scored)]
            dks = [lax.dot_general(ds, sco[2], TN, preferred_element_type=F32) for ds, sco in zip(dss, scored)]
            dvs = [lax.dot_general(p.astype(BF16), dob, TN, preferred_element_type=F32) for p, dob in zip(ps, dobs)]
            for sco, dq, dk, dv in zip(scored, dqs, dks, dvs):
                dq_scr[sco[4], :] += dq
                dk_scr[sco[5], :] += dk
                dv_scr[sco[5], :] += dv
        for j, scr in enumerate((dq_scr, dk_scr, dv_scr)):
            out_ref[j] = scr[...].astype(BF16)
        for j, other in enumerate((o3, o4, o5, o6)):
            out_ref[3 + j] = other[...]

    def seg(i):
        return pl.BlockSpec((s, HEAD), lambda h: (0, i * (dm // HEAD) + h))

    col = pl.BlockSpec((s, HEAD), lambda h: (0, h))
    return pl.pallas_call(
        body, out_shape=jax.ShapeDtypeStruct((7, s, dm), BF16), grid=(dm // HEAD,),
        in_specs=[pl.BlockSpec(memory_space=pltpu.SMEM), seg(0), seg(1), seg(2), col, col, col, col, col, col, col],
        out_specs=pl.BlockSpec((7, s, HEAD), lambda h: (0, 0, h)), scratch_shapes=[pltpu.VMEM((s, HEAD), F32)] * 4,
        compiler_params=_params(("parallel",)), name="attn_bwd")(slopes, proj, proj, proj, datt, att, lse, *others)


VEC_CB, VEC_BA, VEC_BX, VEC_LAM = 0, 1, 2, 3
SEG_Q, SEG_K, SEG_V, SEG_X, SEG_GATE, SEG_GA, SEG_GL = range(7)


def _softplus(z):
    return jnp.maximum(z, 0.0) + jnp.log1p(jnp.exp(-jnp.abs(z)))


def _gate_math(xc, wa_ref, wx_ref, vec):
    xcb = xc.astype(BF16)
    nh = xc.shape[1] // HEAD
    pre_a = jnp.concatenate([jnp.dot(xcb[:, h * HEAD:(h + 1) * HEAD], wa_ref[h], preferred_element_type=F32) for h in range(nh)], axis=1)
    pre_x = jnp.concatenate([jnp.dot(xcb[:, h * HEAD:(h + 1) * HEAD], wx_ref[h], preferred_element_type=F32) for h in range(nh)], axis=1)
    ra = _sigmoid(pre_a + vec[VEC_BA:VEC_BA + 1])
    ig = _sigmoid(pre_x + vec[VEC_BX:VEC_BX + 1])
    sp = _softplus(-vec[VEC_LAM:VEC_LAM + 1])
    log_a = -LRU_C * ra * sp
    a = jnp.exp(log_a)
    z = 2.0 * log_a
    one_minus_a2 = jnp.where(z > -0.01, -z * (1.0 + z * (0.5 + z * (1.0 / 6.0))), 1.0 - jnp.exp(z))
    mult = jnp.sqrt(one_minus_a2)
    return dict(xcb=xcb, ra=ra, ig=ig, sp=sp, a=a, mult=mult)


def _conv_pad_prev(pad_ref, cur, halo, first):
    pad_ref[0:SUBLANES, :] = jnp.where(first, 0.0, halo)
    pad_ref[SUBLANES:SUBLANES + cur.shape[0], :] = cur


def _gates_fwd(proj, d, cw8, vec8, wa, wx):
    s = proj.shape[0]
    hb = ROWS // SUBLANES

    def body(x_ref, halo_ref, cw_ref, vec_ref, wa_ref, wx_ref, a_ref, u_ref, xc_ref, pad):
        _conv_pad_prev(pad, x_ref[...], halo_ref[...], pl.program_id(0) == 0)
        vec = vec_ref[...]
        xc = vec[VEC_CB:VEC_CB + 1]
        for k in range(CONV_TAPS):
            xc = xc + cw_ref[k:k + 1, :] * pad[pl.ds(SUBLANES - (CONV_TAPS - 1) + k, ROWS), :]
        gm = _gate_math(xc, wa_ref, wx_ref, vec)
        xc_ref[...] = xc
        a_ref[...] = gm["a"]
        u_ref[...] = gm["mult"] * (gm["ig"] * xc)

    wspec = pl.BlockSpec(wa.shape, lambda i: (0, 0, 0))
    return pl.pallas_call(
        body, out_shape=[jax.ShapeDtypeStruct((s, d), F32)] * 3, grid=(s // ROWS,),
        in_specs=[pl.BlockSpec((ROWS, d), lambda i: (i, SEG_X)),
                  pl.BlockSpec((SUBLANES, d), lambda i: (jnp.maximum(i * hb - 1, 0), SEG_X)),
                  _vec_spec(d, SUBLANES), _vec_spec(d, SUBLANES), wspec, wspec],
        out_specs=[_row_spec(d)] * 3, scratch_shapes=[pltpu.VMEM((ROWS + SUBLANES, d), F32)],
        compiler_params=_params(("parallel",)), name="lru_gates_fwd")(proj, proj, cw8, vec8, wa, wx)


def _shift_rows(x, s, fill, up=False):
    rid = lax.broadcasted_iota(jnp.int32, x.shape, 0)
    if up:
        return jnp.where(rid < SUBLANES - s, pltpu.roll(x, SUBLANES - s, axis=0), fill)
    return jnp.where(rid >= s, pltpu.roll(x, s, axis=0), fill)


def _scan_fwd(a, u, proj):
    s, d = a.shape

    def body(a_ref, u_ref, g_ref, hp_ref, h2_ref, y_ref, carry):
        @pl.when(pl.program_id(0) == 0)
        def _():
            carry[...] = jnp.zeros_like(carry)

        def group(gi, before):
            rows = pl.ds(pl.multiple_of(gi * SUBLANES, SUBLANES), SUBLANES)
            ca, cb = a_ref[rows, :], u_ref[rows, :]
            for sh in (1, 2, 4):
                cb = ca * _shift_rows(cb, sh, 0.0) + cb
                ca = ca * _shift_rows(ca, sh, 1.0)
            h = cb + ca * before
            hp_ref[rows, :] = jnp.where(lax.broadcasted_iota(jnp.int32, h.shape, 0) == 0, before, pltpu.roll(h, 1, axis=0))
            h2_ref[rows, :] = h.astype(BF16)
            y_ref[rows, :] = (h * _gelu(g_ref[rows, :])).astype(BF16)
            return jnp.broadcast_to(h[SUBLANES - 1:SUBLANES, :], h.shape)

        carry[...] = lax.fori_loop(0, ROWS // SUBLANES, group, carry[...])

    return pl.pallas_call(
        body, out_shape=[jax.ShapeDtypeStruct((s, d), F32), jax.ShapeDtypeStruct((s, d), BF16), jax.ShapeDtypeStruct((s, d), BF16)],
        grid=(s // ROWS,), in_specs=[_row_spec(d), _row_spec(d), pl.BlockSpec((ROWS, d), lambda i: (i, SEG_GATE))],
        out_specs=[_row_spec(d)] * 3, scratch_shapes=[pltpu.VMEM((SUBLANES, d), F32)],
        compiler_params=_params(("arbitrary",)), name="lru_scan_fwd")(a, u, proj)


def _scan_bwd(a, hp, dh):
    s, d = a.shape
    nb = s // ROWS

    def body(a_ref, hp_ref, dh_ref, g_ref, da_ref, carry):
        @pl.when(pl.program_id(0) == 0)
        def _():
            carry[...] = jnp.zeros_like(carry)

        def group(j, after):
            rows = pl.ds(pl.multiple_of((ROWS // SUBLANES - 1 - j) * SUBLANES, SUBLANES), SUBLANES)
            ca, dhv = a_ref[rows, :], dh_ref[rows, :]
            cb = ca * dhv
            for sh in (1, 2, 4):
                cb = ca * _shift_rows(cb, sh, 0.0, up=True) + cb
                ca = ca * _shift_rows(ca, sh, 1.0, up=True)
            c = cb + ca * after
            last = lax.broadcasted_iota(jnp.int32, c.shape, 0) == SUBLANES - 1
            g = dhv + jnp.where(last, after, pltpu.roll(c, SUBLANES - 1, axis=0))
            g_ref[rows, :] = g
            da_ref[rows, :] = g * hp_ref[rows, :]
            return jnp.broadcast_to(c[0:1, :], c.shape)

        carry[...] = lax.fori_loop(0, ROWS // SUBLANES, group, carry[...])

    rows_rev = pl.BlockSpec((ROWS, d), lambda i: (nb - 1 - i, 0))
    return pl.pallas_call(
        body, out_shape=[jax.ShapeDtypeStruct((s, d), F32)] * 2, grid=(nb,), in_specs=[rows_rev] * 3, out_specs=[rows_rev] * 2,
        scratch_shapes=[pltpu.VMEM((SUBLANES, d), F32)], compiler_params=_params(("arbitrary",)), name="lru_scan_bwd")(a, hp, dh)


def _gates_bwd(g, da_in, xc, wa, wx, vec8):
    s, d = xc.shape
    nh = d // HEAD

    def body(g_ref, da_ref, xc_ref, wa_ref, wx_ref, vec_ref, dxc_ref, dwa_ref, dwx_ref, dvec_ref):
        @pl.when(pl.program_id(0) == 0)
        def _():
            dwa_ref[...] = jnp.zeros_like(dwa_ref)
            dwx_ref[...] = jnp.zeros_like(dwx_ref)
            dvec_ref[...] = jnp.zeros_like(dvec_ref)

        xc_v, vec = xc_ref[...], vec_ref[...]
        du, da = g_ref[...], da_ref[...]
        gm = _gate_math(xc_v, wa_ref, wx_ref, vec)
        ra, ig, sp, a, mult = gm["ra"], gm["ig"], gm["sp"], gm["a"], gm["mult"]
        dmult = du * ig * xc_v
        dlog_a = da * a - dmult * (a * a) / mult
        dpre_a = dlog_a * (-LRU_C * sp) * ra * (1.0 - ra)
        dpre_x = du * mult * xc_v * ig * (1.0 - ig)
        dlam = jnp.sum(dlog_a * (-LRU_C * ra), axis=0, keepdims=True) * (-_sigmoid(-vec[VEC_LAM:VEC_LAM + 1]))
        dvec_ref[VEC_BA:VEC_BA + 1, :] += jnp.sum(dpre_a, axis=0, keepdims=True)
        dvec_ref[VEC_BX:VEC_BX + 1, :] += jnp.sum(dpre_x, axis=0, keepdims=True)
        dvec_ref[VEC_LAM:VEC_LAM + 1, :] += dlam
        dab, dxb, xcb = dpre_a.astype(BF16), dpre_x.astype(BF16), gm["xcb"]
        back = []
        for h in range(nh):
            cols = slice(h * HEAD, (h + 1) * HEAD)
            dwa_ref[h] += lax.dot_general(xcb[:, cols], dab[:, cols], TN, preferred_element_type=F32)
            dwx_ref[h] += lax.dot_general(xcb[:, cols], dxb[:, cols], TN, preferred_element_type=F32)
            back.append(lax.dot_general(dab[:, cols], wa_ref[h], NT, preferred_element_type=F32)
                        + lax.dot_general(dxb[:, cols], wx_ref[h], NT, preferred_element_type=F32))
        dxc_ref[...] = du * mult * ig + jnp.concatenate(back, axis=1)

    wspec = pl.BlockSpec(wa.shape, lambda i: (0, 0, 0))
    return pl.pallas_call(
        body, out_shape=[jax.ShapeDtypeStruct((s, d), F32), jax.ShapeDtypeStruct(wa.shape, F32), jax.ShapeDtypeStruct(wa.shape, F32),
                         jax.ShapeDtypeStruct((SUBLANES, d), F32)],
        grid=(s // ROWS,), in_specs=[_row_spec(d), _row_spec(d), _row_spec(d), wspec, wspec, _vec_spec(d, SUBLANES)],
        out_specs=[_row_spec(d), wspec, wspec, _vec_spec(d, SUBLANES)],
        compiler_params=_params(("arbitrary",)), name="lru_gates_bwd")(g, da_in, xc, wa, wx, vec8)


def _conv_bwd(dxc, proj, cw8):
    s, d = dxc.shape
    hb = ROWS // SUBLANES
    last = s // SUBLANES - 1

    def body(dc_ref, dnext_ref, x_ref, xprev_ref, cw_ref, dx_ref, dcw_ref, padd, padx):
        i = pl.program_id(0)

        @pl.when(i == 0)
        def _():
            dcw_ref[...] = jnp.zeros_like(dcw_ref)

        dc = dc_ref[...]
        padd[0:ROWS, :] = dc
        padd[ROWS:ROWS + SUBLANES, :] = jnp.where(i == pl.num_programs(0) - 1, 0.0, dnext_ref[...])
        _conv_pad_prev(padx, x_ref[...], xprev_ref[...], i == 0)
        dx = jnp.zeros_like(dc)
        for k in range(CONV_TAPS):
            dx = dx + cw_ref[k:k + 1, :] * padd[pl.ds(CONV_TAPS - 1 - k, ROWS), :]
            dcw_ref[k:k + 1, :] += jnp.sum(dc * padx[pl.ds(SUBLANES - (CONV_TAPS - 1) + k, ROWS), :], axis=0, keepdims=True)
        dcw_ref[CONV_TAPS:CONV_TAPS + 1, :] += jnp.sum(dc, axis=0, keepdims=True)
        dx_ref[...] = dx.astype(BF16)

    return pl.pallas_call(
        body, out_shape=[jax.ShapeDtypeStruct((s, d), BF16), jax.ShapeDtypeStruct((SUBLANES, d), F32)], grid=(s // ROWS,),
        in_specs=[_row_spec(d), pl.BlockSpec((SUBLANES, d), lambda i: (jnp.minimum((i + 1) * hb, last), 0)),
                  pl.BlockSpec((ROWS, d), lambda i: (i, SEG_X)),
                  pl.BlockSpec((SUBLANES, d), lambda i: (jnp.maximum(i * hb - 1, 0), SEG_X)), _vec_spec(d, SUBLANES)],
        out_specs=[_row_spec(d), _vec_spec(d, SUBLANES)],
        scratch_shapes=[pltpu.VMEM((ROWS + SUBLANES, d), F32), pltpu.VMEM((ROWS + SUBLANES, d), F32)],
        compiler_params=_params(("arbitrary",)), name="lru_conv_bwd")(dxc, dxc, proj, proj, cw8)


def _coords():
    return lax.axis_index("x"), lax.axis_index("y"), lax.axis_index("c")


def _other_chips(x, y):
    return [(1 - x, y), (x, 1 - y), (1 - x, 1 - y)]


def _slab(ref, kind, shard_shape, idx, half=None):
    r, c = shard_shape
    r0, nr = (0, r) if half is None else (half * (r // 2), r // 2)
    if kind == "col":
        return ref.at[pl.ds(r0, nr), pl.ds(pl.multiple_of(idx * c, LANES), c)]
    if kind == "row":
        return ref.at[pl.ds(pl.multiple_of(idx * r, SUBLANES) + r0, nr), :]
    return ref.at[idx, pl.ds(r0, nr), :]


def _full_shape(shard_shape, kind):
    r, c = shard_shape
    return {"col": (r, c * N_DEV), "row": (r * N_DEV, c), "slot": (N_DEV, r, c)}[kind]


def _handshake(peers):
    barrier = pltpu.get_barrier_semaphore()
    for peer in peers:
        pl.semaphore_signal(barrier, inc=1, device_id=peer, device_id_type=MESH)
    pl.semaphore_wait(barrier, len(peers))


def _launch(name, body, out_shape, operands, sems, sequencer_id):
    if sequencer_id is None:
        return pl.pallas_call(body, out_shape=out_shape, in_specs=[HBM] * len(operands), out_specs=[HBM] * len(out_shape),
                              scratch_shapes=sems, name=name)(*operands)
    return pl.kernel(body, out_type=out_shape, mesh=plsc.ScalarSubcoreMesh(axis_name="seq", num_cores=1), name=name,
                     scratch_types=sems, compiler_params=pltpu.CompilerParams(collective_id=sequencer_id))(*operands)


AG_COPIES = 10


def _all_gather(name, shards, kinds, sequencer_id=None):
    n = len(shards)
    shapes = [s.shape for s in shards]

    def body(*refs):
        ins, outs = refs[:n], refs[n:2 * n]
        send_sems, recv_sems, local_sems = refs[2 * n:]
        x, y, c = _coords()
        me, sib, xn, yn, dg = (x, y, c), (x, y, 1 - c), (1 - x, y, c), (x, 1 - y, c), (1 - x, 1 - y, c)
        if sequencer_id is not None:
            _handshake([sib, xn, yn])

        def part(i, dev, half=None):
            return _slab(outs[i], kinds[i], shapes[i], 4 * dev[0] + 2 * dev[1] + dev[2], half)

        def copy(i, k, block, half, to, own=False):
            r = shapes[i][0]
            src = part(i, block, half) if not own else (ins[i] if half is None else ins[i].at[pl.ds(half * (r // 2), r // 2), :])
            return pltpu.make_async_remote_copy(
                src_ref=src, dst_ref=part(i, block, half), send_sem=send_sems.at[AG_COPIES * i + k],
                recv_sem=recv_sems.at[AG_COPIES * i + k], device_id=to, device_id_type=MESH)

        def other_core(dev):
            return (dev[0], dev[1], 1 - c)

        started = []

        def start(cp):
            cp.start()
            started.append(cp)

        for i in range(n):
            start(copy(i, 1, me, 0, xn, own=True))
            start(copy(i, 4, me, 1, yn, own=True))
            start(copy(i, 2, me, 1, xn, own=True))
            start(copy(i, 3, me, 0, yn, own=True))
            start(copy(i, 0, me, None, sib, own=True))
        mine = [pltpu.make_async_copy(ins[i], part(i, me), local_sems.at[i]) for i in range(n)]
        for cp in mine:
            cp.start()
        for i in range(n):
            copy(i, 1, xn, 0, me).wait_recv()
            start(copy(i, 5, xn, 0, yn))
            copy(i, 4, yn, 1, me).wait_recv()
            start(copy(i, 6, yn, 1, xn))
        for i in range(n):
            copy(i, 2, xn, 1, me).wait_recv()
            start(copy(i, 7, xn, None, sib))
            copy(i, 3, yn, 0, me).wait_recv()
            start(copy(i, 8, yn, None, sib))
        for i in range(n):
            copy(i, 5, dg, 0, me).wait_recv()
            copy(i, 6, dg, 1, me).wait_recv()
            start(copy(i, 9, dg, None, sib))
        for i in range(n):
            copy(i, 0, sib, None, me).wait_recv()
            for k, dev in ((7, xn), (8, yn), (9, dg)):
                copy(i, k, other_core(dev), None, me).wait_recv()
        for cp in started:
            cp.wait_send()
        for cp in mine:
            cp.wait()

    out_shape = [jax.ShapeDtypeStruct(_full_shape(s.shape, k), s.dtype) for s, k in zip(shards, kinds)]
    sems = [pltpu.SemaphoreType.DMA((AG_COPIES * n,)), pltpu.SemaphoreType.DMA((AG_COPIES * n,)), pltpu.SemaphoreType.DMA((n,))]
    return _launch(name, body, out_shape, shards, sems, sequencer_id)


def _sibling_copies(kinds, shard_shapes):
    def make(ins, outs, send_sems, recv_sems):
        x, y, c = _coords()
        return [pltpu.make_async_remote_copy(
            src_ref=_slab(ins[i], kinds[i], shard_shapes[i], 2 * q + (1 - c)), dst_ref=outs[i].at[q],
            send_sem=send_sems.at[N_CHIP * i + q], recv_sem=recv_sems.at[N_CHIP * i + q],
            device_id=(x, y, 1 - c), device_id_type=MESH) for i in range(len(ins)) for q in range(N_CHIP)]
    return make


def _sibling_side(partials, kinds, shard_shapes):
    return (partials, [jax.ShapeDtypeStruct((N_CHIP, *s), BF16) for s in shard_shapes], N_CHIP * len(partials),
            _sibling_copies(kinds, shard_shapes))


def _exchange_siblings(name, partials, kinds, shard_shapes, sequencer_id=None):
    n = len(partials)
    make = _sibling_copies(kinds, shard_shapes)

    def body(*refs):
        if sequencer_id is not None:
            x, y, c = _coords()
            _handshake([(x, y, 1 - c)])
        cps = make(refs[:n], refs[n:2 * n], refs[2 * n], refs[2 * n + 1])
        for cp in cps:
            cp.start()
        for cp in cps:
            cp.wait()

    return _launch(name, body, [jax.ShapeDtypeStruct((N_CHIP, *s), BF16) for s in shard_shapes], partials,
                   [pltpu.SemaphoreType.DMA((N_CHIP * n,)), pltpu.SemaphoreType.DMA((N_CHIP * n,))], sequencer_id)


def _exchange_chips(name, chip_sums, sequencer_id=None):
    n = len(chip_sums)

    def body(*refs):
        ins, outs = refs[:n], refs[n:2 * n]
        send_sems, recv_sems = refs[2 * n:]
        x, y, c = _coords()
        if sequencer_id is not None:
            _handshake([(cx, cy, c) for cx, cy in _other_chips(x, y)])
        cps = []
        for i in range(n):
            for k, (cx, cy) in enumerate(_other_chips(x, y)):
                cps.append(pltpu.make_async_remote_copy(
                    src_ref=ins[i].at[2 * cx + cy], dst_ref=outs[i].at[k], send_sem=send_sems.at[3 * i + k],
                    recv_sem=recv_sems.at[3 * i + k], device_id=(cx, cy, c), device_id_type=MESH))
        for cp in cps:
            cp.start()
        for cp in cps:
            cp.wait()

    return _launch(name, body, [jax.ShapeDtypeStruct((3, *t.shape[1:]), BF16) for t in chip_sums], chip_sums,
                   [pltpu.SemaphoreType.DMA((3 * n,)), pltpu.SemaphoreType.DMA((3 * n,))], sequencer_id)


def _all_reduce_small(name, packed):
    rows = packed.shape[0] // N_DEV

    def body(p_ref, out_ref, rb, tot, send_sems, recv_sems):
        x, y, c = _coords()
        me = 4 * x + 2 * y + c

        def peer(k):
            return (x ^ (k >> 2), y ^ ((k >> 1) & 1), c ^ (k & 1))

        def rows_of(idx):
            return pl.ds(pl.multiple_of(idx * rows, SUBLANES), rows)

        def piece(ref, idx):
            return ref.at[rows_of(idx), :]

        scatter = [pltpu.make_async_remote_copy(src_ref=piece(p_ref, me ^ k), dst_ref=rb.at[k], send_sem=send_sems.at[k],
                                                recv_sem=recv_sems.at[k], device_id=peer(k), device_id_type=MESH) for k in range(1, N_DEV)]
        for cp in scatter:
            cp.start()
        acc = p_ref[rows_of(me), :]
        for cp in scatter:
            cp.wait_recv()
        for k in range(1, N_DEV):
            acc = acc + rb[k]
        tot[...] = acc
        out_ref[rows_of(me), :] = acc
        gather = [pltpu.make_async_remote_copy(src_ref=tot, dst_ref=piece(out_ref, me), send_sem=send_sems.at[N_DEV + k],
                                               recv_sem=recv_sems.at[N_DEV + k], device_id=peer(k), device_id_type=MESH)
                  for k in range(1, N_DEV)]
        for cp in gather:
            cp.start()
        for k in range(1, N_DEV):
            pltpu.make_async_remote_copy(src_ref=tot, dst_ref=piece(out_ref, me ^ k), send_sem=send_sems.at[N_DEV + k],
                                         recv_sem=recv_sems.at[N_DEV + k], device_id=peer(k), device_id_type=MESH).wait_recv()
        for cp in scatter + gather:
            cp.wait_send()

    vm = pl.BlockSpec(memory_space=pltpu.VMEM)
    return pl.pallas_call(
        body, out_shape=jax.ShapeDtypeStruct(packed.shape, F32), in_specs=[vm], out_specs=vm,
        scratch_shapes=[pltpu.VMEM((N_DEV, rows, LANES), F32), pltpu.VMEM((rows, LANES), F32),
                        pltpu.SemaphoreType.DMA((2 * N_DEV,)), pltpu.SemaphoreType.DMA((2 * N_DEV,))],
        compiler_params=pltpu.CompilerParams(vmem_limit_bytes=VMEM_LIMIT), name=name)(packed)


def _adamw_math(g, w, m, v):
    m = ADAM_B1 * m + (1.0 - ADAM_B1) * g
    v = ADAM_B2 * v + (1.0 - ADAM_B2) * (g * g)
    delta = -ADAM_LR * ((m / ADAM_C1) / (jnp.sqrt(v / ADAM_C2) + ADAM_EPS) + ADAM_WD * w)
    return delta, m, v


def _slab_spec(kind, shard_shape, tr, slab_of):
    r, c = shard_shape
    if kind == "col":
        return pl.BlockSpec((tr, c), lambda q, i, sc: (i, slab_of(q, sc)))
    return pl.BlockSpec((tr, c), lambda q, i, sc: (slab_of(q, sc) * (r // tr) + i, 0))


def _chip_sum(name, partial, recv, kind, shard_shape, core):
    r, c = shard_shape
    tr = _blk(r, 1024)

    def body(core_ref, p_ref, r_ref, o_ref):
        o_ref[...] = (p_ref[...].astype(F32) + r_ref[...].astype(F32)).astype(BF16)

    spec4 = pl.BlockSpec((None, tr, c), lambda q, i, sc: (q, i, 0))
    grid_spec = pltpu.PrefetchScalarGridSpec(
        num_scalar_prefetch=1, grid=(N_CHIP, r // tr),
        in_specs=[_slab_spec(kind, shard_shape, tr, lambda q, sc: 2 * q + sc[0]), spec4], out_specs=spec4)
    return pl.pallas_call(body, out_shape=jax.ShapeDtypeStruct((N_CHIP, r, c), BF16), grid_spec=grid_spec,
                          compiler_params=_params(("parallel", "parallel")), name=name)(core, partial, recv)


def _adamw_shard(name, parts, w, m, v, chip):
    r, c = w.shape
    n_parts = len(parts)
    tr = _blk(r // n_parts, 256)
    per = r // n_parts // tr

    def body(chip_ref, *refs):
        src, (w_ref, m_ref, v_ref), (g_out, d_out, m_out, v_out) = refs[:2 * n_parts], refs[2 * n_parts:2 * n_parts + 3], refs[2 * n_parts + 3:]
        for p in range(n_parts):
            @pl.when(pl.program_id(0) // per == p)
            def _():
                g = src[2 * p][...].astype(F32)
                for k in range(3):
                    g = g + src[2 * p + 1][k].astype(F32)
                g_out[...] = g
                d_out[...], m_out[...], v_out[...] = _adamw_math(g, w_ref[...], m_ref[...], v_ref[...])

    def part_specs(p):
        at = lambda i: jnp.clip(i - p * per, 0, per - 1)
        return [pl.BlockSpec((None, tr, c), lambda i, sc: (sc[0], at(i), 0)), pl.BlockSpec((3, tr, c), lambda i, sc: (0, at(i), 0))]

    blk = pl.BlockSpec((tr, c), lambda i, sc: (i, 0))
    grid_spec = pltpu.PrefetchScalarGridSpec(
        num_scalar_prefetch=1, grid=(r // tr,), in_specs=[s for p in range(n_parts) for s in part_specs(p)] + [blk, blk, blk], out_specs=[blk] * 4)
    return pl.pallas_call(body, out_shape=[jax.ShapeDtypeStruct((r, c), F32)] * 4, grid_spec=grid_spec,
                          compiler_params=_params(("parallel",)), name=name)(chip, *[a for p in parts for a in p], w, m, v)


def _adamw_small(name, g, w, m, v):
    def body(g_ref, w_ref, m_ref, v_ref, d_out, m_out, v_out):
        d_out[...], m_out[...], v_out[...] = _adamw_math(g_ref[...], w_ref[...], m_ref[...], v_ref[...])

    vm = pl.BlockSpec(memory_space=pltpu.VMEM)
    return pl.pallas_call(body, out_shape=[jax.ShapeDtypeStruct(g.shape, F32)] * 3, in_specs=[vm] * 4, out_specs=[vm] * 3,
                          compiler_params=pltpu.CompilerParams(vmem_limit_bytes=VMEM_LIMIT), name=name)(g, w, m, v)


def _pack_rows(arrays, total_rows):
    flat = [a.reshape(-1, LANES) for a in arrays]
    used = sum(f.shape[0] for f in flat)
    return jnp.concatenate(flat + [jnp.zeros((total_rows - used, LANES), F32)], axis=0)


def _unpack_rows(packed, like):
    out, at = [], 0
    for a in like:
        n = a.size // LANES
        out.append(packed[at:at + n].reshape(a.shape))
        at += n
    return out


def kernel(x, norm_mix_g, w_in, conv_w, conv_b, lru_wa, lru_ba, lru_wx, lru_bx, lru_lambda, w_proj_attn, w_proj_lru, w_out, norm_mlp_g, w_up, w_down, norm_final_g, loss_target, m_norm_mix_g, m_w_in, m_conv_w, m_conv_b, m_lru_wa, m_lru_ba, m_lru_wx, m_lru_bx, m_lru_lambda, m_w_proj_attn, m_w_proj_lru, m_w_out, m_norm_mlp_g, m_w_up, m_w_down, m_norm_final_g, v_norm_mix_g, v_w_in, v_conv_w, v_conv_b, v_lru_wa, v_lru_ba, v_lru_wx, v_lru_bx, v_lru_lambda, v_w_proj_attn, v_w_proj_lru, v_w_out, v_norm_mlp_g, v_w_up, v_w_down, v_norm_final_g):
    xs, tgt = x[0], loss_target[0]
    s, d = xs.shape
    nh = d // HEAD
    ix, iy, ic = _coords()
    core = jnp.reshape(ic, (1,)).astype(jnp.int32)
    chip = jnp.reshape(2 * ix + iy, (1,)).astype(jnp.int32)
    dev = 4 * ix + 2 * iy + ic

    big = [w_in[0], w_proj_attn[0], w_proj_lru[0], w_out[0], w_up[0], w_down[0]]
    big_m = [m_w_in[0], m_w_proj_attn[0], m_w_proj_lru[0], m_w_out[0], m_w_up[0], m_w_down[0]]
    big_v = [v_w_in[0], v_w_proj_attn[0], v_w_proj_lru[0], v_w_out[0], v_w_up[0], v_w_down[0]]
    kinds = ["col", "row", "row", "row", "col", "row"]
    pad_taps = lambda t: jnp.pad(t, ((0, SUBLANES - CONV_TAPS), (0, 0)))
    shards = [w.astype(BF16) for w in big]
    pad_taps2 = lambda t: jnp.pad(t, ((0, 2 * SUBLANES - CONV_TAPS), (0, 0)))
    win, cw_slots = _all_gather("all_gather_w_in", [shards[0], pad_taps2(conv_w[0])], ["col", "slot"])
    later = lax.optimization_barrier((shards[1:], win))[0]
    wpa, wpl, wout = _all_gather("all_gather_mix", later[:3], kinds[1:4], sequencer_id=1)
    wup, wdown = _all_gather("all_gather_mlp", later[3:], kinds[4:], sequencer_id=5)
    cw8 = jnp.transpose(cw_slots[:, :SUBLANES], (1, 0, 2)).reshape(SUBLANES, d)
    row_id = lax.broadcasted_iota(jnp.int32, (SUBLANES, d), 0)
    vec8 = sum(jnp.where(row_id == k, t, 0.0) for k, t in ((VEC_CB, conv_b), (VEC_BA, lru_ba), (VEC_BX, lru_bx), (VEC_LAM, lru_lambda)))
    wa16, wx16 = lru_wa[0].astype(BF16), lru_wx[0].astype(BF16)
    slopes = 2.0 ** (-8.0 * jnp.arange(1, nh + 1, dtype=F32) / nh)

    def seg_specs(*segs):
        return lambda bm, bn: [pl.BlockSpec((bm, bn), (lambda i, j, kk, sg=sg: (i, sg * (d // bn) + j))) for sg in segs]

    def plain_specs(k):
        return lambda bm, bn: [pl.BlockSpec((bm, bn), lambda i, j, kk: (i, j)) for _ in range(k)]

    xn = _rms_fwd("norm_mix", xs, norm_mix_g)
    proj = _mm_fwd("proj_in", xn, win, 0, 7 * d, [F32], bm=2048)[0]
    att, lse = _attn_fwd(proj, d, slopes)
    a3, u3, xc = _gates_fwd(proj, d, cw8, vec8, wa16, wx16)
    hp3, h2d, ylru = _scan_fwd(a3, u3, proj)
    pa = _mm_fwd("proj_attn", att, wpa, 0, d, [BF16], bm=2048)[0]

    def merge(acc, pa_b, ga, gl):
        return acc, _sigmoid(ga) * pa_b.astype(F32) + _sigmoid(gl) * acc

    plr, merged = _mm_fwd("proj_lru_merge", ylru, wpl, 0, d, [BF16, BF16], merge, (pa, proj, proj),
                          lambda bm, bn: plain_specs(1)(bm, bn) + seg_specs(SEG_GA, SEG_GL)(bm, bn), bn=512)
    h1 = _mm_fwd("mix_out", merged, wout, 0, d, [F32], lambda acc, r: (acc + r,), (xs,), plain_specs(1))[0]
    hn = _rms_fwd("norm_mlp", h1, norm_mlp_g)

    def relu2(acc):
        return acc, jnp.square(jnp.maximum(acc, 0.0))

    up, hid = _mm_fwd("mlp_up", hn, wup, 0, wup.shape[1], [BF16, BF16], relu2, bm=2048)
    h2 = _mm_fwd("mlp_down", hid, wdown, 0, d, [F32], lambda acc, r: (acc + r,), (h1,), plain_specs(1))[0]
    dh2, dh2b, dg3, loss_lanes = _final_loss(h2, tgt, norm_final_g.reshape(1, d))
    loss = lax.psum(0.5 / d * jnp.sum(loss_lanes), ("x", "y", "c"))
    dh2b = lax.optimization_barrier((dh2b, loss))[0]

    def reduce_group(tag, kk, shp, partials, from_sibling, sequencer_id):
        sums = [_chip_sum(f"chip_sum_{tag}_{i}", p, f, k, sh, core) for i, (p, f, k, sh) in enumerate(zip(partials, from_sibling, kk, shp))]
        return list(zip(sums, _exchange_chips(f"rs_chips_{tag}", sums, sequencer_id)))

    dup = _mm_nt("mlp_down_dx", dh2b, wdown, [BF16], lambda acc, u: (acc * (2.0 * jnp.maximum(u.astype(F32), 0.0)),), (up,), plain_specs(1))[0]
    g_wdown = _mm_tn("mlp_down_dw", hid, dh2b)
    g_wup = _mm_tn("mlp_up_dw", hn, dup)
    shp_mlp = [w.shape for w in big[4:]]
    (dhn,), sib_mlp = _mm_nt("mlp_up_dx", dup, wup, [F32], side=_sibling_side([g_wup, g_wdown], kinds[4:], shp_mlp))
    red_up, red_down = reduce_group("mlp", kinds[4:], shp_mlp, [g_wup, g_wdown], sib_mlp, 2)
    dhn = lax.optimization_barrier((dhn, red_up[0], red_down[0]))[0]
    dh1, dh1b, dg2 = _rms_bwd("norm_mlp_bwd", h1, norm_mlp_g, dhn, dh2)

    def merge_bwd(acc, pa_b, pl_b, ga, gl):
        sa, sl = _sigmoid(ga), _sigmoid(gl)
        return acc * sa, acc * sl, acc * pa_b.astype(F32) * sa * (1.0 - sa), acc * pl_b.astype(F32) * sl * (1.0 - sl)

    dpa, dpl, dga, dgl = _mm_nt("mix_out_dx", dh1b, wout, [BF16] * 4, merge_bwd, (pa, plr, proj, proj),
                                lambda bm, bn: plain_specs(2)(bm, bn) + seg_specs(SEG_GA, SEG_GL)(bm, bn), bn=512)
    g_wout = _mm_tn("mix_out_dw", merged, dh1b)
    datt = _mm_nt("proj_attn_dx", dpa, wpa, [F32], bm=2048)[0]
    g_wpa = _mm_tn("proj_attn_dw", att, dpa)

    def lru_out_bwd(acc, h_b, gate):
        return acc * _gelu(gate), acc * h_b.astype(F32) * _gelu_grad(gate)

    g_wpl = _mm_tn("proj_lru_dw", ylru, dpl)
    shp_mix = [w.shape for w in big[1:4]]
    (dh, dxg), sib_mix = _mm_nt("proj_lru_dx", dpl, wpl, [F32, BF16], lru_out_bwd, (h2d, proj),
                                lambda bm, bn: plain_specs(1)(bm, bn) + seg_specs(SEG_GATE)(bm, bn), bn=512,
                                side=_sibling_side([g_wpa, g_wpl, g_wout], kinds[1:4], shp_mix))
    red_pa, red_pl, red_out = reduce_group("mix", kinds[1:4], shp_mix, [g_wpa, g_wpl, g_wout], sib_mix, 3)
    g3, da3 = _scan_bwd(a3, hp3, dh)
    dxc, dwa, dwx, dvec = _gates_bwd(g3, da3, xc, wa16, wx16, vec8)
    dxr, dconv = _conv_bwd(dxc, proj, cw8)
    dproj = _attn_bwd(proj, d, datt, att, lse, slopes, (dxr, dxg, dga, dgl))

    def small_step(tag, grads, ws, ms, vs, like, after):
        n_rows = sum(g.size for g in grads) // LANES
        per_dev = -(-n_rows // (N_DEV * SUBLANES)) * SUBLANES
        packed = lax.optimization_barrier((_pack_rows(grads, N_DEV * per_dev), after))[0]
        total = _all_reduce_small(f"all_reduce_{tag}", packed)
        w_rows = -(-(sum(w.size for w in ws) // LANES) // SUBLANES) * SUBLANES
        upd = _adamw_small(f"adamw_{tag}", total[:w_rows], _pack_rows(ws, w_rows), _pack_rows(ms, w_rows), _pack_rows(vs, w_rows))
        return _unpack_rows(total, like), [_unpack_rows(t, ws) for t in upd]

    early_w = [conv_b, lru_wa, lru_ba, lru_wx, lru_bx, lru_lambda, norm_mlp_g, norm_final_g]
    early_m = [m_conv_b, m_lru_wa, m_lru_ba, m_lru_wx, m_lru_bx, m_lru_lambda, m_norm_mlp_g, m_norm_final_g]
    early_v = [v_conv_b, v_lru_wa, v_lru_ba, v_lru_wx, v_lru_bx, v_lru_lambda, v_norm_mlp_g, v_norm_final_g]
    early_g = [dconv[CONV_TAPS:CONV_TAPS + 1], dwa, dvec[VEC_BA:VEC_BA + 1], dwx, dvec[VEC_BX:VEC_BX + 1],
               dvec[VEC_LAM:VEC_LAM + 1], dg2, dg3, dconv[0:CONV_TAPS]]
    dproj = lax.optimization_barrier((dproj, red_up[1], red_down[1]))[0]
    early_sum, early_upd = small_step("small", early_g, early_w, early_m, early_v,
                                      early_w + [jax.ShapeDtypeStruct((1, CONV_TAPS, d), F32)], dproj)
    g_cw_full = early_sum[-1]
    cshard = conv_w.shape[2]
    g_cw = lax.dynamic_slice(g_cw_full, (0, 0, dev * cshard), (1, CONV_TAPS, cshard))
    cw_delta, cw_m, cw_v = (t[:CONV_TAPS][None] for t in _adamw_small(
        "adamw_conv_w", pad_taps(g_cw[0]), pad_taps(conv_w[0]), pad_taps(m_conv_w[0]), pad_taps(v_conv_w[0])))
    dproj = lax.optimization_barrier((dproj, early_sum, red_pa[1], red_pl[1], red_out[1]))[0]
    half = (big[0].shape[0] // 2, big[0].shape[1])
    g_in0 = _mm_tn("proj_in_dw_0", xn, dproj, part=(0, 2))
    g_in1, sib_in0 = _mm_tn("proj_in_dw_1", xn, dproj, part=(1, 2), side=_sibling_side([g_in0], ["col"], [half]))
    red_in = reduce_group("in_0", ["col"], [half], [g_in0], sib_in0, 4)
    dproj = lax.optimization_barrier((dproj, red_in[0][0]))[0]
    (dxn0,), sib_in1 = _mm_nt("proj_in_dx_0", dproj, win, [F32], part=(0, 2), side=_sibling_side([g_in1], ["col"], [half]))
    red_in += reduce_group("in_1", ["col"], [half], [g_in1], sib_in1, 6)
    dproj = lax.optimization_barrier((dproj, red_in[1][0]))[0]
    dxn1 = _mm_nt("proj_in_dx_1", dproj, win, [F32], part=(1, 2))[0]
    dxn = jnp.concatenate([dxn0, dxn1], axis=0)
    dxn = lax.optimization_barrier((dxn, red_in[0][1]))[0]
    grad_x, _, dg1 = _rms_bwd("norm_mix_bwd", xs, norm_mix_g, dxn, dh1)
    red_up, red_down = lax.optimization_barrier(((red_up, red_down), dg1))[0]
    big_out = {i: _adamw_shard(f"adamw_{i}", [red], big[i], big_m[i], big_v[i], chip) for i, red in ((4, red_up), (5, red_down))}
    big_out.update({i: _adamw_shard(f"adamw_{i}", [red], big[i], big_m[i], big_v[i], chip) for i, red in ((1, red_pa), (2, red_pl), (3, red_out))})
    late_sum, late_upd = small_step("norm_mix", [dg1], [norm_mix_g], [m_norm_mix_g], [v_norm_mix_g], [norm_mix_g], (big_out[4], big_out[5]))
    big_out[0] = _adamw_shard("adamw_0", red_in, big[0], big_m[0], big_v[0], chip)
    s_grad = late_sum + early_sum[:-1]
    s_delta, s_m, s_v = (late_upd[j] + early_upd[j] for j in range(3))


    names = ["norm_mix_g", "w_in", "conv_w", "conv_b", "lru_wa", "lru_ba", "lru_wx", "lru_bx", "lru_lambda", "w_proj_attn", "w_proj_lru",
             "w_out", "norm_mlp_g", "w_up", "w_down", "norm_final_g"]
    small_names = ["norm_mix_g", "conv_b", "lru_wa", "lru_ba", "lru_wx", "lru_bx", "lru_lambda", "norm_mlp_g", "norm_final_g"]
    big_names = ["w_in", "w_proj_attn", "w_proj_lru", "w_out", "w_up", "w_down"]
    res = {"conv_w": (g_cw, cw_delta, cw_m, cw_v)}
    for i, nm in enumerate(small_names):
        res[nm] = (s_grad[i], s_delta[i], s_m[i], s_v[i])
    for i, nm in enumerate(big_names):
        res[nm] = tuple(t[None] for t in big_out[i])
    return (loss, grad_x[None], *[res[nm][0] for nm in names], *[res[nm][1] for nm in names],
            *[res[nm][2] for nm in names], *[res[nm][3] for nm in names])
```

```python
import jax
import jax.numpy as jnp
from jax import lax
from jax.experimental import pallas as pl
from jax.experimental.pallas import tpu as pltpu
from jax.experimental.pallas import tpu_sc as plsc

F32, BF16 = jnp.float32, jnp.bfloat16
MESH = pl.DeviceIdType.MESH
HBM = pl.BlockSpec(memory_space=pltpu.HBM)
N_DEV = 8
N_CHIP = 4
HEAD = 128
SPAN = 128
DILATIONS = (1, 4, 16)
CONV_TAPS = 4
LRU_C = 8.0
NORM_EPS = 1e-6
LANES = 128
SUBLANES = 8
VMEM_LIMIT = 56 * 1024 * 1024
ADAM_LR, ADAM_B1, ADAM_B2, ADAM_EPS, ADAM_WD, ADAM_STEP = 0.001, 0.9, 0.999, 1e-08, 0.01, 10
ADAM_C1 = 1.0 - ADAM_B1 ** ADAM_STEP
ADAM_C2 = 1.0 - ADAM_B2 ** ADAM_STEP
NEG = -1e30


def _params(sem=None):
    return pltpu.CompilerParams(dimension_semantics=sem, vmem_limit_bytes=VMEM_LIMIT)


def _sigmoid(v):
    return 1.0 / (1.0 + jnp.exp(-v))


def _gelu(v):
    k = 0.7978845608028654
    return 0.5 * v * (1.0 + jnp.tanh(k * (v + 0.044715 * v * v * v)))


def _gelu_grad(v):
    k = 0.7978845608028654
    t = jnp.tanh(k * (v + 0.044715 * v * v * v))
    return 0.5 * (1.0 + t) + 0.5 * v * (1.0 - t * t) * k * (1.0 + 3.0 * 0.044715 * v * v)


NN = (((1,), (0,)), ((), ()))
NT = (((1,), (1,)), ((), ()))
TN = (((0,), (0,)), ((), ()))


def _mm(name, a, a_spec, b, b_spec, dn, grid, out_shapes, out_specs, acc_block, epilogue=None, extras=(), extra_specs=(), side=None):
    nk, ne, no = grid[2], len(extras), len(out_shapes)
    side_ops, side_shapes, side_copies, make_copies = side if side is not None else ((), (), 0, None)
    ns_in, ns_out = len(side_ops), len(side_shapes)

    def body(*refs):
        a_ref, b_ref = refs[0], refs[1]
        ex, side_in = refs[2:2 + ne], refs[2 + ne:2 + ne + ns_in]
        outs = refs[2 + ne + ns_in:2 + ne + ns_in + no]
        side_out = refs[2 + ne + ns_in + no:2 + ne + ns_in + no + ns_out]
        scratch = refs[2 + ne + ns_in + no + ns_out:]
        at = [pl.program_id(ax) for ax in range(3)]
        if side is not None:
            @pl.when((at[0] == 0) & (at[1] == 0) & (at[2] == 0))
            def _():
                for cp in make_copies(side_in, side_out, scratch[-2], scratch[-1]):
                    cp.start()

        part = lax.dot_general(a_ref[...], b_ref[...], dn, preferred_element_type=F32)

        def finish(acc):
            vals = epilogue(acc, *[e[...] for e in ex]) if epilogue is not None else (acc,)
            for o, v in zip(outs, vals):
                o[...] = v.astype(o.dtype)

        if nk == 1:
            finish(part)
        else:
            acc_ref, k = scratch[0], at[2]

            @pl.when(k == 0)
            def _():
                acc_ref[...] = part

            @pl.when(k > 0)
            def _():
                acc_ref[...] += part

            @pl.when(k == nk - 1)
            def _():
                finish(acc_ref[...])

        if side is not None:
            @pl.when((at[0] == grid[0] - 1) & (at[1] == grid[1] - 1) & (at[2] == grid[2] - 1))
            def _():
                for cp in make_copies(side_in, side_out, scratch[-2], scratch[-1]):
                    cp.wait()

    scratch_shapes = [pltpu.VMEM(acc_block, F32)] if nk > 1 else []
    if side is not None:
        scratch_shapes += [pltpu.SemaphoreType.DMA((side_copies,)), pltpu.SemaphoreType.DMA((side_copies,))]
    res = pl.pallas_call(
        body, out_shape=[*out_shapes, *side_shapes], grid=grid, in_specs=[a_spec, b_spec, *extra_specs, *[HBM] * ns_in],
        out_specs=[*out_specs, *[HBM] * ns_out], scratch_shapes=scratch_shapes,
        compiler_params=_params(("arbitrary",) * 3 if side is not None else ("parallel", "parallel", "arbitrary")),
        name=name)(a, b, *extras, *side_ops)
    return res if side is None else (res[:no], res[no:])


def _blk(n, pref):
    return pref if n % pref == 0 else n


def _kblk(k):
    return k if k <= 2048 else next(b for b in (2048, 1024, 512) if k % b == 0)


def _mm_fwd(name, a, w, col0, ncols, out_dtypes, epilogue=None, extras=(), extra_specs_fn=None, seg_out=None, bm=1024, bn=1024):
    m, k = a.shape
    bm, bn = _blk(m, bm), _blk(ncols, bn)
    bk = _kblk(k)
    nk = k // bk
    cb0 = col0 // bn
    grid = (m // bm, ncols // bn, nk)
    a_spec = pl.BlockSpec((bm, bk), lambda i, j, kk: (i, kk))
    b_spec = pl.BlockSpec((bk, bn), lambda i, j, kk: (kk, cb0 + j))
    if seg_out is None:
        shapes = [jax.ShapeDtypeStruct((m, ncols), dt) for dt in out_dtypes]
        specs = [pl.BlockSpec((bm, bn), lambda i, j, kk: (i, j)) for _ in out_dtypes]
    else:
        per = seg_out // bn
        shapes = [jax.ShapeDtypeStruct((ncols // seg_out, m, seg_out), dt) for dt in out_dtypes]
        specs = [pl.BlockSpec((None, bm, bn), lambda i, j, kk: (j // per, i, j % per)) for _ in out_dtypes]
    ex_specs = extra_specs_fn(bm, bn) if extra_specs_fn else ()
    return _mm(name, a, a_spec, w, b_spec, NN, grid, shapes, specs, (bm, bn), epilogue, extras, ex_specs)


def _mm_nt(name, a, w, out_dtypes, epilogue=None, extras=(), extra_specs_fn=None, part=(0, 1), side=None, bm=1024, bn=1024):
    n = w.shape[0]
    if a.ndim == 3:
        seg_cols, m, k = a.shape[2], a.shape[1], a.shape[0] * a.shape[2]
    else:
        m, k = a.shape
    m = m // part[1]
    bm, bn = _blk(m, bm), _blk(n, bn)
    bk = _kblk(k)
    grid = (m // bm, n // bn, k // bk)
    i0 = part[0] * (m // bm)
    if a.ndim == 3:
        per = seg_cols // bk
        a_spec = pl.BlockSpec((None, bm, bk), lambda i, j, kk: (kk // per, i0 + i, kk % per))
    else:
        a_spec = pl.BlockSpec((bm, bk), lambda i, j, kk: (i0 + i, kk))
    b_spec = pl.BlockSpec((bn, bk), lambda i, j, kk: (j, kk))
    shapes = [jax.ShapeDtypeStruct((m, n), dt) for dt in out_dtypes]
    specs = [pl.BlockSpec((bm, bn), lambda i, j, kk: (i, j)) for _ in out_dtypes]
    ex_specs = extra_specs_fn(bm, bn) if extra_specs_fn else ()
    return _mm(name, a, a_spec, w, b_spec, NT, grid, shapes, specs, (bm, bn), epilogue, extras, ex_specs, side)


def _mm_tn(name, a, b, part=(0, 1), side=None, bm=1024, bn=2048):
    t, m = a.shape
    n = b.shape[1] if b.ndim == 2 else b.shape[0] * b.shape[2]
    m = m // part[1]
    bm, bn = _blk(m, bm), _blk(n, bn)
    grid = (m // bm, n // bn, 1)
    i0 = part[0] * (m // bm)
    a_spec = pl.BlockSpec((t, bm), lambda i, j, kk: (0, i0 + i))
    if b.ndim == 3:
        per = b.shape[2] // bn
        b_spec = pl.BlockSpec((None, t, bn), lambda i, j, kk: (j // per, 0, j % per))
    else:
        b_spec = pl.BlockSpec((t, bn), lambda i, j, kk: (0, j))
    res = _mm(name, a, a_spec, b, b_spec, TN, grid, [jax.ShapeDtypeStruct((m, n), BF16)],
              [pl.BlockSpec((bm, bn), lambda i, j, kk: (i, j))], (bm, bn), side=side)
    return res[0] if side is None else (res[0][0], res[1])


ROWS = 256


def _row_spec(d):
    return pl.BlockSpec((ROWS, d), lambda i: (i, 0))


def _vec_spec(d, rows=1):
    return pl.BlockSpec((rows, d), lambda i: (0, 0))


def _rms_fwd(name, x, g):
    s, d = x.shape

    def body(x_ref, g_ref, o_ref):
        xv = x_ref[...]
        r = lax.rsqrt(jnp.mean(xv * xv, axis=-1, keepdims=True) + NORM_EPS)
        o_ref[...] = (xv * r * g_ref[...]).astype(BF16)

    return pl.pallas_call(body, out_shape=jax.ShapeDtypeStruct((s, d), BF16), grid=(s // ROWS,),
                          in_specs=[_row_spec(d), _vec_spec(d)], out_specs=_row_spec(d),
                          compiler_params=_params(("parallel",)), name=name)(x, g)


def _rms_bwd_math(xv, g, dy):
    r = lax.rsqrt(jnp.mean(xv * xv, axis=-1, keepdims=True) + NORM_EPS)
    n = xv * r
    z = dy * g
    dx = r * (z - n * jnp.mean(z * n, axis=-1, keepdims=True))
    return dx, jnp.sum(dy * n, axis=0, keepdims=True)


def _rms_bwd(name, x, g, dy, resid):
    s, d = x.shape

    def body(x_ref, g_ref, dy_ref, r_ref, dx_ref, dxb_ref, dg_ref):
        dx, dg = _rms_bwd_math(x_ref[...], g_ref[...], dy_ref[...])
        dx = dx + r_ref[...]
        dx_ref[...] = dx
        dxb_ref[...] = dx.astype(BF16)

        @pl.when(pl.program_id(0) == 0)
        def _():
            dg_ref[...] = jnp.zeros_like(dg_ref)

        dg_ref[...] += dg

    return pl.pallas_call(
        body, out_shape=[jax.ShapeDtypeStruct((s, d), F32), jax.ShapeDtypeStruct((s, d), BF16), jax.ShapeDtypeStruct((1, d), F32)],
        grid=(s // ROWS,), in_specs=[_row_spec(d), _vec_spec(d), _row_spec(d), _row_spec(d)],
        out_specs=[_row_spec(d), _row_spec(d), _vec_spec(d)], compiler_params=_params(("arbitrary",)), name=name)(x, g, dy, resid)


def _final_loss(h2, tgt, g):
    s, d = h2.shape

    def body(x_ref, t_ref, g_ref, dx_ref, dxb_ref, dg_ref, ls_ref):
        xv, gv = x_ref[...], g_ref[...]
        r = lax.rsqrt(jnp.mean(xv * xv, axis=-1, keepdims=True) + NORM_EPS)
        diff = xv * r * gv - t_ref[...]
        dx, dg = _rms_bwd_math(xv, gv, diff * (1.0 / d))
        dx_ref[...] = dx
        dxb_ref[...] = dx.astype(BF16)

        @pl.when(pl.program_id(0) == 0)
        def _():
            dg_ref[...] = jnp.zeros_like(dg_ref)
            ls_ref[...] = jnp.zeros_like(ls_ref)

        dg_ref[...] += dg
        ls_ref[...] += jnp.sum(diff * diff, axis=0, keepdims=True)

    return pl.pallas_call(
        body, out_shape=[jax.ShapeDtypeStruct((s, d), F32), jax.ShapeDtypeStruct((s, d), BF16),
                         jax.ShapeDtypeStruct((1, d), F32), jax.ShapeDtypeStruct((1, d), F32)],
        grid=(s // ROWS,), in_specs=[_row_spec(d), _row_spec(d), _vec_spec(d)],
        out_specs=[_row_spec(d), _row_spec(d), _vec_spec(d), _vec_spec(d)],
        compiler_params=_params(("arbitrary",)), name="final_norm_loss")(h2, tgt, g)


ATTN_Q = 128


ATTN_BATCH = 8


def _attn_units(s):
    units = []
    for gi, d in enumerate(DILATIONS):
        for r in range(d):
            for q0 in range(0, s // d, ATTN_Q):
                k0 = max(q0 - SPAN, 0)
                units.append((gi, d, r, q0, k0, q0 + ATTN_Q - k0))
    return units


def _stream_rows(d, r, start, size):
    return pl.ds(r + start * d, size) if d == 1 else pl.ds(r + start * d, size, stride=d)


def _attn_scores(q_ref, k_ref, slope, d, r, q0, k0, nk):
    qrows, krows = _stream_rows(d, r, q0, ATTN_Q), _stream_rows(d, r, k0, nk)
    qb = q_ref[qrows, :].astype(BF16)
    kb = k_ref[krows, :].astype(BF16)
    sc = lax.dot_general(qb, kb, NT, preferred_element_type=F32) * (HEAD ** -0.5)
    qi = lax.broadcasted_iota(jnp.int32, (ATTN_Q, nk), 0)
    kj = lax.broadcasted_iota(jnp.int32, (ATTN_Q, nk), 1)
    dist = (q0 - k0) + qi - kj
    valid = (dist >= 0) & (dist <= SPAN)
    sc = sc - (slope * d) * dist.astype(F32)
    return jnp.where(valid, sc, NEG), valid, qb, kb, qrows, krows


def _attn_fwd(proj, dm, slopes):
    s = proj.shape[0]
    units = _attn_units(s)

    def body(sl_ref, q_ref, k_ref, v_ref, att_ref, lse_ref, *scr):
        o_scr, l_scr = scr[:3], scr[3:]
        slope = sl_ref[pl.program_id(0)]
        for first in range(0, len(units), ATTN_BATCH):
            batch = units[first:first + ATTN_BATCH]
            scored = [_attn_scores(q_ref, k_ref, slope, d, r, q0, k0, nk) for _, d, r, q0, k0, nk in batch]
            soft = []
            for sc, _, _, _, _, _ in scored:
                m = jnp.max(sc, axis=-1, keepdims=True)
                p = jnp.exp(sc - m)
                soft.append((m, p, jnp.sum(p, axis=-1, keepdims=True)))
            outs = [lax.dot_general(p.astype(BF16), v_ref[sco[5], :].astype(BF16), NN, preferred_element_type=F32)
                    for (m, p, l), sco in zip(soft, scored)]
            for (gi, *_), (m, p, l), sco, o in zip(batch, soft, scored, outs):
                o_scr[gi][sco[4], :] = o / l
                l_scr[gi][sco[4], :] = jnp.broadcast_to(m + jnp.log(l), (ATTN_Q, HEAD))
        l0, l1, l2 = l_scr[0][...], l_scr[1][...], l_scr[2][...]
        m = jnp.maximum(jnp.maximum(l0, l1), l2)
        w0, w1, w2 = jnp.exp(l0 - m), jnp.exp(l1 - m), jnp.exp(l2 - m)
        tot = w0 + w1 + w2
        att_ref[...] = ((w0 * o_scr[0][...] + w1 * o_scr[1][...] + w2 * o_scr[2][...]) / tot).astype(BF16)
        lse_ref[...] = m + jnp.log(tot)

    def seg(i):
        return pl.BlockSpec((s, HEAD), lambda h: (0, i * (dm // HEAD) + h))

    col = pl.BlockSpec((s, HEAD), lambda h: (0, h))
    return pl.pallas_call(
        body, out_shape=[jax.ShapeDtypeStruct((s, dm), BF16), jax.ShapeDtypeStruct((s, dm), F32)], grid=(dm // HEAD,),
        in_specs=[pl.BlockSpec(memory_space=pltpu.SMEM), seg(0), seg(1), seg(2)], out_specs=[col, col],
        scratch_shapes=[pltpu.VMEM((s, HEAD), F32)] * (2 * len(DILATIONS)),
        compiler_params=_params(("parallel",)), name="attn_fwd")(slopes, proj, proj, proj)


def _attn_bwd(proj, dm, datt, att, lse, slopes, others):
    s = proj.shape[0]
    units = _attn_units(s)

    def body(sl_ref, q_ref, k_ref, v_ref, do_ref, att_ref, lse_ref, o3, o4, o5, o6, out_ref, dq_scr, dk_scr, dv_scr, dl_scr):
        slope = sl_ref[pl.program_id(0)]
        delta = jnp.sum(do_ref[...] * att_ref[...].astype(F32), axis=-1, keepdims=True)
        dl_scr[...] = jnp.broadcast_to(delta, (s, HEAD))
        dq_scr[...] = jnp.zeros_like(dq_scr)
        dk_scr[...] = jnp.zeros_like(dk_scr)
        dv_scr[...] = jnp.zeros_like(dv_scr)
        for first in range(0, len(units), ATTN_BATCH):
            scored = [_attn_scores(q_ref, k_ref, slope, d, r, q0, k0, nk) for _, d, r, q0, k0, nk in units[first:first + ATTN_BATCH]]
            dobs = [do_ref[sco[4], :].astype(BF16) for sco in scored]
            dps = [lax.dot_general(dob, v_ref[sco[5], :].astype(BF16), NT, preferred_element_type=F32) for dob, sco in zip(dobs, scored)]
            ps = [jnp.where(sco[1], jnp.exp(sco[0] - lse_ref[sco[4], :][:, 0:1]), 0.0) for sco in scored]
            dss = [(p * (dp - dl_scr[sco[4], :][:, 0:1]) * (HEAD ** -0.5)).astype(BF16) for p, dp, sco in zip(ps, dps, scored)]
            dqs = [lax.dot_general(ds, sco[3], NN, preferred_element_type=F32) for ds, sco in zip(dss, scored)]
            dks = [lax.dot_general(ds, sco[2], TN, preferred_element_type=F32) for ds, sco in zip(dss, scored)]
            dvs = [lax.dot_general(p.astype(BF16), dob, TN, preferred_element_type=F32) for p, dob in zip(ps, dobs)]
            for sco, dq, dk, dv in zip(scored, dqs, dks, dvs):
                dq_scr[sco[4], :] += dq
                dk_scr[sco[5], :] += dk
                dv_scr[sco[5], :] += dv
        for j, scr in enumerate((dq_scr, dk_scr, dv_scr)):
            out_ref[j] = scr[...].astype(BF16)
        for j, other in enumerate((o3, o4, o5, o6)):
            out_ref[3 + j] = other[...]

    def seg(i):
        return pl.BlockSpec((s, HEAD), lambda h: (0, i * (dm // HEAD) + h))

    col = pl.BlockSpec((s, HEAD), lambda h: (0, h))
    return pl.pallas_call(
        body, out_shape=jax.ShapeDtypeStruct((7, s, dm), BF16), grid=(dm // HEAD,),
        in_specs=[pl.BlockSpec(memory_space=pltpu.SMEM), seg(0), seg(1), seg(2), col, col, col, col, col, col, col],
        out_specs=pl.BlockSpec((7, s, HEAD), lambda h: (0, 0, h)), scratch_shapes=[pltpu.VMEM((s, HEAD), F32)] * 4,
        compiler_params=_params(("parallel",)), name="attn_bwd")(slopes, proj, proj, proj, datt, att, lse, *others)


VEC_CB, VEC_BA, VEC_BX, VEC_LAM = 0, 1, 2, 3
SEG_Q, SEG_K, SEG_V, SEG_X, SEG_GATE, SEG_GA, SEG_GL = range(7)


def _softplus(z):
    return jnp.maximum(z, 0.0) + jnp.log1p(jnp.exp(-jnp.abs(z)))


def _gate_math(xc, wa_ref, wx_ref, vec):
    xcb = xc.astype(BF16)
    nh = xc.shape[1] // HEAD
    pre_a = jnp.concatenate([jnp.dot(xcb[:, h * HEAD:(h + 1) * HEAD], wa_ref[h], preferred_element_type=F32) for h in range(nh)], axis=1)
    pre_x = jnp.concatenate([jnp.dot(xcb[:, h * HEAD:(h + 1) * HEAD], wx_ref[h], preferred_element_type=F32) for h in range(nh)], axis=1)
    ra = _sigmoid(pre_a + vec[VEC_BA:VEC_BA + 1])
    ig = _sigmoid(pre_x + vec[VEC_BX:VEC_BX + 1])
    sp = _softplus(-vec[VEC_LAM:VEC_LAM + 1])
    log_a = -LRU_C * ra * sp
    a = jnp.exp(log_a)
    z = 2.0 * log_a
    one_minus_a2 = jnp.where(z > -0.01, -z * (1.0 + z * (0.5 + z * (1.0 / 6.0))), 1.0 - jnp.exp(z))
    mult = jnp.sqrt(one_minus_a2)
    return dict(xcb=xcb, ra=ra, ig=ig, sp=sp, a=a, mult=mult)


def _conv_pad_prev(pad_ref, cur, halo, first):
    pad_ref[0:SUBLANES, :] = jnp.where(first, 0.0, halo)
    pad_ref[SUBLANES:SUBLANES + cur.shape[0], :] = cur


def _shift_rows(x, s, fill, up=False):
    rid = lax.broadcasted_iota(jnp.int32, x.shape, 0)
    if up:
        return jnp.where(rid < SUBLANES - s, pltpu.roll(x, SUBLANES - s, axis=0), fill)
    return jnp.where(rid >= s, pltpu.roll(x, s, axis=0), fill)


def _lru_fwd(proj, d, cw8, vec8, wa, wx):
    s = proj.shape[0]
    hb = ROWS // SUBLANES

    def body(x_ref, halo_ref, g_ref, cw_ref, vec_ref, wa_ref, wx_ref, xc_ref, hp_ref, h2_ref, y_ref, pad, a_scr, u_scr, carry):
        @pl.when(pl.program_id(0) == 0)
        def _():
            carry[...] = jnp.zeros_like(carry)

        _conv_pad_prev(pad, x_ref[...], halo_ref[...], pl.program_id(0) == 0)
        vec = vec_ref[...]
        xc = vec[VEC_CB:VEC_CB + 1]
        for k in range(CONV_TAPS):
            xc = xc + cw_ref[k:k + 1, :] * pad[pl.ds(SUBLANES - (CONV_TAPS - 1) + k, ROWS), :]
        gm = _gate_math(xc, wa_ref, wx_ref, vec)
        xc_ref[...] = xc
        a_scr[...] = gm["a"]
        u_scr[...] = gm["mult"] * (gm["ig"] * xc)

        def group(gi, before):
            rows = pl.ds(pl.multiple_of(gi * SUBLANES, SUBLANES), SUBLANES)
            ca, cb = a_scr[rows, :], u_scr[rows, :]
            for sh in (1, 2, 4):
                cb = ca * _shift_rows(cb, sh, 0.0) + cb
                ca = ca * _shift_rows(ca, sh, 1.0)
            h = cb + ca * before
            hp_ref[rows, :] = jnp.where(lax.broadcasted_iota(jnp.int32, h.shape, 0) == 0, before, pltpu.roll(h, 1, axis=0))
            h2_ref[rows, :] = h.astype(BF16)
            y_ref[rows, :] = (h * _gelu(g_ref[rows, :])).astype(BF16)
            return jnp.broadcast_to(h[SUBLANES - 1:SUBLANES, :], h.shape)

        carry[...] = lax.fori_loop(0, ROWS // SUBLANES, group, carry[...])

    wspec = pl.BlockSpec(wa.shape, lambda i: (0, 0, 0))
    return pl.pallas_call(
        body, out_shape=[jax.ShapeDtypeStruct((s, d), F32)] * 2 + [jax.ShapeDtypeStruct((s, d), BF16)] * 2, grid=(s // ROWS,),
        in_specs=[pl.BlockSpec((ROWS, d), lambda i: (i, SEG_X)),
                  pl.BlockSpec((SUBLANES, d), lambda i: (jnp.maximum(i * hb - 1, 0), SEG_X)),
                  pl.BlockSpec((ROWS, d), lambda i: (i, SEG_GATE)),
                  _vec_spec(d, SUBLANES), _vec_spec(d, SUBLANES), wspec, wspec],
        out_specs=[_row_spec(d)] * 4,
        scratch_shapes=[pltpu.VMEM((ROWS + SUBLANES, d), F32), pltpu.VMEM((ROWS, d), F32), pltpu.VMEM((ROWS, d), F32), pltpu.VMEM((SUBLANES, d), F32)],
        compiler_params=_params(("arbitrary",)), name="lru_fwd")(proj, proj, proj, cw8, vec8, wa, wx)


def _lru_bwd(hp, dh, xc, wa, wx, vec8):
    s, d = xc.shape
    nh = d // HEAD
    nb = s // ROWS

    def body(hp_ref, dh_ref, xc_ref, wa_ref, wx_ref, vec_ref, dxc_ref, dwa_ref, dwx_ref, dvec_ref, a_scr, g_scr, da_scr, carry):
        @pl.when(pl.program_id(0) == 0)
        def _():
            carry[...] = jnp.zeros_like(carry)
            dwa_ref[...] = jnp.zeros_like(dwa_ref)
            dwx_ref[...] = jnp.zeros_like(dwx_ref)
            dvec_ref[...] = jnp.zeros_like(dvec_ref)

        xc_v, vec = xc_ref[...], vec_ref[...]
        gm = _gate_math(xc_v, wa_ref, wx_ref, vec)
        ra, ig, sp, a, mult = gm["ra"], gm["ig"], gm["sp"], gm["a"], gm["mult"]
        a_scr[...] = a

        def group(j, after):
            rows = pl.ds(pl.multiple_of((ROWS // SUBLANES - 1 - j) * SUBLANES, SUBLANES), SUBLANES)
            ca, dhv = a_scr[rows, :], dh_ref[rows, :]
            cb = ca * dhv
            for sh in (1, 2, 4):
                cb = ca * _shift_rows(cb, sh, 0.0, up=True) + cb
                ca = ca * _shift_rows(ca, sh, 1.0, up=True)
            c = cb + ca * after
            last = lax.broadcasted_iota(jnp.int32, c.shape, 0) == SUBLANES - 1
            g = dhv + jnp.where(last, after, pltpu.roll(c, SUBLANES - 1, axis=0))
            g_scr[rows, :] = g
            da_scr[rows, :] = g * hp_ref[rows, :]
            return jnp.broadcast_to(c[0:1, :], c.shape)

        carry[...] = lax.fori_loop(0, ROWS // SUBLANES, group, carry[...])
        du, da = g_scr[...], da_scr[...]
        dmult = du * ig * xc_v
        dlog_a = da * a - dmult * (a * a) / mult
        dpre_a = dlog_a * (-LRU_C * sp) * ra * (1.0 - ra)
        dpre_x = du * mult * xc_v * ig * (1.0 - ig)
        dlam = jnp.sum(dlog_a * (-LRU_C * ra), axis=0, keepdims=True) * (-_sigmoid(-vec[VEC_LAM:VEC_LAM + 1]))
        dvec_ref[VEC_BA:VEC_BA + 1, :] += jnp.sum(dpre_a, axis=0, keepdims=True)
        dvec_ref[VEC_BX:VEC_BX + 1, :] += jnp.sum(dpre_x, axis=0, keepdims=True)
        dvec_ref[VEC_LAM:VEC_LAM + 1, :] += dlam
        dab, dxb, xcb = dpre_a.astype(BF16), dpre_x.astype(BF16), gm["xcb"]
        back = []
        for h in range(nh):
            cols = slice(h * HEAD, (h + 1) * HEAD)
            dwa_ref[h] += lax.dot_general(xcb[:, cols], dab[:, cols], TN, preferred_element_type=F32)
            dwx_ref[h] += lax.dot_general(xcb[:, cols], dxb[:, cols], TN, preferred_element_type=F32)
            back.append(lax.dot_general(dab[:, cols], wa_ref[h], NT, preferred_element_type=F32)
                        + lax.dot_general(dxb[:, cols], wx_ref[h], NT, preferred_element_type=F32))
        dxc_ref[...] = du * mult * ig + jnp.concatenate(back, axis=1)

    rows_rev = pl.BlockSpec((ROWS, d), lambda i: (nb - 1 - i, 0))
    wspec = pl.BlockSpec(wa.shape, lambda i: (0, 0, 0))
    return pl.pallas_call(
        body, out_shape=[jax.ShapeDtypeStruct((s, d), F32), jax.ShapeDtypeStruct(wa.shape, F32), jax.ShapeDtypeStruct(wa.shape, F32),
                         jax.ShapeDtypeStruct((SUBLANES, d), F32)],
        grid=(nb,), in_specs=[rows_rev, rows_rev, rows_rev, wspec, wspec, _vec_spec(d, SUBLANES)],
        out_specs=[rows_rev, wspec, wspec, _vec_spec(d, SUBLANES)],
        scratch_shapes=[pltpu.VMEM((ROWS, d), F32)] * 3 + [pltpu.VMEM((SUBLANES, d), F32)],
        compiler_params=_params(("arbitrary",)), name="lru_bwd")(hp, dh, xc, wa, wx, vec8)


def _conv_bwd(dxc, proj, cw8):
    s, d = dxc.shape
    hb = ROWS // SUBLANES
    last = s // SUBLANES - 1

    def body(dc_ref, dnext_ref, x_ref, xprev_ref, cw_ref, dx_ref, dcw_ref, padd, padx):
        i = pl.program_id(0)

        @pl.when(i == 0)
        def _():
            dcw_ref[...] = jnp.zeros_like(dcw_ref)

        dc = dc_ref[...]
        padd[0:ROWS, :] = dc
        padd[ROWS:ROWS + SUBLANES, :] = jnp.where(i == pl.num_programs(0) - 1, 0.0, dnext_ref[...])
        _conv_pad_prev(padx, x_ref[...], xprev_ref[...], i == 0)
        dx = jnp.zeros_like(dc)
        for k in range(CONV_TAPS):
            dx = dx + cw_ref[k:k + 1, :] * padd[pl.ds(CONV_TAPS - 1 - k, ROWS), :]
            dcw_ref[k:k + 1, :] += jnp.sum(dc * padx[pl.ds(SUBLANES - (CONV_TAPS - 1) + k, ROWS), :], axis=0, keepdims=True)
        dcw_ref[CONV_TAPS:CONV_TAPS + 1, :] += jnp.sum(dc, axis=0, keepdims=True)
        dx_ref[...] = dx.astype(BF16)

    return pl.pallas_call(
        body, out_shape=[jax.ShapeDtypeStruct((s, d), BF16), jax.ShapeDtypeStruct((SUBLANES, d), F32)], grid=(s // ROWS,),
        in_specs=[_row_spec(d), pl.BlockSpec((SUBLANES, d), lambda i: (jnp.minimum((i + 1) * hb, last), 0)),
                  pl.BlockSpec((ROWS, d), lambda i: (i, SEG_X)),
                  pl.BlockSpec((SUBLANES, d), lambda i: (jnp.maximum(i * hb - 1, 0), SEG_X)), _vec_spec(d, SUBLANES)],
        out_specs=[_row_spec(d), _vec_spec(d, SUBLANES)],
        scratch_shapes=[pltpu.VMEM((ROWS + SUBLANES, d), F32), pltpu.VMEM((ROWS + SUBLANES, d), F32)],
        compiler_params=_params(("arbitrary",)), name="lru_conv_bwd")(dxc, dxc, proj, proj, cw8)


def _coords():
    return lax.axis_index("x"), lax.axis_index("y"), lax.axis_index("c")


def _other_chips(x, y):
    return [(1 - x, y), (x, 1 - y), (1 - x, 1 - y)]


def _slab(ref, kind, shard_shape, idx, half=None):
    r, c = shard_shape
    r0, nr = (0, r) if half is None else (half * (r // 2), r // 2)
    if kind == "col":
        return ref.at[pl.ds(r0, nr), pl.ds(pl.multiple_of(idx * c, LANES), c)]
    if kind == "row":
        return ref.at[pl.ds(pl.multiple_of(idx * r, SUBLANES) + r0, nr), :]
    return ref.at[idx, pl.ds(r0, nr), :]


def _full_shape(shard_shape, kind):
    r, c = shard_shape
    return {"col": (r, c * N_DEV), "row": (r * N_DEV, c), "slot": (N_DEV, r, c)}[kind]


def _handshake(peers):
    barrier = pltpu.get_barrier_semaphore()
    for peer in peers:
        pl.semaphore_signal(barrier, inc=1, device_id=peer, device_id_type=MESH)
    pl.semaphore_wait(barrier, len(peers))


def _launch(name, body, out_shape, operands, sems, sequencer_id):
    if sequencer_id is None:
        return pl.pallas_call(body, out_shape=out_shape, in_specs=[HBM] * len(operands), out_specs=[HBM] * len(out_shape),
                              scratch_shapes=sems, name=name)(*operands)
    return pl.kernel(body, out_type=out_shape, mesh=plsc.ScalarSubcoreMesh(axis_name="seq", num_cores=1), name=name,
                     scratch_types=sems, compiler_params=pltpu.CompilerParams(collective_id=sequencer_id))(*operands)


AG_COPIES = 10


def _all_gather(name, shards, kinds, sequencer_id=None):
    n = len(shards)
    shapes = [s.shape for s in shards]

    def body(*refs):
        ins, outs = refs[:n], refs[n:2 * n]
        send_sems, recv_sems, local_sems = refs[2 * n:]
        x, y, c = _coords()
        me, sib, xn, yn, dg = (x, y, c), (x, y, 1 - c), (1 - x, y, c), (x, 1 - y, c), (1 - x, 1 - y, c)
        if sequencer_id is not None:
            _handshake([sib, xn, yn])

        def part(i, dev, half=None):
            return _slab(outs[i], kinds[i], shapes[i], 4 * dev[0] + 2 * dev[1] + dev[2], half)

        def copy(i, k, block, half, to, own=False):
            r = shapes[i][0]
            src = part(i, block, half) if not own else (ins[i] if half is None else ins[i].at[pl.ds(half * (r // 2), r // 2), :])
            return pltpu.make_async_remote_copy(
                src_ref=src, dst_ref=part(i, block, half), send_sem=send_sems.at[AG_COPIES * i + k],
                recv_sem=recv_sems.at[AG_COPIES * i + k], device_id=to, device_id_type=MESH)

        def other_core(dev):
            return (dev[0], dev[1], 1 - c)

        started = []

        def start(cp):
            cp.start()
            started.append(cp)

        for i in range(n):
            start(copy(i, 1, me, 0, xn, own=True))
            start(copy(i, 4, me, 1, yn, own=True))
            start(copy(i, 2, me, 1, xn, own=True))
            start(copy(i, 3, me, 0, yn, own=True))
            start(copy(i, 0, me, None, sib, own=True))
        mine = [pltpu.make_async_copy(ins[i], part(i, me), local_sems.at[i]) for i in range(n)]
        for cp in mine:
            cp.start()
        for i in range(n):
            copy(i, 1, xn, 0, me).wait_recv()
            start(copy(i, 5, xn, 0, yn))
            copy(i, 4, yn, 1, me).wait_recv()
            start(copy(i, 6, yn, 1, xn))
        for i in range(n):
            copy(i, 2, xn, 1, me).wait_recv()
            start(copy(i, 7, xn, None, sib))
            copy(i, 3, yn, 0, me).wait_recv()
            start(copy(i, 8, yn, None, sib))
        for i in range(n):
            copy(i, 5, dg, 0, me).wait_recv()
            copy(i, 6, dg, 1, me).wait_recv()
            start(copy(i, 9, dg, None, sib))
        for i in range(n):
            copy(i, 0, sib, None, me).wait_recv()
            for k, dev in ((7, xn), (8, yn), (9, dg)):
                copy(i, k, other_core(dev), None, me).wait_recv()
        for cp in started:
            cp.wait_send()
        for cp in mine:
            cp.wait()

    out_shape = [jax.ShapeDtypeStruct(_full_shape(s.shape, k), s.dtype) for s, k in zip(shards, kinds)]
    sems = [pltpu.SemaphoreType.DMA((AG_COPIES * n,)), pltpu.SemaphoreType.DMA((AG_COPIES * n,)), pltpu.SemaphoreType.DMA((n,))]
    return _launch(name, body, out_shape, shards, sems, sequencer_id)


def _sibling_copies(kinds, shard_shapes):
    def make(ins, outs, send_sems, recv_sems):
        x, y, c = _coords()
        return [pltpu.make_async_remote_copy(
            src_ref=_slab(ins[i], kinds[i], shard_shapes[i], 2 * q + (1 - c)), dst_ref=outs[i].at[q],
            send_sem=send_sems.at[N_CHIP * i + q], recv_sem=recv_sems.at[N_CHIP * i + q],
            device_id=(x, y, 1 - c), device_id_type=MESH) for i in range(len(ins)) for q in range(N_CHIP)]
    return make


def _sibling_side(partials, kinds, shard_shapes):
    return (partials, [jax.ShapeDtypeStruct((N_CHIP, *s), BF16) for s in shard_shapes], N_CHIP * len(partials),
            _sibling_copies(kinds, shard_shapes))


def _exchange_siblings(name, partials, kinds, shard_shapes, sequencer_id=None):
    n = len(partials)
    make = _sibling_copies(kinds, shard_shapes)

    def body(*refs):
        if sequencer_id is not None:
            x, y, c = _coords()
            _handshake([(x, y, 1 - c)])
        cps = make(refs[:n], refs[n:2 * n], refs[2 * n], refs[2 * n + 1])
        for cp in cps:
            cp.start()
        for cp in cps:
            cp.wait()

    return _launch(name, body, [jax.ShapeDtypeStruct((N_CHIP, *s), BF16) for s in shard_shapes], partials,
                   [pltpu.SemaphoreType.DMA((N_CHIP * n,)), pltpu.SemaphoreType.DMA((N_CHIP * n,))], sequencer_id)


def _exchange_chips(name, chip_sums, sequencer_id=None):
    n = len(chip_sums)

    def body(*refs):
        ins, outs = refs[:n], refs[n:2 * n]
        send_sems, recv_sems = refs[2 * n:]
        x, y, c = _coords()
        if sequencer_id is not None:
            _handshake([(cx, cy, c) for cx, cy in _other_chips(x, y)])
        cps = []
        for i in range(n):
            for k, (cx, cy) in enumerate(_other_chips(x, y)):
                cps.append(pltpu.make_async_remote_copy(
                    src_ref=ins[i].at[2 * cx + cy], dst_ref=outs[i].at[k], send_sem=send_sems.at[3 * i + k],
                    recv_sem=recv_sems.at[3 * i + k], device_id=(cx, cy, c), device_id_type=MESH))
        for cp in cps:
            cp.start()
        for cp in cps:
            cp.wait()

    return _launch(name, body, [jax.ShapeDtypeStruct((3, *t.shape[1:]), BF16) for t in chip_sums], chip_sums,
                   [pltpu.SemaphoreType.DMA((3 * n,)), pltpu.SemaphoreType.DMA((3 * n,))], sequencer_id)


def _all_reduce_small(name, packed):
    rows = packed.shape[0] // N_DEV

    def body(p_ref, out_ref, rb, tot, send_sems, recv_sems):
        x, y, c = _coords()
        me = 4 * x + 2 * y + c

        def peer(k):
            return (x ^ (k >> 2), y ^ ((k >> 1) & 1), c ^ (k & 1))

        def rows_of(idx):
            return pl.ds(pl.multiple_of(idx * rows, SUBLANES), rows)

        def piece(ref, idx):
            return ref.at[rows_of(idx), :]

        scatter = [pltpu.make_async_remote_copy(src_ref=piece(p_ref, me ^ k), dst_ref=rb.at[k], send_sem=send_sems.at[k],
                                                recv_sem=recv_sems.at[k], device_id=peer(k), device_id_type=MESH) for k in range(1, N_DEV)]
        for cp in scatter:
            cp.start()
        acc = p_ref[rows_of(me), :]
        for cp in scatter:
            cp.wait_recv()
        for k in range(1, N_DEV):
            acc = acc + rb[k]
        tot[...] = acc
        out_ref[rows_of(me), :] = acc
        gather = [pltpu.make_async_remote_copy(src_ref=tot, dst_ref=piece(out_ref, me), send_sem=send_sems.at[N_DEV + k],
                                               recv_sem=recv_sems.at[N_DEV + k], device_id=peer(k), device_id_type=MESH)
                  for k in range(1, N_DEV)]
        for cp in gather:
            cp.start()
        for k in range(1, N_DEV):
            pltpu.make_async_remote_copy(src_ref=tot, dst_ref=piece(out_ref, me ^ k), send_sem=send_sems.at[N_DEV + k],
                                         recv_sem=recv_sems.at[N_DEV + k], device_id=peer(k), device_id_type=MESH).wait_recv()
        for cp in scatter + gather:
            cp.wait_send()

    vm = pl.BlockSpec(memory_space=pltpu.VMEM)
    return pl.pallas_call(
        body, out_shape=jax.ShapeDtypeStruct(packed.shape, F32), in_specs=[vm], out_specs=vm,
        scratch_shapes=[pltpu.VMEM((N_DEV, rows, LANES), F32), pltpu.VMEM((rows, LANES), F32),
                        pltpu.SemaphoreType.DMA((2 * N_DEV,)), pltpu.SemaphoreType.DMA((2 * N_DEV,))],
        compiler_params=pltpu.CompilerParams(vmem_limit_bytes=VMEM_LIMIT), name=name)(packed)


def _adamw_math(g, w, m, v):
    m = ADAM_B1 * m + (1.0 - ADAM_B1) * g
    v = ADAM_B2 * v + (1.0 - ADAM_B2) * (g * g)
    delta = -ADAM_LR * ((m / ADAM_C1) / (jnp.sqrt(v / ADAM_C2) + ADAM_EPS) + ADAM_WD * w)
    return delta, m, v


def _slab_spec(kind, shard_shape, tr, slab_of):
    r, c = shard_shape
    if kind == "col":
        return pl.BlockSpec((tr, c), lambda q, i, sc: (i, slab_of(q, sc)))
    return pl.BlockSpec((tr, c), lambda q, i, sc: (slab_of(q, sc) * (r // tr) + i, 0))


def _chip_sum(name, partial, recv, kind, shard_shape, core):
    r, c = shard_shape
    tr = _blk(r, 1024)

    def body(core_ref, p_ref, r_ref, o_ref):
        o_ref[...] = (p_ref[...].astype(F32) + r_ref[...].astype(F32)).astype(BF16)

    spec4 = pl.BlockSpec((None, tr, c), lambda q, i, sc: (q, i, 0))
    grid_spec = pltpu.PrefetchScalarGridSpec(
        num_scalar_prefetch=1, grid=(N_CHIP, r // tr),
        in_specs=[_slab_spec(kind, shard_shape, tr, lambda q, sc: 2 * q + sc[0]), spec4], out_specs=spec4)
    return pl.pallas_call(body, out_shape=jax.ShapeDtypeStruct((N_CHIP, r, c), BF16), grid_spec=grid_spec,
                          compiler_params=_params(("parallel", "parallel")), name=name)(core, partial, recv)


def _adamw_shard(name, parts, w, m, v, chip):
    r, c = w.shape
    n_parts = len(parts)
    tr = _blk(r // n_parts, 256)
    per = r // n_parts // tr

    def body(chip_ref, *refs):
        src, (w_ref, m_ref, v_ref), (g_out, d_out, m_out, v_out) = refs[:2 * n_parts], refs[2 * n_parts:2 * n_parts + 3], refs[2 * n_parts + 3:]
        for p in range(n_parts):
            @pl.when(pl.program_id(0) // per == p)
            def _():
                g = src[2 * p][...].astype(F32)
                for k in range(3):
                    g = g + src[2 * p + 1][k].astype(F32)
                g_out[...] = g
                d_out[...], m_out[...], v_out[...] = _adamw_math(g, w_ref[...], m_ref[...], v_ref[...])

    def part_specs(p):
        at = lambda i: jnp.clip(i - p * per, 0, per - 1)
        return [pl.BlockSpec((None, tr, c), lambda i, sc: (sc[0], at(i), 0)), pl.BlockSpec((3, tr, c), lambda i, sc: (0, at(i), 0))]

    blk = pl.BlockSpec((tr, c), lambda i, sc: (i, 0))
    grid_spec = pltpu.PrefetchScalarGridSpec(
        num_scalar_prefetch=1, grid=(r // tr,), in_specs=[s for p in range(n_parts) for s in part_specs(p)] + [blk, blk, blk], out_specs=[blk] * 4)
    return pl.pallas_call(body, out_shape=[jax.ShapeDtypeStruct((r, c), F32)] * 4, grid_spec=grid_spec,
                          compiler_params=_params(("parallel",)), name=name)(chip, *[a for p in parts for a in p], w, m, v)


def _adamw_small(name, g, w, m, v):
    def body(g_ref, w_ref, m_ref, v_ref, d_out, m_out, v_out):
        d_out[...], m_out[...], v_out[...] = _adamw_math(g_ref[...], w_ref[...], m_ref[...], v_ref[...])

    vm = pl.BlockSpec(memory_space=pltpu.VMEM)
    return pl.pallas_call(body, out_shape=[jax.ShapeDtypeStruct(g.shape, F32)] * 3, in_specs=[vm] * 4, out_specs=[vm] * 3,
                          compiler_params=pltpu.CompilerParams(vmem_limit_bytes=VMEM_LIMIT), name=name)(g, w, m, v)


def _pack_rows(arrays, total_rows):
    flat = [a.reshape(-1, LANES) for a in arrays]
    used = sum(f.shape[0] for f in flat)
    return jnp.concatenate(flat + [jnp.zeros((total_rows - used, LANES), F32)], axis=0)


def _unpack_rows(packed, like):
    out, at = [], 0
    for a in like:
        n = a.size // LANES
        out.append(packed[at:at + n].reshape(a.shape))
        at += n
    return out


def kernel(x, norm_mix_g, w_in, conv_w, conv_b, lru_wa, lru_ba, lru_wx, lru_bx, lru_lambda, w_proj_attn, w_proj_lru, w_out, norm_mlp_g, w_up, w_down, norm_final_g, loss_target, m_norm_mix_g, m_w_in, m_conv_w, m_conv_b, m_lru_wa, m_lru_ba, m_lru_wx, m_lru_bx, m_lru_lambda, m_w_proj_attn, m_w_proj_lru, m_w_out, m_norm_mlp_g, m_w_up, m_w_down, m_norm_final_g, v_norm_mix_g, v_w_in, v_conv_w, v_conv_b, v_lru_wa, v_lru_ba, v_lru_wx, v_lru_bx, v_lru_lambda, v_w_proj_attn, v_w_proj_lru, v_w_out, v_norm_mlp_g, v_w_up, v_w_down, v_norm_final_g):
    xs, tgt = x[0], loss_target[0]
    s, d = xs.shape
    nh = d // HEAD
    ix, iy, ic = _coords()
    core = jnp.reshape(ic, (1,)).astype(jnp.int32)
    chip = jnp.reshape(2 * ix + iy, (1,)).astype(jnp.int32)
    dev = 4 * ix + 2 * iy + ic

    big = [w_in[0], w_proj_attn[0], w_proj_lru[0], w_out[0], w_up[0], w_down[0]]
    big_m = [m_w_in[0], m_w_proj_attn[0], m_w_proj_lru[0], m_w_out[0], m_w_up[0], m_w_down[0]]
    big_v = [v_w_in[0], v_w_proj_attn[0], v_w_proj_lru[0], v_w_out[0], v_w_up[0], v_w_down[0]]
    kinds = ["col", "row", "row", "row", "col", "row"]
    pad_taps = lambda t: jnp.pad(t, ((0, SUBLANES - CONV_TAPS), (0, 0)))
    shards = [w.astype(BF16) for w in big]
    pad_taps2 = lambda t: jnp.pad(t, ((0, 2 * SUBLANES - CONV_TAPS), (0, 0)))
    win, cw_slots = _all_gather("all_gather_w_in", [shards[0], pad_taps2(conv_w[0])], ["col", "slot"])
    later = lax.optimization_barrier((shards[1:], win))[0]
    wpa, wpl, wout = _all_gather("all_gather_mix", later[:3], kinds[1:4], sequencer_id=1)
    wup, wdown = _all_gather("all_gather_mlp", later[3:], kinds[4:], sequencer_id=5)
    cw8 = jnp.transpose(cw_slots[:, :SUBLANES], (1, 0, 2)).reshape(SUBLANES, d)
    row_id = lax.broadcasted_iota(jnp.int32, (SUBLANES, d), 0)
    vec8 = sum(jnp.where(row_id == k, t, 0.0) for k, t in ((VEC_CB, conv_b), (VEC_BA, lru_ba), (VEC_BX, lru_bx), (VEC_LAM, lru_lambda)))
    wa16, wx16 = lru_wa[0].astype(BF16), lru_wx[0].astype(BF16)
    slopes = 2.0 ** (-8.0 * jnp.arange(1, nh + 1, dtype=F32) / nh)

    def seg_specs(*segs):
        return lambda bm, bn: [pl.BlockSpec((bm, bn), (lambda i, j, kk, sg=sg: (i, sg * (d // bn) + j))) for sg in segs]

    def plain_specs(k):
        return lambda bm, bn: [pl.BlockSpec((bm, bn), lambda i, j, kk: (i, j)) for _ in range(k)]

    xn = _rms_fwd("norm_mix", xs, norm_mix_g)
    proj = _mm_fwd("proj_in", xn, win, 0, 7 * d, [F32], bm=2048)[0]
    att, lse = _attn_fwd(proj, d, slopes)
    xc, hp, h2d, ylru = _lru_fwd(proj, d, cw8, vec8, wa16, wx16)
    pa = _mm_fwd("proj_attn", att, wpa, 0, d, [BF16], bm=2048)[0]

    def merge(acc, pa_b, ga, gl):
        return acc, _sigmoid(ga) * pa_b.astype(F32) + _sigmoid(gl) * acc

    plr, merged = _mm_fwd("proj_lru_merge", ylru, wpl, 0, d, [BF16, BF16], merge, (pa, proj, proj),
                          lambda bm, bn: plain_specs(1)(bm, bn) + seg_specs(SEG_GA, SEG_GL)(bm, bn), bn=512)
    h1 = _mm_fwd("mix_out", merged, wout, 0, d, [F32], lambda acc, r: (acc + r,), (xs,), plain_specs(1))[0]
    hn = _rms_fwd("norm_mlp", h1, norm_mlp_g)

    def relu2(acc):
        return acc, jnp.square(jnp.maximum(acc, 0.0))

    up, hid = _mm_fwd("mlp_up", hn, wup, 0, wup.shape[1], [BF16, BF16], relu2, bm=2048)
    h2 = _mm_fwd("mlp_down", hid, wdown, 0, d, [F32], lambda acc, r: (acc + r,), (h1,), plain_specs(1))[0]
    dh2, dh2b, dg3, loss_lanes = _final_loss(h2, tgt, norm_final_g.reshape(1, d))
    loss = lax.psum(0.5 / d * jnp.sum(loss_lanes), ("x", "y", "c"))
    dh2b = lax.optimization_barrier((dh2b, loss))[0]

    def reduce_group(tag, kk, shp, partials, from_sibling, sequencer_id):
        sums = [_chip_sum(f"chip_sum_{tag}_{i}", p, f, k, sh, core) for i, (p, f, k, sh) in enumerate(zip(partials, from_sibling, kk, shp))]
        return list(zip(sums, _exchange_chips(f"rs_chips_{tag}", sums, sequencer_id)))

    dup = _mm_nt("mlp_down_dx", dh2b, wdown, [BF16], lambda acc, u: (acc * (2.0 * jnp.maximum(u.astype(F32), 0.0)),), (up,), plain_specs(1))[0]
    g_wdown = _mm_tn("mlp_down_dw", hid, dh2b)
    g_wup = _mm_tn("mlp_up_dw", hn, dup)
    shp_mlp = [w.shape for w in big[4:]]
    (dhn,), sib_mlp = _mm_nt("mlp_up_dx", dup, wup, [F32], side=_sibling_side([g_wup, g_wdown], kinds[4:], shp_mlp))
    red_up, red_down = reduce_group("mlp", kinds[4:], shp_mlp, [g_wup, g_wdown], sib_mlp, 2)
    dhn = lax.optimization_barrier((dhn, red_up[0], red_down[0]))[0]
    dh1, dh1b, dg2 = _rms_bwd("norm_mlp_bwd", h1, norm_mlp_g, dhn, dh2)

    def merge_bwd(acc, pa_b, pl_b, ga, gl):
        sa, sl = _sigmoid(ga), _sigmoid(gl)
        return acc * sa, acc * sl, acc * pa_b.astype(F32) * sa * (1.0 - sa), acc * pl_b.astype(F32) * sl * (1.0 - sl)

    dpa, dpl, dga, dgl = _mm_nt("mix_out_dx", dh1b, wout, [BF16] * 4, merge_bwd, (pa, plr, proj, proj),
                                lambda bm, bn: plain_specs(2)(bm, bn) + seg_specs(SEG_GA, SEG_GL)(bm, bn), bn=512)
    g_wout = _mm_tn("mix_out_dw", merged, dh1b)
    datt = _mm_nt("proj_attn_dx", dpa, wpa, [F32], bm=2048)[0]
    g_wpa = _mm_tn("proj_attn_dw", att, dpa)

    def lru_out_bwd(acc, h_b, gate):
        return acc * _gelu(gate), acc * h_b.astype(F32) * _gelu_grad(gate)

    g_wpl = _mm_tn("proj_lru_dw", ylru, dpl)
    shp_mix = [w.shape for w in big[1:4]]
    (dh, dxg), sib_mix = _mm_nt("proj_lru_dx", dpl, wpl, [F32, BF16], lru_out_bwd, (h2d, proj),
                                lambda bm, bn: plain_specs(1)(bm, bn) + seg_specs(SEG_GATE)(bm, bn), bn=512,
                                side=_sibling_side([g_wpa, g_wpl, g_wout], kinds[1:4], shp_mix))
    red_pa, red_pl, red_out = reduce_group("mix", kinds[1:4], shp_mix, [g_wpa, g_wpl, g_wout], sib_mix, 3)
    dxc, dwa, dwx, dvec = _lru_bwd(hp, dh, xc, wa16, wx16, vec8)
    dxr, dconv = _conv_bwd(dxc, proj, cw8)
    dproj = _attn_bwd(proj, d, datt, att, lse, slopes, (dxr, dxg, dga, dgl))

    def small_step(tag, grads, ws, ms, vs, like, after):
        n_rows = sum(g.size for g in grads) // LANES
        per_dev = -(-n_rows // (N_DEV * SUBLANES)) * SUBLANES
        packed = lax.optimization_barrier((_pack_rows(grads, N_DEV * per_dev), after))[0]
        total = _all_reduce_small(f"all_reduce_{tag}", packed)
        w_rows = -(-(sum(w.size for w in ws) // LANES) // SUBLANES) * SUBLANES
        upd = _adamw_small(f"adamw_{tag}", total[:w_rows], _pack_rows(ws, w_rows), _pack_rows(ms, w_rows), _pack_rows(vs, w_rows))
        return _unpack_rows(total, like), [_unpack_rows(t, ws) for t in upd]

    early_w = [conv_b, lru_wa, lru_ba, lru_wx, lru_bx, lru_lambda, norm_mlp_g, norm_final_g]
    early_m = [m_conv_b, m_lru_wa, m_lru_ba, m_lru_wx, m_lru_bx, m_lru_lambda, m_norm_mlp_g, m_norm_final_g]
    early_v = [v_conv_b, v_lru_wa, v_lru_ba, v_lru_wx, v_lru_bx, v_lru_lambda, v_norm_mlp_g, v_norm_final_g]
    early_g = [dconv[CONV_TAPS:CONV_TAPS + 1], dwa, dvec[VEC_BA:VEC_BA + 1], dwx, dvec[VEC_BX:VEC_BX + 1],
               dvec[VEC_LAM:VEC_LAM + 1], dg2, dg3, dconv[0:CONV_TAPS]]
    dproj = lax.optimization_barrier((dproj, red_up[1], red_down[1]))[0]
    early_sum, early_upd = small_step("small", early_g, early_w, early_m, early_v,
                                      early_w + [jax.ShapeDtypeStruct((1, CONV_TAPS, d), F32)], dproj)
    g_cw_full = early_sum[-1]
    cshard = conv_w.shape[2]
    g_cw = lax.dynamic_slice(g_cw_full, (0, 0, dev * cshard), (1, CONV_TAPS, cshard))
    cw_delta, cw_m, cw_v = (t[:CONV_TAPS][None] for t in _adamw_small(
        "adamw_conv_w", pad_taps(g_cw[0]), pad_taps(conv_w[0]), pad_taps(m_conv_w[0]), pad_taps(v_conv_w[0])))
    dproj = lax.optimization_barrier((dproj, early_sum, red_pa[1], red_pl[1], red_out[1]))[0]
    half = (big[0].shape[0] // 2, big[0].shape[1])
    g_in0 = _mm_tn("proj_in_dw_0", xn, dproj, part=(0, 2))
    g_in1, sib_in0 = _mm_tn("proj_in_dw_1", xn, dproj, part=(1, 2), side=_sibling_side([g_in0], ["col"], [half]))
    red_in = reduce_group("in_0", ["col"], [half], [g_in0], sib_in0, 4)
    dproj = lax.optimization_barrier((dproj, red_in[0][0]))[0]
    (dxn0,), sib_in1 = _mm_nt("proj_in_dx_0", dproj, win, [F32], part=(0, 2), side=_sibling_side([g_in1], ["col"], [half]))
    red_in += reduce_group("in_1", ["col"], [half], [g_in1], sib_in1, 6)
    dproj = lax.optimization_barrier((dproj, red_in[1][0]))[0]
    dxn1 = _mm_nt("proj_in_dx_1", dproj, win, [F32], part=(1, 2))[0]
    dxn = jnp.concatenate([dxn0, dxn1], axis=0)
    dxn = lax.optimization_barrier((dxn, red_in[0][1]))[0]
    grad_x, _, dg1 = _rms_bwd("norm_mix_bwd", xs, norm_mix_g, dxn, dh1)
    red_up, red_down = lax.optimization_barrier(((red_up, red_down), dg1))[0]
    big_out = {i: _adamw_shard(f"adamw_{i}", [red], big[i], big_m[i], big_v[i], chip) for i, red in ((4, red_up), (5, red_down))}
    big_out.update({i: _adamw_shard(f"adamw_{i}", [red], big[i], big_m[i], big_v[i], chip) for i, red in ((1, red_pa), (2, red_pl), (3, red_out))})
    late_sum, late_upd = small_step("norm_mix", [dg1], [norm_mix_g], [m_norm_mix_g], [v_norm_mix_g], [norm_mix_g], (big_out[4], big_out[5]))
    big_out[0] = _adamw_shard("adamw_0", red_in, big[0], big_m[0], big_v[0], chip)
    s_grad = late_sum + early_sum[:-1]
    s_delta, s_m, s_v = (late_upd[j] + early_upd[j] for j in range(3))


    names = ["norm_mix_g", "w_in", "conv_w", "conv_b", "lru_wa", "lru_ba", "lru_wx", "lru_bx", "lru_lambda", "w_proj_attn", "w_proj_lru",
             "w_out", "norm_mlp_g", "w_up", "w_down", "norm_final_g"]
    small_names = ["norm_mix_g", "conv_b", "lru_wa", "lru_ba", "lru_wx", "lru_bx", "lru_lambda", "norm_mlp_g", "norm_final_g"]
    big_names = ["w_in", "w_proj_attn", "w_proj_lru", "w_out", "w_up", "w_down"]
    res = {"conv_w": (g_cw, cw_delta, cw_m, cw_v)}
    for i, nm in enumerate(small_names):
        res[nm] = (s_grad[i], s_delta[i], s_m[i], s_v[i])
    for i, nm in enumerate(big_names):
        res[nm] = tuple(t[None] for t in big_out[i])
    return (loss, grad_x[None], *[res[nm][0] for nm in names], *[res[nm][1] for nm in names],
            *[res[nm][2] for nm in names], *[res[nm][3] for nm in names])
```

```python
import jax
import jax.numpy as jnp
from jax import lax
from jax.experimental import pallas as pl
from jax.experimental.pallas import tpu as pltpu
from jax.experimental.pallas import tpu_sc as plsc

F32, BF16 = jnp.float32, jnp.bfloat16
MESH = pl.DeviceIdType.MESH
HBM = pl.BlockSpec(memory_space=pltpu.HBM)
N_DEV = 8
N_CHIP = 4
HEAD = 128
SPAN = 128
DILATIONS = (1, 4, 16)
CONV_TAPS = 4
LRU_C = 8.0
NORM_EPS = 1e-6
LANES = 128
SUBLANES = 8
VMEM_LIMIT = 56 * 1024 * 1024
ADAM_LR, ADAM_B1, ADAM_B2, ADAM_EPS, ADAM_WD, ADAM_STEP = 0.001, 0.9, 0.999, 1e-08, 0.01, 10
ADAM_C1 = 1.0 - ADAM_B1 ** ADAM_STEP
ADAM_C2 = 1.0 - ADAM_B2 ** ADAM_STEP
NEG = -1e30


def _params(sem=None):
    return pltpu.CompilerParams(dimension_semantics=sem, vmem_limit_bytes=VMEM_LIMIT)


def _sigmoid(v):
    return 1.0 / (1.0 + jnp.exp(-v))


def _gelu(v):
    k = 0.7978845608028654
    return 0.5 * v * (1.0 + jnp.tanh(k * (v + 0.044715 * v * v * v)))


def _gelu_grad(v):
    k = 0.7978845608028654
    t = jnp.tanh(k * (v + 0.044715 * v * v * v))
    return 0.5 * (1.0 + t) + 0.5 * v * (1.0 - t * t) * k * (1.0 + 3.0 * 0.044715 * v * v)


NN = (((1,), (0,)), ((), ()))
NT = (((1,), (1,)), ((), ()))
TN = (((0,), (0,)), ((), ()))


def _mm(name, a, a_spec, b, b_spec, dn, grid, out_shapes, out_specs, acc_block, epilogue=None, extras=(), extra_specs=(), side=None):
    nk, ne, no = grid[2], len(extras), len(out_shapes)
    side_ops, side_shapes, side_copies, make_copies = side if side is not None else ((), (), 0, None)
    ns_in, ns_out = len(side_ops), len(side_shapes)

    def body(*refs):
        a_ref, b_ref = refs[0], refs[1]
        ex, side_in = refs[2:2 + ne], refs[2 + ne:2 + ne + ns_in]
        outs = refs[2 + ne + ns_in:2 + ne + ns_in + no]
        side_out = refs[2 + ne + ns_in + no:2 + ne + ns_in + no + ns_out]
        scratch = refs[2 + ne + ns_in + no + ns_out:]
        at = [pl.program_id(ax) for ax in range(3)]
        if side is not None:
            @pl.when((at[0] == 0) & (at[1] == 0) & (at[2] == 0))
            def _():
                for cp in make_copies(side_in, side_out, scratch[-2], scratch[-1]):
                    cp.start()

        part = lax.dot_general(a_ref[...], b_ref[...], dn, preferred_element_type=F32)

        def finish(acc):
            vals = epilogue(acc, *[e[...] for e in ex]) if epilogue is not None else (acc,)
            for o, v in zip(outs, vals):
                o[...] = v.astype(o.dtype)

        if nk == 1:
            finish(part)
        else:
            acc_ref, k = scratch[0], at[2]

            @pl.when(k == 0)
            def _():
                acc_ref[...] = part

            @pl.when(k > 0)
            def _():
                acc_ref[...] += part

            @pl.when(k == nk - 1)
            def _():
                finish(acc_ref[...])

        if side is not None:
            @pl.when((at[0] == grid[0] - 1) & (at[1] == grid[1] - 1) & (at[2] == grid[2] - 1))
            def _():
                for cp in make_copies(side_in, side_out, scratch[-2], scratch[-1]):
                    cp.wait()

    scratch_shapes = [pltpu.VMEM(acc_block, F32)] if nk > 1 else []
    if side is not None:
        scratch_shapes += [pltpu.SemaphoreType.DMA((side_copies,)), pltpu.SemaphoreType.DMA((side_copies,))]
    res = pl.pallas_call(
        body, out_shape=[*out_shapes, *side_shapes], grid=grid, in_specs=[a_spec, b_spec, *extra_specs, *[HBM] * ns_in],
        out_specs=[*out_specs, *[HBM] * ns_out], scratch_shapes=scratch_shapes,
        compiler_params=_params(("arbitrary",) * 3 if side is not None else ("parallel", "parallel", "arbitrary")),
        name=name)(a, b, *extras, *side_ops)
    return res if side is None else (res[:no], res[no:])


def _blk(n, pref):
    return pref if n % pref == 0 else n


def _kblk(k):
    return k if k <= 2048 else next(b for b in (2048, 1024, 512) if k % b == 0)


def _mm_fwd(name, a, w, col0, ncols, out_dtypes, epilogue=None, extras=(), extra_specs_fn=None, seg_out=None, bm=1024, bn=1024):
    m, k = a.shape
    bm, bn = _blk(m, bm), _blk(ncols, bn)
    bk = _kblk(k)
    nk = k // bk
    cb0 = col0 // bn
    grid = (m // bm, ncols // bn, nk)
    a_spec = pl.BlockSpec((bm, bk), lambda i, j, kk: (i, kk))
    b_spec = pl.BlockSpec((bk, bn), lambda i, j, kk: (kk, cb0 + j))
    if seg_out is None:
        shapes = [jax.ShapeDtypeStruct((m, ncols), dt) for dt in out_dtypes]
        specs = [pl.BlockSpec((bm, bn), lambda i, j, kk: (i, j)) for _ in out_dtypes]
    else:
        per = seg_out // bn
        shapes = [jax.ShapeDtypeStruct((ncols // seg_out, m, seg_out), dt) for dt in out_dtypes]
        specs = [pl.BlockSpec((None, bm, bn), lambda i, j, kk: (j // per, i, j % per)) for _ in out_dtypes]
    ex_specs = extra_specs_fn(bm, bn) if extra_specs_fn else ()
    return _mm(name, a, a_spec, w, b_spec, NN, grid, shapes, specs, (bm, bn), epilogue, extras, ex_specs)


def _mm_nt(name, a, w, out_dtypes, epilogue=None, extras=(), extra_specs_fn=None, part=(0, 1), side=None, bm=1024, bn=1024):
    n = w.shape[0]
    if a.ndim == 3:
        seg_cols, m, k = a.shape[2], a.shape[1], a.shape[0] * a.shape[2]
    else:
        m, k = a.shape
    m = m // part[1]
    bm, bn = _blk(m, bm), _blk(n, bn)
    bk = _kblk(k)
    grid = (m // bm, n // bn, k // bk)
    i0 = part[0] * (m // bm)
    if a.ndim == 3:
        per = seg_cols // bk
        a_spec = pl.BlockSpec((None, bm, bk), lambda i, j, kk: (kk // per, i0 + i, kk % per))
    else:
        a_spec = pl.BlockSpec((bm, bk), lambda i, j, kk: (i0 + i, kk))
    b_spec = pl.BlockSpec((bn, bk), lambda i, j, kk: (j, kk))
    shapes = [jax.ShapeDtypeStruct((m, n), dt) for dt in out_dtypes]
    specs = [pl.BlockSpec((bm, bn), lambda i, j, kk: (i, j)) for _ in out_dtypes]
    ex_specs = extra_specs_fn(bm, bn) if extra_specs_fn else ()
    return _mm(name, a, a_spec, w, b_spec, NT, grid, shapes, specs, (bm, bn), epilogue, extras, ex_specs, side)


def _mm_tn(name, a, b, part=(0, 1), side=None, bm=1024, bn=2048):
    t, m = a.shape
    n = b.shape[1] if b.ndim == 2 else b.shape[0] * b.shape[2]
    m = m // part[1]
    bm, bn = _blk(m, bm), _blk(n, bn)
    grid = (m // bm, n // bn, 1)
    i0 = part[0] * (m // bm)
    a_spec = pl.BlockSpec((t, bm), lambda i, j, kk: (0, i0 + i))
    if b.ndim == 3:
        per = b.shape[2] // bn
        b_spec = pl.BlockSpec((None, t, bn), lambda i, j, kk: (j // per, 0, j % per))
    else:
        b_spec = pl.BlockSpec((t, bn), lambda i, j, kk: (0, j))
    res = _mm(name, a, a_spec, b, b_spec, TN, grid, [jax.ShapeDtypeStruct((m, n), BF16)],
              [pl.BlockSpec((bm, bn), lambda i, j, kk: (i, j))], (bm, bn), side=side)
    return res[0] if side is None else (res[0][0], res[1])


ROWS = 256


def _row_spec(d):
    return pl.BlockSpec((ROWS, d), lambda i: (i, 0))


def _vec_spec(d, rows=1):
    return pl.BlockSpec((rows, d), lambda i: (0, 0))


def _rms_fwd(name, x, g):
    s, d = x.shape

    def body(x_ref, g_ref, o_ref):
        xv = x_ref[...]
        r = lax.rsqrt(jnp.mean(xv * xv, axis=-1, keepdims=True) + NORM_EPS)
        o_ref[...] = (xv * r * g_ref[...]).astype(BF16)

    return pl.pallas_call(body, out_shape=jax.ShapeDtypeStruct((s, d), BF16), grid=(s // ROWS,),
                          in_specs=[_row_spec(d), _vec_spec(d)], out_specs=_row_spec(d),
                          compiler_params=_params(("parallel",)), name=name)(x, g)


def _rms_bwd_math(xv, g, dy):
    r = lax.rsqrt(jnp.mean(xv * xv, axis=-1, keepdims=True) + NORM_EPS)
    n = xv * r
    z = dy * g
    dx = r * (z - n * jnp.mean(z * n, axis=-1, keepdims=True))
    return dx, jnp.sum(dy * n, axis=0, keepdims=True)


def _rms_bwd(name, x, g, dy, resid):
    s, d = x.shape

    def body(x_ref, g_ref, dy_ref, r_ref, dx_ref, dxb_ref, dg_ref):
        dx, dg = _rms_bwd_math(x_ref[...], g_ref[...], dy_ref[...])
        dx = dx + r_ref[...]
        dx_ref[...] = dx
        dxb_ref[...] = dx.astype(BF16)

        @pl.when(pl.program_id(0) == 0)
        def _():
            dg_ref[...] = jnp.zeros_like(dg_ref)

        dg_ref[...] += dg

    return pl.pallas_call(
        body, out_shape=[jax.ShapeDtypeStruct((s, d), F32), jax.ShapeDtypeStruct((s, d), BF16), jax.ShapeDtypeStruct((1, d), F32)],
        grid=(s // ROWS,), in_specs=[_row_spec(d), _vec_spec(d), _row_spec(d), _row_spec(d)],
        out_specs=[_row_spec(d), _row_spec(d), _vec_spec(d)], compiler_params=_params(("arbitrary",)), name=name)(x, g, dy, resid)


def _final_loss(h2, tgt, g):
    s, d = h2.shape

    def body(x_ref, t_ref, g_ref, dx_ref, dxb_ref, dg_ref, ls_ref):
        xv, gv = x_ref[...], g_ref[...]
        r = lax.rsqrt(jnp.mean(xv * xv, axis=-1, keepdims=True) + NORM_EPS)
        diff = xv * r * gv - t_ref[...]
        dx, dg = _rms_bwd_math(xv, gv, diff * (1.0 / d))
        dx_ref[...] = dx
        dxb_ref[...] = dx.astype(BF16)

        @pl.when(pl.program_id(0) == 0)
        def _():
            dg_ref[...] = jnp.zeros_like(dg_ref)
            ls_ref[...] = jnp.zeros_like(ls_ref)

        dg_ref[...] += dg
        ls_ref[...] += jnp.sum(diff * diff, axis=0, keepdims=True)

    return pl.pallas_call(
        body, out_shape=[jax.ShapeDtypeStruct((s, d), F32), jax.ShapeDtypeStruct((s, d), BF16),
                         jax.ShapeDtypeStruct((1, d), F32), jax.ShapeDtypeStruct((1, d), F32)],
        grid=(s // ROWS,), in_specs=[_row_spec(d), _row_spec(d), _vec_spec(d)],
        out_specs=[_row_spec(d), _row_spec(d), _vec_spec(d), _vec_spec(d)],
        compiler_params=_params(("arbitrary",)), name="final_norm_loss")(h2, tgt, g)


ATTN_Q = 128


ATTN_BATCH = 8


def _attn_units(s):
    units = []
    for gi, d in enumerate(DILATIONS):
        for r in range(d):
            for q0 in range(0, s // d, ATTN_Q):
                k0 = max(q0 - SPAN, 0)
                units.append((gi, d, r, q0, k0, q0 + ATTN_Q - k0))
    return units


def _stream_rows(d, r, start, size):
    return pl.ds(r + start * d, size) if d == 1 else pl.ds(r + start * d, size, stride=d)


def _attn_scores(q_ref, k_ref, slope, d, r, q0, k0, nk):
    qrows, krows = _stream_rows(d, r, q0, ATTN_Q), _stream_rows(d, r, k0, nk)
    qb = q_ref[qrows, :].astype(BF16)
    kb = k_ref[krows, :].astype(BF16)
    sc = lax.dot_general(qb, kb, NT, preferred_element_type=F32) * (HEAD ** -0.5)
    qi = lax.broadcasted_iota(jnp.int32, (ATTN_Q, nk), 0)
    kj = lax.broadcasted_iota(jnp.int32, (ATTN_Q, nk), 1)
    dist = (q0 - k0) + qi - kj
    valid = (dist >= 0) & (dist <= SPAN)
    sc = sc - (slope * d) * dist.astype(F32)
    return jnp.where(valid, sc, NEG), valid, qb, kb, qrows, krows


def _attn_fwd(proj, dm, slopes):
    s = proj.shape[0]
    units = _attn_units(s)

    def body(sl_ref, q_ref, k_ref, v_ref, att_ref, lse_ref, *scr):
        o_scr, l_scr = scr[:3], scr[3:]
        slope = sl_ref[pl.program_id(0)]
        for first in range(0, len(units), ATTN_BATCH):
            batch = units[first:first + ATTN_BATCH]
            scored = [_attn_scores(q_ref, k_ref, slope, d, r, q0, k0, nk) for _, d, r, q0, k0, nk in batch]
            soft = []
            for sc, _, _, _, _, _ in scored:
                m = jnp.max(sc, axis=-1, keepdims=True)
                p = jnp.exp(sc - m)
                soft.append((m, p, jnp.sum(p, axis=-1, keepdims=True)))
            outs = [lax.dot_general(p.astype(BF16), v_ref[sco[5], :].astype(BF16), NN, preferred_element_type=F32)
                    for (m, p, l), sco in zip(soft, scored)]
            for (gi, *_), (m, p, l), sco, o in zip(batch, soft, scored, outs):
                o_scr[gi][sco[4], :] = o / l
                l_scr[gi][sco[4], :] = jnp.broadcast_to(m + jnp.log(l), (ATTN_Q, HEAD))
        l0, l1, l2 = l_scr[0][...], l_scr[1][...], l_scr[2][...]
        m = jnp.maximum(jnp.maximum(l0, l1), l2)
        w0, w1, w2 = jnp.exp(l0 - m), jnp.exp(l1 - m), jnp.exp(l2 - m)
        tot = w0 + w1 + w2
        att_ref[...] = ((w0 * o_scr[0][...] + w1 * o_scr[1][...] + w2 * o_scr[2][...]) / tot).astype(BF16)
        lse_ref[...] = m + jnp.log(tot)

    def seg(i):
        return pl.BlockSpec((s, HEAD), lambda h: (0, i * (dm // HEAD) + h))

    col = pl.BlockSpec((s, HEAD), lambda h: (0, h))
    return pl.pallas_call(
        body, out_shape=[jax.ShapeDtypeStruct((s, dm), BF16), jax.ShapeDtypeStruct((s, dm), F32)], grid=(dm // HEAD,),
        in_specs=[pl.BlockSpec(memory_space=pltpu.SMEM), seg(0), seg(1), seg(2)], out_specs=[col, col],
        scratch_shapes=[pltpu.VMEM((s, HEAD), F32)] * (2 * len(DILATIONS)),
        compiler_params=_params(("parallel",)), name="attn_fwd")(slopes, proj, proj, proj)


def _attn_bwd(proj, dm, datt, att, lse, slopes, others):
    s = proj.shape[0]
    units = _attn_units(s)

    def body(sl_ref, q_ref, k_ref, v_ref, do_ref, att_ref, lse_ref, o3, o4, o5, o6, out_ref, dq_scr, dk_scr, dv_scr, dl_scr):
        slope = sl_ref[pl.program_id(0)]
        delta = jnp.sum(do_ref[...] * att_ref[...].astype(F32), axis=-1, keepdims=True)
        dl_scr[...] = jnp.broadcast_to(delta, (s, HEAD))
        dq_scr[...] = jnp.zeros_like(dq_scr)
        dk_scr[...] = jnp.zeros_like(dk_scr)
        dv_scr[...] = jnp.zeros_like(dv_scr)
        for first in range(0, len(units), ATTN_BATCH):
            scored = [_attn_scores(q_ref, k_ref, slope, d, r, q0, k0, nk) for _, d, r, q0, k0, nk in units[first:first + ATTN_BATCH]]
            dobs = [do_ref[sco[4], :].astype(BF16) for sco in scored]
            dps = [lax.dot_general(dob, v_ref[sco[5], :].astype(BF16), NT, preferred_element_type=F32) for dob, sco in zip(dobs, scored)]
            ps = [jnp.where(sco[1], jnp.exp(sco[0] - lse_ref[sco[4], :][:, 0:1]), 0.0) for sco in scored]
            dss = [(p * (dp - dl_scr[sco[4], :][:, 0:1]) * (HEAD ** -0.5)).astype(BF16) for p, dp, sco in zip(ps, dps, scored)]
            dqs = [lax.dot_general(ds, sco[3], NN, preferred_element_type=F32) for ds, sco in zip(dss, scored)]
            dks = [lax.dot_general(ds, sco[2], TN, preferred_element_type=F32) for ds, sco in zip(dss, scored)]
            dvs = [lax.dot_general(p.astype(BF16), dob, TN, preferred_element_type=F32) for p, dob in zip(ps, dobs)]
            for sco, dq, dk, dv in zip(scored, dqs, dks, dvs):
                dq_scr[sco[4], :] += dq
                dk_scr[sco[5], :] += dk
                dv_scr[sco[5], :] += dv
        for j, scr in enumerate((dq_scr, dk_scr, dv_scr)):
            out_ref[j] = scr[...].astype(BF16)
        for j, other in enumerate((o3, o4, o5, o6)):
            out_ref[3 + j] = other[...]

    def seg(i):
        return pl.BlockSpec((s, HEAD), lambda h: (0, i * (dm // HEAD) + h))

    col = pl.BlockSpec((s, HEAD), lambda h: (0, h))
    return pl.pallas_call(
        body, out_shape=jax.ShapeDtypeStruct((7, s, dm), BF16), grid=(dm // HEAD,),
        in_specs=[pl.BlockSpec(memory_space=pltpu.SMEM), seg(0), seg(1), seg(2), col, col, col, col, col, col, col],
        out_specs=pl.BlockSpec((7, s, HEAD), lambda h: (0, 0, h)), scratch_shapes=[pltpu.VMEM((s, HEAD), F32)] * 4,
        compiler_params=_params(("parallel",)), name="attn_bwd")(slopes, proj, proj, proj, datt, att, lse, *others)


VEC_CB, VEC_BA, VEC_BX, VEC_LAM = 0, 1, 2, 3
SEG_Q, SEG_K, SEG_V, SEG_X, SEG_GATE, SEG_GA, SEG_GL = range(7)


def _softplus(z):
    return jnp.maximum(z, 0.0) + jnp.log1p(jnp.exp(-jnp.abs(z)))


def _gate_math(xc, wa_ref, wx_ref, vec):
    xcb = xc.astype(BF16)
    nh = xc.shape[1] // HEAD
    pre_a = jnp.concatenate([jnp.dot(xcb[:, h * HEAD:(h + 1) * HEAD], wa_ref[h], preferred_element_type=F32) for h in range(nh)], axis=1)
    pre_x = jnp.concatenate([jnp.dot(xcb[:, h * HEAD:(h + 1) * HEAD], wx_ref[h], preferred_element_type=F32) for h in range(nh)], axis=1)
    ra = _sigmoid(pre_a + vec[VEC_BA:VEC_BA + 1])
    ig = _sigmoid(pre_x + vec[VEC_BX:VEC_BX + 1])
    sp = _softplus(-vec[VEC_LAM:VEC_LAM + 1])
    log_a = -LRU_C * ra * sp
    a = jnp.exp(log_a)
    z = 2.0 * log_a
    one_minus_a2 = jnp.where(z > -0.01, -z * (1.0 + z * (0.5 + z * (1.0 / 6.0))), 1.0 - jnp.exp(z))
    mult = jnp.sqrt(one_minus_a2)
    return dict(xcb=xcb, ra=ra, ig=ig, sp=sp, a=a, mult=mult)


def _conv_pad_prev(pad_ref, cur, halo, first):
    pad_ref[0:SUBLANES, :] = jnp.where(first, 0.0, halo)
    pad_ref[SUBLANES:SUBLANES + cur.shape[0], :] = cur


def _shift_rows(x, s, fill, up=False):
    rid = lax.broadcasted_iota(jnp.int32, x.shape, 0)
    if up:
        return jnp.where(rid < SUBLANES - s, pltpu.roll(x, SUBLANES - s, axis=0), fill)
    return jnp.where(rid >= s, pltpu.roll(x, s, axis=0), fill)


def _lru_fwd(proj, d, cw8, vec8, wa, wx):
    s = proj.shape[0]
    hb = ROWS // SUBLANES

    def body(x_ref, halo_ref, g_ref, cw_ref, vec_ref, wa_ref, wx_ref, xc_ref, hp_ref, h2_ref, y_ref, pad, a_scr, u_scr, carry):
        @pl.when(pl.program_id(0) == 0)
        def _():
            carry[...] = jnp.zeros_like(carry)

        _conv_pad_prev(pad, x_ref[...], halo_ref[...], pl.program_id(0) == 0)
        vec = vec_ref[...]
        xc = vec[VEC_CB:VEC_CB + 1]
        for k in range(CONV_TAPS):
            xc = xc + cw_ref[k:k + 1, :] * pad[pl.ds(SUBLANES - (CONV_TAPS - 1) + k, ROWS), :]
        gm = _gate_math(xc, wa_ref, wx_ref, vec)
        xc_ref[...] = xc
        a_scr[...] = gm["a"]
        u_scr[...] = gm["mult"] * (gm["ig"] * xc)

        def group(gi, before):
            rows = pl.ds(pl.multiple_of(gi * SUBLANES, SUBLANES), SUBLANES)
            ca, cb = a_scr[rows, :], u_scr[rows, :]
            for sh in (1, 2, 4):
                cb = ca * _shift_rows(cb, sh, 0.0) + cb
                ca = ca * _shift_rows(ca, sh, 1.0)
            h = cb + ca * before
            hp_ref[rows, :] = jnp.where(lax.broadcasted_iota(jnp.int32, h.shape, 0) == 0, before, pltpu.roll(h, 1, axis=0))
            h2_ref[rows, :] = h.astype(BF16)
            y_ref[rows, :] = (h * _gelu(g_ref[rows, :])).astype(BF16)
            return jnp.broadcast_to(h[SUBLANES - 1:SUBLANES, :], h.shape)

        carry[...] = lax.fori_loop(0, ROWS // SUBLANES, group, carry[...])

    wspec = pl.BlockSpec(wa.shape, lambda i: (0, 0, 0))
    return pl.pallas_call(
        body, out_shape=[jax.ShapeDtypeStruct((s, d), F32)] * 2 + [jax.ShapeDtypeStruct((s, d), BF16)] * 2, grid=(s // ROWS,),
        in_specs=[pl.BlockSpec((ROWS, d), lambda i: (i, SEG_X)),
                  pl.BlockSpec((SUBLANES, d), lambda i: (jnp.maximum(i * hb - 1, 0), SEG_X)),
                  pl.BlockSpec((ROWS, d), lambda i: (i, SEG_GATE)),
                  _vec_spec(d, SUBLANES), _vec_spec(d, SUBLANES), wspec, wspec],
        out_specs=[_row_spec(d)] * 4,
        scratch_shapes=[pltpu.VMEM((ROWS + SUBLANES, d), F32), pltpu.VMEM((ROWS, d), F32), pltpu.VMEM((ROWS, d), F32), pltpu.VMEM((SUBLANES, d), F32)],
        compiler_params=_params(("arbitrary",)), name="lru_fwd")(proj, proj, proj, cw8, vec8, wa, wx)


def _lru_bwd(hp, dh, xc, wa, wx, vec8):
    s, d = xc.shape
    nh = d // HEAD
    nb = s // ROWS

    def body(hp_ref, dh_ref, xc_ref, wa_ref, wx_ref, vec_ref, dxc_ref, dwa_ref, dwx_ref, dvec_ref, a_scr, g_scr, da_scr, carry):
        @pl.when(pl.program_id(0) == 0)
        def _():
            carry[...] = jnp.zeros_like(carry)
            dwa_ref[...] = jnp.zeros_like(dwa_ref)
            dwx_ref[...] = jnp.zeros_like(dwx_ref)
            dvec_ref[...] = jnp.zeros_like(dvec_ref)

        xc_v, vec = xc_ref[...], vec_ref[...]
        gm = _gate_math(xc_v, wa_ref, wx_ref, vec)
        ra, ig, sp, a, mult = gm["ra"], gm["ig"], gm["sp"], gm["a"], gm["mult"]
        a_scr[...] = a

        def group(j, after):
            rows = pl.ds(pl.multiple_of((ROWS // SUBLANES - 1 - j) * SUBLANES, SUBLANES), SUBLANES)
            ca, dhv = a_scr[rows, :], dh_ref[rows, :]
            cb = ca * dhv
            for sh in (1, 2, 4):
                cb = ca * _shift_rows(cb, sh, 0.0, up=True) + cb
                ca = ca * _shift_rows(ca, sh, 1.0, up=True)
            c = cb + ca * after
            last = lax.broadcasted_iota(jnp.int32, c.shape, 0) == SUBLANES - 1
            g = dhv + jnp.where(last, after, pltpu.roll(c, SUBLANES - 1, axis=0))
            g_scr[rows, :] = g
            da_scr[rows, :] = g * hp_ref[rows, :]
            return jnp.broadcast_to(c[0:1, :], c.shape)

        carry[...] = lax.fori_loop(0, ROWS // SUBLANES, group, carry[...])
        du, da = g_scr[...], da_scr[...]
        dmult = du * ig * xc_v
        dlog_a = da * a - dmult * (a * a) / mult
        dpre_a = dlog_a * (-LRU_C * sp) * ra * (1.0 - ra)
        dpre_x = du * mult * xc_v * ig * (1.0 - ig)
        dlam = jnp.sum(dlog_a * (-LRU_C * ra), axis=0, keepdims=True) * (-_sigmoid(-vec[VEC_LAM:VEC_LAM + 1]))
        dvec_ref[VEC_BA:VEC_BA + 1, :] += jnp.sum(dpre_a, axis=0, keepdims=True)
        dvec_ref[VEC_BX:VEC_BX + 1, :] += jnp.sum(dpre_x, axis=0, keepdims=True)
        dvec_ref[VEC_LAM:VEC_LAM + 1, :] += dlam
        dab, dxb, xcb = dpre_a.astype(BF16), dpre_x.astype(BF16), gm["xcb"]
        back = []
        for h in range(nh):
            cols = slice(h * HEAD, (h + 1) * HEAD)
            dwa_ref[h] += lax.dot_general(xcb[:, cols], dab[:, cols], TN, preferred_element_type=F32)
            dwx_ref[h] += lax.dot_general(xcb[:, cols], dxb[:, cols], TN, preferred_element_type=F32)
            back.append(lax.dot_general(dab[:, cols], wa_ref[h], NT, preferred_element_type=F32)
                        + lax.dot_general(dxb[:, cols], wx_ref[h], NT, preferred_element_type=F32))
        dxc_ref[...] = du * mult * ig + jnp.concatenate(back, axis=1)

    rows_rev = pl.BlockSpec((ROWS, d), lambda i: (nb - 1 - i, 0))
    wspec = pl.BlockSpec(wa.shape, lambda i: (0, 0, 0))
    return pl.pallas_call(
        body, out_shape=[jax.ShapeDtypeStruct((s, d), F32), jax.ShapeDtypeStruct(wa.shape, F32), jax.ShapeDtypeStruct(wa.shape, F32),
                         jax.ShapeDtypeStruct((SUBLANES, d), F32)],
        grid=(nb,), in_specs=[rows_rev, rows_rev, rows_rev, wspec, wspec, _vec_spec(d, SUBLANES)],
        out_specs=[rows_rev, wspec, wspec, _vec_spec(d, SUBLANES)],
        scratch_shapes=[pltpu.VMEM((ROWS, d), F32)] * 3 + [pltpu.VMEM((SUBLANES, d), F32)],
        compiler_params=_params(("arbitrary",)), name="lru_bwd")(hp, dh, xc, wa, wx, vec8)


def _conv_bwd(dxc, proj, cw8):
    s, d = dxc.shape
    hb = ROWS // SUBLANES
    last = s // SUBLANES - 1

    def body(dc_ref, dnext_ref, x_ref, xprev_ref, cw_ref, dx_ref, dcw_ref, padd, padx):
        i = pl.program_id(0)

        @pl.when(i == 0)
        def _():
            dcw_ref[...] = jnp.zeros_like(dcw_ref)

        dc = dc_ref[...]
        padd[0:ROWS, :] = dc
        padd[ROWS:ROWS + SUBLANES, :] = jnp.where(i == pl.num_programs(0) - 1, 0.0, dnext_ref[...])
        _conv_pad_prev(padx, x_ref[...], xprev_ref[...], i == 0)
        dx = jnp.zeros_like(dc)
        for k in range(CONV_TAPS):
            dx = dx + cw_ref[k:k + 1, :] * padd[pl.ds(CONV_TAPS - 1 - k, ROWS), :]
            dcw_ref[k:k + 1, :] += jnp.sum(dc * padx[pl.ds(SUBLANES - (CONV_TAPS - 1) + k, ROWS), :], axis=0, keepdims=True)
        dcw_ref[CONV_TAPS:CONV_TAPS + 1, :] += jnp.sum(dc, axis=0, keepdims=True)
        dx_ref[...] = dx.astype(BF16)

    return pl.pallas_call(
        body, out_shape=[jax.ShapeDtypeStruct((s, d), BF16), jax.ShapeDtypeStruct((SUBLANES, d), F32)], grid=(s // ROWS,),
        in_specs=[_row_spec(d), pl.BlockSpec((SUBLANES, d), lambda i: (jnp.minimum((i + 1) * hb, last), 0)),
                  pl.BlockSpec((ROWS, d), lambda i: (i, SEG_X)),
                  pl.BlockSpec((SUBLANES, d), lambda i: (jnp.maximum(i * hb - 1, 0), SEG_X)), _vec_spec(d, SUBLANES)],
        out_specs=[_row_spec(d), _vec_spec(d, SUBLANES)],
        scratch_shapes=[pltpu.VMEM((ROWS + SUBLANES, d), F32), pltpu.VMEM((ROWS + SUBLANES, d), F32)],
        compiler_params=_params(("arbitrary",)), name="lru_conv_bwd")(dxc, dxc, proj, proj, cw8)


def _coords():
    return lax.axis_index("x"), lax.axis_index("y"), lax.axis_index("c")


def _other_chips(x, y):
    return [(1 - x, y), (x, 1 - y), (1 - x, 1 - y)]


def _slab(ref, kind, shard_shape, idx, half=None):
    r, c = shard_shape
    r0, nr = (0, r) if half is None else (half * (r // 2), r // 2)
    if kind == "col":
        return ref.at[pl.ds(r0, nr), pl.ds(pl.multiple_of(idx * c, LANES), c)]
    if kind == "row":
        return ref.at[pl.ds(pl.multiple_of(idx * r, SUBLANES) + r0, nr), :]
    return ref.at[idx, pl.ds(r0, nr), :]


def _full_shape(shard_shape, kind):
    r, c = shard_shape
    return {"col": (r, c * N_DEV), "row": (r * N_DEV, c), "slot": (N_DEV, r, c)}[kind]


def _handshake(peers):
    barrier = pltpu.get_barrier_semaphore()
    for peer in peers:
        pl.semaphore_signal(barrier, inc=1, device_id=peer, device_id_type=MESH)
    pl.semaphore_wait(barrier, len(peers))


def _launch(name, body, out_shape, operands, sems, sequencer_id):
    if sequencer_id is None:
        return pl.pallas_call(body, out_shape=out_shape, in_specs=[HBM] * len(operands), out_specs=[HBM] * len(out_shape),
                              scratch_shapes=sems, name=name)(*operands)
    return pl.kernel(body, out_type=out_shape, mesh=plsc.ScalarSubcoreMesh(axis_name="seq", num_cores=1), name=name,
                     scratch_types=sems, compiler_params=pltpu.CompilerParams(collective_id=sequencer_id))(*operands)


AG_COPIES = 10


def _all_gather(name, shards, kinds, sequencer_id=None):
    n = len(shards)
    shapes = [s.shape for s in shards]

    def body(*refs):
        ins, outs = refs[:n], refs[n:2 * n]
        send_sems, recv_sems, local_sems = refs[2 * n:]
        x, y, c = _coords()
        me, sib, xn, yn, dg = (x, y, c), (x, y, 1 - c), (1 - x, y, c), (x, 1 - y, c), (1 - x, 1 - y, c)
        if sequencer_id is not None:
            _handshake([sib, xn, yn])

        def part(i, dev, half=None):
            return _slab(outs[i], kinds[i], shapes[i], 4 * dev[0] + 2 * dev[1] + dev[2], half)

        def copy(i, k, block, half, to, own=False):
            r = shapes[i][0]
            src = part(i, block, half) if not own else (ins[i] if half is None else ins[i].at[pl.ds(half * (r // 2), r // 2), :])
            return pltpu.make_async_remote_copy(
                src_ref=src, dst_ref=part(i, block, half), send_sem=send_sems.at[AG_COPIES * i + k],
                recv_sem=recv_sems.at[AG_COPIES * i + k], device_id=to, device_id_type=MESH)

        def other_core(dev):
            return (dev[0], dev[1], 1 - c)

        started = []

        def start(cp):
            cp.start()
            started.append(cp)

        for i in range(n):
            start(copy(i, 1, me, 0, xn, own=True))
            start(copy(i, 4, me, 1, yn, own=True))
            start(copy(i, 2, me, 1, xn, own=True))
            start(copy(i, 3, me, 0, yn, own=True))
            start(copy(i, 0, me, None, sib, own=True))
        mine = [pltpu.make_async_copy(ins[i], part(i, me), local_sems.at[i]) for i in range(n)]
        for cp in mine:
            cp.start()
        for i in range(n):
            copy(i, 1, xn, 0, me).wait_recv()
            start(copy(i, 5, xn, 0, yn))
            copy(i, 4, yn, 1, me).wait_recv()
            start(copy(i, 6, yn, 1, xn))
        for i in range(n):
            copy(i, 2, xn, 1, me).wait_recv()
            start(copy(i, 7, xn, None, sib))
            copy(i, 3, yn, 0, me).wait_recv()
            start(copy(i, 8, yn, None, sib))
        for i in range(n):
            copy(i, 5, dg, 0, me).wait_recv()
            copy(i, 6, dg, 1, me).wait_recv()
            start(copy(i, 9, dg, None, sib))
        for i in range(n):
            copy(i, 0, sib, None, me).wait_recv()
            for k, dev in ((7, xn), (8, yn), (9, dg)):
                copy(i, k, other_core(dev), None, me).wait_recv()
        for cp in started:
            cp.wait_send()
        for cp in mine:
            cp.wait()

    out_shape = [jax.ShapeDtypeStruct(_full_shape(s.shape, k), s.dtype) for s, k in zip(shards, kinds)]
    sems = [pltpu.SemaphoreType.DMA((AG_COPIES * n,)), pltpu.SemaphoreType.DMA((AG_COPIES * n,)), pltpu.SemaphoreType.DMA((n,))]
    return _launch(name, body, out_shape, shards, sems, sequencer_id)


def _sibling_copies(kinds, shard_shapes):
    def make(ins, outs, send_sems, recv_sems):
        x, y, c = _coords()
        return [pltpu.make_async_remote_copy(
            src_ref=_slab(ins[i], kinds[i], shard_shapes[i], 2 * q + (1 - c)), dst_ref=outs[i].at[q],
            send_sem=send_sems.at[N_CHIP * i + q], recv_sem=recv_sems.at[N_CHIP * i + q],
            device_id=(x, y, 1 - c), device_id_type=MESH) for i in range(len(ins)) for q in range(N_CHIP)]
    return make


def _sibling_side(partials, kinds, shard_shapes):
    return (partials, [jax.ShapeDtypeStruct((N_CHIP, *s), BF16) for s in shard_shapes], N_CHIP * len(partials),
            _sibling_copies(kinds, shard_shapes))


def _exchange_siblings(name, partials, kinds, shard_shapes, sequencer_id=None):
    n = len(partials)
    make = _sibling_copies(kinds, shard_shapes)

    def body(*refs):
        if sequencer_id is not None:
            x, y, c = _coords()
            _handshake([(x, y, 1 - c)])
        cps = make(refs[:n], refs[n:2 * n], refs[2 * n], refs[2 * n + 1])
        for cp in cps:
            cp.start()
        for cp in cps:
            cp.wait()

    return _launch(name, body, [jax.ShapeDtypeStruct((N_CHIP, *s), BF16) for s in shard_shapes], partials,
                   [pltpu.SemaphoreType.DMA((N_CHIP * n,)), pltpu.SemaphoreType.DMA((N_CHIP * n,))], sequencer_id)


def _exchange_chips(name, chip_sums, sequencer_id=None):
    n = len(chip_sums)

    def body(*refs):
        ins, outs = refs[:n], refs[n:2 * n]
        send_sems, recv_sems = refs[2 * n:]
        x, y, c = _coords()
        if sequencer_id is not None:
            _handshake([(cx, cy, c) for cx, cy in _other_chips(x, y)])
        cps = []
        for i in range(n):
            for k, (cx, cy) in enumerate(_other_chips(x, y)):
                cps.append(pltpu.make_async_remote_copy(
                    src_ref=ins[i].at[2 * cx + cy], dst_ref=outs[i].at[k], send_sem=send_sems.at[3 * i + k],
                    recv_sem=recv_sems.at[3 * i + k], device_id=(cx, cy, c), device_id_type=MESH))
        for cp in cps:
            cp.start()
        for cp in cps:
            cp.wait()

    return _launch(name, body, [jax.ShapeDtypeStruct((3, *t.shape[1:]), BF16) for t in chip_sums], chip_sums,
                   [pltpu.SemaphoreType.DMA((3 * n,)), pltpu.SemaphoreType.DMA((3 * n,))], sequencer_id)


def _all_peers(x, y, c):
    return [(x ^ (k >> 2), y ^ ((k >> 1) & 1), c ^ (k & 1)) for k in range(1, N_DEV)]


def _small_scatter(name, packed, sequencer_id):
    rows = packed.shape[0] // N_DEV

    def body(p_ref, rb_ref, send_sems, recv_sems, local_sem):
        x, y, c = _coords()
        me = 4 * x + 2 * y + c
        peers = _all_peers(x, y, c)
        _handshake(peers)

        def piece(idx):
            return p_ref.at[pl.ds(pl.multiple_of(idx * rows, SUBLANES), rows), :]

        cps = [pltpu.make_async_remote_copy(src_ref=piece(me ^ k), dst_ref=rb_ref.at[k], send_sem=send_sems.at[k], recv_sem=recv_sems.at[k],
                                            device_id=peers[k - 1], device_id_type=MESH) for k in range(1, N_DEV)]
        for cp in cps:
            cp.start()
        mine = pltpu.make_async_copy(piece(me), rb_ref.at[0], local_sem)
        mine.start()
        for cp in cps:
            cp.wait()
        mine.wait()

    return _launch(name, body, [jax.ShapeDtypeStruct((N_DEV, rows, LANES), F32)], [packed],
                   [pltpu.SemaphoreType.DMA((N_DEV,)), pltpu.SemaphoreType.DMA((N_DEV,)), pltpu.SemaphoreType.DMA], sequencer_id)[0]


def _small_sum(name, pieces):
    def body(p_ref, o_ref):
        acc = p_ref[0]
        for k in range(1, N_DEV):
            acc = acc + p_ref[k]
        o_ref[...] = acc

    vm = pl.BlockSpec(memory_space=pltpu.VMEM)
    return pl.pallas_call(body, out_shape=jax.ShapeDtypeStruct(pieces.shape[1:], F32), in_specs=[vm], out_specs=vm, name=name)(pieces)


def _small_gather(name, tot, sequencer_id):
    rows = tot.shape[0]

    def body(t_ref, out_ref, send_sems, recv_sems, local_sem):
        x, y, c = _coords()
        me = 4 * x + 2 * y + c
        peers = _all_peers(x, y, c)
        _handshake(peers)

        def piece(idx):
            return out_ref.at[pl.ds(pl.multiple_of(idx * rows, SUBLANES), rows), :]

        cps = [pltpu.make_async_remote_copy(src_ref=t_ref, dst_ref=piece(me), send_sem=send_sems.at[k], recv_sem=recv_sems.at[k],
                                            device_id=peers[k - 1], device_id_type=MESH) for k in range(1, N_DEV)]
        for cp in cps:
            cp.start()
        mine = pltpu.make_async_copy(t_ref, piece(me), local_sem)
        mine.start()
        for cp in cps:
            cp.wait()
        mine.wait()

    return _launch(name, body, [jax.ShapeDtypeStruct((N_DEV * rows, LANES), F32)], [tot],
                   [pltpu.SemaphoreType.DMA((N_DEV,)), pltpu.SemaphoreType.DMA((N_DEV,)), pltpu.SemaphoreType.DMA], sequencer_id)[0]


def _all_reduce_small(name, packed):
    rows = packed.shape[0] // N_DEV

    def body(p_ref, out_ref, rb, tot, send_sems, recv_sems):
        x, y, c = _coords()
        me = 4 * x + 2 * y + c

        def peer(k):
            return (x ^ (k >> 2), y ^ ((k >> 1) & 1), c ^ (k & 1))

        def rows_of(idx):
            return pl.ds(pl.multiple_of(idx * rows, SUBLANES), rows)

        def piece(ref, idx):
            return ref.at[rows_of(idx), :]

        scatter = [pltpu.make_async_remote_copy(src_ref=piece(p_ref, me ^ k), dst_ref=rb.at[k], send_sem=send_sems.at[k],
                                                recv_sem=recv_sems.at[k], device_id=peer(k), device_id_type=MESH) for k in range(1, N_DEV)]
        for cp in scatter:
            cp.start()
        acc = p_ref[rows_of(me), :]
        for cp in scatter:
            cp.wait_recv()
        for k in range(1, N_DEV):
            acc = acc + rb[k]
        tot[...] = acc
        out_ref[rows_of(me), :] = acc
        gather = [pltpu.make_async_remote_copy(src_ref=tot, dst_ref=piece(out_ref, me), send_sem=send_sems.at[N_DEV + k],
                                               recv_sem=recv_sems.at[N_DEV + k], device_id=peer(k), device_id_type=MESH)
                  for k in range(1, N_DEV)]
        for cp in gather:
            cp.start()
        for k in range(1, N_DEV):
            pltpu.make_async_remote_copy(src_ref=tot, dst_ref=piece(out_ref, me ^ k), send_sem=send_sems.at[N_DEV + k],
                                         recv_sem=recv_sems.at[N_DEV + k], device_id=peer(k), device_id_type=MESH).wait_recv()
        for cp in scatter + gather:
            cp.wait_send()

    vm = pl.BlockSpec(memory_space=pltpu.VMEM)
    return pl.pallas_call(
        body, out_shape=jax.ShapeDtypeStruct(packed.shape, F32), in_specs=[vm], out_specs=vm,
        scratch_shapes=[pltpu.VMEM((N_DEV, rows, LANES), F32), pltpu.VMEM((rows, LANES), F32),
                        pltpu.SemaphoreType.DMA((2 * N_DEV,)), pltpu.SemaphoreType.DMA((2 * N_DEV,))],
        compiler_params=pltpu.CompilerParams(vmem_limit_bytes=VMEM_LIMIT), name=name)(packed)


def _adamw_math(g, w, m, v):
    m = ADAM_B1 * m + (1.0 - ADAM_B1) * g
    v = ADAM_B2 * v + (1.0 - ADAM_B2) * (g * g)
    delta = -ADAM_LR * ((m / ADAM_C1) / (jnp.sqrt(v / ADAM_C2) + ADAM_EPS) + ADAM_WD * w)
    return delta, m, v


def _slab_spec(kind, shard_shape, tr, slab_of):
    r, c = shard_shape
    if kind == "col":
        return pl.BlockSpec((tr, c), lambda q, i, sc: (i, slab_of(q, sc)))
    return pl.BlockSpec((tr, c), lambda q, i, sc: (slab_of(q, sc) * (r // tr) + i, 0))


def _chip_sum(name, partial, recv, kind, shard_shape, core):
    r, c = shard_shape
    tr = _blk(r, 1024)

    def body(core_ref, p_ref, r_ref, o_ref):
        o_ref[...] = (p_ref[...].astype(F32) + r_ref[...].astype(F32)).astype(BF16)

    spec4 = pl.BlockSpec((None, tr, c), lambda q, i, sc: (q, i, 0))
    grid_spec = pltpu.PrefetchScalarGridSpec(
        num_scalar_prefetch=1, grid=(N_CHIP, r // tr),
        in_specs=[_slab_spec(kind, shard_shape, tr, lambda q, sc: 2 * q + sc[0]), spec4], out_specs=spec4)
    return pl.pallas_call(body, out_shape=jax.ShapeDtypeStruct((N_CHIP, r, c), BF16), grid_spec=grid_spec,
                          compiler_params=_params(("parallel", "parallel")), name=name)(core, partial, recv)


def _adamw_shard(name, parts, w, m, v, chip):
    r, c = w.shape
    n_parts = len(parts)
    tr = _blk(r // n_parts, 256)
    per = r // n_parts // tr

    def body(chip_ref, *refs):
        src, (w_ref, m_ref, v_ref), (g_out, d_out, m_out, v_out) = refs[:2 * n_parts], refs[2 * n_parts:2 * n_parts + 3], refs[2 * n_parts + 3:]
        for p in range(n_parts):
            @pl.when(pl.program_id(0) // per == p)
            def _():
                g = src[2 * p][...].astype(F32)
                for k in range(3):
                    g = g + src[2 * p + 1][k].astype(F32)
                g_out[...] = g
                d_out[...], m_out[...], v_out[...] = _adamw_math(g, w_ref[...], m_ref[...], v_ref[...])

    def part_specs(p):
        at = lambda i: jnp.clip(i - p * per, 0, per - 1)
        return [pl.BlockSpec((None, tr, c), lambda i, sc: (sc[0], at(i), 0)), pl.BlockSpec((3, tr, c), lambda i, sc: (0, at(i), 0))]

    blk = pl.BlockSpec((tr, c), lambda i, sc: (i, 0))
    grid_spec = pltpu.PrefetchScalarGridSpec(
        num_scalar_prefetch=1, grid=(r // tr,), in_specs=[s for p in range(n_parts) for s in part_specs(p)] + [blk, blk, blk], out_specs=[blk] * 4)
    return pl.pallas_call(body, out_shape=[jax.ShapeDtypeStruct((r, c), F32)] * 4, grid_spec=grid_spec,
                          compiler_params=_params(("parallel",)), name=name)(chip, *[a for p in parts for a in p], w, m, v)


def _adamw_small(name, g, w, m, v):
    def body(g_ref, w_ref, m_ref, v_ref, d_out, m_out, v_out):
        d_out[...], m_out[...], v_out[...] = _adamw_math(g_ref[...], w_ref[...], m_ref[...], v_ref[...])

    vm = pl.BlockSpec(memory_space=pltpu.VMEM)
    return pl.pallas_call(body, out_shape=[jax.ShapeDtypeStruct(g.shape, F32)] * 3, in_specs=[vm] * 4, out_specs=[vm] * 3,
                          compiler_params=pltpu.CompilerParams(vmem_limit_bytes=VMEM_LIMIT), name=name)(g, w, m, v)


def _pack_rows(arrays, total_rows):
    flat = [a.reshape(-1, LANES) for a in arrays]
    used = sum(f.shape[0] for f in flat)
    return jnp.concatenate(flat + [jnp.zeros((total_rows - used, LANES), F32)], axis=0)


def _unpack_rows(packed, like):
    out, at = [], 0
    for a in like:
        n = a.size // LANES
        out.append(packed[at:at + n].reshape(a.shape))
        at += n
    return out


def kernel(x, norm_mix_g, w_in, conv_w, conv_b, lru_wa, lru_ba, lru_wx, lru_bx, lru_lambda, w_proj_attn, w_proj_lru, w_out, norm_mlp_g, w_up, w_down, norm_final_g, loss_target, m_norm_mix_g, m_w_in, m_conv_w, m_conv_b, m_lru_wa, m_lru_ba, m_lru_wx, m_lru_bx, m_lru_lambda, m_w_proj_attn, m_w_proj_lru, m_w_out, m_norm_mlp_g, m_w_up, m_w_down, m_norm_final_g, v_norm_mix_g, v_w_in, v_conv_w, v_conv_b, v_lru_wa, v_lru_ba, v_lru_wx, v_lru_bx, v_lru_lambda, v_w_proj_attn, v_w_proj_lru, v_w_out, v_norm_mlp_g, v_w_up, v_w_down, v_norm_final_g):
    xs, tgt = x[0], loss_target[0]
    s, d = xs.shape
    nh = d // HEAD
    ix, iy, ic = _coords()
    core = jnp.reshape(ic, (1,)).astype(jnp.int32)
    chip = jnp.reshape(2 * ix + iy, (1,)).astype(jnp.int32)
    dev = 4 * ix + 2 * iy + ic

    big = [w_in[0], w_proj_attn[0], w_proj_lru[0], w_out[0], w_up[0], w_down[0]]
    big_m = [m_w_in[0], m_w_proj_attn[0], m_w_proj_lru[0], m_w_out[0], m_w_up[0], m_w_down[0]]
    big_v = [v_w_in[0], v_w_proj_attn[0], v_w_proj_lru[0], v_w_out[0], v_w_up[0], v_w_down[0]]
    kinds = ["col", "row", "row", "row", "col", "row"]
    pad_taps = lambda t: jnp.pad(t, ((0, SUBLANES - CONV_TAPS), (0, 0)))
    shards = [w.astype(BF16) for w in big]
    pad_taps2 = lambda t: jnp.pad(t, ((0, 2 * SUBLANES - CONV_TAPS), (0, 0)))
    win, cw_slots = _all_gather("all_gather_w_in", [shards[0], pad_taps2(conv_w[0])], ["col", "slot"])
    later = lax.optimization_barrier((shards[1:], win))[0]
    wpa, wpl, wout = _all_gather("all_gather_mix", later[:3], kinds[1:4], sequencer_id=1)
    wup, wdown = _all_gather("all_gather_mlp", later[3:], kinds[4:], sequencer_id=5)
    cw8 = jnp.transpose(cw_slots[:, :SUBLANES], (1, 0, 2)).reshape(SUBLANES, d)
    row_id = lax.broadcasted_iota(jnp.int32, (SUBLANES, d), 0)
    vec8 = sum(jnp.where(row_id == k, t, 0.0) for k, t in ((VEC_CB, conv_b), (VEC_BA, lru_ba), (VEC_BX, lru_bx), (VEC_LAM, lru_lambda)))
    wa16, wx16 = lru_wa[0].astype(BF16), lru_wx[0].astype(BF16)
    slopes = 2.0 ** (-8.0 * jnp.arange(1, nh + 1, dtype=F32) / nh)

    def seg_specs(*segs):
        return lambda bm, bn: [pl.BlockSpec((bm, bn), (lambda i, j, kk, sg=sg: (i, sg * (d // bn) + j))) for sg in segs]

    def plain_specs(k):
        return lambda bm, bn: [pl.BlockSpec((bm, bn), lambda i, j, kk: (i, j)) for _ in range(k)]

    xn = _rms_fwd("norm_mix", xs, norm_mix_g)
    proj = _mm_fwd("proj_in", xn, win, 0, 7 * d, [F32], bm=2048)[0]
    att, lse = _attn_fwd(proj, d, slopes)
    xc, hp, h2d, ylru = _lru_fwd(proj, d, cw8, vec8, wa16, wx16)
    pa = _mm_fwd("proj_attn", att, wpa, 0, d, [BF16], bm=2048)[0]

    def merge(acc, pa_b, ga, gl):
        return acc, _sigmoid(ga) * pa_b.astype(F32) + _sigmoid(gl) * acc

    plr, merged = _mm_fwd("proj_lru_merge", ylru, wpl, 0, d, [BF16, BF16], merge, (pa, proj, proj),
                          lambda bm, bn: plain_specs(1)(bm, bn) + seg_specs(SEG_GA, SEG_GL)(bm, bn), bn=512)
    h1 = _mm_fwd("mix_out", merged, wout, 0, d, [F32], lambda acc, r: (acc + r,), (xs,), plain_specs(1))[0]
    hn = _rms_fwd("norm_mlp", h1, norm_mlp_g)

    def relu2(acc):
        return acc, jnp.square(jnp.maximum(acc, 0.0))

    up, hid = _mm_fwd("mlp_up", hn, wup, 0, wup.shape[1], [BF16, BF16], relu2, bm=2048)
    h2 = _mm_fwd("mlp_down", hid, wdown, 0, d, [F32], lambda acc, r: (acc + r,), (h1,), plain_specs(1))[0]
    dh2, dh2b, dg3, loss_lanes = _final_loss(h2, tgt, norm_final_g.reshape(1, d))
    loss = lax.psum(0.5 / d * jnp.sum(loss_lanes), ("x", "y", "c"))
    dh2b = lax.optimization_barrier((dh2b, loss))[0]

    def reduce_group(tag, kk, shp, partials, from_sibling, sequencer_id):
        sums = [_chip_sum(f"chip_sum_{tag}_{i}", p, f, k, sh, core) for i, (p, f, k, sh) in enumerate(zip(partials, from_sibling, kk, shp))]
        return list(zip(sums, _exchange_chips(f"rs_chips_{tag}", sums, sequencer_id)))

    dup = _mm_nt("mlp_down_dx", dh2b, wdown, [BF16], lambda acc, u: (acc * (2.0 * jnp.maximum(u.astype(F32), 0.0)),), (up,), plain_specs(1))[0]
    g_wdown = _mm_tn("mlp_down_dw", hid, dh2b)
    g_wup = _mm_tn("mlp_up_dw", hn, dup)
    shp_mlp = [w.shape for w in big[4:]]
    (dhn,), sib_mlp = _mm_nt("mlp_up_dx", dup, wup, [F32], side=_sibling_side([g_wup, g_wdown], kinds[4:], shp_mlp))
    red_up, red_down = reduce_group("mlp", kinds[4:], shp_mlp, [g_wup, g_wdown], sib_mlp, 2)
    dhn = lax.optimization_barrier((dhn, red_up[0], red_down[0]))[0]
    dh1, dh1b, dg2 = _rms_bwd("norm_mlp_bwd", h1, norm_mlp_g, dhn, dh2)

    def merge_bwd(acc, pa_b, pl_b, ga, gl):
        sa, sl = _sigmoid(ga), _sigmoid(gl)
        return acc * sa, acc * sl, acc * pa_b.astype(F32) * sa * (1.0 - sa), acc * pl_b.astype(F32) * sl * (1.0 - sl)

    dpa, dpl, dga, dgl = _mm_nt("mix_out_dx", dh1b, wout, [BF16] * 4, merge_bwd, (pa, plr, proj, proj),
                                lambda bm, bn: plain_specs(2)(bm, bn) + seg_specs(SEG_GA, SEG_GL)(bm, bn), bn=512)
    g_wout = _mm_tn("mix_out_dw", merged, dh1b)
    datt = _mm_nt("proj_attn_dx", dpa, wpa, [F32], bm=2048)[0]
    g_wpa = _mm_tn("proj_attn_dw", att, dpa)

    def lru_out_bwd(acc, h_b, gate):
        return acc * _gelu(gate), acc * h_b.astype(F32) * _gelu_grad(gate)

    g_wpl = _mm_tn("proj_lru_dw", ylru, dpl)
    shp_mix = [w.shape for w in big[1:4]]
    (dh, dxg), sib_mix = _mm_nt("proj_lru_dx", dpl, wpl, [F32, BF16], lru_out_bwd, (h2d, proj),
                                lambda bm, bn: plain_specs(1)(bm, bn) + seg_specs(SEG_GATE)(bm, bn), bn=512,
                                side=_sibling_side([g_wpa, g_wpl, g_wout], kinds[1:4], shp_mix))
    red_pa, red_pl, red_out = reduce_group("mix", kinds[1:4], shp_mix, [g_wpa, g_wpl, g_wout], sib_mix, 3)
    dxc, dwa, dwx, dvec = _lru_bwd(hp, dh, xc, wa16, wx16, vec8)
    dxr, dconv = _conv_bwd(dxc, proj, cw8)
    dproj = _attn_bwd(proj, d, datt, att, lse, slopes, (dxr, dxg, dga, dgl))

    def small_step(tag, grads, ws, ms, vs, like, after, seq=None):
        n_rows = sum(g.size for g in grads) // LANES
        per_dev = -(-n_rows // (N_DEV * SUBLANES)) * SUBLANES
        packed = lax.optimization_barrier((_pack_rows(grads, N_DEV * per_dev), after))[0]
        if seq is None:
            total = _all_reduce_small(f"all_reduce_{tag}", packed)
        else:
            pieces = lax.optimization_barrier((_small_scatter(f"scatter_{tag}", packed, seq[0]), seq[2]))[0]
            total = _small_gather(f"gather_{tag}", _small_sum(f"sum_{tag}", pieces), seq[1])
        w_rows = -(-(sum(w.size for w in ws) // LANES) // SUBLANES) * SUBLANES
        upd = _adamw_small(f"adamw_{tag}", total[:w_rows], _pack_rows(ws, w_rows), _pack_rows(ms, w_rows), _pack_rows(vs, w_rows))
        return _unpack_rows(total, like), [_unpack_rows(t, ws) for t in upd]

    early_w = [conv_b, lru_wa, lru_ba, lru_wx, lru_bx, lru_lambda, norm_mlp_g, norm_final_g]
    early_m = [m_conv_b, m_lru_wa, m_lru_ba, m_lru_wx, m_lru_bx, m_lru_lambda, m_norm_mlp_g, m_norm_final_g]
    early_v = [v_conv_b, v_lru_wa, v_lru_ba, v_lru_wx, v_lru_bx, v_lru_lambda, v_norm_mlp_g, v_norm_final_g]
    early_g = [dconv[CONV_TAPS:CONV_TAPS + 1], dwa, dvec[VEC_BA:VEC_BA + 1], dwx, dvec[VEC_BX:VEC_BX + 1],
               dvec[VEC_LAM:VEC_LAM + 1], dg2, dg3, dconv[0:CONV_TAPS]]
    dproj = lax.optimization_barrier((dproj, red_up[1], red_down[1]))[0]
    dproj = lax.optimization_barrier((dproj, red_pa[1], red_pl[1], red_out[1]))[0]
    early_sum, early_upd = small_step("small", early_g, early_w, early_m, early_v,
                                      early_w + [jax.ShapeDtypeStruct((1, CONV_TAPS, d), F32)], dxr, seq=(8, 9, dproj))
    g_cw_full = early_sum[-1]
    cshard = conv_w.shape[2]
    g_cw = lax.dynamic_slice(g_cw_full, (0, 0, dev * cshard), (1, CONV_TAPS, cshard))
    cw_delta, cw_m, cw_v = (t[:CONV_TAPS][None] for t in _adamw_small(
        "adamw_conv_w", pad_taps(g_cw[0]), pad_taps(conv_w[0]), pad_taps(m_conv_w[0]), pad_taps(v_conv_w[0])))
    half = (big[0].shape[0] // 2, big[0].shape[1])
    g_in0 = _mm_tn("proj_in_dw_0", xn, dproj, part=(0, 2))
    g_in1, sib_in0 = _mm_tn("proj_in_dw_1", xn, dproj, part=(1, 2), side=_sibling_side([g_in0], ["col"], [half]))
    red_in = reduce_group("in_0", ["col"], [half], [g_in0], sib_in0, 4)
    dproj = lax.optimization_barrier((dproj, red_in[0][0], early_sum))[0]
    (dxn0,), sib_in1 = _mm_nt("proj_in_dx_0", dproj, win, [F32], part=(0, 2), side=_sibling_side([g_in1], ["col"], [half]))
    red_in += reduce_group("in_1", ["col"], [half], [g_in1], sib_in1, 6)
    dproj = lax.optimization_barrier((dproj, red_in[1][0]))[0]
    dxn1 = _mm_nt("proj_in_dx_1", dproj, win, [F32], part=(1, 2))[0]
    dxn = jnp.concatenate([dxn0, dxn1], axis=0)
    dxn = lax.optimization_barrier((dxn, red_in[0][1]))[0]
    grad_x, _, dg1 = _rms_bwd("norm_mix_bwd", xs, norm_mix_g, dxn, dh1)
    red_up, red_down = lax.optimization_barrier(((red_up, red_down), dg1))[0]
    big_out = {i: _adamw_shard(f"adamw_{i}", [red], big[i], big_m[i], big_v[i], chip) for i, red in ((4, red_up), (5, red_down))}
    big_out.update({i: _adamw_shard(f"adamw_{i}", [red], big[i], big_m[i], big_v[i], chip) for i, red in ((1, red_pa), (2, red_pl), (3, red_out))})
    late_sum, late_upd = small_step("norm_mix", [dg1], [norm_mix_g], [m_norm_mix_g], [v_norm_mix_g], [norm_mix_g], (big_out[4], big_out[5]))
    big_out[0] = _adamw_shard("adamw_0", red_in, big[0], big_m[0], big_v[0], chip)
    s_grad = late_sum + early_sum[:-1]
    s_delta, s_m, s_v = (late_upd[j] + early_upd[j] for j in range(3))


    names = ["norm_mix_g", "w_in", "conv_w", "conv_b", "lru_wa", "lru_ba", "lru_wx", "lru_bx", "lru_lambda", "w_proj_attn", "w_proj_lru",
             "w_out", "norm_mlp_g", "w_up", "w_down", "norm_final_g"]
    small_names = ["norm_mix_g", "conv_b", "lru_wa", "lru_ba", "lru_wx", "lru_bx", "lru_lambda", "norm_mlp_g", "norm_final_g"]
    big_names = ["w_in", "w_proj_attn", "w_proj_lru", "w_out", "w_up", "w_down"]
    res = {"conv_w": (g_cw, cw_delta, cw_m, cw_v)}
    for i, nm in enumerate(small_names):
        res[nm] = (s_grad[i], s_delta[i], s_m[i], s_v[i])
    for i, nm in enumerate(big_names):
        res[nm] = tuple(t[None] for t in big_out[i])
    return (loss, grad_x[None], *[res[nm][0] for nm in names], *[res[nm][1] for nm in names],
            *[res[nm][2] for nm in names], *[res[nm][3] for nm in names])
```

```python
import jax
import jax.numpy as jnp
from jax import lax
from jax.experimental import pallas as pl
from jax.experimental.pallas import tpu as pltpu
from jax.experimental.pallas import tpu_sc as plsc

F32, BF16 = jnp.float32, jnp.bfloat16
MESH = pl.DeviceIdType.MESH
HBM = pl.BlockSpec(memory_space=pltpu.HBM)
N_DEV = 8
N_CHIP = 4
HEAD = 128
SPAN = 128
DILATIONS = (1, 4, 16)
CONV_TAPS = 4
LRU_C = 8.0
NORM_EPS = 1e-6
LANES = 128
SUBLANES = 8
VMEM_LIMIT = 56 * 1024 * 1024
ADAM_LR, ADAM_B1, ADAM_B2, ADAM_EPS, ADAM_WD, ADAM_STEP = 0.001, 0.9, 0.999, 1e-08, 0.01, 10
ADAM_C1 = 1.0 - ADAM_B1 ** ADAM_STEP
ADAM_C2 = 1.0 - ADAM_B2 ** ADAM_STEP
NEG = -1e30


def _params(sem=None):
    return pltpu.CompilerParams(dimension_semantics=sem, vmem_limit_bytes=VMEM_LIMIT)


def _sigmoid(v):
    return 1.0 / (1.0 + jnp.exp(-v))


def _gelu(v):
    k = 0.7978845608028654
    return 0.5 * v * (1.0 + jnp.tanh(k * (v + 0.044715 * v * v * v)))


def _gelu_grad(v):
    k = 0.7978845608028654
    t = jnp.tanh(k * (v + 0.044715 * v * v * v))
    return 0.5 * (1.0 + t) + 0.5 * v * (1.0 - t * t) * k * (1.0 + 3.0 * 0.044715 * v * v)


NN = (((1,), (0,)), ((), ()))
NT = (((1,), (1,)), ((), ()))
TN = (((0,), (0,)), ((), ()))


def _mm(name, a, a_spec, b, b_spec, dn, grid, out_shapes, out_specs, acc_block, epilogue=None, extras=(), extra_specs=(), side=None):
    nk, ne, no = grid[2], len(extras), len(out_shapes)
    side_ops, side_shapes, side_copies, make_copies = side if side is not None else ((), (), 0, None)
    ns_in, ns_out = len(side_ops), len(side_shapes)

    def body(*refs):
        a_ref, b_ref = refs[0], refs[1]
        ex, side_in = refs[2:2 + ne], refs[2 + ne:2 + ne + ns_in]
        outs = refs[2 + ne + ns_in:2 + ne + ns_in + no]
        side_out = refs[2 + ne + ns_in + no:2 + ne + ns_in + no + ns_out]
        scratch = refs[2 + ne + ns_in + no + ns_out:]
        at = [pl.program_id(ax) for ax in range(3)]
        if side is not None:
            @pl.when((at[0] == 0) & (at[1] == 0) & (at[2] == 0))
            def _():
                for cp in make_copies(side_in, side_out, scratch[-2], scratch[-1]):
                    cp.start()

        part = lax.dot_general(a_ref[...], b_ref[...], dn, preferred_element_type=F32)

        def finish(acc):
            vals = epilogue(acc, *[e[...] for e in ex]) if epilogue is not None else (acc,)
            for o, v in zip(outs, vals):
                o[...] = v.astype(o.dtype)

        if nk == 1:
            finish(part)
        else:
            acc_ref, k = scratch[0], at[2]

            @pl.when(k == 0)
            def _():
                acc_ref[...] = part

            @pl.when(k > 0)
            def _():
                acc_ref[...] += part

            @pl.when(k == nk - 1)
            def _():
                finish(acc_ref[...])

        if side is not None:
            @pl.when((at[0] == grid[0] - 1) & (at[1] == grid[1] - 1) & (at[2] == grid[2] - 1))
            def _():
                for cp in make_copies(side_in, side_out, scratch[-2], scratch[-1]):
                    cp.wait()

    scratch_shapes = [pltpu.VMEM(acc_block, F32)] if nk > 1 else []
    if side is not None:
        scratch_shapes += [pltpu.SemaphoreType.DMA((side_copies,)), pltpu.SemaphoreType.DMA((side_copies,))]
    res = pl.pallas_call(
        body, out_shape=[*out_shapes, *side_shapes], grid=grid, in_specs=[a_spec, b_spec, *extra_specs, *[HBM] * ns_in],
        out_specs=[*out_specs, *[HBM] * ns_out], scratch_shapes=scratch_shapes,
        compiler_params=_params(("arbitrary",) * 3 if side is not None else ("parallel", "parallel", "arbitrary")),
        name=name)(a, b, *extras, *side_ops)
    return res if side is None else (res[:no], res[no:])


def _blk(n, pref):
    return pref if n % pref == 0 else n


def _kblk(k):
    return k if k <= 2048 else next(b for b in (2048, 1024, 512) if k % b == 0)


def _mm_fwd(name, a, w, col0, ncols, out_dtypes, epilogue=None, extras=(), extra_specs_fn=None, seg_out=None, bm=1024, bn=1024):
    m, k = a.shape
    bm, bn = _blk(m, bm), _blk(ncols, bn)
    bk = _kblk(k)
    nk = k // bk
    cb0 = col0 // bn
    grid = (m // bm, ncols // bn, nk)
    a_spec = pl.BlockSpec((bm, bk), lambda i, j, kk: (i, kk))
    b_spec = pl.BlockSpec((bk, bn), lambda i, j, kk: (kk, cb0 + j))
    if seg_out is None:
        shapes = [jax.ShapeDtypeStruct((m, ncols), dt) for dt in out_dtypes]
        specs = [pl.BlockSpec((bm, bn), lambda i, j, kk: (i, j)) for _ in out_dtypes]
    else:
        per = seg_out // bn
        shapes = [jax.ShapeDtypeStruct((ncols // seg_out, m, seg_out), dt) for dt in out_dtypes]
        specs = [pl.BlockSpec((None, bm, bn), lambda i, j, kk: (j // per, i, j % per)) for _ in out_dtypes]
    ex_specs = extra_specs_fn(bm, bn) if extra_specs_fn else ()
    return _mm(name, a, a_spec, w, b_spec, NN, grid, shapes, specs, (bm, bn), epilogue, extras, ex_specs)


def _mm_nt(name, a, w, out_dtypes, epilogue=None, extras=(), extra_specs_fn=None, part=(0, 1), side=None, bm=1024, bn=1024):
    n = w.shape[0]
    if a.ndim == 3:
        seg_cols, m, k = a.shape[2], a.shape[1], a.shape[0] * a.shape[2]
    else:
        m, k = a.shape
    m = m // part[1]
    bm, bn = _blk(m, bm), _blk(n, bn)
    bk = _kblk(k)
    grid = (m // bm, n // bn, k // bk)
    i0 = part[0] * (m // bm)
    if a.ndim == 3:
        per = seg_cols // bk
        a_spec = pl.BlockSpec((None, bm, bk), lambda i, j, kk: (kk // per, i0 + i, kk % per))
    else:
        a_spec = pl.BlockSpec((bm, bk), lambda i, j, kk: (i0 + i, kk))
    b_spec = pl.BlockSpec((bn, bk), lambda i, j, kk: (j, kk))
    shapes = [jax.ShapeDtypeStruct((m, n), dt) for dt in out_dtypes]
    specs = [pl.BlockSpec((bm, bn), lambda i, j, kk: (i, j)) for _ in out_dtypes]
    ex_specs = extra_specs_fn(bm, bn) if extra_specs_fn else ()
    return _mm(name, a, a_spec, w, b_spec, NT, grid, shapes, specs, (bm, bn), epilogue, extras, ex_specs, side)


def _mm_tn(name, a, b, part=(0, 1), side=None, bm=1024, bn=2048):
    t, m = a.shape
    n = b.shape[1] if b.ndim == 2 else b.shape[0] * b.shape[2]
    m = m // part[1]
    bm, bn = _blk(m, bm), _blk(n, bn)
    grid = (m // bm, n // bn, 1)
    i0 = part[0] * (m // bm)
    a_spec = pl.BlockSpec((t, bm), lambda i, j, kk: (0, i0 + i))
    if b.ndim == 3:
        per = b.shape[2] // bn
        b_spec = pl.BlockSpec((None, t, bn), lambda i, j, kk: (j // per, 0, j % per))
    else:
        b_spec = pl.BlockSpec((t, bn), lambda i, j, kk: (0, j))
    res = _mm(name, a, a_spec, b, b_spec, TN, grid, [jax.ShapeDtypeStruct((m, n), BF16)],
              [pl.BlockSpec((bm, bn), lambda i, j, kk: (i, j))], (bm, bn), side=side)
    return res[0] if side is None else (res[0][0], res[1])


ROWS = 256


def _row_spec(d):
    return pl.BlockSpec((ROWS, d), lambda i: (i, 0))


def _vec_spec(d, rows=1):
    return pl.BlockSpec((rows, d), lambda i: (0, 0))


def _rms_fwd(name, x, g):
    s, d = x.shape

    def body(x_ref, g_ref, o_ref):
        xv = x_ref[...]
        r = lax.rsqrt(jnp.mean(xv * xv, axis=-1, keepdims=True) + NORM_EPS)
        o_ref[...] = (xv * r * g_ref[...]).astype(BF16)

    return pl.pallas_call(body, out_shape=jax.ShapeDtypeStruct((s, d), BF16), grid=(s // ROWS,),
                          in_specs=[_row_spec(d), _vec_spec(d)], out_specs=_row_spec(d),
                          compiler_params=_params(("parallel",)), name=name)(x, g)


def _rms_bwd_math(xv, g, dy):
    r = lax.rsqrt(jnp.mean(xv * xv, axis=-1, keepdims=True) + NORM_EPS)
    n = xv * r
    z = dy * g
    dx = r * (z - n * jnp.mean(z * n, axis=-1, keepdims=True))
    return dx, jnp.sum(dy * n, axis=0, keepdims=True)


def _rms_bwd(name, x, g, dy, resid):
    s, d = x.shape

    def body(x_ref, g_ref, dy_ref, r_ref, dx_ref, dxb_ref, dg_ref):
        dx, dg = _rms_bwd_math(x_ref[...], g_ref[...], dy_ref[...])
        dx = dx + r_ref[...]
        dx_ref[...] = dx
        dxb_ref[...] = dx.astype(BF16)

        @pl.when(pl.program_id(0) == 0)
        def _():
            dg_ref[...] = jnp.zeros_like(dg_ref)

        dg_ref[...] += dg

    return pl.pallas_call(
        body, out_shape=[jax.ShapeDtypeStruct((s, d), F32), jax.ShapeDtypeStruct((s, d), BF16), jax.ShapeDtypeStruct((1, d), F32)],
        grid=(s // ROWS,), in_specs=[_row_spec(d), _vec_spec(d), _row_spec(d), _row_spec(d)],
        out_specs=[_row_spec(d), _row_spec(d), _vec_spec(d)], compiler_params=_params(("arbitrary",)), name=name)(x, g, dy, resid)


def _final_loss(h2, tgt, g):
    s, d = h2.shape

    def body(x_ref, t_ref, g_ref, dx_ref, dxb_ref, dg_ref, ls_ref):
        xv, gv = x_ref[...], g_ref[...]
        r = lax.rsqrt(jnp.mean(xv * xv, axis=-1, keepdims=True) + NORM_EPS)
        diff = xv * r * gv - t_ref[...]
        dx, dg = _rms_bwd_math(xv, gv, diff * (1.0 / d))
        dx_ref[...] = dx
        dxb_ref[...] = dx.astype(BF16)

        @pl.when(pl.program_id(0) == 0)
        def _():
            dg_ref[...] = jnp.zeros_like(dg_ref)
            ls_ref[...] = jnp.zeros_like(ls_ref)

        dg_ref[...] += dg
        ls_ref[...] += jnp.sum(diff * diff, axis=0, keepdims=True)

    return pl.pallas_call(
        body, out_shape=[jax.ShapeDtypeStruct((s, d), F32), jax.ShapeDtypeStruct((s, d), BF16),
                         jax.ShapeDtypeStruct((1, d), F32), jax.ShapeDtypeStruct((1, d), F32)],
        grid=(s // ROWS,), in_specs=[_row_spec(d), _row_spec(d), _vec_spec(d)],
        out_specs=[_row_spec(d), _row_spec(d), _vec_spec(d), _vec_spec(d)],
        compiler_params=_params(("arbitrary",)), name="final_norm_loss")(h2, tgt, g)


ATTN_Q = 128


ATTN_BATCH = 8


def _attn_units(s):
    units = []
    for gi, d in enumerate(DILATIONS):
        for r in range(d):
            for q0 in range(0, s // d, ATTN_Q):
                k0 = max(q0 - SPAN, 0)
                units.append((gi, d, r, q0, k0, q0 + ATTN_Q - k0))
    return units


def _stream_rows(d, r, start, size):
    return pl.ds(r + start * d, size) if d == 1 else pl.ds(r + start * d, size, stride=d)


def _attn_scores(q_ref, k_ref, slope, d, r, q0, k0, nk):
    qrows, krows = _stream_rows(d, r, q0, ATTN_Q), _stream_rows(d, r, k0, nk)
    qb = q_ref[qrows, :].astype(BF16)
    kb = k_ref[krows, :].astype(BF16)
    sc = lax.dot_general(qb, kb, NT, preferred_element_type=F32) * (HEAD ** -0.5)
    qi = lax.broadcasted_iota(jnp.int32, (ATTN_Q, nk), 0)
    kj = lax.broadcasted_iota(jnp.int32, (ATTN_Q, nk), 1)
    dist = (q0 - k0) + qi - kj
    valid = (dist >= 0) & (dist <= SPAN)
    sc = sc - (slope * d) * dist.astype(F32)
    return jnp.where(valid, sc, NEG), valid, qb, kb, qrows, krows


def _attn_fwd(proj, dm, slopes):
    s = proj.shape[0]
    units = _attn_units(s)

    def body(sl_ref, q_ref, k_ref, v_ref, att_ref, lse_ref, *scr):
        o_scr, l_scr = scr[:3], scr[3:]
        slope = sl_ref[pl.program_id(0)]
        for first in range(0, len(units), ATTN_BATCH):
            batch = units[first:first + ATTN_BATCH]
            scored = [_attn_scores(q_ref, k_ref, slope, d, r, q0, k0, nk) for _, d, r, q0, k0, nk in batch]
            soft = []
            for sc, _, _, _, _, _ in scored:
                m = jnp.max(sc, axis=-1, keepdims=True)
                p = jnp.exp(sc - m)
                soft.append((m, p, jnp.sum(p, axis=-1, keepdims=True)))
            outs = [lax.dot_general(p.astype(BF16), v_ref[sco[5], :].astype(BF16), NN, preferred_element_type=F32)
                    for (m, p, l), sco in zip(soft, scored)]
            for (gi, *_), (m, p, l), sco, o in zip(batch, soft, scored, outs):
                o_scr[gi][sco[4], :] = o / l
                l_scr[gi][sco[4], :] = jnp.broadcast_to(m + jnp.log(l), (ATTN_Q, HEAD))
        l0, l1, l2 = l_scr[0][...], l_scr[1][...], l_scr[2][...]
        m = jnp.maximum(jnp.maximum(l0, l1), l2)
        w0, w1, w2 = jnp.exp(l0 - m), jnp.exp(l1 - m), jnp.exp(l2 - m)
        tot = w0 + w1 + w2
        att_ref[...] = ((w0 * o_scr[0][...] + w1 * o_scr[1][...] + w2 * o_scr[2][...]) / tot).astype(BF16)
        lse_ref[...] = m + jnp.log(tot)

    def seg(i):
        return pl.BlockSpec((s, HEAD), lambda h: (0, i * (dm // HEAD) + h))

    col = pl.BlockSpec((s, HEAD), lambda h: (0, h))
    return pl.pallas_call(
        body, out_shape=[jax.ShapeDtypeStruct((s, dm), BF16), jax.ShapeDtypeStruct((s, dm), F32)], grid=(dm // HEAD,),
        in_specs=[pl.BlockSpec(memory_space=pltpu.SMEM), seg(0), seg(1), seg(2)], out_specs=[col, col],
        scratch_shapes=[pltpu.VMEM((s, HEAD), F32)] * (2 * len(DILATIONS)),
        compiler_params=_params(("parallel",)), name="attn_fwd")(slopes, proj, proj, proj)


def _attn_bwd(proj, dm, datt, att, lse, slopes, others):
    s = proj.shape[0]
    units = _attn_units(s)

    def body(sl_ref, q_ref, k_ref, v_ref, do_ref, att_ref, lse_ref, o3, o4, o5, o6, out_ref, dq_scr, dk_scr, dv_scr, dl_scr):
        slope = sl_ref[pl.program_id(0)]
        delta = jnp.sum(do_ref[...] * att_ref[...].astype(F32), axis=-1, keepdims=True)
        dl_scr[...] = jnp.broadcast_to(delta, (s, HEAD))
        dq_scr[...] = jnp.zeros_like(dq_scr)
        dk_scr[...] = jnp.zeros_like(dk_scr)
        dv_scr[...] = jnp.zeros_like(dv_scr)
        for first in range(0, len(units), ATTN_BATCH):
            scored = [_attn_scores(q_ref, k_ref, slope, d, r, q0, k0, nk) for _, d, r, q0, k0, nk in units[first:first + ATTN_BATCH]]
            dobs = [do_ref[sco[4], :].astype(BF16) for sco in scored]
            dps = [lax.dot_general(dob, v_ref[sco[5], :].astype(BF16), NT, preferred_element_type=F32) for dob, sco in zip(dobs, scored)]
            ps = [jnp.where(sco[1], jnp.exp(sco[0] - lse_ref[sco[4], :][:, 0:1]), 0.0) for sco in scored]
            dss = [(p * (dp - dl_scr[sco[4], :][:, 0:1]) * (HEAD ** -0.5)).astype(BF16) for p, dp, sco in zip(ps, dps, scored)]
            dqs = [lax.dot_general(ds, sco[3], NN, preferred_element_type=F32) for ds, sco in zip(dss, scored)]
            dks = [lax.dot_general(ds, sco[2], TN, preferred_element_type=F32) for ds, sco in zip(dss, scored)]
            dvs = [lax.dot_general(p.astype(BF16), dob, TN, preferred_element_type=F32) for p, dob in zip(ps, dobs)]
            for sco, dq, dk, dv in zip(scored, dqs, dks, dvs):
                dq_scr[sco[4], :] += dq
                dk_scr[sco[5], :] += dk
                dv_scr[sco[5], :] += dv
        for j, scr in enumerate((dq_scr, dk_scr, dv_scr)):
            out_ref[j] = scr[...].astype(BF16)
        for j, other in enumerate((o3, o4, o5, o6)):
            out_ref[3 + j] = other[...]

    def seg(i):
        return pl.BlockSpec((s, HEAD), lambda h: (0, i * (dm // HEAD) + h))

    col = pl.BlockSpec((s, HEAD), lambda h: (0, h))
    return pl.pallas_call(
        body, out_shape=jax.ShapeDtypeStruct((7, s, dm), BF16), grid=(dm // HEAD,),
        in_specs=[pl.BlockSpec(memory_space=pltpu.SMEM), seg(0), seg(1), seg(2), col, col, col, col, col, col, col],
        out_specs=pl.BlockSpec((7, s, HEAD), lambda h: (0, 0, h)), scratch_shapes=[pltpu.VMEM((s, HEAD), F32)] * 4,
        compiler_params=_params(("parallel",)), name="attn_bwd")(slopes, proj, proj, proj, datt, att, lse, *others)


VEC_CB, VEC_BA, VEC_BX, VEC_LAM = 0, 1, 2, 3
SEG_Q, SEG_K, SEG_V, SEG_X, SEG_GATE, SEG_GA, SEG_GL = range(7)


def _softplus(z):
    return jnp.maximum(z, 0.0) + jnp.log1p(jnp.exp(-jnp.abs(z)))


def _gate_math(xc, wa_ref, wx_ref, vec):
    xcb = xc.astype(BF16)
    nh = xc.shape[1] // HEAD
    pre_a = jnp.concatenate([jnp.dot(xcb[:, h * HEAD:(h + 1) * HEAD], wa_ref[h], preferred_element_type=F32) for h in range(nh)], axis=1)
    pre_x = jnp.concatenate([jnp.dot(xcb[:, h * HEAD:(h + 1) * HEAD], wx_ref[h], preferred_element_type=F32) for h in range(nh)], axis=1)
    ra = _sigmoid(pre_a + vec[VEC_BA:VEC_BA + 1])
    ig = _sigmoid(pre_x + vec[VEC_BX:VEC_BX + 1])
    sp = _softplus(-vec[VEC_LAM:VEC_LAM + 1])
    log_a = -LRU_C * ra * sp
    a = jnp.exp(log_a)
    z = 2.0 * log_a
    one_minus_a2 = jnp.where(z > -0.01, -z * (1.0 + z * (0.5 + z * (1.0 / 6.0))), 1.0 - jnp.exp(z))
    mult = jnp.sqrt(one_minus_a2)
    return dict(xcb=xcb, ra=ra, ig=ig, sp=sp, a=a, mult=mult)


def _conv_pad_prev(pad_ref, cur, halo, first):
    pad_ref[0:SUBLANES, :] = jnp.where(first, 0.0, halo)
    pad_ref[SUBLANES:SUBLANES + cur.shape[0], :] = cur


def _shift_rows(x, s, fill, up=False):
    rid = lax.broadcasted_iota(jnp.int32, x.shape, 0)
    if up:
        return jnp.where(rid < SUBLANES - s, pltpu.roll(x, SUBLANES - s, axis=0), fill)
    return jnp.where(rid >= s, pltpu.roll(x, s, axis=0), fill)


def _lru_fwd(proj, d, cw8, vec8, wa, wx):
    s = proj.shape[0]
    hb = ROWS // SUBLANES

    def body(x_ref, halo_ref, g_ref, cw_ref, vec_ref, wa_ref, wx_ref, xc_ref, hp_ref, h2_ref, y_ref, pad, a_scr, u_scr, carry):
        @pl.when(pl.program_id(0) == 0)
        def _():
            carry[...] = jnp.zeros_like(carry)

        _conv_pad_prev(pad, x_ref[...], halo_ref[...], pl.program_id(0) == 0)
        vec = vec_ref[...]
        xc = vec[VEC_CB:VEC_CB + 1]
        for k in range(CONV_TAPS):
            xc = xc + cw_ref[k:k + 1, :] * pad[pl.ds(SUBLANES - (CONV_TAPS - 1) + k, ROWS), :]
        gm = _gate_math(xc, wa_ref, wx_ref, vec)
        xc_ref[...] = xc
        a_scr[...] = gm["a"]
        u_scr[...] = gm["mult"] * (gm["ig"] * xc)

        def group(gi, before):
            rows = pl.ds(pl.multiple_of(gi * SUBLANES, SUBLANES), SUBLANES)
            ca, cb = a_scr[rows, :], u_scr[rows, :]
            for sh in (1, 2, 4):
                cb = ca * _shift_rows(cb, sh, 0.0) + cb
                ca = ca * _shift_rows(ca, sh, 1.0)
            h = cb + ca * before
            hp_ref[rows, :] = jnp.where(lax.broadcasted_iota(jnp.int32, h.shape, 0) == 0, before, pltpu.roll(h, 1, axis=0))
            h2_ref[rows, :] = h.astype(BF16)
            y_ref[rows, :] = (h * _gelu(g_ref[rows, :])).astype(BF16)
            return jnp.broadcast_to(h[SUBLANES - 1:SUBLANES, :], h.shape)

        carry[...] = lax.fori_loop(0, ROWS // SUBLANES, group, carry[...])

    wspec = pl.BlockSpec(wa.shape, lambda i: (0, 0, 0))
    return pl.pallas_call(
        body, out_shape=[jax.ShapeDtypeStruct((s, d), F32)] * 2 + [jax.ShapeDtypeStruct((s, d), BF16)] * 2, grid=(s // ROWS,),
        in_specs=[pl.BlockSpec((ROWS, d), lambda i: (i, SEG_X)),
                  pl.BlockSpec((SUBLANES, d), lambda i: (jnp.maximum(i * hb - 1, 0), SEG_X)),
                  pl.BlockSpec((ROWS, d), lambda i: (i, SEG_GATE)),
                  _vec_spec(d, SUBLANES), _vec_spec(d, SUBLANES), wspec, wspec],
        out_specs=[_row_spec(d)] * 4,
        scratch_shapes=[pltpu.VMEM((ROWS + SUBLANES, d), F32), pltpu.VMEM((ROWS, d), F32), pltpu.VMEM((ROWS, d), F32), pltpu.VMEM((SUBLANES, d), F32)],
        compiler_params=_params(("arbitrary",)), name="lru_fwd")(proj, proj, proj, cw8, vec8, wa, wx)


def _lru_bwd(hp, dh, xc, wa, wx, vec8):
    s, d = xc.shape
    nh = d // HEAD
    nb = s // ROWS

    def body(hp_ref, dh_ref, xc_ref, wa_ref, wx_ref, vec_ref, dxc_ref, dwa_ref, dwx_ref, dvec_ref, a_scr, g_scr, da_scr, carry):
        @pl.when(pl.program_id(0) == 0)
        def _():
            carry[...] = jnp.zeros_like(carry)
            dwa_ref[...] = jnp.zeros_like(dwa_ref)
            dwx_ref[...] = jnp.zeros_like(dwx_ref)
            dvec_ref[...] = jnp.zeros_like(dvec_ref)

        xc_v, vec = xc_ref[...], vec_ref[...]
        gm = _gate_math(xc_v, wa_ref, wx_ref, vec)
        ra, ig, sp, a, mult = gm["ra"], gm["ig"], gm["sp"], gm["a"], gm["mult"]
        a_scr[...] = a

        def group(j, after):
            rows = pl.ds(pl.multiple_of((ROWS // SUBLANES - 1 - j) * SUBLANES, SUBLANES), SUBLANES)
            ca, dhv = a_scr[rows, :], dh_ref[rows, :]
            cb = ca * dhv
            for sh in (1, 2, 4):
                cb = ca * _shift_rows(cb, sh, 0.0, up=True) + cb
                ca = ca * _shift_rows(ca, sh, 1.0, up=True)
            c = cb + ca * after
            last = lax.broadcasted_iota(jnp.int32, c.shape, 0) == SUBLANES - 1
            g = dhv + jnp.where(last, after, pltpu.roll(c, SUBLANES - 1, axis=0))
            g_scr[rows, :] = g
            da_scr[rows, :] = g * hp_ref[rows, :]
            return jnp.broadcast_to(c[0:1, :], c.shape)

        carry[...] = lax.fori_loop(0, ROWS // SUBLANES, group, carry[...])
        du, da = g_scr[...], da_scr[...]
        dmult = du * ig * xc_v
        dlog_a = da * a - dmult * (a * a) / mult
        dpre_a = dlog_a * (-LRU_C * sp) * ra * (1.0 - ra)
        dpre_x = du * mult * xc_v * ig * (1.0 - ig)
        dlam = jnp.sum(dlog_a * (-LRU_C * ra), axis=0, keepdims=True) * (-_sigmoid(-vec[VEC_LAM:VEC_LAM + 1]))
        dvec_ref[VEC_BA:VEC_BA + 1, :] += jnp.sum(dpre_a, axis=0, keepdims=True)
        dvec_ref[VEC_BX:VEC_BX + 1, :] += jnp.sum(dpre_x, axis=0, keepdims=True)
        dvec_ref[VEC_LAM:VEC_LAM + 1, :] += dlam
        dab, dxb, xcb = dpre_a.astype(BF16), dpre_x.astype(BF16), gm["xcb"]
        back = []
        for h in range(nh):
            cols = slice(h * HEAD, (h + 1) * HEAD)
            dwa_ref[h] += lax.dot_general(xcb[:, cols], dab[:, cols], TN, preferred_element_type=F32)
            dwx_ref[h] += lax.dot_general(xcb[:, cols], dxb[:, cols], TN, preferred_element_type=F32)
            back.append(lax.dot_general(dab[:, cols], wa_ref[h], NT, preferred_element_type=F32)
                        + lax.dot_general(dxb[:, cols], wx_ref[h], NT, preferred_element_type=F32))
        dxc_ref[...] = du * mult * ig + jnp.concatenate(back, axis=1)

    rows_rev = pl.BlockSpec((ROWS, d), lambda i: (nb - 1 - i, 0))
    wspec = pl.BlockSpec(wa.shape, lambda i: (0, 0, 0))
    return pl.pallas_call(
        body, out_shape=[jax.ShapeDtypeStruct((s, d), F32), jax.ShapeDtypeStruct(wa.shape, F32), jax.ShapeDtypeStruct(wa.shape, F32),
                         jax.ShapeDtypeStruct((SUBLANES, d), F32)],
        grid=(nb,), in_specs=[rows_rev, rows_rev, rows_rev, wspec, wspec, _vec_spec(d, SUBLANES)],
        out_specs=[rows_rev, wspec, wspec, _vec_spec(d, SUBLANES)],
        scratch_shapes=[pltpu.VMEM((ROWS, d), F32)] * 3 + [pltpu.VMEM((SUBLANES, d), F32)],
        compiler_params=_params(("arbitrary",)), name="lru_bwd")(hp, dh, xc, wa, wx, vec8)


def _conv_bwd(dxc, proj, cw8):
    s, d = dxc.shape
    hb = ROWS // SUBLANES
    last = s // SUBLANES - 1

    def body(dc_ref, dnext_ref, x_ref, xprev_ref, cw_ref, dx_ref, dcw_ref, padd, padx):
        i = pl.program_id(0)

        @pl.when(i == 0)
        def _():
            dcw_ref[...] = jnp.zeros_like(dcw_ref)

        dc = dc_ref[...]
        padd[0:ROWS, :] = dc
        padd[ROWS:ROWS + SUBLANES, :] = jnp.where(i == pl.num_programs(0) - 1, 0.0, dnext_ref[...])
        _conv_pad_prev(padx, x_ref[...], xprev_ref[...], i == 0)
        dx = jnp.zeros_like(dc)
        for k in range(CONV_TAPS):
            dx = dx + cw_ref[k:k + 1, :] * padd[pl.ds(CONV_TAPS - 1 - k, ROWS), :]
            dcw_ref[k:k + 1, :] += jnp.sum(dc * padx[pl.ds(SUBLANES - (CONV_TAPS - 1) + k, ROWS), :], axis=0, keepdims=True)
        dcw_ref[CONV_TAPS:CONV_TAPS + 1, :] += jnp.sum(dc, axis=0, keepdims=True)
        dx_ref[...] = dx.astype(BF16)

    return pl.pallas_call(
        body, out_shape=[jax.ShapeDtypeStruct((s, d), BF16), jax.ShapeDtypeStruct((SUBLANES, d), F32)], grid=(s // ROWS,),
        in_specs=[_row_spec(d), pl.BlockSpec((SUBLANES, d), lambda i: (jnp.minimum((i + 1) * hb, last), 0)),
                  pl.BlockSpec((ROWS, d), lambda i: (i, SEG_X)),
                  pl.BlockSpec((SUBLANES, d), lambda i: (jnp.maximum(i * hb - 1, 0), SEG_X)), _vec_spec(d, SUBLANES)],
        out_specs=[_row_spec(d), _vec_spec(d, SUBLANES)],
        scratch_shapes=[pltpu.VMEM((ROWS + SUBLANES, d), F32), pltpu.VMEM((ROWS + SUBLANES, d), F32)],
        compiler_params=_params(("arbitrary",)), name="lru_conv_bwd")(dxc, dxc, proj, proj, cw8)


def _coords():
    return lax.axis_index("x"), lax.axis_index("y"), lax.axis_index("c")


def _other_chips(x, y):
    return [(1 - x, y), (x, 1 - y), (1 - x, 1 - y)]


def _slab(ref, kind, shard_shape, idx, half=None):
    r, c = shard_shape
    r0, nr = (0, r) if half is None else (half * (r // 2), r // 2)
    if kind == "col":
        return ref.at[pl.ds(r0, nr), pl.ds(pl.multiple_of(idx * c, LANES), c)]
    if kind == "row":
        return ref.at[pl.ds(pl.multiple_of(idx * r, SUBLANES) + r0, nr), :]
    return ref.at[idx, pl.ds(r0, nr), :]


def _full_shape(shard_shape, kind):
    r, c = shard_shape
    return {"col": (r, c * N_DEV), "row": (r * N_DEV, c), "slot": (N_DEV, r, c)}[kind]


def _handshake(peers):
    barrier = pltpu.get_barrier_semaphore()
    for peer in peers:
        pl.semaphore_signal(barrier, inc=1, device_id=peer, device_id_type=MESH)
    pl.semaphore_wait(barrier, len(peers))


def _launch(name, body, out_shape, operands, sems, sequencer_id):
    if sequencer_id is None:
        return pl.pallas_call(body, out_shape=out_shape, in_specs=[HBM] * len(operands), out_specs=[HBM] * len(out_shape),
                              scratch_shapes=sems, name=name)(*operands)
    return pl.kernel(body, out_type=out_shape, mesh=plsc.ScalarSubcoreMesh(axis_name="seq", num_cores=1), name=name,
                     scratch_types=sems, compiler_params=pltpu.CompilerParams(collective_id=sequencer_id))(*operands)


AG_COPIES = 10


def _all_gather(name, shards, kinds, sequencer_id=None):
    n = len(shards)
    shapes = [s.shape for s in shards]

    def body(*refs):
        ins, outs = refs[:n], refs[n:2 * n]
        send_sems, recv_sems, local_sems = refs[2 * n:]
        x, y, c = _coords()
        me, sib, xn, yn, dg = (x, y, c), (x, y, 1 - c), (1 - x, y, c), (x, 1 - y, c), (1 - x, 1 - y, c)
        if sequencer_id is not None:
            _handshake([sib, xn, yn])

        def part(i, dev, half=None):
            return _slab(outs[i], kinds[i], shapes[i], 4 * dev[0] + 2 * dev[1] + dev[2], half)

        def copy(i, k, block, half, to, own=False):
            r = shapes[i][0]
            src = part(i, block, half) if not own else (ins[i] if half is None else ins[i].at[pl.ds(half * (r // 2), r // 2), :])
            return pltpu.make_async_remote_copy(
                src_ref=src, dst_ref=part(i, block, half), send_sem=send_sems.at[AG_COPIES * i + k],
                recv_sem=recv_sems.at[AG_COPIES * i + k], device_id=to, device_id_type=MESH)

        def other_core(dev):
            return (dev[0], dev[1], 1 - c)

        started = []

        def start(cp):
            cp.start()
            started.append(cp)

        for i in range(n):
            start(copy(i, 1, me, 0, xn, own=True))
            start(copy(i, 4, me, 1, yn, own=True))
            start(copy(i, 2, me, 1, xn, own=True))
            start(copy(i, 3, me, 0, yn, own=True))
            start(copy(i, 0, me, None, sib, own=True))
        mine = [pltpu.make_async_copy(ins[i], part(i, me), local_sems.at[i]) for i in range(n)]
        for cp in mine:
            cp.start()
        for i in range(n):
            copy(i, 1, xn, 0, me).wait_recv()
            start(copy(i, 5, xn, 0, yn))
            copy(i, 4, yn, 1, me).wait_recv()
            start(copy(i, 6, yn, 1, xn))
        for i in range(n):
            copy(i, 2, xn, 1, me).wait_recv()
            start(copy(i, 7, xn, None, sib))
            copy(i, 3, yn, 0, me).wait_recv()
            start(copy(i, 8, yn, None, sib))
        for i in range(n):
            copy(i, 5, dg, 0, me).wait_recv()
            copy(i, 6, dg, 1, me).wait_recv()
            start(copy(i, 9, dg, None, sib))
        for i in range(n):
            copy(i, 0, sib, None, me).wait_recv()
            for k, dev in ((7, xn), (8, yn), (9, dg)):
                copy(i, k, other_core(dev), None, me).wait_recv()
        for cp in started:
            cp.wait_send()
        for cp in mine:
            cp.wait()

    out_shape = [jax.ShapeDtypeStruct(_full_shape(s.shape, k), s.dtype) for s, k in zip(shards, kinds)]
    sems = [pltpu.SemaphoreType.DMA((AG_COPIES * n,)), pltpu.SemaphoreType.DMA((AG_COPIES * n,)), pltpu.SemaphoreType.DMA((n,))]
    return _launch(name, body, out_shape, shards, sems, sequencer_id)


def _sibling_copies(kinds, shard_shapes):
    def make(ins, outs, send_sems, recv_sems):
        x, y, c = _coords()
        return [pltpu.make_async_remote_copy(
            src_ref=_slab(ins[i], kinds[i], shard_shapes[i], 2 * q + (1 - c)), dst_ref=outs[i].at[q],
            send_sem=send_sems.at[N_CHIP * i + q], recv_sem=recv_sems.at[N_CHIP * i + q],
            device_id=(x, y, 1 - c), device_id_type=MESH) for i in range(len(ins)) for q in range(N_CHIP)]
    return make


def _sibling_side(partials, kinds, shard_shapes):
    return (partials, [jax.ShapeDtypeStruct((N_CHIP, *s), BF16) for s in shard_shapes], N_CHIP * len(partials),
            _sibling_copies(kinds, shard_shapes))


def _exchange_siblings(name, partials, kinds, shard_shapes, sequencer_id=None):
    n = len(partials)
    make = _sibling_copies(kinds, shard_shapes)

    def body(*refs):
        if sequencer_id is not None:
            x, y, c = _coords()
            _handshake([(x, y, 1 - c)])
        cps = make(refs[:n], refs[n:2 * n], refs[2 * n], refs[2 * n + 1])
        for cp in cps:
            cp.start()
        for cp in cps:
            cp.wait()

    return _launch(name, body, [jax.ShapeDtypeStruct((N_CHIP, *s), BF16) for s in shard_shapes], partials,
                   [pltpu.SemaphoreType.DMA((N_CHIP * n,)), pltpu.SemaphoreType.DMA((N_CHIP * n,))], sequencer_id)


def _exchange_chips(name, chip_sums, sequencer_id=None):
    n = len(chip_sums)

    def body(*refs):
        ins, outs = refs[:n], refs[n:2 * n]
        send_sems, recv_sems = refs[2 * n:]
        x, y, c = _coords()
        if sequencer_id is not None:
            _handshake([(cx, cy, c) for cx, cy in _other_chips(x, y)])
        cps = []
        for i in range(n):
            for k, (cx, cy) in enumerate(_other_chips(x, y)):
                cps.append(pltpu.make_async_remote_copy(
                    src_ref=ins[i].at[2 * cx + cy], dst_ref=outs[i].at[k], send_sem=send_sems.at[3 * i + k],
                    recv_sem=recv_sems.at[3 * i + k], device_id=(cx, cy, c), device_id_type=MESH))
        for cp in cps:
            cp.start()
        for cp in cps:
            cp.wait()

    return _launch(name, body, [jax.ShapeDtypeStruct((3, *t.shape[1:]), BF16) for t in chip_sums], chip_sums,
                   [pltpu.SemaphoreType.DMA((3 * n,)), pltpu.SemaphoreType.DMA((3 * n,))], sequencer_id)


def _all_peers(x, y, c):
    return [(x ^ (k >> 2), y ^ ((k >> 1) & 1), c ^ (k & 1)) for k in range(1, N_DEV)]


def _small_scatter(name, packed, sequencer_id):
    rows = packed.shape[0] // N_DEV

    def body(p_ref, rb_ref, send_sems, recv_sems, local_sem):
        x, y, c = _coords()
        me = 4 * x + 2 * y + c
        peers = _all_peers(x, y, c)
        _handshake(peers)

        def piece(idx):
            return p_ref.at[pl.ds(pl.multiple_of(idx * rows, SUBLANES), rows), :]

        cps = [pltpu.make_async_remote_copy(src_ref=piece(me ^ k), dst_ref=rb_ref.at[k], send_sem=send_sems.at[k], recv_sem=recv_sems.at[k],
                                            device_id=peers[k - 1], device_id_type=MESH) for k in range(1, N_DEV)]
        for cp in cps:
            cp.start()
        mine = pltpu.make_async_copy(piece(me), rb_ref.at[0], local_sem)
        mine.start()
        for cp in cps:
            cp.wait()
        mine.wait()

    return _launch(name, body, [jax.ShapeDtypeStruct((N_DEV, rows, LANES), F32)], [packed],
                   [pltpu.SemaphoreType.DMA((N_DEV,)), pltpu.SemaphoreType.DMA((N_DEV,)), pltpu.SemaphoreType.DMA], sequencer_id)[0]


def _small_sum(name, pieces):
    def body(p_ref, o_ref):
        acc = p_ref[0]
        for k in range(1, N_DEV):
            acc = acc + p_ref[k]
        o_ref[...] = acc

    vm = pl.BlockSpec(memory_space=pltpu.VMEM)
    return pl.pallas_call(body, out_shape=jax.ShapeDtypeStruct(pieces.shape[1:], F32), in_specs=[vm], out_specs=vm, name=name)(pieces)


def _small_gather(name, tot, sequencer_id):
    rows = tot.shape[0]

    def body(t_ref, out_ref, send_sems, recv_sems, local_sem):
        x, y, c = _coords()
        me = 4 * x + 2 * y + c
        peers = _all_peers(x, y, c)
        _handshake(peers)

        def piece(idx):
            return out_ref.at[pl.ds(pl.multiple_of(idx * rows, SUBLANES), rows), :]

        cps = [pltpu.make_async_remote_copy(src_ref=t_ref, dst_ref=piece(me), send_sem=send_sems.at[k], recv_sem=recv_sems.at[k],
                                            device_id=peers[k - 1], device_id_type=MESH) for k in range(1, N_DEV)]
        for cp in cps:
            cp.start()
        mine = pltpu.make_async_copy(t_ref, piece(me), local_sem)
        mine.start()
        for cp in cps:
            cp.wait()
        mine.wait()

    return _launch(name, body, [jax.ShapeDtypeStruct((N_DEV * rows, LANES), F32)], [tot],
                   [pltpu.SemaphoreType.DMA((N_DEV,)), pltpu.SemaphoreType.DMA((N_DEV,)), pltpu.SemaphoreType.DMA], sequencer_id)[0]


def _all_reduce_small(name, packed):
    rows = packed.shape[0] // N_DEV

    def body(p_ref, out_ref, rb, tot, send_sems, recv_sems):
        x, y, c = _coords()
        me = 4 * x + 2 * y + c

        def peer(k):
            return (x ^ (k >> 2), y ^ ((k >> 1) & 1), c ^ (k & 1))

        def rows_of(idx):
            return pl.ds(pl.multiple_of(idx * rows, SUBLANES), rows)

        def piece(ref, idx):
            return ref.at[rows_of(idx), :]

        scatter = [pltpu.make_async_remote_copy(src_ref=piece(p_ref, me ^ k), dst_ref=rb.at[k], send_sem=send_sems.at[k],
                                                recv_sem=recv_sems.at[k], device_id=peer(k), device_id_type=MESH) for k in range(1, N_DEV)]
        for cp in scatter:
            cp.start()
        acc = p_ref[rows_of(me), :]
        for cp in scatter:
            cp.wait_recv()
        for k in range(1, N_DEV):
            acc = acc + rb[k]
        tot[...] = acc
        out_ref[rows_of(me), :] = acc
        gather = [pltpu.make_async_remote_copy(src_ref=tot, dst_ref=piece(out_ref, me), send_sem=send_sems.at[N_DEV + k],
                                               recv_sem=recv_sems.at[N_DEV + k], device_id=peer(k), device_id_type=MESH)
                  for k in range(1, N_DEV)]
        for cp in gather:
            cp.start()
        for k in range(1, N_DEV):
            pltpu.make_async_remote_copy(src_ref=tot, dst_ref=piece(out_ref, me ^ k), send_sem=send_sems.at[N_DEV + k],
                                         recv_sem=recv_sems.at[N_DEV + k], device_id=peer(k), device_id_type=MESH).wait_recv()
        for cp in scatter + gather:
            cp.wait_send()

    vm = pl.BlockSpec(memory_space=pltpu.VMEM)
    return pl.pallas_call(
        body, out_shape=jax.ShapeDtypeStruct(packed.shape, F32), in_specs=[vm], out_specs=vm,
        scratch_shapes=[pltpu.VMEM((N_DEV, rows, LANES), F32), pltpu.VMEM((rows, LANES), F32),
                        pltpu.SemaphoreType.DMA((2 * N_DEV,)), pltpu.SemaphoreType.DMA((2 * N_DEV,))],
        compiler_params=pltpu.CompilerParams(vmem_limit_bytes=VMEM_LIMIT), name=name)(packed)


def _adamw_math(g, w, m, v):
    m = ADAM_B1 * m + (1.0 - ADAM_B1) * g
    v = ADAM_B2 * v + (1.0 - ADAM_B2) * (g * g)
    delta = -ADAM_LR * ((m / ADAM_C1) / (jnp.sqrt(v / ADAM_C2) + ADAM_EPS) + ADAM_WD * w)
    return delta, m, v


def _slab_spec(kind, shard_shape, tr, slab_of):
    r, c = shard_shape
    if kind == "col":
        return pl.BlockSpec((tr, c), lambda q, i, sc: (i, slab_of(q, sc)))
    return pl.BlockSpec((tr, c), lambda q, i, sc: (slab_of(q, sc) * (r // tr) + i, 0))


def _chip_sum(name, partial, recv, kind, shard_shape, core):
    r, c = shard_shape
    tr = _blk(r, 1024)

    def body(core_ref, p_ref, r_ref, o_ref):
        o_ref[...] = (p_ref[...].astype(F32) + r_ref[...].astype(F32)).astype(BF16)

    spec4 = pl.BlockSpec((None, tr, c), lambda q, i, sc: (q, i, 0))
    grid_spec = pltpu.PrefetchScalarGridSpec(
        num_scalar_prefetch=1, grid=(N_CHIP, r // tr),
        in_specs=[_slab_spec(kind, shard_shape, tr, lambda q, sc: 2 * q + sc[0]), spec4], out_specs=spec4)
    return pl.pallas_call(body, out_shape=jax.ShapeDtypeStruct((N_CHIP, r, c), BF16), grid_spec=grid_spec,
                          compiler_params=_params(("parallel", "parallel")), name=name)(core, partial, recv)


def _adamw_shard(name, parts, w, m, v, chip):
    r, c = w.shape
    n_parts = len(parts)
    tr = _blk(r // n_parts, 256)
    per = r // n_parts // tr

    def body(chip_ref, *refs):
        src, (w_ref, m_ref, v_ref), (g_out, d_out, m_out, v_out) = refs[:2 * n_parts], refs[2 * n_parts:2 * n_parts + 3], refs[2 * n_parts + 3:]
        for p in range(n_parts):
            @pl.when(pl.program_id(0) // per == p)
            def _():
                g = src[2 * p][...].astype(F32)
                for k in range(3):
                    g = g + src[2 * p + 1][k].astype(F32)
                g_out[...] = g
                d_out[...], m_out[...], v_out[...] = _adamw_math(g, w_ref[...], m_ref[...], v_ref[...])

    def part_specs(p):
        at = lambda i: jnp.clip(i - p * per, 0, per - 1)
        return [pl.BlockSpec((None, tr, c), lambda i, sc: (sc[0], at(i), 0)), pl.BlockSpec((3, tr, c), lambda i, sc: (0, at(i), 0))]

    blk = pl.BlockSpec((tr, c), lambda i, sc: (i, 0))
    grid_spec = pltpu.PrefetchScalarGridSpec(
        num_scalar_prefetch=1, grid=(r // tr,), in_specs=[s for p in range(n_parts) for s in part_specs(p)] + [blk, blk, blk], out_specs=[blk] * 4)
    return pl.pallas_call(body, out_shape=[jax.ShapeDtypeStruct((r, c), F32)] * 4, grid_spec=grid_spec,
                          compiler_params=_params(("parallel",)), name=name)(chip, *[a for p in parts for a in p], w, m, v)


def _adamw_small(name, g, w, m, v):
    def body(g_ref, w_ref, m_ref, v_ref, d_out, m_out, v_out):
        d_out[...], m_out[...], v_out[...] = _adamw_math(g_ref[...], w_ref[...], m_ref[...], v_ref[...])

    vm = pl.BlockSpec(memory_space=pltpu.VMEM)
    return pl.pallas_call(body, out_shape=[jax.ShapeDtypeStruct(g.shape, F32)] * 3, in_specs=[vm] * 4, out_specs=[vm] * 3,
                          compiler_params=pltpu.CompilerParams(vmem_limit_bytes=VMEM_LIMIT), name=name)(g, w, m, v)


def _pack_rows(arrays, total_rows):
    flat = [a.reshape(-1, LANES) for a in arrays]
    used = sum(f.shape[0] for f in flat)
    return jnp.concatenate(flat + [jnp.zeros((total_rows - used, LANES), F32)], axis=0)


def _unpack_rows(packed, like):
    out, at = [], 0
    for a in like:
        n = a.size // LANES
        out.append(packed[at:at + n].reshape(a.shape))
        at += n
    return out


def kernel(x, norm_mix_g, w_in, conv_w, conv_b, lru_wa, lru_ba, lru_wx, lru_bx, lru_lambda, w_proj_attn, w_proj_lru, w_out, norm_mlp_g, w_up, w_down, norm_final_g, loss_target, m_norm_mix_g, m_w_in, m_conv_w, m_conv_b, m_lru_wa, m_lru_ba, m_lru_wx, m_lru_bx, m_lru_lambda, m_w_proj_attn, m_w_proj_lru, m_w_out, m_norm_mlp_g, m_w_up, m_w_down, m_norm_final_g, v_norm_mix_g, v_w_in, v_conv_w, v_conv_b, v_lru_wa, v_lru_ba, v_lru_wx, v_lru_bx, v_lru_lambda, v_w_proj_attn, v_w_proj_lru, v_w_out, v_norm_mlp_g, v_w_up, v_w_down, v_norm_final_g):
    xs, tgt = x[0], loss_target[0]
    s, d = xs.shape
    nh = d // HEAD
    ix, iy, ic = _coords()
    core = jnp.reshape(ic, (1,)).astype(jnp.int32)
    chip = jnp.reshape(2 * ix + iy, (1,)).astype(jnp.int32)
    dev = 4 * ix + 2 * iy + ic

    big = [w_in[0], w_proj_attn[0], w_proj_lru[0], w_out[0], w_up[0], w_down[0]]
    big_m = [m_w_in[0], m_w_proj_attn[0], m_w_proj_lru[0], m_w_out[0], m_w_up[0], m_w_down[0]]
    big_v = [v_w_in[0], v_w_proj_attn[0], v_w_proj_lru[0], v_w_out[0], v_w_up[0], v_w_down[0]]
    kinds = ["col", "row", "row", "row", "col", "row"]
    pad_taps = lambda t: jnp.pad(t, ((0, SUBLANES - CONV_TAPS), (0, 0)))
    shards = [w.astype(BF16) for w in big]
    pad_taps2 = lambda t: jnp.pad(t, ((0, 2 * SUBLANES - CONV_TAPS), (0, 0)))
    win, cw_slots = _all_gather("all_gather_w_in", [shards[0], pad_taps2(conv_w[0])], ["col", "slot"], sequencer_id=7)
    later = lax.optimization_barrier((shards[1:], win))[0]
    wpa, wpl, wout = _all_gather("all_gather_mix", later[:3], kinds[1:4], sequencer_id=1)
    wup, wdown = _all_gather("all_gather_mlp", later[3:], kinds[4:], sequencer_id=5)
    cw8 = jnp.transpose(cw_slots[:, :SUBLANES], (1, 0, 2)).reshape(SUBLANES, d)
    row_id = lax.broadcasted_iota(jnp.int32, (SUBLANES, d), 0)
    vec8 = sum(jnp.where(row_id == k, t, 0.0) for k, t in ((VEC_CB, conv_b), (VEC_BA, lru_ba), (VEC_BX, lru_bx), (VEC_LAM, lru_lambda)))
    wa16, wx16 = lru_wa[0].astype(BF16), lru_wx[0].astype(BF16)
    slopes = 2.0 ** (-8.0 * jnp.arange(1, nh + 1, dtype=F32) / nh)

    def seg_specs(*segs):
        return lambda bm, bn: [pl.BlockSpec((bm, bn), (lambda i, j, kk, sg=sg: (i, sg * (d // bn) + j))) for sg in segs]

    def plain_specs(k):
        return lambda bm, bn: [pl.BlockSpec((bm, bn), lambda i, j, kk: (i, j)) for _ in range(k)]

    xn = _rms_fwd("norm_mix", xs, norm_mix_g)
    proj = _mm_fwd("proj_in", xn, win, 0, 7 * d, [F32], bm=2048)[0]
    att, lse = _attn_fwd(proj, d, slopes)
    xc, hp, h2d, ylru = _lru_fwd(proj, d, cw8, vec8, wa16, wx16)
    pa = _mm_fwd("proj_attn", att, wpa, 0, d, [BF16], bm=2048)[0]

    def merge(acc, pa_b, ga, gl):
        return acc, _sigmoid(ga) * pa_b.astype(F32) + _sigmoid(gl) * acc

    plr, merged = _mm_fwd("proj_lru_merge", ylru, wpl, 0, d, [BF16, BF16], merge, (pa, proj, proj),
                          lambda bm, bn: plain_specs(1)(bm, bn) + seg_specs(SEG_GA, SEG_GL)(bm, bn), bn=512)
    h1 = _mm_fwd("mix_out", merged, wout, 0, d, [F32], lambda acc, r: (acc + r,), (xs,), plain_specs(1))[0]
    hn = _rms_fwd("norm_mlp", h1, norm_mlp_g)

    def relu2(acc):
        return acc, jnp.square(jnp.maximum(acc, 0.0))

    up, hid = _mm_fwd("mlp_up", hn, wup, 0, wup.shape[1], [BF16, BF16], relu2, bm=2048)
    h2 = _mm_fwd("mlp_down", hid, wdown, 0, d, [F32], lambda acc, r: (acc + r,), (h1,), plain_specs(1))[0]
    dh2, dh2b, dg3, loss_lanes = _final_loss(h2, tgt, norm_final_g.reshape(1, d))
    loss = lax.psum(0.5 / d * jnp.sum(loss_lanes), ("x", "y", "c"))
    dh2b = lax.optimization_barrier((dh2b, loss))[0]

    def reduce_group(tag, kk, shp, partials, from_sibling, sequencer_id):
        sums = [_chip_sum(f"chip_sum_{tag}_{i}", p, f, k, sh, core) for i, (p, f, k, sh) in enumerate(zip(partials, from_sibling, kk, shp))]
        return list(zip(sums, _exchange_chips(f"rs_chips_{tag}", sums, sequencer_id)))

    dup = _mm_nt("mlp_down_dx", dh2b, wdown, [BF16], lambda acc, u: (acc * (2.0 * jnp.maximum(u.astype(F32), 0.0)),), (up,), plain_specs(1))[0]
    g_wdown = _mm_tn("mlp_down_dw", hid, dh2b)
    g_wup = _mm_tn("mlp_up_dw", hn, dup)
    shp_mlp = [w.shape for w in big[4:]]
    (dhn,), sib_mlp = _mm_nt("mlp_up_dx", dup, wup, [F32], side=_sibling_side([g_wup, g_wdown], kinds[4:], shp_mlp))
    red_up, red_down = reduce_group("mlp", kinds[4:], shp_mlp, [g_wup, g_wdown], sib_mlp, 2)
    dhn = lax.optimization_barrier((dhn, red_up[0], red_down[0]))[0]
    dh1, dh1b, dg2 = _rms_bwd("norm_mlp_bwd", h1, norm_mlp_g, dhn, dh2)

    def merge_bwd(acc, pa_b, pl_b, ga, gl):
        sa, sl = _sigmoid(ga), _sigmoid(gl)
        return acc * sa, acc * sl, acc * pa_b.astype(F32) * sa * (1.0 - sa), acc * pl_b.astype(F32) * sl * (1.0 - sl)

    dpa, dpl, dga, dgl = _mm_nt("mix_out_dx", dh1b, wout, [BF16] * 4, merge_bwd, (pa, plr, proj, proj),
                                lambda bm, bn: plain_specs(2)(bm, bn) + seg_specs(SEG_GA, SEG_GL)(bm, bn), bn=512)
    g_wout = _mm_tn("mix_out_dw", merged, dh1b)
    datt = _mm_nt("proj_attn_dx", dpa, wpa, [F32], bm=2048)[0]
    g_wpa = _mm_tn("proj_attn_dw", att, dpa)

    def lru_out_bwd(acc, h_b, gate):
        return acc * _gelu(gate), acc * h_b.astype(F32) * _gelu_grad(gate)

    g_wpl = _mm_tn("proj_lru_dw", ylru, dpl)
    shp_mix = [w.shape for w in big[1:4]]
    (dh, dxg), sib_mix = _mm_nt("proj_lru_dx", dpl, wpl, [F32, BF16], lru_out_bwd, (h2d, proj),
                                lambda bm, bn: plain_specs(1)(bm, bn) + seg_specs(SEG_GATE)(bm, bn), bn=512,
                                side=_sibling_side([g_wpa, g_wpl, g_wout], kinds[1:4], shp_mix))
    red_pa, red_pl, red_out = reduce_group("mix", kinds[1:4], shp_mix, [g_wpa, g_wpl, g_wout], sib_mix, 3)
    dxc, dwa, dwx, dvec = _lru_bwd(hp, dh, xc, wa16, wx16, vec8)
    dxr, dconv = _conv_bwd(dxc, proj, cw8)
    dproj = _attn_bwd(proj, d, datt, att, lse, slopes, (dxr, dxg, dga, dgl))

    def small_step(tag, grads, ws, ms, vs, like, after, seq=None):
        n_rows = sum(g.size for g in grads) // LANES
        per_dev = -(-n_rows // (N_DEV * SUBLANES)) * SUBLANES
        packed = lax.optimization_barrier((_pack_rows(grads, N_DEV * per_dev), after))[0]
        if seq is None:
            total = _all_reduce_small(f"all_reduce_{tag}", packed)
        else:
            pieces = lax.optimization_barrier((_small_scatter(f"scatter_{tag}", packed, seq[0]), seq[2]))[0]
            total = _small_gather(f"gather_{tag}", _small_sum(f"sum_{tag}", pieces), seq[1])
        w_rows = -(-(sum(w.size for w in ws) // LANES) // SUBLANES) * SUBLANES
        upd = _adamw_small(f"adamw_{tag}", total[:w_rows], _pack_rows(ws, w_rows), _pack_rows(ms, w_rows), _pack_rows(vs, w_rows))
        return _unpack_rows(total, like), [_unpack_rows(t, ws) for t in upd]

    early_w = [conv_b, lru_wa, lru_ba, lru_wx, lru_bx, lru_lambda, norm_mlp_g, norm_final_g]
    early_m = [m_conv_b, m_lru_wa, m_lru_ba, m_lru_wx, m_lru_bx, m_lru_lambda, m_norm_mlp_g, m_norm_final_g]
    early_v = [v_conv_b, v_lru_wa, v_lru_ba, v_lru_wx, v_lru_bx, v_lru_lambda, v_norm_mlp_g, v_norm_final_g]
    early_g = [dconv[CONV_TAPS:CONV_TAPS + 1], dwa, dvec[VEC_BA:VEC_BA + 1], dwx, dvec[VEC_BX:VEC_BX + 1],
               dvec[VEC_LAM:VEC_LAM + 1], dg2, dg3, dconv[0:CONV_TAPS]]
    dproj = lax.optimization_barrier((dproj, red_up[1], red_down[1]))[0]
    dproj = lax.optimization_barrier((dproj, red_pa[1], red_pl[1], red_out[1]))[0]
    early_sum, early_upd = small_step("small", early_g, early_w, early_m, early_v,
                                      early_w + [jax.ShapeDtypeStruct((1, CONV_TAPS, d), F32)], dxr, seq=(8, 9, dproj))
    g_cw_full = early_sum[-1]
    cshard = conv_w.shape[2]
    g_cw = lax.dynamic_slice(g_cw_full, (0, 0, dev * cshard), (1, CONV_TAPS, cshard))
    cw_delta, cw_m, cw_v = (t[:CONV_TAPS][None] for t in _adamw_small(
        "adamw_conv_w", pad_taps(g_cw[0]), pad_taps(conv_w[0]), pad_taps(m_conv_w[0]), pad_taps(v_conv_w[0])))
    half = (big[0].shape[0] // 2, big[0].shape[1])
    g_in0 = _mm_tn("proj_in_dw_0", xn, dproj, part=(0, 2))
    g_in1, sib_in0 = _mm_tn("proj_in_dw_1", xn, dproj, part=(1, 2), side=_sibling_side([g_in0], ["col"], [half]))
    red_in = reduce_group("in_0", ["col"], [half], [g_in0], sib_in0, 4)
    dproj = lax.optimization_barrier((dproj, red_in[0][0], early_sum))[0]
    (dxn0,), sib_in1 = _mm_nt("proj_in_dx_0", dproj, win, [F32], part=(0, 2), side=_sibling_side([g_in1], ["col"], [half]))
    red_in += reduce_group("in_1", ["col"], [half], [g_in1], sib_in1, 6)
    dproj = lax.optimization_barrier((dproj, red_in[1][0]))[0]
    dxn1 = _mm_nt("proj_in_dx_1", dproj, win, [F32], part=(1, 2))[0]
    dxn = jnp.concatenate([dxn0, dxn1], axis=0)
    dxn = lax.optimization_barrier((dxn, red_in[0][1]))[0]
    grad_x, _, dg1 = _rms_bwd("norm_mix_bwd", xs, norm_mix_g, dxn, dh1)
    red_up, red_down = lax.optimization_barrier(((red_up, red_down), dg1))[0]
    big_out = {i: _adamw_shard(f"adamw_{i}", [red], big[i], big_m[i], big_v[i], chip) for i, red in ((4, red_up), (5, red_down))}
    big_out.update({i: _adamw_shard(f"adamw_{i}", [red], big[i], big_m[i], big_v[i], chip) for i, red in ((1, red_pa), (2, red_pl), (3, red_out))})
    late_sum, late_upd = small_step("norm_mix", [dg1], [norm_mix_g], [m_norm_mix_g], [v_norm_mix_g], [norm_mix_g], (big_out[4], big_out[5]))
    big_out[0] = _adamw_shard("adamw_0", red_in, big[0], big_m[0], big_v[0], chip)
    s_grad = late_sum + early_sum[:-1]
    s_delta, s_m, s_v = (late_upd[j] + early_upd[j] for j in range(3))


    names = ["norm_mix_g", "w_in", "conv_w", "conv_b", "lru_wa", "lru_ba", "lru_wx", "lru_bx", "lru_lambda", "w_proj_attn", "w_proj_lru",
             "w_out", "norm_mlp_g", "w_up", "w_down", "norm_final_g"]
    small_names = ["norm_mix_g", "conv_b", "lru_wa", "lru_ba", "lru_wx", "lru_bx", "lru_lambda", "norm_mlp_g", "norm_final_g"]
    big_names = ["w_in", "w_proj_attn", "w_proj_lru", "w_out", "w_up", "w_down"]
    res = {"conv_w": (g_cw, cw_delta, cw_m, cw_v)}
    for i, nm in enumerate(small_names):
        res[nm] = (s_grad[i], s_delta[i], s_m[i], s_v[i])
    for i, nm in enumerate(big_names):
        res[nm] = tuple(t[None] for t in big_out[i])
    return (loss, grad_x[None], *[res[nm][0] for nm in names], *[res[nm][1] for nm in names],
            *[res[nm][2] for nm in names], *[res[nm][3] for nm in names])
```

```python
import jax
import jax.numpy as jnp
from jax import lax
from jax.experimental import pallas as pl
from jax.experimental.pallas import tpu as pltpu
from jax.experimental.pallas import tpu_sc as plsc

F32, BF16 = jnp.float32, jnp.bfloat16
MESH = pl.DeviceIdType.MESH
HBM = pl.BlockSpec(memory_space=pltpu.HBM)
N_DEV = 8
N_CHIP = 4
HEAD = 128
SPAN = 128
DILATIONS = (1, 4, 16)
CONV_TAPS = 4
LRU_C = 8.0
NORM_EPS = 1e-6
LANES = 128
SUBLANES = 8
VMEM_LIMIT = 56 * 1024 * 1024
ADAM_LR, ADAM_B1, ADAM_B2, ADAM_EPS, ADAM_WD, ADAM_STEP = 0.001, 0.9, 0.999, 1e-08, 0.01, 10
ADAM_C1 = 1.0 - ADAM_B1 ** ADAM_STEP
ADAM_C2 = 1.0 - ADAM_B2 ** ADAM_STEP
NEG = -1e30


def _params(sem=None):
    return pltpu.CompilerParams(dimension_semantics=sem, vmem_limit_bytes=VMEM_LIMIT)


def _sigmoid(v):
    return 1.0 / (1.0 + jnp.exp(-v))


def _gelu(v):
    k = 0.7978845608028654
    return 0.5 * v * (1.0 + jnp.tanh(k * (v + 0.044715 * v * v * v)))


def _gelu_grad(v):
    k = 0.7978845608028654
    t = jnp.tanh(k * (v + 0.044715 * v * v * v))
    return 0.5 * (1.0 + t) + 0.5 * v * (1.0 - t * t) * k * (1.0 + 3.0 * 0.044715 * v * v)


NN = (((1,), (0,)), ((), ()))
NT = (((1,), (1,)), ((), ()))
TN = (((0,), (0,)), ((), ()))


def _mm(name, a, a_spec, b, b_spec, dn, grid, out_shapes, out_specs, acc_block, epilogue=None, extras=(), extra_specs=(), side=None):
    nk, ne, no = grid[2], len(extras), len(out_shapes)
    side_ops, side_shapes, side_copies, make_copies = side if side is not None else ((), (), 0, None)
    ns_in, ns_out = len(side_ops), len(side_shapes)

    def body(*refs):
        a_ref, b_ref = refs[0], refs[1]
        ex, side_in = refs[2:2 + ne], refs[2 + ne:2 + ne + ns_in]
        outs = refs[2 + ne + ns_in:2 + ne + ns_in + no]
        side_out = refs[2 + ne + ns_in + no:2 + ne + ns_in + no + ns_out]
        scratch = refs[2 + ne + ns_in + no + ns_out:]
        at = [pl.program_id(ax) for ax in range(3)]
        if side is not None:
            @pl.when((at[0] == 0) & (at[1] == 0) & (at[2] == 0))
            def _():
                for cp in make_copies(side_in, side_out, scratch[-2], scratch[-1]):
                    cp.start()

        part = lax.dot_general(a_ref[...], b_ref[...], dn, preferred_element_type=F32)

        def finish(acc):
            vals = epilogue(acc, *[e[...] for e in ex]) if epilogue is not None else (acc,)
            for o, v in zip(outs, vals):
                o[...] = v.astype(o.dtype)

        if nk == 1:
            finish(part)
        else:
            acc_ref, k = scratch[0], at[2]

            @pl.when(k == 0)
            def _():
                acc_ref[...] = part

            @pl.when(k > 0)
            def _():
                acc_ref[...] += part

            @pl.when(k == nk - 1)
            def _():
                finish(acc_ref[...])

        if side is not None:
            @pl.when((at[0] == grid[0] - 1) & (at[1] == grid[1] - 1) & (at[2] == grid[2] - 1))
            def _():
                for cp in make_copies(side_in, side_out, scratch[-2], scratch[-1]):
                    cp.wait()

    scratch_shapes = [pltpu.VMEM(acc_block, F32)] if nk > 1 else []
    if side is not None:
        scratch_shapes += [pltpu.SemaphoreType.DMA((side_copies,)), pltpu.SemaphoreType.DMA((side_copies,))]
    res = pl.pallas_call(
        body, out_shape=[*out_shapes, *side_shapes], grid=grid, in_specs=[a_spec, b_spec, *extra_specs, *[HBM] * ns_in],
        out_specs=[*out_specs, *[HBM] * ns_out], scratch_shapes=scratch_shapes,
        compiler_params=_params(("arbitrary",) * 3 if side is not None else ("parallel", "parallel", "arbitrary")),
        name=name)(a, b, *extras, *side_ops)
    return res if side is None else (res[:no], res[no:])


def _blk(n, pref):
    return pref if n % pref == 0 else n


def _kblk(k):
    return k if k <= 2048 else next(b for b in (2048, 1024, 512) if k % b == 0)


def _mm_fwd(name, a, w, col0, ncols, out_dtypes, epilogue=None, extras=(), extra_specs_fn=None, bm=1024, bn=1024):
    m, k = a.shape
    bm, bn = _blk(m, bm), _blk(ncols, bn)
    bk = _kblk(k)
    nk = k // bk
    cb0 = col0 // bn
    grid = (m // bm, ncols // bn, nk)
    a_spec = pl.BlockSpec((bm, bk), lambda i, j, kk: (i, kk))
    b_spec = pl.BlockSpec((bk, bn), lambda i, j, kk: (kk, cb0 + j))
    shapes = [jax.ShapeDtypeStruct((m, ncols), dt) for dt in out_dtypes]
    specs = [pl.BlockSpec((bm, bn), lambda i, j, kk: (i, j)) for _ in out_dtypes]
    ex_specs = extra_specs_fn(bm, bn) if extra_specs_fn else ()
    return _mm(name, a, a_spec, w, b_spec, NN, grid, shapes, specs, (bm, bn), epilogue, extras, ex_specs)


def _mm_nt(name, a, w, out_dtypes, epilogue=None, extras=(), extra_specs_fn=None, part=(0, 1), side=None, bm=1024, bn=1024):
    n = w.shape[0]
    if a.ndim == 3:
        seg_cols, m, k = a.shape[2], a.shape[1], a.shape[0] * a.shape[2]
    else:
        m, k = a.shape
    m = m // part[1]
    bm, bn = _blk(m, bm), _blk(n, bn)
    bk = _kblk(k)
    grid = (m // bm, n // bn, k // bk)
    i0 = part[0] * (m // bm)
    if a.ndim == 3:
        per = seg_cols // bk
        a_spec = pl.BlockSpec((None, bm, bk), lambda i, j, kk: (kk // per, i0 + i, kk % per))
    else:
        a_spec = pl.BlockSpec((bm, bk), lambda i, j, kk: (i0 + i, kk))
    b_spec = pl.BlockSpec((bn, bk), lambda i, j, kk: (j, kk))
    shapes = [jax.ShapeDtypeStruct((m, n), dt) for dt in out_dtypes]
    specs = [pl.BlockSpec((bm, bn), lambda i, j, kk: (i, j)) for _ in out_dtypes]
    ex_specs = extra_specs_fn(bm, bn) if extra_specs_fn else ()
    return _mm(name, a, a_spec, w, b_spec, NT, grid, shapes, specs, (bm, bn), epilogue, extras, ex_specs, side)


def _mm_tn(name, a, b, part=(0, 1), side=None, bm=1024, bn=2048):
    t, m = a.shape
    n = b.shape[1] if b.ndim == 2 else b.shape[0] * b.shape[2]
    m = m // part[1]
    bm, bn = _blk(m, bm), _blk(n, bn)
    grid = (m // bm, n // bn, 1)
    i0 = part[0] * (m // bm)
    a_spec = pl.BlockSpec((t, bm), lambda i, j, kk: (0, i0 + i))
    if b.ndim == 3:
        per = b.shape[2] // bn
        b_spec = pl.BlockSpec((None, t, bn), lambda i, j, kk: (j // per, 0, j % per))
    else:
        b_spec = pl.BlockSpec((t, bn), lambda i, j, kk: (0, j))
    res = _mm(name, a, a_spec, b, b_spec, TN, grid, [jax.ShapeDtypeStruct((m, n), BF16)],
              [pl.BlockSpec((bm, bn), lambda i, j, kk: (i, j))], (bm, bn), side=side)
    return res[0] if side is None else (res[0][0], res[1])


ROWS = 256


def _row_spec(d):
    return pl.BlockSpec((ROWS, d), lambda i: (i, 0))


def _vec_spec(d, rows=1):
    return pl.BlockSpec((rows, d), lambda i: (0, 0))


def _rms_fwd(name, x, g):
    s, d = x.shape

    def body(x_ref, g_ref, o_ref):
        xv = x_ref[...]
        r = lax.rsqrt(jnp.mean(xv * xv, axis=-1, keepdims=True) + NORM_EPS)
        o_ref[...] = (xv * r * g_ref[...]).astype(BF16)

    return pl.pallas_call(body, out_shape=jax.ShapeDtypeStruct((s, d), BF16), grid=(s // ROWS,),
                          in_specs=[_row_spec(d), _vec_spec(d)], out_specs=_row_spec(d),
                          compiler_params=_params(("parallel",)), name=name)(x, g)


def _rms_bwd_math(xv, g, dy):
    r = lax.rsqrt(jnp.mean(xv * xv, axis=-1, keepdims=True) + NORM_EPS)
    n = xv * r
    z = dy * g
    dx = r * (z - n * jnp.mean(z * n, axis=-1, keepdims=True))
    return dx, jnp.sum(dy * n, axis=0, keepdims=True)


def _rms_bwd(name, x, g, dy, resid):
    s, d = x.shape

    def body(x_ref, g_ref, dy_ref, r_ref, dx_ref, dxb_ref, dg_ref):
        dx, dg = _rms_bwd_math(x_ref[...], g_ref[...], dy_ref[...])
        dx = dx + r_ref[...]
        dx_ref[...] = dx
        dxb_ref[...] = dx.astype(BF16)

        @pl.when(pl.program_id(0) == 0)
        def _():
            dg_ref[...] = jnp.zeros_like(dg_ref)

        dg_ref[...] += dg

    return pl.pallas_call(
        body, out_shape=[jax.ShapeDtypeStruct((s, d), F32), jax.ShapeDtypeStruct((s, d), BF16), jax.ShapeDtypeStruct((1, d), F32)],
        grid=(s // ROWS,), in_specs=[_row_spec(d), _vec_spec(d), _row_spec(d), _row_spec(d)],
        out_specs=[_row_spec(d), _row_spec(d), _vec_spec(d)], compiler_params=_params(("arbitrary",)), name=name)(x, g, dy, resid)


def _final_loss(h2, tgt, g):
    s, d = h2.shape

    def body(x_ref, t_ref, g_ref, dx_ref, dxb_ref, dg_ref, ls_ref):
        xv, gv = x_ref[...], g_ref[...]
        r = lax.rsqrt(jnp.mean(xv * xv, axis=-1, keepdims=True) + NORM_EPS)
        diff = xv * r * gv - t_ref[...]
        dx, dg = _rms_bwd_math(xv, gv, diff * (1.0 / d))
        dx_ref[...] = dx
        dxb_ref[...] = dx.astype(BF16)

        @pl.when(pl.program_id(0) == 0)
        def _():
            dg_ref[...] = jnp.zeros_like(dg_ref)
            ls_ref[...] = jnp.zeros_like(ls_ref)

        dg_ref[...] += dg
        ls_ref[...] += jnp.sum(diff * diff, axis=0, keepdims=True)

    return pl.pallas_call(
        body, out_shape=[jax.ShapeDtypeStruct((s, d), F32), jax.ShapeDtypeStruct((s, d), BF16),
                         jax.ShapeDtypeStruct((1, d), F32), jax.ShapeDtypeStruct((1, d), F32)],
        grid=(s // ROWS,), in_specs=[_row_spec(d), _row_spec(d), _vec_spec(d)],
        out_specs=[_row_spec(d), _row_spec(d), _vec_spec(d), _vec_spec(d)],
        compiler_params=_params(("arbitrary",)), name="final_norm_loss")(h2, tgt, g)


ATTN_Q = 128


ATTN_BATCH = 8


def _attn_units(s):
    units = []
    for gi, d in enumerate(DILATIONS):
        for r in range(d):
            for q0 in range(0, s // d, ATTN_Q):
                k0 = max(q0 - SPAN, 0)
                units.append((gi, d, r, q0, k0, q0 + ATTN_Q - k0))
    return units


def _stream_rows(d, r, start, size):
    return pl.ds(r + start * d, size) if d == 1 else pl.ds(r + start * d, size, stride=d)


def _attn_scores(q_ref, k_ref, slope, d, r, q0, k0, nk):
    qrows, krows = _stream_rows(d, r, q0, ATTN_Q), _stream_rows(d, r, k0, nk)
    qb = q_ref[qrows, :].astype(BF16)
    kb = k_ref[krows, :].astype(BF16)
    sc = lax.dot_general(qb, kb, NT, preferred_element_type=F32) * (HEAD ** -0.5)
    qi = lax.broadcasted_iota(jnp.int32, (ATTN_Q, nk), 0)
    kj = lax.broadcasted_iota(jnp.int32, (ATTN_Q, nk), 1)
    dist = (q0 - k0) + qi - kj
    valid = (dist >= 0) & (dist <= SPAN)
    sc = sc - (slope * d) * dist.astype(F32)
    return jnp.where(valid, sc, NEG), valid, qb, kb, qrows, krows


def _attn_fwd(proj, dm, slopes):
    s = proj.shape[0]
    units = _attn_units(s)

    def body(sl_ref, q_ref, k_ref, v_ref, att_ref, lse_ref, *scr):
        o_scr, l_scr = scr[:3], scr[3:]
        slope = sl_ref[pl.program_id(0)]
        for first in range(0, len(units), ATTN_BATCH):
            batch = units[first:first + ATTN_BATCH]
            scored = [_attn_scores(q_ref, k_ref, slope, d, r, q0, k0, nk) for _, d, r, q0, k0, nk in batch]
            soft = []
            for sc, _, _, _, _, _ in scored:
                m = jnp.max(sc, axis=-1, keepdims=True)
                p = jnp.exp(sc - m)
                soft.append((m, p, jnp.sum(p, axis=-1, keepdims=True)))
            outs = [lax.dot_general(p.astype(BF16), v_ref[sco[5], :].astype(BF16), NN, preferred_element_type=F32)
                    for (m, p, l), sco in zip(soft, scored)]
            for (gi, *_), (m, p, l), sco, o in zip(batch, soft, scored, outs):
                o_scr[gi][sco[4], :] = o / l
                l_scr[gi][sco[4], :] = jnp.broadcast_to(m + jnp.log(l), (ATTN_Q, HEAD))
        l0, l1, l2 = l_scr[0][...], l_scr[1][...], l_scr[2][...]
        m = jnp.maximum(jnp.maximum(l0, l1), l2)
        w0, w1, w2 = jnp.exp(l0 - m), jnp.exp(l1 - m), jnp.exp(l2 - m)
        tot = w0 + w1 + w2
        att_ref[...] = ((w0 * o_scr[0][...] + w1 * o_scr[1][...] + w2 * o_scr[2][...]) / tot).astype(BF16)
        lse_ref[...] = m + jnp.log(tot)

    def seg(i):
        return pl.BlockSpec((s, HEAD), lambda h: (0, i * (dm // HEAD) + h))

    col = pl.BlockSpec((s, HEAD), lambda h: (0, h))
    return pl.pallas_call(
        body, out_shape=[jax.ShapeDtypeStruct((s, dm), BF16), jax.ShapeDtypeStruct((s, dm), F32)], grid=(dm // HEAD,),
        in_specs=[pl.BlockSpec(memory_space=pltpu.SMEM), seg(0), seg(1), seg(2)], out_specs=[col, col],
        scratch_shapes=[pltpu.VMEM((s, HEAD), F32)] * (2 * len(DILATIONS)),
        compiler_params=_params(("parallel",)), name="attn_fwd")(slopes, proj, proj, proj)


def _attn_bwd(proj, dm, datt, att, lse, slopes, others):
    s = proj.shape[0]
    units = _attn_units(s)

    def body(sl_ref, q_ref, k_ref, v_ref, do_ref, att_ref, lse_ref, o3, o4, o5, o6, out_ref, dq_scr, dk_scr, dv_scr, dl_scr):
        slope = sl_ref[pl.program_id(0)]
        delta = jnp.sum(do_ref[...] * att_ref[...].astype(F32), axis=-1, keepdims=True)
        dl_scr[...] = jnp.broadcast_to(delta, (s, HEAD))
        dq_scr[...] = jnp.zeros_like(dq_scr)
        dk_scr[...] = jnp.zeros_like(dk_scr)
        dv_scr[...] = jnp.zeros_like(dv_scr)
        for first in range(0, len(units), ATTN_BATCH):
            scored = [_attn_scores(q_ref, k_ref, slope, d, r, q0, k0, nk) for _, d, r, q0, k0, nk in units[first:first + ATTN_BATCH]]
            dobs = [do_ref[sco[4], :].astype(BF16) for sco in scored]
            dps = [lax.dot_general(dob, v_ref[sco[5], :].astype(BF16), NT, preferred_element_type=F32) for dob, sco in zip(dobs, scored)]
            ps = [jnp.where(sco[1], jnp.exp(sco[0] - lse_ref[sco[4], :][:, 0:1]), 0.0) for sco in scored]
            dss = [(p * (dp - dl_scr[sco[4], :][:, 0:1]) * (HEAD ** -0.5)).astype(BF16) for p, dp, sco in zip(ps, dps, scored)]
            dqs = [lax.dot_general(ds, sco[3], NN, preferred_element_type=F32) for ds, sco in zip(dss, scored)]
            dks = [lax.dot_general(ds, sco[2], TN, preferred_element_type=F32) for ds, sco in zip(dss, scored)]
            dvs = [lax.dot_general(p.astype(BF16), dob, TN, preferred_element_type=F32) for p, dob in zip(ps, dobs)]
            for sco, dq, dk, dv in zip(scored, dqs, dks, dvs):
                dq_scr[sco[4], :] += dq
                dk_scr[sco[5], :] += dk
                dv_scr[sco[5], :] += dv
        for j, scr in enumerate((dq_scr, dk_scr, dv_scr)):
            out_ref[j] = scr[...].astype(BF16)
        for j, other in enumerate((o3, o4, o5, o6)):
            out_ref[3 + j] = other[...]

    def seg(i):
        return pl.BlockSpec((s, HEAD), lambda h: (0, i * (dm // HEAD) + h))

    col = pl.BlockSpec((s, HEAD), lambda h: (0, h))
    return pl.pallas_call(
        body, out_shape=jax.ShapeDtypeStruct((7, s, dm), BF16), grid=(dm // HEAD,),
        in_specs=[pl.BlockSpec(memory_space=pltpu.SMEM), seg(0), seg(1), seg(2), col, col, col, col, col, col, col],
        out_specs=pl.BlockSpec((7, s, HEAD), lambda h: (0, 0, h)), scratch_shapes=[pltpu.VMEM((s, HEAD), F32)] * 4,
        compiler_params=_params(("parallel",)), name="attn_bwd")(slopes, proj, proj, proj, datt, att, lse, *others)


VEC_CB, VEC_BA, VEC_BX, VEC_LAM = 0, 1, 2, 3
SEG_Q, SEG_K, SEG_V, SEG_X, SEG_GATE, SEG_GA, SEG_GL = range(7)


def _softplus(z):
    return jnp.maximum(z, 0.0) + jnp.log1p(jnp.exp(-jnp.abs(z)))


def _gate_math(xc, wa_ref, wx_ref, vec):
    xcb = xc.astype(BF16)
    nh = xc.shape[1] // HEAD
    pre_a = jnp.concatenate([jnp.dot(xcb[:, h * HEAD:(h + 1) * HEAD], wa_ref[h], preferred_element_type=F32) for h in range(nh)], axis=1)
    pre_x = jnp.concatenate([jnp.dot(xcb[:, h * HEAD:(h + 1) * HEAD], wx_ref[h], preferred_element_type=F32) for h in range(nh)], axis=1)
    ra = _sigmoid(pre_a + vec[VEC_BA:VEC_BA + 1])
    ig = _sigmoid(pre_x + vec[VEC_BX:VEC_BX + 1])
    sp = _softplus(-vec[VEC_LAM:VEC_LAM + 1])
    log_a = -LRU_C * ra * sp
    a = jnp.exp(log_a)
    z = 2.0 * log_a
    one_minus_a2 = jnp.where(z > -0.01, -z * (1.0 + z * (0.5 + z * (1.0 / 6.0))), 1.0 - jnp.exp(z))
    mult = jnp.sqrt(one_minus_a2)
    return dict(xcb=xcb, ra=ra, ig=ig, sp=sp, a=a, mult=mult)


def _conv_pad_prev(pad_ref, cur, halo, first):
    pad_ref[0:SUBLANES, :] = jnp.where(first, 0.0, halo)
    pad_ref[SUBLANES:SUBLANES + cur.shape[0], :] = cur


def _shift_rows(x, s, fill, up=False):
    rid = lax.broadcasted_iota(jnp.int32, x.shape, 0)
    if up:
        return jnp.where(rid < SUBLANES - s, pltpu.roll(x, SUBLANES - s, axis=0), fill)
    return jnp.where(rid >= s, pltpu.roll(x, s, axis=0), fill)


def _lru_fwd(proj, d, cw8, vec8, wa, wx):
    s = proj.shape[0]
    hb = ROWS // SUBLANES

    def body(x_ref, halo_ref, g_ref, cw_ref, vec_ref, wa_ref, wx_ref, xc_ref, hp_ref, h2_ref, y_ref, pad, a_scr, u_scr, carry):
        @pl.when(pl.program_id(0) == 0)
        def _():
            carry[...] = jnp.zeros_like(carry)

        _conv_pad_prev(pad, x_ref[...], halo_ref[...], pl.program_id(0) == 0)
        vec = vec_ref[...]
        xc = vec[VEC_CB:VEC_CB + 1]
        for k in range(CONV_TAPS):
            xc = xc + cw_ref[k:k + 1, :] * pad[pl.ds(SUBLANES - (CONV_TAPS - 1) + k, ROWS), :]
        gm = _gate_math(xc, wa_ref, wx_ref, vec)
        xc_ref[...] = xc
        a_scr[...] = gm["a"]
        u_scr[...] = gm["mult"] * (gm["ig"] * xc)

        def group(gi, before):
            rows = pl.ds(pl.multiple_of(gi * SUBLANES, SUBLANES), SUBLANES)
            ca, cb = a_scr[rows, :], u_scr[rows, :]
            for sh in (1, 2, 4):
                cb = ca * _shift_rows(cb, sh, 0.0) + cb
                ca = ca * _shift_rows(ca, sh, 1.0)
            h = cb + ca * before
            hp_ref[rows, :] = jnp.where(lax.broadcasted_iota(jnp.int32, h.shape, 0) == 0, before, pltpu.roll(h, 1, axis=0))
            h2_ref[rows, :] = h.astype(BF16)
            y_ref[rows, :] = (h * _gelu(g_ref[rows, :])).astype(BF16)
            return jnp.broadcast_to(h[SUBLANES - 1:SUBLANES, :], h.shape)

        carry[...] = lax.fori_loop(0, ROWS // SUBLANES, group, carry[...])

    wspec = pl.BlockSpec(wa.shape, lambda i: (0, 0, 0))
    return pl.pallas_call(
        body, out_shape=[jax.ShapeDtypeStruct((s, d), F32)] * 2 + [jax.ShapeDtypeStruct((s, d), BF16)] * 2, grid=(s // ROWS,),
        in_specs=[pl.BlockSpec((ROWS, d), lambda i: (i, SEG_X)),
                  pl.BlockSpec((SUBLANES, d), lambda i: (jnp.maximum(i * hb - 1, 0), SEG_X)),
                  pl.BlockSpec((ROWS, d), lambda i: (i, SEG_GATE)),
                  _vec_spec(d, SUBLANES), _vec_spec(d, SUBLANES), wspec, wspec],
        out_specs=[_row_spec(d)] * 4,
        scratch_shapes=[pltpu.VMEM((ROWS + SUBLANES, d), F32), pltpu.VMEM((ROWS, d), F32), pltpu.VMEM((ROWS, d), F32), pltpu.VMEM((SUBLANES, d), F32)],
        compiler_params=_params(("arbitrary",)), name="lru_fwd")(proj, proj, proj, cw8, vec8, wa, wx)


def _lru_bwd(hp, dh, xc, proj, cw8, wa, wx, vec8):
    s, d = xc.shape
    nh = d // HEAD
    nb = s // ROWS
    hb = ROWS // SUBLANES

    def body(hp_ref, dh_ref, xc_ref, x_ref, xprev_ref, cw_ref, wa_ref, wx_ref, vec_ref, dx_ref, dwa_ref, dwx_ref, dvec_ref, dcw_ref,
             a_scr, g_scr, da_scr, carry, padd, padx, dnext):
        step = pl.program_id(0)

        @pl.when(step == 0)
        def _():
            carry[...] = jnp.zeros_like(carry)
            dnext[...] = jnp.zeros_like(dnext)
            dwa_ref[...] = jnp.zeros_like(dwa_ref)
            dwx_ref[...] = jnp.zeros_like(dwx_ref)
            dvec_ref[...] = jnp.zeros_like(dvec_ref)
            dcw_ref[...] = jnp.zeros_like(dcw_ref)

        xc_v, vec = xc_ref[...], vec_ref[...]
        gm = _gate_math(xc_v, wa_ref, wx_ref, vec)
        ra, ig, sp, a, mult = gm["ra"], gm["ig"], gm["sp"], gm["a"], gm["mult"]
        a_scr[...] = a

        def group(j, after):
            rows = pl.ds(pl.multiple_of((ROWS // SUBLANES - 1 - j) * SUBLANES, SUBLANES), SUBLANES)
            ca, dhv = a_scr[rows, :], dh_ref[rows, :]
            cb = ca * dhv
            for sh in (1, 2, 4):
                cb = ca * _shift_rows(cb, sh, 0.0, up=True) + cb
                ca = ca * _shift_rows(ca, sh, 1.0, up=True)
            c = cb + ca * after
            last = lax.broadcasted_iota(jnp.int32, c.shape, 0) == SUBLANES - 1
            g = dhv + jnp.where(last, after, pltpu.roll(c, SUBLANES - 1, axis=0))
            g_scr[rows, :] = g
            da_scr[rows, :] = g * hp_ref[rows, :]
            return jnp.broadcast_to(c[0:1, :], c.shape)

        carry[...] = lax.fori_loop(0, ROWS // SUBLANES, group, carry[...])
        du, da = g_scr[...], da_scr[...]
        dmult = du * ig * xc_v
        dlog_a = da * a - dmult * (a * a) / mult
        dpre_a = dlog_a * (-LRU_C * sp) * ra * (1.0 - ra)
        dpre_x = du * mult * xc_v * ig * (1.0 - ig)
        dlam = jnp.sum(dlog_a * (-LRU_C * ra), axis=0, keepdims=True) * (-_sigmoid(-vec[VEC_LAM:VEC_LAM + 1]))
        dvec_ref[VEC_BA:VEC_BA + 1, :] += jnp.sum(dpre_a, axis=0, keepdims=True)
        dvec_ref[VEC_BX:VEC_BX + 1, :] += jnp.sum(dpre_x, axis=0, keepdims=True)
        dvec_ref[VEC_LAM:VEC_LAM + 1, :] += dlam
        dab, dxb, xcb = dpre_a.astype(BF16), dpre_x.astype(BF16), gm["xcb"]
        back = []
        for h in range(nh):
            cols = slice(h * HEAD, (h + 1) * HEAD)
            dwa_ref[h] += lax.dot_general(xcb[:, cols], dab[:, cols], TN, preferred_element_type=F32)
            dwx_ref[h] += lax.dot_general(xcb[:, cols], dxb[:, cols], TN, preferred_element_type=F32)
            back.append(lax.dot_general(dab[:, cols], wa_ref[h], NT, preferred_element_type=F32)
                        + lax.dot_general(dxb[:, cols], wx_ref[h], NT, preferred_element_type=F32))
        dc = du * mult * ig + jnp.concatenate(back, axis=1)
        padd[0:ROWS, :] = dc
        padd[ROWS:ROWS + SUBLANES, :] = dnext[...]
        dnext[...] = dc[0:SUBLANES, :]
        _conv_pad_prev(padx, x_ref[...], xprev_ref[...], step == nb - 1)
        dx = jnp.zeros_like(dc)
        for k in range(CONV_TAPS):
            dx = dx + cw_ref[k:k + 1, :] * padd[pl.ds(CONV_TAPS - 1 - k, ROWS), :]
            dcw_ref[k:k + 1, :] += jnp.sum(dc * padx[pl.ds(SUBLANES - (CONV_TAPS - 1) + k, ROWS), :], axis=0, keepdims=True)
        dcw_ref[CONV_TAPS:CONV_TAPS + 1, :] += jnp.sum(dc, axis=0, keepdims=True)
        dx_ref[...] = dx.astype(BF16)

    rows_rev = pl.BlockSpec((ROWS, d), lambda i: (nb - 1 - i, 0))
    wspec = pl.BlockSpec(wa.shape, lambda i: (0, 0, 0))
    return pl.pallas_call(
        body, out_shape=[jax.ShapeDtypeStruct((s, d), BF16), jax.ShapeDtypeStruct(wa.shape, F32), jax.ShapeDtypeStruct(wa.shape, F32),
                         jax.ShapeDtypeStruct((SUBLANES, d), F32), jax.ShapeDtypeStruct((SUBLANES, d), F32)],
        grid=(nb,),
        in_specs=[rows_rev, rows_rev, rows_rev, pl.BlockSpec((ROWS, d), lambda i: (nb - 1 - i, SEG_X)),
                  pl.BlockSpec((SUBLANES, d), lambda i: (jnp.maximum((nb - 1 - i) * hb - 1, 0), SEG_X)),
                  _vec_spec(d, SUBLANES), wspec, wspec, _vec_spec(d, SUBLANES)],
        out_specs=[rows_rev, wspec, wspec, _vec_spec(d, SUBLANES), _vec_spec(d, SUBLANES)],
        scratch_shapes=[pltpu.VMEM((ROWS, d), F32)] * 3 + [pltpu.VMEM((SUBLANES, d), F32)]
        + [pltpu.VMEM((ROWS + SUBLANES, d), F32)] * 2 + [pltpu.VMEM((SUBLANES, d), F32)],
        compiler_params=_params(("arbitrary",)), name="lru_bwd")(hp, dh, xc, proj, proj, cw8, wa, wx, vec8)


def _coords():
    return lax.axis_index("x"), lax.axis_index("y"), lax.axis_index("c")


def _other_chips(x, y):
    return [(1 - x, y), (x, 1 - y), (1 - x, 1 - y)]


def _slab(ref, kind, shard_shape, idx, half=None):
    r, c = shard_shape
    r0, nr = (0, r) if half is None else (half * (r // 2), r // 2)
    if kind == "col":
        return ref.at[pl.ds(r0, nr), pl.ds(pl.multiple_of(idx * c, LANES), c)]
    if kind == "row":
        return ref.at[pl.ds(pl.multiple_of(idx * r, SUBLANES) + r0, nr), :]
    return ref.at[idx, pl.ds(r0, nr), :]


def _full_shape(shard_shape, kind):
    r, c = shard_shape
    return {"col": (r, c * N_DEV), "row": (r * N_DEV, c), "slot": (N_DEV, r, c)}[kind]


def _handshake(peers):
    barrier = pltpu.get_barrier_semaphore()
    for peer in peers:
        pl.semaphore_signal(barrier, inc=1, device_id=peer, device_id_type=MESH)
    pl.semaphore_wait(barrier, len(peers))


def _launch(name, body, out_shape, operands, sems, sequencer_id):
    if sequencer_id is None:
        return pl.pallas_call(body, out_shape=out_shape, in_specs=[HBM] * len(operands), out_specs=[HBM] * len(out_shape),
                              scratch_shapes=sems, name=name)(*operands)
    return pl.kernel(body, out_type=out_shape, mesh=plsc.ScalarSubcoreMesh(axis_name="seq", num_cores=1), name=name,
                     scratch_types=sems, compiler_params=pltpu.CompilerParams(collective_id=sequencer_id))(*operands)


AG_COPIES = 10


def _all_gather(name, shards, kinds, sequencer_id=None):
    n = len(shards)
    shapes = [s.shape for s in shards]

    def body(*refs):
        ins, outs = refs[:n], refs[n:2 * n]
        send_sems, recv_sems, local_sems = refs[2 * n:]
        x, y, c = _coords()
        me, sib, xn, yn, dg = (x, y, c), (x, y, 1 - c), (1 - x, y, c), (x, 1 - y, c), (1 - x, 1 - y, c)
        if sequencer_id is not None:
            _handshake([sib, xn, yn])

        def part(i, dev, half=None):
            return _slab(outs[i], kinds[i], shapes[i], 4 * dev[0] + 2 * dev[1] + dev[2], half)

        def copy(i, k, block, half, to, own=False):
            r = shapes[i][0]
            src = part(i, block, half) if not own else (ins[i] if half is None else ins[i].at[pl.ds(half * (r // 2), r // 2), :])
            return pltpu.make_async_remote_copy(
                src_ref=src, dst_ref=part(i, block, half), send_sem=send_sems.at[AG_COPIES * i + k],
                recv_sem=recv_sems.at[AG_COPIES * i + k], device_id=to, device_id_type=MESH)

        def other_core(dev):
            return (dev[0], dev[1], 1 - c)

        started = []

        def start(cp):
            cp.start()
            started.append(cp)

        for i in range(n):
            start(copy(i, 1, me, 0, xn, own=True))
            start(copy(i, 4, me, 1, yn, own=True))
            start(copy(i, 2, me, 1, xn, own=True))
            start(copy(i, 3, me, 0, yn, own=True))
            start(copy(i, 0, me, None, sib, own=True))
        mine = [pltpu.make_async_copy(ins[i], part(i, me), local_sems.at[i]) for i in range(n)]
        for cp in mine:
            cp.start()
        for i in range(n):
            copy(i, 1, xn, 0, me).wait_recv()
            start(copy(i, 5, xn, 0, yn))
            copy(i, 4, yn, 1, me).wait_recv()
            start(copy(i, 6, yn, 1, xn))
        for i in range(n):
            copy(i, 2, xn, 1, me).wait_recv()
            start(copy(i, 7, xn, None, sib))
            copy(i, 3, yn, 0, me).wait_recv()
            start(copy(i, 8, yn, None, sib))
        for i in range(n):
            copy(i, 5, dg, 0, me).wait_recv()
            copy(i, 6, dg, 1, me).wait_recv()
            start(copy(i, 9, dg, None, sib))
        for i in range(n):
            copy(i, 0, sib, None, me).wait_recv()
            for k, dev in ((7, xn), (8, yn), (9, dg)):
                copy(i, k, other_core(dev), None, me).wait_recv()
        for cp in started:
            cp.wait_send()
        for cp in mine:
            cp.wait()

    out_shape = [jax.ShapeDtypeStruct(_full_shape(s.shape, k), s.dtype) for s, k in zip(shards, kinds)]
    sems = [pltpu.SemaphoreType.DMA((AG_COPIES * n,)), pltpu.SemaphoreType.DMA((AG_COPIES * n,)), pltpu.SemaphoreType.DMA((n,))]
    return _launch(name, body, out_shape, shards, sems, sequencer_id)


def _sibling_copies(kinds, shard_shapes):
    def make(ins, outs, send_sems, recv_sems):
        x, y, c = _coords()
        return [pltpu.make_async_remote_copy(
            src_ref=_slab(ins[i], kinds[i], shard_shapes[i], 2 * q + (1 - c)), dst_ref=outs[i].at[q],
            send_sem=send_sems.at[N_CHIP * i + q], recv_sem=recv_sems.at[N_CHIP * i + q],
            device_id=(x, y, 1 - c), device_id_type=MESH) for i in range(len(ins)) for q in range(N_CHIP)]
    return make


def _sibling_side(partials, kinds, shard_shapes):
    return (partials, [jax.ShapeDtypeStruct((N_CHIP, *s), BF16) for s in shard_shapes], N_CHIP * len(partials),
            _sibling_copies(kinds, shard_shapes))


def _exchange_chips(name, chip_sums, sequencer_id=None):
    n = len(chip_sums)

    def body(*refs):
        ins, outs = refs[:n], refs[n:2 * n]
        send_sems, recv_sems = refs[2 * n:]
        x, y, c = _coords()
        if sequencer_id is not None:
            _handshake([(cx, cy, c) for cx, cy in _other_chips(x, y)])
        cps = []
        for i in range(n):
            for k, (cx, cy) in enumerate(_other_chips(x, y)):
                cps.append(pltpu.make_async_remote_copy(
                    src_ref=ins[i].at[2 * cx + cy], dst_ref=outs[i].at[k], send_sem=send_sems.at[3 * i + k],
                    recv_sem=recv_sems.at[3 * i + k], device_id=(cx, cy, c), device_id_type=MESH))
        for cp in cps:
            cp.start()
        for cp in cps:
            cp.wait()

    return _launch(name, body, [jax.ShapeDtypeStruct((3, *t.shape[1:]), BF16) for t in chip_sums], chip_sums,
                   [pltpu.SemaphoreType.DMA((3 * n,)), pltpu.SemaphoreType.DMA((3 * n,))], sequencer_id)


def _all_peers(x, y, c):
    return [(x ^ (k >> 2), y ^ ((k >> 1) & 1), c ^ (k & 1)) for k in range(1, N_DEV)]


def _small_scatter(name, packed, sequencer_id):
    rows = packed.shape[0] // N_DEV

    def body(p_ref, rb_ref, send_sems, recv_sems, local_sem):
        x, y, c = _coords()
        me = 4 * x + 2 * y + c
        peers = _all_peers(x, y, c)
        _handshake(peers)

        def piece(idx):
            return p_ref.at[pl.ds(pl.multiple_of(idx * rows, SUBLANES), rows), :]

        cps = [pltpu.make_async_remote_copy(src_ref=piece(me ^ k), dst_ref=rb_ref.at[k], send_sem=send_sems.at[k], recv_sem=recv_sems.at[k],
                                            device_id=peers[k - 1], device_id_type=MESH) for k in range(1, N_DEV)]
        for cp in cps:
            cp.start()
        mine = pltpu.make_async_copy(piece(me), rb_ref.at[0], local_sem)
        mine.start()
        for cp in cps:
            cp.wait()
        mine.wait()

    return _launch(name, body, [jax.ShapeDtypeStruct((N_DEV, rows, LANES), F32)], [packed],
                   [pltpu.SemaphoreType.DMA((N_DEV,)), pltpu.SemaphoreType.DMA((N_DEV,)), pltpu.SemaphoreType.DMA], sequencer_id)[0]


def _small_sum(name, pieces):
    def body(p_ref, o_ref):
        acc = p_ref[0]
        for k in range(1, N_DEV):
            acc = acc + p_ref[k]
        o_ref[...] = acc

    vm = pl.BlockSpec(memory_space=pltpu.VMEM)
    return pl.pallas_call(body, out_shape=jax.ShapeDtypeStruct(pieces.shape[1:], F32), in_specs=[vm], out_specs=vm, name=name)(pieces)


def _small_gather(name, tot, sequencer_id):
    rows = tot.shape[0]

    def body(t_ref, out_ref, send_sems, recv_sems, local_sem):
        x, y, c = _coords()
        me = 4 * x + 2 * y + c
        peers = _all_peers(x, y, c)
        _handshake(peers)

        def piece(idx):
            return out_ref.at[pl.ds(pl.multiple_of(idx * rows, SUBLANES), rows), :]

        cps = [pltpu.make_async_remote_copy(src_ref=t_ref, dst_ref=piece(me), send_sem=send_sems.at[k], recv_sem=recv_sems.at[k],
                                            device_id=peers[k - 1], device_id_type=MESH) for k in range(1, N_DEV)]
        for cp in cps:
            cp.start()
        mine = pltpu.make_async_copy(t_ref, piece(me), local_sem)
        mine.start()
        for cp in cps:
            cp.wait()
        mine.wait()

    return _launch(name, body, [jax.ShapeDtypeStruct((N_DEV * rows, LANES), F32)], [tot],
                   [pltpu.SemaphoreType.DMA((N_DEV,)), pltpu.SemaphoreType.DMA((N_DEV,)), pltpu.SemaphoreType.DMA], sequencer_id)[0]


def _all_reduce_small(name, packed):
    rows = packed.shape[0] // N_DEV

    def body(p_ref, out_ref, rb, tot, send_sems, recv_sems):
        x, y, c = _coords()
        me = 4 * x + 2 * y + c

        def peer(k):
            return (x ^ (k >> 2), y ^ ((k >> 1) & 1), c ^ (k & 1))

        def rows_of(idx):
            return pl.ds(pl.multiple_of(idx * rows, SUBLANES), rows)

        def piece(ref, idx):
            return ref.at[rows_of(idx), :]

        scatter = [pltpu.make_async_remote_copy(src_ref=piece(p_ref, me ^ k), dst_ref=rb.at[k], send_sem=send_sems.at[k],
                                                recv_sem=recv_sems.at[k], device_id=peer(k), device_id_type=MESH) for k in range(1, N_DEV)]
        for cp in scatter:
            cp.start()
        acc = p_ref[rows_of(me), :]
        for cp in scatter:
            cp.wait_recv()
        for k in range(1, N_DEV):
            acc = acc + rb[k]
        tot[...] = acc
        out_ref[rows_of(me), :] = acc
        gather = [pltpu.make_async_remote_copy(src_ref=tot, dst_ref=piece(out_ref, me), send_sem=send_sems.at[N_DEV + k],
                                               recv_sem=recv_sems.at[N_DEV + k], device_id=peer(k), device_id_type=MESH)
                  for k in range(1, N_DEV)]
        for cp in gather:
            cp.start()
        for k in range(1, N_DEV):
            pltpu.make_async_remote_copy(src_ref=tot, dst_ref=piece(out_ref, me ^ k), send_sem=send_sems.at[N_DEV + k],
                                         recv_sem=recv_sems.at[N_DEV + k], device_id=peer(k), device_id_type=MESH).wait_recv()
        for cp in scatter + gather:
            cp.wait_send()

    vm = pl.BlockSpec(memory_space=pltpu.VMEM)
    return pl.pallas_call(
        body, out_shape=jax.ShapeDtypeStruct(packed.shape, F32), in_specs=[vm], out_specs=vm,
        scratch_shapes=[pltpu.VMEM((N_DEV, rows, LANES), F32), pltpu.VMEM((rows, LANES), F32),
                        pltpu.SemaphoreType.DMA((2 * N_DEV,)), pltpu.SemaphoreType.DMA((2 * N_DEV,))],
        compiler_params=pltpu.CompilerParams(vmem_limit_bytes=VMEM_LIMIT), name=name)(packed)


def _adamw_math(g, w, m, v):
    m = ADAM_B1 * m + (1.0 - ADAM_B1) * g
    v = ADAM_B2 * v + (1.0 - ADAM_B2) * (g * g)
    delta = -ADAM_LR * ((m / ADAM_C1) / (jnp.sqrt(v / ADAM_C2) + ADAM_EPS) + ADAM_WD * w)
    return delta, m, v


def _slab_spec(kind, shard_shape, tr, slab_of):
    r, c = shard_shape
    if kind == "col":
        return pl.BlockSpec((tr, c), lambda q, i, sc: (i, slab_of(q, sc)))
    return pl.BlockSpec((tr, c), lambda q, i, sc: (slab_of(q, sc) * (r // tr) + i, 0))


def _chip_sum(name, partial, recv, kind, shard_shape, core):
    r, c = shard_shape
    tr = _blk(r, 1024)

    def body(core_ref, p_ref, r_ref, o_ref):
        o_ref[...] = (p_ref[...].astype(F32) + r_ref[...].astype(F32)).astype(BF16)

    spec4 = pl.BlockSpec((None, tr, c), lambda q, i, sc: (q, i, 0))
    grid_spec = pltpu.PrefetchScalarGridSpec(
        num_scalar_prefetch=1, grid=(N_CHIP, r // tr),
        in_specs=[_slab_spec(kind, shard_shape, tr, lambda q, sc: 2 * q + sc[0]), spec4], out_specs=spec4)
    return pl.pallas_call(body, out_shape=jax.ShapeDtypeStruct((N_CHIP, r, c), BF16), grid_spec=grid_spec,
                          compiler_params=_params(("parallel", "parallel")), name=name)(core, partial, recv)


def _adamw_shard(name, parts, w, m, v, chip):
    r, c = w.shape
    n_parts = len(parts)
    tr = _blk(r // n_parts, 256)
    per = r // n_parts // tr

    def body(chip_ref, *refs):
        src, (w_ref, m_ref, v_ref), (g_out, d_out, m_out, v_out) = refs[:2 * n_parts], refs[2 * n_parts:2 * n_parts + 3], refs[2 * n_parts + 3:]
        for p in range(n_parts):
            @pl.when(pl.program_id(0) // per == p)
            def _():
                g = src[2 * p][...].astype(F32)
                for k in range(3):
                    g = g + src[2 * p + 1][k].astype(F32)
                g_out[...] = g
                d_out[...], m_out[...], v_out[...] = _adamw_math(g, w_ref[...], m_ref[...], v_ref[...])

    def part_specs(p):
        at = lambda i: jnp.clip(i - p * per, 0, per - 1)
        return [pl.BlockSpec((None, tr, c), lambda i, sc: (sc[0], at(i), 0)), pl.BlockSpec((3, tr, c), lambda i, sc: (0, at(i), 0))]

    blk = pl.BlockSpec((tr, c), lambda i, sc: (i, 0))
    grid_spec = pltpu.PrefetchScalarGridSpec(
        num_scalar_prefetch=1, grid=(r // tr,), in_specs=[s for p in range(n_parts) for s in part_specs(p)] + [blk, blk, blk], out_specs=[blk] * 4)
    return pl.pallas_call(body, out_shape=[jax.ShapeDtypeStruct((r, c), F32)] * 4, grid_spec=grid_spec,
                          compiler_params=_params(("parallel",)), name=name)(chip, *[a for p in parts for a in p], w, m, v)


def _adamw_small(name, g, w, m, v):
    def body(g_ref, w_ref, m_ref, v_ref, d_out, m_out, v_out):
        d_out[...], m_out[...], v_out[...] = _adamw_math(g_ref[...], w_ref[...], m_ref[...], v_ref[...])

    vm = pl.BlockSpec(memory_space=pltpu.VMEM)
    return pl.pallas_call(body, out_shape=[jax.ShapeDtypeStruct(g.shape, F32)] * 3, in_specs=[vm] * 4, out_specs=[vm] * 3,
                          compiler_params=pltpu.CompilerParams(vmem_limit_bytes=VMEM_LIMIT), name=name)(g, w, m, v)


def _pack_rows(arrays, total_rows):
    flat = [a.reshape(-1, LANES) for a in arrays]
    used = sum(f.shape[0] for f in flat)
    return jnp.concatenate(flat + [jnp.zeros((total_rows - used, LANES), F32)], axis=0)


def _unpack_rows(packed, like):
    out, at = [], 0
    for a in like:
        n = a.size // LANES
        out.append(packed[at:at + n].reshape(a.shape))
        at += n
    return out


def kernel(x, norm_mix_g, w_in, conv_w, conv_b, lru_wa, lru_ba, lru_wx, lru_bx, lru_lambda, w_proj_attn, w_proj_lru, w_out, norm_mlp_g, w_up, w_down, norm_final_g, loss_target, m_norm_mix_g, m_w_in, m_conv_w, m_conv_b, m_lru_wa, m_lru_ba, m_lru_wx, m_lru_bx, m_lru_lambda, m_w_proj_attn, m_w_proj_lru, m_w_out, m_norm_mlp_g, m_w_up, m_w_down, m_norm_final_g, v_norm_mix_g, v_w_in, v_conv_w, v_conv_b, v_lru_wa, v_lru_ba, v_lru_wx, v_lru_bx, v_lru_lambda, v_w_proj_attn, v_w_proj_lru, v_w_out, v_norm_mlp_g, v_w_up, v_w_down, v_norm_final_g):
    xs, tgt = x[0], loss_target[0]
    s, d = xs.shape
    nh = d // HEAD
    ix, iy, ic = _coords()
    core = jnp.reshape(ic, (1,)).astype(jnp.int32)
    chip = jnp.reshape(2 * ix + iy, (1,)).astype(jnp.int32)
    dev = 4 * ix + 2 * iy + ic

    big = [w_in[0], w_proj_attn[0], w_proj_lru[0], w_out[0], w_up[0], w_down[0]]
    big_m = [m_w_in[0], m_w_proj_attn[0], m_w_proj_lru[0], m_w_out[0], m_w_up[0], m_w_down[0]]
    big_v = [v_w_in[0], v_w_proj_attn[0], v_w_proj_lru[0], v_w_out[0], v_w_up[0], v_w_down[0]]
    kinds = ["col", "row", "row", "row", "col", "row"]
    pad_taps = lambda t: jnp.pad(t, ((0, SUBLANES - CONV_TAPS), (0, 0)))
    shards = [w.astype(BF16) for w in big]
    pad_taps2 = lambda t: jnp.pad(t, ((0, 2 * SUBLANES - CONV_TAPS), (0, 0)))
    win, cw_slots = _all_gather("all_gather_w_in", [shards[0], pad_taps2(conv_w[0])], ["col", "slot"], sequencer_id=7)
    later = lax.optimization_barrier((shards[1:], win))[0]
    wpa, wpl, wout = _all_gather("all_gather_mix", later[:3], kinds[1:4], sequencer_id=1)
    wup, wdown = _all_gather("all_gather_mlp", later[3:], kinds[4:], sequencer_id=5)
    cw8 = jnp.transpose(cw_slots[:, :SUBLANES], (1, 0, 2)).reshape(SUBLANES, d)
    row_id = lax.broadcasted_iota(jnp.int32, (SUBLANES, d), 0)
    vec8 = sum(jnp.where(row_id == k, t, 0.0) for k, t in ((VEC_CB, conv_b), (VEC_BA, lru_ba), (VEC_BX, lru_bx), (VEC_LAM, lru_lambda)))
    wa16, wx16 = lru_wa[0].astype(BF16), lru_wx[0].astype(BF16)
    slopes = 2.0 ** (-8.0 * jnp.arange(1, nh + 1, dtype=F32) / nh)

    def seg_specs(*segs):
        return lambda bm, bn: [pl.BlockSpec((bm, bn), (lambda i, j, kk, sg=sg: (i, sg * (d // bn) + j))) for sg in segs]

    def plain_specs(k):
        return lambda bm, bn: [pl.BlockSpec((bm, bn), lambda i, j, kk: (i, j)) for _ in range(k)]

    xn = _rms_fwd("norm_mix", xs, norm_mix_g)
    proj = _mm_fwd("proj_in", xn, win, 0, 7 * d, [F32], bm=2048)[0]
    att, lse = _attn_fwd(proj, d, slopes)
    xc, hp, h2d, ylru = _lru_fwd(proj, d, cw8, vec8, wa16, wx16)
    pa = _mm_fwd("proj_attn", att, wpa, 0, d, [BF16], bm=2048)[0]

    def merge(acc, pa_b, ga, gl):
        return acc, _sigmoid(ga) * pa_b.astype(F32) + _sigmoid(gl) * acc

    plr, merged = _mm_fwd("proj_lru_merge", ylru, wpl, 0, d, [BF16, BF16], merge, (pa, proj, proj),
                          lambda bm, bn: plain_specs(1)(bm, bn) + seg_specs(SEG_GA, SEG_GL)(bm, bn), bn=512)
    h1 = _mm_fwd("mix_out", merged, wout, 0, d, [F32], lambda acc, r: (acc + r,), (xs,), plain_specs(1))[0]
    hn = _rms_fwd("norm_mlp", h1, norm_mlp_g)

    def relu2(acc):
        return acc, jnp.square(jnp.maximum(acc, 0.0))

    up, hid = _mm_fwd("mlp_up", hn, wup, 0, wup.shape[1], [BF16, BF16], relu2, bm=2048)
    h2 = _mm_fwd("mlp_down", hid, wdown, 0, d, [F32], lambda acc, r: (acc + r,), (h1,), plain_specs(1))[0]
    dh2, dh2b, dg3, loss_lanes = _final_loss(h2, tgt, norm_final_g.reshape(1, d))
    loss = lax.psum(0.5 / d * jnp.sum(loss_lanes), ("x", "y", "c"))
    dh2b = lax.optimization_barrier((dh2b, loss))[0]

    def reduce_group(tag, kk, shp, partials, from_sibling, sequencer_id):
        sums = [_chip_sum(f"chip_sum_{tag}_{i}", p, f, k, sh, core) for i, (p, f, k, sh) in enumerate(zip(partials, from_sibling, kk, shp))]
        return list(zip(sums, _exchange_chips(f"rs_chips_{tag}", sums, sequencer_id)))

    dup = _mm_nt("mlp_down_dx", dh2b, wdown, [BF16], lambda acc, u: (acc * (2.0 * jnp.maximum(u.astype(F32), 0.0)),), (up,), plain_specs(1))[0]
    g_wdown = _mm_tn("mlp_down_dw", hid, dh2b)
    g_wup = _mm_tn("mlp_up_dw", hn, dup)
    shp_mlp = [w.shape for w in big[4:]]
    (dhn,), sib_mlp = _mm_nt("mlp_up_dx", dup, wup, [F32], side=_sibling_side([g_wup, g_wdown], kinds[4:], shp_mlp))
    red_up, red_down = reduce_group("mlp", kinds[4:], shp_mlp, [g_wup, g_wdown], sib_mlp, 2)
    dhn = lax.optimization_barrier((dhn, red_up[0], red_down[0]))[0]
    dh1, dh1b, dg2 = _rms_bwd("norm_mlp_bwd", h1, norm_mlp_g, dhn, dh2)

    def merge_bwd(acc, pa_b, pl_b, ga, gl):
        sa, sl = _sigmoid(ga), _sigmoid(gl)
        return acc * sa, acc * sl, acc * pa_b.astype(F32) * sa * (1.0 - sa), acc * pl_b.astype(F32) * sl * (1.0 - sl)

    dpa, dpl, dga, dgl = _mm_nt("mix_out_dx", dh1b, wout, [BF16] * 4, merge_bwd, (pa, plr, proj, proj),
                                lambda bm, bn: plain_specs(2)(bm, bn) + seg_specs(SEG_GA, SEG_GL)(bm, bn), bn=512)
    g_wout = _mm_tn("mix_out_dw", merged, dh1b)
    datt = _mm_nt("proj_attn_dx", dpa, wpa, [F32], bm=2048)[0]
    g_wpa = _mm_tn("proj_attn_dw", att, dpa)

    def lru_out_bwd(acc, h_b, gate):
        return acc * _gelu(gate), acc * h_b.astype(F32) * _gelu_grad(gate)

    g_wpl = _mm_tn("proj_lru_dw", ylru, dpl)
    shp_mix = [w.shape for w in big[1:4]]
    (dh, dxg), sib_mix = _mm_nt("proj_lru_dx", dpl, wpl, [F32, BF16], lru_out_bwd, (h2d, proj),
                                lambda bm, bn: plain_specs(1)(bm, bn) + seg_specs(SEG_GATE)(bm, bn), bn=512,
                                side=_sibling_side([g_wpa, g_wpl, g_wout], kinds[1:4], shp_mix))
    red_pa, red_pl, red_out = reduce_group("mix", kinds[1:4], shp_mix, [g_wpa, g_wpl, g_wout], sib_mix, 3)
    dxr, dwa, dwx, dvec, dconv = _lru_bwd(hp, dh, xc, proj, cw8, wa16, wx16, vec8)
    dproj = _attn_bwd(proj, d, datt, att, lse, slopes, (dxr, dxg, dga, dgl))

    def small_step(tag, grads, ws, ms, vs, like, after, seq=None):
        n_rows = sum(g.size for g in grads) // LANES
        per_dev = -(-n_rows // (N_DEV * SUBLANES)) * SUBLANES
        packed = lax.optimization_barrier((_pack_rows(grads, N_DEV * per_dev), after))[0]
        if seq is None:
            total = _all_reduce_small(f"all_reduce_{tag}", packed)
        else:
            pieces = lax.optimization_barrier((_small_scatter(f"scatter_{tag}", packed, seq[0]), seq[2]))[0]
            total = _small_gather(f"gather_{tag}", _small_sum(f"sum_{tag}", pieces), seq[1])
        w_rows = -(-(sum(w.size for w in ws) // LANES) // SUBLANES) * SUBLANES
        upd = _adamw_small(f"adamw_{tag}", total[:w_rows], _pack_rows(ws, w_rows), _pack_rows(ms, w_rows), _pack_rows(vs, w_rows))
        return _unpack_rows(total, like), [_unpack_rows(t, ws) for t in upd]

    early_w = [conv_b, lru_wa, lru_ba, lru_wx, lru_bx, lru_lambda, norm_mlp_g, norm_final_g]
    early_m = [m_conv_b, m_lru_wa, m_lru_ba, m_lru_wx, m_lru_bx, m_lru_lambda, m_norm_mlp_g, m_norm_final_g]
    early_v = [v_conv_b, v_lru_wa, v_lru_ba, v_lru_wx, v_lru_bx, v_lru_lambda, v_norm_mlp_g, v_norm_final_g]
    early_g = [dconv[CONV_TAPS:CONV_TAPS + 1], dwa, dvec[VEC_BA:VEC_BA + 1], dwx, dvec[VEC_BX:VEC_BX + 1],
               dvec[VEC_LAM:VEC_LAM + 1], dg2, dg3, dconv[0:CONV_TAPS]]
    dproj = lax.optimization_barrier((dproj, red_up[1], red_down[1]))[0]
    dproj = lax.optimization_barrier((dproj, red_pa[1], red_pl[1], red_out[1]))[0]
    early_sum, early_upd = small_step("small", early_g, early_w, early_m, early_v,
                                      early_w + [jax.ShapeDtypeStruct((1, CONV_TAPS, d), F32)], dxr, seq=(8, 9, dproj))
    g_cw_full = early_sum[-1]
    cshard = conv_w.shape[2]
    g_cw = lax.dynamic_slice(g_cw_full, (0, 0, dev * cshard), (1, CONV_TAPS, cshard))
    cw_delta, cw_m, cw_v = (t[:CONV_TAPS][None] for t in _adamw_small(
        "adamw_conv_w", pad_taps(g_cw[0]), pad_taps(conv_w[0]), pad_taps(m_conv_w[0]), pad_taps(v_conv_w[0])))
    half = (big[0].shape[0] // 2, big[0].shape[1])
    g_in0 = _mm_tn("proj_in_dw_0", xn, dproj, part=(0, 2))
    g_in1, sib_in0 = _mm_tn("proj_in_dw_1", xn, dproj, part=(1, 2), side=_sibling_side([g_in0], ["col"], [half]))
    red_in = reduce_group("in_0", ["col"], [half], [g_in0], sib_in0, 4)
    dproj = lax.optimization_barrier((dproj, red_in[0][0], early_sum))[0]
    (dxn0,), sib_in1 = _mm_nt("proj_in_dx_0", dproj, win, [F32], part=(0, 2), side=_sibling_side([g_in1], ["col"], [half]))
    red_in += reduce_group("in_1", ["col"], [half], [g_in1], sib_in1, 6)
    dproj = lax.optimization_barrier((dproj, red_in[1][0]))[0]
    dxn1 = _mm_nt("proj_in_dx_1", dproj, win, [F32], part=(1, 2))[0]
    dxn = jnp.concatenate([dxn0, dxn1], axis=0)
    dxn = lax.optimization_barrier((dxn, red_in[0][1]))[0]
    grad_x, _, dg1 = _rms_bwd("norm_mix_bwd", xs, norm_mix_g, dxn, dh1)
    red_up, red_down = lax.optimization_barrier(((red_up, red_down), dg1))[0]
    big_out = {i: _adamw_shard(f"adamw_{i}", [red], big[i], big_m[i], big_v[i], chip) for i, red in ((4, red_up), (5, red_down))}
    big_out.update({i: _adamw_shard(f"adamw_{i}", [red], big[i], big_m[i], big_v[i], chip) for i, red in ((1, red_pa), (2, red_pl), (3, red_out))})
    late_sum, late_upd = small_step("norm_mix", [dg1], [norm_mix_g], [m_norm_mix_g], [v_norm_mix_g], [norm_mix_g], (big_out[4], big_out[5]))
    big_out[0] = _adamw_shard("adamw_0", red_in, big[0], big_m[0], big_v[0], chip)
    s_grad = late_sum + early_sum[:-1]
    s_delta, s_m, s_v = (late_upd[j] + early_upd[j] for j in range(3))


    names = ["norm_mix_g", "w_in", "conv_w", "conv_b", "lru_wa", "lru_ba", "lru_wx", "lru_bx", "lru_lambda", "w_proj_attn", "w_proj_lru",
             "w_out", "norm_mlp_g", "w_up", "w_down", "norm_final_g"]
    small_names = ["norm_mix_g", "conv_b", "lru_wa", "lru_ba", "lru_wx", "lru_bx", "lru_lambda", "norm_mlp_g", "norm_final_g"]
    big_names = ["w_in", "w_proj_attn", "w_proj_lru", "w_out", "w_up", "w_down"]
    res = {"conv_w": (g_cw, cw_delta, cw_m, cw_v)}
    for i, nm in enumerate(small_names):
        res[nm] = (s_grad[i], s_delta[i], s_m[i], s_v[i])
    for i, nm in enumerate(big_names):
        res[nm] = tuple(t[None] for t in big_out[i])
    return (loss, grad_x[None], *[res[nm][0] for nm in names], *[res[nm][1] for nm in names],
            *[res[nm][2] for nm in names], *[res[nm][3] for nm in names])
```

```python
import jax
import jax.numpy as jnp
from jax import lax
from jax.experimental import pallas as pl
from jax.experimental.pallas import tpu as pltpu
from jax.experimental.pallas import tpu_sc as plsc

F32, BF16 = jnp.float32, jnp.bfloat16
MESH = pl.DeviceIdType.MESH
HBM = pl.BlockSpec(memory_space=pltpu.HBM)
N_DEV = 8
N_CHIP = 4
HEAD = 128
SPAN = 128
DILATIONS = (1, 4, 16)
CONV_TAPS = 4
LRU_C = 8.0
NORM_EPS = 1e-6
LANES = 128
SUBLANES = 8
VMEM_LIMIT = 56 * 1024 * 1024
ADAM_LR, ADAM_B1, ADAM_B2, ADAM_EPS, ADAM_WD, ADAM_STEP = 0.001, 0.9, 0.999, 1e-08, 0.01, 10
ADAM_C1 = 1.0 - ADAM_B1 ** ADAM_STEP
ADAM_C2 = 1.0 - ADAM_B2 ** ADAM_STEP
NEG = -1e30


def _params(sem=None):
    return pltpu.CompilerParams(dimension_semantics=sem, vmem_limit_bytes=VMEM_LIMIT)


def _sigmoid(v):
    return 1.0 / (1.0 + jnp.exp(-v))


def _gelu(v):
    k = 0.7978845608028654
    return 0.5 * v * (1.0 + jnp.tanh(k * (v + 0.044715 * v * v * v)))


def _gelu_grad(v):
    k = 0.7978845608028654
    t = jnp.tanh(k * (v + 0.044715 * v * v * v))
    return 0.5 * (1.0 + t) + 0.5 * v * (1.0 - t * t) * k * (1.0 + 3.0 * 0.044715 * v * v)


NN = (((1,), (0,)), ((), ()))
NT = (((1,), (1,)), ((), ()))
TN = (((0,), (0,)), ((), ()))


def _mm(name, a, a_spec, b, b_spec, dn, grid, out_shapes, out_specs, acc_block, epilogue=None, extras=(), extra_specs=(), side=None):
    nk, ne, no = grid[2], len(extras), len(out_shapes)
    side_ops, side_shapes, side_copies, make_copies = side if side is not None else ((), (), 0, None)
    ns_in, ns_out = len(side_ops), len(side_shapes)

    def body(*refs):
        a_ref, b_ref = refs[0], refs[1]
        ex, side_in = refs[2:2 + ne], refs[2 + ne:2 + ne + ns_in]
        outs = refs[2 + ne + ns_in:2 + ne + ns_in + no]
        side_out = refs[2 + ne + ns_in + no:2 + ne + ns_in + no + ns_out]
        scratch = refs[2 + ne + ns_in + no + ns_out:]
        at = [pl.program_id(ax) for ax in range(3)]
        if side is not None:
            @pl.when((at[0] == 0) & (at[1] == 0) & (at[2] == 0))
            def _():
                for cp in make_copies(side_in, side_out, scratch[-2], scratch[-1]):
                    cp.start()

        part = lax.dot_general(a_ref[...], b_ref[...], dn, preferred_element_type=F32)

        def finish(acc):
            vals = epilogue(acc, *[e[...] for e in ex]) if epilogue is not None else (acc,)
            for o, v in zip(outs, vals):
                o[...] = v.astype(o.dtype)

        if nk == 1:
            finish(part)
        else:
            acc_ref, k = scratch[0], at[2]

            @pl.when(k == 0)
            def _():
                acc_ref[...] = part

            @pl.when(k > 0)
            def _():
                acc_ref[...] += part

            @pl.when(k == nk - 1)
            def _():
                finish(acc_ref[...])

        if side is not None:
            @pl.when((at[0] == grid[0] - 1) & (at[1] == grid[1] - 1) & (at[2] == grid[2] - 1))
            def _():
                for cp in make_copies(side_in, side_out, scratch[-2], scratch[-1]):
                    cp.wait()

    scratch_shapes = [pltpu.VMEM(acc_block, F32)] if nk > 1 else []
    if side is not None:
        scratch_shapes += [pltpu.SemaphoreType.DMA((side_copies,)), pltpu.SemaphoreType.DMA((side_copies,))]
    res = pl.pallas_call(
        body, out_shape=[*out_shapes, *side_shapes], grid=grid, in_specs=[a_spec, b_spec, *extra_specs, *[HBM] * ns_in],
        out_specs=[*out_specs, *[HBM] * ns_out], scratch_shapes=scratch_shapes,
        compiler_params=_params(("arbitrary",) * 3 if side is not None else ("parallel", "parallel", "arbitrary")),
        name=name)(a, b, *extras, *side_ops)
    return res if side is None else (res[:no], res[no:])


def _blk(n, pref):
    return pref if n % pref == 0 else n


def _kblk(k):
    return k if k <= 2048 else next(b for b in (2048, 1024, 512) if k % b == 0)


def _mm_fwd(name, a, w, col0, ncols, out_dtypes, epilogue=None, extras=(), extra_specs_fn=None, bm=1024, bn=1024):
    m, k = a.shape
    bm, bn = _blk(m, bm), _blk(ncols, bn)
    bk = _kblk(k)
    nk = k // bk
    cb0 = col0 // bn
    grid = (m // bm, ncols // bn, nk)
    a_spec = pl.BlockSpec((bm, bk), lambda i, j, kk: (i, kk))
    b_spec = pl.BlockSpec((bk, bn), lambda i, j, kk: (kk, cb0 + j))
    shapes = [jax.ShapeDtypeStruct((m, ncols), dt) for dt in out_dtypes]
    specs = [pl.BlockSpec((bm, bn), lambda i, j, kk: (i, j)) for _ in out_dtypes]
    ex_specs = extra_specs_fn(bm, bn) if extra_specs_fn else ()
    return _mm(name, a, a_spec, w, b_spec, NN, grid, shapes, specs, (bm, bn), epilogue, extras, ex_specs)


def _mm_nt(name, a, w, out_dtypes, epilogue=None, extras=(), extra_specs_fn=None, part=(0, 1), side=None, bm=1024, bn=1024):
    n = w.shape[0]
    if a.ndim == 3:
        seg_cols, m, k = a.shape[2], a.shape[1], a.shape[0] * a.shape[2]
    else:
        m, k = a.shape
    m = m // part[1]
    bm, bn = _blk(m, bm), _blk(n, bn)
    bk = _kblk(k)
    grid = (m // bm, n // bn, k // bk)
    i0 = part[0] * (m // bm)
    if a.ndim == 3:
        per = seg_cols // bk
        a_spec = pl.BlockSpec((None, bm, bk), lambda i, j, kk: (kk // per, i0 + i, kk % per))
    else:
        a_spec = pl.BlockSpec((bm, bk), lambda i, j, kk: (i0 + i, kk))
    b_spec = pl.BlockSpec((bn, bk), lambda i, j, kk: (j, kk))
    shapes = [jax.ShapeDtypeStruct((m, n), dt) for dt in out_dtypes]
    specs = [pl.BlockSpec((bm, bn), lambda i, j, kk: (i, j)) for _ in out_dtypes]
    ex_specs = extra_specs_fn(bm, bn) if extra_specs_fn else ()
    return _mm(name, a, a_spec, w, b_spec, NT, grid, shapes, specs, (bm, bn), epilogue, extras, ex_specs, side)


def _mm_tn(name, a, b, part=(0, 1), side=None, bm=1024, bn=2048):
    t, m = a.shape
    n = b.shape[1] if b.ndim == 2 else b.shape[0] * b.shape[2]
    m = m // part[1]
    bm, bn = _blk(m, bm), _blk(n, bn)
    grid = (m // bm, n // bn, 1)
    i0 = part[0] * (m // bm)
    a_spec = pl.BlockSpec((t, bm), lambda i, j, kk: (0, i0 + i))
    if b.ndim == 3:
        per = b.shape[2] // bn
        b_spec = pl.BlockSpec((None, t, bn), lambda i, j, kk: (j // per, 0, j % per))
    else:
        b_spec = pl.BlockSpec((t, bn), lambda i, j, kk: (0, j))
    res = _mm(name, a, a_spec, b, b_spec, TN, grid, [jax.ShapeDtypeStruct((m, n), BF16)],
              [pl.BlockSpec((bm, bn), lambda i, j, kk: (i, j))], (bm, bn), side=side)
    return res[0] if side is None else (res[0][0], res[1])


ROWS = 256


def _row_spec(d):
    return pl.BlockSpec((ROWS, d), lambda i: (i, 0))


def _vec_spec(d, rows=1):
    return pl.BlockSpec((rows, d), lambda i: (0, 0))


def _rms_fwd(name, x, g):
    s, d = x.shape

    def body(x_ref, g_ref, o_ref):
        xv = x_ref[...]
        r = lax.rsqrt(jnp.mean(xv * xv, axis=-1, keepdims=True) + NORM_EPS)
        o_ref[...] = (xv * r * g_ref[...]).astype(BF16)

    return pl.pallas_call(body, out_shape=jax.ShapeDtypeStruct((s, d), BF16), grid=(s // ROWS,),
                          in_specs=[_row_spec(d), _vec_spec(d)], out_specs=_row_spec(d),
                          compiler_params=_params(("parallel",)), name=name)(x, g)


def _rms_bwd_math(xv, g, dy):
    r = lax.rsqrt(jnp.mean(xv * xv, axis=-1, keepdims=True) + NORM_EPS)
    n = xv * r
    z = dy * g
    dx = r * (z - n * jnp.mean(z * n, axis=-1, keepdims=True))
    return dx, jnp.sum(dy * n, axis=0, keepdims=True)


def _rms_bwd(name, x, g, dy, resid):
    s, d = x.shape

    def body(x_ref, g_ref, dy_ref, r_ref, dx_ref, dxb_ref, dg_ref):
        dx, dg = _rms_bwd_math(x_ref[...], g_ref[...], dy_ref[...])
        dx = dx + r_ref[...]
        dx_ref[...] = dx
        dxb_ref[...] = dx.astype(BF16)

        @pl.when(pl.program_id(0) == 0)
        def _():
            dg_ref[...] = jnp.zeros_like(dg_ref)

        dg_ref[...] += dg

    return pl.pallas_call(
        body, out_shape=[jax.ShapeDtypeStruct((s, d), F32), jax.ShapeDtypeStruct((s, d), BF16), jax.ShapeDtypeStruct((1, d), F32)],
        grid=(s // ROWS,), in_specs=[_row_spec(d), _vec_spec(d), _row_spec(d), _row_spec(d)],
        out_specs=[_row_spec(d), _row_spec(d), _vec_spec(d)], compiler_params=_params(("arbitrary",)), name=name)(x, g, dy, resid)


def _final_loss(h2, tgt, g):
    s, d = h2.shape

    def body(x_ref, t_ref, g_ref, dx_ref, dxb_ref, dg_ref, ls_ref):
        xv, gv = x_ref[...], g_ref[...]
        r = lax.rsqrt(jnp.mean(xv * xv, axis=-1, keepdims=True) + NORM_EPS)
        diff = xv * r * gv - t_ref[...]
        dx, dg = _rms_bwd_math(xv, gv, diff * (1.0 / d))
        dx_ref[...] = dx
        dxb_ref[...] = dx.astype(BF16)

        @pl.when(pl.program_id(0) == 0)
        def _():
            dg_ref[...] = jnp.zeros_like(dg_ref)
            ls_ref[...] = jnp.zeros_like(ls_ref)

        dg_ref[...] += dg
        ls_ref[...] += jnp.sum(diff * diff, axis=0, keepdims=True)

    return pl.pallas_call(
        body, out_shape=[jax.ShapeDtypeStruct((s, d), F32), jax.ShapeDtypeStruct((s, d), BF16),
                         jax.ShapeDtypeStruct((1, d), F32), jax.ShapeDtypeStruct((1, d), F32)],
        grid=(s // ROWS,), in_specs=[_row_spec(d), _row_spec(d), _vec_spec(d)],
        out_specs=[_row_spec(d), _row_spec(d), _vec_spec(d), _vec_spec(d)],
        compiler_params=_params(("arbitrary",)), name="final_norm_loss")(h2, tgt, g)


ATTN_Q = 128


ATTN_BATCH = 8


def _attn_units(s):
    units = []
    for gi, d in enumerate(DILATIONS):
        for r in range(d):
            for q0 in range(0, s // d, ATTN_Q):
                k0 = max(q0 - SPAN, 0)
                units.append((gi, d, r, q0, k0, q0 + ATTN_Q - k0))
    return units


def _stream_rows(d, r, start, size):
    return pl.ds(r + start * d, size) if d == 1 else pl.ds(r + start * d, size, stride=d)


def _attn_scores(q_ref, k_ref, slope, d, r, q0, k0, nk):
    qrows, krows = _stream_rows(d, r, q0, ATTN_Q), _stream_rows(d, r, k0, nk)
    qb = q_ref[qrows, :].astype(BF16)
    kb = k_ref[krows, :].astype(BF16)
    sc = lax.dot_general(qb, kb, NT, preferred_element_type=F32) * (HEAD ** -0.5)
    qi = lax.broadcasted_iota(jnp.int32, (ATTN_Q, nk), 0)
    kj = lax.broadcasted_iota(jnp.int32, (ATTN_Q, nk), 1)
    dist = (q0 - k0) + qi - kj
    valid = (dist >= 0) & (dist <= SPAN)
    sc = sc - (slope * d) * dist.astype(F32)
    return jnp.where(valid, sc, NEG), valid, qb, kb, qrows, krows


def _attn_fwd(proj, dm, slopes):
    s = proj.shape[0]
    units = _attn_units(s)

    def body(sl_ref, q_ref, k_ref, v_ref, att_ref, lse_ref, *scr):
        o_scr, l_scr = scr[:3], scr[3:]
        slope = sl_ref[pl.program_id(0)]
        for first in range(0, len(units), ATTN_BATCH):
            batch = units[first:first + ATTN_BATCH]
            scored = [_attn_scores(q_ref, k_ref, slope, d, r, q0, k0, nk) for _, d, r, q0, k0, nk in batch]
            soft = []
            for sc, _, _, _, _, _ in scored:
                m = jnp.max(sc, axis=-1, keepdims=True)
                p = jnp.exp(sc - m)
                soft.append((m, p, jnp.sum(p, axis=-1, keepdims=True)))
            outs = [lax.dot_general(p.astype(BF16), v_ref[sco[5], :].astype(BF16), NN, preferred_element_type=F32)
                    for (m, p, l), sco in zip(soft, scored)]
            for (gi, *_), (m, p, l), sco, o in zip(batch, soft, scored, outs):
                o_scr[gi][sco[4], :] = o / l
                l_scr[gi][sco[4], :] = jnp.broadcast_to(m + jnp.log(l), (ATTN_Q, HEAD))
        l0, l1, l2 = l_scr[0][...], l_scr[1][...], l_scr[2][...]
        m = jnp.maximum(jnp.maximum(l0, l1), l2)
        w0, w1, w2 = jnp.exp(l0 - m), jnp.exp(l1 - m), jnp.exp(l2 - m)
        tot = w0 + w1 + w2
        att_ref[...] = ((w0 * o_scr[0][...] + w1 * o_scr[1][...] + w2 * o_scr[2][...]) / tot).astype(BF16)
        lse_ref[...] = m + jnp.log(tot)

    def seg(i):
        return pl.BlockSpec((s, HEAD), lambda h: (0, i * (dm // HEAD) + h))

    col = pl.BlockSpec((s, HEAD), lambda h: (0, h))
    return pl.pallas_call(
        body, out_shape=[jax.ShapeDtypeStruct((s, dm), BF16), jax.ShapeDtypeStruct((s, dm), F32)], grid=(dm // HEAD,),
        in_specs=[pl.BlockSpec(memory_space=pltpu.SMEM), seg(0), seg(1), seg(2)], out_specs=[col, col],
        scratch_shapes=[pltpu.VMEM((s, HEAD), F32)] * (2 * len(DILATIONS)),
        compiler_params=_params(("parallel",)), name="attn_fwd")(slopes, proj, proj, proj)


def _attn_bwd(proj, dm, datt, att, lse, slopes, others):
    s = proj.shape[0]
    units = _attn_units(s)

    def body(sl_ref, q_ref, k_ref, v_ref, do_ref, att_ref, lse_ref, o3, o4, o5, o6, out_ref, dq_scr, dk_scr, dv_scr, dl_scr):
        slope = sl_ref[pl.program_id(0)]
        delta = jnp.sum(do_ref[...] * att_ref[...].astype(F32), axis=-1, keepdims=True)
        dl_scr[...] = jnp.broadcast_to(delta, (s, HEAD))
        dq_scr[...] = jnp.zeros_like(dq_scr)
        dk_scr[...] = jnp.zeros_like(dk_scr)
        dv_scr[...] = jnp.zeros_like(dv_scr)
        for first in range(0, len(units), ATTN_BATCH):
            scored = [_attn_scores(q_ref, k_ref, slope, d, r, q0, k0, nk) for _, d, r, q0, k0, nk in units[first:first + ATTN_BATCH]]
            dobs = [do_ref[sco[4], :].astype(BF16) for sco in scored]
            dps = [lax.dot_general(dob, v_ref[sco[5], :].astype(BF16), NT, preferred_element_type=F32) for dob, sco in zip(dobs, scored)]
            ps = [jnp.where(sco[1], jnp.exp(sco[0] - lse_ref[sco[4], :][:, 0:1]), 0.0) for sco in scored]
            dss = [(p * (dp - dl_scr[sco[4], :][:, 0:1]) * (HEAD ** -0.5)).astype(BF16) for p, dp, sco in zip(ps, dps, scored)]
            dqs = [lax.dot_general(ds, sco[3], NN, preferred_element_type=F32) for ds, sco in zip(dss, scored)]
            dks = [lax.dot_general(ds, sco[2], TN, preferred_element_type=F32) for ds, sco in zip(dss, scored)]
            dvs = [lax.dot_general(p.astype(BF16), dob, TN, preferred_element_type=F32) for p, dob in zip(ps, dobs)]
            for sco, dq, dk, dv in zip(scored, dqs, dks, dvs):
                dq_scr[sco[4], :] += dq
                dk_scr[sco[5], :] += dk
                dv_scr[sco[5], :] += dv
        for j, scr in enumerate((dq_scr, dk_scr, dv_scr)):
            out_ref[j] = scr[...].astype(BF16)
        for j, other in enumerate((o3, o4, o5, o6)):
            out_ref[3 + j] = other[...]

    def seg(i):
        return pl.BlockSpec((s, HEAD), lambda h: (0, i * (dm // HEAD) + h))

    col = pl.BlockSpec((s, HEAD), lambda h: (0, h))
    return pl.pallas_call(
        body, out_shape=jax.ShapeDtypeStruct((7, s, dm), BF16), grid=(dm // HEAD,),
        in_specs=[pl.BlockSpec(memory_space=pltpu.SMEM), seg(0), seg(1), seg(2), col, col, col, col, col, col, col],
        out_specs=pl.BlockSpec((7, s, HEAD), lambda h: (0, 0, h)), scratch_shapes=[pltpu.VMEM((s, HEAD), F32)] * 4,
        compiler_params=_params(("parallel",)), name="attn_bwd")(slopes, proj, proj, proj, datt, att, lse, *others)


VEC_CB, VEC_BA, VEC_BX, VEC_LAM = 0, 1, 2, 3
SEG_Q, SEG_K, SEG_V, SEG_X, SEG_GATE, SEG_GA, SEG_GL = range(7)


def _softplus(z):
    return jnp.maximum(z, 0.0) + jnp.log1p(jnp.exp(-jnp.abs(z)))


def _gate_math(xc, wa_ref, wx_ref, vec):
    xcb = xc.astype(BF16)
    nh = xc.shape[1] // HEAD
    pre_a = jnp.concatenate([jnp.dot(xcb[:, h * HEAD:(h + 1) * HEAD], wa_ref[h], preferred_element_type=F32) for h in range(nh)], axis=1)
    pre_x = jnp.concatenate([jnp.dot(xcb[:, h * HEAD:(h + 1) * HEAD], wx_ref[h], preferred_element_type=F32) for h in range(nh)], axis=1)
    ra = _sigmoid(pre_a + vec[VEC_BA:VEC_BA + 1])
    ig = _sigmoid(pre_x + vec[VEC_BX:VEC_BX + 1])
    sp = _softplus(-vec[VEC_LAM:VEC_LAM + 1])
    log_a = -LRU_C * ra * sp
    a = jnp.exp(log_a)
    z = 2.0 * log_a
    one_minus_a2 = jnp.where(z > -0.01, -z * (1.0 + z * (0.5 + z * (1.0 / 6.0))), 1.0 - jnp.exp(z))
    mult = jnp.sqrt(one_minus_a2)
    return dict(xcb=xcb, ra=ra, ig=ig, sp=sp, a=a, mult=mult)


def _conv_pad_prev(pad_ref, cur, halo, first):
    pad_ref[0:SUBLANES, :] = jnp.where(first, 0.0, halo)
    pad_ref[SUBLANES:SUBLANES + cur.shape[0], :] = cur


def _shift_rows(x, s, fill, up=False):
    rid = lax.broadcasted_iota(jnp.int32, x.shape, 0)
    if up:
        return jnp.where(rid < SUBLANES - s, pltpu.roll(x, SUBLANES - s, axis=0), fill)
    return jnp.where(rid >= s, pltpu.roll(x, s, axis=0), fill)


def _lru_fwd(proj, d, cw8, vec8, wa, wx):
    s = proj.shape[0]
    hb = ROWS // SUBLANES

    def body(x_ref, halo_ref, g_ref, cw_ref, vec_ref, wa_ref, wx_ref, xc_ref, hp_ref, h2_ref, y_ref, pad, a_scr, u_scr, carry):
        @pl.when(pl.program_id(0) == 0)
        def _():
            carry[...] = jnp.zeros_like(carry)

        _conv_pad_prev(pad, x_ref[...], halo_ref[...], pl.program_id(0) == 0)
        vec = vec_ref[...]
        xc = vec[VEC_CB:VEC_CB + 1]
        for k in range(CONV_TAPS):
            xc = xc + cw_ref[k:k + 1, :] * pad[pl.ds(SUBLANES - (CONV_TAPS - 1) + k, ROWS), :]
        gm = _gate_math(xc, wa_ref, wx_ref, vec)
        xc_ref[...] = xc
        a_scr[...] = gm["a"]
        u_scr[...] = gm["mult"] * (gm["ig"] * xc)

        def group(gi, before):
            rows = pl.ds(pl.multiple_of(gi * SUBLANES, SUBLANES), SUBLANES)
            ca, cb = a_scr[rows, :], u_scr[rows, :]
            for sh in (1, 2, 4):
                cb = ca * _shift_rows(cb, sh, 0.0) + cb
                ca = ca * _shift_rows(ca, sh, 1.0)
            h = cb + ca * before
            hp_ref[rows, :] = jnp.where(lax.broadcasted_iota(jnp.int32, h.shape, 0) == 0, before, pltpu.roll(h, 1, axis=0))
            h2_ref[rows, :] = h.astype(BF16)
            y_ref[rows, :] = (h * _gelu(g_ref[rows, :])).astype(BF16)
            return jnp.broadcast_to(h[SUBLANES - 1:SUBLANES, :], h.shape)

        carry[...] = lax.fori_loop(0, ROWS // SUBLANES, group, carry[...])

    wspec = pl.BlockSpec(wa.shape, lambda i: (0, 0, 0))
    return pl.pallas_call(
        body, out_shape=[jax.ShapeDtypeStruct((s, d), F32)] * 2 + [jax.ShapeDtypeStruct((s, d), BF16)] * 2, grid=(s // ROWS,),
        in_specs=[pl.BlockSpec((ROWS, d), lambda i: (i, SEG_X)),
                  pl.BlockSpec((SUBLANES, d), lambda i: (jnp.maximum(i * hb - 1, 0), SEG_X)),
                  pl.BlockSpec((ROWS, d), lambda i: (i, SEG_GATE)),
                  _vec_spec(d, SUBLANES), _vec_spec(d, SUBLANES), wspec, wspec],
        out_specs=[_row_spec(d)] * 4,
        scratch_shapes=[pltpu.VMEM((ROWS + SUBLANES, d), F32), pltpu.VMEM((ROWS, d), F32), pltpu.VMEM((ROWS, d), F32), pltpu.VMEM((SUBLANES, d), F32)],
        compiler_params=_params(("arbitrary",)), name="lru_fwd")(proj, proj, proj, cw8, vec8, wa, wx)


def _lru_bwd(hp, dh, xc, proj, cw8, wa, wx, vec8):
    s, d = xc.shape
    nh = d // HEAD
    nb = s // ROWS
    hb = ROWS // SUBLANES

    def body(hp_ref, dh_ref, xc_ref, x_ref, xprev_ref, cw_ref, wa_ref, wx_ref, vec_ref, dx_ref, dwa_ref, dwx_ref, dvec_ref, dcw_ref,
             a_scr, g_scr, da_scr, carry, padd, padx, dnext):
        step = pl.program_id(0)

        @pl.when(step == 0)
        def _():
            carry[...] = jnp.zeros_like(carry)
            dnext[...] = jnp.zeros_like(dnext)
            dwa_ref[...] = jnp.zeros_like(dwa_ref)
            dwx_ref[...] = jnp.zeros_like(dwx_ref)
            dvec_ref[...] = jnp.zeros_like(dvec_ref)
            dcw_ref[...] = jnp.zeros_like(dcw_ref)

        xc_v, vec = xc_ref[...], vec_ref[...]
        gm = _gate_math(xc_v, wa_ref, wx_ref, vec)
        ra, ig, sp, a, mult = gm["ra"], gm["ig"], gm["sp"], gm["a"], gm["mult"]
        a_scr[...] = a

        def group(j, after):
            rows = pl.ds(pl.multiple_of((ROWS // SUBLANES - 1 - j) * SUBLANES, SUBLANES), SUBLANES)
            ca, dhv = a_scr[rows, :], dh_ref[rows, :]
            cb = ca * dhv
            for sh in (1, 2, 4):
                cb = ca * _shift_rows(cb, sh, 0.0, up=True) + cb
                ca = ca * _shift_rows(ca, sh, 1.0, up=True)
            c = cb + ca * after
            last = lax.broadcasted_iota(jnp.int32, c.shape, 0) == SUBLANES - 1
            g = dhv + jnp.where(last, after, pltpu.roll(c, SUBLANES - 1, axis=0))
            g_scr[rows, :] = g
            da_scr[rows, :] = g * hp_ref[rows, :]
            return jnp.broadcast_to(c[0:1, :], c.shape)

        carry[...] = lax.fori_loop(0, ROWS // SUBLANES, group, carry[...])
        du, da = g_scr[...], da_scr[...]
        dmult = du * ig * xc_v
        dlog_a = da * a - dmult * (a * a) / mult
        dpre_a = dlog_a * (-LRU_C * sp) * ra * (1.0 - ra)
        dpre_x = du * mult * xc_v * ig * (1.0 - ig)
        dlam = jnp.sum(dlog_a * (-LRU_C * ra), axis=0, keepdims=True) * (-_sigmoid(-vec[VEC_LAM:VEC_LAM + 1]))
        dvec_ref[VEC_BA:VEC_BA + 1, :] += jnp.sum(dpre_a, axis=0, keepdims=True)
        dvec_ref[VEC_BX:VEC_BX + 1, :] += jnp.sum(dpre_x, axis=0, keepdims=True)
        dvec_ref[VEC_LAM:VEC_LAM + 1, :] += dlam
        dab, dxb, xcb = dpre_a.astype(BF16), dpre_x.astype(BF16), gm["xcb"]
        back = []
        for h in range(nh):
            cols = slice(h * HEAD, (h + 1) * HEAD)
            dwa_ref[h] += lax.dot_general(xcb[:, cols], dab[:, cols], TN, preferred_element_type=F32)
            dwx_ref[h] += lax.dot_general(xcb[:, cols], dxb[:, cols], TN, preferred_element_type=F32)
            back.append(lax.dot_general(dab[:, cols], wa_ref[h], NT, preferred_element_type=F32)
                        + lax.dot_general(dxb[:, cols], wx_ref[h], NT, preferred_element_type=F32))
        dc = du * mult * ig + jnp.concatenate(back, axis=1)
        padd[0:ROWS, :] = dc
        padd[ROWS:ROWS + SUBLANES, :] = dnext[...]
        dnext[...] = dc[0:SUBLANES, :]
        _conv_pad_prev(padx, x_ref[...], xprev_ref[...], step == nb - 1)
        dx = jnp.zeros_like(dc)
        for k in range(CONV_TAPS):
            dx = dx + cw_ref[k:k + 1, :] * padd[pl.ds(CONV_TAPS - 1 - k, ROWS), :]
            dcw_ref[k:k + 1, :] += jnp.sum(dc * padx[pl.ds(SUBLANES - (CONV_TAPS - 1) + k, ROWS), :], axis=0, keepdims=True)
        dcw_ref[CONV_TAPS:CONV_TAPS + 1, :] += jnp.sum(dc, axis=0, keepdims=True)
        dx_ref[...] = dx.astype(BF16)

    rows_rev = pl.BlockSpec((ROWS, d), lambda i: (nb - 1 - i, 0))
    wspec = pl.BlockSpec(wa.shape, lambda i: (0, 0, 0))
    return pl.pallas_call(
        body, out_shape=[jax.ShapeDtypeStruct((s, d), BF16), jax.ShapeDtypeStruct(wa.shape, F32), jax.ShapeDtypeStruct(wa.shape, F32),
                         jax.ShapeDtypeStruct((SUBLANES, d), F32), jax.ShapeDtypeStruct((SUBLANES, d), F32)],
        grid=(nb,),
        in_specs=[rows_rev, rows_rev, rows_rev, pl.BlockSpec((ROWS, d), lambda i: (nb - 1 - i, SEG_X)),
                  pl.BlockSpec((SUBLANES, d), lambda i: (jnp.maximum((nb - 1 - i) * hb - 1, 0), SEG_X)),
                  _vec_spec(d, SUBLANES), wspec, wspec, _vec_spec(d, SUBLANES)],
        out_specs=[rows_rev, wspec, wspec, _vec_spec(d, SUBLANES), _vec_spec(d, SUBLANES)],
        scratch_shapes=[pltpu.VMEM((ROWS, d), F32)] * 3 + [pltpu.VMEM((SUBLANES, d), F32)]
        + [pltpu.VMEM((ROWS + SUBLANES, d), F32)] * 2 + [pltpu.VMEM((SUBLANES, d), F32)],
        compiler_params=_params(("arbitrary",)), name="lru_bwd")(hp, dh, xc, proj, proj, cw8, wa, wx, vec8)


def _coords():
    return lax.axis_index("x"), lax.axis_index("y"), lax.axis_index("c")


def _other_chips(x, y):
    return [(1 - x, y), (x, 1 - y), (1 - x, 1 - y)]


def _slab(ref, kind, shard_shape, idx, half=None):
    r, c = shard_shape
    r0, nr = (0, r) if half is None else (half * (r // 2), r // 2)
    if kind == "col":
        return ref.at[pl.ds(r0, nr), pl.ds(pl.multiple_of(idx * c, LANES), c)]
    if kind == "row":
        return ref.at[pl.ds(pl.multiple_of(idx * r, SUBLANES) + r0, nr), :]
    return ref.at[idx, pl.ds(r0, nr), :]


def _full_shape(shard_shape, kind):
    r, c = shard_shape
    return {"col": (r, c * N_DEV), "row": (r * N_DEV, c), "slot": (N_DEV, r, c)}[kind]


def _handshake(peers):
    barrier = pltpu.get_barrier_semaphore()
    for peer in peers:
        pl.semaphore_signal(barrier, inc=1, device_id=peer, device_id_type=MESH)
    pl.semaphore_wait(barrier, len(peers))


def _launch(name, body, out_shape, operands, sems, sequencer_id):
    if sequencer_id is None:
        return pl.pallas_call(body, out_shape=out_shape, in_specs=[HBM] * len(operands), out_specs=[HBM] * len(out_shape),
                              scratch_shapes=sems, name=name)(*operands)
    return pl.kernel(body, out_type=out_shape, mesh=plsc.ScalarSubcoreMesh(axis_name="seq", num_cores=1), name=name,
                     scratch_types=sems, compiler_params=pltpu.CompilerParams(collective_id=sequencer_id))(*operands)


AG_COPIES = 10


def _all_gather(name, shards, kinds, sequencer_id=None):
    n = len(shards)
    shapes = [s.shape for s in shards]

    def body(*refs):
        ins, outs = refs[:n], refs[n:2 * n]
        send_sems, recv_sems, local_sems = refs[2 * n:]
        x, y, c = _coords()
        me, sib, xn, yn, dg = (x, y, c), (x, y, 1 - c), (1 - x, y, c), (x, 1 - y, c), (1 - x, 1 - y, c)
        if sequencer_id is not None:
            _handshake([sib, xn, yn])

        def part(i, dev, half=None):
            return _slab(outs[i], kinds[i], shapes[i], 4 * dev[0] + 2 * dev[1] + dev[2], half)

        def copy(i, k, block, half, to, own=False):
            r = shapes[i][0]
            src = part(i, block, half) if not own else (ins[i] if half is None else ins[i].at[pl.ds(half * (r // 2), r // 2), :])
            return pltpu.make_async_remote_copy(
                src_ref=src, dst_ref=part(i, block, half), send_sem=send_sems.at[AG_COPIES * i + k],
                recv_sem=recv_sems.at[AG_COPIES * i + k], device_id=to, device_id_type=MESH)

        def other_core(dev):
            return (dev[0], dev[1], 1 - c)

        started = []

        def start(cp):
            cp.start()
            started.append(cp)

        for i in range(n):
            start(copy(i, 1, me, 0, xn, own=True))
            start(copy(i, 4, me, 1, yn, own=True))
            start(copy(i, 2, me, 1, xn, own=True))
            start(copy(i, 3, me, 0, yn, own=True))
            start(copy(i, 0, me, None, sib, own=True))
        mine = [pltpu.make_async_copy(ins[i], part(i, me), local_sems.at[i]) for i in range(n)]
        for cp in mine:
            cp.start()
        for i in range(n):
            copy(i, 1, xn, 0, me).wait_recv()
            start(copy(i, 5, xn, 0, yn))
            copy(i, 4, yn, 1, me).wait_recv()
            start(copy(i, 6, yn, 1, xn))
        for i in range(n):
            copy(i, 2, xn, 1, me).wait_recv()
            start(copy(i, 7, xn, None, sib))
            copy(i, 3, yn, 0, me).wait_recv()
            start(copy(i, 8, yn, None, sib))
        for i in range(n):
            copy(i, 5, dg, 0, me).wait_recv()
            copy(i, 6, dg, 1, me).wait_recv()
            start(copy(i, 9, dg, None, sib))
        for i in range(n):
            copy(i, 0, sib, None, me).wait_recv()
            for k, dev in ((7, xn), (8, yn), (9, dg)):
                copy(i, k, other_core(dev), None, me).wait_recv()
        for cp in started:
            cp.wait_send()
        for cp in mine:
            cp.wait()

    out_shape = [jax.ShapeDtypeStruct(_full_shape(s.shape, k), s.dtype) for s, k in zip(shards, kinds)]
    sems = [pltpu.SemaphoreType.DMA((AG_COPIES * n,)), pltpu.SemaphoreType.DMA((AG_COPIES * n,)), pltpu.SemaphoreType.DMA((n,))]
    return _launch(name, body, out_shape, shards, sems, sequencer_id)


def _sibling_copies(kinds, shard_shapes):
    def make(ins, outs, send_sems, recv_sems):
        x, y, c = _coords()
        return [pltpu.make_async_remote_copy(
            src_ref=_slab(ins[i], kinds[i], shard_shapes[i], 2 * q + (1 - c)), dst_ref=outs[i].at[q],
            send_sem=send_sems.at[N_CHIP * i + q], recv_sem=recv_sems.at[N_CHIP * i + q],
            device_id=(x, y, 1 - c), device_id_type=MESH) for i in range(len(ins)) for q in range(N_CHIP)]
    return make


def _sibling_side(partials, kinds, shard_shapes):
    return (partials, [jax.ShapeDtypeStruct((N_CHIP, *s), BF16) for s in shard_shapes], N_CHIP * len(partials),
            _sibling_copies(kinds, shard_shapes))


def _exchange_chips(name, chip_sums, sequencer_id=None):
    n = len(chip_sums)

    def body(*refs):
        ins, outs = refs[:n], refs[n:2 * n]
        send_sems, recv_sems = refs[2 * n:]
        x, y, c = _coords()
        if sequencer_id is not None:
            _handshake([(cx, cy, c) for cx, cy in _other_chips(x, y)])
        cps = []
        for i in range(n):
            for k, (cx, cy) in enumerate(_other_chips(x, y)):
                cps.append(pltpu.make_async_remote_copy(
                    src_ref=ins[i].at[2 * cx + cy], dst_ref=outs[i].at[k], send_sem=send_sems.at[3 * i + k],
                    recv_sem=recv_sems.at[3 * i + k], device_id=(cx, cy, c), device_id_type=MESH))
        for cp in cps:
            cp.start()
        for cp in cps:
            cp.wait()

    return _launch(name, body, [jax.ShapeDtypeStruct((3, *t.shape[1:]), BF16) for t in chip_sums], chip_sums,
                   [pltpu.SemaphoreType.DMA((3 * n,)), pltpu.SemaphoreType.DMA((3 * n,))], sequencer_id)


def _all_peers(x, y, c):
    return [(x ^ (k >> 2), y ^ ((k >> 1) & 1), c ^ (k & 1)) for k in range(1, N_DEV)]


def _small_scatter(name, packed, sequencer_id):
    rows = packed.shape[0] // N_DEV

    def body(p_ref, rb_ref, send_sems, recv_sems, local_sem):
        x, y, c = _coords()
        me = 4 * x + 2 * y + c
        peers = _all_peers(x, y, c)
        _handshake(peers)

        def piece(idx):
            return p_ref.at[pl.ds(pl.multiple_of(idx * rows, SUBLANES), rows), :]

        cps = [pltpu.make_async_remote_copy(src_ref=piece(me ^ k), dst_ref=rb_ref.at[k], send_sem=send_sems.at[k], recv_sem=recv_sems.at[k],
                                            device_id=peers[k - 1], device_id_type=MESH) for k in range(1, N_DEV)]
        for cp in cps:
            cp.start()
        mine = pltpu.make_async_copy(piece(me), rb_ref.at[0], local_sem)
        mine.start()
        for cp in cps:
            cp.wait()
        mine.wait()

    return _launch(name, body, [jax.ShapeDtypeStruct((N_DEV, rows, LANES), F32)], [packed],
                   [pltpu.SemaphoreType.DMA((N_DEV,)), pltpu.SemaphoreType.DMA((N_DEV,)), pltpu.SemaphoreType.DMA], sequencer_id)[0]


def _small_sum(name, pieces):
    def body(p_ref, o_ref):
        acc = p_ref[0]
        for k in range(1, N_DEV):
            acc = acc + p_ref[k]
        o_ref[...] = acc

    vm = pl.BlockSpec(memory_space=pltpu.VMEM)
    return pl.pallas_call(body, out_shape=jax.ShapeDtypeStruct(pieces.shape[1:], F32), in_specs=[vm], out_specs=vm, name=name)(pieces)


def _small_gather(name, tot, sequencer_id):
    rows = tot.shape[0]

    def body(t_ref, out_ref, send_sems, recv_sems, local_sem):
        x, y, c = _coords()
        me = 4 * x + 2 * y + c
        peers = _all_peers(x, y, c)
        _handshake(peers)

        def piece(idx):
            return out_ref.at[pl.ds(pl.multiple_of(idx * rows, SUBLANES), rows), :]

        cps = [pltpu.make_async_remote_copy(src_ref=t_ref, dst_ref=piece(me), send_sem=send_sems.at[k], recv_sem=recv_sems.at[k],
                                            device_id=peers[k - 1], device_id_type=MESH) for k in range(1, N_DEV)]
        for cp in cps:
            cp.start()
        mine = pltpu.make_async_copy(t_ref, piece(me), local_sem)
        mine.start()
        for cp in cps:
            cp.wait()
        mine.wait()

    return _launch(name, body, [jax.ShapeDtypeStruct((N_DEV * rows, LANES), F32)], [tot],
                   [pltpu.SemaphoreType.DMA((N_DEV,)), pltpu.SemaphoreType.DMA((N_DEV,)), pltpu.SemaphoreType.DMA], sequencer_id)[0]


def _all_reduce_small(name, packed):
    rows = packed.shape[0] // N_DEV

    def body(p_ref, out_ref, rb, tot, send_sems, recv_sems):
        x, y, c = _coords()
        me = 4 * x + 2 * y + c

        def peer(k):
            return (x ^ (k >> 2), y ^ ((k >> 1) & 1), c ^ (k & 1))

        def rows_of(idx):
            return pl.ds(pl.multiple_of(idx * rows, SUBLANES), rows)

        def piece(ref, idx):
            return ref.at[rows_of(idx), :]

        scatter = [pltpu.make_async_remote_copy(src_ref=piece(p_ref, me ^ k), dst_ref=rb.at[k], send_sem=send_sems.at[k],
                                                recv_sem=recv_sems.at[k], device_id=peer(k), device_id_type=MESH) for k in range(1, N_DEV)]
        for cp in scatter:
            cp.start()
        acc = p_ref[rows_of(me), :]
        for cp in scatter:
            cp.wait_recv()
        for k in range(1, N_DEV):
            acc = acc + rb[k]
        tot[...] = acc
        out_ref[rows_of(me), :] = acc
        gather = [pltpu.make_async_remote_copy(src_ref=tot, dst_ref=piece(out_ref, me), send_sem=send_sems.at[N_DEV + k],
                                               recv_sem=recv_sems.at[N_DEV + k], device_id=peer(k), device_id_type=MESH)
                  for k in range(1, N_DEV)]
        for cp in gather:
            cp.start()
        for k in range(1, N_DEV):
            pltpu.make_async_remote_copy(src_ref=tot, dst_ref=piece(out_ref, me ^ k), send_sem=send_sems.at[N_DEV + k],
                                         recv_sem=recv_sems.at[N_DEV + k], device_id=peer(k), device_id_type=MESH).wait_recv()
        for cp in scatter + gather:
            cp.wait_send()

    vm = pl.BlockSpec(memory_space=pltpu.VMEM)
    return pl.pallas_call(
        body, out_shape=jax.ShapeDtypeStruct(packed.shape, F32), in_specs=[vm], out_specs=vm,
        scratch_shapes=[pltpu.VMEM((N_DEV, rows, LANES), F32), pltpu.VMEM((rows, LANES), F32),
                        pltpu.SemaphoreType.DMA((2 * N_DEV,)), pltpu.SemaphoreType.DMA((2 * N_DEV,))],
        compiler_params=pltpu.CompilerParams(vmem_limit_bytes=VMEM_LIMIT), name=name)(packed)


def _adamw_math(g, w, m, v):
    m = ADAM_B1 * m + (1.0 - ADAM_B1) * g
    v = ADAM_B2 * v + (1.0 - ADAM_B2) * (g * g)
    delta = -ADAM_LR * ((m / ADAM_C1) / (jnp.sqrt(v / ADAM_C2) + ADAM_EPS) + ADAM_WD * w)
    return delta, m, v


def _slab_spec(kind, shard_shape, tr, slab_of):
    r, c = shard_shape
    if kind == "col":
        return pl.BlockSpec((tr, c), lambda q, i, sc: (i, slab_of(q, sc)))
    return pl.BlockSpec((tr, c), lambda q, i, sc: (slab_of(q, sc) * (r // tr) + i, 0))


def _chip_sum(name, partial, recv, kind, shard_shape, core):
    r, c = shard_shape
    tr = _blk(r, 1024)

    def body(core_ref, p_ref, r_ref, o_ref):
        o_ref[...] = (p_ref[...].astype(F32) + r_ref[...].astype(F32)).astype(BF16)

    spec4 = pl.BlockSpec((None, tr, c), lambda q, i, sc: (q, i, 0))
    grid_spec = pltpu.PrefetchScalarGridSpec(
        num_scalar_prefetch=1, grid=(N_CHIP, r // tr),
        in_specs=[_slab_spec(kind, shard_shape, tr, lambda q, sc: 2 * q + sc[0]), spec4], out_specs=spec4)
    return pl.pallas_call(body, out_shape=jax.ShapeDtypeStruct((N_CHIP, r, c), BF16), grid_spec=grid_spec,
                          compiler_params=_params(("parallel", "parallel")), name=name)(core, partial, recv)


def _adamw_shard(name, parts, w, m, v, chip):
    r, c = w.shape
    n_parts = len(parts)
    tr = _blk(r // n_parts, 256)
    per = r // n_parts // tr

    def body(chip_ref, *refs):
        src, (w_ref, m_ref, v_ref), (g_out, d_out, m_out, v_out) = refs[:2 * n_parts], refs[2 * n_parts:2 * n_parts + 3], refs[2 * n_parts + 3:]
        for p in range(n_parts):
            @pl.when(pl.program_id(0) // per == p)
            def _():
                g = src[2 * p][...].astype(F32)
                for k in range(3):
                    g = g + src[2 * p + 1][k].astype(F32)
                g_out[...] = g
                d_out[...], m_out[...], v_out[...] = _adamw_math(g, w_ref[...], m_ref[...], v_ref[...])

    def part_specs(p):
        at = lambda i: jnp.clip(i - p * per, 0, per - 1)
        return [pl.BlockSpec((None, tr, c), lambda i, sc: (sc[0], at(i), 0)), pl.BlockSpec((3, tr, c), lambda i, sc: (0, at(i), 0))]

    blk = pl.BlockSpec((tr, c), lambda i, sc: (i, 0))
    grid_spec = pltpu.PrefetchScalarGridSpec(
        num_scalar_prefetch=1, grid=(r // tr,), in_specs=[s for p in range(n_parts) for s in part_specs(p)] + [blk, blk, blk], out_specs=[blk] * 4)
    return pl.pallas_call(body, out_shape=[jax.ShapeDtypeStruct((r, c), F32)] * 4, grid_spec=grid_spec,
                          compiler_params=_params(("parallel",)), name=name)(chip, *[a for p in parts for a in p], w, m, v)


def _adamw_small(name, g, w, m, v):
    def body(g_ref, w_ref, m_ref, v_ref, d_out, m_out, v_out):
        d_out[...], m_out[...], v_out[...] = _adamw_math(g_ref[...], w_ref[...], m_ref[...], v_ref[...])

    vm = pl.BlockSpec(memory_space=pltpu.VMEM)
    return pl.pallas_call(body, out_shape=[jax.ShapeDtypeStruct(g.shape, F32)] * 3, in_specs=[vm] * 4, out_specs=[vm] * 3,
                          compiler_params=pltpu.CompilerParams(vmem_limit_bytes=VMEM_LIMIT), name=name)(g, w, m, v)


def _pack_rows(arrays, total_rows):
    flat = [a.reshape(-1, LANES) for a in arrays]
    used = sum(f.shape[0] for f in flat)
    return jnp.concatenate(flat + [jnp.zeros((total_rows - used, LANES), F32)], axis=0)


def _unpack_rows(packed, like):
    out, at = [], 0
    for a in like:
        n = a.size // LANES
        out.append(packed[at:at + n].reshape(a.shape))
        at += n
    return out


def kernel(x, norm_mix_g, w_in, conv_w, conv_b, lru_wa, lru_ba, lru_wx, lru_bx, lru_lambda, w_proj_attn, w_proj_lru, w_out, norm_mlp_g, w_up, w_down, norm_final_g, loss_target, m_norm_mix_g, m_w_in, m_conv_w, m_conv_b, m_lru_wa, m_lru_ba, m_lru_wx, m_lru_bx, m_lru_lambda, m_w_proj_attn, m_w_proj_lru, m_w_out, m_norm_mlp_g, m_w_up, m_w_down, m_norm_final_g, v_norm_mix_g, v_w_in, v_conv_w, v_conv_b, v_lru_wa, v_lru_ba, v_lru_wx, v_lru_bx, v_lru_lambda, v_w_proj_attn, v_w_proj_lru, v_w_out, v_norm_mlp_g, v_w_up, v_w_down, v_norm_final_g):
    xs, tgt = x[0], loss_target[0]
    s, d = xs.shape
    nh = d // HEAD
    ix, iy, ic = _coords()
    core = jnp.reshape(ic, (1,)).astype(jnp.int32)
    chip = jnp.reshape(2 * ix + iy, (1,)).astype(jnp.int32)
    dev = 4 * ix + 2 * iy + ic

    big = [w_in[0], w_proj_attn[0], w_proj_lru[0], w_out[0], w_up[0], w_down[0]]
    big_m = [m_w_in[0], m_w_proj_attn[0], m_w_proj_lru[0], m_w_out[0], m_w_up[0], m_w_down[0]]
    big_v = [v_w_in[0], v_w_proj_attn[0], v_w_proj_lru[0], v_w_out[0], v_w_up[0], v_w_down[0]]
    kinds = ["col", "row", "row", "row", "col", "row"]
    pad_taps = lambda t: jnp.pad(t, ((0, SUBLANES - CONV_TAPS), (0, 0)))
    shards = [w.astype(BF16) for w in big]
    pad_taps2 = lambda t: jnp.pad(t, ((0, 2 * SUBLANES - CONV_TAPS), (0, 0)))
    win, cw_slots = _all_gather("all_gather_w_in", [shards[0], pad_taps2(conv_w[0])], ["col", "slot"], sequencer_id=7)
    later = lax.optimization_barrier((shards[1:], win))[0]
    wpa, wpl, wout = _all_gather("all_gather_mix", later[:3], kinds[1:4], sequencer_id=1)
    wup, wdown = _all_gather("all_gather_mlp", later[3:], kinds[4:], sequencer_id=5)
    cw8 = jnp.transpose(cw_slots[:, :SUBLANES], (1, 0, 2)).reshape(SUBLANES, d)
    row_id = lax.broadcasted_iota(jnp.int32, (SUBLANES, d), 0)
    vec8 = sum(jnp.where(row_id == k, t, 0.0) for k, t in ((VEC_CB, conv_b), (VEC_BA, lru_ba), (VEC_BX, lru_bx), (VEC_LAM, lru_lambda)))
    wa16, wx16 = lru_wa[0].astype(BF16), lru_wx[0].astype(BF16)
    slopes = 2.0 ** (-8.0 * jnp.arange(1, nh + 1, dtype=F32) / nh)

    def seg_specs(*segs):
        return lambda bm, bn: [pl.BlockSpec((bm, bn), (lambda i, j, kk, sg=sg: (i, sg * (d // bn) + j))) for sg in segs]

    def plain_specs(k):
        return lambda bm, bn: [pl.BlockSpec((bm, bn), lambda i, j, kk: (i, j)) for _ in range(k)]

    xn = _rms_fwd("norm_mix", xs, norm_mix_g)
    proj = _mm_fwd("proj_in", xn, win, 0, 7 * d, [F32], bm=2048)[0]
    att, lse = _attn_fwd(proj, d, slopes)
    xc, hp, h2d, ylru = _lru_fwd(proj, d, cw8, vec8, wa16, wx16)
    pa = _mm_fwd("proj_attn", att, wpa, 0, d, [BF16], bm=2048)[0]

    def merge(acc, pa_b, ga, gl):
        return acc, _sigmoid(ga) * pa_b.astype(F32) + _sigmoid(gl) * acc

    plr, merged = _mm_fwd("proj_lru_merge", ylru, wpl, 0, d, [BF16, BF16], merge, (pa, proj, proj),
                          lambda bm, bn: plain_specs(1)(bm, bn) + seg_specs(SEG_GA, SEG_GL)(bm, bn), bn=512)
    h1 = _mm_fwd("mix_out", merged, wout, 0, d, [F32], lambda acc, r: (acc + r,), (xs,), plain_specs(1))[0]
    hn = _rms_fwd("norm_mlp", h1, norm_mlp_g)

    def relu2(acc):
        return acc, jnp.square(jnp.maximum(acc, 0.0))

    up, hid = _mm_fwd("mlp_up", hn, wup, 0, wup.shape[1], [BF16, BF16], relu2, bm=2048)
    h2 = _mm_fwd("mlp_down", hid, wdown, 0, d, [F32], lambda acc, r: (acc + r,), (h1,), plain_specs(1))[0]
    dh2, dh2b, dg3, loss_lanes = _final_loss(h2, tgt, norm_final_g.reshape(1, d))
    loss = lax.psum(0.5 / d * jnp.sum(loss_lanes), ("x", "y", "c"))
    dh2b = lax.optimization_barrier((dh2b, loss))[0]

    def reduce_group(tag, kk, shp, partials, from_sibling, sequencer_id):
        sums = [_chip_sum(f"chip_sum_{tag}_{i}", p, f, k, sh, core) for i, (p, f, k, sh) in enumerate(zip(partials, from_sibling, kk, shp))]
        return list(zip(sums, _exchange_chips(f"rs_chips_{tag}", sums, sequencer_id)))

    g_wdown = _mm_tn("mlp_down_dw", hid, dh2b)
    (dup,), sib_down = _mm_nt("mlp_down_dx", dh2b, wdown, [BF16], lambda acc, u: (acc * (2.0 * jnp.maximum(u.astype(F32), 0.0)),), (up,),
                              plain_specs(1), side=_sibling_side([g_wdown], kinds[5:], [big[5].shape]))
    (red_down,) = reduce_group("mlp_down", kinds[5:], [big[5].shape], [g_wdown], sib_down, 2)
    dup = lax.optimization_barrier((dup, red_down[0]))[0]
    g_wup = _mm_tn("mlp_up_dw", hn, dup)
    (dhn,), sib_up = _mm_nt("mlp_up_dx", dup, wup, [F32], side=_sibling_side([g_wup], kinds[4:5], [big[4].shape]))
    (red_up,) = reduce_group("mlp_up", kinds[4:5], [big[4].shape], [g_wup], sib_up, 10)
    dhn = lax.optimization_barrier((dhn, red_up[0]))[0]
    dh1, dh1b, dg2 = _rms_bwd("norm_mlp_bwd", h1, norm_mlp_g, dhn, dh2)

    def merge_bwd(acc, pa_b, pl_b, ga, gl):
        sa, sl = _sigmoid(ga), _sigmoid(gl)
        return acc * sa, acc * sl, acc * pa_b.astype(F32) * sa * (1.0 - sa), acc * pl_b.astype(F32) * sl * (1.0 - sl)

    dpa, dpl, dga, dgl = _mm_nt("mix_out_dx", dh1b, wout, [BF16] * 4, merge_bwd, (pa, plr, proj, proj),
                                lambda bm, bn: plain_specs(2)(bm, bn) + seg_specs(SEG_GA, SEG_GL)(bm, bn), bn=512)
    g_wout = _mm_tn("mix_out_dw", merged, dh1b)
    datt = _mm_nt("proj_attn_dx", dpa, wpa, [F32], bm=2048)[0]
    g_wpa = _mm_tn("proj_attn_dw", att, dpa)

    def lru_out_bwd(acc, h_b, gate):
        return acc * _gelu(gate), acc * h_b.astype(F32) * _gelu_grad(gate)

    g_wpl = _mm_tn("proj_lru_dw", ylru, dpl)
    shp_mix = [w.shape for w in big[1:4]]
    (dh, dxg), sib_mix = _mm_nt("proj_lru_dx", dpl, wpl, [F32, BF16], lru_out_bwd, (h2d, proj),
                                lambda bm, bn: plain_specs(1)(bm, bn) + seg_specs(SEG_GATE)(bm, bn), bn=512,
                                side=_sibling_side([g_wpa, g_wpl, g_wout], kinds[1:4], shp_mix))
    red_pa, red_pl, red_out = reduce_group("mix", kinds[1:4], shp_mix, [g_wpa, g_wpl, g_wout], sib_mix, 3)
    dh = lax.optimization_barrier((dh, red_down[1]))[0]
    dxr, dwa, dwx, dvec, dconv = _lru_bwd(hp, dh, xc, proj, cw8, wa16, wx16, vec8)
    dproj = _attn_bwd(proj, d, datt, att, lse, slopes, (dxr, dxg, dga, dgl))

    def small_step(tag, grads, ws, ms, vs, like, after, seq=None):
        n_rows = sum(g.size for g in grads) // LANES
        per_dev = -(-n_rows // (N_DEV * SUBLANES)) * SUBLANES
        packed = lax.optimization_barrier((_pack_rows(grads, N_DEV * per_dev), after))[0]
        if seq is None:
            total = _all_reduce_small(f"all_reduce_{tag}", packed)
        else:
            pieces = lax.optimization_barrier((_small_scatter(f"scatter_{tag}", packed, seq[0]), seq[2]))[0]
            total = _small_gather(f"gather_{tag}", _small_sum(f"sum_{tag}", pieces), seq[1])
        w_rows = -(-(sum(w.size for w in ws) // LANES) // SUBLANES) * SUBLANES
        upd = _adamw_small(f"adamw_{tag}", total[:w_rows], _pack_rows(ws, w_rows), _pack_rows(ms, w_rows), _pack_rows(vs, w_rows))
        return _unpack_rows(total, like), [_unpack_rows(t, ws) for t in upd]

    early_w = [conv_b, lru_wa, lru_ba, lru_wx, lru_bx, lru_lambda, norm_mlp_g, norm_final_g]
    early_m = [m_conv_b, m_lru_wa, m_lru_ba, m_lru_wx, m_lru_bx, m_lru_lambda, m_norm_mlp_g, m_norm_final_g]
    early_v = [v_conv_b, v_lru_wa, v_lru_ba, v_lru_wx, v_lru_bx, v_lru_lambda, v_norm_mlp_g, v_norm_final_g]
    early_g = [dconv[CONV_TAPS:CONV_TAPS + 1], dwa, dvec[VEC_BA:VEC_BA + 1], dwx, dvec[VEC_BX:VEC_BX + 1],
               dvec[VEC_LAM:VEC_LAM + 1], dg2, dg3, dconv[0:CONV_TAPS]]
    dproj = lax.optimization_barrier((dproj, red_up[1]))[0]
    dproj = lax.optimization_barrier((dproj, red_pa[1], red_pl[1], red_out[1]))[0]
    early_sum, early_upd = small_step("small", early_g, early_w, early_m, early_v,
                                      early_w + [jax.ShapeDtypeStruct((1, CONV_TAPS, d), F32)], dxr, seq=(8, 9, dproj))
    g_cw_full = early_sum[-1]
    cshard = conv_w.shape[2]
    g_cw = lax.dynamic_slice(g_cw_full, (0, 0, dev * cshard), (1, CONV_TAPS, cshard))
    cw_delta, cw_m, cw_v = (t[:CONV_TAPS][None] for t in _adamw_small(
        "adamw_conv_w", pad_taps(g_cw[0]), pad_taps(conv_w[0]), pad_taps(m_conv_w[0]), pad_taps(v_conv_w[0])))
    half = (big[0].shape[0] // 2, big[0].shape[1])
    g_in0 = _mm_tn("proj_in_dw_0", xn, dproj, part=(0, 2))
    g_in1, sib_in0 = _mm_tn("proj_in_dw_1", xn, dproj, part=(1, 2), side=_sibling_side([g_in0], ["col"], [half]))
    red_in = reduce_group("in_0", ["col"], [half], [g_in0], sib_in0, 4)
    dproj = lax.optimization_barrier((dproj, red_in[0][0], early_sum))[0]
    (dxn0,), sib_in1 = _mm_nt("proj_in_dx_0", dproj, win, [F32], part=(0, 2), side=_sibling_side([g_in1], ["col"], [half]))
    red_in += reduce_group("in_1", ["col"], [half], [g_in1], sib_in1, 6)
    dproj = lax.optimization_barrier((dproj, red_in[1][0]))[0]
    dxn1 = _mm_nt("proj_in_dx_1", dproj, win, [F32], part=(1, 2))[0]
    dxn = jnp.concatenate([dxn0, dxn1], axis=0)
    dxn = lax.optimization_barrier((dxn, red_in[0][1]))[0]
    grad_x, _, dg1 = _rms_bwd("norm_mix_bwd", xs, norm_mix_g, dxn, dh1)
    red_up, red_down = lax.optimization_barrier(((red_up, red_down), dg1))[0]
    big_out = {i: _adamw_shard(f"adamw_{i}", [red], big[i], big_m[i], big_v[i], chip) for i, red in ((4, red_up), (5, red_down))}
    big_out.update({i: _adamw_shard(f"adamw_{i}", [red], big[i], big_m[i], big_v[i], chip) for i, red in ((1, red_pa), (2, red_pl), (3, red_out))})
    late_sum, late_upd = small_step("norm_mix", [dg1], [norm_mix_g], [m_norm_mix_g], [v_norm_mix_g], [norm_mix_g], (big_out[4], big_out[5]))
    big_out[0] = _adamw_shard("adamw_0", red_in, big[0], big_m[0], big_v[0], chip)
    s_grad = late_sum + early_sum[:-1]
    s_delta, s_m, s_v = (late_upd[j] + early_upd[j] for j in range(3))


    names = ["norm_mix_g", "w_in", "conv_w", "conv_b", "lru_wa", "lru_ba", "lru_wx", "lru_bx", "lru_lambda", "w_proj_attn", "w_proj_lru",
             "w_out", "norm_mlp_g", "w_up", "w_down", "norm_final_g"]
    small_names = ["norm_mix_g", "conv_b", "lru_wa", "lru_ba", "lru_wx", "lru_bx", "lru_lambda", "norm_mlp_g", "norm_final_g"]
    big_names = ["w_in", "w_proj_attn", "w_proj_lru", "w_out", "w_up", "w_down"]
    res = {"conv_w": (g_cw, cw_delta, cw_m, cw_v)}
    for i, nm in enumerate(small_names):
        res[nm] = (s_grad[i], s_delta[i], s_m[i], s_v[i])
    for i, nm in enumerate(big_names):
        res[nm] = tuple(t[None] for t in big_out[i])
    return (loss, grad_x[None], *[res[nm][0] for nm in names], *[res[nm][1] for nm in names],
            *[res[nm][2] for nm in names], *[res[nm][3] for nm in names])
```

```python
import jax
import jax.numpy as jnp
from jax import lax
from jax.experimental import pallas as pl
from jax.experimental.pallas import tpu as pltpu
from jax.experimental.pallas import tpu_sc as plsc

F32, BF16 = jnp.float32, jnp.bfloat16
MESH = pl.DeviceIdType.MESH
HBM = pl.BlockSpec(memory_space=pltpu.HBM)
N_DEV = 8
N_CHIP = 4
HEAD = 128
SPAN = 128
DILATIONS = (1, 4, 16)
CONV_TAPS = 4
LRU_C = 8.0
NORM_EPS = 1e-6
LANES = 128
SUBLANES = 8
VMEM_LIMIT = 56 * 1024 * 1024
ADAM_LR, ADAM_B1, ADAM_B2, ADAM_EPS, ADAM_WD, ADAM_STEP = 0.001, 0.9, 0.999, 1e-08, 0.01, 10
ADAM_C1 = 1.0 - ADAM_B1 ** ADAM_STEP
ADAM_C2 = 1.0 - ADAM_B2 ** ADAM_STEP
NEG = -1e30


def _params(sem=None):
    return pltpu.CompilerParams(dimension_semantics=sem, vmem_limit_bytes=VMEM_LIMIT)


def _sigmoid(v):
    return 1.0 / (1.0 + jnp.exp(-v))


def _gelu(v):
    k = 0.7978845608028654
    return 0.5 * v * (1.0 + jnp.tanh(k * (v + 0.044715 * v * v * v)))


def _gelu_grad(v):
    k = 0.7978845608028654
    t = jnp.tanh(k * (v + 0.044715 * v * v * v))
    return 0.5 * (1.0 + t) + 0.5 * v * (1.0 - t * t) * k * (1.0 + 3.0 * 0.044715 * v * v)


NN = (((1,), (0,)), ((), ()))
NT = (((1,), (1,)), ((), ()))
TN = (((0,), (0,)), ((), ()))


def _mm(name, a, a_spec, b, b_spec, dn, grid, out_shapes, out_specs, acc_block, epilogue=None, extras=(), extra_specs=(), side=None):
    nk, ne, no = grid[2], len(extras), len(out_shapes)
    side_ops, side_shapes, side_copies, make_copies = side if side is not None else ((), (), 0, None)
    ns_in, ns_out = len(side_ops), len(side_shapes)

    def body(*refs):
        a_ref, b_ref = refs[0], refs[1]
        ex, side_in = refs[2:2 + ne], refs[2 + ne:2 + ne + ns_in]
        outs = refs[2 + ne + ns_in:2 + ne + ns_in + no]
        side_out = refs[2 + ne + ns_in + no:2 + ne + ns_in + no + ns_out]
        scratch = refs[2 + ne + ns_in + no + ns_out:]
        at = [pl.program_id(ax) for ax in range(3)]
        if side is not None:
            @pl.when((at[0] == 0) & (at[1] == 0) & (at[2] == 0))
            def _():
                for cp in make_copies(side_in, side_out, scratch[-2], scratch[-1]):
                    cp.start()

        part = lax.dot_general(a_ref[...], b_ref[...], dn, preferred_element_type=F32)

        def finish(acc):
            vals = epilogue(acc, *[e[...] for e in ex]) if epilogue is not None else (acc,)
            for o, v in zip(outs, vals):
                o[...] = v.astype(o.dtype)

        if nk == 1:
            finish(part)
        else:
            acc_ref, k = scratch[0], at[2]

            @pl.when(k == 0)
            def _():
                acc_ref[...] = part

            @pl.when(k > 0)
            def _():
                acc_ref[...] += part

            @pl.when(k == nk - 1)
            def _():
                finish(acc_ref[...])

        if side is not None:
            @pl.when((at[0] == grid[0] - 1) & (at[1] == grid[1] - 1) & (at[2] == grid[2] - 1))
            def _():
                for cp in make_copies(side_in, side_out, scratch[-2], scratch[-1]):
                    cp.wait()

    scratch_shapes = [pltpu.VMEM(acc_block, F32)] if nk > 1 else []
    if side is not None:
        scratch_shapes += [pltpu.SemaphoreType.DMA((side_copies,)), pltpu.SemaphoreType.DMA((side_copies,))]
    res = pl.pallas_call(
        body, out_shape=[*out_shapes, *side_shapes], grid=grid, in_specs=[a_spec, b_spec, *extra_specs, *[HBM] * ns_in],
        out_specs=[*out_specs, *[HBM] * ns_out], scratch_shapes=scratch_shapes,
        compiler_params=_params(("arbitrary",) * 3 if side is not None else ("parallel", "parallel", "arbitrary")),
        name=name)(a, b, *extras, *side_ops)
    return res if side is None else (res[:no], res[no:])


def _blk(n, pref):
    return pref if n % pref == 0 else n


def _kblk(k):
    return k if k <= 2048 else next(b for b in (2048, 1024, 512) if k % b == 0)


def _mm_fwd(name, a, w, col0, ncols, out_dtypes, epilogue=None, extras=(), extra_specs_fn=None, bm=1024, bn=1024):
    m, k = a.shape
    bm, bn = _blk(m, bm), _blk(ncols, bn)
    bk = _kblk(k)
    nk = k // bk
    cb0 = col0 // bn
    grid = (m // bm, ncols // bn, nk)
    a_spec = pl.BlockSpec((bm, bk), lambda i, j, kk: (i, kk))
    b_spec = pl.BlockSpec((bk, bn), lambda i, j, kk: (kk, cb0 + j))
    shapes = [jax.ShapeDtypeStruct((m, ncols), dt) for dt in out_dtypes]
    specs = [pl.BlockSpec((bm, bn), lambda i, j, kk: (i, j)) for _ in out_dtypes]
    ex_specs = extra_specs_fn(bm, bn) if extra_specs_fn else ()
    return _mm(name, a, a_spec, w, b_spec, NN, grid, shapes, specs, (bm, bn), epilogue, extras, ex_specs)


def _mm_nt(name, a, w, out_dtypes, epilogue=None, extras=(), extra_specs_fn=None, part=(0, 1), side=None, bm=1024, bn=1024):
    n = w.shape[0]
    if a.ndim == 3:
        seg_cols, m, k = a.shape[2], a.shape[1], a.shape[0] * a.shape[2]
    else:
        m, k = a.shape
    m = m // part[1]
    bm, bn = _blk(m, bm), _blk(n, bn)
    bk = _kblk(k)
    grid = (m // bm, n // bn, k // bk)
    i0 = part[0] * (m // bm)
    if a.ndim == 3:
        per = seg_cols // bk
        a_spec = pl.BlockSpec((None, bm, bk), lambda i, j, kk: (kk // per, i0 + i, kk % per))
    else:
        a_spec = pl.BlockSpec((bm, bk), lambda i, j, kk: (i0 + i, kk))
    b_spec = pl.BlockSpec((bn, bk), lambda i, j, kk: (j, kk))
    shapes = [jax.ShapeDtypeStruct((m, n), dt) for dt in out_dtypes]
    specs = [pl.BlockSpec((bm, bn), lambda i, j, kk: (i, j)) for _ in out_dtypes]
    ex_specs = extra_specs_fn(bm, bn) if extra_specs_fn else ()
    return _mm(name, a, a_spec, w, b_spec, NT, grid, shapes, specs, (bm, bn), epilogue, extras, ex_specs, side)


def _mm_tn(name, a, b, part=(0, 1), side=None, bm=1024, bn=2048):
    t, m = a.shape
    n = b.shape[1] if b.ndim == 2 else b.shape[0] * b.shape[2]
    m = m // part[1]
    bm, bn = _blk(m, bm), _blk(n, bn)
    grid = (m // bm, n // bn, 1)
    i0 = part[0] * (m // bm)
    a_spec = pl.BlockSpec((t, bm), lambda i, j, kk: (0, i0 + i))
    if b.ndim == 3:
        per = b.shape[2] // bn
        b_spec = pl.BlockSpec((None, t, bn), lambda i, j, kk: (j // per, 0, j % per))
    else:
        b_spec = pl.BlockSpec((t, bn), lambda i, j, kk: (0, j))
    res = _mm(name, a, a_spec, b, b_spec, TN, grid, [jax.ShapeDtypeStruct((m, n), BF16)],
              [pl.BlockSpec((bm, bn), lambda i, j, kk: (i, j))], (bm, bn), side=side)
    return res[0] if side is None else (res[0][0], res[1])


ROWS = 256
NORM_ROWS = 512


def _row_spec(d, rows=ROWS):
    return pl.BlockSpec((rows, d), lambda i: (i, 0))


def _vec_spec(d, rows=1):
    return pl.BlockSpec((rows, d), lambda i: (0, 0))


def _rms_fwd(name, x, g):
    s, d = x.shape

    def body(x_ref, g_ref, o_ref):
        xv = x_ref[...]
        r = lax.rsqrt(jnp.mean(xv * xv, axis=-1, keepdims=True) + NORM_EPS)
        o_ref[...] = (xv * r * g_ref[...]).astype(BF16)

    return pl.pallas_call(body, out_shape=jax.ShapeDtypeStruct((s, d), BF16), grid=(s // NORM_ROWS,),
                          in_specs=[_row_spec(d, NORM_ROWS), _vec_spec(d)], out_specs=_row_spec(d, NORM_ROWS),
                          compiler_params=_params(("parallel",)), name=name)(x, g)


def _rms_bwd_math(xv, g, dy):
    r = lax.rsqrt(jnp.mean(xv * xv, axis=-1, keepdims=True) + NORM_EPS)
    n = xv * r
    z = dy * g
    dx = r * (z - n * jnp.mean(z * n, axis=-1, keepdims=True))
    return dx, jnp.sum(dy * n, axis=0, keepdims=True)


def _rms_bwd(name, x, g, dy, resid):
    s, d = x.shape

    def body(x_ref, g_ref, dy_ref, r_ref, dx_ref, dxb_ref, dg_ref):
        dx, dg = _rms_bwd_math(x_ref[...], g_ref[...], dy_ref[...])
        dx = dx + r_ref[...]
        dx_ref[...] = dx
        dxb_ref[...] = dx.astype(BF16)

        @pl.when(pl.program_id(0) == 0)
        def _():
            dg_ref[...] = jnp.zeros_like(dg_ref)

        dg_ref[...] += dg

    return pl.pallas_call(
        body, out_shape=[jax.ShapeDtypeStruct((s, d), F32), jax.ShapeDtypeStruct((s, d), BF16), jax.ShapeDtypeStruct((1, d), F32)],
        grid=(s // NORM_ROWS,), in_specs=[_row_spec(d, NORM_ROWS), _vec_spec(d), _row_spec(d, NORM_ROWS), _row_spec(d, NORM_ROWS)],
        out_specs=[_row_spec(d, NORM_ROWS), _row_spec(d, NORM_ROWS), _vec_spec(d)], compiler_params=_params(("arbitrary",)), name=name)(x, g, dy, resid)


def _final_loss(h2, tgt, g):
    s, d = h2.shape

    def body(x_ref, t_ref, g_ref, dx_ref, dxb_ref, dg_ref, ls_ref):
        xv, gv = x_ref[...], g_ref[...]
        r = lax.rsqrt(jnp.mean(xv * xv, axis=-1, keepdims=True) + NORM_EPS)
        diff = xv * r * gv - t_ref[...]
        dx, dg = _rms_bwd_math(xv, gv, diff * (1.0 / d))
        dx_ref[...] = dx
        dxb_ref[...] = dx.astype(BF16)

        @pl.when(pl.program_id(0) == 0)
        def _():
            dg_ref[...] = jnp.zeros_like(dg_ref)
            ls_ref[...] = jnp.zeros_like(ls_ref)

        dg_ref[...] += dg
        ls_ref[...] += jnp.sum(diff * diff, axis=0, keepdims=True)

    return pl.pallas_call(
        body, out_shape=[jax.ShapeDtypeStruct((s, d), F32), jax.ShapeDtypeStruct((s, d), BF16),
                         jax.ShapeDtypeStruct((1, d), F32), jax.ShapeDtypeStruct((1, d), F32)],
        grid=(s // NORM_ROWS,), in_specs=[_row_spec(d, NORM_ROWS), _row_spec(d, NORM_ROWS), _vec_spec(d)],
        out_specs=[_row_spec(d, NORM_ROWS), _row_spec(d, NORM_ROWS), _vec_spec(d), _vec_spec(d)],
        compiler_params=_params(("arbitrary",)), name="final_norm_loss")(h2, tgt, g)


ATTN_Q = 128


ATTN_BATCH = 8


def _attn_units(s):
    units = []
    for gi, d in enumerate(DILATIONS):
        for r in range(d):
            for q0 in range(0, s // d, ATTN_Q):
                k0 = max(q0 - SPAN, 0)
                units.append((gi, d, r, q0, k0, q0 + ATTN_Q - k0))
    return units


def _stream_rows(d, r, start, size):
    return pl.ds(r + start * d, size) if d == 1 else pl.ds(r + start * d, size, stride=d)


def _attn_scores(q_ref, k_ref, slope, d, r, q0, k0, nk):
    qrows, krows = _stream_rows(d, r, q0, ATTN_Q), _stream_rows(d, r, k0, nk)
    qb = q_ref[qrows, :].astype(BF16)
    kb = k_ref[krows, :].astype(BF16)
    sc = lax.dot_general(qb, kb, NT, preferred_element_type=F32) * (HEAD ** -0.5)
    qi = lax.broadcasted_iota(jnp.int32, (ATTN_Q, nk), 0)
    kj = lax.broadcasted_iota(jnp.int32, (ATTN_Q, nk), 1)
    dist = (q0 - k0) + qi - kj
    valid = (dist >= 0) & (dist <= SPAN)
    sc = sc - (slope * d) * dist.astype(F32)
    return jnp.where(valid, sc, NEG), valid, qb, kb, qrows, krows


def _attn_fwd(proj, dm, slopes):
    s = proj.shape[0]
    units = _attn_units(s)

    def body(sl_ref, q_ref, k_ref, v_ref, att_ref, lse_ref, *scr):
        o_scr, l_scr = scr[:3], scr[3:]
        slope = sl_ref[pl.program_id(0)]
        for first in range(0, len(units), ATTN_BATCH):
            batch = units[first:first + ATTN_BATCH]
            scored = [_attn_scores(q_ref, k_ref, slope, d, r, q0, k0, nk) for _, d, r, q0, k0, nk in batch]
            soft = []
            for sc, _, _, _, _, _ in scored:
                m = jnp.max(sc, axis=-1, keepdims=True)
                p = jnp.exp(sc - m)
                soft.append((m, p, jnp.sum(p, axis=-1, keepdims=True)))
            outs = [lax.dot_general(p.astype(BF16), v_ref[sco[5], :].astype(BF16), NN, preferred_element_type=F32)
                    for (m, p, l), sco in zip(soft, scored)]
            for (gi, *_), (m, p, l), sco, o in zip(batch, soft, scored, outs):
                o_scr[gi][sco[4], :] = o / l
                l_scr[gi][sco[4], :] = jnp.broadcast_to(m + jnp.log(l), (ATTN_Q, HEAD))
        l0, l1, l2 = l_scr[0][...], l_scr[1][...], l_scr[2][...]
        m = jnp.maximum(jnp.maximum(l0, l1), l2)
        w0, w1, w2 = jnp.exp(l0 - m), jnp.exp(l1 - m), jnp.exp(l2 - m)
        tot = w0 + w1 + w2
        att_ref[...] = ((w0 * o_scr[0][...] + w1 * o_scr[1][...] + w2 * o_scr[2][...]) / tot).astype(BF16)
        lse_ref[...] = m + jnp.log(tot)

    def seg(i):
        return pl.BlockSpec((s, HEAD), lambda h: (0, i * (dm // HEAD) + h))

    col = pl.BlockSpec((s, HEAD), lambda h: (0, h))
    return pl.pallas_call(
        body, out_shape=[jax.ShapeDtypeStruct((s, dm), BF16), jax.ShapeDtypeStruct((s, dm), F32)], grid=(dm // HEAD,),
        in_specs=[pl.BlockSpec(memory_space=pltpu.SMEM), seg(0), seg(1), seg(2)], out_specs=[col, col],
        scratch_shapes=[pltpu.VMEM((s, HEAD), F32)] * (2 * len(DILATIONS)),
        compiler_params=_params(("parallel",)), name="attn_fwd")(slopes, proj, proj, proj)


def _attn_bwd(proj, dm, datt, att, lse, slopes, others):
    s = proj.shape[0]
    units = _attn_units(s)

    def body(sl_ref, q_ref, k_ref, v_ref, do_ref, att_ref, lse_ref, o3, o4, o5, o6, out_ref, dq_scr, dk_scr, dv_scr, dl_scr):
        slope = sl_ref[pl.program_id(0)]
        delta = jnp.sum(do_ref[...] * att_ref[...].astype(F32), axis=-1, keepdims=True)
        dl_scr[...] = jnp.broadcast_to(delta, (s, HEAD))
        dq_scr[...] = jnp.zeros_like(dq_scr)
        dk_scr[...] = jnp.zeros_like(dk_scr)
        dv_scr[...] = jnp.zeros_like(dv_scr)
        for first in range(0, len(units), ATTN_BATCH):
            scored = [_attn_scores(q_ref, k_ref, slope, d, r, q0, k0, nk) for _, d, r, q0, k0, nk in units[first:first + ATTN_BATCH]]
            dobs = [do_ref[sco[4], :].astype(BF16) for sco in scored]
            dps = [lax.dot_general(dob, v_ref[sco[5], :].astype(BF16), NT, preferred_element_type=F32) for dob, sco in zip(dobs, scored)]
            ps = [jnp.where(sco[1], jnp.exp(sco[0] - lse_ref[sco[4], :][:, 0:1]), 0.0) for sco in scored]
            dss = [(p * (dp - dl_scr[sco[4], :][:, 0:1]) * (HEAD ** -0.5)).astype(BF16) for p, dp, sco in zip(ps, dps, scored)]
            dqs = [lax.dot_general(ds, sco[3], NN, preferred_element_type=F32) for ds, sco in zip(dss, scored)]
            dks = [lax.dot_general(ds, sco[2], TN, preferred_element_type=F32) for ds, sco in zip(dss, scored)]
            dvs = [lax.dot_general(p.astype(BF16), dob, TN, preferred_element_type=F32) for p, dob in zip(ps, dobs)]
            for sco, dq, dk, dv in zip(scored, dqs, dks, dvs):
                dq_scr[sco[4], :] += dq
                dk_scr[sco[5], :] += dk
                dv_scr[sco[5], :] += dv
        for j, scr in enumerate((dq_scr, dk_scr, dv_scr)):
            out_ref[j] = scr[...].astype(BF16)
        for j, other in enumerate((o3, o4, o5, o6)):
            out_ref[3 + j] = other[...]

    def seg(i):
        return pl.BlockSpec((s, HEAD), lambda h: (0, i * (dm // HEAD) + h))

    col = pl.BlockSpec((s, HEAD), lambda h: (0, h))
    return pl.pallas_call(
        body, out_shape=jax.ShapeDtypeStruct((7, s, dm), BF16), grid=(dm // HEAD,),
        in_specs=[pl.BlockSpec(memory_space=pltpu.SMEM), seg(0), seg(1), seg(2), col, col, col, col, col, col, col],
        out_specs=pl.BlockSpec((7, s, HEAD), lambda h: (0, 0, h)), scratch_shapes=[pltpu.VMEM((s, HEAD), F32)] * 4,
        compiler_params=_params(("parallel",)), name="attn_bwd")(slopes, proj, proj, proj, datt, att, lse, *others)


VEC_CB, VEC_BA, VEC_BX, VEC_LAM = 0, 1, 2, 3
SEG_Q, SEG_K, SEG_V, SEG_X, SEG_GATE, SEG_GA, SEG_GL = range(7)


def _softplus(z):
    return jnp.maximum(z, 0.0) + jnp.log1p(jnp.exp(-jnp.abs(z)))


def _gate_math(xc, wa_ref, wx_ref, vec):
    xcb = xc.astype(BF16)
    nh = xc.shape[1] // HEAD
    pre_a = jnp.concatenate([jnp.dot(xcb[:, h * HEAD:(h + 1) * HEAD], wa_ref[h], preferred_element_type=F32) for h in range(nh)], axis=1)
    pre_x = jnp.concatenate([jnp.dot(xcb[:, h * HEAD:(h + 1) * HEAD], wx_ref[h], preferred_element_type=F32) for h in range(nh)], axis=1)
    ra = _sigmoid(pre_a + vec[VEC_BA:VEC_BA + 1])
    ig = _sigmoid(pre_x + vec[VEC_BX:VEC_BX + 1])
    sp = _softplus(-vec[VEC_LAM:VEC_LAM + 1])
    log_a = -LRU_C * ra * sp
    a = jnp.exp(log_a)
    z = 2.0 * log_a
    one_minus_a2 = jnp.where(z > -0.01, -z * (1.0 + z * (0.5 + z * (1.0 / 6.0))), 1.0 - jnp.exp(z))
    mult = jnp.sqrt(one_minus_a2)
    return dict(xcb=xcb, ra=ra, ig=ig, sp=sp, a=a, mult=mult)


def _conv_pad_prev(pad_ref, cur, halo, first):
    pad_ref[0:SUBLANES, :] = jnp.where(first, 0.0, halo)
    pad_ref[SUBLANES:SUBLANES + cur.shape[0], :] = cur


def _shift_rows(x, s, fill, up=False):
    rid = lax.broadcasted_iota(jnp.int32, x.shape, 0)
    if up:
        return jnp.where(rid < SUBLANES - s, pltpu.roll(x, SUBLANES - s, axis=0), fill)
    return jnp.where(rid >= s, pltpu.roll(x, s, axis=0), fill)


def _lru_fwd(proj, d, cw8, vec8, wa, wx):
    s = proj.shape[0]
    hb = ROWS // SUBLANES

    def body(x_ref, halo_ref, g_ref, cw_ref, vec_ref, wa_ref, wx_ref, xc_ref, hp_ref, h2_ref, y_ref, pad, a_scr, u_scr, carry):
        @pl.when(pl.program_id(0) == 0)
        def _():
            carry[...] = jnp.zeros_like(carry)

        _conv_pad_prev(pad, x_ref[...], halo_ref[...], pl.program_id(0) == 0)
        vec = vec_ref[...]
        xc = vec[VEC_CB:VEC_CB + 1]
        for k in range(CONV_TAPS):
            xc = xc + cw_ref[k:k + 1, :] * pad[pl.ds(SUBLANES - (CONV_TAPS - 1) + k, ROWS), :]
        gm = _gate_math(xc, wa_ref, wx_ref, vec)
        xc_ref[...] = xc
        a_scr[...] = gm["a"]
        u_scr[...] = gm["mult"] * (gm["ig"] * xc)

        def group(gi, before):
            rows = pl.ds(pl.multiple_of(gi * SUBLANES, SUBLANES), SUBLANES)
            ca, cb = a_scr[rows, :], u_scr[rows, :]
            for sh in (1, 2, 4):
                cb = ca * _shift_rows(cb, sh, 0.0) + cb
                ca = ca * _shift_rows(ca, sh, 1.0)
            h = cb + ca * before
            hp_ref[rows, :] = jnp.where(lax.broadcasted_iota(jnp.int32, h.shape, 0) == 0, before, pltpu.roll(h, 1, axis=0))
            h2_ref[rows, :] = h.astype(BF16)
            y_ref[rows, :] = (h * _gelu(g_ref[rows, :])).astype(BF16)
            return jnp.broadcast_to(h[SUBLANES - 1:SUBLANES, :], h.shape)

        carry[...] = lax.fori_loop(0, ROWS // SUBLANES, group, carry[...])

    wspec = pl.BlockSpec(wa.shape, lambda i: (0, 0, 0))
    return pl.pallas_call(
        body, out_shape=[jax.ShapeDtypeStruct((s, d), F32)] * 2 + [jax.ShapeDtypeStruct((s, d), BF16)] * 2, grid=(s // ROWS,),
        in_specs=[pl.BlockSpec((ROWS, d), lambda i: (i, SEG_X)),
                  pl.BlockSpec((SUBLANES, d), lambda i: (jnp.maximum(i * hb - 1, 0), SEG_X)),
                  pl.BlockSpec((ROWS, d), lambda i: (i, SEG_GATE)),
                  _vec_spec(d, SUBLANES), _vec_spec(d, SUBLANES), wspec, wspec],
        out_specs=[_row_spec(d)] * 4,
        scratch_shapes=[pltpu.VMEM((ROWS + SUBLANES, d), F32), pltpu.VMEM((ROWS, d), F32), pltpu.VMEM((ROWS, d), F32), pltpu.VMEM((SUBLANES, d), F32)],
        compiler_params=_params(("arbitrary",)), name="lru_fwd")(proj, proj, proj, cw8, vec8, wa, wx)


def _lru_bwd(hp, dh, xc, proj, cw8, wa, wx, vec8):
    s, d = xc.shape
    nh = d // HEAD
    nb = s // ROWS
    hb = ROWS // SUBLANES

    def body(hp_ref, dh_ref, xc_ref, x_ref, xprev_ref, cw_ref, wa_ref, wx_ref, vec_ref, dx_ref, dwa_ref, dwx_ref, dvec_ref, dcw_ref,
             a_scr, g_scr, da_scr, carry, padd, padx, dnext):
        step = pl.program_id(0)

        @pl.when(step == 0)
        def _():
            carry[...] = jnp.zeros_like(carry)
            dnext[...] = jnp.zeros_like(dnext)
            dwa_ref[...] = jnp.zeros_like(dwa_ref)
            dwx_ref[...] = jnp.zeros_like(dwx_ref)
            dvec_ref[...] = jnp.zeros_like(dvec_ref)
            dcw_ref[...] = jnp.zeros_like(dcw_ref)

        xc_v, vec = xc_ref[...], vec_ref[...]
        gm = _gate_math(xc_v, wa_ref, wx_ref, vec)
        ra, ig, sp, a, mult = gm["ra"], gm["ig"], gm["sp"], gm["a"], gm["mult"]
        a_scr[...] = a

        def group(j, after):
            rows = pl.ds(pl.multiple_of((ROWS // SUBLANES - 1 - j) * SUBLANES, SUBLANES), SUBLANES)
            ca, dhv = a_scr[rows, :], dh_ref[rows, :]
            cb = ca * dhv
            for sh in (1, 2, 4):
                cb = ca * _shift_rows(cb, sh, 0.0, up=True) + cb
                ca = ca * _shift_rows(ca, sh, 1.0, up=True)
            c = cb + ca * after
            last = lax.broadcasted_iota(jnp.int32, c.shape, 0) == SUBLANES - 1
            g = dhv + jnp.where(last, after, pltpu.roll(c, SUBLANES - 1, axis=0))
            g_scr[rows, :] = g
            da_scr[rows, :] = g * hp_ref[rows, :]
            return jnp.broadcast_to(c[0:1, :], c.shape)

        carry[...] = lax.fori_loop(0, ROWS // SUBLANES, group, carry[...])
        du, da = g_scr[...], da_scr[...]
        dmult = du * ig * xc_v
        dlog_a = da * a - dmult * (a * a) / mult
        dpre_a = dlog_a * (-LRU_C * sp) * ra * (1.0 - ra)
        dpre_x = du * mult * xc_v * ig * (1.0 - ig)
        dlam = jnp.sum(dlog_a * (-LRU_C * ra), axis=0, keepdims=True) * (-_sigmoid(-vec[VEC_LAM:VEC_LAM + 1]))
        dvec_ref[VEC_BA:VEC_BA + 1, :] += jnp.sum(dpre_a, axis=0, keepdims=True)
        dvec_ref[VEC_BX:VEC_BX + 1, :] += jnp.sum(dpre_x, axis=0, keepdims=True)
        dvec_ref[VEC_LAM:VEC_LAM + 1, :] += dlam
        dab, dxb, xcb = dpre_a.astype(BF16), dpre_x.astype(BF16), gm["xcb"]
        back = []
        for h in range(nh):
            cols = slice(h * HEAD, (h + 1) * HEAD)
            dwa_ref[h] += lax.dot_general(xcb[:, cols], dab[:, cols], TN, preferred_element_type=F32)
            dwx_ref[h] += lax.dot_general(xcb[:, cols], dxb[:, cols], TN, preferred_element_type=F32)
            back.append(lax.dot_general(dab[:, cols], wa_ref[h], NT, preferred_element_type=F32)
                        + lax.dot_general(dxb[:, cols], wx_ref[h], NT, preferred_element_type=F32))
        dc = du * mult * ig + jnp.concatenate(back, axis=1)
        padd[0:ROWS, :] = dc
        padd[ROWS:ROWS + SUBLANES, :] = dnext[...]
        dnext[...] = dc[0:SUBLANES, :]
        _conv_pad_prev(padx, x_ref[...], xprev_ref[...], step == nb - 1)
        dx = jnp.zeros_like(dc)
        for k in range(CONV_TAPS):
            dx = dx + cw_ref[k:k + 1, :] * padd[pl.ds(CONV_TAPS - 1 - k, ROWS), :]
            dcw_ref[k:k + 1, :] += jnp.sum(dc * padx[pl.ds(SUBLANES - (CONV_TAPS - 1) + k, ROWS), :], axis=0, keepdims=True)
        dcw_ref[CONV_TAPS:CONV_TAPS + 1, :] += jnp.sum(dc, axis=0, keepdims=True)
        dx_ref[...] = dx.astype(BF16)

    rows_rev = pl.BlockSpec((ROWS, d), lambda i: (nb - 1 - i, 0))
    wspec = pl.BlockSpec(wa.shape, lambda i: (0, 0, 0))
    return pl.pallas_call(
        body, out_shape=[jax.ShapeDtypeStruct((s, d), BF16), jax.ShapeDtypeStruct(wa.shape, F32), jax.ShapeDtypeStruct(wa.shape, F32),
                         jax.ShapeDtypeStruct((SUBLANES, d), F32), jax.ShapeDtypeStruct((SUBLANES, d), F32)],
        grid=(nb,),
        in_specs=[rows_rev, rows_rev, rows_rev, pl.BlockSpec((ROWS, d), lambda i: (nb - 1 - i, SEG_X)),
                  pl.BlockSpec((SUBLANES, d), lambda i: (jnp.maximum((nb - 1 - i) * hb - 1, 0), SEG_X)),
                  _vec_spec(d, SUBLANES), wspec, wspec, _vec_spec(d, SUBLANES)],
        out_specs=[rows_rev, wspec, wspec, _vec_spec(d, SUBLANES), _vec_spec(d, SUBLANES)],
        scratch_shapes=[pltpu.VMEM((ROWS, d), F32)] * 3 + [pltpu.VMEM((SUBLANES, d), F32)]
        + [pltpu.VMEM((ROWS + SUBLANES, d), F32)] * 2 + [pltpu.VMEM((SUBLANES, d), F32)],
        compiler_params=_params(("arbitrary",)), name="lru_bwd")(hp, dh, xc, proj, proj, cw8, wa, wx, vec8)


def _coords():
    return lax.axis_index("x"), lax.axis_index("y"), lax.axis_index("c")


def _other_chips(x, y):
    return [(1 - x, y), (x, 1 - y), (1 - x, 1 - y)]


def _slab(ref, kind, shard_shape, idx, half=None):
    r, c = shard_shape
    r0, nr = (0, r) if half is None else (half * (r // 2), r // 2)
    if kind == "col":
        return ref.at[pl.ds(r0, nr), pl.ds(pl.multiple_of(idx * c, LANES), c)]
    if kind == "row":
        return ref.at[pl.ds(pl.multiple_of(idx * r, SUBLANES) + r0, nr), :]
    return ref.at[idx, pl.ds(r0, nr), :]


def _full_shape(shard_shape, kind):
    r, c = shard_shape
    return {"col": (r, c * N_DEV), "row": (r * N_DEV, c), "slot": (N_DEV, r, c)}[kind]


def _handshake(peers):
    barrier = pltpu.get_barrier_semaphore()
    for peer in peers:
        pl.semaphore_signal(barrier, inc=1, device_id=peer, device_id_type=MESH)
    pl.semaphore_wait(barrier, len(peers))


def _launch(name, body, out_shape, operands, sems, sequencer_id):
    if sequencer_id is None:
        return pl.pallas_call(body, out_shape=out_shape, in_specs=[HBM] * len(operands), out_specs=[HBM] * len(out_shape),
                              scratch_shapes=sems, name=name)(*operands)
    return pl.kernel(body, out_type=out_shape, mesh=plsc.ScalarSubcoreMesh(axis_name="seq", num_cores=1), name=name,
                     scratch_types=sems, compiler_params=pltpu.CompilerParams(collective_id=sequencer_id))(*operands)


AG_COPIES = 10


def _all_gather(name, shards, kinds, sequencer_id=None):
    n = len(shards)
    shapes = [s.shape for s in shards]

    def body(*refs):
        ins, outs = refs[:n], refs[n:2 * n]
        send_sems, recv_sems, local_sems = refs[2 * n:]
        x, y, c = _coords()
        me, sib, xn, yn, dg = (x, y, c), (x, y, 1 - c), (1 - x, y, c), (x, 1 - y, c), (1 - x, 1 - y, c)
        if sequencer_id is not None:
            _handshake([sib, xn, yn])

        def part(i, dev, half=None):
            return _slab(outs[i], kinds[i], shapes[i], 4 * dev[0] + 2 * dev[1] + dev[2], half)

        def copy(i, k, block, half, to, own=False):
            r = shapes[i][0]
            src = part(i, block, half) if not own else (ins[i] if half is None else ins[i].at[pl.ds(half * (r // 2), r // 2), :])
            return pltpu.make_async_remote_copy(
                src_ref=src, dst_ref=part(i, block, half), send_sem=send_sems.at[AG_COPIES * i + k],
                recv_sem=recv_sems.at[AG_COPIES * i + k], device_id=to, device_id_type=MESH)

        def other_core(dev):
            return (dev[0], dev[1], 1 - c)

        started = []

        def start(cp):
            cp.start()
            started.append(cp)

        for i in range(n):
            start(copy(i, 1, me, 0, xn, own=True))
            start(copy(i, 4, me, 1, yn, own=True))
            start(copy(i, 2, me, 1, xn, own=True))
            start(copy(i, 3, me, 0, yn, own=True))
            start(copy(i, 0, me, None, sib, own=True))
        mine = [pltpu.make_async_copy(ins[i], part(i, me), local_sems.at[i]) for i in range(n)]
        for cp in mine:
            cp.start()
        for i in range(n):
            copy(i, 1, xn, 0, me).wait_recv()
            start(copy(i, 5, xn, 0, yn))
            copy(i, 4, yn, 1, me).wait_recv()
            start(copy(i, 6, yn, 1, xn))
        for i in range(n):
            copy(i, 2, xn, 1, me).wait_recv()
            start(copy(i, 7, xn, None, sib))
            copy(i, 3, yn, 0, me).wait_recv()
            start(copy(i, 8, yn, None, sib))
        for i in range(n):
            copy(i, 5, dg, 0, me).wait_recv()
            copy(i, 6, dg, 1, me).wait_recv()
            start(copy(i, 9, dg, None, sib))
        for i in range(n):
            copy(i, 0, sib, None, me).wait_recv()
            for k, dev in ((7, xn), (8, yn), (9, dg)):
                copy(i, k, other_core(dev), None, me).wait_recv()
        for cp in started:
            cp.wait_send()
        for cp in mine:
            cp.wait()

    out_shape = [jax.ShapeDtypeStruct(_full_shape(s.shape, k), s.dtype) for s, k in zip(shards, kinds)]
    sems = [pltpu.SemaphoreType.DMA((AG_COPIES * n,)), pltpu.SemaphoreType.DMA((AG_COPIES * n,)), pltpu.SemaphoreType.DMA((n,))]
    return _launch(name, body, out_shape, shards, sems, sequencer_id)


def _sibling_copies(kinds, shard_shapes):
    def make(ins, outs, send_sems, recv_sems):
        x, y, c = _coords()
        return [pltpu.make_async_remote_copy(
            src_ref=_slab(ins[i], kinds[i], shard_shapes[i], 2 * q + (1 - c)), dst_ref=outs[i].at[q],
            send_sem=send_sems.at[N_CHIP * i + q], recv_sem=recv_sems.at[N_CHIP * i + q],
            device_id=(x, y, 1 - c), device_id_type=MESH) for i in range(len(ins)) for q in range(N_CHIP)]
    return make


def _sibling_side(partials, kinds, shard_shapes):
    return (partials, [jax.ShapeDtypeStruct((N_CHIP, *s), BF16) for s in shard_shapes], N_CHIP * len(partials),
            _sibling_copies(kinds, shard_shapes))


def _exchange_chips(name, chip_sums, sequencer_id=None):
    n = len(chip_sums)

    def body(*refs):
        ins, outs = refs[:n], refs[n:2 * n]
        send_sems, recv_sems = refs[2 * n:]
        x, y, c = _coords()
        if sequencer_id is not None:
            _handshake([(cx, cy, c) for cx, cy in _other_chips(x, y)])
        cps = []
        for i in range(n):
            for k, (cx, cy) in enumerate(_other_chips(x, y)):
                cps.append(pltpu.make_async_remote_copy(
                    src_ref=ins[i].at[2 * cx + cy], dst_ref=outs[i].at[k], send_sem=send_sems.at[3 * i + k],
                    recv_sem=recv_sems.at[3 * i + k], device_id=(cx, cy, c), device_id_type=MESH))
        for cp in cps:
            cp.start()
        for cp in cps:
            cp.wait()

    return _launch(name, body, [jax.ShapeDtypeStruct((3, *t.shape[1:]), BF16) for t in chip_sums], chip_sums,
                   [pltpu.SemaphoreType.DMA((3 * n,)), pltpu.SemaphoreType.DMA((3 * n,))], sequencer_id)


def _all_peers(x, y, c):
    return [(x ^ (k >> 2), y ^ ((k >> 1) & 1), c ^ (k & 1)) for k in range(1, N_DEV)]


def _small_scatter(name, packed, sequencer_id):
    rows = packed.shape[0] // N_DEV

    def body(p_ref, rb_ref, send_sems, recv_sems, local_sem):
        x, y, c = _coords()
        me = 4 * x + 2 * y + c
        peers = _all_peers(x, y, c)
        _handshake(peers)

        def piece(idx):
            return p_ref.at[pl.ds(pl.multiple_of(idx * rows, SUBLANES), rows), :]

        cps = [pltpu.make_async_remote_copy(src_ref=piece(me ^ k), dst_ref=rb_ref.at[k], send_sem=send_sems.at[k], recv_sem=recv_sems.at[k],
                                            device_id=peers[k - 1], device_id_type=MESH) for k in range(1, N_DEV)]
        for cp in cps:
            cp.start()
        mine = pltpu.make_async_copy(piece(me), rb_ref.at[0], local_sem)
        mine.start()
        for cp in cps:
            cp.wait()
        mine.wait()

    return _launch(name, body, [jax.ShapeDtypeStruct((N_DEV, rows, LANES), F32)], [packed],
                   [pltpu.SemaphoreType.DMA((N_DEV,)), pltpu.SemaphoreType.DMA((N_DEV,)), pltpu.SemaphoreType.DMA], sequencer_id)[0]


def _small_sum(name, pieces):
    def body(p_ref, o_ref):
        acc = p_ref[0]
        for k in range(1, N_DEV):
            acc = acc + p_ref[k]
        o_ref[...] = acc

    vm = pl.BlockSpec(memory_space=pltpu.VMEM)
    return pl.pallas_call(body, out_shape=jax.ShapeDtypeStruct(pieces.shape[1:], F32), in_specs=[vm], out_specs=vm, name=name)(pieces)


def _small_gather(name, tot, sequencer_id):
    rows = tot.shape[0]

    def body(t_ref, out_ref, send_sems, recv_sems, local_sem):
        x, y, c = _coords()
        me = 4 * x + 2 * y + c
        peers = _all_peers(x, y, c)
        _handshake(peers)

        def piece(idx):
            return out_ref.at[pl.ds(pl.multiple_of(idx * rows, SUBLANES), rows), :]

        cps = [pltpu.make_async_remote_copy(src_ref=t_ref, dst_ref=piece(me), send_sem=send_sems.at[k], recv_sem=recv_sems.at[k],
                                            device_id=peers[k - 1], device_id_type=MESH) for k in range(1, N_DEV)]
        for cp in cps:
            cp.start()
        mine = pltpu.make_async_copy(t_ref, piece(me), local_sem)
        mine.start()
        for cp in cps:
            cp.wait()
        mine.wait()

    return _launch(name, body, [jax.ShapeDtypeStruct((N_DEV * rows, LANES), F32)], [tot],
                   [pltpu.SemaphoreType.DMA((N_DEV,)), pltpu.SemaphoreType.DMA((N_DEV,)), pltpu.SemaphoreType.DMA], sequencer_id)[0]


def _all_reduce_small(name, packed):
    rows = packed.shape[0] // N_DEV

    def body(p_ref, out_ref, rb, tot, send_sems, recv_sems):
        x, y, c = _coords()
        me = 4 * x + 2 * y + c

        def peer(k):
            return (x ^ (k >> 2), y ^ ((k >> 1) & 1), c ^ (k & 1))

        def rows_of(idx):
            return pl.ds(pl.multiple_of(idx * rows, SUBLANES), rows)

        def piece(ref, idx):
            return ref.at[rows_of(idx), :]

        scatter = [pltpu.make_async_remote_copy(src_ref=piece(p_ref, me ^ k), dst_ref=rb.at[k], send_sem=send_sems.at[k],
                                                recv_sem=recv_sems.at[k], device_id=peer(k), device_id_type=MESH) for k in range(1, N_DEV)]
        for cp in scatter:
            cp.start()
        acc = p_ref[rows_of(me), :]
        for cp in scatter:
            cp.wait_recv()
        for k in range(1, N_DEV):
            acc = acc + rb[k]
        tot[...] = acc
        out_ref[rows_of(me), :] = acc
        gather = [pltpu.make_async_remote_copy(src_ref=tot, dst_ref=piece(out_ref, me), send_sem=send_sems.at[N_DEV + k],
                                               recv_sem=recv_sems.at[N_DEV + k], device_id=peer(k), device_id_type=MESH)
                  for k in range(1, N_DEV)]
        for cp in gather:
            cp.start()
        for k in range(1, N_DEV):
            pltpu.make_async_remote_copy(src_ref=tot, dst_ref=piece(out_ref, me ^ k), send_sem=send_sems.at[N_DEV + k],
                                         recv_sem=recv_sems.at[N_DEV + k], device_id=peer(k), device_id_type=MESH).wait_recv()
        for cp in scatter + gather:
            cp.wait_send()

    vm = pl.BlockSpec(memory_space=pltpu.VMEM)
    return pl.pallas_call(
        body, out_shape=jax.ShapeDtypeStruct(packed.shape, F32), in_specs=[vm], out_specs=vm,
        scratch_shapes=[pltpu.VMEM((N_DEV, rows, LANES), F32), pltpu.VMEM((rows, LANES), F32),
                        pltpu.SemaphoreType.DMA((2 * N_DEV,)), pltpu.SemaphoreType.DMA((2 * N_DEV,))],
        compiler_params=pltpu.CompilerParams(vmem_limit_bytes=VMEM_LIMIT), name=name)(packed)


def _adamw_math(g, w, m, v):
    m = ADAM_B1 * m + (1.0 - ADAM_B1) * g
    v = ADAM_B2 * v + (1.0 - ADAM_B2) * (g * g)
    delta = -ADAM_LR * ((m / ADAM_C1) / (jnp.sqrt(v / ADAM_C2) + ADAM_EPS) + ADAM_WD * w)
    return delta, m, v


def _slab_spec(kind, shard_shape, tr, slab_of):
    r, c = shard_shape
    if kind == "col":
        return pl.BlockSpec((tr, c), lambda q, i, sc: (i, slab_of(q, sc)))
    return pl.BlockSpec((tr, c), lambda q, i, sc: (slab_of(q, sc) * (r // tr) + i, 0))


def _chip_sum(name, partial, recv, kind, shard_shape, core):
    r, c = shard_shape
    tr = _blk(r, 1024)

    def body(core_ref, p_ref, r_ref, o_ref):
        o_ref[...] = (p_ref[...].astype(F32) + r_ref[...].astype(F32)).astype(BF16)

    spec4 = pl.BlockSpec((None, tr, c), lambda q, i, sc: (q, i, 0))
    grid_spec = pltpu.PrefetchScalarGridSpec(
        num_scalar_prefetch=1, grid=(N_CHIP, r // tr),
        in_specs=[_slab_spec(kind, shard_shape, tr, lambda q, sc: 2 * q + sc[0]), spec4], out_specs=spec4)
    return pl.pallas_call(body, out_shape=jax.ShapeDtypeStruct((N_CHIP, r, c), BF16), grid_spec=grid_spec,
                          compiler_params=_params(("parallel", "parallel")), name=name)(core, partial, recv)


def _adamw_shard(name, parts, w, m, v, chip):
    r, c = w.shape
    n_parts = len(parts)
    tr = _blk(r // n_parts, 256)
    per = r // n_parts // tr

    def body(chip_ref, *refs):
        src, (w_ref, m_ref, v_ref), (g_out, d_out, m_out, v_out) = refs[:2 * n_parts], refs[2 * n_parts:2 * n_parts + 3], refs[2 * n_parts + 3:]
        for p in range(n_parts):
            @pl.when(pl.program_id(0) // per == p)
            def _():
                g = src[2 * p][...].astype(F32)
                for k in range(3):
                    g = g + src[2 * p + 1][k].astype(F32)
                g_out[...] = g
                d_out[...], m_out[...], v_out[...] = _adamw_math(g, w_ref[...], m_ref[...], v_ref[...])

    def part_specs(p):
        at = lambda i: jnp.clip(i - p * per, 0, per - 1)
        return [pl.BlockSpec((None, tr, c), lambda i, sc: (sc[0], at(i), 0)), pl.BlockSpec((3, tr, c), lambda i, sc: (0, at(i), 0))]

    blk = pl.BlockSpec((tr, c), lambda i, sc: (i, 0))
    grid_spec = pltpu.PrefetchScalarGridSpec(
        num_scalar_prefetch=1, grid=(r // tr,), in_specs=[s for p in range(n_parts) for s in part_specs(p)] + [blk, blk, blk], out_specs=[blk] * 4)
    return pl.pallas_call(body, out_shape=[jax.ShapeDtypeStruct((r, c), F32)] * 4, grid_spec=grid_spec,
                          compiler_params=_params(("parallel",)), name=name)(chip, *[a for p in parts for a in p], w, m, v)


def _adamw_small(name, g, w, m, v):
    def body(g_ref, w_ref, m_ref, v_ref, d_out, m_out, v_out):
        d_out[...], m_out[...], v_out[...] = _adamw_math(g_ref[...], w_ref[...], m_ref[...], v_ref[...])

    vm = pl.BlockSpec(memory_space=pltpu.VMEM)
    return pl.pallas_call(body, out_shape=[jax.ShapeDtypeStruct(g.shape, F32)] * 3, in_specs=[vm] * 4, out_specs=[vm] * 3,
                          compiler_params=pltpu.CompilerParams(vmem_limit_bytes=VMEM_LIMIT), name=name)(g, w, m, v)


def _pack_rows(arrays, total_rows):
    flat = [a.reshape(-1, LANES) for a in arrays]
    used = sum(f.shape[0] for f in flat)
    return jnp.concatenate(flat + [jnp.zeros((total_rows - used, LANES), F32)], axis=0)


def _unpack_rows(packed, like):
    out, at = [], 0
    for a in like:
        n = a.size // LANES
        out.append(packed[at:at + n].reshape(a.shape))
        at += n
    return out


def kernel(x, norm_mix_g, w_in, conv_w, conv_b, lru_wa, lru_ba, lru_wx, lru_bx, lru_lambda, w_proj_attn, w_proj_lru, w_out, norm_mlp_g, w_up, w_down, norm_final_g, loss_target, m_norm_mix_g, m_w_in, m_conv_w, m_conv_b, m_lru_wa, m_lru_ba, m_lru_wx, m_lru_bx, m_lru_lambda, m_w_proj_attn, m_w_proj_lru, m_w_out, m_norm_mlp_g, m_w_up, m_w_down, m_norm_final_g, v_norm_mix_g, v_w_in, v_conv_w, v_conv_b, v_lru_wa, v_lru_ba, v_lru_wx, v_lru_bx, v_lru_lambda, v_w_proj_attn, v_w_proj_lru, v_w_out, v_norm_mlp_g, v_w_up, v_w_down, v_norm_final_g):
    xs, tgt = x[0], loss_target[0]
    s, d = xs.shape
    nh = d // HEAD
    ix, iy, ic = _coords()
    core = jnp.reshape(ic, (1,)).astype(jnp.int32)
    chip = jnp.reshape(2 * ix + iy, (1,)).astype(jnp.int32)
    dev = 4 * ix + 2 * iy + ic

    big = [w_in[0], w_proj_attn[0], w_proj_lru[0], w_out[0], w_up[0], w_down[0]]
    big_m = [m_w_in[0], m_w_proj_attn[0], m_w_proj_lru[0], m_w_out[0], m_w_up[0], m_w_down[0]]
    big_v = [v_w_in[0], v_w_proj_attn[0], v_w_proj_lru[0], v_w_out[0], v_w_up[0], v_w_down[0]]
    kinds = ["col", "row", "row", "row", "col", "row"]
    pad_taps = lambda t: jnp.pad(t, ((0, SUBLANES - CONV_TAPS), (0, 0)))
    shards = [w.astype(BF16) for w in big]
    pad_taps2 = lambda t: jnp.pad(t, ((0, 2 * SUBLANES - CONV_TAPS), (0, 0)))
    win, cw_slots = _all_gather("all_gather_w_in", [shards[0], pad_taps2(conv_w[0])], ["col", "slot"], sequencer_id=7)
    later = lax.optimization_barrier((shards[1:], win))[0]
    wpa, wpl, wout = _all_gather("all_gather_mix", later[:3], kinds[1:4], sequencer_id=1)
    wup, wdown = _all_gather("all_gather_mlp", later[3:], kinds[4:], sequencer_id=5)
    cw8 = jnp.transpose(cw_slots[:, :SUBLANES], (1, 0, 2)).reshape(SUBLANES, d)
    row_id = lax.broadcasted_iota(jnp.int32, (SUBLANES, d), 0)
    vec8 = sum(jnp.where(row_id == k, t, 0.0) for k, t in ((VEC_CB, conv_b), (VEC_BA, lru_ba), (VEC_BX, lru_bx), (VEC_LAM, lru_lambda)))
    wa16, wx16 = lru_wa[0].astype(BF16), lru_wx[0].astype(BF16)
    slopes = 2.0 ** (-8.0 * jnp.arange(1, nh + 1, dtype=F32) / nh)

    def seg_specs(*segs):
        return lambda bm, bn: [pl.BlockSpec((bm, bn), (lambda i, j, kk, sg=sg: (i, sg * (d // bn) + j))) for sg in segs]

    def plain_specs(k):
        return lambda bm, bn: [pl.BlockSpec((bm, bn), lambda i, j, kk: (i, j)) for _ in range(k)]

    xn = _rms_fwd("norm_mix", xs, norm_mix_g)
    proj = _mm_fwd("proj_in", xn, win, 0, 7 * d, [F32], bm=2048)[0]
    att, lse = _attn_fwd(proj, d, slopes)
    xc, hp, h2d, ylru = _lru_fwd(proj, d, cw8, vec8, wa16, wx16)
    pa = _mm_fwd("proj_attn", att, wpa, 0, d, [BF16], bm=2048)[0]

    def merge(acc, pa_b, ga, gl):
        return acc, _sigmoid(ga) * pa_b.astype(F32) + _sigmoid(gl) * acc

    plr, merged = _mm_fwd("proj_lru_merge", ylru, wpl, 0, d, [BF16, BF16], merge, (pa, proj, proj),
                          lambda bm, bn: plain_specs(1)(bm, bn) + seg_specs(SEG_GA, SEG_GL)(bm, bn), bn=512)
    h1 = _mm_fwd("mix_out", merged, wout, 0, d, [F32], lambda acc, r: (acc + r,), (xs,), plain_specs(1))[0]
    hn = _rms_fwd("norm_mlp", h1, norm_mlp_g)

    def relu2(acc):
        return acc, jnp.square(jnp.maximum(acc, 0.0))

    up, hid = _mm_fwd("mlp_up", hn, wup, 0, wup.shape[1], [BF16, BF16], relu2, bm=2048)
    h2 = _mm_fwd("mlp_down", hid, wdown, 0, d, [F32], lambda acc, r: (acc + r,), (h1,), plain_specs(1))[0]
    dh2, dh2b, dg3, loss_lanes = _final_loss(h2, tgt, norm_final_g.reshape(1, d))
    loss = lax.psum(0.5 / d * jnp.sum(loss_lanes), ("x", "y", "c"))
    dh2b = lax.optimization_barrier((dh2b, loss))[0]

    def reduce_group(tag, kk, shp, partials, from_sibling, sequencer_id):
        sums = [_chip_sum(f"chip_sum_{tag}_{i}", p, f, k, sh, core) for i, (p, f, k, sh) in enumerate(zip(partials, from_sibling, kk, shp))]
        return list(zip(sums, _exchange_chips(f"rs_chips_{tag}", sums, sequencer_id)))

    g_wdown = _mm_tn("mlp_down_dw", hid, dh2b)
    (dup,), sib_down = _mm_nt("mlp_down_dx", dh2b, wdown, [BF16], lambda acc, u: (acc * (2.0 * jnp.maximum(u.astype(F32), 0.0)),), (up,),
                              plain_specs(1), side=_sibling_side([g_wdown], kinds[5:], [big[5].shape]))
    (red_down,) = reduce_group("mlp_down", kinds[5:], [big[5].shape], [g_wdown], sib_down, 2)
    dup = lax.optimization_barrier((dup, red_down[0]))[0]
    g_wup = _mm_tn("mlp_up_dw", hn, dup)
    (dhn,), sib_up = _mm_nt("mlp_up_dx", dup, wup, [F32], side=_sibling_side([g_wup], kinds[4:5], [big[4].shape]))
    (red_up,) = reduce_group("mlp_up", kinds[4:5], [big[4].shape], [g_wup], sib_up, 10)
    dhn = lax.optimization_barrier((dhn, red_up[0]))[0]
    dh1, dh1b, dg2 = _rms_bwd("norm_mlp_bwd", h1, norm_mlp_g, dhn, dh2)

    def merge_bwd(acc, pa_b, pl_b, ga, gl):
        sa, sl = _sigmoid(ga), _sigmoid(gl)
        return acc * sa, acc * sl, acc * pa_b.astype(F32) * sa * (1.0 - sa), acc * pl_b.astype(F32) * sl * (1.0 - sl)

    dpa, dpl, dga, dgl = _mm_nt("mix_out_dx", dh1b, wout, [BF16] * 4, merge_bwd, (pa, plr, proj, proj),
                                lambda bm, bn: plain_specs(2)(bm, bn) + seg_specs(SEG_GA, SEG_GL)(bm, bn), bn=512)
    g_wout = _mm_tn("mix_out_dw", merged, dh1b)
    datt = _mm_nt("proj_attn_dx", dpa, wpa, [F32], bm=2048)[0]
    g_wpa = _mm_tn("proj_attn_dw", att, dpa)

    def lru_out_bwd(acc, h_b, gate):
        return acc * _gelu(gate), acc * h_b.astype(F32) * _gelu_grad(gate)

    g_wpl = _mm_tn("proj_lru_dw", ylru, dpl)
    shp_mix = [w.shape for w in big[1:4]]
    (dh, dxg), sib_mix = _mm_nt("proj_lru_dx", dpl, wpl, [F32, BF16], lru_out_bwd, (h2d, proj),
                                lambda bm, bn: plain_specs(1)(bm, bn) + seg_specs(SEG_GATE)(bm, bn), bn=512,
                                side=_sibling_side([g_wpa, g_wpl, g_wout], kinds[1:4], shp_mix))
    red_pa, red_pl, red_out = reduce_group("mix", kinds[1:4], shp_mix, [g_wpa, g_wpl, g_wout], sib_mix, 3)
    dh = lax.optimization_barrier((dh, red_down[1]))[0]
    dxr, dwa, dwx, dvec, dconv = _lru_bwd(hp, dh, xc, proj, cw8, wa16, wx16, vec8)
    dproj = _attn_bwd(proj, d, datt, att, lse, slopes, (dxr, dxg, dga, dgl))

    def small_step(tag, grads, ws, ms, vs, like, after, seq=None):
        n_rows = sum(g.size for g in grads) // LANES
        per_dev = -(-n_rows // (N_DEV * SUBLANES)) * SUBLANES
        packed = lax.optimization_barrier((_pack_rows(grads, N_DEV * per_dev), after))[0]
        if seq is None:
            total = _all_reduce_small(f"all_reduce_{tag}", packed)
        else:
            pieces = lax.optimization_barrier((_small_scatter(f"scatter_{tag}", packed, seq[0]), seq[2]))[0]
            total = _small_gather(f"gather_{tag}", _small_sum(f"sum_{tag}", pieces), seq[1])
        w_rows = -(-(sum(w.size for w in ws) // LANES) // SUBLANES) * SUBLANES
        upd = _adamw_small(f"adamw_{tag}", total[:w_rows], _pack_rows(ws, w_rows), _pack_rows(ms, w_rows), _pack_rows(vs, w_rows))
        return _unpack_rows(total, like), [_unpack_rows(t, ws) for t in upd]

    early_w = [conv_b, lru_wa, lru_ba, lru_wx, lru_bx, lru_lambda, norm_mlp_g, norm_final_g]
    early_m = [m_conv_b, m_lru_wa, m_lru_ba, m_lru_wx, m_lru_bx, m_lru_lambda, m_norm_mlp_g, m_norm_final_g]
    early_v = [v_conv_b, v_lru_wa, v_lru_ba, v_lru_wx, v_lru_bx, v_lru_lambda, v_norm_mlp_g, v_norm_final_g]
    early_g = [dconv[CONV_TAPS:CONV_TAPS + 1], dwa, dvec[VEC_BA:VEC_BA + 1], dwx, dvec[VEC_BX:VEC_BX + 1],
               dvec[VEC_LAM:VEC_LAM + 1], dg2, dg3, dconv[0:CONV_TAPS]]
    dproj = lax.optimization_barrier((dproj, red_up[1]))[0]
    dproj = lax.optimization_barrier((dproj, red_pa[1], red_pl[1], red_out[1]))[0]
    early_sum, early_upd = small_step("small", early_g, early_w, early_m, early_v,
                                      early_w + [jax.ShapeDtypeStruct((1, CONV_TAPS, d), F32)], dxr, seq=(8, 9, dproj))
    g_cw_full = early_sum[-1]
    cshard = conv_w.shape[2]
    g_cw = lax.dynamic_slice(g_cw_full, (0, 0, dev * cshard), (1, CONV_TAPS, cshard))
    cw_delta, cw_m, cw_v = (t[:CONV_TAPS][None] for t in _adamw_small(
        "adamw_conv_w", pad_taps(g_cw[0]), pad_taps(conv_w[0]), pad_taps(m_conv_w[0]), pad_taps(v_conv_w[0])))
    half = (big[0].shape[0] // 2, big[0].shape[1])
    g_in0 = _mm_tn("proj_in_dw_0", xn, dproj, part=(0, 2))
    g_in1, sib_in0 = _mm_tn("proj_in_dw_1", xn, dproj, part=(1, 2), side=_sibling_side([g_in0], ["col"], [half]))
    red_in = reduce_group("in_0", ["col"], [half], [g_in0], sib_in0, 4)
    dproj = lax.optimization_barrier((dproj, red_in[0][0], early_sum))[0]
    (dxn0,), sib_in1 = _mm_nt("proj_in_dx_0", dproj, win, [F32], part=(0, 2), side=_sibling_side([g_in1], ["col"], [half]))
    red_in += reduce_group("in_1", ["col"], [half], [g_in1], sib_in1, 6)
    dproj = lax.optimization_barrier((dproj, red_in[1][0]))[0]
    dxn1 = _mm_nt("proj_in_dx_1", dproj, win, [F32], part=(1, 2))[0]
    dxn = jnp.concatenate([dxn0, dxn1], axis=0)
    dxn = lax.optimization_barrier((dxn, red_in[0][1]))[0]
    grad_x, _, dg1 = _rms_bwd("norm_mix_bwd", xs, norm_mix_g, dxn, dh1)
    red_up, red_down = lax.optimization_barrier(((red_up, red_down), dg1))[0]
    big_out = {i: _adamw_shard(f"adamw_{i}", [red], big[i], big_m[i], big_v[i], chip) for i, red in ((4, red_up), (5, red_down))}
    big_out.update({i: _adamw_shard(f"adamw_{i}", [red], big[i], big_m[i], big_v[i], chip) for i, red in ((1, red_pa), (2, red_pl), (3, red_out))})
    late_sum, late_upd = small_step("norm_mix", [dg1], [norm_mix_g], [m_norm_mix_g], [v_norm_mix_g], [norm_mix_g], (big_out[4], big_out[5]))
    big_out[0] = _adamw_shard("adamw_0", red_in, big[0], big_m[0], big_v[0], chip)
    s_grad = late_sum + early_sum[:-1]
    s_delta, s_m, s_v = (late_upd[j] + early_upd[j] for j in range(3))


    names = ["norm_mix_g", "w_in", "conv_w", "conv_b", "lru_wa", "lru_ba", "lru_wx", "lru_bx", "lru_lambda", "w_proj_attn", "w_proj_lru",
             "w_out", "norm_mlp_g", "w_up", "w_down", "norm_final_g"]
    small_names = ["norm_mix_g", "conv_b", "lru_wa", "lru_ba", "lru_wx", "lru_bx", "lru_lambda", "norm_mlp_g", "norm_final_g"]
    big_names = ["w_in", "w_proj_attn", "w_proj_lru", "w_out", "w_up", "w_down"]
    res = {"conv_w": (g_cw, cw_delta, cw_m, cw_v)}
    for i, nm in enumerate(small_names):
        res[nm] = (s_grad[i], s_delta[i], s_m[i], s_v[i])
    for i, nm in enumerate(big_names):
        res[nm] = tuple(t[None] for t in big_out[i])
    return (loss, grad_x[None], *[res[nm][0] for nm in names], *[res[nm][1] for nm in names],
            *[res[nm][2] for nm in names], *[res[nm][3] for nm in names])
```

```python
import jax
import jax.numpy as jnp
from jax import lax
from jax.experimental import pallas as pl
from jax.experimental.pallas import tpu as pltpu
from jax.experimental.pallas import tpu_sc as plsc

F32, BF16 = jnp.float32, jnp.bfloat16
MESH = pl.DeviceIdType.MESH
HBM = pl.BlockSpec(memory_space=pltpu.HBM)
N_DEV = 8
N_CHIP = 4
HEAD = 128
SPAN = 128
DILATIONS = (1, 4, 16)
CONV_TAPS = 4
LRU_C = 8.0
NORM_EPS = 1e-6
LANES = 128
SUBLANES = 8
VMEM_LIMIT = 56 * 1024 * 1024
ADAM_LR, ADAM_B1, ADAM_B2, ADAM_EPS, ADAM_WD, ADAM_STEP = 0.001, 0.9, 0.999, 1e-08, 0.01, 10
ADAM_C1 = 1.0 - ADAM_B1 ** ADAM_STEP
ADAM_C2 = 1.0 - ADAM_B2 ** ADAM_STEP
NEG = -1e30


def _params(sem=None):
    return pltpu.CompilerParams(dimension_semantics=sem, vmem_limit_bytes=VMEM_LIMIT)


def _sigmoid(v):
    return 1.0 / (1.0 + jnp.exp(-v))


def _gelu(v):
    k = 0.7978845608028654
    return 0.5 * v * (1.0 + jnp.tanh(k * (v + 0.044715 * v * v * v)))


def _gelu_grad(v):
    k = 0.7978845608028654
    t = jnp.tanh(k * (v + 0.044715 * v * v * v))
    return 0.5 * (1.0 + t) + 0.5 * v * (1.0 - t * t) * k * (1.0 + 3.0 * 0.044715 * v * v)


NN = (((1,), (0,)), ((), ()))
NT = (((1,), (1,)), ((), ()))
TN = (((0,), (0,)), ((), ()))


def _mm(name, a, a_spec, b, b_spec, dn, grid, out_shapes, out_specs, acc_block, epilogue=None, extras=(), extra_specs=(), side=None):
    nk, ne, no = grid[2], len(extras), len(out_shapes)
    side_ops, side_shapes, side_copies, make_copies = side if side is not None else ((), (), 0, None)
    ns_in, ns_out = len(side_ops), len(side_shapes)

    def body(*refs):
        a_ref, b_ref = refs[0], refs[1]
        ex, side_in = refs[2:2 + ne], refs[2 + ne:2 + ne + ns_in]
        outs = refs[2 + ne + ns_in:2 + ne + ns_in + no]
        side_out = refs[2 + ne + ns_in + no:2 + ne + ns_in + no + ns_out]
        scratch = refs[2 + ne + ns_in + no + ns_out:]
        at = [pl.program_id(ax) for ax in range(3)]
        if side is not None:
            @pl.when((at[0] == 0) & (at[1] == 0) & (at[2] == 0))
            def _():
                for cp in make_copies(side_in, side_out, scratch[-2], scratch[-1]):
                    cp.start()

        part = lax.dot_general(a_ref[...], b_ref[...], dn, preferred_element_type=F32)

        def finish(acc):
            vals = epilogue(acc, *[e[...] for e in ex]) if epilogue is not None else (acc,)
            for o, v in zip(outs, vals):
                o[...] = v.astype(o.dtype)

        if nk == 1:
            finish(part)
        else:
            acc_ref, k = scratch[0], at[2]

            @pl.when(k == 0)
            def _():
                acc_ref[...] = part

            @pl.when(k > 0)
            def _():
                acc_ref[...] += part

            @pl.when(k == nk - 1)
            def _():
                finish(acc_ref[...])

        if side is not None:
            @pl.when((at[0] == grid[0] - 1) & (at[1] == grid[1] - 1) & (at[2] == grid[2] - 1))
            def _():
                for cp in make_copies(side_in, side_out, scratch[-2], scratch[-1]):
                    cp.wait()

    scratch_shapes = [pltpu.VMEM(acc_block, F32)] if nk > 1 else []
    if side is not None:
        scratch_shapes += [pltpu.SemaphoreType.DMA((side_copies,)), pltpu.SemaphoreType.DMA((side_copies,))]
    res = pl.pallas_call(
        body, out_shape=[*out_shapes, *side_shapes], grid=grid, in_specs=[a_spec, b_spec, *extra_specs, *[HBM] * ns_in],
        out_specs=[*out_specs, *[HBM] * ns_out], scratch_shapes=scratch_shapes,
        compiler_params=_params(("arbitrary",) * 3 if side is not None else ("parallel", "parallel", "arbitrary")),
        name=name)(a, b, *extras, *side_ops)
    return res if side is None else (res[:no], res[no:])


def _blk(n, pref):
    return pref if n % pref == 0 else n


def _kblk(k):
    return k if k <= 2048 else next(b for b in (2048, 1024, 512) if k % b == 0)


def _mm_fwd(name, a, w, col0, ncols, out_dtypes, epilogue=None, extras=(), extra_specs_fn=None, bm=1024, bn=1024):
    m, k = a.shape
    bm, bn = _blk(m, bm), _blk(ncols, bn)
    bk = _kblk(k)
    nk = k // bk
    cb0 = col0 // bn
    grid = (m // bm, ncols // bn, nk)
    a_spec = pl.BlockSpec((bm, bk), lambda i, j, kk: (i, kk))
    b_spec = pl.BlockSpec((bk, bn), lambda i, j, kk: (kk, cb0 + j))
    shapes = [jax.ShapeDtypeStruct((m, ncols), dt) for dt in out_dtypes]
    specs = [pl.BlockSpec((bm, bn), lambda i, j, kk: (i, j)) for _ in out_dtypes]
    ex_specs = extra_specs_fn(bm, bn) if extra_specs_fn else ()
    return _mm(name, a, a_spec, w, b_spec, NN, grid, shapes, specs, (bm, bn), epilogue, extras, ex_specs)


def _mm_nt(name, a, w, out_dtypes, epilogue=None, extras=(), extra_specs_fn=None, part=(0, 1), side=None, bm=1024, bn=1024):
    n = w.shape[0]
    if a.ndim == 3:
        seg_cols, m, k = a.shape[2], a.shape[1], a.shape[0] * a.shape[2]
    else:
        m, k = a.shape
    m = m // part[1]
    bm, bn = _blk(m, bm), _blk(n, bn)
    bk = _kblk(k)
    grid = (m // bm, n // bn, k // bk)
    i0 = part[0] * (m // bm)
    if a.ndim == 3:
        per = seg_cols // bk
        a_spec = pl.BlockSpec((None, bm, bk), lambda i, j, kk: (kk // per, i0 + i, kk % per))
    else:
        a_spec = pl.BlockSpec((bm, bk), lambda i, j, kk: (i0 + i, kk))
    b_spec = pl.BlockSpec((bn, bk), lambda i, j, kk: (j, kk))
    shapes = [jax.ShapeDtypeStruct((m, n), dt) for dt in out_dtypes]
    specs = [pl.BlockSpec((bm, bn), lambda i, j, kk: (i, j)) for _ in out_dtypes]
    ex_specs = extra_specs_fn(bm, bn) if extra_specs_fn else ()
    return _mm(name, a, a_spec, w, b_spec, NT, grid, shapes, specs, (bm, bn), epilogue, extras, ex_specs, side)


def _mm_tn(name, a, b, part=(0, 1), side=None, bm=1024, bn=2048):
    t, m = a.shape
    n = b.shape[1] if b.ndim == 2 else b.shape[0] * b.shape[2]
    m = m // part[1]
    bm, bn = _blk(m, bm), _blk(n, bn)
    grid = (m // bm, n // bn, 1)
    i0 = part[0] * (m // bm)
    a_spec = pl.BlockSpec((t, bm), lambda i, j, kk: (0, i0 + i))
    if b.ndim == 3:
        per = b.shape[2] // bn
        b_spec = pl.BlockSpec((None, t, bn), lambda i, j, kk: (j // per, 0, j % per))
    else:
        b_spec = pl.BlockSpec((t, bn), lambda i, j, kk: (0, j))
    res = _mm(name, a, a_spec, b, b_spec, TN, grid, [jax.ShapeDtypeStruct((m, n), BF16)],
              [pl.BlockSpec((bm, bn), lambda i, j, kk: (i, j))], (bm, bn), side=side)
    return res[0] if side is None else (res[0][0], res[1])


ROWS = 256


def _row_spec(d):
    return pl.BlockSpec((ROWS, d), lambda i: (i, 0))


def _vec_spec(d, rows=1):
    return pl.BlockSpec((rows, d), lambda i: (0, 0))


def _rms_fwd(name, x, g):
    s, d = x.shape

    def body(x_ref, g_ref, o_ref):
        xv = x_ref[...]
        r = lax.rsqrt(jnp.mean(xv * xv, axis=-1, keepdims=True) + NORM_EPS)
        o_ref[...] = (xv * r * g_ref[...]).astype(BF16)

    return pl.pallas_call(body, out_shape=jax.ShapeDtypeStruct((s, d), BF16), grid=(s // ROWS,),
                          in_specs=[_row_spec(d), _vec_spec(d)], out_specs=_row_spec(d),
                          compiler_params=_params(("parallel",)), name=name)(x, g)


def _rms_bwd_math(xv, g, dy):
    r = lax.rsqrt(jnp.mean(xv * xv, axis=-1, keepdims=True) + NORM_EPS)
    n = xv * r
    z = dy * g
    dx = r * (z - n * jnp.mean(z * n, axis=-1, keepdims=True))
    return dx, jnp.sum(dy * n, axis=0, keepdims=True)


def _rms_bwd(name, x, g, dy, resid):
    s, d = x.shape

    def body(x_ref, g_ref, dy_ref, r_ref, dx_ref, dxb_ref, dg_ref):
        dx, dg = _rms_bwd_math(x_ref[...], g_ref[...], dy_ref[...])
        dx = dx + r_ref[...]
        dx_ref[...] = dx
        dxb_ref[...] = dx.astype(BF16)

        @pl.when(pl.program_id(0) == 0)
        def _():
            dg_ref[...] = jnp.zeros_like(dg_ref)

        dg_ref[...] += dg

    return pl.pallas_call(
        body, out_shape=[jax.ShapeDtypeStruct((s, d), F32), jax.ShapeDtypeStruct((s, d), BF16), jax.ShapeDtypeStruct((1, d), F32)],
        grid=(s // ROWS,), in_specs=[_row_spec(d), _vec_spec(d), _row_spec(d), _row_spec(d)],
        out_specs=[_row_spec(d), _row_spec(d), _vec_spec(d)], compiler_params=_params(("arbitrary",)), name=name)(x, g, dy, resid)


def _final_loss(h2, tgt, g):
    s, d = h2.shape

    def body(x_ref, t_ref, g_ref, dx_ref, dxb_ref, dg_ref, ls_ref):
        xv, gv = x_ref[...], g_ref[...]
        r = lax.rsqrt(jnp.mean(xv * xv, axis=-1, keepdims=True) + NORM_EPS)
        diff = xv * r * gv - t_ref[...]
        dx, dg = _rms_bwd_math(xv, gv, diff * (1.0 / d))
        dx_ref[...] = dx
        dxb_ref[...] = dx.astype(BF16)

        @pl.when(pl.program_id(0) == 0)
        def _():
            dg_ref[...] = jnp.zeros_like(dg_ref)
            ls_ref[...] = jnp.zeros_like(ls_ref)

        dg_ref[...] += dg
        ls_ref[...] += jnp.sum(diff * diff, axis=0, keepdims=True)

    return pl.pallas_call(
        body, out_shape=[jax.ShapeDtypeStruct((s, d), F32), jax.ShapeDtypeStruct((s, d), BF16),
                         jax.ShapeDtypeStruct((1, d), F32), jax.ShapeDtypeStruct((1, d), F32)],
        grid=(s // ROWS,), in_specs=[_row_spec(d), _row_spec(d), _vec_spec(d)],
        out_specs=[_row_spec(d), _row_spec(d), _vec_spec(d), _vec_spec(d)],
        compiler_params=_params(("arbitrary",)), name="final_norm_loss")(h2, tgt, g)


ATTN_Q = 128


ATTN_BATCH = 8


def _attn_units(s):
    units = []
    for gi, d in enumerate(DILATIONS):
        for r in range(d):
            for q0 in range(0, s // d, ATTN_Q):
                k0 = max(q0 - SPAN, 0)
                units.append((gi, d, r, q0, k0, q0 + ATTN_Q - k0))
    return units


def _stream_rows(d, r, start, size):
    return pl.ds(r + start * d, size) if d == 1 else pl.ds(r + start * d, size, stride=d)


def _attn_bias(bias_ref, slope):
    qi = lax.broadcasted_iota(jnp.int32, (ATTN_Q, ATTN_Q + SPAN), 0)
    kj = lax.broadcasted_iota(jnp.int32, (ATTN_Q, ATTN_Q + SPAN), 1)
    dist = SPAN + qi - kj
    valid = (dist >= 0) & (dist <= SPAN)
    for gi, d in enumerate(DILATIONS):
        bias_ref[gi] = jnp.where(valid, -(slope * d) * dist.astype(F32), NEG)


def _attn_scores(q_ref, k_ref, bias_ref, gi, d, r, q0, k0, nk):
    qrows, krows = _stream_rows(d, r, q0, ATTN_Q), _stream_rows(d, r, k0, nk)
    qb = q_ref[qrows, :].astype(BF16)
    kb = k_ref[krows, :].astype(BF16)
    sc = lax.dot_general(qb, kb, NT, preferred_element_type=F32) * (HEAD ** -0.5) + bias_ref[gi, :, pl.ds(ATTN_Q + SPAN - nk, nk)]
    return sc, qb, kb, qrows, krows


def _attn_fwd(proj, dm, slopes):
    s = proj.shape[0]
    units = _attn_units(s)

    def body(sl_ref, q_ref, k_ref, v_ref, att_ref, lse_ref, *scr):
        o_scr, l_scr, bias = scr[:3], scr[3:6], scr[6]
        _attn_bias(bias, sl_ref[pl.program_id(0)])
        for first in range(0, len(units), ATTN_BATCH):
            batch = units[first:first + ATTN_BATCH]
            scored = [_attn_scores(q_ref, k_ref, bias, gi, d, r, q0, k0, nk) for gi, d, r, q0, k0, nk in batch]
            soft = []
            for sc, _, _, _, _ in scored:
                m = jnp.max(sc, axis=-1, keepdims=True)
                p = jnp.exp(sc - m)
                soft.append((m, p, jnp.sum(p, axis=-1, keepdims=True)))
            outs = [lax.dot_general(p.astype(BF16), v_ref[sco[4], :].astype(BF16), NN, preferred_element_type=F32)
                    for (m, p, l), sco in zip(soft, scored)]
            for (gi, *_), (m, p, l), sco, o in zip(batch, soft, scored, outs):
                o_scr[gi][sco[3], :] = o / l
                l_scr[gi][sco[3], :] = jnp.broadcast_to(m + jnp.log(l), (ATTN_Q, HEAD))
        l0, l1, l2 = l_scr[0][...], l_scr[1][...], l_scr[2][...]
        m = jnp.maximum(jnp.maximum(l0, l1), l2)
        w0, w1, w2 = jnp.exp(l0 - m), jnp.exp(l1 - m), jnp.exp(l2 - m)
        tot = w0 + w1 + w2
        att_ref[...] = ((w0 * o_scr[0][...] + w1 * o_scr[1][...] + w2 * o_scr[2][...]) / tot).astype(BF16)
        lse_ref[...] = m + jnp.log(tot)

    def seg(i):
        return pl.BlockSpec((s, HEAD), lambda h: (0, i * (dm // HEAD) + h))

    col = pl.BlockSpec((s, HEAD), lambda h: (0, h))
    return pl.pallas_call(
        body, out_shape=[jax.ShapeDtypeStruct((s, dm), BF16), jax.ShapeDtypeStruct((s, dm), F32)], grid=(dm // HEAD,),
        in_specs=[pl.BlockSpec(memory_space=pltpu.SMEM), seg(0), seg(1), seg(2)], out_specs=[col, col],
        scratch_shapes=[pltpu.VMEM((s, HEAD), F32)] * (2 * len(DILATIONS)) + [pltpu.VMEM((len(DILATIONS), ATTN_Q, ATTN_Q + SPAN), F32)],
        compiler_params=_params(("parallel",)), name="attn_fwd")(slopes, proj, proj, proj)


def _attn_bwd(proj, dm, datt, att, lse, slopes, others):
    s = proj.shape[0]
    units = _attn_units(s)

    def body(sl_ref, q_ref, k_ref, v_ref, do_ref, att_ref, lse_ref, o3, o4, o5, o6, out_ref, dq_scr, dk_scr, dv_scr, dl_scr, bias):
        _attn_bias(bias, sl_ref[pl.program_id(0)])
        delta = jnp.sum(do_ref[...] * att_ref[...].astype(F32), axis=-1, keepdims=True)
        dl_scr[...] = jnp.broadcast_to(delta, (s, HEAD))
        dq_scr[...] = jnp.zeros_like(dq_scr)
        dk_scr[...] = jnp.zeros_like(dk_scr)
        dv_scr[...] = jnp.zeros_like(dv_scr)
        for first in range(0, len(units), ATTN_BATCH):
            scored = [_attn_scores(q_ref, k_ref, bias, gi, d, r, q0, k0, nk) for gi, d, r, q0, k0, nk in units[first:first + ATTN_BATCH]]
            dobs = [do_ref[sco[3], :].astype(BF16) for sco in scored]
            dps = [lax.dot_general(dob, v_ref[sco[4], :].astype(BF16), NT, preferred_element_type=F32) for dob, sco in zip(dobs, scored)]
            ps = [jnp.exp(sco[0] - lse_ref[sco[3], :][:, 0:1]) for sco in scored]
            dss = [(p * (dp - dl_scr[sco[3], :][:, 0:1]) * (HEAD ** -0.5)).astype(BF16) for p, dp, sco in zip(ps, dps, scored)]
            dqs = [lax.dot_general(ds, sco[2], NN, preferred_element_type=F32) for ds, sco in zip(dss, scored)]
            dks = [lax.dot_general(ds, sco[1], TN, preferred_element_type=F32) for ds, sco in zip(dss, scored)]
            dvs = [lax.dot_general(p.astype(BF16), dob, TN, preferred_element_type=F32) for p, dob in zip(ps, dobs)]
            for sco, dq, dk, dv in zip(scored, dqs, dks, dvs):
                dq_scr[sco[3], :] += dq
                dk_scr[sco[4], :] += dk
                dv_scr[sco[4], :] += dv
        for j, scr in enumerate((dq_scr, dk_scr, dv_scr)):
            out_ref[j] = scr[...].astype(BF16)
        for j, other in enumerate((o3, o4, o5, o6)):
            out_ref[3 + j] = other[...]

    def seg(i):
        return pl.BlockSpec((s, HEAD), lambda h: (0, i * (dm // HEAD) + h))

    col = pl.BlockSpec((s, HEAD), lambda h: (0, h))
    return pl.pallas_call(
        body, out_shape=jax.ShapeDtypeStruct((7, s, dm), BF16), grid=(dm // HEAD,),
        in_specs=[pl.BlockSpec(memory_space=pltpu.SMEM), seg(0), seg(1), seg(2), col, col, col, col, col, col, col],
        out_specs=pl.BlockSpec((7, s, HEAD), lambda h: (0, 0, h)),
        scratch_shapes=[pltpu.VMEM((s, HEAD), F32)] * 4 + [pltpu.VMEM((len(DILATIONS), ATTN_Q, ATTN_Q + SPAN), F32)],
        compiler_params=_params(("parallel",)), name="attn_bwd")(slopes, proj, proj, proj, datt, att, lse, *others)


VEC_CB, VEC_BA, VEC_BX, VEC_LAM = 0, 1, 2, 3
SEG_Q, SEG_K, SEG_V, SEG_X, SEG_GATE, SEG_GA, SEG_GL = range(7)


def _softplus(z):
    return jnp.maximum(z, 0.0) + jnp.log1p(jnp.exp(-jnp.abs(z)))


def _gate_math(xc, wa_ref, wx_ref, vec):
    xcb = xc.astype(BF16)
    nh = xc.shape[1] // HEAD
    pre_a = jnp.concatenate([jnp.dot(xcb[:, h * HEAD:(h + 1) * HEAD], wa_ref[h], preferred_element_type=F32) for h in range(nh)], axis=1)
    pre_x = jnp.concatenate([jnp.dot(xcb[:, h * HEAD:(h + 1) * HEAD], wx_ref[h], preferred_element_type=F32) for h in range(nh)], axis=1)
    ra = _sigmoid(pre_a + vec[VEC_BA:VEC_BA + 1])
    ig = _sigmoid(pre_x + vec[VEC_BX:VEC_BX + 1])
    sp = _softplus(-vec[VEC_LAM:VEC_LAM + 1])
    log_a = -LRU_C * ra * sp
    a = jnp.exp(log_a)
    z = 2.0 * log_a
    one_minus_a2 = jnp.where(z > -0.01, -z * (1.0 + z * (0.5 + z * (1.0 / 6.0))), 1.0 - jnp.exp(z))
    mult = jnp.sqrt(one_minus_a2)
    return dict(xcb=xcb, ra=ra, ig=ig, sp=sp, a=a, mult=mult)


def _conv_pad_prev(pad_ref, cur, halo, first):
    pad_ref[0:SUBLANES, :] = jnp.where(first, 0.0, halo)
    pad_ref[SUBLANES:SUBLANES + cur.shape[0], :] = cur


def _shift_rows(x, s, fill, up=False):
    rid = lax.broadcasted_iota(jnp.int32, x.shape, 0)
    if up:
        return jnp.where(rid < SUBLANES - s, pltpu.roll(x, SUBLANES - s, axis=0), fill)
    return jnp.where(rid >= s, pltpu.roll(x, s, axis=0), fill)


def _lru_fwd(proj, d, cw8, vec8, wa, wx):
    s = proj.shape[0]
    hb = ROWS // SUBLANES

    def body(x_ref, halo_ref, g_ref, cw_ref, vec_ref, wa_ref, wx_ref, xc_ref, hp_ref, h2_ref, y_ref, pad, a_scr, u_scr, carry):
        @pl.when(pl.program_id(0) == 0)
        def _():
            carry[...] = jnp.zeros_like(carry)

        _conv_pad_prev(pad, x_ref[...], halo_ref[...], pl.program_id(0) == 0)
        vec = vec_ref[...]
        xc = vec[VEC_CB:VEC_CB + 1]
        for k in range(CONV_TAPS):
            xc = xc + cw_ref[k:k + 1, :] * pad[pl.ds(SUBLANES - (CONV_TAPS - 1) + k, ROWS), :]
        gm = _gate_math(xc, wa_ref, wx_ref, vec)
        xc_ref[...] = xc
        a_scr[...] = gm["a"]
        u_scr[...] = gm["mult"] * (gm["ig"] * xc)

        def group(gi, before):
            rows = pl.ds(pl.multiple_of(gi * SUBLANES, SUBLANES), SUBLANES)
            ca, cb = a_scr[rows, :], u_scr[rows, :]
            for sh in (1, 2, 4):
                cb = ca * _shift_rows(cb, sh, 0.0) + cb
                ca = ca * _shift_rows(ca, sh, 1.0)
            h = cb + ca * before
            hp_ref[rows, :] = jnp.where(lax.broadcasted_iota(jnp.int32, h.shape, 0) == 0, before, pltpu.roll(h, 1, axis=0))
            h2_ref[rows, :] = h.astype(BF16)
            y_ref[rows, :] = (h * _gelu(g_ref[rows, :])).astype(BF16)
            return jnp.broadcast_to(h[SUBLANES - 1:SUBLANES, :], h.shape)

        carry[...] = lax.fori_loop(0, ROWS // SUBLANES, group, carry[...])

    wspec = pl.BlockSpec(wa.shape, lambda i: (0, 0, 0))
    return pl.pallas_call(
        body, out_shape=[jax.ShapeDtypeStruct((s, d), F32)] * 2 + [jax.ShapeDtypeStruct((s, d), BF16)] * 2, grid=(s // ROWS,),
        in_specs=[pl.BlockSpec((ROWS, d), lambda i: (i, SEG_X)),
                  pl.BlockSpec((SUBLANES, d), lambda i: (jnp.maximum(i * hb - 1, 0), SEG_X)),
                  pl.BlockSpec((ROWS, d), lambda i: (i, SEG_GATE)),
                  _vec_spec(d, SUBLANES), _vec_spec(d, SUBLANES), wspec, wspec],
        out_specs=[_row_spec(d)] * 4,
        scratch_shapes=[pltpu.VMEM((ROWS + SUBLANES, d), F32), pltpu.VMEM((ROWS, d), F32), pltpu.VMEM((ROWS, d), F32), pltpu.VMEM((SUBLANES, d), F32)],
        compiler_params=_params(("arbitrary",)), name="lru_fwd")(proj, proj, proj, cw8, vec8, wa, wx)


def _lru_bwd(hp, dh, xc, proj, cw8, wa, wx, vec8):
    s, d = xc.shape
    nh = d // HEAD
    nb = s // ROWS
    hb = ROWS // SUBLANES

    def body(hp_ref, dh_ref, xc_ref, x_ref, xprev_ref, cw_ref, wa_ref, wx_ref, vec_ref, dx_ref, dwa_ref, dwx_ref, dvec_ref, dcw_ref,
             a_scr, g_scr, da_scr, carry, padd, padx, dnext):
        step = pl.program_id(0)

        @pl.when(step == 0)
        def _():
            carry[...] = jnp.zeros_like(carry)
            dnext[...] = jnp.zeros_like(dnext)
            dwa_ref[...] = jnp.zeros_like(dwa_ref)
            dwx_ref[...] = jnp.zeros_like(dwx_ref)
            dvec_ref[...] = jnp.zeros_like(dvec_ref)
            dcw_ref[...] = jnp.zeros_like(dcw_ref)

        xc_v, vec = xc_ref[...], vec_ref[...]
        gm = _gate_math(xc_v, wa_ref, wx_ref, vec)
        ra, ig, sp, a, mult = gm["ra"], gm["ig"], gm["sp"], gm["a"], gm["mult"]
        a_scr[...] = a

        def group(j, after):
            rows = pl.ds(pl.multiple_of((ROWS // SUBLANES - 1 - j) * SUBLANES, SUBLANES), SUBLANES)
            ca, dhv = a_scr[rows, :], dh_ref[rows, :]
            cb = ca * dhv
            for sh in (1, 2, 4):
                cb = ca * _shift_rows(cb, sh, 0.0, up=True) + cb
                ca = ca * _shift_rows(ca, sh, 1.0, up=True)
            c = cb + ca * after
            last = lax.broadcasted_iota(jnp.int32, c.shape, 0) == SUBLANES - 1
            g = dhv + jnp.where(last, after, pltpu.roll(c, SUBLANES - 1, axis=0))
            g_scr[rows, :] = g
            da_scr[rows, :] = g * hp_ref[rows, :]
            return jnp.broadcast_to(c[0:1, :], c.shape)

        carry[...] = lax.fori_loop(0, ROWS // SUBLANES, group, carry[...])
        du, da = g_scr[...], da_scr[...]
        dmult = du * ig * xc_v
        dlog_a = da * a - dmult * (a * a) / mult
        dpre_a = dlog_a * (-LRU_C * sp) * ra * (1.0 - ra)
        dpre_x = du * mult * xc_v * ig * (1.0 - ig)
        dlam = jnp.sum(dlog_a * (-LRU_C * ra), axis=0, keepdims=True) * (-_sigmoid(-vec[VEC_LAM:VEC_LAM + 1]))
        dvec_ref[VEC_BA:VEC_BA + 1, :] += jnp.sum(dpre_a, axis=0, keepdims=True)
        dvec_ref[VEC_BX:VEC_BX + 1, :] += jnp.sum(dpre_x, axis=0, keepdims=True)
        dvec_ref[VEC_LAM:VEC_LAM + 1, :] += dlam
        dab, dxb, xcb = dpre_a.astype(BF16), dpre_x.astype(BF16), gm["xcb"]
        back = []
        for h in range(nh):
            cols = slice(h * HEAD, (h + 1) * HEAD)
            dwa_ref[h] += lax.dot_general(xcb[:, cols], dab[:, cols], TN, preferred_element_type=F32)
            dwx_ref[h] += lax.dot_general(xcb[:, cols], dxb[:, cols], TN, preferred_element_type=F32)
            back.append(lax.dot_general(dab[:, cols], wa_ref[h], NT, preferred_element_type=F32)
                        + lax.dot_general(dxb[:, cols], wx_ref[h], NT, preferred_element_type=F32))
        dc = du * mult * ig + jnp.concatenate(back, axis=1)
        padd[0:ROWS, :] = dc
        padd[ROWS:ROWS + SUBLANES, :] = dnext[...]
        dnext[...] = dc[0:SUBLANES, :]
        _conv_pad_prev(padx, x_ref[...], xprev_ref[...], step == nb - 1)
        dx = jnp.zeros_like(dc)
        for k in range(CONV_TAPS):
            dx = dx + cw_ref[k:k + 1, :] * padd[pl.ds(CONV_TAPS - 1 - k, ROWS), :]
            dcw_ref[k:k + 1, :] += jnp.sum(dc * padx[pl.ds(SUBLANES - (CONV_TAPS - 1) + k, ROWS), :], axis=0, keepdims=True)
        dcw_ref[CONV_TAPS:CONV_TAPS + 1, :] += jnp.sum(dc, axis=0, keepdims=True)
        dx_ref[...] = dx.astype(BF16)

    rows_rev = pl.BlockSpec((ROWS, d), lambda i: (nb - 1 - i, 0))
    wspec = pl.BlockSpec(wa.shape, lambda i: (0, 0, 0))
    return pl.pallas_call(
        body, out_shape=[jax.ShapeDtypeStruct((s, d), BF16), jax.ShapeDtypeStruct(wa.shape, F32), jax.ShapeDtypeStruct(wa.shape, F32),
                         jax.ShapeDtypeStruct((SUBLANES, d), F32), jax.ShapeDtypeStruct((SUBLANES, d), F32)],
        grid=(nb,),
        in_specs=[rows_rev, rows_rev, rows_rev, pl.BlockSpec((ROWS, d), lambda i: (nb - 1 - i, SEG_X)),
                  pl.BlockSpec((SUBLANES, d), lambda i: (jnp.maximum((nb - 1 - i) * hb - 1, 0), SEG_X)),
                  _vec_spec(d, SUBLANES), wspec, wspec, _vec_spec(d, SUBLANES)],
        out_specs=[rows_rev, wspec, wspec, _vec_spec(d, SUBLANES), _vec_spec(d, SUBLANES)],
        scratch_shapes=[pltpu.VMEM((ROWS, d), F32)] * 3 + [pltpu.VMEM((SUBLANES, d), F32)]
        + [pltpu.VMEM((ROWS + SUBLANES, d), F32)] * 2 + [pltpu.VMEM((SUBLANES, d), F32)],
        compiler_params=_params(("arbitrary",)), name="lru_bwd")(hp, dh, xc, proj, proj, cw8, wa, wx, vec8)


def _coords():
    return lax.axis_index("x"), lax.axis_index("y"), lax.axis_index("c")


def _other_chips(x, y):
    return [(1 - x, y), (x, 1 - y), (1 - x, 1 - y)]


def _slab(ref, kind, shard_shape, idx, half=None):
    r, c = shard_shape
    r0, nr = (0, r) if half is None else (half * (r // 2), r // 2)
    if kind == "col":
        return ref.at[pl.ds(r0, nr), pl.ds(pl.multiple_of(idx * c, LANES), c)]
    if kind == "row":
        return ref.at[pl.ds(pl.multiple_of(idx * r, SUBLANES) + r0, nr), :]
    return ref.at[idx, pl.ds(r0, nr), :]


def _full_shape(shard_shape, kind):
    r, c = shard_shape
    return {"col": (r, c * N_DEV), "row": (r * N_DEV, c), "slot": (N_DEV, r, c)}[kind]


def _handshake(peers):
    barrier = pltpu.get_barrier_semaphore()
    for peer in peers:
        pl.semaphore_signal(barrier, inc=1, device_id=peer, device_id_type=MESH)
    pl.semaphore_wait(barrier, len(peers))


def _launch(name, body, out_shape, operands, sems, sequencer_id):
    if sequencer_id is None:
        return pl.pallas_call(body, out_shape=out_shape, in_specs=[HBM] * len(operands), out_specs=[HBM] * len(out_shape),
                              scratch_shapes=sems, name=name)(*operands)
    return pl.kernel(body, out_type=out_shape, mesh=plsc.ScalarSubcoreMesh(axis_name="seq", num_cores=1), name=name,
                     scratch_types=sems, compiler_params=pltpu.CompilerParams(collective_id=sequencer_id))(*operands)


AG_COPIES = 10


def _all_gather(name, shards, kinds, sequencer_id=None):
    n = len(shards)
    shapes = [s.shape for s in shards]

    def body(*refs):
        ins, outs = refs[:n], refs[n:2 * n]
        send_sems, recv_sems, local_sems = refs[2 * n:]
        x, y, c = _coords()
        me, sib, xn, yn, dg = (x, y, c), (x, y, 1 - c), (1 - x, y, c), (x, 1 - y, c), (1 - x, 1 - y, c)
        if sequencer_id is not None:
            _handshake([sib, xn, yn])

        def part(i, dev, half=None):
            return _slab(outs[i], kinds[i], shapes[i], 4 * dev[0] + 2 * dev[1] + dev[2], half)

        def copy(i, k, block, half, to, own=False):
            r = shapes[i][0]
            src = part(i, block, half) if not own else (ins[i] if half is None else ins[i].at[pl.ds(half * (r // 2), r // 2), :])
            return pltpu.make_async_remote_copy(
                src_ref=src, dst_ref=part(i, block, half), send_sem=send_sems.at[AG_COPIES * i + k],
                recv_sem=recv_sems.at[AG_COPIES * i + k], device_id=to, device_id_type=MESH)

        def other_core(dev):
            return (dev[0], dev[1], 1 - c)

        started = []

        def start(cp):
            cp.start()
            started.append(cp)

        for i in range(n):
            start(copy(i, 1, me, 0, xn, own=True))
            start(copy(i, 4, me, 1, yn, own=True))
            start(copy(i, 2, me, 1, xn, own=True))
            start(copy(i, 3, me, 0, yn, own=True))
            start(copy(i, 0, me, None, sib, own=True))
        mine = [pltpu.make_async_copy(ins[i], part(i, me), local_sems.at[i]) for i in range(n)]
        for cp in mine:
            cp.start()
        for i in range(n):
            copy(i, 1, xn, 0, me).wait_recv()
            start(copy(i, 5, xn, 0, yn))
            copy(i, 4, yn, 1, me).wait_recv()
            start(copy(i, 6, yn, 1, xn))
        for i in range(n):
            copy(i, 2, xn, 1, me).wait_recv()
            start(copy(i, 7, xn, None, sib))
            copy(i, 3, yn, 0, me).wait_recv()
            start(copy(i, 8, yn, None, sib))
        for i in range(n):
            copy(i, 5, dg, 0, me).wait_recv()
            copy(i, 6, dg, 1, me).wait_recv()
            start(copy(i, 9, dg, None, sib))
        for i in range(n):
            copy(i, 0, sib, None, me).wait_recv()
            for k, dev in ((7, xn), (8, yn), (9, dg)):
                copy(i, k, other_core(dev), None, me).wait_recv()
        for cp in started:
            cp.wait_send()
        for cp in mine:
            cp.wait()

    out_shape = [jax.ShapeDtypeStruct(_full_shape(s.shape, k), s.dtype) for s, k in zip(shards, kinds)]
    sems = [pltpu.SemaphoreType.DMA((AG_COPIES * n,)), pltpu.SemaphoreType.DMA((AG_COPIES * n,)), pltpu.SemaphoreType.DMA((n,))]
    return _launch(name, body, out_shape, shards, sems, sequencer_id)


def _sibling_copies(kinds, shard_shapes):
    def make(ins, outs, send_sems, recv_sems):
        x, y, c = _coords()
        return [pltpu.make_async_remote_copy(
            src_ref=_slab(ins[i], kinds[i], shard_shapes[i], 2 * q + (1 - c)), dst_ref=outs[i].at[q],
            send_sem=send_sems.at[N_CHIP * i + q], recv_sem=recv_sems.at[N_CHIP * i + q],
            device_id=(x, y, 1 - c), device_id_type=MESH) for i in range(len(ins)) for q in range(N_CHIP)]
    return make


def _sibling_side(partials, kinds, shard_shapes):
    return (partials, [jax.ShapeDtypeStruct((N_CHIP, *s), BF16) for s in shard_shapes], N_CHIP * len(partials),
            _sibling_copies(kinds, shard_shapes))


def _exchange_chips(name, chip_sums, sequencer_id=None):
    n = len(chip_sums)

    def body(*refs):
        ins, outs = refs[:n], refs[n:2 * n]
        send_sems, recv_sems = refs[2 * n:]
        x, y, c = _coords()
        if sequencer_id is not None:
            _handshake([(cx, cy, c) for cx, cy in _other_chips(x, y)])
        cps = []
        for i in range(n):
            for k, (cx, cy) in enumerate(_other_chips(x, y)):
                cps.append(pltpu.make_async_remote_copy(
                    src_ref=ins[i].at[2 * cx + cy], dst_ref=outs[i].at[k], send_sem=send_sems.at[3 * i + k],
                    recv_sem=recv_sems.at[3 * i + k], device_id=(cx, cy, c), device_id_type=MESH))
        for cp in cps:
            cp.start()
        for cp in cps:
            cp.wait()

    return _launch(name, body, [jax.ShapeDtypeStruct((3, *t.shape[1:]), BF16) for t in chip_sums], chip_sums,
                   [pltpu.SemaphoreType.DMA((3 * n,)), pltpu.SemaphoreType.DMA((3 * n,))], sequencer_id)


def _all_peers(x, y, c):
    return [(x ^ (k >> 2), y ^ ((k >> 1) & 1), c ^ (k & 1)) for k in range(1, N_DEV)]


def _small_scatter(name, packed, sequencer_id):
    rows = packed.shape[0] // N_DEV

    def body(p_ref, rb_ref, send_sems, recv_sems, local_sem):
        x, y, c = _coords()
        me = 4 * x + 2 * y + c
        peers = _all_peers(x, y, c)
        _handshake(peers)

        def piece(idx):
            return p_ref.at[pl.ds(pl.multiple_of(idx * rows, SUBLANES), rows), :]

        cps = [pltpu.make_async_remote_copy(src_ref=piece(me ^ k), dst_ref=rb_ref.at[k], send_sem=send_sems.at[k], recv_sem=recv_sems.at[k],
                                            device_id=peers[k - 1], device_id_type=MESH) for k in range(1, N_DEV)]
        for cp in cps:
            cp.start()
        mine = pltpu.make_async_copy(piece(me), rb_ref.at[0], local_sem)
        mine.start()
        for cp in cps:
            cp.wait()
        mine.wait()

    return _launch(name, body, [jax.ShapeDtypeStruct((N_DEV, rows, LANES), F32)], [packed],
                   [pltpu.SemaphoreType.DMA((N_DEV,)), pltpu.SemaphoreType.DMA((N_DEV,)), pltpu.SemaphoreType.DMA], sequencer_id)[0]


def _small_sum(name, pieces):
    def body(p_ref, o_ref):
        acc = p_ref[0]
        for k in range(1, N_DEV):
            acc = acc + p_ref[k]
        o_ref[...] = acc

    vm = pl.BlockSpec(memory_space=pltpu.VMEM)
    return pl.pallas_call(body, out_shape=jax.ShapeDtypeStruct(pieces.shape[1:], F32), in_specs=[vm], out_specs=vm, name=name)(pieces)


def _small_gather(name, tot, sequencer_id):
    rows = tot.shape[0]

    def body(t_ref, out_ref, send_sems, recv_sems, local_sem):
        x, y, c = _coords()
        me = 4 * x + 2 * y + c
        peers = _all_peers(x, y, c)
        _handshake(peers)

        def piece(idx):
            return out_ref.at[pl.ds(pl.multiple_of(idx * rows, SUBLANES), rows), :]

        cps = [pltpu.make_async_remote_copy(src_ref=t_ref, dst_ref=piece(me), send_sem=send_sems.at[k], recv_sem=recv_sems.at[k],
                                            device_id=peers[k - 1], device_id_type=MESH) for k in range(1, N_DEV)]
        for cp in cps:
            cp.start()
        mine = pltpu.make_async_copy(t_ref, piece(me), local_sem)
        mine.start()
        for cp in cps:
            cp.wait()
        mine.wait()

    return _launch(name, body, [jax.ShapeDtypeStruct((N_DEV * rows, LANES), F32)], [tot],
                   [pltpu.SemaphoreType.DMA((N_DEV,)), pltpu.SemaphoreType.DMA((N_DEV,)), pltpu.SemaphoreType.DMA], sequencer_id)[0]


def _all_reduce_small(name, packed):
    rows = packed.shape[0] // N_DEV

    def body(p_ref, out_ref, rb, tot, send_sems, recv_sems):
        x, y, c = _coords()
        me = 4 * x + 2 * y + c

        def peer(k):
            return (x ^ (k >> 2), y ^ ((k >> 1) & 1), c ^ (k & 1))

        def rows_of(idx):
            return pl.ds(pl.multiple_of(idx * rows, SUBLANES), rows)

        def piece(ref, idx):
            return ref.at[rows_of(idx), :]

        scatter = [pltpu.make_async_remote_copy(src_ref=piece(p_ref, me ^ k), dst_ref=rb.at[k], send_sem=send_sems.at[k],
                                                recv_sem=recv_sems.at[k], device_id=peer(k), device_id_type=MESH) for k in range(1, N_DEV)]
        for cp in scatter:
            cp.start()
        acc = p_ref[rows_of(me), :]
        for cp in scatter:
            cp.wait_recv()
        for k in range(1, N_DEV):
            acc = acc + rb[k]
        tot[...] = acc
        out_ref[rows_of(me), :] = acc
        gather = [pltpu.make_async_remote_copy(src_ref=tot, dst_ref=piece(out_ref, me), send_sem=send_sems.at[N_DEV + k],
                                               recv_sem=recv_sems.at[N_DEV + k], device_id=peer(k), device_id_type=MESH)
                  for k in range(1, N_DEV)]
        for cp in gather:
            cp.start()
        for k in range(1, N_DEV):
            pltpu.make_async_remote_copy(src_ref=tot, dst_ref=piece(out_ref, me ^ k), send_sem=send_sems.at[N_DEV + k],
                                         recv_sem=recv_sems.at[N_DEV + k], device_id=peer(k), device_id_type=MESH).wait_recv()
        for cp in scatter + gather:
            cp.wait_send()

    vm = pl.BlockSpec(memory_space=pltpu.VMEM)
    return pl.pallas_call(
        body, out_shape=jax.ShapeDtypeStruct(packed.shape, F32), in_specs=[vm], out_specs=vm,
        scratch_shapes=[pltpu.VMEM((N_DEV, rows, LANES), F32), pltpu.VMEM((rows, LANES), F32),
                        pltpu.SemaphoreType.DMA((2 * N_DEV,)), pltpu.SemaphoreType.DMA((2 * N_DEV,))],
        compiler_params=pltpu.CompilerParams(vmem_limit_bytes=VMEM_LIMIT), name=name)(packed)


def _adamw_math(g, w, m, v):
    m = ADAM_B1 * m + (1.0 - ADAM_B1) * g
    v = ADAM_B2 * v + (1.0 - ADAM_B2) * (g * g)
    delta = -ADAM_LR * ((m / ADAM_C1) / (jnp.sqrt(v / ADAM_C2) + ADAM_EPS) + ADAM_WD * w)
    return delta, m, v


def _slab_spec(kind, shard_shape, tr, slab_of):
    r, c = shard_shape
    if kind == "col":
        return pl.BlockSpec((tr, c), lambda q, i, sc: (i, slab_of(q, sc)))
    return pl.BlockSpec((tr, c), lambda q, i, sc: (slab_of(q, sc) * (r // tr) + i, 0))


def _chip_sum(name, partial, recv, kind, shard_shape, core):
    r, c = shard_shape
    tr = _blk(r, 1024)

    def body(core_ref, p_ref, r_ref, o_ref):
        o_ref[...] = (p_ref[...].astype(F32) + r_ref[...].astype(F32)).astype(BF16)

    spec4 = pl.BlockSpec((None, tr, c), lambda q, i, sc: (q, i, 0))
    grid_spec = pltpu.PrefetchScalarGridSpec(
        num_scalar_prefetch=1, grid=(N_CHIP, r // tr),
        in_specs=[_slab_spec(kind, shard_shape, tr, lambda q, sc: 2 * q + sc[0]), spec4], out_specs=spec4)
    return pl.pallas_call(body, out_shape=jax.ShapeDtypeStruct((N_CHIP, r, c), BF16), grid_spec=grid_spec,
                          compiler_params=_params(("parallel", "parallel")), name=name)(core, partial, recv)


def _adamw_shard(name, parts, w, m, v, chip):
    r, c = w.shape
    n_parts = len(parts)
    tr = _blk(r // n_parts, 256)
    per = r // n_parts // tr

    def body(chip_ref, *refs):
        src, (w_ref, m_ref, v_ref), (g_out, d_out, m_out, v_out) = refs[:2 * n_parts], refs[2 * n_parts:2 * n_parts + 3], refs[2 * n_parts + 3:]
        for p in range(n_parts):
            @pl.when(pl.program_id(0) // per == p)
            def _():
                g = src[2 * p][...].astype(F32)
                for k in range(3):
                    g = g + src[2 * p + 1][k].astype(F32)
                g_out[...] = g
                d_out[...], m_out[...], v_out[...] = _adamw_math(g, w_ref[...], m_ref[...], v_ref[...])

    def part_specs(p):
        at = lambda i: jnp.clip(i - p * per, 0, per - 1)
        return [pl.BlockSpec((None, tr, c), lambda i, sc: (sc[0], at(i), 0)), pl.BlockSpec((3, tr, c), lambda i, sc: (0, at(i), 0))]

    blk = pl.BlockSpec((tr, c), lambda i, sc: (i, 0))
    grid_spec = pltpu.PrefetchScalarGridSpec(
        num_scalar_prefetch=1, grid=(r // tr,), in_specs=[s for p in range(n_parts) for s in part_specs(p)] + [blk, blk, blk], out_specs=[blk] * 4)
    return pl.pallas_call(body, out_shape=[jax.ShapeDtypeStruct((r, c), F32)] * 4, grid_spec=grid_spec,
                          compiler_params=_params(("parallel",)), name=name)(chip, *[a for p in parts for a in p], w, m, v)


def _adamw_small(name, g, w, m, v):
    def body(g_ref, w_ref, m_ref, v_ref, d_out, m_out, v_out):
        d_out[...], m_out[...], v_out[...] = _adamw_math(g_ref[...], w_ref[...], m_ref[...], v_ref[...])

    vm = pl.BlockSpec(memory_space=pltpu.VMEM)
    return pl.pallas_call(body, out_shape=[jax.ShapeDtypeStruct(g.shape, F32)] * 3, in_specs=[vm] * 4, out_specs=[vm] * 3,
                          compiler_params=pltpu.CompilerParams(vmem_limit_bytes=VMEM_LIMIT), name=name)(g, w, m, v)


def _pack_rows(arrays, total_rows):
    flat = [a.reshape(-1, LANES) for a in arrays]
    used = sum(f.shape[0] for f in flat)
    return jnp.concatenate(flat + [jnp.zeros((total_rows - used, LANES), F32)], axis=0)


def _unpack_rows(packed, like):
    out, at = [], 0
    for a in like:
        n = a.size // LANES
        out.append(packed[at:at + n].reshape(a.shape))
        at += n
    return out


def kernel(x, norm_mix_g, w_in, conv_w, conv_b, lru_wa, lru_ba, lru_wx, lru_bx, lru_lambda, w_proj_attn, w_proj_lru, w_out, norm_mlp_g, w_up, w_down, norm_final_g, loss_target, m_norm_mix_g, m_w_in, m_conv_w, m_conv_b, m_lru_wa, m_lru_ba, m_lru_wx, m_lru_bx, m_lru_lambda, m_w_proj_attn, m_w_proj_lru, m_w_out, m_norm_mlp_g, m_w_up, m_w_down, m_norm_final_g, v_norm_mix_g, v_w_in, v_conv_w, v_conv_b, v_lru_wa, v_lru_ba, v_lru_wx, v_lru_bx, v_lru_lambda, v_w_proj_attn, v_w_proj_lru, v_w_out, v_norm_mlp_g, v_w_up, v_w_down, v_norm_final_g):
    xs, tgt = x[0], loss_target[0]
    s, d = xs.shape
    nh = d // HEAD
    ix, iy, ic = _coords()
    core = jnp.reshape(ic, (1,)).astype(jnp.int32)
    chip = jnp.reshape(2 * ix + iy, (1,)).astype(jnp.int32)
    dev = 4 * ix + 2 * iy + ic

    big = [w_in[0], w_proj_attn[0], w_proj_lru[0], w_out[0], w_up[0], w_down[0]]
    big_m = [m_w_in[0], m_w_proj_attn[0], m_w_proj_lru[0], m_w_out[0], m_w_up[0], m_w_down[0]]
    big_v = [v_w_in[0], v_w_proj_attn[0], v_w_proj_lru[0], v_w_out[0], v_w_up[0], v_w_down[0]]
    kinds = ["col", "row", "row", "row", "col", "row"]
    pad_taps = lambda t: jnp.pad(t, ((0, SUBLANES - CONV_TAPS), (0, 0)))
    shards = [w.astype(BF16) for w in big]
    pad_taps2 = lambda t: jnp.pad(t, ((0, 2 * SUBLANES - CONV_TAPS), (0, 0)))
    win, cw_slots = _all_gather("all_gather_w_in", [shards[0], pad_taps2(conv_w[0])], ["col", "slot"], sequencer_id=7)
    later = lax.optimization_barrier((shards[1:], win))[0]
    wpa, wpl, wout = _all_gather("all_gather_mix", later[:3], kinds[1:4], sequencer_id=1)
    wup, wdown = _all_gather("all_gather_mlp", later[3:], kinds[4:], sequencer_id=5)
    cw8 = jnp.transpose(cw_slots[:, :SUBLANES], (1, 0, 2)).reshape(SUBLANES, d)
    row_id = lax.broadcasted_iota(jnp.int32, (SUBLANES, d), 0)
    vec8 = sum(jnp.where(row_id == k, t, 0.0) for k, t in ((VEC_CB, conv_b), (VEC_BA, lru_ba), (VEC_BX, lru_bx), (VEC_LAM, lru_lambda)))
    wa16, wx16 = lru_wa[0].astype(BF16), lru_wx[0].astype(BF16)
    slopes = 2.0 ** (-8.0 * jnp.arange(1, nh + 1, dtype=F32) / nh)

    def seg_specs(*segs):
        return lambda bm, bn: [pl.BlockSpec((bm, bn), (lambda i, j, kk, sg=sg: (i, sg * (d // bn) + j))) for sg in segs]

    def plain_specs(k):
        return lambda bm, bn: [pl.BlockSpec((bm, bn), lambda i, j, kk: (i, j)) for _ in range(k)]

    xn = _rms_fwd("norm_mix", xs, norm_mix_g)
    proj = _mm_fwd("proj_in", xn, win, 0, 7 * d, [F32], bm=2048)[0]
    att, lse = _attn_fwd(proj, d, slopes)
    xc, hp, h2d, ylru = _lru_fwd(proj, d, cw8, vec8, wa16, wx16)
    pa = _mm_fwd("proj_attn", att, wpa, 0, d, [BF16], bm=2048)[0]

    def merge(acc, pa_b, ga, gl):
        return acc, _sigmoid(ga) * pa_b.astype(F32) + _sigmoid(gl) * acc

    plr, merged = _mm_fwd("proj_lru_merge", ylru, wpl, 0, d, [BF16, BF16], merge, (pa, proj, proj),
                          lambda bm, bn: plain_specs(1)(bm, bn) + seg_specs(SEG_GA, SEG_GL)(bm, bn), bn=512)
    h1 = _mm_fwd("mix_out", merged, wout, 0, d, [F32], lambda acc, r: (acc + r,), (xs,), plain_specs(1))[0]
    hn = _rms_fwd("norm_mlp", h1, norm_mlp_g)

    def relu2(acc):
        return acc, jnp.square(jnp.maximum(acc, 0.0))

    up, hid = _mm_fwd("mlp_up", hn, wup, 0, wup.shape[1], [BF16, BF16], relu2, bm=2048)
    h2 = _mm_fwd("mlp_down", hid, wdown, 0, d, [F32], lambda acc, r: (acc + r,), (h1,), plain_specs(1))[0]
    dh2, dh2b, dg3, loss_lanes = _final_loss(h2, tgt, norm_final_g.reshape(1, d))
    loss = lax.psum(0.5 / d * jnp.sum(loss_lanes), ("x", "y", "c"))
    dh2b = lax.optimization_barrier((dh2b, loss))[0]

    def reduce_group(tag, kk, shp, partials, from_sibling, sequencer_id):
        sums = [_chip_sum(f"chip_sum_{tag}_{i}", p, f, k, sh, core) for i, (p, f, k, sh) in enumerate(zip(partials, from_sibling, kk, shp))]
        return list(zip(sums, _exchange_chips(f"rs_chips_{tag}", sums, sequencer_id)))

    g_wdown = _mm_tn("mlp_down_dw", hid, dh2b)
    (dup,), sib_down = _mm_nt("mlp_down_dx", dh2b, wdown, [BF16], lambda acc, u: (acc * (2.0 * jnp.maximum(u.astype(F32), 0.0)),), (up,),
                              plain_specs(1), side=_sibling_side([g_wdown], kinds[5:], [big[5].shape]))
    (red_down,) = reduce_group("mlp_down", kinds[5:], [big[5].shape], [g_wdown], sib_down, 2)
    dup = lax.optimization_barrier((dup, red_down[0]))[0]
    g_wup = _mm_tn("mlp_up_dw", hn, dup)
    (dhn,), sib_up = _mm_nt("mlp_up_dx", dup, wup, [F32], side=_sibling_side([g_wup], kinds[4:5], [big[4].shape]))
    (red_up,) = reduce_group("mlp_up", kinds[4:5], [big[4].shape], [g_wup], sib_up, 10)
    dhn = lax.optimization_barrier((dhn, red_up[0]))[0]
    dh1, dh1b, dg2 = _rms_bwd("norm_mlp_bwd", h1, norm_mlp_g, dhn, dh2)

    def merge_bwd(acc, pa_b, pl_b, ga, gl):
        sa, sl = _sigmoid(ga), _sigmoid(gl)
        return acc * sa, acc * sl, acc * pa_b.astype(F32) * sa * (1.0 - sa), acc * pl_b.astype(F32) * sl * (1.0 - sl)

    dpa, dpl, dga, dgl = _mm_nt("mix_out_dx", dh1b, wout, [BF16] * 4, merge_bwd, (pa, plr, proj, proj),
                                lambda bm, bn: plain_specs(2)(bm, bn) + seg_specs(SEG_GA, SEG_GL)(bm, bn), bn=512)
    g_wout = _mm_tn("mix_out_dw", merged, dh1b)
    datt = _mm_nt("proj_attn_dx", dpa, wpa, [F32], bm=2048)[0]
    g_wpa = _mm_tn("proj_attn_dw", att, dpa)

    def lru_out_bwd(acc, h_b, gate):
        return acc * _gelu(gate), acc * h_b.astype(F32) * _gelu_grad(gate)

    g_wpl = _mm_tn("proj_lru_dw", ylru, dpl)
    shp_mix = [w.shape for w in big[1:4]]
    (dh, dxg), sib_mix = _mm_nt("proj_lru_dx", dpl, wpl, [F32, BF16], lru_out_bwd, (h2d, proj),
                                lambda bm, bn: plain_specs(1)(bm, bn) + seg_specs(SEG_GATE)(bm, bn), bn=512,
                                side=_sibling_side([g_wpa, g_wpl, g_wout], kinds[1:4], shp_mix))
    red_pa, red_pl, red_out = reduce_group("mix", kinds[1:4], shp_mix, [g_wpa, g_wpl, g_wout], sib_mix, 3)
    dh = lax.optimization_barrier((dh, red_down[1]))[0]
    dxr, dwa, dwx, dvec, dconv = _lru_bwd(hp, dh, xc, proj, cw8, wa16, wx16, vec8)
    dproj = _attn_bwd(proj, d, datt, att, lse, slopes, (dxr, dxg, dga, dgl))

    def small_step(tag, grads, ws, ms, vs, like, after, seq=None):
        n_rows = sum(g.size for g in grads) // LANES
        per_dev = -(-n_rows // (N_DEV * SUBLANES)) * SUBLANES
        packed = lax.optimization_barrier((_pack_rows(grads, N_DEV * per_dev), after))[0]
        if seq is None:
            total = _all_reduce_small(f"all_reduce_{tag}", packed)
        else:
            pieces = lax.optimization_barrier((_small_scatter(f"scatter_{tag}", packed, seq[0]), seq[2]))[0]
            total = _small_gather(f"gather_{tag}", _small_sum(f"sum_{tag}", pieces), seq[1])
        w_rows = -(-(sum(w.size for w in ws) // LANES) // SUBLANES) * SUBLANES
        upd = _adamw_small(f"adamw_{tag}", total[:w_rows], _pack_rows(ws, w_rows), _pack_rows(ms, w_rows), _pack_rows(vs, w_rows))
        return _unpack_rows(total, like), [_unpack_rows(t, ws) for t in upd]

    early_w = [conv_b, lru_wa, lru_ba, lru_wx, lru_bx, lru_lambda, norm_mlp_g, norm_final_g]
    early_m = [m_conv_b, m_lru_wa, m_lru_ba, m_lru_wx, m_lru_bx, m_lru_lambda, m_norm_mlp_g, m_norm_final_g]
    early_v = [v_conv_b, v_lru_wa, v_lru_ba, v_lru_wx, v_lru_bx, v_lru_lambda, v_norm_mlp_g, v_norm_final_g]
    early_g = [dconv[CONV_TAPS:CONV_TAPS + 1], dwa, dvec[VEC_BA:VEC_BA + 1], dwx, dvec[VEC_BX:VEC_BX + 1],
               dvec[VEC_LAM:VEC_LAM + 1], dg2, dg3, dconv[0:CONV_TAPS]]
    dproj = lax.optimization_barrier((dproj, red_up[1]))[0]
    dproj = lax.optimization_barrier((dproj, red_pa[1], red_pl[1], red_out[1]))[0]
    early_sum, early_upd = small_step("small", early_g, early_w, early_m, early_v,
                                      early_w + [jax.ShapeDtypeStruct((1, CONV_TAPS, d), F32)], dxr, seq=(8, 9, dproj))
    g_cw_full = early_sum[-1]
    cshard = conv_w.shape[2]
    g_cw = lax.dynamic_slice(g_cw_full, (0, 0, dev * cshard), (1, CONV_TAPS, cshard))
    cw_delta, cw_m, cw_v = (t[:CONV_TAPS][None] for t in _adamw_small(
        "adamw_conv_w", pad_taps(g_cw[0]), pad_taps(conv_w[0]), pad_taps(m_conv_w[0]), pad_taps(v_conv_w[0])))
    half = (big[0].shape[0] // 2, big[0].shape[1])
    g_in0 = _mm_tn("proj_in_dw_0", xn, dproj, part=(0, 2))
    g_in1, sib_in0 = _mm_tn("proj_in_dw_1", xn, dproj, part=(1, 2), side=_sibling_side([g_in0], ["col"], [half]))
    red_in = reduce_group("in_0", ["col"], [half], [g_in0], sib_in0, 4)
    dproj = lax.optimization_barrier((dproj, red_in[0][0], early_sum))[0]
    (dxn0,), sib_in1 = _mm_nt("proj_in_dx_0", dproj, win, [F32], part=(0, 2), side=_sibling_side([g_in1], ["col"], [half]))
    red_in += reduce_group("in_1", ["col"], [half], [g_in1], sib_in1, 6)
    dproj = lax.optimization_barrier((dproj, red_in[1][0]))[0]
    dxn1 = _mm_nt("proj_in_dx_1", dproj, win, [F32], part=(1, 2))[0]
    dxn = jnp.concatenate([dxn0, dxn1], axis=0)
    dxn = lax.optimization_barrier((dxn, red_in[0][1]))[0]
    grad_x, _, dg1 = _rms_bwd("norm_mix_bwd", xs, norm_mix_g, dxn, dh1)
    red_up, red_down = lax.optimization_barrier(((red_up, red_down), dg1))[0]
    big_out = {i: _adamw_shard(f"adamw_{i}", [red], big[i], big_m[i], big_v[i], chip) for i, red in ((4, red_up), (5, red_down))}
    big_out.update({i: _adamw_shard(f"adamw_{i}", [red], big[i], big_m[i], big_v[i], chip) for i, red in ((1, red_pa), (2, red_pl), (3, red_out))})
    late_sum, late_upd = small_step("norm_mix", [dg1], [norm_mix_g], [m_norm_mix_g], [v_norm_mix_g], [norm_mix_g], (big_out[4], big_out[5]))
    big_out[0] = _adamw_shard("adamw_0", red_in, big[0], big_m[0], big_v[0], chip)
    s_grad = late_sum + early_sum[:-1]
    s_delta, s_m, s_v = (late_upd[j] + early_upd[j] for j in range(3))


    names = ["norm_mix_g", "w_in", "conv_w", "conv_b", "lru_wa", "lru_ba", "lru_wx", "lru_bx", "lru_lambda", "w_proj_attn", "w_proj_lru",
             "w_out", "norm_mlp_g", "w_up", "w_down", "norm_final_g"]
    small_names = ["norm_mix_g", "conv_b", "lru_wa", "lru_ba", "lru_wx", "lru_bx", "lru_lambda", "norm_mlp_g", "norm_final_g"]
    big_names = ["w_in", "w_proj_attn", "w_proj_lru", "w_out", "w_up", "w_down"]
    res = {"conv_w": (g_cw, cw_delta, cw_m, cw_v)}
    for i, nm in enumerate(small_names):
        res[nm] = (s_grad[i], s_delta[i], s_m[i], s_v[i])
    for i, nm in enumerate(big_names):
        res[nm] = tuple(t[None] for t in big_out[i])
    return (loss, grad_x[None], *[res[nm][0] for nm in names], *[res[nm][1] for nm in names],
            *[res[nm][2] for nm in names], *[res[nm][3] for nm in names])
```

```python
import jax
import jax.numpy as jnp
from jax import lax
from jax.experimental import pallas as pl
from jax.experimental.pallas import tpu as pltpu
from jax.experimental.pallas import tpu_sc as plsc

F32, BF16 = jnp.float32, jnp.bfloat16
MESH = pl.DeviceIdType.MESH
HBM = pl.BlockSpec(memory_space=pltpu.HBM)
N_DEV = 8
N_CHIP = 4
HEAD = 128
SPAN = 128
DILATIONS = (1, 4, 16)
CONV_TAPS = 4
LRU_C = 8.0
NORM_EPS = 1e-6
LANES = 128
SUBLANES = 8
VMEM_LIMIT = 56 * 1024 * 1024
ADAM_LR, ADAM_B1, ADAM_B2, ADAM_EPS, ADAM_WD, ADAM_STEP = 0.001, 0.9, 0.999, 1e-08, 0.01, 10
ADAM_C1 = 1.0 - ADAM_B1 ** ADAM_STEP
ADAM_C2 = 1.0 - ADAM_B2 ** ADAM_STEP
NEG = -1e30


def _params(sem=None):
    return pltpu.CompilerParams(dimension_semantics=sem, vmem_limit_bytes=VMEM_LIMIT)


def _sigmoid(v):
    return 1.0 / (1.0 + jnp.exp(-v))


def _gelu(v):
    k = 0.7978845608028654
    return 0.5 * v * (1.0 + jnp.tanh(k * (v + 0.044715 * v * v * v)))


def _gelu_grad(v):
    k = 0.7978845608028654
    t = jnp.tanh(k * (v + 0.044715 * v * v * v))
    return 0.5 * (1.0 + t) + 0.5 * v * (1.0 - t * t) * k * (1.0 + 3.0 * 0.044715 * v * v)


NN = (((1,), (0,)), ((), ()))
NT = (((1,), (1,)), ((), ()))
TN = (((0,), (0,)), ((), ()))


def _mm(name, a, a_spec, b, b_spec, dn, grid, out_shapes, out_specs, acc_block, epilogue=None, extras=(), extra_specs=(), side=None):
    nk, ne, no = grid[2], len(extras), len(out_shapes)
    side_ops, side_shapes, side_copies, make_copies = side if side is not None else ((), (), 0, None)
    ns_in, ns_out = len(side_ops), len(side_shapes)

    def body(*refs):
        a_ref, b_ref = refs[0], refs[1]
        ex, side_in = refs[2:2 + ne], refs[2 + ne:2 + ne + ns_in]
        outs = refs[2 + ne + ns_in:2 + ne + ns_in + no]
        side_out = refs[2 + ne + ns_in + no:2 + ne + ns_in + no + ns_out]
        scratch = refs[2 + ne + ns_in + no + ns_out:]
        at = [pl.program_id(ax) for ax in range(3)]
        if side is not None:
            @pl.when((at[0] == 0) & (at[1] == 0) & (at[2] == 0))
            def _():
                for cp in make_copies(side_in, side_out, scratch[-2], scratch[-1]):
                    cp.start()

        part = lax.dot_general(a_ref[...], b_ref[...], dn, preferred_element_type=F32)

        def finish(acc):
            vals = epilogue(acc, *[e[...] for e in ex]) if epilogue is not None else (acc,)
            for o, v in zip(outs, vals):
                o[...] = v.astype(o.dtype)

        if nk == 1:
            finish(part)
        else:
            acc_ref, k = scratch[0], at[2]

            @pl.when(k == 0)
            def _():
                acc_ref[...] = part

            @pl.when(k > 0)
            def _():
                acc_ref[...] += part

            @pl.when(k == nk - 1)
            def _():
                finish(acc_ref[...])

        if side is not None:
            @pl.when((at[0] == grid[0] - 1) & (at[1] == grid[1] - 1) & (at[2] == grid[2] - 1))
            def _():
                for cp in make_copies(side_in, side_out, scratch[-2], scratch[-1]):
                    cp.wait()

    scratch_shapes = [pltpu.VMEM(acc_block, F32)] if nk > 1 else []
    if side is not None:
        scratch_shapes += [pltpu.SemaphoreType.DMA((side_copies,)), pltpu.SemaphoreType.DMA((side_copies,))]
    res = pl.pallas_call(
        body, out_shape=[*out_shapes, *side_shapes], grid=grid, in_specs=[a_spec, b_spec, *extra_specs, *[HBM] * ns_in],
        out_specs=[*out_specs, *[HBM] * ns_out], scratch_shapes=scratch_shapes,
        compiler_params=_params(("arbitrary",) * 3 if side is not None else ("parallel", "parallel", "arbitrary")),
        name=name)(a, b, *extras, *side_ops)
    return res if side is None else (res[:no], res[no:])


def _blk(n, pref):
    return pref if n % pref == 0 else n


def _kblk(k):
    return k if k <= 2048 else next(b for b in (2048, 1024, 512) if k % b == 0)


def _mm_fwd(name, a, w, col0, ncols, out_dtypes, epilogue=None, extras=(), extra_specs_fn=None, bm=1024, bn=1024):
    m, k = a.shape
    bm, bn = _blk(m, bm), _blk(ncols, bn)
    bk = _kblk(k)
    nk = k // bk
    cb0 = col0 // bn
    grid = (m // bm, ncols // bn, nk)
    a_spec = pl.BlockSpec((bm, bk), lambda i, j, kk: (i, kk))
    b_spec = pl.BlockSpec((bk, bn), lambda i, j, kk: (kk, cb0 + j))
    shapes = [jax.ShapeDtypeStruct((m, ncols), dt) for dt in out_dtypes]
    specs = [pl.BlockSpec((bm, bn), lambda i, j, kk: (i, j)) for _ in out_dtypes]
    ex_specs = extra_specs_fn(bm, bn) if extra_specs_fn else ()
    return _mm(name, a, a_spec, w, b_spec, NN, grid, shapes, specs, (bm, bn), epilogue, extras, ex_specs)


def _mm_nt(name, a, w, out_dtypes, epilogue=None, extras=(), extra_specs_fn=None, part=(0, 1), side=None, bm=1024, bn=1024):
    n = w.shape[0]
    if a.ndim == 3:
        seg_cols, m, k = a.shape[2], a.shape[1], a.shape[0] * a.shape[2]
    else:
        m, k = a.shape
    m = m // part[1]
    bm, bn = _blk(m, bm), _blk(n, bn)
    bk = _kblk(k)
    grid = (m // bm, n // bn, k // bk)
    i0 = part[0] * (m // bm)
    if a.ndim == 3:
        per = seg_cols // bk
        a_spec = pl.BlockSpec((None, bm, bk), lambda i, j, kk: (kk // per, i0 + i, kk % per))
    else:
        a_spec = pl.BlockSpec((bm, bk), lambda i, j, kk: (i0 + i, kk))
    b_spec = pl.BlockSpec((bn, bk), lambda i, j, kk: (j, kk))
    shapes = [jax.ShapeDtypeStruct((m, n), dt) for dt in out_dtypes]
    specs = [pl.BlockSpec((bm, bn), lambda i, j, kk: (i, j)) for _ in out_dtypes]
    ex_specs = extra_specs_fn(bm, bn) if extra_specs_fn else ()
    return _mm(name, a, a_spec, w, b_spec, NT, grid, shapes, specs, (bm, bn), epilogue, extras, ex_specs, side)


def _mm_tn(name, a, b, part=(0, 1), side=None, bm=1024, bn=2048):
    t, m = a.shape
    n = b.shape[1] if b.ndim == 2 else b.shape[0] * b.shape[2]
    m = m // part[1]
    bm, bn = _blk(m, bm), _blk(n, bn)
    grid = (m // bm, n // bn, 1)
    i0 = part[0] * (m // bm)
    a_spec = pl.BlockSpec((t, bm), lambda i, j, kk: (0, i0 + i))
    if b.ndim == 3:
        per = b.shape[2] // bn
        b_spec = pl.BlockSpec((None, t, bn), lambda i, j, kk: (j // per, 0, j % per))
    else:
        b_spec = pl.BlockSpec((t, bn), lambda i, j, kk: (0, j))
    res = _mm(name, a, a_spec, b, b_spec, TN, grid, [jax.ShapeDtypeStruct((m, n), BF16)],
              [pl.BlockSpec((bm, bn), lambda i, j, kk: (i, j))], (bm, bn), side=side)
    return res[0] if side is None else (res[0][0], res[1])


ROWS = 256


def _row_spec(d):
    return pl.BlockSpec((ROWS, d), lambda i: (i, 0))


def _vec_spec(d, rows=1):
    return pl.BlockSpec((rows, d), lambda i: (0, 0))


def _rms_fwd(name, x, g):
    s, d = x.shape

    def body(x_ref, g_ref, o_ref):
        xv = x_ref[...]
        r = lax.rsqrt(jnp.mean(xv * xv, axis=-1, keepdims=True) + NORM_EPS)
        o_ref[...] = (xv * r * g_ref[...]).astype(BF16)

    return pl.pallas_call(body, out_shape=jax.ShapeDtypeStruct((s, d), BF16), grid=(s // ROWS,),
                          in_specs=[_row_spec(d), _vec_spec(d)], out_specs=_row_spec(d),
                          compiler_params=_params(("parallel",)), name=name)(x, g)


def _rms_bwd_math(xv, g, dy):
    r = lax.rsqrt(jnp.mean(xv * xv, axis=-1, keepdims=True) + NORM_EPS)
    n = xv * r
    z = dy * g
    dx = r * (z - n * jnp.mean(z * n, axis=-1, keepdims=True))
    return dx, jnp.sum(dy * n, axis=0, keepdims=True)


def _rms_bwd(name, x, g, dy, resid):
    s, d = x.shape

    def body(x_ref, g_ref, dy_ref, r_ref, dx_ref, dxb_ref, dg_ref):
        dx, dg = _rms_bwd_math(x_ref[...], g_ref[...], dy_ref[...])
        dx = dx + r_ref[...]
        dx_ref[...] = dx
        dxb_ref[...] = dx.astype(BF16)

        @pl.when(pl.program_id(0) == 0)
        def _():
            dg_ref[...] = jnp.zeros_like(dg_ref)

        dg_ref[...] += dg

    return pl.pallas_call(
        body, out_shape=[jax.ShapeDtypeStruct((s, d), F32), jax.ShapeDtypeStruct((s, d), BF16), jax.ShapeDtypeStruct((1, d), F32)],
        grid=(s // ROWS,), in_specs=[_row_spec(d), _vec_spec(d), _row_spec(d), _row_spec(d)],
        out_specs=[_row_spec(d), _row_spec(d), _vec_spec(d)], compiler_params=_params(("arbitrary",)), name=name)(x, g, dy, resid)


def _final_loss(h2, tgt, g):
    s, d = h2.shape

    def body(x_ref, t_ref, g_ref, dx_ref, dxb_ref, dg_ref, ls_ref):
        xv, gv = x_ref[...], g_ref[...]
        r = lax.rsqrt(jnp.mean(xv * xv, axis=-1, keepdims=True) + NORM_EPS)
        diff = xv * r * gv - t_ref[...]
        dx, dg = _rms_bwd_math(xv, gv, diff * (1.0 / d))
        dx_ref[...] = dx
        dxb_ref[...] = dx.astype(BF16)

        @pl.when(pl.program_id(0) == 0)
        def _():
            dg_ref[...] = jnp.zeros_like(dg_ref)
            ls_ref[...] = jnp.zeros_like(ls_ref)

        dg_ref[...] += dg
        ls_ref[...] += jnp.sum(diff * diff, axis=0, keepdims=True)

    return pl.pallas_call(
        body, out_shape=[jax.ShapeDtypeStruct((s, d), F32), jax.ShapeDtypeStruct((s, d), BF16),
                         jax.ShapeDtypeStruct((1, d), F32), jax.ShapeDtypeStruct((1, d), F32)],
        grid=(s // ROWS,), in_specs=[_row_spec(d), _row_spec(d), _vec_spec(d)],
        out_specs=[_row_spec(d), _row_spec(d), _vec_spec(d), _vec_spec(d)],
        compiler_params=_params(("arbitrary",)), name="final_norm_loss")(h2, tgt, g)


ATTN_Q = 128


ATTN_BATCH = 8


def _attn_units(s):
    units = []
    for gi, d in enumerate(DILATIONS):
        for r in range(d):
            for q0 in range(0, s // d, ATTN_Q):
                k0 = max(q0 - SPAN, 0)
                units.append((gi, d, r, q0, k0, q0 + ATTN_Q - k0))
    return units


def _stream_rows(d, r, start, size):
    return pl.ds(r + start * d, size) if d == 1 else pl.ds(r + start * d, size, stride=d)


def _attn_bias(bias_ref, slope):
    qi = lax.broadcasted_iota(jnp.int32, (ATTN_Q, ATTN_Q + SPAN), 0)
    kj = lax.broadcasted_iota(jnp.int32, (ATTN_Q, ATTN_Q + SPAN), 1)
    dist = SPAN + qi - kj
    valid = (dist >= 0) & (dist <= SPAN)
    for gi, d in enumerate(DILATIONS):
        bias_ref[gi] = jnp.where(valid, -(slope * d) * dist.astype(F32), NEG)


def _attn_scores(q_ref, k_ref, bias_ref, gi, d, r, q0, k0, nk):
    qrows, krows = _stream_rows(d, r, q0, ATTN_Q), _stream_rows(d, r, k0, nk)
    qb = q_ref[qrows, :].astype(BF16)
    kb = k_ref[krows, :].astype(BF16)
    sc = lax.dot_general(qb, kb, NT, preferred_element_type=F32) * (HEAD ** -0.5) + bias_ref[gi, :, pl.ds(ATTN_Q + SPAN - nk, nk)]
    return sc, qb, kb, qrows, krows


def _attn_fwd(proj, dm, slopes):
    s = proj.shape[0]
    units = _attn_units(s)

    def body(sl_ref, q_ref, k_ref, v_ref, att_ref, lse_ref, *scr):
        o_scr, l_scr, bias = scr[:3], scr[3:6], scr[6]
        _attn_bias(bias, sl_ref[pl.program_id(0)])
        for first in range(0, len(units), ATTN_BATCH):
            batch = units[first:first + ATTN_BATCH]
            scored = [_attn_scores(q_ref, k_ref, bias, gi, d, r, q0, k0, nk) for gi, d, r, q0, k0, nk in batch]
            soft = []
            for sc, _, _, _, _ in scored:
                m = jnp.max(sc, axis=-1, keepdims=True)
                p = jnp.exp(sc - m)
                soft.append((m, p, jnp.sum(p, axis=-1, keepdims=True)))
            outs = [lax.dot_general(p.astype(BF16), v_ref[sco[4], :].astype(BF16), NN, preferred_element_type=F32)
                    for (m, p, l), sco in zip(soft, scored)]
            for (gi, *_), (m, p, l), sco, o in zip(batch, soft, scored, outs):
                o_scr[gi][sco[3], :] = o / l
                l_scr[gi][sco[3], :] = jnp.broadcast_to(m + jnp.log(l), (ATTN_Q, HEAD))
        l0, l1, l2 = l_scr[0][...], l_scr[1][...], l_scr[2][...]
        m = jnp.maximum(jnp.maximum(l0, l1), l2)
        w0, w1, w2 = jnp.exp(l0 - m), jnp.exp(l1 - m), jnp.exp(l2 - m)
        tot = w0 + w1 + w2
        att_ref[...] = ((w0 * o_scr[0][...] + w1 * o_scr[1][...] + w2 * o_scr[2][...]) / tot).astype(BF16)
        lse_ref[...] = m + jnp.log(tot)

    def seg(i):
        return pl.BlockSpec((s, HEAD), lambda h: (0, i * (dm // HEAD) + h))

    col = pl.BlockSpec((s, HEAD), lambda h: (0, h))
    return pl.pallas_call(
        body, out_shape=[jax.ShapeDtypeStruct((s, dm), BF16), jax.ShapeDtypeStruct((s, dm), F32)], grid=(dm // HEAD,),
        in_specs=[pl.BlockSpec(memory_space=pltpu.SMEM), seg(0), seg(1), seg(2)], out_specs=[col, col],
        scratch_shapes=[pltpu.VMEM((s, HEAD), F32)] * (2 * len(DILATIONS)) + [pltpu.VMEM((len(DILATIONS), ATTN_Q, ATTN_Q + SPAN), F32)],
        compiler_params=_params(("parallel",)), name="attn_fwd")(slopes, proj, proj, proj)


def _attn_bwd(proj, dm, datt, att, lse, slopes, others):
    s = proj.shape[0]
    units = _attn_units(s)

    def body(sl_ref, q_ref, k_ref, v_ref, do_ref, att_ref, lse_ref, o3, o4, o5, o6, out_ref, dq_scr, dk_scr, dv_scr, dl_scr, bias):
        _attn_bias(bias, sl_ref[pl.program_id(0)])
        delta = jnp.sum(do_ref[...] * att_ref[...].astype(F32), axis=-1, keepdims=True)
        dl_scr[...] = jnp.broadcast_to(delta, (s, HEAD))
        dq_scr[...] = jnp.zeros_like(dq_scr)
        dk_scr[...] = jnp.zeros_like(dk_scr)
        dv_scr[...] = jnp.zeros_like(dv_scr)
        for first in range(0, len(units), ATTN_BATCH):
            scored = [_attn_scores(q_ref, k_ref, bias, gi, d, r, q0, k0, nk) for gi, d, r, q0, k0, nk in units[first:first + ATTN_BATCH]]
            dobs = [do_ref[sco[3], :].astype(BF16) for sco in scored]
            dps = [lax.dot_general(dob, v_ref[sco[4], :].astype(BF16), NT, preferred_element_type=F32) for dob, sco in zip(dobs, scored)]
            ps = [jnp.exp(sco[0] - lse_ref[sco[3], :][:, 0:1]) for sco in scored]
            dss = [(p * (dp - dl_scr[sco[3], :][:, 0:1]) * (HEAD ** -0.5)).astype(BF16) for p, dp, sco in zip(ps, dps, scored)]
            dqs = [lax.dot_general(ds, sco[2], NN, preferred_element_type=F32) for ds, sco in zip(dss, scored)]
            dks = [lax.dot_general(ds, sco[1], TN, preferred_element_type=F32) for ds, sco in zip(dss, scored)]
            dvs = [lax.dot_general(p.astype(BF16), dob, TN, preferred_element_type=F32) for p, dob in zip(ps, dobs)]
            for sco, dq, dk, dv in zip(scored, dqs, dks, dvs):
                dq_scr[sco[3], :] += dq
                dk_scr[sco[4], :] += dk
                dv_scr[sco[4], :] += dv
        for j, scr in enumerate((dq_scr, dk_scr, dv_scr)):
            out_ref[j] = scr[...].astype(BF16)
        for j, other in enumerate((o3, o4, o5, o6)):
            out_ref[3 + j] = other[...]

    def seg(i):
        return pl.BlockSpec((s, HEAD), lambda h: (0, i * (dm // HEAD) + h))

    col = pl.BlockSpec((s, HEAD), lambda h: (0, h))
    return pl.pallas_call(
        body, out_shape=jax.ShapeDtypeStruct((7, s, dm), BF16), grid=(dm // HEAD,),
        in_specs=[pl.BlockSpec(memory_space=pltpu.SMEM), seg(0), seg(1), seg(2), col, col, col, col, col, col, col],
        out_specs=pl.BlockSpec((7, s, HEAD), lambda h: (0, 0, h)),
        scratch_shapes=[pltpu.VMEM((s, HEAD), F32)] * 4 + [pltpu.VMEM((len(DILATIONS), ATTN_Q, ATTN_Q + SPAN), F32)],
        compiler_params=_params(("parallel",)), name="attn_bwd")(slopes, proj, proj, proj, datt, att, lse, *others)


VEC_CB, VEC_BA, VEC_BX, VEC_LAM = 0, 1, 2, 3
SEG_Q, SEG_K, SEG_V, SEG_X, SEG_GATE, SEG_GA, SEG_GL = range(7)


def _softplus(z):
    return jnp.maximum(z, 0.0) + jnp.log1p(jnp.exp(-jnp.abs(z)))


def _gate_math(xc, wa_ref, wx_ref, vec):
    xcb = xc.astype(BF16)
    nh = xc.shape[1] // HEAD
    pre_a = jnp.concatenate([jnp.dot(xcb[:, h * HEAD:(h + 1) * HEAD], wa_ref[h], preferred_element_type=F32) for h in range(nh)], axis=1)
    pre_x = jnp.concatenate([jnp.dot(xcb[:, h * HEAD:(h + 1) * HEAD], wx_ref[h], preferred_element_type=F32) for h in range(nh)], axis=1)
    ra = _sigmoid(pre_a + vec[VEC_BA:VEC_BA + 1])
    ig = _sigmoid(pre_x + vec[VEC_BX:VEC_BX + 1])
    sp = _softplus(-vec[VEC_LAM:VEC_LAM + 1])
    log_a = -LRU_C * ra * sp
    a = jnp.exp(log_a)
    z = 2.0 * log_a
    one_minus_a2 = jnp.where(z > -0.01, -z * (1.0 + z * (0.5 + z * (1.0 / 6.0))), 1.0 - jnp.exp(z))
    mult = jnp.sqrt(one_minus_a2)
    return dict(xcb=xcb, ra=ra, ig=ig, sp=sp, a=a, mult=mult)


def _conv_pad_prev(pad_ref, cur, halo, first):
    pad_ref[0:SUBLANES, :] = jnp.where(first, 0.0, halo)
    pad_ref[SUBLANES:SUBLANES + cur.shape[0], :] = cur


def _shift_rows(x, s, fill, up=False):
    rid = lax.broadcasted_iota(jnp.int32, x.shape, 0)
    if up:
        return jnp.where(rid < SUBLANES - s, pltpu.roll(x, SUBLANES - s, axis=0), fill)
    return jnp.where(rid >= s, pltpu.roll(x, s, axis=0), fill)


def _lru_fwd(proj, d, cw8, vec8, wa, wx):
    s = proj.shape[0]
    hb = ROWS // SUBLANES

    def body(x_ref, halo_ref, g_ref, cw_ref, vec_ref, wa_ref, wx_ref, xc_ref, hp_ref, h2_ref, y_ref, pad, a_scr, u_scr, carry):
        @pl.when(pl.program_id(0) == 0)
        def _():
            carry[...] = jnp.zeros_like(carry)

        _conv_pad_prev(pad, x_ref[...], halo_ref[...], pl.program_id(0) == 0)
        vec = vec_ref[...]
        xc = vec[VEC_CB:VEC_CB + 1]
        for k in range(CONV_TAPS):
            xc = xc + cw_ref[k:k + 1, :] * pad[pl.ds(SUBLANES - (CONV_TAPS - 1) + k, ROWS), :]
        gm = _gate_math(xc, wa_ref, wx_ref, vec)
        xc_ref[...] = xc
        a_scr[...] = gm["a"]
        u_scr[...] = gm["mult"] * (gm["ig"] * xc)

        def group(gi, before):
            rows = pl.ds(pl.multiple_of(gi * SUBLANES, SUBLANES), SUBLANES)
            ca, cb = a_scr[rows, :], u_scr[rows, :]
            for sh in (1, 2, 4):
                cb = ca * _shift_rows(cb, sh, 0.0) + cb
                ca = ca * _shift_rows(ca, sh, 1.0)
            h = cb + ca * before
            hp_ref[rows, :] = jnp.where(lax.broadcasted_iota(jnp.int32, h.shape, 0) == 0, before, pltpu.roll(h, 1, axis=0))
            h2_ref[rows, :] = h.astype(BF16)
            y_ref[rows, :] = (h * _gelu(g_ref[rows, :])).astype(BF16)
            return jnp.broadcast_to(h[SUBLANES - 1:SUBLANES, :], h.shape)

        carry[...] = lax.fori_loop(0, ROWS // SUBLANES, group, carry[...], unroll=2)

    wspec = pl.BlockSpec(wa.shape, lambda i: (0, 0, 0))
    return pl.pallas_call(
        body, out_shape=[jax.ShapeDtypeStruct((s, d), F32)] * 2 + [jax.ShapeDtypeStruct((s, d), BF16)] * 2, grid=(s // ROWS,),
        in_specs=[pl.BlockSpec((ROWS, d), lambda i: (i, SEG_X)),
                  pl.BlockSpec((SUBLANES, d), lambda i: (jnp.maximum(i * hb - 1, 0), SEG_X)),
                  pl.BlockSpec((ROWS, d), lambda i: (i, SEG_GATE)),
                  _vec_spec(d, SUBLANES), _vec_spec(d, SUBLANES), wspec, wspec],
        out_specs=[_row_spec(d)] * 4,
        scratch_shapes=[pltpu.VMEM((ROWS + SUBLANES, d), F32), pltpu.VMEM((ROWS, d), F32), pltpu.VMEM((ROWS, d), F32), pltpu.VMEM((SUBLANES, d), F32)],
        compiler_params=_params(("arbitrary",)), name="lru_fwd")(proj, proj, proj, cw8, vec8, wa, wx)


def _lru_bwd(hp, dh, xc, proj, cw8, wa, wx, vec8):
    s, d = xc.shape
    nh = d // HEAD
    nb = s // ROWS
    hb = ROWS // SUBLANES

    def body(hp_ref, dh_ref, xc_ref, x_ref, xprev_ref, cw_ref, wa_ref, wx_ref, vec_ref, dx_ref, dwa_ref, dwx_ref, dvec_ref, dcw_ref,
             a_scr, g_scr, da_scr, carry, padd, padx, dnext):
        step = pl.program_id(0)

        @pl.when(step == 0)
        def _():
            carry[...] = jnp.zeros_like(carry)
            dnext[...] = jnp.zeros_like(dnext)
            dwa_ref[...] = jnp.zeros_like(dwa_ref)
            dwx_ref[...] = jnp.zeros_like(dwx_ref)
            dvec_ref[...] = jnp.zeros_like(dvec_ref)
            dcw_ref[...] = jnp.zeros_like(dcw_ref)

        xc_v, vec = xc_ref[...], vec_ref[...]
        gm = _gate_math(xc_v, wa_ref, wx_ref, vec)
        ra, ig, sp, a, mult = gm["ra"], gm["ig"], gm["sp"], gm["a"], gm["mult"]
        a_scr[...] = a

        def group(j, after):
            rows = pl.ds(pl.multiple_of((ROWS // SUBLANES - 1 - j) * SUBLANES, SUBLANES), SUBLANES)
            ca, dhv = a_scr[rows, :], dh_ref[rows, :]
            cb = ca * dhv
            for sh in (1, 2, 4):
                cb = ca * _shift_rows(cb, sh, 0.0, up=True) + cb
                ca = ca * _shift_rows(ca, sh, 1.0, up=True)
            c = cb + ca * after
            last = lax.broadcasted_iota(jnp.int32, c.shape, 0) == SUBLANES - 1
            g = dhv + jnp.where(last, after, pltpu.roll(c, SUBLANES - 1, axis=0))
            g_scr[rows, :] = g
            da_scr[rows, :] = g * hp_ref[rows, :]
            return jnp.broadcast_to(c[0:1, :], c.shape)

        carry[...] = lax.fori_loop(0, ROWS // SUBLANES, group, carry[...], unroll=2)
        du, da = g_scr[...], da_scr[...]
        dmult = du * ig * xc_v
        dlog_a = da * a - dmult * (a * a) / mult
        dpre_a = dlog_a * (-LRU_C * sp) * ra * (1.0 - ra)
        dpre_x = du * mult * xc_v * ig * (1.0 - ig)
        dlam = jnp.sum(dlog_a * (-LRU_C * ra), axis=0, keepdims=True) * (-_sigmoid(-vec[VEC_LAM:VEC_LAM + 1]))
        dvec_ref[VEC_BA:VEC_BA + 1, :] += jnp.sum(dpre_a, axis=0, keepdims=True)
        dvec_ref[VEC_BX:VEC_BX + 1, :] += jnp.sum(dpre_x, axis=0, keepdims=True)
        dvec_ref[VEC_LAM:VEC_LAM + 1, :] += dlam
        dab, dxb, xcb = dpre_a.astype(BF16), dpre_x.astype(BF16), gm["xcb"]
        back = []
        for h in range(nh):
            cols = slice(h * HEAD, (h + 1) * HEAD)
            dwa_ref[h] += lax.dot_general(xcb[:, cols], dab[:, cols], TN, preferred_element_type=F32)
            dwx_ref[h] += lax.dot_general(xcb[:, cols], dxb[:, cols], TN, preferred_element_type=F32)
            back.append(lax.dot_general(dab[:, cols], wa_ref[h], NT, preferred_element_type=F32)
                        + lax.dot_general(dxb[:, cols], wx_ref[h], NT, preferred_element_type=F32))
        dc = du * mult * ig + jnp.concatenate(back, axis=1)
        padd[0:ROWS, :] = dc
        padd[ROWS:ROWS + SUBLANES, :] = dnext[...]
        dnext[...] = dc[0:SUBLANES, :]
        _conv_pad_prev(padx, x_ref[...], xprev_ref[...], step == nb - 1)
        dx = jnp.zeros_like(dc)
        for k in range(CONV_TAPS):
            dx = dx + cw_ref[k:k + 1, :] * padd[pl.ds(CONV_TAPS - 1 - k, ROWS), :]
            dcw_ref[k:k + 1, :] += jnp.sum(dc * padx[pl.ds(SUBLANES - (CONV_TAPS - 1) + k, ROWS), :], axis=0, keepdims=True)
        dcw_ref[CONV_TAPS:CONV_TAPS + 1, :] += jnp.sum(dc, axis=0, keepdims=True)
        dx_ref[...] = dx.astype(BF16)

    rows_rev = pl.BlockSpec((ROWS, d), lambda i: (nb - 1 - i, 0))
    wspec = pl.BlockSpec(wa.shape, lambda i: (0, 0, 0))
    return pl.pallas_call(
        body, out_shape=[jax.ShapeDtypeStruct((s, d), BF16), jax.ShapeDtypeStruct(wa.shape, F32), jax.ShapeDtypeStruct(wa.shape, F32),
                         jax.ShapeDtypeStruct((SUBLANES, d), F32), jax.ShapeDtypeStruct((SUBLANES, d), F32)],
        grid=(nb,),
        in_specs=[rows_rev, rows_rev, rows_rev, pl.BlockSpec((ROWS, d), lambda i: (nb - 1 - i, SEG_X)),
                  pl.BlockSpec((SUBLANES, d), lambda i: (jnp.maximum((nb - 1 - i) * hb - 1, 0), SEG_X)),
                  _vec_spec(d, SUBLANES), wspec, wspec, _vec_spec(d, SUBLANES)],
        out_specs=[rows_rev, wspec, wspec, _vec_spec(d, SUBLANES), _vec_spec(d, SUBLANES)],
        scratch_shapes=[pltpu.VMEM((ROWS, d), F32)] * 3 + [pltpu.VMEM((SUBLANES, d), F32)]
        + [pltpu.VMEM((ROWS + SUBLANES, d), F32)] * 2 + [pltpu.VMEM((SUBLANES, d), F32)],
        compiler_params=_params(("arbitrary",)), name="lru_bwd")(hp, dh, xc, proj, proj, cw8, wa, wx, vec8)


def _coords():
    return lax.axis_index("x"), lax.axis_index("y"), lax.axis_index("c")


def _other_chips(x, y):
    return [(1 - x, y), (x, 1 - y), (1 - x, 1 - y)]


def _slab(ref, kind, shard_shape, idx, half=None):
    r, c = shard_shape
    r0, nr = (0, r) if half is None else (half * (r // 2), r // 2)
    if kind == "col":
        return ref.at[pl.ds(r0, nr), pl.ds(pl.multiple_of(idx * c, LANES), c)]
    if kind == "row":
        return ref.at[pl.ds(pl.multiple_of(idx * r, SUBLANES) + r0, nr), :]
    return ref.at[idx, pl.ds(r0, nr), :]


def _full_shape(shard_shape, kind):
    r, c = shard_shape
    return {"col": (r, c * N_DEV), "row": (r * N_DEV, c), "slot": (N_DEV, r, c)}[kind]


def _handshake(peers):
    barrier = pltpu.get_barrier_semaphore()
    for peer in peers:
        pl.semaphore_signal(barrier, inc=1, device_id=peer, device_id_type=MESH)
    pl.semaphore_wait(barrier, len(peers))


def _launch(name, body, out_shape, operands, sems, sequencer_id):
    if sequencer_id is None:
        return pl.pallas_call(body, out_shape=out_shape, in_specs=[HBM] * len(operands), out_specs=[HBM] * len(out_shape),
                              scratch_shapes=sems, name=name)(*operands)
    return pl.kernel(body, out_type=out_shape, mesh=plsc.ScalarSubcoreMesh(axis_name="seq", num_cores=1), name=name,
                     scratch_types=sems, compiler_params=pltpu.CompilerParams(collective_id=sequencer_id))(*operands)


AG_COPIES = 10


def _all_gather(name, shards, kinds, sequencer_id=None):
    n = len(shards)
    shapes = [s.shape for s in shards]

    def body(*refs):
        ins, outs = refs[:n], refs[n:2 * n]
        send_sems, recv_sems, local_sems = refs[2 * n:]
        x, y, c = _coords()
        me, sib, xn, yn, dg = (x, y, c), (x, y, 1 - c), (1 - x, y, c), (x, 1 - y, c), (1 - x, 1 - y, c)
        if sequencer_id is not None:
            _handshake([sib, xn, yn])

        def part(i, dev, half=None):
            return _slab(outs[i], kinds[i], shapes[i], 4 * dev[0] + 2 * dev[1] + dev[2], half)

        def copy(i, k, block, half, to, own=False):
            r = shapes[i][0]
            src = part(i, block, half) if not own else (ins[i] if half is None else ins[i].at[pl.ds(half * (r // 2), r // 2), :])
            return pltpu.make_async_remote_copy(
                src_ref=src, dst_ref=part(i, block, half), send_sem=send_sems.at[AG_COPIES * i + k],
                recv_sem=recv_sems.at[AG_COPIES * i + k], device_id=to, device_id_type=MESH)

        def other_core(dev):
            return (dev[0], dev[1], 1 - c)

        started = []

        def start(cp):
            cp.start()
            started.append(cp)

        for i in range(n):
            start(copy(i, 1, me, 0, xn, own=True))
            start(copy(i, 4, me, 1, yn, own=True))
            start(copy(i, 2, me, 1, xn, own=True))
            start(copy(i, 3, me, 0, yn, own=True))
            start(copy(i, 0, me, None, sib, own=True))
        mine = [pltpu.make_async_copy(ins[i], part(i, me), local_sems.at[i]) for i in range(n)]
        for cp in mine:
            cp.start()
        for i in range(n):
            copy(i, 1, xn, 0, me).wait_recv()
            start(copy(i, 5, xn, 0, yn))
            copy(i, 4, yn, 1, me).wait_recv()
            start(copy(i, 6, yn, 1, xn))
        for i in range(n):
            copy(i, 2, xn, 1, me).wait_recv()
            start(copy(i, 7, xn, None, sib))
            copy(i, 3, yn, 0, me).wait_recv()
            start(copy(i, 8, yn, None, sib))
        for i in range(n):
            copy(i, 5, dg, 0, me).wait_recv()
            copy(i, 6, dg, 1, me).wait_recv()
            start(copy(i, 9, dg, None, sib))
        for i in range(n):
            copy(i, 0, sib, None, me).wait_recv()
            for k, dev in ((7, xn), (8, yn), (9, dg)):
                copy(i, k, other_core(dev), None, me).wait_recv()
        for cp in started:
            cp.wait_send()
        for cp in mine:
            cp.wait()

    out_shape = [jax.ShapeDtypeStruct(_full_shape(s.shape, k), s.dtype) for s, k in zip(shards, kinds)]
    sems = [pltpu.SemaphoreType.DMA((AG_COPIES * n,)), pltpu.SemaphoreType.DMA((AG_COPIES * n,)), pltpu.SemaphoreType.DMA((n,))]
    return _launch(name, body, out_shape, shards, sems, sequencer_id)


def _sibling_copies(kinds, shard_shapes):
    def make(ins, outs, send_sems, recv_sems):
        x, y, c = _coords()
        return [pltpu.make_async_remote_copy(
            src_ref=_slab(ins[i], kinds[i], shard_shapes[i], 2 * q + (1 - c)), dst_ref=outs[i].at[q],
            send_sem=send_sems.at[N_CHIP * i + q], recv_sem=recv_sems.at[N_CHIP * i + q],
            device_id=(x, y, 1 - c), device_id_type=MESH) for i in range(len(ins)) for q in range(N_CHIP)]
    return make


def _sibling_side(partials, kinds, shard_shapes):
    return (partials, [jax.ShapeDtypeStruct((N_CHIP, *s), BF16) for s in shard_shapes], N_CHIP * len(partials),
            _sibling_copies(kinds, shard_shapes))


def _exchange_chips(name, chip_sums, sequencer_id=None):
    n = len(chip_sums)

    def body(*refs):
        ins, outs = refs[:n], refs[n:2 * n]
        send_sems, recv_sems = refs[2 * n:]
        x, y, c = _coords()
        if sequencer_id is not None:
            _handshake([(cx, cy, c) for cx, cy in _other_chips(x, y)])
        cps = []
        for i in range(n):
            for k, (cx, cy) in enumerate(_other_chips(x, y)):
                cps.append(pltpu.make_async_remote_copy(
                    src_ref=ins[i].at[2 * cx + cy], dst_ref=outs[i].at[k], send_sem=send_sems.at[3 * i + k],
                    recv_sem=recv_sems.at[3 * i + k], device_id=(cx, cy, c), device_id_type=MESH))
        for cp in cps:
            cp.start()
        for cp in cps:
            cp.wait()

    return _launch(name, body, [jax.ShapeDtypeStruct((3, *t.shape[1:]), BF16) for t in chip_sums], chip_sums,
                   [pltpu.SemaphoreType.DMA((3 * n,)), pltpu.SemaphoreType.DMA((3 * n,))], sequencer_id)


def _all_peers(x, y, c):
    return [(x ^ (k >> 2), y ^ ((k >> 1) & 1), c ^ (k & 1)) for k in range(1, N_DEV)]


def _small_scatter(name, packed, sequencer_id):
    rows = packed.shape[0] // N_DEV

    def body(p_ref, rb_ref, send_sems, recv_sems, local_sem):
        x, y, c = _coords()
        me = 4 * x + 2 * y + c
        peers = _all_peers(x, y, c)
        _handshake(peers)

        def piece(idx):
            return p_ref.at[pl.ds(pl.multiple_of(idx * rows, SUBLANES), rows), :]

        cps = [pltpu.make_async_remote_copy(src_ref=piece(me ^ k), dst_ref=rb_ref.at[k], send_sem=send_sems.at[k], recv_sem=recv_sems.at[k],
                                            device_id=peers[k - 1], device_id_type=MESH) for k in range(1, N_DEV)]
        for cp in cps:
            cp.start()
        mine = pltpu.make_async_copy(piece(me), rb_ref.at[0], local_sem)
        mine.start()
        for cp in cps:
            cp.wait()
        mine.wait()

    return _launch(name, body, [jax.ShapeDtypeStruct((N_DEV, rows, LANES), F32)], [packed],
                   [pltpu.SemaphoreType.DMA((N_DEV,)), pltpu.SemaphoreType.DMA((N_DEV,)), pltpu.SemaphoreType.DMA], sequencer_id)[0]


def _small_sum(name, pieces):
    def body(p_ref, o_ref):
        acc = p_ref[0]
        for k in range(1, N_DEV):
            acc = acc + p_ref[k]
        o_ref[...] = acc

    vm = pl.BlockSpec(memory_space=pltpu.VMEM)
    return pl.pallas_call(body, out_shape=jax.ShapeDtypeStruct(pieces.shape[1:], F32), in_specs=[vm], out_specs=vm, name=name)(pieces)


def _small_gather(name, tot, sequencer_id):
    rows = tot.shape[0]

    def body(t_ref, out_ref, send_sems, recv_sems, local_sem):
        x, y, c = _coords()
        me = 4 * x + 2 * y + c
        peers = _all_peers(x, y, c)
        _handshake(peers)

        def piece(idx):
            return out_ref.at[pl.ds(pl.multiple_of(idx * rows, SUBLANES), rows), :]

        cps = [pltpu.make_async_remote_copy(src_ref=t_ref, dst_ref=piece(me), send_sem=send_sems.at[k], recv_sem=recv_sems.at[k],
                                            device_id=peers[k - 1], device_id_type=MESH) for k in range(1, N_DEV)]
        for cp in cps:
            cp.start()
        mine = pltpu.make_async_copy(t_ref, piece(me), local_sem)
        mine.start()
        for cp in cps:
            cp.wait()
        mine.wait()

    return _launch(name, body, [jax.ShapeDtypeStruct((N_DEV * rows, LANES), F32)], [tot],
                   [pltpu.SemaphoreType.DMA((N_DEV,)), pltpu.SemaphoreType.DMA((N_DEV,)), pltpu.SemaphoreType.DMA], sequencer_id)[0]


def _all_reduce_small(name, packed):
    rows = packed.shape[0] // N_DEV

    def body(p_ref, out_ref, rb, tot, send_sems, recv_sems):
        x, y, c = _coords()
        me = 4 * x + 2 * y + c

        def peer(k):
            return (x ^ (k >> 2), y ^ ((k >> 1) & 1), c ^ (k & 1))

        def rows_of(idx):
            return pl.ds(pl.multiple_of(idx * rows, SUBLANES), rows)

        def piece(ref, idx):
            return ref.at[rows_of(idx), :]

        scatter = [pltpu.make_async_remote_copy(src_ref=piece(p_ref, me ^ k), dst_ref=rb.at[k], send_sem=send_sems.at[k],
                                                recv_sem=recv_sems.at[k], device_id=peer(k), device_id_type=MESH) for k in range(1, N_DEV)]
        for cp in scatter:
            cp.start()
        acc = p_ref[rows_of(me), :]
        for cp in scatter:
            cp.wait_recv()
        for k in range(1, N_DEV):
            acc = acc + rb[k]
        tot[...] = acc
        out_ref[rows_of(me), :] = acc
        gather = [pltpu.make_async_remote_copy(src_ref=tot, dst_ref=piece(out_ref, me), send_sem=send_sems.at[N_DEV + k],
                                               recv_sem=recv_sems.at[N_DEV + k], device_id=peer(k), device_id_type=MESH)
                  for k in range(1, N_DEV)]
        for cp in gather:
            cp.start()
        for k in range(1, N_DEV):
            pltpu.make_async_remote_copy(src_ref=tot, dst_ref=piece(out_ref, me ^ k), send_sem=send_sems.at[N_DEV + k],
                                         recv_sem=recv_sems.at[N_DEV + k], device_id=peer(k), device_id_type=MESH).wait_recv()
        for cp in scatter + gather:
            cp.wait_send()

    vm = pl.BlockSpec(memory_space=pltpu.VMEM)
    return pl.pallas_call(
        body, out_shape=jax.ShapeDtypeStruct(packed.shape, F32), in_specs=[vm], out_specs=vm,
        scratch_shapes=[pltpu.VMEM((N_DEV, rows, LANES), F32), pltpu.VMEM((rows, LANES), F32),
                        pltpu.SemaphoreType.DMA((2 * N_DEV,)), pltpu.SemaphoreType.DMA((2 * N_DEV,))],
        compiler_params=pltpu.CompilerParams(vmem_limit_bytes=VMEM_LIMIT), name=name)(packed)


def _adamw_math(g, w, m, v):
    m = ADAM_B1 * m + (1.0 - ADAM_B1) * g
    v = ADAM_B2 * v + (1.0 - ADAM_B2) * (g * g)
    delta = -ADAM_LR * ((m / ADAM_C1) / (jnp.sqrt(v / ADAM_C2) + ADAM_EPS) + ADAM_WD * w)
    return delta, m, v


def _slab_spec(kind, shard_shape, tr, slab_of):
    r, c = shard_shape
    if kind == "col":
        return pl.BlockSpec((tr, c), lambda q, i, sc: (i, slab_of(q, sc)))
    return pl.BlockSpec((tr, c), lambda q, i, sc: (slab_of(q, sc) * (r // tr) + i, 0))


def _chip_sum(name, partial, recv, kind, shard_shape, core):
    r, c = shard_shape
    tr = _blk(r, 1024)

    def body(core_ref, p_ref, r_ref, o_ref):
        o_ref[...] = (p_ref[...].astype(F32) + r_ref[...].astype(F32)).astype(BF16)

    spec4 = pl.BlockSpec((None, tr, c), lambda q, i, sc: (q, i, 0))
    grid_spec = pltpu.PrefetchScalarGridSpec(
        num_scalar_prefetch=1, grid=(N_CHIP, r // tr),
        in_specs=[_slab_spec(kind, shard_shape, tr, lambda q, sc: 2 * q + sc[0]), spec4], out_specs=spec4)
    return pl.pallas_call(body, out_shape=jax.ShapeDtypeStruct((N_CHIP, r, c), BF16), grid_spec=grid_spec,
                          compiler_params=_params(("parallel", "parallel")), name=name)(core, partial, recv)


def _adamw_shard(name, parts, w, m, v, chip):
    r, c = w.shape
    n_parts = len(parts)
    tr = _blk(r // n_parts, 256)
    per = r // n_parts // tr

    def body(chip_ref, *refs):
        src, (w_ref, m_ref, v_ref), (g_out, d_out, m_out, v_out) = refs[:2 * n_parts], refs[2 * n_parts:2 * n_parts + 3], refs[2 * n_parts + 3:]
        for p in range(n_parts):
            @pl.when(pl.program_id(0) // per == p)
            def _():
                g = src[2 * p][...].astype(F32)
                for k in range(3):
                    g = g + src[2 * p + 1][k].astype(F32)
                g_out[...] = g
                d_out[...], m_out[...], v_out[...] = _adamw_math(g, w_ref[...], m_ref[...], v_ref[...])

    def part_specs(p):
        at = lambda i: jnp.clip(i - p * per, 0, per - 1)
        return [pl.BlockSpec((None, tr, c), lambda i, sc: (sc[0], at(i), 0)), pl.BlockSpec((3, tr, c), lambda i, sc: (0, at(i), 0))]

    blk = pl.BlockSpec((tr, c), lambda i, sc: (i, 0))
    grid_spec = pltpu.PrefetchScalarGridSpec(
        num_scalar_prefetch=1, grid=(r // tr,), in_specs=[s for p in range(n_parts) for s in part_specs(p)] + [blk, blk, blk], out_specs=[blk] * 4)
    return pl.pallas_call(body, out_shape=[jax.ShapeDtypeStruct((r, c), F32)] * 4, grid_spec=grid_spec,
                          compiler_params=_params(("parallel",)), name=name)(chip, *[a for p in parts for a in p], w, m, v)


def _adamw_small(name, g, w, m, v):
    def body(g_ref, w_ref, m_ref, v_ref, d_out, m_out, v_out):
        d_out[...], m_out[...], v_out[...] = _adamw_math(g_ref[...], w_ref[...], m_ref[...], v_ref[...])

    vm = pl.BlockSpec(memory_space=pltpu.VMEM)
    return pl.pallas_call(body, out_shape=[jax.ShapeDtypeStruct(g.shape, F32)] * 3, in_specs=[vm] * 4, out_specs=[vm] * 3,
                          compiler_params=pltpu.CompilerParams(vmem_limit_bytes=VMEM_LIMIT), name=name)(g, w, m, v)


def _pack_rows(arrays, total_rows):
    flat = [a.reshape(-1, LANES) for a in arrays]
    used = sum(f.shape[0] for f in flat)
    return jnp.concatenate(flat + [jnp.zeros((total_rows - used, LANES), F32)], axis=0)


def _unpack_rows(packed, like):
    out, at = [], 0
    for a in like:
        n = a.size // LANES
        out.append(packed[at:at + n].reshape(a.shape))
        at += n
    return out


def kernel(x, norm_mix_g, w_in, conv_w, conv_b, lru_wa, lru_ba, lru_wx, lru_bx, lru_lambda, w_proj_attn, w_proj_lru, w_out, norm_mlp_g, w_up, w_down, norm_final_g, loss_target, m_norm_mix_g, m_w_in, m_conv_w, m_conv_b, m_lru_wa, m_lru_ba, m_lru_wx, m_lru_bx, m_lru_lambda, m_w_proj_attn, m_w_proj_lru, m_w_out, m_norm_mlp_g, m_w_up, m_w_down, m_norm_final_g, v_norm_mix_g, v_w_in, v_conv_w, v_conv_b, v_lru_wa, v_lru_ba, v_lru_wx, v_lru_bx, v_lru_lambda, v_w_proj_attn, v_w_proj_lru, v_w_out, v_norm_mlp_g, v_w_up, v_w_down, v_norm_final_g):
    xs, tgt = x[0], loss_target[0]
    s, d = xs.shape
    nh = d // HEAD
    ix, iy, ic = _coords()
    core = jnp.reshape(ic, (1,)).astype(jnp.int32)
    chip = jnp.reshape(2 * ix + iy, (1,)).astype(jnp.int32)
    dev = 4 * ix + 2 * iy + ic

    big = [w_in[0], w_proj_attn[0], w_proj_lru[0], w_out[0], w_up[0], w_down[0]]
    big_m = [m_w_in[0], m_w_proj_attn[0], m_w_proj_lru[0], m_w_out[0], m_w_up[0], m_w_down[0]]
    big_v = [v_w_in[0], v_w_proj_attn[0], v_w_proj_lru[0], v_w_out[0], v_w_up[0], v_w_down[0]]
    kinds = ["col", "row", "row", "row", "col", "row"]
    pad_taps = lambda t: jnp.pad(t, ((0, SUBLANES - CONV_TAPS), (0, 0)))
    shards = [w.astype(BF16) for w in big]
    pad_taps2 = lambda t: jnp.pad(t, ((0, 2 * SUBLANES - CONV_TAPS), (0, 0)))
    win, cw_slots = _all_gather("all_gather_w_in", [shards[0], pad_taps2(conv_w[0])], ["col", "slot"], sequencer_id=7)
    later = lax.optimization_barrier((shards[1:], win))[0]
    wpa, wpl, wout = _all_gather("all_gather_mix", later[:3], kinds[1:4], sequencer_id=1)
    wup, wdown = _all_gather("all_gather_mlp", later[3:], kinds[4:], sequencer_id=5)
    cw8 = jnp.transpose(cw_slots[:, :SUBLANES], (1, 0, 2)).reshape(SUBLANES, d)
    row_id = lax.broadcasted_iota(jnp.int32, (SUBLANES, d), 0)
    vec8 = sum(jnp.where(row_id == k, t, 0.0) for k, t in ((VEC_CB, conv_b), (VEC_BA, lru_ba), (VEC_BX, lru_bx), (VEC_LAM, lru_lambda)))
    wa16, wx16 = lru_wa[0].astype(BF16), lru_wx[0].astype(BF16)
    slopes = 2.0 ** (-8.0 * jnp.arange(1, nh + 1, dtype=F32) / nh)

    def seg_specs(*segs):
        return lambda bm, bn: [pl.BlockSpec((bm, bn), (lambda i, j, kk, sg=sg: (i, sg * (d // bn) + j))) for sg in segs]

    def plain_specs(k):
        return lambda bm, bn: [pl.BlockSpec((bm, bn), lambda i, j, kk: (i, j)) for _ in range(k)]

    xn = _rms_fwd("norm_mix", xs, norm_mix_g)
    proj = _mm_fwd("proj_in", xn, win, 0, 7 * d, [F32], bm=2048)[0]
    att, lse = _attn_fwd(proj, d, slopes)
    xc, hp, h2d, ylru = _lru_fwd(proj, d, cw8, vec8, wa16, wx16)
    pa = _mm_fwd("proj_attn", att, wpa, 0, d, [BF16], bm=2048)[0]

    def merge(acc, pa_b, ga, gl):
        return acc, _sigmoid(ga) * pa_b.astype(F32) + _sigmoid(gl) * acc

    plr, merged = _mm_fwd("proj_lru_merge", ylru, wpl, 0, d, [BF16, BF16], merge, (pa, proj, proj),
                          lambda bm, bn: plain_specs(1)(bm, bn) + seg_specs(SEG_GA, SEG_GL)(bm, bn), bn=512)
    h1 = _mm_fwd("mix_out", merged, wout, 0, d, [F32], lambda acc, r: (acc + r,), (xs,), plain_specs(1))[0]
    hn = _rms_fwd("norm_mlp", h1, norm_mlp_g)

    def relu2(acc):
        return acc, jnp.square(jnp.maximum(acc, 0.0))

    up, hid = _mm_fwd("mlp_up", hn, wup, 0, wup.shape[1], [BF16, BF16], relu2, bm=2048)
    h2 = _mm_fwd("mlp_down", hid, wdown, 0, d, [F32], lambda acc, r: (acc + r,), (h1,), plain_specs(1))[0]
    dh2, dh2b, dg3, loss_lanes = _final_loss(h2, tgt, norm_final_g.reshape(1, d))
    loss_rows = jnp.pad((0.5 / d * jnp.sum(loss_lanes)).reshape(1, 1), ((0, SUBLANES - 1), (0, LANES - 1)))

    def reduce_group(tag, kk, shp, partials, from_sibling, sequencer_id):
        sums = [_chip_sum(f"chip_sum_{tag}_{i}", p, f, k, sh, core) for i, (p, f, k, sh) in enumerate(zip(partials, from_sibling, kk, shp))]
        return list(zip(sums, _exchange_chips(f"rs_chips_{tag}", sums, sequencer_id)))

    g_wdown = _mm_tn("mlp_down_dw", hid, dh2b)
    (dup,), sib_down = _mm_nt("mlp_down_dx", dh2b, wdown, [BF16], lambda acc, u: (acc * (2.0 * jnp.maximum(u.astype(F32), 0.0)),), (up,),
                              plain_specs(1), side=_sibling_side([g_wdown], kinds[5:], [big[5].shape]))
    (red_down,) = reduce_group("mlp_down", kinds[5:], [big[5].shape], [g_wdown], sib_down, 2)
    dup = lax.optimization_barrier((dup, red_down[0]))[0]
    g_wup = _mm_tn("mlp_up_dw", hn, dup)
    (dhn,), sib_up = _mm_nt("mlp_up_dx", dup, wup, [F32], side=_sibling_side([g_wup], kinds[4:5], [big[4].shape]))
    (red_up,) = reduce_group("mlp_up", kinds[4:5], [big[4].shape], [g_wup], sib_up, 10)
    dhn = lax.optimization_barrier((dhn, red_up[0]))[0]
    dh1, dh1b, dg2 = _rms_bwd("norm_mlp_bwd", h1, norm_mlp_g, dhn, dh2)

    def merge_bwd(acc, pa_b, pl_b, ga, gl):
        sa, sl = _sigmoid(ga), _sigmoid(gl)
        return acc * sa, acc * sl, acc * pa_b.astype(F32) * sa * (1.0 - sa), acc * pl_b.astype(F32) * sl * (1.0 - sl)

    dpa, dpl, dga, dgl = _mm_nt("mix_out_dx", dh1b, wout, [BF16] * 4, merge_bwd, (pa, plr, proj, proj),
                                lambda bm, bn: plain_specs(2)(bm, bn) + seg_specs(SEG_GA, SEG_GL)(bm, bn), bn=512)
    g_wout = _mm_tn("mix_out_dw", merged, dh1b)
    datt = _mm_nt("proj_attn_dx", dpa, wpa, [F32], bm=2048)[0]
    g_wpa = _mm_tn("proj_attn_dw", att, dpa)

    def lru_out_bwd(acc, h_b, gate):
        return acc * _gelu(gate), acc * h_b.astype(F32) * _gelu_grad(gate)

    g_wpl = _mm_tn("proj_lru_dw", ylru, dpl)
    shp_mix = [w.shape for w in big[1:4]]
    (dh, dxg), sib_mix = _mm_nt("proj_lru_dx", dpl, wpl, [F32, BF16], lru_out_bwd, (h2d, proj),
                                lambda bm, bn: plain_specs(1)(bm, bn) + seg_specs(SEG_GATE)(bm, bn), bn=512,
                                side=_sibling_side([g_wpa, g_wpl, g_wout], kinds[1:4], shp_mix))
    red_pa, red_pl, red_out = reduce_group("mix", kinds[1:4], shp_mix, [g_wpa, g_wpl, g_wout], sib_mix, 3)
    dh = lax.optimization_barrier((dh, red_down[1]))[0]
    dxr, dwa, dwx, dvec, dconv = _lru_bwd(hp, dh, xc, proj, cw8, wa16, wx16, vec8)
    dproj = _attn_bwd(proj, d, datt, att, lse, slopes, (dxr, dxg, dga, dgl))

    def small_step(tag, grads, ws, ms, vs, like, after, seq=None):
        n_rows = sum(g.size for g in grads) // LANES
        per_dev = -(-n_rows // (N_DEV * SUBLANES)) * SUBLANES
        packed = lax.optimization_barrier((_pack_rows(grads, N_DEV * per_dev), after))[0]
        if seq is None:
            total = _all_reduce_small(f"all_reduce_{tag}", packed)
        else:
            pieces = lax.optimization_barrier((_small_scatter(f"scatter_{tag}", packed, seq[0]), seq[2]))[0]
            total = _small_gather(f"gather_{tag}", _small_sum(f"sum_{tag}", pieces), seq[1])
        w_rows = -(-(sum(w.size for w in ws) // LANES) // SUBLANES) * SUBLANES
        upd = _adamw_small(f"adamw_{tag}", total[:w_rows], _pack_rows(ws, w_rows), _pack_rows(ms, w_rows), _pack_rows(vs, w_rows))
        return _unpack_rows(total, like), [_unpack_rows(t, ws) for t in upd]

    early_w = [conv_b, lru_wa, lru_ba, lru_wx, lru_bx, lru_lambda, norm_mlp_g, norm_final_g]
    early_m = [m_conv_b, m_lru_wa, m_lru_ba, m_lru_wx, m_lru_bx, m_lru_lambda, m_norm_mlp_g, m_norm_final_g]
    early_v = [v_conv_b, v_lru_wa, v_lru_ba, v_lru_wx, v_lru_bx, v_lru_lambda, v_norm_mlp_g, v_norm_final_g]
    early_g = [dconv[CONV_TAPS:CONV_TAPS + 1], dwa, dvec[VEC_BA:VEC_BA + 1], dwx, dvec[VEC_BX:VEC_BX + 1],
               dvec[VEC_LAM:VEC_LAM + 1], dg2, dg3, dconv[0:CONV_TAPS], loss_rows]
    dproj = lax.optimization_barrier((dproj, red_up[1]))[0]
    dproj = lax.optimization_barrier((dproj, red_pa[1], red_pl[1], red_out[1]))[0]
    early_sum, early_upd = small_step("small", early_g, early_w, early_m, early_v,
                                      early_w + [jax.ShapeDtypeStruct((1, CONV_TAPS, d), F32), jax.ShapeDtypeStruct((SUBLANES, LANES), F32)],
                                      dxr, seq=(8, 9, dproj))
    g_cw_full, loss = early_sum[-2], early_sum[-1][0, 0]
    cshard = conv_w.shape[2]
    g_cw = lax.dynamic_slice(g_cw_full, (0, 0, dev * cshard), (1, CONV_TAPS, cshard))
    cw_delta, cw_m, cw_v = (t[:CONV_TAPS][None] for t in _adamw_small(
        "adamw_conv_w", pad_taps(g_cw[0]), pad_taps(conv_w[0]), pad_taps(m_conv_w[0]), pad_taps(v_conv_w[0])))
    half = (big[0].shape[0] // 2, big[0].shape[1])
    g_in0 = _mm_tn("proj_in_dw_0", xn, dproj, part=(0, 2))
    g_in1, sib_in0 = _mm_tn("proj_in_dw_1", xn, dproj, part=(1, 2), side=_sibling_side([g_in0], ["col"], [half]))
    red_in = reduce_group("in_0", ["col"], [half], [g_in0], sib_in0, 4)
    dproj = lax.optimization_barrier((dproj, red_in[0][0], early_sum))[0]
    (dxn0,), sib_in1 = _mm_nt("proj_in_dx_0", dproj, win, [F32], part=(0, 2), side=_sibling_side([g_in1], ["col"], [half]))
    red_in += reduce_group("in_1", ["col"], [half], [g_in1], sib_in1, 6)
    dproj = lax.optimization_barrier((dproj, red_in[1][0]))[0]
    dxn1 = _mm_nt("proj_in_dx_1", dproj, win, [F32], part=(1, 2))[0]
    dxn = jnp.concatenate([dxn0, dxn1], axis=0)
    dxn = lax.optimization_barrier((dxn, red_in[0][1]))[0]
    grad_x, _, dg1 = _rms_bwd("norm_mix_bwd", xs, norm_mix_g, dxn, dh1)
    red_up, red_down = lax.optimization_barrier(((red_up, red_down), dg1))[0]
    big_out = {i: _adamw_shard(f"adamw_{i}", [red], big[i], big_m[i], big_v[i], chip) for i, red in ((4, red_up), (5, red_down))}
    big_out.update({i: _adamw_shard(f"adamw_{i}", [red], big[i], big_m[i], big_v[i], chip) for i, red in ((1, red_pa), (2, red_pl), (3, red_out))})
    late_sum, late_upd = small_step("norm_mix", [dg1], [norm_mix_g], [m_norm_mix_g], [v_norm_mix_g], [norm_mix_g], (big_out[4], big_out[5]))
    big_out[0] = _adamw_shard("adamw_0", red_in, big[0], big_m[0], big_v[0], chip)
    s_grad = late_sum + early_sum[:-2]
    s_delta, s_m, s_v = (late_upd[j] + early_upd[j] for j in range(3))


    names = ["norm_mix_g", "w_in", "conv_w", "conv_b", "lru_wa", "lru_ba", "lru_wx", "lru_bx", "lru_lambda", "w_proj_attn", "w_proj_lru",
             "w_out", "norm_mlp_g", "w_up", "w_down", "norm_final_g"]
    small_names = ["norm_mix_g", "conv_b", "lru_wa", "lru_ba", "lru_wx", "lru_bx", "lru_lambda", "norm_mlp_g", "norm_final_g"]
    big_names = ["w_in", "w_proj_attn", "w_proj_lru", "w_out", "w_up", "w_down"]
    res = {"conv_w": (g_cw, cw_delta, cw_m, cw_v)}
    for i, nm in enumerate(small_names):
        res[nm] = (s_grad[i], s_delta[i], s_m[i], s_v[i])
    for i, nm in enumerate(big_names):
        res[nm] = tuple(t[None] for t in big_out[i])
    return (loss, grad_x[None], *[res[nm][0] for nm in names], *[res[nm][1] for nm in names],
            *[res[nm][2] for nm in names], *[res[nm][3] for nm in names])
```

```python
import jax
import jax.numpy as jnp
from jax import lax
from jax.experimental import pallas as pl
from jax.experimental.pallas import tpu as pltpu
from jax.experimental.pallas import tpu_sc as plsc

F32, BF16 = jnp.float32, jnp.bfloat16
MESH = pl.DeviceIdType.MESH
HBM = pl.BlockSpec(memory_space=pltpu.HBM)
N_DEV = 8
N_CHIP = 4
HEAD = 128
SPAN = 128
DILATIONS = (1, 4, 16)
CONV_TAPS = 4
LRU_C = 8.0
NORM_EPS = 1e-6
LANES = 128
SUBLANES = 8
VMEM_LIMIT = 56 * 1024 * 1024
ADAM_LR, ADAM_B1, ADAM_B2, ADAM_EPS, ADAM_WD, ADAM_STEP = 0.001, 0.9, 0.999, 1e-08, 0.01, 10
ADAM_C1 = 1.0 - ADAM_B1 ** ADAM_STEP
ADAM_C2 = 1.0 - ADAM_B2 ** ADAM_STEP
NEG = -1e30


def _params(sem=None):
    return pltpu.CompilerParams(dimension_semantics=sem, vmem_limit_bytes=VMEM_LIMIT)


def _sigmoid(v):
    return 1.0 / (1.0 + jnp.exp(-v))


def _gelu(v):
    k = 0.7978845608028654
    return 0.5 * v * (1.0 + jnp.tanh(k * (v + 0.044715 * v * v * v)))


def _gelu_grad(v):
    k = 0.7978845608028654
    t = jnp.tanh(k * (v + 0.044715 * v * v * v))
    return 0.5 * (1.0 + t) + 0.5 * v * (1.0 - t * t) * k * (1.0 + 3.0 * 0.044715 * v * v)


NN = (((1,), (0,)), ((), ()))
NT = (((1,), (1,)), ((), ()))
TN = (((0,), (0,)), ((), ()))


def _mm(name, a, a_spec, b, b_spec, dn, grid, out_shapes, out_specs, acc_block, epilogue=None, extras=(), extra_specs=(), side=None):
    nk, ne, no = grid[2], len(extras), len(out_shapes)
    side_ops, side_shapes, side_copies, make_copies = side if side is not None else ((), (), 0, None)
    ns_in, ns_out = len(side_ops), len(side_shapes)

    def body(*refs):
        a_ref, b_ref = refs[0], refs[1]
        ex, side_in = refs[2:2 + ne], refs[2 + ne:2 + ne + ns_in]
        outs = refs[2 + ne + ns_in:2 + ne + ns_in + no]
        side_out = refs[2 + ne + ns_in + no:2 + ne + ns_in + no + ns_out]
        scratch = refs[2 + ne + ns_in + no + ns_out:]
        at = [pl.program_id(ax) for ax in range(3)]
        if side is not None:
            @pl.when((at[0] == 0) & (at[1] == 0) & (at[2] == 0))
            def _():
                for cp in make_copies(side_in, side_out, scratch[-2], scratch[-1]):
                    cp.start()

        part = lax.dot_general(a_ref[...], b_ref[...], dn, preferred_element_type=F32)

        def finish(acc):
            vals = epilogue(acc, *[e[...] for e in ex]) if epilogue is not None else (acc,)
            for o, v in zip(outs, vals):
                o[...] = v.astype(o.dtype)

        if nk == 1:
            finish(part)
        else:
            acc_ref, k = scratch[0], at[2]

            @pl.when(k == 0)
            def _():
                acc_ref[...] = part

            @pl.when(k > 0)
            def _():
                acc_ref[...] += part

            @pl.when(k == nk - 1)
            def _():
                finish(acc_ref[...])

        if side is not None:
            @pl.when((at[0] == grid[0] - 1) & (at[1] == grid[1] - 1) & (at[2] == grid[2] - 1))
            def _():
                for cp in make_copies(side_in, side_out, scratch[-2], scratch[-1]):
                    cp.wait()

    scratch_shapes = [pltpu.VMEM(acc_block, F32)] if nk > 1 else []
    if side is not None:
        scratch_shapes += [pltpu.SemaphoreType.DMA((side_copies,)), pltpu.SemaphoreType.DMA((side_copies,))]
    res = pl.pallas_call(
        body, out_shape=[*out_shapes, *side_shapes], grid=grid, in_specs=[a_spec, b_spec, *extra_specs, *[HBM] * ns_in],
        out_specs=[*out_specs, *[HBM] * ns_out], scratch_shapes=scratch_shapes,
        compiler_params=_params(("arbitrary",) * 3 if side is not None else ("parallel", "parallel", "arbitrary")),
        name=name)(a, b, *extras, *side_ops)
    return res if side is None else (res[:no], res[no:])


def _blk(n, pref):
    return pref if n % pref == 0 else n


def _kblk(k):
    return k if k <= 2048 else next(b for b in (2048, 1024, 512) if k % b == 0)


def _mm_fwd(name, a, w, col0, ncols, out_dtypes, epilogue=None, extras=(), extra_specs_fn=None, bm=1024, bn=1024):
    m, k = a.shape
    bm, bn = _blk(m, bm), _blk(ncols, bn)
    bk = _kblk(k)
    nk = k // bk
    cb0 = col0 // bn
    grid = (m // bm, ncols // bn, nk)
    a_spec = pl.BlockSpec((bm, bk), lambda i, j, kk: (i, kk))
    b_spec = pl.BlockSpec((bk, bn), lambda i, j, kk: (kk, cb0 + j))
    shapes = [jax.ShapeDtypeStruct((m, ncols), dt) for dt in out_dtypes]
    specs = [pl.BlockSpec((bm, bn), lambda i, j, kk: (i, j)) for _ in out_dtypes]
    ex_specs = extra_specs_fn(bm, bn) if extra_specs_fn else ()
    return _mm(name, a, a_spec, w, b_spec, NN, grid, shapes, specs, (bm, bn), epilogue, extras, ex_specs)


def _mm_nt(name, a, w, out_dtypes, epilogue=None, extras=(), extra_specs_fn=None, part=(0, 1), side=None, bm=1024, bn=1024):
    n = w.shape[0]
    if a.ndim == 3:
        seg_cols, m, k = a.shape[2], a.shape[1], a.shape[0] * a.shape[2]
    else:
        m, k = a.shape
    m = m // part[1]
    bm, bn = _blk(m, bm), _blk(n, bn)
    bk = _kblk(k)
    grid = (m // bm, n // bn, k // bk)
    i0 = part[0] * (m // bm)
    if a.ndim == 3:
        per = seg_cols // bk
        a_spec = pl.BlockSpec((None, bm, bk), lambda i, j, kk: (kk // per, i0 + i, kk % per))
    else:
        a_spec = pl.BlockSpec((bm, bk), lambda i, j, kk: (i0 + i, kk))
    b_spec = pl.BlockSpec((bn, bk), lambda i, j, kk: (j, kk))
    shapes = [jax.ShapeDtypeStruct((m, n), dt) for dt in out_dtypes]
    specs = [pl.BlockSpec((bm, bn), lambda i, j, kk: (i, j)) for _ in out_dtypes]
    ex_specs = extra_specs_fn(bm, bn) if extra_specs_fn else ()
    return _mm(name, a, a_spec, w, b_spec, NT, grid, shapes, specs, (bm, bn), epilogue, extras, ex_specs, side)


def _mm_tn(name, a, b, part=(0, 1), side=None, bm=1024, bn=2048):
    t, m = a.shape
    n = b.shape[1] if b.ndim == 2 else b.shape[0] * b.shape[2]
    m = m // part[1]
    bm, bn = _blk(m, bm), _blk(n, bn)
    grid = (m // bm, n // bn, 1)
    i0 = part[0] * (m // bm)
    a_spec = pl.BlockSpec((t, bm), lambda i, j, kk: (0, i0 + i))
    if b.ndim == 3:
        per = b.shape[2] // bn
        b_spec = pl.BlockSpec((None, t, bn), lambda i, j, kk: (j // per, 0, j % per))
    else:
        b_spec = pl.BlockSpec((t, bn), lambda i, j, kk: (0, j))
    res = _mm(name, a, a_spec, b, b_spec, TN, grid, [jax.ShapeDtypeStruct((m, n), BF16)],
              [pl.BlockSpec((bm, bn), lambda i, j, kk: (i, j))], (bm, bn), side=side)
    return res[0] if side is None else (res[0][0], res[1])


ROWS = 256


def _row_spec(d):
    return pl.BlockSpec((ROWS, d), lambda i: (i, 0))


def _vec_spec(d, rows=1):
    return pl.BlockSpec((rows, d), lambda i: (0, 0))


def _rms_fwd(name, x, g):
    s, d = x.shape

    def body(x_ref, g_ref, o_ref):
        xv = x_ref[...]
        r = lax.rsqrt(jnp.mean(xv * xv, axis=-1, keepdims=True) + NORM_EPS)
        o_ref[...] = (xv * r * g_ref[...]).astype(BF16)

    return pl.pallas_call(body, out_shape=jax.ShapeDtypeStruct((s, d), BF16), grid=(s // ROWS,),
                          in_specs=[_row_spec(d), _vec_spec(d)], out_specs=_row_spec(d),
                          compiler_params=_params(("parallel",)), name=name)(x, g)


def _rms_bwd_math(xv, g, dy):
    r = lax.rsqrt(jnp.mean(xv * xv, axis=-1, keepdims=True) + NORM_EPS)
    n = xv * r
    z = dy * g
    dx = r * (z - n * jnp.mean(z * n, axis=-1, keepdims=True))
    return dx, jnp.sum(dy * n, axis=0, keepdims=True)


def _rms_bwd(name, x, g, dy, resid):
    s, d = x.shape

    def body(x_ref, g_ref, dy_ref, r_ref, dx_ref, dxb_ref, dg_ref):
        dx, dg = _rms_bwd_math(x_ref[...], g_ref[...], dy_ref[...])
        dx = dx + r_ref[...]
        dx_ref[...] = dx
        dxb_ref[...] = dx.astype(BF16)

        @pl.when(pl.program_id(0) == 0)
        def _():
            dg_ref[...] = jnp.zeros_like(dg_ref)

        dg_ref[...] += dg

    return pl.pallas_call(
        body, out_shape=[jax.ShapeDtypeStruct((s, d), F32), jax.ShapeDtypeStruct((s, d), BF16), jax.ShapeDtypeStruct((1, d), F32)],
        grid=(s // ROWS,), in_specs=[_row_spec(d), _vec_spec(d), _row_spec(d), _row_spec(d)],
        out_specs=[_row_spec(d), _row_spec(d), _vec_spec(d)], compiler_params=_params(("arbitrary",)), name=name)(x, g, dy, resid)


def _final_loss(h2, tgt, g):
    s, d = h2.shape

    def body(x_ref, t_ref, g_ref, dx_ref, dxb_ref, dg_ref, ls_ref):
        xv, gv = x_ref[...], g_ref[...]
        r = lax.rsqrt(jnp.mean(xv * xv, axis=-1, keepdims=True) + NORM_EPS)
        diff = xv * r * gv - t_ref[...]
        dx, dg = _rms_bwd_math(xv, gv, diff * (1.0 / d))
        dx_ref[...] = dx
        dxb_ref[...] = dx.astype(BF16)

        @pl.when(pl.program_id(0) == 0)
        def _():
            dg_ref[...] = jnp.zeros_like(dg_ref)
            ls_ref[...] = jnp.zeros_like(ls_ref)

        dg_ref[...] += dg
        ls_ref[...] += jnp.sum(diff * diff, axis=0, keepdims=True)

    return pl.pallas_call(
        body, out_shape=[jax.ShapeDtypeStruct((s, d), F32), jax.ShapeDtypeStruct((s, d), BF16),
                         jax.ShapeDtypeStruct((1, d), F32), jax.ShapeDtypeStruct((1, d), F32)],
        grid=(s // ROWS,), in_specs=[_row_spec(d), _row_spec(d), _vec_spec(d)],
        out_specs=[_row_spec(d), _row_spec(d), _vec_spec(d), _vec_spec(d)],
        compiler_params=_params(("arbitrary",)), name="final_norm_loss")(h2, tgt, g)


ATTN_Q = 128


ATTN_BATCH = 8


def _attn_units(s):
    units = []
    for gi, d in enumerate(DILATIONS):
        for r in range(d):
            for q0 in range(0, s // d, ATTN_Q):
                k0 = max(q0 - SPAN, 0)
                units.append((gi, d, r, q0, k0, q0 + ATTN_Q - k0))
    return units


def _stream_rows(d, r, start, size):
    return pl.ds(r + start * d, size) if d == 1 else pl.ds(r + start * d, size, stride=d)


def _attn_bias(bias_ref, slope):
    qi = lax.broadcasted_iota(jnp.int32, (ATTN_Q, ATTN_Q + SPAN), 0)
    kj = lax.broadcasted_iota(jnp.int32, (ATTN_Q, ATTN_Q + SPAN), 1)
    dist = SPAN + qi - kj
    valid = (dist >= 0) & (dist <= SPAN)
    for gi, d in enumerate(DILATIONS):
        bias_ref[gi] = jnp.where(valid, -(slope * d) * dist.astype(F32), NEG)


def _attn_scores(q_ref, k_ref, bias_ref, gi, d, r, q0, k0, nk):
    qrows, krows = _stream_rows(d, r, q0, ATTN_Q), _stream_rows(d, r, k0, nk)
    qb = q_ref[qrows, :].astype(BF16)
    kb = k_ref[krows, :].astype(BF16)
    sc = lax.dot_general(qb, kb, NT, preferred_element_type=F32) * (HEAD ** -0.5) + bias_ref[gi, :, pl.ds(ATTN_Q + SPAN - nk, nk)]
    return sc, qb, kb, qrows, krows


def _attn_fwd(proj, dm, slopes):
    s = proj.shape[0]
    units = _attn_units(s)

    def body(sl_ref, q_ref, k_ref, v_ref, att_ref, lse_ref, *scr):
        o_scr, l_scr, bias = scr[:3], scr[3:6], scr[6]
        _attn_bias(bias, sl_ref[pl.program_id(0)])
        for first in range(0, len(units), ATTN_BATCH):
            batch = units[first:first + ATTN_BATCH]
            scored = [_attn_scores(q_ref, k_ref, bias, gi, d, r, q0, k0, nk) for gi, d, r, q0, k0, nk in batch]
            soft = []
            for sc, _, _, _, _ in scored:
                m = jnp.max(sc, axis=-1, keepdims=True)
                p = jnp.exp(sc - m)
                soft.append((m, p, jnp.sum(p, axis=-1, keepdims=True)))
            outs = [lax.dot_general(p.astype(BF16), v_ref[sco[4], :].astype(BF16), NN, preferred_element_type=F32)
                    for (m, p, l), sco in zip(soft, scored)]
            for (gi, *_), (m, p, l), sco, o in zip(batch, soft, scored, outs):
                o_scr[gi][sco[3], :] = o / l
                l_scr[gi][sco[3], :] = jnp.broadcast_to(m + jnp.log(l), (ATTN_Q, HEAD))
        l0, l1, l2 = l_scr[0][...], l_scr[1][...], l_scr[2][...]
        m = jnp.maximum(jnp.maximum(l0, l1), l2)
        w0, w1, w2 = jnp.exp(l0 - m), jnp.exp(l1 - m), jnp.exp(l2 - m)
        tot = w0 + w1 + w2
        att_ref[...] = ((w0 * o_scr[0][...] + w1 * o_scr[1][...] + w2 * o_scr[2][...]) / tot).astype(BF16)
        lse_ref[...] = m + jnp.log(tot)

    def seg(i):
        return pl.BlockSpec((s, HEAD), lambda h: (0, i * (dm // HEAD) + h))

    col = pl.BlockSpec((s, HEAD), lambda h: (0, h))
    return pl.pallas_call(
        body, out_shape=[jax.ShapeDtypeStruct((s, dm), BF16), jax.ShapeDtypeStruct((s, dm), F32)], grid=(dm // HEAD,),
        in_specs=[pl.BlockSpec(memory_space=pltpu.SMEM), seg(0), seg(1), seg(2)], out_specs=[col, col],
        scratch_shapes=[pltpu.VMEM((s, HEAD), F32)] * (2 * len(DILATIONS)) + [pltpu.VMEM((len(DILATIONS), ATTN_Q, ATTN_Q + SPAN), F32)],
        compiler_params=_params(("parallel",)), name="attn_fwd")(slopes, proj, proj, proj)


def _attn_bwd(proj, dm, datt, att, lse, slopes, others):
    s = proj.shape[0]
    units = _attn_units(s)

    def body(sl_ref, q_ref, k_ref, v_ref, do_ref, att_ref, lse_ref, o3, o4, o5, o6, out_ref, dq_scr, dk_scr, dv_scr, dl_scr, bias):
        _attn_bias(bias, sl_ref[pl.program_id(0)])
        delta = jnp.sum(do_ref[...] * att_ref[...].astype(F32), axis=-1, keepdims=True)
        dl_scr[...] = jnp.broadcast_to(delta, (s, HEAD))
        dq_scr[...] = jnp.zeros_like(dq_scr)
        dk_scr[...] = jnp.zeros_like(dk_scr)
        dv_scr[...] = jnp.zeros_like(dv_scr)
        for first in range(0, len(units), ATTN_BATCH):
            scored = [_attn_scores(q_ref, k_ref, bias, gi, d, r, q0, k0, nk) for gi, d, r, q0, k0, nk in units[first:first + ATTN_BATCH]]
            dobs = [do_ref[sco[3], :].astype(BF16) for sco in scored]
            dps = [lax.dot_general(dob, v_ref[sco[4], :].astype(BF16), NT, preferred_element_type=F32) for dob, sco in zip(dobs, scored)]
            ps = [jnp.exp(sco[0] - lse_ref[sco[3], :][:, 0:1]) for sco in scored]
            dss = [(p * (dp - dl_scr[sco[3], :][:, 0:1]) * (HEAD ** -0.5)).astype(BF16) for p, dp, sco in zip(ps, dps, scored)]
            dqs = [lax.dot_general(ds, sco[2], NN, preferred_element_type=F32) for ds, sco in zip(dss, scored)]
            dks = [lax.dot_general(ds, sco[1], TN, preferred_element_type=F32) for ds, sco in zip(dss, scored)]
            dvs = [lax.dot_general(p.astype(BF16), dob, TN, preferred_element_type=F32) for p, dob in zip(ps, dobs)]
            for sco, dq, dk, dv in zip(scored, dqs, dks, dvs):
                dq_scr[sco[3], :] += dq
                dk_scr[sco[4], :] += dk
                dv_scr[sco[4], :] += dv
        for j, scr in enumerate((dq_scr, dk_scr, dv_scr)):
            out_ref[j] = scr[...].astype(BF16)
        for j, other in enumerate((o3, o4, o5, o6)):
            out_ref[3 + j] = other[...]

    def seg(i):
        return pl.BlockSpec((s, HEAD), lambda h: (0, i * (dm // HEAD) + h))

    col = pl.BlockSpec((s, HEAD), lambda h: (0, h))
    return pl.pallas_call(
        body, out_shape=jax.ShapeDtypeStruct((7, s, dm), BF16), grid=(dm // HEAD,),
        in_specs=[pl.BlockSpec(memory_space=pltpu.SMEM), seg(0), seg(1), seg(2), col, col, col, col, col, col, col],
        out_specs=pl.BlockSpec((7, s, HEAD), lambda h: (0, 0, h)),
        scratch_shapes=[pltpu.VMEM((s, HEAD), F32)] * 4 + [pltpu.VMEM((len(DILATIONS), ATTN_Q, ATTN_Q + SPAN), F32)],
        compiler_params=_params(("parallel",)), name="attn_bwd")(slopes, proj, proj, proj, datt, att, lse, *others)


VEC_CB, VEC_BA, VEC_BX, VEC_LAM = 0, 1, 2, 3
SEG_Q, SEG_K, SEG_V, SEG_X, SEG_GATE, SEG_GA, SEG_GL = range(7)


def _softplus(z):
    return jnp.maximum(z, 0.0) + jnp.log1p(jnp.exp(-jnp.abs(z)))


def _gate_math(xc, wa_ref, wx_ref, vec):
    xcb = xc.astype(BF16)
    nh = xc.shape[1] // HEAD
    pre_a = jnp.concatenate([jnp.dot(xcb[:, h * HEAD:(h + 1) * HEAD], wa_ref[h], preferred_element_type=F32) for h in range(nh)], axis=1)
    pre_x = jnp.concatenate([jnp.dot(xcb[:, h * HEAD:(h + 1) * HEAD], wx_ref[h], preferred_element_type=F32) for h in range(nh)], axis=1)
    ra = _sigmoid(pre_a + vec[VEC_BA:VEC_BA + 1])
    ig = _sigmoid(pre_x + vec[VEC_BX:VEC_BX + 1])
    sp = _softplus(-vec[VEC_LAM:VEC_LAM + 1])
    log_a = -LRU_C * ra * sp
    a = jnp.exp(log_a)
    z = 2.0 * log_a
    one_minus_a2 = jnp.where(z > -0.01, -z * (1.0 + z * (0.5 + z * (1.0 / 6.0))), 1.0 - jnp.exp(z))
    mult = jnp.sqrt(one_minus_a2)
    return dict(xcb=xcb, ra=ra, ig=ig, sp=sp, a=a, mult=mult)


def _conv_pad_prev(pad_ref, cur, halo, first):
    pad_ref[0:SUBLANES, :] = jnp.where(first, 0.0, halo)
    pad_ref[SUBLANES:SUBLANES + cur.shape[0], :] = cur


def _shift_rows(x, s, fill, up=False):
    rid = lax.broadcasted_iota(jnp.int32, x.shape, 0)
    if up:
        return jnp.where(rid < SUBLANES - s, pltpu.roll(x, SUBLANES - s, axis=0), fill)
    return jnp.where(rid >= s, pltpu.roll(x, s, axis=0), fill)


def _lru_fwd(proj, d, cw8, vec8, wa, wx):
    s = proj.shape[0]
    hb = ROWS // SUBLANES

    def body(x_ref, halo_ref, g_ref, cw_ref, vec_ref, wa_ref, wx_ref, xc_ref, hp_ref, h2_ref, y_ref, pad, a_scr, u_scr, carry):
        @pl.when(pl.program_id(0) == 0)
        def _():
            carry[...] = jnp.zeros_like(carry)

        _conv_pad_prev(pad, x_ref[...], halo_ref[...], pl.program_id(0) == 0)
        vec = vec_ref[...]
        xc = vec[VEC_CB:VEC_CB + 1]
        for k in range(CONV_TAPS):
            xc = xc + cw_ref[k:k + 1, :] * pad[pl.ds(SUBLANES - (CONV_TAPS - 1) + k, ROWS), :]
        gm = _gate_math(xc, wa_ref, wx_ref, vec)
        xc_ref[...] = xc
        a_scr[...] = gm["a"]
        u_scr[...] = gm["mult"] * (gm["ig"] * xc)

        def group(gi, before):
            rows = pl.ds(pl.multiple_of(gi * SUBLANES, SUBLANES), SUBLANES)
            ca, cb = a_scr[rows, :], u_scr[rows, :]
            for sh in (1, 2, 4):
                cb = ca * _shift_rows(cb, sh, 0.0) + cb
                ca = ca * _shift_rows(ca, sh, 1.0)
            h = cb + ca * before
            hp_ref[rows, :] = jnp.where(lax.broadcasted_iota(jnp.int32, h.shape, 0) == 0, before, pltpu.roll(h, 1, axis=0))
            h2_ref[rows, :] = h.astype(BF16)
            y_ref[rows, :] = (h * _gelu(g_ref[rows, :])).astype(BF16)
            return jnp.broadcast_to(h[SUBLANES - 1:SUBLANES, :], h.shape)

        carry[...] = lax.fori_loop(0, ROWS // SUBLANES, group, carry[...], unroll=2)

    wspec = pl.BlockSpec(wa.shape, lambda i: (0, 0, 0))
    return pl.pallas_call(
        body, out_shape=[jax.ShapeDtypeStruct((s, d), F32)] * 2 + [jax.ShapeDtypeStruct((s, d), BF16)] * 2, grid=(s // ROWS,),
        in_specs=[pl.BlockSpec((ROWS, d), lambda i: (i, SEG_X)),
                  pl.BlockSpec((SUBLANES, d), lambda i: (jnp.maximum(i * hb - 1, 0), SEG_X)),
                  pl.BlockSpec((ROWS, d), lambda i: (i, SEG_GATE)),
                  _vec_spec(d, SUBLANES), _vec_spec(d, SUBLANES), wspec, wspec],
        out_specs=[_row_spec(d)] * 4,
        scratch_shapes=[pltpu.VMEM((ROWS + SUBLANES, d), F32), pltpu.VMEM((ROWS, d), F32), pltpu.VMEM((ROWS, d), F32), pltpu.VMEM((SUBLANES, d), F32)],
        compiler_params=_params(("arbitrary",)), name="lru_fwd")(proj, proj, proj, cw8, vec8, wa, wx)


def _lru_bwd(hp, dh, xc, proj, cw8, wa, wx, vec8):
    s, d = xc.shape
    nh = d // HEAD
    nb = s // ROWS
    hb = ROWS // SUBLANES

    def body(hp_ref, dh_ref, xc_ref, x_ref, xprev_ref, cw_ref, wa_ref, wx_ref, vec_ref, dx_ref, dwa_ref, dwx_ref, dvec_ref, dcw_ref,
             a_scr, g_scr, da_scr, carry, padd, padx, dnext):
        step = pl.program_id(0)

        @pl.when(step == 0)
        def _():
            carry[...] = jnp.zeros_like(carry)
            dnext[...] = jnp.zeros_like(dnext)
            dwa_ref[...] = jnp.zeros_like(dwa_ref)
            dwx_ref[...] = jnp.zeros_like(dwx_ref)
            dvec_ref[...] = jnp.zeros_like(dvec_ref)
            dcw_ref[...] = jnp.zeros_like(dcw_ref)

        xc_v, vec = xc_ref[...], vec_ref[...]
        gm = _gate_math(xc_v, wa_ref, wx_ref, vec)
        ra, ig, sp, a, mult = gm["ra"], gm["ig"], gm["sp"], gm["a"], gm["mult"]
        a_scr[...] = a

        def group(j, after):
            rows = pl.ds(pl.multiple_of((ROWS // SUBLANES - 1 - j) * SUBLANES, SUBLANES), SUBLANES)
            ca, dhv = a_scr[rows, :], dh_ref[rows, :]
            cb = ca * dhv
            for sh in (1, 2, 4):
                cb = ca * _shift_rows(cb, sh, 0.0, up=True) + cb
                ca = ca * _shift_rows(ca, sh, 1.0, up=True)
            c = cb + ca * after
            last = lax.broadcasted_iota(jnp.int32, c.shape, 0) == SUBLANES - 1
            g = dhv + jnp.where(last, after, pltpu.roll(c, SUBLANES - 1, axis=0))
            g_scr[rows, :] = g
            da_scr[rows, :] = g * hp_ref[rows, :]
            return jnp.broadcast_to(c[0:1, :], c.shape)

        carry[...] = lax.fori_loop(0, ROWS // SUBLANES, group, carry[...], unroll=2)
        du, da = g_scr[...], da_scr[...]
        dmult = du * ig * xc_v
        dlog_a = da * a - dmult * (a * a) / mult
        dpre_a = dlog_a * (-LRU_C * sp) * ra * (1.0 - ra)
        dpre_x = du * mult * xc_v * ig * (1.0 - ig)
        dlam = jnp.sum(dlog_a * (-LRU_C * ra), axis=0, keepdims=True) * (-_sigmoid(-vec[VEC_LAM:VEC_LAM + 1]))
        dvec_ref[VEC_BA:VEC_BA + 1, :] += jnp.sum(dpre_a, axis=0, keepdims=True)
        dvec_ref[VEC_BX:VEC_BX + 1, :] += jnp.sum(dpre_x, axis=0, keepdims=True)
        dvec_ref[VEC_LAM:VEC_LAM + 1, :] += dlam
        dab, dxb, xcb = dpre_a.astype(BF16), dpre_x.astype(BF16), gm["xcb"]
        back = []
        for h in range(nh):
            cols = slice(h * HEAD, (h + 1) * HEAD)
            dwa_ref[h] += lax.dot_general(xcb[:, cols], dab[:, cols], TN, preferred_element_type=F32)
            dwx_ref[h] += lax.dot_general(xcb[:, cols], dxb[:, cols], TN, preferred_element_type=F32)
            back.append(lax.dot_general(dab[:, cols], wa_ref[h], NT, preferred_element_type=F32)
                        + lax.dot_general(dxb[:, cols], wx_ref[h], NT, preferred_element_type=F32))
        dc = du * mult * ig + jnp.concatenate(back, axis=1)
        padd[0:ROWS, :] = dc
        padd[ROWS:ROWS + SUBLANES, :] = dnext[...]
        dnext[...] = dc[0:SUBLANES, :]
        _conv_pad_prev(padx, x_ref[...], xprev_ref[...], step == nb - 1)
        dx = jnp.zeros_like(dc)
        for k in range(CONV_TAPS):
            dx = dx + cw_ref[k:k + 1, :] * padd[pl.ds(CONV_TAPS - 1 - k, ROWS), :]
            dcw_ref[k:k + 1, :] += jnp.sum(dc * padx[pl.ds(SUBLANES - (CONV_TAPS - 1) + k, ROWS), :], axis=0, keepdims=True)
        dcw_ref[CONV_TAPS:CONV_TAPS + 1, :] += jnp.sum(dc, axis=0, keepdims=True)
        dx_ref[...] = dx.astype(BF16)

    rows_rev = pl.BlockSpec((ROWS, d), lambda i: (nb - 1 - i, 0))
    wspec = pl.BlockSpec(wa.shape, lambda i: (0, 0, 0))
    return pl.pallas_call(
        body, out_shape=[jax.ShapeDtypeStruct((s, d), BF16), jax.ShapeDtypeStruct(wa.shape, F32), jax.ShapeDtypeStruct(wa.shape, F32),
                         jax.ShapeDtypeStruct((SUBLANES, d), F32), jax.ShapeDtypeStruct((SUBLANES, d), F32)],
        grid=(nb,),
        in_specs=[rows_rev, rows_rev, rows_rev, pl.BlockSpec((ROWS, d), lambda i: (nb - 1 - i, SEG_X)),
                  pl.BlockSpec((SUBLANES, d), lambda i: (jnp.maximum((nb - 1 - i) * hb - 1, 0), SEG_X)),
                  _vec_spec(d, SUBLANES), wspec, wspec, _vec_spec(d, SUBLANES)],
        out_specs=[rows_rev, wspec, wspec, _vec_spec(d, SUBLANES), _vec_spec(d, SUBLANES)],
        scratch_shapes=[pltpu.VMEM((ROWS, d), F32)] * 3 + [pltpu.VMEM((SUBLANES, d), F32)]
        + [pltpu.VMEM((ROWS + SUBLANES, d), F32)] * 2 + [pltpu.VMEM((SUBLANES, d), F32)],
        compiler_params=_params(("arbitrary",)), name="lru_bwd")(hp, dh, xc, proj, proj, cw8, wa, wx, vec8)


def _coords():
    return lax.axis_index("x"), lax.axis_index("y"), lax.axis_index("c")


def _other_chips(x, y):
    return [(1 - x, y), (x, 1 - y), (1 - x, 1 - y)]


def _slab(ref, kind, shard_shape, idx, half=None):
    r, c = shard_shape
    r0, nr = (0, r) if half is None else (half * (r // 2), r // 2)
    if kind == "col":
        return ref.at[pl.ds(r0, nr), pl.ds(pl.multiple_of(idx * c, LANES), c)]
    if kind == "row":
        return ref.at[pl.ds(pl.multiple_of(idx * r, SUBLANES) + r0, nr), :]
    return ref.at[idx, pl.ds(r0, nr), :]


def _full_shape(shard_shape, kind):
    r, c = shard_shape
    return {"col": (r, c * N_DEV), "row": (r * N_DEV, c), "slot": (N_DEV, r, c)}[kind]


def _handshake(peers):
    barrier = pltpu.get_barrier_semaphore()
    for peer in peers:
        pl.semaphore_signal(barrier, inc=1, device_id=peer, device_id_type=MESH)
    pl.semaphore_wait(barrier, len(peers))


def _launch(name, body, out_shape, operands, sems, sequencer_id):
    if sequencer_id is None:
        return pl.pallas_call(body, out_shape=out_shape, in_specs=[HBM] * len(operands), out_specs=[HBM] * len(out_shape),
                              scratch_shapes=sems, name=name)(*operands)
    return pl.kernel(body, out_type=out_shape, mesh=plsc.ScalarSubcoreMesh(axis_name="seq", num_cores=1), name=name,
                     scratch_types=sems, compiler_params=pltpu.CompilerParams(collective_id=sequencer_id))(*operands)


AG_COPIES = 10


def _all_gather(name, shards, kinds, sequencer_id=None):
    n = len(shards)
    shapes = [s.shape for s in shards]

    def body(*refs):
        ins, outs = refs[:n], refs[n:2 * n]
        send_sems, recv_sems, local_sems = refs[2 * n:]
        x, y, c = _coords()
        me, sib, xn, yn, dg = (x, y, c), (x, y, 1 - c), (1 - x, y, c), (x, 1 - y, c), (1 - x, 1 - y, c)
        if sequencer_id is not None:
            _handshake([sib, xn, yn])

        def part(i, dev, half=None):
            return _slab(outs[i], kinds[i], shapes[i], 4 * dev[0] + 2 * dev[1] + dev[2], half)

        def copy(i, k, block, half, to, own=False):
            r = shapes[i][0]
            src = part(i, block, half) if not own else (ins[i] if half is None else ins[i].at[pl.ds(half * (r // 2), r // 2), :])
            return pltpu.make_async_remote_copy(
                src_ref=src, dst_ref=part(i, block, half), send_sem=send_sems.at[AG_COPIES * i + k],
                recv_sem=recv_sems.at[AG_COPIES * i + k], device_id=to, device_id_type=MESH)

        def other_core(dev):
            return (dev[0], dev[1], 1 - c)

        started = []

        def start(cp):
            cp.start()
            started.append(cp)

        for i in range(n):
            start(copy(i, 1, me, 0, xn, own=True))
            start(copy(i, 4, me, 1, yn, own=True))
            start(copy(i, 2, me, 1, xn, own=True))
            start(copy(i, 3, me, 0, yn, own=True))
            start(copy(i, 0, me, None, sib, own=True))
        mine = [pltpu.make_async_copy(ins[i], part(i, me), local_sems.at[i]) for i in range(n)]
        for cp in mine:
            cp.start()
        for i in range(n):
            copy(i, 1, xn, 0, me).wait_recv()
            start(copy(i, 5, xn, 0, yn))
            copy(i, 4, yn, 1, me).wait_recv()
            start(copy(i, 6, yn, 1, xn))
        for i in range(n):
            copy(i, 2, xn, 1, me).wait_recv()
            start(copy(i, 7, xn, None, sib))
            copy(i, 3, yn, 0, me).wait_recv()
            start(copy(i, 8, yn, None, sib))
        for i in range(n):
            copy(i, 5, dg, 0, me).wait_recv()
            copy(i, 6, dg, 1, me).wait_recv()
            start(copy(i, 9, dg, None, sib))
        for i in range(n):
            copy(i, 0, sib, None, me).wait_recv()
            for k, dev in ((7, xn), (8, yn), (9, dg)):
                copy(i, k, other_core(dev), None, me).wait_recv()
        for cp in started:
            cp.wait_send()
        for cp in mine:
            cp.wait()

    out_shape = [jax.ShapeDtypeStruct(_full_shape(s.shape, k), s.dtype) for s, k in zip(shards, kinds)]
    sems = [pltpu.SemaphoreType.DMA((AG_COPIES * n,)), pltpu.SemaphoreType.DMA((AG_COPIES * n,)), pltpu.SemaphoreType.DMA((n,))]
    return _launch(name, body, out_shape, shards, sems, sequencer_id)


def _sibling_copies(kinds, shard_shapes):
    def make(ins, outs, send_sems, recv_sems):
        x, y, c = _coords()
        return [pltpu.make_async_remote_copy(
            src_ref=_slab(ins[i], kinds[i], shard_shapes[i], 2 * q + (1 - c)), dst_ref=outs[i].at[q],
            send_sem=send_sems.at[N_CHIP * i + q], recv_sem=recv_sems.at[N_CHIP * i + q],
            device_id=(x, y, 1 - c), device_id_type=MESH) for i in range(len(ins)) for q in range(N_CHIP)]
    return make


def _sibling_side(partials, kinds, shard_shapes):
    return (partials, [jax.ShapeDtypeStruct((N_CHIP, *s), BF16) for s in shard_shapes], N_CHIP * len(partials),
            _sibling_copies(kinds, shard_shapes))


def _exchange_chips(name, chip_sums, sequencer_id=None):
    n = len(chip_sums)

    def body(*refs):
        ins, outs = refs[:n], refs[n:2 * n]
        send_sems, recv_sems = refs[2 * n:]
        x, y, c = _coords()
        if sequencer_id is not None:
            _handshake([(cx, cy, c) for cx, cy in _other_chips(x, y)])
        cps = []
        for i in range(n):
            for k, (cx, cy) in enumerate(_other_chips(x, y)):
                cps.append(pltpu.make_async_remote_copy(
                    src_ref=ins[i].at[2 * cx + cy], dst_ref=outs[i].at[k], send_sem=send_sems.at[3 * i + k],
                    recv_sem=recv_sems.at[3 * i + k], device_id=(cx, cy, c), device_id_type=MESH))
        for cp in cps:
            cp.start()
        for cp in cps:
            cp.wait()

    return _launch(name, body, [jax.ShapeDtypeStruct((3, *t.shape[1:]), BF16) for t in chip_sums], chip_sums,
                   [pltpu.SemaphoreType.DMA((3 * n,)), pltpu.SemaphoreType.DMA((3 * n,))], sequencer_id)


def _all_peers(x, y, c):
    return [(x ^ (k >> 2), y ^ ((k >> 1) & 1), c ^ (k & 1)) for k in range(1, N_DEV)]


def _small_scatter(name, packed, sequencer_id):
    rows = packed.shape[0] // N_DEV

    def body(p_ref, rb_ref, send_sems, recv_sems, local_sem):
        x, y, c = _coords()
        me = 4 * x + 2 * y + c
        peers = _all_peers(x, y, c)
        _handshake(peers)

        def piece(idx):
            return p_ref.at[pl.ds(pl.multiple_of(idx * rows, SUBLANES), rows), :]

        cps = [pltpu.make_async_remote_copy(src_ref=piece(me ^ k), dst_ref=rb_ref.at[k], send_sem=send_sems.at[k], recv_sem=recv_sems.at[k],
                                            device_id=peers[k - 1], device_id_type=MESH) for k in range(1, N_DEV)]
        for cp in cps:
            cp.start()
        mine = pltpu.make_async_copy(piece(me), rb_ref.at[0], local_sem)
        mine.start()
        for cp in cps:
            cp.wait()
        mine.wait()

    return _launch(name, body, [jax.ShapeDtypeStruct((N_DEV, rows, LANES), F32)], [packed],
                   [pltpu.SemaphoreType.DMA((N_DEV,)), pltpu.SemaphoreType.DMA((N_DEV,)), pltpu.SemaphoreType.DMA], sequencer_id)[0]


def _small_sum(name, pieces):
    def body(p_ref, o_ref):
        acc = p_ref[0]
        for k in range(1, N_DEV):
            acc = acc + p_ref[k]
        o_ref[...] = acc

    vm = pl.BlockSpec(memory_space=pltpu.VMEM)
    return pl.pallas_call(body, out_shape=jax.ShapeDtypeStruct(pieces.shape[1:], F32), in_specs=[vm], out_specs=vm, name=name)(pieces)


def _small_gather(name, tot, sequencer_id):
    rows = tot.shape[0]

    def body(t_ref, out_ref, send_sems, recv_sems, local_sem):
        x, y, c = _coords()
        me = 4 * x + 2 * y + c
        peers = _all_peers(x, y, c)
        _handshake(peers)

        def piece(idx):
            return out_ref.at[pl.ds(pl.multiple_of(idx * rows, SUBLANES), rows), :]

        cps = [pltpu.make_async_remote_copy(src_ref=t_ref, dst_ref=piece(me), send_sem=send_sems.at[k], recv_sem=recv_sems.at[k],
                                            device_id=peers[k - 1], device_id_type=MESH) for k in range(1, N_DEV)]
        for cp in cps:
            cp.start()
        mine = pltpu.make_async_copy(t_ref, piece(me), local_sem)
        mine.start()
        for cp in cps:
            cp.wait()
        mine.wait()

    return _launch(name, body, [jax.ShapeDtypeStruct((N_DEV * rows, LANES), F32)], [tot],
                   [pltpu.SemaphoreType.DMA((N_DEV,)), pltpu.SemaphoreType.DMA((N_DEV,)), pltpu.SemaphoreType.DMA], sequencer_id)[0]


def _all_reduce_small(name, packed):
    rows = packed.shape[0] // N_DEV

    def body(p_ref, out_ref, rb, tot, send_sems, recv_sems):
        x, y, c = _coords()
        me = 4 * x + 2 * y + c

        def peer(k):
            return (x ^ (k >> 2), y ^ ((k >> 1) & 1), c ^ (k & 1))

        def rows_of(idx):
            return pl.ds(pl.multiple_of(idx * rows, SUBLANES), rows)

        def piece(ref, idx):
            return ref.at[rows_of(idx), :]

        scatter = [pltpu.make_async_remote_copy(src_ref=piece(p_ref, me ^ k), dst_ref=rb.at[k], send_sem=send_sems.at[k],
                                                recv_sem=recv_sems.at[k], device_id=peer(k), device_id_type=MESH) for k in range(1, N_DEV)]
        for cp in scatter:
            cp.start()
        acc = p_ref[rows_of(me), :]
        for cp in scatter:
            cp.wait_recv()
        for k in range(1, N_DEV):
            acc = acc + rb[k]
        tot[...] = acc
        out_ref[rows_of(me), :] = acc
        gather = [pltpu.make_async_remote_copy(src_ref=tot, dst_ref=piece(out_ref, me), send_sem=send_sems.at[N_DEV + k],
                                               recv_sem=recv_sems.at[N_DEV + k], device_id=peer(k), device_id_type=MESH)
                  for k in range(1, N_DEV)]
        for cp in gather:
            cp.start()
        for k in range(1, N_DEV):
            pltpu.make_async_remote_copy(src_ref=tot, dst_ref=piece(out_ref, me ^ k), send_sem=send_sems.at[N_DEV + k],
                                         recv_sem=recv_sems.at[N_DEV + k], device_id=peer(k), device_id_type=MESH).wait_recv()
        for cp in scatter + gather:
            cp.wait_send()

    vm = pl.BlockSpec(memory_space=pltpu.VMEM)
    return pl.pallas_call(
        body, out_shape=jax.ShapeDtypeStruct(packed.shape, F32), in_specs=[vm], out_specs=vm,
        scratch_shapes=[pltpu.VMEM((N_DEV, rows, LANES), F32), pltpu.VMEM((rows, LANES), F32),
                        pltpu.SemaphoreType.DMA((2 * N_DEV,)), pltpu.SemaphoreType.DMA((2 * N_DEV,))],
        compiler_params=pltpu.CompilerParams(vmem_limit_bytes=VMEM_LIMIT), name=name)(packed)


def _adamw_math(g, w, m, v):
    m = ADAM_B1 * m + (1.0 - ADAM_B1) * g
    v = ADAM_B2 * v + (1.0 - ADAM_B2) * (g * g)
    delta = -ADAM_LR * ((m / ADAM_C1) / (jnp.sqrt(v / ADAM_C2) + ADAM_EPS) + ADAM_WD * w)
    return delta, m, v


def _slab_spec(kind, shard_shape, tr, slab_of):
    r, c = shard_shape
    if kind == "col":
        return pl.BlockSpec((tr, c), lambda q, i, sc: (i, slab_of(q, sc)))
    return pl.BlockSpec((tr, c), lambda q, i, sc: (slab_of(q, sc) * (r // tr) + i, 0))


def _chip_sum(name, partial, recv, kind, shard_shape, core):
    r, c = shard_shape
    tr = _blk(r, 1024)

    def body(core_ref, p_ref, r_ref, o_ref):
        o_ref[...] = (p_ref[...].astype(F32) + r_ref[...].astype(F32)).astype(BF16)

    spec4 = pl.BlockSpec((None, tr, c), lambda q, i, sc: (q, i, 0))
    grid_spec = pltpu.PrefetchScalarGridSpec(
        num_scalar_prefetch=1, grid=(N_CHIP, r // tr),
        in_specs=[_slab_spec(kind, shard_shape, tr, lambda q, sc: 2 * q + sc[0]), spec4], out_specs=spec4)
    return pl.pallas_call(body, out_shape=jax.ShapeDtypeStruct((N_CHIP, r, c), BF16), grid_spec=grid_spec,
                          compiler_params=_params(("parallel", "parallel")), name=name)(core, partial, recv)


def _adamw_shard(name, parts, w, m, v, chip):
    r, c = w.shape
    n_parts = len(parts)
    tr = _blk(r // n_parts, 256)
    per = r // n_parts // tr

    def body(chip_ref, *refs):
        src, (w_ref, m_ref, v_ref), (g_out, d_out, m_out, v_out) = refs[:2 * n_parts], refs[2 * n_parts:2 * n_parts + 3], refs[2 * n_parts + 3:]
        for p in range(n_parts):
            @pl.when(pl.program_id(0) // per == p)
            def _():
                g = src[2 * p][...].astype(F32)
                for k in range(3):
                    g = g + src[2 * p + 1][k].astype(F32)
                g_out[...] = g
                d_out[...], m_out[...], v_out[...] = _adamw_math(g, w_ref[...], m_ref[...], v_ref[...])

    def part_specs(p):
        at = lambda i: jnp.clip(i - p * per, 0, per - 1)
        return [pl.BlockSpec((None, tr, c), lambda i, sc: (sc[0], at(i), 0)), pl.BlockSpec((3, tr, c), lambda i, sc: (0, at(i), 0))]

    blk = pl.BlockSpec((tr, c), lambda i, sc: (i, 0))
    grid_spec = pltpu.PrefetchScalarGridSpec(
        num_scalar_prefetch=1, grid=(r // tr,), in_specs=[s for p in range(n_parts) for s in part_specs(p)] + [blk, blk, blk], out_specs=[blk] * 4)
    return pl.pallas_call(body, out_shape=[jax.ShapeDtypeStruct((r, c), F32)] * 4, grid_spec=grid_spec,
                          compiler_params=_params(("parallel",)), name=name)(chip, *[a for p in parts for a in p], w, m, v)


def _adamw_small(name, g, w, m, v):
    def body(g_ref, w_ref, m_ref, v_ref, d_out, m_out, v_out):
        d_out[...], m_out[...], v_out[...] = _adamw_math(g_ref[...], w_ref[...], m_ref[...], v_ref[...])

    vm = pl.BlockSpec(memory_space=pltpu.VMEM)
    return pl.pallas_call(body, out_shape=[jax.ShapeDtypeStruct(g.shape, F32)] * 3, in_specs=[vm] * 4, out_specs=[vm] * 3,
                          compiler_params=pltpu.CompilerParams(vmem_limit_bytes=VMEM_LIMIT), name=name)(g, w, m, v)


def _pack_rows(arrays, total_rows):
    flat = [a.reshape(-1, LANES) for a in arrays]
    used = sum(f.shape[0] for f in flat)
    return jnp.concatenate(flat + [jnp.zeros((total_rows - used, LANES), F32)], axis=0)


def _unpack_rows(packed, like):
    out, at = [], 0
    for a in like:
        n = a.size // LANES
        out.append(packed[at:at + n].reshape(a.shape))
        at += n
    return out


def kernel(x, norm_mix_g, w_in, conv_w, conv_b, lru_wa, lru_ba, lru_wx, lru_bx, lru_lambda, w_proj_attn, w_proj_lru, w_out, norm_mlp_g, w_up, w_down, norm_final_g, loss_target, m_norm_mix_g, m_w_in, m_conv_w, m_conv_b, m_lru_wa, m_lru_ba, m_lru_wx, m_lru_bx, m_lru_lambda, m_w_proj_attn, m_w_proj_lru, m_w_out, m_norm_mlp_g, m_w_up, m_w_down, m_norm_final_g, v_norm_mix_g, v_w_in, v_conv_w, v_conv_b, v_lru_wa, v_lru_ba, v_lru_wx, v_lru_bx, v_lru_lambda, v_w_proj_attn, v_w_proj_lru, v_w_out, v_norm_mlp_g, v_w_up, v_w_down, v_norm_final_g):
    xs, tgt = x[0], loss_target[0]
    s, d = xs.shape
    nh = d // HEAD
    ix, iy, ic = _coords()
    core = jnp.reshape(ic, (1,)).astype(jnp.int32)
    chip = jnp.reshape(2 * ix + iy, (1,)).astype(jnp.int32)
    dev = 4 * ix + 2 * iy + ic

    big = [w_in[0], w_proj_attn[0], w_proj_lru[0], w_out[0], w_up[0], w_down[0]]
    big_m = [m_w_in[0], m_w_proj_attn[0], m_w_proj_lru[0], m_w_out[0], m_w_up[0], m_w_down[0]]
    big_v = [v_w_in[0], v_w_proj_attn[0], v_w_proj_lru[0], v_w_out[0], v_w_up[0], v_w_down[0]]
    kinds = ["col", "row", "row", "row", "col", "row"]
    pad_taps = lambda t: jnp.pad(t, ((0, SUBLANES - CONV_TAPS), (0, 0)))
    shards = [w.astype(BF16) for w in big]
    pad_taps2 = lambda t: jnp.pad(t, ((0, 2 * SUBLANES - CONV_TAPS), (0, 0)))
    win, cw_slots = _all_gather("all_gather_w_in", [shards[0], pad_taps2(conv_w[0])], ["col", "slot"], sequencer_id=7)
    wpa, wpl, wout = _all_gather("all_gather_mix", shards[1:4], kinds[1:4], sequencer_id=1)
    wup, wdown = _all_gather("all_gather_mlp", shards[4:], kinds[4:], sequencer_id=5)
    cw8 = jnp.transpose(cw_slots[:, :SUBLANES], (1, 0, 2)).reshape(SUBLANES, d)
    row_id = lax.broadcasted_iota(jnp.int32, (SUBLANES, d), 0)
    vec8 = sum(jnp.where(row_id == k, t, 0.0) for k, t in ((VEC_CB, conv_b), (VEC_BA, lru_ba), (VEC_BX, lru_bx), (VEC_LAM, lru_lambda)))
    wa16, wx16 = lru_wa[0].astype(BF16), lru_wx[0].astype(BF16)
    slopes = 2.0 ** (-8.0 * jnp.arange(1, nh + 1, dtype=F32) / nh)

    def seg_specs(*segs):
        return lambda bm, bn: [pl.BlockSpec((bm, bn), (lambda i, j, kk, sg=sg: (i, sg * (d // bn) + j))) for sg in segs]

    def plain_specs(k):
        return lambda bm, bn: [pl.BlockSpec((bm, bn), lambda i, j, kk: (i, j)) for _ in range(k)]

    xn = _rms_fwd("norm_mix", xs, norm_mix_g)
    proj = _mm_fwd("proj_in", xn, win, 0, 7 * d, [F32], bm=2048)[0]
    att, lse = _attn_fwd(proj, d, slopes)
    xc, hp, h2d, ylru = _lru_fwd(proj, d, cw8, vec8, wa16, wx16)
    pa = _mm_fwd("proj_attn", att, wpa, 0, d, [BF16], bm=2048)[0]

    def merge(acc, pa_b, ga, gl):
        return acc, _sigmoid(ga) * pa_b.astype(F32) + _sigmoid(gl) * acc

    plr, merged = _mm_fwd("proj_lru_merge", ylru, wpl, 0, d, [BF16, BF16], merge, (pa, proj, proj),
                          lambda bm, bn: plain_specs(1)(bm, bn) + seg_specs(SEG_GA, SEG_GL)(bm, bn), bn=512)
    h1 = _mm_fwd("mix_out", merged, wout, 0, d, [F32], lambda acc, r: (acc + r,), (xs,), plain_specs(1))[0]
    hn = _rms_fwd("norm_mlp", h1, norm_mlp_g)

    def relu2(acc):
        return acc, jnp.square(jnp.maximum(acc, 0.0))

    up, hid = _mm_fwd("mlp_up", hn, wup, 0, wup.shape[1], [BF16, BF16], relu2, bm=2048)
    h2 = _mm_fwd("mlp_down", hid, wdown, 0, d, [F32], lambda acc, r: (acc + r,), (h1,), plain_specs(1))[0]
    dh2, dh2b, dg3, loss_lanes = _final_loss(h2, tgt, norm_final_g.reshape(1, d))
    loss_rows = jnp.pad((0.5 / d * jnp.sum(loss_lanes)).reshape(1, 1), ((0, SUBLANES - 1), (0, LANES - 1)))

    def reduce_group(tag, kk, shp, partials, from_sibling, sequencer_id):
        sums = [_chip_sum(f"chip_sum_{tag}_{i}", p, f, k, sh, core) for i, (p, f, k, sh) in enumerate(zip(partials, from_sibling, kk, shp))]
        return list(zip(sums, _exchange_chips(f"rs_chips_{tag}", sums, sequencer_id)))

    g_wdown = _mm_tn("mlp_down_dw", hid, dh2b)
    (dup,), sib_down = _mm_nt("mlp_down_dx", dh2b, wdown, [BF16], lambda acc, u: (acc * (2.0 * jnp.maximum(u.astype(F32), 0.0)),), (up,),
                              plain_specs(1), side=_sibling_side([g_wdown], kinds[5:], [big[5].shape]))
    (red_down,) = reduce_group("mlp_down", kinds[5:], [big[5].shape], [g_wdown], sib_down, 2)
    dup = lax.optimization_barrier((dup, red_down[0]))[0]
    g_wup = _mm_tn("mlp_up_dw", hn, dup)
    (dhn,), sib_up = _mm_nt("mlp_up_dx", dup, wup, [F32], side=_sibling_side([g_wup], kinds[4:5], [big[4].shape]))
    (red_up,) = reduce_group("mlp_up", kinds[4:5], [big[4].shape], [g_wup], sib_up, 10)
    dhn = lax.optimization_barrier((dhn, red_up[0]))[0]
    dh1, dh1b, dg2 = _rms_bwd("norm_mlp_bwd", h1, norm_mlp_g, dhn, dh2)

    def merge_bwd(acc, pa_b, pl_b, ga, gl):
        sa, sl = _sigmoid(ga), _sigmoid(gl)
        return acc * sa, acc * sl, acc * pa_b.astype(F32) * sa * (1.0 - sa), acc * pl_b.astype(F32) * sl * (1.0 - sl)

    dpa, dpl, dga, dgl = _mm_nt("mix_out_dx", dh1b, wout, [BF16] * 4, merge_bwd, (pa, plr, proj, proj),
                                lambda bm, bn: plain_specs(2)(bm, bn) + seg_specs(SEG_GA, SEG_GL)(bm, bn), bn=512)
    g_wout = _mm_tn("mix_out_dw", merged, dh1b)
    datt = _mm_nt("proj_attn_dx", dpa, wpa, [F32], bm=2048)[0]
    g_wpa = _mm_tn("proj_attn_dw", att, dpa)

    def lru_out_bwd(acc, h_b, gate):
        return acc * _gelu(gate), acc * h_b.astype(F32) * _gelu_grad(gate)

    g_wpl = _mm_tn("proj_lru_dw", ylru, dpl)
    shp_mix = [w.shape for w in big[1:4]]
    (dh, dxg), sib_mix = _mm_nt("proj_lru_dx", dpl, wpl, [F32, BF16], lru_out_bwd, (h2d, proj),
                                lambda bm, bn: plain_specs(1)(bm, bn) + seg_specs(SEG_GATE)(bm, bn), bn=512,
                                side=_sibling_side([g_wpa, g_wpl, g_wout], kinds[1:4], shp_mix))
    red_pa, red_pl, red_out = reduce_group("mix", kinds[1:4], shp_mix, [g_wpa, g_wpl, g_wout], sib_mix, 3)
    dh = lax.optimization_barrier((dh, red_down[1]))[0]
    dxr, dwa, dwx, dvec, dconv = _lru_bwd(hp, dh, xc, proj, cw8, wa16, wx16, vec8)
    dproj = _attn_bwd(proj, d, datt, att, lse, slopes, (dxr, dxg, dga, dgl))

    def small_step(tag, grads, ws, ms, vs, like, after, seq=None):
        n_rows = sum(g.size for g in grads) // LANES
        per_dev = -(-n_rows // (N_DEV * SUBLANES)) * SUBLANES
        packed = lax.optimization_barrier((_pack_rows(grads, N_DEV * per_dev), after))[0]
        if seq is None:
            total = _all_reduce_small(f"all_reduce_{tag}", packed)
        else:
            pieces = lax.optimization_barrier((_small_scatter(f"scatter_{tag}", packed, seq[0]), seq[2]))[0]
            total = _small_gather(f"gather_{tag}", _small_sum(f"sum_{tag}", pieces), seq[1])
        w_rows = -(-(sum(w.size for w in ws) // LANES) // SUBLANES) * SUBLANES
        upd = _adamw_small(f"adamw_{tag}", total[:w_rows], _pack_rows(ws, w_rows), _pack_rows(ms, w_rows), _pack_rows(vs, w_rows))
        return _unpack_rows(total, like), [_unpack_rows(t, ws) for t in upd]

    early_w = [conv_b, lru_wa, lru_ba, lru_wx, lru_bx, lru_lambda, norm_mlp_g, norm_final_g]
    early_m = [m_conv_b, m_lru_wa, m_lru_ba, m_lru_wx, m_lru_bx, m_lru_lambda, m_norm_mlp_g, m_norm_final_g]
    early_v = [v_conv_b, v_lru_wa, v_lru_ba, v_lru_wx, v_lru_bx, v_lru_lambda, v_norm_mlp_g, v_norm_final_g]
    early_g = [dconv[CONV_TAPS:CONV_TAPS + 1], dwa, dvec[VEC_BA:VEC_BA + 1], dwx, dvec[VEC_BX:VEC_BX + 1],
               dvec[VEC_LAM:VEC_LAM + 1], dg2, dg3, dconv[0:CONV_TAPS], loss_rows]
    dproj = lax.optimization_barrier((dproj, red_up[1]))[0]
    dproj = lax.optimization_barrier((dproj, red_pa[1], red_pl[1], red_out[1]))[0]
    early_sum, early_upd = small_step("small", early_g, early_w, early_m, early_v,
                                      early_w + [jax.ShapeDtypeStruct((1, CONV_TAPS, d), F32), jax.ShapeDtypeStruct((SUBLANES, LANES), F32)],
                                      dxr, seq=(8, 9, dproj))
    g_cw_full, loss = early_sum[-2], early_sum[-1][0, 0]
    cshard = conv_w.shape[2]
    g_cw = lax.dynamic_slice(g_cw_full, (0, 0, dev * cshard), (1, CONV_TAPS, cshard))
    cw_delta, cw_m, cw_v = (t[:CONV_TAPS][None] for t in _adamw_small(
        "adamw_conv_w", pad_taps(g_cw[0]), pad_taps(conv_w[0]), pad_taps(m_conv_w[0]), pad_taps(v_conv_w[0])))
    half = (big[0].shape[0] // 2, big[0].shape[1])
    g_in0 = _mm_tn("proj_in_dw_0", xn, dproj, part=(0, 2))
    g_in1, sib_in0 = _mm_tn("proj_in_dw_1", xn, dproj, part=(1, 2), side=_sibling_side([g_in0], ["col"], [half]))
    red_in = reduce_group("in_0", ["col"], [half], [g_in0], sib_in0, 4)
    dproj = lax.optimization_barrier((dproj, red_in[0][0], early_sum))[0]
    (dxn0,), sib_in1 = _mm_nt("proj_in_dx_0", dproj, win, [F32], part=(0, 2), side=_sibling_side([g_in1], ["col"], [half]))
    red_in += reduce_group("in_1", ["col"], [half], [g_in1], sib_in1, 6)
    dproj = lax.optimization_barrier((dproj, red_in[1][0]))[0]
    dxn1 = _mm_nt("proj_in_dx_1", dproj, win, [F32], part=(1, 2))[0]
    dxn = jnp.concatenate([dxn0, dxn1], axis=0)
    dxn = lax.optimization_barrier((dxn, red_in[0][1]))[0]
    grad_x, _, dg1 = _rms_bwd("norm_mix_bwd", xs, norm_mix_g, dxn, dh1)
    red_up, red_down = lax.optimization_barrier(((red_up, red_down), dg1))[0]
    big_out = {i: _adamw_shard(f"adamw_{i}", [red], big[i], big_m[i], big_v[i], chip) for i, red in ((4, red_up), (5, red_down))}
    big_out.update({i: _adamw_shard(f"adamw_{i}", [red], big[i], big_m[i], big_v[i], chip) for i, red in ((1, red_pa), (2, red_pl), (3, red_out))})
    late_sum, late_upd = small_step("norm_mix", [dg1], [norm_mix_g], [m_norm_mix_g], [v_norm_mix_g], [norm_mix_g], (big_out[4], big_out[5]))
    big_out[0] = _adamw_shard("adamw_0", red_in, big[0], big_m[0], big_v[0], chip)
    s_grad = late_sum + early_sum[:-2]
    s_delta, s_m, s_v = (late_upd[j] + early_upd[j] for j in range(3))


    names = ["norm_mix_g", "w_in", "conv_w", "conv_b", "lru_wa", "lru_ba", "lru_wx", "lru_bx", "lru_lambda", "w_proj_attn", "w_proj_lru",
             "w_out", "norm_mlp_g", "w_up", "w_down", "norm_final_g"]
    small_names = ["norm_mix_g", "conv_b", "lru_wa", "lru_ba", "lru_wx", "lru_bx", "lru_lambda", "norm_mlp_g", "norm_final_g"]
    big_names = ["w_in", "w_proj_attn", "w_proj_lru", "w_out", "w_up", "w_down"]
    res = {"conv_w": (g_cw, cw_delta, cw_m, cw_v)}
    for i, nm in enumerate(small_names):
        res[nm] = (s_grad[i], s_delta[i], s_m[i], s_v[i])
    for i, nm in enumerate(big_names):
        res[nm] = tuple(t[None] for t in big_out[i])
    return (loss, grad_x[None], *[res[nm][0] for nm in names], *[res[nm][1] for nm in names],
            *[res[nm][2] for nm in names], *[res[nm][3] for nm in names])
```

```python
import jax
import jax.numpy as jnp
from jax import lax
from jax.experimental import pallas as pl
from jax.experimental.pallas import tpu as pltpu
from jax.experimental.pallas import tpu_sc as plsc

F32, BF16 = jnp.float32, jnp.bfloat16
MESH = pl.DeviceIdType.MESH
HBM = pl.BlockSpec(memory_space=pltpu.HBM)
N_DEV = 8
N_CHIP = 4
HEAD = 128
SPAN = 128
DILATIONS = (1, 4, 16)
CONV_TAPS = 4
LRU_C = 8.0
NORM_EPS = 1e-6
LANES = 128
SUBLANES = 8
VMEM_LIMIT = 56 * 1024 * 1024
ADAM_LR, ADAM_B1, ADAM_B2, ADAM_EPS, ADAM_WD, ADAM_STEP = 0.001, 0.9, 0.999, 1e-08, 0.01, 10
ADAM_C1 = 1.0 - ADAM_B1 ** ADAM_STEP
ADAM_C2 = 1.0 - ADAM_B2 ** ADAM_STEP
NEG = -1e30


def _params(sem=None):
    return pltpu.CompilerParams(dimension_semantics=sem, vmem_limit_bytes=VMEM_LIMIT)


def _sigmoid(v):
    return 1.0 / (1.0 + jnp.exp(-v))


def _gelu(v):
    k = 0.7978845608028654
    return 0.5 * v * (1.0 + jnp.tanh(k * (v + 0.044715 * v * v * v)))


def _gelu_grad(v):
    k = 0.7978845608028654
    t = jnp.tanh(k * (v + 0.044715 * v * v * v))
    return 0.5 * (1.0 + t) + 0.5 * v * (1.0 - t * t) * k * (1.0 + 3.0 * 0.044715 * v * v)


NN = (((1,), (0,)), ((), ()))
NT = (((1,), (1,)), ((), ()))
TN = (((0,), (0,)), ((), ()))


def _mm(name, a, a_spec, b, b_spec, dn, grid, out_shapes, out_specs, acc_block, epilogue=None, extras=(), extra_specs=(), side=None):
    nk, ne, no = grid[2], len(extras), len(out_shapes)
    side_ops, side_shapes, side_copies, make_copies = side if side is not None else ((), (), 0, None)
    ns_in, ns_out = len(side_ops), len(side_shapes)

    def body(*refs):
        a_ref, b_ref = refs[0], refs[1]
        ex, side_in = refs[2:2 + ne], refs[2 + ne:2 + ne + ns_in]
        outs = refs[2 + ne + ns_in:2 + ne + ns_in + no]
        side_out = refs[2 + ne + ns_in + no:2 + ne + ns_in + no + ns_out]
        scratch = refs[2 + ne + ns_in + no + ns_out:]
        at = [pl.program_id(ax) for ax in range(3)]
        if side is not None:
            @pl.when((at[0] == 0) & (at[1] == 0) & (at[2] == 0))
            def _():
                for cp in make_copies(side_in, side_out, scratch[-2], scratch[-1]):
                    cp.start()

        part = lax.dot_general(a_ref[...], b_ref[...], dn, preferred_element_type=F32)

        def finish(acc):
            vals = epilogue(acc, *[e[...] for e in ex]) if epilogue is not None else (acc,)
            for o, v in zip(outs, vals):
                o[...] = v.astype(o.dtype)

        if nk == 1:
            finish(part)
        else:
            acc_ref, k = scratch[0], at[2]

            @pl.when(k == 0)
            def _():
                acc_ref[...] = part

            @pl.when(k > 0)
            def _():
                acc_ref[...] += part

            @pl.when(k == nk - 1)
            def _():
                finish(acc_ref[...])

        if side is not None:
            @pl.when((at[0] == grid[0] - 1) & (at[1] == grid[1] - 1) & (at[2] == grid[2] - 1))
            def _():
                for cp in make_copies(side_in, side_out, scratch[-2], scratch[-1]):
                    cp.wait()

    scratch_shapes = [pltpu.VMEM(acc_block, F32)] if nk > 1 else []
    if side is not None:
        scratch_shapes += [pltpu.SemaphoreType.DMA((side_copies,)), pltpu.SemaphoreType.DMA((side_copies,))]
    res = pl.pallas_call(
        body, out_shape=[*out_shapes, *side_shapes], grid=grid, in_specs=[a_spec, b_spec, *extra_specs, *[HBM] * ns_in],
        out_specs=[*out_specs, *[HBM] * ns_out], scratch_shapes=scratch_shapes,
        compiler_params=_params(("arbitrary",) * 3 if side is not None else ("parallel", "parallel", "arbitrary")),
        name=name)(a, b, *extras, *side_ops)
    return res if side is None else (res[:no], res[no:])


def _blk(n, pref):
    return pref if n % pref == 0 else n


def _kblk(k):
    return k if k <= 2048 else next(b for b in (2048, 1024, 512) if k % b == 0)


def _mm_fwd(name, a, w, col0, ncols, out_dtypes, epilogue=None, extras=(), extra_specs_fn=None, bm=1024, bn=1024):
    m, k = a.shape
    bm, bn = _blk(m, bm), _blk(ncols, bn)
    bk = _kblk(k)
    nk = k // bk
    cb0 = col0 // bn
    grid = (m // bm, ncols // bn, nk)
    a_spec = pl.BlockSpec((bm, bk), lambda i, j, kk: (i, kk))
    b_spec = pl.BlockSpec((bk, bn), lambda i, j, kk: (kk, cb0 + j))
    shapes = [jax.ShapeDtypeStruct((m, ncols), dt) for dt in out_dtypes]
    specs = [pl.BlockSpec((bm, bn), lambda i, j, kk: (i, j)) for _ in out_dtypes]
    ex_specs = extra_specs_fn(bm, bn) if extra_specs_fn else ()
    return _mm(name, a, a_spec, w, b_spec, NN, grid, shapes, specs, (bm, bn), epilogue, extras, ex_specs)


def _mm_nt(name, a, w, out_dtypes, epilogue=None, extras=(), extra_specs_fn=None, part=(0, 1), side=None, bm=1024, bn=1024):
    n = w.shape[0]
    if a.ndim == 3:
        seg_cols, m, k = a.shape[2], a.shape[1], a.shape[0] * a.shape[2]
    else:
        m, k = a.shape
    m = m // part[1]
    bm, bn = _blk(m, bm), _blk(n, bn)
    bk = _kblk(k)
    grid = (m // bm, n // bn, k // bk)
    i0 = part[0] * (m // bm)
    if a.ndim == 3:
        per = seg_cols // bk
        a_spec = pl.BlockSpec((None, bm, bk), lambda i, j, kk: (kk // per, i0 + i, kk % per))
    else:
        a_spec = pl.BlockSpec((bm, bk), lambda i, j, kk: (i0 + i, kk))
    b_spec = pl.BlockSpec((bn, bk), lambda i, j, kk: (j, kk))
    shapes = [jax.ShapeDtypeStruct((m, n), dt) for dt in out_dtypes]
    specs = [pl.BlockSpec((bm, bn), lambda i, j, kk: (i, j)) for _ in out_dtypes]
    ex_specs = extra_specs_fn(bm, bn) if extra_specs_fn else ()
    return _mm(name, a, a_spec, w, b_spec, NT, grid, shapes, specs, (bm, bn), epilogue, extras, ex_specs, side)


def _mm_tn(name, a, b, part=(0, 1), side=None, bm=1024, bn=2048):
    t, m = a.shape
    n = b.shape[1] if b.ndim == 2 else b.shape[0] * b.shape[2]
    m = m // part[1]
    bm, bn = _blk(m, bm), _blk(n, bn)
    grid = (m // bm, n // bn, 1)
    i0 = part[0] * (m // bm)
    a_spec = pl.BlockSpec((t, bm), lambda i, j, kk: (0, i0 + i))
    if b.ndim == 3:
        per = b.shape[2] // bn
        b_spec = pl.BlockSpec((None, t, bn), lambda i, j, kk: (j // per, 0, j % per))
    else:
        b_spec = pl.BlockSpec((t, bn), lambda i, j, kk: (0, j))
    res = _mm(name, a, a_spec, b, b_spec, TN, grid, [jax.ShapeDtypeStruct((m, n), BF16)],
              [pl.BlockSpec((bm, bn), lambda i, j, kk: (i, j))], (bm, bn), side=side)
    return res[0] if side is None else (res[0][0], res[1])


ROWS = 256


def _row_spec(d):
    return pl.BlockSpec((ROWS, d), lambda i: (i, 0))


def _vec_spec(d, rows=1):
    return pl.BlockSpec((rows, d), lambda i: (0, 0))


def _rms_fwd(name, x, g):
    s, d = x.shape

    def body(x_ref, g_ref, o_ref):
        xv = x_ref[...]
        r = lax.rsqrt(jnp.mean(xv * xv, axis=-1, keepdims=True) + NORM_EPS)
        o_ref[...] = (xv * r * g_ref[...]).astype(BF16)

    return pl.pallas_call(body, out_shape=jax.ShapeDtypeStruct((s, d), BF16), grid=(s // ROWS,),
                          in_specs=[_row_spec(d), _vec_spec(d)], out_specs=_row_spec(d),
                          compiler_params=_params(("parallel",)), name=name)(x, g)


def _rms_bwd_math(xv, g, dy):
    r = lax.rsqrt(jnp.mean(xv * xv, axis=-1, keepdims=True) + NORM_EPS)
    n = xv * r
    z = dy * g
    dx = r * (z - n * jnp.mean(z * n, axis=-1, keepdims=True))
    return dx, jnp.sum(dy * n, axis=0, keepdims=True)


def _rms_bwd(name, x, g, dy, resid):
    s, d = x.shape
    parts = dy if isinstance(dy, tuple) else (dy,)
    per = s // ROWS // len(parts)
    assert all(p.shape == (per * ROWS, d) for p in parts)

    def body(x_ref, g_ref, *refs):
        dy_refs, (r_ref, dx_ref, dxb_ref, dg_ref) = refs[:len(parts)], refs[len(parts):]
        dyv = dy_refs[0][...]
        for p in range(1, len(parts)):
            dyv = jnp.where(pl.program_id(0) >= p * per, dy_refs[p][...], dyv)
        dx, dg = _rms_bwd_math(x_ref[...], g_ref[...], dyv)
        dx = dx + r_ref[...]
        dx_ref[...] = dx
        dxb_ref[...] = dx.astype(BF16)

        @pl.when(pl.program_id(0) == 0)
        def _():
            dg_ref[...] = jnp.zeros_like(dg_ref)

        dg_ref[...] += dg

    part_specs = [pl.BlockSpec((ROWS, d), lambda i, p=p: (jnp.clip(i - p * per, 0, per - 1), 0)) for p in range(len(parts))]
    return pl.pallas_call(
        body, out_shape=[jax.ShapeDtypeStruct((s, d), F32), jax.ShapeDtypeStruct((s, d), BF16), jax.ShapeDtypeStruct((1, d), F32)],
        grid=(s // ROWS,), in_specs=[_row_spec(d), _vec_spec(d), *part_specs, _row_spec(d)],
        out_specs=[_row_spec(d), _row_spec(d), _vec_spec(d)], compiler_params=_params(("arbitrary",)), name=name)(x, g, *parts, resid)


def _final_loss(h2, tgt, g):
    s, d = h2.shape

    def body(x_ref, t_ref, g_ref, dx_ref, dxb_ref, dg_ref, ls_ref):
        xv, gv = x_ref[...], g_ref[...]
        r = lax.rsqrt(jnp.mean(xv * xv, axis=-1, keepdims=True) + NORM_EPS)
        diff = xv * r * gv - t_ref[...]
        dx, dg = _rms_bwd_math(xv, gv, diff * (1.0 / d))
        dx_ref[...] = dx
        dxb_ref[...] = dx.astype(BF16)

        @pl.when(pl.program_id(0) == 0)
        def _():
            dg_ref[...] = jnp.zeros_like(dg_ref)
            ls_ref[...] = jnp.zeros_like(ls_ref)

        dg_ref[...] += dg
        ls_ref[...] += jnp.sum(diff * diff, axis=0, keepdims=True)

    return pl.pallas_call(
        body, out_shape=[jax.ShapeDtypeStruct((s, d), F32), jax.ShapeDtypeStruct((s, d), BF16),
                         jax.ShapeDtypeStruct((1, d), F32), jax.ShapeDtypeStruct((1, d), F32)],
        grid=(s // ROWS,), in_specs=[_row_spec(d), _row_spec(d), _vec_spec(d)],
        out_specs=[_row_spec(d), _row_spec(d), _vec_spec(d), _vec_spec(d)],
        compiler_params=_params(("arbitrary",)), name="final_norm_loss")(h2, tgt, g)


ATTN_Q = 128


ATTN_BATCH = 8


def _attn_units(s):
    units = []
    for gi, d in enumerate(DILATIONS):
        for r in range(d):
            for q0 in range(0, s // d, ATTN_Q):
                k0 = max(q0 - SPAN, 0)
                units.append((gi, d, r, q0, k0, q0 + ATTN_Q - k0))
    return units


def _stream_rows(d, r, start, size):
    return pl.ds(r + start * d, size) if d == 1 else pl.ds(r + start * d, size, stride=d)


def _attn_bias(bias_ref, slope):
    qi = lax.broadcasted_iota(jnp.int32, (ATTN_Q, ATTN_Q + SPAN), 0)
    kj = lax.broadcasted_iota(jnp.int32, (ATTN_Q, ATTN_Q + SPAN), 1)
    dist = SPAN + qi - kj
    valid = (dist >= 0) & (dist <= SPAN)
    for gi, d in enumerate(DILATIONS):
        bias_ref[gi] = jnp.where(valid, -(slope * d) * dist.astype(F32), NEG)


def _attn_scores(q_ref, k_ref, bias_ref, gi, d, r, q0, k0, nk):
    qrows, krows = _stream_rows(d, r, q0, ATTN_Q), _stream_rows(d, r, k0, nk)
    qb = q_ref[qrows, :].astype(BF16)
    kb = k_ref[krows, :].astype(BF16)
    sc = lax.dot_general(qb, kb, NT, preferred_element_type=F32) * (HEAD ** -0.5) + bias_ref[gi, :, pl.ds(ATTN_Q + SPAN - nk, nk)]
    return sc, qb, kb, qrows, krows


def _attn_fwd(proj, dm, slopes):
    s = proj.shape[0]
    units = _attn_units(s)

    def body(sl_ref, q_ref, k_ref, v_ref, att_ref, lse_ref, *scr):
        o_scr, l_scr, bias = scr[:3], scr[3:6], scr[6]
        _attn_bias(bias, sl_ref[pl.program_id(0)])
        for first in range(0, len(units), ATTN_BATCH):
            batch = units[first:first + ATTN_BATCH]
            scored = [_attn_scores(q_ref, k_ref, bias, gi, d, r, q0, k0, nk) for gi, d, r, q0, k0, nk in batch]
            soft = []
            for sc, _, _, _, _ in scored:
                m = jnp.max(sc, axis=-1, keepdims=True)
                p = jnp.exp(sc - m)
                soft.append((m, p, jnp.sum(p, axis=-1, keepdims=True)))
            outs = [lax.dot_general(p.astype(BF16), v_ref[sco[4], :].astype(BF16), NN, preferred_element_type=F32)
                    for (m, p, l), sco in zip(soft, scored)]
            for (gi, *_), (m, p, l), sco, o in zip(batch, soft, scored, outs):
                o_scr[gi][sco[3], :] = o / l
                l_scr[gi][sco[3], :] = jnp.broadcast_to(m + jnp.log(l), (ATTN_Q, HEAD))
        l0, l1, l2 = l_scr[0][...], l_scr[1][...], l_scr[2][...]
        m = jnp.maximum(jnp.maximum(l0, l1), l2)
        w0, w1, w2 = jnp.exp(l0 - m), jnp.exp(l1 - m), jnp.exp(l2 - m)
        tot = w0 + w1 + w2
        att_ref[...] = ((w0 * o_scr[0][...] + w1 * o_scr[1][...] + w2 * o_scr[2][...]) / tot).astype(BF16)
        lse_ref[...] = m + jnp.log(tot)

    def seg(i):
        return pl.BlockSpec((s, HEAD), lambda h: (0, i * (dm // HEAD) + h))

    col = pl.BlockSpec((s, HEAD), lambda h: (0, h))
    return pl.pallas_call(
        body, out_shape=[jax.ShapeDtypeStruct((s, dm), BF16), jax.ShapeDtypeStruct((s, dm), F32)], grid=(dm // HEAD,),
        in_specs=[pl.BlockSpec(memory_space=pltpu.SMEM), seg(0), seg(1), seg(2)], out_specs=[col, col],
        scratch_shapes=[pltpu.VMEM((s, HEAD), F32)] * (2 * len(DILATIONS)) + [pltpu.VMEM((len(DILATIONS), ATTN_Q, ATTN_Q + SPAN), F32)],
        compiler_params=_params(("parallel",)), name="attn_fwd")(slopes, proj, proj, proj)


def _attn_bwd(proj, dm, datt, att, lse, slopes, others):
    s = proj.shape[0]
    units = _attn_units(s)

    def body(sl_ref, q_ref, k_ref, v_ref, do_ref, att_ref, lse_ref, o3, o4, o5, o6, out_ref, dq_scr, dk_scr, dv_scr, dl_scr, bias):
        _attn_bias(bias, sl_ref[pl.program_id(0)])
        delta = jnp.sum(do_ref[...] * att_ref[...].astype(F32), axis=-1, keepdims=True)
        dl_scr[...] = jnp.broadcast_to(delta, (s, HEAD))
        dq_scr[...] = jnp.zeros_like(dq_scr)
        dk_scr[...] = jnp.zeros_like(dk_scr)
        dv_scr[...] = jnp.zeros_like(dv_scr)
        for first in range(0, len(units), ATTN_BATCH):
            scored = [_attn_scores(q_ref, k_ref, bias, gi, d, r, q0, k0, nk) for gi, d, r, q0, k0, nk in units[first:first + ATTN_BATCH]]
            dobs = [do_ref[sco[3], :].astype(BF16) for sco in scored]
            dps = [lax.dot_general(dob, v_ref[sco[4], :].astype(BF16), NT, preferred_element_type=F32) for dob, sco in zip(dobs, scored)]
            ps = [jnp.exp(sco[0] - lse_ref[sco[3], :][:, 0:1]) for sco in scored]
            dss = [(p * (dp - dl_scr[sco[3], :][:, 0:1]) * (HEAD ** -0.5)).astype(BF16) for p, dp, sco in zip(ps, dps, scored)]
            dqs = [lax.dot_general(ds, sco[2], NN, preferred_element_type=F32) for ds, sco in zip(dss, scored)]
            dks = [lax.dot_general(ds, sco[1], TN, preferred_element_type=F32) for ds, sco in zip(dss, scored)]
            dvs = [lax.dot_general(p.astype(BF16), dob, TN, preferred_element_type=F32) for p, dob in zip(ps, dobs)]
            for sco, dq, dk, dv in zip(scored, dqs, dks, dvs):
                dq_scr[sco[3], :] += dq
                dk_scr[sco[4], :] += dk
                dv_scr[sco[4], :] += dv
        for j, scr in enumerate((dq_scr, dk_scr, dv_scr)):
            out_ref[j] = scr[...].astype(BF16)
        for j, other in enumerate((o3, o4, o5, o6)):
            out_ref[3 + j] = other[...]

    def seg(i):
        return pl.BlockSpec((s, HEAD), lambda h: (0, i * (dm // HEAD) + h))

    col = pl.BlockSpec((s, HEAD), lambda h: (0, h))
    return pl.pallas_call(
        body, out_shape=jax.ShapeDtypeStruct((7, s, dm), BF16), grid=(dm // HEAD,),
        in_specs=[pl.BlockSpec(memory_space=pltpu.SMEM), seg(0), seg(1), seg(2), col, col, col, col, col, col, col],
        out_specs=pl.BlockSpec((7, s, HEAD), lambda h: (0, 0, h)),
        scratch_shapes=[pltpu.VMEM((s, HEAD), F32)] * 4 + [pltpu.VMEM((len(DILATIONS), ATTN_Q, ATTN_Q + SPAN), F32)],
        compiler_params=_params(("parallel",)), name="attn_bwd")(slopes, proj, proj, proj, datt, att, lse, *others)


VEC_CB, VEC_BA, VEC_BX, VEC_LAM = 0, 1, 2, 3
SEG_Q, SEG_K, SEG_V, SEG_X, SEG_GATE, SEG_GA, SEG_GL = range(7)


def _softplus(z):
    return jnp.maximum(z, 0.0) + jnp.log1p(jnp.exp(-jnp.abs(z)))


def _gate_math(xc, wa_ref, wx_ref, vec):
    xcb = xc.astype(BF16)
    nh = xc.shape[1] // HEAD
    pre_a = jnp.concatenate([jnp.dot(xcb[:, h * HEAD:(h + 1) * HEAD], wa_ref[h], preferred_element_type=F32) for h in range(nh)], axis=1)
    pre_x = jnp.concatenate([jnp.dot(xcb[:, h * HEAD:(h + 1) * HEAD], wx_ref[h], preferred_element_type=F32) for h in range(nh)], axis=1)
    ra = _sigmoid(pre_a + vec[VEC_BA:VEC_BA + 1])
    ig = _sigmoid(pre_x + vec[VEC_BX:VEC_BX + 1])
    sp = _softplus(-vec[VEC_LAM:VEC_LAM + 1])
    log_a = -LRU_C * ra * sp
    a = jnp.exp(log_a)
    z = 2.0 * log_a
    one_minus_a2 = jnp.where(z > -0.01, -z * (1.0 + z * (0.5 + z * (1.0 / 6.0))), 1.0 - jnp.exp(z))
    mult = jnp.sqrt(one_minus_a2)
    return dict(xcb=xcb, ra=ra, ig=ig, sp=sp, a=a, mult=mult)


def _conv_pad_prev(pad_ref, cur, halo, first):
    pad_ref[0:SUBLANES, :] = jnp.where(first, 0.0, halo)
    pad_ref[SUBLANES:SUBLANES + cur.shape[0], :] = cur


def _shift_rows(x, s, fill, up=False):
    rid = lax.broadcasted_iota(jnp.int32, x.shape, 0)
    if up:
        return jnp.where(rid < SUBLANES - s, pltpu.roll(x, SUBLANES - s, axis=0), fill)
    return jnp.where(rid >= s, pltpu.roll(x, s, axis=0), fill)


def _lru_fwd(proj, d, cw8, vec8, wa, wx):
    s = proj.shape[0]
    hb = ROWS // SUBLANES

    def body(x_ref, halo_ref, g_ref, cw_ref, vec_ref, wa_ref, wx_ref, xc_ref, hp_ref, h2_ref, y_ref, pad, a_scr, u_scr, carry):
        @pl.when(pl.program_id(0) == 0)
        def _():
            carry[...] = jnp.zeros_like(carry)

        _conv_pad_prev(pad, x_ref[...], halo_ref[...], pl.program_id(0) == 0)
        vec = vec_ref[...]
        xc = vec[VEC_CB:VEC_CB + 1]
        for k in range(CONV_TAPS):
            xc = xc + cw_ref[k:k + 1, :] * pad[pl.ds(SUBLANES - (CONV_TAPS - 1) + k, ROWS), :]
        gm = _gate_math(xc, wa_ref, wx_ref, vec)
        xc_ref[...] = xc
        a_scr[...] = gm["a"]
        u_scr[...] = gm["mult"] * (gm["ig"] * xc)

        def group(gi, before):
            rows = pl.ds(pl.multiple_of(gi * SUBLANES, SUBLANES), SUBLANES)
            ca, cb = a_scr[rows, :], u_scr[rows, :]
            for sh in (1, 2, 4):
                cb = ca * _shift_rows(cb, sh, 0.0) + cb
                ca = ca * _shift_rows(ca, sh, 1.0)
            h = cb + ca * before
            hp_ref[rows, :] = jnp.where(lax.broadcasted_iota(jnp.int32, h.shape, 0) == 0, before, pltpu.roll(h, 1, axis=0))
            h2_ref[rows, :] = h.astype(BF16)
            y_ref[rows, :] = (h * _gelu(g_ref[rows, :])).astype(BF16)
            return jnp.broadcast_to(h[SUBLANES - 1:SUBLANES, :], h.shape)

        carry[...] = lax.fori_loop(0, ROWS // SUBLANES, group, carry[...], unroll=2)

    wspec = pl.BlockSpec(wa.shape, lambda i: (0, 0, 0))
    return pl.pallas_call(
        body, out_shape=[jax.ShapeDtypeStruct((s, d), F32)] * 2 + [jax.ShapeDtypeStruct((s, d), BF16)] * 2, grid=(s // ROWS,),
        in_specs=[pl.BlockSpec((ROWS, d), lambda i: (i, SEG_X)),
                  pl.BlockSpec((SUBLANES, d), lambda i: (jnp.maximum(i * hb - 1, 0), SEG_X)),
                  pl.BlockSpec((ROWS, d), lambda i: (i, SEG_GATE)),
                  _vec_spec(d, SUBLANES), _vec_spec(d, SUBLANES), wspec, wspec],
        out_specs=[_row_spec(d)] * 4,
        scratch_shapes=[pltpu.VMEM((ROWS + SUBLANES, d), F32), pltpu.VMEM((ROWS, d), F32), pltpu.VMEM((ROWS, d), F32), pltpu.VMEM((SUBLANES, d), F32)],
        compiler_params=_params(("arbitrary",)), name="lru_fwd")(proj, proj, proj, cw8, vec8, wa, wx)


def _lru_bwd(hp, dh, xc, proj, cw8, wa, wx, vec8):
    s, d = xc.shape
    nh = d // HEAD
    nb = s // ROWS
    hb = ROWS // SUBLANES

    def body(hp_ref, dh_ref, xc_ref, x_ref, xprev_ref, cw_ref, wa_ref, wx_ref, vec_ref, dx_ref, dwa_ref, dwx_ref, dvec_ref, dcw_ref,
             a_scr, g_scr, da_scr, carry, padd, padx, dnext):
        step = pl.program_id(0)

        @pl.when(step == 0)
        def _():
            carry[...] = jnp.zeros_like(carry)
            dnext[...] = jnp.zeros_like(dnext)
            dwa_ref[...] = jnp.zeros_like(dwa_ref)
            dwx_ref[...] = jnp.zeros_like(dwx_ref)
            dvec_ref[...] = jnp.zeros_like(dvec_ref)
            dcw_ref[...] = jnp.zeros_like(dcw_ref)

        xc_v, vec = xc_ref[...], vec_ref[...]
        gm = _gate_math(xc_v, wa_ref, wx_ref, vec)
        ra, ig, sp, a, mult = gm["ra"], gm["ig"], gm["sp"], gm["a"], gm["mult"]
        a_scr[...] = a

        def group(j, after):
            rows = pl.ds(pl.multiple_of((ROWS // SUBLANES - 1 - j) * SUBLANES, SUBLANES), SUBLANES)
            ca, dhv = a_scr[rows, :], dh_ref[rows, :]
            cb = ca * dhv
            for sh in (1, 2, 4):
                cb = ca * _shift_rows(cb, sh, 0.0, up=True) + cb
                ca = ca * _shift_rows(ca, sh, 1.0, up=True)
            c = cb + ca * after
            last = lax.broadcasted_iota(jnp.int32, c.shape, 0) == SUBLANES - 1
            g = dhv + jnp.where(last, after, pltpu.roll(c, SUBLANES - 1, axis=0))
            g_scr[rows, :] = g
            da_scr[rows, :] = g * hp_ref[rows, :]
            return jnp.broadcast_to(c[0:1, :], c.shape)

        carry[...] = lax.fori_loop(0, ROWS // SUBLANES, group, carry[...], unroll=2)
        du, da = g_scr[...], da_scr[...]
        dmult = du * ig * xc_v
        dlog_a = da * a - dmult * (a * a) / mult
        dpre_a = dlog_a * (-LRU_C * sp) * ra * (1.0 - ra)
        dpre_x = du * mult * xc_v * ig * (1.0 - ig)
        dlam = jnp.sum(dlog_a * (-LRU_C * ra), axis=0, keepdims=True) * (-_sigmoid(-vec[VEC_LAM:VEC_LAM + 1]))
        dvec_ref[VEC_BA:VEC_BA + 1, :] += jnp.sum(dpre_a, axis=0, keepdims=True)
        dvec_ref[VEC_BX:VEC_BX + 1, :] += jnp.sum(dpre_x, axis=0, keepdims=True)
        dvec_ref[VEC_LAM:VEC_LAM + 1, :] += dlam
        dab, dxb, xcb = dpre_a.astype(BF16), dpre_x.astype(BF16), gm["xcb"]
        back = []
        for h in range(nh):
            cols = slice(h * HEAD, (h + 1) * HEAD)
            dwa_ref[h] += lax.dot_general(xcb[:, cols], dab[:, cols], TN, preferred_element_type=F32)
            dwx_ref[h] += lax.dot_general(xcb[:, cols], dxb[:, cols], TN, preferred_element_type=F32)
            back.append(lax.dot_general(dab[:, cols], wa_ref[h], NT, preferred_element_type=F32)
                        + lax.dot_general(dxb[:, cols], wx_ref[h], NT, preferred_element_type=F32))
        dc = du * mult * ig + jnp.concatenate(back, axis=1)
        padd[0:ROWS, :] = dc
        padd[ROWS:ROWS + SUBLANES, :] = dnext[...]
        dnext[...] = dc[0:SUBLANES, :]
        _conv_pad_prev(padx, x_ref[...], xprev_ref[...], step == nb - 1)
        dx = jnp.zeros_like(dc)
        for k in range(CONV_TAPS):
            dx = dx + cw_ref[k:k + 1, :] * padd[pl.ds(CONV_TAPS - 1 - k, ROWS), :]
            dcw_ref[k:k + 1, :] += jnp.sum(dc * padx[pl.ds(SUBLANES - (CONV_TAPS - 1) + k, ROWS), :], axis=0, keepdims=True)
        dcw_ref[CONV_TAPS:CONV_TAPS + 1, :] += jnp.sum(dc, axis=0, keepdims=True)
        dx_ref[...] = dx.astype(BF16)

    rows_rev = pl.BlockSpec((ROWS, d), lambda i: (nb - 1 - i, 0))
    wspec = pl.BlockSpec(wa.shape, lambda i: (0, 0, 0))
    return pl.pallas_call(
        body, out_shape=[jax.ShapeDtypeStruct((s, d), BF16), jax.ShapeDtypeStruct(wa.shape, F32), jax.ShapeDtypeStruct(wa.shape, F32),
                         jax.ShapeDtypeStruct((SUBLANES, d), F32), jax.ShapeDtypeStruct((SUBLANES, d), F32)],
        grid=(nb,),
        in_specs=[rows_rev, rows_rev, rows_rev, pl.BlockSpec((ROWS, d), lambda i: (nb - 1 - i, SEG_X)),
                  pl.BlockSpec((SUBLANES, d), lambda i: (jnp.maximum((nb - 1 - i) * hb - 1, 0), SEG_X)),
                  _vec_spec(d, SUBLANES), wspec, wspec, _vec_spec(d, SUBLANES)],
        out_specs=[rows_rev, wspec, wspec, _vec_spec(d, SUBLANES), _vec_spec(d, SUBLANES)],
        scratch_shapes=[pltpu.VMEM((ROWS, d), F32)] * 3 + [pltpu.VMEM((SUBLANES, d), F32)]
        + [pltpu.VMEM((ROWS + SUBLANES, d), F32)] * 2 + [pltpu.VMEM((SUBLANES, d), F32)],
        compiler_params=_params(("arbitrary",)), name="lru_bwd")(hp, dh, xc, proj, proj, cw8, wa, wx, vec8)


def _coords():
    return lax.axis_index("x"), lax.axis_index("y"), lax.axis_index("c")


def _other_chips(x, y):
    return [(1 - x, y), (x, 1 - y), (1 - x, 1 - y)]


def _slab(ref, kind, shard_shape, idx, half=None):
    r, c = shard_shape
    r0, nr = (0, r) if half is None else (half * (r // 2), r // 2)
    if kind == "col":
        return ref.at[pl.ds(r0, nr), pl.ds(pl.multiple_of(idx * c, LANES), c)]
    if kind == "row":
        return ref.at[pl.ds(pl.multiple_of(idx * r, SUBLANES) + r0, nr), :]
    return ref.at[idx, pl.ds(r0, nr), :]


def _full_shape(shard_shape, kind):
    r, c = shard_shape
    return {"col": (r, c * N_DEV), "row": (r * N_DEV, c), "slot": (N_DEV, r, c)}[kind]


def _handshake(peers):
    barrier = pltpu.get_barrier_semaphore()
    for peer in peers:
        pl.semaphore_signal(barrier, inc=1, device_id=peer, device_id_type=MESH)
    pl.semaphore_wait(barrier, len(peers))


def _launch(name, body, out_shape, operands, sems, sequencer_id):
    if sequencer_id is None:
        return pl.pallas_call(body, out_shape=out_shape, in_specs=[HBM] * len(operands), out_specs=[HBM] * len(out_shape),
                              scratch_shapes=sems, name=name)(*operands)
    return pl.kernel(body, out_type=out_shape, mesh=plsc.ScalarSubcoreMesh(axis_name="seq", num_cores=1), name=name,
                     scratch_types=sems, compiler_params=pltpu.CompilerParams(collective_id=sequencer_id))(*operands)


AG_COPIES = 10


def _all_gather(name, shards, kinds, sequencer_id=None):
    n = len(shards)
    shapes = [s.shape for s in shards]

    def body(*refs):
        ins, outs = refs[:n], refs[n:2 * n]
        send_sems, recv_sems, local_sems = refs[2 * n:]
        x, y, c = _coords()
        me, sib, xn, yn, dg = (x, y, c), (x, y, 1 - c), (1 - x, y, c), (x, 1 - y, c), (1 - x, 1 - y, c)
        if sequencer_id is not None:
            _handshake([sib, xn, yn])

        def part(i, dev, half=None):
            return _slab(outs[i], kinds[i], shapes[i], 4 * dev[0] + 2 * dev[1] + dev[2], half)

        def copy(i, k, block, half, to, own=False):
            r = shapes[i][0]
            src = part(i, block, half) if not own else (ins[i] if half is None else ins[i].at[pl.ds(half * (r // 2), r // 2), :])
            return pltpu.make_async_remote_copy(
                src_ref=src, dst_ref=part(i, block, half), send_sem=send_sems.at[AG_COPIES * i + k],
                recv_sem=recv_sems.at[AG_COPIES * i + k], device_id=to, device_id_type=MESH)

        def other_core(dev):
            return (dev[0], dev[1], 1 - c)

        started = []

        def start(cp):
            cp.start()
            started.append(cp)

        for i in range(n):
            start(copy(i, 1, me, 0, xn, own=True))
            start(copy(i, 4, me, 1, yn, own=True))
            start(copy(i, 2, me, 1, xn, own=True))
            start(copy(i, 3, me, 0, yn, own=True))
            start(copy(i, 0, me, None, sib, own=True))
        mine = [pltpu.make_async_copy(ins[i], part(i, me), local_sems.at[i]) for i in range(n)]
        for cp in mine:
            cp.start()
        for i in range(n):
            copy(i, 1, xn, 0, me).wait_recv()
            start(copy(i, 5, xn, 0, yn))
            copy(i, 4, yn, 1, me).wait_recv()
            start(copy(i, 6, yn, 1, xn))
        for i in range(n):
            copy(i, 2, xn, 1, me).wait_recv()
            start(copy(i, 7, xn, None, sib))
            copy(i, 3, yn, 0, me).wait_recv()
            start(copy(i, 8, yn, None, sib))
        for i in range(n):
            copy(i, 5, dg, 0, me).wait_recv()
            copy(i, 6, dg, 1, me).wait_recv()
            start(copy(i, 9, dg, None, sib))
        for i in range(n):
            copy(i, 0, sib, None, me).wait_recv()
            for k, dev in ((7, xn), (8, yn), (9, dg)):
                copy(i, k, other_core(dev), None, me).wait_recv()
        for cp in started:
            cp.wait_send()
        for cp in mine:
            cp.wait()

    out_shape = [jax.ShapeDtypeStruct(_full_shape(s.shape, k), s.dtype) for s, k in zip(shards, kinds)]
    sems = [pltpu.SemaphoreType.DMA((AG_COPIES * n,)), pltpu.SemaphoreType.DMA((AG_COPIES * n,)), pltpu.SemaphoreType.DMA((n,))]
    return _launch(name, body, out_shape, shards, sems, sequencer_id)


def _sibling_copies(kinds, shard_shapes):
    def make(ins, outs, send_sems, recv_sems):
        x, y, c = _coords()
        return [pltpu.make_async_remote_copy(
            src_ref=_slab(ins[i], kinds[i], shard_shapes[i], 2 * q + (1 - c)), dst_ref=outs[i].at[q],
            send_sem=send_sems.at[N_CHIP * i + q], recv_sem=recv_sems.at[N_CHIP * i + q],
            device_id=(x, y, 1 - c), device_id_type=MESH) for i in range(len(ins)) for q in range(N_CHIP)]
    return make


def _sibling_side(partials, kinds, shard_shapes):
    return (partials, [jax.ShapeDtypeStruct((N_CHIP, *s), BF16) for s in shard_shapes], N_CHIP * len(partials),
            _sibling_copies(kinds, shard_shapes))


def _exchange_chips(name, chip_sums, sequencer_id=None):
    n = len(chip_sums)

    def body(*refs):
        ins, outs = refs[:n], refs[n:2 * n]
        send_sems, recv_sems = refs[2 * n:]
        x, y, c = _coords()
        if sequencer_id is not None:
            _handshake([(cx, cy, c) for cx, cy in _other_chips(x, y)])
        cps = []
        for i in range(n):
            for k, (cx, cy) in enumerate(_other_chips(x, y)):
                cps.append(pltpu.make_async_remote_copy(
                    src_ref=ins[i].at[2 * cx + cy], dst_ref=outs[i].at[k], send_sem=send_sems.at[3 * i + k],
                    recv_sem=recv_sems.at[3 * i + k], device_id=(cx, cy, c), device_id_type=MESH))
        for cp in cps:
            cp.start()
        for cp in cps:
            cp.wait()

    return _launch(name, body, [jax.ShapeDtypeStruct((3, *t.shape[1:]), BF16) for t in chip_sums], chip_sums,
                   [pltpu.SemaphoreType.DMA((3 * n,)), pltpu.SemaphoreType.DMA((3 * n,))], sequencer_id)


def _all_peers(x, y, c):
    return [(x ^ (k >> 2), y ^ ((k >> 1) & 1), c ^ (k & 1)) for k in range(1, N_DEV)]


def _small_scatter(name, packed, sequencer_id):
    rows = packed.shape[0] // N_DEV

    def body(p_ref, rb_ref, send_sems, recv_sems, local_sem):
        x, y, c = _coords()
        me = 4 * x + 2 * y + c
        peers = _all_peers(x, y, c)
        _handshake(peers)

        def piece(idx):
            return p_ref.at[pl.ds(pl.multiple_of(idx * rows, SUBLANES), rows), :]

        cps = [pltpu.make_async_remote_copy(src_ref=piece(me ^ k), dst_ref=rb_ref.at[k], send_sem=send_sems.at[k], recv_sem=recv_sems.at[k],
                                            device_id=peers[k - 1], device_id_type=MESH) for k in range(1, N_DEV)]
        for cp in cps:
            cp.start()
        mine = pltpu.make_async_copy(piece(me), rb_ref.at[0], local_sem)
        mine.start()
        for cp in cps:
            cp.wait()
        mine.wait()

    return _launch(name, body, [jax.ShapeDtypeStruct((N_DEV, rows, LANES), F32)], [packed],
                   [pltpu.SemaphoreType.DMA((N_DEV,)), pltpu.SemaphoreType.DMA((N_DEV,)), pltpu.SemaphoreType.DMA], sequencer_id)[0]


def _small_sum(name, pieces):
    def body(p_ref, o_ref):
        acc = p_ref[0]
        for k in range(1, N_DEV):
            acc = acc + p_ref[k]
        o_ref[...] = acc

    vm = pl.BlockSpec(memory_space=pltpu.VMEM)
    return pl.pallas_call(body, out_shape=jax.ShapeDtypeStruct(pieces.shape[1:], F32), in_specs=[vm], out_specs=vm, name=name)(pieces)


def _small_gather(name, tot, sequencer_id):
    rows = tot.shape[0]

    def body(t_ref, out_ref, send_sems, recv_sems, local_sem):
        x, y, c = _coords()
        me = 4 * x + 2 * y + c
        peers = _all_peers(x, y, c)
        _handshake(peers)

        def piece(idx):
            return out_ref.at[pl.ds(pl.multiple_of(idx * rows, SUBLANES), rows), :]

        cps = [pltpu.make_async_remote_copy(src_ref=t_ref, dst_ref=piece(me), send_sem=send_sems.at[k], recv_sem=recv_sems.at[k],
                                            device_id=peers[k - 1], device_id_type=MESH) for k in range(1, N_DEV)]
        for cp in cps:
            cp.start()
        mine = pltpu.make_async_copy(t_ref, piece(me), local_sem)
        mine.start()
        for cp in cps:
            cp.wait()
        mine.wait()

    return _launch(name, body, [jax.ShapeDtypeStruct((N_DEV * rows, LANES), F32)], [tot],
                   [pltpu.SemaphoreType.DMA((N_DEV,)), pltpu.SemaphoreType.DMA((N_DEV,)), pltpu.SemaphoreType.DMA], sequencer_id)[0]


def _all_reduce_small(name, packed):
    rows = packed.shape[0] // N_DEV

    def body(p_ref, out_ref, rb, tot, send_sems, recv_sems):
        x, y, c = _coords()
        me = 4 * x + 2 * y + c

        def peer(k):
            return (x ^ (k >> 2), y ^ ((k >> 1) & 1), c ^ (k & 1))

        def rows_of(idx):
            return pl.ds(pl.multiple_of(idx * rows, SUBLANES), rows)

        def piece(ref, idx):
            return ref.at[rows_of(idx), :]

        scatter = [pltpu.make_async_remote_copy(src_ref=piece(p_ref, me ^ k), dst_ref=rb.at[k], send_sem=send_sems.at[k],
                                                recv_sem=recv_sems.at[k], device_id=peer(k), device_id_type=MESH) for k in range(1, N_DEV)]
        for cp in scatter:
            cp.start()
        acc = p_ref[rows_of(me), :]
        for cp in scatter:
            cp.wait_recv()
        for k in range(1, N_DEV):
            acc = acc + rb[k]
        tot[...] = acc
        out_ref[rows_of(me), :] = acc
        gather = [pltpu.make_async_remote_copy(src_ref=tot, dst_ref=piece(out_ref, me), send_sem=send_sems.at[N_DEV + k],
                                               recv_sem=recv_sems.at[N_DEV + k], device_id=peer(k), device_id_type=MESH)
                  for k in range(1, N_DEV)]
        for cp in gather:
            cp.start()
        for k in range(1, N_DEV):
            pltpu.make_async_remote_copy(src_ref=tot, dst_ref=piece(out_ref, me ^ k), send_sem=send_sems.at[N_DEV + k],
                                         recv_sem=recv_sems.at[N_DEV + k], device_id=peer(k), device_id_type=MESH).wait_recv()
        for cp in scatter + gather:
            cp.wait_send()

    vm = pl.BlockSpec(memory_space=pltpu.VMEM)
    return pl.pallas_call(
        body, out_shape=jax.ShapeDtypeStruct(packed.shape, F32), in_specs=[vm], out_specs=vm,
        scratch_shapes=[pltpu.VMEM((N_DEV, rows, LANES), F32), pltpu.VMEM((rows, LANES), F32),
                        pltpu.SemaphoreType.DMA((2 * N_DEV,)), pltpu.SemaphoreType.DMA((2 * N_DEV,))],
        compiler_params=pltpu.CompilerParams(vmem_limit_bytes=VMEM_LIMIT), name=name)(packed)


def _adamw_math(g, w, m, v):
    m = ADAM_B1 * m + (1.0 - ADAM_B1) * g
    v = ADAM_B2 * v + (1.0 - ADAM_B2) * (g * g)
    delta = -ADAM_LR * ((m / ADAM_C1) / (jnp.sqrt(v / ADAM_C2) + ADAM_EPS) + ADAM_WD * w)
    return delta, m, v


def _slab_spec(kind, shard_shape, tr, slab_of):
    r, c = shard_shape
    if kind == "col":
        return pl.BlockSpec((tr, c), lambda q, i, sc: (i, slab_of(q, sc)))
    return pl.BlockSpec((tr, c), lambda q, i, sc: (slab_of(q, sc) * (r // tr) + i, 0))


def _chip_sum(name, partial, recv, kind, shard_shape, core):
    r, c = shard_shape
    tr = _blk(r, 1024)

    def body(core_ref, p_ref, r_ref, o_ref):
        o_ref[...] = (p_ref[...].astype(F32) + r_ref[...].astype(F32)).astype(BF16)

    spec4 = pl.BlockSpec((None, tr, c), lambda q, i, sc: (q, i, 0))
    grid_spec = pltpu.PrefetchScalarGridSpec(
        num_scalar_prefetch=1, grid=(N_CHIP, r // tr),
        in_specs=[_slab_spec(kind, shard_shape, tr, lambda q, sc: 2 * q + sc[0]), spec4], out_specs=spec4)
    return pl.pallas_call(body, out_shape=jax.ShapeDtypeStruct((N_CHIP, r, c), BF16), grid_spec=grid_spec,
                          compiler_params=_params(("parallel", "parallel")), name=name)(core, partial, recv)


def _adamw_shard(name, parts, w, m, v, chip):
    r, c = w.shape
    n_parts = len(parts)
    tr = _blk(r // n_parts, 256)
    per = r // n_parts // tr

    def body(chip_ref, *refs):
        src, (w_ref, m_ref, v_ref), (g_out, d_out, m_out, v_out) = refs[:2 * n_parts], refs[2 * n_parts:2 * n_parts + 3], refs[2 * n_parts + 3:]
        for p in range(n_parts):
            @pl.when(pl.program_id(0) // per == p)
            def _():
                g = src[2 * p][...].astype(F32)
                for k in range(3):
                    g = g + src[2 * p + 1][k].astype(F32)
                g_out[...] = g
                d_out[...], m_out[...], v_out[...] = _adamw_math(g, w_ref[...], m_ref[...], v_ref[...])

    def part_specs(p):
        at = lambda i: jnp.clip(i - p * per, 0, per - 1)
        return [pl.BlockSpec((None, tr, c), lambda i, sc: (sc[0], at(i), 0)), pl.BlockSpec((3, tr, c), lambda i, sc: (0, at(i), 0))]

    blk = pl.BlockSpec((tr, c), lambda i, sc: (i, 0))
    grid_spec = pltpu.PrefetchScalarGridSpec(
        num_scalar_prefetch=1, grid=(r // tr,), in_specs=[s for p in range(n_parts) for s in part_specs(p)] + [blk, blk, blk], out_specs=[blk] * 4)
    return pl.pallas_call(body, out_shape=[jax.ShapeDtypeStruct((r, c), F32)] * 4, grid_spec=grid_spec,
                          compiler_params=_params(("parallel",)), name=name)(chip, *[a for p in parts for a in p], w, m, v)


def _adamw_small(name, g, w, m, v):
    def body(g_ref, w_ref, m_ref, v_ref, d_out, m_out, v_out):
        d_out[...], m_out[...], v_out[...] = _adamw_math(g_ref[...], w_ref[...], m_ref[...], v_ref[...])

    vm = pl.BlockSpec(memory_space=pltpu.VMEM)
    return pl.pallas_call(body, out_shape=[jax.ShapeDtypeStruct(g.shape, F32)] * 3, in_specs=[vm] * 4, out_specs=[vm] * 3,
                          compiler_params=pltpu.CompilerParams(vmem_limit_bytes=VMEM_LIMIT), name=name)(g, w, m, v)


def _pack_rows(arrays, total_rows):
    flat = [a.reshape(-1, LANES) for a in arrays]
    used = sum(f.shape[0] for f in flat)
    return jnp.concatenate(flat + [jnp.zeros((total_rows - used, LANES), F32)], axis=0)


def _unpack_rows(packed, like):
    out, at = [], 0
    for a in like:
        n = a.size // LANES
        out.append(packed[at:at + n].reshape(a.shape))
        at += n
    return out


def kernel(x, norm_mix_g, w_in, conv_w, conv_b, lru_wa, lru_ba, lru_wx, lru_bx, lru_lambda, w_proj_attn, w_proj_lru, w_out, norm_mlp_g, w_up, w_down, norm_final_g, loss_target, m_norm_mix_g, m_w_in, m_conv_w, m_conv_b, m_lru_wa, m_lru_ba, m_lru_wx, m_lru_bx, m_lru_lambda, m_w_proj_attn, m_w_proj_lru, m_w_out, m_norm_mlp_g, m_w_up, m_w_down, m_norm_final_g, v_norm_mix_g, v_w_in, v_conv_w, v_conv_b, v_lru_wa, v_lru_ba, v_lru_wx, v_lru_bx, v_lru_lambda, v_w_proj_attn, v_w_proj_lru, v_w_out, v_norm_mlp_g, v_w_up, v_w_down, v_norm_final_g):
    xs, tgt = x[0], loss_target[0]
    s, d = xs.shape
    nh = d // HEAD
    ix, iy, ic = _coords()
    core = jnp.reshape(ic, (1,)).astype(jnp.int32)
    chip = jnp.reshape(2 * ix + iy, (1,)).astype(jnp.int32)
    dev = 4 * ix + 2 * iy + ic

    big = [w_in[0], w_proj_attn[0], w_proj_lru[0], w_out[0], w_up[0], w_down[0]]
    big_m = [m_w_in[0], m_w_proj_attn[0], m_w_proj_lru[0], m_w_out[0], m_w_up[0], m_w_down[0]]
    big_v = [v_w_in[0], v_w_proj_attn[0], v_w_proj_lru[0], v_w_out[0], v_w_up[0], v_w_down[0]]
    kinds = ["col", "row", "row", "row", "col", "row"]
    pad_taps = lambda t: jnp.pad(t, ((0, SUBLANES - CONV_TAPS), (0, 0)))
    shards = [w.astype(BF16) for w in big]
    pad_taps2 = lambda t: jnp.pad(t, ((0, 2 * SUBLANES - CONV_TAPS), (0, 0)))
    win, cw_slots = _all_gather("all_gather_w_in", [shards[0], pad_taps2(conv_w[0])], ["col", "slot"], sequencer_id=7)
    wpa, wpl, wout = _all_gather("all_gather_mix", shards[1:4], kinds[1:4], sequencer_id=1)
    wup, wdown = _all_gather("all_gather_mlp", shards[4:], kinds[4:], sequencer_id=5)
    cw8 = jnp.transpose(cw_slots[:, :SUBLANES], (1, 0, 2)).reshape(SUBLANES, d)
    row_id = lax.broadcasted_iota(jnp.int32, (SUBLANES, d), 0)
    vec8 = sum(jnp.where(row_id == k, t, 0.0) for k, t in ((VEC_CB, conv_b), (VEC_BA, lru_ba), (VEC_BX, lru_bx), (VEC_LAM, lru_lambda)))
    wa16, wx16 = lru_wa[0].astype(BF16), lru_wx[0].astype(BF16)
    slopes = 2.0 ** (-8.0 * jnp.arange(1, nh + 1, dtype=F32) / nh)

    def seg_specs(*segs):
        return lambda bm, bn: [pl.BlockSpec((bm, bn), (lambda i, j, kk, sg=sg: (i, sg * (d // bn) + j))) for sg in segs]

    def plain_specs(k):
        return lambda bm, bn: [pl.BlockSpec((bm, bn), lambda i, j, kk: (i, j)) for _ in range(k)]

    xn = _rms_fwd("norm_mix", xs, norm_mix_g)
    proj = _mm_fwd("proj_in", xn, win, 0, 7 * d, [F32], bm=2048)[0]
    att, lse = _attn_fwd(proj, d, slopes)
    xc, hp, h2d, ylru = _lru_fwd(proj, d, cw8, vec8, wa16, wx16)
    pa = _mm_fwd("proj_attn", att, wpa, 0, d, [BF16], bm=2048)[0]

    def merge(acc, pa_b, ga, gl):
        return acc, _sigmoid(ga) * pa_b.astype(F32) + _sigmoid(gl) * acc

    plr, merged = _mm_fwd("proj_lru_merge", ylru, wpl, 0, d, [BF16, BF16], merge, (pa, proj, proj),
                          lambda bm, bn: plain_specs(1)(bm, bn) + seg_specs(SEG_GA, SEG_GL)(bm, bn), bn=512)
    h1 = _mm_fwd("mix_out", merged, wout, 0, d, [F32], lambda acc, r: (acc + r,), (xs,), plain_specs(1))[0]
    hn = _rms_fwd("norm_mlp", h1, norm_mlp_g)

    def relu2(acc):
        return acc, jnp.square(jnp.maximum(acc, 0.0))

    up, hid = _mm_fwd("mlp_up", hn, wup, 0, wup.shape[1], [BF16, BF16], relu2, bm=2048)
    h2 = _mm_fwd("mlp_down", hid, wdown, 0, d, [F32], lambda acc, r: (acc + r,), (h1,), plain_specs(1))[0]
    dh2, dh2b, dg3, loss_lanes = _final_loss(h2, tgt, norm_final_g.reshape(1, d))
    loss_rows = jnp.pad((0.5 / d * jnp.sum(loss_lanes)).reshape(1, 1), ((0, SUBLANES - 1), (0, LANES - 1)))

    def reduce_group(tag, kk, shp, partials, from_sibling, sequencer_id):
        sums = [_chip_sum(f"chip_sum_{tag}_{i}", p, f, k, sh, core) for i, (p, f, k, sh) in enumerate(zip(partials, from_sibling, kk, shp))]
        return list(zip(sums, _exchange_chips(f"rs_chips_{tag}", sums, sequencer_id)))

    g_wdown = _mm_tn("mlp_down_dw", hid, dh2b)
    (dup,), sib_down = _mm_nt("mlp_down_dx", dh2b, wdown, [BF16], lambda acc, u: (acc * (2.0 * jnp.maximum(u.astype(F32), 0.0)),), (up,),
                              plain_specs(1), side=_sibling_side([g_wdown], kinds[5:], [big[5].shape]))
    (red_down,) = reduce_group("mlp_down", kinds[5:], [big[5].shape], [g_wdown], sib_down, 2)
    dup = lax.optimization_barrier((dup, red_down[0]))[0]
    g_wup = _mm_tn("mlp_up_dw", hn, dup)
    (dhn,), sib_up = _mm_nt("mlp_up_dx", dup, wup, [F32], side=_sibling_side([g_wup], kinds[4:5], [big[4].shape]))
    (red_up,) = reduce_group("mlp_up", kinds[4:5], [big[4].shape], [g_wup], sib_up, 10)
    dhn = lax.optimization_barrier((dhn, red_up[0]))[0]
    dh1, dh1b, dg2 = _rms_bwd("norm_mlp_bwd", h1, norm_mlp_g, dhn, dh2)

    def merge_bwd(acc, pa_b, pl_b, ga, gl):
        sa, sl = _sigmoid(ga), _sigmoid(gl)
        return acc * sa, acc * sl, acc * pa_b.astype(F32) * sa * (1.0 - sa), acc * pl_b.astype(F32) * sl * (1.0 - sl)

    dpa, dpl, dga, dgl = _mm_nt("mix_out_dx", dh1b, wout, [BF16] * 4, merge_bwd, (pa, plr, proj, proj),
                                lambda bm, bn: plain_specs(2)(bm, bn) + seg_specs(SEG_GA, SEG_GL)(bm, bn), bn=512)
    g_wout = _mm_tn("mix_out_dw", merged, dh1b)
    datt = _mm_nt("proj_attn_dx", dpa, wpa, [F32], bm=2048)[0]
    g_wpa = _mm_tn("proj_attn_dw", att, dpa)

    def lru_out_bwd(acc, h_b, gate):
        return acc * _gelu(gate), acc * h_b.astype(F32) * _gelu_grad(gate)

    g_wpl = _mm_tn("proj_lru_dw", ylru, dpl)
    shp_mix = [w.shape for w in big[1:4]]
    (dh, dxg), sib_mix = _mm_nt("proj_lru_dx", dpl, wpl, [F32, BF16], lru_out_bwd, (h2d, proj),
                                lambda bm, bn: plain_specs(1)(bm, bn) + seg_specs(SEG_GATE)(bm, bn), bn=512,
                                side=_sibling_side([g_wpa, g_wpl, g_wout], kinds[1:4], shp_mix))
    red_pa, red_pl, red_out = reduce_group("mix", kinds[1:4], shp_mix, [g_wpa, g_wpl, g_wout], sib_mix, 3)
    dh = lax.optimization_barrier((dh, red_down[1]))[0]
    dxr, dwa, dwx, dvec, dconv = _lru_bwd(hp, dh, xc, proj, cw8, wa16, wx16, vec8)
    dproj = _attn_bwd(proj, d, datt, att, lse, slopes, (dxr, dxg, dga, dgl))

    def small_step(tag, grads, ws, ms, vs, like, after, seq=None):
        n_rows = sum(g.size for g in grads) // LANES
        per_dev = -(-n_rows // (N_DEV * SUBLANES)) * SUBLANES
        packed = lax.optimization_barrier((_pack_rows(grads, N_DEV * per_dev), after))[0]
        if seq is None:
            total = _all_reduce_small(f"all_reduce_{tag}", packed)
        else:
            pieces = lax.optimization_barrier((_small_scatter(f"scatter_{tag}", packed, seq[0]), seq[2]))[0]
            total = _small_gather(f"gather_{tag}", _small_sum(f"sum_{tag}", pieces), seq[1])
        w_rows = -(-(sum(w.size for w in ws) // LANES) // SUBLANES) * SUBLANES
        upd = _adamw_small(f"adamw_{tag}", total[:w_rows], _pack_rows(ws, w_rows), _pack_rows(ms, w_rows), _pack_rows(vs, w_rows))
        return _unpack_rows(total, like), [_unpack_rows(t, ws) for t in upd]

    early_w = [conv_b, lru_wa, lru_ba, lru_wx, lru_bx, lru_lambda, norm_mlp_g, norm_final_g]
    early_m = [m_conv_b, m_lru_wa, m_lru_ba, m_lru_wx, m_lru_bx, m_lru_lambda, m_norm_mlp_g, m_norm_final_g]
    early_v = [v_conv_b, v_lru_wa, v_lru_ba, v_lru_wx, v_lru_bx, v_lru_lambda, v_norm_mlp_g, v_norm_final_g]
    early_g = [dconv[CONV_TAPS:CONV_TAPS + 1], dwa, dvec[VEC_BA:VEC_BA + 1], dwx, dvec[VEC_BX:VEC_BX + 1],
               dvec[VEC_LAM:VEC_LAM + 1], dg2, dg3, dconv[0:CONV_TAPS], loss_rows]
    dproj = lax.optimization_barrier((dproj, red_up[1]))[0]
    dproj = lax.optimization_barrier((dproj, red_pa[1], red_pl[1], red_out[1]))[0]
    early_sum, early_upd = small_step("small", early_g, early_w, early_m, early_v,
                                      early_w + [jax.ShapeDtypeStruct((1, CONV_TAPS, d), F32), jax.ShapeDtypeStruct((SUBLANES, LANES), F32)],
                                      dxr, seq=(8, 9, dproj))
    g_cw_full, loss = early_sum[-2], early_sum[-1][0, 0]
    cshard = conv_w.shape[2]
    g_cw = lax.dynamic_slice(g_cw_full, (0, 0, dev * cshard), (1, CONV_TAPS, cshard))
    cw_delta, cw_m, cw_v = (t[:CONV_TAPS][None] for t in _adamw_small(
        "adamw_conv_w", pad_taps(g_cw[0]), pad_taps(conv_w[0]), pad_taps(m_conv_w[0]), pad_taps(v_conv_w[0])))
    half = (big[0].shape[0] // 2, big[0].shape[1])
    g_in0 = _mm_tn("proj_in_dw_0", xn, dproj, part=(0, 2))
    g_in1, sib_in0 = _mm_tn("proj_in_dw_1", xn, dproj, part=(1, 2), side=_sibling_side([g_in0], ["col"], [half]))
    red_in = reduce_group("in_0", ["col"], [half], [g_in0], sib_in0, 4)
    dproj = lax.optimization_barrier((dproj, red_in[0][0], early_sum))[0]
    (dxn0,), sib_in1 = _mm_nt("proj_in_dx_0", dproj, win, [F32], part=(0, 2), side=_sibling_side([g_in1], ["col"], [half]))
    red_in += reduce_group("in_1", ["col"], [half], [g_in1], sib_in1, 6)
    dproj = lax.optimization_barrier((dproj, red_in[1][0]))[0]
    dxn1 = _mm_nt("proj_in_dx_1", dproj, win, [F32], part=(1, 2))[0]
    dxn = lax.optimization_barrier(((dxn0, dxn1), red_in[0][1]))[0]
    grad_x, _, dg1 = _rms_bwd("norm_mix_bwd", xs, norm_mix_g, dxn, dh1)
    red_up, red_down = lax.optimization_barrier(((red_up, red_down), dg1))[0]
    big_out = {i: _adamw_shard(f"adamw_{i}", [red], big[i], big_m[i], big_v[i], chip) for i, red in ((4, red_up), (5, red_down))}
    big_out.update({i: _adamw_shard(f"adamw_{i}", [red], big[i], big_m[i], big_v[i], chip) for i, red in ((1, red_pa), (2, red_pl), (3, red_out))})
    late_sum, late_upd = small_step("norm_mix", [dg1], [norm_mix_g], [m_norm_mix_g], [v_norm_mix_g], [norm_mix_g], (big_out[4], big_out[5]))
    big_out[0] = _adamw_shard("adamw_0", red_in, big[0], big_m[0], big_v[0], chip)
    s_grad = late_sum + early_sum[:-2]
    s_delta, s_m, s_v = (late_upd[j] + early_upd[j] for j in range(3))


    names = ["norm_mix_g", "w_in", "conv_w", "conv_b", "lru_wa", "lru_ba", "lru_wx", "lru_bx", "lru_lambda", "w_proj_attn", "w_proj_lru",
             "w_out", "norm_mlp_g", "w_up", "w_down", "norm_final_g"]
    small_names = ["norm_mix_g", "conv_b", "lru_wa", "lru_ba", "lru_wx", "lru_bx", "lru_lambda", "norm_mlp_g", "norm_final_g"]
    big_names = ["w_in", "w_proj_attn", "w_proj_lru", "w_out", "w_up", "w_down"]
    res = {"conv_w": (g_cw, cw_delta, cw_m, cw_v)}
    for i, nm in enumerate(small_names):
        res[nm] = (s_grad[i], s_delta[i], s_m[i], s_v[i])
    for i, nm in enumerate(big_names):
        res[nm] = tuple(t[None] for t in big_out[i])
    return (loss, grad_x[None], *[res[nm][0] for nm in names], *[res[nm][1] for nm in names],
            *[res[nm][2] for nm in names], *[res[nm][3] for nm in names])
```

```python
import jax
import jax.numpy as jnp
from jax import lax
from jax.experimental import pallas as pl
from jax.experimental.pallas import tpu as pltpu
from jax.experimental.pallas import tpu_sc as plsc

F32, BF16 = jnp.float32, jnp.bfloat16
MESH = pl.DeviceIdType.MESH
HBM = pl.BlockSpec(memory_space=pltpu.HBM)
N_DEV = 8
N_CHIP = 4
HEAD = 128
SPAN = 128
DILATIONS = (1, 4, 16)
CONV_TAPS = 4
LRU_C = 8.0
NORM_EPS = 1e-6
LANES = 128
SUBLANES = 8
VMEM_LIMIT = 56 * 1024 * 1024
ADAM_LR, ADAM_B1, ADAM_B2, ADAM_EPS, ADAM_WD, ADAM_STEP = 0.001, 0.9, 0.999, 1e-08, 0.01, 10
ADAM_C1 = 1.0 - ADAM_B1 ** ADAM_STEP
ADAM_C2 = 1.0 - ADAM_B2 ** ADAM_STEP
NEG = -1e30


def _params(sem=None):
    return pltpu.CompilerParams(dimension_semantics=sem, vmem_limit_bytes=VMEM_LIMIT)


def _sigmoid(v):
    return 1.0 / (1.0 + jnp.exp(-v))


def _gelu(v):
    k = 0.7978845608028654
    return 0.5 * v * (1.0 + jnp.tanh(k * (v + 0.044715 * v * v * v)))


def _gelu_grad(v):
    k = 0.7978845608028654
    t = jnp.tanh(k * (v + 0.044715 * v * v * v))
    return 0.5 * (1.0 + t) + 0.5 * v * (1.0 - t * t) * k * (1.0 + 3.0 * 0.044715 * v * v)


NN = (((1,), (0,)), ((), ()))
NT = (((1,), (1,)), ((), ()))
TN = (((0,), (0,)), ((), ()))


def _mm(name, a, a_spec, b, b_spec, dn, grid, out_shapes, out_specs, acc_block, epilogue=None, extras=(), extra_specs=(), side=None):
    nk, ne, no = grid[2], len(extras), len(out_shapes)
    side_ops, side_shapes, side_copies, make_copies = side if side is not None else ((), (), 0, None)
    ns_in, ns_out = len(side_ops), len(side_shapes)

    def body(*refs):
        a_ref, b_ref = refs[0], refs[1]
        ex, side_in = refs[2:2 + ne], refs[2 + ne:2 + ne + ns_in]
        outs = refs[2 + ne + ns_in:2 + ne + ns_in + no]
        side_out = refs[2 + ne + ns_in + no:2 + ne + ns_in + no + ns_out]
        scratch = refs[2 + ne + ns_in + no + ns_out:]
        at = [pl.program_id(ax) for ax in range(3)]
        if side is not None:
            @pl.when((at[0] == 0) & (at[1] == 0) & (at[2] == 0))
            def _():
                for cp in make_copies(side_in, side_out, scratch[-2], scratch[-1]):
                    cp.start()

        part = lax.dot_general(a_ref[...], b_ref[...], dn, preferred_element_type=F32)

        def finish(acc):
            vals = epilogue(acc, *[e[...] for e in ex]) if epilogue is not None else (acc,)
            for o, v in zip(outs, vals):
                o[...] = v.astype(o.dtype)

        if nk == 1:
            finish(part)
        else:
            acc_ref, k = scratch[0], at[2]

            @pl.when(k == 0)
            def _():
                acc_ref[...] = part

            @pl.when((k > 0) & (k < nk - 1))
            def _():
                acc_ref[...] += part

            @pl.when(k == nk - 1)
            def _():
                finish(acc_ref[...] + part)

        if side is not None:
            @pl.when((at[0] == grid[0] - 1) & (at[1] == grid[1] - 1) & (at[2] == grid[2] - 1))
            def _():
                for cp in make_copies(side_in, side_out, scratch[-2], scratch[-1]):
                    cp.wait()

    scratch_shapes = [pltpu.VMEM(acc_block, F32)] if nk > 1 else []
    if side is not None:
        scratch_shapes += [pltpu.SemaphoreType.DMA((side_copies,)), pltpu.SemaphoreType.DMA((side_copies,))]
    res = pl.pallas_call(
        body, out_shape=[*out_shapes, *side_shapes], grid=grid, in_specs=[a_spec, b_spec, *extra_specs, *[HBM] * ns_in],
        out_specs=[*out_specs, *[HBM] * ns_out], scratch_shapes=scratch_shapes,
        compiler_params=_params(("arbitrary",) * 3 if side is not None else ("parallel", "parallel", "arbitrary")),
        name=name)(a, b, *extras, *side_ops)
    return res if side is None else (res[:no], res[no:])


def _blk(n, pref):
    return pref if n % pref == 0 else n


def _kblk(k):
    return k if k <= 2048 else next(b for b in (2048, 1024, 512) if k % b == 0)


def _mm_fwd(name, a, w, col0, ncols, out_dtypes, epilogue=None, extras=(), extra_specs_fn=None, bm=1024, bn=1024):
    m, k = a.shape
    bm, bn = _blk(m, bm), _blk(ncols, bn)
    bk = _kblk(k)
    nk = k // bk
    cb0 = col0 // bn
    grid = (m // bm, ncols // bn, nk)
    a_spec = pl.BlockSpec((bm, bk), lambda i, j, kk: (i, kk))
    b_spec = pl.BlockSpec((bk, bn), lambda i, j, kk: (kk, cb0 + j))
    shapes = [jax.ShapeDtypeStruct((m, ncols), dt) for dt in out_dtypes]
    specs = [pl.BlockSpec((bm, bn), lambda i, j, kk: (i, j)) for _ in out_dtypes]
    ex_specs = extra_specs_fn(bm, bn) if extra_specs_fn else ()
    return _mm(name, a, a_spec, w, b_spec, NN, grid, shapes, specs, (bm, bn), epilogue, extras, ex_specs)


def _mm_nt(name, a, w, out_dtypes, epilogue=None, extras=(), extra_specs_fn=None, part=(0, 1), side=None, bm=1024, bn=1024):
    n = w.shape[0]
    if a.ndim == 3:
        seg_cols, m, k = a.shape[2], a.shape[1], a.shape[0] * a.shape[2]
    else:
        m, k = a.shape
    m = m // part[1]
    bm, bn = _blk(m, bm), _blk(n, bn)
    bk = _kblk(k)
    grid = (m // bm, n // bn, k // bk)
    i0 = part[0] * (m // bm)
    if a.ndim == 3:
        per = seg_cols // bk
        a_spec = pl.BlockSpec((None, bm, bk), lambda i, j, kk: (kk // per, i0 + i, kk % per))
    else:
        a_spec = pl.BlockSpec((bm, bk), lambda i, j, kk: (i0 + i, kk))
    b_spec = pl.BlockSpec((bn, bk), lambda i, j, kk: (j, kk))
    shapes = [jax.ShapeDtypeStruct((m, n), dt) for dt in out_dtypes]
    specs = [pl.BlockSpec((bm, bn), lambda i, j, kk: (i, j)) for _ in out_dtypes]
    ex_specs = extra_specs_fn(bm, bn) if extra_specs_fn else ()
    return _mm(name, a, a_spec, w, b_spec, NT, grid, shapes, specs, (bm, bn), epilogue, extras, ex_specs, side)


def _mm_tn(name, a, b, part=(0, 1), side=None, bm=1024, bn=2048):
    t, m = a.shape
    n = b.shape[1] if b.ndim == 2 else b.shape[0] * b.shape[2]
    m = m // part[1]
    bm, bn = _blk(m, bm), _blk(n, bn)
    grid = (m // bm, n // bn, 1)
    i0 = part[0] * (m // bm)
    a_spec = pl.BlockSpec((t, bm), lambda i, j, kk: (0, i0 + i))
    if b.ndim == 3:
        per = b.shape[2] // bn
        b_spec = pl.BlockSpec((None, t, bn), lambda i, j, kk: (j // per, 0, j % per))
    else:
        b_spec = pl.BlockSpec((t, bn), lambda i, j, kk: (0, j))
    res = _mm(name, a, a_spec, b, b_spec, TN, grid, [jax.ShapeDtypeStruct((m, n), BF16)],
              [pl.BlockSpec((bm, bn), lambda i, j, kk: (i, j))], (bm, bn), side=side)
    return res[0] if side is None else (res[0][0], res[1])


ROWS = 256


def _row_spec(d):
    return pl.BlockSpec((ROWS, d), lambda i: (i, 0))


def _vec_spec(d, rows=1):
    return pl.BlockSpec((rows, d), lambda i: (0, 0))


def _rms_fwd(name, x, g):
    s, d = x.shape

    def body(x_ref, g_ref, o_ref):
        xv = x_ref[...]
        r = lax.rsqrt(jnp.mean(xv * xv, axis=-1, keepdims=True) + NORM_EPS)
        o_ref[...] = (xv * r * g_ref[...]).astype(BF16)

    return pl.pallas_call(body, out_shape=jax.ShapeDtypeStruct((s, d), BF16), grid=(s // ROWS,),
                          in_specs=[_row_spec(d), _vec_spec(d)], out_specs=_row_spec(d),
                          compiler_params=_params(("parallel",)), name=name)(x, g)


def _rms_bwd_math(xv, g, dy):
    r = lax.rsqrt(jnp.mean(xv * xv, axis=-1, keepdims=True) + NORM_EPS)
    n = xv * r
    z = dy * g
    dx = r * (z - n * jnp.mean(z * n, axis=-1, keepdims=True))
    return dx, jnp.sum(dy * n, axis=0, keepdims=True)


def _rms_bwd(name, x, g, dy, resid):
    s, d = x.shape
    parts = dy if isinstance(dy, tuple) else (dy,)
    per = s // ROWS // len(parts)
    assert all(p.shape == (per * ROWS, d) for p in parts)

    def body(x_ref, g_ref, *refs):
        dy_refs, (r_ref, dx_ref, dxb_ref, dg_ref) = refs[:len(parts)], refs[len(parts):]
        dyv = dy_refs[0][...]
        for p in range(1, len(parts)):
            dyv = jnp.where(pl.program_id(0) >= p * per, dy_refs[p][...], dyv)
        dx, dg = _rms_bwd_math(x_ref[...], g_ref[...], dyv)
        dx = dx + r_ref[...]
        dx_ref[...] = dx
        dxb_ref[...] = dx.astype(BF16)

        @pl.when(pl.program_id(0) == 0)
        def _():
            dg_ref[...] = jnp.zeros_like(dg_ref)

        dg_ref[...] += dg

    part_specs = [pl.BlockSpec((ROWS, d), lambda i, p=p: (jnp.clip(i - p * per, 0, per - 1), 0)) for p in range(len(parts))]
    return pl.pallas_call(
        body, out_shape=[jax.ShapeDtypeStruct((s, d), F32), jax.ShapeDtypeStruct((s, d), BF16), jax.ShapeDtypeStruct((1, d), F32)],
        grid=(s // ROWS,), in_specs=[_row_spec(d), _vec_spec(d), *part_specs, _row_spec(d)],
        out_specs=[_row_spec(d), _row_spec(d), _vec_spec(d)], compiler_params=_params(("arbitrary",)), name=name)(x, g, *parts, resid)


def _final_loss(h2, tgt, g):
    s, d = h2.shape

    def body(x_ref, t_ref, g_ref, dx_ref, dxb_ref, dg_ref, ls_ref):
        xv, gv = x_ref[...], g_ref[...]
        r = lax.rsqrt(jnp.mean(xv * xv, axis=-1, keepdims=True) + NORM_EPS)
        diff = xv * r * gv - t_ref[...]
        dx, dg = _rms_bwd_math(xv, gv, diff * (1.0 / d))
        dx_ref[...] = dx
        dxb_ref[...] = dx.astype(BF16)

        @pl.when(pl.program_id(0) == 0)
        def _():
            dg_ref[...] = jnp.zeros_like(dg_ref)
            ls_ref[...] = jnp.zeros_like(ls_ref)

        dg_ref[...] += dg
        ls_ref[...] += jnp.sum(diff * diff, axis=0, keepdims=True)

    return pl.pallas_call(
        body, out_shape=[jax.ShapeDtypeStruct((s, d), F32), jax.ShapeDtypeStruct((s, d), BF16),
                         jax.ShapeDtypeStruct((1, d), F32), jax.ShapeDtypeStruct((1, d), F32)],
        grid=(s // ROWS,), in_specs=[_row_spec(d), _row_spec(d), _vec_spec(d)],
        out_specs=[_row_spec(d), _row_spec(d), _vec_spec(d), _vec_spec(d)],
        compiler_params=_params(("arbitrary",)), name="final_norm_loss")(h2, tgt, g)


ATTN_Q = 128


ATTN_BATCH = 8


def _attn_units(s):
    units = []
    for gi, d in enumerate(DILATIONS):
        for r in range(d):
            for q0 in range(0, s // d, ATTN_Q):
                k0 = max(q0 - SPAN, 0)
                units.append((gi, d, r, q0, k0, q0 + ATTN_Q - k0))
    return units


def _stream_rows(d, r, start, size):
    return pl.ds(r + start * d, size) if d == 1 else pl.ds(r + start * d, size, stride=d)


def _attn_bias(bias_ref, slope):
    qi = lax.broadcasted_iota(jnp.int32, (ATTN_Q, ATTN_Q + SPAN), 0)
    kj = lax.broadcasted_iota(jnp.int32, (ATTN_Q, ATTN_Q + SPAN), 1)
    dist = SPAN + qi - kj
    valid = (dist >= 0) & (dist <= SPAN)
    for gi, d in enumerate(DILATIONS):
        bias_ref[gi] = jnp.where(valid, -(slope * d) * dist.astype(F32), NEG)


def _attn_scores(q_ref, k_ref, bias_ref, gi, d, r, q0, k0, nk):
    qrows, krows = _stream_rows(d, r, q0, ATTN_Q), _stream_rows(d, r, k0, nk)
    qb = q_ref[qrows, :].astype(BF16)
    kb = k_ref[krows, :].astype(BF16)
    sc = lax.dot_general(qb, kb, NT, preferred_element_type=F32) * (HEAD ** -0.5) + bias_ref[gi, :, pl.ds(ATTN_Q + SPAN - nk, nk)]
    return sc, qb, kb, qrows, krows


def _attn_fwd(proj, dm, slopes):
    s = proj.shape[0]
    units = _attn_units(s)

    def body(sl_ref, q_ref, k_ref, v_ref, att_ref, lse_ref, *scr):
        o_scr, l_scr, bias = scr[:3], scr[3:6], scr[6]
        _attn_bias(bias, sl_ref[pl.program_id(0)])
        for first in range(0, len(units), ATTN_BATCH):
            batch = units[first:first + ATTN_BATCH]
            scored = [_attn_scores(q_ref, k_ref, bias, gi, d, r, q0, k0, nk) for gi, d, r, q0, k0, nk in batch]
            soft = []
            for sc, _, _, _, _ in scored:
                m = jnp.max(sc, axis=-1, keepdims=True)
                p = jnp.exp(sc - m)
                soft.append((m, p, jnp.sum(p, axis=-1, keepdims=True)))
            outs = [lax.dot_general(p.astype(BF16), v_ref[sco[4], :].astype(BF16), NN, preferred_element_type=F32)
                    for (m, p, l), sco in zip(soft, scored)]
            for (gi, *_), (m, p, l), sco, o in zip(batch, soft, scored, outs):
                o_scr[gi][sco[3], :] = o / l
                l_scr[gi][sco[3], :] = jnp.broadcast_to(m + jnp.log(l), (ATTN_Q, HEAD))
        l0, l1, l2 = l_scr[0][...], l_scr[1][...], l_scr[2][...]
        m = jnp.maximum(jnp.maximum(l0, l1), l2)
        w0, w1, w2 = jnp.exp(l0 - m), jnp.exp(l1 - m), jnp.exp(l2 - m)
        tot = w0 + w1 + w2
        att_ref[...] = ((w0 * o_scr[0][...] + w1 * o_scr[1][...] + w2 * o_scr[2][...]) / tot).astype(BF16)
        lse_ref[...] = m + jnp.log(tot)

    def seg(i):
        return pl.BlockSpec((s, HEAD), lambda h: (0, i * (dm // HEAD) + h))

    col = pl.BlockSpec((s, HEAD), lambda h: (0, h))
    return pl.pallas_call(
        body, out_shape=[jax.ShapeDtypeStruct((s, dm), BF16), jax.ShapeDtypeStruct((s, dm), F32)], grid=(dm // HEAD,),
        in_specs=[pl.BlockSpec(memory_space=pltpu.SMEM), seg(0), seg(1), seg(2)], out_specs=[col, col],
        scratch_shapes=[pltpu.VMEM((s, HEAD), F32)] * (2 * len(DILATIONS)) + [pltpu.VMEM((len(DILATIONS), ATTN_Q, ATTN_Q + SPAN), F32)],
        compiler_params=_params(("parallel",)), name="attn_fwd")(slopes, proj, proj, proj)


def _attn_bwd(proj, dm, datt, att, lse, slopes, others):
    s = proj.shape[0]
    units = _attn_units(s)

    def body(sl_ref, q_ref, k_ref, v_ref, do_ref, att_ref, lse_ref, o3, o4, o5, o6, out_ref, dq_scr, dk_scr, dv_scr, dl_scr, bias):
        _attn_bias(bias, sl_ref[pl.program_id(0)])
        delta = jnp.sum(do_ref[...] * att_ref[...].astype(F32), axis=-1, keepdims=True)
        dl_scr[...] = jnp.broadcast_to(delta, (s, HEAD))
        dq_scr[...] = jnp.zeros_like(dq_scr)
        dk_scr[...] = jnp.zeros_like(dk_scr)
        dv_scr[...] = jnp.zeros_like(dv_scr)
        for first in range(0, len(units), ATTN_BATCH):
            scored = [_attn_scores(q_ref, k_ref, bias, gi, d, r, q0, k0, nk) for gi, d, r, q0, k0, nk in units[first:first + ATTN_BATCH]]
            dobs = [do_ref[sco[3], :].astype(BF16) for sco in scored]
            dps = [lax.dot_general(dob, v_ref[sco[4], :].astype(BF16), NT, preferred_element_type=F32) for dob, sco in zip(dobs, scored)]
            ps = [jnp.exp(sco[0] - lse_ref[sco[3], :][:, 0:1]) for sco in scored]
            dss = [(p * (dp - dl_scr[sco[3], :][:, 0:1]) * (HEAD ** -0.5)).astype(BF16) for p, dp, sco in zip(ps, dps, scored)]
            dqs = [lax.dot_general(ds, sco[2], NN, preferred_element_type=F32) for ds, sco in zip(dss, scored)]
            dks = [lax.dot_general(ds, sco[1], TN, preferred_element_type=F32) for ds, sco in zip(dss, scored)]
            dvs = [lax.dot_general(p.astype(BF16), dob, TN, preferred_element_type=F32) for p, dob in zip(ps, dobs)]
            for sco, dq, dk, dv in zip(scored, dqs, dks, dvs):
                dq_scr[sco[3], :] += dq
                dk_scr[sco[4], :] += dk
                dv_scr[sco[4], :] += dv
        for j, scr in enumerate((dq_scr, dk_scr, dv_scr)):
            out_ref[j] = scr[...].astype(BF16)
        for j, other in enumerate((o3, o4, o5, o6)):
            out_ref[3 + j] = other[...]

    def seg(i):
        return pl.BlockSpec((s, HEAD), lambda h: (0, i * (dm // HEAD) + h))

    col = pl.BlockSpec((s, HEAD), lambda h: (0, h))
    return pl.pallas_call(
        body, out_shape=jax.ShapeDtypeStruct((7, s, dm), BF16), grid=(dm // HEAD,),
        in_specs=[pl.BlockSpec(memory_space=pltpu.SMEM), seg(0), seg(1), seg(2), col, col, col, col, col, col, col],
        out_specs=pl.BlockSpec((7, s, HEAD), lambda h: (0, 0, h)),
        scratch_shapes=[pltpu.VMEM((s, HEAD), F32)] * 4 + [pltpu.VMEM((len(DILATIONS), ATTN_Q, ATTN_Q + SPAN), F32)],
        compiler_params=_params(("parallel",)), name="attn_bwd")(slopes, proj, proj, proj, datt, att, lse, *others)


VEC_CB, VEC_BA, VEC_BX, VEC_LAM = 0, 1, 2, 3
SEG_Q, SEG_K, SEG_V, SEG_X, SEG_GATE, SEG_GA, SEG_GL = range(7)


def _softplus(z):
    return jnp.maximum(z, 0.0) + jnp.log1p(jnp.exp(-jnp.abs(z)))


def _gate_math(xc, wa_ref, wx_ref, vec):
    xcb = xc.astype(BF16)
    nh = xc.shape[1] // HEAD
    pre_a = jnp.concatenate([jnp.dot(xcb[:, h * HEAD:(h + 1) * HEAD], wa_ref[h], preferred_element_type=F32) for h in range(nh)], axis=1)
    pre_x = jnp.concatenate([jnp.dot(xcb[:, h * HEAD:(h + 1) * HEAD], wx_ref[h], preferred_element_type=F32) for h in range(nh)], axis=1)
    ra = _sigmoid(pre_a + vec[VEC_BA:VEC_BA + 1])
    ig = _sigmoid(pre_x + vec[VEC_BX:VEC_BX + 1])
    sp = _softplus(-vec[VEC_LAM:VEC_LAM + 1])
    log_a = -LRU_C * ra * sp
    a = jnp.exp(log_a)
    z = 2.0 * log_a
    one_minus_a2 = jnp.where(z > -0.01, -z * (1.0 + z * (0.5 + z * (1.0 / 6.0))), 1.0 - jnp.exp(z))
    mult = jnp.sqrt(one_minus_a2)
    return dict(xcb=xcb, ra=ra, ig=ig, sp=sp, a=a, mult=mult)


def _conv_pad_prev(pad_ref, cur, halo, first):
    pad_ref[0:SUBLANES, :] = jnp.where(first, 0.0, halo)
    pad_ref[SUBLANES:SUBLANES + cur.shape[0], :] = cur


def _shift_rows(x, s, fill, up=False):
    rid = lax.broadcasted_iota(jnp.int32, x.shape, 0)
    if up:
        return jnp.where(rid < SUBLANES - s, pltpu.roll(x, SUBLANES - s, axis=0), fill)
    return jnp.where(rid >= s, pltpu.roll(x, s, axis=0), fill)


def _lru_fwd(proj, d, cw8, vec8, wa, wx):
    s = proj.shape[0]
    hb = ROWS // SUBLANES

    def body(x_ref, halo_ref, g_ref, cw_ref, vec_ref, wa_ref, wx_ref, xc_ref, hp_ref, h2_ref, y_ref, pad, a_scr, u_scr, carry):
        @pl.when(pl.program_id(0) == 0)
        def _():
            carry[...] = jnp.zeros_like(carry)

        _conv_pad_prev(pad, x_ref[...], halo_ref[...], pl.program_id(0) == 0)
        vec = vec_ref[...]
        xc = vec[VEC_CB:VEC_CB + 1]
        for k in range(CONV_TAPS):
            xc = xc + cw_ref[k:k + 1, :] * pad[pl.ds(SUBLANES - (CONV_TAPS - 1) + k, ROWS), :]
        gm = _gate_math(xc, wa_ref, wx_ref, vec)
        xc_ref[...] = xc
        a_scr[...] = gm["a"]
        u_scr[...] = gm["mult"] * (gm["ig"] * xc)

        def group(gi, before):
            rows = pl.ds(pl.multiple_of(gi * SUBLANES, SUBLANES), SUBLANES)
            ca, cb = a_scr[rows, :], u_scr[rows, :]
            for sh in (1, 2, 4):
                cb = ca * _shift_rows(cb, sh, 0.0) + cb
                ca = ca * _shift_rows(ca, sh, 1.0)
            h = cb + ca * before
            hp_ref[rows, :] = jnp.where(lax.broadcasted_iota(jnp.int32, h.shape, 0) == 0, before, pltpu.roll(h, 1, axis=0))
            h2_ref[rows, :] = h.astype(BF16)
            y_ref[rows, :] = (h * _gelu(g_ref[rows, :])).astype(BF16)
            return jnp.broadcast_to(h[SUBLANES - 1:SUBLANES, :], h.shape)

        carry[...] = lax.fori_loop(0, ROWS // SUBLANES, group, carry[...], unroll=2)

    wspec = pl.BlockSpec(wa.shape, lambda i: (0, 0, 0))
    return pl.pallas_call(
        body, out_shape=[jax.ShapeDtypeStruct((s, d), F32)] * 2 + [jax.ShapeDtypeStruct((s, d), BF16)] * 2, grid=(s // ROWS,),
        in_specs=[pl.BlockSpec((ROWS, d), lambda i: (i, SEG_X)),
                  pl.BlockSpec((SUBLANES, d), lambda i: (jnp.maximum(i * hb - 1, 0), SEG_X)),
                  pl.BlockSpec((ROWS, d), lambda i: (i, SEG_GATE)),
                  _vec_spec(d, SUBLANES), _vec_spec(d, SUBLANES), wspec, wspec],
        out_specs=[_row_spec(d)] * 4,
        scratch_shapes=[pltpu.VMEM((ROWS + SUBLANES, d), F32), pltpu.VMEM((ROWS, d), F32), pltpu.VMEM((ROWS, d), F32), pltpu.VMEM((SUBLANES, d), F32)],
        compiler_params=_params(("arbitrary",)), name="lru_fwd")(proj, proj, proj, cw8, vec8, wa, wx)


def _lru_bwd(hp, dh, xc, proj, cw8, wa, wx, vec8):
    s, d = xc.shape
    nh = d // HEAD
    nb = s // ROWS
    hb = ROWS // SUBLANES

    def body(hp_ref, dh_ref, xc_ref, x_ref, xprev_ref, cw_ref, wa_ref, wx_ref, vec_ref, dx_ref, dwa_ref, dwx_ref, dvec_ref, dcw_ref,
             a_scr, g_scr, da_scr, carry, padd, padx, dnext):
        step = pl.program_id(0)

        @pl.when(step == 0)
        def _():
            carry[...] = jnp.zeros_like(carry)
            dnext[...] = jnp.zeros_like(dnext)
            dwa_ref[...] = jnp.zeros_like(dwa_ref)
            dwx_ref[...] = jnp.zeros_like(dwx_ref)
            dvec_ref[...] = jnp.zeros_like(dvec_ref)
            dcw_ref[...] = jnp.zeros_like(dcw_ref)

        xc_v, vec = xc_ref[...], vec_ref[...]
        gm = _gate_math(xc_v, wa_ref, wx_ref, vec)
        ra, ig, sp, a, mult = gm["ra"], gm["ig"], gm["sp"], gm["a"], gm["mult"]
        a_scr[...] = a

        def group(j, after):
            rows = pl.ds(pl.multiple_of((ROWS // SUBLANES - 1 - j) * SUBLANES, SUBLANES), SUBLANES)
            ca, dhv = a_scr[rows, :], dh_ref[rows, :]
            cb = ca * dhv
            for sh in (1, 2, 4):
                cb = ca * _shift_rows(cb, sh, 0.0, up=True) + cb
                ca = ca * _shift_rows(ca, sh, 1.0, up=True)
            c = cb + ca * after
            last = lax.broadcasted_iota(jnp.int32, c.shape, 0) == SUBLANES - 1
            g = dhv + jnp.where(last, after, pltpu.roll(c, SUBLANES - 1, axis=0))
            g_scr[rows, :] = g
            da_scr[rows, :] = g * hp_ref[rows, :]
            return jnp.broadcast_to(c[0:1, :], c.shape)

        carry[...] = lax.fori_loop(0, ROWS // SUBLANES, group, carry[...], unroll=2)
        du, da = g_scr[...], da_scr[...]
        dmult = du * ig * xc_v
        dlog_a = da * a - dmult * (a * a) / mult
        dpre_a = dlog_a * (-LRU_C * sp) * ra * (1.0 - ra)
        dpre_x = du * mult * xc_v * ig * (1.0 - ig)
        dlam = jnp.sum(dlog_a * (-LRU_C * ra), axis=0, keepdims=True) * (-_sigmoid(-vec[VEC_LAM:VEC_LAM + 1]))
        dvec_ref[VEC_BA:VEC_BA + 1, :] += jnp.sum(dpre_a, axis=0, keepdims=True)
        dvec_ref[VEC_BX:VEC_BX + 1, :] += jnp.sum(dpre_x, axis=0, keepdims=True)
        dvec_ref[VEC_LAM:VEC_LAM + 1, :] += dlam
        dab, dxb, xcb = dpre_a.astype(BF16), dpre_x.astype(BF16), gm["xcb"]
        back = []
        for h in range(nh):
            cols = slice(h * HEAD, (h + 1) * HEAD)
            dwa_ref[h] += lax.dot_general(xcb[:, cols], dab[:, cols], TN, preferred_element_type=F32)
            dwx_ref[h] += lax.dot_general(xcb[:, cols], dxb[:, cols], TN, preferred_element_type=F32)
            back.append(lax.dot_general(dab[:, cols], wa_ref[h], NT, preferred_element_type=F32)
                        + lax.dot_general(dxb[:, cols], wx_ref[h], NT, preferred_element_type=F32))
        dc = du * mult * ig + jnp.concatenate(back, axis=1)
        padd[0:ROWS, :] = dc
        padd[ROWS:ROWS + SUBLANES, :] = dnext[...]
        dnext[...] = dc[0:SUBLANES, :]
        _conv_pad_prev(padx, x_ref[...], xprev_ref[...], step == nb - 1)
        dx = jnp.zeros_like(dc)
        for k in range(CONV_TAPS):
            dx = dx + cw_ref[k:k + 1, :] * padd[pl.ds(CONV_TAPS - 1 - k, ROWS), :]
            dcw_ref[k:k + 1, :] += jnp.sum(dc * padx[pl.ds(SUBLANES - (CONV_TAPS - 1) + k, ROWS), :], axis=0, keepdims=True)
        dcw_ref[CONV_TAPS:CONV_TAPS + 1, :] += jnp.sum(dc, axis=0, keepdims=True)
        dx_ref[...] = dx.astype(BF16)

    rows_rev = pl.BlockSpec((ROWS, d), lambda i: (nb - 1 - i, 0))
    wspec = pl.BlockSpec(wa.shape, lambda i: (0, 0, 0))
    return pl.pallas_call(
        body, out_shape=[jax.ShapeDtypeStruct((s, d), BF16), jax.ShapeDtypeStruct(wa.shape, F32), jax.ShapeDtypeStruct(wa.shape, F32),
                         jax.ShapeDtypeStruct((SUBLANES, d), F32), jax.ShapeDtypeStruct((SUBLANES, d), F32)],
        grid=(nb,),
        in_specs=[rows_rev, rows_rev, rows_rev, pl.BlockSpec((ROWS, d), lambda i: (nb - 1 - i, SEG_X)),
                  pl.BlockSpec((SUBLANES, d), lambda i: (jnp.maximum((nb - 1 - i) * hb - 1, 0), SEG_X)),
                  _vec_spec(d, SUBLANES), wspec, wspec, _vec_spec(d, SUBLANES)],
        out_specs=[rows_rev, wspec, wspec, _vec_spec(d, SUBLANES), _vec_spec(d, SUBLANES)],
        scratch_shapes=[pltpu.VMEM((ROWS, d), F32)] * 3 + [pltpu.VMEM((SUBLANES, d), F32)]
        + [pltpu.VMEM((ROWS + SUBLANES, d), F32)] * 2 + [pltpu.VMEM((SUBLANES, d), F32)],
        compiler_params=_params(("arbitrary",)), name="lru_bwd")(hp, dh, xc, proj, proj, cw8, wa, wx, vec8)


def _coords():
    return lax.axis_index("x"), lax.axis_index("y"), lax.axis_index("c")


def _other_chips(x, y):
    return [(1 - x, y), (x, 1 - y), (1 - x, 1 - y)]


def _slab(ref, kind, shard_shape, idx, half=None):
    r, c = shard_shape
    r0, nr = (0, r) if half is None else (half * (r // 2), r // 2)
    if kind == "col":
        return ref.at[pl.ds(r0, nr), pl.ds(pl.multiple_of(idx * c, LANES), c)]
    if kind == "row":
        return ref.at[pl.ds(pl.multiple_of(idx * r, SUBLANES) + r0, nr), :]
    return ref.at[idx, pl.ds(r0, nr), :]


def _full_shape(shard_shape, kind):
    r, c = shard_shape
    return {"col": (r, c * N_DEV), "row": (r * N_DEV, c), "slot": (N_DEV, r, c)}[kind]


def _handshake(peers):
    barrier = pltpu.get_barrier_semaphore()
    for peer in peers:
        pl.semaphore_signal(barrier, inc=1, device_id=peer, device_id_type=MESH)
    pl.semaphore_wait(barrier, len(peers))


def _launch(name, body, out_shape, operands, sems, sequencer_id):
    if sequencer_id is None:
        return pl.pallas_call(body, out_shape=out_shape, in_specs=[HBM] * len(operands), out_specs=[HBM] * len(out_shape),
                              scratch_shapes=sems, name=name)(*operands)
    return pl.kernel(body, out_type=out_shape, mesh=plsc.ScalarSubcoreMesh(axis_name="seq", num_cores=1), name=name,
                     scratch_types=sems, compiler_params=pltpu.CompilerParams(collective_id=sequencer_id))(*operands)


AG_COPIES = 10


def _all_gather(name, shards, kinds, sequencer_id=None):
    n = len(shards)
    shapes = [s.shape for s in shards]

    def body(*refs):
        ins, outs = refs[:n], refs[n:2 * n]
        send_sems, recv_sems, local_sems = refs[2 * n:]
        x, y, c = _coords()
        me, sib, xn, yn, dg = (x, y, c), (x, y, 1 - c), (1 - x, y, c), (x, 1 - y, c), (1 - x, 1 - y, c)
        if sequencer_id is not None:
            _handshake([sib, xn, yn])

        def part(i, dev, half=None):
            return _slab(outs[i], kinds[i], shapes[i], 4 * dev[0] + 2 * dev[1] + dev[2], half)

        def copy(i, k, block, half, to, own=False):
            r = shapes[i][0]
            src = part(i, block, half) if not own else (ins[i] if half is None else ins[i].at[pl.ds(half * (r // 2), r // 2), :])
            return pltpu.make_async_remote_copy(
                src_ref=src, dst_ref=part(i, block, half), send_sem=send_sems.at[AG_COPIES * i + k],
                recv_sem=recv_sems.at[AG_COPIES * i + k], device_id=to, device_id_type=MESH)

        def other_core(dev):
            return (dev[0], dev[1], 1 - c)

        started = []

        def start(cp):
            cp.start()
            started.append(cp)

        for i in range(n):
            start(copy(i, 1, me, 0, xn, own=True))
            start(copy(i, 4, me, 1, yn, own=True))
            start(copy(i, 2, me, 1, xn, own=True))
            start(copy(i, 3, me, 0, yn, own=True))
            start(copy(i, 0, me, None, sib, own=True))
        mine = [pltpu.make_async_copy(ins[i], part(i, me), local_sems.at[i]) for i in range(n)]
        for cp in mine:
            cp.start()
        for i in range(n):
            copy(i, 1, xn, 0, me).wait_recv()
            start(copy(i, 5, xn, 0, yn))
            copy(i, 4, yn, 1, me).wait_recv()
            start(copy(i, 6, yn, 1, xn))
        for i in range(n):
            copy(i, 2, xn, 1, me).wait_recv()
            start(copy(i, 7, xn, None, sib))
            copy(i, 3, yn, 0, me).wait_recv()
            start(copy(i, 8, yn, None, sib))
        for i in range(n):
            copy(i, 5, dg, 0, me).wait_recv()
            copy(i, 6, dg, 1, me).wait_recv()
            start(copy(i, 9, dg, None, sib))
        for i in range(n):
            copy(i, 0, sib, None, me).wait_recv()
            for k, dev in ((7, xn), (8, yn), (9, dg)):
                copy(i, k, other_core(dev), None, me).wait_recv()
        for cp in started:
            cp.wait_send()
        for cp in mine:
            cp.wait()

    out_shape = [jax.ShapeDtypeStruct(_full_shape(s.shape, k), s.dtype) for s, k in zip(shards, kinds)]
    sems = [pltpu.SemaphoreType.DMA((AG_COPIES * n,)), pltpu.SemaphoreType.DMA((AG_COPIES * n,)), pltpu.SemaphoreType.DMA((n,))]
    return _launch(name, body, out_shape, shards, sems, sequencer_id)


def _sibling_copies(kinds, shard_shapes):
    def make(ins, outs, send_sems, recv_sems):
        x, y, c = _coords()
        return [pltpu.make_async_remote_copy(
            src_ref=_slab(ins[i], kinds[i], shard_shapes[i], 2 * q + (1 - c)), dst_ref=outs[i].at[q],
            send_sem=send_sems.at[N_CHIP * i + q], recv_sem=recv_sems.at[N_CHIP * i + q],
            device_id=(x, y, 1 - c), device_id_type=MESH) for i in range(len(ins)) for q in range(N_CHIP)]
    return make


def _sibling_side(partials, kinds, shard_shapes):
    return (partials, [jax.ShapeDtypeStruct((N_CHIP, *s), BF16) for s in shard_shapes], N_CHIP * len(partials),
            _sibling_copies(kinds, shard_shapes))


def _exchange_chips(name, chip_sums, sequencer_id=None):
    n = len(chip_sums)

    def body(*refs):
        ins, outs = refs[:n], refs[n:2 * n]
        send_sems, recv_sems = refs[2 * n:]
        x, y, c = _coords()
        if sequencer_id is not None:
            _handshake([(cx, cy, c) for cx, cy in _other_chips(x, y)])
        cps = []
        for i in range(n):
            for k, (cx, cy) in enumerate(_other_chips(x, y)):
                cps.append(pltpu.make_async_remote_copy(
                    src_ref=ins[i].at[2 * cx + cy], dst_ref=outs[i].at[k], send_sem=send_sems.at[3 * i + k],
                    recv_sem=recv_sems.at[3 * i + k], device_id=(cx, cy, c), device_id_type=MESH))
        for cp in cps:
            cp.start()
        for cp in cps:
            cp.wait()

    return _launch(name, body, [jax.ShapeDtypeStruct((3, *t.shape[1:]), BF16) for t in chip_sums], chip_sums,
                   [pltpu.SemaphoreType.DMA((3 * n,)), pltpu.SemaphoreType.DMA((3 * n,))], sequencer_id)


def _all_peers(x, y, c):
    return [(x ^ (k >> 2), y ^ ((k >> 1) & 1), c ^ (k & 1)) for k in range(1, N_DEV)]


def _small_scatter(name, packed, sequencer_id):
    rows = packed.shape[0] // N_DEV

    def body(p_ref, rb_ref, send_sems, recv_sems, local_sem):
        x, y, c = _coords()
        me = 4 * x + 2 * y + c
        peers = _all_peers(x, y, c)
        _handshake(peers)

        def piece(idx):
            return p_ref.at[pl.ds(pl.multiple_of(idx * rows, SUBLANES), rows), :]

        cps = [pltpu.make_async_remote_copy(src_ref=piece(me ^ k), dst_ref=rb_ref.at[k], send_sem=send_sems.at[k], recv_sem=recv_sems.at[k],
                                            device_id=peers[k - 1], device_id_type=MESH) for k in range(1, N_DEV)]
        for cp in cps:
            cp.start()
        mine = pltpu.make_async_copy(piece(me), rb_ref.at[0], local_sem)
        mine.start()
        for cp in cps:
            cp.wait()
        mine.wait()

    return _launch(name, body, [jax.ShapeDtypeStruct((N_DEV, rows, LANES), F32)], [packed],
                   [pltpu.SemaphoreType.DMA((N_DEV,)), pltpu.SemaphoreType.DMA((N_DEV,)), pltpu.SemaphoreType.DMA], sequencer_id)[0]


def _small_sum(name, pieces):
    def body(p_ref, o_ref):
        acc = p_ref[0]
        for k in range(1, N_DEV):
            acc = acc + p_ref[k]
        o_ref[...] = acc

    vm = pl.BlockSpec(memory_space=pltpu.VMEM)
    return pl.pallas_call(body, out_shape=jax.ShapeDtypeStruct(pieces.shape[1:], F32), in_specs=[vm], out_specs=vm, name=name)(pieces)


def _small_gather(name, tot, sequencer_id):
    rows = tot.shape[0]

    def body(t_ref, out_ref, send_sems, recv_sems, local_sem):
        x, y, c = _coords()
        me = 4 * x + 2 * y + c
        peers = _all_peers(x, y, c)
        _handshake(peers)

        def piece(idx):
            return out_ref.at[pl.ds(pl.multiple_of(idx * rows, SUBLANES), rows), :]

        cps = [pltpu.make_async_remote_copy(src_ref=t_ref, dst_ref=piece(me), send_sem=send_sems.at[k], recv_sem=recv_sems.at[k],
                                            device_id=peers[k - 1], device_id_type=MESH) for k in range(1, N_DEV)]
        for cp in cps:
            cp.start()
        mine = pltpu.make_async_copy(t_ref, piece(me), local_sem)
        mine.start()
        for cp in cps:
            cp.wait()
        mine.wait()

    return _launch(name, body, [jax.ShapeDtypeStruct((N_DEV * rows, LANES), F32)], [tot],
                   [pltpu.SemaphoreType.DMA((N_DEV,)), pltpu.SemaphoreType.DMA((N_DEV,)), pltpu.SemaphoreType.DMA], sequencer_id)[0]


def _all_reduce_small(name, packed):
    rows = packed.shape[0] // N_DEV

    def body(p_ref, out_ref, rb, tot, send_sems, recv_sems):
        x, y, c = _coords()
        me = 4 * x + 2 * y + c

        def peer(k):
            return (x ^ (k >> 2), y ^ ((k >> 1) & 1), c ^ (k & 1))

        def rows_of(idx):
            return pl.ds(pl.multiple_of(idx * rows, SUBLANES), rows)

        def piece(ref, idx):
            return ref.at[rows_of(idx), :]

        scatter = [pltpu.make_async_remote_copy(src_ref=piece(p_ref, me ^ k), dst_ref=rb.at[k], send_sem=send_sems.at[k],
                                                recv_sem=recv_sems.at[k], device_id=peer(k), device_id_type=MESH) for k in range(1, N_DEV)]
        for cp in scatter:
            cp.start()
        acc = p_ref[rows_of(me), :]
        for cp in scatter:
            cp.wait_recv()
        for k in range(1, N_DEV):
            acc = acc + rb[k]
        tot[...] = acc
        out_ref[rows_of(me), :] = acc
        gather = [pltpu.make_async_remote_copy(src_ref=tot, dst_ref=piece(out_ref, me), send_sem=send_sems.at[N_DEV + k],
                                               recv_sem=recv_sems.at[N_DEV + k], device_id=peer(k), device_id_type=MESH)
                  for k in range(1, N_DEV)]
        for cp in gather:
            cp.start()
        for k in range(1, N_DEV):
            pltpu.make_async_remote_copy(src_ref=tot, dst_ref=piece(out_ref, me ^ k), send_sem=send_sems.at[N_DEV + k],
                                         recv_sem=recv_sems.at[N_DEV + k], device_id=peer(k), device_id_type=MESH).wait_recv()
        for cp in scatter + gather:
            cp.wait_send()

    vm = pl.BlockSpec(memory_space=pltpu.VMEM)
    return pl.pallas_call(
        body, out_shape=jax.ShapeDtypeStruct(packed.shape, F32), in_specs=[vm], out_specs=vm,
        scratch_shapes=[pltpu.VMEM((N_DEV, rows, LANES), F32), pltpu.VMEM((rows, LANES), F32),
                        pltpu.SemaphoreType.DMA((2 * N_DEV,)), pltpu.SemaphoreType.DMA((2 * N_DEV,))],
        compiler_params=pltpu.CompilerParams(vmem_limit_bytes=VMEM_LIMIT), name=name)(packed)


def _adamw_math(g, w, m, v):
    m = ADAM_B1 * m + (1.0 - ADAM_B1) * g
    v = ADAM_B2 * v + (1.0 - ADAM_B2) * (g * g)
    delta = -ADAM_LR * ((m / ADAM_C1) / (jnp.sqrt(v / ADAM_C2) + ADAM_EPS) + ADAM_WD * w)
    return delta, m, v


def _slab_spec(kind, shard_shape, tr, slab_of):
    r, c = shard_shape
    if kind == "col":
        return pl.BlockSpec((tr, c), lambda q, i, sc: (i, slab_of(q, sc)))
    return pl.BlockSpec((tr, c), lambda q, i, sc: (slab_of(q, sc) * (r // tr) + i, 0))


def _chip_sum(name, partial, recv, kind, shard_shape, core):
    r, c = shard_shape
    tr = _blk(r, 1024)

    def body(core_ref, p_ref, r_ref, o_ref):
        o_ref[...] = (p_ref[...].astype(F32) + r_ref[...].astype(F32)).astype(BF16)

    spec4 = pl.BlockSpec((None, tr, c), lambda q, i, sc: (q, i, 0))
    grid_spec = pltpu.PrefetchScalarGridSpec(
        num_scalar_prefetch=1, grid=(N_CHIP, r // tr),
        in_specs=[_slab_spec(kind, shard_shape, tr, lambda q, sc: 2 * q + sc[0]), spec4], out_specs=spec4)
    return pl.pallas_call(body, out_shape=jax.ShapeDtypeStruct((N_CHIP, r, c), BF16), grid_spec=grid_spec,
                          compiler_params=_params(("parallel", "parallel")), name=name)(core, partial, recv)


def _adamw_shard(name, parts, w, m, v, chip):
    r, c = w.shape
    n_parts = len(parts)
    tr = _blk(r // n_parts, 256)
    per = r // n_parts // tr

    def body(chip_ref, *refs):
        src, (w_ref, m_ref, v_ref), (g_out, d_out, m_out, v_out) = refs[:2 * n_parts], refs[2 * n_parts:2 * n_parts + 3], refs[2 * n_parts + 3:]
        for p in range(n_parts):
            @pl.when(pl.program_id(0) // per == p)
            def _():
                g = src[2 * p][...].astype(F32)
                for k in range(3):
                    g = g + src[2 * p + 1][k].astype(F32)
                g_out[...] = g
                d_out[...], m_out[...], v_out[...] = _adamw_math(g, w_ref[...], m_ref[...], v_ref[...])

    def part_specs(p):
        at = lambda i: jnp.clip(i - p * per, 0, per - 1)
        return [pl.BlockSpec((None, tr, c), lambda i, sc: (sc[0], at(i), 0)), pl.BlockSpec((3, tr, c), lambda i, sc: (0, at(i), 0))]

    blk = pl.BlockSpec((tr, c), lambda i, sc: (i, 0))
    grid_spec = pltpu.PrefetchScalarGridSpec(
        num_scalar_prefetch=1, grid=(r // tr,), in_specs=[s for p in range(n_parts) for s in part_specs(p)] + [blk, blk, blk], out_specs=[blk] * 4)
    return pl.pallas_call(body, out_shape=[jax.ShapeDtypeStruct((r, c), F32)] * 4, grid_spec=grid_spec,
                          compiler_params=_params(("parallel",)), name=name)(chip, *[a for p in parts for a in p], w, m, v)


def _adamw_small(name, g, w, m, v):
    def body(g_ref, w_ref, m_ref, v_ref, d_out, m_out, v_out):
        d_out[...], m_out[...], v_out[...] = _adamw_math(g_ref[...], w_ref[...], m_ref[...], v_ref[...])

    vm = pl.BlockSpec(memory_space=pltpu.VMEM)
    return pl.pallas_call(body, out_shape=[jax.ShapeDtypeStruct(g.shape, F32)] * 3, in_specs=[vm] * 4, out_specs=[vm] * 3,
                          compiler_params=pltpu.CompilerParams(vmem_limit_bytes=VMEM_LIMIT), name=name)(g, w, m, v)


def _pack_rows(arrays, total_rows):
    flat = [a.reshape(-1, LANES) for a in arrays]
    used = sum(f.shape[0] for f in flat)
    return jnp.concatenate(flat + [jnp.zeros((total_rows - used, LANES), F32)], axis=0)


def _unpack_rows(packed, like):
    out, at = [], 0
    for a in like:
        n = a.size // LANES
        out.append(packed[at:at + n].reshape(a.shape))
        at += n
    return out


def kernel(x, norm_mix_g, w_in, conv_w, conv_b, lru_wa, lru_ba, lru_wx, lru_bx, lru_lambda, w_proj_attn, w_proj_lru, w_out, norm_mlp_g, w_up, w_down, norm_final_g, loss_target, m_norm_mix_g, m_w_in, m_conv_w, m_conv_b, m_lru_wa, m_lru_ba, m_lru_wx, m_lru_bx, m_lru_lambda, m_w_proj_attn, m_w_proj_lru, m_w_out, m_norm_mlp_g, m_w_up, m_w_down, m_norm_final_g, v_norm_mix_g, v_w_in, v_conv_w, v_conv_b, v_lru_wa, v_lru_ba, v_lru_wx, v_lru_bx, v_lru_lambda, v_w_proj_attn, v_w_proj_lru, v_w_out, v_norm_mlp_g, v_w_up, v_w_down, v_norm_final_g):
    xs, tgt = x[0], loss_target[0]
    s, d = xs.shape
    nh = d // HEAD
    ix, iy, ic = _coords()
    core = jnp.reshape(ic, (1,)).astype(jnp.int32)
    chip = jnp.reshape(2 * ix + iy, (1,)).astype(jnp.int32)
    dev = 4 * ix + 2 * iy + ic

    big = [w_in[0], w_proj_attn[0], w_proj_lru[0], w_out[0], w_up[0], w_down[0]]
    big_m = [m_w_in[0], m_w_proj_attn[0], m_w_proj_lru[0], m_w_out[0], m_w_up[0], m_w_down[0]]
    big_v = [v_w_in[0], v_w_proj_attn[0], v_w_proj_lru[0], v_w_out[0], v_w_up[0], v_w_down[0]]
    kinds = ["col", "row", "row", "row", "col", "row"]
    pad_taps = lambda t: jnp.pad(t, ((0, SUBLANES - CONV_TAPS), (0, 0)))
    shards = [w.astype(BF16) for w in big]
    pad_taps2 = lambda t: jnp.pad(t, ((0, 2 * SUBLANES - CONV_TAPS), (0, 0)))
    win, cw_slots = _all_gather("all_gather_w_in", [shards[0], pad_taps2(conv_w[0])], ["col", "slot"], sequencer_id=7)
    wpa, wpl, wout = _all_gather("all_gather_mix", shards[1:4], kinds[1:4], sequencer_id=1)
    wup, wdown = _all_gather("all_gather_mlp", shards[4:], kinds[4:], sequencer_id=5)
    cw8 = jnp.transpose(cw_slots[:, :SUBLANES], (1, 0, 2)).reshape(SUBLANES, d)
    row_id = lax.broadcasted_iota(jnp.int32, (SUBLANES, d), 0)
    vec8 = sum(jnp.where(row_id == k, t, 0.0) for k, t in ((VEC_CB, conv_b), (VEC_BA, lru_ba), (VEC_BX, lru_bx), (VEC_LAM, lru_lambda)))
    wa16, wx16 = lru_wa[0].astype(BF16), lru_wx[0].astype(BF16)
    slopes = 2.0 ** (-8.0 * jnp.arange(1, nh + 1, dtype=F32) / nh)

    def seg_specs(*segs):
        return lambda bm, bn: [pl.BlockSpec((bm, bn), (lambda i, j, kk, sg=sg: (i, sg * (d // bn) + j))) for sg in segs]

    def plain_specs(k):
        return lambda bm, bn: [pl.BlockSpec((bm, bn), lambda i, j, kk: (i, j)) for _ in range(k)]

    xn = _rms_fwd("norm_mix", xs, norm_mix_g)
    proj = _mm_fwd("proj_in", xn, win, 0, 7 * d, [F32], bm=2048)[0]
    att, lse = _attn_fwd(proj, d, slopes)
    xc, hp, h2d, ylru = _lru_fwd(proj, d, cw8, vec8, wa16, wx16)
    pa = _mm_fwd("proj_attn", att, wpa, 0, d, [BF16], bm=2048)[0]

    def merge(acc, pa_b, ga, gl):
        return acc, _sigmoid(ga) * pa_b.astype(F32) + _sigmoid(gl) * acc

    plr, merged = _mm_fwd("proj_lru_merge", ylru, wpl, 0, d, [BF16, BF16], merge, (pa, proj, proj),
                          lambda bm, bn: plain_specs(1)(bm, bn) + seg_specs(SEG_GA, SEG_GL)(bm, bn), bn=512)
    h1 = _mm_fwd("mix_out", merged, wout, 0, d, [F32], lambda acc, r: (acc + r,), (xs,), plain_specs(1))[0]
    hn = _rms_fwd("norm_mlp", h1, norm_mlp_g)

    def relu2(acc):
        return acc, jnp.square(jnp.maximum(acc, 0.0))

    up, hid = _mm_fwd("mlp_up", hn, wup, 0, wup.shape[1], [BF16, BF16], relu2, bm=2048)
    h2 = _mm_fwd("mlp_down", hid, wdown, 0, d, [F32], lambda acc, r: (acc + r,), (h1,), plain_specs(1))[0]
    dh2, dh2b, dg3, loss_lanes = _final_loss(h2, tgt, norm_final_g.reshape(1, d))
    loss_rows = jnp.pad((0.5 / d * jnp.sum(loss_lanes)).reshape(1, 1), ((0, SUBLANES - 1), (0, LANES - 1)))

    def reduce_group(tag, kk, shp, partials, from_sibling, sequencer_id):
        sums = [_chip_sum(f"chip_sum_{tag}_{i}", p, f, k, sh, core) for i, (p, f, k, sh) in enumerate(zip(partials, from_sibling, kk, shp))]
        return list(zip(sums, _exchange_chips(f"rs_chips_{tag}", sums, sequencer_id)))

    g_wdown = _mm_tn("mlp_down_dw", hid, dh2b)
    (dup,), sib_down = _mm_nt("mlp_down_dx", dh2b, wdown, [BF16], lambda acc, u: (acc * (2.0 * jnp.maximum(u.astype(F32), 0.0)),), (up,),
                              plain_specs(1), side=_sibling_side([g_wdown], kinds[5:], [big[5].shape]))
    (red_down,) = reduce_group("mlp_down", kinds[5:], [big[5].shape], [g_wdown], sib_down, 2)
    dup = lax.optimization_barrier((dup, red_down[0]))[0]
    g_wup = _mm_tn("mlp_up_dw", hn, dup)
    (dhn,), sib_up = _mm_nt("mlp_up_dx", dup, wup, [F32], side=_sibling_side([g_wup], kinds[4:5], [big[4].shape]))
    (red_up,) = reduce_group("mlp_up", kinds[4:5], [big[4].shape], [g_wup], sib_up, 10)
    dhn = lax.optimization_barrier((dhn, red_up[0]))[0]
    dh1, dh1b, dg2 = _rms_bwd("norm_mlp_bwd", h1, norm_mlp_g, dhn, dh2)

    def merge_bwd(acc, pa_b, pl_b, ga, gl):
        sa, sl = _sigmoid(ga), _sigmoid(gl)
        return acc * sa, acc * sl, acc * pa_b.astype(F32) * sa * (1.0 - sa), acc * pl_b.astype(F32) * sl * (1.0 - sl)

    dpa, dpl, dga, dgl = _mm_nt("mix_out_dx", dh1b, wout, [BF16] * 4, merge_bwd, (pa, plr, proj, proj),
                                lambda bm, bn: plain_specs(2)(bm, bn) + seg_specs(SEG_GA, SEG_GL)(bm, bn), bn=512)
    g_wout = _mm_tn("mix_out_dw", merged, dh1b)
    datt = _mm_nt("proj_attn_dx", dpa, wpa, [F32], bm=2048)[0]
    g_wpa = _mm_tn("proj_attn_dw", att, dpa)

    def lru_out_bwd(acc, h_b, gate):
        return acc * _gelu(gate), acc * h_b.astype(F32) * _gelu_grad(gate)

    g_wpl = _mm_tn("proj_lru_dw", ylru, dpl)
    shp_mix = [w.shape for w in big[1:4]]
    (dh, dxg), sib_mix = _mm_nt("proj_lru_dx", dpl, wpl, [F32, BF16], lru_out_bwd, (h2d, proj),
                                lambda bm, bn: plain_specs(1)(bm, bn) + seg_specs(SEG_GATE)(bm, bn), bn=512,
                                side=_sibling_side([g_wpa, g_wpl, g_wout], kinds[1:4], shp_mix))
    red_pa, red_pl, red_out = reduce_group("mix", kinds[1:4], shp_mix, [g_wpa, g_wpl, g_wout], sib_mix, 3)
    dh = lax.optimization_barrier((dh, red_down[1]))[0]
    dxr, dwa, dwx, dvec, dconv = _lru_bwd(hp, dh, xc, proj, cw8, wa16, wx16, vec8)
    dproj = _attn_bwd(proj, d, datt, att, lse, slopes, (dxr, dxg, dga, dgl))

    def small_step(tag, grads, ws, ms, vs, like, after, seq=None):
        n_rows = sum(g.size for g in grads) // LANES
        per_dev = -(-n_rows // (N_DEV * SUBLANES)) * SUBLANES
        packed = lax.optimization_barrier((_pack_rows(grads, N_DEV * per_dev), after))[0]
        if seq is None:
            total = _all_reduce_small(f"all_reduce_{tag}", packed)
        else:
            pieces = lax.optimization_barrier((_small_scatter(f"scatter_{tag}", packed, seq[0]), seq[2]))[0]
            total = _small_gather(f"gather_{tag}", _small_sum(f"sum_{tag}", pieces), seq[1])
        w_rows = -(-(sum(w.size for w in ws) // LANES) // SUBLANES) * SUBLANES
        upd = _adamw_small(f"adamw_{tag}", total[:w_rows], _pack_rows(ws, w_rows), _pack_rows(ms, w_rows), _pack_rows(vs, w_rows))
        return _unpack_rows(total, like), [_unpack_rows(t, ws) for t in upd]

    early_w = [conv_b, lru_wa, lru_ba, lru_wx, lru_bx, lru_lambda, norm_mlp_g, norm_final_g]
    early_m = [m_conv_b, m_lru_wa, m_lru_ba, m_lru_wx, m_lru_bx, m_lru_lambda, m_norm_mlp_g, m_norm_final_g]
    early_v = [v_conv_b, v_lru_wa, v_lru_ba, v_lru_wx, v_lru_bx, v_lru_lambda, v_norm_mlp_g, v_norm_final_g]
    early_g = [dconv[CONV_TAPS:CONV_TAPS + 1], dwa, dvec[VEC_BA:VEC_BA + 1], dwx, dvec[VEC_BX:VEC_BX + 1],
               dvec[VEC_LAM:VEC_LAM + 1], dg2, dg3, dconv[0:CONV_TAPS], loss_rows]
    dproj = lax.optimization_barrier((dproj, red_up[1]))[0]
    dproj = lax.optimization_barrier((dproj, red_pa[1], red_pl[1], red_out[1]))[0]
    early_sum, early_upd = small_step("small", early_g, early_w, early_m, early_v,
                                      early_w + [jax.ShapeDtypeStruct((1, CONV_TAPS, d), F32), jax.ShapeDtypeStruct((SUBLANES, LANES), F32)],
                                      dxr, seq=(8, 9, dproj))
    g_cw_full, loss = early_sum[-2], early_sum[-1][0, 0]
    cshard = conv_w.shape[2]
    g_cw = lax.dynamic_slice(g_cw_full, (0, 0, dev * cshard), (1, CONV_TAPS, cshard))
    cw_delta, cw_m, cw_v = (t[:CONV_TAPS][None] for t in _adamw_small(
        "adamw_conv_w", pad_taps(g_cw[0]), pad_taps(conv_w[0]), pad_taps(m_conv_w[0]), pad_taps(v_conv_w[0])))
    half = (big[0].shape[0] // 2, big[0].shape[1])
    g_in0 = _mm_tn("proj_in_dw_0", xn, dproj, part=(0, 2))
    g_in1, sib_in0 = _mm_tn("proj_in_dw_1", xn, dproj, part=(1, 2), side=_sibling_side([g_in0], ["col"], [half]))
    red_in = reduce_group("in_0", ["col"], [half], [g_in0], sib_in0, 4)
    dproj = lax.optimization_barrier((dproj, red_in[0][0], early_sum))[0]
    (dxn0,), sib_in1 = _mm_nt("proj_in_dx_0", dproj, win, [F32], part=(0, 2), side=_sibling_side([g_in1], ["col"], [half]))
    red_in += reduce_group("in_1", ["col"], [half], [g_in1], sib_in1, 6)
    dproj = lax.optimization_barrier((dproj, red_in[1][0]))[0]
    dxn1 = _mm_nt("proj_in_dx_1", dproj, win, [F32], part=(1, 2))[0]
    dxn = lax.optimization_barrier(((dxn0, dxn1), red_in[0][1]))[0]
    grad_x, _, dg1 = _rms_bwd("norm_mix_bwd", xs, norm_mix_g, dxn, dh1)
    red_up, red_down = lax.optimization_barrier(((red_up, red_down), dg1))[0]
    big_out = {i: _adamw_shard(f"adamw_{i}", [red], big[i], big_m[i], big_v[i], chip) for i, red in ((4, red_up), (5, red_down))}
    big_out.update({i: _adamw_shard(f"adamw_{i}", [red], big[i], big_m[i], big_v[i], chip) for i, red in ((1, red_pa), (2, red_pl), (3, red_out))})
    late_sum, late_upd = small_step("norm_mix", [dg1], [norm_mix_g], [m_norm_mix_g], [v_norm_mix_g], [norm_mix_g], (big_out[4], big_out[5]))
    big_out[0] = _adamw_shard("adamw_0", red_in, big[0], big_m[0], big_v[0], chip)
    s_grad = late_sum + early_sum[:-2]
    s_delta, s_m, s_v = (late_upd[j] + early_upd[j] for j in range(3))


    names = ["norm_mix_g", "w_in", "conv_w", "conv_b", "lru_wa", "lru_ba", "lru_wx", "lru_bx", "lru_lambda", "w_proj_attn", "w_proj_lru",
             "w_out", "norm_mlp_g", "w_up", "w_down", "norm_final_g"]
    small_names = ["norm_mix_g", "conv_b", "lru_wa", "lru_ba", "lru_wx", "lru_bx", "lru_lambda", "norm_mlp_g", "norm_final_g"]
    big_names = ["w_in", "w_proj_attn", "w_proj_lru", "w_out", "w_up", "w_down"]
    res = {"conv_w": (g_cw, cw_delta, cw_m, cw_v)}
    for i, nm in enumerate(small_names):
        res[nm] = (s_grad[i], s_delta[i], s_m[i], s_v[i])
    for i, nm in enumerate(big_names):
        res[nm] = tuple(t[None] for t in big_out[i])
    return (loss, grad_x[None], *[res[nm][0] for nm in names], *[res[nm][1] for nm in names],
            *[res[nm][2] for nm in names], *[res[nm][3] for nm in names])
```

```python
import jax
import jax.numpy as jnp
from jax import lax
from jax.experimental import pallas as pl
from jax.experimental.pallas import tpu as pltpu
from jax.experimental.pallas import tpu_sc as plsc

F32, BF16 = jnp.float32, jnp.bfloat16
MESH = pl.DeviceIdType.MESH
HBM = pl.BlockSpec(memory_space=pltpu.HBM)
N_DEV = 8
N_CHIP = 4
HEAD = 128
SPAN = 128
DILATIONS = (1, 4, 16)
CONV_TAPS = 4
LRU_C = 8.0
NORM_EPS = 1e-6
LANES = 128
SUBLANES = 8
VMEM_LIMIT = 56 * 1024 * 1024
ADAM_LR, ADAM_B1, ADAM_B2, ADAM_EPS, ADAM_WD, ADAM_STEP = 0.001, 0.9, 0.999, 1e-08, 0.01, 10
ADAM_C1 = 1.0 - ADAM_B1 ** ADAM_STEP
ADAM_C2 = 1.0 - ADAM_B2 ** ADAM_STEP
NEG = -1e30


def _params(sem=None):
    return pltpu.CompilerParams(dimension_semantics=sem, vmem_limit_bytes=VMEM_LIMIT)


def _sigmoid(v):
    return 1.0 / (1.0 + jnp.exp(-v))


def _gelu(v):
    k = 0.7978845608028654
    return 0.5 * v * (1.0 + jnp.tanh(k * (v + 0.044715 * v * v * v)))


def _gelu_grad(v):
    k = 0.7978845608028654
    t = jnp.tanh(k * (v + 0.044715 * v * v * v))
    return 0.5 * (1.0 + t) + 0.5 * v * (1.0 - t * t) * k * (1.0 + 3.0 * 0.044715 * v * v)


NN = (((1,), (0,)), ((), ()))
NT = (((1,), (1,)), ((), ()))
TN = (((0,), (0,)), ((), ()))


def _mm(name, a, a_spec, b, b_spec, dn, grid, out_shapes, out_specs, acc_block, epilogue=None, extras=(), extra_specs=(), side=None):
    nk, ne, no = grid[2], len(extras), len(out_shapes)
    side_ops, side_shapes, side_copies, make_copies = side if side is not None else ((), (), 0, None)
    ns_in, ns_out = len(side_ops), len(side_shapes)

    def body(*refs):
        a_ref, b_ref = refs[0], refs[1]
        ex, side_in = refs[2:2 + ne], refs[2 + ne:2 + ne + ns_in]
        outs = refs[2 + ne + ns_in:2 + ne + ns_in + no]
        side_out = refs[2 + ne + ns_in + no:2 + ne + ns_in + no + ns_out]
        scratch = refs[2 + ne + ns_in + no + ns_out:]
        at = [pl.program_id(ax) for ax in range(3)]
        if side is not None:
            @pl.when((at[0] == 0) & (at[1] == 0) & (at[2] == 0))
            def _():
                for cp in make_copies(side_in, side_out, scratch[-2], scratch[-1]):
                    cp.start()

        part = lax.dot_general(a_ref[...], b_ref[...], dn, preferred_element_type=F32)

        def finish(acc):
            vals = epilogue(acc, *[e[...] for e in ex]) if epilogue is not None else (acc,)
            for o, v in zip(outs, vals):
                o[...] = v.astype(o.dtype)

        if nk == 1:
            finish(part)
        else:
            acc_ref, k = scratch[0], at[2]

            @pl.when(k == 0)
            def _():
                acc_ref[...] = part

            @pl.when((k > 0) & (k < nk - 1))
            def _():
                acc_ref[...] += part

            @pl.when(k == nk - 1)
            def _():
                finish(acc_ref[...] + part)

        if side is not None:
            @pl.when((at[0] == grid[0] - 1) & (at[1] == grid[1] - 1) & (at[2] == grid[2] - 1))
            def _():
                for cp in make_copies(side_in, side_out, scratch[-2], scratch[-1]):
                    cp.wait()

    scratch_shapes = [pltpu.VMEM(acc_block, F32)] if nk > 1 else []
    if side is not None:
        scratch_shapes += [pltpu.SemaphoreType.DMA((side_copies,)), pltpu.SemaphoreType.DMA((side_copies,))]
    res = pl.pallas_call(
        body, out_shape=[*out_shapes, *side_shapes], grid=grid, in_specs=[a_spec, b_spec, *extra_specs, *[HBM] * ns_in],
        out_specs=[*out_specs, *[HBM] * ns_out], scratch_shapes=scratch_shapes,
        compiler_params=_params(("arbitrary",) * 3 if side is not None else ("parallel", "parallel", "arbitrary")),
        name=name)(a, b, *extras, *side_ops)
    return res if side is None else (res[:no], res[no:])


def _blk(n, pref):
    return pref if n % pref == 0 else n


def _kblk(k):
    return k if k <= 2048 else next(b for b in (2048, 1024, 512) if k % b == 0)


def _mm_fwd(name, a, w, col0, ncols, out_dtypes, epilogue=None, extras=(), extra_specs_fn=None, bm=1024, bn=1024):
    m, k = a.shape
    bm, bn = _blk(m, bm), _blk(ncols, bn)
    bk = _kblk(k)
    nk = k // bk
    cb0 = col0 // bn
    grid = (m // bm, ncols // bn, nk)
    a_spec = pl.BlockSpec((bm, bk), lambda i, j, kk: (i, kk))
    b_spec = pl.BlockSpec((bk, bn), lambda i, j, kk: (kk, cb0 + j))
    shapes = [jax.ShapeDtypeStruct((m, ncols), dt) for dt in out_dtypes]
    specs = [pl.BlockSpec((bm, bn), lambda i, j, kk: (i, j)) for _ in out_dtypes]
    ex_specs = extra_specs_fn(bm, bn) if extra_specs_fn else ()
    return _mm(name, a, a_spec, w, b_spec, NN, grid, shapes, specs, (bm, bn), epilogue, extras, ex_specs)


def _mm_nt(name, a, w, out_dtypes, epilogue=None, extras=(), extra_specs_fn=None, part=(0, 1), side=None, bm=1024, bn=1024):
    n = w.shape[0]
    if a.ndim == 3:
        seg_cols, m, k = a.shape[2], a.shape[1], a.shape[0] * a.shape[2]
    else:
        m, k = a.shape
    m = m // part[1]
    bm, bn = _blk(m, bm), _blk(n, bn)
    bk = _kblk(k)
    grid = (m // bm, n // bn, k // bk)
    i0 = part[0] * (m // bm)
    if a.ndim == 3:
        per = seg_cols // bk
        a_spec = pl.BlockSpec((None, bm, bk), lambda i, j, kk: (kk // per, i0 + i, kk % per))
    else:
        a_spec = pl.BlockSpec((bm, bk), lambda i, j, kk: (i0 + i, kk))
    b_spec = pl.BlockSpec((bn, bk), lambda i, j, kk: (j, kk))
    shapes = [jax.ShapeDtypeStruct((m, n), dt) for dt in out_dtypes]
    specs = [pl.BlockSpec((bm, bn), lambda i, j, kk: (i, j)) for _ in out_dtypes]
    ex_specs = extra_specs_fn(bm, bn) if extra_specs_fn else ()
    return _mm(name, a, a_spec, w, b_spec, NT, grid, shapes, specs, (bm, bn), epilogue, extras, ex_specs, side)


def _mm_tn(name, a, b, part=(0, 1), side=None, bm=1024, bn=2048):
    t, m = a.shape
    n = b.shape[1] if b.ndim == 2 else b.shape[0] * b.shape[2]
    m = m // part[1]
    bm, bn = _blk(m, bm), _blk(n, bn)
    grid = (m // bm, n // bn, 1)
    i0 = part[0] * (m // bm)
    a_spec = pl.BlockSpec((t, bm), lambda i, j, kk: (0, i0 + i))
    if b.ndim == 3:
        per = b.shape[2] // bn
        b_spec = pl.BlockSpec((None, t, bn), lambda i, j, kk: (j // per, 0, j % per))
    else:
        b_spec = pl.BlockSpec((t, bn), lambda i, j, kk: (0, j))
    res = _mm(name, a, a_spec, b, b_spec, TN, grid, [jax.ShapeDtypeStruct((m, n), BF16)],
              [pl.BlockSpec((bm, bn), lambda i, j, kk: (i, j))], (bm, bn), side=side)
    return res[0] if side is None else (res[0][0], res[1])


ROWS = 256


def _row_spec(d):
    return pl.BlockSpec((ROWS, d), lambda i: (i, 0))


def _vec_spec(d, rows=1):
    return pl.BlockSpec((rows, d), lambda i: (0, 0))


def _rms_fwd(name, x, g):
    s, d = x.shape

    def body(x_ref, g_ref, o_ref):
        xv = x_ref[...]
        r = lax.rsqrt(jnp.mean(xv * xv, axis=-1, keepdims=True) + NORM_EPS)
        o_ref[...] = (xv * r * g_ref[...]).astype(BF16)

    return pl.pallas_call(body, out_shape=jax.ShapeDtypeStruct((s, d), BF16), grid=(s // ROWS,),
                          in_specs=[_row_spec(d), _vec_spec(d)], out_specs=_row_spec(d),
                          compiler_params=_params(("parallel",)), name=name)(x, g)


def _rms_bwd_math(xv, g, dy):
    r = lax.rsqrt(jnp.mean(xv * xv, axis=-1, keepdims=True) + NORM_EPS)
    n = xv * r
    z = dy * g
    dx = r * (z - n * jnp.mean(z * n, axis=-1, keepdims=True))
    return dx, jnp.sum(dy * n, axis=0, keepdims=True)


def _rms_bwd(name, x, g, dy, resid):
    s, d = x.shape
    parts = dy if isinstance(dy, tuple) else (dy,)
    per = s // ROWS // len(parts)
    assert all(p.shape == (per * ROWS, d) for p in parts)

    def body(x_ref, g_ref, *refs):
        dy_refs, (r_ref, dx_ref, dxb_ref, dg_ref) = refs[:len(parts)], refs[len(parts):]
        dyv = dy_refs[0][...]
        for p in range(1, len(parts)):
            dyv = jnp.where(pl.program_id(0) >= p * per, dy_refs[p][...], dyv)
        dx, dg = _rms_bwd_math(x_ref[...], g_ref[...], dyv)
        dx = dx + r_ref[...]
        dx_ref[...] = dx
        dxb_ref[...] = dx.astype(BF16)

        @pl.when(pl.program_id(0) == 0)
        def _():
            dg_ref[...] = jnp.zeros_like(dg_ref)

        dg_ref[...] += dg

    part_specs = [pl.BlockSpec((ROWS, d), lambda i, p=p: (jnp.clip(i - p * per, 0, per - 1), 0)) for p in range(len(parts))]
    return pl.pallas_call(
        body, out_shape=[jax.ShapeDtypeStruct((s, d), F32), jax.ShapeDtypeStruct((s, d), BF16), jax.ShapeDtypeStruct((1, d), F32)],
        grid=(s // ROWS,), in_specs=[_row_spec(d), _vec_spec(d), *part_specs, _row_spec(d)],
        out_specs=[_row_spec(d), _row_spec(d), _vec_spec(d)], compiler_params=_params(("arbitrary",)), name=name)(x, g, *parts, resid)


def _final_loss(h2, tgt, g):
    s, d = h2.shape

    def body(x_ref, t_ref, g_ref, dx_ref, dxb_ref, dg_ref, ls_ref):
        xv, gv = x_ref[...], g_ref[...]
        r = lax.rsqrt(jnp.mean(xv * xv, axis=-1, keepdims=True) + NORM_EPS)
        diff = xv * r * gv - t_ref[...]
        dx, dg = _rms_bwd_math(xv, gv, diff * (1.0 / d))
        dx_ref[...] = dx
        dxb_ref[...] = dx.astype(BF16)

        @pl.when(pl.program_id(0) == 0)
        def _():
            dg_ref[...] = jnp.zeros_like(dg_ref)
            ls_ref[...] = jnp.zeros_like(ls_ref)

        dg_ref[...] += dg
        ls_ref[...] += jnp.sum(diff * diff, axis=0, keepdims=True)

    return pl.pallas_call(
        body, out_shape=[jax.ShapeDtypeStruct((s, d), F32), jax.ShapeDtypeStruct((s, d), BF16),
                         jax.ShapeDtypeStruct((1, d), F32), jax.ShapeDtypeStruct((1, d), F32)],
        grid=(s // ROWS,), in_specs=[_row_spec(d), _row_spec(d), _vec_spec(d)],
        out_specs=[_row_spec(d), _row_spec(d), _vec_spec(d), _vec_spec(d)],
        compiler_params=_params(("arbitrary",)), name="final_norm_loss")(h2, tgt, g)


ATTN_Q = 128


ATTN_BATCH = 8


def _attn_units(s):
    units = []
    for gi, d in enumerate(DILATIONS):
        for r in range(d):
            for q0 in range(0, s // d, ATTN_Q):
                k0 = max(q0 - SPAN, 0)
                units.append((gi, d, r, q0, k0, q0 + ATTN_Q - k0))
    return units


def _stream_rows(d, r, start, size):
    return pl.ds(r + start * d, size) if d == 1 else pl.ds(r + start * d, size, stride=d)


def _attn_bias(bias_ref, slope):
    qi = lax.broadcasted_iota(jnp.int32, (ATTN_Q, ATTN_Q + SPAN), 0)
    kj = lax.broadcasted_iota(jnp.int32, (ATTN_Q, ATTN_Q + SPAN), 1)
    dist = SPAN + qi - kj
    valid = (dist >= 0) & (dist <= SPAN)
    for gi, d in enumerate(DILATIONS):
        bias_ref[gi] = jnp.where(valid, -(slope * d) * dist.astype(F32), NEG)


def _attn_scores(q_ref, k_ref, bias_ref, gi, d, r, q0, k0, nk):
    qrows, krows = _stream_rows(d, r, q0, ATTN_Q), _stream_rows(d, r, k0, nk)
    qb = q_ref[qrows, :].astype(BF16)
    kb = k_ref[krows, :].astype(BF16)
    sc = lax.dot_general(qb, kb, NT, preferred_element_type=F32) * (HEAD ** -0.5) + bias_ref[gi, :, pl.ds(ATTN_Q + SPAN - nk, nk)]
    return sc, qb, kb, qrows, krows


def _attn_fwd(proj, dm, slopes):
    s = proj.shape[0]
    units = _attn_units(s)

    def body(sl_ref, q_ref, k_ref, v_ref, att_ref, lse_ref, *scr):
        o_scr, l_scr, bias = scr[:3], scr[3:6], scr[6]
        _attn_bias(bias, sl_ref[pl.program_id(0)])
        for first in range(0, len(units), ATTN_BATCH):
            batch = units[first:first + ATTN_BATCH]
            scored = [_attn_scores(q_ref, k_ref, bias, gi, d, r, q0, k0, nk) for gi, d, r, q0, k0, nk in batch]
            soft = []
            for sc, _, _, _, _ in scored:
                m = jnp.max(sc, axis=-1, keepdims=True)
                p = jnp.exp(sc - m)
                soft.append((m, p, jnp.sum(p, axis=-1, keepdims=True)))
            outs = [lax.dot_general(p.astype(BF16), v_ref[sco[4], :].astype(BF16), NN, preferred_element_type=F32)
                    for (m, p, l), sco in zip(soft, scored)]
            for (gi, *_), (m, p, l), sco, o in zip(batch, soft, scored, outs):
                o_scr[gi][sco[3], :] = o / l
                l_scr[gi][sco[3], :] = jnp.broadcast_to(m + jnp.log(l), (ATTN_Q, HEAD))
        l0, l1, l2 = l_scr[0][...], l_scr[1][...], l_scr[2][...]
        m = jnp.maximum(jnp.maximum(l0, l1), l2)
        w0, w1, w2 = jnp.exp(l0 - m), jnp.exp(l1 - m), jnp.exp(l2 - m)
        tot = w0 + w1 + w2
        att_ref[...] = ((w0 * o_scr[0][...] + w1 * o_scr[1][...] + w2 * o_scr[2][...]) / tot).astype(BF16)
        lse_ref[...] = m + jnp.log(tot)

    def seg(i):
        return pl.BlockSpec((s, HEAD), lambda h: (0, i * (dm // HEAD) + h))

    col = pl.BlockSpec((s, HEAD), lambda h: (0, h))
    return pl.pallas_call(
        body, out_shape=[jax.ShapeDtypeStruct((s, dm), BF16), jax.ShapeDtypeStruct((s, dm), F32)], grid=(dm // HEAD,),
        in_specs=[pl.BlockSpec(memory_space=pltpu.SMEM), seg(0), seg(1), seg(2)], out_specs=[col, col],
        scratch_shapes=[pltpu.VMEM((s, HEAD), F32)] * (2 * len(DILATIONS)) + [pltpu.VMEM((len(DILATIONS), ATTN_Q, ATTN_Q + SPAN), F32)],
        compiler_params=_params(("parallel",)), name="attn_fwd")(slopes, proj, proj, proj)


def _attn_bwd(proj, dm, datt, att, lse, slopes, others):
    s = proj.shape[0]
    units = _attn_units(s)

    def body(sl_ref, q_ref, k_ref, v_ref, do_ref, att_ref, lse_ref, o3, o4, o5, o6, out_ref, dq_scr, dk_scr, dv_scr, dl_scr, bias):
        _attn_bias(bias, sl_ref[pl.program_id(0)])
        delta = jnp.sum(do_ref[...] * att_ref[...].astype(F32), axis=-1, keepdims=True)
        dl_scr[...] = jnp.broadcast_to(delta, (s, HEAD))
        dq_scr[...] = jnp.zeros_like(dq_scr)
        dk_scr[...] = jnp.zeros_like(dk_scr)
        dv_scr[...] = jnp.zeros_like(dv_scr)
        for first in range(0, len(units), ATTN_BATCH):
            scored = [_attn_scores(q_ref, k_ref, bias, gi, d, r, q0, k0, nk) for gi, d, r, q0, k0, nk in units[first:first + ATTN_BATCH]]
            dobs = [do_ref[sco[3], :].astype(BF16) for sco in scored]
            dps = [lax.dot_general(dob, v_ref[sco[4], :].astype(BF16), NT, preferred_element_type=F32) for dob, sco in zip(dobs, scored)]
            ps = [jnp.exp(sco[0] - lse_ref[sco[3], :][:, 0:1]) for sco in scored]
            dss = [(p * (dp - dl_scr[sco[3], :][:, 0:1]) * (HEAD ** -0.5)).astype(BF16) for p, dp, sco in zip(ps, dps, scored)]
            dqs = [lax.dot_general(ds, sco[2], NN, preferred_element_type=F32) for ds, sco in zip(dss, scored)]
            dks = [lax.dot_general(ds, sco[1], TN, preferred_element_type=F32) for ds, sco in zip(dss, scored)]
            dvs = [lax.dot_general(p.astype(BF16), dob, TN, preferred_element_type=F32) for p, dob in zip(ps, dobs)]
            for sco, dq, dk, dv in zip(scored, dqs, dks, dvs):
                dq_scr[sco[3], :] += dq
                dk_scr[sco[4], :] += dk
                dv_scr[sco[4], :] += dv
        for j, scr in enumerate((dq_scr, dk_scr, dv_scr)):
            out_ref[j] = scr[...].astype(BF16)
        for j, other in enumerate((o3, o4, o5, o6)):
            out_ref[3 + j] = other[...]

    def seg(i):
        return pl.BlockSpec((s, HEAD), lambda h: (0, i * (dm // HEAD) + h))

    col = pl.BlockSpec((s, HEAD), lambda h: (0, h))
    return pl.pallas_call(
        body, out_shape=jax.ShapeDtypeStruct((7, s, dm), BF16), grid=(dm // HEAD,),
        in_specs=[pl.BlockSpec(memory_space=pltpu.SMEM), seg(0), seg(1), seg(2), col, col, col, col, col, col, col],
        out_specs=pl.BlockSpec((7, s, HEAD), lambda h: (0, 0, h)),
        scratch_shapes=[pltpu.VMEM((s, HEAD), F32)] * 4 + [pltpu.VMEM((len(DILATIONS), ATTN_Q, ATTN_Q + SPAN), F32)],
        compiler_params=_params(("parallel",)), name="attn_bwd")(slopes, proj, proj, proj, datt, att, lse, *others)


VEC_CB, VEC_BA, VEC_BX, VEC_LAM = 0, 1, 2, 3
SEG_Q, SEG_K, SEG_V, SEG_X, SEG_GATE, SEG_GA, SEG_GL = range(7)


def _softplus(z):
    return jnp.maximum(z, 0.0) + jnp.log1p(jnp.exp(-jnp.abs(z)))


def _gate_math(xc, wa_ref, wx_ref, vec):
    xcb = xc.astype(BF16)
    nh = xc.shape[1] // HEAD
    pre_a = jnp.concatenate([jnp.dot(xcb[:, h * HEAD:(h + 1) * HEAD], wa_ref[h], preferred_element_type=F32) for h in range(nh)], axis=1)
    pre_x = jnp.concatenate([jnp.dot(xcb[:, h * HEAD:(h + 1) * HEAD], wx_ref[h], preferred_element_type=F32) for h in range(nh)], axis=1)
    ra = _sigmoid(pre_a + vec[VEC_BA:VEC_BA + 1])
    ig = _sigmoid(pre_x + vec[VEC_BX:VEC_BX + 1])
    sp = _softplus(-vec[VEC_LAM:VEC_LAM + 1])
    log_a = -LRU_C * ra * sp
    a = jnp.exp(log_a)
    z = 2.0 * log_a
    one_minus_a2 = jnp.where(z > -0.01, -z * (1.0 + z * (0.5 + z * (1.0 / 6.0))), 1.0 - jnp.exp(z))
    mult = jnp.sqrt(one_minus_a2)
    return dict(xcb=xcb, ra=ra, ig=ig, sp=sp, a=a, mult=mult)


def _conv_pad_prev(pad_ref, cur, halo, first):
    pad_ref[0:SUBLANES, :] = jnp.where(first, 0.0, halo)
    pad_ref[SUBLANES:SUBLANES + cur.shape[0], :] = cur


def _shift_rows(x, s, fill, up=False):
    rid = lax.broadcasted_iota(jnp.int32, x.shape, 0)
    if up:
        return jnp.where(rid < SUBLANES - s, pltpu.roll(x, SUBLANES - s, axis=0), fill)
    return jnp.where(rid >= s, pltpu.roll(x, s, axis=0), fill)


def _lru_fwd(proj, d, cw8, vec8, wa, wx):
    s = proj.shape[0]
    hb = ROWS // SUBLANES

    def body(x_ref, halo_ref, g_ref, cw_ref, vec_ref, wa_ref, wx_ref, xc_ref, hp_ref, h2_ref, y_ref, pad, a_scr, u_scr, carry):
        @pl.when(pl.program_id(0) == 0)
        def _():
            carry[...] = jnp.zeros_like(carry)

        _conv_pad_prev(pad, x_ref[...], halo_ref[...], pl.program_id(0) == 0)
        vec = vec_ref[...]
        xc = vec[VEC_CB:VEC_CB + 1]
        for k in range(CONV_TAPS):
            xc = xc + cw_ref[k:k + 1, :] * pad[pl.ds(SUBLANES - (CONV_TAPS - 1) + k, ROWS), :]
        gm = _gate_math(xc, wa_ref, wx_ref, vec)
        xc_ref[...] = xc
        a_scr[...] = gm["a"]
        u_scr[...] = gm["mult"] * (gm["ig"] * xc)

        def group(gi, before):
            rows = pl.ds(pl.multiple_of(gi * SUBLANES, SUBLANES), SUBLANES)
            ca, cb = a_scr[rows, :], u_scr[rows, :]
            for sh in (1, 2, 4):
                cb = ca * _shift_rows(cb, sh, 0.0) + cb
                ca = ca * _shift_rows(ca, sh, 1.0)
            h = cb + ca * before
            hp_ref[rows, :] = jnp.where(lax.broadcasted_iota(jnp.int32, h.shape, 0) == 0, before, pltpu.roll(h, 1, axis=0))
            h2_ref[rows, :] = h.astype(BF16)
            y_ref[rows, :] = (h * _gelu(g_ref[rows, :])).astype(BF16)
            return jnp.broadcast_to(h[SUBLANES - 1:SUBLANES, :], h.shape)

        carry[...] = lax.fori_loop(0, ROWS // SUBLANES, group, carry[...], unroll=2)

    wspec = pl.BlockSpec(wa.shape, lambda i: (0, 0, 0))
    return pl.pallas_call(
        body, out_shape=[jax.ShapeDtypeStruct((s, d), F32)] * 2 + [jax.ShapeDtypeStruct((s, d), BF16)] * 2, grid=(s // ROWS,),
        in_specs=[pl.BlockSpec((ROWS, d), lambda i: (i, SEG_X)),
                  pl.BlockSpec((SUBLANES, d), lambda i: (jnp.maximum(i * hb - 1, 0), SEG_X)),
                  pl.BlockSpec((ROWS, d), lambda i: (i, SEG_GATE)),
                  _vec_spec(d, SUBLANES), _vec_spec(d, SUBLANES), wspec, wspec],
        out_specs=[_row_spec(d)] * 4,
        scratch_shapes=[pltpu.VMEM((ROWS + SUBLANES, d), F32), pltpu.VMEM((ROWS, d), F32), pltpu.VMEM((ROWS, d), F32), pltpu.VMEM((SUBLANES, d), F32)],
        compiler_params=_params(("arbitrary",)), name="lru_fwd")(proj, proj, proj, cw8, vec8, wa, wx)


def _lru_bwd(hp, dh, xc, proj, cw8, wa, wx, vec8):
    s, d = xc.shape
    nh = d // HEAD
    nb = s // ROWS
    hb = ROWS // SUBLANES

    def body(hp_ref, dh_ref, xc_ref, x_ref, xprev_ref, cw_ref, wa_ref, wx_ref, vec_ref, dx_ref, dwa_ref, dwx_ref, dvec_ref, dcw_ref,
             a_scr, g_scr, da_scr, carry, padd, padx, dnext):
        step = pl.program_id(0)

        @pl.when(step == 0)
        def _():
            carry[...] = jnp.zeros_like(carry)
            dnext[...] = jnp.zeros_like(dnext)
            dwa_ref[...] = jnp.zeros_like(dwa_ref)
            dwx_ref[...] = jnp.zeros_like(dwx_ref)
            dvec_ref[...] = jnp.zeros_like(dvec_ref)
            dcw_ref[...] = jnp.zeros_like(dcw_ref)

        xc_v, vec = xc_ref[...], vec_ref[...]
        gm = _gate_math(xc_v, wa_ref, wx_ref, vec)
        ra, ig, sp, a, mult = gm["ra"], gm["ig"], gm["sp"], gm["a"], gm["mult"]
        a_scr[...] = a

        def group(j, after):
            rows = pl.ds(pl.multiple_of((ROWS // SUBLANES - 1 - j) * SUBLANES, SUBLANES), SUBLANES)
            ca, dhv = a_scr[rows, :], dh_ref[rows, :]
            cb = ca * dhv
            for sh in (1, 2, 4):
                cb = ca * _shift_rows(cb, sh, 0.0, up=True) + cb
                ca = ca * _shift_rows(ca, sh, 1.0, up=True)
            c = cb + ca * after
            last = lax.broadcasted_iota(jnp.int32, c.shape, 0) == SUBLANES - 1
            g = dhv + jnp.where(last, after, pltpu.roll(c, SUBLANES - 1, axis=0))
            g_scr[rows, :] = g
            da_scr[rows, :] = g * hp_ref[rows, :]
            return jnp.broadcast_to(c[0:1, :], c.shape)

        carry[...] = lax.fori_loop(0, ROWS // SUBLANES, group, carry[...], unroll=2)
        du, da = g_scr[...], da_scr[...]
        dmult = du * ig * xc_v
        dlog_a = da * a - dmult * (a * a) / mult
        dpre_a = dlog_a * (-LRU_C * sp) * ra * (1.0 - ra)
        dpre_x = du * mult * xc_v * ig * (1.0 - ig)
        dlam = jnp.sum(dlog_a * (-LRU_C * ra), axis=0, keepdims=True) * (-_sigmoid(-vec[VEC_LAM:VEC_LAM + 1]))
        dvec_ref[VEC_BA:VEC_BA + 1, :] += jnp.sum(dpre_a, axis=0, keepdims=True)
        dvec_ref[VEC_BX:VEC_BX + 1, :] += jnp.sum(dpre_x, axis=0, keepdims=True)
        dvec_ref[VEC_LAM:VEC_LAM + 1, :] += dlam
        dab, dxb, xcb = dpre_a.astype(BF16), dpre_x.astype(BF16), gm["xcb"]
        back = []
        for h in range(nh):
            cols = slice(h * HEAD, (h + 1) * HEAD)
            dwa_ref[h] += lax.dot_general(xcb[:, cols], dab[:, cols], TN, preferred_element_type=F32)
            dwx_ref[h] += lax.dot_general(xcb[:, cols], dxb[:, cols], TN, preferred_element_type=F32)
            back.append(lax.dot_general(dab[:, cols], wa_ref[h], NT, preferred_element_type=F32)
                        + lax.dot_general(dxb[:, cols], wx_ref[h], NT, preferred_element_type=F32))
        dc = du * mult * ig + jnp.concatenate(back, axis=1)
        padd[0:ROWS, :] = dc
        padd[ROWS:ROWS + SUBLANES, :] = dnext[...]
        dnext[...] = dc[0:SUBLANES, :]
        _conv_pad_prev(padx, x_ref[...], xprev_ref[...], step == nb - 1)
        dx = jnp.zeros_like(dc)
        for k in range(CONV_TAPS):
            dx = dx + cw_ref[k:k + 1, :] * padd[pl.ds(CONV_TAPS - 1 - k, ROWS), :]
            dcw_ref[k:k + 1, :] += jnp.sum(dc * padx[pl.ds(SUBLANES - (CONV_TAPS - 1) + k, ROWS), :], axis=0, keepdims=True)
        dcw_ref[CONV_TAPS:CONV_TAPS + 1, :] += jnp.sum(dc, axis=0, keepdims=True)
        dx_ref[...] = dx.astype(BF16)

    rows_rev = pl.BlockSpec((ROWS, d), lambda i: (nb - 1 - i, 0))
    wspec = pl.BlockSpec(wa.shape, lambda i: (0, 0, 0))
    return pl.pallas_call(
        body, out_shape=[jax.ShapeDtypeStruct((s, d), BF16), jax.ShapeDtypeStruct(wa.shape, F32), jax.ShapeDtypeStruct(wa.shape, F32),
                         jax.ShapeDtypeStruct((SUBLANES, d), F32), jax.ShapeDtypeStruct((SUBLANES, d), F32)],
        grid=(nb,),
        in_specs=[rows_rev, rows_rev, rows_rev, pl.BlockSpec((ROWS, d), lambda i: (nb - 1 - i, SEG_X)),
                  pl.BlockSpec((SUBLANES, d), lambda i: (jnp.maximum((nb - 1 - i) * hb - 1, 0), SEG_X)),
                  _vec_spec(d, SUBLANES), wspec, wspec, _vec_spec(d, SUBLANES)],
        out_specs=[rows_rev, wspec, wspec, _vec_spec(d, SUBLANES), _vec_spec(d, SUBLANES)],
        scratch_shapes=[pltpu.VMEM((ROWS, d), F32)] * 3 + [pltpu.VMEM((SUBLANES, d), F32)]
        + [pltpu.VMEM((ROWS + SUBLANES, d), F32)] * 2 + [pltpu.VMEM((SUBLANES, d), F32)],
        compiler_params=_params(("arbitrary",)), name="lru_bwd")(hp, dh, xc, proj, proj, cw8, wa, wx, vec8)


def _coords():
    return lax.axis_index("x"), lax.axis_index("y"), lax.axis_index("c")


def _other_chips(x, y):
    return [(1 - x, y), (x, 1 - y), (1 - x, 1 - y)]


def _slab(ref, kind, shard_shape, idx, half=None):
    r, c = shard_shape
    r0, nr = (0, r) if half is None else (half * (r // 2), r // 2)
    if kind == "col":
        return ref.at[pl.ds(r0, nr), pl.ds(pl.multiple_of(idx * c, LANES), c)]
    if kind == "row":
        return ref.at[pl.ds(pl.multiple_of(idx * r, SUBLANES) + r0, nr), :]
    return ref.at[idx, pl.ds(r0, nr), :]


def _full_shape(shard_shape, kind):
    r, c = shard_shape
    return {"col": (r, c * N_DEV), "row": (r * N_DEV, c), "slot": (N_DEV, r, c)}[kind]


def _handshake(peers):
    barrier = pltpu.get_barrier_semaphore()
    for peer in peers:
        pl.semaphore_signal(barrier, inc=1, device_id=peer, device_id_type=MESH)
    pl.semaphore_wait(barrier, len(peers))


def _launch(name, body, out_shape, operands, sems, sequencer_id):
    if sequencer_id is None:
        return pl.pallas_call(body, out_shape=out_shape, in_specs=[HBM] * len(operands), out_specs=[HBM] * len(out_shape),
                              scratch_shapes=sems, name=name)(*operands)
    return pl.kernel(body, out_type=out_shape, mesh=plsc.ScalarSubcoreMesh(axis_name="seq", num_cores=1), name=name,
                     scratch_types=sems, compiler_params=pltpu.CompilerParams(collective_id=sequencer_id))(*operands)


AG_COPIES = 10


def _all_gather(name, shards, kinds, sequencer_id=None):
    n = len(shards)
    shapes = [s.shape for s in shards]

    def body(*refs):
        ins, outs = refs[:n], refs[n:2 * n]
        send_sems, recv_sems, local_sems = refs[2 * n:]
        x, y, c = _coords()
        me, sib, xn, yn, dg = (x, y, c), (x, y, 1 - c), (1 - x, y, c), (x, 1 - y, c), (1 - x, 1 - y, c)
        if sequencer_id is not None:
            _handshake([sib, xn, yn])

        def part(i, dev, half=None):
            return _slab(outs[i], kinds[i], shapes[i], 4 * dev[0] + 2 * dev[1] + dev[2], half)

        def copy(i, k, block, half, to, own=False):
            r = shapes[i][0]
            src = part(i, block, half) if not own else (ins[i] if half is None else ins[i].at[pl.ds(half * (r // 2), r // 2), :])
            return pltpu.make_async_remote_copy(
                src_ref=src, dst_ref=part(i, block, half), send_sem=send_sems.at[AG_COPIES * i + k],
                recv_sem=recv_sems.at[AG_COPIES * i + k], device_id=to, device_id_type=MESH)

        def other_core(dev):
            return (dev[0], dev[1], 1 - c)

        started = []

        def start(cp):
            cp.start()
            started.append(cp)

        for i in range(n):
            start(copy(i, 1, me, 0, xn, own=True))
            start(copy(i, 4, me, 1, yn, own=True))
            start(copy(i, 2, me, 1, xn, own=True))
            start(copy(i, 3, me, 0, yn, own=True))
            start(copy(i, 0, me, None, sib, own=True))
        mine = [pltpu.make_async_copy(ins[i], part(i, me), local_sems.at[i]) for i in range(n)]
        for cp in mine:
            cp.start()
        for i in range(n):
            copy(i, 1, xn, 0, me).wait_recv()
            start(copy(i, 5, xn, 0, yn))
            copy(i, 4, yn, 1, me).wait_recv()
            start(copy(i, 6, yn, 1, xn))
        for i in range(n):
            copy(i, 2, xn, 1, me).wait_recv()
            start(copy(i, 7, xn, None, sib))
            copy(i, 3, yn, 0, me).wait_recv()
            start(copy(i, 8, yn, None, sib))
        for i in range(n):
            copy(i, 5, dg, 0, me).wait_recv()
            copy(i, 6, dg, 1, me).wait_recv()
            start(copy(i, 9, dg, None, sib))
        for i in range(n):
            copy(i, 0, sib, None, me).wait_recv()
            for k, dev in ((7, xn), (8, yn), (9, dg)):
                copy(i, k, other_core(dev), None, me).wait_recv()
        for cp in started:
            cp.wait_send()
        for cp in mine:
            cp.wait()

    out_shape = [jax.ShapeDtypeStruct(_full_shape(s.shape, k), s.dtype) for s, k in zip(shards, kinds)]
    sems = [pltpu.SemaphoreType.DMA((AG_COPIES * n,)), pltpu.SemaphoreType.DMA((AG_COPIES * n,)), pltpu.SemaphoreType.DMA((n,))]
    return _launch(name, body, out_shape, shards, sems, sequencer_id)


def _sibling_copies(kinds, shard_shapes):
    def make(ins, outs, send_sems, recv_sems):
        x, y, c = _coords()
        return [pltpu.make_async_remote_copy(
            src_ref=_slab(ins[i], kinds[i], shard_shapes[i], 2 * q + (1 - c)), dst_ref=outs[i].at[q],
            send_sem=send_sems.at[N_CHIP * i + q], recv_sem=recv_sems.at[N_CHIP * i + q],
            device_id=(x, y, 1 - c), device_id_type=MESH) for i in range(len(ins)) for q in range(N_CHIP)]
    return make


def _sibling_side(partials, kinds, shard_shapes):
    return (partials, [jax.ShapeDtypeStruct((N_CHIP, *s), BF16) for s in shard_shapes], N_CHIP * len(partials),
            _sibling_copies(kinds, shard_shapes))


def _exchange_chips(name, chip_sums, sequencer_id=None):
    n = len(chip_sums)

    def body(*refs):
        ins, outs = refs[:n], refs[n:2 * n]
        send_sems, recv_sems = refs[2 * n:]
        x, y, c = _coords()
        if sequencer_id is not None:
            _handshake([(cx, cy, c) for cx, cy in _other_chips(x, y)])
        cps = []
        for i in range(n):
            for k, (cx, cy) in enumerate(_other_chips(x, y)):
                cps.append(pltpu.make_async_remote_copy(
                    src_ref=ins[i].at[2 * cx + cy], dst_ref=outs[i].at[k], send_sem=send_sems.at[3 * i + k],
                    recv_sem=recv_sems.at[3 * i + k], device_id=(cx, cy, c), device_id_type=MESH))
        for cp in cps:
            cp.start()
        for cp in cps:
            cp.wait()

    return _launch(name, body, [jax.ShapeDtypeStruct((3, *t.shape[1:]), BF16) for t in chip_sums], chip_sums,
                   [pltpu.SemaphoreType.DMA((3 * n,)), pltpu.SemaphoreType.DMA((3 * n,))], sequencer_id)


def _all_peers(x, y, c):
    return [(x ^ (k >> 2), y ^ ((k >> 1) & 1), c ^ (k & 1)) for k in range(1, N_DEV)]


def _small_scatter(name, packed, sequencer_id):
    rows = packed.shape[0] // N_DEV

    def body(p_ref, rb_ref, send_sems, recv_sems, local_sem):
        x, y, c = _coords()
        me = 4 * x + 2 * y + c
        peers = _all_peers(x, y, c)
        _handshake(peers)

        def piece(idx):
            return p_ref.at[pl.ds(pl.multiple_of(idx * rows, SUBLANES), rows), :]

        cps = [pltpu.make_async_remote_copy(src_ref=piece(me ^ k), dst_ref=rb_ref.at[k], send_sem=send_sems.at[k], recv_sem=recv_sems.at[k],
                                            device_id=peers[k - 1], device_id_type=MESH) for k in range(1, N_DEV)]
        for cp in cps:
            cp.start()
        mine = pltpu.make_async_copy(piece(me), rb_ref.at[0], local_sem)
        mine.start()
        for cp in cps:
            cp.wait()
        mine.wait()

    return _launch(name, body, [jax.ShapeDtypeStruct((N_DEV, rows, LANES), F32)], [packed],
                   [pltpu.SemaphoreType.DMA((N_DEV,)), pltpu.SemaphoreType.DMA((N_DEV,)), pltpu.SemaphoreType.DMA], sequencer_id)[0]


def _small_sum(name, pieces):
    def body(p_ref, o_ref):
        acc = p_ref[0]
        for k in range(1, N_DEV):
            acc = acc + p_ref[k]
        o_ref[...] = acc

    vm = pl.BlockSpec(memory_space=pltpu.VMEM)
    return pl.pallas_call(body, out_shape=jax.ShapeDtypeStruct(pieces.shape[1:], F32), in_specs=[vm], out_specs=vm, name=name)(pieces)


def _small_gather(name, tot, sequencer_id):
    rows = tot.shape[0]

    def body(t_ref, out_ref, send_sems, recv_sems, local_sem):
        x, y, c = _coords()
        me = 4 * x + 2 * y + c
        peers = _all_peers(x, y, c)
        _handshake(peers)

        def piece(idx):
            return out_ref.at[pl.ds(pl.multiple_of(idx * rows, SUBLANES), rows), :]

        cps = [pltpu.make_async_remote_copy(src_ref=t_ref, dst_ref=piece(me), send_sem=send_sems.at[k], recv_sem=recv_sems.at[k],
                                            device_id=peers[k - 1], device_id_type=MESH) for k in range(1, N_DEV)]
        for cp in cps:
            cp.start()
        mine = pltpu.make_async_copy(t_ref, piece(me), local_sem)
        mine.start()
        for cp in cps:
            cp.wait()
        mine.wait()

    return _launch(name, body, [jax.ShapeDtypeStruct((N_DEV * rows, LANES), F32)], [tot],
                   [pltpu.SemaphoreType.DMA((N_DEV,)), pltpu.SemaphoreType.DMA((N_DEV,)), pltpu.SemaphoreType.DMA], sequencer_id)[0]


def _all_reduce_small(name, packed):
    rows = packed.shape[0] // N_DEV

    def body(p_ref, out_ref, rb, tot, send_sems, recv_sems):
        x, y, c = _coords()
        me = 4 * x + 2 * y + c

        def peer(k):
            return (x ^ (k >> 2), y ^ ((k >> 1) & 1), c ^ (k & 1))

        def rows_of(idx):
            return pl.ds(pl.multiple_of(idx * rows, SUBLANES), rows)

        def piece(ref, idx):
            return ref.at[rows_of(idx), :]

        scatter = [pltpu.make_async_remote_copy(src_ref=piece(p_ref, me ^ k), dst_ref=rb.at[k], send_sem=send_sems.at[k],
                                                recv_sem=recv_sems.at[k], device_id=peer(k), device_id_type=MESH) for k in range(1, N_DEV)]
        for cp in scatter:
            cp.start()
        acc = p_ref[rows_of(me), :]
        for cp in scatter:
            cp.wait_recv()
        for k in range(1, N_DEV):
            acc = acc + rb[k]
        tot[...] = acc
        out_ref[rows_of(me), :] = acc
        gather = [pltpu.make_async_remote_copy(src_ref=tot, dst_ref=piece(out_ref, me), send_sem=send_sems.at[N_DEV + k],
                                               recv_sem=recv_sems.at[N_DEV + k], device_id=peer(k), device_id_type=MESH)
                  for k in range(1, N_DEV)]
        for cp in gather:
            cp.start()
        for k in range(1, N_DEV):
            pltpu.make_async_remote_copy(src_ref=tot, dst_ref=piece(out_ref, me ^ k), send_sem=send_sems.at[N_DEV + k],
                                         recv_sem=recv_sems.at[N_DEV + k], device_id=peer(k), device_id_type=MESH).wait_recv()
        for cp in scatter + gather:
            cp.wait_send()

    vm = pl.BlockSpec(memory_space=pltpu.VMEM)
    return pl.pallas_call(
        body, out_shape=jax.ShapeDtypeStruct(packed.shape, F32), in_specs=[vm], out_specs=vm,
        scratch_shapes=[pltpu.VMEM((N_DEV, rows, LANES), F32), pltpu.VMEM((rows, LANES), F32),
                        pltpu.SemaphoreType.DMA((2 * N_DEV,)), pltpu.SemaphoreType.DMA((2 * N_DEV,))],
        compiler_params=pltpu.CompilerParams(vmem_limit_bytes=VMEM_LIMIT), name=name)(packed)


def _adamw_math(g, w, m, v):
    m = ADAM_B1 * m + (1.0 - ADAM_B1) * g
    v = ADAM_B2 * v + (1.0 - ADAM_B2) * (g * g)
    delta = -ADAM_LR * ((m / ADAM_C1) / (jnp.sqrt(v / ADAM_C2) + ADAM_EPS) + ADAM_WD * w)
    return delta, m, v


def _slab_spec(kind, shard_shape, tr, slab_of):
    r, c = shard_shape
    if kind == "col":
        return pl.BlockSpec((tr, c), lambda q, i, sc: (i, slab_of(q, sc)))
    return pl.BlockSpec((tr, c), lambda q, i, sc: (slab_of(q, sc) * (r // tr) + i, 0))


def _chip_sum(name, partial, recv, kind, shard_shape, core):
    r, c = shard_shape
    tr = _blk(r, 1024)

    def body(core_ref, p_ref, r_ref, o_ref):
        o_ref[...] = (p_ref[...].astype(F32) + r_ref[...].astype(F32)).astype(BF16)

    spec4 = pl.BlockSpec((None, tr, c), lambda q, i, sc: (q, i, 0))
    grid_spec = pltpu.PrefetchScalarGridSpec(
        num_scalar_prefetch=1, grid=(N_CHIP, r // tr),
        in_specs=[_slab_spec(kind, shard_shape, tr, lambda q, sc: 2 * q + sc[0]), spec4], out_specs=spec4)
    return pl.pallas_call(body, out_shape=jax.ShapeDtypeStruct((N_CHIP, r, c), BF16), grid_spec=grid_spec,
                          compiler_params=_params(("parallel", "parallel")), name=name)(core, partial, recv)


def _adamw_shard(name, parts, w, m, v, chip):
    r, c = w.shape
    n_parts = len(parts)
    tr = _blk(r // n_parts, 256)
    per = r // n_parts // tr

    def body(chip_ref, *refs):
        src, (w_ref, m_ref, v_ref), (g_out, d_out, m_out, v_out) = refs[:2 * n_parts], refs[2 * n_parts:2 * n_parts + 3], refs[2 * n_parts + 3:]
        for p in range(n_parts):
            @pl.when(pl.program_id(0) // per == p)
            def _():
                g = src[2 * p][...].astype(F32)
                for k in range(3):
                    g = g + src[2 * p + 1][k].astype(F32)
                g_out[...] = g
                d_out[...], m_out[...], v_out[...] = _adamw_math(g, w_ref[...], m_ref[...], v_ref[...])

    def part_specs(p):
        at = lambda i: jnp.clip(i - p * per, 0, per - 1)
        return [pl.BlockSpec((None, tr, c), lambda i, sc: (sc[0], at(i), 0)), pl.BlockSpec((3, tr, c), lambda i, sc: (0, at(i), 0))]

    blk = pl.BlockSpec((tr, c), lambda i, sc: (i, 0))
    grid_spec = pltpu.PrefetchScalarGridSpec(
        num_scalar_prefetch=1, grid=(r // tr,), in_specs=[s for p in range(n_parts) for s in part_specs(p)] + [blk, blk, blk], out_specs=[blk] * 4)
    return pl.pallas_call(body, out_shape=[jax.ShapeDtypeStruct((r, c), F32)] * 4, grid_spec=grid_spec,
                          compiler_params=_params(("parallel",)), name=name)(chip, *[a for p in parts for a in p], w, m, v)


def _adamw_small(name, g, w, m, v):
    def body(g_ref, w_ref, m_ref, v_ref, d_out, m_out, v_out):
        d_out[...], m_out[...], v_out[...] = _adamw_math(g_ref[...], w_ref[...], m_ref[...], v_ref[...])

    vm = pl.BlockSpec(memory_space=pltpu.VMEM)
    return pl.pallas_call(body, out_shape=[jax.ShapeDtypeStruct(g.shape, F32)] * 3, in_specs=[vm] * 4, out_specs=[vm] * 3,
                          compiler_params=pltpu.CompilerParams(vmem_limit_bytes=VMEM_LIMIT, allow_input_fusion=[True] * 4), name=name)(g, w, m, v)


def _pack_rows(arrays, total_rows):
    flat = [a.reshape(-1, LANES) for a in arrays]
    used = sum(f.shape[0] for f in flat)
    return jnp.concatenate(flat + [jnp.zeros((total_rows - used, LANES), F32)], axis=0)


def _unpack_rows(packed, like):
    out, at = [], 0
    for a in like:
        n = a.size // LANES
        out.append(packed[at:at + n].reshape(a.shape))
        at += n
    return out


def kernel(x, norm_mix_g, w_in, conv_w, conv_b, lru_wa, lru_ba, lru_wx, lru_bx, lru_lambda, w_proj_attn, w_proj_lru, w_out, norm_mlp_g, w_up, w_down, norm_final_g, loss_target, m_norm_mix_g, m_w_in, m_conv_w, m_conv_b, m_lru_wa, m_lru_ba, m_lru_wx, m_lru_bx, m_lru_lambda, m_w_proj_attn, m_w_proj_lru, m_w_out, m_norm_mlp_g, m_w_up, m_w_down, m_norm_final_g, v_norm_mix_g, v_w_in, v_conv_w, v_conv_b, v_lru_wa, v_lru_ba, v_lru_wx, v_lru_bx, v_lru_lambda, v_w_proj_attn, v_w_proj_lru, v_w_out, v_norm_mlp_g, v_w_up, v_w_down, v_norm_final_g):
    xs, tgt = x[0], loss_target[0]
    s, d = xs.shape
    nh = d // HEAD
    ix, iy, ic = _coords()
    core = jnp.reshape(ic, (1,)).astype(jnp.int32)
    chip = jnp.reshape(2 * ix + iy, (1,)).astype(jnp.int32)
    dev = 4 * ix + 2 * iy + ic

    big = [w_in[0], w_proj_attn[0], w_proj_lru[0], w_out[0], w_up[0], w_down[0]]
    big_m = [m_w_in[0], m_w_proj_attn[0], m_w_proj_lru[0], m_w_out[0], m_w_up[0], m_w_down[0]]
    big_v = [v_w_in[0], v_w_proj_attn[0], v_w_proj_lru[0], v_w_out[0], v_w_up[0], v_w_down[0]]
    kinds = ["col", "row", "row", "row", "col", "row"]
    pad_taps = lambda t: jnp.pad(t, ((0, SUBLANES - CONV_TAPS), (0, 0)))
    shards = [w.astype(BF16) for w in big]
    pad_taps2 = lambda t: jnp.pad(t, ((0, 2 * SUBLANES - CONV_TAPS), (0, 0)))
    win, cw_slots = _all_gather("all_gather_w_in", [shards[0], pad_taps2(conv_w[0])], ["col", "slot"], sequencer_id=7)
    wpa, wpl, wout = _all_gather("all_gather_mix", shards[1:4], kinds[1:4], sequencer_id=1)
    wup, wdown = _all_gather("all_gather_mlp", shards[4:], kinds[4:], sequencer_id=5)
    cw8 = jnp.transpose(cw_slots[:, :SUBLANES], (1, 0, 2)).reshape(SUBLANES, d)
    row_id = lax.broadcasted_iota(jnp.int32, (SUBLANES, d), 0)
    vec8 = sum(jnp.where(row_id == k, t, 0.0) for k, t in ((VEC_CB, conv_b), (VEC_BA, lru_ba), (VEC_BX, lru_bx), (VEC_LAM, lru_lambda)))
    wa16, wx16 = lru_wa[0].astype(BF16), lru_wx[0].astype(BF16)
    slopes = 2.0 ** (-8.0 * jnp.arange(1, nh + 1, dtype=F32) / nh)

    def seg_specs(*segs):
        return lambda bm, bn: [pl.BlockSpec((bm, bn), (lambda i, j, kk, sg=sg: (i, sg * (d // bn) + j))) for sg in segs]

    def plain_specs(k):
        return lambda bm, bn: [pl.BlockSpec((bm, bn), lambda i, j, kk: (i, j)) for _ in range(k)]

    xn = _rms_fwd("norm_mix", xs, norm_mix_g)
    proj = _mm_fwd("proj_in", xn, win, 0, 7 * d, [F32], bm=2048)[0]
    att, lse = _attn_fwd(proj, d, slopes)
    xc, hp, h2d, ylru = _lru_fwd(proj, d, cw8, vec8, wa16, wx16)
    pa = _mm_fwd("proj_attn", att, wpa, 0, d, [BF16], bm=2048)[0]

    def merge(acc, pa_b, ga, gl):
        return acc, _sigmoid(ga) * pa_b.astype(F32) + _sigmoid(gl) * acc

    plr, merged = _mm_fwd("proj_lru_merge", ylru, wpl, 0, d, [BF16, BF16], merge, (pa, proj, proj),
                          lambda bm, bn: plain_specs(1)(bm, bn) + seg_specs(SEG_GA, SEG_GL)(bm, bn), bn=512)
    h1 = _mm_fwd("mix_out", merged, wout, 0, d, [F32], lambda acc, r: (acc + r,), (xs,), plain_specs(1))[0]
    hn = _rms_fwd("norm_mlp", h1, norm_mlp_g)

    def relu2(acc):
        return acc, jnp.square(jnp.maximum(acc, 0.0))

    up, hid = _mm_fwd("mlp_up", hn, wup, 0, wup.shape[1], [BF16, BF16], relu2, bm=2048)
    h2 = _mm_fwd("mlp_down", hid, wdown, 0, d, [F32], lambda acc, r: (acc + r,), (h1,), plain_specs(1))[0]
    dh2, dh2b, dg3, loss_lanes = _final_loss(h2, tgt, norm_final_g.reshape(1, d))
    loss_rows = jnp.pad((0.5 / d * jnp.sum(loss_lanes)).reshape(1, 1), ((0, SUBLANES - 1), (0, LANES - 1)))

    def reduce_group(tag, kk, shp, partials, from_sibling, sequencer_id):
        sums = [_chip_sum(f"chip_sum_{tag}_{i}", p, f, k, sh, core) for i, (p, f, k, sh) in enumerate(zip(partials, from_sibling, kk, shp))]
        return list(zip(sums, _exchange_chips(f"rs_chips_{tag}", sums, sequencer_id)))

    g_wdown = _mm_tn("mlp_down_dw", hid, dh2b)
    (dup,), sib_down = _mm_nt("mlp_down_dx", dh2b, wdown, [BF16], lambda acc, u: (acc * (2.0 * jnp.maximum(u.astype(F32), 0.0)),), (up,),
                              plain_specs(1), side=_sibling_side([g_wdown], kinds[5:], [big[5].shape]))
    (red_down,) = reduce_group("mlp_down", kinds[5:], [big[5].shape], [g_wdown], sib_down, 2)
    dup = lax.optimization_barrier((dup, red_down[0]))[0]
    g_wup = _mm_tn("mlp_up_dw", hn, dup)
    (dhn,), sib_up = _mm_nt("mlp_up_dx", dup, wup, [F32], side=_sibling_side([g_wup], kinds[4:5], [big[4].shape]))
    (red_up,) = reduce_group("mlp_up", kinds[4:5], [big[4].shape], [g_wup], sib_up, 10)
    dhn = lax.optimization_barrier((dhn, red_up[0]))[0]
    dh1, dh1b, dg2 = _rms_bwd("norm_mlp_bwd", h1, norm_mlp_g, dhn, dh2)

    def merge_bwd(acc, pa_b, pl_b, ga, gl):
        sa, sl = _sigmoid(ga), _sigmoid(gl)
        return acc * sa, acc * sl, acc * pa_b.astype(F32) * sa * (1.0 - sa), acc * pl_b.astype(F32) * sl * (1.0 - sl)

    dpa, dpl, dga, dgl = _mm_nt("mix_out_dx", dh1b, wout, [BF16] * 4, merge_bwd, (pa, plr, proj, proj),
                                lambda bm, bn: plain_specs(2)(bm, bn) + seg_specs(SEG_GA, SEG_GL)(bm, bn), bn=512)
    g_wout = _mm_tn("mix_out_dw", merged, dh1b)
    datt = _mm_nt("proj_attn_dx", dpa, wpa, [F32], bm=2048)[0]
    g_wpa = _mm_tn("proj_attn_dw", att, dpa)

    def lru_out_bwd(acc, h_b, gate):
        return acc * _gelu(gate), acc * h_b.astype(F32) * _gelu_grad(gate)

    g_wpl = _mm_tn("proj_lru_dw", ylru, dpl)
    shp_mix = [w.shape for w in big[1:4]]
    (dh, dxg), sib_mix = _mm_nt("proj_lru_dx", dpl, wpl, [F32, BF16], lru_out_bwd, (h2d, proj),
                                lambda bm, bn: plain_specs(1)(bm, bn) + seg_specs(SEG_GATE)(bm, bn), bn=512,
                                side=_sibling_side([g_wpa, g_wpl, g_wout], kinds[1:4], shp_mix))
    red_pa, red_pl, red_out = reduce_group("mix", kinds[1:4], shp_mix, [g_wpa, g_wpl, g_wout], sib_mix, 3)
    dh = lax.optimization_barrier((dh, red_down[1]))[0]
    dxr, dwa, dwx, dvec, dconv = _lru_bwd(hp, dh, xc, proj, cw8, wa16, wx16, vec8)
    dproj = _attn_bwd(proj, d, datt, att, lse, slopes, (dxr, dxg, dga, dgl))

    def small_step(tag, grads, ws, ms, vs, like, after, seq=None):
        n_rows = sum(g.size for g in grads) // LANES
        per_dev = -(-n_rows // (N_DEV * SUBLANES)) * SUBLANES
        packed = lax.optimization_barrier((_pack_rows(grads, N_DEV * per_dev), after))[0]
        if seq is None:
            total = _all_reduce_small(f"all_reduce_{tag}", packed)
        else:
            pieces = lax.optimization_barrier((_small_scatter(f"scatter_{tag}", packed, seq[0]), seq[2]))[0]
            total = _small_gather(f"gather_{tag}", _small_sum(f"sum_{tag}", pieces), seq[1])
        w_rows = -(-(sum(w.size for w in ws) // LANES) // SUBLANES) * SUBLANES
        upd = _adamw_small(f"adamw_{tag}", total[:w_rows], _pack_rows(ws, w_rows), _pack_rows(ms, w_rows), _pack_rows(vs, w_rows))
        return _unpack_rows(total, like), [_unpack_rows(t, ws) for t in upd]

    early_w = [conv_b, lru_wa, lru_ba, lru_wx, lru_bx, lru_lambda, norm_mlp_g, norm_final_g]
    early_m = [m_conv_b, m_lru_wa, m_lru_ba, m_lru_wx, m_lru_bx, m_lru_lambda, m_norm_mlp_g, m_norm_final_g]
    early_v = [v_conv_b, v_lru_wa, v_lru_ba, v_lru_wx, v_lru_bx, v_lru_lambda, v_norm_mlp_g, v_norm_final_g]
    early_g = [dconv[CONV_TAPS:CONV_TAPS + 1], dwa, dvec[VEC_BA:VEC_BA + 1], dwx, dvec[VEC_BX:VEC_BX + 1],
               dvec[VEC_LAM:VEC_LAM + 1], dg2, dg3, dconv[0:CONV_TAPS], loss_rows]
    dproj = lax.optimization_barrier((dproj, red_up[1]))[0]
    dproj = lax.optimization_barrier((dproj, red_pa[1], red_pl[1], red_out[1]))[0]
    early_sum, early_upd = small_step("small", early_g, early_w, early_m, early_v,
                                      early_w + [jax.ShapeDtypeStruct((1, CONV_TAPS, d), F32), jax.ShapeDtypeStruct((SUBLANES, LANES), F32)],
                                      dxr, seq=(8, 9, dproj))
    g_cw_full, loss = early_sum[-2], early_sum[-1][0, 0]
    cshard = conv_w.shape[2]
    g_cw = lax.dynamic_slice(g_cw_full, (0, 0, dev * cshard), (1, CONV_TAPS, cshard))
    cw_delta, cw_m, cw_v = (t[:CONV_TAPS][None] for t in _adamw_small(
        "adamw_conv_w", pad_taps(g_cw[0]), pad_taps(conv_w[0]), pad_taps(m_conv_w[0]), pad_taps(v_conv_w[0])))
    half = (big[0].shape[0] // 2, big[0].shape[1])
    g_in0 = _mm_tn("proj_in_dw_0", xn, dproj, part=(0, 2))
    g_in1, sib_in0 = _mm_tn("proj_in_dw_1", xn, dproj, part=(1, 2), side=_sibling_side([g_in0], ["col"], [half]))
    red_in = reduce_group("in_0", ["col"], [half], [g_in0], sib_in0, 4)
    dproj = lax.optimization_barrier((dproj, red_in[0][0], early_sum))[0]
    (dxn0,), sib_in1 = _mm_nt("proj_in_dx_0", dproj, win, [F32], part=(0, 2), side=_sibling_side([g_in1], ["col"], [half]))
    red_in += reduce_group("in_1", ["col"], [half], [g_in1], sib_in1, 6)
    dproj = lax.optimization_barrier((dproj, red_in[1][0]))[0]
    dxn1 = _mm_nt("proj_in_dx_1", dproj, win, [F32], part=(1, 2))[0]
    dxn = lax.optimization_barrier(((dxn0, dxn1), red_in[0][1]))[0]
    grad_x, _, dg1 = _rms_bwd("norm_mix_bwd", xs, norm_mix_g, dxn, dh1)
    red_up, red_down = lax.optimization_barrier(((red_up, red_down), dg1))[0]
    big_out = {i: _adamw_shard(f"adamw_{i}", [red], big[i], big_m[i], big_v[i], chip) for i, red in ((4, red_up), (5, red_down))}
    big_out.update({i: _adamw_shard(f"adamw_{i}", [red], big[i], big_m[i], big_v[i], chip) for i, red in ((1, red_pa), (2, red_pl), (3, red_out))})
    late_sum, late_upd = small_step("norm_mix", [dg1], [norm_mix_g], [m_norm_mix_g], [v_norm_mix_g], [norm_mix_g], (big_out[4], big_out[5]))
    big_out[0] = _adamw_shard("adamw_0", red_in, big[0], big_m[0], big_v[0], chip)
    s_grad = late_sum + early_sum[:-2]
    s_delta, s_m, s_v = (late_upd[j] + early_upd[j] for j in range(3))


    names = ["norm_mix_g", "w_in", "conv_w", "conv_b", "lru_wa", "lru_ba", "lru_wx", "lru_bx", "lru_lambda", "w_proj_attn", "w_proj_lru",
             "w_out", "norm_mlp_g", "w_up", "w_down", "norm_final_g"]
    small_names = ["norm_mix_g", "conv_b", "lru_wa", "lru_ba", "lru_wx", "lru_bx", "lru_lambda", "norm_mlp_g", "norm_final_g"]
    big_names = ["w_in", "w_proj_attn", "w_proj_lru", "w_out", "w_up", "w_down"]
    res = {"conv_w": (g_cw, cw_delta, cw_m, cw_v)}
    for i, nm in enumerate(small_names):
        res[nm] = (s_grad[i], s_delta[i], s_m[i], s_v[i])
    for i, nm in enumerate(big_names):
        res[nm] = tuple(t[None] for t in big_out[i])
    return (loss, grad_x[None], *[res[nm][0] for nm in names], *[res[nm][1] for nm in names],
            *[res[nm][2] for nm in names], *[res[nm][3] for nm in names])
```
